```python
import math
import jax
import jax.numpy as jnp
from jax import lax
import numpy as np

D_MODEL = 1024
BATCH = 16
SEQ = 2048
DEPTH = 1

PLE_DIM = 256
D_FF = 2816
SSM_GROUP_CH = 16
SSM_GROUPS = 32
D_SSM = SSM_GROUPS * SSM_GROUP_CH
SSM_STATE = 64
GMLP_HEADS = 8
GMLP_HEAD_DIM = 64
D_GMLP = GMLP_HEADS * GMLP_HEAD_DIM
CHUNK = 128
D_IN = D_SSM + 2 * D_GMLP + 2 * D_MODEL
LN_EPS = 1e-5
DEEPNORM_ALPHA = (2.0 * DEPTH) ** 0.25
DEEPNORM_BETA = (8.0 * DEPTH) ** -0.25

kernel_name = "hybrid_s5_gmlp_macaron_deepnorm"


def _layer_norm(x, g, b):
    xf = x.astype(jnp.float32)
    mu = jnp.mean(xf, axis=-1, keepdims=True)
    var = jnp.mean(jnp.square(xf - mu), axis=-1, keepdims=True)
    y = (xf - mu) * lax.rsqrt(var + LN_EPS) * g.astype(jnp.float32) + b.astype(jnp.float32)
    return y.astype(x.dtype)


def _swiglu(x, w_in, w_out):
    h = x @ w_in
    gate, up = jnp.split(h, 2, axis=-1)
    return (jax.nn.silu(gate) * up) @ w_out


def _complex_affine_combine(e1, e2):
    a1r, a1i, b1r, b1i = e1
    a2r, a2i, b2r, b2i = e2
    ar = a2r * a1r - a2i * a1i
    ai = a2r * a1i + a2i * a1r
    br = a2r * b1r - a2i * b1i + b2r
    bi = a2r * b1i + a2i * b1r + b2i
    return (ar, ai, br, bi)


def _s5_branch(u, lam_re, lam_im, log_dt, b_re, b_im, c_re, c_im, d_skip, glu_w, glu_b):
    bsz, seq, _ = u.shape
    f32 = jnp.float32
    ug = u.reshape(bsz, seq, SSM_GROUPS, SSM_GROUP_CH).astype(f32)
    dt = jnp.exp(log_dt.astype(f32))[:, None]
    lre = lam_re.astype(f32)
    lim = lam_im.astype(f32)
    mag = jnp.exp(lre * dt)
    ab_re = mag * jnp.cos(lim * dt)
    ab_im = mag * jnp.sin(lim * dt)
    nr = ab_re - 1.0
    ni = ab_im
    den = lre * lre + lim * lim
    coef_re = ((nr * lre + ni * lim) / den)[..., None]
    coef_im = ((ni * lre - nr * lim) / den)[..., None]
    bre = b_re.astype(f32)
    bim = b_im.astype(f32)
    bb_re = coef_re * bre - coef_im * bim
    bb_im = coef_re * bim + coef_im * bre
    bu_re = jnp.einsum('blgi,gpi->blgp', ug, bb_re)
    bu_im = jnp.einsum('blgi,gpi->blgp', ug, bb_im)
    a_re = jnp.broadcast_to(ab_re, bu_re.shape)
    a_im = jnp.broadcast_to(ab_im, bu_im.shape)
    _, _, h_re, h_im = lax.associative_scan(
        _complex_affine_combine, (a_re, a_im, bu_re, bu_im), axis=1)
    y = (jnp.einsum('gip,blgp->blgi', c_re.astype(f32), h_re)
         - jnp.einsum('gip,blgp->blgi', c_im.astype(f32), h_im))
    y = y.reshape(bsz, seq, D_SSM).astype(u.dtype) + d_skip * u
    y = jax.nn.gelu(y)
    return y * jax.nn.sigmoid(y @ glu_w + glu_b)


def _gmlp_branch(z_u, z_v, ln_g, ln_b, w_s, b_s):
    bsz, seq, _ = z_v.shape
    n_chunks = seq // CHUNK
    u = jax.nn.gelu(z_u)
    v = _layer_norm(jax.nn.gelu(z_v), ln_g, ln_b)
    vh = v.reshape(bsz, n_chunks, CHUNK, GMLP_HEADS, GMLP_HEAD_DIM)
    causal = jnp.tril(jnp.ones((CHUNK, CHUNK), dtype=bool))
    ws = jnp.where(causal[None], w_s, 0.0)
    s = jnp.einsum('hts,bcshd->bcthd', ws, vh) + b_s.T[:, :, None]
    return u * s.reshape(bsz, seq, D_GMLP)


def _fwd_setup_inputs(seed: int = 0) -> dict:
    key = jax.random.key(seed)
    ks = jax.random.split(key, 40)
    f32 = jnp.float32

    def nrm(k, shape, scale):
        return jax.random.normal(k, shape, f32) * scale

    def gain(k, shape):
        return 1.0 + 0.02 * jax.random.normal(k, shape, f32)

    def bias(k, shape):
        return 0.02 * jax.random.normal(k, shape, f32)

    L = DEPTH
    x = jax.random.normal(ks[0], (BATCH, SEQ, D_MODEL), f32)
    p = jax.random.normal(ks[1], (DEPTH, BATCH, SEQ, PLE_DIM), f32)

    ffn1_w_in = nrm(ks[2], (L, D_MODEL, 2 * D_FF), D_MODEL ** -0.5)
    ffn1_w_out = nrm(ks[3], (L, D_FF, D_MODEL), D_FF ** -0.5 * DEEPNORM_BETA)
    ln1_g = gain(ks[4], (L, D_MODEL))
    ln1_b = bias(ks[5], (L, D_MODEL))

    mix_w_in = nrm(ks[6], (L, D_MODEL, D_IN), D_MODEL ** -0.5)
    n_idx = jnp.arange(SSM_STATE, dtype=f32)
    ssm_lambda_re = -0.5 + 0.01 * jax.random.normal(ks[7], (L, SSM_GROUPS, SSM_STATE), f32)
    ssm_lambda_im = (math.pi * n_idx)[None, None, :] + 0.01 * jax.random.normal(
        ks[8], (L, SSM_GROUPS, SSM_STATE), f32)
    ssm_log_dt = jax.random.uniform(ks[9], (L, SSM_GROUPS), f32,
                                    minval=math.log(1e-3), maxval=math.log(1e-1))
    b_scale = (2.0 * SSM_GROUP_CH) ** -0.5
    c_scale = (2.0 * SSM_STATE) ** -0.5
    ssm_b_re = nrm(ks[10], (L, SSM_GROUPS, SSM_STATE, SSM_GROUP_CH), b_scale)
    ssm_b_im = nrm(ks[11], (L, SSM_GROUPS, SSM_STATE, SSM_GROUP_CH), b_scale)
    ssm_c_re = nrm(ks[12], (L, SSM_GROUPS, SSM_GROUP_CH, SSM_STATE), c_scale)
    ssm_c_im = nrm(ks[13], (L, SSM_GROUPS, SSM_GROUP_CH, SSM_STATE), c_scale)
    ssm_d = nrm(ks[14], (L, D_SSM), 1.0)
    ssm_glu_w = nrm(ks[15], (L, D_SSM, D_SSM), D_SSM ** -0.5)
    ssm_glu_b = bias(ks[16], (L, D_SSM))

    gmlp_ln_g = gain(ks[17], (L, D_GMLP))
    gmlp_ln_b = bias(ks[18], (L, D_GMLP))
    gmlp_w_s = nrm(ks[19], (L, GMLP_HEADS, CHUNK, CHUNK), CHUNK ** -0.5)
    gmlp_b_s = 1.0 + 0.02 * jax.random.normal(ks[20], (L, GMLP_HEADS, CHUNK), f32)

    up_a = nrm(ks[21], (L, D_SSM, D_MODEL), D_SSM ** -0.5)
    up_b = nrm(ks[22], (L, D_GMLP, D_MODEL), D_GMLP ** -0.5)
    mix_w_out = nrm(ks[23], (L, D_MODEL, D_MODEL), D_MODEL ** -0.5 * DEEPNORM_BETA)
    ln2_g = gain(ks[24], (L, D_MODEL))
    ln2_b = bias(ks[25], (L, D_MODEL))

    ffn2_w_in = nrm(ks[26], (L, D_MODEL, 2 * D_FF), D_MODEL ** -0.5)
    ffn2_w_out = nrm(ks[27], (L, D_FF, D_MODEL), D_FF ** -0.5 * DEEPNORM_BETA)
    ln3_g = gain(ks[28], (L, D_MODEL))
    ln3_b = bias(ks[29], (L, D_MODEL))

    ple_w_proj = nrm(ks[30], (L, PLE_DIM, D_MODEL), PLE_DIM ** -0.5 * DEEPNORM_BETA)
    ple_w_gate = nrm(ks[31], (L, D_MODEL, D_MODEL), D_MODEL ** -0.5)

    return {
        "x": x, "p": p,
        "ffn1_w_in": ffn1_w_in, "ffn1_w_out": ffn1_w_out, "ln1_g": ln1_g, "ln1_b": ln1_b,
        "mix_w_in": mix_w_in,
        "ssm_lambda_re": ssm_lambda_re, "ssm_lambda_im": ssm_lambda_im, "ssm_log_dt": ssm_log_dt,
        "ssm_b_re": ssm_b_re, "ssm_b_im": ssm_b_im, "ssm_c_re": ssm_c_re, "ssm_c_im": ssm_c_im,
        "ssm_d": ssm_d, "ssm_glu_w": ssm_glu_w, "ssm_glu_b": ssm_glu_b,
        "gmlp_ln_g": gmlp_ln_g, "gmlp_ln_b": gmlp_ln_b, "gmlp_w_s": gmlp_w_s, "gmlp_b_s": gmlp_b_s,
        "up_a": up_a, "up_b": up_b, "mix_w_out": mix_w_out, "ln2_g": ln2_g, "ln2_b": ln2_b,
        "ffn2_w_in": ffn2_w_in, "ffn2_w_out": ffn2_w_out, "ln3_g": ln3_g, "ln3_b": ln3_b,
        "ple_w_proj": ple_w_proj, "ple_w_gate": ple_w_gate,
    }


def _fwd_reference(x, p, ffn1_w_in, ffn1_w_out, ln1_g, ln1_b, mix_w_in,
              ssm_lambda_re, ssm_lambda_im, ssm_log_dt, ssm_b_re, ssm_b_im, ssm_c_re, ssm_c_im,
              ssm_d, ssm_glu_w, ssm_glu_b, gmlp_ln_g, gmlp_ln_b, gmlp_w_s, gmlp_b_s,
              up_a, up_b, mix_w_out, ln2_g, ln2_b, ffn2_w_in, ffn2_w_out, ln3_g, ln3_b,
              ple_w_proj, ple_w_gate):
    splits = [D_SSM, D_SSM + D_GMLP, D_SSM + 2 * D_GMLP, D_SSM + 2 * D_GMLP + D_MODEL]
    for i in range(DEPTH):
        x = _layer_norm(DEEPNORM_ALPHA * x + 0.5 * _swiglu(x, ffn1_w_in[i], ffn1_w_out[i]),
                        ln1_g[i], ln1_b[i])
        proj = x @ mix_w_in[i]
        z_a, z_u, z_v, g_a, g_b = jnp.split(proj, splits, axis=-1)
        y_a = _s5_branch(z_a, ssm_lambda_re[i], ssm_lambda_im[i], ssm_log_dt[i],
                         ssm_b_re[i], ssm_b_im[i], ssm_c_re[i], ssm_c_im[i],
                         ssm_d[i], ssm_glu_w[i], ssm_glu_b[i]) @ up_a[i]
        y_b = _gmlp_branch(z_u, z_v, gmlp_ln_g[i], gmlp_ln_b[i],
                           gmlp_w_s[i], gmlp_b_s[i]) @ up_b[i]
        mixed = (jax.nn.sigmoid(g_a) * y_a + jax.nn.sigmoid(g_b) * y_b) @ mix_w_out[i]
        x = _layer_norm(DEEPNORM_ALPHA * x + mixed, ln2_g[i], ln2_b[i])
        x = _layer_norm(DEEPNORM_ALPHA * x + 0.5 * _swiglu(x, ffn2_w_in[i], ffn2_w_out[i]),
                        ln3_g[i], ln3_b[i])
        x = x + jax.nn.sigmoid(x @ ple_w_gate[i]) * (p[i] @ ple_w_proj[i])
    return x


import jax as _jax
import jax.numpy as _jnp

TWIN_FORMAT = 'train_step'
FWD_PARAMS = ['x', 'p', 'ffn1_w_in', 'ffn1_w_out', 'ln1_g', 'ln1_b', 'mix_w_in', 'ssm_lambda_re', 'ssm_lambda_im', 'ssm_log_dt', 'ssm_b_re', 'ssm_b_im', 'ssm_c_re', 'ssm_c_im', 'ssm_d', 'ssm_glu_w', 'ssm_glu_b', 'gmlp_ln_g', 'gmlp_ln_b', 'gmlp_w_s', 'gmlp_b_s', 'up_a', 'up_b', 'mix_w_out', 'ln2_g', 'ln2_b', 'ffn2_w_in', 'ffn2_w_out', 'ln3_g', 'ln3_b', 'ple_w_proj', 'ple_w_gate']
TWIN_WEIGHTS = ['ffn1_w_in', 'ffn1_w_out', 'ln1_g', 'ln1_b', 'mix_w_in', 'ssm_lambda_re', 'ssm_lambda_im', 'ssm_log_dt', 'ssm_b_re', 'ssm_b_im', 'ssm_c_re', 'ssm_c_im', 'ssm_d', 'ssm_glu_w', 'ssm_glu_b', 'gmlp_ln_g', 'gmlp_ln_b', 'gmlp_w_s', 'gmlp_b_s', 'up_a', 'up_b', 'mix_w_out', 'ln2_g', 'ln2_b', 'ffn2_w_in', 'ffn2_w_out', 'ln3_g', 'ln3_b', 'ple_w_proj', 'ple_w_gate']
TWIN_DIFF_INPUT = 'x'
TWIN_INPUTS = ['x', 'p', 'ffn1_w_in', 'ffn1_w_out', 'ln1_g', 'ln1_b', 'mix_w_in', 'ssm_lambda_re', 'ssm_lambda_im', 'ssm_log_dt', 'ssm_b_re', 'ssm_b_im', 'ssm_c_re', 'ssm_c_im', 'ssm_d', 'ssm_glu_w', 'ssm_glu_b', 'gmlp_ln_g', 'gmlp_ln_b', 'gmlp_w_s', 'gmlp_b_s', 'up_a', 'up_b', 'mix_w_out', 'ln2_g', 'ln2_b', 'ffn2_w_in', 'ffn2_w_out', 'ln3_g', 'ln3_b', 'ple_w_proj', 'ple_w_gate', 'loss_target', 'm_ffn1_w_in', 'm_ffn1_w_out', 'm_ln1_g', 'm_ln1_b', 'm_mix_w_in', 'm_ssm_lambda_re', 'm_ssm_lambda_im', 'm_ssm_log_dt', 'm_ssm_b_re', 'm_ssm_b_im', 'm_ssm_c_re', 'm_ssm_c_im', 'm_ssm_d', 'm_ssm_glu_w', 'm_ssm_glu_b', 'm_gmlp_ln_g', 'm_gmlp_ln_b', 'm_gmlp_w_s', 'm_gmlp_b_s', 'm_up_a', 'm_up_b', 'm_mix_w_out', 'm_ln2_g', 'm_ln2_b', 'm_ffn2_w_in', 'm_ffn2_w_out', 'm_ln3_g', 'm_ln3_b', 'm_ple_w_proj', 'm_ple_w_gate', 'v_ffn1_w_in', 'v_ffn1_w_out', 'v_ln1_g', 'v_ln1_b', 'v_mix_w_in', 'v_ssm_lambda_re', 'v_ssm_lambda_im', 'v_ssm_log_dt', 'v_ssm_b_re', 'v_ssm_b_im', 'v_ssm_c_re', 'v_ssm_c_im', 'v_ssm_d', 'v_ssm_glu_w', 'v_ssm_glu_b', 'v_gmlp_ln_g', 'v_gmlp_ln_b', 'v_gmlp_w_s', 'v_gmlp_b_s', 'v_up_a', 'v_up_b', 'v_mix_w_out', 'v_ln2_g', 'v_ln2_b', 'v_ffn2_w_in', 'v_ffn2_w_out', 'v_ln3_g', 'v_ln3_b', 'v_ple_w_proj', 'v_ple_w_gate']
TWIN_OUTPUTS = ['loss', 'grad_x', 'grad_ffn1_w_in', 'grad_ffn1_w_out', 'grad_ln1_g', 'grad_ln1_b', 'grad_mix_w_in', 'grad_ssm_lambda_re', 'grad_ssm_lambda_im', 'grad_ssm_log_dt', 'grad_ssm_b_re', 'grad_ssm_b_im', 'grad_ssm_c_re', 'grad_ssm_c_im', 'grad_ssm_d', 'grad_ssm_glu_w', 'grad_ssm_glu_b', 'grad_gmlp_ln_g', 'grad_gmlp_ln_b', 'grad_gmlp_w_s', 'grad_gmlp_b_s', 'grad_up_a', 'grad_up_b', 'grad_mix_w_out', 'grad_ln2_g', 'grad_ln2_b', 'grad_ffn2_w_in', 'grad_ffn2_w_out', 'grad_ln3_g', 'grad_ln3_b', 'grad_ple_w_proj', 'grad_ple_w_gate', 'delta_ffn1_w_in', 'delta_ffn1_w_out', 'delta_ln1_g', 'delta_ln1_b', 'delta_mix_w_in', 'delta_ssm_lambda_re', 'delta_ssm_lambda_im', 'delta_ssm_log_dt', 'delta_ssm_b_re', 'delta_ssm_b_im', 'delta_ssm_c_re', 'delta_ssm_c_im', 'delta_ssm_d', 'delta_ssm_glu_w', 'delta_ssm_glu_b', 'delta_gmlp_ln_g', 'delta_gmlp_ln_b', 'delta_gmlp_w_s', 'delta_gmlp_b_s', 'delta_up_a', 'delta_up_b', 'delta_mix_w_out', 'delta_ln2_g', 'delta_ln2_b', 'delta_ffn2_w_in', 'delta_ffn2_w_out', 'delta_ln3_g', 'delta_ln3_b', 'delta_ple_w_proj', 'delta_ple_w_gate', 'new_m_ffn1_w_in', 'new_m_ffn1_w_out', 'new_m_ln1_g', 'new_m_ln1_b', 'new_m_mix_w_in', 'new_m_ssm_lambda_re', 'new_m_ssm_lambda_im', 'new_m_ssm_log_dt', 'new_m_ssm_b_re', 'new_m_ssm_b_im', 'new_m_ssm_c_re', 'new_m_ssm_c_im', 'new_m_ssm_d', 'new_m_ssm_glu_w', 'new_m_ssm_glu_b', 'new_m_gmlp_ln_g', 'new_m_gmlp_ln_b', 'new_m_gmlp_w_s', 'new_m_gmlp_b_s', 'new_m_up_a', 'new_m_up_b', 'new_m_mix_w_out', 'new_m_ln2_g', 'new_m_ln2_b', 'new_m_ffn2_w_in', 'new_m_ffn2_w_out', 'new_m_ln3_g', 'new_m_ln3_b', 'new_m_ple_w_proj', 'new_m_ple_w_gate', 'new_v_ffn1_w_in', 'new_v_ffn1_w_out', 'new_v_ln1_g', 'new_v_ln1_b', 'new_v_mix_w_in', 'new_v_ssm_lambda_re', 'new_v_ssm_lambda_im', 'new_v_ssm_log_dt', 'new_v_ssm_b_re', 'new_v_ssm_b_im', 'new_v_ssm_c_re', 'new_v_ssm_c_im', 'new_v_ssm_d', 'new_v_ssm_glu_w', 'new_v_ssm_glu_b', 'new_v_gmlp_ln_g', 'new_v_gmlp_ln_b', 'new_v_gmlp_w_s', 'new_v_gmlp_b_s', 'new_v_up_a', 'new_v_up_b', 'new_v_mix_w_out', 'new_v_ln2_g', 'new_v_ln2_b', 'new_v_ffn2_w_in', 'new_v_ffn2_w_out', 'new_v_ln3_g', 'new_v_ln3_b', 'new_v_ple_w_proj', 'new_v_ple_w_gate']
TWIN_LEAF_KINDS = {'loss': 'loss', 'grad_x': 'grad_x', 'grad_ffn1_w_in': 'grad_w', 'grad_ffn1_w_out': 'grad_w', 'grad_ln1_g': 'grad_w', 'grad_ln1_b': 'grad_w', 'grad_mix_w_in': 'grad_w', 'grad_ssm_lambda_re': 'grad_w', 'grad_ssm_lambda_im': 'grad_w', 'grad_ssm_log_dt': 'grad_w', 'grad_ssm_b_re': 'grad_w', 'grad_ssm_b_im': 'grad_w', 'grad_ssm_c_re': 'grad_w', 'grad_ssm_c_im': 'grad_w', 'grad_ssm_d': 'grad_w', 'grad_ssm_glu_w': 'grad_w', 'grad_ssm_glu_b': 'grad_w', 'grad_gmlp_ln_g': 'grad_w', 'grad_gmlp_ln_b': 'grad_w', 'grad_gmlp_w_s': 'grad_w', 'grad_gmlp_b_s': 'grad_w', 'grad_up_a': 'grad_w', 'grad_up_b': 'grad_w', 'grad_mix_w_out': 'grad_w', 'grad_ln2_g': 'grad_w', 'grad_ln2_b': 'grad_w', 'grad_ffn2_w_in': 'grad_w', 'grad_ffn2_w_out': 'grad_w', 'grad_ln3_g': 'grad_w', 'grad_ln3_b': 'grad_w', 'grad_ple_w_proj': 'grad_w', 'grad_ple_w_gate': 'grad_w', 'delta_ffn1_w_in': 'delta_w', 'delta_ffn1_w_out': 'delta_w', 'delta_ln1_g': 'delta_w', 'delta_ln1_b': 'delta_w', 'delta_mix_w_in': 'delta_w', 'delta_ssm_lambda_re': 'delta_w', 'delta_ssm_lambda_im': 'delta_w', 'delta_ssm_log_dt': 'delta_w', 'delta_ssm_b_re': 'delta_w', 'delta_ssm_b_im': 'delta_w', 'delta_ssm_c_re': 'delta_w', 'delta_ssm_c_im': 'delta_w', 'delta_ssm_d': 'delta_w', 'delta_ssm_glu_w': 'delta_w', 'delta_ssm_glu_b': 'delta_w', 'delta_gmlp_ln_g': 'delta_w', 'delta_gmlp_ln_b': 'delta_w', 'delta_gmlp_w_s': 'delta_w', 'delta_gmlp_b_s': 'delta_w', 'delta_up_a': 'delta_w', 'delta_up_b': 'delta_w', 'delta_mix_w_out': 'delta_w', 'delta_ln2_g': 'delta_w', 'delta_ln2_b': 'delta_w', 'delta_ffn2_w_in': 'delta_w', 'delta_ffn2_w_out': 'delta_w', 'delta_ln3_g': 'delta_w', 'delta_ln3_b': 'delta_w', 'delta_ple_w_proj': 'delta_w', 'delta_ple_w_gate': 'delta_w', 'new_m_ffn1_w_in': 'new_m', 'new_m_ffn1_w_out': 'new_m', 'new_m_ln1_g': 'new_m', 'new_m_ln1_b': 'new_m', 'new_m_mix_w_in': 'new_m', 'new_m_ssm_lambda_re': 'new_m', 'new_m_ssm_lambda_im': 'new_m', 'new_m_ssm_log_dt': 'new_m', 'new_m_ssm_b_re': 'new_m', 'new_m_ssm_b_im': 'new_m', 'new_m_ssm_c_re': 'new_m', 'new_m_ssm_c_im': 'new_m', 'new_m_ssm_d': 'new_m', 'new_m_ssm_glu_w': 'new_m', 'new_m_ssm_glu_b': 'new_m', 'new_m_gmlp_ln_g': 'new_m', 'new_m_gmlp_ln_b': 'new_m', 'new_m_gmlp_w_s': 'new_m', 'new_m_gmlp_b_s': 'new_m', 'new_m_up_a': 'new_m', 'new_m_up_b': 'new_m', 'new_m_mix_w_out': 'new_m', 'new_m_ln2_g': 'new_m', 'new_m_ln2_b': 'new_m', 'new_m_ffn2_w_in': 'new_m', 'new_m_ffn2_w_out': 'new_m', 'new_m_ln3_g': 'new_m', 'new_m_ln3_b': 'new_m', 'new_m_ple_w_proj': 'new_m', 'new_m_ple_w_gate': 'new_m', 'new_v_ffn1_w_in': 'new_v', 'new_v_ffn1_w_out': 'new_v', 'new_v_ln1_g': 'new_v', 'new_v_ln1_b': 'new_v', 'new_v_mix_w_in': 'new_v', 'new_v_ssm_lambda_re': 'new_v', 'new_v_ssm_lambda_im': 'new_v', 'new_v_ssm_log_dt': 'new_v', 'new_v_ssm_b_re': 'new_v', 'new_v_ssm_b_im': 'new_v', 'new_v_ssm_c_re': 'new_v', 'new_v_ssm_c_im': 'new_v', 'new_v_ssm_d': 'new_v', 'new_v_ssm_glu_w': 'new_v', 'new_v_ssm_glu_b': 'new_v', 'new_v_gmlp_ln_g': 'new_v', 'new_v_gmlp_ln_b': 'new_v', 'new_v_gmlp_w_s': 'new_v', 'new_v_gmlp_b_s': 'new_v', 'new_v_up_a': 'new_v', 'new_v_up_b': 'new_v', 'new_v_mix_w_out': 'new_v', 'new_v_ln2_g': 'new_v', 'new_v_ln2_b': 'new_v', 'new_v_ffn2_w_in': 'new_v', 'new_v_ffn2_w_out': 'new_v', 'new_v_ln3_g': 'new_v', 'new_v_ln3_b': 'new_v', 'new_v_ple_w_proj': 'new_v', 'new_v_ple_w_gate': 'new_v'}


def _forward(args):
    return _fwd_reference(*[args[k] for k in FWD_PARAMS])


def _output_shape():
    out = _jax.eval_shape(lambda: _forward(_fwd_setup_inputs(0)))
    return out.shape, out.dtype

N_MICROBATCH = 1
ADAM_LR = 0.001
ADAM_B1 = 0.9
ADAM_B2 = 0.999
ADAM_EPS = 1e-08
ADAM_WD = 0.01
ADAM_STEP = 10
PER_EXAMPLE_BATCH_AXIS = {'x': 0, 'p': 1, 'loss_target': 0}
SHARED_INPUTS = []
_WEIGHT_DTYPES = {'ffn1_w_in': _jnp.float32, 'ffn1_w_out': _jnp.float32, 'ln1_g': _jnp.float32, 'ln1_b': _jnp.float32, 'mix_w_in': _jnp.float32, 'ssm_lambda_re': _jnp.float32, 'ssm_lambda_im': _jnp.float32, 'ssm_log_dt': _jnp.float32, 'ssm_b_re': _jnp.float32, 'ssm_b_im': _jnp.float32, 'ssm_c_re': _jnp.float32, 'ssm_c_im': _jnp.float32, 'ssm_d': _jnp.float32, 'ssm_glu_w': _jnp.float32, 'ssm_glu_b': _jnp.float32, 'gmlp_ln_g': _jnp.float32, 'gmlp_ln_b': _jnp.float32, 'gmlp_w_s': _jnp.float32, 'gmlp_b_s': _jnp.float32, 'up_a': _jnp.float32, 'up_b': _jnp.float32, 'mix_w_out': _jnp.float32, 'ln2_g': _jnp.float32, 'ln2_b': _jnp.float32, 'ffn2_w_in': _jnp.float32, 'ffn2_w_out': _jnp.float32, 'ln3_g': _jnp.float32, 'ln3_b': _jnp.float32, 'ple_w_proj': _jnp.float32, 'ple_w_gate': _jnp.float32}
MOMENT_SCALE = {'ffn1_w_in': 1.757893e-02, 'ffn1_w_out': 4.823885e-02, 'ln1_g': 1.037117e+00, 'ln1_b': 1.053230e+00, 'mix_w_in': 2.985671e-02, 'ssm_lambda_re': 1.373371e-03, 'ssm_lambda_im': 1.410626e-03, 'ssm_log_dt': 5.278422e-01, 'ssm_b_re': 8.776804e-04, 'ssm_b_im': 8.998813e-04, 'ssm_c_re': 1.797099e-03, 'ssm_c_im': 1.834072e-03, 'ssm_d': 6.288007e-02, 'ssm_glu_w': 8.991039e-03, 'ssm_glu_b': 2.270919e-02, 'gmlp_ln_g': 3.210229e-02, 'gmlp_ln_b': 3.423098e-02, 'gmlp_w_s': 2.331353e-02, 'gmlp_b_s': 3.363833e-02, 'up_a': 3.665402e-02, 'up_b': 6.665149e-02, 'mix_w_out': 1.284754e-01, 'ln2_g': 1.110295e+00, 'ln2_b': 1.094017e+00, 'ffn2_w_in': 1.715646e-02, 'ffn2_w_out': 4.748341e-02, 'ln3_g': 3.232021e+01, 'ln3_b': 2.680780e+00, 'ple_w_proj': 2.513527e-01, 'ple_w_gate': 7.018074e-02}


def _to_microbatches(a, axis):
    t = _jnp.moveaxis(a, axis, 0)
    t = t.reshape((N_MICROBATCH, t.shape[0] // N_MICROBATCH) + t.shape[1:])
    return _jnp.moveaxis(t, 1, axis + 1)


def setup_inputs(seed: int = 0) -> dict:
    inp = _fwd_setup_inputs(seed)
    key = _jax.random.fold_in(_jax.random.key(seed), 7919)
    shape, _ = _output_shape()
    out = dict(inp)
    out["loss_target"] = _jax.random.normal(_jax.random.fold_in(key, 0), shape, _jnp.float32)
    for i, name in enumerate(TWIN_WEIGHTS):
        w = inp[name].astype(_jnp.float32)
        if MOMENT_SCALE is None:
            s = _jnp.sqrt(_jnp.mean(_jnp.square(w)) + 1e-30)
        else:
            s = MOMENT_SCALE[name]
        km, kv = _jax.random.split(_jax.random.fold_in(key, i + 1))
        out[name] = w
        out["m_" + name] = s * _jax.random.normal(km, w.shape, _jnp.float32)
        out["v_" + name] = (s * s) * _jax.random.uniform(kv, w.shape, _jnp.float32, 0.5, 1.5)
    if N_MICROBATCH > 1:
        for name, axis in PER_EXAMPLE_BATCH_AXIS.items():
            out[name] = _to_microbatches(out[name], axis)
    return {'x': out['x'], 'p': out['p'], 'ffn1_w_in': out['ffn1_w_in'], 'ffn1_w_out': out['ffn1_w_out'], 'ln1_g': out['ln1_g'], 'ln1_b': out['ln1_b'], 'mix_w_in': out['mix_w_in'], 'ssm_lambda_re': out['ssm_lambda_re'], 'ssm_lambda_im': out['ssm_lambda_im'], 'ssm_log_dt': out['ssm_log_dt'], 'ssm_b_re': out['ssm_b_re'], 'ssm_b_im': out['ssm_b_im'], 'ssm_c_re': out['ssm_c_re'], 'ssm_c_im': out['ssm_c_im'], 'ssm_d': out['ssm_d'], 'ssm_glu_w': out['ssm_glu_w'], 'ssm_glu_b': out['ssm_glu_b'], 'gmlp_ln_g': out['gmlp_ln_g'], 'gmlp_ln_b': out['gmlp_ln_b'], 'gmlp_w_s': out['gmlp_w_s'], 'gmlp_b_s': out['gmlp_b_s'], 'up_a': out['up_a'], 'up_b': out['up_b'], 'mix_w_out': out['mix_w_out'], 'ln2_g': out['ln2_g'], 'ln2_b': out['ln2_b'], 'ffn2_w_in': out['ffn2_w_in'], 'ffn2_w_out': out['ffn2_w_out'], 'ln3_g': out['ln3_g'], 'ln3_b': out['ln3_b'], 'ple_w_proj': out['ple_w_proj'], 'ple_w_gate': out['ple_w_gate'], 'loss_target': out['loss_target'], 'm_ffn1_w_in': out['m_ffn1_w_in'], 'm_ffn1_w_out': out['m_ffn1_w_out'], 'm_ln1_g': out['m_ln1_g'], 'm_ln1_b': out['m_ln1_b'], 'm_mix_w_in': out['m_mix_w_in'], 'm_ssm_lambda_re': out['m_ssm_lambda_re'], 'm_ssm_lambda_im': out['m_ssm_lambda_im'], 'm_ssm_log_dt': out['m_ssm_log_dt'], 'm_ssm_b_re': out['m_ssm_b_re'], 'm_ssm_b_im': out['m_ssm_b_im'], 'm_ssm_c_re': out['m_ssm_c_re'], 'm_ssm_c_im': out['m_ssm_c_im'], 'm_ssm_d': out['m_ssm_d'], 'm_ssm_glu_w': out['m_ssm_glu_w'], 'm_ssm_glu_b': out['m_ssm_glu_b'], 'm_gmlp_ln_g': out['m_gmlp_ln_g'], 'm_gmlp_ln_b': out['m_gmlp_ln_b'], 'm_gmlp_w_s': out['m_gmlp_w_s'], 'm_gmlp_b_s': out['m_gmlp_b_s'], 'm_up_a': out['m_up_a'], 'm_up_b': out['m_up_b'], 'm_mix_w_out': out['m_mix_w_out'], 'm_ln2_g': out['m_ln2_g'], 'm_ln2_b': out['m_ln2_b'], 'm_ffn2_w_in': out['m_ffn2_w_in'], 'm_ffn2_w_out': out['m_ffn2_w_out'], 'm_ln3_g': out['m_ln3_g'], 'm_ln3_b': out['m_ln3_b'], 'm_ple_w_proj': out['m_ple_w_proj'], 'm_ple_w_gate': out['m_ple_w_gate'], 'v_ffn1_w_in': out['v_ffn1_w_in'], 'v_ffn1_w_out': out['v_ffn1_w_out'], 'v_ln1_g': out['v_ln1_g'], 'v_ln1_b': out['v_ln1_b'], 'v_mix_w_in': out['v_mix_w_in'], 'v_ssm_lambda_re': out['v_ssm_lambda_re'], 'v_ssm_lambda_im': out['v_ssm_lambda_im'], 'v_ssm_log_dt': out['v_ssm_log_dt'], 'v_ssm_b_re': out['v_ssm_b_re'], 'v_ssm_b_im': out['v_ssm_b_im'], 'v_ssm_c_re': out['v_ssm_c_re'], 'v_ssm_c_im': out['v_ssm_c_im'], 'v_ssm_d': out['v_ssm_d'], 'v_ssm_glu_w': out['v_ssm_glu_w'], 'v_ssm_glu_b': out['v_ssm_glu_b'], 'v_gmlp_ln_g': out['v_gmlp_ln_g'], 'v_gmlp_ln_b': out['v_gmlp_ln_b'], 'v_gmlp_w_s': out['v_gmlp_w_s'], 'v_gmlp_b_s': out['v_gmlp_b_s'], 'v_up_a': out['v_up_a'], 'v_up_b': out['v_up_b'], 'v_mix_w_out': out['v_mix_w_out'], 'v_ln2_g': out['v_ln2_g'], 'v_ln2_b': out['v_ln2_b'], 'v_ffn2_w_in': out['v_ffn2_w_in'], 'v_ffn2_w_out': out['v_ffn2_w_out'], 'v_ln3_g': out['v_ln3_g'], 'v_ln3_b': out['v_ln3_b'], 'v_ple_w_proj': out['v_ple_w_proj'], 'v_ple_w_gate': out['v_ple_w_gate']}


def _loss(weights, diff, rest, loss_target):
    with _jax.named_scope("forward"):
        args = {**rest, TWIN_DIFF_INPUT: diff, **{k: w.astype(_WEIGHT_DTYPES[k]) for k, w in weights.items()}}
        y = _forward(args)
    with _jax.named_scope("loss_head"):
        err = _jnp.square(y.astype(_jnp.float32) - loss_target)
        return 0.5 * _jnp.sum(_jnp.mean(err, axis=-1)) if err.ndim else 0.5 * err


def _adamw(w, g, m, v):
    m = ADAM_B1 * m + (1.0 - ADAM_B1) * g
    v = ADAM_B2 * v + (1.0 - ADAM_B2) * _jnp.square(g)
    m_hat = m / (1.0 - ADAM_B1 ** ADAM_STEP)
    v_hat = v / (1.0 - ADAM_B2 ** ADAM_STEP)
    delta = -ADAM_LR * (m_hat / (_jnp.sqrt(v_hat) + ADAM_EPS) + ADAM_WD * w)
    return delta, m, v


def reference(x, p, ffn1_w_in, ffn1_w_out, ln1_g, ln1_b, mix_w_in, ssm_lambda_re, ssm_lambda_im, ssm_log_dt, ssm_b_re, ssm_b_im, ssm_c_re, ssm_c_im, ssm_d, ssm_glu_w, ssm_glu_b, gmlp_ln_g, gmlp_ln_b, gmlp_w_s, gmlp_b_s, up_a, up_b, mix_w_out, ln2_g, ln2_b, ffn2_w_in, ffn2_w_out, ln3_g, ln3_b, ple_w_proj, ple_w_gate, loss_target, m_ffn1_w_in, m_ffn1_w_out, m_ln1_g, m_ln1_b, m_mix_w_in, m_ssm_lambda_re, m_ssm_lambda_im, m_ssm_log_dt, m_ssm_b_re, m_ssm_b_im, m_ssm_c_re, m_ssm_c_im, m_ssm_d, m_ssm_glu_w, m_ssm_glu_b, m_gmlp_ln_g, m_gmlp_ln_b, m_gmlp_w_s, m_gmlp_b_s, m_up_a, m_up_b, m_mix_w_out, m_ln2_g, m_ln2_b, m_ffn2_w_in, m_ffn2_w_out, m_ln3_g, m_ln3_b, m_ple_w_proj, m_ple_w_gate, v_ffn1_w_in, v_ffn1_w_out, v_ln1_g, v_ln1_b, v_mix_w_in, v_ssm_lambda_re, v_ssm_lambda_im, v_ssm_log_dt, v_ssm_b_re, v_ssm_b_im, v_ssm_c_re, v_ssm_c_im, v_ssm_d, v_ssm_glu_w, v_ssm_glu_b, v_gmlp_ln_g, v_gmlp_ln_b, v_gmlp_w_s, v_gmlp_b_s, v_up_a, v_up_b, v_mix_w_out, v_ln2_g, v_ln2_b, v_ffn2_w_in, v_ffn2_w_out, v_ln3_g, v_ln3_b, v_ple_w_proj, v_ple_w_gate):
    given = dict(x=x, p=p, ffn1_w_in=ffn1_w_in, ffn1_w_out=ffn1_w_out, ln1_g=ln1_g, ln1_b=ln1_b, mix_w_in=mix_w_in, ssm_lambda_re=ssm_lambda_re, ssm_lambda_im=ssm_lambda_im, ssm_log_dt=ssm_log_dt, ssm_b_re=ssm_b_re, ssm_b_im=ssm_b_im, ssm_c_re=ssm_c_re, ssm_c_im=ssm_c_im, ssm_d=ssm_d, ssm_glu_w=ssm_glu_w, ssm_glu_b=ssm_glu_b, gmlp_ln_g=gmlp_ln_g, gmlp_ln_b=gmlp_ln_b, gmlp_w_s=gmlp_w_s, gmlp_b_s=gmlp_b_s, up_a=up_a, up_b=up_b, mix_w_out=mix_w_out, ln2_g=ln2_g, ln2_b=ln2_b, ffn2_w_in=ffn2_w_in, ffn2_w_out=ffn2_w_out, ln3_g=ln3_g, ln3_b=ln3_b, ple_w_proj=ple_w_proj, ple_w_gate=ple_w_gate, loss_target=loss_target, m_ffn1_w_in=m_ffn1_w_in, m_ffn1_w_out=m_ffn1_w_out, m_ln1_g=m_ln1_g, m_ln1_b=m_ln1_b, m_mix_w_in=m_mix_w_in, m_ssm_lambda_re=m_ssm_lambda_re, m_ssm_lambda_im=m_ssm_lambda_im, m_ssm_log_dt=m_ssm_log_dt, m_ssm_b_re=m_ssm_b_re, m_ssm_b_im=m_ssm_b_im, m_ssm_c_re=m_ssm_c_re, m_ssm_c_im=m_ssm_c_im, m_ssm_d=m_ssm_d, m_ssm_glu_w=m_ssm_glu_w, m_ssm_glu_b=m_ssm_glu_b, m_gmlp_ln_g=m_gmlp_ln_g, m_gmlp_ln_b=m_gmlp_ln_b, m_gmlp_w_s=m_gmlp_w_s, m_gmlp_b_s=m_gmlp_b_s, m_up_a=m_up_a, m_up_b=m_up_b, m_mix_w_out=m_mix_w_out, m_ln2_g=m_ln2_g, m_ln2_b=m_ln2_b, m_ffn2_w_in=m_ffn2_w_in, m_ffn2_w_out=m_ffn2_w_out, m_ln3_g=m_ln3_g, m_ln3_b=m_ln3_b, m_ple_w_proj=m_ple_w_proj, m_ple_w_gate=m_ple_w_gate, v_ffn1_w_in=v_ffn1_w_in, v_ffn1_w_out=v_ffn1_w_out, v_ln1_g=v_ln1_g, v_ln1_b=v_ln1_b, v_mix_w_in=v_mix_w_in, v_ssm_lambda_re=v_ssm_lambda_re, v_ssm_lambda_im=v_ssm_lambda_im, v_ssm_log_dt=v_ssm_log_dt, v_ssm_b_re=v_ssm_b_re, v_ssm_b_im=v_ssm_b_im, v_ssm_c_re=v_ssm_c_re, v_ssm_c_im=v_ssm_c_im, v_ssm_d=v_ssm_d, v_ssm_glu_w=v_ssm_glu_w, v_ssm_glu_b=v_ssm_glu_b, v_gmlp_ln_g=v_gmlp_ln_g, v_gmlp_ln_b=v_gmlp_ln_b, v_gmlp_w_s=v_gmlp_w_s, v_gmlp_b_s=v_gmlp_b_s, v_up_a=v_up_a, v_up_b=v_up_b, v_mix_w_out=v_mix_w_out, v_ln2_g=v_ln2_g, v_ln2_b=v_ln2_b, v_ffn2_w_in=v_ffn2_w_in, v_ffn2_w_out=v_ffn2_w_out, v_ln3_g=v_ln3_g, v_ln3_b=v_ln3_b, v_ple_w_proj=v_ple_w_proj, v_ple_w_gate=v_ple_w_gate)
    weights = {n: given[n] for n in TWIN_WEIGHTS}
    shared = {n: given[n] for n in SHARED_INPUTS}
    per_example = {n: given[n] for n in ['x', 'p']}
    grad_fn = _jax.value_and_grad(_loss, argnums=(0, 1))

    def one_microbatch(ex, loss_target):
        ex = dict(ex)
        diff = ex.pop(TWIN_DIFF_INPUT)
        return grad_fn(weights, diff, {**shared, **ex}, loss_target)

    if N_MICROBATCH == 1:
        loss, (grad_w, grad_x) = one_microbatch(per_example, given["loss_target"])
    else:
        def body(carry, xs):
            loss_sum, grad_sum = carry
            l_k, (gw_k, gx_k) = one_microbatch(xs[0], xs[1])
            with _jax.named_scope("update"):
                return (loss_sum + l_k, _jax.tree.map(_jnp.add, grad_sum, gw_k)), gx_k

        init = (_jnp.zeros((), _jnp.float32), _jax.tree.map(_jnp.zeros_like, weights))
        (loss, grad_w), grad_x = _jax.lax.scan(body, init, (per_example, given["loss_target"]))
    with _jax.named_scope("update"):
        delta_w, new_m, new_v = {}, {}, {}
        for n in TWIN_WEIGHTS:
            delta_w[n], new_m[n], new_v[n] = _adamw(weights[n], grad_w[n], given["m_" + n], given["v_" + n])
    return (loss, grad_x, *[grad_w[n] for n in TWIN_WEIGHTS], *[delta_w[n] for n in TWIN_WEIGHTS],
            *[new_m[n] for n in TWIN_WEIGHTS], *[new_v[n] for n in TWIN_WEIGHTS])
```

```python
import functools
import math

import jax
import jax.numpy as jnp
from jax import lax
from jax.experimental import pallas as pl
from jax.experimental.pallas import tpu as pltpu

F32 = jnp.float32
BF16 = jnp.bfloat16

N_DEV = 8
LN_EPS = 1e-5
CHUNK = 128
N_SEG = 8
PACK_W = 1024
PACK_ROW_ALIGN = 16
VMEM_LIMIT_BYTES = 56 * 1024 * 1024

ADAM_LR = 0.001
ADAM_B1 = 0.9
ADAM_B2 = 0.999
ADAM_EPS = 1e-08
ADAM_WD = 0.01
ADAM_STEP = 10

COL_SHARDED = ("ffn1_w_in", "mix_w_in", "up_a", "up_b", "ffn2_w_in", "ple_w_proj")
ROW_SHARDED = ("ffn1_w_out", "ssm_glu_w", "mix_w_out", "ffn2_w_out", "ple_w_gate")
SHARDED = ("ffn1_w_in", "ffn1_w_out", "mix_w_in", "ssm_glu_w", "up_a", "up_b", "mix_w_out",
           "ffn2_w_in", "ffn2_w_out", "ple_w_proj", "ple_w_gate")
WEIGHTS = ("ffn1_w_in", "ffn1_w_out", "ln1_g", "ln1_b", "mix_w_in", "ssm_lambda_re", "ssm_lambda_im",
           "ssm_log_dt", "ssm_b_re", "ssm_b_im", "ssm_c_re", "ssm_c_im", "ssm_d", "ssm_glu_w",
           "ssm_glu_b", "gmlp_ln_g", "gmlp_ln_b", "gmlp_w_s", "gmlp_b_s", "up_a", "up_b", "mix_w_out",
           "ln2_g", "ln2_b", "ffn2_w_in", "ffn2_w_out", "ln3_g", "ln3_b", "ple_w_proj", "ple_w_gate")
SMALL = tuple(n for n in WEIGHTS if n not in SHARDED)
INPUT_NAMES = ("x", "p") + WEIGHTS + ("loss_target",) + tuple("m_" + n for n in WEIGHTS) + tuple(
    "v_" + n for n in WEIGHTS)


def _pick(dim, pref, mult=128):
    if dim <= pref:
        return dim
    d = (pref // mult) * mult
    while d >= mult:
        if dim % d == 0:
            return d
        d -= mult
    return dim


def _params(n_axes=1):
    return pltpu.CompilerParams(dimension_semantics=("arbitrary",) * n_axes,
                                vmem_limit_bytes=VMEM_LIMIT_BYTES)


def _sigmoid(v):
    return 1.0 / (1.0 + jnp.exp(-v))


_GELU_C = math.sqrt(2.0 / math.pi)


def _gelu(v):
    return 0.5 * v * (1.0 + jnp.tanh(_GELU_C * (v + 0.044715 * (v * v * v))))


def _gelu_grad(v):
    t = jnp.tanh(_GELU_C * (v + 0.044715 * (v * v * v)))
    return 0.5 * (1.0 + t) + 0.5 * v * (1.0 - t * t) * (_GELU_C * (1.0 + 3.0 * 0.044715 * v * v))


def _ln_stats(r):
    mu = jnp.mean(r, axis=-1, keepdims=True)
    xc = r - mu
    var = jnp.mean(xc * xc, axis=-1, keepdims=True)
    rstd = lax.rsqrt(var + LN_EPS)
    return xc * rstd, rstd


def _ln_bwd(dy, xhat, rstd, g):
    dxh = dy * g
    m1 = jnp.mean(dxh, axis=-1, keepdims=True)
    m2 = jnp.mean(dxh * xhat, axis=-1, keepdims=True)
    return rstd * (dxh - m1 - xhat * m2)


def _fold8(v):
    rows, w = v.shape
    return jnp.sum(v.reshape(rows // 8, 8, w), axis=0)


def _dot(a, b, ta=False, tb=False):
    dn = (((0 if ta else 1,), (1 if tb else 0,)), ((), ()))
    return lax.dot_general(a.astype(BF16), b.astype(BF16), dn, preferred_element_type=F32)


def _mm(a, b, *, name, ta=False, tb=False, out_dtype=F32, add=None, add_scale=1.0,
        tm=1024, tn=512, tk=1024):
    m_dim = a.shape[1] if ta else a.shape[0]
    k_dim = a.shape[0] if ta else a.shape[1]
    n_dim = b.shape[0] if tb else b.shape[1]
    assert (b.shape[1] if tb else b.shape[0]) == k_dim, (name, a.shape, b.shape)
    tm, tn, tk = _pick(m_dim, tm), _pick(n_dim, tn), _pick(k_dim, tk)
    nk = k_dim // tk
    has_add = add is not None

    def body(*refs):
        if has_add:
            a_ref, b_ref, add_ref, o_ref, acc_ref = refs
        else:
            a_ref, b_ref, o_ref, acc_ref = refs
        k = pl.program_id(2)

        @pl.when(k == 0)
        def _():
            acc_ref[...] = jnp.zeros_like(acc_ref)

        acc_ref[...] += _dot(a_ref[...], b_ref[...], ta, tb)

        @pl.when(k == nk - 1)
        def _():
            r = acc_ref[...]
            if has_add:
                r = r + add_scale * add_ref[...].astype(F32)
            o_ref[...] = r.astype(out_dtype)

    a_spec = (pl.BlockSpec((tk, tm), lambda i, j, k: (k, i)) if ta
              else pl.BlockSpec((tm, tk), lambda i, j, k: (i, k)))
    b_spec = (pl.BlockSpec((tn, tk), lambda i, j, k: (j, k)) if tb
              else pl.BlockSpec((tk, tn), lambda i, j, k: (k, j)))
    in_specs = [a_spec, b_spec]
    operands = [a, b]
    if has_add:
        in_specs.append(pl.BlockSpec((tm, tn), lambda i, j, k: (i, j)))
        operands.append(add)
    return pl.pallas_call(
        body, name=name,
        out_shape=jax.ShapeDtypeStruct((m_dim, n_dim), out_dtype),
        grid=(m_dim // tm, n_dim // tn, nk),
        in_specs=in_specs,
        out_specs=pl.BlockSpec((tm, tn), lambda i, j, k: (i, j)),
        scratch_shapes=[pltpu.VMEM((tm, tn), F32)],
        compiler_params=_params(3),
    )(*operands)


def _rows(tr, w, cb=0):
    return pl.BlockSpec((tr, w), lambda i: (i, cb))


def _whole(shape):
    nd = len(shape)
    return pl.BlockSpec(tuple(shape), lambda i: (0,) * nd)


def _swiglu_fwd(h, name):
    t, f2 = h.shape
    f = f2 // 2
    tr = _pick(t, 256, 8)

    def body(g_ref, u_ref, a_ref):
        g = g_ref[...]
        a_ref[...] = (g * _sigmoid(g) * u_ref[...]).astype(BF16)

    return pl.pallas_call(
        body, name=name, out_shape=jax.ShapeDtypeStruct((t, f), BF16), grid=(t // tr,),
        in_specs=[_rows(tr, f, 0), _rows(tr, f, 1)], out_specs=_rows(tr, f),
        compiler_params=_params())(h, h)


def _swiglu_bwd(h, da, name):
    t, f2 = h.shape
    f = f2 // 2
    tr = _pick(t, 256, 8)

    def body(g_ref, u_ref, da_ref, dh_ref):
        g = g_ref[...]
        sg = _sigmoid(g)
        d = da_ref[...]
        dh_ref[:, :f] = (d * u_ref[...] * (sg * (1.0 + g * (1.0 - sg)))).astype(BF16)
        dh_ref[:, f:] = (d * (g * sg)).astype(BF16)

    return pl.pallas_call(
        body, name=name, out_shape=jax.ShapeDtypeStruct((t, f2), BF16), grid=(t // tr,),
        in_specs=[_rows(tr, f, 0), _rows(tr, f, 1), _rows(tr, f)], out_specs=_rows(tr, f2),
        compiler_params=_params())(h, h, da)


def _resid_ln_fwd(xin, f, g, b, alpha, scale, name):
    t, d = xin.shape
    tr = _pick(t, 256, 8)

    def body(x_ref, f_ref, g_ref, b_ref, r_ref, y_ref):
        r = alpha * x_ref[...] + scale * f_ref[...]
        xhat, _ = _ln_stats(r)
        r_ref[...] = r
        y_ref[...] = xhat * g_ref[...] + b_ref[...]

    return pl.pallas_call(
        body, name=name,
        out_shape=(jax.ShapeDtypeStruct((t, d), F32), jax.ShapeDtypeStruct((t, d), F32)),
        grid=(t // tr,),
        in_specs=[_rows(tr, d), _rows(tr, d), _whole((1, d)), _whole((1, d))],
        out_specs=(_rows(tr, d), _rows(tr, d)),
        compiler_params=_params())(xin, f, g, b)


def _ln_bwd_call(r, dy_a, dy_b, b_scale, g, out_scale, name):
    t, d = r.shape
    tr = _pick(t, 256, 8)
    has_b = dy_b is not None

    def body(*refs):
        if has_b:
            r_ref, da_ref, db_ref, g_ref, dr_ref, drs_ref, dg_ref, dbeta_ref = refs
            dy = da_ref[...] + b_scale * db_ref[...]
        else:
            r_ref, da_ref, g_ref, dr_ref, drs_ref, dg_ref, dbeta_ref = refs
            dy = da_ref[...]
        xhat, rstd = _ln_stats(r_ref[...])
        dr = _ln_bwd(dy, xhat, rstd, g_ref[...])
        dr_ref[...] = dr
        drs_ref[...] = (out_scale * dr).astype(BF16)

        @pl.when(pl.program_id(0) == 0)
        def _():
            dg_ref[...] = jnp.zeros_like(dg_ref)
            dbeta_ref[...] = jnp.zeros_like(dbeta_ref)

        dg_ref[...] += _fold8(dy * xhat)
        dbeta_ref[...] += _fold8(dy)

    ins = [r, dy_a] + ([dy_b] if has_b else []) + [g]
    in_specs = [_rows(tr, d)] * (3 if has_b else 2) + [_whole((1, d))]
    return pl.pallas_call(
        body, name=name,
        out_shape=(jax.ShapeDtypeStruct((t, d), F32), jax.ShapeDtypeStruct((t, d), BF16),
                   jax.ShapeDtypeStruct((8, d), F32), jax.ShapeDtypeStruct((8, d), F32)),
        grid=(t // tr,), in_specs=in_specs,
        out_specs=(_rows(tr, d), _rows(tr, d), _whole((8, d)), _whole((8, d))),
        compiler_params=_params())(*ins)


def _take_row(v, row_id, s):
    return jnp.sum(jnp.where(row_id == s, v, 0.0), axis=0, keepdims=True)


def _seg_carries(f_re, f_im, p_re, p_im, reverse):
    row_id = lax.broadcasted_iota(jnp.int32, f_re.shape, 0)
    p_re, p_im = _take_row(p_re, row_id, 0), _take_row(p_im, row_id, 0)
    out_re, out_im = jnp.zeros_like(f_re), jnp.zeros_like(f_im)
    c_re, c_im = jnp.zeros_like(p_re), jnp.zeros_like(p_im)
    order = range(N_SEG - 1, -1, -1) if reverse else range(N_SEG)
    for s in order:
        out_re = jnp.where(row_id == s, c_re, out_re)
        out_im = jnp.where(row_id == s, c_im, out_im)
        fr, fi = _take_row(f_re, row_id, s), _take_row(f_im, row_id, s)
        c_re, c_im = fr + p_re * c_re - p_im * c_im, fi + p_re * c_im + p_im * c_re
    return out_re, out_im


def _scan_fwd(bu_re, bu_im, a_re, a_im, n_seq, name):
    t, n = bu_re.shape
    seq = t // n_seq
    steps = seq // N_SEG
    cb = _pick(n, 256)

    def body(bre, bim, are, aim, hre, him):
        ar = jnp.broadcast_to(are[...], (N_SEG, cb))
        ai = jnp.broadcast_to(aim[...], (N_SEG, cb))
        zero = jnp.zeros((N_SEG, cb), F32)

        def local(k, carry):
            lr, li, pr, pi = carry
            rows = pl.ds(pl.multiple_of(k * N_SEG, N_SEG), N_SEG)
            nr = ar * lr - ai * li + bre[rows, :]
            ni = ar * li + ai * lr + bim[rows, :]
            hre[rows, :] = nr
            him[rows, :] = ni
            return nr, ni, ar * pr - ai * pi, ar * pi + ai * pr

        f_re, f_im, p_re, p_im = lax.fori_loop(0, steps, local, (zero, zero, zero + 1.0, zero))
        c_re, c_im = _seg_carries(f_re, f_im, p_re, p_im, reverse=False)

        def fix(k, carry):
            pr, pi = carry
            rows = pl.ds(pl.multiple_of(k * N_SEG, N_SEG), N_SEG)
            hre[rows, :] = hre[rows, :] + (pr * c_re - pi * c_im)
            him[rows, :] = him[rows, :] + (pr * c_im + pi * c_re)
            return ar * pr - ai * pi, ar * pi + ai * pr

        lax.fori_loop(0, steps, fix, (ar, ai))

    blk = pl.BlockSpec((seq, cb), lambda s, j: (s, j))
    vec = pl.BlockSpec((1, cb), lambda s, j: (0, j))
    return pl.pallas_call(
        body, name=name,
        out_shape=(jax.ShapeDtypeStruct((t, n), F32), jax.ShapeDtypeStruct((t, n), F32)),
        grid=(n_seq, n // cb), in_specs=[blk, blk, vec, vec], out_specs=(blk, blk),
        compiler_params=_params(2))(bu_re, bu_im, a_re, a_im)


def _scan_bwd(g_re, g_im, h_re, h_im, a_re, a_im, n_seq, name):
    t, n = g_re.shape
    seq = t // n_seq
    steps = seq // N_SEG
    cb = _pick(n, 256)

    def body(gre, gim, hre, him, are, aim, lre, lim, dare, daim):
        ar = jnp.broadcast_to(are[...], (N_SEG, cb))
        ai = -jnp.broadcast_to(aim[...], (N_SEG, cb))
        zero = jnp.zeros((N_SEG, cb), F32)

        def local(i, carry):
            lr, li, pr, pi = carry
            k = steps - 1 - i
            rows = pl.ds(pl.multiple_of(k * N_SEG, N_SEG), N_SEG)
            nr = ar * lr - ai * li + gre[rows, :]
            ni = ar * li + ai * lr + gim[rows, :]
            lre[rows, :] = nr
            lim[rows, :] = ni
            return nr, ni, ar * pr - ai * pi, ar * pi + ai * pr

        f_re, f_im, p_re, p_im = lax.fori_loop(0, steps, local, (zero, zero, zero + 1.0, zero))
        c_re, c_im = _seg_carries(f_re, f_im, p_re, p_im, reverse=True)

        def accumulate(lam_r, lam_i, hp_r, hp_i, acc_r, acc_i):
            return acc_r + (lam_r * hp_r + lam_i * hp_i), acc_i + (lam_i * hp_r - lam_r * hp_i)

        def fix(i, carry):
            pr, pi, acc_r, acc_i = carry
            k = steps - 1 - i
            rows = pl.ds(pl.multiple_of(k * N_SEG, N_SEG), N_SEG)
            prev = pl.ds(pl.multiple_of((k - 1) * N_SEG, N_SEG), N_SEG)
            lam_r = lre[rows, :] + (pr * c_re - pi * c_im)
            lam_i = lim[rows, :] + (pr * c_im + pi * c_re)
            lre[rows, :] = lam_r
            lim[rows, :] = lam_i
            acc_r, acc_i = accumulate(lam_r, lam_i, hre[prev, :], him[prev, :], acc_r, acc_i)
            return ar * pr - ai * pi, ar * pi + ai * pr, acc_r, acc_i

        pr, pi, acc_r, acc_i = lax.fori_loop(0, steps - 1, fix, (ar, ai, zero, zero))
        first = pl.ds(0, N_SEG)
        last = pl.ds((steps - 1) * N_SEG, N_SEG)
        lam_r = lre[first, :] + (pr * c_re - pi * c_im)
        lam_i = lim[first, :] + (pr * c_im + pi * c_re)
        lre[first, :] = lam_r
        lim[first, :] = lam_i
        row_id = lax.broadcasted_iota(jnp.int32, (N_SEG, cb), 0)
        hp_r = jnp.where(row_id == 0, 0.0, pltpu.roll(hre[last, :], 1, 0))
        hp_i = jnp.where(row_id == 0, 0.0, pltpu.roll(him[last, :], 1, 0))
        acc_r, acc_i = accumulate(lam_r, lam_i, hp_r, hp_i, acc_r, acc_i)
        dare[...] = acc_r
        daim[...] = acc_i

    blk = pl.BlockSpec((seq, cb), lambda s, j: (s, j))
    vec = pl.BlockSpec((1, cb), lambda s, j: (0, j))
    acc = pl.BlockSpec((N_SEG, cb), lambda s, j: (s, j))
    return pl.pallas_call(
        body, name=name,
        out_shape=(jax.ShapeDtypeStruct((t, n), F32), jax.ShapeDtypeStruct((t, n), F32),
                   jax.ShapeDtypeStruct((n_seq * N_SEG, n), F32),
                   jax.ShapeDtypeStruct((n_seq * N_SEG, n), F32)),
        grid=(n_seq, n // cb), in_specs=[blk, blk, blk, blk, vec, vec],
        out_specs=(blk, blk, acc, acc),
        compiler_params=_params(2))(g_re, g_im, h_re, h_im, a_re, a_im)


def _s5_post_fwd(yssm, u, d_skip, glu_w, glu_b, name):
    t, w = yssm.shape
    tr = _pick(t, 512, 8)

    def body(y_ref, u_ref, d_ref, w_ref, b_ref, o_ref):
        yg = _gelu(y_ref[...] + d_ref[...] * u_ref[...])
        pre = _dot(yg, w_ref[...]) + b_ref[...]
        o_ref[...] = yg * _sigmoid(pre)

    return pl.pallas_call(
        body, name=name, out_shape=jax.ShapeDtypeStruct((t, w), F32), grid=(t // tr,),
        in_specs=[_rows(tr, w), _rows(tr, w), _whole((1, w)), _whole((w, w)), _whole((1, w))],
        out_specs=_rows(tr, w), compiler_params=_params())(yssm, u, d_skip, glu_w, glu_b)


def _s5_post_bwd(yssm, u, dout, d_skip, glu_w, glu_b, name):
    t, w = yssm.shape
    tr = _pick(t, 512, 8)

    def body(y_ref, u_ref, do_ref, d_ref, w_ref, b_ref, dy_ref, du_ref, dw_ref, dd_ref, db_ref):
        uu = u_ref[...]
        y = y_ref[...] + d_ref[...] * uu
        yg = _gelu(y)
        sg = _sigmoid(_dot(yg, w_ref[...]) + b_ref[...])
        do = do_ref[...]
        dpre = do * yg * sg * (1.0 - sg)
        dyg = do * sg + _dot(dpre, w_ref[...], tb=True)
        dy = dyg * _gelu_grad(y)
        dy_ref[...] = dy.astype(BF16)
        du_ref[...] = dy * d_ref[...]

        @pl.when(pl.program_id(0) == 0)
        def _():
            dw_ref[...] = jnp.zeros_like(dw_ref)
            dd_ref[...] = jnp.zeros_like(dd_ref)
            db_ref[...] = jnp.zeros_like(db_ref)

        dw_ref[...] += _dot(yg, dpre, ta=True)
        dd_ref[...] += _fold8(dy * uu)
        db_ref[...] += _fold8(dpre)

    return pl.pallas_call(
        body, name=name,
        out_shape=(jax.ShapeDtypeStruct((t, w), BF16), jax.ShapeDtypeStruct((t, w), F32),
                   jax.ShapeDtypeStruct((w, w), F32), jax.ShapeDtypeStruct((8, w), F32),
                   jax.ShapeDtypeStruct((8, w), F32)),
        grid=(t // tr,),
        in_specs=[_rows(tr, w), _rows(tr, w), _rows(tr, w), _whole((1, w)), _whole((w, w)),
                  _whole((1, w))],
        out_specs=(_rows(tr, w), _rows(tr, w), _whole((w, w)), _whole((8, w)), _whole((8, w))),
        compiler_params=_params())(yssm, u, dout, d_skip, glu_w, glu_b)


def _head_select(parts, head_dim):
    col_head = lax.broadcasted_iota(jnp.int32, parts[0].shape, 1) // head_dim
    out = jnp.zeros_like(parts[0])
    for h, part in enumerate(parts):
        out = jnp.where(col_head == h, part, out)
    return out


def _gmlp_fwd(proj, ln_g, ln_b, ws, bs_cols, name):
    t = proj.shape[0]
    n_heads = ws.shape[0]
    w = bs_cols.shape[1]
    head_dim = w // n_heads
    cpb = _pick(t // CHUNK, 4, 1)
    tr = cpb * CHUNK

    def body(zu_ref, zv_ref, g_ref, b_ref, ws_ref, bs_ref, o_ref):
        for c in range(cpb):
            rows = pl.ds(c * CHUNK, CHUNK)
            xhat, _ = _ln_stats(_gelu(zv_ref[rows, :]))
            vn = (xhat * g_ref[...] + b_ref[...]).astype(BF16)
            s = _head_select([_dot(ws_ref[h], vn) for h in range(n_heads)], head_dim) + bs_ref[...]
            o_ref[rows, :] = _gelu(zu_ref[rows, :]) * s

    return pl.pallas_call(
        body, name=name, out_shape=jax.ShapeDtypeStruct((t, w), F32), grid=(t // tr,),
        in_specs=[_rows(tr, w, 1), _rows(tr, w, 2), _whole((1, w)), _whole((1, w)),
                  _whole(ws.shape), _whole(bs_cols.shape)],
        out_specs=_rows(tr, w), compiler_params=_params())(proj, proj, ln_g, ln_b, ws, bs_cols)


def _gmlp_bwd(proj, dout, ln_g, ln_b, ws, ws_t, bs_cols, name):
    t = proj.shape[0]
    n_heads = ws.shape[0]
    w = bs_cols.shape[1]
    head_dim = w // n_heads
    cpb = _pick(t // CHUNK, 4, 1)
    tr = cpb * CHUNK

    def body(zu_ref, zv_ref, do_ref, g_ref, b_ref, ws_ref, wst_ref, bs_ref,
             dzu_ref, dzv_ref, dws_ref, dbs_ref, dg_ref, dbeta_ref):
        @pl.when(pl.program_id(0) == 0)
        def _():
            dws_ref[...] = jnp.zeros_like(dws_ref)
            dbs_ref[...] = jnp.zeros_like(dbs_ref)
            dg_ref[...] = jnp.zeros_like(dg_ref)
            dbeta_ref[...] = jnp.zeros_like(dbeta_ref)

        col_head = lax.broadcasted_iota(jnp.int32, (CHUNK, w), 1) // head_dim
        for c in range(cpb):
            rows = pl.ds(c * CHUNK, CHUNK)
            zu, zv, do = zu_ref[rows, :], zv_ref[rows, :], do_ref[rows, :]
            xhat, rstd = _ln_stats(_gelu(zv))
            vn = (xhat * g_ref[...] + b_ref[...]).astype(BF16)
            s = _head_select([_dot(ws_ref[h], vn) for h in range(n_heads)], head_dim) + bs_ref[...]
            dzu_ref[rows, :] = (do * s * _gelu_grad(zu)).astype(BF16)
            ds = do * _gelu(zu)
            ds16 = ds.astype(BF16)
            dbs_ref[...] += ds
            for h in range(n_heads):
                dws_ref[h] += _dot(jnp.where(col_head == h, ds16, jnp.zeros_like(ds16)), vn, tb=True)
            dvn = _head_select([_dot(wst_ref[h], ds16) for h in range(n_heads)], head_dim)
            dg_ref[...] += _fold8(dvn * xhat)
            dbeta_ref[...] += _fold8(dvn)
            dgv = _ln_bwd(dvn, xhat, rstd, g_ref[...])
            dzv_ref[rows, :] = (dgv * _gelu_grad(zv)).astype(BF16)

    return pl.pallas_call(
        body, name=name,
        out_shape=(jax.ShapeDtypeStruct((t, w), BF16), jax.ShapeDtypeStruct((t, w), BF16),
                   jax.ShapeDtypeStruct(ws.shape, F32), jax.ShapeDtypeStruct((CHUNK, w), F32),
                   jax.ShapeDtypeStruct((8, w), F32), jax.ShapeDtypeStruct((8, w), F32)),
        grid=(t // tr,),
        in_specs=[_rows(tr, w, 1), _rows(tr, w, 2), _rows(tr, w), _whole((1, w)), _whole((1, w)),
                  _whole(ws.shape), _whole(ws.shape), _whole(bs_cols.shape)],
        out_specs=(_rows(tr, w), _rows(tr, w), _whole(ws.shape), _whole((CHUNK, w)),
                   _whole((8, w)), _whole((8, w))),
        compiler_params=_params())(proj, proj, dout, ln_g, ln_b, ws, ws_t, bs_cols)


def _merge_fwd(s5out, gm, proj, up_a, up_b, name):
    t, w = s5out.shape
    d = up_a.shape[1]
    assert d == 2 * w
    tr = _pick(t, 256, 8)

    def body(s_ref, gm_ref, ga0, ga1, gb0, gb1, ua_ref, ub_ref, m_ref, ya_ref, yb_ref):
        ya = _dot(s_ref[...], ua_ref[...])
        yb = _dot(gm_ref[...], ub_ref[...])
        ya_ref[...] = ya
        yb_ref[...] = yb
        for half, (ga, gb) in enumerate(((ga0, gb0), (ga1, gb1))):
            cols = slice(half * w, (half + 1) * w)
            m_ref[:, cols] = (_sigmoid(ga[...]) * ya[:, cols] + _sigmoid(gb[...]) * yb[:, cols]).astype(BF16)

    return pl.pallas_call(
        body, name=name,
        out_shape=(jax.ShapeDtypeStruct((t, d), BF16), jax.ShapeDtypeStruct((t, d), F32),
                   jax.ShapeDtypeStruct((t, d), F32)),
        grid=(t // tr,),
        in_specs=[_rows(tr, w), _rows(tr, w), _rows(tr, w, 3), _rows(tr, w, 4), _rows(tr, w, 5),
                  _rows(tr, w, 6), _whole(up_a.shape), _whole(up_b.shape)],
        out_specs=(_rows(tr, d), _rows(tr, d), _rows(tr, d)),
        compiler_params=_params())(s5out, gm, proj, proj, proj, proj, up_a, up_b)


def _merge_bwd(dm, ya, yb, s5out, gm, proj, up_a, up_b, name):
    t, d = dm.shape
    w = d // 2
    tr = _pick(t, 256, 8)

    def body(dm_ref, ya_ref, yb_ref, s_ref, gm_ref, ga0, ga1, gb0, gb1, ua_ref, ub_ref,
             dga_ref, dgb_ref, ds_ref, dgm_ref, dua_ref, dub_ref):
        dm_v, ya, yb = dm_ref[...], ya_ref[...], yb_ref[...]
        dya, dyb = [], []
        for half, (ga, gb) in enumerate(((ga0, gb0), (ga1, gb1))):
            cols = slice(half * w, (half + 1) * w)
            sa, sb = _sigmoid(ga[...]), _sigmoid(gb[...])
            dmh = dm_v[:, cols]
            dga_ref[:, cols] = (dmh * ya[:, cols] * sa * (1.0 - sa)).astype(BF16)
            dgb_ref[:, cols] = (dmh * yb[:, cols] * sb * (1.0 - sb)).astype(BF16)
            dya.append((dmh * sa).astype(BF16))
            dyb.append((dmh * sb).astype(BF16))
        dya = jnp.concatenate(dya, axis=1)
        dyb = jnp.concatenate(dyb, axis=1)
        ds_ref[...] = _dot(dya, ua_ref[...], tb=True)
        dgm_ref[...] = _dot(dyb, ub_ref[...], tb=True)

        @pl.when(pl.program_id(0) == 0)
        def _():
            dua_ref[...] = jnp.zeros_like(dua_ref)
            dub_ref[...] = jnp.zeros_like(dub_ref)

        dua_ref[...] += _dot(s_ref[...], dya, ta=True)
        dub_ref[...] += _dot(gm_ref[...], dyb, ta=True)

    return pl.pallas_call(
        body, name=name,
        out_shape=(jax.ShapeDtypeStruct((t, d), BF16), jax.ShapeDtypeStruct((t, d), BF16),
                   jax.ShapeDtypeStruct((t, w), F32), jax.ShapeDtypeStruct((t, w), F32),
                   jax.ShapeDtypeStruct(up_a.shape, F32), jax.ShapeDtypeStruct(up_b.shape, F32)),
        grid=(t // tr,),
        in_specs=[_rows(tr, d), _rows(tr, d), _rows(tr, d), _rows(tr, w), _rows(tr, w),
                  _rows(tr, w, 3), _rows(tr, w, 4), _rows(tr, w, 5), _rows(tr, w, 6),
                  _whole(up_a.shape), _whole(up_b.shape)],
        out_specs=(_rows(tr, d), _rows(tr, d), _rows(tr, w), _rows(tr, w), _whole(up_a.shape),
                   _whole(up_b.shape)),
        compiler_params=_params())(dm, ya, yb, s5out, gm, proj, proj, proj, proj, up_a, up_b)


def _head(x3, p, target, w_gate, w_proj, name):
    t, d = x3.shape
    pd = p.shape[1]
    tr = _pick(t, 256, 8)
    nb = t // tr

    def body(x_ref, p_ref, t_ref, wg_ref, wp_ref, loss_ref, dx_ref, dwg_ref, dwp_ref):
        xv = x_ref[...]
        sg = _sigmoid(_dot(xv, wg_ref[...]))
        pp = _dot(p_ref[...], wp_ref[...])
        err = xv + sg * pp - t_ref[...]
        loss_ref[...] = jnp.full((8, 128), 0.5 * jnp.sum(jnp.mean(err * err, axis=-1)), F32)
        dout = err * (1.0 / d)
        dgpre = dout * pp * sg * (1.0 - sg)
        dpp = dout * sg
        dx_ref[...] = dout + _dot(dgpre, wg_ref[...], tb=True)

        @pl.when(pl.program_id(0) == 0)
        def _():
            dwg_ref[...] = jnp.zeros_like(dwg_ref)
            dwp_ref[...] = jnp.zeros_like(dwp_ref)

        dwg_ref[...] += _dot(xv, dgpre, ta=True)
        dwp_ref[...] += _dot(p_ref[...], dpp, ta=True)

    return pl.pallas_call(
        body, name=name,
        out_shape=(jax.ShapeDtypeStruct((nb * 8, 128), F32), jax.ShapeDtypeStruct((t, d), F32),
                   jax.ShapeDtypeStruct((d, d), F32), jax.ShapeDtypeStruct((pd, d), F32)),
        grid=(nb,),
        in_specs=[_rows(tr, d), _rows(tr, pd), _rows(tr, d), _whole((d, d)), _whole((pd, d))],
        out_specs=(pl.BlockSpec((8, 128), lambda i: (i, 0)), _rows(tr, d), _whole((d, d)),
                   _whole((pd, d))),
        compiler_params=_params())(x3, p, target, w_gate, w_proj)


_HBM = pl.BlockSpec(memory_space=pltpu.HBM)


def _all_gather(packed, name):
    r, w = packed.shape

    def body(x_ref, out_ref, send_sems, recv_sems, local_sem):
        x, y, c = lax.axis_index("x"), lax.axis_index("y"), lax.axis_index("c")
        me, sibling = (x, y, c), (x, y, 1 - c)
        chips = [(1 - x, y), (x, 1 - y), (1 - x, 1 - y)]

        def slot(px, py, pc):
            return out_ref.at[4 * px + 2 * py + pc]

        def copy(k, block, to, src=None):
            return pltpu.make_async_remote_copy(
                src_ref=slot(*block) if src is None else src, dst_ref=slot(*block),
                send_sem=send_sems.at[k], recv_sem=recv_sems.at[k],
                device_id=to, device_id_type=pl.DeviceIdType.MESH)

        mine = pltpu.make_async_copy(x_ref, slot(*me), local_sem)
        mine.start()
        first = [copy(0, me, sibling, src=x_ref)]
        first += [copy(1 + j, me, (*chip, c), src=x_ref) for j, chip in enumerate(chips)]
        for cp in first:
            cp.start()
        passed = [copy(4 + j, (*chip, c), sibling) for j, chip in enumerate(chips)]
        for j, chip in enumerate(chips):
            copy(1 + j, (*chip, c), me).wait_recv()
            passed[j].start()
        copy(0, sibling, me).wait_recv()
        for j, chip in enumerate(chips):
            copy(4 + j, (*chip, 1 - c), me).wait_recv()
        for cp in first + passed:
            cp.wait_send()
        mine.wait()

    return pl.pallas_call(
        body, name=name, out_shape=jax.ShapeDtypeStruct((N_DEV, r, w), packed.dtype),
        in_specs=[_HBM], out_specs=_HBM,
        scratch_shapes=[pltpu.SemaphoreType.DMA((7,)), pltpu.SemaphoreType.DMA((7,)),
                        pltpu.SemaphoreType.DMA],
    )(packed)


def _exchange(send, name):
    def body(s_ref, r_ref, send_sems, recv_sems, local_sem):
        x, y, c = lax.axis_index("x"), lax.axis_index("y"), lax.axis_index("c")
        me = 4 * x + 2 * y + c
        mine = pltpu.make_async_copy(s_ref.at[me], r_ref.at[me], local_sem)
        mine.start()
        copies = []
        for k in range(1, N_DEV):
            px = 1 - x if k & 4 else x
            py = 1 - y if k & 2 else y
            pc = 1 - c if k & 1 else c
            copies.append(pltpu.make_async_remote_copy(
                src_ref=s_ref.at[4 * px + 2 * py + pc], dst_ref=r_ref.at[me],
                send_sem=send_sems.at[k - 1], recv_sem=recv_sems.at[k - 1],
                device_id=(px, py, pc), device_id_type=pl.DeviceIdType.MESH))
        for cp in copies:
            cp.start()
        for cp in copies:
            cp.wait()
        mine.wait()

    return pl.pallas_call(
        body, name=name, out_shape=jax.ShapeDtypeStruct(send.shape, send.dtype),
        in_specs=[_HBM], out_specs=_HBM,
        scratch_shapes=[pltpu.SemaphoreType.DMA((7,)), pltpu.SemaphoreType.DMA((7,)),
                        pltpu.SemaphoreType.DMA],
    )(send)


def _adamw(recv, w, m, v, name):
    n, rows, width = recv.shape
    tr = _pick(rows, 128, 8)
    c_m = 1.0 - ADAM_B1 ** ADAM_STEP
    c_v = 1.0 - ADAM_B2 ** ADAM_STEP

    def body(r_ref, w_ref, m_ref, v_ref, g_ref, d_ref, nm_ref, nv_ref):
        g = r_ref[0]
        for i in range(1, n):
            g = g + r_ref[i]
        m2 = ADAM_B1 * m_ref[...] + (1.0 - ADAM_B1) * g
        v2 = ADAM_B2 * v_ref[...] + (1.0 - ADAM_B2) * (g * g)
        m_hat = m2 / c_m
        v_hat = v2 / c_v
        g_ref[...] = g
        d_ref[...] = -ADAM_LR * (m_hat / (jnp.sqrt(v_hat) + ADAM_EPS) + ADAM_WD * w_ref[...])
        nm_ref[...] = m2
        nv_ref[...] = v2

    blk = _rows(tr, width)
    shp = jax.ShapeDtypeStruct((rows, width), F32)
    return pl.pallas_call(
        body, name=name, out_shape=(shp, shp, shp, shp), grid=(rows // tr,),
        in_specs=[pl.BlockSpec((n, tr, width), lambda i: (0, i, 0)), blk, blk, blk],
        out_specs=(blk, blk, blk, blk), compiler_params=_params())(recv, w, m, v)


def _padded_rows(n_elems):
    per = PACK_W * PACK_ROW_ALIGN
    return -(-n_elems // per) * PACK_ROW_ALIGN


def _to_rows(flat):
    n = flat.shape[-1]
    rows = _padded_rows(n)
    pad = [(0, 0)] * (flat.ndim - 1) + [(0, rows * PACK_W - n)]
    return jnp.pad(flat, pad).reshape(flat.shape[:-1] + (rows, PACK_W))


def _split_shards(name, full):
    r, c = full.shape
    if name in COL_SHARDED:
        return full.reshape(r, N_DEV, c // N_DEV).transpose(1, 0, 2).reshape(N_DEV, -1)
    return full.reshape(N_DEV, -1)


def _join_shards(name, parts, shard_shape):
    r, c = shard_shape
    parts = parts.reshape(N_DEV, r, c)
    if name in COL_SHARDED:
        return parts.transpose(1, 0, 2).reshape(r, N_DEV * c)
    return parts.reshape(N_DEV * r, c)


def _discretise(lam_re, lam_im, log_dt, b_re, b_im):
    dt = jnp.exp(log_dt)[:, None]
    mag = jnp.exp(lam_re * dt)
    ab_re = mag * jnp.cos(lam_im * dt)
    ab_im = mag * jnp.sin(lam_im * dt)
    nr = ab_re - 1.0
    ni = ab_im
    den = lam_re * lam_re + lam_im * lam_im
    coef_re = ((nr * lam_re + ni * lam_im) / den)[..., None]
    coef_im = ((ni * lam_re - nr * lam_im) / den)[..., None]
    bb_re = coef_re * b_re - coef_im * b_im
    bb_im = coef_re * b_im + coef_im * b_re
    return ab_re, ab_im, bb_re, bb_im


def _same_group(g):
    return jnp.eye(g, dtype=bool)[:, None, :, None]


def _block_diag_in(bb):
    g, p, i = bb.shape
    placed = jnp.where(_same_group(g), bb.transpose(0, 2, 1)[:, :, None, :], 0.0)
    return placed.reshape(g * i, g * p)


def _block_diag_in_take(dense, g, p, i):
    blocks = jnp.where(_same_group(g), dense.reshape(g, i, g, p), 0.0).sum(axis=2)
    return blocks.transpose(0, 2, 1)


def _block_diag_out(cc):
    g, i, p = cc.shape
    placed = jnp.where(_same_group(g), cc.transpose(0, 2, 1)[:, :, None, :], 0.0)
    return placed.reshape(g * p, g * i)


def _block_diag_out_take(dense, g, i, p):
    blocks = jnp.where(_same_group(g), dense.reshape(g, p, g, i), 0.0).sum(axis=2)
    return blocks.transpose(0, 2, 1)


def _perm_rows(z, n_seq):
    t, w = z.shape
    steps = t // n_seq // N_SEG
    return z.reshape(n_seq, N_SEG, steps, w).transpose(0, 2, 1, 3).reshape(t, w)


def _unperm_rows(z, n_seq):
    t, w = z.shape
    steps = t // n_seq // N_SEG
    return z.reshape(n_seq, steps, N_SEG, w).transpose(0, 2, 1, 3).reshape(t, w)


def kernel(*args):
    assert len(args) == len(INPUT_NAMES)
    inp = dict(zip(INPUT_NAMES, args))
    depth = inp["ffn1_w_in"].shape[0]
    assert depth == 1
    alpha = (2.0 * depth) ** 0.25

    n_seq, seq, d_model = inp["x"].shape
    t = n_seq * seq
    x = inp["x"].reshape(t, d_model)
    p = inp["p"][0].reshape(t, -1)
    target = inp["loss_target"].reshape(t, d_model)
    wts = {n: inp[n][0] for n in WEIGHTS}
    mom = {n: inp["m_" + n][0] for n in WEIGHTS}
    var = {n: inp["v_" + n][0] for n in WEIGHTS}

    shard_shapes = {n: wts[n].shape for n in SHARDED}
    packed = jnp.concatenate([_to_rows(wts[n].reshape(-1)) for n in SHARDED], axis=0).astype(BF16)
    gathered = _all_gather(packed, "weights_all_gather")
    full, row0 = {}, 0
    for n in SHARDED:
        r, c = shard_shapes[n]
        rows = _padded_rows(r * c)
        part = gathered[:, row0:row0 + rows, :].reshape(N_DEV, -1)[:, :r * c]
        full[n] = _join_shards(n, part, (r, c))
        row0 += rows

    def row(v):
        return v.reshape(1, -1)

    n_groups, n_state = wts["ssm_lambda_re"].shape
    n_ch = wts["ssm_b_re"].shape[2]
    disc_in = (wts["ssm_lambda_re"], wts["ssm_lambda_im"], wts["ssm_log_dt"], wts["ssm_b_re"],
               wts["ssm_b_im"])
    (ab_re, ab_im, bb_re, bb_im), disc_vjp = jax.vjp(_discretise, *disc_in)
    a_re, a_im = row(ab_re), row(ab_im)
    b_dense_re = _block_diag_in(bb_re).astype(BF16)
    b_dense_im = _block_diag_in(bb_im).astype(BF16)
    c_dense_re = _block_diag_out(wts["ssm_c_re"]).astype(BF16)
    c_dense_im = _block_diag_out(-wts["ssm_c_im"]).astype(BF16)

    ws = wts["gmlp_w_s"]
    n_heads = ws.shape[0]
    d_gmlp = wts["gmlp_ln_g"].shape[0]
    causal = jnp.tril(jnp.ones((CHUNK, CHUNK), dtype=bool))
    ws_masked = jnp.where(causal[None], ws, 0.0).astype(BF16)
    ws_masked_t = ws_masked.transpose(0, 2, 1)
    bs_cols = jnp.repeat(wts["gmlp_b_s"].T, d_gmlp // n_heads, axis=1)
    d_ssm = n_groups * n_ch
    assert d_ssm == d_gmlp and d_model == 2 * d_ssm

    h1 = _mm(x, full["ffn1_w_in"], name="ffn1_in")
    a1 = _swiglu_fwd(h1, "ffn1_act")
    f1 = _mm(a1, full["ffn1_w_out"], name="ffn1_out")
    r1, x1 = _resid_ln_fwd(x, f1, row(wts["ln1_g"]), row(wts["ln1_b"]), alpha, 0.5, "ln1")

    proj = _mm(x1, full["mix_w_in"], name="mix_in")
    za_p = _perm_rows(proj[:, :d_ssm], n_seq)
    bu_re = _mm(za_p, b_dense_re, name="s5_bu_re")
    bu_im = _mm(za_p, b_dense_im, name="s5_bu_im")
    h_re, h_im = _scan_fwd(bu_re, bu_im, a_re, a_im, n_seq, "s5_scan")
    yssm = _mm(h_re, c_dense_re, name="s5_y_re")
    yssm = _mm(h_im, c_dense_im, name="s5_y_im", add=yssm)
    s5out_p = _s5_post_fwd(yssm, za_p, row(wts["ssm_d"]), full["ssm_glu_w"], row(wts["ssm_glu_b"]),
                           "s5_post")
    s5out = _unperm_rows(s5out_p, n_seq)
    gm = _gmlp_fwd(proj, row(wts["gmlp_ln_g"]), row(wts["gmlp_ln_b"]), ws_masked, bs_cols, "gmlp")
    m_mix, y_a, y_b = _merge_fwd(s5out, gm, proj, full["up_a"], full["up_b"], "merge")
    mixed = _mm(m_mix, full["mix_w_out"], name="mix_out")
    r2, x2 = _resid_ln_fwd(x1, mixed, row(wts["ln2_g"]), row(wts["ln2_b"]), alpha, 1.0, "ln2")

    h2 = _mm(x2, full["ffn2_w_in"], name="ffn2_in")
    a2 = _swiglu_fwd(h2, "ffn2_act")
    f2 = _mm(a2, full["ffn2_w_out"], name="ffn2_out")
    r3, x3 = _resid_ln_fwd(x2, f2, row(wts["ln3_g"]), row(wts["ln3_b"]), alpha, 0.5, "ln3")

    grads = {}
    loss_parts, dx3, grads["ple_w_gate"], grads["ple_w_proj"] = _head(
        x3, p, target, full["ple_w_gate"], full["ple_w_proj"], "head")
    loss = lax.psum(jnp.sum(loss_parts[::8, 0]), ("x", "y", "c"))

    dr3, dr3_h, dg, db = _ln_bwd_call(r3, dx3, None, 1.0, row(wts["ln3_g"]), 0.5, "ln3_bwd")
    grads["ln3_g"], grads["ln3_b"] = dg.sum(0), db.sum(0)
    da2 = _mm(dr3_h, full["ffn2_w_out"], tb=True, name="ffn2_out_dx")
    grads["ffn2_w_out"] = _mm(a2, dr3_h, ta=True, name="ffn2_out_dw")
    dh2 = _swiglu_bwd(h2, da2, "ffn2_act_bwd")
    grads["ffn2_w_in"] = _mm(x2, dh2, ta=True, name="ffn2_in_dw")
    dx2_f = _mm(dh2, full["ffn2_w_in"], tb=True, name="ffn2_in_dx")

    dr2, dr2_h, dg, db = _ln_bwd_call(r2, dx2_f, dr3, alpha, row(wts["ln2_g"]), 1.0, "ln2_bwd")
    grads["ln2_g"], grads["ln2_b"] = dg.sum(0), db.sum(0)
    dm = _mm(dr2_h, full["mix_w_out"], tb=True, name="mix_out_dx")
    grads["mix_w_out"] = _mm(m_mix, dr2_h, ta=True, name="mix_out_dw")
    dga, dgb, ds5out, dgm, grads["up_a"], grads["up_b"] = _merge_bwd(
        dm, y_a, y_b, s5out, gm, proj, full["up_a"], full["up_b"], "merge_bwd")

    dzu, dzv, dws, dbs_cols, dg, db = _gmlp_bwd(
        proj, dgm, row(wts["gmlp_ln_g"]), row(wts["gmlp_ln_b"]), ws_masked, ws_masked_t, bs_cols,
        "gmlp_bwd")
    grads["gmlp_ln_g"], grads["gmlp_ln_b"] = dg.sum(0), db.sum(0)
    grads["gmlp_w_s"] = jnp.where(causal[None], dws, 0.0)
    grads["gmlp_b_s"] = dbs_cols.reshape(CHUNK, n_heads, -1).sum(-1).T

    ds5out_p = _perm_rows(ds5out, n_seq)
    dy_p, dza_skip, grads["ssm_glu_w"], dd, dgb_glu = _s5_post_bwd(
        yssm, za_p, ds5out_p, row(wts["ssm_d"]), full["ssm_glu_w"], row(wts["ssm_glu_b"]),
        "s5_post_bwd")
    grads["ssm_d"], grads["ssm_glu_b"] = dd.sum(0), dgb_glu.sum(0)
    g_re = _mm(dy_p, c_dense_re, tb=True, name="s5_y_re_dx")
    g_im = _mm(dy_p, c_dense_im, tb=True, name="s5_y_im_dx")
    dc_re = _mm(h_re, dy_p, ta=True, name="s5_y_re_dw")
    dc_im = _mm(h_im, dy_p, ta=True, name="s5_y_im_dw")
    grads["ssm_c_re"] = _block_diag_out_take(dc_re, n_groups, n_ch, n_state)
    grads["ssm_c_im"] = -_block_diag_out_take(dc_im, n_groups, n_ch, n_state)
    lam_re, lam_im, da_re, da_im = _scan_bwd(g_re, g_im, h_re, h_im, a_re, a_im, n_seq, "s5_scan_bwd")
    dza_p = _mm(lam_re, b_dense_re, tb=True, name="s5_bu_re_dx", add=dza_skip)
    dza_p = _mm(lam_im, b_dense_im, tb=True, name="s5_bu_im_dx", add=dza_p, out_dtype=BF16)
    dbb_re = _block_diag_in_take(_mm(za_p, lam_re, ta=True, name="s5_bu_re_dw"), n_groups, n_state, n_ch)
    dbb_im = _block_diag_in_take(_mm(za_p, lam_im, ta=True, name="s5_bu_im_dw"), n_groups, n_state, n_ch)
    dab_re = da_re.sum(0).reshape(n_groups, n_state)
    dab_im = da_im.sum(0).reshape(n_groups, n_state)
    (grads["ssm_lambda_re"], grads["ssm_lambda_im"], grads["ssm_log_dt"], grads["ssm_b_re"],
     grads["ssm_b_im"]) = disc_vjp((dab_re, dab_im, dbb_re, dbb_im))

    dproj = jnp.concatenate([_unperm_rows(dza_p, n_seq), dzu, dzv, dga, dgb], axis=1)
    grads["mix_w_in"] = _mm(x1, dproj, ta=True, name="mix_in_dw")
    dx1_f = _mm(dproj, full["mix_w_in"], tb=True, name="mix_in_dx")

    dr1, dr1_h, dg, db = _ln_bwd_call(r1, dx1_f, dr2, alpha, row(wts["ln1_g"]), 0.5, "ln1_bwd")
    grads["ln1_g"], grads["ln1_b"] = dg.sum(0), db.sum(0)
    da1 = _mm(dr1_h, full["ffn1_w_out"], tb=True, name="ffn1_out_dx")
    grads["ffn1_w_out"] = _mm(a1, dr1_h, ta=True, name="ffn1_out_dw")
    dh1 = _swiglu_bwd(h1, da1, "ffn1_act_bwd")
    grads["ffn1_w_in"] = _mm(x, dh1, ta=True, name="ffn1_in_dw")
    grad_x = _mm(dh1, full["ffn1_w_in"], tb=True, name="ffn1_in_dx", add=dr1, add_scale=alpha)

    order = SHARDED + SMALL
    send_parts, w_parts, m_parts, v_parts = [], [], [], []
    for n in order:
        if n in SHARDED:
            send_parts.append(_to_rows(_split_shards(n, grads[n])))
        else:
            one = _to_rows(grads[n].reshape(-1))
            send_parts.append(jnp.broadcast_to(one[None], (N_DEV,) + one.shape))
        w_parts.append(_to_rows(wts[n].reshape(-1)))
        m_parts.append(_to_rows(mom[n].reshape(-1)))
        v_parts.append(_to_rows(var[n].reshape(-1)))
    total = sum(part.shape[0] for part in w_parts)
    tail = -total % 128

    def cat(parts, axis):
        out = jnp.concatenate(parts, axis=axis)
        pad = [(0, 0)] * out.ndim
        pad[axis] = (0, tail)
        return jnp.pad(out, pad)

    recv = _exchange(cat(send_parts, 1), "grad_exchange")
    g_pk, d_pk, m_pk, v_pk = _adamw(recv, cat(w_parts, 0), cat(m_parts, 0), cat(v_parts, 0), "adamw")

    def unpack(pk):
        out, r0 = {}, 0
        for n, part in zip(order, w_parts):
            rows = part.shape[0]
            shape = inp[n].shape
            out[n] = pk[r0:r0 + rows].reshape(-1)[:math.prod(shape)].reshape(shape)
            r0 += rows
        return out

    g_out, d_out, m_out, v_out = unpack(g_pk), unpack(d_pk), unpack(m_pk), unpack(v_pk)
    outs = [loss, grad_x.reshape(n_seq, seq, d_model)]
    for group in (g_out, d_out, m_out, v_out):
        outs += [group[n] for n in WEIGHTS]
    return tuple(outs)
```

```python
import functools
import math

import jax
import jax.numpy as jnp
from jax import lax
from jax.experimental import pallas as pl
from jax.experimental.pallas import tpu as pltpu

F32 = jnp.float32
BF16 = jnp.bfloat16

N_DEV = 8
LN_EPS = 1e-5
CHUNK = 128
N_SEG = 8
PACK_W = 1024
VMEM_LIMIT_BYTES = 56 * 1024 * 1024
MM_OPERAND_BLOCK_BYTES = 8 * 1024 * 1024
ADAM_BLOCK_BYTES = 6 * 1024 * 1024

ADAM_LR = 0.001
ADAM_B1 = 0.9
ADAM_B2 = 0.999
ADAM_EPS = 1e-08
ADAM_WD = 0.01
ADAM_STEP = 10

COL_SHARDED = ("ffn1_w_in", "mix_w_in", "up_a", "up_b", "ffn2_w_in", "ple_w_proj")
ROW_SHARDED = ("ffn1_w_out", "ssm_glu_w", "mix_w_out", "ffn2_w_out", "ple_w_gate")
SHARDED = ("ffn1_w_in", "ffn1_w_out", "mix_w_in", "ssm_glu_w", "up_a", "up_b", "mix_w_out",
           "ffn2_w_in", "ffn2_w_out", "ple_w_proj", "ple_w_gate")
WEIGHTS = ("ffn1_w_in", "ffn1_w_out", "ln1_g", "ln1_b", "mix_w_in", "ssm_lambda_re", "ssm_lambda_im",
           "ssm_log_dt", "ssm_b_re", "ssm_b_im", "ssm_c_re", "ssm_c_im", "ssm_d", "ssm_glu_w",
           "ssm_glu_b", "gmlp_ln_g", "gmlp_ln_b", "gmlp_w_s", "gmlp_b_s", "up_a", "up_b", "mix_w_out",
           "ln2_g", "ln2_b", "ffn2_w_in", "ffn2_w_out", "ln3_g", "ln3_b", "ple_w_proj", "ple_w_gate")
SMALL = tuple(n for n in WEIGHTS if n not in SHARDED)
INPUT_NAMES = ("x", "p") + WEIGHTS + ("loss_target",) + tuple("m_" + n for n in WEIGHTS) + tuple(
    "v_" + n for n in WEIGHTS)


def _pick(dim, pref, mult=128):
    if dim <= pref:
        return dim
    d = (pref // mult) * mult
    while d >= mult:
        if dim % d == 0:
            return d
        d -= mult
    return dim


def _params(n_axes=1):
    return pltpu.CompilerParams(dimension_semantics=("arbitrary",) * n_axes,
                                vmem_limit_bytes=VMEM_LIMIT_BYTES)


def _sigmoid(v):
    return 1.0 / (1.0 + jnp.exp(-v))


_GELU_C = math.sqrt(2.0 / math.pi)


def _gelu(v):
    return 0.5 * v * (1.0 + jnp.tanh(_GELU_C * (v + 0.044715 * (v * v * v))))


def _gelu_grad(v):
    t = jnp.tanh(_GELU_C * (v + 0.044715 * (v * v * v)))
    return 0.5 * (1.0 + t) + 0.5 * v * (1.0 - t * t) * (_GELU_C * (1.0 + 3.0 * 0.044715 * v * v))


def _ln_stats(r):
    mu = jnp.mean(r, axis=-1, keepdims=True)
    xc = r - mu
    var = jnp.mean(xc * xc, axis=-1, keepdims=True)
    rstd = lax.rsqrt(var + LN_EPS)
    return xc * rstd, rstd


def _ln_bwd(dy, xhat, rstd, g):
    dxh = dy * g
    m1 = jnp.mean(dxh, axis=-1, keepdims=True)
    m2 = jnp.mean(dxh * xhat, axis=-1, keepdims=True)
    return rstd * (dxh - m1 - xhat * m2)


def _fold8(v):
    rows, w = v.shape
    return jnp.sum(v.reshape(rows // 8, 8, w), axis=0)


def _dot(a, b, ta=False, tb=False):
    dn = (((0 if ta else 1,), (1 if tb else 0,)), ((), ()))
    return lax.dot_general(a.astype(BF16), b.astype(BF16), dn, preferred_element_type=F32)


def _mm(a, b, *, name, ta=False, tb=False, out_dtype=F32, add=None, add_scale=1.0,
        tm=2048, tn=512, tk=4096):
    m_dim = a.shape[1] if ta else a.shape[0]
    k_dim = a.shape[0] if ta else a.shape[1]
    n_dim = b.shape[0] if tb else b.shape[1]
    assert (b.shape[1] if tb else b.shape[0]) == k_dim, (name, a.shape, b.shape)
    tk = _pick(k_dim, tk)
    tm = min(tm, max(256, MM_OPERAND_BLOCK_BYTES // (tk * a.dtype.itemsize)))
    tm, tn = _pick(m_dim, tm), _pick(n_dim, tn)
    nk = k_dim // tk
    has_add = add is not None

    def finish(r, add_ref, o_ref):
        if has_add:
            r = r + add_scale * add_ref[...].astype(F32)
        o_ref[...] = r.astype(out_dtype)

    def body(*refs):
        a_ref, b_ref = refs[:2]
        add_ref = refs[2] if has_add else None
        o_ref = refs[3] if has_add else refs[2]
        if nk == 1:
            finish(_dot(a_ref[...], b_ref[...], ta, tb), add_ref, o_ref)
            return
        acc_ref = refs[-1]
        k = pl.program_id(2)

        @pl.when(k == 0)
        def _():
            acc_ref[...] = jnp.zeros_like(acc_ref)

        acc_ref[...] += _dot(a_ref[...], b_ref[...], ta, tb)

        @pl.when(k == nk - 1)
        def _():
            finish(acc_ref[...], add_ref, o_ref)

    a_spec = (pl.BlockSpec((tk, tm), lambda i, j, k: (k, i)) if ta
              else pl.BlockSpec((tm, tk), lambda i, j, k: (i, k)))
    b_spec = (pl.BlockSpec((tn, tk), lambda i, j, k: (j, k)) if tb
              else pl.BlockSpec((tk, tn), lambda i, j, k: (k, j)))
    in_specs = [a_spec, b_spec]
    operands = [a, b]
    if has_add:
        in_specs.append(pl.BlockSpec((tm, tn), lambda i, j, k: (i, j)))
        operands.append(add)
    return pl.pallas_call(
        body, name=name,
        out_shape=jax.ShapeDtypeStruct((m_dim, n_dim), out_dtype),
        grid=(m_dim // tm, n_dim // tn, nk),
        in_specs=in_specs,
        out_specs=pl.BlockSpec((tm, tn), lambda i, j, k: (i, j)),
        scratch_shapes=[pltpu.VMEM((tm, tn), F32)] if nk > 1 else [],
        compiler_params=_params(3),
    )(*operands)


def _rows(tr, w, cb=0):
    return pl.BlockSpec((tr, w), lambda i: (i, cb))


def _whole(shape):
    nd = len(shape)
    return pl.BlockSpec(tuple(shape), lambda i: (0,) * nd)


def _ffn_in(x16, w_in, name):
    t, d = x16.shape
    _, nb, _, fb = w_in.shape
    tm = _pick(t, 1024, 8)

    def body(x_ref, wg_ref, wu_ref, h_ref, a_ref):
        xv = x_ref[...]
        g = _dot(xv, wg_ref[0, 0])
        u = _dot(xv, wu_ref[0, 0])
        h_ref[0, 0] = g
        h_ref[0, 1] = u
        a_ref[0] = (g * _sigmoid(g) * u).astype(BF16)

    return pl.pallas_call(
        body, name=name,
        out_shape=(jax.ShapeDtypeStruct((nb, 2, t, fb), F32), jax.ShapeDtypeStruct((nb, t, fb), BF16)),
        grid=(t // tm, nb),
        in_specs=[pl.BlockSpec((tm, d), lambda i, j: (i, 0)),
                  pl.BlockSpec((1, 1, d, fb), lambda i, j: (0, j, 0, 0)),
                  pl.BlockSpec((1, 1, d, fb), lambda i, j: (1, j, 0, 0))],
        out_specs=(pl.BlockSpec((1, 2, tm, fb), lambda i, j: (j, 0, i, 0)),
                   pl.BlockSpec((1, tm, fb), lambda i, j: (j, i, 0))),
        compiler_params=_params(2))(x16, w_in, w_in)


def _ffn_out_ln(a, w_out, xin, g, b, alpha, name):
    nb, t, fb = a.shape
    d = w_out.shape[2]
    tm = _pick(t, 256, 8)

    def body(a_ref, w_ref, x_ref, g_ref, b_ref, r_ref, y_ref, y16_ref):
        f = _dot(a_ref[0], w_ref[0])
        for jb in range(1, nb):
            f = f + _dot(a_ref[jb], w_ref[jb])
        r = alpha * x_ref[...] + 0.5 * f
        xhat, _ = _ln_stats(r)
        y = xhat * g_ref[...] + b_ref[...]
        r_ref[...] = r
        y_ref[...] = y
        y16_ref[...] = y.astype(BF16)

    return pl.pallas_call(
        body, name=name,
        out_shape=(jax.ShapeDtypeStruct((t, d), F32), jax.ShapeDtypeStruct((t, d), F32),
                   jax.ShapeDtypeStruct((t, d), BF16)),
        grid=(t // tm,),
        in_specs=[pl.BlockSpec((nb, tm, fb), lambda i: (0, i, 0)), _whole(w_out.shape), _rows(tm, d),
                  _whole((1, d)), _whole((1, d))],
        out_specs=(_rows(tm, d), _rows(tm, d), _rows(tm, d)),
        compiler_params=_params())(a, w_out, xin, g, b)


def _ffn_out_bwd(drs, w_out, h, a, name):
    nb, t, fb = a.shape
    d = w_out.shape[2]
    tm = _pick(t, 1024, 8)
    n_i = t // tm

    def body(dr_ref, w_ref, h_ref, a_ref, dh_ref, dw_ref, acc_ref):
        i = pl.program_id(1)
        dr = dr_ref[...]
        da = _dot(dr, w_ref[0], tb=True)
        g, u = h_ref[0, 0], h_ref[0, 1]
        sg = _sigmoid(g)
        dh_ref[0, 0] = (da * u * (sg * (1.0 + g * (1.0 - sg)))).astype(BF16)
        dh_ref[0, 1] = (da * (g * sg)).astype(BF16)

        @pl.when(i == 0)
        def _():
            acc_ref[...] = jnp.zeros_like(acc_ref)

        acc_ref[...] += _dot(a_ref[0], dr, ta=True)

        @pl.when(i == n_i - 1)
        def _():
            dw_ref[0] = acc_ref[...].astype(BF16)

    return pl.pallas_call(
        body, name=name,
        out_shape=(jax.ShapeDtypeStruct((nb, 2, t, fb), BF16), jax.ShapeDtypeStruct((nb, fb, d), BF16)),
        grid=(nb, n_i),
        in_specs=[pl.BlockSpec((tm, d), lambda j, i: (i, 0)),
                  pl.BlockSpec((1, fb, d), lambda j, i: (j, 0, 0)),
                  pl.BlockSpec((1, 2, tm, fb), lambda j, i: (j, 0, i, 0)),
                  pl.BlockSpec((1, tm, fb), lambda j, i: (j, i, 0))],
        out_specs=(pl.BlockSpec((1, 2, tm, fb), lambda j, i: (j, 0, i, 0)),
                   pl.BlockSpec((1, fb, d), lambda j, i: (j, 0, 0))),
        scratch_shapes=[pltpu.VMEM((fb, d), F32)],
        compiler_params=_params(2))(drs, w_out, h, a)


def _ffn_in_dw(x16, dh, name):
    t, d = x16.shape
    nb, _, _, fb = dh.shape

    def body(x_ref, dh_ref, o_ref):
        o_ref[0, 0] = _dot(x_ref[...], dh_ref[0, 0], ta=True).astype(BF16)

    return pl.pallas_call(
        body, name=name, out_shape=jax.ShapeDtypeStruct((2, nb, d, fb), BF16), grid=(nb, 2),
        in_specs=[pl.BlockSpec((t, d), lambda j, s: (0, 0)),
                  pl.BlockSpec((1, 1, t, fb), lambda j, s: (j, s, 0, 0))],
        out_specs=pl.BlockSpec((1, 1, d, fb), lambda j, s: (s, j, 0, 0)),
        compiler_params=_params(2))(x16, dh)


def _ffn_in_dx(dh, w_in, add, add_scale, name):
    nb, _, t, fb = dh.shape
    d = w_in.shape[2]
    tm = _pick(t, 512, 8)
    has_add = add is not None

    def body(*refs):
        dh_ref, w_ref = refs[:2]
        o_ref = refs[-1]
        acc = None
        for jb in range(nb):
            for s in range(2):
                part = _dot(dh_ref[jb, s], w_ref[s, jb], tb=True)
                acc = part if acc is None else acc + part
        if has_add:
            acc = acc + add_scale * refs[2][...]
        o_ref[...] = acc

    return pl.pallas_call(
        body, name=name, out_shape=jax.ShapeDtypeStruct((t, d), F32), grid=(t // tm,),
        in_specs=[pl.BlockSpec((nb, 2, tm, fb), lambda i: (0, 0, i, 0)), _whole(w_in.shape)]
        + ([_rows(tm, d)] if has_add else []),
        out_specs=_rows(tm, d),
        compiler_params=_params())(*([dh, w_in] + ([add] if has_add else [])))


def _resid_ln_fwd(xin, f, g, b, alpha, scale, name):
    t, d = xin.shape
    tr = _pick(t, 256, 8)

    def body(x_ref, f_ref, g_ref, b_ref, r_ref, y_ref, y16_ref):
        r = alpha * x_ref[...] + scale * f_ref[...]
        xhat, _ = _ln_stats(r)
        y = xhat * g_ref[...] + b_ref[...]
        r_ref[...] = r
        y_ref[...] = y
        y16_ref[...] = y.astype(BF16)

    return pl.pallas_call(
        body, name=name,
        out_shape=(jax.ShapeDtypeStruct((t, d), F32), jax.ShapeDtypeStruct((t, d), F32),
                   jax.ShapeDtypeStruct((t, d), BF16)),
        grid=(t // tr,),
        in_specs=[_rows(tr, d), _rows(tr, d), _whole((1, d)), _whole((1, d))],
        out_specs=(_rows(tr, d), _rows(tr, d), _rows(tr, d)),
        compiler_params=_params())(xin, f, g, b)


def _ln_bwd_call(r, dy_a, dy_b, b_scale, g, out_scale, name):
    t, d = r.shape
    tr = _pick(t, 256, 8)
    has_b = dy_b is not None

    def body(*refs):
        if has_b:
            r_ref, da_ref, db_ref, g_ref, dr_ref, drs_ref, dg_ref, dbeta_ref = refs
            dy = da_ref[...] + b_scale * db_ref[...]
        else:
            r_ref, da_ref, g_ref, dr_ref, drs_ref, dg_ref, dbeta_ref = refs
            dy = da_ref[...]
        xhat, rstd = _ln_stats(r_ref[...])
        dr = _ln_bwd(dy, xhat, rstd, g_ref[...])
        dr_ref[...] = dr
        drs_ref[...] = (out_scale * dr).astype(BF16)

        @pl.when(pl.program_id(0) == 0)
        def _():
            dg_ref[...] = jnp.zeros_like(dg_ref)
            dbeta_ref[...] = jnp.zeros_like(dbeta_ref)

        dg_ref[...] += _fold8(dy * xhat)
        dbeta_ref[...] += _fold8(dy)

    ins = [r, dy_a] + ([dy_b] if has_b else []) + [g]
    in_specs = [_rows(tr, d)] * (3 if has_b else 2) + [_whole((1, d))]
    return pl.pallas_call(
        body, name=name,
        out_shape=(jax.ShapeDtypeStruct((t, d), F32), jax.ShapeDtypeStruct((t, d), BF16),
                   jax.ShapeDtypeStruct((8, d), F32), jax.ShapeDtypeStruct((8, d), F32)),
        grid=(t // tr,), in_specs=in_specs,
        out_specs=(_rows(tr, d), _rows(tr, d), _whole((8, d)), _whole((8, d))),
        compiler_params=_params())(*ins)


def _take_row(v, row_id, s):
    return jnp.sum(jnp.where(row_id == s, v, 0.0), axis=0, keepdims=True)


def _seg_carries(f_re, f_im, p_re, p_im, reverse):
    row_id = lax.broadcasted_iota(jnp.int32, f_re.shape, 0)
    p_re, p_im = _take_row(p_re, row_id, 0), _take_row(p_im, row_id, 0)
    out_re, out_im = jnp.zeros_like(f_re), jnp.zeros_like(f_im)
    c_re, c_im = jnp.zeros_like(p_re), jnp.zeros_like(p_im)
    order = range(N_SEG - 1, -1, -1) if reverse else range(N_SEG)
    for s in order:
        out_re = jnp.where(row_id == s, c_re, out_re)
        out_im = jnp.where(row_id == s, c_im, out_im)
        fr, fi = _take_row(f_re, row_id, s), _take_row(f_im, row_id, s)
        c_re, c_im = fr + p_re * c_re - p_im * c_im, fi + p_re * c_im + p_im * c_re
    return out_re, out_im


def _scan_fwd(bu_re, bu_im, a_re, a_im, n_seq, name):
    t, n = bu_re.shape
    seq = t // n_seq
    steps = seq // N_SEG
    cb = _pick(n, 256)

    def body(bre, bim, are, aim, hre, him):
        ar = jnp.broadcast_to(are[...], (N_SEG, cb))
        ai = jnp.broadcast_to(aim[...], (N_SEG, cb))
        zero = jnp.zeros((N_SEG, cb), F32)

        def local(k, carry):
            lr, li, pr, pi = carry
            rows = pl.ds(pl.multiple_of(k * N_SEG, N_SEG), N_SEG)
            nr = ar * lr - ai * li + bre[rows, :]
            ni = ar * li + ai * lr + bim[rows, :]
            hre[rows, :] = nr
            him[rows, :] = ni
            return nr, ni, ar * pr - ai * pi, ar * pi + ai * pr

        f_re, f_im, p_re, p_im = lax.fori_loop(0, steps, local, (zero, zero, zero + 1.0, zero))
        c_re, c_im = _seg_carries(f_re, f_im, p_re, p_im, reverse=False)

        def fix(k, carry):
            pr, pi = carry
            rows = pl.ds(pl.multiple_of(k * N_SEG, N_SEG), N_SEG)
            hre[rows, :] = hre[rows, :] + (pr * c_re - pi * c_im)
            him[rows, :] = him[rows, :] + (pr * c_im + pi * c_re)
            return ar * pr - ai * pi, ar * pi + ai * pr

        lax.fori_loop(0, steps, fix, (ar, ai))

    blk = pl.BlockSpec((seq, cb), lambda s, j: (s, j))
    vec = pl.BlockSpec((1, cb), lambda s, j: (0, j))
    return pl.pallas_call(
        body, name=name,
        out_shape=(jax.ShapeDtypeStruct((t, n), F32), jax.ShapeDtypeStruct((t, n), F32)),
        grid=(n_seq, n // cb), in_specs=[blk, blk, vec, vec], out_specs=(blk, blk),
        compiler_params=_params(2))(bu_re, bu_im, a_re, a_im)


def _scan_bwd(g_re, g_im, h_re, h_im, a_re, a_im, n_seq, name):
    t, n = g_re.shape
    seq = t // n_seq
    steps = seq // N_SEG
    cb = _pick(n, 256)

    def body(gre, gim, hre, him, are, aim, lre, lim, dare, daim):
        ar = jnp.broadcast_to(are[...], (N_SEG, cb))
        ai = -jnp.broadcast_to(aim[...], (N_SEG, cb))
        zero = jnp.zeros((N_SEG, cb), F32)

        def local(i, carry):
            lr, li, pr, pi = carry
            k = steps - 1 - i
            rows = pl.ds(pl.multiple_of(k * N_SEG, N_SEG), N_SEG)
            nr = ar * lr - ai * li + gre[rows, :]
            ni = ar * li + ai * lr + gim[rows, :]
            lre[rows, :] = nr
            lim[rows, :] = ni
            return nr, ni, ar * pr - ai * pi, ar * pi + ai * pr

        f_re, f_im, p_re, p_im = lax.fori_loop(0, steps, local, (zero, zero, zero + 1.0, zero))
        c_re, c_im = _seg_carries(f_re, f_im, p_re, p_im, reverse=True)

        def accumulate(lam_r, lam_i, hp_r, hp_i, acc_r, acc_i):
            return acc_r + (lam_r * hp_r + lam_i * hp_i), acc_i + (lam_i * hp_r - lam_r * hp_i)

        def fix(i, carry):
            pr, pi, acc_r, acc_i = carry
            k = steps - 1 - i
            rows = pl.ds(pl.multiple_of(k * N_SEG, N_SEG), N_SEG)
            prev = pl.ds(pl.multiple_of((k - 1) * N_SEG, N_SEG), N_SEG)
            lam_r = lre[rows, :] + (pr * c_re - pi * c_im)
            lam_i = lim[rows, :] + (pr * c_im + pi * c_re)
            lre[rows, :] = lam_r
            lim[rows, :] = lam_i
            acc_r, acc_i = accumulate(lam_r, lam_i, hre[prev, :], him[prev, :], acc_r, acc_i)
            return ar * pr - ai * pi, ar * pi + ai * pr, acc_r, acc_i

        pr, pi, acc_r, acc_i = lax.fori_loop(0, steps - 1, fix, (ar, ai, zero, zero))
        first = pl.ds(0, N_SEG)
        last = pl.ds((steps - 1) * N_SEG, N_SEG)
        lam_r = lre[first, :] + (pr * c_re - pi * c_im)
        lam_i = lim[first, :] + (pr * c_im + pi * c_re)
        lre[first, :] = lam_r
        lim[first, :] = lam_i
        row_id = lax.broadcasted_iota(jnp.int32, (N_SEG, cb), 0)
        hp_r = jnp.where(row_id == 0, 0.0, pltpu.roll(hre[last, :], 1, 0))
        hp_i = jnp.where(row_id == 0, 0.0, pltpu.roll(him[last, :], 1, 0))
        acc_r, acc_i = accumulate(lam_r, lam_i, hp_r, hp_i, acc_r, acc_i)
        dare[...] = acc_r
        daim[...] = acc_i

    blk = pl.BlockSpec((seq, cb), lambda s, j: (s, j))
    vec = pl.BlockSpec((1, cb), lambda s, j: (0, j))
    acc = pl.BlockSpec((N_SEG, cb), lambda s, j: (s, j))
    return pl.pallas_call(
        body, name=name,
        out_shape=(jax.ShapeDtypeStruct((t, n), F32), jax.ShapeDtypeStruct((t, n), F32),
                   jax.ShapeDtypeStruct((n_seq * N_SEG, n), F32),
                   jax.ShapeDtypeStruct((n_seq * N_SEG, n), F32)),
        grid=(n_seq, n // cb), in_specs=[blk, blk, blk, blk, vec, vec],
        out_specs=(blk, blk, acc, acc),
        compiler_params=_params(2))(g_re, g_im, h_re, h_im, a_re, a_im)


def _s5_post_fwd(yssm, u, d_skip, glu_w, glu_b, name):
    t, w = yssm.shape
    tr = _pick(t, 512, 8)

    def body(y_ref, u_ref, d_ref, w_ref, b_ref, o_ref):
        yg = _gelu(y_ref[...] + d_ref[...] * u_ref[...])
        pre = _dot(yg, w_ref[...]) + b_ref[...]
        o_ref[...] = yg * _sigmoid(pre)

    return pl.pallas_call(
        body, name=name, out_shape=jax.ShapeDtypeStruct((t, w), F32), grid=(t // tr,),
        in_specs=[_rows(tr, w), _rows(tr, w), _whole((1, w)), _whole((w, w)), _whole((1, w))],
        out_specs=_rows(tr, w), compiler_params=_params())(yssm, u, d_skip, glu_w, glu_b)


def _s5_post_bwd(yssm, u, dout, d_skip, glu_w, glu_b, name):
    t, w = yssm.shape
    tr = _pick(t, 512, 8)

    def body(y_ref, u_ref, do_ref, d_ref, w_ref, b_ref, dy_ref, du_ref, dw_ref, dd_ref, db_ref):
        uu = u_ref[...]
        y = y_ref[...] + d_ref[...] * uu
        yg = _gelu(y)
        sg = _sigmoid(_dot(yg, w_ref[...]) + b_ref[...])
        do = do_ref[...]
        dpre = do * yg * sg * (1.0 - sg)
        dyg = do * sg + _dot(dpre, w_ref[...], tb=True)
        dy = dyg * _gelu_grad(y)
        dy_ref[...] = dy.astype(BF16)
        du_ref[...] = dy * d_ref[...]

        @pl.when(pl.program_id(0) == 0)
        def _():
            dw_ref[...] = jnp.zeros_like(dw_ref)
            dd_ref[...] = jnp.zeros_like(dd_ref)
            db_ref[...] = jnp.zeros_like(db_ref)

        dw_ref[...] += _dot(yg, dpre, ta=True)
        dd_ref[...] += _fold8(dy * uu)
        db_ref[...] += _fold8(dpre)

    return pl.pallas_call(
        body, name=name,
        out_shape=(jax.ShapeDtypeStruct((t, w), BF16), jax.ShapeDtypeStruct((t, w), F32),
                   jax.ShapeDtypeStruct((w, w), F32), jax.ShapeDtypeStruct((8, w), F32),
                   jax.ShapeDtypeStruct((8, w), F32)),
        grid=(t // tr,),
        in_specs=[_rows(tr, w), _rows(tr, w), _rows(tr, w), _whole((1, w)), _whole((w, w)),
                  _whole((1, w))],
        out_specs=(_rows(tr, w), _rows(tr, w), _whole((w, w)), _whole((8, w)), _whole((8, w))),
        compiler_params=_params())(yssm, u, dout, d_skip, glu_w, glu_b)


def _head_select(parts, head_dim):
    col_head = lax.broadcasted_iota(jnp.int32, parts[0].shape, 1) // head_dim
    out = jnp.zeros_like(parts[0])
    for h, part in enumerate(parts):
        out = jnp.where(col_head == h, part, out)
    return out


def _gmlp_fwd(proj, ln_g, ln_b, ws, bs_cols, name):
    t = proj.shape[0]
    n_heads = ws.shape[0]
    w = bs_cols.shape[1]
    head_dim = w // n_heads
    cpb = _pick(t // CHUNK, 4, 1)
    tr = cpb * CHUNK

    def body(zu_ref, zv_ref, g_ref, b_ref, ws_ref, bs_ref, o_ref):
        for c in range(cpb):
            rows = pl.ds(c * CHUNK, CHUNK)
            xhat, _ = _ln_stats(_gelu(zv_ref[rows, :]))
            vn = (xhat * g_ref[...] + b_ref[...]).astype(BF16)
            s = _head_select([_dot(ws_ref[h], vn) for h in range(n_heads)], head_dim) + bs_ref[...]
            o_ref[rows, :] = _gelu(zu_ref[rows, :]) * s

    return pl.pallas_call(
        body, name=name, out_shape=jax.ShapeDtypeStruct((t, w), F32), grid=(t // tr,),
        in_specs=[_rows(tr, w, 1), _rows(tr, w, 2), _whole((1, w)), _whole((1, w)),
                  _whole(ws.shape), _whole(bs_cols.shape)],
        out_specs=_rows(tr, w), compiler_params=_params())(proj, proj, ln_g, ln_b, ws, bs_cols)


def _gmlp_bwd(proj, dout, ln_g, ln_b, ws, ws_t, bs_cols, name):
    t = proj.shape[0]
    n_heads = ws.shape[0]
    w = bs_cols.shape[1]
    head_dim = w // n_heads
    cpb = _pick(t // CHUNK, 4, 1)
    tr = cpb * CHUNK

    def body(zu_ref, zv_ref, do_ref, g_ref, b_ref, ws_ref, wst_ref, bs_ref,
             dzu_ref, dzv_ref, dws_ref, dbs_ref, dg_ref, dbeta_ref):
        @pl.when(pl.program_id(0) == 0)
        def _():
            dws_ref[...] = jnp.zeros_like(dws_ref)
            dbs_ref[...] = jnp.zeros_like(dbs_ref)
            dg_ref[...] = jnp.zeros_like(dg_ref)
            dbeta_ref[...] = jnp.zeros_like(dbeta_ref)

        col_head = lax.broadcasted_iota(jnp.int32, (CHUNK, w), 1) // head_dim
        for c in range(cpb):
            rows = pl.ds(c * CHUNK, CHUNK)
            zu, zv, do = zu_ref[rows, :], zv_ref[rows, :], do_ref[rows, :]
            xhat, rstd = _ln_stats(_gelu(zv))
            vn = (xhat * g_ref[...] + b_ref[...]).astype(BF16)
            s = _head_select([_dot(ws_ref[h], vn) for h in range(n_heads)], head_dim) + bs_ref[...]
            dzu_ref[rows, :] = (do * s * _gelu_grad(zu)).astype(BF16)
            ds = do * _gelu(zu)
            ds16 = ds.astype(BF16)
            dbs_ref[...] += ds
            for h in range(n_heads):
                dws_ref[h] += _dot(jnp.where(col_head == h, ds16, jnp.zeros_like(ds16)), vn, tb=True)
            dvn = _head_select([_dot(wst_ref[h], ds16) for h in range(n_heads)], head_dim)
            dg_ref[...] += _fold8(dvn * xhat)
            dbeta_ref[...] += _fold8(dvn)
            dgv = _ln_bwd(dvn, xhat, rstd, g_ref[...])
            dzv_ref[rows, :] = (dgv * _gelu_grad(zv)).astype(BF16)

    return pl.pallas_call(
        body, name=name,
        out_shape=(jax.ShapeDtypeStruct((t, w), BF16), jax.ShapeDtypeStruct((t, w), BF16),
                   jax.ShapeDtypeStruct(ws.shape, F32), jax.ShapeDtypeStruct((CHUNK, w), F32),
                   jax.ShapeDtypeStruct((8, w), F32), jax.ShapeDtypeStruct((8, w), F32)),
        grid=(t // tr,),
        in_specs=[_rows(tr, w, 1), _rows(tr, w, 2), _rows(tr, w), _whole((1, w)), _whole((1, w)),
                  _whole(ws.shape), _whole(ws.shape), _whole(bs_cols.shape)],
        out_specs=(_rows(tr, w), _rows(tr, w), _whole(ws.shape), _whole((CHUNK, w)),
                   _whole((8, w)), _whole((8, w))),
        compiler_params=_params())(proj, proj, dout, ln_g, ln_b, ws, ws_t, bs_cols)


def _merge_fwd(s5out, gm, proj, up_a, up_b, name):
    t, w = s5out.shape
    d = up_a.shape[1]
    assert d == 2 * w
    tr = _pick(t, 256, 8)

    def body(s_ref, gm_ref, ga0, ga1, gb0, gb1, ua_ref, ub_ref, m_ref, ya_ref, yb_ref):
        ya = _dot(s_ref[...], ua_ref[...])
        yb = _dot(gm_ref[...], ub_ref[...])
        ya_ref[...] = ya
        yb_ref[...] = yb
        for half, (ga, gb) in enumerate(((ga0, gb0), (ga1, gb1))):
            cols = slice(half * w, (half + 1) * w)
            m_ref[:, cols] = (_sigmoid(ga[...]) * ya[:, cols] + _sigmoid(gb[...]) * yb[:, cols]).astype(BF16)

    return pl.pallas_call(
        body, name=name,
        out_shape=(jax.ShapeDtypeStruct((t, d), BF16), jax.ShapeDtypeStruct((t, d), F32),
                   jax.ShapeDtypeStruct((t, d), F32)),
        grid=(t // tr,),
        in_specs=[_rows(tr, w), _rows(tr, w), _rows(tr, w, 3), _rows(tr, w, 4), _rows(tr, w, 5),
                  _rows(tr, w, 6), _whole(up_a.shape), _whole(up_b.shape)],
        out_specs=(_rows(tr, d), _rows(tr, d), _rows(tr, d)),
        compiler_params=_params())(s5out, gm, proj, proj, proj, proj, up_a, up_b)


def _merge_bwd(dm, ya, yb, s5out, gm, proj, up_a, up_b, name):
    t, d = dm.shape
    w = d // 2
    tr = _pick(t, 256, 8)

    def body(dm_ref, ya_ref, yb_ref, s_ref, gm_ref, ga0, ga1, gb0, gb1, ua_ref, ub_ref,
             dga_ref, dgb_ref, ds_ref, dgm_ref, dua_ref, dub_ref):
        dm_v, ya, yb = dm_ref[...], ya_ref[...], yb_ref[...]
        dya, dyb = [], []
        for half, (ga, gb) in enumerate(((ga0, gb0), (ga1, gb1))):
            cols = slice(half * w, (half + 1) * w)
            sa, sb = _sigmoid(ga[...]), _sigmoid(gb[...])
            dmh = dm_v[:, cols]
            dga_ref[:, cols] = (dmh * ya[:, cols] * sa * (1.0 - sa)).astype(BF16)
            dgb_ref[:, cols] = (dmh * yb[:, cols] * sb * (1.0 - sb)).astype(BF16)
            dya.append((dmh * sa).astype(BF16))
            dyb.append((dmh * sb).astype(BF16))
        dya = jnp.concatenate(dya, axis=1)
        dyb = jnp.concatenate(dyb, axis=1)
        ds_ref[...] = _dot(dya, ua_ref[...], tb=True)
        dgm_ref[...] = _dot(dyb, ub_ref[...], tb=True)

        @pl.when(pl.program_id(0) == 0)
        def _():
            dua_ref[...] = jnp.zeros_like(dua_ref)
            dub_ref[...] = jnp.zeros_like(dub_ref)

        dua_ref[...] += _dot(s_ref[...], dya, ta=True)
        dub_ref[...] += _dot(gm_ref[...], dyb, ta=True)

    return pl.pallas_call(
        body, name=name,
        out_shape=(jax.ShapeDtypeStruct((t, d), BF16), jax.ShapeDtypeStruct((t, d), BF16),
                   jax.ShapeDtypeStruct((t, w), F32), jax.ShapeDtypeStruct((t, w), F32),
                   jax.ShapeDtypeStruct(up_a.shape, F32), jax.ShapeDtypeStruct(up_b.shape, F32)),
        grid=(t // tr,),
        in_specs=[_rows(tr, d), _rows(tr, d), _rows(tr, d), _rows(tr, w), _rows(tr, w),
                  _rows(tr, w, 3), _rows(tr, w, 4), _rows(tr, w, 5), _rows(tr, w, 6),
                  _whole(up_a.shape), _whole(up_b.shape)],
        out_specs=(_rows(tr, d), _rows(tr, d), _rows(tr, w), _rows(tr, w), _whole(up_a.shape),
                   _whole(up_b.shape)),
        compiler_params=_params())(dm, ya, yb, s5out, gm, proj, proj, proj, proj, up_a, up_b)


def _head(x3, p, target, w_gate, w_proj, name):
    t, d = x3.shape
    pd = p.shape[1]
    tr = _pick(t, 256, 8)
    nb = t // tr

    def body(x_ref, p_ref, t_ref, wg_ref, wp_ref, loss_ref, dx_ref, dwg_ref, dwp_ref):
        xv = x_ref[...]
        sg = _sigmoid(_dot(xv, wg_ref[...]))
        pp = _dot(p_ref[...], wp_ref[...])
        err = xv + sg * pp - t_ref[...]
        loss_ref[...] = jnp.full((8, 128), 0.5 * jnp.sum(jnp.mean(err * err, axis=-1)), F32)
        dout = err * (1.0 / d)
        dgpre = dout * pp * sg * (1.0 - sg)
        dpp = dout * sg
        dx_ref[...] = dout + _dot(dgpre, wg_ref[...], tb=True)

        @pl.when(pl.program_id(0) == 0)
        def _():
            dwg_ref[...] = jnp.zeros_like(dwg_ref)
            dwp_ref[...] = jnp.zeros_like(dwp_ref)

        dwg_ref[...] += _dot(xv, dgpre, ta=True)
        dwp_ref[...] += _dot(p_ref[...], dpp, ta=True)

    return pl.pallas_call(
        body, name=name,
        out_shape=(jax.ShapeDtypeStruct((nb * 8, 128), F32), jax.ShapeDtypeStruct((t, d), F32),
                   jax.ShapeDtypeStruct((d, d), F32), jax.ShapeDtypeStruct((pd, d), F32)),
        grid=(nb,),
        in_specs=[_rows(tr, d), _rows(tr, pd), _rows(tr, d), _whole((d, d)), _whole((pd, d))],
        out_specs=(pl.BlockSpec((8, 128), lambda i: (i, 0)), _rows(tr, d), _whole((d, d)),
                   _whole((pd, d))),
        compiler_params=_params())(x3, p, target, w_gate, w_proj)


_HBM = pl.BlockSpec(memory_space=pltpu.HBM)


def _all_gather(shards, name):
    n = len(shards)

    def body(*refs):
        x_refs, out_refs = refs[:n], refs[n:2 * n]
        send_sems, recv_sems, local_sems = refs[2 * n:]
        x, y, c = lax.axis_index("x"), lax.axis_index("y"), lax.axis_index("c")
        me, sibling = (x, y, c), (x, y, 1 - c)
        chips = [(1 - x, y), (x, 1 - y), (1 - x, 1 - y)]

        def copy(a, k, block, to, own=False):
            slot = out_refs[a].at[4 * block[0] + 2 * block[1] + block[2]]
            return pltpu.make_async_remote_copy(
                src_ref=x_refs[a] if own else slot, dst_ref=slot,
                send_sem=send_sems.at[7 * a + k], recv_sem=recv_sems.at[7 * a + k],
                device_id=to, device_id_type=pl.DeviceIdType.MESH)

        mine = [pltpu.make_async_copy(x_refs[a], out_refs[a].at[4 * x + 2 * y + c], local_sems.at[a])
                for a in range(n)]
        sends = []
        for a in range(n):
            mine[a].start()
            first = [copy(a, 0, me, sibling, own=True)]
            first += [copy(a, 1 + j, me, (*chip, c), own=True) for j, chip in enumerate(chips)]
            for cp in first:
                cp.start()
            sends += first
        for a in range(n):
            for j, chip in enumerate(chips):
                copy(a, 1 + j, (*chip, c), me).wait_recv()
                passed = copy(a, 4 + j, (*chip, c), sibling)
                passed.start()
                sends.append(passed)
        for a in range(n):
            copy(a, 0, sibling, me).wait_recv()
            for j, chip in enumerate(chips):
                copy(a, 4 + j, (*chip, 1 - c), me).wait_recv()
        for cp in sends:
            cp.wait_send()
        for cp in mine:
            cp.wait()

    return pl.pallas_call(
        body, name=name,
        out_shape=tuple(jax.ShapeDtypeStruct((N_DEV,) + s.shape, s.dtype) for s in shards),
        in_specs=[_HBM] * n, out_specs=tuple([_HBM] * n),
        scratch_shapes=[pltpu.SemaphoreType.DMA((7 * n,)), pltpu.SemaphoreType.DMA((7 * n,)),
                        pltpu.SemaphoreType.DMA((n,))],
    )(*shards)


def _exchange(sends, name):
    n = len(sends)

    def body(*refs):
        s_refs, r_refs = refs[:n], refs[n:2 * n]
        send_sems, recv_sems, local_sems = refs[2 * n:]
        x, y, c = lax.axis_index("x"), lax.axis_index("y"), lax.axis_index("c")
        me = 4 * x + 2 * y + c
        copies = []
        for a in range(n):
            same = sends[a].shape[0] == 1
            mine = pltpu.make_async_copy(s_refs[a].at[0 if same else me], r_refs[a].at[me], local_sems.at[a])
            mine.start()
            copies.append(mine)
            for k in range(1, N_DEV):
                px = 1 - x if k & 4 else x
                py = 1 - y if k & 2 else y
                pc = 1 - c if k & 1 else c
                cp = pltpu.make_async_remote_copy(
                    src_ref=s_refs[a].at[0 if same else 4 * px + 2 * py + pc], dst_ref=r_refs[a].at[me],
                    send_sem=send_sems.at[7 * a + k - 1], recv_sem=recv_sems.at[7 * a + k - 1],
                    device_id=(px, py, pc), device_id_type=pl.DeviceIdType.MESH)
                cp.start()
                copies.append(cp)
        for cp in copies:
            cp.wait()

    return pl.pallas_call(
        body, name=name,
        out_shape=tuple(jax.ShapeDtypeStruct((N_DEV,) + s.shape[1:], s.dtype) for s in sends),
        in_specs=[_HBM] * n, out_specs=tuple([_HBM] * n),
        scratch_shapes=[pltpu.SemaphoreType.DMA((7 * n,)), pltpu.SemaphoreType.DMA((7 * n,)),
                        pltpu.SemaphoreType.DMA((n,))],
    )(*sends)


def _adamw(recv, w, m, v, name):
    n, rows, width = recv.shape
    tr = _pick(rows, max(8, ADAM_BLOCK_BYTES // (n * width * recv.dtype.itemsize)), 8)
    c_m = 1.0 - ADAM_B1 ** ADAM_STEP
    c_v = 1.0 - ADAM_B2 ** ADAM_STEP

    def body(r_ref, w_ref, m_ref, v_ref, g_ref, d_ref, nm_ref, nv_ref):
        g = r_ref[0].astype(F32)
        for i in range(1, n):
            g = g + r_ref[i].astype(F32)
        m2 = ADAM_B1 * m_ref[...] + (1.0 - ADAM_B1) * g
        v2 = ADAM_B2 * v_ref[...] + (1.0 - ADAM_B2) * (g * g)
        m_hat = m2 / c_m
        v_hat = v2 / c_v
        g_ref[...] = g
        d_ref[...] = -ADAM_LR * (m_hat / (jnp.sqrt(v_hat) + ADAM_EPS) + ADAM_WD * w_ref[...])
        nm_ref[...] = m2
        nv_ref[...] = v2

    blk = _rows(tr, width)
    shp = jax.ShapeDtypeStruct((rows, width), F32)
    return pl.pallas_call(
        body, name=name, out_shape=(shp, shp, shp, shp), grid=(rows // tr,),
        in_specs=[pl.BlockSpec((n, tr, width), lambda i: (0, i, 0)), blk, blk, blk],
        out_specs=(blk, blk, blk, blk), compiler_params=_params())(recv, w, m, v)


def _join_cols(gathered):
    n, r, c = gathered.shape
    return gathered.transpose(1, 0, 2).reshape(r, n * c)


def _split_cols(full):
    r, c = full.shape
    return full.reshape(r, N_DEV, c // N_DEV).transpose(1, 0, 2)


def _small_rows(size):
    return -(-size // (8 * PACK_W)) * 8


def _pack_small(parts):
    rows = []
    for n in SMALL:
        flat = parts[n].reshape(-1)
        r = _small_rows(flat.shape[0])
        rows.append(jnp.pad(flat, (0, r * PACK_W - flat.shape[0])).reshape(r, PACK_W))
    return jnp.concatenate(rows, axis=0)


def _unpack_small(packed, shapes):
    out, r0 = {}, 0
    for n in SMALL:
        size = math.prod(shapes[n])
        rows = _small_rows(size)
        out[n] = packed[r0:r0 + rows].reshape(-1)[:size].reshape(shapes[n])
        r0 += rows
    return out


def _discretise(lam_re, lam_im, log_dt, b_re, b_im):
    dt = jnp.exp(log_dt)[:, None]
    mag = jnp.exp(lam_re * dt)
    ab_re = mag * jnp.cos(lam_im * dt)
    ab_im = mag * jnp.sin(lam_im * dt)
    nr = ab_re - 1.0
    ni = ab_im
    den = lam_re * lam_re + lam_im * lam_im
    coef_re = ((nr * lam_re + ni * lam_im) / den)[..., None]
    coef_im = ((ni * lam_re - nr * lam_im) / den)[..., None]
    bb_re = coef_re * b_re - coef_im * b_im
    bb_im = coef_re * b_im + coef_im * b_re
    return ab_re, ab_im, bb_re, bb_im


def _same_group(g):
    return jnp.eye(g, dtype=bool)[:, None, :, None]


def _block_diag_in(bb):
    g, p, i = bb.shape
    placed = jnp.where(_same_group(g), bb.transpose(0, 2, 1)[:, :, None, :], 0.0)
    return placed.reshape(g * i, g * p)


def _block_diag_in_take(dense, g, p, i):
    blocks = jnp.where(_same_group(g), dense.reshape(g, i, g, p), 0.0).sum(axis=2)
    return blocks.transpose(0, 2, 1)


def _block_diag_out(cc):
    g, i, p = cc.shape
    placed = jnp.where(_same_group(g), cc.transpose(0, 2, 1)[:, :, None, :], 0.0)
    return placed.reshape(g * p, g * i)


def _block_diag_out_take(dense, g, i, p):
    blocks = jnp.where(_same_group(g), dense.reshape(g, p, g, i), 0.0).sum(axis=2)
    return blocks.transpose(0, 2, 1)


def _perm_rows(z, n_seq):
    t, w = z.shape
    steps = t // n_seq // N_SEG
    return z.reshape(n_seq, N_SEG, steps, w).transpose(0, 2, 1, 3).reshape(t, w)


def _unperm_rows(z, n_seq):
    t, w = z.shape
    steps = t // n_seq // N_SEG
    return z.reshape(n_seq, steps, N_SEG, w).transpose(0, 2, 1, 3).reshape(t, w)


def kernel(*args):
    assert len(args) == len(INPUT_NAMES)
    inp = dict(zip(INPUT_NAMES, args))
    depth = inp["ffn1_w_in"].shape[0]
    assert depth == 1
    alpha = (2.0 * depth) ** 0.25

    n_seq, seq, d_model = inp["x"].shape
    t = n_seq * seq
    x = inp["x"].reshape(t, d_model)
    x16 = x.astype(BF16)
    p16 = inp["p"][0].reshape(t, -1).astype(BF16)
    target = inp["loss_target"].reshape(t, d_model)
    wts = {n: inp[n][0] for n in WEIGHTS}
    mom = {n: inp["m_" + n][0] for n in WEIGHTS}
    var = {n: inp["v_" + n][0] for n in WEIGHTS}

    gathered = dict(zip(SHARDED, _all_gather([wts[n].astype(BF16) for n in SHARDED], "weights_all_gather")))
    full = {}
    for n in SHARDED:
        g = gathered[n]
        if n in ("ffn1_w_in", "ffn2_w_in"):
            full[n] = g.reshape((2, N_DEV // 2) + g.shape[1:])
        elif n in ("ffn1_w_out", "ffn2_w_out"):
            full[n] = g.reshape(N_DEV // 2, 2 * g.shape[1], g.shape[2])
        elif n in COL_SHARDED:
            full[n] = _join_cols(g)
        else:
            full[n] = g.reshape(N_DEV * g.shape[1], g.shape[2])

    def row(v):
        return v.reshape(1, -1)

    n_groups, n_state = wts["ssm_lambda_re"].shape
    n_ch = wts["ssm_b_re"].shape[2]
    disc_in = (wts["ssm_lambda_re"], wts["ssm_lambda_im"], wts["ssm_log_dt"], wts["ssm_b_re"],
               wts["ssm_b_im"])
    (ab_re, ab_im, bb_re, bb_im), disc_vjp = jax.vjp(_discretise, *disc_in)
    a_re, a_im = row(ab_re), row(ab_im)
    b_dense_re = _block_diag_in(bb_re).astype(BF16)
    b_dense_im = _block_diag_in(bb_im).astype(BF16)
    c_dense_re = _block_diag_out(wts["ssm_c_re"]).astype(BF16)
    c_dense_im = _block_diag_out(-wts["ssm_c_im"]).astype(BF16)

    ws = wts["gmlp_w_s"]
    n_heads = ws.shape[0]
    d_gmlp = wts["gmlp_ln_g"].shape[0]
    causal = jnp.tril(jnp.ones((CHUNK, CHUNK), dtype=bool))
    ws_masked = jnp.where(causal[None], ws, 0.0).astype(BF16)
    ws_masked_t = ws_masked.transpose(0, 2, 1)
    bs_cols = jnp.repeat(wts["gmlp_b_s"].T, d_gmlp // n_heads, axis=1)
    d_ssm = n_groups * n_ch
    assert d_ssm == d_gmlp and d_model == 2 * d_ssm

    h1, a1 = _ffn_in(x16, full["ffn1_w_in"], "ffn1_in")
    r1, x1, x1_16 = _ffn_out_ln(a1, full["ffn1_w_out"], x, row(wts["ln1_g"]), row(wts["ln1_b"]), alpha,
                                "ffn1_out_ln")

    proj = _mm(x1_16, full["mix_w_in"], name="mix_in")
    za_p = _perm_rows(proj[:, :d_ssm], n_seq)
    za_p16 = za_p.astype(BF16)
    bu_re = _mm(za_p16, b_dense_re, name="s5_bu_re")
    bu_im = _mm(za_p16, b_dense_im, name="s5_bu_im")
    h_re, h_im = _scan_fwd(bu_re, bu_im, a_re, a_im, n_seq, "s5_scan")
    yssm = _mm(h_re, c_dense_re, name="s5_y_re")
    yssm = _mm(h_im, c_dense_im, name="s5_y_im", add=yssm)
    s5out_p = _s5_post_fwd(yssm, za_p, row(wts["ssm_d"]), full["ssm_glu_w"], row(wts["ssm_glu_b"]),
                           "s5_post")
    s5out = _unperm_rows(s5out_p, n_seq)
    gm = _gmlp_fwd(proj, row(wts["gmlp_ln_g"]), row(wts["gmlp_ln_b"]), ws_masked, bs_cols, "gmlp")
    m_mix, y_a, y_b = _merge_fwd(s5out, gm, proj, full["up_a"], full["up_b"], "merge")
    mixed = _mm(m_mix, full["mix_w_out"], name="mix_out")
    r2, x2, x2_16 = _resid_ln_fwd(x1, mixed, row(wts["ln2_g"]), row(wts["ln2_b"]), alpha, 1.0, "ln2")

    h2, a2 = _ffn_in(x2_16, full["ffn2_w_in"], "ffn2_in")
    r3, x3, _ = _ffn_out_ln(a2, full["ffn2_w_out"], x2, row(wts["ln3_g"]), row(wts["ln3_b"]), alpha,
                            "ffn2_out_ln")

    small = {}
    send = {}
    loss_parts, dx3, dw_gate, dw_proj = _head(x3, p16, target, full["ple_w_gate"], full["ple_w_proj"], "head")
    loss = lax.psum(jnp.sum(loss_parts[::8, 0]), ("x", "y", "c"))
    send["ple_w_gate"] = dw_gate.astype(BF16).reshape(N_DEV, -1, d_model)
    send["ple_w_proj"] = _split_cols(dw_proj.astype(BF16))

    dr3, dr3_h, dg, db = _ln_bwd_call(r3, dx3, None, 1.0, row(wts["ln3_g"]), 0.5, "ln3_bwd")
    small["ln3_g"], small["ln3_b"] = dg.sum(0), db.sum(0)
    dh2, dw = _ffn_out_bwd(dr3_h, full["ffn2_w_out"], h2, a2, "ffn2_out_bwd")
    send["ffn2_w_out"] = dw.reshape(N_DEV, -1, d_model)
    dw = _ffn_in_dw(x2_16, dh2, "ffn2_in_dw")
    send["ffn2_w_in"] = dw.reshape((N_DEV,) + dw.shape[2:])
    dx2_f = _ffn_in_dx(dh2, full["ffn2_w_in"], None, 1.0, "ffn2_in_dx")

    dr2, dr2_h, dg, db = _ln_bwd_call(r2, dx2_f, dr3, alpha, row(wts["ln2_g"]), 1.0, "ln2_bwd")
    small["ln2_g"], small["ln2_b"] = dg.sum(0), db.sum(0)
    dm = _mm(dr2_h, full["mix_w_out"], tb=True, name="mix_out_dx")
    send["mix_w_out"] = _mm(m_mix, dr2_h, ta=True, name="mix_out_dw", out_dtype=BF16).reshape(
        N_DEV, -1, d_model)
    dga, dgb, ds5out, dgm, dw_a, dw_b = _merge_bwd(
        dm, y_a, y_b, s5out, gm, proj, full["up_a"], full["up_b"], "merge_bwd")
    send["up_a"] = _split_cols(dw_a.astype(BF16))
    send["up_b"] = _split_cols(dw_b.astype(BF16))

    dzu, dzv, dws, dbs_cols, dg, db = _gmlp_bwd(
        proj, dgm, row(wts["gmlp_ln_g"]), row(wts["gmlp_ln_b"]), ws_masked, ws_masked_t, bs_cols,
        "gmlp_bwd")
    small["gmlp_ln_g"], small["gmlp_ln_b"] = dg.sum(0), db.sum(0)
    small["gmlp_w_s"] = jnp.where(causal[None], dws, 0.0)
    small["gmlp_b_s"] = dbs_cols.reshape(CHUNK, n_heads, -1).sum(-1).T

    ds5out_p = _perm_rows(ds5out, n_seq)
    dy_p, dza_skip, dw_glu, dd, dgb_glu = _s5_post_bwd(
        yssm, za_p, ds5out_p, row(wts["ssm_d"]), full["ssm_glu_w"], row(wts["ssm_glu_b"]),
        "s5_post_bwd")
    send["ssm_glu_w"] = dw_glu.astype(BF16).reshape(N_DEV, -1, d_ssm)
    small["ssm_d"], small["ssm_glu_b"] = dd.sum(0), dgb_glu.sum(0)
    g_re = _mm(dy_p, c_dense_re, tb=True, name="s5_y_re_dx")
    g_im = _mm(dy_p, c_dense_im, tb=True, name="s5_y_im_dx")
    dc_re = _mm(h_re, dy_p, ta=True, name="s5_y_re_dw")
    dc_im = _mm(h_im, dy_p, ta=True, name="s5_y_im_dw")
    small["ssm_c_re"] = _block_diag_out_take(dc_re, n_groups, n_ch, n_state)
    small["ssm_c_im"] = -_block_diag_out_take(dc_im, n_groups, n_ch, n_state)
    lam_re, lam_im, da_re, da_im = _scan_bwd(g_re, g_im, h_re, h_im, a_re, a_im, n_seq, "s5_scan_bwd")
    dza_p = _mm(lam_re, b_dense_re, tb=True, name="s5_bu_re_dx", add=dza_skip)
    dza_p = _mm(lam_im, b_dense_im, tb=True, name="s5_bu_im_dx", add=dza_p, out_dtype=BF16)
    dbb_re = _block_diag_in_take(_mm(za_p16, lam_re, ta=True, name="s5_bu_re_dw"), n_groups, n_state, n_ch)
    dbb_im = _block_diag_in_take(_mm(za_p16, lam_im, ta=True, name="s5_bu_im_dw"), n_groups, n_state, n_ch)
    dab_re = da_re.sum(0).reshape(n_groups, n_state)
    dab_im = da_im.sum(0).reshape(n_groups, n_state)
    (small["ssm_lambda_re"], small["ssm_lambda_im"], small["ssm_log_dt"], small["ssm_b_re"],
     small["ssm_b_im"]) = disc_vjp((dab_re, dab_im, dbb_re, dbb_im))

    dproj = jnp.concatenate([_unperm_rows(dza_p, n_seq), dzu, dzv, dga, dgb], axis=1)
    send["mix_w_in"] = _split_cols(_mm(x1_16, dproj, ta=True, name="mix_in_dw", out_dtype=BF16))
    dx1_f = _mm(dproj, full["mix_w_in"], tb=True, name="mix_in_dx")

    dr1, dr1_h, dg, db = _ln_bwd_call(r1, dx1_f, dr2, alpha, row(wts["ln1_g"]), 0.5, "ln1_bwd")
    small["ln1_g"], small["ln1_b"] = dg.sum(0), db.sum(0)
    dh1, dw = _ffn_out_bwd(dr1_h, full["ffn1_w_out"], h1, a1, "ffn1_out_bwd")
    send["ffn1_w_out"] = dw.reshape(N_DEV, -1, d_model)
    dw = _ffn_in_dw(x16, dh1, "ffn1_in_dw")
    send["ffn1_w_in"] = dw.reshape((N_DEV,) + dw.shape[2:])
    grad_x = _ffn_in_dx(dh1, full["ffn1_w_in"], dr1, alpha, "ffn1_in_dx")

    small_shapes = {n: wts[n].shape for n in SMALL}
    send_small = _pack_small(small)[None]
    recv = _exchange([send[n] for n in SHARDED] + [send_small], "grad_exchange")
    results = {}
    for n, r in zip(SHARDED, recv[:-1]):
        results[n] = _adamw(r, wts[n], mom[n], var[n], "adamw_" + n)
    packed = _adamw(recv[-1], _pack_small(wts), _pack_small(mom), _pack_small(var), "adamw_small")
    unpacked = [_unpack_small(pk, small_shapes) for pk in packed]
    for n in SMALL:
        results[n] = tuple(u[n] for u in unpacked)

    outs = [loss, grad_x.reshape(n_seq, seq, d_model)]
    for k in range(4):
        outs += [results[n][k][None] for n in WEIGHTS]
    return tuple(outs)
```

```python
import functools
import math

import jax
import jax.numpy as jnp
from jax import lax
from jax.experimental import pallas as pl
from jax.experimental.pallas import tpu as pltpu

F32 = jnp.float32
BF16 = jnp.bfloat16

N_DEV = 8
LN_EPS = 1e-5
CHUNK = 128
N_SEG = 8
PACK_W = 1024
VMEM_LIMIT_BYTES = 56 * 1024 * 1024
MM_OPERAND_BLOCK_BYTES = 8 * 1024 * 1024
ADAM_BLOCK_BYTES = 6 * 1024 * 1024

ADAM_LR = 0.001
ADAM_B1 = 0.9
ADAM_B2 = 0.999
ADAM_EPS = 1e-08
ADAM_WD = 0.01
ADAM_STEP = 10

COL_SHARDED = ("ffn1_w_in", "mix_w_in", "up_a", "up_b", "ffn2_w_in", "ple_w_proj")
ROW_SHARDED = ("ffn1_w_out", "ssm_glu_w", "mix_w_out", "ffn2_w_out", "ple_w_gate")
SHARDED = ("ffn1_w_in", "ffn1_w_out", "mix_w_in", "ssm_glu_w", "up_a", "up_b", "mix_w_out",
           "ffn2_w_in", "ffn2_w_out", "ple_w_proj", "ple_w_gate")
WEIGHTS = ("ffn1_w_in", "ffn1_w_out", "ln1_g", "ln1_b", "mix_w_in", "ssm_lambda_re", "ssm_lambda_im",
           "ssm_log_dt", "ssm_b_re", "ssm_b_im", "ssm_c_re", "ssm_c_im", "ssm_d", "ssm_glu_w",
           "ssm_glu_b", "gmlp_ln_g", "gmlp_ln_b", "gmlp_w_s", "gmlp_b_s", "up_a", "up_b", "mix_w_out",
           "ln2_g", "ln2_b", "ffn2_w_in", "ffn2_w_out", "ln3_g", "ln3_b", "ple_w_proj", "ple_w_gate")
GATHER_FIRST = ("ffn1_w_in", "ffn1_w_out")
GATHER_MIX = ("mix_w_in", "ssm_glu_w", "up_a", "up_b", "mix_w_out")
GATHER_LAST = ("ffn2_w_in", "ffn2_w_out", "ple_w_proj", "ple_w_gate")
SMALL = tuple(n for n in WEIGHTS if n not in SHARDED)
SMALL_LATE = ("ln1_g", "ln1_b")
SMALL_EARLY = tuple(n for n in SMALL if n not in SMALL_LATE)
INPUT_NAMES = ("x", "p") + WEIGHTS + ("loss_target",) + tuple("m_" + n for n in WEIGHTS) + tuple(
    "v_" + n for n in WEIGHTS)


def _pick(dim, pref, mult=128):
    if dim <= pref:
        return dim
    d = (pref // mult) * mult
    while d >= mult:
        if dim % d == 0:
            return d
        d -= mult
    return dim


def _params(n_axes=1):
    return pltpu.CompilerParams(dimension_semantics=("arbitrary",) * n_axes,
                                vmem_limit_bytes=VMEM_LIMIT_BYTES)


def _sigmoid(v):
    return 1.0 / (1.0 + jnp.exp(-v))


_GELU_C = math.sqrt(2.0 / math.pi)


def _gelu(v):
    return 0.5 * v * (1.0 + jnp.tanh(_GELU_C * (v + 0.044715 * (v * v * v))))


def _gelu_grad(v):
    t = jnp.tanh(_GELU_C * (v + 0.044715 * (v * v * v)))
    return 0.5 * (1.0 + t) + 0.5 * v * (1.0 - t * t) * (_GELU_C * (1.0 + 3.0 * 0.044715 * v * v))


def _ln_stats(r):
    mu = jnp.mean(r, axis=-1, keepdims=True)
    xc = r - mu
    var = jnp.mean(xc * xc, axis=-1, keepdims=True)
    rstd = lax.rsqrt(var + LN_EPS)
    return xc * rstd, rstd


def _ln_bwd(dy, xhat, rstd, g):
    dxh = dy * g
    m1 = jnp.mean(dxh, axis=-1, keepdims=True)
    m2 = jnp.mean(dxh * xhat, axis=-1, keepdims=True)
    return rstd * (dxh - m1 - xhat * m2)


def _fold8(v):
    rows, w = v.shape
    return jnp.sum(v.reshape(rows // 8, 8, w), axis=0)


def _dot(a, b, ta=False, tb=False):
    dn = (((0 if ta else 1,), (1 if tb else 0,)), ((), ()))
    return lax.dot_general(a.astype(BF16), b.astype(BF16), dn, preferred_element_type=F32)


def _mm(a, b, *, name, ta=False, tb=False, out_dtype=F32, add=None, add_scale=1.0,
        tm=2048, tn=512, tk=4096):
    m_dim = a.shape[1] if ta else a.shape[0]
    k_dim = a.shape[0] if ta else a.shape[1]
    n_dim = b.shape[0] if tb else b.shape[1]
    assert (b.shape[1] if tb else b.shape[0]) == k_dim, (name, a.shape, b.shape)
    tk = _pick(k_dim, tk)
    tm = min(tm, max(256, MM_OPERAND_BLOCK_BYTES // (tk * a.dtype.itemsize)))
    tm, tn = _pick(m_dim, tm), _pick(n_dim, tn)
    nk = k_dim // tk
    has_add = add is not None

    def finish(r, add_ref, o_ref):
        if has_add:
            r = r + add_scale * add_ref[...].astype(F32)
        o_ref[...] = r.astype(out_dtype)

    def body(*refs):
        a_ref, b_ref = refs[:2]
        add_ref = refs[2] if has_add else None
        o_ref = refs[3] if has_add else refs[2]
        if nk == 1:
            finish(_dot(a_ref[...], b_ref[...], ta, tb), add_ref, o_ref)
            return
        acc_ref = refs[-1]
        k = pl.program_id(2)

        @pl.when(k == 0)
        def _():
            acc_ref[...] = jnp.zeros_like(acc_ref)

        acc_ref[...] += _dot(a_ref[...], b_ref[...], ta, tb)

        @pl.when(k == nk - 1)
        def _():
            finish(acc_ref[...], add_ref, o_ref)

    a_spec = (pl.BlockSpec((tk, tm), lambda i, j, k: (k, i)) if ta
              else pl.BlockSpec((tm, tk), lambda i, j, k: (i, k)))
    b_spec = (pl.BlockSpec((tn, tk), lambda i, j, k: (j, k)) if tb
              else pl.BlockSpec((tk, tn), lambda i, j, k: (k, j)))
    in_specs = [a_spec, b_spec]
    operands = [a, b]
    if has_add:
        in_specs.append(pl.BlockSpec((tm, tn), lambda i, j, k: (i, j)))
        operands.append(add)
    return pl.pallas_call(
        body, name=name,
        out_shape=jax.ShapeDtypeStruct((m_dim, n_dim), out_dtype),
        grid=(m_dim // tm, n_dim // tn, nk),
        in_specs=in_specs,
        out_specs=pl.BlockSpec((tm, tn), lambda i, j, k: (i, j)),
        scratch_shapes=[pltpu.VMEM((tm, tn), F32)] if nk > 1 else [],
        compiler_params=_params(3),
    )(*operands)


def _rows(tr, w, cb=0):
    return pl.BlockSpec((tr, w), lambda i: (i, cb))


def _whole(shape):
    nd = len(shape)
    return pl.BlockSpec(tuple(shape), lambda i: (0,) * nd)


def _ffn_in(x16, w_in, name):
    t, d = x16.shape
    _, nb, _, fb = w_in.shape
    tm = _pick(t, 1024, 8)

    def body(x_ref, wg_ref, wu_ref, h_ref, a_ref):
        xv = x_ref[...]
        g = _dot(xv, wg_ref[0, 0])
        u = _dot(xv, wu_ref[0, 0])
        h_ref[0, 0] = g
        h_ref[0, 1] = u
        a_ref[0] = (g * _sigmoid(g) * u).astype(BF16)

    return pl.pallas_call(
        body, name=name,
        out_shape=(jax.ShapeDtypeStruct((nb, 2, t, fb), F32), jax.ShapeDtypeStruct((nb, t, fb), BF16)),
        grid=(t // tm, nb),
        in_specs=[pl.BlockSpec((tm, d), lambda i, j: (i, 0)),
                  pl.BlockSpec((1, 1, d, fb), lambda i, j: (0, j, 0, 0)),
                  pl.BlockSpec((1, 1, d, fb), lambda i, j: (1, j, 0, 0))],
        out_specs=(pl.BlockSpec((1, 2, tm, fb), lambda i, j: (j, 0, i, 0)),
                   pl.BlockSpec((1, tm, fb), lambda i, j: (j, i, 0))),
        compiler_params=_params(2))(x16, w_in, w_in)


def _ffn_out_ln(a, w_out, xin, g, b, alpha, name):
    nb, t, fb = a.shape
    d = w_out.shape[2]
    tm = _pick(t, 256, 8)

    def body(a_ref, w_ref, x_ref, g_ref, b_ref, r_ref, y_ref, y16_ref):
        f = _dot(a_ref[0], w_ref[0])
        for jb in range(1, nb):
            f = f + _dot(a_ref[jb], w_ref[jb])
        r = alpha * x_ref[...] + 0.5 * f
        xhat, _ = _ln_stats(r)
        y = xhat * g_ref[...] + b_ref[...]
        r_ref[...] = r
        y_ref[...] = y
        y16_ref[...] = y.astype(BF16)

    return pl.pallas_call(
        body, name=name,
        out_shape=(jax.ShapeDtypeStruct((t, d), F32), jax.ShapeDtypeStruct((t, d), F32),
                   jax.ShapeDtypeStruct((t, d), BF16)),
        grid=(t // tm,),
        in_specs=[pl.BlockSpec((nb, tm, fb), lambda i: (0, i, 0)), _whole(w_out.shape), _rows(tm, d),
                  _whole((1, d)), _whole((1, d))],
        out_specs=(_rows(tm, d), _rows(tm, d), _rows(tm, d)),
        compiler_params=_params())(a, w_out, xin, g, b)


def _ffn_out_bwd(drs, w_out, h, a, name):
    nb, t, fb = a.shape
    d = w_out.shape[2]
    tm = _pick(t, 1024, 8)
    n_i = t // tm

    def body(dr_ref, w_ref, h_ref, a_ref, dh_ref, dw_ref, acc_ref):
        i = pl.program_id(1)
        dr = dr_ref[...]
        da = _dot(dr, w_ref[0], tb=True)
        g, u = h_ref[0, 0], h_ref[0, 1]
        sg = _sigmoid(g)
        dh_ref[0, 0] = (da * u * (sg * (1.0 + g * (1.0 - sg)))).astype(BF16)
        dh_ref[0, 1] = (da * (g * sg)).astype(BF16)

        @pl.when(i == 0)
        def _():
            acc_ref[...] = jnp.zeros_like(acc_ref)

        acc_ref[...] += _dot(a_ref[0], dr, ta=True)

        @pl.when(i == n_i - 1)
        def _():
            dw_ref[0] = acc_ref[...].astype(BF16)

    return pl.pallas_call(
        body, name=name,
        out_shape=(jax.ShapeDtypeStruct((nb, 2, t, fb), BF16), jax.ShapeDtypeStruct((nb, fb, d), BF16)),
        grid=(nb, n_i),
        in_specs=[pl.BlockSpec((tm, d), lambda j, i: (i, 0)),
                  pl.BlockSpec((1, fb, d), lambda j, i: (j, 0, 0)),
                  pl.BlockSpec((1, 2, tm, fb), lambda j, i: (j, 0, i, 0)),
                  pl.BlockSpec((1, tm, fb), lambda j, i: (j, i, 0))],
        out_specs=(pl.BlockSpec((1, 2, tm, fb), lambda j, i: (j, 0, i, 0)),
                   pl.BlockSpec((1, fb, d), lambda j, i: (j, 0, 0))),
        scratch_shapes=[pltpu.VMEM((fb, d), F32)],
        compiler_params=_params(2))(drs, w_out, h, a)


def _ffn_in_dw(x16, dh, name):
    t, d = x16.shape
    nb, _, _, fb = dh.shape

    def body(x_ref, dh_ref, o_ref):
        o_ref[0, 0] = _dot(x_ref[...], dh_ref[0, 0], ta=True).astype(BF16)

    return pl.pallas_call(
        body, name=name, out_shape=jax.ShapeDtypeStruct((2, nb, d, fb), BF16), grid=(nb, 2),
        in_specs=[pl.BlockSpec((t, d), lambda j, s: (0, 0)),
                  pl.BlockSpec((1, 1, t, fb), lambda j, s: (j, s, 0, 0))],
        out_specs=pl.BlockSpec((1, 1, d, fb), lambda j, s: (s, j, 0, 0)),
        compiler_params=_params(2))(x16, dh)


def _ffn_in_dx(dh, w_in, add, add_scale, name):
    nb, _, t, fb = dh.shape
    d = w_in.shape[2]
    tm = _pick(t, 512, 8)
    has_add = add is not None

    def body(*refs):
        dh_ref, w_ref = refs[:2]
        o_ref = refs[-1]
        acc = None
        for jb in range(nb):
            for s in range(2):
                part = _dot(dh_ref[jb, s], w_ref[s, jb], tb=True)
                acc = part if acc is None else acc + part
        if has_add:
            acc = acc + add_scale * refs[2][...]
        o_ref[...] = acc

    return pl.pallas_call(
        body, name=name, out_shape=jax.ShapeDtypeStruct((t, d), F32), grid=(t // tm,),
        in_specs=[pl.BlockSpec((nb, 2, tm, fb), lambda i: (0, 0, i, 0)), _whole(w_in.shape)]
        + ([_rows(tm, d)] if has_add else []),
        out_specs=_rows(tm, d),
        compiler_params=_params())(*([dh, w_in] + ([add] if has_add else [])))


def _resid_ln_fwd(xin, f, g, b, alpha, scale, name):
    t, d = xin.shape
    tr = _pick(t, 256, 8)

    def body(x_ref, f_ref, g_ref, b_ref, r_ref, y_ref, y16_ref):
        r = alpha * x_ref[...] + scale * f_ref[...]
        xhat, _ = _ln_stats(r)
        y = xhat * g_ref[...] + b_ref[...]
        r_ref[...] = r
        y_ref[...] = y
        y16_ref[...] = y.astype(BF16)

    return pl.pallas_call(
        body, name=name,
        out_shape=(jax.ShapeDtypeStruct((t, d), F32), jax.ShapeDtypeStruct((t, d), F32),
                   jax.ShapeDtypeStruct((t, d), BF16)),
        grid=(t // tr,),
        in_specs=[_rows(tr, d), _rows(tr, d), _whole((1, d)), _whole((1, d))],
        out_specs=(_rows(tr, d), _rows(tr, d), _rows(tr, d)),
        compiler_params=_params())(xin, f, g, b)


def _ln_bwd_call(r, dy_a, dy_b, b_scale, g, out_scale, name):
    t, d = r.shape
    tr = _pick(t, 256, 8)
    has_b = dy_b is not None

    def body(*refs):
        if has_b:
            r_ref, da_ref, db_ref, g_ref, dr_ref, drs_ref, dg_ref, dbeta_ref = refs
            dy = da_ref[...] + b_scale * db_ref[...]
        else:
            r_ref, da_ref, g_ref, dr_ref, drs_ref, dg_ref, dbeta_ref = refs
            dy = da_ref[...]
        xhat, rstd = _ln_stats(r_ref[...])
        dr = _ln_bwd(dy, xhat, rstd, g_ref[...])
        dr_ref[...] = dr
        drs_ref[...] = (out_scale * dr).astype(BF16)

        @pl.when(pl.program_id(0) == 0)
        def _():
            dg_ref[...] = jnp.zeros_like(dg_ref)
            dbeta_ref[...] = jnp.zeros_like(dbeta_ref)

        dg_ref[...] += _fold8(dy * xhat)
        dbeta_ref[...] += _fold8(dy)

    ins = [r, dy_a] + ([dy_b] if has_b else []) + [g]
    in_specs = [_rows(tr, d)] * (3 if has_b else 2) + [_whole((1, d))]
    return pl.pallas_call(
        body, name=name,
        out_shape=(jax.ShapeDtypeStruct((t, d), F32), jax.ShapeDtypeStruct((t, d), BF16),
                   jax.ShapeDtypeStruct((8, d), F32), jax.ShapeDtypeStruct((8, d), F32)),
        grid=(t // tr,), in_specs=in_specs,
        out_specs=(_rows(tr, d), _rows(tr, d), _whole((8, d)), _whole((8, d))),
        compiler_params=_params())(*ins)


def _take_row(v, row_id, s):
    return jnp.sum(jnp.where(row_id == s, v, 0.0), axis=0, keepdims=True)


def _seg_carries(f_re, f_im, p_re, p_im, reverse):
    row_id = lax.broadcasted_iota(jnp.int32, f_re.shape, 0)
    p_re, p_im = _take_row(p_re, row_id, 0), _take_row(p_im, row_id, 0)
    out_re, out_im = jnp.zeros_like(f_re), jnp.zeros_like(f_im)
    c_re, c_im = jnp.zeros_like(p_re), jnp.zeros_like(p_im)
    order = range(N_SEG - 1, -1, -1) if reverse else range(N_SEG)
    for s in order:
        out_re = jnp.where(row_id == s, c_re, out_re)
        out_im = jnp.where(row_id == s, c_im, out_im)
        fr, fi = _take_row(f_re, row_id, s), _take_row(f_im, row_id, s)
        c_re, c_im = fr + p_re * c_re - p_im * c_im, fi + p_re * c_im + p_im * c_re
    return out_re, out_im


def _scan_fwd(bu_re, bu_im, a_re, a_im, n_seq, name):
    t, n = bu_re.shape
    seq = t // n_seq
    steps = seq // N_SEG
    cb = _pick(n, 256)

    def body(bre, bim, are, aim, hre, him):
        ar = jnp.broadcast_to(are[...], (N_SEG, cb))
        ai = jnp.broadcast_to(aim[...], (N_SEG, cb))
        zero = jnp.zeros((N_SEG, cb), F32)

        def local(k, carry):
            lr, li, pr, pi = carry
            rows = pl.ds(pl.multiple_of(k * N_SEG, N_SEG), N_SEG)
            nr = ar * lr - ai * li + bre[rows, :]
            ni = ar * li + ai * lr + bim[rows, :]
            hre[rows, :] = nr
            him[rows, :] = ni
            return nr, ni, ar * pr - ai * pi, ar * pi + ai * pr

        f_re, f_im, p_re, p_im = lax.fori_loop(0, steps, local, (zero, zero, zero + 1.0, zero))
        c_re, c_im = _seg_carries(f_re, f_im, p_re, p_im, reverse=False)

        def fix(k, carry):
            pr, pi = carry
            rows = pl.ds(pl.multiple_of(k * N_SEG, N_SEG), N_SEG)
            hre[rows, :] = hre[rows, :] + (pr * c_re - pi * c_im)
            him[rows, :] = him[rows, :] + (pr * c_im + pi * c_re)
            return ar * pr - ai * pi, ar * pi + ai * pr

        lax.fori_loop(0, steps, fix, (ar, ai))

    blk = pl.BlockSpec((seq, cb), lambda s, j: (s, j))
    vec = pl.BlockSpec((1, cb), lambda s, j: (0, j))
    return pl.pallas_call(
        body, name=name,
        out_shape=(jax.ShapeDtypeStruct((t, n), F32), jax.ShapeDtypeStruct((t, n), F32)),
        grid=(n_seq, n // cb), in_specs=[blk, blk, vec, vec], out_specs=(blk, blk),
        compiler_params=_params(2))(bu_re, bu_im, a_re, a_im)


def _scan_bwd(g_re, g_im, h_re, h_im, a_re, a_im, n_seq, name):
    t, n = g_re.shape
    seq = t // n_seq
    steps = seq // N_SEG
    cb = _pick(n, 256)

    def body(gre, gim, hre, him, are, aim, lre, lim, dare, daim):
        ar = jnp.broadcast_to(are[...], (N_SEG, cb))
        ai = -jnp.broadcast_to(aim[...], (N_SEG, cb))
        zero = jnp.zeros((N_SEG, cb), F32)

        def local(i, carry):
            lr, li, pr, pi = carry
            k = steps - 1 - i
            rows = pl.ds(pl.multiple_of(k * N_SEG, N_SEG), N_SEG)
            nr = ar * lr - ai * li + gre[rows, :]
            ni = ar * li + ai * lr + gim[rows, :]
            lre[rows, :] = nr
            lim[rows, :] = ni
            return nr, ni, ar * pr - ai * pi, ar * pi + ai * pr

        f_re, f_im, p_re, p_im = lax.fori_loop(0, steps, local, (zero, zero, zero + 1.0, zero))
        c_re, c_im = _seg_carries(f_re, f_im, p_re, p_im, reverse=True)

        def accumulate(lam_r, lam_i, hp_r, hp_i, acc_r, acc_i):
            return acc_r + (lam_r * hp_r + lam_i * hp_i), acc_i + (lam_i * hp_r - lam_r * hp_i)

        def fix(i, carry):
            pr, pi, acc_r, acc_i = carry
            k = steps - 1 - i
            rows = pl.ds(pl.multiple_of(k * N_SEG, N_SEG), N_SEG)
            prev = pl.ds(pl.multiple_of((k - 1) * N_SEG, N_SEG), N_SEG)
            lam_r = lre[rows, :] + (pr * c_re - pi * c_im)
            lam_i = lim[rows, :] + (pr * c_im + pi * c_re)
            lre[rows, :] = lam_r
            lim[rows, :] = lam_i
            acc_r, acc_i = accumulate(lam_r, lam_i, hre[prev, :], him[prev, :], acc_r, acc_i)
            return ar * pr - ai * pi, ar * pi + ai * pr, acc_r, acc_i

        pr, pi, acc_r, acc_i = lax.fori_loop(0, steps - 1, fix, (ar, ai, zero, zero))
        first = pl.ds(0, N_SEG)
        last = pl.ds((steps - 1) * N_SEG, N_SEG)
        lam_r = lre[first, :] + (pr * c_re - pi * c_im)
        lam_i = lim[first, :] + (pr * c_im + pi * c_re)
        lre[first, :] = lam_r
        lim[first, :] = lam_i
        row_id = lax.broadcasted_iota(jnp.int32, (N_SEG, cb), 0)
        hp_r = jnp.where(row_id == 0, 0.0, pltpu.roll(hre[last, :], 1, 0))
        hp_i = jnp.where(row_id == 0, 0.0, pltpu.roll(him[last, :], 1, 0))
        acc_r, acc_i = accumulate(lam_r, lam_i, hp_r, hp_i, acc_r, acc_i)
        dare[...] = acc_r
        daim[...] = acc_i

    blk = pl.BlockSpec((seq, cb), lambda s, j: (s, j))
    vec = pl.BlockSpec((1, cb), lambda s, j: (0, j))
    acc = pl.BlockSpec((N_SEG, cb), lambda s, j: (s, j))
    return pl.pallas_call(
        body, name=name,
        out_shape=(jax.ShapeDtypeStruct((t, n), F32), jax.ShapeDtypeStruct((t, n), F32),
                   jax.ShapeDtypeStruct((n_seq * N_SEG, n), F32),
                   jax.ShapeDtypeStruct((n_seq * N_SEG, n), F32)),
        grid=(n_seq, n // cb), in_specs=[blk, blk, blk, blk, vec, vec],
        out_specs=(blk, blk, acc, acc),
        compiler_params=_params(2))(g_re, g_im, h_re, h_im, a_re, a_im)


def _s5_post_fwd(yssm, u, d_skip, glu_w, glu_b, name):
    t, w = yssm.shape
    tr = _pick(t, 512, 8)

    def body(y_ref, u_ref, d_ref, w_ref, b_ref, o_ref):
        yg = _gelu(y_ref[...] + d_ref[...] * u_ref[...])
        pre = _dot(yg, w_ref[...]) + b_ref[...]
        o_ref[...] = yg * _sigmoid(pre)

    return pl.pallas_call(
        body, name=name, out_shape=jax.ShapeDtypeStruct((t, w), F32), grid=(t // tr,),
        in_specs=[_rows(tr, w), _rows(tr, w), _whole((1, w)), _whole((w, w)), _whole((1, w))],
        out_specs=_rows(tr, w), compiler_params=_params())(yssm, u, d_skip, glu_w, glu_b)


def _s5_post_bwd(yssm, u, dout, d_skip, glu_w, glu_b, name):
    t, w = yssm.shape
    tr = _pick(t, 512, 8)

    def body(y_ref, u_ref, do_ref, d_ref, w_ref, b_ref, dy_ref, du_ref, dw_ref, dd_ref, db_ref):
        uu = u_ref[...]
        y = y_ref[...] + d_ref[...] * uu
        yg = _gelu(y)
        sg = _sigmoid(_dot(yg, w_ref[...]) + b_ref[...])
        do = do_ref[...]
        dpre = do * yg * sg * (1.0 - sg)
        dyg = do * sg + _dot(dpre, w_ref[...], tb=True)
        dy = dyg * _gelu_grad(y)
        dy_ref[...] = dy.astype(BF16)
        du_ref[...] = dy * d_ref[...]

        @pl.when(pl.program_id(0) == 0)
        def _():
            dw_ref[...] = jnp.zeros_like(dw_ref)
            dd_ref[...] = jnp.zeros_like(dd_ref)
            db_ref[...] = jnp.zeros_like(db_ref)

        dw_ref[...] += _dot(yg, dpre, ta=True)
        dd_ref[...] += _fold8(dy * uu)
        db_ref[...] += _fold8(dpre)

    return pl.pallas_call(
        body, name=name,
        out_shape=(jax.ShapeDtypeStruct((t, w), BF16), jax.ShapeDtypeStruct((t, w), F32),
                   jax.ShapeDtypeStruct((w, w), F32), jax.ShapeDtypeStruct((8, w), F32),
                   jax.ShapeDtypeStruct((8, w), F32)),
        grid=(t // tr,),
        in_specs=[_rows(tr, w), _rows(tr, w), _rows(tr, w), _whole((1, w)), _whole((w, w)),
                  _whole((1, w))],
        out_specs=(_rows(tr, w), _rows(tr, w), _whole((w, w)), _whole((8, w)), _whole((8, w))),
        compiler_params=_params())(yssm, u, dout, d_skip, glu_w, glu_b)


def _head_select(parts, head_dim):
    col_head = lax.broadcasted_iota(jnp.int32, parts[0].shape, 1) // head_dim
    out = jnp.zeros_like(parts[0])
    for h, part in enumerate(parts):
        out = jnp.where(col_head == h, part, out)
    return out


def _gmlp_fwd(proj, ln_g, ln_b, ws, bs_cols, name):
    t = proj.shape[0]
    n_heads = ws.shape[0]
    w = bs_cols.shape[1]
    head_dim = w // n_heads
    cpb = _pick(t // CHUNK, 4, 1)
    tr = cpb * CHUNK

    def body(zu_ref, zv_ref, g_ref, b_ref, ws_ref, bs_ref, o_ref):
        for c in range(cpb):
            rows = pl.ds(c * CHUNK, CHUNK)
            xhat, _ = _ln_stats(_gelu(zv_ref[rows, :]))
            vn = (xhat * g_ref[...] + b_ref[...]).astype(BF16)
            s = _head_select([_dot(ws_ref[h], vn) for h in range(n_heads)], head_dim) + bs_ref[...]
            o_ref[rows, :] = _gelu(zu_ref[rows, :]) * s

    return pl.pallas_call(
        body, name=name, out_shape=jax.ShapeDtypeStruct((t, w), F32), grid=(t // tr,),
        in_specs=[_rows(tr, w, 1), _rows(tr, w, 2), _whole((1, w)), _whole((1, w)),
                  _whole(ws.shape), _whole(bs_cols.shape)],
        out_specs=_rows(tr, w), compiler_params=_params())(proj, proj, ln_g, ln_b, ws, bs_cols)


def _gmlp_bwd(proj, dout, ln_g, ln_b, ws, ws_t, bs_cols, name):
    t = proj.shape[0]
    n_heads = ws.shape[0]
    w = bs_cols.shape[1]
    head_dim = w // n_heads
    cpb = _pick(t // CHUNK, 4, 1)
    tr = cpb * CHUNK

    def body(zu_ref, zv_ref, do_ref, g_ref, b_ref, ws_ref, wst_ref, bs_ref,
             dzu_ref, dzv_ref, dws_ref, dbs_ref, dg_ref, dbeta_ref):
        @pl.when(pl.program_id(0) == 0)
        def _():
            dws_ref[...] = jnp.zeros_like(dws_ref)
            dbs_ref[...] = jnp.zeros_like(dbs_ref)
            dg_ref[...] = jnp.zeros_like(dg_ref)
            dbeta_ref[...] = jnp.zeros_like(dbeta_ref)

        col_head = lax.broadcasted_iota(jnp.int32, (CHUNK, w), 1) // head_dim
        for c in range(cpb):
            rows = pl.ds(c * CHUNK, CHUNK)
            zu, zv, do = zu_ref[rows, :], zv_ref[rows, :], do_ref[rows, :]
            xhat, rstd = _ln_stats(_gelu(zv))
            vn = (xhat * g_ref[...] + b_ref[...]).astype(BF16)
            s = _head_select([_dot(ws_ref[h], vn) for h in range(n_heads)], head_dim) + bs_ref[...]
            dzu_ref[rows, :] = (do * s * _gelu_grad(zu)).astype(BF16)
            ds = do * _gelu(zu)
            ds16 = ds.astype(BF16)
            dbs_ref[...] += ds
            for h in range(n_heads):
                dws_ref[h] += _dot(jnp.where(col_head == h, ds16, jnp.zeros_like(ds16)), vn, tb=True)
            dvn = _head_select([_dot(wst_ref[h], ds16) for h in range(n_heads)], head_dim)
            dg_ref[...] += _fold8(dvn * xhat)
            dbeta_ref[...] += _fold8(dvn)
            dgv = _ln_bwd(dvn, xhat, rstd, g_ref[...])
            dzv_ref[rows, :] = (dgv * _gelu_grad(zv)).astype(BF16)

    return pl.pallas_call(
        body, name=name,
        out_shape=(jax.ShapeDtypeStruct((t, w), BF16), jax.ShapeDtypeStruct((t, w), BF16),
                   jax.ShapeDtypeStruct(ws.shape, F32), jax.ShapeDtypeStruct((CHUNK, w), F32),
                   jax.ShapeDtypeStruct((8, w), F32), jax.ShapeDtypeStruct((8, w), F32)),
        grid=(t // tr,),
        in_specs=[_rows(tr, w, 1), _rows(tr, w, 2), _rows(tr, w), _whole((1, w)), _whole((1, w)),
                  _whole(ws.shape), _whole(ws.shape), _whole(bs_cols.shape)],
        out_specs=(_rows(tr, w), _rows(tr, w), _whole(ws.shape), _whole((CHUNK, w)),
                   _whole((8, w)), _whole((8, w))),
        compiler_params=_params())(proj, proj, dout, ln_g, ln_b, ws, ws_t, bs_cols)


def _merge_fwd(s5out, gm, proj, up_a, up_b, name):
    t, w = s5out.shape
    d = up_a.shape[1]
    assert d == 2 * w
    tr = _pick(t, 256, 8)

    def body(s_ref, gm_ref, ga0, ga1, gb0, gb1, ua_ref, ub_ref, m_ref, ya_ref, yb_ref):
        ya = _dot(s_ref[...], ua_ref[...])
        yb = _dot(gm_ref[...], ub_ref[...])
        ya_ref[...] = ya
        yb_ref[...] = yb
        for half, (ga, gb) in enumerate(((ga0, gb0), (ga1, gb1))):
            cols = slice(half * w, (half + 1) * w)
            m_ref[:, cols] = (_sigmoid(ga[...]) * ya[:, cols] + _sigmoid(gb[...]) * yb[:, cols]).astype(BF16)

    return pl.pallas_call(
        body, name=name,
        out_shape=(jax.ShapeDtypeStruct((t, d), BF16), jax.ShapeDtypeStruct((t, d), F32),
                   jax.ShapeDtypeStruct((t, d), F32)),
        grid=(t // tr,),
        in_specs=[_rows(tr, w), _rows(tr, w), _rows(tr, w, 3), _rows(tr, w, 4), _rows(tr, w, 5),
                  _rows(tr, w, 6), _whole(up_a.shape), _whole(up_b.shape)],
        out_specs=(_rows(tr, d), _rows(tr, d), _rows(tr, d)),
        compiler_params=_params())(s5out, gm, proj, proj, proj, proj, up_a, up_b)


def _merge_bwd(dm, ya, yb, s5out, gm, proj, up_a, up_b, name):
    t, d = dm.shape
    w = d // 2
    tr = _pick(t, 256, 8)

    def body(dm_ref, ya_ref, yb_ref, s_ref, gm_ref, ga0, ga1, gb0, gb1, ua_ref, ub_ref,
             dga_ref, dgb_ref, ds_ref, dgm_ref, dua_ref, dub_ref):
        dm_v, ya, yb = dm_ref[...], ya_ref[...], yb_ref[...]
        dya, dyb = [], []
        for half, (ga, gb) in enumerate(((ga0, gb0), (ga1, gb1))):
            cols = slice(half * w, (half + 1) * w)
            sa, sb = _sigmoid(ga[...]), _sigmoid(gb[...])
            dmh = dm_v[:, cols]
            dga_ref[:, cols] = (dmh * ya[:, cols] * sa * (1.0 - sa)).astype(BF16)
            dgb_ref[:, cols] = (dmh * yb[:, cols] * sb * (1.0 - sb)).astype(BF16)
            dya.append((dmh * sa).astype(BF16))
            dyb.append((dmh * sb).astype(BF16))
        dya = jnp.concatenate(dya, axis=1)
        dyb = jnp.concatenate(dyb, axis=1)
        ds_ref[...] = _dot(dya, ua_ref[...], tb=True)
        dgm_ref[...] = _dot(dyb, ub_ref[...], tb=True)

        @pl.when(pl.program_id(0) == 0)
        def _():
            dua_ref[...] = jnp.zeros_like(dua_ref)
            dub_ref[...] = jnp.zeros_like(dub_ref)

        dua_ref[...] += _dot(s_ref[...], dya, ta=True)
        dub_ref[...] += _dot(gm_ref[...], dyb, ta=True)

    return pl.pallas_call(
        body, name=name,
        out_shape=(jax.ShapeDtypeStruct((t, d), BF16), jax.ShapeDtypeStruct((t, d), BF16),
                   jax.ShapeDtypeStruct((t, w), F32), jax.ShapeDtypeStruct((t, w), F32),
                   jax.ShapeDtypeStruct(up_a.shape, F32), jax.ShapeDtypeStruct(up_b.shape, F32)),
        grid=(t // tr,),
        in_specs=[_rows(tr, d), _rows(tr, d), _rows(tr, d), _rows(tr, w), _rows(tr, w),
                  _rows(tr, w, 3), _rows(tr, w, 4), _rows(tr, w, 5), _rows(tr, w, 6),
                  _whole(up_a.shape), _whole(up_b.shape)],
        out_specs=(_rows(tr, d), _rows(tr, d), _rows(tr, w), _rows(tr, w), _whole(up_a.shape),
                   _whole(up_b.shape)),
        compiler_params=_params())(dm, ya, yb, s5out, gm, proj, proj, proj, proj, up_a, up_b)


def _head(x3, p, target, w_gate, w_proj, name):
    t, d = x3.shape
    pd = p.shape[1]
    tr = _pick(t, 256, 8)
    nb = t // tr

    def body(x_ref, p_ref, t_ref, wg_ref, wp_ref, loss_ref, dx_ref, dwg_ref, dwp_ref):
        xv = x_ref[...]
        sg = _sigmoid(_dot(xv, wg_ref[...]))
        pp = _dot(p_ref[...], wp_ref[...])
        err = xv + sg * pp - t_ref[...]
        loss_ref[...] = jnp.full((8, 128), 0.5 * jnp.sum(jnp.mean(err * err, axis=-1)), F32)
        dout = err * (1.0 / d)
        dgpre = dout * pp * sg * (1.0 - sg)
        dpp = dout * sg
        dx_ref[...] = dout + _dot(dgpre, wg_ref[...], tb=True)

        @pl.when(pl.program_id(0) == 0)
        def _():
            dwg_ref[...] = jnp.zeros_like(dwg_ref)
            dwp_ref[...] = jnp.zeros_like(dwp_ref)

        dwg_ref[...] += _dot(xv, dgpre, ta=True)
        dwp_ref[...] += _dot(p_ref[...], dpp, ta=True)

    return pl.pallas_call(
        body, name=name,
        out_shape=(jax.ShapeDtypeStruct((nb * 8, 128), F32), jax.ShapeDtypeStruct((t, d), F32),
                   jax.ShapeDtypeStruct((d, d), F32), jax.ShapeDtypeStruct((pd, d), F32)),
        grid=(nb,),
        in_specs=[_rows(tr, d), _rows(tr, pd), _rows(tr, d), _whole((d, d)), _whole((pd, d))],
        out_specs=(pl.BlockSpec((8, 128), lambda i: (i, 0)), _rows(tr, d), _whole((d, d)),
                   _whole((pd, d))),
        compiler_params=_params())(x3, p, target, w_gate, w_proj)


_HBM = pl.BlockSpec(memory_space=pltpu.HBM)


def _all_gather(shards, name):
    n = len(shards)

    def body(*refs):
        x_refs, out_refs = refs[:n], refs[n:2 * n]
        send_sems, recv_sems, local_sems = refs[2 * n:]
        x, y, c = lax.axis_index("x"), lax.axis_index("y"), lax.axis_index("c")
        me, sibling = (x, y, c), (x, y, 1 - c)
        chips = [(1 - x, y), (x, 1 - y), (1 - x, 1 - y)]

        def copy(a, k, block, to, own=False):
            slot = out_refs[a].at[4 * block[0] + 2 * block[1] + block[2]]
            return pltpu.make_async_remote_copy(
                src_ref=x_refs[a] if own else slot, dst_ref=slot,
                send_sem=send_sems.at[7 * a + k], recv_sem=recv_sems.at[7 * a + k],
                device_id=to, device_id_type=pl.DeviceIdType.MESH)

        mine = [pltpu.make_async_copy(x_refs[a], out_refs[a].at[4 * x + 2 * y + c], local_sems.at[a])
                for a in range(n)]
        sends = []
        for a in range(n):
            mine[a].start()
            first = [copy(a, 0, me, sibling, own=True)]
            first += [copy(a, 1 + j, me, (*chip, c), own=True) for j, chip in enumerate(chips)]
            for cp in first:
                cp.start()
            sends += first
        for a in range(n):
            for j, chip in enumerate(chips):
                copy(a, 1 + j, (*chip, c), me).wait_recv()
                passed = copy(a, 4 + j, (*chip, c), sibling)
                passed.start()
                sends.append(passed)
        for a in range(n):
            copy(a, 0, sibling, me).wait_recv()
            for j, chip in enumerate(chips):
                copy(a, 4 + j, (*chip, 1 - c), me).wait_recv()
        for cp in sends:
            cp.wait_send()
        for cp in mine:
            cp.wait()

    return pl.pallas_call(
        body, name=name,
        out_shape=tuple(jax.ShapeDtypeStruct((N_DEV,) + s.shape, s.dtype) for s in shards),
        in_specs=[_HBM] * n, out_specs=tuple([_HBM] * n),
        scratch_shapes=[pltpu.SemaphoreType.DMA((7 * n,)), pltpu.SemaphoreType.DMA((7 * n,)),
                        pltpu.SemaphoreType.DMA((n,))],
    )(*shards)


_SEM = pl.BlockSpec(memory_space=pltpu.SEMAPHORE)
_ANY = pl.BlockSpec(memory_space=pl.ANY)
_DATAFLOW = pltpu.SideEffectType.DATAFLOW_SIDE_EFFECTING


def _peer(k, x, y, c):
    return (1 - x if k & 4 else x, 1 - y if k & 2 else y, 1 - c if k & 1 else c)


def _exchange_start(sends, after, name):
    n = len(sends)
    me = 4 * lax.axis_index("x") + 2 * lax.axis_index("y") + lax.axis_index("c")
    lands = []
    for s in sends:
        own = s[0] if s.shape[0] == 1 else lax.dynamic_index_in_dim(s, me, 0, keepdims=False)
        land = jnp.zeros((N_DEV,) + s.shape[1:], s.dtype)
        lands.append(lax.dynamic_update_index_in_dim(land, own, me, 0))

    def body(*refs):
        s_refs, l_refs = refs[:n], refs[n:2 * n]
        send_sems, recv_sems = refs[2 * n + 1], refs[2 * n + 2]
        x, y, c = lax.axis_index("x"), lax.axis_index("y"), lax.axis_index("c")
        for a in range(n):
            same = sends[a].shape[0] == 1
            for k in range(1, N_DEV):
                px, py, pc = _peer(k, x, y, c)
                pltpu.make_async_remote_copy(
                    src_ref=s_refs[a].at[0 if same else 4 * px + 2 * py + pc],
                    dst_ref=l_refs[a].at[4 * x + 2 * y + c],
                    send_sem=send_sems.at[7 * a + k - 1], recv_sem=recv_sems.at[7 * a + k - 1],
                    device_id=(px, py, pc), device_id_type=pl.DeviceIdType.MESH).start()

    outs = pl.pallas_call(
        body, name=name,
        out_shape=(pltpu.SemaphoreType.DMA((7 * n,)), pltpu.SemaphoreType.DMA((7 * n,)),
                   *[pltpu.HBM(s.shape, s.dtype) for s in sends],
                   *[pltpu.HBM(l.shape, l.dtype) for l in lands]),
        in_specs=[_HBM] * (2 * n) + [_ANY],
        out_specs=(_SEM, _SEM, *([_HBM] * (2 * n))),
        input_output_aliases={i: 2 + i for i in range(2 * n)},
        compiler_params=pltpu.CompilerParams(has_side_effects=_DATAFLOW),
    )(*[pltpu.with_memory_space_constraint(v, pltpu.HBM) for v in list(sends) + lands], after)
    return outs[0], outs[1], list(outs[2:2 + n]), list(outs[2 + n:])


def _exchange_wait(handle, after, name):
    send_sems, recv_sems, sends, lands = handle
    n = len(sends)

    def body(*refs):
        s_refs, l_refs = refs[:n], refs[n:2 * n]
        send_sems, recv_sems = refs[2 * n], refs[2 * n + 1]
        x, y, c = lax.axis_index("x"), lax.axis_index("y"), lax.axis_index("c")
        for a in range(n):
            for k in range(1, N_DEV):
                copy = pltpu.make_async_remote_copy(
                    src_ref=s_refs[a].at[0], dst_ref=l_refs[a].at[0],
                    send_sem=send_sems.at[7 * a + k - 1], recv_sem=recv_sems.at[7 * a + k - 1],
                    device_id=_peer(k, x, y, c), device_id_type=pl.DeviceIdType.MESH)
                copy.wait_send()
                copy.wait_recv()

    outs = pl.pallas_call(
        body, name=name,
        out_shape=(*[pltpu.HBM(s.shape, s.dtype) for s in sends], *[pltpu.HBM(l.shape, l.dtype) for l in lands]),
        in_specs=[_HBM] * (2 * n) + [_SEM, _SEM, _ANY],
        out_specs=tuple([_HBM] * (2 * n)),
        input_output_aliases={i: i for i in range(2 * n)},
        compiler_params=pltpu.CompilerParams(has_side_effects=_DATAFLOW),
    )(*sends, *lands, send_sems, recv_sems, after)
    return list(outs[n:])


def _adamw(recv, w, m, v, name):
    n, rows, width = recv.shape
    tr = _pick(rows, max(8, ADAM_BLOCK_BYTES // (n * width * recv.dtype.itemsize)), 8)
    c_m = 1.0 - ADAM_B1 ** ADAM_STEP
    c_v = 1.0 - ADAM_B2 ** ADAM_STEP

    def body(r_ref, w_ref, m_ref, v_ref, g_ref, d_ref, nm_ref, nv_ref):
        g = r_ref[0].astype(F32)
        for i in range(1, n):
            g = g + r_ref[i].astype(F32)
        m2 = ADAM_B1 * m_ref[...] + (1.0 - ADAM_B1) * g
        v2 = ADAM_B2 * v_ref[...] + (1.0 - ADAM_B2) * (g * g)
        m_hat = m2 / c_m
        v_hat = v2 / c_v
        g_ref[...] = g
        d_ref[...] = -ADAM_LR * (m_hat / (jnp.sqrt(v_hat) + ADAM_EPS) + ADAM_WD * w_ref[...])
        nm_ref[...] = m2
        nv_ref[...] = v2

    blk = _rows(tr, width)
    shp = jax.ShapeDtypeStruct((rows, width), F32)
    return pl.pallas_call(
        body, name=name, out_shape=(shp, shp, shp, shp), grid=(rows // tr,),
        in_specs=[pl.BlockSpec((n, tr, width), lambda i: (0, i, 0)), blk, blk, blk],
        out_specs=(blk, blk, blk, blk), compiler_params=_params())(recv, w, m, v)


def _join_cols(gathered):
    n, r, c = gathered.shape
    return gathered.transpose(1, 0, 2).reshape(r, n * c)


def _split_cols(full):
    r, c = full.shape
    return full.reshape(r, N_DEV, c // N_DEV).transpose(1, 0, 2)


def _small_rows(size):
    return -(-size // (8 * PACK_W)) * 8


def _pack_small(parts, names):
    rows = []
    for n in names:
        flat = parts[n].reshape(-1)
        r = _small_rows(flat.shape[0])
        rows.append(jnp.pad(flat, (0, r * PACK_W - flat.shape[0])).reshape(r, PACK_W))
    return jnp.concatenate(rows, axis=0)


def _unpack_small(packed, shapes, names):
    out, r0 = {}, 0
    for n in names:
        size = math.prod(shapes[n])
        rows = _small_rows(size)
        out[n] = packed[r0:r0 + rows].reshape(-1)[:size].reshape(shapes[n])
        r0 += rows
    return out


def _discretise(lam_re, lam_im, log_dt, b_re, b_im):
    dt = jnp.exp(log_dt)[:, None]
    mag = jnp.exp(lam_re * dt)
    ab_re = mag * jnp.cos(lam_im * dt)
    ab_im = mag * jnp.sin(lam_im * dt)
    nr = ab_re - 1.0
    ni = ab_im
    den = lam_re * lam_re + lam_im * lam_im
    coef_re = ((nr * lam_re + ni * lam_im) / den)[..., None]
    coef_im = ((ni * lam_re - nr * lam_im) / den)[..., None]
    bb_re = coef_re * b_re - coef_im * b_im
    bb_im = coef_re * b_im + coef_im * b_re
    return ab_re, ab_im, bb_re, bb_im


def _same_group(g):
    return jnp.eye(g, dtype=bool)[:, None, :, None]


def _block_diag_in(bb):
    g, p, i = bb.shape
    placed = jnp.where(_same_group(g), bb.transpose(0, 2, 1)[:, :, None, :], 0.0)
    return placed.reshape(g * i, g * p)


def _block_diag_in_take(dense, g, p, i):
    blocks = jnp.where(_same_group(g), dense.reshape(g, i, g, p), 0.0).sum(axis=2)
    return blocks.transpose(0, 2, 1)


def _block_diag_out(cc):
    g, i, p = cc.shape
    placed = jnp.where(_same_group(g), cc.transpose(0, 2, 1)[:, :, None, :], 0.0)
    return placed.reshape(g * p, g * i)


def _block_diag_out_take(dense, g, i, p):
    blocks = jnp.where(_same_group(g), dense.reshape(g, p, g, i), 0.0).sum(axis=2)
    return blocks.transpose(0, 2, 1)


def _perm_rows(z, n_seq):
    t, w = z.shape
    steps = t // n_seq // N_SEG
    return z.reshape(n_seq, N_SEG, steps, w).transpose(0, 2, 1, 3).reshape(t, w)


def _unperm_rows(z, n_seq):
    t, w = z.shape
    steps = t // n_seq // N_SEG
    return z.reshape(n_seq, steps, N_SEG, w).transpose(0, 2, 1, 3).reshape(t, w)


def kernel(*args):
    assert len(args) == len(INPUT_NAMES)
    inp = dict(zip(INPUT_NAMES, args))
    depth = inp["ffn1_w_in"].shape[0]
    assert depth == 1
    alpha = (2.0 * depth) ** 0.25

    n_seq, seq, d_model = inp["x"].shape
    t = n_seq * seq
    x = inp["x"].reshape(t, d_model)
    x16 = x.astype(BF16)
    p16 = inp["p"][0].reshape(t, -1).astype(BF16)
    target = inp["loss_target"].reshape(t, d_model)
    wts = {n: inp[n][0] for n in WEIGHTS}
    mom = {n: inp["m_" + n][0] for n in WEIGHTS}
    var = {n: inp["v_" + n][0] for n in WEIGHTS}

    full = {}

    def place(n, g):
        if n in ("ffn1_w_in", "ffn2_w_in"):
            full[n] = g.reshape((2, N_DEV // 2) + g.shape[1:])
        elif n in ("ffn1_w_out", "ffn2_w_out"):
            full[n] = g.reshape(N_DEV // 2, 2 * g.shape[1], g.shape[2])
        elif n in COL_SHARDED:
            full[n] = _join_cols(g)
        else:
            full[n] = g.reshape(N_DEV * g.shape[1], g.shape[2])

    w16 = {n: wts[n].astype(BF16) for n in SHARDED}
    for n, g in zip(GATHER_FIRST, _all_gather([w16[n] for n in GATHER_FIRST], "gather_ffn1")):
        place(n, g)
    gather_mix = _exchange_start([w16[n][None] for n in GATHER_MIX], full["ffn1_w_out"], "gather_mix_start")

    def row(v):
        return v.reshape(1, -1)

    n_groups, n_state = wts["ssm_lambda_re"].shape
    n_ch = wts["ssm_b_re"].shape[2]
    disc_in = (wts["ssm_lambda_re"], wts["ssm_lambda_im"], wts["ssm_log_dt"], wts["ssm_b_re"],
               wts["ssm_b_im"])
    (ab_re, ab_im, bb_re, bb_im), disc_vjp = jax.vjp(_discretise, *disc_in)
    a_re, a_im = row(ab_re), row(ab_im)
    b_dense_re = _block_diag_in(bb_re).astype(BF16)
    b_dense_im = _block_diag_in(bb_im).astype(BF16)
    c_dense_re = _block_diag_out(wts["ssm_c_re"]).astype(BF16)
    c_dense_im = _block_diag_out(-wts["ssm_c_im"]).astype(BF16)

    ws = wts["gmlp_w_s"]
    n_heads = ws.shape[0]
    d_gmlp = wts["gmlp_ln_g"].shape[0]
    causal = jnp.tril(jnp.ones((CHUNK, CHUNK), dtype=bool))
    ws_masked = jnp.where(causal[None], ws, 0.0).astype(BF16)
    ws_masked_t = ws_masked.transpose(0, 2, 1)
    bs_cols = jnp.repeat(wts["gmlp_b_s"].T, d_gmlp // n_heads, axis=1)
    d_ssm = n_groups * n_ch
    assert d_ssm == d_gmlp and d_model == 2 * d_ssm

    h1, a1 = _ffn_in(x16, full["ffn1_w_in"], "ffn1_in")
    r1, x1, x1_16 = _ffn_out_ln(a1, full["ffn1_w_out"], x, row(wts["ln1_g"]), row(wts["ln1_b"]), alpha,
                                "ffn1_out_ln")

    for n, g in zip(GATHER_MIX, _exchange_wait(gather_mix, x1_16, "gather_mix_wait")):
        place(n, g)
    gather_last = _exchange_start([w16[n][None] for n in GATHER_LAST], full["mix_w_in"], "gather_ffn2_start")
    proj = _mm(x1_16, full["mix_w_in"], name="mix_in")
    za_p = _perm_rows(proj[:, :d_ssm], n_seq)
    za_p16 = za_p.astype(BF16)
    bu_re = _mm(za_p16, b_dense_re, name="s5_bu_re")
    bu_im = _mm(za_p16, b_dense_im, name="s5_bu_im")
    h_re, h_im = _scan_fwd(bu_re, bu_im, a_re, a_im, n_seq, "s5_scan")
    yssm = _mm(h_re, c_dense_re, name="s5_y_re")
    yssm = _mm(h_im, c_dense_im, name="s5_y_im", add=yssm)
    s5out_p = _s5_post_fwd(yssm, za_p, row(wts["ssm_d"]), full["ssm_glu_w"], row(wts["ssm_glu_b"]),
                           "s5_post")
    s5out = _unperm_rows(s5out_p, n_seq)
    gm = _gmlp_fwd(proj, row(wts["gmlp_ln_g"]), row(wts["gmlp_ln_b"]), ws_masked, bs_cols, "gmlp")
    m_mix, y_a, y_b = _merge_fwd(s5out, gm, proj, full["up_a"], full["up_b"], "merge")
    mixed = _mm(m_mix, full["mix_w_out"], name="mix_out")
    r2, x2, x2_16 = _resid_ln_fwd(x1, mixed, row(wts["ln2_g"]), row(wts["ln2_b"]), alpha, 1.0, "ln2")

    for n, g in zip(GATHER_LAST, _exchange_wait(gather_last, x2_16, "gather_ffn2_wait")):
        place(n, g)
    h2, a2 = _ffn_in(x2_16, full["ffn2_w_in"], "ffn2_in")
    r3, x3, _ = _ffn_out_ln(a2, full["ffn2_w_out"], x2, row(wts["ln3_g"]), row(wts["ln3_b"]), alpha,
                            "ffn2_out_ln")

    small = {}
    send = {}
    loss_parts, dx3, dw_gate, dw_proj = _head(x3, p16, target, full["ple_w_gate"], full["ple_w_proj"], "head")
    loss = lax.psum(jnp.sum(loss_parts[::8, 0]), ("x", "y", "c"))
    send["ple_w_gate"] = dw_gate.astype(BF16).reshape(N_DEV, -1, d_model)
    send["ple_w_proj"] = _split_cols(dw_proj.astype(BF16))

    dr3, dr3_h, dg, db = _ln_bwd_call(r3, dx3, None, 1.0, row(wts["ln3_g"]), 0.5, "ln3_bwd")
    small["ln3_g"], small["ln3_b"] = dg.sum(0), db.sum(0)
    dh2, dw = _ffn_out_bwd(dr3_h, full["ffn2_w_out"], h2, a2, "ffn2_out_bwd")
    send["ffn2_w_out"] = dw.reshape(N_DEV, -1, d_model)
    dw = _ffn_in_dw(x2_16, dh2, "ffn2_in_dw")
    send["ffn2_w_in"] = dw.reshape((N_DEV,) + dw.shape[2:])
    group1 = ("ffn2_w_in", "ffn2_w_out", "ple_w_gate", "ple_w_proj")
    exchange1 = _exchange_start([send[n] for n in group1], dh2, "grad_exchange1_start")
    dx2_f = _ffn_in_dx(dh2, full["ffn2_w_in"], None, 1.0, "ffn2_in_dx")

    dr2, dr2_h, dg, db = _ln_bwd_call(r2, dx2_f, dr3, alpha, row(wts["ln2_g"]), 1.0, "ln2_bwd")
    small["ln2_g"], small["ln2_b"] = dg.sum(0), db.sum(0)
    dm = _mm(dr2_h, full["mix_w_out"], tb=True, name="mix_out_dx")
    send["mix_w_out"] = _mm(m_mix, dr2_h, ta=True, name="mix_out_dw", out_dtype=BF16).reshape(
        N_DEV, -1, d_model)
    dga, dgb, ds5out, dgm, dw_a, dw_b = _merge_bwd(
        dm, y_a, y_b, s5out, gm, proj, full["up_a"], full["up_b"], "merge_bwd")
    send["up_a"] = _split_cols(dw_a.astype(BF16))
    send["up_b"] = _split_cols(dw_b.astype(BF16))

    dzu, dzv, dws, dbs_cols, dg, db = _gmlp_bwd(
        proj, dgm, row(wts["gmlp_ln_g"]), row(wts["gmlp_ln_b"]), ws_masked, ws_masked_t, bs_cols,
        "gmlp_bwd")
    small["gmlp_ln_g"], small["gmlp_ln_b"] = dg.sum(0), db.sum(0)
    small["gmlp_w_s"] = jnp.where(causal[None], dws, 0.0)
    small["gmlp_b_s"] = dbs_cols.reshape(CHUNK, n_heads, -1).sum(-1).T

    ds5out_p = _perm_rows(ds5out, n_seq)
    dy_p, dza_skip, dw_glu, dd, dgb_glu = _s5_post_bwd(
        yssm, za_p, ds5out_p, row(wts["ssm_d"]), full["ssm_glu_w"], row(wts["ssm_glu_b"]),
        "s5_post_bwd")
    send["ssm_glu_w"] = dw_glu.astype(BF16).reshape(N_DEV, -1, d_ssm)
    small["ssm_d"], small["ssm_glu_b"] = dd.sum(0), dgb_glu.sum(0)
    g_re = _mm(dy_p, c_dense_re, tb=True, name="s5_y_re_dx")
    g_im = _mm(dy_p, c_dense_im, tb=True, name="s5_y_im_dx")
    dc_re = _mm(h_re, dy_p, ta=True, name="s5_y_re_dw")
    dc_im = _mm(h_im, dy_p, ta=True, name="s5_y_im_dw")
    small["ssm_c_re"] = _block_diag_out_take(dc_re, n_groups, n_ch, n_state)
    small["ssm_c_im"] = -_block_diag_out_take(dc_im, n_groups, n_ch, n_state)
    lam_re, lam_im, da_re, da_im = _scan_bwd(g_re, g_im, h_re, h_im, a_re, a_im, n_seq, "s5_scan_bwd")
    dza_p = _mm(lam_re, b_dense_re, tb=True, name="s5_bu_re_dx", add=dza_skip)
    dza_p = _mm(lam_im, b_dense_im, tb=True, name="s5_bu_im_dx", add=dza_p, out_dtype=BF16)
    dbb_re = _block_diag_in_take(_mm(za_p16, lam_re, ta=True, name="s5_bu_re_dw"), n_groups, n_state, n_ch)
    dbb_im = _block_diag_in_take(_mm(za_p16, lam_im, ta=True, name="s5_bu_im_dw"), n_groups, n_state, n_ch)
    dab_re = da_re.sum(0).reshape(n_groups, n_state)
    dab_im = da_im.sum(0).reshape(n_groups, n_state)
    (small["ssm_lambda_re"], small["ssm_lambda_im"], small["ssm_log_dt"], small["ssm_b_re"],
     small["ssm_b_im"]) = disc_vjp((dab_re, dab_im, dbb_re, dbb_im))

    dproj = jnp.concatenate([_unperm_rows(dza_p, n_seq), dzu, dzv, dga, dgb], axis=1)
    send["mix_w_in"] = _split_cols(_mm(x1_16, dproj, ta=True, name="mix_in_dw", out_dtype=BF16))
    group2 = ("mix_w_in", "mix_w_out", "up_a", "up_b", "ssm_glu_w")
    exchange2 = _exchange_start([send[n] for n in group2] + [_pack_small(small, SMALL_EARLY)[None]],
                                send["mix_w_in"], "grad_exchange2_start")
    dx1_f = _mm(dproj, full["mix_w_in"], tb=True, name="mix_in_dx")

    dr1, dr1_h, dg, db = _ln_bwd_call(r1, dx1_f, dr2, alpha, row(wts["ln1_g"]), 0.5, "ln1_bwd")
    small["ln1_g"], small["ln1_b"] = dg.sum(0), db.sum(0)
    dh1, dw = _ffn_out_bwd(dr1_h, full["ffn1_w_out"], h1, a1, "ffn1_out_bwd")
    send["ffn1_w_out"] = dw.reshape(N_DEV, -1, d_model)
    exchange3 = _exchange_start([send["ffn1_w_out"], _pack_small(small, SMALL_LATE)[None]], dh1,
                                "grad_exchange3_start")
    dw = _ffn_in_dw(x16, dh1, "ffn1_in_dw")
    send["ffn1_w_in"] = dw.reshape((N_DEV,) + dw.shape[2:])
    exchange4 = _exchange_start([send["ffn1_w_in"]], dh1, "grad_exchange4_start")
    grad_x = _ffn_in_dx(dh1, full["ffn1_w_in"], dr1, alpha, "ffn1_in_dx")

    results = {}

    def update(names, recv, prefix):
        for n, r in zip(names, recv):
            results[n] = _adamw(r, wts[n], mom[n], var[n], prefix + n)

    def update_small(names, recv, name):
        packed = _adamw(recv, _pack_small(wts, names), _pack_small(mom, names), _pack_small(var, names), name)
        shapes = {n: wts[n].shape for n in names}
        unpacked = [_unpack_small(pk, shapes, names) for pk in packed]
        for n in names:
            results[n] = tuple(u[n] for u in unpacked)

    update(group1, _exchange_wait(exchange1, grad_x, "grad_exchange1_wait"), "adamw_")
    recv = _exchange_wait(exchange2, results[group1[-1]][0], "grad_exchange2_wait")
    update(group2, recv[:-1], "adamw_")
    update_small(SMALL_EARLY, recv[-1], "adamw_small")
    recv = _exchange_wait(exchange3, results[SMALL_EARLY[-1]][0], "grad_exchange3_wait")
    update(("ffn1_w_out",), recv[:1], "adamw_")
    update_small(SMALL_LATE, recv[1], "adamw_ln1")
    recv = _exchange_wait(exchange4, results["ln1_b"][0], "grad_exchange4_wait")
    update(("ffn1_w_in",), recv, "adamw_")

    outs = [loss, grad_x.reshape(n_seq, seq, d_model)]
    for k in range(4):
        outs += [results[n][k][None] for n in WEIGHTS]
    return tuple(outs)
```

```python
import functools
import math

import jax
import jax.numpy as jnp
from jax import lax
from jax.experimental import pallas as pl
from jax.experimental.pallas import tpu as pltpu

F32 = jnp.float32
BF16 = jnp.bfloat16

N_DEV = 8
LN_EPS = 1e-5
CHUNK = 128
N_SEG = 8
PACK_W = 1024
VMEM_LIMIT_BYTES = 56 * 1024 * 1024
MM_OPERAND_BLOCK_BYTES = 8 * 1024 * 1024
ADAM_BLOCK_BYTES = 6 * 1024 * 1024

ADAM_LR = 0.001
ADAM_B1 = 0.9
ADAM_B2 = 0.999
ADAM_EPS = 1e-08
ADAM_WD = 0.01
ADAM_STEP = 10

COL_SHARDED = ("ffn1_w_in", "mix_w_in", "up_a", "up_b", "ffn2_w_in", "ple_w_proj")
ROW_SHARDED = ("ffn1_w_out", "ssm_glu_w", "mix_w_out", "ffn2_w_out", "ple_w_gate")
SHARDED = ("ffn1_w_in", "ffn1_w_out", "mix_w_in", "ssm_glu_w", "up_a", "up_b", "mix_w_out",
           "ffn2_w_in", "ffn2_w_out", "ple_w_proj", "ple_w_gate")
WEIGHTS = ("ffn1_w_in", "ffn1_w_out", "ln1_g", "ln1_b", "mix_w_in", "ssm_lambda_re", "ssm_lambda_im",
           "ssm_log_dt", "ssm_b_re", "ssm_b_im", "ssm_c_re", "ssm_c_im", "ssm_d", "ssm_glu_w",
           "ssm_glu_b", "gmlp_ln_g", "gmlp_ln_b", "gmlp_w_s", "gmlp_b_s", "up_a", "up_b", "mix_w_out",
           "ln2_g", "ln2_b", "ffn2_w_in", "ffn2_w_out", "ln3_g", "ln3_b", "ple_w_proj", "ple_w_gate")
GATHER_FIRST = ("ffn1_w_in", "ffn1_w_out")
GATHER_MIX = ("mix_w_in", "ssm_glu_w", "up_a", "up_b", "mix_w_out")
GATHER_LAST = ("ffn2_w_in", "ffn2_w_out", "ple_w_proj", "ple_w_gate")
SMALL = tuple(n for n in WEIGHTS if n not in SHARDED)
SMALL_LATE = ("ln1_g", "ln1_b")
SMALL_EARLY = tuple(n for n in SMALL if n not in SMALL_LATE)
INPUT_NAMES = ("x", "p") + WEIGHTS + ("loss_target",) + tuple("m_" + n for n in WEIGHTS) + tuple(
    "v_" + n for n in WEIGHTS)


def _pick(dim, pref, mult=128):
    if dim <= pref:
        return dim
    d = (pref // mult) * mult
    while d >= mult:
        if dim % d == 0:
            return d
        d -= mult
    return dim


def _params(n_axes=1):
    return pltpu.CompilerParams(dimension_semantics=("arbitrary",) * n_axes,
                                vmem_limit_bytes=VMEM_LIMIT_BYTES)


def _sigmoid(v):
    return 1.0 / (1.0 + jnp.exp(-v))


_GELU_C = math.sqrt(2.0 / math.pi)


def _gelu(v):
    return 0.5 * v * (1.0 + jnp.tanh(_GELU_C * (v + 0.044715 * (v * v * v))))


def _gelu_grad(v):
    t = jnp.tanh(_GELU_C * (v + 0.044715 * (v * v * v)))
    return 0.5 * (1.0 + t) + 0.5 * v * (1.0 - t * t) * (_GELU_C * (1.0 + 3.0 * 0.044715 * v * v))


def _ln_stats(r):
    mu = jnp.mean(r, axis=-1, keepdims=True)
    xc = r - mu
    var = jnp.mean(xc * xc, axis=-1, keepdims=True)
    rstd = lax.rsqrt(var + LN_EPS)
    return xc * rstd, rstd


def _ln_bwd(dy, xhat, rstd, g):
    dxh = dy * g
    m1 = jnp.mean(dxh, axis=-1, keepdims=True)
    m2 = jnp.mean(dxh * xhat, axis=-1, keepdims=True)
    return rstd * (dxh - m1 - xhat * m2)


def _fold8(v):
    rows, w = v.shape
    return jnp.sum(v.reshape(rows // 8, 8, w), axis=0)


def _dot(a, b, ta=False, tb=False):
    dn = (((0 if ta else 1,), (1 if tb else 0,)), ((), ()))
    return lax.dot_general(a.astype(BF16), b.astype(BF16), dn, preferred_element_type=F32)


def _mm(a, b, *, name, ta=False, tb=False, out_dtype=F32, add=None, add_scale=1.0,
        tm=2048, tn=512, tk=4096):
    m_dim = a.shape[1] if ta else a.shape[0]
    k_dim = a.shape[0] if ta else a.shape[1]
    n_dim = b.shape[0] if tb else b.shape[1]
    assert (b.shape[1] if tb else b.shape[0]) == k_dim, (name, a.shape, b.shape)
    tk = _pick(k_dim, tk)
    tm = min(tm, max(256, MM_OPERAND_BLOCK_BYTES // (tk * a.dtype.itemsize)))
    tm, tn = _pick(m_dim, tm), _pick(n_dim, tn)
    nk = k_dim // tk
    has_add = add is not None

    def finish(r, add_ref, o_ref):
        if has_add:
            r = r + add_scale * add_ref[...].astype(F32)
        o_ref[...] = r.astype(out_dtype)

    def body(*refs):
        a_ref, b_ref = refs[:2]
        add_ref = refs[2] if has_add else None
        o_ref = refs[3] if has_add else refs[2]
        if nk == 1:
            finish(_dot(a_ref[...], b_ref[...], ta, tb), add_ref, o_ref)
            return
        acc_ref = refs[-1]
        k = pl.program_id(2)

        @pl.when(k == 0)
        def _():
            acc_ref[...] = jnp.zeros_like(acc_ref)

        acc_ref[...] += _dot(a_ref[...], b_ref[...], ta, tb)

        @pl.when(k == nk - 1)
        def _():
            finish(acc_ref[...], add_ref, o_ref)

    a_spec = (pl.BlockSpec((tk, tm), lambda i, j, k: (k, i)) if ta
              else pl.BlockSpec((tm, tk), lambda i, j, k: (i, k)))
    b_spec = (pl.BlockSpec((tn, tk), lambda i, j, k: (j, k)) if tb
              else pl.BlockSpec((tk, tn), lambda i, j, k: (k, j)))
    in_specs = [a_spec, b_spec]
    operands = [a, b]
    if has_add:
        in_specs.append(pl.BlockSpec((tm, tn), lambda i, j, k: (i, j)))
        operands.append(add)
    return pl.pallas_call(
        body, name=name,
        out_shape=jax.ShapeDtypeStruct((m_dim, n_dim), out_dtype),
        grid=(m_dim // tm, n_dim // tn, nk),
        in_specs=in_specs,
        out_specs=pl.BlockSpec((tm, tn), lambda i, j, k: (i, j)),
        scratch_shapes=[pltpu.VMEM((tm, tn), F32)] if nk > 1 else [],
        compiler_params=_params(3),
    )(*operands)


def _rows(tr, w, cb=0):
    return pl.BlockSpec((tr, w), lambda i: (i, cb))


def _whole(shape):
    nd = len(shape)
    return pl.BlockSpec(tuple(shape), lambda i: (0,) * nd)


def _ffn_in(x16, w_in, name):
    t, d = x16.shape
    _, nb, _, fb = w_in.shape
    tm = _pick(t, 1024, 8)

    def body(x_ref, wg_ref, wu_ref, h_ref, a_ref):
        xv = x_ref[...]
        g = _dot(xv, wg_ref[0, 0])
        u = _dot(xv, wu_ref[0, 0])
        h_ref[0, 0] = g
        h_ref[0, 1] = u
        a_ref[0] = (g * _sigmoid(g) * u).astype(BF16)

    return pl.pallas_call(
        body, name=name,
        out_shape=(jax.ShapeDtypeStruct((nb, 2, t, fb), F32), jax.ShapeDtypeStruct((nb, t, fb), BF16)),
        grid=(t // tm, nb),
        in_specs=[pl.BlockSpec((tm, d), lambda i, j: (i, 0)),
                  pl.BlockSpec((1, 1, d, fb), lambda i, j: (0, j, 0, 0)),
                  pl.BlockSpec((1, 1, d, fb), lambda i, j: (1, j, 0, 0))],
        out_specs=(pl.BlockSpec((1, 2, tm, fb), lambda i, j: (j, 0, i, 0)),
                   pl.BlockSpec((1, tm, fb), lambda i, j: (j, i, 0))),
        compiler_params=_params(2))(x16, w_in, w_in)


def _ffn_out_ln(a, w_out, xin, g, b, alpha, name):
    nb, t, fb = a.shape
    d = w_out.shape[2]
    tm = _pick(t, 256, 8)

    def body(a_ref, w_ref, x_ref, g_ref, b_ref, r_ref, y_ref, y16_ref):
        f = _dot(a_ref[0], w_ref[0])
        for jb in range(1, nb):
            f = f + _dot(a_ref[jb], w_ref[jb])
        r = alpha * x_ref[...] + 0.5 * f
        xhat, _ = _ln_stats(r)
        y = xhat * g_ref[...] + b_ref[...]
        r_ref[...] = r
        y_ref[...] = y
        y16_ref[...] = y.astype(BF16)

    return pl.pallas_call(
        body, name=name,
        out_shape=(jax.ShapeDtypeStruct((t, d), F32), jax.ShapeDtypeStruct((t, d), F32),
                   jax.ShapeDtypeStruct((t, d), BF16)),
        grid=(t // tm,),
        in_specs=[pl.BlockSpec((nb, tm, fb), lambda i: (0, i, 0)), _whole(w_out.shape), _rows(tm, d),
                  _whole((1, d)), _whole((1, d))],
        out_specs=(_rows(tm, d), _rows(tm, d), _rows(tm, d)),
        compiler_params=_params())(a, w_out, xin, g, b)


def _ffn_out_bwd(drs, w_out, h, a, name):
    nb, t, fb = a.shape
    d = w_out.shape[2]
    tm = _pick(t, 1024, 8)
    n_i = t // tm

    def body(dr_ref, w_ref, h_ref, a_ref, dh_ref, dw_ref, acc_ref):
        i = pl.program_id(1)
        dr = dr_ref[...]
        da = _dot(dr, w_ref[0], tb=True)
        g, u = h_ref[0, 0], h_ref[0, 1]
        sg = _sigmoid(g)
        dh_ref[0, 0] = (da * u * (sg * (1.0 + g * (1.0 - sg)))).astype(BF16)
        dh_ref[0, 1] = (da * (g * sg)).astype(BF16)

        @pl.when(i == 0)
        def _():
            acc_ref[...] = jnp.zeros_like(acc_ref)

        acc_ref[...] += _dot(a_ref[0], dr, ta=True)

        @pl.when(i == n_i - 1)
        def _():
            dw_ref[0] = acc_ref[...].astype(BF16)

    return pl.pallas_call(
        body, name=name,
        out_shape=(jax.ShapeDtypeStruct((nb, 2, t, fb), BF16), jax.ShapeDtypeStruct((nb, fb, d), BF16)),
        grid=(nb, n_i),
        in_specs=[pl.BlockSpec((tm, d), lambda j, i: (i, 0)),
                  pl.BlockSpec((1, fb, d), lambda j, i: (j, 0, 0)),
                  pl.BlockSpec((1, 2, tm, fb), lambda j, i: (j, 0, i, 0)),
                  pl.BlockSpec((1, tm, fb), lambda j, i: (j, i, 0))],
        out_specs=(pl.BlockSpec((1, 2, tm, fb), lambda j, i: (j, 0, i, 0)),
                   pl.BlockSpec((1, fb, d), lambda j, i: (j, 0, 0))),
        scratch_shapes=[pltpu.VMEM((fb, d), F32)],
        compiler_params=_params(2))(drs, w_out, h, a)


def _ffn_in_dw(x16, dh, name):
    t, d = x16.shape
    nb, _, _, fb = dh.shape

    def body(x_ref, dh_ref, o_ref):
        o_ref[0, 0] = _dot(x_ref[...], dh_ref[0, 0], ta=True).astype(BF16)

    return pl.pallas_call(
        body, name=name, out_shape=jax.ShapeDtypeStruct((2, nb, d, fb), BF16), grid=(nb, 2),
        in_specs=[pl.BlockSpec((t, d), lambda j, s: (0, 0)),
                  pl.BlockSpec((1, 1, t, fb), lambda j, s: (j, s, 0, 0))],
        out_specs=pl.BlockSpec((1, 1, d, fb), lambda j, s: (s, j, 0, 0)),
        compiler_params=_params(2))(x16, dh)


def _ffn_in_dx(dh, w_in, add, add_scale, name):
    nb, _, t, fb = dh.shape
    d = w_in.shape[2]
    tm = _pick(t, 512, 8)
    has_add = add is not None

    def body(*refs):
        dh_ref, w_ref = refs[:2]
        o_ref = refs[-1]
        acc = None
        for jb in range(nb):
            for s in range(2):
                part = _dot(dh_ref[jb, s], w_ref[s, jb], tb=True)
                acc = part if acc is None else acc + part
        if has_add:
            acc = acc + add_scale * refs[2][...]
        o_ref[...] = acc

    return pl.pallas_call(
        body, name=name, out_shape=jax.ShapeDtypeStruct((t, d), F32), grid=(t // tm,),
        in_specs=[pl.BlockSpec((nb, 2, tm, fb), lambda i: (0, 0, i, 0)), _whole(w_in.shape)]
        + ([_rows(tm, d)] if has_add else []),
        out_specs=_rows(tm, d),
        compiler_params=_params())(*([dh, w_in] + ([add] if has_add else [])))


def _resid_ln_fwd(xin, f, g, b, alpha, scale, name):
    t, d = xin.shape
    tr = _pick(t, 256, 8)

    def body(x_ref, f_ref, g_ref, b_ref, r_ref, y_ref, y16_ref):
        r = alpha * x_ref[...] + scale * f_ref[...]
        xhat, _ = _ln_stats(r)
        y = xhat * g_ref[...] + b_ref[...]
        r_ref[...] = r
        y_ref[...] = y
        y16_ref[...] = y.astype(BF16)

    return pl.pallas_call(
        body, name=name,
        out_shape=(jax.ShapeDtypeStruct((t, d), F32), jax.ShapeDtypeStruct((t, d), F32),
                   jax.ShapeDtypeStruct((t, d), BF16)),
        grid=(t // tr,),
        in_specs=[_rows(tr, d), _rows(tr, d), _whole((1, d)), _whole((1, d))],
        out_specs=(_rows(tr, d), _rows(tr, d), _rows(tr, d)),
        compiler_params=_params())(xin, f, g, b)


def _ln_bwd_call(r, dy_a, dy_b, b_scale, g, out_scale, name):
    t, d = r.shape
    tr = _pick(t, 256, 8)
    has_b = dy_b is not None

    def body(*refs):
        if has_b:
            r_ref, da_ref, db_ref, g_ref, dr_ref, drs_ref, dg_ref, dbeta_ref = refs
            dy = da_ref[...] + b_scale * db_ref[...]
        else:
            r_ref, da_ref, g_ref, dr_ref, drs_ref, dg_ref, dbeta_ref = refs
            dy = da_ref[...]
        xhat, rstd = _ln_stats(r_ref[...])
        dr = _ln_bwd(dy, xhat, rstd, g_ref[...])
        dr_ref[...] = dr
        drs_ref[...] = (out_scale * dr).astype(BF16)

        @pl.when(pl.program_id(0) == 0)
        def _():
            dg_ref[...] = jnp.zeros_like(dg_ref)
            dbeta_ref[...] = jnp.zeros_like(dbeta_ref)

        dg_ref[...] += _fold8(dy * xhat)
        dbeta_ref[...] += _fold8(dy)

    ins = [r, dy_a] + ([dy_b] if has_b else []) + [g]
    in_specs = [_rows(tr, d)] * (3 if has_b else 2) + [_whole((1, d))]
    return pl.pallas_call(
        body, name=name,
        out_shape=(jax.ShapeDtypeStruct((t, d), F32), jax.ShapeDtypeStruct((t, d), BF16),
                   jax.ShapeDtypeStruct((8, d), F32), jax.ShapeDtypeStruct((8, d), F32)),
        grid=(t // tr,), in_specs=in_specs,
        out_specs=(_rows(tr, d), _rows(tr, d), _whole((8, d)), _whole((8, d))),
        compiler_params=_params())(*ins)


def _take_row(v, row_id, s):
    return jnp.sum(jnp.where(row_id == s, v, 0.0), axis=0, keepdims=True)


def _seg_carries(f_re, f_im, p_re, p_im, reverse):
    row_id = lax.broadcasted_iota(jnp.int32, f_re.shape, 0)
    p_re, p_im = _take_row(p_re, row_id, 0), _take_row(p_im, row_id, 0)
    out_re, out_im = jnp.zeros_like(f_re), jnp.zeros_like(f_im)
    c_re, c_im = jnp.zeros_like(p_re), jnp.zeros_like(p_im)
    order = range(N_SEG - 1, -1, -1) if reverse else range(N_SEG)
    for s in order:
        out_re = jnp.where(row_id == s, c_re, out_re)
        out_im = jnp.where(row_id == s, c_im, out_im)
        fr, fi = _take_row(f_re, row_id, s), _take_row(f_im, row_id, s)
        c_re, c_im = fr + p_re * c_re - p_im * c_im, fi + p_re * c_im + p_im * c_re
    return out_re, out_im


def _scan_fwd(bu_re, bu_im, a_re, a_im, n_seq, name):
    t, n = bu_re.shape
    seq = t // n_seq
    steps = seq // N_SEG
    cb = _pick(n, 256)

    def body(bre, bim, are, aim, hre, him):
        ar = jnp.broadcast_to(are[...], (N_SEG, cb))
        ai = jnp.broadcast_to(aim[...], (N_SEG, cb))
        zero = jnp.zeros((N_SEG, cb), F32)

        def local(k, carry):
            lr, li, pr, pi = carry
            rows = pl.ds(pl.multiple_of(k * N_SEG, N_SEG), N_SEG)
            nr = ar * lr - ai * li + bre[rows, :]
            ni = ar * li + ai * lr + bim[rows, :]
            hre[rows, :] = nr
            him[rows, :] = ni
            return nr, ni, ar * pr - ai * pi, ar * pi + ai * pr

        f_re, f_im, p_re, p_im = lax.fori_loop(0, steps, local, (zero, zero, zero + 1.0, zero))
        c_re, c_im = _seg_carries(f_re, f_im, p_re, p_im, reverse=False)

        def fix(k, carry):
            pr, pi = carry
            rows = pl.ds(pl.multiple_of(k * N_SEG, N_SEG), N_SEG)
            hre[rows, :] = hre[rows, :] + (pr * c_re - pi * c_im)
            him[rows, :] = him[rows, :] + (pr * c_im + pi * c_re)
            return ar * pr - ai * pi, ar * pi + ai * pr

        lax.fori_loop(0, steps, fix, (ar, ai))

    blk = pl.BlockSpec((seq, cb), lambda s, j: (s, j))
    vec = pl.BlockSpec((1, cb), lambda s, j: (0, j))
    return pl.pallas_call(
        body, name=name,
        out_shape=(jax.ShapeDtypeStruct((t, n), F32), jax.ShapeDtypeStruct((t, n), F32)),
        grid=(n_seq, n // cb), in_specs=[blk, blk, vec, vec], out_specs=(blk, blk),
        compiler_params=_params(2))(bu_re, bu_im, a_re, a_im)


def _scan_bwd(g_re, g_im, h_re, h_im, a_re, a_im, n_seq, name):
    t, n = g_re.shape
    seq = t // n_seq
    steps = seq // N_SEG
    cb = _pick(n, 256)

    def body(gre, gim, hre, him, are, aim, lre, lim, dare, daim):
        ar = jnp.broadcast_to(are[...], (N_SEG, cb))
        ai = -jnp.broadcast_to(aim[...], (N_SEG, cb))
        zero = jnp.zeros((N_SEG, cb), F32)

        def local(i, carry):
            lr, li, pr, pi = carry
            k = steps - 1 - i
            rows = pl.ds(pl.multiple_of(k * N_SEG, N_SEG), N_SEG)
            nr = ar * lr - ai * li + gre[rows, :]
            ni = ar * li + ai * lr + gim[rows, :]
            lre[rows, :] = nr
            lim[rows, :] = ni
            return nr, ni, ar * pr - ai * pi, ar * pi + ai * pr

        f_re, f_im, p_re, p_im = lax.fori_loop(0, steps, local, (zero, zero, zero + 1.0, zero))
        c_re, c_im = _seg_carries(f_re, f_im, p_re, p_im, reverse=True)

        def accumulate(lam_r, lam_i, hp_r, hp_i, acc_r, acc_i):
            return acc_r + (lam_r * hp_r + lam_i * hp_i), acc_i + (lam_i * hp_r - lam_r * hp_i)

        def fix(i, carry):
            pr, pi, acc_r, acc_i = carry
            k = steps - 1 - i
            rows = pl.ds(pl.multiple_of(k * N_SEG, N_SEG), N_SEG)
            prev = pl.ds(pl.multiple_of((k - 1) * N_SEG, N_SEG), N_SEG)
            lam_r = lre[rows, :] + (pr * c_re - pi * c_im)
            lam_i = lim[rows, :] + (pr * c_im + pi * c_re)
            lre[rows, :] = lam_r
            lim[rows, :] = lam_i
            acc_r, acc_i = accumulate(lam_r, lam_i, hre[prev, :], him[prev, :], acc_r, acc_i)
            return ar * pr - ai * pi, ar * pi + ai * pr, acc_r, acc_i

        pr, pi, acc_r, acc_i = lax.fori_loop(0, steps - 1, fix, (ar, ai, zero, zero))
        first = pl.ds(0, N_SEG)
        last = pl.ds((steps - 1) * N_SEG, N_SEG)
        lam_r = lre[first, :] + (pr * c_re - pi * c_im)
        lam_i = lim[first, :] + (pr * c_im + pi * c_re)
        lre[first, :] = lam_r
        lim[first, :] = lam_i
        row_id = lax.broadcasted_iota(jnp.int32, (N_SEG, cb), 0)
        hp_r = jnp.where(row_id == 0, 0.0, pltpu.roll(hre[last, :], 1, 0))
        hp_i = jnp.where(row_id == 0, 0.0, pltpu.roll(him[last, :], 1, 0))
        acc_r, acc_i = accumulate(lam_r, lam_i, hp_r, hp_i, acc_r, acc_i)
        dare[...] = acc_r
        daim[...] = acc_i

    blk = pl.BlockSpec((seq, cb), lambda s, j: (s, j))
    vec = pl.BlockSpec((1, cb), lambda s, j: (0, j))
    acc = pl.BlockSpec((N_SEG, cb), lambda s, j: (s, j))
    return pl.pallas_call(
        body, name=name,
        out_shape=(jax.ShapeDtypeStruct((t, n), F32), jax.ShapeDtypeStruct((t, n), F32),
                   jax.ShapeDtypeStruct((n_seq * N_SEG, n), F32),
                   jax.ShapeDtypeStruct((n_seq * N_SEG, n), F32)),
        grid=(n_seq, n // cb), in_specs=[blk, blk, blk, blk, vec, vec],
        out_specs=(blk, blk, acc, acc),
        compiler_params=_params(2))(g_re, g_im, h_re, h_im, a_re, a_im)


def _s5_post_fwd(yssm, u, d_skip, glu_w, glu_b, name):
    t, w = yssm.shape
    tr = _pick(t, 512, 8)

    def body(y_ref, u_ref, d_ref, w_ref, b_ref, o_ref):
        yg = _gelu(y_ref[...] + d_ref[...] * u_ref[...])
        pre = _dot(yg, w_ref[...]) + b_ref[...]
        o_ref[...] = yg * _sigmoid(pre)

    return pl.pallas_call(
        body, name=name, out_shape=jax.ShapeDtypeStruct((t, w), F32), grid=(t // tr,),
        in_specs=[_rows(tr, w), _rows(tr, w), _whole((1, w)), _whole((w, w)), _whole((1, w))],
        out_specs=_rows(tr, w), compiler_params=_params())(yssm, u, d_skip, glu_w, glu_b)


def _s5_post_bwd(yssm, u, dout, d_skip, glu_w, glu_b, name):
    t, w = yssm.shape
    tr = _pick(t, 512, 8)

    def body(y_ref, u_ref, do_ref, d_ref, w_ref, b_ref, dy_ref, du_ref, dw_ref, dd_ref, db_ref):
        uu = u_ref[...]
        y = y_ref[...] + d_ref[...] * uu
        yg = _gelu(y)
        sg = _sigmoid(_dot(yg, w_ref[...]) + b_ref[...])
        do = do_ref[...]
        dpre = do * yg * sg * (1.0 - sg)
        dyg = do * sg + _dot(dpre, w_ref[...], tb=True)
        dy = dyg * _gelu_grad(y)
        dy_ref[...] = dy.astype(BF16)
        du_ref[...] = dy * d_ref[...]

        @pl.when(pl.program_id(0) == 0)
        def _():
            dw_ref[...] = jnp.zeros_like(dw_ref)
            dd_ref[...] = jnp.zeros_like(dd_ref)
            db_ref[...] = jnp.zeros_like(db_ref)

        dw_ref[...] += _dot(yg, dpre, ta=True)
        dd_ref[...] += _fold8(dy * uu)
        db_ref[...] += _fold8(dpre)

    return pl.pallas_call(
        body, name=name,
        out_shape=(jax.ShapeDtypeStruct((t, w), BF16), jax.ShapeDtypeStruct((t, w), F32),
                   jax.ShapeDtypeStruct((w, w), F32), jax.ShapeDtypeStruct((8, w), F32),
                   jax.ShapeDtypeStruct((8, w), F32)),
        grid=(t // tr,),
        in_specs=[_rows(tr, w), _rows(tr, w), _rows(tr, w), _whole((1, w)), _whole((w, w)),
                  _whole((1, w))],
        out_specs=(_rows(tr, w), _rows(tr, w), _whole((w, w)), _whole((8, w)), _whole((8, w))),
        compiler_params=_params())(yssm, u, dout, d_skip, glu_w, glu_b)


def _head_select(parts, head_dim):
    col_head = lax.broadcasted_iota(jnp.int32, parts[0].shape, 1) // head_dim
    out = jnp.zeros_like(parts[0])
    for h, part in enumerate(parts):
        out = jnp.where(col_head == h, part, out)
    return out


def _gmlp_fwd(proj, ln_g, ln_b, ws, bs_cols, name):
    t = proj.shape[0]
    n_heads = ws.shape[0]
    w = bs_cols.shape[1]
    head_dim = w // n_heads
    cpb = _pick(t // CHUNK, 4, 1)
    tr = cpb * CHUNK

    def body(zu_ref, zv_ref, g_ref, b_ref, ws_ref, bs_ref, o_ref):
        for c in range(cpb):
            rows = pl.ds(c * CHUNK, CHUNK)
            xhat, _ = _ln_stats(_gelu(zv_ref[rows, :]))
            vn = (xhat * g_ref[...] + b_ref[...]).astype(BF16)
            s = _head_select([_dot(ws_ref[h], vn) for h in range(n_heads)], head_dim) + bs_ref[...]
            o_ref[rows, :] = _gelu(zu_ref[rows, :]) * s

    return pl.pallas_call(
        body, name=name, out_shape=jax.ShapeDtypeStruct((t, w), F32), grid=(t // tr,),
        in_specs=[_rows(tr, w, 1), _rows(tr, w, 2), _whole((1, w)), _whole((1, w)),
                  _whole(ws.shape), _whole(bs_cols.shape)],
        out_specs=_rows(tr, w), compiler_params=_params())(proj, proj, ln_g, ln_b, ws, bs_cols)


def _gmlp_bwd(proj, dout, ln_g, ln_b, ws, ws_t, bs_cols, name):
    t = proj.shape[0]
    n_heads = ws.shape[0]
    w = bs_cols.shape[1]
    head_dim = w // n_heads
    cpb = _pick(t // CHUNK, 4, 1)
    tr = cpb * CHUNK

    def body(zu_ref, zv_ref, do_ref, g_ref, b_ref, ws_ref, wst_ref, bs_ref,
             dzu_ref, dzv_ref, dws_ref, dbs_ref, dg_ref, dbeta_ref):
        @pl.when(pl.program_id(0) == 0)
        def _():
            dws_ref[...] = jnp.zeros_like(dws_ref)
            dbs_ref[...] = jnp.zeros_like(dbs_ref)
            dg_ref[...] = jnp.zeros_like(dg_ref)
            dbeta_ref[...] = jnp.zeros_like(dbeta_ref)

        col_head = lax.broadcasted_iota(jnp.int32, (CHUNK, w), 1) // head_dim
        for c in range(cpb):
            rows = pl.ds(c * CHUNK, CHUNK)
            zu, zv, do = zu_ref[rows, :], zv_ref[rows, :], do_ref[rows, :]
            xhat, rstd = _ln_stats(_gelu(zv))
            vn = (xhat * g_ref[...] + b_ref[...]).astype(BF16)
            s = _head_select([_dot(ws_ref[h], vn) for h in range(n_heads)], head_dim) + bs_ref[...]
            dzu_ref[rows, :] = (do * s * _gelu_grad(zu)).astype(BF16)
            ds = do * _gelu(zu)
            ds16 = ds.astype(BF16)
            dbs_ref[...] += ds
            for h in range(n_heads):
                dws_ref[h] += _dot(jnp.where(col_head == h, ds16, jnp.zeros_like(ds16)), vn, tb=True)
            dvn = _head_select([_dot(wst_ref[h], ds16) for h in range(n_heads)], head_dim)
            dg_ref[...] += _fold8(dvn * xhat)
            dbeta_ref[...] += _fold8(dvn)
            dgv = _ln_bwd(dvn, xhat, rstd, g_ref[...])
            dzv_ref[rows, :] = (dgv * _gelu_grad(zv)).astype(BF16)

    return pl.pallas_call(
        body, name=name,
        out_shape=(jax.ShapeDtypeStruct((t, w), BF16), jax.ShapeDtypeStruct((t, w), BF16),
                   jax.ShapeDtypeStruct(ws.shape, F32), jax.ShapeDtypeStruct((CHUNK, w), F32),
                   jax.ShapeDtypeStruct((8, w), F32), jax.ShapeDtypeStruct((8, w), F32)),
        grid=(t // tr,),
        in_specs=[_rows(tr, w, 1), _rows(tr, w, 2), _rows(tr, w), _whole((1, w)), _whole((1, w)),
                  _whole(ws.shape), _whole(ws.shape), _whole(bs_cols.shape)],
        out_specs=(_rows(tr, w), _rows(tr, w), _whole(ws.shape), _whole((CHUNK, w)),
                   _whole((8, w)), _whole((8, w))),
        compiler_params=_params())(proj, proj, dout, ln_g, ln_b, ws, ws_t, bs_cols)


def _merge_fwd(s5out, gm, proj, up_a, up_b, name):
    t, w = s5out.shape
    d = up_a.shape[1]
    assert d == 2 * w
    tr = _pick(t, 256, 8)

    def body(s_ref, gm_ref, ga0, ga1, gb0, gb1, ua_ref, ub_ref, m_ref, ya_ref, yb_ref):
        ya = _dot(s_ref[...], ua_ref[...])
        yb = _dot(gm_ref[...], ub_ref[...])
        ya_ref[...] = ya
        yb_ref[...] = yb
        for half, (ga, gb) in enumerate(((ga0, gb0), (ga1, gb1))):
            cols = slice(half * w, (half + 1) * w)
            m_ref[:, cols] = (_sigmoid(ga[...]) * ya[:, cols] + _sigmoid(gb[...]) * yb[:, cols]).astype(BF16)

    return pl.pallas_call(
        body, name=name,
        out_shape=(jax.ShapeDtypeStruct((t, d), BF16), jax.ShapeDtypeStruct((t, d), F32),
                   jax.ShapeDtypeStruct((t, d), F32)),
        grid=(t // tr,),
        in_specs=[_rows(tr, w), _rows(tr, w), _rows(tr, w, 3), _rows(tr, w, 4), _rows(tr, w, 5),
                  _rows(tr, w, 6), _whole(up_a.shape), _whole(up_b.shape)],
        out_specs=(_rows(tr, d), _rows(tr, d), _rows(tr, d)),
        compiler_params=_params())(s5out, gm, proj, proj, proj, proj, up_a, up_b)


def _merge_bwd(dm, ya, yb, s5out, gm, proj, up_a, up_b, name):
    t, d = dm.shape
    w = d // 2
    tr = _pick(t, 256, 8)

    def body(dm_ref, ya_ref, yb_ref, s_ref, gm_ref, ga0, ga1, gb0, gb1, ua_ref, ub_ref,
             dga_ref, dgb_ref, ds_ref, dgm_ref, dua_ref, dub_ref):
        dm_v, ya, yb = dm_ref[...], ya_ref[...], yb_ref[...]
        dya, dyb = [], []
        for half, (ga, gb) in enumerate(((ga0, gb0), (ga1, gb1))):
            cols = slice(half * w, (half + 1) * w)
            sa, sb = _sigmoid(ga[...]), _sigmoid(gb[...])
            dmh = dm_v[:, cols]
            dga_ref[:, cols] = (dmh * ya[:, cols] * sa * (1.0 - sa)).astype(BF16)
            dgb_ref[:, cols] = (dmh * yb[:, cols] * sb * (1.0 - sb)).astype(BF16)
            dya.append((dmh * sa).astype(BF16))
            dyb.append((dmh * sb).astype(BF16))
        dya = jnp.concatenate(dya, axis=1)
        dyb = jnp.concatenate(dyb, axis=1)
        ds_ref[...] = _dot(dya, ua_ref[...], tb=True)
        dgm_ref[...] = _dot(dyb, ub_ref[...], tb=True)

        @pl.when(pl.program_id(0) == 0)
        def _():
            dua_ref[...] = jnp.zeros_like(dua_ref)
            dub_ref[...] = jnp.zeros_like(dub_ref)

        dua_ref[...] += _dot(s_ref[...], dya, ta=True)
        dub_ref[...] += _dot(gm_ref[...], dyb, ta=True)

    return pl.pallas_call(
        body, name=name,
        out_shape=(jax.ShapeDtypeStruct((t, d), BF16), jax.ShapeDtypeStruct((t, d), BF16),
                   jax.ShapeDtypeStruct((t, w), F32), jax.ShapeDtypeStruct((t, w), F32),
                   jax.ShapeDtypeStruct(up_a.shape, F32), jax.ShapeDtypeStruct(up_b.shape, F32)),
        grid=(t // tr,),
        in_specs=[_rows(tr, d), _rows(tr, d), _rows(tr, d), _rows(tr, w), _rows(tr, w),
                  _rows(tr, w, 3), _rows(tr, w, 4), _rows(tr, w, 5), _rows(tr, w, 6),
                  _whole(up_a.shape), _whole(up_b.shape)],
        out_specs=(_rows(tr, d), _rows(tr, d), _rows(tr, w), _rows(tr, w), _whole(up_a.shape),
                   _whole(up_b.shape)),
        compiler_params=_params())(dm, ya, yb, s5out, gm, proj, proj, proj, proj, up_a, up_b)


def _head(x3, p, target, w_gate, w_proj, name):
    t, d = x3.shape
    pd = p.shape[1]
    tr = _pick(t, 256, 8)
    nb = t // tr

    def body(x_ref, p_ref, t_ref, wg_ref, wp_ref, loss_ref, dx_ref, dwg_ref, dwp_ref):
        xv = x_ref[...]
        sg = _sigmoid(_dot(xv, wg_ref[...]))
        pp = _dot(p_ref[...], wp_ref[...])
        err = xv + sg * pp - t_ref[...]
        loss_ref[...] = jnp.full((8, 128), 0.5 * jnp.sum(jnp.mean(err * err, axis=-1)), F32)
        dout = err * (1.0 / d)
        dgpre = dout * pp * sg * (1.0 - sg)
        dpp = dout * sg
        dx_ref[...] = dout + _dot(dgpre, wg_ref[...], tb=True)

        @pl.when(pl.program_id(0) == 0)
        def _():
            dwg_ref[...] = jnp.zeros_like(dwg_ref)
            dwp_ref[...] = jnp.zeros_like(dwp_ref)

        dwg_ref[...] += _dot(xv, dgpre, ta=True)
        dwp_ref[...] += _dot(p_ref[...], dpp, ta=True)

    return pl.pallas_call(
        body, name=name,
        out_shape=(jax.ShapeDtypeStruct((nb * 8, 128), F32), jax.ShapeDtypeStruct((t, d), F32),
                   jax.ShapeDtypeStruct((d, d), F32), jax.ShapeDtypeStruct((pd, d), F32)),
        grid=(nb,),
        in_specs=[_rows(tr, d), _rows(tr, pd), _rows(tr, d), _whole((d, d)), _whole((pd, d))],
        out_specs=(pl.BlockSpec((8, 128), lambda i: (i, 0)), _rows(tr, d), _whole((d, d)),
                   _whole((pd, d))),
        compiler_params=_params())(x3, p, target, w_gate, w_proj)


_HBM = pl.BlockSpec(memory_space=pltpu.HBM)


def _all_gather(shards, name):
    n = len(shards)

    def body(*refs):
        x_refs, out_refs = refs[:n], refs[n:2 * n]
        send_sems, recv_sems, local_sems = refs[2 * n:]
        x, y, c = lax.axis_index("x"), lax.axis_index("y"), lax.axis_index("c")
        me, sibling = (x, y, c), (x, y, 1 - c)
        chips = [(1 - x, y), (x, 1 - y), (1 - x, 1 - y)]

        def copy(a, k, block, to, own=False):
            slot = out_refs[a].at[4 * block[0] + 2 * block[1] + block[2]]
            return pltpu.make_async_remote_copy(
                src_ref=x_refs[a] if own else slot, dst_ref=slot,
                send_sem=send_sems.at[7 * a + k], recv_sem=recv_sems.at[7 * a + k],
                device_id=to, device_id_type=pl.DeviceIdType.MESH)

        mine = [pltpu.make_async_copy(x_refs[a], out_refs[a].at[4 * x + 2 * y + c], local_sems.at[a])
                for a in range(n)]
        sends = []
        for a in range(n):
            mine[a].start()
            first = [copy(a, 0, me, sibling, own=True)]
            first += [copy(a, 1 + j, me, (*chip, c), own=True) for j, chip in enumerate(chips)]
            for cp in first:
                cp.start()
            sends += first
        for a in range(n):
            for j, chip in enumerate(chips):
                copy(a, 1 + j, (*chip, c), me).wait_recv()
                passed = copy(a, 4 + j, (*chip, c), sibling)
                passed.start()
                sends.append(passed)
        for a in range(n):
            copy(a, 0, sibling, me).wait_recv()
            for j, chip in enumerate(chips):
                copy(a, 4 + j, (*chip, 1 - c), me).wait_recv()
        for cp in sends:
            cp.wait_send()
        for cp in mine:
            cp.wait()

    return pl.pallas_call(
        body, name=name,
        out_shape=tuple(jax.ShapeDtypeStruct((N_DEV,) + s.shape, s.dtype) for s in shards),
        in_specs=[_HBM] * n, out_specs=tuple([_HBM] * n),
        scratch_shapes=[pltpu.SemaphoreType.DMA((7 * n,)), pltpu.SemaphoreType.DMA((7 * n,)),
                        pltpu.SemaphoreType.DMA((n,))],
    )(*shards)


_SEM = pl.BlockSpec(memory_space=pltpu.SEMAPHORE)
_ANY = pl.BlockSpec(memory_space=pl.ANY)
_DATAFLOW = pltpu.SideEffectType.DATAFLOW_SIDE_EFFECTING


def _peer(k, x, y, c):
    return (1 - x if k & 4 else x, 1 - y if k & 2 else y, 1 - c if k & 1 else c)


def _exchange_start(sends, after, name):
    n = len(sends)
    me = 4 * lax.axis_index("x") + 2 * lax.axis_index("y") + lax.axis_index("c")
    lands = []
    for s in sends:
        own = s[0] if s.shape[0] == 1 else lax.dynamic_index_in_dim(s, me, 0, keepdims=False)
        land = jnp.zeros((N_DEV,) + s.shape[1:], s.dtype)
        lands.append(lax.dynamic_update_index_in_dim(land, own, me, 0))

    def body(*refs):
        s_refs, l_refs = refs[:n], refs[n:2 * n]
        send_sems, recv_sems, token = refs[2 * n + 1], refs[2 * n + 2], refs[-1]
        x, y, c = lax.axis_index("x"), lax.axis_index("y"), lax.axis_index("c")
        for a in range(n):
            same = sends[a].shape[0] == 1
            for k in range(1, N_DEV):
                px, py, pc = _peer(k, x, y, c)
                pltpu.make_async_remote_copy(
                    src_ref=s_refs[a].at[0 if same else 4 * px + 2 * py + pc],
                    dst_ref=l_refs[a].at[4 * x + 2 * y + c],
                    send_sem=send_sems.at[7 * a + k - 1], recv_sem=recv_sems.at[7 * a + k - 1],
                    device_id=(px, py, pc), device_id_type=pl.DeviceIdType.MESH).start()
        token[...] = jnp.zeros_like(token)

    outs = pl.pallas_call(
        body, name=name,
        out_shape=(pltpu.SemaphoreType.DMA((7 * n,)), pltpu.SemaphoreType.DMA((7 * n,)),
                   *[pltpu.HBM(s.shape, s.dtype) for s in sends],
                   *[pltpu.HBM(l.shape, l.dtype) for l in lands],
                   jax.ShapeDtypeStruct((8, 128), F32)),
        in_specs=[_HBM] * (2 * n) + [_ANY],
        out_specs=(_SEM, _SEM, *([_HBM] * (2 * n)), pl.BlockSpec(memory_space=pltpu.VMEM)),
        input_output_aliases={i: 2 + i for i in range(2 * n)},
        compiler_params=pltpu.CompilerParams(has_side_effects=_DATAFLOW),
    )(*[pltpu.with_memory_space_constraint(v, pltpu.HBM) for v in list(sends) + lands], after)
    return (outs[0], outs[1], list(outs[2:2 + n]), list(outs[2 + n:2 + 2 * n])), outs[-1]


def _behind(value, token):
    return lax.optimization_barrier((value, token))[0]


def _exchange_wait(handle, after, name):
    send_sems, recv_sems, sends, lands = handle
    n = len(sends)

    def body(*refs):
        s_refs, l_refs = refs[:n], refs[n:2 * n]
        send_sems, recv_sems = refs[2 * n], refs[2 * n + 1]
        x, y, c = lax.axis_index("x"), lax.axis_index("y"), lax.axis_index("c")
        for a in range(n):
            for k in range(1, N_DEV):
                copy = pltpu.make_async_remote_copy(
                    src_ref=s_refs[a].at[0], dst_ref=l_refs[a].at[0],
                    send_sem=send_sems.at[7 * a + k - 1], recv_sem=recv_sems.at[7 * a + k - 1],
                    device_id=_peer(k, x, y, c), device_id_type=pl.DeviceIdType.MESH)
                copy.wait_send()
                copy.wait_recv()

    outs = pl.pallas_call(
        body, name=name,
        out_shape=(*[pltpu.HBM(s.shape, s.dtype) for s in sends], *[pltpu.HBM(l.shape, l.dtype) for l in lands]),
        in_specs=[_HBM] * (2 * n) + [_SEM, _SEM, _ANY],
        out_specs=tuple([_HBM] * (2 * n)),
        input_output_aliases={i: i for i in range(2 * n)},
        compiler_params=pltpu.CompilerParams(has_side_effects=_DATAFLOW),
    )(*sends, *lands, send_sems, recv_sems, after)
    return list(outs[n:])


def _adamw(recv, w, m, v, name):
    n, rows, width = recv.shape
    tr = _pick(rows, max(8, ADAM_BLOCK_BYTES // (n * width * recv.dtype.itemsize)), 8)
    c_m = 1.0 - ADAM_B1 ** ADAM_STEP
    c_v = 1.0 - ADAM_B2 ** ADAM_STEP

    def body(r_ref, w_ref, m_ref, v_ref, g_ref, d_ref, nm_ref, nv_ref):
        g = r_ref[0].astype(F32)
        for i in range(1, n):
            g = g + r_ref[i].astype(F32)
        m2 = ADAM_B1 * m_ref[...] + (1.0 - ADAM_B1) * g
        v2 = ADAM_B2 * v_ref[...] + (1.0 - ADAM_B2) * (g * g)
        m_hat = m2 / c_m
        v_hat = v2 / c_v
        g_ref[...] = g
        d_ref[...] = -ADAM_LR * (m_hat / (jnp.sqrt(v_hat) + ADAM_EPS) + ADAM_WD * w_ref[...])
        nm_ref[...] = m2
        nv_ref[...] = v2

    blk = _rows(tr, width)
    shp = jax.ShapeDtypeStruct((rows, width), F32)
    return pl.pallas_call(
        body, name=name, out_shape=(shp, shp, shp, shp), grid=(rows // tr,),
        in_specs=[pl.BlockSpec((n, tr, width), lambda i: (0, i, 0)), blk, blk, blk],
        out_specs=(blk, blk, blk, blk), compiler_params=_params())(recv, w, m, v)


def _join_cols(gathered):
    n, r, c = gathered.shape
    return gathered.transpose(1, 0, 2).reshape(r, n * c)


def _split_cols(full):
    r, c = full.shape
    return full.reshape(r, N_DEV, c // N_DEV).transpose(1, 0, 2)


def _small_rows(size):
    return -(-size // (8 * PACK_W)) * 8


def _pack_small(parts, names):
    rows = []
    for n in names:
        flat = parts[n].reshape(-1)
        r = _small_rows(flat.shape[0])
        rows.append(jnp.pad(flat, (0, r * PACK_W - flat.shape[0])).reshape(r, PACK_W))
    return jnp.concatenate(rows, axis=0)


def _unpack_small(packed, shapes, names):
    out, r0 = {}, 0
    for n in names:
        size = math.prod(shapes[n])
        rows = _small_rows(size)
        out[n] = packed[r0:r0 + rows].reshape(-1)[:size].reshape(shapes[n])
        r0 += rows
    return out


def _discretise(lam_re, lam_im, log_dt, b_re, b_im):
    dt = jnp.exp(log_dt)[:, None]
    mag = jnp.exp(lam_re * dt)
    ab_re = mag * jnp.cos(lam_im * dt)
    ab_im = mag * jnp.sin(lam_im * dt)
    nr = ab_re - 1.0
    ni = ab_im
    den = lam_re * lam_re + lam_im * lam_im
    coef_re = ((nr * lam_re + ni * lam_im) / den)[..., None]
    coef_im = ((ni * lam_re - nr * lam_im) / den)[..., None]
    bb_re = coef_re * b_re - coef_im * b_im
    bb_im = coef_re * b_im + coef_im * b_re
    return ab_re, ab_im, bb_re, bb_im


def _same_group(g):
    return jnp.eye(g, dtype=bool)[:, None, :, None]


def _block_diag_in(bb):
    g, p, i = bb.shape
    placed = jnp.where(_same_group(g), bb.transpose(0, 2, 1)[:, :, None, :], 0.0)
    return placed.reshape(g * i, g * p)


def _block_diag_in_take(dense, g, p, i):
    blocks = jnp.where(_same_group(g), dense.reshape(g, i, g, p), 0.0).sum(axis=2)
    return blocks.transpose(0, 2, 1)


def _block_diag_out(cc):
    g, i, p = cc.shape
    placed = jnp.where(_same_group(g), cc.transpose(0, 2, 1)[:, :, None, :], 0.0)
    return placed.reshape(g * p, g * i)


def _block_diag_out_take(dense, g, i, p):
    blocks = jnp.where(_same_group(g), dense.reshape(g, p, g, i), 0.0).sum(axis=2)
    return blocks.transpose(0, 2, 1)


def _perm_rows(z, n_seq):
    t, w = z.shape
    steps = t // n_seq // N_SEG
    return z.reshape(n_seq, N_SEG, steps, w).transpose(0, 2, 1, 3).reshape(t, w)


def _unperm_rows(z, n_seq):
    t, w = z.shape
    steps = t // n_seq // N_SEG
    return z.reshape(n_seq, steps, N_SEG, w).transpose(0, 2, 1, 3).reshape(t, w)


def kernel(*args):
    assert len(args) == len(INPUT_NAMES)
    inp = dict(zip(INPUT_NAMES, args))
    depth = inp["ffn1_w_in"].shape[0]
    assert depth == 1
    alpha = (2.0 * depth) ** 0.25

    n_seq, seq, d_model = inp["x"].shape
    t = n_seq * seq
    x = inp["x"].reshape(t, d_model)
    x16 = x.astype(BF16)
    p16 = inp["p"][0].reshape(t, -1).astype(BF16)
    target = inp["loss_target"].reshape(t, d_model)
    wts = {n: inp[n][0] for n in WEIGHTS}
    mom = {n: inp["m_" + n][0] for n in WEIGHTS}
    var = {n: inp["v_" + n][0] for n in WEIGHTS}

    full = {}

    def place(n, g):
        if n in ("ffn1_w_in", "ffn2_w_in"):
            full[n] = g.reshape((2, N_DEV // 2) + g.shape[1:])
        elif n in ("ffn1_w_out", "ffn2_w_out"):
            full[n] = g.reshape(N_DEV // 2, 2 * g.shape[1], g.shape[2])
        elif n in COL_SHARDED:
            full[n] = _join_cols(g)
        else:
            full[n] = g.reshape(N_DEV * g.shape[1], g.shape[2])

    w16 = {n: wts[n].astype(BF16) for n in SHARDED}
    for n, g in zip(GATHER_FIRST, _all_gather([w16[n] for n in GATHER_FIRST], "gather_ffn1")):
        place(n, g)
    gather_mix, token = _exchange_start([w16[n][None] for n in GATHER_MIX], full["ffn1_w_out"],
                                        "gather_mix_start")
    x16 = _behind(x16, token)

    def row(v):
        return v.reshape(1, -1)

    n_groups, n_state = wts["ssm_lambda_re"].shape
    n_ch = wts["ssm_b_re"].shape[2]
    disc_in = (wts["ssm_lambda_re"], wts["ssm_lambda_im"], wts["ssm_log_dt"], wts["ssm_b_re"],
               wts["ssm_b_im"])
    (ab_re, ab_im, bb_re, bb_im), disc_vjp = jax.vjp(_discretise, *disc_in)
    a_re, a_im = row(ab_re), row(ab_im)
    b_dense_re = _block_diag_in(bb_re).astype(BF16)
    b_dense_im = _block_diag_in(bb_im).astype(BF16)
    c_dense_re = _block_diag_out(wts["ssm_c_re"]).astype(BF16)
    c_dense_im = _block_diag_out(-wts["ssm_c_im"]).astype(BF16)

    ws = wts["gmlp_w_s"]
    n_heads = ws.shape[0]
    d_gmlp = wts["gmlp_ln_g"].shape[0]
    causal = jnp.tril(jnp.ones((CHUNK, CHUNK), dtype=bool))
    ws_masked = jnp.where(causal[None], ws, 0.0).astype(BF16)
    ws_masked_t = ws_masked.transpose(0, 2, 1)
    bs_cols = jnp.repeat(wts["gmlp_b_s"].T, d_gmlp // n_heads, axis=1)
    d_ssm = n_groups * n_ch
    assert d_ssm == d_gmlp and d_model == 2 * d_ssm

    h1, a1 = _ffn_in(x16, full["ffn1_w_in"], "ffn1_in")
    r1, x1, x1_16 = _ffn_out_ln(a1, full["ffn1_w_out"], x, row(wts["ln1_g"]), row(wts["ln1_b"]), alpha,
                                "ffn1_out_ln")

    for n, g in zip(GATHER_MIX, _exchange_wait(gather_mix, x1_16, "gather_mix_wait")):
        place(n, g)
    gather_last, token = _exchange_start([w16[n][None] for n in GATHER_LAST], full["mix_w_in"],
                                         "gather_ffn2_start")
    x1_16 = _behind(x1_16, token)
    proj = _mm(x1_16, full["mix_w_in"], name="mix_in")
    za_p = _perm_rows(proj[:, :d_ssm], n_seq)
    za_p16 = za_p.astype(BF16)
    bu_re = _mm(za_p16, b_dense_re, name="s5_bu_re")
    bu_im = _mm(za_p16, b_dense_im, name="s5_bu_im")
    h_re, h_im = _scan_fwd(bu_re, bu_im, a_re, a_im, n_seq, "s5_scan")
    yssm = _mm(h_re, c_dense_re, name="s5_y_re")
    yssm = _mm(h_im, c_dense_im, name="s5_y_im", add=yssm)
    s5out_p = _s5_post_fwd(yssm, za_p, row(wts["ssm_d"]), full["ssm_glu_w"], row(wts["ssm_glu_b"]),
                           "s5_post")
    s5out = _unperm_rows(s5out_p, n_seq)
    gm = _gmlp_fwd(proj, row(wts["gmlp_ln_g"]), row(wts["gmlp_ln_b"]), ws_masked, bs_cols, "gmlp")
    m_mix, y_a, y_b = _merge_fwd(s5out, gm, proj, full["up_a"], full["up_b"], "merge")
    mixed = _mm(m_mix, full["mix_w_out"], name="mix_out")
    r2, x2, x2_16 = _resid_ln_fwd(x1, mixed, row(wts["ln2_g"]), row(wts["ln2_b"]), alpha, 1.0, "ln2")

    for n, g in zip(GATHER_LAST, _exchange_wait(gather_last, x2_16, "gather_ffn2_wait")):
        place(n, g)
    h2, a2 = _ffn_in(x2_16, full["ffn2_w_in"], "ffn2_in")
    r3, x3, _ = _ffn_out_ln(a2, full["ffn2_w_out"], x2, row(wts["ln3_g"]), row(wts["ln3_b"]), alpha,
                            "ffn2_out_ln")

    small = {}
    send = {}
    loss_parts, dx3, dw_gate, dw_proj = _head(x3, p16, target, full["ple_w_gate"], full["ple_w_proj"], "head")
    loss = lax.psum(jnp.sum(loss_parts[::8, 0]), ("x", "y", "c"))
    send["ple_w_gate"] = dw_gate.astype(BF16).reshape(N_DEV, -1, d_model)
    send["ple_w_proj"] = _split_cols(dw_proj.astype(BF16))

    dr3, dr3_h, dg, db = _ln_bwd_call(r3, dx3, None, 1.0, row(wts["ln3_g"]), 0.5, "ln3_bwd")
    small["ln3_g"], small["ln3_b"] = dg.sum(0), db.sum(0)
    dh2, dw = _ffn_out_bwd(dr3_h, full["ffn2_w_out"], h2, a2, "ffn2_out_bwd")
    send["ffn2_w_out"] = dw.reshape(N_DEV, -1, d_model)
    dw = _ffn_in_dw(x2_16, dh2, "ffn2_in_dw")
    send["ffn2_w_in"] = dw.reshape((N_DEV,) + dw.shape[2:])
    group1 = ("ffn2_w_in", "ffn2_w_out", "ple_w_gate", "ple_w_proj")
    exchange1, token = _exchange_start([send[n] for n in group1], dh2, "grad_exchange1_start")
    dx2_f = _ffn_in_dx(_behind(dh2, token), full["ffn2_w_in"], None, 1.0, "ffn2_in_dx")

    dr2, dr2_h, dg, db = _ln_bwd_call(r2, dx2_f, dr3, alpha, row(wts["ln2_g"]), 1.0, "ln2_bwd")
    small["ln2_g"], small["ln2_b"] = dg.sum(0), db.sum(0)
    dm = _mm(dr2_h, full["mix_w_out"], tb=True, name="mix_out_dx")
    send["mix_w_out"] = _mm(m_mix, dr2_h, ta=True, name="mix_out_dw", out_dtype=BF16).reshape(
        N_DEV, -1, d_model)
    dga, dgb, ds5out, dgm, dw_a, dw_b = _merge_bwd(
        dm, y_a, y_b, s5out, gm, proj, full["up_a"], full["up_b"], "merge_bwd")
    send["up_a"] = _split_cols(dw_a.astype(BF16))
    send["up_b"] = _split_cols(dw_b.astype(BF16))

    dzu, dzv, dws, dbs_cols, dg, db = _gmlp_bwd(
        proj, dgm, row(wts["gmlp_ln_g"]), row(wts["gmlp_ln_b"]), ws_masked, ws_masked_t, bs_cols,
        "gmlp_bwd")
    small["gmlp_ln_g"], small["gmlp_ln_b"] = dg.sum(0), db.sum(0)
    small["gmlp_w_s"] = jnp.where(causal[None], dws, 0.0)
    small["gmlp_b_s"] = dbs_cols.reshape(CHUNK, n_heads, -1).sum(-1).T

    ds5out_p = _perm_rows(ds5out, n_seq)
    dy_p, dza_skip, dw_glu, dd, dgb_glu = _s5_post_bwd(
        yssm, za_p, ds5out_p, row(wts["ssm_d"]), full["ssm_glu_w"], row(wts["ssm_glu_b"]),
        "s5_post_bwd")
    send["ssm_glu_w"] = dw_glu.astype(BF16).reshape(N_DEV, -1, d_ssm)
    small["ssm_d"], small["ssm_glu_b"] = dd.sum(0), dgb_glu.sum(0)
    g_re = _mm(dy_p, c_dense_re, tb=True, name="s5_y_re_dx")
    g_im = _mm(dy_p, c_dense_im, tb=True, name="s5_y_im_dx")
    dc_re = _mm(h_re, dy_p, ta=True, name="s5_y_re_dw")
    dc_im = _mm(h_im, dy_p, ta=True, name="s5_y_im_dw")
    small["ssm_c_re"] = _block_diag_out_take(dc_re, n_groups, n_ch, n_state)
    small["ssm_c_im"] = -_block_diag_out_take(dc_im, n_groups, n_ch, n_state)
    lam_re, lam_im, da_re, da_im = _scan_bwd(g_re, g_im, h_re, h_im, a_re, a_im, n_seq, "s5_scan_bwd")
    dza_p = _mm(lam_re, b_dense_re, tb=True, name="s5_bu_re_dx", add=dza_skip)
    dza_p = _mm(lam_im, b_dense_im, tb=True, name="s5_bu_im_dx", add=dza_p, out_dtype=BF16)
    dbb_re = _block_diag_in_take(_mm(za_p16, lam_re, ta=True, name="s5_bu_re_dw"), n_groups, n_state, n_ch)
    dbb_im = _block_diag_in_take(_mm(za_p16, lam_im, ta=True, name="s5_bu_im_dw"), n_groups, n_state, n_ch)
    dab_re = da_re.sum(0).reshape(n_groups, n_state)
    dab_im = da_im.sum(0).reshape(n_groups, n_state)
    (small["ssm_lambda_re"], small["ssm_lambda_im"], small["ssm_log_dt"], small["ssm_b_re"],
     small["ssm_b_im"]) = disc_vjp((dab_re, dab_im, dbb_re, dbb_im))

    dproj = jnp.concatenate([_unperm_rows(dza_p, n_seq), dzu, dzv, dga, dgb], axis=1)
    send["mix_w_in"] = _split_cols(_mm(x1_16, dproj, ta=True, name="mix_in_dw", out_dtype=BF16))
    group2 = ("mix_w_in", "mix_w_out", "up_a", "up_b", "ssm_glu_w")
    exchange2, token = _exchange_start(
        [send[n] for n in group2] + [_pack_small(small, SMALL_EARLY)[None]], dproj, "grad_exchange2_start")
    dx1_f = _mm(_behind(dproj, token), full["mix_w_in"], tb=True, name="mix_in_dx")

    dr1, dr1_h, dg, db = _ln_bwd_call(r1, dx1_f, dr2, alpha, row(wts["ln1_g"]), 0.5, "ln1_bwd")
    small["ln1_g"], small["ln1_b"] = dg.sum(0), db.sum(0)
    dh1, dw = _ffn_out_bwd(dr1_h, full["ffn1_w_out"], h1, a1, "ffn1_out_bwd")
    send["ffn1_w_out"] = dw.reshape(N_DEV, -1, d_model)
    exchange3, token = _exchange_start([send["ffn1_w_out"], _pack_small(small, SMALL_LATE)[None]], dh1,
                                       "grad_exchange3_start")
    dh1 = _behind(dh1, token)
    dw = _ffn_in_dw(x16, dh1, "ffn1_in_dw")
    send["ffn1_w_in"] = dw.reshape((N_DEV,) + dw.shape[2:])
    exchange4, token = _exchange_start([send["ffn1_w_in"]], dh1, "grad_exchange4_start")
    grad_x = _ffn_in_dx(_behind(dh1, token), full["ffn1_w_in"], dr1, alpha, "ffn1_in_dx")

    results = {}

    def update(names, recv, prefix):
        for n, r in zip(names, recv):
            results[n] = _adamw(r, wts[n], mom[n], var[n], prefix + n)

    def update_small(names, recv, name):
        packed = _adamw(recv, _pack_small(wts, names), _pack_small(mom, names), _pack_small(var, names), name)
        shapes = {n: wts[n].shape for n in names}
        unpacked = [_unpack_small(pk, shapes, names) for pk in packed]
        for n in names:
            results[n] = tuple(u[n] for u in unpacked)

    update(group1, _exchange_wait(exchange1, grad_x, "grad_exchange1_wait"), "adamw_")
    recv = _exchange_wait(exchange2, results[group1[-1]][0], "grad_exchange2_wait")
    update(group2, recv[:-1], "adamw_")
    update_small(SMALL_EARLY, recv[-1], "adamw_small")
    recv = _exchange_wait(exchange3, results[SMALL_EARLY[-1]][0], "grad_exchange3_wait")
    update(("ffn1_w_out",), recv[:1], "adamw_")
    update_small(SMALL_LATE, recv[1], "adamw_ln1")
    recv = _exchange_wait(exchange4, results["ln1_b"][0], "grad_exchange4_wait")
    update(("ffn1_w_in",), recv, "adamw_")

    outs = [loss, grad_x.reshape(n_seq, seq, d_model)]
    for k in range(4):
        outs += [results[n][k][None] for n in WEIGHTS]
    return tuple(outs)
```

```python
import functools
import math

import jax
import jax.numpy as jnp
from jax import lax
from jax.experimental import pallas as pl
from jax.experimental.pallas import tpu as pltpu

F32 = jnp.float32
BF16 = jnp.bfloat16

N_DEV = 8
LN_EPS = 1e-5
CHUNK = 128
N_SEG = 8
PACK_W = 1024
VMEM_LIMIT_BYTES = 56 * 1024 * 1024
MM_OPERAND_BLOCK_BYTES = 8 * 1024 * 1024
ADAM_BLOCK_BYTES = 6 * 1024 * 1024

ADAM_LR = 0.001
ADAM_B1 = 0.9
ADAM_B2 = 0.999
ADAM_EPS = 1e-08
ADAM_WD = 0.01
ADAM_STEP = 10

COL_SHARDED = ("ffn1_w_in", "mix_w_in", "up_a", "up_b", "ffn2_w_in", "ple_w_proj")
ROW_SHARDED = ("ffn1_w_out", "ssm_glu_w", "mix_w_out", "ffn2_w_out", "ple_w_gate")
SHARDED = ("ffn1_w_in", "ffn1_w_out", "mix_w_in", "ssm_glu_w", "up_a", "up_b", "mix_w_out",
           "ffn2_w_in", "ffn2_w_out", "ple_w_proj", "ple_w_gate")
WEIGHTS = ("ffn1_w_in", "ffn1_w_out", "ln1_g", "ln1_b", "mix_w_in", "ssm_lambda_re", "ssm_lambda_im",
           "ssm_log_dt", "ssm_b_re", "ssm_b_im", "ssm_c_re", "ssm_c_im", "ssm_d", "ssm_glu_w",
           "ssm_glu_b", "gmlp_ln_g", "gmlp_ln_b", "gmlp_w_s", "gmlp_b_s", "up_a", "up_b", "mix_w_out",
           "ln2_g", "ln2_b", "ffn2_w_in", "ffn2_w_out", "ln3_g", "ln3_b", "ple_w_proj", "ple_w_gate")
GATHER_FIRST = ("ffn1_w_in", "ffn1_w_out")
GATHER_MIX = ("mix_w_in", "ssm_glu_w", "up_a", "up_b", "mix_w_out")
GATHER_LAST = ("ffn2_w_in", "ffn2_w_out", "ple_w_proj", "ple_w_gate")
SMALL = tuple(n for n in WEIGHTS if n not in SHARDED)
SMALL_LATE = ("ln1_g", "ln1_b")
SMALL_EARLY = tuple(n for n in SMALL if n not in SMALL_LATE)
INPUT_NAMES = ("x", "p") + WEIGHTS + ("loss_target",) + tuple("m_" + n for n in WEIGHTS) + tuple(
    "v_" + n for n in WEIGHTS)


def _pick(dim, pref, mult=128):
    if dim <= pref:
        return dim
    d = (pref // mult) * mult
    while d >= mult:
        if dim % d == 0:
            return d
        d -= mult
    return dim


def _params(n_axes=1):
    return pltpu.CompilerParams(dimension_semantics=("arbitrary",) * n_axes,
                                vmem_limit_bytes=VMEM_LIMIT_BYTES)


def _sigmoid(v):
    return 1.0 / (1.0 + jnp.exp(-v))


_GELU_C = math.sqrt(2.0 / math.pi)


def _gelu(v):
    return 0.5 * v * (1.0 + jnp.tanh(_GELU_C * (v + 0.044715 * (v * v * v))))


def _gelu_grad(v):
    t = jnp.tanh(_GELU_C * (v + 0.044715 * (v * v * v)))
    return 0.5 * (1.0 + t) + 0.5 * v * (1.0 - t * t) * (_GELU_C * (1.0 + 3.0 * 0.044715 * v * v))


def _ln_stats(r):
    mu = jnp.mean(r, axis=-1, keepdims=True)
    xc = r - mu
    var = jnp.mean(xc * xc, axis=-1, keepdims=True)
    rstd = lax.rsqrt(var + LN_EPS)
    return xc * rstd, rstd


def _ln_bwd(dy, xhat, rstd, g):
    dxh = dy * g
    m1 = jnp.mean(dxh, axis=-1, keepdims=True)
    m2 = jnp.mean(dxh * xhat, axis=-1, keepdims=True)
    return rstd * (dxh - m1 - xhat * m2)


def _fold8(v):
    rows, w = v.shape
    return jnp.sum(v.reshape(rows // 8, 8, w), axis=0)


def _dot(a, b, ta=False, tb=False):
    dn = (((0 if ta else 1,), (1 if tb else 0,)), ((), ()))
    return lax.dot_general(a.astype(BF16), b.astype(BF16), dn, preferred_element_type=F32)


_TOKEN = pl.BlockSpec((8, 128), lambda *_: (0, 0))


def _mm(a, b, *, name, ta=False, tb=False, out_dtype=F32, add=None, add_scale=1.0, token=None,
        tm=2048, tn=512, tk=4096):
    m_dim = a.shape[1] if ta else a.shape[0]
    k_dim = a.shape[0] if ta else a.shape[1]
    n_dim = b.shape[0] if tb else b.shape[1]
    assert (b.shape[1] if tb else b.shape[0]) == k_dim, (name, a.shape, b.shape)
    tk = _pick(k_dim, tk)
    tm = min(tm, max(256, MM_OPERAND_BLOCK_BYTES // (tk * a.dtype.itemsize)))
    tm, tn = _pick(m_dim, tm), _pick(n_dim, tn)
    nk = k_dim // tk
    has_add = add is not None

    def finish(r, add_ref, o_ref):
        if has_add:
            r = r + add_scale * add_ref[...].astype(F32)
        o_ref[...] = r.astype(out_dtype)

    def body(*refs):
        a_ref, b_ref = refs[:2]
        add_ref = refs[2] if has_add else None
        o_ref = refs[n_in]
        if nk == 1:
            finish(_dot(a_ref[...], b_ref[...], ta, tb), add_ref, o_ref)
            return
        acc_ref = refs[-1]
        k = pl.program_id(2)

        @pl.when(k == 0)
        def _():
            acc_ref[...] = jnp.zeros_like(acc_ref)

        acc_ref[...] += _dot(a_ref[...], b_ref[...], ta, tb)

        @pl.when(k == nk - 1)
        def _():
            finish(acc_ref[...], add_ref, o_ref)

    a_spec = (pl.BlockSpec((tk, tm), lambda i, j, k: (k, i)) if ta
              else pl.BlockSpec((tm, tk), lambda i, j, k: (i, k)))
    b_spec = (pl.BlockSpec((tn, tk), lambda i, j, k: (j, k)) if tb
              else pl.BlockSpec((tk, tn), lambda i, j, k: (k, j)))
    in_specs = [a_spec, b_spec]
    operands = [a, b]
    if has_add:
        in_specs.append(pl.BlockSpec((tm, tn), lambda i, j, k: (i, j)))
        operands.append(add)
    if token is not None:
        in_specs.append(_TOKEN)
        operands.append(token)
    n_in = len(operands)
    return pl.pallas_call(
        body, name=name,
        out_shape=jax.ShapeDtypeStruct((m_dim, n_dim), out_dtype),
        grid=(m_dim // tm, n_dim // tn, nk),
        in_specs=in_specs,
        out_specs=pl.BlockSpec((tm, tn), lambda i, j, k: (i, j)),
        scratch_shapes=[pltpu.VMEM((tm, tn), F32)] if nk > 1 else [],
        compiler_params=_params(3),
    )(*operands)


def _rows(tr, w, cb=0):
    return pl.BlockSpec((tr, w), lambda i: (i, cb))


def _whole(shape):
    nd = len(shape)
    return pl.BlockSpec(tuple(shape), lambda i: (0,) * nd)


def _ffn_in(x16, w_in, name, token=None):
    t, d = x16.shape
    _, nb, _, fb = w_in.shape
    tm = _pick(t, 1024, 8)
    extra = [] if token is None else [token]

    def body(*refs):
        x_ref, wg_ref, wu_ref = refs[:3]
        h_ref, a_ref = refs[-2:]
        xv = x_ref[...]
        g = _dot(xv, wg_ref[0, 0])
        u = _dot(xv, wu_ref[0, 0])
        h_ref[0, 0] = g
        h_ref[0, 1] = u
        a_ref[0] = (g * _sigmoid(g) * u).astype(BF16)

    return pl.pallas_call(
        body, name=name,
        out_shape=(jax.ShapeDtypeStruct((nb, 2, t, fb), F32), jax.ShapeDtypeStruct((nb, t, fb), BF16)),
        grid=(t // tm, nb),
        in_specs=[pl.BlockSpec((tm, d), lambda i, j: (i, 0)),
                  pl.BlockSpec((1, 1, d, fb), lambda i, j: (0, j, 0, 0)),
                  pl.BlockSpec((1, 1, d, fb), lambda i, j: (1, j, 0, 0))] + [_TOKEN] * len(extra),
        out_specs=(pl.BlockSpec((1, 2, tm, fb), lambda i, j: (j, 0, i, 0)),
                   pl.BlockSpec((1, tm, fb), lambda i, j: (j, i, 0))),
        compiler_params=_params(2))(x16, w_in, w_in, *extra)


def _ffn_out_ln(a, w_out, xin, g, b, alpha, name):
    nb, t, fb = a.shape
    d = w_out.shape[2]
    tm = _pick(t, 256, 8)

    def body(a_ref, w_ref, x_ref, g_ref, b_ref, r_ref, y_ref, y16_ref):
        f = _dot(a_ref[0], w_ref[0])
        for jb in range(1, nb):
            f = f + _dot(a_ref[jb], w_ref[jb])
        r = alpha * x_ref[...] + 0.5 * f
        xhat, _ = _ln_stats(r)
        y = xhat * g_ref[...] + b_ref[...]
        r_ref[...] = r
        y_ref[...] = y
        y16_ref[...] = y.astype(BF16)

    return pl.pallas_call(
        body, name=name,
        out_shape=(jax.ShapeDtypeStruct((t, d), F32), jax.ShapeDtypeStruct((t, d), F32),
                   jax.ShapeDtypeStruct((t, d), BF16)),
        grid=(t // tm,),
        in_specs=[pl.BlockSpec((nb, tm, fb), lambda i: (0, i, 0)), _whole(w_out.shape), _rows(tm, d),
                  _whole((1, d)), _whole((1, d))],
        out_specs=(_rows(tm, d), _rows(tm, d), _rows(tm, d)),
        compiler_params=_params())(a, w_out, xin, g, b)


def _ffn_out_bwd(drs, w_out, h, a, name):
    nb, t, fb = a.shape
    d = w_out.shape[2]
    tm = _pick(t, 1024, 8)
    n_i = t // tm

    def body(dr_ref, w_ref, h_ref, a_ref, dh_ref, dw_ref, acc_ref):
        i = pl.program_id(1)
        dr = dr_ref[...]
        da = _dot(dr, w_ref[0], tb=True)
        g, u = h_ref[0, 0], h_ref[0, 1]
        sg = _sigmoid(g)
        dh_ref[0, 0] = (da * u * (sg * (1.0 + g * (1.0 - sg)))).astype(BF16)
        dh_ref[0, 1] = (da * (g * sg)).astype(BF16)

        @pl.when(i == 0)
        def _():
            acc_ref[...] = jnp.zeros_like(acc_ref)

        acc_ref[...] += _dot(a_ref[0], dr, ta=True)

        @pl.when(i == n_i - 1)
        def _():
            dw_ref[0] = acc_ref[...].astype(BF16)

    return pl.pallas_call(
        body, name=name,
        out_shape=(jax.ShapeDtypeStruct((nb, 2, t, fb), BF16), jax.ShapeDtypeStruct((nb, fb, d), BF16)),
        grid=(nb, n_i),
        in_specs=[pl.BlockSpec((tm, d), lambda j, i: (i, 0)),
                  pl.BlockSpec((1, fb, d), lambda j, i: (j, 0, 0)),
                  pl.BlockSpec((1, 2, tm, fb), lambda j, i: (j, 0, i, 0)),
                  pl.BlockSpec((1, tm, fb), lambda j, i: (j, i, 0))],
        out_specs=(pl.BlockSpec((1, 2, tm, fb), lambda j, i: (j, 0, i, 0)),
                   pl.BlockSpec((1, fb, d), lambda j, i: (j, 0, 0))),
        scratch_shapes=[pltpu.VMEM((fb, d), F32)],
        compiler_params=_params(2))(drs, w_out, h, a)


def _ffn_in_dw(x16, dh, name, token=None):
    t, d = x16.shape
    nb, _, _, fb = dh.shape
    extra = [] if token is None else [token]

    def body(*refs):
        x_ref, dh_ref, o_ref = refs[0], refs[1], refs[-1]
        o_ref[0, 0] = _dot(x_ref[...], dh_ref[0, 0], ta=True).astype(BF16)

    return pl.pallas_call(
        body, name=name, out_shape=jax.ShapeDtypeStruct((2, nb, d, fb), BF16), grid=(nb, 2),
        in_specs=[pl.BlockSpec((t, d), lambda j, s: (0, 0)),
                  pl.BlockSpec((1, 1, t, fb), lambda j, s: (j, s, 0, 0))] + [_TOKEN] * len(extra),
        out_specs=pl.BlockSpec((1, 1, d, fb), lambda j, s: (s, j, 0, 0)),
        compiler_params=_params(2))(x16, dh, *extra)


def _ffn_in_dx(dh, w_in, add, add_scale, name, token=None):
    nb, _, t, fb = dh.shape
    d = w_in.shape[2]
    tm = _pick(t, 512, 8)
    has_add = add is not None
    extra = [] if token is None else [token]

    def body(*refs):
        dh_ref, w_ref = refs[:2]
        o_ref = refs[-1]
        acc = None
        for jb in range(nb):
            for s in range(2):
                part = _dot(dh_ref[jb, s], w_ref[s, jb], tb=True)
                acc = part if acc is None else acc + part
        if has_add:
            acc = acc + add_scale * refs[2][...]
        o_ref[...] = acc

    return pl.pallas_call(
        body, name=name, out_shape=jax.ShapeDtypeStruct((t, d), F32), grid=(t // tm,),
        in_specs=[pl.BlockSpec((nb, 2, tm, fb), lambda i: (0, 0, i, 0)), _whole(w_in.shape)]
        + ([_rows(tm, d)] if has_add else []) + [_TOKEN] * len(extra),
        out_specs=_rows(tm, d),
        compiler_params=_params())(*([dh, w_in] + ([add] if has_add else []) + extra))


def _resid_ln_fwd(xin, f, g, b, alpha, scale, name):
    t, d = xin.shape
    tr = _pick(t, 256, 8)

    def body(x_ref, f_ref, g_ref, b_ref, r_ref, y_ref, y16_ref):
        r = alpha * x_ref[...] + scale * f_ref[...]
        xhat, _ = _ln_stats(r)
        y = xhat * g_ref[...] + b_ref[...]
        r_ref[...] = r
        y_ref[...] = y
        y16_ref[...] = y.astype(BF16)

    return pl.pallas_call(
        body, name=name,
        out_shape=(jax.ShapeDtypeStruct((t, d), F32), jax.ShapeDtypeStruct((t, d), F32),
                   jax.ShapeDtypeStruct((t, d), BF16)),
        grid=(t // tr,),
        in_specs=[_rows(tr, d), _rows(tr, d), _whole((1, d)), _whole((1, d))],
        out_specs=(_rows(tr, d), _rows(tr, d), _rows(tr, d)),
        compiler_params=_params())(xin, f, g, b)


def _ln_bwd_call(r, dy_a, dy_b, b_scale, g, out_scale, name):
    t, d = r.shape
    tr = _pick(t, 256, 8)
    has_b = dy_b is not None

    def body(*refs):
        if has_b:
            r_ref, da_ref, db_ref, g_ref, dr_ref, drs_ref, dg_ref, dbeta_ref = refs
            dy = da_ref[...] + b_scale * db_ref[...]
        else:
            r_ref, da_ref, g_ref, dr_ref, drs_ref, dg_ref, dbeta_ref = refs
            dy = da_ref[...]
        xhat, rstd = _ln_stats(r_ref[...])
        dr = _ln_bwd(dy, xhat, rstd, g_ref[...])
        dr_ref[...] = dr
        drs_ref[...] = (out_scale * dr).astype(BF16)

        @pl.when(pl.program_id(0) == 0)
        def _():
            dg_ref[...] = jnp.zeros_like(dg_ref)
            dbeta_ref[...] = jnp.zeros_like(dbeta_ref)

        dg_ref[...] += _fold8(dy * xhat)
        dbeta_ref[...] += _fold8(dy)

    ins = [r, dy_a] + ([dy_b] if has_b else []) + [g]
    in_specs = [_rows(tr, d)] * (3 if has_b else 2) + [_whole((1, d))]
    return pl.pallas_call(
        body, name=name,
        out_shape=(jax.ShapeDtypeStruct((t, d), F32), jax.ShapeDtypeStruct((t, d), BF16),
                   jax.ShapeDtypeStruct((8, d), F32), jax.ShapeDtypeStruct((8, d), F32)),
        grid=(t // tr,), in_specs=in_specs,
        out_specs=(_rows(tr, d), _rows(tr, d), _whole((8, d)), _whole((8, d))),
        compiler_params=_params())(*ins)


def _take_row(v, row_id, s):
    return jnp.sum(jnp.where(row_id == s, v, 0.0), axis=0, keepdims=True)


def _seg_carries(f_re, f_im, p_re, p_im, reverse):
    row_id = lax.broadcasted_iota(jnp.int32, f_re.shape, 0)
    p_re, p_im = _take_row(p_re, row_id, 0), _take_row(p_im, row_id, 0)
    out_re, out_im = jnp.zeros_like(f_re), jnp.zeros_like(f_im)
    c_re, c_im = jnp.zeros_like(p_re), jnp.zeros_like(p_im)
    order = range(N_SEG - 1, -1, -1) if reverse else range(N_SEG)
    for s in order:
        out_re = jnp.where(row_id == s, c_re, out_re)
        out_im = jnp.where(row_id == s, c_im, out_im)
        fr, fi = _take_row(f_re, row_id, s), _take_row(f_im, row_id, s)
        c_re, c_im = fr + p_re * c_re - p_im * c_im, fi + p_re * c_im + p_im * c_re
    return out_re, out_im


def _scan_fwd(bu_re, bu_im, a_re, a_im, n_seq, name):
    t, n = bu_re.shape
    seq = t // n_seq
    steps = seq // N_SEG
    cb = _pick(n, 256)

    def body(bre, bim, are, aim, hre, him):
        ar = jnp.broadcast_to(are[...], (N_SEG, cb))
        ai = jnp.broadcast_to(aim[...], (N_SEG, cb))
        zero = jnp.zeros((N_SEG, cb), F32)

        def local(k, carry):
            lr, li, pr, pi = carry
            rows = pl.ds(pl.multiple_of(k * N_SEG, N_SEG), N_SEG)
            nr = ar * lr - ai * li + bre[rows, :]
            ni = ar * li + ai * lr + bim[rows, :]
            hre[rows, :] = nr
            him[rows, :] = ni
            return nr, ni, ar * pr - ai * pi, ar * pi + ai * pr

        f_re, f_im, p_re, p_im = lax.fori_loop(0, steps, local, (zero, zero, zero + 1.0, zero))
        c_re, c_im = _seg_carries(f_re, f_im, p_re, p_im, reverse=False)

        def fix(k, carry):
            pr, pi = carry
            rows = pl.ds(pl.multiple_of(k * N_SEG, N_SEG), N_SEG)
            hre[rows, :] = hre[rows, :] + (pr * c_re - pi * c_im)
            him[rows, :] = him[rows, :] + (pr * c_im + pi * c_re)
            return ar * pr - ai * pi, ar * pi + ai * pr

        lax.fori_loop(0, steps, fix, (ar, ai))

    blk = pl.BlockSpec((seq, cb), lambda s, j: (s, j))
    vec = pl.BlockSpec((1, cb), lambda s, j: (0, j))
    return pl.pallas_call(
        body, name=name,
        out_shape=(jax.ShapeDtypeStruct((t, n), F32), jax.ShapeDtypeStruct((t, n), F32)),
        grid=(n_seq, n // cb), in_specs=[blk, blk, vec, vec], out_specs=(blk, blk),
        compiler_params=_params(2))(bu_re, bu_im, a_re, a_im)


def _scan_bwd(g_re, g_im, h_re, h_im, a_re, a_im, n_seq, name):
    t, n = g_re.shape
    seq = t // n_seq
    steps = seq // N_SEG
    cb = _pick(n, 256)

    def body(gre, gim, hre, him, are, aim, lre, lim, dare, daim):
        ar = jnp.broadcast_to(are[...], (N_SEG, cb))
        ai = -jnp.broadcast_to(aim[...], (N_SEG, cb))
        zero = jnp.zeros((N_SEG, cb), F32)

        def local(i, carry):
            lr, li, pr, pi = carry
            k = steps - 1 - i
            rows = pl.ds(pl.multiple_of(k * N_SEG, N_SEG), N_SEG)
            nr = ar * lr - ai * li + gre[rows, :]
            ni = ar * li + ai * lr + gim[rows, :]
            lre[rows, :] = nr
            lim[rows, :] = ni
            return nr, ni, ar * pr - ai * pi, ar * pi + ai * pr

        f_re, f_im, p_re, p_im = lax.fori_loop(0, steps, local, (zero, zero, zero + 1.0, zero))
        c_re, c_im = _seg_carries(f_re, f_im, p_re, p_im, reverse=True)

        def accumulate(lam_r, lam_i, hp_r, hp_i, acc_r, acc_i):
            return acc_r + (lam_r * hp_r + lam_i * hp_i), acc_i + (lam_i * hp_r - lam_r * hp_i)

        def fix(i, carry):
            pr, pi, acc_r, acc_i = carry
            k = steps - 1 - i
            rows = pl.ds(pl.multiple_of(k * N_SEG, N_SEG), N_SEG)
            prev = pl.ds(pl.multiple_of((k - 1) * N_SEG, N_SEG), N_SEG)
            lam_r = lre[rows, :] + (pr * c_re - pi * c_im)
            lam_i = lim[rows, :] + (pr * c_im + pi * c_re)
            lre[rows, :] = lam_r
            lim[rows, :] = lam_i
            acc_r, acc_i = accumulate(lam_r, lam_i, hre[prev, :], him[prev, :], acc_r, acc_i)
            return ar * pr - ai * pi, ar * pi + ai * pr, acc_r, acc_i

        pr, pi, acc_r, acc_i = lax.fori_loop(0, steps - 1, fix, (ar, ai, zero, zero))
        first = pl.ds(0, N_SEG)
        last = pl.ds((steps - 1) * N_SEG, N_SEG)
        lam_r = lre[first, :] + (pr * c_re - pi * c_im)
        lam_i = lim[first, :] + (pr * c_im + pi * c_re)
        lre[first, :] = lam_r
        lim[first, :] = lam_i
        row_id = lax.broadcasted_iota(jnp.int32, (N_SEG, cb), 0)
        hp_r = jnp.where(row_id == 0, 0.0, pltpu.roll(hre[last, :], 1, 0))
        hp_i = jnp.where(row_id == 0, 0.0, pltpu.roll(him[last, :], 1, 0))
        acc_r, acc_i = accumulate(lam_r, lam_i, hp_r, hp_i, acc_r, acc_i)
        dare[...] = acc_r
        daim[...] = acc_i

    blk = pl.BlockSpec((seq, cb), lambda s, j: (s, j))
    vec = pl.BlockSpec((1, cb), lambda s, j: (0, j))
    acc = pl.BlockSpec((N_SEG, cb), lambda s, j: (s, j))
    return pl.pallas_call(
        body, name=name,
        out_shape=(jax.ShapeDtypeStruct((t, n), F32), jax.ShapeDtypeStruct((t, n), F32),
                   jax.ShapeDtypeStruct((n_seq * N_SEG, n), F32),
                   jax.ShapeDtypeStruct((n_seq * N_SEG, n), F32)),
        grid=(n_seq, n // cb), in_specs=[blk, blk, blk, blk, vec, vec],
        out_specs=(blk, blk, acc, acc),
        compiler_params=_params(2))(g_re, g_im, h_re, h_im, a_re, a_im)


def _s5_post_fwd(yssm, u, d_skip, glu_w, glu_b, name):
    t, w = yssm.shape
    tr = _pick(t, 512, 8)

    def body(y_ref, u_ref, d_ref, w_ref, b_ref, o_ref):
        yg = _gelu(y_ref[...] + d_ref[...] * u_ref[...])
        pre = _dot(yg, w_ref[...]) + b_ref[...]
        o_ref[...] = yg * _sigmoid(pre)

    return pl.pallas_call(
        body, name=name, out_shape=jax.ShapeDtypeStruct((t, w), F32), grid=(t // tr,),
        in_specs=[_rows(tr, w), _rows(tr, w), _whole((1, w)), _whole((w, w)), _whole((1, w))],
        out_specs=_rows(tr, w), compiler_params=_params())(yssm, u, d_skip, glu_w, glu_b)


def _s5_post_bwd(yssm, u, dout, d_skip, glu_w, glu_b, name):
    t, w = yssm.shape
    tr = _pick(t, 512, 8)

    def body(y_ref, u_ref, do_ref, d_ref, w_ref, b_ref, dy_ref, du_ref, dw_ref, dd_ref, db_ref):
        uu = u_ref[...]
        y = y_ref[...] + d_ref[...] * uu
        yg = _gelu(y)
        sg = _sigmoid(_dot(yg, w_ref[...]) + b_ref[...])
        do = do_ref[...]
        dpre = do * yg * sg * (1.0 - sg)
        dyg = do * sg + _dot(dpre, w_ref[...], tb=True)
        dy = dyg * _gelu_grad(y)
        dy_ref[...] = dy.astype(BF16)
        du_ref[...] = dy * d_ref[...]

        @pl.when(pl.program_id(0) == 0)
        def _():
            dw_ref[...] = jnp.zeros_like(dw_ref)
            dd_ref[...] = jnp.zeros_like(dd_ref)
            db_ref[...] = jnp.zeros_like(db_ref)

        dw_ref[...] += _dot(yg, dpre, ta=True)
        dd_ref[...] += _fold8(dy * uu)
        db_ref[...] += _fold8(dpre)

    return pl.pallas_call(
        body, name=name,
        out_shape=(jax.ShapeDtypeStruct((t, w), BF16), jax.ShapeDtypeStruct((t, w), F32),
                   jax.ShapeDtypeStruct((w, w), F32), jax.ShapeDtypeStruct((8, w), F32),
                   jax.ShapeDtypeStruct((8, w), F32)),
        grid=(t // tr,),
        in_specs=[_rows(tr, w), _rows(tr, w), _rows(tr, w), _whole((1, w)), _whole((w, w)),
                  _whole((1, w))],
        out_specs=(_rows(tr, w), _rows(tr, w), _whole((w, w)), _whole((8, w)), _whole((8, w))),
        compiler_params=_params())(yssm, u, dout, d_skip, glu_w, glu_b)


def _head_select(parts, head_dim):
    col_head = lax.broadcasted_iota(jnp.int32, parts[0].shape, 1) // head_dim
    out = jnp.zeros_like(parts[0])
    for h, part in enumerate(parts):
        out = jnp.where(col_head == h, part, out)
    return out


def _gmlp_fwd(proj, ln_g, ln_b, ws, bs_cols, name):
    t = proj.shape[0]
    n_heads = ws.shape[0]
    w = bs_cols.shape[1]
    head_dim = w // n_heads
    cpb = _pick(t // CHUNK, 4, 1)
    tr = cpb * CHUNK

    def body(zu_ref, zv_ref, g_ref, b_ref, ws_ref, bs_ref, o_ref):
        for c in range(cpb):
            rows = pl.ds(c * CHUNK, CHUNK)
            xhat, _ = _ln_stats(_gelu(zv_ref[rows, :]))
            vn = (xhat * g_ref[...] + b_ref[...]).astype(BF16)
            s = _head_select([_dot(ws_ref[h], vn) for h in range(n_heads)], head_dim) + bs_ref[...]
            o_ref[rows, :] = _gelu(zu_ref[rows, :]) * s

    return pl.pallas_call(
        body, name=name, out_shape=jax.ShapeDtypeStruct((t, w), F32), grid=(t // tr,),
        in_specs=[_rows(tr, w, 1), _rows(tr, w, 2), _whole((1, w)), _whole((1, w)),
                  _whole(ws.shape), _whole(bs_cols.shape)],
        out_specs=_rows(tr, w), compiler_params=_params())(proj, proj, ln_g, ln_b, ws, bs_cols)


def _gmlp_bwd(proj, dout, ln_g, ln_b, ws, ws_t, bs_cols, name):
    t = proj.shape[0]
    n_heads = ws.shape[0]
    w = bs_cols.shape[1]
    head_dim = w // n_heads
    cpb = _pick(t // CHUNK, 4, 1)
    tr = cpb * CHUNK

    def body(zu_ref, zv_ref, do_ref, g_ref, b_ref, ws_ref, wst_ref, bs_ref,
             dzu_ref, dzv_ref, dws_ref, dbs_ref, dg_ref, dbeta_ref):
        @pl.when(pl.program_id(0) == 0)
        def _():
            dws_ref[...] = jnp.zeros_like(dws_ref)
            dbs_ref[...] = jnp.zeros_like(dbs_ref)
            dg_ref[...] = jnp.zeros_like(dg_ref)
            dbeta_ref[...] = jnp.zeros_like(dbeta_ref)

        col_head = lax.broadcasted_iota(jnp.int32, (CHUNK, w), 1) // head_dim
        for c in range(cpb):
            rows = pl.ds(c * CHUNK, CHUNK)
            zu, zv, do = zu_ref[rows, :], zv_ref[rows, :], do_ref[rows, :]
            xhat, rstd = _ln_stats(_gelu(zv))
            vn = (xhat * g_ref[...] + b_ref[...]).astype(BF16)
            s = _head_select([_dot(ws_ref[h], vn) for h in range(n_heads)], head_dim) + bs_ref[...]
            dzu_ref[rows, :] = (do * s * _gelu_grad(zu)).astype(BF16)
            ds = do * _gelu(zu)
            ds16 = ds.astype(BF16)
            dbs_ref[...] += ds
            for h in range(n_heads):
                dws_ref[h] += _dot(jnp.where(col_head == h, ds16, jnp.zeros_like(ds16)), vn, tb=True)
            dvn = _head_select([_dot(wst_ref[h], ds16) for h in range(n_heads)], head_dim)
            dg_ref[...] += _fold8(dvn * xhat)
            dbeta_ref[...] += _fold8(dvn)
            dgv = _ln_bwd(dvn, xhat, rstd, g_ref[...])
            dzv_ref[rows, :] = (dgv * _gelu_grad(zv)).astype(BF16)

    return pl.pallas_call(
        body, name=name,
        out_shape=(jax.ShapeDtypeStruct((t, w), BF16), jax.ShapeDtypeStruct((t, w), BF16),
                   jax.ShapeDtypeStruct(ws.shape, F32), jax.ShapeDtypeStruct((CHUNK, w), F32),
                   jax.ShapeDtypeStruct((8, w), F32), jax.ShapeDtypeStruct((8, w), F32)),
        grid=(t // tr,),
        in_specs=[_rows(tr, w, 1), _rows(tr, w, 2), _rows(tr, w), _whole((1, w)), _whole((1, w)),
                  _whole(ws.shape), _whole(ws.shape), _whole(bs_cols.shape)],
        out_specs=(_rows(tr, w), _rows(tr, w), _whole(ws.shape), _whole((CHUNK, w)),
                   _whole((8, w)), _whole((8, w))),
        compiler_params=_params())(proj, proj, dout, ln_g, ln_b, ws, ws_t, bs_cols)


def _merge_fwd(s5out, gm, proj, up_a, up_b, name):
    t, w = s5out.shape
    d = up_a.shape[1]
    assert d == 2 * w
    tr = _pick(t, 256, 8)

    def body(s_ref, gm_ref, ga0, ga1, gb0, gb1, ua_ref, ub_ref, m_ref, ya_ref, yb_ref):
        ya = _dot(s_ref[...], ua_ref[...])
        yb = _dot(gm_ref[...], ub_ref[...])
        ya_ref[...] = ya
        yb_ref[...] = yb
        for half, (ga, gb) in enumerate(((ga0, gb0), (ga1, gb1))):
            cols = slice(half * w, (half + 1) * w)
            m_ref[:, cols] = (_sigmoid(ga[...]) * ya[:, cols] + _sigmoid(gb[...]) * yb[:, cols]).astype(BF16)

    return pl.pallas_call(
        body, name=name,
        out_shape=(jax.ShapeDtypeStruct((t, d), BF16), jax.ShapeDtypeStruct((t, d), F32),
                   jax.ShapeDtypeStruct((t, d), F32)),
        grid=(t // tr,),
        in_specs=[_rows(tr, w), _rows(tr, w), _rows(tr, w, 3), _rows(tr, w, 4), _rows(tr, w, 5),
                  _rows(tr, w, 6), _whole(up_a.shape), _whole(up_b.shape)],
        out_specs=(_rows(tr, d), _rows(tr, d), _rows(tr, d)),
        compiler_params=_params())(s5out, gm, proj, proj, proj, proj, up_a, up_b)


def _merge_bwd(dm, ya, yb, s5out, gm, proj, up_a, up_b, name):
    t, d = dm.shape
    w = d // 2
    tr = _pick(t, 256, 8)

    def body(dm_ref, ya_ref, yb_ref, s_ref, gm_ref, ga0, ga1, gb0, gb1, ua_ref, ub_ref,
             dga_ref, dgb_ref, ds_ref, dgm_ref, dua_ref, dub_ref):
        dm_v, ya, yb = dm_ref[...], ya_ref[...], yb_ref[...]
        dya, dyb = [], []
        for half, (ga, gb) in enumerate(((ga0, gb0), (ga1, gb1))):
            cols = slice(half * w, (half + 1) * w)
            sa, sb = _sigmoid(ga[...]), _sigmoid(gb[...])
            dmh = dm_v[:, cols]
            dga_ref[:, cols] = (dmh * ya[:, cols] * sa * (1.0 - sa)).astype(BF16)
            dgb_ref[:, cols] = (dmh * yb[:, cols] * sb * (1.0 - sb)).astype(BF16)
            dya.append((dmh * sa).astype(BF16))
            dyb.append((dmh * sb).astype(BF16))
        dya = jnp.concatenate(dya, axis=1)
        dyb = jnp.concatenate(dyb, axis=1)
        ds_ref[...] = _dot(dya, ua_ref[...], tb=True)
        dgm_ref[...] = _dot(dyb, ub_ref[...], tb=True)

        @pl.when(pl.program_id(0) == 0)
        def _():
            dua_ref[...] = jnp.zeros_like(dua_ref)
            dub_ref[...] = jnp.zeros_like(dub_ref)

        dua_ref[...] += _dot(s_ref[...], dya, ta=True)
        dub_ref[...] += _dot(gm_ref[...], dyb, ta=True)

    return pl.pallas_call(
        body, name=name,
        out_shape=(jax.ShapeDtypeStruct((t, d), BF16), jax.ShapeDtypeStruct((t, d), BF16),
                   jax.ShapeDtypeStruct((t, w), F32), jax.ShapeDtypeStruct((t, w), F32),
                   jax.ShapeDtypeStruct(up_a.shape, F32), jax.ShapeDtypeStruct(up_b.shape, F32)),
        grid=(t // tr,),
        in_specs=[_rows(tr, d), _rows(tr, d), _rows(tr, d), _rows(tr, w), _rows(tr, w),
                  _rows(tr, w, 3), _rows(tr, w, 4), _rows(tr, w, 5), _rows(tr, w, 6),
                  _whole(up_a.shape), _whole(up_b.shape)],
        out_specs=(_rows(tr, d), _rows(tr, d), _rows(tr, w), _rows(tr, w), _whole(up_a.shape),
                   _whole(up_b.shape)),
        compiler_params=_params())(dm, ya, yb, s5out, gm, proj, proj, proj, proj, up_a, up_b)


def _head(x3, p, target, w_gate, w_proj, name):
    t, d = x3.shape
    pd = p.shape[1]
    tr = _pick(t, 256, 8)
    nb = t // tr

    def body(x_ref, p_ref, t_ref, wg_ref, wp_ref, loss_ref, dx_ref, dwg_ref, dwp_ref):
        xv = x_ref[...]
        sg = _sigmoid(_dot(xv, wg_ref[...]))
        pp = _dot(p_ref[...], wp_ref[...])
        err = xv + sg * pp - t_ref[...]
        loss_ref[...] = jnp.full((8, 128), 0.5 * jnp.sum(jnp.mean(err * err, axis=-1)), F32)
        dout = err * (1.0 / d)
        dgpre = dout * pp * sg * (1.0 - sg)
        dpp = dout * sg
        dx_ref[...] = dout + _dot(dgpre, wg_ref[...], tb=True)

        @pl.when(pl.program_id(0) == 0)
        def _():
            dwg_ref[...] = jnp.zeros_like(dwg_ref)
            dwp_ref[...] = jnp.zeros_like(dwp_ref)

        dwg_ref[...] += _dot(xv, dgpre, ta=True)
        dwp_ref[...] += _dot(p_ref[...], dpp, ta=True)

    return pl.pallas_call(
        body, name=name,
        out_shape=(jax.ShapeDtypeStruct((nb * 8, 128), F32), jax.ShapeDtypeStruct((t, d), F32),
                   jax.ShapeDtypeStruct((d, d), F32), jax.ShapeDtypeStruct((pd, d), F32)),
        grid=(nb,),
        in_specs=[_rows(tr, d), _rows(tr, pd), _rows(tr, d), _whole((d, d)), _whole((pd, d))],
        out_specs=(pl.BlockSpec((8, 128), lambda i: (i, 0)), _rows(tr, d), _whole((d, d)),
                   _whole((pd, d))),
        compiler_params=_params())(x3, p, target, w_gate, w_proj)


_HBM = pl.BlockSpec(memory_space=pltpu.HBM)


def _all_gather(shards, name):
    n = len(shards)

    def body(*refs):
        x_refs, out_refs = refs[:n], refs[n:2 * n]
        send_sems, recv_sems, local_sems = refs[2 * n:]
        x, y, c = lax.axis_index("x"), lax.axis_index("y"), lax.axis_index("c")
        me, sibling = (x, y, c), (x, y, 1 - c)
        chips = [(1 - x, y), (x, 1 - y), (1 - x, 1 - y)]

        def copy(a, k, block, to, own=False):
            slot = out_refs[a].at[4 * block[0] + 2 * block[1] + block[2]]
            return pltpu.make_async_remote_copy(
                src_ref=x_refs[a] if own else slot, dst_ref=slot,
                send_sem=send_sems.at[7 * a + k], recv_sem=recv_sems.at[7 * a + k],
                device_id=to, device_id_type=pl.DeviceIdType.MESH)

        mine = [pltpu.make_async_copy(x_refs[a], out_refs[a].at[4 * x + 2 * y + c], local_sems.at[a])
                for a in range(n)]
        sends = []
        for a in range(n):
            mine[a].start()
            first = [copy(a, 0, me, sibling, own=True)]
            first += [copy(a, 1 + j, me, (*chip, c), own=True) for j, chip in enumerate(chips)]
            for cp in first:
                cp.start()
            sends += first
        for a in range(n):
            for j, chip in enumerate(chips):
                copy(a, 1 + j, (*chip, c), me).wait_recv()
                passed = copy(a, 4 + j, (*chip, c), sibling)
                passed.start()
                sends.append(passed)
        for a in range(n):
            copy(a, 0, sibling, me).wait_recv()
            for j, chip in enumerate(chips):
                copy(a, 4 + j, (*chip, 1 - c), me).wait_recv()
        for cp in sends:
            cp.wait_send()
        for cp in mine:
            cp.wait()

    return pl.pallas_call(
        body, name=name,
        out_shape=tuple(jax.ShapeDtypeStruct((N_DEV,) + s.shape, s.dtype) for s in shards),
        in_specs=[_HBM] * n, out_specs=tuple([_HBM] * n),
        scratch_shapes=[pltpu.SemaphoreType.DMA((7 * n,)), pltpu.SemaphoreType.DMA((7 * n,)),
                        pltpu.SemaphoreType.DMA((n,))],
    )(*shards)


_SEM = pl.BlockSpec(memory_space=pltpu.SEMAPHORE)
_ANY = pl.BlockSpec(memory_space=pl.ANY)
_DATAFLOW = pltpu.SideEffectType.DATAFLOW_SIDE_EFFECTING


def _peer(k, x, y, c):
    return (1 - x if k & 4 else x, 1 - y if k & 2 else y, 1 - c if k & 1 else c)


def _exchange_start(sends, after, name):
    n = len(sends)
    me = 4 * lax.axis_index("x") + 2 * lax.axis_index("y") + lax.axis_index("c")
    lands = []
    for s in sends:
        own = s[0] if s.shape[0] == 1 else lax.dynamic_index_in_dim(s, me, 0, keepdims=False)
        land = jnp.zeros((N_DEV,) + s.shape[1:], s.dtype)
        lands.append(lax.dynamic_update_index_in_dim(land, own, me, 0))

    def body(*refs):
        s_refs, l_refs = refs[:n], refs[n:2 * n]
        send_sems, recv_sems, token = refs[2 * n + 1], refs[2 * n + 2], refs[-1]
        x, y, c = lax.axis_index("x"), lax.axis_index("y"), lax.axis_index("c")
        for a in range(n):
            same = sends[a].shape[0] == 1
            for k in range(1, N_DEV):
                px, py, pc = _peer(k, x, y, c)
                pltpu.make_async_remote_copy(
                    src_ref=s_refs[a].at[0 if same else 4 * px + 2 * py + pc],
                    dst_ref=l_refs[a].at[4 * x + 2 * y + c],
                    send_sem=send_sems.at[7 * a + k - 1], recv_sem=recv_sems.at[7 * a + k - 1],
                    device_id=(px, py, pc), device_id_type=pl.DeviceIdType.MESH).start()
        token[...] = jnp.zeros_like(token)

    outs = pl.pallas_call(
        body, name=name,
        out_shape=(pltpu.SemaphoreType.DMA((7 * n,)), pltpu.SemaphoreType.DMA((7 * n,)),
                   *[pltpu.HBM(s.shape, s.dtype) for s in sends],
                   *[pltpu.HBM(l.shape, l.dtype) for l in lands],
                   jax.ShapeDtypeStruct((8, 128), F32)),
        in_specs=[_HBM] * (2 * n) + [_ANY],
        out_specs=(_SEM, _SEM, *([_HBM] * (2 * n)), pl.BlockSpec(memory_space=pltpu.VMEM)),
        input_output_aliases={i: 2 + i for i in range(2 * n)},
        compiler_params=pltpu.CompilerParams(has_side_effects=_DATAFLOW),
    )(*[pltpu.with_memory_space_constraint(v, pltpu.HBM) for v in list(sends) + lands], after)
    return (outs[0], outs[1], list(outs[2:2 + n]), list(outs[2 + n:2 + 2 * n])), outs[-1]


def _exchange_wait(handle, after, name):
    send_sems, recv_sems, sends, lands = handle
    n = len(sends)

    def body(*refs):
        s_refs, l_refs = refs[:n], refs[n:2 * n]
        send_sems, recv_sems = refs[2 * n], refs[2 * n + 1]
        x, y, c = lax.axis_index("x"), lax.axis_index("y"), lax.axis_index("c")
        for a in range(n):
            for k in range(1, N_DEV):
                copy = pltpu.make_async_remote_copy(
                    src_ref=s_refs[a].at[0], dst_ref=l_refs[a].at[0],
                    send_sem=send_sems.at[7 * a + k - 1], recv_sem=recv_sems.at[7 * a + k - 1],
                    device_id=_peer(k, x, y, c), device_id_type=pl.DeviceIdType.MESH)
                copy.wait_send()
                copy.wait_recv()

    outs = pl.pallas_call(
        body, name=name,
        out_shape=(*[pltpu.HBM(s.shape, s.dtype) for s in sends], *[pltpu.HBM(l.shape, l.dtype) for l in lands]),
        in_specs=[_HBM] * (2 * n) + [_SEM, _SEM, _ANY],
        out_specs=tuple([_HBM] * (2 * n)),
        input_output_aliases={i: i for i in range(2 * n)},
        compiler_params=pltpu.CompilerParams(has_side_effects=_DATAFLOW),
    )(*sends, *lands, send_sems, recv_sems, after)
    return list(outs[n:])


def _adamw(recv, w, m, v, name):
    n, rows, width = recv.shape
    tr = _pick(rows, max(8, ADAM_BLOCK_BYTES // (n * width * recv.dtype.itemsize)), 8)
    c_m = 1.0 - ADAM_B1 ** ADAM_STEP
    c_v = 1.0 - ADAM_B2 ** ADAM_STEP

    def body(r_ref, w_ref, m_ref, v_ref, g_ref, d_ref, nm_ref, nv_ref):
        g = r_ref[0].astype(F32)
        for i in range(1, n):
            g = g + r_ref[i].astype(F32)
        m2 = ADAM_B1 * m_ref[...] + (1.0 - ADAM_B1) * g
        v2 = ADAM_B2 * v_ref[...] + (1.0 - ADAM_B2) * (g * g)
        m_hat = m2 / c_m
        v_hat = v2 / c_v
        g_ref[...] = g
        d_ref[...] = -ADAM_LR * (m_hat / (jnp.sqrt(v_hat) + ADAM_EPS) + ADAM_WD * w_ref[...])
        nm_ref[...] = m2
        nv_ref[...] = v2

    blk = _rows(tr, width)
    shp = jax.ShapeDtypeStruct((rows, width), F32)
    return pl.pallas_call(
        body, name=name, out_shape=(shp, shp, shp, shp), grid=(rows // tr,),
        in_specs=[pl.BlockSpec((n, tr, width), lambda i: (0, i, 0)), blk, blk, blk],
        out_specs=(blk, blk, blk, blk), compiler_params=_params())(recv, w, m, v)


def _join_cols(gathered):
    n, r, c = gathered.shape
    return gathered.transpose(1, 0, 2).reshape(r, n * c)


def _split_cols(full):
    r, c = full.shape
    return full.reshape(r, N_DEV, c // N_DEV).transpose(1, 0, 2)


def _small_rows(size):
    return -(-size // (8 * PACK_W)) * 8


def _pack_small(parts, names):
    rows = []
    for n in names:
        flat = parts[n].reshape(-1)
        r = _small_rows(flat.shape[0])
        rows.append(jnp.pad(flat, (0, r * PACK_W - flat.shape[0])).reshape(r, PACK_W))
    return jnp.concatenate(rows, axis=0)


def _unpack_small(packed, shapes, names):
    out, r0 = {}, 0
    for n in names:
        size = math.prod(shapes[n])
        rows = _small_rows(size)
        out[n] = packed[r0:r0 + rows].reshape(-1)[:size].reshape(shapes[n])
        r0 += rows
    return out


def _discretise(lam_re, lam_im, log_dt, b_re, b_im):
    dt = jnp.exp(log_dt)[:, None]
    mag = jnp.exp(lam_re * dt)
    ab_re = mag * jnp.cos(lam_im * dt)
    ab_im = mag * jnp.sin(lam_im * dt)
    nr = ab_re - 1.0
    ni = ab_im
    den = lam_re * lam_re + lam_im * lam_im
    coef_re = ((nr * lam_re + ni * lam_im) / den)[..., None]
    coef_im = ((ni * lam_re - nr * lam_im) / den)[..., None]
    bb_re = coef_re * b_re - coef_im * b_im
    bb_im = coef_re * b_im + coef_im * b_re
    return ab_re, ab_im, bb_re, bb_im


def _same_group(g):
    return jnp.eye(g, dtype=bool)[:, None, :, None]


def _block_diag_in(bb):
    g, p, i = bb.shape
    placed = jnp.where(_same_group(g), bb.transpose(0, 2, 1)[:, :, None, :], 0.0)
    return placed.reshape(g * i, g * p)


def _block_diag_in_take(dense, g, p, i):
    blocks = jnp.where(_same_group(g), dense.reshape(g, i, g, p), 0.0).sum(axis=2)
    return blocks.transpose(0, 2, 1)


def _block_diag_out(cc):
    g, i, p = cc.shape
    placed = jnp.where(_same_group(g), cc.transpose(0, 2, 1)[:, :, None, :], 0.0)
    return placed.reshape(g * p, g * i)


def _block_diag_out_take(dense, g, i, p):
    blocks = jnp.where(_same_group(g), dense.reshape(g, p, g, i), 0.0).sum(axis=2)
    return blocks.transpose(0, 2, 1)


def _perm_rows(z, n_seq):
    t, w = z.shape
    steps = t // n_seq // N_SEG
    return z.reshape(n_seq, N_SEG, steps, w).transpose(0, 2, 1, 3).reshape(t, w)


def _unperm_rows(z, n_seq):
    t, w = z.shape
    steps = t // n_seq // N_SEG
    return z.reshape(n_seq, steps, N_SEG, w).transpose(0, 2, 1, 3).reshape(t, w)


def kernel(*args):
    assert len(args) == len(INPUT_NAMES)
    inp = dict(zip(INPUT_NAMES, args))
    depth = inp["ffn1_w_in"].shape[0]
    assert depth == 1
    alpha = (2.0 * depth) ** 0.25

    n_seq, seq, d_model = inp["x"].shape
    t = n_seq * seq
    x = inp["x"].reshape(t, d_model)
    x16 = x.astype(BF16)
    p16 = inp["p"][0].reshape(t, -1).astype(BF16)
    target = inp["loss_target"].reshape(t, d_model)
    wts = {n: inp[n][0] for n in WEIGHTS}
    mom = {n: inp["m_" + n][0] for n in WEIGHTS}
    var = {n: inp["v_" + n][0] for n in WEIGHTS}

    full = {}

    def place(n, g):
        if n in ("ffn1_w_in", "ffn2_w_in"):
            full[n] = g.reshape((2, N_DEV // 2) + g.shape[1:])
        elif n in ("ffn1_w_out", "ffn2_w_out"):
            full[n] = g.reshape(N_DEV // 2, 2 * g.shape[1], g.shape[2])
        elif n in COL_SHARDED:
            full[n] = _join_cols(g)
        else:
            full[n] = g.reshape(N_DEV * g.shape[1], g.shape[2])

    w16 = {n: wts[n].astype(BF16) for n in SHARDED}
    for n, g in zip(GATHER_FIRST, _all_gather([w16[n] for n in GATHER_FIRST], "gather_ffn1")):
        place(n, g)
    gather_mix, gather_mix_token = _exchange_start([w16[n][None] for n in GATHER_MIX], full["ffn1_w_out"],
                                                   "gather_mix_start")

    def row(v):
        return v.reshape(1, -1)

    n_groups, n_state = wts["ssm_lambda_re"].shape
    n_ch = wts["ssm_b_re"].shape[2]
    disc_in = (wts["ssm_lambda_re"], wts["ssm_lambda_im"], wts["ssm_log_dt"], wts["ssm_b_re"],
               wts["ssm_b_im"])
    (ab_re, ab_im, bb_re, bb_im), disc_vjp = jax.vjp(_discretise, *disc_in)
    a_re, a_im = row(ab_re), row(ab_im)
    b_dense_re = _block_diag_in(bb_re).astype(BF16)
    b_dense_im = _block_diag_in(bb_im).astype(BF16)
    c_dense_re = _block_diag_out(wts["ssm_c_re"]).astype(BF16)
    c_dense_im = _block_diag_out(-wts["ssm_c_im"]).astype(BF16)

    ws = wts["gmlp_w_s"]
    n_heads = ws.shape[0]
    d_gmlp = wts["gmlp_ln_g"].shape[0]
    causal = jnp.tril(jnp.ones((CHUNK, CHUNK), dtype=bool))
    ws_masked = jnp.where(causal[None], ws, 0.0).astype(BF16)
    ws_masked_t = ws_masked.transpose(0, 2, 1)
    bs_cols = jnp.repeat(wts["gmlp_b_s"].T, d_gmlp // n_heads, axis=1)
    d_ssm = n_groups * n_ch
    assert d_ssm == d_gmlp and d_model == 2 * d_ssm

    h1, a1 = _ffn_in(x16, full["ffn1_w_in"], "ffn1_in", token=gather_mix_token)
    r1, x1, x1_16 = _ffn_out_ln(a1, full["ffn1_w_out"], x, row(wts["ln1_g"]), row(wts["ln1_b"]), alpha,
                                "ffn1_out_ln")

    for n, g in zip(GATHER_MIX, _exchange_wait(gather_mix, x1_16, "gather_mix_wait")):
        place(n, g)
    gather_last, token = _exchange_start([w16[n][None] for n in GATHER_LAST], full["mix_w_in"],
                                         "gather_ffn2_start")
    proj = _mm(x1_16, full["mix_w_in"], name="mix_in", token=token)
    za_p = _perm_rows(proj[:, :d_ssm], n_seq)
    za_p16 = za_p.astype(BF16)
    bu_re = _mm(za_p16, b_dense_re, name="s5_bu_re")
    bu_im = _mm(za_p16, b_dense_im, name="s5_bu_im")
    h_re, h_im = _scan_fwd(bu_re, bu_im, a_re, a_im, n_seq, "s5_scan")
    yssm = _mm(h_re, c_dense_re, name="s5_y_re")
    yssm = _mm(h_im, c_dense_im, name="s5_y_im", add=yssm)
    s5out_p = _s5_post_fwd(yssm, za_p, row(wts["ssm_d"]), full["ssm_glu_w"], row(wts["ssm_glu_b"]),
                           "s5_post")
    s5out = _unperm_rows(s5out_p, n_seq)
    gm = _gmlp_fwd(proj, row(wts["gmlp_ln_g"]), row(wts["gmlp_ln_b"]), ws_masked, bs_cols, "gmlp")
    m_mix, y_a, y_b = _merge_fwd(s5out, gm, proj, full["up_a"], full["up_b"], "merge")
    mixed = _mm(m_mix, full["mix_w_out"], name="mix_out")
    r2, x2, x2_16 = _resid_ln_fwd(x1, mixed, row(wts["ln2_g"]), row(wts["ln2_b"]), alpha, 1.0, "ln2")

    for n, g in zip(GATHER_LAST, _exchange_wait(gather_last, x2_16, "gather_ffn2_wait")):
        place(n, g)
    h2, a2 = _ffn_in(x2_16, full["ffn2_w_in"], "ffn2_in")
    r3, x3, _ = _ffn_out_ln(a2, full["ffn2_w_out"], x2, row(wts["ln3_g"]), row(wts["ln3_b"]), alpha,
                            "ffn2_out_ln")

    small = {}
    send = {}
    loss_parts, dx3, dw_gate, dw_proj = _head(x3, p16, target, full["ple_w_gate"], full["ple_w_proj"], "head")
    loss = lax.psum(jnp.sum(loss_parts[::8, 0]), ("x", "y", "c"))
    send["ple_w_gate"] = dw_gate.astype(BF16).reshape(N_DEV, -1, d_model)
    send["ple_w_proj"] = _split_cols(dw_proj.astype(BF16))

    dr3, dr3_h, dg, db = _ln_bwd_call(r3, dx3, None, 1.0, row(wts["ln3_g"]), 0.5, "ln3_bwd")
    small["ln3_g"], small["ln3_b"] = dg.sum(0), db.sum(0)
    dh2, dw = _ffn_out_bwd(dr3_h, full["ffn2_w_out"], h2, a2, "ffn2_out_bwd")
    send["ffn2_w_out"] = dw.reshape(N_DEV, -1, d_model)
    dw = _ffn_in_dw(x2_16, dh2, "ffn2_in_dw")
    send["ffn2_w_in"] = dw.reshape((N_DEV,) + dw.shape[2:])
    group1 = ("ffn2_w_in", "ffn2_w_out", "ple_w_gate", "ple_w_proj")
    exchange1, token = _exchange_start([send[n] for n in group1], dh2, "grad_exchange1_start")
    dx2_f = _ffn_in_dx(dh2, full["ffn2_w_in"], None, 1.0, "ffn2_in_dx", token=token)

    dr2, dr2_h, dg, db = _ln_bwd_call(r2, dx2_f, dr3, alpha, row(wts["ln2_g"]), 1.0, "ln2_bwd")
    small["ln2_g"], small["ln2_b"] = dg.sum(0), db.sum(0)
    dm = _mm(dr2_h, full["mix_w_out"], tb=True, name="mix_out_dx")
    send["mix_w_out"] = _mm(m_mix, dr2_h, ta=True, name="mix_out_dw", out_dtype=BF16).reshape(
        N_DEV, -1, d_model)
    dga, dgb, ds5out, dgm, dw_a, dw_b = _merge_bwd(
        dm, y_a, y_b, s5out, gm, proj, full["up_a"], full["up_b"], "merge_bwd")
    send["up_a"] = _split_cols(dw_a.astype(BF16))
    send["up_b"] = _split_cols(dw_b.astype(BF16))

    dzu, dzv, dws, dbs_cols, dg, db = _gmlp_bwd(
        proj, dgm, row(wts["gmlp_ln_g"]), row(wts["gmlp_ln_b"]), ws_masked, ws_masked_t, bs_cols,
        "gmlp_bwd")
    small["gmlp_ln_g"], small["gmlp_ln_b"] = dg.sum(0), db.sum(0)
    small["gmlp_w_s"] = jnp.where(causal[None], dws, 0.0)
    small["gmlp_b_s"] = dbs_cols.reshape(CHUNK, n_heads, -1).sum(-1).T

    ds5out_p = _perm_rows(ds5out, n_seq)
    dy_p, dza_skip, dw_glu, dd, dgb_glu = _s5_post_bwd(
        yssm, za_p, ds5out_p, row(wts["ssm_d"]), full["ssm_glu_w"], row(wts["ssm_glu_b"]),
        "s5_post_bwd")
    send["ssm_glu_w"] = dw_glu.astype(BF16).reshape(N_DEV, -1, d_ssm)
    small["ssm_d"], small["ssm_glu_b"] = dd.sum(0), dgb_glu.sum(0)
    g_re = _mm(dy_p, c_dense_re, tb=True, name="s5_y_re_dx")
    g_im = _mm(dy_p, c_dense_im, tb=True, name="s5_y_im_dx")
    dc_re = _mm(h_re, dy_p, ta=True, name="s5_y_re_dw")
    dc_im = _mm(h_im, dy_p, ta=True, name="s5_y_im_dw")
    small["ssm_c_re"] = _block_diag_out_take(dc_re, n_groups, n_ch, n_state)
    small["ssm_c_im"] = -_block_diag_out_take(dc_im, n_groups, n_ch, n_state)
    lam_re, lam_im, da_re, da_im = _scan_bwd(g_re, g_im, h_re, h_im, a_re, a_im, n_seq, "s5_scan_bwd")
    dza_p = _mm(lam_re, b_dense_re, tb=True, name="s5_bu_re_dx", add=dza_skip)
    dza_p = _mm(lam_im, b_dense_im, tb=True, name="s5_bu_im_dx", add=dza_p, out_dtype=BF16)
    dbb_re = _block_diag_in_take(_mm(za_p16, lam_re, ta=True, name="s5_bu_re_dw"), n_groups, n_state, n_ch)
    dbb_im = _block_diag_in_take(_mm(za_p16, lam_im, ta=True, name="s5_bu_im_dw"), n_groups, n_state, n_ch)
    dab_re = da_re.sum(0).reshape(n_groups, n_state)
    dab_im = da_im.sum(0).reshape(n_groups, n_state)
    (small["ssm_lambda_re"], small["ssm_lambda_im"], small["ssm_log_dt"], small["ssm_b_re"],
     small["ssm_b_im"]) = disc_vjp((dab_re, dab_im, dbb_re, dbb_im))

    dproj = jnp.concatenate([_unperm_rows(dza_p, n_seq), dzu, dzv, dga, dgb], axis=1)
    send["mix_w_in"] = _split_cols(_mm(x1_16, dproj, ta=True, name="mix_in_dw", out_dtype=BF16))
    group2 = ("mix_w_in", "mix_w_out", "up_a", "up_b", "ssm_glu_w")
    exchange2, token = _exchange_start(
        [send[n] for n in group2] + [_pack_small(small, SMALL_EARLY)[None]], dproj, "grad_exchange2_start")
    dx1_f = _mm(dproj, full["mix_w_in"], tb=True, name="mix_in_dx", token=token)

    dr1, dr1_h, dg, db = _ln_bwd_call(r1, dx1_f, dr2, alpha, row(wts["ln1_g"]), 0.5, "ln1_bwd")
    small["ln1_g"], small["ln1_b"] = dg.sum(0), db.sum(0)
    dh1, dw = _ffn_out_bwd(dr1_h, full["ffn1_w_out"], h1, a1, "ffn1_out_bwd")
    send["ffn1_w_out"] = dw.reshape(N_DEV, -1, d_model)
    exchange3, token = _exchange_start([send["ffn1_w_out"], _pack_small(small, SMALL_LATE)[None]], dh1,
                                       "grad_exchange3_start")
    dw = _ffn_in_dw(x16, dh1, "ffn1_in_dw", token=token)
    send["ffn1_w_in"] = dw.reshape((N_DEV,) + dw.shape[2:])
    exchange4, token = _exchange_start([send["ffn1_w_in"]], dh1, "grad_exchange4_start")
    grad_x = _ffn_in_dx(dh1, full["ffn1_w_in"], dr1, alpha, "ffn1_in_dx", token=token)

    results = {}

    def update(names, recv, prefix):
        for n, r in zip(names, recv):
            results[n] = _adamw(r, wts[n], mom[n], var[n], prefix + n)

    def update_small(names, recv, name):
        packed = _adamw(recv, _pack_small(wts, names), _pack_small(mom, names), _pack_small(var, names), name)
        shapes = {n: wts[n].shape for n in names}
        unpacked = [_unpack_small(pk, shapes, names) for pk in packed]
        for n in names:
            results[n] = tuple(u[n] for u in unpacked)

    update(group1, _exchange_wait(exchange1, grad_x, "grad_exchange1_wait"), "adamw_")
    recv = _exchange_wait(exchange2, results[group1[-1]][0], "grad_exchange2_wait")
    update(group2, recv[:-1], "adamw_")
    update_small(SMALL_EARLY, recv[-1], "adamw_small")
    recv = _exchange_wait(exchange3, results[SMALL_EARLY[-1]][0], "grad_exchange3_wait")
    update(("ffn1_w_out",), recv[:1], "adamw_")
    update_small(SMALL_LATE, recv[1], "adamw_ln1")
    recv = _exchange_wait(exchange4, results["ln1_b"][0], "grad_exchange4_wait")
    update(("ffn1_w_in",), recv, "adamw_")

    outs = [loss, grad_x.reshape(n_seq, seq, d_model)]
    for k in range(4):
        outs += [results[n][k][None] for n in WEIGHTS]
    return tuple(outs)
```

```python
import functools
import math

import jax
import jax.numpy as jnp
from jax import lax
from jax.experimental import pallas as pl
from jax.experimental.pallas import tpu as pltpu

F32 = jnp.float32
BF16 = jnp.bfloat16

N_DEV = 8
LN_EPS = 1e-5
CHUNK = 128
N_SEG = 8
PACK_W = 1024
VMEM_LIMIT_BYTES = 56 * 1024 * 1024
MM_OPERAND_BLOCK_BYTES = 8 * 1024 * 1024
ADAM_BLOCK_BYTES = 6 * 1024 * 1024

ADAM_LR = 0.001
ADAM_B1 = 0.9
ADAM_B2 = 0.999
ADAM_EPS = 1e-08
ADAM_WD = 0.01
ADAM_STEP = 10

COL_SHARDED = ("ffn1_w_in", "mix_w_in", "up_a", "up_b", "ffn2_w_in", "ple_w_proj")
ROW_SHARDED = ("ffn1_w_out", "ssm_glu_w", "mix_w_out", "ffn2_w_out", "ple_w_gate")
SHARDED = ("ffn1_w_in", "ffn1_w_out", "mix_w_in", "ssm_glu_w", "up_a", "up_b", "mix_w_out",
           "ffn2_w_in", "ffn2_w_out", "ple_w_proj", "ple_w_gate")
WEIGHTS = ("ffn1_w_in", "ffn1_w_out", "ln1_g", "ln1_b", "mix_w_in", "ssm_lambda_re", "ssm_lambda_im",
           "ssm_log_dt", "ssm_b_re", "ssm_b_im", "ssm_c_re", "ssm_c_im", "ssm_d", "ssm_glu_w",
           "ssm_glu_b", "gmlp_ln_g", "gmlp_ln_b", "gmlp_w_s", "gmlp_b_s", "up_a", "up_b", "mix_w_out",
           "ln2_g", "ln2_b", "ffn2_w_in", "ffn2_w_out", "ln3_g", "ln3_b", "ple_w_proj", "ple_w_gate")
GATHER_FIRST = ("ffn1_w_in", "ffn1_w_out")
GATHER_MIX = ("mix_w_in", "ssm_glu_w", "up_a", "up_b", "mix_w_out")
GATHER_LAST = ("ffn2_w_in", "ffn2_w_out", "ple_w_proj", "ple_w_gate")
SMALL = tuple(n for n in WEIGHTS if n not in SHARDED)
SMALL_LATE = ("ln1_g", "ln1_b")
SMALL_EARLY = tuple(n for n in SMALL if n not in SMALL_LATE)
INPUT_NAMES = ("x", "p") + WEIGHTS + ("loss_target",) + tuple("m_" + n for n in WEIGHTS) + tuple(
    "v_" + n for n in WEIGHTS)


def _pick(dim, pref, mult=128):
    if dim <= pref:
        return dim
    d = (pref // mult) * mult
    while d >= mult:
        if dim % d == 0:
            return d
        d -= mult
    return dim


def _params(n_axes=1):
    return pltpu.CompilerParams(dimension_semantics=("arbitrary",) * n_axes,
                                vmem_limit_bytes=VMEM_LIMIT_BYTES)


def _sigmoid(v):
    return 1.0 / (1.0 + jnp.exp(-v))


_GELU_C = math.sqrt(2.0 / math.pi)


def _gelu(v):
    return 0.5 * v * (1.0 + jnp.tanh(_GELU_C * (v + 0.044715 * (v * v * v))))


def _gelu_grad(v):
    t = jnp.tanh(_GELU_C * (v + 0.044715 * (v * v * v)))
    return 0.5 * (1.0 + t) + 0.5 * v * (1.0 - t * t) * (_GELU_C * (1.0 + 3.0 * 0.044715 * v * v))


def _ln_stats(r):
    mu = jnp.mean(r, axis=-1, keepdims=True)
    xc = r - mu
    var = jnp.mean(xc * xc, axis=-1, keepdims=True)
    rstd = lax.rsqrt(var + LN_EPS)
    return xc * rstd, rstd


def _ln_bwd(dy, xhat, rstd, g):
    dxh = dy * g
    m1 = jnp.mean(dxh, axis=-1, keepdims=True)
    m2 = jnp.mean(dxh * xhat, axis=-1, keepdims=True)
    return rstd * (dxh - m1 - xhat * m2)


def _fold8(v):
    rows, w = v.shape
    return jnp.sum(v.reshape(rows // 8, 8, w), axis=0)


def _dot(a, b, ta=False, tb=False):
    dn = (((0 if ta else 1,), (1 if tb else 0,)), ((), ()))
    return lax.dot_general(a.astype(BF16), b.astype(BF16), dn, preferred_element_type=F32)


_TOKEN = pl.BlockSpec((8, 128), lambda *_: (0, 0))


def _mm(a, b, *, name, ta=False, tb=False, out_dtype=F32, add=None, add_scale=1.0, token=None,
        tm=2048, tn=512, tk=4096):
    m_dim = a.shape[1] if ta else a.shape[0]
    k_dim = a.shape[0] if ta else a.shape[1]
    n_dim = b.shape[0] if tb else b.shape[1]
    assert (b.shape[1] if tb else b.shape[0]) == k_dim, (name, a.shape, b.shape)
    tk = _pick(k_dim, tk)
    tm = min(tm, max(256, MM_OPERAND_BLOCK_BYTES // (tk * a.dtype.itemsize)))
    tm, tn = _pick(m_dim, tm), _pick(n_dim, tn)
    nk = k_dim // tk
    has_add = add is not None

    def finish(r, add_ref, o_ref):
        if has_add:
            r = r + add_scale * add_ref[...].astype(F32)
        o_ref[...] = r.astype(out_dtype)

    def body(*refs):
        a_ref, b_ref = refs[:2]
        add_ref = refs[2] if has_add else None
        o_ref = refs[n_in]
        if nk == 1:
            finish(_dot(a_ref[...], b_ref[...], ta, tb), add_ref, o_ref)
            return
        acc_ref = refs[-1]
        k = pl.program_id(2)

        @pl.when(k == 0)
        def _():
            acc_ref[...] = jnp.zeros_like(acc_ref)

        acc_ref[...] += _dot(a_ref[...], b_ref[...], ta, tb)

        @pl.when(k == nk - 1)
        def _():
            finish(acc_ref[...], add_ref, o_ref)

    a_spec = (pl.BlockSpec((tk, tm), lambda i, j, k: (k, i)) if ta
              else pl.BlockSpec((tm, tk), lambda i, j, k: (i, k)))
    b_spec = (pl.BlockSpec((tn, tk), lambda i, j, k: (j, k)) if tb
              else pl.BlockSpec((tk, tn), lambda i, j, k: (k, j)))
    in_specs = [a_spec, b_spec]
    operands = [a, b]
    if has_add:
        in_specs.append(pl.BlockSpec((tm, tn), lambda i, j, k: (i, j)))
        operands.append(add)
    if token is not None:
        in_specs.append(_TOKEN)
        operands.append(token)
    n_in = len(operands)
    return pl.pallas_call(
        body, name=name,
        out_shape=jax.ShapeDtypeStruct((m_dim, n_dim), out_dtype),
        grid=(m_dim // tm, n_dim // tn, nk),
        in_specs=in_specs,
        out_specs=pl.BlockSpec((tm, tn), lambda i, j, k: (i, j)),
        scratch_shapes=[pltpu.VMEM((tm, tn), F32)] if nk > 1 else [],
        compiler_params=_params(3),
    )(*operands)


def _to_bf16(arrays, name):
    n = len(arrays)

    def body(*refs):
        for src, dst in zip(refs[:n], refs[n:]):
            dst[...] = src[...].astype(BF16)

    vmem = pl.BlockSpec(memory_space=pltpu.VMEM)
    return pl.pallas_call(
        body, name=name, out_shape=tuple(jax.ShapeDtypeStruct(a.shape, BF16) for a in arrays),
        in_specs=[vmem] * n, out_specs=tuple([vmem] * n),
        compiler_params=pltpu.CompilerParams(vmem_limit_bytes=VMEM_LIMIT_BYTES))(*arrays)


def _rows(tr, w, cb=0):
    return pl.BlockSpec((tr, w), lambda i: (i, cb))


def _whole(shape):
    nd = len(shape)
    return pl.BlockSpec(tuple(shape), lambda i: (0,) * nd)


def _ffn_in(x16, w_in, name, token=None):
    t, d = x16.shape
    _, nb, _, fb = w_in.shape
    tm = _pick(t, 1024, 8)
    extra = [] if token is None else [token]

    def body(*refs):
        x_ref, wg_ref, wu_ref = refs[:3]
        h_ref, a_ref = refs[-2:]
        xv = x_ref[...]
        g = _dot(xv, wg_ref[0, 0])
        u = _dot(xv, wu_ref[0, 0])
        h_ref[0, 0] = g
        h_ref[0, 1] = u
        a_ref[0] = (g * _sigmoid(g) * u).astype(BF16)

    return pl.pallas_call(
        body, name=name,
        out_shape=(jax.ShapeDtypeStruct((nb, 2, t, fb), F32), jax.ShapeDtypeStruct((nb, t, fb), BF16)),
        grid=(t // tm, nb),
        in_specs=[pl.BlockSpec((tm, d), lambda i, j: (i, 0)),
                  pl.BlockSpec((1, 1, d, fb), lambda i, j: (0, j, 0, 0)),
                  pl.BlockSpec((1, 1, d, fb), lambda i, j: (1, j, 0, 0))] + [_TOKEN] * len(extra),
        out_specs=(pl.BlockSpec((1, 2, tm, fb), lambda i, j: (j, 0, i, 0)),
                   pl.BlockSpec((1, tm, fb), lambda i, j: (j, i, 0))),
        compiler_params=_params(2))(x16, w_in, w_in, *extra)


def _ffn_out_ln(a, w_out, xin, g, b, alpha, name):
    nb, t, fb = a.shape
    d = w_out.shape[2]
    tm = _pick(t, 256, 8)

    def body(a_ref, w_ref, x_ref, g_ref, b_ref, r_ref, y_ref, y16_ref):
        f = _dot(a_ref[0], w_ref[0])
        for jb in range(1, nb):
            f = f + _dot(a_ref[jb], w_ref[jb])
        r = alpha * x_ref[...] + 0.5 * f
        xhat, _ = _ln_stats(r)
        y = xhat * g_ref[...] + b_ref[...]
        r_ref[...] = r
        y_ref[...] = y
        y16_ref[...] = y.astype(BF16)

    return pl.pallas_call(
        body, name=name,
        out_shape=(jax.ShapeDtypeStruct((t, d), F32), jax.ShapeDtypeStruct((t, d), F32),
                   jax.ShapeDtypeStruct((t, d), BF16)),
        grid=(t // tm,),
        in_specs=[pl.BlockSpec((nb, tm, fb), lambda i: (0, i, 0)), _whole(w_out.shape), _rows(tm, d),
                  _whole((1, d)), _whole((1, d))],
        out_specs=(_rows(tm, d), _rows(tm, d), _rows(tm, d)),
        compiler_params=_params())(a, w_out, xin, g, b)


def _ffn_out_bwd(drs, w_out, h, a, name):
    nb, t, fb = a.shape
    d = w_out.shape[2]
    tm = _pick(t, 1024, 8)
    n_i = t // tm

    def body(dr_ref, w_ref, h_ref, a_ref, dh_ref, dw_ref, acc_ref):
        i = pl.program_id(1)
        dr = dr_ref[...]
        da = _dot(dr, w_ref[0], tb=True)
        g, u = h_ref[0, 0], h_ref[0, 1]
        sg = _sigmoid(g)
        dh_ref[0, 0] = (da * u * (sg * (1.0 + g * (1.0 - sg)))).astype(BF16)
        dh_ref[0, 1] = (da * (g * sg)).astype(BF16)

        @pl.when(i == 0)
        def _():
            acc_ref[...] = jnp.zeros_like(acc_ref)

        acc_ref[...] += _dot(a_ref[0], dr, ta=True)

        @pl.when(i == n_i - 1)
        def _():
            dw_ref[0] = acc_ref[...].astype(BF16)

    return pl.pallas_call(
        body, name=name,
        out_shape=(jax.ShapeDtypeStruct((nb, 2, t, fb), BF16), jax.ShapeDtypeStruct((nb, fb, d), BF16)),
        grid=(nb, n_i),
        in_specs=[pl.BlockSpec((tm, d), lambda j, i: (i, 0)),
                  pl.BlockSpec((1, fb, d), lambda j, i: (j, 0, 0)),
                  pl.BlockSpec((1, 2, tm, fb), lambda j, i: (j, 0, i, 0)),
                  pl.BlockSpec((1, tm, fb), lambda j, i: (j, i, 0))],
        out_specs=(pl.BlockSpec((1, 2, tm, fb), lambda j, i: (j, 0, i, 0)),
                   pl.BlockSpec((1, fb, d), lambda j, i: (j, 0, 0))),
        scratch_shapes=[pltpu.VMEM((fb, d), F32)],
        compiler_params=_params(2))(drs, w_out, h, a)


def _ffn_in_dw(x16, dh, name, token=None):
    t, d = x16.shape
    nb, _, _, fb = dh.shape
    extra = [] if token is None else [token]

    def body(*refs):
        x_ref, dh_ref, o_ref = refs[0], refs[1], refs[-1]
        o_ref[0, 0] = _dot(x_ref[...], dh_ref[0, 0], ta=True).astype(BF16)

    return pl.pallas_call(
        body, name=name, out_shape=jax.ShapeDtypeStruct((2, nb, d, fb), BF16), grid=(nb, 2),
        in_specs=[pl.BlockSpec((t, d), lambda j, s: (0, 0)),
                  pl.BlockSpec((1, 1, t, fb), lambda j, s: (j, s, 0, 0))] + [_TOKEN] * len(extra),
        out_specs=pl.BlockSpec((1, 1, d, fb), lambda j, s: (s, j, 0, 0)),
        compiler_params=_params(2))(x16, dh, *extra)


def _ffn_in_dx(dh, w_in, add, add_scale, name, token=None):
    nb, _, t, fb = dh.shape
    d = w_in.shape[2]
    tm = _pick(t, 512, 8)
    has_add = add is not None
    extra = [] if token is None else [token]

    def body(*refs):
        dh_ref, w_ref = refs[:2]
        o_ref = refs[-1]
        acc = None
        for jb in range(nb):
            for s in range(2):
                part = _dot(dh_ref[jb, s], w_ref[s, jb], tb=True)
                acc = part if acc is None else acc + part
        if has_add:
            acc = acc + add_scale * refs[2][...]
        o_ref[...] = acc

    return pl.pallas_call(
        body, name=name, out_shape=jax.ShapeDtypeStruct((t, d), F32), grid=(t // tm,),
        in_specs=[pl.BlockSpec((nb, 2, tm, fb), lambda i: (0, 0, i, 0)), _whole(w_in.shape)]
        + ([_rows(tm, d)] if has_add else []) + [_TOKEN] * len(extra),
        out_specs=_rows(tm, d),
        compiler_params=_params())(*([dh, w_in] + ([add] if has_add else []) + extra))


def _resid_ln_fwd(xin, f, g, b, alpha, scale, name):
    t, d = xin.shape
    tr = _pick(t, 256, 8)

    def body(x_ref, f_ref, g_ref, b_ref, r_ref, y_ref, y16_ref):
        r = alpha * x_ref[...] + scale * f_ref[...]
        xhat, _ = _ln_stats(r)
        y = xhat * g_ref[...] + b_ref[...]
        r_ref[...] = r
        y_ref[...] = y
        y16_ref[...] = y.astype(BF16)

    return pl.pallas_call(
        body, name=name,
        out_shape=(jax.ShapeDtypeStruct((t, d), F32), jax.ShapeDtypeStruct((t, d), F32),
                   jax.ShapeDtypeStruct((t, d), BF16)),
        grid=(t // tr,),
        in_specs=[_rows(tr, d), _rows(tr, d), _whole((1, d)), _whole((1, d))],
        out_specs=(_rows(tr, d), _rows(tr, d), _rows(tr, d)),
        compiler_params=_params())(xin, f, g, b)


def _ln_bwd_call(r, dy_a, dy_b, b_scale, g, out_scale, name):
    t, d = r.shape
    tr = _pick(t, 256, 8)
    has_b = dy_b is not None

    def body(*refs):
        if has_b:
            r_ref, da_ref, db_ref, g_ref, dr_ref, drs_ref, dg_ref, dbeta_ref = refs
            dy = da_ref[...] + b_scale * db_ref[...]
        else:
            r_ref, da_ref, g_ref, dr_ref, drs_ref, dg_ref, dbeta_ref = refs
            dy = da_ref[...]
        xhat, rstd = _ln_stats(r_ref[...])
        dr = _ln_bwd(dy, xhat, rstd, g_ref[...])
        dr_ref[...] = dr
        drs_ref[...] = (out_scale * dr).astype(BF16)

        @pl.when(pl.program_id(0) == 0)
        def _():
            dg_ref[...] = jnp.zeros_like(dg_ref)
            dbeta_ref[...] = jnp.zeros_like(dbeta_ref)

        dg_ref[...] += _fold8(dy * xhat)
        dbeta_ref[...] += _fold8(dy)

    ins = [r, dy_a] + ([dy_b] if has_b else []) + [g]
    in_specs = [_rows(tr, d)] * (3 if has_b else 2) + [_whole((1, d))]
    return pl.pallas_call(
        body, name=name,
        out_shape=(jax.ShapeDtypeStruct((t, d), F32), jax.ShapeDtypeStruct((t, d), BF16),
                   jax.ShapeDtypeStruct((8, d), F32), jax.ShapeDtypeStruct((8, d), F32)),
        grid=(t // tr,), in_specs=in_specs,
        out_specs=(_rows(tr, d), _rows(tr, d), _whole((8, d)), _whole((8, d))),
        compiler_params=_params())(*ins)


def _take_row(v, row_id, s):
    return jnp.sum(jnp.where(row_id == s, v, 0.0), axis=0, keepdims=True)


def _seg_carries(f_re, f_im, p_re, p_im, reverse):
    row_id = lax.broadcasted_iota(jnp.int32, f_re.shape, 0)
    p_re, p_im = _take_row(p_re, row_id, 0), _take_row(p_im, row_id, 0)
    out_re, out_im = jnp.zeros_like(f_re), jnp.zeros_like(f_im)
    c_re, c_im = jnp.zeros_like(p_re), jnp.zeros_like(p_im)
    order = range(N_SEG - 1, -1, -1) if reverse else range(N_SEG)
    for s in order:
        out_re = jnp.where(row_id == s, c_re, out_re)
        out_im = jnp.where(row_id == s, c_im, out_im)
        fr, fi = _take_row(f_re, row_id, s), _take_row(f_im, row_id, s)
        c_re, c_im = fr + p_re * c_re - p_im * c_im, fi + p_re * c_im + p_im * c_re
    return out_re, out_im


def _scan_fwd(bu_re, bu_im, a_re, a_im, n_seq, name):
    t, n = bu_re.shape
    seq = t // n_seq
    steps = seq // N_SEG
    cb = _pick(n, 256)

    def body(bre, bim, are, aim, hre, him):
        ar = jnp.broadcast_to(are[...], (N_SEG, cb))
        ai = jnp.broadcast_to(aim[...], (N_SEG, cb))
        zero = jnp.zeros((N_SEG, cb), F32)

        def local(k, carry):
            lr, li, pr, pi = carry
            rows = pl.ds(pl.multiple_of(k * N_SEG, N_SEG), N_SEG)
            nr = ar * lr - ai * li + bre[rows, :]
            ni = ar * li + ai * lr + bim[rows, :]
            hre[rows, :] = nr
            him[rows, :] = ni
            return nr, ni, ar * pr - ai * pi, ar * pi + ai * pr

        f_re, f_im, p_re, p_im = lax.fori_loop(0, steps, local, (zero, zero, zero + 1.0, zero))
        c_re, c_im = _seg_carries(f_re, f_im, p_re, p_im, reverse=False)

        def fix(k, carry):
            pr, pi = carry
            rows = pl.ds(pl.multiple_of(k * N_SEG, N_SEG), N_SEG)
            hre[rows, :] = hre[rows, :] + (pr * c_re - pi * c_im)
            him[rows, :] = him[rows, :] + (pr * c_im + pi * c_re)
            return ar * pr - ai * pi, ar * pi + ai * pr

        lax.fori_loop(0, steps, fix, (ar, ai))

    blk = pl.BlockSpec((seq, cb), lambda s, j: (s, j))
    vec = pl.BlockSpec((1, cb), lambda s, j: (0, j))
    return pl.pallas_call(
        body, name=name,
        out_shape=(jax.ShapeDtypeStruct((t, n), F32), jax.ShapeDtypeStruct((t, n), F32)),
        grid=(n_seq, n // cb), in_specs=[blk, blk, vec, vec], out_specs=(blk, blk),
        compiler_params=_params(2))(bu_re, bu_im, a_re, a_im)


def _scan_bwd(g_re, g_im, h_re, h_im, a_re, a_im, n_seq, name):
    t, n = g_re.shape
    seq = t // n_seq
    steps = seq // N_SEG
    cb = _pick(n, 256)

    def body(gre, gim, hre, him, are, aim, lre, lim, dare, daim):
        ar = jnp.broadcast_to(are[...], (N_SEG, cb))
        ai = -jnp.broadcast_to(aim[...], (N_SEG, cb))
        zero = jnp.zeros((N_SEG, cb), F32)

        def local(i, carry):
            lr, li, pr, pi = carry
            k = steps - 1 - i
            rows = pl.ds(pl.multiple_of(k * N_SEG, N_SEG), N_SEG)
            nr = ar * lr - ai * li + gre[rows, :]
            ni = ar * li + ai * lr + gim[rows, :]
            lre[rows, :] = nr
            lim[rows, :] = ni
            return nr, ni, ar * pr - ai * pi, ar * pi + ai * pr

        f_re, f_im, p_re, p_im = lax.fori_loop(0, steps, local, (zero, zero, zero + 1.0, zero))
        c_re, c_im = _seg_carries(f_re, f_im, p_re, p_im, reverse=True)

        def accumulate(lam_r, lam_i, hp_r, hp_i, acc_r, acc_i):
            return acc_r + (lam_r * hp_r + lam_i * hp_i), acc_i + (lam_i * hp_r - lam_r * hp_i)

        def fix(i, carry):
            pr, pi, acc_r, acc_i = carry
            k = steps - 1 - i
            rows = pl.ds(pl.multiple_of(k * N_SEG, N_SEG), N_SEG)
            prev = pl.ds(pl.multiple_of((k - 1) * N_SEG, N_SEG), N_SEG)
            lam_r = lre[rows, :] + (pr * c_re - pi * c_im)
            lam_i = lim[rows, :] + (pr * c_im + pi * c_re)
            lre[rows, :] = lam_r
            lim[rows, :] = lam_i
            acc_r, acc_i = accumulate(lam_r, lam_i, hre[prev, :], him[prev, :], acc_r, acc_i)
            return ar * pr - ai * pi, ar * pi + ai * pr, acc_r, acc_i

        pr, pi, acc_r, acc_i = lax.fori_loop(0, steps - 1, fix, (ar, ai, zero, zero))
        first = pl.ds(0, N_SEG)
        last = pl.ds((steps - 1) * N_SEG, N_SEG)
        lam_r = lre[first, :] + (pr * c_re - pi * c_im)
        lam_i = lim[first, :] + (pr * c_im + pi * c_re)
        lre[first, :] = lam_r
        lim[first, :] = lam_i
        row_id = lax.broadcasted_iota(jnp.int32, (N_SEG, cb), 0)
        hp_r = jnp.where(row_id == 0, 0.0, pltpu.roll(hre[last, :], 1, 0))
        hp_i = jnp.where(row_id == 0, 0.0, pltpu.roll(him[last, :], 1, 0))
        acc_r, acc_i = accumulate(lam_r, lam_i, hp_r, hp_i, acc_r, acc_i)
        dare[...] = acc_r
        daim[...] = acc_i

    blk = pl.BlockSpec((seq, cb), lambda s, j: (s, j))
    vec = pl.BlockSpec((1, cb), lambda s, j: (0, j))
    acc = pl.BlockSpec((N_SEG, cb), lambda s, j: (s, j))
    return pl.pallas_call(
        body, name=name,
        out_shape=(jax.ShapeDtypeStruct((t, n), F32), jax.ShapeDtypeStruct((t, n), F32),
                   jax.ShapeDtypeStruct((n_seq * N_SEG, n), F32),
                   jax.ShapeDtypeStruct((n_seq * N_SEG, n), F32)),
        grid=(n_seq, n // cb), in_specs=[blk, blk, blk, blk, vec, vec],
        out_specs=(blk, blk, acc, acc),
        compiler_params=_params(2))(g_re, g_im, h_re, h_im, a_re, a_im)


def _s5_post_fwd(yssm, u, d_skip, glu_w, glu_b, name):
    t, w = yssm.shape
    tr = _pick(t, 512, 8)

    def body(y_ref, u_ref, d_ref, w_ref, b_ref, o_ref):
        yg = _gelu(y_ref[...] + d_ref[...] * u_ref[...])
        pre = _dot(yg, w_ref[...]) + b_ref[...]
        o_ref[...] = yg * _sigmoid(pre)

    return pl.pallas_call(
        body, name=name, out_shape=jax.ShapeDtypeStruct((t, w), F32), grid=(t // tr,),
        in_specs=[_rows(tr, w), _rows(tr, w), _whole((1, w)), _whole((w, w)), _whole((1, w))],
        out_specs=_rows(tr, w), compiler_params=_params())(yssm, u, d_skip, glu_w, glu_b)


def _s5_post_bwd(yssm, u, dout, d_skip, glu_w, glu_b, name):
    t, w = yssm.shape
    tr = _pick(t, 512, 8)

    def body(y_ref, u_ref, do_ref, d_ref, w_ref, b_ref, dy_ref, du_ref, dw_ref, dd_ref, db_ref):
        uu = u_ref[...]
        y = y_ref[...] + d_ref[...] * uu
        yg = _gelu(y)
        sg = _sigmoid(_dot(yg, w_ref[...]) + b_ref[...])
        do = do_ref[...]
        dpre = do * yg * sg * (1.0 - sg)
        dyg = do * sg + _dot(dpre, w_ref[...], tb=True)
        dy = dyg * _gelu_grad(y)
        dy_ref[...] = dy.astype(BF16)
        du_ref[...] = dy * d_ref[...]

        @pl.when(pl.program_id(0) == 0)
        def _():
            dw_ref[...] = jnp.zeros_like(dw_ref)
            dd_ref[...] = jnp.zeros_like(dd_ref)
            db_ref[...] = jnp.zeros_like(db_ref)

        dw_ref[...] += _dot(yg, dpre, ta=True)
        dd_ref[...] += _fold8(dy * uu)
        db_ref[...] += _fold8(dpre)

    return pl.pallas_call(
        body, name=name,
        out_shape=(jax.ShapeDtypeStruct((t, w), BF16), jax.ShapeDtypeStruct((t, w), F32),
                   jax.ShapeDtypeStruct((w, w), F32), jax.ShapeDtypeStruct((8, w), F32),
                   jax.ShapeDtypeStruct((8, w), F32)),
        grid=(t // tr,),
        in_specs=[_rows(tr, w), _rows(tr, w), _rows(tr, w), _whole((1, w)), _whole((w, w)),
                  _whole((1, w))],
        out_specs=(_rows(tr, w), _rows(tr, w), _whole((w, w)), _whole((8, w)), _whole((8, w))),
        compiler_params=_params())(yssm, u, dout, d_skip, glu_w, glu_b)


def _head_select(parts, head_dim):
    col_head = lax.broadcasted_iota(jnp.int32, parts[0].shape, 1) // head_dim
    out = jnp.zeros_like(parts[0])
    for h, part in enumerate(parts):
        out = jnp.where(col_head == h, part, out)
    return out


def _gmlp_fwd(proj, ln_g, ln_b, ws, bs_cols, name):
    t = proj.shape[0]
    n_heads = ws.shape[0]
    w = bs_cols.shape[1]
    head_dim = w // n_heads
    cpb = _pick(t // CHUNK, 4, 1)
    tr = cpb * CHUNK

    def body(zu_ref, zv_ref, g_ref, b_ref, ws_ref, bs_ref, o_ref):
        for c in range(cpb):
            rows = pl.ds(c * CHUNK, CHUNK)
            xhat, _ = _ln_stats(_gelu(zv_ref[rows, :]))
            vn = (xhat * g_ref[...] + b_ref[...]).astype(BF16)
            s = _head_select([_dot(ws_ref[h], vn) for h in range(n_heads)], head_dim) + bs_ref[...]
            o_ref[rows, :] = _gelu(zu_ref[rows, :]) * s

    return pl.pallas_call(
        body, name=name, out_shape=jax.ShapeDtypeStruct((t, w), F32), grid=(t // tr,),
        in_specs=[_rows(tr, w, 1), _rows(tr, w, 2), _whole((1, w)), _whole((1, w)),
                  _whole(ws.shape), _whole(bs_cols.shape)],
        out_specs=_rows(tr, w), compiler_params=_params())(proj, proj, ln_g, ln_b, ws, bs_cols)


def _gmlp_bwd(proj, dout, ln_g, ln_b, ws, ws_t, bs_cols, name):
    t = proj.shape[0]
    n_heads = ws.shape[0]
    w = bs_cols.shape[1]
    head_dim = w // n_heads
    cpb = _pick(t // CHUNK, 4, 1)
    tr = cpb * CHUNK

    def body(zu_ref, zv_ref, do_ref, g_ref, b_ref, ws_ref, wst_ref, bs_ref,
             dzu_ref, dzv_ref, dws_ref, dbs_ref, dg_ref, dbeta_ref):
        @pl.when(pl.program_id(0) == 0)
        def _():
            dws_ref[...] = jnp.zeros_like(dws_ref)
            dbs_ref[...] = jnp.zeros_like(dbs_ref)
            dg_ref[...] = jnp.zeros_like(dg_ref)
            dbeta_ref[...] = jnp.zeros_like(dbeta_ref)

        col_head = lax.broadcasted_iota(jnp.int32, (CHUNK, w), 1) // head_dim
        for c in range(cpb):
            rows = pl.ds(c * CHUNK, CHUNK)
            zu, zv, do = zu_ref[rows, :], zv_ref[rows, :], do_ref[rows, :]
            xhat, rstd = _ln_stats(_gelu(zv))
            vn = (xhat * g_ref[...] + b_ref[...]).astype(BF16)
            s = _head_select([_dot(ws_ref[h], vn) for h in range(n_heads)], head_dim) + bs_ref[...]
            dzu_ref[rows, :] = (do * s * _gelu_grad(zu)).astype(BF16)
            ds = do * _gelu(zu)
            ds16 = ds.astype(BF16)
            dbs_ref[...] += ds
            for h in range(n_heads):
                dws_ref[h] += _dot(jnp.where(col_head == h, ds16, jnp.zeros_like(ds16)), vn, tb=True)
            dvn = _head_select([_dot(wst_ref[h], ds16) for h in range(n_heads)], head_dim)
            dg_ref[...] += _fold8(dvn * xhat)
            dbeta_ref[...] += _fold8(dvn)
            dgv = _ln_bwd(dvn, xhat, rstd, g_ref[...])
            dzv_ref[rows, :] = (dgv * _gelu_grad(zv)).astype(BF16)

    return pl.pallas_call(
        body, name=name,
        out_shape=(jax.ShapeDtypeStruct((t, w), BF16), jax.ShapeDtypeStruct((t, w), BF16),
                   jax.ShapeDtypeStruct(ws.shape, F32), jax.ShapeDtypeStruct((CHUNK, w), F32),
                   jax.ShapeDtypeStruct((8, w), F32), jax.ShapeDtypeStruct((8, w), F32)),
        grid=(t // tr,),
        in_specs=[_rows(tr, w, 1), _rows(tr, w, 2), _rows(tr, w), _whole((1, w)), _whole((1, w)),
                  _whole(ws.shape), _whole(ws.shape), _whole(bs_cols.shape)],
        out_specs=(_rows(tr, w), _rows(tr, w), _whole(ws.shape), _whole((CHUNK, w)),
                   _whole((8, w)), _whole((8, w))),
        compiler_params=_params())(proj, proj, dout, ln_g, ln_b, ws, ws_t, bs_cols)


def _merge_fwd(s5out, gm, proj, up_a, up_b, name):
    t, w = s5out.shape
    d = up_a.shape[1]
    assert d == 2 * w
    tr = _pick(t, 256, 8)

    def body(s_ref, gm_ref, ga0, ga1, gb0, gb1, ua_ref, ub_ref, m_ref, ya_ref, yb_ref):
        ya = _dot(s_ref[...], ua_ref[...])
        yb = _dot(gm_ref[...], ub_ref[...])
        ya_ref[...] = ya
        yb_ref[...] = yb
        for half, (ga, gb) in enumerate(((ga0, gb0), (ga1, gb1))):
            cols = slice(half * w, (half + 1) * w)
            m_ref[:, cols] = (_sigmoid(ga[...]) * ya[:, cols] + _sigmoid(gb[...]) * yb[:, cols]).astype(BF16)

    return pl.pallas_call(
        body, name=name,
        out_shape=(jax.ShapeDtypeStruct((t, d), BF16), jax.ShapeDtypeStruct((t, d), F32),
                   jax.ShapeDtypeStruct((t, d), F32)),
        grid=(t // tr,),
        in_specs=[_rows(tr, w), _rows(tr, w), _rows(tr, w, 3), _rows(tr, w, 4), _rows(tr, w, 5),
                  _rows(tr, w, 6), _whole(up_a.shape), _whole(up_b.shape)],
        out_specs=(_rows(tr, d), _rows(tr, d), _rows(tr, d)),
        compiler_params=_params())(s5out, gm, proj, proj, proj, proj, up_a, up_b)


def _merge_bwd(dm, ya, yb, s5out, gm, proj, up_a, up_b, name):
    t, d = dm.shape
    w = d // 2
    tr = _pick(t, 256, 8)

    def body(dm_ref, ya_ref, yb_ref, s_ref, gm_ref, ga0, ga1, gb0, gb1, ua_ref, ub_ref,
             dga_ref, dgb_ref, ds_ref, dgm_ref, dua_ref, dub_ref):
        dm_v, ya, yb = dm_ref[...], ya_ref[...], yb_ref[...]
        dya, dyb = [], []
        for half, (ga, gb) in enumerate(((ga0, gb0), (ga1, gb1))):
            cols = slice(half * w, (half + 1) * w)
            sa, sb = _sigmoid(ga[...]), _sigmoid(gb[...])
            dmh = dm_v[:, cols]
            dga_ref[:, cols] = (dmh * ya[:, cols] * sa * (1.0 - sa)).astype(BF16)
            dgb_ref[:, cols] = (dmh * yb[:, cols] * sb * (1.0 - sb)).astype(BF16)
            dya.append((dmh * sa).astype(BF16))
            dyb.append((dmh * sb).astype(BF16))
        dya = jnp.concatenate(dya, axis=1)
        dyb = jnp.concatenate(dyb, axis=1)
        ds_ref[...] = _dot(dya, ua_ref[...], tb=True)
        dgm_ref[...] = _dot(dyb, ub_ref[...], tb=True)

        @pl.when(pl.program_id(0) == 0)
        def _():
            dua_ref[...] = jnp.zeros_like(dua_ref)
            dub_ref[...] = jnp.zeros_like(dub_ref)

        dua_ref[...] += _dot(s_ref[...], dya, ta=True)
        dub_ref[...] += _dot(gm_ref[...], dyb, ta=True)

    return pl.pallas_call(
        body, name=name,
        out_shape=(jax.ShapeDtypeStruct((t, d), BF16), jax.ShapeDtypeStruct((t, d), BF16),
                   jax.ShapeDtypeStruct((t, w), F32), jax.ShapeDtypeStruct((t, w), F32),
                   jax.ShapeDtypeStruct(up_a.shape, F32), jax.ShapeDtypeStruct(up_b.shape, F32)),
        grid=(t // tr,),
        in_specs=[_rows(tr, d), _rows(tr, d), _rows(tr, d), _rows(tr, w), _rows(tr, w),
                  _rows(tr, w, 3), _rows(tr, w, 4), _rows(tr, w, 5), _rows(tr, w, 6),
                  _whole(up_a.shape), _whole(up_b.shape)],
        out_specs=(_rows(tr, d), _rows(tr, d), _rows(tr, w), _rows(tr, w), _whole(up_a.shape),
                   _whole(up_b.shape)),
        compiler_params=_params())(dm, ya, yb, s5out, gm, proj, proj, proj, proj, up_a, up_b)


def _head(x3, p, target, w_gate, w_proj, name):
    t, d = x3.shape
    pd = p.shape[1]
    tr = _pick(t, 256, 8)
    nb = t // tr

    def body(x_ref, p_ref, t_ref, wg_ref, wp_ref, loss_ref, dx_ref, dwg_ref, dwp_ref):
        xv = x_ref[...]
        sg = _sigmoid(_dot(xv, wg_ref[...]))
        pp = _dot(p_ref[...], wp_ref[...])
        err = xv + sg * pp - t_ref[...]
        loss_ref[...] = jnp.full((8, 128), 0.5 * jnp.sum(jnp.mean(err * err, axis=-1)), F32)
        dout = err * (1.0 / d)
        dgpre = dout * pp * sg * (1.0 - sg)
        dpp = dout * sg
        dx_ref[...] = dout + _dot(dgpre, wg_ref[...], tb=True)

        @pl.when(pl.program_id(0) == 0)
        def _():
            dwg_ref[...] = jnp.zeros_like(dwg_ref)
            dwp_ref[...] = jnp.zeros_like(dwp_ref)

        dwg_ref[...] += _dot(xv, dgpre, ta=True)
        dwp_ref[...] += _dot(p_ref[...], dpp, ta=True)

    return pl.pallas_call(
        body, name=name,
        out_shape=(jax.ShapeDtypeStruct((nb * 8, 128), F32), jax.ShapeDtypeStruct((t, d), F32),
                   jax.ShapeDtypeStruct((d, d), F32), jax.ShapeDtypeStruct((pd, d), F32)),
        grid=(nb,),
        in_specs=[_rows(tr, d), _rows(tr, pd), _rows(tr, d), _whole((d, d)), _whole((pd, d))],
        out_specs=(pl.BlockSpec((8, 128), lambda i: (i, 0)), _rows(tr, d), _whole((d, d)),
                   _whole((pd, d))),
        compiler_params=_params())(x3, p, target, w_gate, w_proj)


_HBM = pl.BlockSpec(memory_space=pltpu.HBM)


def _all_gather(shards, name):
    n = len(shards)

    def body(*refs):
        x_refs, out_refs = refs[:n], refs[n:2 * n]
        send_sems, recv_sems, local_sems = refs[2 * n:]
        x, y, c = lax.axis_index("x"), lax.axis_index("y"), lax.axis_index("c")
        me, sibling = (x, y, c), (x, y, 1 - c)
        chips = [(1 - x, y), (x, 1 - y), (1 - x, 1 - y)]

        def copy(a, k, block, to, own=False):
            slot = out_refs[a].at[4 * block[0] + 2 * block[1] + block[2]]
            return pltpu.make_async_remote_copy(
                src_ref=x_refs[a] if own else slot, dst_ref=slot,
                send_sem=send_sems.at[7 * a + k], recv_sem=recv_sems.at[7 * a + k],
                device_id=to, device_id_type=pl.DeviceIdType.MESH)

        mine = [pltpu.make_async_copy(x_refs[a], out_refs[a].at[4 * x + 2 * y + c], local_sems.at[a])
                for a in range(n)]
        sends = []
        for a in range(n):
            mine[a].start()
            first = [copy(a, 0, me, sibling, own=True)]
            first += [copy(a, 1 + j, me, (*chip, c), own=True) for j, chip in enumerate(chips)]
            for cp in first:
                cp.start()
            sends += first
        for a in range(n):
            for j, chip in enumerate(chips):
                copy(a, 1 + j, (*chip, c), me).wait_recv()
                passed = copy(a, 4 + j, (*chip, c), sibling)
                passed.start()
                sends.append(passed)
        for a in range(n):
            copy(a, 0, sibling, me).wait_recv()
            for j, chip in enumerate(chips):
                copy(a, 4 + j, (*chip, 1 - c), me).wait_recv()
        for cp in sends:
            cp.wait_send()
        for cp in mine:
            cp.wait()

    return pl.pallas_call(
        body, name=name,
        out_shape=tuple(jax.ShapeDtypeStruct((N_DEV,) + s.shape, s.dtype) for s in shards),
        in_specs=[_HBM] * n, out_specs=tuple([_HBM] * n),
        scratch_shapes=[pltpu.SemaphoreType.DMA((7 * n,)), pltpu.SemaphoreType.DMA((7 * n,)),
                        pltpu.SemaphoreType.DMA((n,))],
    )(*shards)


_SEM = pl.BlockSpec(memory_space=pltpu.SEMAPHORE)
_ANY = pl.BlockSpec(memory_space=pl.ANY)
_DATAFLOW = pltpu.SideEffectType.DATAFLOW_SIDE_EFFECTING


def _peer(k, x, y, c):
    return (1 - x if k & 4 else x, 1 - y if k & 2 else y, 1 - c if k & 1 else c)


def _exchange_start(sends, after, name):
    n = len(sends)
    me = 4 * lax.axis_index("x") + 2 * lax.axis_index("y") + lax.axis_index("c")
    lands = []
    for s in sends:
        own = s[0] if s.shape[0] == 1 else lax.dynamic_index_in_dim(s, me, 0, keepdims=False)
        land = lax.empty((N_DEV,) + s.shape[1:], s.dtype)
        lands.append(lax.dynamic_update_index_in_dim(land, own, me, 0))

    def body(*refs):
        s_refs, l_refs = refs[:n], refs[n:2 * n]
        send_sems, recv_sems, token = refs[2 * n + 1], refs[2 * n + 2], refs[-1]
        x, y, c = lax.axis_index("x"), lax.axis_index("y"), lax.axis_index("c")
        for a in range(n):
            same = sends[a].shape[0] == 1
            for k in range(1, N_DEV):
                px, py, pc = _peer(k, x, y, c)
                pltpu.make_async_remote_copy(
                    src_ref=s_refs[a].at[0 if same else 4 * px + 2 * py + pc],
                    dst_ref=l_refs[a].at[4 * x + 2 * y + c],
                    send_sem=send_sems.at[7 * a + k - 1], recv_sem=recv_sems.at[7 * a + k - 1],
                    device_id=(px, py, pc), device_id_type=pl.DeviceIdType.MESH).start()
        token[...] = jnp.zeros_like(token)

    outs = pl.pallas_call(
        body, name=name,
        out_shape=(pltpu.SemaphoreType.DMA((7 * n,)), pltpu.SemaphoreType.DMA((7 * n,)),
                   *[pltpu.HBM(s.shape, s.dtype) for s in sends],
                   *[pltpu.HBM(l.shape, l.dtype) for l in lands],
                   jax.ShapeDtypeStruct((8, 128), F32)),
        in_specs=[_HBM] * (2 * n) + [_ANY],
        out_specs=(_SEM, _SEM, *([_HBM] * (2 * n)), pl.BlockSpec(memory_space=pltpu.VMEM)),
        input_output_aliases={i: 2 + i for i in range(2 * n)},
        compiler_params=pltpu.CompilerParams(has_side_effects=_DATAFLOW),
    )(*[pltpu.with_memory_space_constraint(v, pltpu.HBM) for v in list(sends) + lands], after)
    return (outs[0], outs[1], list(outs[2:2 + n]), list(outs[2 + n:2 + 2 * n])), outs[-1]


def _exchange_wait(handle, after, name):
    send_sems, recv_sems, sends, lands = handle
    n = len(sends)

    def body(*refs):
        s_refs, l_refs = refs[:n], refs[n:2 * n]
        send_sems, recv_sems = refs[2 * n], refs[2 * n + 1]
        x, y, c = lax.axis_index("x"), lax.axis_index("y"), lax.axis_index("c")
        for a in range(n):
            for k in range(1, N_DEV):
                copy = pltpu.make_async_remote_copy(
                    src_ref=s_refs[a].at[0], dst_ref=l_refs[a].at[0],
                    send_sem=send_sems.at[7 * a + k - 1], recv_sem=recv_sems.at[7 * a + k - 1],
                    device_id=_peer(k, x, y, c), device_id_type=pl.DeviceIdType.MESH)
                copy.wait_send()
                copy.wait_recv()

    outs = pl.pallas_call(
        body, name=name,
        out_shape=(*[pltpu.HBM(s.shape, s.dtype) for s in sends], *[pltpu.HBM(l.shape, l.dtype) for l in lands]),
        in_specs=[_HBM] * (2 * n) + [_SEM, _SEM] + [_ANY] * len(after),
        out_specs=tuple([_HBM] * (2 * n)),
        input_output_aliases={i: i for i in range(2 * n)},
        compiler_params=pltpu.CompilerParams(has_side_effects=_DATAFLOW),
    )(*sends, *lands, send_sems, recv_sems, *after)
    return list(outs[n:])


def _adamw(recv, w, m, v, name):
    n, rows, width = recv.shape
    tr = _pick(rows, max(8, ADAM_BLOCK_BYTES // (n * width * recv.dtype.itemsize)), 8)
    c_m = 1.0 - ADAM_B1 ** ADAM_STEP
    c_v = 1.0 - ADAM_B2 ** ADAM_STEP

    def body(r_ref, w_ref, m_ref, v_ref, g_ref, d_ref, nm_ref, nv_ref):
        g = r_ref[0].astype(F32)
        for i in range(1, n):
            g = g + r_ref[i].astype(F32)
        m2 = ADAM_B1 * m_ref[...] + (1.0 - ADAM_B1) * g
        v2 = ADAM_B2 * v_ref[...] + (1.0 - ADAM_B2) * (g * g)
        m_hat = m2 / c_m
        v_hat = v2 / c_v
        g_ref[...] = g
        d_ref[...] = -ADAM_LR * (m_hat / (jnp.sqrt(v_hat) + ADAM_EPS) + ADAM_WD * w_ref[...])
        nm_ref[...] = m2
        nv_ref[...] = v2

    blk = _rows(tr, width)
    shp = jax.ShapeDtypeStruct((rows, width), F32)
    return pl.pallas_call(
        body, name=name, out_shape=(shp, shp, shp, shp), grid=(rows // tr,),
        in_specs=[pl.BlockSpec((n, tr, width), lambda i: (0, i, 0)), blk, blk, blk],
        out_specs=(blk, blk, blk, blk), compiler_params=_params())(recv, w, m, v)


def _join_cols(gathered):
    n, r, c = gathered.shape
    return gathered.transpose(1, 0, 2).reshape(r, n * c)


def _split_cols(full):
    r, c = full.shape
    return full.reshape(r, N_DEV, c // N_DEV).transpose(1, 0, 2)


def _small_rows(size):
    return -(-size // (8 * PACK_W)) * 8


def _pack_small(parts, names):
    rows = []
    for n in names:
        flat = parts[n].reshape(-1)
        r = _small_rows(flat.shape[0])
        rows.append(jnp.pad(flat, (0, r * PACK_W - flat.shape[0])).reshape(r, PACK_W))
    return jnp.concatenate(rows, axis=0)


def _unpack_small(packed, shapes, names):
    out, r0 = {}, 0
    for n in names:
        size = math.prod(shapes[n])
        rows = _small_rows(size)
        out[n] = packed[r0:r0 + rows].reshape(-1)[:size].reshape(shapes[n])
        r0 += rows
    return out


def _discretise(lam_re, lam_im, log_dt, b_re, b_im):
    dt = jnp.exp(log_dt)[:, None]
    mag = jnp.exp(lam_re * dt)
    ab_re = mag * jnp.cos(lam_im * dt)
    ab_im = mag * jnp.sin(lam_im * dt)
    nr = ab_re - 1.0
    ni = ab_im
    den = lam_re * lam_re + lam_im * lam_im
    coef_re = ((nr * lam_re + ni * lam_im) / den)[..., None]
    coef_im = ((ni * lam_re - nr * lam_im) / den)[..., None]
    bb_re = coef_re * b_re - coef_im * b_im
    bb_im = coef_re * b_im + coef_im * b_re
    return ab_re, ab_im, bb_re, bb_im


def _same_group(g):
    return jnp.eye(g, dtype=bool)[:, None, :, None]


def _block_diag_in(bb):
    g, p, i = bb.shape
    placed = jnp.where(_same_group(g), bb.transpose(0, 2, 1)[:, :, None, :], 0.0)
    return placed.reshape(g * i, g * p)


def _block_diag_in_take(dense, g, p, i):
    blocks = jnp.where(_same_group(g), dense.reshape(g, i, g, p), 0.0).sum(axis=2)
    return blocks.transpose(0, 2, 1)


def _block_diag_out(cc):
    g, i, p = cc.shape
    placed = jnp.where(_same_group(g), cc.transpose(0, 2, 1)[:, :, None, :], 0.0)
    return placed.reshape(g * p, g * i)


def _block_diag_out_take(dense, g, i, p):
    blocks = jnp.where(_same_group(g), dense.reshape(g, p, g, i), 0.0).sum(axis=2)
    return blocks.transpose(0, 2, 1)


def _perm_rows(z, n_seq):
    t, w = z.shape
    steps = t // n_seq // N_SEG
    return z.reshape(n_seq, N_SEG, steps, w).transpose(0, 2, 1, 3).reshape(t, w)


def _unperm_rows(z, n_seq):
    t, w = z.shape
    steps = t // n_seq // N_SEG
    return z.reshape(n_seq, steps, N_SEG, w).transpose(0, 2, 1, 3).reshape(t, w)


def kernel(*args):
    assert len(args) == len(INPUT_NAMES)
    inp = dict(zip(INPUT_NAMES, args))
    depth = inp["ffn1_w_in"].shape[0]
    assert depth == 1
    alpha = (2.0 * depth) ** 0.25

    n_seq, seq, d_model = inp["x"].shape
    t = n_seq * seq
    x = inp["x"].reshape(t, d_model)
    x16 = x.astype(BF16)
    p16 = inp["p"][0].reshape(t, -1).astype(BF16)
    target = inp["loss_target"].reshape(t, d_model)
    wts = {n: inp[n][0] for n in WEIGHTS}
    mom = {n: inp["m_" + n][0] for n in WEIGHTS}
    var = {n: inp["v_" + n][0] for n in WEIGHTS}

    full = {}

    def place(n, g):
        if n in ("ffn1_w_in", "ffn2_w_in"):
            full[n] = g.reshape((2, N_DEV // 2) + g.shape[1:])
        elif n in ("ffn1_w_out", "ffn2_w_out"):
            full[n] = g.reshape(N_DEV // 2, 2 * g.shape[1], g.shape[2])
        elif n in COL_SHARDED:
            full[n] = _join_cols(g)
        else:
            full[n] = g.reshape(N_DEV * g.shape[1], g.shape[2])

    w16 = dict(zip(GATHER_FIRST, _to_bf16([wts[n] for n in GATHER_FIRST], "cast_ffn1")))
    later = GATHER_MIX + GATHER_LAST
    w16.update(zip(later, _to_bf16([wts[n] for n in later], "cast_rest")))
    for n, g in zip(GATHER_FIRST, _all_gather([w16[n] for n in GATHER_FIRST], "gather_ffn1")):
        place(n, g)
    gather_mix, gather_mix_token = _exchange_start([w16[n][None] for n in GATHER_MIX], full["ffn1_w_out"],
                                                   "gather_mix_start")

    def row(v):
        return v.reshape(1, -1)

    n_groups, n_state = wts["ssm_lambda_re"].shape
    n_ch = wts["ssm_b_re"].shape[2]
    disc_in = (wts["ssm_lambda_re"], wts["ssm_lambda_im"], wts["ssm_log_dt"], wts["ssm_b_re"],
               wts["ssm_b_im"])
    (ab_re, ab_im, bb_re, bb_im), disc_vjp = jax.vjp(_discretise, *disc_in)
    a_re, a_im = row(ab_re), row(ab_im)
    b_dense_re = _block_diag_in(bb_re).astype(BF16)
    b_dense_im = _block_diag_in(bb_im).astype(BF16)
    c_dense_re = _block_diag_out(wts["ssm_c_re"]).astype(BF16)
    c_dense_im = _block_diag_out(-wts["ssm_c_im"]).astype(BF16)

    ws = wts["gmlp_w_s"]
    n_heads = ws.shape[0]
    d_gmlp = wts["gmlp_ln_g"].shape[0]
    causal = jnp.tril(jnp.ones((CHUNK, CHUNK), dtype=bool))
    ws_masked = jnp.where(causal[None], ws, 0.0).astype(BF16)
    ws_masked_t = ws_masked.transpose(0, 2, 1)
    bs_cols = jnp.repeat(wts["gmlp_b_s"].T, d_gmlp // n_heads, axis=1)
    d_ssm = n_groups * n_ch
    assert d_ssm == d_gmlp and d_model == 2 * d_ssm

    h1, a1 = _ffn_in(x16, full["ffn1_w_in"], "ffn1_in", token=gather_mix_token)
    r1, x1, x1_16 = _ffn_out_ln(a1, full["ffn1_w_out"], x, row(wts["ln1_g"]), row(wts["ln1_b"]), alpha,
                                "ffn1_out_ln")

    for n, g in zip(GATHER_MIX, _exchange_wait(gather_mix, [x1_16], "gather_mix_wait")):
        place(n, g)
    gather_last, token = _exchange_start([w16[n][None] for n in GATHER_LAST], full["mix_w_in"],
                                         "gather_ffn2_start")
    proj = _mm(x1_16, full["mix_w_in"], name="mix_in", token=token)
    za_p = _perm_rows(proj[:, :d_ssm], n_seq)
    za_p16 = za_p.astype(BF16)
    bu_re = _mm(za_p16, b_dense_re, name="s5_bu_re")
    bu_im = _mm(za_p16, b_dense_im, name="s5_bu_im")
    h_re, h_im = _scan_fwd(bu_re, bu_im, a_re, a_im, n_seq, "s5_scan")
    yssm = _mm(h_re, c_dense_re, name="s5_y_re")
    yssm = _mm(h_im, c_dense_im, name="s5_y_im", add=yssm)
    s5out_p = _s5_post_fwd(yssm, za_p, row(wts["ssm_d"]), full["ssm_glu_w"], row(wts["ssm_glu_b"]),
                           "s5_post")
    s5out = _unperm_rows(s5out_p, n_seq)
    gm = _gmlp_fwd(proj, row(wts["gmlp_ln_g"]), row(wts["gmlp_ln_b"]), ws_masked, bs_cols, "gmlp")
    m_mix, y_a, y_b = _merge_fwd(s5out, gm, proj, full["up_a"], full["up_b"], "merge")
    mixed = _mm(m_mix, full["mix_w_out"], name="mix_out")
    r2, x2, x2_16 = _resid_ln_fwd(x1, mixed, row(wts["ln2_g"]), row(wts["ln2_b"]), alpha, 1.0, "ln2")

    for n, g in zip(GATHER_LAST, _exchange_wait(gather_last, [x2_16], "gather_ffn2_wait")):
        place(n, g)
    h2, a2 = _ffn_in(x2_16, full["ffn2_w_in"], "ffn2_in")
    r3, x3, _ = _ffn_out_ln(a2, full["ffn2_w_out"], x2, row(wts["ln3_g"]), row(wts["ln3_b"]), alpha,
                            "ffn2_out_ln")

    small = {}
    send = {}
    loss_parts, dx3, dw_gate, dw_proj = _head(x3, p16, target, full["ple_w_gate"], full["ple_w_proj"], "head")
    loss = lax.psum(jnp.sum(loss_parts[::8, 0]), ("x", "y", "c"))
    send["ple_w_gate"] = dw_gate.astype(BF16).reshape(N_DEV, -1, d_model)
    send["ple_w_proj"] = _split_cols(dw_proj.astype(BF16))

    dr3, dr3_h, dg, db = _ln_bwd_call(r3, dx3, None, 1.0, row(wts["ln3_g"]), 0.5, "ln3_bwd")
    small["ln3_g"], small["ln3_b"] = dg.sum(0), db.sum(0)
    dh2, dw = _ffn_out_bwd(dr3_h, full["ffn2_w_out"], h2, a2, "ffn2_out_bwd")
    send["ffn2_w_out"] = dw.reshape(N_DEV, -1, d_model)
    dw = _ffn_in_dw(x2_16, dh2, "ffn2_in_dw")
    send["ffn2_w_in"] = dw.reshape((N_DEV,) + dw.shape[2:])
    group1 = ("ffn2_w_in", "ffn2_w_out", "ple_w_gate", "ple_w_proj")
    exchange1, token = _exchange_start([send[n] for n in group1], dh2, "grad_exchange1_start")
    dx2_f = _ffn_in_dx(dh2, full["ffn2_w_in"], None, 1.0, "ffn2_in_dx", token=token)

    dr2, dr2_h, dg, db = _ln_bwd_call(r2, dx2_f, dr3, alpha, row(wts["ln2_g"]), 1.0, "ln2_bwd")
    small["ln2_g"], small["ln2_b"] = dg.sum(0), db.sum(0)
    dm = _mm(dr2_h, full["mix_w_out"], tb=True, name="mix_out_dx")
    send["mix_w_out"] = _mm(m_mix, dr2_h, ta=True, name="mix_out_dw", out_dtype=BF16).reshape(
        N_DEV, -1, d_model)
    dga, dgb, ds5out, dgm, dw_a, dw_b = _merge_bwd(
        dm, y_a, y_b, s5out, gm, proj, full["up_a"], full["up_b"], "merge_bwd")
    send["up_a"] = _split_cols(dw_a.astype(BF16))
    send["up_b"] = _split_cols(dw_b.astype(BF16))

    dzu, dzv, dws, dbs_cols, dg, db = _gmlp_bwd(
        proj, dgm, row(wts["gmlp_ln_g"]), row(wts["gmlp_ln_b"]), ws_masked, ws_masked_t, bs_cols,
        "gmlp_bwd")
    small["gmlp_ln_g"], small["gmlp_ln_b"] = dg.sum(0), db.sum(0)
    small["gmlp_w_s"] = jnp.where(causal[None], dws, 0.0)
    small["gmlp_b_s"] = dbs_cols.reshape(CHUNK, n_heads, -1).sum(-1).T

    ds5out_p = _perm_rows(ds5out, n_seq)
    dy_p, dza_skip, dw_glu, dd, dgb_glu = _s5_post_bwd(
        yssm, za_p, ds5out_p, row(wts["ssm_d"]), full["ssm_glu_w"], row(wts["ssm_glu_b"]),
        "s5_post_bwd")
    send["ssm_glu_w"] = dw_glu.astype(BF16).reshape(N_DEV, -1, d_ssm)
    small["ssm_d"], small["ssm_glu_b"] = dd.sum(0), dgb_glu.sum(0)
    g_re = _mm(dy_p, c_dense_re, tb=True, name="s5_y_re_dx")
    g_im = _mm(dy_p, c_dense_im, tb=True, name="s5_y_im_dx")
    dc_re = _mm(h_re, dy_p, ta=True, name="s5_y_re_dw")
    dc_im = _mm(h_im, dy_p, ta=True, name="s5_y_im_dw")
    small["ssm_c_re"] = _block_diag_out_take(dc_re, n_groups, n_ch, n_state)
    small["ssm_c_im"] = -_block_diag_out_take(dc_im, n_groups, n_ch, n_state)
    lam_re, lam_im, da_re, da_im = _scan_bwd(g_re, g_im, h_re, h_im, a_re, a_im, n_seq, "s5_scan_bwd")
    dza_p = _mm(lam_re, b_dense_re, tb=True, name="s5_bu_re_dx", add=dza_skip)
    dza_p = _mm(lam_im, b_dense_im, tb=True, name="s5_bu_im_dx", add=dza_p, out_dtype=BF16)
    dbb_re = _block_diag_in_take(_mm(za_p16, lam_re, ta=True, name="s5_bu_re_dw"), n_groups, n_state, n_ch)
    dbb_im = _block_diag_in_take(_mm(za_p16, lam_im, ta=True, name="s5_bu_im_dw"), n_groups, n_state, n_ch)
    dab_re = da_re.sum(0).reshape(n_groups, n_state)
    dab_im = da_im.sum(0).reshape(n_groups, n_state)
    (small["ssm_lambda_re"], small["ssm_lambda_im"], small["ssm_log_dt"], small["ssm_b_re"],
     small["ssm_b_im"]) = disc_vjp((dab_re, dab_im, dbb_re, dbb_im))

    dproj = jnp.concatenate([_unperm_rows(dza_p, n_seq), dzu, dzv, dga, dgb], axis=1)
    send["mix_w_in"] = _split_cols(_mm(x1_16, dproj, ta=True, name="mix_in_dw", out_dtype=BF16))
    group2 = ("mix_w_in", "mix_w_out", "up_a", "up_b", "ssm_glu_w")
    exchange2, token = _exchange_start(
        [send[n] for n in group2] + [_pack_small(small, SMALL_EARLY)[None]], dproj, "grad_exchange2_start")
    dx1_f = _mm(dproj, full["mix_w_in"], tb=True, name="mix_in_dx", token=token)

    dr1, dr1_h, dg, db = _ln_bwd_call(r1, dx1_f, dr2, alpha, row(wts["ln1_g"]), 0.5, "ln1_bwd")
    small["ln1_g"], small["ln1_b"] = dg.sum(0), db.sum(0)
    dh1, dw = _ffn_out_bwd(dr1_h, full["ffn1_w_out"], h1, a1, "ffn1_out_bwd")
    send["ffn1_w_out"] = dw.reshape(N_DEV, -1, d_model)
    exchange3, token = _exchange_start([send["ffn1_w_out"], _pack_small(small, SMALL_LATE)[None]], dh1,
                                       "grad_exchange3_start")
    dw = _ffn_in_dw(x16, dh1, "ffn1_in_dw", token=token)
    send["ffn1_w_in"] = dw.reshape((N_DEV,) + dw.shape[2:])
    exchange4, token = _exchange_start([send["ffn1_w_in"]], dh1, "grad_exchange4_start")
    grad_x = _ffn_in_dx(dh1, full["ffn1_w_in"], dr1, alpha, "ffn1_in_dx", token=token)

    results = {}

    def update(names, recv, prefix):
        for n, r in zip(names, recv):
            results[n] = _adamw(r, wts[n], mom[n], var[n], prefix + n)

    def update_small(names, recv, name):
        packed = _adamw(recv, _pack_small(wts, names), _pack_small(mom, names), _pack_small(var, names), name)
        shapes = {n: wts[n].shape for n in names}
        unpacked = [_unpack_small(pk, shapes, names) for pk in packed]
        for n in names:
            results[n] = tuple(u[n] for u in unpacked)

    def done(names):
        return [results[n][3] for n in names]

    update(group1, _exchange_wait(exchange1, [grad_x], "grad_exchange1_wait"), "adamw_")
    recv = _exchange_wait(exchange2, done(group1), "grad_exchange2_wait")
    update(group2, recv[:-1], "adamw_")
    update_small(SMALL_EARLY, recv[-1], "adamw_small")
    recv = _exchange_wait(exchange3, done(group2 + SMALL_EARLY), "grad_exchange3_wait")
    update(("ffn1_w_out",), recv[:1], "adamw_")
    update_small(SMALL_LATE, recv[1], "adamw_ln1")
    recv = _exchange_wait(exchange4, done(("ffn1_w_out",) + SMALL_LATE), "grad_exchange4_wait")
    update(("ffn1_w_in",), recv, "adamw_")

    outs = [loss, grad_x.reshape(n_seq, seq, d_model)]
    for k in range(4):
        outs += [results[n][k][None] for n in WEIGHTS]
    return tuple(outs)
```

```python
import functools
import math

import jax
import jax.numpy as jnp
from jax import lax
from jax.experimental import pallas as pl
from jax.experimental.pallas import tpu as pltpu

F32 = jnp.float32
BF16 = jnp.bfloat16

N_DEV = 8
LN_EPS = 1e-5
CHUNK = 128
N_SEG = 8
S5_CHANNELS_PER_STEP = 128
PACK_W = 1024
VMEM_LIMIT_BYTES = 56 * 1024 * 1024
MM_OPERAND_BLOCK_BYTES = 8 * 1024 * 1024
ADAM_BLOCK_BYTES = 6 * 1024 * 1024

ADAM_LR = 0.001
ADAM_B1 = 0.9
ADAM_B2 = 0.999
ADAM_EPS = 1e-08
ADAM_WD = 0.01
ADAM_STEP = 10

COL_SHARDED = ("ffn1_w_in", "mix_w_in", "up_a", "up_b", "ffn2_w_in", "ple_w_proj")
ROW_SHARDED = ("ffn1_w_out", "ssm_glu_w", "mix_w_out", "ffn2_w_out", "ple_w_gate")
SHARDED = ("ffn1_w_in", "ffn1_w_out", "mix_w_in", "ssm_glu_w", "up_a", "up_b", "mix_w_out",
           "ffn2_w_in", "ffn2_w_out", "ple_w_proj", "ple_w_gate")
WEIGHTS = ("ffn1_w_in", "ffn1_w_out", "ln1_g", "ln1_b", "mix_w_in", "ssm_lambda_re", "ssm_lambda_im",
           "ssm_log_dt", "ssm_b_re", "ssm_b_im", "ssm_c_re", "ssm_c_im", "ssm_d", "ssm_glu_w",
           "ssm_glu_b", "gmlp_ln_g", "gmlp_ln_b", "gmlp_w_s", "gmlp_b_s", "up_a", "up_b", "mix_w_out",
           "ln2_g", "ln2_b", "ffn2_w_in", "ffn2_w_out", "ln3_g", "ln3_b", "ple_w_proj", "ple_w_gate")
GATHER_FIRST = ("ffn1_w_in", "ffn1_w_out")
GATHER_MIX = ("mix_w_in", "ssm_glu_w", "up_a", "up_b", "mix_w_out")
GATHER_LAST = ("ffn2_w_in", "ffn2_w_out", "ple_w_proj", "ple_w_gate")
SMALL = tuple(n for n in WEIGHTS if n not in SHARDED)
SMALL_LATE = ("ln1_g", "ln1_b")
SMALL_EARLY = tuple(n for n in SMALL if n not in SMALL_LATE)
INPUT_NAMES = ("x", "p") + WEIGHTS + ("loss_target",) + tuple("m_" + n for n in WEIGHTS) + tuple(
    "v_" + n for n in WEIGHTS)


def _pick(dim, pref, mult=128):
    if dim <= pref:
        return dim
    d = (pref // mult) * mult
    while d >= mult:
        if dim % d == 0:
            return d
        d -= mult
    return dim


def _params(n_axes=1):
    return pltpu.CompilerParams(dimension_semantics=("arbitrary",) * n_axes,
                                vmem_limit_bytes=VMEM_LIMIT_BYTES)


def _sigmoid(v):
    return 1.0 / (1.0 + jnp.exp(-v))


_GELU_C = math.sqrt(2.0 / math.pi)


def _gelu(v):
    return 0.5 * v * (1.0 + jnp.tanh(_GELU_C * (v + 0.044715 * (v * v * v))))


def _gelu_grad(v):
    t = jnp.tanh(_GELU_C * (v + 0.044715 * (v * v * v)))
    return 0.5 * (1.0 + t) + 0.5 * v * (1.0 - t * t) * (_GELU_C * (1.0 + 3.0 * 0.044715 * v * v))


def _ln_stats(r):
    mu = jnp.mean(r, axis=-1, keepdims=True)
    xc = r - mu
    var = jnp.mean(xc * xc, axis=-1, keepdims=True)
    rstd = lax.rsqrt(var + LN_EPS)
    return xc * rstd, rstd


def _ln_bwd(dy, xhat, rstd, g):
    dxh = dy * g
    m1 = jnp.mean(dxh, axis=-1, keepdims=True)
    m2 = jnp.mean(dxh * xhat, axis=-1, keepdims=True)
    return rstd * (dxh - m1 - xhat * m2)


def _fold8(v):
    rows, w = v.shape
    return jnp.sum(v.reshape(rows // 8, 8, w), axis=0)


def _dot(a, b, ta=False, tb=False):
    dn = (((0 if ta else 1,), (1 if tb else 0,)), ((), ()))
    return lax.dot_general(a.astype(BF16), b.astype(BF16), dn, preferred_element_type=F32)


_TOKEN = pl.BlockSpec((8, 128), lambda *_: (0, 0))


def _mm(a, b, *, name, ta=False, tb=False, out_dtype=F32, add=None, add_scale=1.0, token=None,
        tm=2048, tn=512, tk=4096):
    m_dim = a.shape[1] if ta else a.shape[0]
    k_dim = a.shape[0] if ta else a.shape[1]
    n_dim = b.shape[0] if tb else b.shape[1]
    assert (b.shape[1] if tb else b.shape[0]) == k_dim, (name, a.shape, b.shape)
    tk = _pick(k_dim, tk)
    tm = min(tm, max(256, MM_OPERAND_BLOCK_BYTES // (tk * a.dtype.itemsize)))
    tm, tn = _pick(m_dim, tm), _pick(n_dim, tn)
    nk = k_dim // tk
    has_add = add is not None

    def finish(r, add_ref, o_ref):
        if has_add:
            r = r + add_scale * add_ref[...].astype(F32)
        o_ref[...] = r.astype(out_dtype)

    def body(*refs):
        a_ref, b_ref = refs[:2]
        add_ref = refs[2] if has_add else None
        o_ref = refs[n_in]
        if nk == 1:
            finish(_dot(a_ref[...], b_ref[...], ta, tb), add_ref, o_ref)
            return
        acc_ref = refs[-1]
        k = pl.program_id(2)

        @pl.when(k == 0)
        def _():
            acc_ref[...] = jnp.zeros_like(acc_ref)

        acc_ref[...] += _dot(a_ref[...], b_ref[...], ta, tb)

        @pl.when(k == nk - 1)
        def _():
            finish(acc_ref[...], add_ref, o_ref)

    a_spec = (pl.BlockSpec((tk, tm), lambda i, j, k: (k, i)) if ta
              else pl.BlockSpec((tm, tk), lambda i, j, k: (i, k)))
    b_spec = (pl.BlockSpec((tn, tk), lambda i, j, k: (j, k)) if tb
              else pl.BlockSpec((tk, tn), lambda i, j, k: (k, j)))
    in_specs = [a_spec, b_spec]
    operands = [a, b]
    if has_add:
        in_specs.append(pl.BlockSpec((tm, tn), lambda i, j, k: (i, j)))
        operands.append(add)
    if token is not None:
        in_specs.append(_TOKEN)
        operands.append(token)
    n_in = len(operands)
    return pl.pallas_call(
        body, name=name,
        out_shape=jax.ShapeDtypeStruct((m_dim, n_dim), out_dtype),
        grid=(m_dim // tm, n_dim // tn, nk),
        in_specs=in_specs,
        out_specs=pl.BlockSpec((tm, tn), lambda i, j, k: (i, j)),
        scratch_shapes=[pltpu.VMEM((tm, tn), F32)] if nk > 1 else [],
        compiler_params=_params(3),
    )(*operands)


def _to_bf16(arrays, name):
    n = len(arrays)

    def body(*refs):
        for src, dst in zip(refs[:n], refs[n:]):
            dst[...] = src[...].astype(BF16)

    vmem = pl.BlockSpec(memory_space=pltpu.VMEM)
    return pl.pallas_call(
        body, name=name, out_shape=tuple(jax.ShapeDtypeStruct(a.shape, BF16) for a in arrays),
        in_specs=[vmem] * n, out_specs=tuple([vmem] * n),
        compiler_params=pltpu.CompilerParams(vmem_limit_bytes=VMEM_LIMIT_BYTES))(*arrays)


def _rows(tr, w, cb=0):
    return pl.BlockSpec((tr, w), lambda i: (i, cb))


def _whole(shape):
    nd = len(shape)
    return pl.BlockSpec(tuple(shape), lambda i: (0,) * nd)


def _ffn_in(x16, w_in, name, token=None):
    t, d = x16.shape
    _, nb, _, fb = w_in.shape
    tm = _pick(t, 1024, 8)
    extra = [] if token is None else [token]

    def body(*refs):
        x_ref, wg_ref, wu_ref = refs[:3]
        h_ref, a_ref = refs[-2:]
        xv = x_ref[...]
        g = _dot(xv, wg_ref[0, 0])
        u = _dot(xv, wu_ref[0, 0])
        h_ref[0, 0] = g.astype(BF16)
        h_ref[0, 1] = u.astype(BF16)
        a_ref[0] = (g * _sigmoid(g) * u).astype(BF16)

    return pl.pallas_call(
        body, name=name,
        out_shape=(jax.ShapeDtypeStruct((nb, 2, t, fb), BF16), jax.ShapeDtypeStruct((nb, t, fb), BF16)),
        grid=(t // tm, nb),
        in_specs=[pl.BlockSpec((tm, d), lambda i, j: (i, 0)),
                  pl.BlockSpec((1, 1, d, fb), lambda i, j: (0, j, 0, 0)),
                  pl.BlockSpec((1, 1, d, fb), lambda i, j: (1, j, 0, 0))] + [_TOKEN] * len(extra),
        out_specs=(pl.BlockSpec((1, 2, tm, fb), lambda i, j: (j, 0, i, 0)),
                   pl.BlockSpec((1, tm, fb), lambda i, j: (j, i, 0))),
        compiler_params=_params(2))(x16, w_in, w_in, *extra)


def _ffn_out_ln(a, w_out, xin, g, b, alpha, name):
    nb, t, fb = a.shape
    d = w_out.shape[2]
    tm = _pick(t, 256, 8)

    def body(a_ref, w_ref, x_ref, g_ref, b_ref, r_ref, y_ref, y16_ref):
        f = _dot(a_ref[0], w_ref[0])
        for jb in range(1, nb):
            f = f + _dot(a_ref[jb], w_ref[jb])
        r = alpha * x_ref[...] + 0.5 * f
        xhat, _ = _ln_stats(r)
        y = xhat * g_ref[...] + b_ref[...]
        r_ref[...] = r
        y_ref[...] = y
        y16_ref[...] = y.astype(BF16)

    return pl.pallas_call(
        body, name=name,
        out_shape=(jax.ShapeDtypeStruct((t, d), F32), jax.ShapeDtypeStruct((t, d), F32),
                   jax.ShapeDtypeStruct((t, d), BF16)),
        grid=(t // tm,),
        in_specs=[pl.BlockSpec((nb, tm, fb), lambda i: (0, i, 0)), _whole(w_out.shape), _rows(tm, d),
                  _whole((1, d)), _whole((1, d))],
        out_specs=(_rows(tm, d), _rows(tm, d), _rows(tm, d)),
        compiler_params=_params())(a, w_out, xin, g, b)


def _ffn_out_bwd(drs, w_out, h, a, name):
    nb, t, fb = a.shape
    d = w_out.shape[2]
    tm = _pick(t, 1024, 8)
    n_i = t // tm

    def body(dr_ref, w_ref, h_ref, a_ref, dh_ref, dw_ref, acc_ref):
        i = pl.program_id(1)
        dr = dr_ref[...]
        da = _dot(dr, w_ref[0], tb=True)
        g, u = h_ref[0, 0].astype(F32), h_ref[0, 1].astype(F32)
        sg = _sigmoid(g)
        dh_ref[0, 0] = (da * u * (sg * (1.0 + g * (1.0 - sg)))).astype(BF16)
        dh_ref[0, 1] = (da * (g * sg)).astype(BF16)

        @pl.when(i == 0)
        def _():
            acc_ref[...] = jnp.zeros_like(acc_ref)

        acc_ref[...] += _dot(a_ref[0], dr, ta=True)

        @pl.when(i == n_i - 1)
        def _():
            dw_ref[0] = acc_ref[...].astype(BF16)

    return pl.pallas_call(
        body, name=name,
        out_shape=(jax.ShapeDtypeStruct((nb, 2, t, fb), BF16), jax.ShapeDtypeStruct((nb, fb, d), BF16)),
        grid=(nb, n_i),
        in_specs=[pl.BlockSpec((tm, d), lambda j, i: (i, 0)),
                  pl.BlockSpec((1, fb, d), lambda j, i: (j, 0, 0)),
                  pl.BlockSpec((1, 2, tm, fb), lambda j, i: (j, 0, i, 0)),
                  pl.BlockSpec((1, tm, fb), lambda j, i: (j, i, 0))],
        out_specs=(pl.BlockSpec((1, 2, tm, fb), lambda j, i: (j, 0, i, 0)),
                   pl.BlockSpec((1, fb, d), lambda j, i: (j, 0, 0))),
        scratch_shapes=[pltpu.VMEM((fb, d), F32)],
        compiler_params=_params(2))(drs, w_out, h, a)


def _ffn_in_dw(x16, dh, name, token=None):
    t, d = x16.shape
    nb, _, _, fb = dh.shape
    extra = [] if token is None else [token]

    def body(*refs):
        x_ref, dh_ref, o_ref = refs[0], refs[1], refs[-1]
        o_ref[0, 0] = _dot(x_ref[...], dh_ref[0, 0], ta=True).astype(BF16)

    return pl.pallas_call(
        body, name=name, out_shape=jax.ShapeDtypeStruct((2, nb, d, fb), BF16), grid=(nb, 2),
        in_specs=[pl.BlockSpec((t, d), lambda j, s: (0, 0)),
                  pl.BlockSpec((1, 1, t, fb), lambda j, s: (j, s, 0, 0))] + [_TOKEN] * len(extra),
        out_specs=pl.BlockSpec((1, 1, d, fb), lambda j, s: (s, j, 0, 0)),
        compiler_params=_params(2))(x16, dh, *extra)


def _ffn_in_dx(dh, w_in, add, add_scale, name, token=None):
    nb, _, t, fb = dh.shape
    d = w_in.shape[2]
    tm = _pick(t, 512, 8)
    has_add = add is not None
    extra = [] if token is None else [token]

    def body(*refs):
        dh_ref, w_ref = refs[:2]
        o_ref = refs[-1]
        acc = None
        for jb in range(nb):
            for s in range(2):
                part = _dot(dh_ref[jb, s], w_ref[s, jb], tb=True)
                acc = part if acc is None else acc + part
        if has_add:
            acc = acc + add_scale * refs[2][...]
        o_ref[...] = acc

    return pl.pallas_call(
        body, name=name, out_shape=jax.ShapeDtypeStruct((t, d), F32), grid=(t // tm,),
        in_specs=[pl.BlockSpec((nb, 2, tm, fb), lambda i: (0, 0, i, 0)), _whole(w_in.shape)]
        + ([_rows(tm, d)] if has_add else []) + [_TOKEN] * len(extra),
        out_specs=_rows(tm, d),
        compiler_params=_params())(*([dh, w_in] + ([add] if has_add else []) + extra))


def _resid_ln_fwd(xin, f, g, b, alpha, scale, name):
    t, d = xin.shape
    tr = _pick(t, 256, 8)

    def body(x_ref, f_ref, g_ref, b_ref, r_ref, y_ref, y16_ref):
        r = alpha * x_ref[...] + scale * f_ref[...]
        xhat, _ = _ln_stats(r)
        y = xhat * g_ref[...] + b_ref[...]
        r_ref[...] = r
        y_ref[...] = y
        y16_ref[...] = y.astype(BF16)

    return pl.pallas_call(
        body, name=name,
        out_shape=(jax.ShapeDtypeStruct((t, d), F32), jax.ShapeDtypeStruct((t, d), F32),
                   jax.ShapeDtypeStruct((t, d), BF16)),
        grid=(t // tr,),
        in_specs=[_rows(tr, d), _rows(tr, d), _whole((1, d)), _whole((1, d))],
        out_specs=(_rows(tr, d), _rows(tr, d), _rows(tr, d)),
        compiler_params=_params())(xin, f, g, b)


def _ln_bwd_call(r, dy_a, dy_b, b_scale, g, out_scale, name):
    t, d = r.shape
    tr = _pick(t, 256, 8)
    has_b = dy_b is not None

    def body(*refs):
        if has_b:
            r_ref, da_ref, db_ref, g_ref, dr_ref, drs_ref, dg_ref, dbeta_ref = refs
            dy = da_ref[...] + b_scale * db_ref[...]
        else:
            r_ref, da_ref, g_ref, dr_ref, drs_ref, dg_ref, dbeta_ref = refs
            dy = da_ref[...]
        xhat, rstd = _ln_stats(r_ref[...])
        dr = _ln_bwd(dy, xhat, rstd, g_ref[...])
        dr_ref[...] = dr
        drs_ref[...] = (out_scale * dr).astype(BF16)

        @pl.when(pl.program_id(0) == 0)
        def _():
            dg_ref[...] = jnp.zeros_like(dg_ref)
            dbeta_ref[...] = jnp.zeros_like(dbeta_ref)

        dg_ref[...] += _fold8(dy * xhat)
        dbeta_ref[...] += _fold8(dy)

    ins = [r, dy_a] + ([dy_b] if has_b else []) + [g]
    in_specs = [_rows(tr, d)] * (3 if has_b else 2) + [_whole((1, d))]
    return pl.pallas_call(
        body, name=name,
        out_shape=(jax.ShapeDtypeStruct((t, d), F32), jax.ShapeDtypeStruct((t, d), BF16),
                   jax.ShapeDtypeStruct((8, d), F32), jax.ShapeDtypeStruct((8, d), F32)),
        grid=(t // tr,), in_specs=in_specs,
        out_specs=(_rows(tr, d), _rows(tr, d), _whole((8, d)), _whole((8, d))),
        compiler_params=_params())(*ins)


def _take_row(v, row_id, s):
    return jnp.sum(jnp.where(row_id == s, v, 0.0), axis=0, keepdims=True)


def _seg_carries(f_re, f_im, p_re, p_im, reverse):
    row_id = lax.broadcasted_iota(jnp.int32, f_re.shape, 0)
    p_re, p_im = _take_row(p_re, row_id, 0), _take_row(p_im, row_id, 0)
    out_re, out_im = jnp.zeros_like(f_re), jnp.zeros_like(f_im)
    c_re, c_im = jnp.zeros_like(p_re), jnp.zeros_like(p_im)
    order = range(N_SEG - 1, -1, -1) if reverse else range(N_SEG)
    for s in order:
        out_re = jnp.where(row_id == s, c_re, out_re)
        out_im = jnp.where(row_id == s, c_im, out_im)
        fr, fi = _take_row(f_re, row_id, s), _take_row(f_im, row_id, s)
        c_re, c_im = fr + p_re * c_re - p_im * c_im, fi + p_re * c_im + p_im * c_re
    return out_re, out_im


def _s5_fwd(za16, b_re, b_im, a_re, a_im, c_re, c_im, n_seq, name):
    t = za16.shape[0]
    n_sg, ch, st = b_re.shape
    seq = t // n_seq
    steps = seq // N_SEG

    def body(za_ref, bre_ref, bim_ref, are, aim, cre_ref, cim_ref, hre, him, y_ref):
        za = za_ref[...]
        hre[...] = _dot(za, bre_ref[0])
        him[...] = _dot(za, bim_ref[0])
        ar = jnp.broadcast_to(are[...], (N_SEG, st))
        ai = jnp.broadcast_to(aim[...], (N_SEG, st))
        zero = jnp.zeros((N_SEG, st), F32)

        def local(k, carry):
            lr, li, pr, pi = carry
            rows = pl.ds(pl.multiple_of(k * N_SEG, N_SEG), N_SEG)
            nr = ar * lr - ai * li + hre[rows, :]
            ni = ar * li + ai * lr + him[rows, :]
            hre[rows, :] = nr
            him[rows, :] = ni
            return nr, ni, ar * pr - ai * pi, ar * pi + ai * pr

        f_re, f_im, p_re, p_im = lax.fori_loop(0, steps, local, (zero, zero, zero + 1.0, zero))
        c_re, c_im = _seg_carries(f_re, f_im, p_re, p_im, reverse=False)

        def fix(k, carry):
            pr, pi = carry
            rows = pl.ds(pl.multiple_of(k * N_SEG, N_SEG), N_SEG)
            hre[rows, :] = hre[rows, :] + (pr * c_re - pi * c_im)
            him[rows, :] = him[rows, :] + (pr * c_im + pi * c_re)
            return ar * pr - ai * pi, ar * pi + ai * pr

        lax.fori_loop(0, steps, fix, (ar, ai))
        y_ref[...] = _dot(hre[...], cre_ref[0]) + _dot(him[...], cim_ref[0])

    rows_ch = pl.BlockSpec((seq, ch), lambda s, j: (s, j))
    rows_st = pl.BlockSpec((seq, st), lambda s, j: (s, j))
    vec = pl.BlockSpec((1, st), lambda s, j: (0, j))
    b_blk = pl.BlockSpec((1, ch, st), lambda s, j: (j, 0, 0))
    c_blk = pl.BlockSpec((1, st, ch), lambda s, j: (j, 0, 0))
    return pl.pallas_call(
        body, name=name,
        out_shape=(jax.ShapeDtypeStruct((t, n_sg * st), F32), jax.ShapeDtypeStruct((t, n_sg * st), F32),
                   jax.ShapeDtypeStruct((t, n_sg * ch), F32)),
        grid=(n_seq, n_sg), in_specs=[rows_ch, b_blk, b_blk, vec, vec, c_blk, c_blk],
        out_specs=(rows_st, rows_st, rows_ch),
        compiler_params=_params(2))(za16, b_re, b_im, a_re, a_im, c_re, c_im)


def _s5_bwd(dy16, dza_skip, h_re, h_im, za16, b_re, b_im, a_re, a_im, c_re, c_im, n_seq, name):
    t = dy16.shape[0]
    n_sg, ch, st = b_re.shape
    seq = t // n_seq
    steps = seq // N_SEG

    def body(dy_ref, skip_ref, hre, him, za_ref, bre_ref, bim_ref, are, aim, cre_ref, cim_ref,
             dza_ref, dbre_ref, dbim_ref, dcre_ref, dcim_ref, dare, daim, lre, lim):
        dy = dy_ref[...]
        lre[...] = _dot(dy, cre_ref[0], tb=True)
        lim[...] = _dot(dy, cim_ref[0], tb=True)
        ar = jnp.broadcast_to(are[...], (N_SEG, st))
        ai = -jnp.broadcast_to(aim[...], (N_SEG, st))
        zero = jnp.zeros((N_SEG, st), F32)

        def local(i, carry):
            lr, li, pr, pi = carry
            k = steps - 1 - i
            rows = pl.ds(pl.multiple_of(k * N_SEG, N_SEG), N_SEG)
            nr = ar * lr - ai * li + lre[rows, :]
            ni = ar * li + ai * lr + lim[rows, :]
            lre[rows, :] = nr
            lim[rows, :] = ni
            return nr, ni, ar * pr - ai * pi, ar * pi + ai * pr

        f_re, f_im, p_re, p_im = lax.fori_loop(0, steps, local, (zero, zero, zero + 1.0, zero))
        c_re, c_im = _seg_carries(f_re, f_im, p_re, p_im, reverse=True)

        def accumulate(lam_r, lam_i, hp_r, hp_i, acc_r, acc_i):
            return acc_r + (lam_r * hp_r + lam_i * hp_i), acc_i + (lam_i * hp_r - lam_r * hp_i)

        def fix(i, carry):
            pr, pi, acc_r, acc_i = carry
            k = steps - 1 - i
            rows = pl.ds(pl.multiple_of(k * N_SEG, N_SEG), N_SEG)
            prev = pl.ds(pl.multiple_of((k - 1) * N_SEG, N_SEG), N_SEG)
            lam_r = lre[rows, :] + (pr * c_re - pi * c_im)
            lam_i = lim[rows, :] + (pr * c_im + pi * c_re)
            lre[rows, :] = lam_r
            lim[rows, :] = lam_i
            acc_r, acc_i = accumulate(lam_r, lam_i, hre[prev, :], him[prev, :], acc_r, acc_i)
            return ar * pr - ai * pi, ar * pi + ai * pr, acc_r, acc_i

        pr, pi, acc_r, acc_i = lax.fori_loop(0, steps - 1, fix, (ar, ai, zero, zero))
        first = pl.ds(0, N_SEG)
        last = pl.ds((steps - 1) * N_SEG, N_SEG)
        lam_r = lre[first, :] + (pr * c_re - pi * c_im)
        lam_i = lim[first, :] + (pr * c_im + pi * c_re)
        lre[first, :] = lam_r
        lim[first, :] = lam_i
        row_id = lax.broadcasted_iota(jnp.int32, (N_SEG, st), 0)
        hp_r = jnp.where(row_id == 0, 0.0, pltpu.roll(hre[last, :], 1, 0))
        hp_i = jnp.where(row_id == 0, 0.0, pltpu.roll(him[last, :], 1, 0))
        acc_r, acc_i = accumulate(lam_r, lam_i, hp_r, hp_i, acc_r, acc_i)

        lam_re16 = lre[...].astype(BF16)
        lam_im16 = lim[...].astype(BF16)
        dza_ref[...] = (_dot(lam_re16, bre_ref[0], tb=True) + _dot(lam_im16, bim_ref[0], tb=True)
                        + skip_ref[...]).astype(BF16)

        @pl.when(pl.program_id(1) == 0)
        def _():
            for ref in (dbre_ref, dbim_ref, dcre_ref, dcim_ref, dare, daim):
                ref[...] = jnp.zeros_like(ref)

        za = za_ref[...]
        dbre_ref[0] += _dot(za, lam_re16, ta=True)
        dbim_ref[0] += _dot(za, lam_im16, ta=True)
        dcre_ref[0] += _dot(hre[...], dy, ta=True)
        dcim_ref[0] += _dot(him[...], dy, ta=True)
        dare[...] += acc_r
        daim[...] += acc_i

    rows_ch = pl.BlockSpec((seq, ch), lambda j, s: (s, j))
    rows_st = pl.BlockSpec((seq, st), lambda j, s: (s, j))
    vec = pl.BlockSpec((1, st), lambda j, s: (0, j))
    b_blk = pl.BlockSpec((1, ch, st), lambda j, s: (j, 0, 0))
    c_blk = pl.BlockSpec((1, st, ch), lambda j, s: (j, 0, 0))
    acc = pl.BlockSpec((N_SEG, st), lambda j, s: (0, j))
    return pl.pallas_call(
        body, name=name,
        out_shape=(jax.ShapeDtypeStruct((t, n_sg * ch), BF16),
                   jax.ShapeDtypeStruct((n_sg, ch, st), F32), jax.ShapeDtypeStruct((n_sg, ch, st), F32),
                   jax.ShapeDtypeStruct((n_sg, st, ch), F32), jax.ShapeDtypeStruct((n_sg, st, ch), F32),
                   jax.ShapeDtypeStruct((N_SEG, n_sg * st), F32), jax.ShapeDtypeStruct((N_SEG, n_sg * st), F32)),
        grid=(n_sg, n_seq),
        in_specs=[rows_ch, rows_ch, rows_st, rows_st, rows_ch, b_blk, b_blk, vec, vec, c_blk, c_blk],
        out_specs=(rows_ch, b_blk, b_blk, c_blk, c_blk, acc, acc),
        scratch_shapes=[pltpu.VMEM((seq, st), F32), pltpu.VMEM((seq, st), F32)],
        compiler_params=_params(2))(dy16, dza_skip, h_re, h_im, za16, b_re, b_im, a_re, a_im, c_re, c_im)


def _s5_post_fwd(yssm, u, d_skip, glu_w, glu_b, name):
    t, w = yssm.shape
    tr = _pick(t, 512, 8)

    def body(y_ref, u_ref, d_ref, w_ref, b_ref, o_ref):
        yg = _gelu(y_ref[...] + d_ref[...] * u_ref[...])
        pre = _dot(yg, w_ref[...]) + b_ref[...]
        o_ref[...] = yg * _sigmoid(pre)

    return pl.pallas_call(
        body, name=name, out_shape=jax.ShapeDtypeStruct((t, w), F32), grid=(t // tr,),
        in_specs=[_rows(tr, w), _rows(tr, w), _whole((1, w)), _whole((w, w)), _whole((1, w))],
        out_specs=_rows(tr, w), compiler_params=_params())(yssm, u, d_skip, glu_w, glu_b)


def _s5_post_bwd(yssm, u, dout, d_skip, glu_w, glu_b, name):
    t, w = yssm.shape
    tr = _pick(t, 512, 8)

    def body(y_ref, u_ref, do_ref, d_ref, w_ref, b_ref, dy_ref, du_ref, dw_ref, dd_ref, db_ref):
        uu = u_ref[...]
        y = y_ref[...] + d_ref[...] * uu
        yg = _gelu(y)
        sg = _sigmoid(_dot(yg, w_ref[...]) + b_ref[...])
        do = do_ref[...]
        dpre = do * yg * sg * (1.0 - sg)
        dyg = do * sg + _dot(dpre, w_ref[...], tb=True)
        dy = dyg * _gelu_grad(y)
        dy_ref[...] = dy.astype(BF16)
        du_ref[...] = dy * d_ref[...]

        @pl.when(pl.program_id(0) == 0)
        def _():
            dw_ref[...] = jnp.zeros_like(dw_ref)
            dd_ref[...] = jnp.zeros_like(dd_ref)
            db_ref[...] = jnp.zeros_like(db_ref)

        dw_ref[...] += _dot(yg, dpre, ta=True)
        dd_ref[...] += _fold8(dy * uu)
        db_ref[...] += _fold8(dpre)

    return pl.pallas_call(
        body, name=name,
        out_shape=(jax.ShapeDtypeStruct((t, w), BF16), jax.ShapeDtypeStruct((t, w), F32),
                   jax.ShapeDtypeStruct((w, w), F32), jax.ShapeDtypeStruct((8, w), F32),
                   jax.ShapeDtypeStruct((8, w), F32)),
        grid=(t // tr,),
        in_specs=[_rows(tr, w), _rows(tr, w), _rows(tr, w), _whole((1, w)), _whole((w, w)),
                  _whole((1, w))],
        out_specs=(_rows(tr, w), _rows(tr, w), _whole((w, w)), _whole((8, w)), _whole((8, w))),
        compiler_params=_params())(yssm, u, dout, d_skip, glu_w, glu_b)


def _head_select(parts, head_dim):
    col_head = lax.broadcasted_iota(jnp.int32, parts[0].shape, 1) // head_dim
    out = jnp.zeros_like(parts[0])
    for h, part in enumerate(parts):
        out = jnp.where(col_head == h, part, out)
    return out


def _gmlp_fwd(proj, ln_g, ln_b, ws, bs_cols, name):
    t = proj.shape[0]
    n_heads = ws.shape[0]
    w = bs_cols.shape[1]
    head_dim = w // n_heads
    cpb = _pick(t // CHUNK, 4, 1)
    tr = cpb * CHUNK

    def body(zu_ref, zv_ref, g_ref, b_ref, ws_ref, bs_ref, o_ref):
        for c in range(cpb):
            rows = pl.ds(c * CHUNK, CHUNK)
            xhat, _ = _ln_stats(_gelu(zv_ref[rows, :]))
            vn = (xhat * g_ref[...] + b_ref[...]).astype(BF16)
            s = _head_select([_dot(ws_ref[h], vn) for h in range(n_heads)], head_dim) + bs_ref[...]
            o_ref[rows, :] = _gelu(zu_ref[rows, :]) * s

    return pl.pallas_call(
        body, name=name, out_shape=jax.ShapeDtypeStruct((t, w), F32), grid=(t // tr,),
        in_specs=[_rows(tr, w, 1), _rows(tr, w, 2), _whole((1, w)), _whole((1, w)),
                  _whole(ws.shape), _whole(bs_cols.shape)],
        out_specs=_rows(tr, w), compiler_params=_params())(proj, proj, ln_g, ln_b, ws, bs_cols)


def _gmlp_bwd(proj, dout, ln_g, ln_b, ws, ws_t, bs_cols, name):
    t = proj.shape[0]
    n_heads = ws.shape[0]
    w = bs_cols.shape[1]
    head_dim = w // n_heads
    cpb = _pick(t // CHUNK, 4, 1)
    tr = cpb * CHUNK

    def body(zu_ref, zv_ref, do_ref, g_ref, b_ref, ws_ref, wst_ref, bs_ref,
             dzu_ref, dzv_ref, dws_ref, dbs_ref, dg_ref, dbeta_ref):
        @pl.when(pl.program_id(0) == 0)
        def _():
            dws_ref[...] = jnp.zeros_like(dws_ref)
            dbs_ref[...] = jnp.zeros_like(dbs_ref)
            dg_ref[...] = jnp.zeros_like(dg_ref)
            dbeta_ref[...] = jnp.zeros_like(dbeta_ref)

        col_head = lax.broadcasted_iota(jnp.int32, (CHUNK, w), 1) // head_dim
        for c in range(cpb):
            rows = pl.ds(c * CHUNK, CHUNK)
            zu, zv, do = zu_ref[rows, :], zv_ref[rows, :], do_ref[rows, :]
            xhat, rstd = _ln_stats(_gelu(zv))
            vn = (xhat * g_ref[...] + b_ref[...]).astype(BF16)
            s = _head_select([_dot(ws_ref[h], vn) for h in range(n_heads)], head_dim) + bs_ref[...]
            dzu_ref[rows, :] = (do * s * _gelu_grad(zu)).astype(BF16)
            ds = do * _gelu(zu)
            ds16 = ds.astype(BF16)
            dbs_ref[...] += ds
            for h in range(n_heads):
                dws_ref[h] += _dot(jnp.where(col_head == h, ds16, jnp.zeros_like(ds16)), vn, tb=True)
            dvn = _head_select([_dot(wst_ref[h], ds16) for h in range(n_heads)], head_dim)
            dg_ref[...] += _fold8(dvn * xhat)
            dbeta_ref[...] += _fold8(dvn)
            dgv = _ln_bwd(dvn, xhat, rstd, g_ref[...])
            dzv_ref[rows, :] = (dgv * _gelu_grad(zv)).astype(BF16)

    return pl.pallas_call(
        body, name=name,
        out_shape=(jax.ShapeDtypeStruct((t, w), BF16), jax.ShapeDtypeStruct((t, w), BF16),
                   jax.ShapeDtypeStruct(ws.shape, F32), jax.ShapeDtypeStruct((CHUNK, w), F32),
                   jax.ShapeDtypeStruct((8, w), F32), jax.ShapeDtypeStruct((8, w), F32)),
        grid=(t // tr,),
        in_specs=[_rows(tr, w, 1), _rows(tr, w, 2), _rows(tr, w), _whole((1, w)), _whole((1, w)),
                  _whole(ws.shape), _whole(ws.shape), _whole(bs_cols.shape)],
        out_specs=(_rows(tr, w), _rows(tr, w), _whole(ws.shape), _whole((CHUNK, w)),
                   _whole((8, w)), _whole((8, w))),
        compiler_params=_params())(proj, proj, dout, ln_g, ln_b, ws, ws_t, bs_cols)


def _merge_fwd(s5out, gm, proj, up_a, up_b, name):
    t, w = s5out.shape
    d = up_a.shape[1]
    assert d == 2 * w
    tr = _pick(t, 256, 8)

    def body(s_ref, gm_ref, ga0, ga1, gb0, gb1, ua_ref, ub_ref, m_ref, ya_ref, yb_ref):
        ya = _dot(s_ref[...], ua_ref[...])
        yb = _dot(gm_ref[...], ub_ref[...])
        ya_ref[...] = ya
        yb_ref[...] = yb
        for half, (ga, gb) in enumerate(((ga0, gb0), (ga1, gb1))):
            cols = slice(half * w, (half + 1) * w)
            m_ref[:, cols] = (_sigmoid(ga[...]) * ya[:, cols] + _sigmoid(gb[...]) * yb[:, cols]).astype(BF16)

    return pl.pallas_call(
        body, name=name,
        out_shape=(jax.ShapeDtypeStruct((t, d), BF16), jax.ShapeDtypeStruct((t, d), F32),
                   jax.ShapeDtypeStruct((t, d), F32)),
        grid=(t // tr,),
        in_specs=[_rows(tr, w), _rows(tr, w), _rows(tr, w, 3), _rows(tr, w, 4), _rows(tr, w, 5),
                  _rows(tr, w, 6), _whole(up_a.shape), _whole(up_b.shape)],
        out_specs=(_rows(tr, d), _rows(tr, d), _rows(tr, d)),
        compiler_params=_params())(s5out, gm, proj, proj, proj, proj, up_a, up_b)


def _merge_bwd(dm, ya, yb, s5out, gm, proj, up_a, up_b, name):
    t, d = dm.shape
    w = d // 2
    tr = _pick(t, 256, 8)

    def body(dm_ref, ya_ref, yb_ref, s_ref, gm_ref, ga0, ga1, gb0, gb1, ua_ref, ub_ref,
             dga_ref, dgb_ref, ds_ref, dgm_ref, dua_ref, dub_ref):
        dm_v, ya, yb = dm_ref[...], ya_ref[...], yb_ref[...]
        dya, dyb = [], []
        for half, (ga, gb) in enumerate(((ga0, gb0), (ga1, gb1))):
            cols = slice(half * w, (half + 1) * w)
            sa, sb = _sigmoid(ga[...]), _sigmoid(gb[...])
            dmh = dm_v[:, cols]
            dga_ref[:, cols] = (dmh * ya[:, cols] * sa * (1.0 - sa)).astype(BF16)
            dgb_ref[:, cols] = (dmh * yb[:, cols] * sb * (1.0 - sb)).astype(BF16)
            dya.append((dmh * sa).astype(BF16))
            dyb.append((dmh * sb).astype(BF16))
        dya = jnp.concatenate(dya, axis=1)
        dyb = jnp.concatenate(dyb, axis=1)
        ds_ref[...] = _dot(dya, ua_ref[...], tb=True)
        dgm_ref[...] = _dot(dyb, ub_ref[...], tb=True)

        @pl.when(pl.program_id(0) == 0)
        def _():
            dua_ref[...] = jnp.zeros_like(dua_ref)
            dub_ref[...] = jnp.zeros_like(dub_ref)

        dua_ref[...] += _dot(s_ref[...], dya, ta=True)
        dub_ref[...] += _dot(gm_ref[...], dyb, ta=True)

    return pl.pallas_call(
        body, name=name,
        out_shape=(jax.ShapeDtypeStruct((t, d), BF16), jax.ShapeDtypeStruct((t, d), BF16),
                   jax.ShapeDtypeStruct((t, w), F32), jax.ShapeDtypeStruct((t, w), F32),
                   jax.ShapeDtypeStruct(up_a.shape, F32), jax.ShapeDtypeStruct(up_b.shape, F32)),
        grid=(t // tr,),
        in_specs=[_rows(tr, d), _rows(tr, d), _rows(tr, d), _rows(tr, w), _rows(tr, w),
                  _rows(tr, w, 3), _rows(tr, w, 4), _rows(tr, w, 5), _rows(tr, w, 6),
                  _whole(up_a.shape), _whole(up_b.shape)],
        out_specs=(_rows(tr, d), _rows(tr, d), _rows(tr, w), _rows(tr, w), _whole(up_a.shape),
                   _whole(up_b.shape)),
        compiler_params=_params())(dm, ya, yb, s5out, gm, proj, proj, proj, proj, up_a, up_b)


def _head(x3, p, target, w_gate, w_proj, name):
    t, d = x3.shape
    pd = p.shape[1]
    tr = _pick(t, 256, 8)
    nb = t // tr

    def body(x_ref, p_ref, t_ref, wg_ref, wp_ref, loss_ref, dx_ref, dwg_ref, dwp_ref):
        xv = x_ref[...]
        sg = _sigmoid(_dot(xv, wg_ref[...]))
        pp = _dot(p_ref[...], wp_ref[...])
        err = xv + sg * pp - t_ref[...]
        loss_ref[...] = jnp.full((8, 128), 0.5 * jnp.sum(jnp.mean(err * err, axis=-1)), F32)
        dout = err * (1.0 / d)
        dgpre = dout * pp * sg * (1.0 - sg)
        dpp = dout * sg
        dx_ref[...] = dout + _dot(dgpre, wg_ref[...], tb=True)

        @pl.when(pl.program_id(0) == 0)
        def _():
            dwg_ref[...] = jnp.zeros_like(dwg_ref)
            dwp_ref[...] = jnp.zeros_like(dwp_ref)

        dwg_ref[...] += _dot(xv, dgpre, ta=True)
        dwp_ref[...] += _dot(p_ref[...], dpp, ta=True)

    return pl.pallas_call(
        body, name=name,
        out_shape=(jax.ShapeDtypeStruct((nb * 8, 128), F32), jax.ShapeDtypeStruct((t, d), F32),
                   jax.ShapeDtypeStruct((d, d), F32), jax.ShapeDtypeStruct((pd, d), F32)),
        grid=(nb,),
        in_specs=[_rows(tr, d), _rows(tr, pd), _rows(tr, d), _whole((d, d)), _whole((pd, d))],
        out_specs=(pl.BlockSpec((8, 128), lambda i: (i, 0)), _rows(tr, d), _whole((d, d)),
                   _whole((pd, d))),
        compiler_params=_params())(x3, p, target, w_gate, w_proj)


_HBM = pl.BlockSpec(memory_space=pltpu.HBM)


def _all_gather(shards, name):
    n = len(shards)

    def body(*refs):
        x_refs, out_refs = refs[:n], refs[n:2 * n]
        send_sems, recv_sems, local_sems = refs[2 * n:]
        x, y, c = lax.axis_index("x"), lax.axis_index("y"), lax.axis_index("c")
        me, sibling = (x, y, c), (x, y, 1 - c)
        chips = [(1 - x, y), (x, 1 - y), (1 - x, 1 - y)]

        def copy(a, k, block, to, own=False):
            slot = out_refs[a].at[4 * block[0] + 2 * block[1] + block[2]]
            return pltpu.make_async_remote_copy(
                src_ref=x_refs[a] if own else slot, dst_ref=slot,
                send_sem=send_sems.at[7 * a + k], recv_sem=recv_sems.at[7 * a + k],
                device_id=to, device_id_type=pl.DeviceIdType.MESH)

        mine = [pltpu.make_async_copy(x_refs[a], out_refs[a].at[4 * x + 2 * y + c], local_sems.at[a])
                for a in range(n)]
        sends = []
        for a in range(n):
            mine[a].start()
            first = [copy(a, 0, me, sibling, own=True)]
            first += [copy(a, 1 + j, me, (*chip, c), own=True) for j, chip in enumerate(chips)]
            for cp in first:
                cp.start()
            sends += first
        for a in range(n):
            for j, chip in enumerate(chips):
                copy(a, 1 + j, (*chip, c), me).wait_recv()
                passed = copy(a, 4 + j, (*chip, c), sibling)
                passed.start()
                sends.append(passed)
        for a in range(n):
            copy(a, 0, sibling, me).wait_recv()
            for j, chip in enumerate(chips):
                copy(a, 4 + j, (*chip, 1 - c), me).wait_recv()
        for cp in sends:
            cp.wait_send()
        for cp in mine:
            cp.wait()

    return pl.pallas_call(
        body, name=name,
        out_shape=tuple(jax.ShapeDtypeStruct((N_DEV,) + s.shape, s.dtype) for s in shards),
        in_specs=[_HBM] * n, out_specs=tuple([_HBM] * n),
        scratch_shapes=[pltpu.SemaphoreType.DMA((7 * n,)), pltpu.SemaphoreType.DMA((7 * n,)),
                        pltpu.SemaphoreType.DMA((n,))],
    )(*shards)


_SEM = pl.BlockSpec(memory_space=pltpu.SEMAPHORE)
_ANY = pl.BlockSpec(memory_space=pl.ANY)
_DATAFLOW = pltpu.SideEffectType.DATAFLOW_SIDE_EFFECTING


def _peer(k, x, y, c):
    return (1 - x if k & 4 else x, 1 - y if k & 2 else y, 1 - c if k & 1 else c)


def _exchange_start(sends, after, name):
    n = len(sends)
    me = 4 * lax.axis_index("x") + 2 * lax.axis_index("y") + lax.axis_index("c")
    lands = []
    for s in sends:
        own = s[0] if s.shape[0] == 1 else lax.dynamic_index_in_dim(s, me, 0, keepdims=False)
        land = lax.empty((N_DEV,) + s.shape[1:], s.dtype)
        lands.append(lax.dynamic_update_index_in_dim(land, own, me, 0))

    def body(*refs):
        s_refs, l_refs = refs[:n], refs[n:2 * n]
        send_sems, recv_sems, token = refs[2 * n + 1], refs[2 * n + 2], refs[-1]
        x, y, c = lax.axis_index("x"), lax.axis_index("y"), lax.axis_index("c")
        for a in range(n):
            same = sends[a].shape[0] == 1
            for k in range(1, N_DEV):
                px, py, pc = _peer(k, x, y, c)
                pltpu.make_async_remote_copy(
                    src_ref=s_refs[a].at[0 if same else 4 * px + 2 * py + pc],
                    dst_ref=l_refs[a].at[4 * x + 2 * y + c],
                    send_sem=send_sems.at[7 * a + k - 1], recv_sem=recv_sems.at[7 * a + k - 1],
                    device_id=(px, py, pc), device_id_type=pl.DeviceIdType.MESH).start()
        token[...] = jnp.zeros_like(token)

    outs = pl.pallas_call(
        body, name=name,
        out_shape=(pltpu.SemaphoreType.DMA((7 * n,)), pltpu.SemaphoreType.DMA((7 * n,)),
                   *[pltpu.HBM(s.shape, s.dtype) for s in sends],
                   *[pltpu.HBM(l.shape, l.dtype) for l in lands],
                   jax.ShapeDtypeStruct((8, 128), F32)),
        in_specs=[_HBM] * (2 * n) + [_ANY],
        out_specs=(_SEM, _SEM, *([_HBM] * (2 * n)), pl.BlockSpec(memory_space=pltpu.VMEM)),
        input_output_aliases={i: 2 + i for i in range(2 * n)},
        compiler_params=pltpu.CompilerParams(has_side_effects=_DATAFLOW),
    )(*[pltpu.with_memory_space_constraint(v, pltpu.HBM) for v in list(sends) + lands], after)
    return (outs[0], outs[1], list(outs[2:2 + n]), list(outs[2 + n:2 + 2 * n])), outs[-1]


def _exchange_wait(handle, after, name):
    send_sems, recv_sems, sends, lands = handle
    n = len(sends)

    def body(*refs):
        s_refs, l_refs = refs[:n], refs[n:2 * n]
        send_sems, recv_sems = refs[2 * n], refs[2 * n + 1]
        x, y, c = lax.axis_index("x"), lax.axis_index("y"), lax.axis_index("c")
        for a in range(n):
            for k in range(1, N_DEV):
                copy = pltpu.make_async_remote_copy(
                    src_ref=s_refs[a].at[0], dst_ref=l_refs[a].at[0],
                    send_sem=send_sems.at[7 * a + k - 1], recv_sem=recv_sems.at[7 * a + k - 1],
                    device_id=_peer(k, x, y, c), device_id_type=pl.DeviceIdType.MESH)
                copy.wait_send()
                copy.wait_recv()

    outs = pl.pallas_call(
        body, name=name,
        out_shape=(*[pltpu.HBM(s.shape, s.dtype) for s in sends], *[pltpu.HBM(l.shape, l.dtype) for l in lands]),
        in_specs=[_HBM] * (2 * n) + [_SEM, _SEM] + [_ANY] * len(after),
        out_specs=tuple([_HBM] * (2 * n)),
        input_output_aliases={i: i for i in range(2 * n)},
        compiler_params=pltpu.CompilerParams(has_side_effects=_DATAFLOW),
    )(*sends, *lands, send_sems, recv_sems, *after)
    return list(outs[n:])


def _adamw(recv, w, m, v, name):
    n, rows, width = recv.shape
    tr = _pick(rows, max(8, ADAM_BLOCK_BYTES // (n * width * recv.dtype.itemsize)), 8)
    c_m = 1.0 - ADAM_B1 ** ADAM_STEP
    c_v = 1.0 - ADAM_B2 ** ADAM_STEP

    def body(r_ref, w_ref, m_ref, v_ref, g_ref, d_ref, nm_ref, nv_ref):
        g = r_ref[0].astype(F32)
        for i in range(1, n):
            g = g + r_ref[i].astype(F32)
        m2 = ADAM_B1 * m_ref[...] + (1.0 - ADAM_B1) * g
        v2 = ADAM_B2 * v_ref[...] + (1.0 - ADAM_B2) * (g * g)
        m_hat = m2 / c_m
        v_hat = v2 / c_v
        g_ref[...] = g
        d_ref[...] = -ADAM_LR * (m_hat / (jnp.sqrt(v_hat) + ADAM_EPS) + ADAM_WD * w_ref[...])
        nm_ref[...] = m2
        nv_ref[...] = v2

    blk = _rows(tr, width)
    shp = jax.ShapeDtypeStruct((rows, width), F32)
    return pl.pallas_call(
        body, name=name, out_shape=(shp, shp, shp, shp), grid=(rows // tr,),
        in_specs=[pl.BlockSpec((n, tr, width), lambda i: (0, i, 0)), blk, blk, blk],
        out_specs=(blk, blk, blk, blk), compiler_params=_params())(recv, w, m, v)


def _join_cols(gathered):
    n, r, c = gathered.shape
    return gathered.transpose(1, 0, 2).reshape(r, n * c)


def _split_cols(full):
    r, c = full.shape
    return full.reshape(r, N_DEV, c // N_DEV).transpose(1, 0, 2)


def _small_rows(size):
    return -(-size // (8 * PACK_W)) * 8


def _pack_small(parts, names):
    rows = []
    for n in names:
        flat = parts[n].reshape(-1)
        r = _small_rows(flat.shape[0])
        rows.append(jnp.pad(flat, (0, r * PACK_W - flat.shape[0])).reshape(r, PACK_W))
    return jnp.concatenate(rows, axis=0)


def _unpack_small(packed, shapes, names):
    out, r0 = {}, 0
    for n in names:
        size = math.prod(shapes[n])
        rows = _small_rows(size)
        out[n] = packed[r0:r0 + rows].reshape(-1)[:size].reshape(shapes[n])
        r0 += rows
    return out


def _discretise(lam_re, lam_im, log_dt, b_re, b_im):
    dt = jnp.exp(log_dt)[:, None]
    mag = jnp.exp(lam_re * dt)
    ab_re = mag * jnp.cos(lam_im * dt)
    ab_im = mag * jnp.sin(lam_im * dt)
    nr = ab_re - 1.0
    ni = ab_im
    den = lam_re * lam_re + lam_im * lam_im
    coef_re = ((nr * lam_re + ni * lam_im) / den)[..., None]
    coef_im = ((ni * lam_re - nr * lam_im) / den)[..., None]
    bb_re = coef_re * b_re - coef_im * b_im
    bb_im = coef_re * b_im + coef_im * b_re
    return ab_re, ab_im, bb_re, bb_im


def _same_group(gl):
    return jnp.eye(gl, dtype=bool)[None, :, None, :, None]


def _super_groups_in(bb, gl):
    g, p, i = bb.shape
    blocks = bb.reshape(g // gl, gl, p, i).transpose(0, 1, 3, 2)[:, :, :, None, :]
    return jnp.where(_same_group(gl), blocks, 0.0).reshape(g // gl, gl * i, gl * p)


def _super_groups_in_take(dense, gl, p, i):
    n_sg = dense.shape[0]
    blocks = jnp.where(_same_group(gl), dense.reshape(n_sg, gl, i, gl, p), 0.0).sum(axis=3)
    return blocks.transpose(0, 1, 3, 2).reshape(n_sg * gl, p, i)


def _super_groups_out(cc, gl):
    g, i, p = cc.shape
    blocks = cc.reshape(g // gl, gl, i, p).transpose(0, 1, 3, 2)[:, :, :, None, :]
    return jnp.where(_same_group(gl), blocks, 0.0).reshape(g // gl, gl * p, gl * i)


def _super_groups_out_take(dense, gl, i, p):
    n_sg = dense.shape[0]
    blocks = jnp.where(_same_group(gl), dense.reshape(n_sg, gl, p, gl, i), 0.0).sum(axis=3)
    return blocks.transpose(0, 1, 3, 2).reshape(n_sg * gl, i, p)


def _perm_rows(z, n_seq):
    t, w = z.shape
    steps = t // n_seq // N_SEG
    return z.reshape(n_seq, N_SEG, steps, w).transpose(0, 2, 1, 3).reshape(t, w)


def _unperm_rows(z, n_seq):
    t, w = z.shape
    steps = t // n_seq // N_SEG
    return z.reshape(n_seq, steps, N_SEG, w).transpose(0, 2, 1, 3).reshape(t, w)


def kernel(*args):
    assert len(args) == len(INPUT_NAMES)
    inp = dict(zip(INPUT_NAMES, args))
    depth = inp["ffn1_w_in"].shape[0]
    assert depth == 1
    alpha = (2.0 * depth) ** 0.25

    n_seq, seq, d_model = inp["x"].shape
    t = n_seq * seq
    x = inp["x"].reshape(t, d_model)
    x16 = x.astype(BF16)
    p16 = inp["p"][0].reshape(t, -1).astype(BF16)
    target = inp["loss_target"].reshape(t, d_model)
    wts = {n: inp[n][0] for n in WEIGHTS}
    mom = {n: inp["m_" + n][0] for n in WEIGHTS}
    var = {n: inp["v_" + n][0] for n in WEIGHTS}

    full = {}

    def place(n, g):
        if n in ("ffn1_w_in", "ffn2_w_in"):
            full[n] = g.reshape((2, N_DEV // 2) + g.shape[1:])
        elif n in ("ffn1_w_out", "ffn2_w_out"):
            full[n] = g.reshape(N_DEV // 2, 2 * g.shape[1], g.shape[2])
        elif n in COL_SHARDED:
            full[n] = _join_cols(g)
        else:
            full[n] = g.reshape(N_DEV * g.shape[1], g.shape[2])

    w16 = dict(zip(GATHER_FIRST, _to_bf16([wts[n] for n in GATHER_FIRST], "cast_ffn1")))
    later = GATHER_MIX + GATHER_LAST
    w16.update(zip(later, _to_bf16([wts[n] for n in later], "cast_rest")))
    for n, g in zip(GATHER_FIRST, _all_gather([w16[n] for n in GATHER_FIRST], "gather_ffn1")):
        place(n, g)
    gather_mix, gather_mix_token = _exchange_start([w16[n][None] for n in GATHER_MIX], full["ffn1_w_out"],
                                                   "gather_mix_start")

    def row(v):
        return v.reshape(1, -1)

    n_groups, n_state = wts["ssm_lambda_re"].shape
    n_ch = wts["ssm_b_re"].shape[2]
    disc_in = (wts["ssm_lambda_re"], wts["ssm_lambda_im"], wts["ssm_log_dt"], wts["ssm_b_re"],
               wts["ssm_b_im"])
    (ab_re, ab_im, bb_re, bb_im), disc_vjp = jax.vjp(_discretise, *disc_in)
    a_re, a_im = row(ab_re), row(ab_im)
    gl = S5_CHANNELS_PER_STEP // n_ch
    b_sg_re = _super_groups_in(bb_re, gl).astype(BF16)
    b_sg_im = _super_groups_in(bb_im, gl).astype(BF16)
    c_sg_re = _super_groups_out(wts["ssm_c_re"], gl).astype(BF16)
    c_sg_im = _super_groups_out(-wts["ssm_c_im"], gl).astype(BF16)

    ws = wts["gmlp_w_s"]
    n_heads = ws.shape[0]
    d_gmlp = wts["gmlp_ln_g"].shape[0]
    causal = jnp.tril(jnp.ones((CHUNK, CHUNK), dtype=bool))
    ws_masked = jnp.where(causal[None], ws, 0.0).astype(BF16)
    ws_masked_t = ws_masked.transpose(0, 2, 1)
    bs_cols = jnp.repeat(wts["gmlp_b_s"].T, d_gmlp // n_heads, axis=1)
    d_ssm = n_groups * n_ch
    assert d_ssm == d_gmlp and d_model == 2 * d_ssm

    h1, a1 = _ffn_in(x16, full["ffn1_w_in"], "ffn1_in", token=gather_mix_token)
    r1, x1, x1_16 = _ffn_out_ln(a1, full["ffn1_w_out"], x, row(wts["ln1_g"]), row(wts["ln1_b"]), alpha,
                                "ffn1_out_ln")

    for n, g in zip(GATHER_MIX, _exchange_wait(gather_mix, [x1_16], "gather_mix_wait")):
        place(n, g)
    gather_last, token = _exchange_start([w16[n][None] for n in GATHER_LAST], full["mix_w_in"],
                                         "gather_ffn2_start")
    proj = _mm(x1_16, full["mix_w_in"], name="mix_in", token=token)
    za_p = _perm_rows(proj[:, :d_ssm], n_seq)
    za_p16 = za_p.astype(BF16)
    h_re, h_im, yssm = _s5_fwd(za_p16, b_sg_re, b_sg_im, a_re, a_im, c_sg_re, c_sg_im, n_seq, "s5_fwd")
    s5out_p = _s5_post_fwd(yssm, za_p, row(wts["ssm_d"]), full["ssm_glu_w"], row(wts["ssm_glu_b"]),
                           "s5_post")
    s5out = _unperm_rows(s5out_p, n_seq)
    gm = _gmlp_fwd(proj, row(wts["gmlp_ln_g"]), row(wts["gmlp_ln_b"]), ws_masked, bs_cols, "gmlp")
    m_mix, y_a, y_b = _merge_fwd(s5out, gm, proj, full["up_a"], full["up_b"], "merge")
    mixed = _mm(m_mix, full["mix_w_out"], name="mix_out")
    r2, x2, x2_16 = _resid_ln_fwd(x1, mixed, row(wts["ln2_g"]), row(wts["ln2_b"]), alpha, 1.0, "ln2")

    for n, g in zip(GATHER_LAST, _exchange_wait(gather_last, [x2_16], "gather_ffn2_wait")):
        place(n, g)
    h2, a2 = _ffn_in(x2_16, full["ffn2_w_in"], "ffn2_in")
    r3, x3, _ = _ffn_out_ln(a2, full["ffn2_w_out"], x2, row(wts["ln3_g"]), row(wts["ln3_b"]), alpha,
                            "ffn2_out_ln")

    small = {}
    send = {}
    loss_parts, dx3, dw_gate, dw_proj = _head(x3, p16, target, full["ple_w_gate"], full["ple_w_proj"], "head")
    loss = lax.psum(jnp.sum(loss_parts[::8, 0]), ("x", "y", "c"))
    send["ple_w_gate"] = dw_gate.astype(BF16).reshape(N_DEV, -1, d_model)
    send["ple_w_proj"] = _split_cols(dw_proj.astype(BF16))

    dr3, dr3_h, dg, db = _ln_bwd_call(r3, dx3, None, 1.0, row(wts["ln3_g"]), 0.5, "ln3_bwd")
    small["ln3_g"], small["ln3_b"] = dg.sum(0), db.sum(0)
    dh2, dw = _ffn_out_bwd(dr3_h, full["ffn2_w_out"], h2, a2, "ffn2_out_bwd")
    send["ffn2_w_out"] = dw.reshape(N_DEV, -1, d_model)
    dw = _ffn_in_dw(x2_16, dh2, "ffn2_in_dw")
    send["ffn2_w_in"] = dw.reshape((N_DEV,) + dw.shape[2:])
    group1 = ("ffn2_w_in", "ffn2_w_out", "ple_w_gate", "ple_w_proj")
    exchange1, token = _exchange_start([send[n] for n in group1], dh2, "grad_exchange1_start")
    dx2_f = _ffn_in_dx(dh2, full["ffn2_w_in"], None, 1.0, "ffn2_in_dx", token=token)

    dr2, dr2_h, dg, db = _ln_bwd_call(r2, dx2_f, dr3, alpha, row(wts["ln2_g"]), 1.0, "ln2_bwd")
    small["ln2_g"], small["ln2_b"] = dg.sum(0), db.sum(0)
    dm = _mm(dr2_h, full["mix_w_out"], tb=True, name="mix_out_dx")
    send["mix_w_out"] = _mm(m_mix, dr2_h, ta=True, name="mix_out_dw", out_dtype=BF16).reshape(
        N_DEV, -1, d_model)
    dga, dgb, ds5out, dgm, dw_a, dw_b = _merge_bwd(
        dm, y_a, y_b, s5out, gm, proj, full["up_a"], full["up_b"], "merge_bwd")
    send["up_a"] = _split_cols(dw_a.astype(BF16))
    send["up_b"] = _split_cols(dw_b.astype(BF16))

    dzu, dzv, dws, dbs_cols, dg, db = _gmlp_bwd(
        proj, dgm, row(wts["gmlp_ln_g"]), row(wts["gmlp_ln_b"]), ws_masked, ws_masked_t, bs_cols,
        "gmlp_bwd")
    small["gmlp_ln_g"], small["gmlp_ln_b"] = dg.sum(0), db.sum(0)
    small["gmlp_w_s"] = jnp.where(causal[None], dws, 0.0)
    small["gmlp_b_s"] = dbs_cols.reshape(CHUNK, n_heads, -1).sum(-1).T

    ds5out_p = _perm_rows(ds5out, n_seq)
    dy_p, dza_skip, dw_glu, dd, dgb_glu = _s5_post_bwd(
        yssm, za_p, ds5out_p, row(wts["ssm_d"]), full["ssm_glu_w"], row(wts["ssm_glu_b"]),
        "s5_post_bwd")
    send["ssm_glu_w"] = dw_glu.astype(BF16).reshape(N_DEV, -1, d_ssm)
    small["ssm_d"], small["ssm_glu_b"] = dd.sum(0), dgb_glu.sum(0)
    dza_p, dbb_re, dbb_im, dc_re, dc_im, da_re, da_im = _s5_bwd(
        dy_p, dza_skip, h_re, h_im, za_p16, b_sg_re, b_sg_im, a_re, a_im, c_sg_re, c_sg_im, n_seq, "s5_bwd")
    small["ssm_c_re"] = _super_groups_out_take(dc_re, gl, n_ch, n_state)
    small["ssm_c_im"] = -_super_groups_out_take(dc_im, gl, n_ch, n_state)
    dbb_re = _super_groups_in_take(dbb_re, gl, n_state, n_ch)
    dbb_im = _super_groups_in_take(dbb_im, gl, n_state, n_ch)
    dab_re = da_re.sum(0).reshape(n_groups, n_state)
    dab_im = da_im.sum(0).reshape(n_groups, n_state)
    (small["ssm_lambda_re"], small["ssm_lambda_im"], small["ssm_log_dt"], small["ssm_b_re"],
     small["ssm_b_im"]) = disc_vjp((dab_re, dab_im, dbb_re, dbb_im))

    dproj = jnp.concatenate([_unperm_rows(dza_p, n_seq), dzu, dzv, dga, dgb], axis=1)
    send["mix_w_in"] = _split_cols(_mm(x1_16, dproj, ta=True, name="mix_in_dw", out_dtype=BF16))
    group2 = ("mix_w_in", "mix_w_out", "up_a", "up_b", "ssm_glu_w")
    exchange2, token = _exchange_start(
        [send[n] for n in group2] + [_pack_small(small, SMALL_EARLY)[None]], dproj, "grad_exchange2_start")
    dx1_f = _mm(dproj, full["mix_w_in"], tb=True, name="mix_in_dx", token=token)

    dr1, dr1_h, dg, db = _ln_bwd_call(r1, dx1_f, dr2, alpha, row(wts["ln1_g"]), 0.5, "ln1_bwd")
    small["ln1_g"], small["ln1_b"] = dg.sum(0), db.sum(0)
    dh1, dw = _ffn_out_bwd(dr1_h, full["ffn1_w_out"], h1, a1, "ffn1_out_bwd")
    send["ffn1_w_out"] = dw.reshape(N_DEV, -1, d_model)
    exchange3, token = _exchange_start([send["ffn1_w_out"], _pack_small(small, SMALL_LATE)[None]], dh1,
                                       "grad_exchange3_start")
    dw = _ffn_in_dw(x16, dh1, "ffn1_in_dw", token=token)
    send["ffn1_w_in"] = dw.reshape((N_DEV,) + dw.shape[2:])
    exchange4, token = _exchange_start([send["ffn1_w_in"]], dh1, "grad_exchange4_start")
    grad_x = _ffn_in_dx(dh1, full["ffn1_w_in"], dr1, alpha, "ffn1_in_dx", token=token)

    results = {}

    def update(names, recv, prefix):
        for n, r in zip(names, recv):
            results[n] = _adamw(r, wts[n], mom[n], var[n], prefix + n)

    def update_small(names, recv, name):
        packed = _adamw(recv, _pack_small(wts, names), _pack_small(mom, names), _pack_small(var, names), name)
        shapes = {n: wts[n].shape for n in names}
        unpacked = [_unpack_small(pk, shapes, names) for pk in packed]
        for n in names:
            results[n] = tuple(u[n] for u in unpacked)

    def done(names):
        return [results[n][3] for n in names]

    update(group1, _exchange_wait(exchange1, [grad_x], "grad_exchange1_wait"), "adamw_")
    recv = _exchange_wait(exchange2, done(group1), "grad_exchange2_wait")
    update(group2, recv[:-1], "adamw_")
    update_small(SMALL_EARLY, recv[-1], "adamw_small")
    recv = _exchange_wait(exchange3, done(group2 + SMALL_EARLY), "grad_exchange3_wait")
    update(("ffn1_w_out",), recv[:1], "adamw_")
    update_small(SMALL_LATE, recv[1], "adamw_ln1")
    recv = _exchange_wait(exchange4, done(("ffn1_w_out",) + SMALL_LATE), "grad_exchange4_wait")
    update(("ffn1_w_in",), recv, "adamw_")

    outs = [loss, grad_x.reshape(n_seq, seq, d_model)]
    for k in range(4):
        outs += [results[n][k][None] for n in WEIGHTS]
    return tuple(outs)
```

```python
import functools
import math

import jax
import jax.numpy as jnp
from jax import lax
from jax.experimental import pallas as pl
from jax.experimental.pallas import tpu as pltpu

F32 = jnp.float32
BF16 = jnp.bfloat16

N_DEV = 8
LN_EPS = 1e-5
CHUNK = 128
N_SEG = 8
S5_CHANNELS_PER_STEP = 128
VMEM_LIMIT_BYTES = 56 * 1024 * 1024
MM_OPERAND_BLOCK_BYTES = 8 * 1024 * 1024
ADAM_BLOCK_BYTES = 6 * 1024 * 1024

ADAM_LR = 0.001
ADAM_B1 = 0.9
ADAM_B2 = 0.999
ADAM_EPS = 1e-08
ADAM_WD = 0.01
ADAM_STEP = 10

COL_SHARDED = ("ffn1_w_in", "mix_w_in", "up_a", "up_b", "ffn2_w_in", "ple_w_proj")
ROW_SHARDED = ("ffn1_w_out", "ssm_glu_w", "mix_w_out", "ffn2_w_out", "ple_w_gate")
SHARDED = ("ffn1_w_in", "ffn1_w_out", "mix_w_in", "ssm_glu_w", "up_a", "up_b", "mix_w_out",
           "ffn2_w_in", "ffn2_w_out", "ple_w_proj", "ple_w_gate")
WEIGHTS = ("ffn1_w_in", "ffn1_w_out", "ln1_g", "ln1_b", "mix_w_in", "ssm_lambda_re", "ssm_lambda_im",
           "ssm_log_dt", "ssm_b_re", "ssm_b_im", "ssm_c_re", "ssm_c_im", "ssm_d", "ssm_glu_w",
           "ssm_glu_b", "gmlp_ln_g", "gmlp_ln_b", "gmlp_w_s", "gmlp_b_s", "up_a", "up_b", "mix_w_out",
           "ln2_g", "ln2_b", "ffn2_w_in", "ffn2_w_out", "ln3_g", "ln3_b", "ple_w_proj", "ple_w_gate")
GATHER_FIRST = ("ffn1_w_in", "ffn1_w_out")
GATHER_MIX = ("mix_w_in", "ssm_glu_w", "up_a", "up_b", "mix_w_out")
GATHER_LAST = ("ffn2_w_in", "ffn2_w_out", "ple_w_proj", "ple_w_gate")
SMALL = tuple(n for n in WEIGHTS if n not in SHARDED)
SMALL_LATE = ("ln1_g", "ln1_b")
SMALL_EARLY = tuple(n for n in SMALL if n not in SMALL_LATE)
INPUT_NAMES = ("x", "p") + WEIGHTS + ("loss_target",) + tuple("m_" + n for n in WEIGHTS) + tuple(
    "v_" + n for n in WEIGHTS)


def _pick(dim, pref, mult=128):
    if dim <= pref:
        return dim
    d = (pref // mult) * mult
    while d >= mult:
        if dim % d == 0:
            return d
        d -= mult
    return dim


def _params(n_axes=1):
    return pltpu.CompilerParams(dimension_semantics=("arbitrary",) * n_axes,
                                vmem_limit_bytes=VMEM_LIMIT_BYTES)


def _sigmoid(v):
    return 1.0 / (1.0 + jnp.exp(-v))


_GELU_C = math.sqrt(2.0 / math.pi)


def _gelu(v):
    return 0.5 * v * (1.0 + jnp.tanh(_GELU_C * (v + 0.044715 * (v * v * v))))


def _gelu_grad(v):
    t = jnp.tanh(_GELU_C * (v + 0.044715 * (v * v * v)))
    return 0.5 * (1.0 + t) + 0.5 * v * (1.0 - t * t) * (_GELU_C * (1.0 + 3.0 * 0.044715 * v * v))


def _ln_stats(r):
    mu = jnp.mean(r, axis=-1, keepdims=True)
    xc = r - mu
    var = jnp.mean(xc * xc, axis=-1, keepdims=True)
    rstd = lax.rsqrt(var + LN_EPS)
    return xc * rstd, rstd


def _ln_bwd(dy, xhat, rstd, g):
    dxh = dy * g
    m1 = jnp.mean(dxh, axis=-1, keepdims=True)
    m2 = jnp.mean(dxh * xhat, axis=-1, keepdims=True)
    return rstd * (dxh - m1 - xhat * m2)


def _fold8(v):
    rows, w = v.shape
    return jnp.sum(v.reshape(rows // 8, 8, w), axis=0)


def _dot(a, b, ta=False, tb=False):
    dn = (((0 if ta else 1,), (1 if tb else 0,)), ((), ()))
    return lax.dot_general(a.astype(BF16), b.astype(BF16), dn, preferred_element_type=F32)


_TOKEN = pl.BlockSpec((8, 128), lambda *_: (0, 0))


def _mm(a, b, *, name, ta=False, tb=False, out_dtype=F32, add=None, add_scale=1.0, token=None,
        tm=2048, tn=512, tk=4096):
    m_dim = a.shape[1] if ta else a.shape[0]
    k_dim = a.shape[0] if ta else a.shape[1]
    n_dim = b.shape[0] if tb else b.shape[1]
    assert (b.shape[1] if tb else b.shape[0]) == k_dim, (name, a.shape, b.shape)
    tk = _pick(k_dim, tk)
    tm = min(tm, max(256, MM_OPERAND_BLOCK_BYTES // (tk * a.dtype.itemsize)))
    tm, tn = _pick(m_dim, tm), _pick(n_dim, tn)
    nk = k_dim // tk
    has_add = add is not None

    def finish(r, add_ref, o_ref):
        if has_add:
            r = r + add_scale * add_ref[...].astype(F32)
        o_ref[...] = r.astype(out_dtype)

    def body(*refs):
        a_ref, b_ref = refs[:2]
        add_ref = refs[2] if has_add else None
        o_ref = refs[n_in]
        if nk == 1:
            finish(_dot(a_ref[...], b_ref[...], ta, tb), add_ref, o_ref)
            return
        acc_ref = refs[-1]
        k = pl.program_id(2)

        @pl.when(k == 0)
        def _():
            acc_ref[...] = jnp.zeros_like(acc_ref)

        acc_ref[...] += _dot(a_ref[...], b_ref[...], ta, tb)

        @pl.when(k == nk - 1)
        def _():
            finish(acc_ref[...], add_ref, o_ref)

    a_spec = (pl.BlockSpec((tk, tm), lambda i, j, k: (k, i)) if ta
              else pl.BlockSpec((tm, tk), lambda i, j, k: (i, k)))
    b_spec = (pl.BlockSpec((tn, tk), lambda i, j, k: (j, k)) if tb
              else pl.BlockSpec((tk, tn), lambda i, j, k: (k, j)))
    in_specs = [a_spec, b_spec]
    operands = [a, b]
    if has_add:
        in_specs.append(pl.BlockSpec((tm, tn), lambda i, j, k: (i, j)))
        operands.append(add)
    if token is not None:
        in_specs.append(_TOKEN)
        operands.append(token)
    n_in = len(operands)
    return pl.pallas_call(
        body, name=name,
        out_shape=jax.ShapeDtypeStruct((m_dim, n_dim), out_dtype),
        grid=(m_dim // tm, n_dim // tn, nk),
        in_specs=in_specs,
        out_specs=pl.BlockSpec((tm, tn), lambda i, j, k: (i, j)),
        scratch_shapes=[pltpu.VMEM((tm, tn), F32)] if nk > 1 else [],
        compiler_params=_params(3),
    )(*operands)


def _to_bf16(arrays, name):
    n = len(arrays)

    def body(*refs):
        for src, dst in zip(refs[:n], refs[n:]):
            dst[...] = src[...].astype(BF16)

    vmem = pl.BlockSpec(memory_space=pltpu.VMEM)
    return pl.pallas_call(
        body, name=name, out_shape=tuple(jax.ShapeDtypeStruct(a.shape, BF16) for a in arrays),
        in_specs=[vmem] * n, out_specs=tuple([vmem] * n),
        compiler_params=pltpu.CompilerParams(vmem_limit_bytes=VMEM_LIMIT_BYTES))(*arrays)


def _rows(tr, w, cb=0):
    return pl.BlockSpec((tr, w), lambda i: (i, cb))


def _whole(shape):
    nd = len(shape)
    return pl.BlockSpec(tuple(shape), lambda i: (0,) * nd)


def _ffn_in(x16, w_in, name, token=None):
    t, d = x16.shape
    _, nb, _, fb = w_in.shape
    tm = _pick(t, 1024, 8)
    extra = [] if token is None else [token]

    def body(*refs):
        x_ref, wg_ref, wu_ref = refs[:3]
        h_ref, a_ref = refs[-2:]
        xv = x_ref[...]
        g = _dot(xv, wg_ref[0, 0])
        u = _dot(xv, wu_ref[0, 0])
        h_ref[0, 0] = g.astype(BF16)
        h_ref[0, 1] = u.astype(BF16)
        a_ref[0] = (g * _sigmoid(g) * u).astype(BF16)

    return pl.pallas_call(
        body, name=name,
        out_shape=(jax.ShapeDtypeStruct((nb, 2, t, fb), BF16), jax.ShapeDtypeStruct((nb, t, fb), BF16)),
        grid=(t // tm, nb),
        in_specs=[pl.BlockSpec((tm, d), lambda i, j: (i, 0)),
                  pl.BlockSpec((1, 1, d, fb), lambda i, j: (0, j, 0, 0)),
                  pl.BlockSpec((1, 1, d, fb), lambda i, j: (1, j, 0, 0))] + [_TOKEN] * len(extra),
        out_specs=(pl.BlockSpec((1, 2, tm, fb), lambda i, j: (j, 0, i, 0)),
                   pl.BlockSpec((1, tm, fb), lambda i, j: (j, i, 0))),
        compiler_params=_params(2))(x16, w_in, w_in, *extra)


def _ffn_out_ln(a, w_out, xin, g, b, alpha, name):
    nb, t, fb = a.shape
    d = w_out.shape[2]
    tm = _pick(t, 256, 8)

    def body(a_ref, w_ref, x_ref, g_ref, b_ref, r_ref, y_ref, y16_ref):
        f = _dot(a_ref[0], w_ref[0])
        for jb in range(1, nb):
            f = f + _dot(a_ref[jb], w_ref[jb])
        r = alpha * x_ref[...] + 0.5 * f
        xhat, _ = _ln_stats(r)
        y = xhat * g_ref[...] + b_ref[...]
        r_ref[...] = r
        y_ref[...] = y
        y16_ref[...] = y.astype(BF16)

    return pl.pallas_call(
        body, name=name,
        out_shape=(jax.ShapeDtypeStruct((t, d), F32), jax.ShapeDtypeStruct((t, d), F32),
                   jax.ShapeDtypeStruct((t, d), BF16)),
        grid=(t // tm,),
        in_specs=[pl.BlockSpec((nb, tm, fb), lambda i: (0, i, 0)), _whole(w_out.shape), _rows(tm, d),
                  _whole((1, d)), _whole((1, d))],
        out_specs=(_rows(tm, d), _rows(tm, d), _rows(tm, d)),
        compiler_params=_params())(a, w_out, xin, g, b)


def _ffn_out_bwd(drs, w_out, h, a, name):
    nb, t, fb = a.shape
    d = w_out.shape[2]
    tm = _pick(t, 1024, 8)
    n_i = t // tm

    def body(dr_ref, w_ref, h_ref, a_ref, dh_ref, dw_ref, acc_ref):
        i = pl.program_id(1)
        dr = dr_ref[...]
        da = _dot(dr, w_ref[0], tb=True)
        g, u = h_ref[0, 0].astype(F32), h_ref[0, 1].astype(F32)
        sg = _sigmoid(g)
        dh_ref[0, 0] = (da * u * (sg * (1.0 + g * (1.0 - sg)))).astype(BF16)
        dh_ref[0, 1] = (da * (g * sg)).astype(BF16)

        @pl.when(i == 0)
        def _():
            acc_ref[...] = jnp.zeros_like(acc_ref)

        acc_ref[...] += _dot(a_ref[0], dr, ta=True)

        @pl.when(i == n_i - 1)
        def _():
            dw_ref[0] = acc_ref[...].astype(BF16)

    return pl.pallas_call(
        body, name=name,
        out_shape=(jax.ShapeDtypeStruct((nb, 2, t, fb), BF16), jax.ShapeDtypeStruct((nb, fb, d), BF16)),
        grid=(nb, n_i),
        in_specs=[pl.BlockSpec((tm, d), lambda j, i: (i, 0)),
                  pl.BlockSpec((1, fb, d), lambda j, i: (j, 0, 0)),
                  pl.BlockSpec((1, 2, tm, fb), lambda j, i: (j, 0, i, 0)),
                  pl.BlockSpec((1, tm, fb), lambda j, i: (j, i, 0))],
        out_specs=(pl.BlockSpec((1, 2, tm, fb), lambda j, i: (j, 0, i, 0)),
                   pl.BlockSpec((1, fb, d), lambda j, i: (j, 0, 0))),
        scratch_shapes=[pltpu.VMEM((fb, d), F32)],
        compiler_params=_params(2))(drs, w_out, h, a)


def _ffn_in_dw(x16, dh, name, token=None):
    t, d = x16.shape
    nb, _, _, fb = dh.shape
    extra = [] if token is None else [token]

    def body(*refs):
        x_ref, dh_ref, o_ref = refs[0], refs[1], refs[-1]
        o_ref[0, 0] = _dot(x_ref[...], dh_ref[0, 0], ta=True).astype(BF16)

    return pl.pallas_call(
        body, name=name, out_shape=jax.ShapeDtypeStruct((2, nb, d, fb), BF16), grid=(nb, 2),
        in_specs=[pl.BlockSpec((t, d), lambda j, s: (0, 0)),
                  pl.BlockSpec((1, 1, t, fb), lambda j, s: (j, s, 0, 0))] + [_TOKEN] * len(extra),
        out_specs=pl.BlockSpec((1, 1, d, fb), lambda j, s: (s, j, 0, 0)),
        compiler_params=_params(2))(x16, dh, *extra)


def _ffn_in_dx(dh, w_in, add, add_scale, name, token=None):
    nb, _, t, fb = dh.shape
    d = w_in.shape[2]
    tm = _pick(t, 512, 8)
    has_add = add is not None
    extra = [] if token is None else [token]

    def body(*refs):
        dh_ref, w_ref = refs[:2]
        o_ref = refs[-1]
        acc = None
        for jb in range(nb):
            for s in range(2):
                part = _dot(dh_ref[jb, s], w_ref[s, jb], tb=True)
                acc = part if acc is None else acc + part
        if has_add:
            acc = acc + add_scale * refs[2][...]
        o_ref[...] = acc

    return pl.pallas_call(
        body, name=name, out_shape=jax.ShapeDtypeStruct((t, d), F32), grid=(t // tm,),
        in_specs=[pl.BlockSpec((nb, 2, tm, fb), lambda i: (0, 0, i, 0)), _whole(w_in.shape)]
        + ([_rows(tm, d)] if has_add else []) + [_TOKEN] * len(extra),
        out_specs=_rows(tm, d),
        compiler_params=_params())(*([dh, w_in] + ([add] if has_add else []) + extra))


def _resid_ln_fwd(xin, f, g, b, alpha, scale, name):
    t, d = xin.shape
    tr = _pick(t, 256, 8)

    def body(x_ref, f_ref, g_ref, b_ref, r_ref, y_ref, y16_ref):
        r = alpha * x_ref[...] + scale * f_ref[...]
        xhat, _ = _ln_stats(r)
        y = xhat * g_ref[...] + b_ref[...]
        r_ref[...] = r
        y_ref[...] = y
        y16_ref[...] = y.astype(BF16)

    return pl.pallas_call(
        body, name=name,
        out_shape=(jax.ShapeDtypeStruct((t, d), F32), jax.ShapeDtypeStruct((t, d), F32),
                   jax.ShapeDtypeStruct((t, d), BF16)),
        grid=(t // tr,),
        in_specs=[_rows(tr, d), _rows(tr, d), _whole((1, d)), _whole((1, d))],
        out_specs=(_rows(tr, d), _rows(tr, d), _rows(tr, d)),
        compiler_params=_params())(xin, f, g, b)


def _ln_bwd_call(r, dy_a, dy_b, b_scale, g, out_scale, name):
    t, d = r.shape
    tr = _pick(t, 256, 8)
    has_b = dy_b is not None

    def body(*refs):
        if has_b:
            r_ref, da_ref, db_ref, g_ref, dr_ref, drs_ref, dg_ref, dbeta_ref = refs
            dy = da_ref[...] + b_scale * db_ref[...]
        else:
            r_ref, da_ref, g_ref, dr_ref, drs_ref, dg_ref, dbeta_ref = refs
            dy = da_ref[...]
        xhat, rstd = _ln_stats(r_ref[...])
        dr = _ln_bwd(dy, xhat, rstd, g_ref[...])
        dr_ref[...] = dr
        drs_ref[...] = (out_scale * dr).astype(BF16)

        @pl.when(pl.program_id(0) == 0)
        def _():
            dg_ref[...] = jnp.zeros_like(dg_ref)
            dbeta_ref[...] = jnp.zeros_like(dbeta_ref)

        dg_ref[...] += _fold8(dy * xhat)
        dbeta_ref[...] += _fold8(dy)

    ins = [r, dy_a] + ([dy_b] if has_b else []) + [g]
    in_specs = [_rows(tr, d)] * (3 if has_b else 2) + [_whole((1, d))]
    return pl.pallas_call(
        body, name=name,
        out_shape=(jax.ShapeDtypeStruct((t, d), F32), jax.ShapeDtypeStruct((t, d), BF16),
                   jax.ShapeDtypeStruct((8, d), F32), jax.ShapeDtypeStruct((8, d), F32)),
        grid=(t // tr,), in_specs=in_specs,
        out_specs=(_rows(tr, d), _rows(tr, d), _whole((8, d)), _whole((8, d))),
        compiler_params=_params())(*ins)


def _take_row(v, row_id, s):
    return jnp.sum(jnp.where(row_id == s, v, 0.0), axis=0, keepdims=True)


def _seg_carries(f_re, f_im, p_re, p_im, reverse):
    row_id = lax.broadcasted_iota(jnp.int32, f_re.shape, 0)
    p_re, p_im = _take_row(p_re, row_id, 0), _take_row(p_im, row_id, 0)
    out_re, out_im = jnp.zeros_like(f_re), jnp.zeros_like(f_im)
    c_re, c_im = jnp.zeros_like(p_re), jnp.zeros_like(p_im)
    order = range(N_SEG - 1, -1, -1) if reverse else range(N_SEG)
    for s in order:
        out_re = jnp.where(row_id == s, c_re, out_re)
        out_im = jnp.where(row_id == s, c_im, out_im)
        fr, fi = _take_row(f_re, row_id, s), _take_row(f_im, row_id, s)
        c_re, c_im = fr + p_re * c_re - p_im * c_im, fi + p_re * c_im + p_im * c_re
    return out_re, out_im


def _s5_fwd(za16, b_re, b_im, a_re, a_im, c_re, c_im, n_seq, name):
    t = za16.shape[0]
    n_sg, ch, st = b_re.shape
    seq = t // n_seq
    steps = seq // N_SEG

    def body(za_ref, bre_ref, bim_ref, are, aim, cre_ref, cim_ref, hre, him, y_ref):
        za = za_ref[...]
        hre[...] = _dot(za, bre_ref[0])
        him[...] = _dot(za, bim_ref[0])
        ar = jnp.broadcast_to(are[...], (N_SEG, st))
        ai = jnp.broadcast_to(aim[...], (N_SEG, st))
        zero = jnp.zeros((N_SEG, st), F32)

        def local(k, carry):
            lr, li, pr, pi = carry
            rows = pl.ds(pl.multiple_of(k * N_SEG, N_SEG), N_SEG)
            nr = ar * lr - ai * li + hre[rows, :]
            ni = ar * li + ai * lr + him[rows, :]
            hre[rows, :] = nr
            him[rows, :] = ni
            return nr, ni, ar * pr - ai * pi, ar * pi + ai * pr

        f_re, f_im, p_re, p_im = lax.fori_loop(0, steps, local, (zero, zero, zero + 1.0, zero))
        c_re, c_im = _seg_carries(f_re, f_im, p_re, p_im, reverse=False)

        def fix(k, carry):
            pr, pi = carry
            rows = pl.ds(pl.multiple_of(k * N_SEG, N_SEG), N_SEG)
            hre[rows, :] = hre[rows, :] + (pr * c_re - pi * c_im)
            him[rows, :] = him[rows, :] + (pr * c_im + pi * c_re)
            return ar * pr - ai * pi, ar * pi + ai * pr

        lax.fori_loop(0, steps, fix, (ar, ai))
        y_ref[...] = _dot(hre[...], cre_ref[0]) + _dot(him[...], cim_ref[0])

    rows_ch = pl.BlockSpec((seq, ch), lambda s, j: (s, j))
    rows_st = pl.BlockSpec((seq, st), lambda s, j: (s, j))
    vec = pl.BlockSpec((1, st), lambda s, j: (0, j))
    b_blk = pl.BlockSpec((1, ch, st), lambda s, j: (j, 0, 0))
    c_blk = pl.BlockSpec((1, st, ch), lambda s, j: (j, 0, 0))
    return pl.pallas_call(
        body, name=name,
        out_shape=(jax.ShapeDtypeStruct((t, n_sg * st), F32), jax.ShapeDtypeStruct((t, n_sg * st), F32),
                   jax.ShapeDtypeStruct((t, n_sg * ch), F32)),
        grid=(n_seq, n_sg), in_specs=[rows_ch, b_blk, b_blk, vec, vec, c_blk, c_blk],
        out_specs=(rows_st, rows_st, rows_ch),
        compiler_params=_params(2))(za16, b_re, b_im, a_re, a_im, c_re, c_im)


def _s5_bwd(dy16, dza_skip, h_re, h_im, za16, b_re, b_im, a_re, a_im, c_re, c_im, n_seq, name):
    t = dy16.shape[0]
    n_sg, ch, st = b_re.shape
    seq = t // n_seq
    steps = seq // N_SEG

    def body(dy_ref, skip_ref, hre, him, za_ref, bre_ref, bim_ref, are, aim, cre_ref, cim_ref,
             dza_ref, dbre_ref, dbim_ref, dcre_ref, dcim_ref, dare, daim, lre, lim):
        dy = dy_ref[...]
        lre[...] = _dot(dy, cre_ref[0], tb=True)
        lim[...] = _dot(dy, cim_ref[0], tb=True)
        ar = jnp.broadcast_to(are[...], (N_SEG, st))
        ai = -jnp.broadcast_to(aim[...], (N_SEG, st))
        zero = jnp.zeros((N_SEG, st), F32)

        def local(i, carry):
            lr, li, pr, pi = carry
            k = steps - 1 - i
            rows = pl.ds(pl.multiple_of(k * N_SEG, N_SEG), N_SEG)
            nr = ar * lr - ai * li + lre[rows, :]
            ni = ar * li + ai * lr + lim[rows, :]
            lre[rows, :] = nr
            lim[rows, :] = ni
            return nr, ni, ar * pr - ai * pi, ar * pi + ai * pr

        f_re, f_im, p_re, p_im = lax.fori_loop(0, steps, local, (zero, zero, zero + 1.0, zero))
        c_re, c_im = _seg_carries(f_re, f_im, p_re, p_im, reverse=True)

        def accumulate(lam_r, lam_i, hp_r, hp_i, acc_r, acc_i):
            return acc_r + (lam_r * hp_r + lam_i * hp_i), acc_i + (lam_i * hp_r - lam_r * hp_i)

        def fix(i, carry):
            pr, pi, acc_r, acc_i = carry
            k = steps - 1 - i
            rows = pl.ds(pl.multiple_of(k * N_SEG, N_SEG), N_SEG)
            prev = pl.ds(pl.multiple_of((k - 1) * N_SEG, N_SEG), N_SEG)
            lam_r = lre[rows, :] + (pr * c_re - pi * c_im)
            lam_i = lim[rows, :] + (pr * c_im + pi * c_re)
            lre[rows, :] = lam_r
            lim[rows, :] = lam_i
            acc_r, acc_i = accumulate(lam_r, lam_i, hre[prev, :], him[prev, :], acc_r, acc_i)
            return ar * pr - ai * pi, ar * pi + ai * pr, acc_r, acc_i

        pr, pi, acc_r, acc_i = lax.fori_loop(0, steps - 1, fix, (ar, ai, zero, zero))
        first = pl.ds(0, N_SEG)
        last = pl.ds((steps - 1) * N_SEG, N_SEG)
        lam_r = lre[first, :] + (pr * c_re - pi * c_im)
        lam_i = lim[first, :] + (pr * c_im + pi * c_re)
        lre[first, :] = lam_r
        lim[first, :] = lam_i
        row_id = lax.broadcasted_iota(jnp.int32, (N_SEG, st), 0)
        hp_r = jnp.where(row_id == 0, 0.0, pltpu.roll(hre[last, :], 1, 0))
        hp_i = jnp.where(row_id == 0, 0.0, pltpu.roll(him[last, :], 1, 0))
        acc_r, acc_i = accumulate(lam_r, lam_i, hp_r, hp_i, acc_r, acc_i)

        lam_re16 = lre[...].astype(BF16)
        lam_im16 = lim[...].astype(BF16)
        dza_ref[...] = (_dot(lam_re16, bre_ref[0], tb=True) + _dot(lam_im16, bim_ref[0], tb=True)
                        + skip_ref[...]).astype(BF16)

        @pl.when(pl.program_id(1) == 0)
        def _():
            for ref in (dbre_ref, dbim_ref, dcre_ref, dcim_ref, dare, daim):
                ref[...] = jnp.zeros_like(ref)

        za = za_ref[...]
        dbre_ref[0] += _dot(za, lam_re16, ta=True)
        dbim_ref[0] += _dot(za, lam_im16, ta=True)
        dcre_ref[0] += _dot(hre[...], dy, ta=True)
        dcim_ref[0] += _dot(him[...], dy, ta=True)
        dare[...] += acc_r
        daim[...] += acc_i

    rows_ch = pl.BlockSpec((seq, ch), lambda j, s: (s, j))
    rows_st = pl.BlockSpec((seq, st), lambda j, s: (s, j))
    vec = pl.BlockSpec((1, st), lambda j, s: (0, j))
    b_blk = pl.BlockSpec((1, ch, st), lambda j, s: (j, 0, 0))
    c_blk = pl.BlockSpec((1, st, ch), lambda j, s: (j, 0, 0))
    acc = pl.BlockSpec((N_SEG, st), lambda j, s: (0, j))
    return pl.pallas_call(
        body, name=name,
        out_shape=(jax.ShapeDtypeStruct((t, n_sg * ch), BF16),
                   jax.ShapeDtypeStruct((n_sg, ch, st), F32), jax.ShapeDtypeStruct((n_sg, ch, st), F32),
                   jax.ShapeDtypeStruct((n_sg, st, ch), F32), jax.ShapeDtypeStruct((n_sg, st, ch), F32),
                   jax.ShapeDtypeStruct((N_SEG, n_sg * st), F32), jax.ShapeDtypeStruct((N_SEG, n_sg * st), F32)),
        grid=(n_sg, n_seq),
        in_specs=[rows_ch, rows_ch, rows_st, rows_st, rows_ch, b_blk, b_blk, vec, vec, c_blk, c_blk],
        out_specs=(rows_ch, b_blk, b_blk, c_blk, c_blk, acc, acc),
        scratch_shapes=[pltpu.VMEM((seq, st), F32), pltpu.VMEM((seq, st), F32)],
        compiler_params=_params(2))(dy16, dza_skip, h_re, h_im, za16, b_re, b_im, a_re, a_im, c_re, c_im)


def _s5_post_fwd(yssm, u, d_skip, glu_w, glu_b, name):
    t, w = yssm.shape
    tr = _pick(t, 512, 8)

    def body(y_ref, u_ref, d_ref, w_ref, b_ref, o_ref):
        yg = _gelu(y_ref[...] + d_ref[...] * u_ref[...])
        pre = _dot(yg, w_ref[...]) + b_ref[...]
        o_ref[...] = yg * _sigmoid(pre)

    return pl.pallas_call(
        body, name=name, out_shape=jax.ShapeDtypeStruct((t, w), F32), grid=(t // tr,),
        in_specs=[_rows(tr, w), _rows(tr, w), _whole((1, w)), _whole((w, w)), _whole((1, w))],
        out_specs=_rows(tr, w), compiler_params=_params())(yssm, u, d_skip, glu_w, glu_b)


def _s5_post_bwd(yssm, u, dout, d_skip, glu_w, glu_b, name):
    t, w = yssm.shape
    tr = _pick(t, 512, 8)

    def body(y_ref, u_ref, do_ref, d_ref, w_ref, b_ref, dy_ref, du_ref, dw_ref, dd_ref, db_ref):
        uu = u_ref[...]
        y = y_ref[...] + d_ref[...] * uu
        yg = _gelu(y)
        sg = _sigmoid(_dot(yg, w_ref[...]) + b_ref[...])
        do = do_ref[...]
        dpre = do * yg * sg * (1.0 - sg)
        dyg = do * sg + _dot(dpre, w_ref[...], tb=True)
        dy = dyg * _gelu_grad(y)
        dy_ref[...] = dy.astype(BF16)
        du_ref[...] = dy * d_ref[...]

        @pl.when(pl.program_id(0) == 0)
        def _():
            dw_ref[...] = jnp.zeros_like(dw_ref)
            dd_ref[...] = jnp.zeros_like(dd_ref)
            db_ref[...] = jnp.zeros_like(db_ref)

        dw_ref[...] += _dot(yg, dpre, ta=True)
        dd_ref[...] += _fold8(dy * uu)
        db_ref[...] += _fold8(dpre)

    return pl.pallas_call(
        body, name=name,
        out_shape=(jax.ShapeDtypeStruct((t, w), BF16), jax.ShapeDtypeStruct((t, w), F32),
                   jax.ShapeDtypeStruct((w, w), F32), jax.ShapeDtypeStruct((8, w), F32),
                   jax.ShapeDtypeStruct((8, w), F32)),
        grid=(t // tr,),
        in_specs=[_rows(tr, w), _rows(tr, w), _rows(tr, w), _whole((1, w)), _whole((w, w)),
                  _whole((1, w))],
        out_specs=(_rows(tr, w), _rows(tr, w), _whole((w, w)), _whole((8, w)), _whole((8, w))),
        compiler_params=_params())(yssm, u, dout, d_skip, glu_w, glu_b)


def _head_select(parts, head_dim):
    col_head = lax.broadcasted_iota(jnp.int32, parts[0].shape, 1) // head_dim
    out = jnp.zeros_like(parts[0])
    for h, part in enumerate(parts):
        out = jnp.where(col_head == h, part, out)
    return out


def _gmlp_fwd(proj, ln_g, ln_b, ws, bs_cols, name):
    t = proj.shape[0]
    n_heads = ws.shape[0]
    w = bs_cols.shape[1]
    head_dim = w // n_heads
    cpb = _pick(t // CHUNK, 4, 1)
    tr = cpb * CHUNK

    def body(zu_ref, zv_ref, g_ref, b_ref, ws_ref, bs_ref, o_ref):
        for c in range(cpb):
            rows = pl.ds(c * CHUNK, CHUNK)
            xhat, _ = _ln_stats(_gelu(zv_ref[rows, :]))
            vn = (xhat * g_ref[...] + b_ref[...]).astype(BF16)
            s = _head_select([_dot(ws_ref[h], vn) for h in range(n_heads)], head_dim) + bs_ref[...]
            o_ref[rows, :] = _gelu(zu_ref[rows, :]) * s

    return pl.pallas_call(
        body, name=name, out_shape=jax.ShapeDtypeStruct((t, w), F32), grid=(t // tr,),
        in_specs=[_rows(tr, w, 1), _rows(tr, w, 2), _whole((1, w)), _whole((1, w)),
                  _whole(ws.shape), _whole(bs_cols.shape)],
        out_specs=_rows(tr, w), compiler_params=_params())(proj, proj, ln_g, ln_b, ws, bs_cols)


def _gmlp_bwd(proj, dout, ln_g, ln_b, ws, ws_t, bs_cols, name):
    t = proj.shape[0]
    n_heads = ws.shape[0]
    w = bs_cols.shape[1]
    head_dim = w // n_heads
    cpb = _pick(t // CHUNK, 4, 1)
    tr = cpb * CHUNK

    def body(zu_ref, zv_ref, do_ref, g_ref, b_ref, ws_ref, wst_ref, bs_ref,
             dzu_ref, dzv_ref, dws_ref, dbs_ref, dg_ref, dbeta_ref):
        @pl.when(pl.program_id(0) == 0)
        def _():
            dws_ref[...] = jnp.zeros_like(dws_ref)
            dbs_ref[...] = jnp.zeros_like(dbs_ref)
            dg_ref[...] = jnp.zeros_like(dg_ref)
            dbeta_ref[...] = jnp.zeros_like(dbeta_ref)

        col_head = lax.broadcasted_iota(jnp.int32, (CHUNK, w), 1) // head_dim
        for c in range(cpb):
            rows = pl.ds(c * CHUNK, CHUNK)
            zu, zv, do = zu_ref[rows, :], zv_ref[rows, :], do_ref[rows, :]
            xhat, rstd = _ln_stats(_gelu(zv))
            vn = (xhat * g_ref[...] + b_ref[...]).astype(BF16)
            s = _head_select([_dot(ws_ref[h], vn) for h in range(n_heads)], head_dim) + bs_ref[...]
            dzu_ref[rows, :] = (do * s * _gelu_grad(zu)).astype(BF16)
            ds = do * _gelu(zu)
            ds16 = ds.astype(BF16)
            dbs_ref[...] += ds
            for h in range(n_heads):
                dws_ref[h] += _dot(jnp.where(col_head == h, ds16, jnp.zeros_like(ds16)), vn, tb=True)
            dvn = _head_select([_dot(wst_ref[h], ds16) for h in range(n_heads)], head_dim)
            dg_ref[...] += _fold8(dvn * xhat)
            dbeta_ref[...] += _fold8(dvn)
            dgv = _ln_bwd(dvn, xhat, rstd, g_ref[...])
            dzv_ref[rows, :] = (dgv * _gelu_grad(zv)).astype(BF16)

    return pl.pallas_call(
        body, name=name,
        out_shape=(jax.ShapeDtypeStruct((t, w), BF16), jax.ShapeDtypeStruct((t, w), BF16),
                   jax.ShapeDtypeStruct(ws.shape, F32), jax.ShapeDtypeStruct((CHUNK, w), F32),
                   jax.ShapeDtypeStruct((8, w), F32), jax.ShapeDtypeStruct((8, w), F32)),
        grid=(t // tr,),
        in_specs=[_rows(tr, w, 1), _rows(tr, w, 2), _rows(tr, w), _whole((1, w)), _whole((1, w)),
                  _whole(ws.shape), _whole(ws.shape), _whole(bs_cols.shape)],
        out_specs=(_rows(tr, w), _rows(tr, w), _whole(ws.shape), _whole((CHUNK, w)),
                   _whole((8, w)), _whole((8, w))),
        compiler_params=_params())(proj, proj, dout, ln_g, ln_b, ws, ws_t, bs_cols)


def _merge_fwd(s5out, gm, proj, up_a, up_b, name):
    t, w = s5out.shape
    d = up_a.shape[1]
    assert d == 2 * w
    tr = _pick(t, 256, 8)

    def body(s_ref, gm_ref, ga0, ga1, gb0, gb1, ua_ref, ub_ref, m_ref, ya_ref, yb_ref):
        ya = _dot(s_ref[...], ua_ref[...])
        yb = _dot(gm_ref[...], ub_ref[...])
        ya_ref[...] = ya
        yb_ref[...] = yb
        for half, (ga, gb) in enumerate(((ga0, gb0), (ga1, gb1))):
            cols = slice(half * w, (half + 1) * w)
            m_ref[:, cols] = (_sigmoid(ga[...]) * ya[:, cols] + _sigmoid(gb[...]) * yb[:, cols]).astype(BF16)

    return pl.pallas_call(
        body, name=name,
        out_shape=(jax.ShapeDtypeStruct((t, d), BF16), jax.ShapeDtypeStruct((t, d), F32),
                   jax.ShapeDtypeStruct((t, d), F32)),
        grid=(t // tr,),
        in_specs=[_rows(tr, w), _rows(tr, w), _rows(tr, w, 3), _rows(tr, w, 4), _rows(tr, w, 5),
                  _rows(tr, w, 6), _whole(up_a.shape), _whole(up_b.shape)],
        out_specs=(_rows(tr, d), _rows(tr, d), _rows(tr, d)),
        compiler_params=_params())(s5out, gm, proj, proj, proj, proj, up_a, up_b)


def _merge_bwd(dm, ya, yb, s5out, gm, proj, up_a, up_b, name):
    t, d = dm.shape
    w = d // 2
    tr = _pick(t, 256, 8)

    def body(dm_ref, ya_ref, yb_ref, s_ref, gm_ref, ga0, ga1, gb0, gb1, ua_ref, ub_ref,
             dga_ref, dgb_ref, ds_ref, dgm_ref, dua_ref, dub_ref):
        dm_v, ya, yb = dm_ref[...], ya_ref[...], yb_ref[...]
        dya, dyb = [], []
        for half, (ga, gb) in enumerate(((ga0, gb0), (ga1, gb1))):
            cols = slice(half * w, (half + 1) * w)
            sa, sb = _sigmoid(ga[...]), _sigmoid(gb[...])
            dmh = dm_v[:, cols]
            dga_ref[:, cols] = (dmh * ya[:, cols] * sa * (1.0 - sa)).astype(BF16)
            dgb_ref[:, cols] = (dmh * yb[:, cols] * sb * (1.0 - sb)).astype(BF16)
            dya.append((dmh * sa).astype(BF16))
            dyb.append((dmh * sb).astype(BF16))
        dya = jnp.concatenate(dya, axis=1)
        dyb = jnp.concatenate(dyb, axis=1)
        ds_ref[...] = _dot(dya, ua_ref[...], tb=True)
        dgm_ref[...] = _dot(dyb, ub_ref[...], tb=True)

        @pl.when(pl.program_id(0) == 0)
        def _():
            dua_ref[...] = jnp.zeros_like(dua_ref)
            dub_ref[...] = jnp.zeros_like(dub_ref)

        dua_ref[...] += _dot(s_ref[...], dya, ta=True)
        dub_ref[...] += _dot(gm_ref[...], dyb, ta=True)

    return pl.pallas_call(
        body, name=name,
        out_shape=(jax.ShapeDtypeStruct((t, d), BF16), jax.ShapeDtypeStruct((t, d), BF16),
                   jax.ShapeDtypeStruct((t, w), F32), jax.ShapeDtypeStruct((t, w), F32),
                   jax.ShapeDtypeStruct(up_a.shape, F32), jax.ShapeDtypeStruct(up_b.shape, F32)),
        grid=(t // tr,),
        in_specs=[_rows(tr, d), _rows(tr, d), _rows(tr, d), _rows(tr, w), _rows(tr, w),
                  _rows(tr, w, 3), _rows(tr, w, 4), _rows(tr, w, 5), _rows(tr, w, 6),
                  _whole(up_a.shape), _whole(up_b.shape)],
        out_specs=(_rows(tr, d), _rows(tr, d), _rows(tr, w), _rows(tr, w), _whole(up_a.shape),
                   _whole(up_b.shape)),
        compiler_params=_params())(dm, ya, yb, s5out, gm, proj, proj, proj, proj, up_a, up_b)


def _head(x3, p, target, w_gate, w_proj, name):
    t, d = x3.shape
    pd = p.shape[1]
    tr = _pick(t, 256, 8)
    nb = t // tr

    def body(x_ref, p_ref, t_ref, wg_ref, wp_ref, loss_ref, dx_ref, dwg_ref, dwp_ref):
        xv = x_ref[...]
        sg = _sigmoid(_dot(xv, wg_ref[...]))
        pp = _dot(p_ref[...], wp_ref[...])
        err = xv + sg * pp - t_ref[...]
        loss_ref[...] = jnp.full((8, 128), 0.5 * jnp.sum(jnp.mean(err * err, axis=-1)), F32)
        dout = err * (1.0 / d)
        dgpre = dout * pp * sg * (1.0 - sg)
        dpp = dout * sg
        dx_ref[...] = dout + _dot(dgpre, wg_ref[...], tb=True)

        @pl.when(pl.program_id(0) == 0)
        def _():
            dwg_ref[...] = jnp.zeros_like(dwg_ref)
            dwp_ref[...] = jnp.zeros_like(dwp_ref)

        dwg_ref[...] += _dot(xv, dgpre, ta=True)
        dwp_ref[...] += _dot(p_ref[...], dpp, ta=True)

    return pl.pallas_call(
        body, name=name,
        out_shape=(jax.ShapeDtypeStruct((nb * 8, 128), F32), jax.ShapeDtypeStruct((t, d), F32),
                   jax.ShapeDtypeStruct((d, d), F32), jax.ShapeDtypeStruct((pd, d), F32)),
        grid=(nb,),
        in_specs=[_rows(tr, d), _rows(tr, pd), _rows(tr, d), _whole((d, d)), _whole((pd, d))],
        out_specs=(pl.BlockSpec((8, 128), lambda i: (i, 0)), _rows(tr, d), _whole((d, d)),
                   _whole((pd, d))),
        compiler_params=_params())(x3, p, target, w_gate, w_proj)


_HBM = pl.BlockSpec(memory_space=pltpu.HBM)


def _all_gather(shards, name):
    n = len(shards)

    def body(*refs):
        x_refs, out_refs = refs[:n], refs[n:2 * n]
        send_sems, recv_sems, local_sems = refs[2 * n:]
        x, y, c = lax.axis_index("x"), lax.axis_index("y"), lax.axis_index("c")
        me, sibling = (x, y, c), (x, y, 1 - c)
        chips = [(1 - x, y), (x, 1 - y), (1 - x, 1 - y)]

        def copy(a, k, block, to, own=False):
            slot = out_refs[a].at[4 * block[0] + 2 * block[1] + block[2]]
            return pltpu.make_async_remote_copy(
                src_ref=x_refs[a] if own else slot, dst_ref=slot,
                send_sem=send_sems.at[7 * a + k], recv_sem=recv_sems.at[7 * a + k],
                device_id=to, device_id_type=pl.DeviceIdType.MESH)

        mine = [pltpu.make_async_copy(x_refs[a], out_refs[a].at[4 * x + 2 * y + c], local_sems.at[a])
                for a in range(n)]
        sends = []
        for a in range(n):
            mine[a].start()
            first = [copy(a, 0, me, sibling, own=True)]
            first += [copy(a, 1 + j, me, (*chip, c), own=True) for j, chip in enumerate(chips)]
            for cp in first:
                cp.start()
            sends += first
        for a in range(n):
            for j, chip in enumerate(chips):
                copy(a, 1 + j, (*chip, c), me).wait_recv()
                passed = copy(a, 4 + j, (*chip, c), sibling)
                passed.start()
                sends.append(passed)
        for a in range(n):
            copy(a, 0, sibling, me).wait_recv()
            for j, chip in enumerate(chips):
                copy(a, 4 + j, (*chip, 1 - c), me).wait_recv()
        for cp in sends:
            cp.wait_send()
        for cp in mine:
            cp.wait()

    return pl.pallas_call(
        body, name=name,
        out_shape=tuple(jax.ShapeDtypeStruct((N_DEV,) + s.shape, s.dtype) for s in shards),
        in_specs=[_HBM] * n, out_specs=tuple([_HBM] * n),
        scratch_shapes=[pltpu.SemaphoreType.DMA((7 * n,)), pltpu.SemaphoreType.DMA((7 * n,)),
                        pltpu.SemaphoreType.DMA((n,))],
    )(*shards)


_SEM = pl.BlockSpec(memory_space=pltpu.SEMAPHORE)
_ANY = pl.BlockSpec(memory_space=pl.ANY)
_DATAFLOW = pltpu.SideEffectType.DATAFLOW_SIDE_EFFECTING


def _peer(k, x, y, c):
    return (1 - x if k & 4 else x, 1 - y if k & 2 else y, 1 - c if k & 1 else c)


def _exchange_start(sends, after, name):
    n = len(sends)
    me = 4 * lax.axis_index("x") + 2 * lax.axis_index("y") + lax.axis_index("c")
    lands = []
    for s in sends:
        own = s[0] if s.shape[0] == 1 else lax.dynamic_index_in_dim(s, me, 0, keepdims=False)
        land = lax.empty((N_DEV,) + s.shape[1:], s.dtype)
        lands.append(lax.dynamic_update_index_in_dim(land, own, me, 0))

    def body(*refs):
        s_refs, l_refs = refs[:n], refs[n:2 * n]
        send_sems, recv_sems, token = refs[2 * n + 1], refs[2 * n + 2], refs[-1]
        x, y, c = lax.axis_index("x"), lax.axis_index("y"), lax.axis_index("c")
        for a in range(n):
            same = sends[a].shape[0] == 1
            for k in range(1, N_DEV):
                px, py, pc = _peer(k, x, y, c)
                pltpu.make_async_remote_copy(
                    src_ref=s_refs[a].at[0 if same else 4 * px + 2 * py + pc],
                    dst_ref=l_refs[a].at[4 * x + 2 * y + c],
                    send_sem=send_sems.at[7 * a + k - 1], recv_sem=recv_sems.at[7 * a + k - 1],
                    device_id=(px, py, pc), device_id_type=pl.DeviceIdType.MESH).start()
        token[...] = jnp.zeros_like(token)

    outs = pl.pallas_call(
        body, name=name,
        out_shape=(pltpu.SemaphoreType.DMA((7 * n,)), pltpu.SemaphoreType.DMA((7 * n,)),
                   *[pltpu.HBM(s.shape, s.dtype) for s in sends],
                   *[pltpu.HBM(l.shape, l.dtype) for l in lands],
                   jax.ShapeDtypeStruct((8, 128), F32)),
        in_specs=[_HBM] * (2 * n) + [_ANY],
        out_specs=(_SEM, _SEM, *([_HBM] * (2 * n)), pl.BlockSpec(memory_space=pltpu.VMEM)),
        input_output_aliases={i: 2 + i for i in range(2 * n)},
        compiler_params=pltpu.CompilerParams(has_side_effects=_DATAFLOW),
    )(*[pltpu.with_memory_space_constraint(v, pltpu.HBM) for v in list(sends) + lands], after)
    return (outs[0], outs[1], list(outs[2:2 + n]), list(outs[2 + n:2 + 2 * n])), outs[-1]


def _exchange_wait(handle, after, name):
    send_sems, recv_sems, sends, lands = handle
    n = len(sends)

    def body(*refs):
        s_refs, l_refs = refs[:n], refs[n:2 * n]
        send_sems, recv_sems = refs[2 * n], refs[2 * n + 1]
        x, y, c = lax.axis_index("x"), lax.axis_index("y"), lax.axis_index("c")
        for a in range(n):
            for k in range(1, N_DEV):
                copy = pltpu.make_async_remote_copy(
                    src_ref=s_refs[a].at[0], dst_ref=l_refs[a].at[0],
                    send_sem=send_sems.at[7 * a + k - 1], recv_sem=recv_sems.at[7 * a + k - 1],
                    device_id=_peer(k, x, y, c), device_id_type=pl.DeviceIdType.MESH)
                copy.wait_send()
                copy.wait_recv()

    outs = pl.pallas_call(
        body, name=name,
        out_shape=(*[pltpu.HBM(s.shape, s.dtype) for s in sends], *[pltpu.HBM(l.shape, l.dtype) for l in lands]),
        in_specs=[_HBM] * (2 * n) + [_SEM, _SEM] + [_ANY] * len(after),
        out_specs=tuple([_HBM] * (2 * n)),
        input_output_aliases={i: i for i in range(2 * n)},
        compiler_params=pltpu.CompilerParams(has_side_effects=_DATAFLOW),
    )(*sends, *lands, send_sems, recv_sems, *after)
    return list(outs[n:])


def _adamw(recv, w, m, v, name):
    n, rows, width = recv.shape
    tr = _pick(rows, max(8, ADAM_BLOCK_BYTES // (n * width * recv.dtype.itemsize)), 8)
    c_m = 1.0 - ADAM_B1 ** ADAM_STEP
    c_v = 1.0 - ADAM_B2 ** ADAM_STEP

    def body(r_ref, w_ref, m_ref, v_ref, g_ref, d_ref, nm_ref, nv_ref):
        g = r_ref[0].astype(F32)
        for i in range(1, n):
            g = g + r_ref[i].astype(F32)
        m2 = ADAM_B1 * m_ref[...] + (1.0 - ADAM_B1) * g
        v2 = ADAM_B2 * v_ref[...] + (1.0 - ADAM_B2) * (g * g)
        m_hat = m2 / c_m
        v_hat = v2 / c_v
        g_ref[...] = g
        d_ref[...] = -ADAM_LR * (m_hat / (jnp.sqrt(v_hat) + ADAM_EPS) + ADAM_WD * w_ref[...])
        nm_ref[...] = m2
        nv_ref[...] = v2

    blk = _rows(tr, width)
    shp = jax.ShapeDtypeStruct((rows, width), F32)
    return pl.pallas_call(
        body, name=name, out_shape=(shp, shp, shp, shp), grid=(rows // tr,),
        in_specs=[pl.BlockSpec((n, tr, width), lambda i: (0, i, 0)), blk, blk, blk],
        out_specs=(blk, blk, blk, blk), compiler_params=_params())(recv, w, m, v)


def _join_cols(gathered):
    n, r, c = gathered.shape
    return gathered.transpose(1, 0, 2).reshape(r, n * c)


def _split_cols(full):
    r, c = full.shape
    return full.reshape(r, N_DEV, c // N_DEV).transpose(1, 0, 2)


def _adamw_small(items, name):
    n = len(items)
    c_m = 1.0 - ADAM_B1 ** ADAM_STEP
    c_v = 1.0 - ADAM_B2 ** ADAM_STEP

    def body(*refs):
        ins, outs = refs[:4 * n], refs[4 * n:]
        for k in range(n):
            r_ref, w_ref, m_ref, v_ref = ins[4 * k:4 * k + 4]
            g = r_ref[0]
            for i in range(1, N_DEV):
                g = g + r_ref[i]
            m2 = ADAM_B1 * m_ref[...] + (1.0 - ADAM_B1) * g
            v2 = ADAM_B2 * v_ref[...] + (1.0 - ADAM_B2) * (g * g)
            outs[4 * k][...] = g
            outs[4 * k + 1][...] = -ADAM_LR * ((m2 / c_m) / (jnp.sqrt(v2 / c_v) + ADAM_EPS) + ADAM_WD * w_ref[...])
            outs[4 * k + 2][...] = m2
            outs[4 * k + 3][...] = v2

    vmem = pl.BlockSpec(memory_space=pltpu.VMEM)
    flat = pl.pallas_call(
        body, name=name,
        out_shape=tuple(jax.ShapeDtypeStruct(w.shape, F32) for _, w, _, _ in items for _ in range(4)),
        in_specs=[vmem] * (4 * n), out_specs=tuple([vmem] * (4 * n)),
        compiler_params=pltpu.CompilerParams(vmem_limit_bytes=VMEM_LIMIT_BYTES),
    )(*[a for item in items for a in item])
    return [tuple(flat[4 * k:4 * k + 4]) for k in range(n)]


def _discretise(lam_re, lam_im, log_dt, b_re, b_im):
    dt = jnp.exp(log_dt)[:, None]
    mag = jnp.exp(lam_re * dt)
    ab_re = mag * jnp.cos(lam_im * dt)
    ab_im = mag * jnp.sin(lam_im * dt)
    nr = ab_re - 1.0
    ni = ab_im
    den = lam_re * lam_re + lam_im * lam_im
    coef_re = ((nr * lam_re + ni * lam_im) / den)[..., None]
    coef_im = ((ni * lam_re - nr * lam_im) / den)[..., None]
    bb_re = coef_re * b_re - coef_im * b_im
    bb_im = coef_re * b_im + coef_im * b_re
    return ab_re, ab_im, bb_re, bb_im


def _same_group(gl):
    return jnp.eye(gl, dtype=bool)[None, :, None, :, None]


def _super_groups_in(bb, gl):
    g, p, i = bb.shape
    blocks = bb.reshape(g // gl, gl, p, i).transpose(0, 1, 3, 2)[:, :, :, None, :]
    return jnp.where(_same_group(gl), blocks, 0.0).reshape(g // gl, gl * i, gl * p)


def _super_groups_in_take(dense, gl, p, i):
    n_sg = dense.shape[0]
    blocks = jnp.where(_same_group(gl), dense.reshape(n_sg, gl, i, gl, p), 0.0).sum(axis=3)
    return blocks.transpose(0, 1, 3, 2).reshape(n_sg * gl, p, i)


def _super_groups_out(cc, gl):
    g, i, p = cc.shape
    blocks = cc.reshape(g // gl, gl, i, p).transpose(0, 1, 3, 2)[:, :, :, None, :]
    return jnp.where(_same_group(gl), blocks, 0.0).reshape(g // gl, gl * p, gl * i)


def _super_groups_out_take(dense, gl, i, p):
    n_sg = dense.shape[0]
    blocks = jnp.where(_same_group(gl), dense.reshape(n_sg, gl, p, gl, i), 0.0).sum(axis=3)
    return blocks.transpose(0, 1, 3, 2).reshape(n_sg * gl, i, p)


def _perm_rows(z, n_seq):
    t, w = z.shape
    steps = t // n_seq // N_SEG
    return z.reshape(n_seq, N_SEG, steps, w).transpose(0, 2, 1, 3).reshape(t, w)


def _unperm_rows(z, n_seq):
    t, w = z.shape
    steps = t // n_seq // N_SEG
    return z.reshape(n_seq, steps, N_SEG, w).transpose(0, 2, 1, 3).reshape(t, w)


def kernel(*args):
    assert len(args) == len(INPUT_NAMES)
    inp = dict(zip(INPUT_NAMES, args))
    depth = inp["ffn1_w_in"].shape[0]
    assert depth == 1
    alpha = (2.0 * depth) ** 0.25

    n_seq, seq, d_model = inp["x"].shape
    t = n_seq * seq
    x = inp["x"].reshape(t, d_model)
    x16 = x.astype(BF16)
    p16 = inp["p"][0].reshape(t, -1).astype(BF16)
    target = inp["loss_target"].reshape(t, d_model)
    wts = {n: inp[n][0] if n in SHARDED else inp[n] for n in WEIGHTS}
    mom = {n: inp["m_" + n][0] if n in SHARDED else inp["m_" + n] for n in WEIGHTS}
    var = {n: inp["v_" + n][0] if n in SHARDED else inp["v_" + n] for n in WEIGHTS}

    full = {}

    def place(n, g):
        if n in ("ffn1_w_in", "ffn2_w_in"):
            full[n] = g.reshape((2, N_DEV // 2) + g.shape[1:])
        elif n in ("ffn1_w_out", "ffn2_w_out"):
            full[n] = g.reshape(N_DEV // 2, 2 * g.shape[1], g.shape[2])
        elif n in COL_SHARDED:
            full[n] = _join_cols(g)
        else:
            full[n] = g.reshape(N_DEV * g.shape[1], g.shape[2])

    w16 = dict(zip(GATHER_FIRST, _to_bf16([wts[n] for n in GATHER_FIRST], "cast_ffn1")))
    later = GATHER_MIX + GATHER_LAST
    w16.update(zip(later, _to_bf16([wts[n] for n in later], "cast_rest")))
    for n, g in zip(GATHER_FIRST, _all_gather([w16[n] for n in GATHER_FIRST], "gather_ffn1")):
        place(n, g)
    gather_mix, gather_mix_token = _exchange_start([w16[n][None] for n in GATHER_MIX], full["ffn1_w_out"],
                                                   "gather_mix_start")

    def row(v):
        return v.reshape(1, -1)

    _, n_groups, n_state = wts["ssm_lambda_re"].shape
    n_ch = wts["ssm_b_re"].shape[3]
    disc_in = tuple(wts[n][0] for n in ("ssm_lambda_re", "ssm_lambda_im", "ssm_log_dt", "ssm_b_re", "ssm_b_im"))
    (ab_re, ab_im, bb_re, bb_im), disc_vjp = jax.vjp(_discretise, *disc_in)
    a_re, a_im = row(ab_re), row(ab_im)
    gl = S5_CHANNELS_PER_STEP // n_ch
    b_sg_re = _super_groups_in(bb_re, gl).astype(BF16)
    b_sg_im = _super_groups_in(bb_im, gl).astype(BF16)
    c_sg_re = _super_groups_out(wts["ssm_c_re"][0], gl).astype(BF16)
    c_sg_im = _super_groups_out(-wts["ssm_c_im"][0], gl).astype(BF16)

    ws = wts["gmlp_w_s"][0]
    n_heads = ws.shape[0]
    d_gmlp = wts["gmlp_ln_g"].shape[1]
    causal = jnp.tril(jnp.ones((CHUNK, CHUNK), dtype=bool))
    ws_masked = jnp.where(causal[None], ws, 0.0).astype(BF16)
    ws_masked_t = ws_masked.transpose(0, 2, 1)
    bs_cols = jnp.repeat(wts["gmlp_b_s"][0].T, d_gmlp // n_heads, axis=1)
    d_ssm = n_groups * n_ch
    assert d_ssm == d_gmlp and d_model == 2 * d_ssm

    h1, a1 = _ffn_in(x16, full["ffn1_w_in"], "ffn1_in", token=gather_mix_token)
    r1, x1, x1_16 = _ffn_out_ln(a1, full["ffn1_w_out"], x, row(wts["ln1_g"]), row(wts["ln1_b"]), alpha,
                                "ffn1_out_ln")

    for n, g in zip(GATHER_MIX, _exchange_wait(gather_mix, [x1_16], "gather_mix_wait")):
        place(n, g)
    gather_last, token = _exchange_start([w16[n][None] for n in GATHER_LAST], full["mix_w_in"],
                                         "gather_ffn2_start")
    proj = _mm(x1_16, full["mix_w_in"], name="mix_in", token=token)
    za_p = _perm_rows(proj[:, :d_ssm], n_seq)
    za_p16 = za_p.astype(BF16)
    h_re, h_im, yssm = _s5_fwd(za_p16, b_sg_re, b_sg_im, a_re, a_im, c_sg_re, c_sg_im, n_seq, "s5_fwd")
    s5out_p = _s5_post_fwd(yssm, za_p, row(wts["ssm_d"]), full["ssm_glu_w"], row(wts["ssm_glu_b"]),
                           "s5_post")
    s5out = _unperm_rows(s5out_p, n_seq)
    gm = _gmlp_fwd(proj, row(wts["gmlp_ln_g"]), row(wts["gmlp_ln_b"]), ws_masked, bs_cols, "gmlp")
    m_mix, y_a, y_b = _merge_fwd(s5out, gm, proj, full["up_a"], full["up_b"], "merge")
    mixed = _mm(m_mix, full["mix_w_out"], name="mix_out")
    r2, x2, x2_16 = _resid_ln_fwd(x1, mixed, row(wts["ln2_g"]), row(wts["ln2_b"]), alpha, 1.0, "ln2")

    for n, g in zip(GATHER_LAST, _exchange_wait(gather_last, [x2_16], "gather_ffn2_wait")):
        place(n, g)
    h2, a2 = _ffn_in(x2_16, full["ffn2_w_in"], "ffn2_in")
    r3, x3, _ = _ffn_out_ln(a2, full["ffn2_w_out"], x2, row(wts["ln3_g"]), row(wts["ln3_b"]), alpha,
                            "ffn2_out_ln")

    small = {}
    send = {}

    def lane_dense(n, v):
        return v.reshape(v.shape[:2] + (-1,)) if n in ("ssm_b_re", "ssm_b_im") else v
    loss_parts, dx3, dw_gate, dw_proj = _head(x3, p16, target, full["ple_w_gate"], full["ple_w_proj"], "head")
    loss = lax.psum(jnp.sum(loss_parts[::8, 0]), ("x", "y", "c"))
    send["ple_w_gate"] = dw_gate.astype(BF16).reshape(N_DEV, -1, d_model)
    send["ple_w_proj"] = _split_cols(dw_proj.astype(BF16))

    dr3, dr3_h, dg, db = _ln_bwd_call(r3, dx3, None, 1.0, row(wts["ln3_g"]), 0.5, "ln3_bwd")
    small["ln3_g"], small["ln3_b"] = dg.sum(0, keepdims=True), db.sum(0, keepdims=True)
    dh2, dw = _ffn_out_bwd(dr3_h, full["ffn2_w_out"], h2, a2, "ffn2_out_bwd")
    send["ffn2_w_out"] = dw.reshape(N_DEV, -1, d_model)
    dw = _ffn_in_dw(x2_16, dh2, "ffn2_in_dw")
    send["ffn2_w_in"] = dw.reshape((N_DEV,) + dw.shape[2:])
    group1 = ("ffn2_w_in", "ffn2_w_out", "ple_w_gate", "ple_w_proj")
    exchange1, token = _exchange_start([send[n] for n in group1], dh2, "grad_exchange1_start")
    dx2_f = _ffn_in_dx(dh2, full["ffn2_w_in"], None, 1.0, "ffn2_in_dx", token=token)

    dr2, dr2_h, dg, db = _ln_bwd_call(r2, dx2_f, dr3, alpha, row(wts["ln2_g"]), 1.0, "ln2_bwd")
    small["ln2_g"], small["ln2_b"] = dg.sum(0, keepdims=True), db.sum(0, keepdims=True)
    dm = _mm(dr2_h, full["mix_w_out"], tb=True, name="mix_out_dx")
    send["mix_w_out"] = _mm(m_mix, dr2_h, ta=True, name="mix_out_dw", out_dtype=BF16).reshape(
        N_DEV, -1, d_model)
    dga, dgb, ds5out, dgm, dw_a, dw_b = _merge_bwd(
        dm, y_a, y_b, s5out, gm, proj, full["up_a"], full["up_b"], "merge_bwd")
    send["up_a"] = _split_cols(dw_a.astype(BF16))
    send["up_b"] = _split_cols(dw_b.astype(BF16))

    dzu, dzv, dws, dbs_cols, dg, db = _gmlp_bwd(
        proj, dgm, row(wts["gmlp_ln_g"]), row(wts["gmlp_ln_b"]), ws_masked, ws_masked_t, bs_cols,
        "gmlp_bwd")
    small["gmlp_ln_g"], small["gmlp_ln_b"] = dg.sum(0, keepdims=True), db.sum(0, keepdims=True)
    small["gmlp_w_s"] = jnp.where(causal[None], dws, 0.0)[None]
    small["gmlp_b_s"] = dbs_cols.reshape(CHUNK, n_heads, -1).sum(-1).T[None]

    ds5out_p = _perm_rows(ds5out, n_seq)
    dy_p, dza_skip, dw_glu, dd, dgb_glu = _s5_post_bwd(
        yssm, za_p, ds5out_p, row(wts["ssm_d"]), full["ssm_glu_w"], row(wts["ssm_glu_b"]),
        "s5_post_bwd")
    send["ssm_glu_w"] = dw_glu.astype(BF16).reshape(N_DEV, -1, d_ssm)
    small["ssm_d"], small["ssm_glu_b"] = dd.sum(0, keepdims=True), dgb_glu.sum(0, keepdims=True)
    dza_p, dbb_re, dbb_im, dc_re, dc_im, da_re, da_im = _s5_bwd(
        dy_p, dza_skip, h_re, h_im, za_p16, b_sg_re, b_sg_im, a_re, a_im, c_sg_re, c_sg_im, n_seq, "s5_bwd")
    small["ssm_c_re"] = _super_groups_out_take(dc_re, gl, n_ch, n_state)[None]
    small["ssm_c_im"] = -_super_groups_out_take(dc_im, gl, n_ch, n_state)[None]
    dbb_re = _super_groups_in_take(dbb_re, gl, n_state, n_ch)
    dbb_im = _super_groups_in_take(dbb_im, gl, n_state, n_ch)
    dab_re = da_re.sum(0).reshape(n_groups, n_state)
    dab_im = da_im.sum(0).reshape(n_groups, n_state)
    for n, g in zip(("ssm_lambda_re", "ssm_lambda_im", "ssm_log_dt", "ssm_b_re", "ssm_b_im"),
                    disc_vjp((dab_re, dab_im, dbb_re, dbb_im))):
        small[n] = g[None]

    dproj = jnp.concatenate([_unperm_rows(dza_p, n_seq), dzu, dzv, dga, dgb], axis=1)
    send["mix_w_in"] = _split_cols(_mm(x1_16, dproj, ta=True, name="mix_in_dw", out_dtype=BF16))
    group2 = ("mix_w_in", "mix_w_out", "up_a", "up_b", "ssm_glu_w")
    exchange2, token = _exchange_start(
        [send[n] for n in group2] + [lane_dense(n, small[n])[None] for n in SMALL_EARLY], dproj, "grad_exchange2_start")
    dx1_f = _mm(dproj, full["mix_w_in"], tb=True, name="mix_in_dx", token=token)

    dr1, dr1_h, dg, db = _ln_bwd_call(r1, dx1_f, dr2, alpha, row(wts["ln1_g"]), 0.5, "ln1_bwd")
    small["ln1_g"], small["ln1_b"] = dg.sum(0, keepdims=True), db.sum(0, keepdims=True)
    dh1, dw = _ffn_out_bwd(dr1_h, full["ffn1_w_out"], h1, a1, "ffn1_out_bwd")
    send["ffn1_w_out"] = dw.reshape(N_DEV, -1, d_model)
    exchange3, token = _exchange_start([send["ffn1_w_out"]] + [small[n][None] for n in SMALL_LATE], dh1,
                                       "grad_exchange3_start")
    dw = _ffn_in_dw(x16, dh1, "ffn1_in_dw", token=token)
    send["ffn1_w_in"] = dw.reshape((N_DEV,) + dw.shape[2:])
    exchange4, token = _exchange_start([send["ffn1_w_in"]], dh1, "grad_exchange4_start")
    grad_x = _ffn_in_dx(dh1, full["ffn1_w_in"], dr1, alpha, "ffn1_in_dx", token=token)

    results = {}

    def update(names, recv, prefix):
        for n, r in zip(names, recv):
            results[n] = _adamw(r, wts[n], mom[n], var[n], prefix + n)

    def update_small(names, recv, name):
        items = [(r, lane_dense(n, wts[n]), lane_dense(n, mom[n]), lane_dense(n, var[n])) for n, r in zip(names, recv)]
        for n, outs in zip(names, _adamw_small(items, name)):
            results[n] = tuple(o.reshape(wts[n].shape) for o in outs)

    def done(names):
        return [results[n][3] for n in names]

    update(group1, _exchange_wait(exchange1, [grad_x], "grad_exchange1_wait"), "adamw_")
    recv = _exchange_wait(exchange2, done(group1), "grad_exchange2_wait")
    update(group2, recv[:len(group2)], "adamw_")
    update_small(SMALL_EARLY, recv[len(group2):], "adamw_small")
    recv = _exchange_wait(exchange3, done(group2 + SMALL_EARLY), "grad_exchange3_wait")
    update(("ffn1_w_out",), recv[:1], "adamw_")
    update_small(SMALL_LATE, recv[1:], "adamw_ln1")
    recv = _exchange_wait(exchange4, done(("ffn1_w_out",) + SMALL_LATE), "grad_exchange4_wait")
    update(("ffn1_w_in",), recv, "adamw_")

    outs = [loss, grad_x.reshape(n_seq, seq, d_model)]
    for k in range(4):
        outs += [results[n][k][None] if n in SHARDED else results[n][k] for n in WEIGHTS]
    return tuple(outs)
```

```python
import functools
import math

import jax
import jax.numpy as jnp
from jax import lax
from jax.experimental import pallas as pl
from jax.experimental.pallas import tpu as pltpu

F32 = jnp.float32
BF16 = jnp.bfloat16

N_DEV = 8
LN_EPS = 1e-5
CHUNK = 128
N_SEG = 8
S5_CHANNELS_PER_STEP = 128
VMEM_LIMIT_BYTES = 56 * 1024 * 1024
MM_OPERAND_BLOCK_BYTES = 8 * 1024 * 1024
ADAM_BLOCK_BYTES = 6 * 1024 * 1024

ADAM_LR = 0.001
ADAM_B1 = 0.9
ADAM_B2 = 0.999
ADAM_EPS = 1e-08
ADAM_WD = 0.01
ADAM_STEP = 10

COL_SHARDED = ("ffn1_w_in", "mix_w_in", "up_a", "up_b", "ffn2_w_in", "ple_w_proj")
ROW_SHARDED = ("ffn1_w_out", "ssm_glu_w", "mix_w_out", "ffn2_w_out", "ple_w_gate")
SHARDED = ("ffn1_w_in", "ffn1_w_out", "mix_w_in", "ssm_glu_w", "up_a", "up_b", "mix_w_out",
           "ffn2_w_in", "ffn2_w_out", "ple_w_proj", "ple_w_gate")
WEIGHTS = ("ffn1_w_in", "ffn1_w_out", "ln1_g", "ln1_b", "mix_w_in", "ssm_lambda_re", "ssm_lambda_im",
           "ssm_log_dt", "ssm_b_re", "ssm_b_im", "ssm_c_re", "ssm_c_im", "ssm_d", "ssm_glu_w",
           "ssm_glu_b", "gmlp_ln_g", "gmlp_ln_b", "gmlp_w_s", "gmlp_b_s", "up_a", "up_b", "mix_w_out",
           "ln2_g", "ln2_b", "ffn2_w_in", "ffn2_w_out", "ln3_g", "ln3_b", "ple_w_proj", "ple_w_gate")
GATHER_FIRST = ("ffn1_w_in", "ffn1_w_out")
GATHER_MIX = ("mix_w_in", "ssm_glu_w", "up_a", "up_b", "mix_w_out")
GATHER_LAST = ("ffn2_w_in", "ffn2_w_out", "ple_w_proj", "ple_w_gate")
SMALL = tuple(n for n in WEIGHTS if n not in SHARDED)
SMALL_LATE = ("ln1_g", "ln1_b")
SMALL_EARLY = tuple(n for n in SMALL if n not in SMALL_LATE)
INPUT_NAMES = ("x", "p") + WEIGHTS + ("loss_target",) + tuple("m_" + n for n in WEIGHTS) + tuple(
    "v_" + n for n in WEIGHTS)


def _pick(dim, pref, mult=128):
    if dim <= pref:
        return dim
    d = (pref // mult) * mult
    while d >= mult:
        if dim % d == 0:
            return d
        d -= mult
    return dim


def _params(n_axes=1):
    return pltpu.CompilerParams(dimension_semantics=("arbitrary",) * n_axes,
                                vmem_limit_bytes=VMEM_LIMIT_BYTES)


def _sigmoid(v):
    return 1.0 / (1.0 + jnp.exp(-v))


_GELU_C = math.sqrt(2.0 / math.pi)


def _gelu(v):
    return 0.5 * v * (1.0 + jnp.tanh(_GELU_C * (v + 0.044715 * (v * v * v))))


def _gelu_grad(v):
    t = jnp.tanh(_GELU_C * (v + 0.044715 * (v * v * v)))
    return 0.5 * (1.0 + t) + 0.5 * v * (1.0 - t * t) * (_GELU_C * (1.0 + 3.0 * 0.044715 * v * v))


def _ln_stats(r):
    mu = jnp.mean(r, axis=-1, keepdims=True)
    xc = r - mu
    var = jnp.mean(xc * xc, axis=-1, keepdims=True)
    rstd = lax.rsqrt(var + LN_EPS)
    return xc * rstd, rstd


def _ln_bwd(dy, xhat, rstd, g):
    dxh = dy * g
    m1 = jnp.mean(dxh, axis=-1, keepdims=True)
    m2 = jnp.mean(dxh * xhat, axis=-1, keepdims=True)
    return rstd * (dxh - m1 - xhat * m2)


def _fold8(v):
    rows, w = v.shape
    return jnp.sum(v.reshape(rows // 8, 8, w), axis=0)


def _dot(a, b, ta=False, tb=False):
    dn = (((0 if ta else 1,), (1 if tb else 0,)), ((), ()))
    return lax.dot_general(a.astype(BF16), b.astype(BF16), dn, preferred_element_type=F32)


_TOKEN = pl.BlockSpec((8, 128), lambda *_: (0, 0))


def _mm(a, b, *, name, ta=False, tb=False, out_dtype=F32, add=None, add_scale=1.0, token=None,
        tm=2048, tn=512, tk=4096):
    m_dim = a.shape[1] if ta else a.shape[0]
    k_dim = a.shape[0] if ta else a.shape[1]
    n_dim = b.shape[0] if tb else b.shape[1]
    assert (b.shape[1] if tb else b.shape[0]) == k_dim, (name, a.shape, b.shape)
    tk = _pick(k_dim, tk)
    tm = min(tm, max(256, MM_OPERAND_BLOCK_BYTES // (tk * a.dtype.itemsize)))
    tm, tn = _pick(m_dim, tm), _pick(n_dim, tn)
    nk = k_dim // tk
    has_add = add is not None

    def finish(r, add_ref, o_ref):
        if has_add:
            r = r + add_scale * add_ref[...].astype(F32)
        o_ref[...] = r.astype(out_dtype)

    def body(*refs):
        a_ref, b_ref = refs[:2]
        add_ref = refs[2] if has_add else None
        o_ref = refs[n_in]
        if nk == 1:
            finish(_dot(a_ref[...], b_ref[...], ta, tb), add_ref, o_ref)
            return
        acc_ref = refs[-1]
        k = pl.program_id(2)

        @pl.when(k == 0)
        def _():
            acc_ref[...] = jnp.zeros_like(acc_ref)

        acc_ref[...] += _dot(a_ref[...], b_ref[...], ta, tb)

        @pl.when(k == nk - 1)
        def _():
            finish(acc_ref[...], add_ref, o_ref)

    a_spec = (pl.BlockSpec((tk, tm), lambda i, j, k: (k, i)) if ta
              else pl.BlockSpec((tm, tk), lambda i, j, k: (i, k)))
    b_spec = (pl.BlockSpec((tn, tk), lambda i, j, k: (j, k)) if tb
              else pl.BlockSpec((tk, tn), lambda i, j, k: (k, j)))
    in_specs = [a_spec, b_spec]
    operands = [a, b]
    if has_add:
        in_specs.append(pl.BlockSpec((tm, tn), lambda i, j, k: (i, j)))
        operands.append(add)
    if token is not None:
        in_specs.append(_TOKEN)
        operands.append(token)
    n_in = len(operands)
    return pl.pallas_call(
        body, name=name,
        out_shape=jax.ShapeDtypeStruct((m_dim, n_dim), out_dtype),
        grid=(m_dim // tm, n_dim // tn, nk),
        in_specs=in_specs,
        out_specs=pl.BlockSpec((tm, tn), lambda i, j, k: (i, j)),
        scratch_shapes=[pltpu.VMEM((tm, tn), F32)] if nk > 1 else [],
        compiler_params=_params(3),
    )(*operands)


def _to_bf16(arrays, name):
    n = len(arrays)

    def body(*refs):
        for src, dst in zip(refs[:n], refs[n:]):
            dst[...] = src[...].astype(BF16)

    vmem = pl.BlockSpec(memory_space=pltpu.VMEM)
    return pl.pallas_call(
        body, name=name, out_shape=tuple(jax.ShapeDtypeStruct(a.shape, BF16) for a in arrays),
        in_specs=[vmem] * n, out_specs=tuple([vmem] * n),
        compiler_params=pltpu.CompilerParams(vmem_limit_bytes=VMEM_LIMIT_BYTES))(*arrays)


def _rows(tr, w, cb=0):
    return pl.BlockSpec((tr, w), lambda i: (i, cb))


def _whole(shape):
    nd = len(shape)
    return pl.BlockSpec(tuple(shape), lambda i: (0,) * nd)


def _ffn_in(x16, w_in, name, token=None):
    t, d = x16.shape
    _, nb, _, fb = w_in.shape
    tm = _pick(t, 1024, 8)
    extra = [] if token is None else [token]

    def body(*refs):
        x_ref, wg_ref, wu_ref = refs[:3]
        h_ref, a_ref = refs[-2:]
        xv = x_ref[...]
        g = _dot(xv, wg_ref[0, 0])
        u = _dot(xv, wu_ref[0, 0])
        h_ref[0, 0] = g.astype(BF16)
        h_ref[0, 1] = u.astype(BF16)
        a_ref[0] = (g * _sigmoid(g) * u).astype(BF16)

    return pl.pallas_call(
        body, name=name,
        out_shape=(jax.ShapeDtypeStruct((nb, 2, t, fb), BF16), jax.ShapeDtypeStruct((nb, t, fb), BF16)),
        grid=(t // tm, nb),
        in_specs=[pl.BlockSpec((tm, d), lambda i, j: (i, 0)),
                  pl.BlockSpec((1, 1, d, fb), lambda i, j: (0, j, 0, 0)),
                  pl.BlockSpec((1, 1, d, fb), lambda i, j: (1, j, 0, 0))] + [_TOKEN] * len(extra),
        out_specs=(pl.BlockSpec((1, 2, tm, fb), lambda i, j: (j, 0, i, 0)),
                   pl.BlockSpec((1, tm, fb), lambda i, j: (j, i, 0))),
        compiler_params=_params(2))(x16, w_in, w_in, *extra)


def _ffn_out_ln(a, w_out, xin, g, b, alpha, name):
    nb, t, fb = a.shape
    d = w_out.shape[2]
    tm = _pick(t, 512, 8)

    def body(a_ref, w_ref, x_ref, g_ref, b_ref, r_ref, y_ref, y16_ref):
        f = _dot(a_ref[0], w_ref[0])
        for jb in range(1, nb):
            f = f + _dot(a_ref[jb], w_ref[jb])
        r = alpha * x_ref[...] + 0.5 * f
        xhat, _ = _ln_stats(r)
        y = xhat * g_ref[...] + b_ref[...]
        r_ref[...] = r
        y_ref[...] = y
        y16_ref[...] = y.astype(BF16)

    return pl.pallas_call(
        body, name=name,
        out_shape=(jax.ShapeDtypeStruct((t, d), F32), jax.ShapeDtypeStruct((t, d), F32),
                   jax.ShapeDtypeStruct((t, d), BF16)),
        grid=(t // tm,),
        in_specs=[pl.BlockSpec((nb, tm, fb), lambda i: (0, i, 0)), _whole(w_out.shape), _rows(tm, d),
                  _whole((1, d)), _whole((1, d))],
        out_specs=(_rows(tm, d), _rows(tm, d), _rows(tm, d)),
        compiler_params=_params())(a, w_out, xin, g, b)


def _ffn_out_dx(drs, w_out, h, name, token=None):
    nb, _, t, fb = h.shape
    d = w_out.shape[2]
    tm = _pick(t, 1024, 8)
    extra = [] if token is None else [token]

    def body(*refs):
        dr_ref, w_ref, h_ref, dh_ref = refs[0], refs[1], refs[2], refs[-1]
        da = _dot(dr_ref[...], w_ref[0], tb=True)
        g, u = h_ref[0, 0].astype(F32), h_ref[0, 1].astype(F32)
        sg = _sigmoid(g)
        dh_ref[0, 0] = (da * u * (sg * (1.0 + g * (1.0 - sg)))).astype(BF16)
        dh_ref[0, 1] = (da * (g * sg)).astype(BF16)

    return pl.pallas_call(
        body, name=name, out_shape=jax.ShapeDtypeStruct((nb, 2, t, fb), BF16), grid=(t // tm, nb),
        in_specs=[pl.BlockSpec((tm, d), lambda i, j: (i, 0)),
                  pl.BlockSpec((1, fb, d), lambda i, j: (j, 0, 0)),
                  pl.BlockSpec((1, 2, tm, fb), lambda i, j: (j, 0, i, 0))] + [_TOKEN] * len(extra),
        out_specs=pl.BlockSpec((1, 2, tm, fb), lambda i, j: (j, 0, i, 0)),
        compiler_params=_params(2))(drs, w_out, h, *extra)


def _ffn_out_dw(a, drs, name):
    nb, t, fb = a.shape
    d = drs.shape[1]

    def body(a_ref, dr_ref, o_ref):
        o_ref[0] = _dot(a_ref[0], dr_ref[...], ta=True).astype(BF16)

    return pl.pallas_call(
        body, name=name, out_shape=jax.ShapeDtypeStruct((nb, fb, d), BF16), grid=(nb,),
        in_specs=[pl.BlockSpec((1, t, fb), lambda j: (j, 0, 0)), pl.BlockSpec((t, d), lambda j: (0, 0))],
        out_specs=pl.BlockSpec((1, fb, d), lambda j: (j, 0, 0)),
        compiler_params=_params())(a, drs)


def _ffn_in_dw(x16, dh, name, token=None):
    t, d = x16.shape
    nb, _, _, fb = dh.shape
    extra = [] if token is None else [token]

    def body(*refs):
        x_ref, dh_ref, o_ref = refs[0], refs[1], refs[-1]
        o_ref[0, 0] = _dot(x_ref[...], dh_ref[0, 0], ta=True).astype(BF16)

    return pl.pallas_call(
        body, name=name, out_shape=jax.ShapeDtypeStruct((2, nb, d, fb), BF16), grid=(nb, 2),
        in_specs=[pl.BlockSpec((t, d), lambda j, s: (0, 0)),
                  pl.BlockSpec((1, 1, t, fb), lambda j, s: (j, s, 0, 0))] + [_TOKEN] * len(extra),
        out_specs=pl.BlockSpec((1, 1, d, fb), lambda j, s: (s, j, 0, 0)),
        compiler_params=_params(2))(x16, dh, *extra)


def _ffn_in_dx(dh, w_in, add, add_scale, name, token=None):
    nb, _, t, fb = dh.shape
    d = w_in.shape[2]
    tm = _pick(t, 512, 8)
    has_add = add is not None
    extra = [] if token is None else [token]

    def body(*refs):
        dh_ref, w_ref = refs[:2]
        o_ref = refs[-1]
        acc = None
        for jb in range(nb):
            for s in range(2):
                part = _dot(dh_ref[jb, s], w_ref[s, jb], tb=True)
                acc = part if acc is None else acc + part
        if has_add:
            acc = acc + add_scale * refs[2][...]
        o_ref[...] = acc

    return pl.pallas_call(
        body, name=name, out_shape=jax.ShapeDtypeStruct((t, d), F32), grid=(t // tm,),
        in_specs=[pl.BlockSpec((nb, 2, tm, fb), lambda i: (0, 0, i, 0)), _whole(w_in.shape)]
        + ([_rows(tm, d)] if has_add else []) + [_TOKEN] * len(extra),
        out_specs=_rows(tm, d),
        compiler_params=_params())(*([dh, w_in] + ([add] if has_add else []) + extra))


def _resid_ln_fwd(xin, f, g, b, alpha, scale, name):
    t, d = xin.shape
    tr = _pick(t, 256, 8)

    def body(x_ref, f_ref, g_ref, b_ref, r_ref, y_ref, y16_ref):
        r = alpha * x_ref[...] + scale * f_ref[...]
        xhat, _ = _ln_stats(r)
        y = xhat * g_ref[...] + b_ref[...]
        r_ref[...] = r
        y_ref[...] = y
        y16_ref[...] = y.astype(BF16)

    return pl.pallas_call(
        body, name=name,
        out_shape=(jax.ShapeDtypeStruct((t, d), F32), jax.ShapeDtypeStruct((t, d), F32),
                   jax.ShapeDtypeStruct((t, d), BF16)),
        grid=(t // tr,),
        in_specs=[_rows(tr, d), _rows(tr, d), _whole((1, d)), _whole((1, d))],
        out_specs=(_rows(tr, d), _rows(tr, d), _rows(tr, d)),
        compiler_params=_params())(xin, f, g, b)


def _ln_bwd_call(r, dy_a, dy_b, b_scale, g, out_scale, name):
    t, d = r.shape
    tr = _pick(t, 256, 8)
    has_b = dy_b is not None

    def body(*refs):
        if has_b:
            r_ref, da_ref, db_ref, g_ref, dr_ref, drs_ref, dg_ref, dbeta_ref = refs
            dy = da_ref[...] + b_scale * db_ref[...]
        else:
            r_ref, da_ref, g_ref, dr_ref, drs_ref, dg_ref, dbeta_ref = refs
            dy = da_ref[...]
        xhat, rstd = _ln_stats(r_ref[...])
        dr = _ln_bwd(dy, xhat, rstd, g_ref[...])
        dr_ref[...] = dr
        drs_ref[...] = (out_scale * dr).astype(BF16)

        @pl.when(pl.program_id(0) == 0)
        def _():
            dg_ref[...] = jnp.zeros_like(dg_ref)
            dbeta_ref[...] = jnp.zeros_like(dbeta_ref)

        dg_ref[...] += _fold8(dy * xhat)
        dbeta_ref[...] += _fold8(dy)

    ins = [r, dy_a] + ([dy_b] if has_b else []) + [g]
    in_specs = [_rows(tr, d)] * (3 if has_b else 2) + [_whole((1, d))]
    return pl.pallas_call(
        body, name=name,
        out_shape=(jax.ShapeDtypeStruct((t, d), F32), jax.ShapeDtypeStruct((t, d), BF16),
                   jax.ShapeDtypeStruct((8, d), F32), jax.ShapeDtypeStruct((8, d), F32)),
        grid=(t // tr,), in_specs=in_specs,
        out_specs=(_rows(tr, d), _rows(tr, d), _whole((8, d)), _whole((8, d))),
        compiler_params=_params())(*ins)


def _take_row(v, row_id, s):
    return jnp.sum(jnp.where(row_id == s, v, 0.0), axis=0, keepdims=True)


def _complex_power(ar, ai, n):
    out = None
    while n:
        if n & 1:
            out = (ar, ai) if out is None else (out[0] * ar - out[1] * ai, out[0] * ai + out[1] * ar)
        ar, ai = ar * ar - ai * ai, 2.0 * ar * ai
        n >>= 1
    return out


def _seg_carries(f_re, f_im, p_re, p_im, reverse):
    row_id = lax.broadcasted_iota(jnp.int32, f_re.shape, 0)
    p_re, p_im = _take_row(p_re, row_id, 0), _take_row(p_im, row_id, 0)
    out_re, out_im = jnp.zeros_like(f_re), jnp.zeros_like(f_im)
    c_re, c_im = jnp.zeros_like(p_re), jnp.zeros_like(p_im)
    order = range(N_SEG - 1, -1, -1) if reverse else range(N_SEG)
    for s in order:
        out_re = jnp.where(row_id == s, c_re, out_re)
        out_im = jnp.where(row_id == s, c_im, out_im)
        fr, fi = _take_row(f_re, row_id, s), _take_row(f_im, row_id, s)
        c_re, c_im = fr + p_re * c_re - p_im * c_im, fi + p_re * c_im + p_im * c_re
    return out_re, out_im


def _s5_fwd(za16, b_re, b_im, a_re, a_im, c_re, c_im, n_seq, name):
    t = za16.shape[0]
    n_sg, ch, st = b_re.shape
    seq = t // n_seq
    steps = seq // N_SEG

    def body(za_ref, bre_ref, bim_ref, are, aim, cre_ref, cim_ref, hre, him, y_ref):
        za = za_ref[...]
        hre[...] = _dot(za, bre_ref[0])
        him[...] = _dot(za, bim_ref[0])
        ar = jnp.broadcast_to(are[...], (N_SEG, st))
        ai = jnp.broadcast_to(aim[...], (N_SEG, st))
        zero = jnp.zeros((N_SEG, st), F32)

        def local(k, carry):
            lr, li = carry
            rows = pl.ds(pl.multiple_of(k * N_SEG, N_SEG), N_SEG)
            nr = ar * lr - ai * li + hre[rows, :]
            ni = ar * li + ai * lr + him[rows, :]
            hre[rows, :] = nr
            him[rows, :] = ni
            return nr, ni

        f_re, f_im = lax.fori_loop(0, steps, local, (zero, zero))
        c_re, c_im = _seg_carries(f_re, f_im, *_complex_power(ar, ai, steps), reverse=False)

        def fix(k, carry):
            qr, qi = carry
            rows = pl.ds(pl.multiple_of(k * N_SEG, N_SEG), N_SEG)
            hre[rows, :] = hre[rows, :] + qr
            him[rows, :] = him[rows, :] + qi
            return ar * qr - ai * qi, ar * qi + ai * qr

        lax.fori_loop(0, steps, fix, (ar * c_re - ai * c_im, ar * c_im + ai * c_re))
        y_ref[...] = _dot(hre[...], cre_ref[0]) + _dot(him[...], cim_ref[0])

    rows_ch = pl.BlockSpec((seq, ch), lambda s, j: (s, j))
    rows_st = pl.BlockSpec((seq, st), lambda s, j: (s, j))
    vec = pl.BlockSpec((1, st), lambda s, j: (0, j))
    b_blk = pl.BlockSpec((1, ch, st), lambda s, j: (j, 0, 0))
    c_blk = pl.BlockSpec((1, st, ch), lambda s, j: (j, 0, 0))
    return pl.pallas_call(
        body, name=name,
        out_shape=(jax.ShapeDtypeStruct((t, n_sg * st), F32), jax.ShapeDtypeStruct((t, n_sg * st), F32),
                   jax.ShapeDtypeStruct((t, n_sg * ch), F32)),
        grid=(n_seq, n_sg), in_specs=[rows_ch, b_blk, b_blk, vec, vec, c_blk, c_blk],
        out_specs=(rows_st, rows_st, rows_ch),
        compiler_params=_params(2))(za16, b_re, b_im, a_re, a_im, c_re, c_im)


def _s5_bwd(dy16, dza_skip, h_re, h_im, za16, b_re, b_im, a_re, a_im, c_re, c_im, n_seq, name):
    t = dy16.shape[0]
    n_sg, ch, st = b_re.shape
    seq = t // n_seq
    steps = seq // N_SEG

    def body(dy_ref, skip_ref, hre, him, za_ref, bre_ref, bim_ref, are, aim, cre_ref, cim_ref,
             dza_ref, dbre_ref, dbim_ref, dcre_ref, dcim_ref, dare, daim, lre, lim):
        dy = dy_ref[...]
        lre[...] = _dot(dy, cre_ref[0], tb=True)
        lim[...] = _dot(dy, cim_ref[0], tb=True)
        ar = jnp.broadcast_to(are[...], (N_SEG, st))
        ai = -jnp.broadcast_to(aim[...], (N_SEG, st))
        zero = jnp.zeros((N_SEG, st), F32)

        def local(i, carry):
            lr, li = carry
            k = steps - 1 - i
            rows = pl.ds(pl.multiple_of(k * N_SEG, N_SEG), N_SEG)
            nr = ar * lr - ai * li + lre[rows, :]
            ni = ar * li + ai * lr + lim[rows, :]
            lre[rows, :] = nr
            lim[rows, :] = ni
            return nr, ni

        f_re, f_im = lax.fori_loop(0, steps, local, (zero, zero))
        c_re, c_im = _seg_carries(f_re, f_im, *_complex_power(ar, ai, steps), reverse=True)

        def accumulate(lam_r, lam_i, hp_r, hp_i, acc_r, acc_i):
            return acc_r + (lam_r * hp_r + lam_i * hp_i), acc_i + (lam_i * hp_r - lam_r * hp_i)

        def fix(i, carry):
            qr, qi, acc_r, acc_i = carry
            k = steps - 1 - i
            rows = pl.ds(pl.multiple_of(k * N_SEG, N_SEG), N_SEG)
            prev = pl.ds(pl.multiple_of((k - 1) * N_SEG, N_SEG), N_SEG)
            lam_r = lre[rows, :] + qr
            lam_i = lim[rows, :] + qi
            lre[rows, :] = lam_r
            lim[rows, :] = lam_i
            acc_r, acc_i = accumulate(lam_r, lam_i, hre[prev, :], him[prev, :], acc_r, acc_i)
            return ar * qr - ai * qi, ar * qi + ai * qr, acc_r, acc_i

        qr, qi, acc_r, acc_i = lax.fori_loop(
            0, steps - 1, fix, (ar * c_re - ai * c_im, ar * c_im + ai * c_re, zero, zero))
        first = pl.ds(0, N_SEG)
        last = pl.ds((steps - 1) * N_SEG, N_SEG)
        lam_r = lre[first, :] + qr
        lam_i = lim[first, :] + qi
        lre[first, :] = lam_r
        lim[first, :] = lam_i
        row_id = lax.broadcasted_iota(jnp.int32, (N_SEG, st), 0)
        hp_r = jnp.where(row_id == 0, 0.0, pltpu.roll(hre[last, :], 1, 0))
        hp_i = jnp.where(row_id == 0, 0.0, pltpu.roll(him[last, :], 1, 0))
        acc_r, acc_i = accumulate(lam_r, lam_i, hp_r, hp_i, acc_r, acc_i)

        lam_re16 = lre[...].astype(BF16)
        lam_im16 = lim[...].astype(BF16)
        dza_ref[...] = (_dot(lam_re16, bre_ref[0], tb=True) + _dot(lam_im16, bim_ref[0], tb=True)
                        + skip_ref[...]).astype(BF16)

        @pl.when(pl.program_id(1) == 0)
        def _():
            for ref in (dbre_ref, dbim_ref, dcre_ref, dcim_ref, dare, daim):
                ref[...] = jnp.zeros_like(ref)

        za = za_ref[...]
        dbre_ref[0] += _dot(za, lam_re16, ta=True)
        dbim_ref[0] += _dot(za, lam_im16, ta=True)
        dcre_ref[0] += _dot(hre[...], dy, ta=True)
        dcim_ref[0] += _dot(him[...], dy, ta=True)
        dare[...] += acc_r
        daim[...] += acc_i

    rows_ch = pl.BlockSpec((seq, ch), lambda j, s: (s, j))
    rows_st = pl.BlockSpec((seq, st), lambda j, s: (s, j))
    vec = pl.BlockSpec((1, st), lambda j, s: (0, j))
    b_blk = pl.BlockSpec((1, ch, st), lambda j, s: (j, 0, 0))
    c_blk = pl.BlockSpec((1, st, ch), lambda j, s: (j, 0, 0))
    acc = pl.BlockSpec((N_SEG, st), lambda j, s: (0, j))
    return pl.pallas_call(
        body, name=name,
        out_shape=(jax.ShapeDtypeStruct((t, n_sg * ch), BF16),
                   jax.ShapeDtypeStruct((n_sg, ch, st), F32), jax.ShapeDtypeStruct((n_sg, ch, st), F32),
                   jax.ShapeDtypeStruct((n_sg, st, ch), F32), jax.ShapeDtypeStruct((n_sg, st, ch), F32),
                   jax.ShapeDtypeStruct((N_SEG, n_sg * st), F32), jax.ShapeDtypeStruct((N_SEG, n_sg * st), F32)),
        grid=(n_sg, n_seq),
        in_specs=[rows_ch, rows_ch, rows_st, rows_st, rows_ch, b_blk, b_blk, vec, vec, c_blk, c_blk],
        out_specs=(rows_ch, b_blk, b_blk, c_blk, c_blk, acc, acc),
        scratch_shapes=[pltpu.VMEM((seq, st), F32), pltpu.VMEM((seq, st), F32)],
        compiler_params=_params(2))(dy16, dza_skip, h_re, h_im, za16, b_re, b_im, a_re, a_im, c_re, c_im)


def _s5_post_fwd(yssm, u, d_skip, glu_w, glu_b, name):
    t, w = yssm.shape
    tr = _pick(t, 512, 8)

    def body(y_ref, u_ref, d_ref, w_ref, b_ref, o_ref):
        yg = _gelu(y_ref[...] + d_ref[...] * u_ref[...])
        pre = _dot(yg, w_ref[...]) + b_ref[...]
        o_ref[...] = yg * _sigmoid(pre)

    return pl.pallas_call(
        body, name=name, out_shape=jax.ShapeDtypeStruct((t, w), F32), grid=(t // tr,),
        in_specs=[_rows(tr, w), _rows(tr, w), _whole((1, w)), _whole((w, w)), _whole((1, w))],
        out_specs=_rows(tr, w), compiler_params=_params())(yssm, u, d_skip, glu_w, glu_b)


def _s5_post_bwd(yssm, u, dout, d_skip, glu_w, glu_b, name):
    t, w = yssm.shape
    tr = _pick(t, 512, 8)

    def body(y_ref, u_ref, do_ref, d_ref, w_ref, b_ref, dy_ref, du_ref, dw_ref, dd_ref, db_ref):
        uu = u_ref[...]
        y = y_ref[...] + d_ref[...] * uu
        yg = _gelu(y)
        sg = _sigmoid(_dot(yg, w_ref[...]) + b_ref[...])
        do = do_ref[...]
        dpre = do * yg * sg * (1.0 - sg)
        dyg = do * sg + _dot(dpre, w_ref[...], tb=True)
        dy = dyg * _gelu_grad(y)
        dy_ref[...] = dy.astype(BF16)
        du_ref[...] = dy * d_ref[...]

        @pl.when(pl.program_id(0) == 0)
        def _():
            dw_ref[...] = jnp.zeros_like(dw_ref)
            dd_ref[...] = jnp.zeros_like(dd_ref)
            db_ref[...] = jnp.zeros_like(db_ref)

        dw_ref[...] += _dot(yg, dpre, ta=True)
        dd_ref[...] += _fold8(dy * uu)
        db_ref[...] += _fold8(dpre)

    return pl.pallas_call(
        body, name=name,
        out_shape=(jax.ShapeDtypeStruct((t, w), BF16), jax.ShapeDtypeStruct((t, w), F32),
                   jax.ShapeDtypeStruct((w, w), F32), jax.ShapeDtypeStruct((8, w), F32),
                   jax.ShapeDtypeStruct((8, w), F32)),
        grid=(t // tr,),
        in_specs=[_rows(tr, w), _rows(tr, w), _rows(tr, w), _whole((1, w)), _whole((w, w)),
                  _whole((1, w))],
        out_specs=(_rows(tr, w), _rows(tr, w), _whole((w, w)), _whole((8, w)), _whole((8, w))),
        compiler_params=_params())(yssm, u, dout, d_skip, glu_w, glu_b)


def _head_select(parts, head_dim):
    col_head = lax.broadcasted_iota(jnp.int32, parts[0].shape, 1) // head_dim
    out = jnp.zeros_like(parts[0])
    for h, part in enumerate(parts):
        out = jnp.where(col_head == h, part, out)
    return out


def _gmlp_fwd(proj, ln_g, ln_b, ws, bs_cols, name):
    t = proj.shape[0]
    n_heads = ws.shape[0]
    w = bs_cols.shape[1]
    head_dim = w // n_heads
    cpb = _pick(t // CHUNK, 4, 1)
    tr = cpb * CHUNK

    def body(zu_ref, zv_ref, g_ref, b_ref, ws_ref, bs_ref, o_ref):
        for c in range(cpb):
            rows = pl.ds(c * CHUNK, CHUNK)
            xhat, _ = _ln_stats(_gelu(zv_ref[rows, :]))
            vn = (xhat * g_ref[...] + b_ref[...]).astype(BF16)
            s = _head_select([_dot(ws_ref[h], vn) for h in range(n_heads)], head_dim) + bs_ref[...]
            o_ref[rows, :] = _gelu(zu_ref[rows, :]) * s

    return pl.pallas_call(
        body, name=name, out_shape=jax.ShapeDtypeStruct((t, w), F32), grid=(t // tr,),
        in_specs=[_rows(tr, w, 1), _rows(tr, w, 2), _whole((1, w)), _whole((1, w)),
                  _whole(ws.shape), _whole(bs_cols.shape)],
        out_specs=_rows(tr, w), compiler_params=_params())(proj, proj, ln_g, ln_b, ws, bs_cols)


def _gmlp_bwd(proj, dout, ln_g, ln_b, ws, ws_t, bs_cols, name):
    t = proj.shape[0]
    n_heads = ws.shape[0]
    w = bs_cols.shape[1]
    head_dim = w // n_heads
    cpb = _pick(t // CHUNK, 4, 1)
    tr = cpb * CHUNK

    def body(zu_ref, zv_ref, do_ref, g_ref, b_ref, ws_ref, wst_ref, bs_ref,
             dzu_ref, dzv_ref, dws_ref, dbs_ref, dg_ref, dbeta_ref):
        @pl.when(pl.program_id(0) == 0)
        def _():
            dws_ref[...] = jnp.zeros_like(dws_ref)
            dbs_ref[...] = jnp.zeros_like(dbs_ref)
            dg_ref[...] = jnp.zeros_like(dg_ref)
            dbeta_ref[...] = jnp.zeros_like(dbeta_ref)

        col_head = lax.broadcasted_iota(jnp.int32, (CHUNK, w), 1) // head_dim
        for c in range(cpb):
            rows = pl.ds(c * CHUNK, CHUNK)
            zu, zv, do = zu_ref[rows, :], zv_ref[rows, :], do_ref[rows, :]
            xhat, rstd = _ln_stats(_gelu(zv))
            vn = (xhat * g_ref[...] + b_ref[...]).astype(BF16)
            s = _head_select([_dot(ws_ref[h], vn) for h in range(n_heads)], head_dim) + bs_ref[...]
            dzu_ref[rows, :] = (do * s * _gelu_grad(zu)).astype(BF16)
            ds = do * _gelu(zu)
            ds16 = ds.astype(BF16)
            dbs_ref[...] += ds
            for h in range(n_heads):
                dws_ref[h] += _dot(jnp.where(col_head == h, ds16, jnp.zeros_like(ds16)), vn, tb=True)
            dvn = _head_select([_dot(wst_ref[h], ds16) for h in range(n_heads)], head_dim)
            dg_ref[...] += _fold8(dvn * xhat)
            dbeta_ref[...] += _fold8(dvn)
            dgv = _ln_bwd(dvn, xhat, rstd, g_ref[...])
            dzv_ref[rows, :] = (dgv * _gelu_grad(zv)).astype(BF16)

    return pl.pallas_call(
        body, name=name,
        out_shape=(jax.ShapeDtypeStruct((t, w), BF16), jax.ShapeDtypeStruct((t, w), BF16),
                   jax.ShapeDtypeStruct(ws.shape, F32), jax.ShapeDtypeStruct((CHUNK, w), F32),
                   jax.ShapeDtypeStruct((8, w), F32), jax.ShapeDtypeStruct((8, w), F32)),
        grid=(t // tr,),
        in_specs=[_rows(tr, w, 1), _rows(tr, w, 2), _rows(tr, w), _whole((1, w)), _whole((1, w)),
                  _whole(ws.shape), _whole(ws.shape), _whole(bs_cols.shape)],
        out_specs=(_rows(tr, w), _rows(tr, w), _whole(ws.shape), _whole((CHUNK, w)),
                   _whole((8, w)), _whole((8, w))),
        compiler_params=_params())(proj, proj, dout, ln_g, ln_b, ws, ws_t, bs_cols)


def _merge_fwd(s5out, gm, proj, up_a, up_b, name):
    t, w = s5out.shape
    d = up_a.shape[1]
    assert d == 2 * w
    tr = _pick(t, 256, 8)

    def body(s_ref, gm_ref, ga0, ga1, gb0, gb1, ua_ref, ub_ref, m_ref, ya_ref, yb_ref):
        ya = _dot(s_ref[...], ua_ref[...])
        yb = _dot(gm_ref[...], ub_ref[...])
        ya_ref[...] = ya
        yb_ref[...] = yb
        for half, (ga, gb) in enumerate(((ga0, gb0), (ga1, gb1))):
            cols = slice(half * w, (half + 1) * w)
            m_ref[:, cols] = (_sigmoid(ga[...]) * ya[:, cols] + _sigmoid(gb[...]) * yb[:, cols]).astype(BF16)

    return pl.pallas_call(
        body, name=name,
        out_shape=(jax.ShapeDtypeStruct((t, d), BF16), jax.ShapeDtypeStruct((t, d), F32),
                   jax.ShapeDtypeStruct((t, d), F32)),
        grid=(t // tr,),
        in_specs=[_rows(tr, w), _rows(tr, w), _rows(tr, w, 3), _rows(tr, w, 4), _rows(tr, w, 5),
                  _rows(tr, w, 6), _whole(up_a.shape), _whole(up_b.shape)],
        out_specs=(_rows(tr, d), _rows(tr, d), _rows(tr, d)),
        compiler_params=_params())(s5out, gm, proj, proj, proj, proj, up_a, up_b)


def _merge_bwd(dm, ya, yb, s5out, gm, proj, up_a, up_b, name):
    t, d = dm.shape
    w = d // 2
    tr = _pick(t, 256, 8)

    def body(dm_ref, ya_ref, yb_ref, s_ref, gm_ref, ga0, ga1, gb0, gb1, ua_ref, ub_ref,
             dga_ref, dgb_ref, ds_ref, dgm_ref, dua_ref, dub_ref):
        dm_v, ya, yb = dm_ref[...], ya_ref[...], yb_ref[...]
        dya, dyb = [], []
        for half, (ga, gb) in enumerate(((ga0, gb0), (ga1, gb1))):
            cols = slice(half * w, (half + 1) * w)
            sa, sb = _sigmoid(ga[...]), _sigmoid(gb[...])
            dmh = dm_v[:, cols]
            dga_ref[:, cols] = (dmh * ya[:, cols] * sa * (1.0 - sa)).astype(BF16)
            dgb_ref[:, cols] = (dmh * yb[:, cols] * sb * (1.0 - sb)).astype(BF16)
            dya.append((dmh * sa).astype(BF16))
            dyb.append((dmh * sb).astype(BF16))
        dya = jnp.concatenate(dya, axis=1)
        dyb = jnp.concatenate(dyb, axis=1)
        ds_ref[...] = _dot(dya, ua_ref[...], tb=True)
        dgm_ref[...] = _dot(dyb, ub_ref[...], tb=True)

        @pl.when(pl.program_id(0) == 0)
        def _():
            dua_ref[...] = jnp.zeros_like(dua_ref)
            dub_ref[...] = jnp.zeros_like(dub_ref)

        dua_ref[...] += _dot(s_ref[...], dya, ta=True)
        dub_ref[...] += _dot(gm_ref[...], dyb, ta=True)

    return pl.pallas_call(
        body, name=name,
        out_shape=(jax.ShapeDtypeStruct((t, d), BF16), jax.ShapeDtypeStruct((t, d), BF16),
                   jax.ShapeDtypeStruct((t, w), F32), jax.ShapeDtypeStruct((t, w), F32),
                   jax.ShapeDtypeStruct(up_a.shape, F32), jax.ShapeDtypeStruct(up_b.shape, F32)),
        grid=(t // tr,),
        in_specs=[_rows(tr, d), _rows(tr, d), _rows(tr, d), _rows(tr, w), _rows(tr, w),
                  _rows(tr, w, 3), _rows(tr, w, 4), _rows(tr, w, 5), _rows(tr, w, 6),
                  _whole(up_a.shape), _whole(up_b.shape)],
        out_specs=(_rows(tr, d), _rows(tr, d), _rows(tr, w), _rows(tr, w), _whole(up_a.shape),
                   _whole(up_b.shape)),
        compiler_params=_params())(dm, ya, yb, s5out, gm, proj, proj, proj, proj, up_a, up_b)


def _head(x3, p, target, w_gate, w_proj, name):
    t, d = x3.shape
    pd = p.shape[1]
    tr = _pick(t, 256, 8)
    nb = t // tr

    def body(x_ref, p_ref, t_ref, wg_ref, wp_ref, loss_ref, dx_ref, dwg_ref, dwp_ref):
        xv = x_ref[...]
        sg = _sigmoid(_dot(xv, wg_ref[...]))
        pp = _dot(p_ref[...], wp_ref[...])
        err = xv + sg * pp - t_ref[...]
        loss_ref[...] = jnp.full((8, 128), 0.5 * jnp.sum(jnp.mean(err * err, axis=-1)), F32)
        dout = err * (1.0 / d)
        dgpre = dout * pp * sg * (1.0 - sg)
        dpp = dout * sg
        dx_ref[...] = dout + _dot(dgpre, wg_ref[...], tb=True)

        @pl.when(pl.program_id(0) == 0)
        def _():
            dwg_ref[...] = jnp.zeros_like(dwg_ref)
            dwp_ref[...] = jnp.zeros_like(dwp_ref)

        dwg_ref[...] += _dot(xv, dgpre, ta=True)
        dwp_ref[...] += _dot(p_ref[...], dpp, ta=True)

    return pl.pallas_call(
        body, name=name,
        out_shape=(jax.ShapeDtypeStruct((nb * 8, 128), F32), jax.ShapeDtypeStruct((t, d), F32),
                   jax.ShapeDtypeStruct((d, d), F32), jax.ShapeDtypeStruct((pd, d), F32)),
        grid=(nb,),
        in_specs=[_rows(tr, d), _rows(tr, pd), _rows(tr, d), _whole((d, d)), _whole((pd, d))],
        out_specs=(pl.BlockSpec((8, 128), lambda i: (i, 0)), _rows(tr, d), _whole((d, d)),
                   _whole((pd, d))),
        compiler_params=_params())(x3, p, target, w_gate, w_proj)


_HBM = pl.BlockSpec(memory_space=pltpu.HBM)


def _all_gather(shards, name):
    n = len(shards)

    def body(*refs):
        x_refs, out_refs = refs[:n], refs[n:2 * n]
        send_sems, recv_sems, local_sems = refs[2 * n:]
        x, y, c = lax.axis_index("x"), lax.axis_index("y"), lax.axis_index("c")
        me, sibling = (x, y, c), (x, y, 1 - c)
        chips = [(1 - x, y), (x, 1 - y), (1 - x, 1 - y)]

        def copy(a, k, block, to, own=False):
            slot = out_refs[a].at[4 * block[0] + 2 * block[1] + block[2]]
            return pltpu.make_async_remote_copy(
                src_ref=x_refs[a] if own else slot, dst_ref=slot,
                send_sem=send_sems.at[7 * a + k], recv_sem=recv_sems.at[7 * a + k],
                device_id=to, device_id_type=pl.DeviceIdType.MESH)

        mine = [pltpu.make_async_copy(x_refs[a], out_refs[a].at[4 * x + 2 * y + c], local_sems.at[a])
                for a in range(n)]
        sends = []
        for a in range(n):
            mine[a].start()
            first = [copy(a, 0, me, sibling, own=True)]
            first += [copy(a, 1 + j, me, (*chip, c), own=True) for j, chip in enumerate(chips)]
            for cp in first:
                cp.start()
            sends += first
        for a in range(n):
            for j, chip in enumerate(chips):
                copy(a, 1 + j, (*chip, c), me).wait_recv()
                passed = copy(a, 4 + j, (*chip, c), sibling)
                passed.start()
                sends.append(passed)
        for a in range(n):
            copy(a, 0, sibling, me).wait_recv()
            for j, chip in enumerate(chips):
                copy(a, 4 + j, (*chip, 1 - c), me).wait_recv()
        for cp in sends:
            cp.wait_send()
        for cp in mine:
            cp.wait()

    return pl.pallas_call(
        body, name=name,
        out_shape=tuple(jax.ShapeDtypeStruct((N_DEV,) + s.shape, s.dtype) for s in shards),
        in_specs=[_HBM] * n, out_specs=tuple([_HBM] * n),
        scratch_shapes=[pltpu.SemaphoreType.DMA((7 * n,)), pltpu.SemaphoreType.DMA((7 * n,)),
                        pltpu.SemaphoreType.DMA((n,))],
    )(*shards)


_SEM = pl.BlockSpec(memory_space=pltpu.SEMAPHORE)
_ANY = pl.BlockSpec(memory_space=pl.ANY)
_DATAFLOW = pltpu.SideEffectType.DATAFLOW_SIDE_EFFECTING


def _peer(k, x, y, c):
    return (1 - x if k & 4 else x, 1 - y if k & 2 else y, 1 - c if k & 1 else c)


def _exchange_start(sends, after, name):
    n = len(sends)
    me = 4 * lax.axis_index("x") + 2 * lax.axis_index("y") + lax.axis_index("c")
    lands = []
    for s in sends:
        own = s[0] if s.shape[0] == 1 else lax.dynamic_index_in_dim(s, me, 0, keepdims=False)
        land = lax.empty((N_DEV,) + s.shape[1:], s.dtype)
        lands.append(lax.dynamic_update_index_in_dim(land, own, me, 0))

    def body(*refs):
        s_refs, l_refs = refs[:n], refs[n:2 * n]
        send_sems, recv_sems, token = refs[2 * n + 1], refs[2 * n + 2], refs[-1]
        x, y, c = lax.axis_index("x"), lax.axis_index("y"), lax.axis_index("c")
        for a in range(n):
            same = sends[a].shape[0] == 1
            for k in range(1, N_DEV):
                px, py, pc = _peer(k, x, y, c)
                pltpu.make_async_remote_copy(
                    src_ref=s_refs[a].at[0 if same else 4 * px + 2 * py + pc],
                    dst_ref=l_refs[a].at[4 * x + 2 * y + c],
                    send_sem=send_sems.at[7 * a + k - 1], recv_sem=recv_sems.at[7 * a + k - 1],
                    device_id=(px, py, pc), device_id_type=pl.DeviceIdType.MESH).start()
        token[...] = jnp.zeros_like(token)

    outs = pl.pallas_call(
        body, name=name,
        out_shape=(pltpu.SemaphoreType.DMA((7 * n,)), pltpu.SemaphoreType.DMA((7 * n,)),
                   *[pltpu.HBM(s.shape, s.dtype) for s in sends],
                   *[pltpu.HBM(l.shape, l.dtype) for l in lands],
                   jax.ShapeDtypeStruct((8, 128), F32)),
        in_specs=[_HBM] * (2 * n) + [_ANY],
        out_specs=(_SEM, _SEM, *([_HBM] * (2 * n)), pl.BlockSpec(memory_space=pltpu.VMEM)),
        input_output_aliases={i: 2 + i for i in range(2 * n)},
        compiler_params=pltpu.CompilerParams(has_side_effects=_DATAFLOW),
    )(*[pltpu.with_memory_space_constraint(v, pltpu.HBM) for v in list(sends) + lands], after)
    return (outs[0], outs[1], list(outs[2:2 + n]), list(outs[2 + n:2 + 2 * n])), outs[-1]


def _exchange_wait(handle, after, name):
    send_sems, recv_sems, sends, lands = handle
    n = len(sends)

    def body(*refs):
        s_refs, l_refs = refs[:n], refs[n:2 * n]
        send_sems, recv_sems = refs[2 * n], refs[2 * n + 1]
        x, y, c = lax.axis_index("x"), lax.axis_index("y"), lax.axis_index("c")
        for a in range(n):
            for k in range(1, N_DEV):
                copy = pltpu.make_async_remote_copy(
                    src_ref=s_refs[a].at[0], dst_ref=l_refs[a].at[0],
                    send_sem=send_sems.at[7 * a + k - 1], recv_sem=recv_sems.at[7 * a + k - 1],
                    device_id=_peer(k, x, y, c), device_id_type=pl.DeviceIdType.MESH)
                copy.wait_send()
                copy.wait_recv()

    outs = pl.pallas_call(
        body, name=name,
        out_shape=(*[pltpu.HBM(s.shape, s.dtype) for s in sends], *[pltpu.HBM(l.shape, l.dtype) for l in lands]),
        in_specs=[_HBM] * (2 * n) + [_SEM, _SEM] + [_ANY] * len(after),
        out_specs=tuple([_HBM] * (2 * n)),
        input_output_aliases={i: i for i in range(2 * n)},
        compiler_params=pltpu.CompilerParams(has_side_effects=_DATAFLOW),
    )(*sends, *lands, send_sems, recv_sems, *after)
    return list(outs[n:])


def _adamw(recv, w, m, v, name):
    n, rows, width = recv.shape
    tr = _pick(rows, max(8, ADAM_BLOCK_BYTES // (n * width * recv.dtype.itemsize)), 8)
    c_m = 1.0 - ADAM_B1 ** ADAM_STEP
    c_v = 1.0 - ADAM_B2 ** ADAM_STEP

    def body(r_ref, w_ref, m_ref, v_ref, g_ref, d_ref, nm_ref, nv_ref):
        g = r_ref[0].astype(F32)
        for i in range(1, n):
            g = g + r_ref[i].astype(F32)
        m2 = ADAM_B1 * m_ref[...] + (1.0 - ADAM_B1) * g
        v2 = ADAM_B2 * v_ref[...] + (1.0 - ADAM_B2) * (g * g)
        m_hat = m2 / c_m
        v_hat = v2 / c_v
        g_ref[...] = g
        d_ref[...] = -ADAM_LR * (m_hat / (jnp.sqrt(v_hat) + ADAM_EPS) + ADAM_WD * w_ref[...])
        nm_ref[...] = m2
        nv_ref[...] = v2

    blk = _rows(tr, width)
    shp = jax.ShapeDtypeStruct((rows, width), F32)
    return pl.pallas_call(
        body, name=name, out_shape=(shp, shp, shp, shp), grid=(rows // tr,),
        in_specs=[pl.BlockSpec((n, tr, width), lambda i: (0, i, 0)), blk, blk, blk],
        out_specs=(blk, blk, blk, blk), compiler_params=_params())(recv, w, m, v)


def _join_cols(gathered):
    n, r, c = gathered.shape
    return gathered.transpose(1, 0, 2).reshape(r, n * c)


def _split_cols(full):
    r, c = full.shape
    return full.reshape(r, N_DEV, c // N_DEV).transpose(1, 0, 2)


def _adamw_small(items, name):
    n = len(items)
    c_m = 1.0 - ADAM_B1 ** ADAM_STEP
    c_v = 1.0 - ADAM_B2 ** ADAM_STEP

    def body(*refs):
        ins, outs = refs[:4 * n], refs[4 * n:]
        for k in range(n):
            r_ref, w_ref, m_ref, v_ref = ins[4 * k:4 * k + 4]
            g = r_ref[0]
            for i in range(1, N_DEV):
                g = g + r_ref[i]
            m2 = ADAM_B1 * m_ref[...] + (1.0 - ADAM_B1) * g
            v2 = ADAM_B2 * v_ref[...] + (1.0 - ADAM_B2) * (g * g)
            outs[4 * k][...] = g
            outs[4 * k + 1][...] = -ADAM_LR * ((m2 / c_m) / (jnp.sqrt(v2 / c_v) + ADAM_EPS) + ADAM_WD * w_ref[...])
            outs[4 * k + 2][...] = m2
            outs[4 * k + 3][...] = v2

    vmem = pl.BlockSpec(memory_space=pltpu.VMEM)
    flat = pl.pallas_call(
        body, name=name,
        out_shape=tuple(jax.ShapeDtypeStruct(w.shape, F32) for _, w, _, _ in items for _ in range(4)),
        in_specs=[vmem] * (4 * n), out_specs=tuple([vmem] * (4 * n)),
        compiler_params=pltpu.CompilerParams(vmem_limit_bytes=VMEM_LIMIT_BYTES),
    )(*[a for item in items for a in item])
    return [tuple(flat[4 * k:4 * k + 4]) for k in range(n)]


def _discretise(lam_re, lam_im, log_dt, b_re, b_im):
    dt = jnp.exp(log_dt)[:, None]
    mag = jnp.exp(lam_re * dt)
    ab_re = mag * jnp.cos(lam_im * dt)
    ab_im = mag * jnp.sin(lam_im * dt)
    nr = ab_re - 1.0
    ni = ab_im
    den = lam_re * lam_re + lam_im * lam_im
    coef_re = ((nr * lam_re + ni * lam_im) / den)[..., None]
    coef_im = ((ni * lam_re - nr * lam_im) / den)[..., None]
    bb_re = coef_re * b_re - coef_im * b_im
    bb_im = coef_re * b_im + coef_im * b_re
    return ab_re, ab_im, bb_re, bb_im


def _same_group(gl):
    return jnp.eye(gl, dtype=bool)[None, :, None, :, None]


def _super_groups_in(bb, gl):
    g, p, i = bb.shape
    blocks = bb.reshape(g // gl, gl, p, i).transpose(0, 1, 3, 2)[:, :, :, None, :]
    return jnp.where(_same_group(gl), blocks, 0.0).reshape(g // gl, gl * i, gl * p)


def _super_groups_in_take(dense, gl, p, i):
    n_sg = dense.shape[0]
    blocks = jnp.where(_same_group(gl), dense.reshape(n_sg, gl, i, gl, p), 0.0).sum(axis=3)
    return blocks.transpose(0, 1, 3, 2).reshape(n_sg * gl, p, i)


def _super_groups_out(cc, gl):
    g, i, p = cc.shape
    blocks = cc.reshape(g // gl, gl, i, p).transpose(0, 1, 3, 2)[:, :, :, None, :]
    return jnp.where(_same_group(gl), blocks, 0.0).reshape(g // gl, gl * p, gl * i)


def _super_groups_out_take(dense, gl, i, p):
    n_sg = dense.shape[0]
    blocks = jnp.where(_same_group(gl), dense.reshape(n_sg, gl, p, gl, i), 0.0).sum(axis=3)
    return blocks.transpose(0, 1, 3, 2).reshape(n_sg * gl, i, p)


def _perm_rows(z, n_seq):
    t, w = z.shape
    steps = t // n_seq // N_SEG
    return z.reshape(n_seq, N_SEG, steps, w).transpose(0, 2, 1, 3).reshape(t, w)


def _unperm_rows(z, n_seq):
    t, w = z.shape
    steps = t // n_seq // N_SEG
    return z.reshape(n_seq, steps, N_SEG, w).transpose(0, 2, 1, 3).reshape(t, w)


def kernel(*args):
    assert len(args) == len(INPUT_NAMES)
    inp = dict(zip(INPUT_NAMES, args))
    depth = inp["ffn1_w_in"].shape[0]
    assert depth == 1
    alpha = (2.0 * depth) ** 0.25

    n_seq, seq, d_model = inp["x"].shape
    t = n_seq * seq
    x = inp["x"].reshape(t, d_model)
    x16 = x.astype(BF16)
    p16 = inp["p"][0].reshape(t, -1).astype(BF16)
    target = inp["loss_target"].reshape(t, d_model)
    wts = {n: inp[n][0] if n in SHARDED else inp[n] for n in WEIGHTS}
    mom = {n: inp["m_" + n][0] if n in SHARDED else inp["m_" + n] for n in WEIGHTS}
    var = {n: inp["v_" + n][0] if n in SHARDED else inp["v_" + n] for n in WEIGHTS}

    full = {}

    def place(n, g):
        if n in ("ffn1_w_in", "ffn2_w_in"):
            full[n] = g.reshape((2, N_DEV // 2) + g.shape[1:])
        elif n in ("ffn1_w_out", "ffn2_w_out"):
            full[n] = g.reshape(N_DEV // 2, 2 * g.shape[1], g.shape[2])
        elif n in COL_SHARDED:
            full[n] = _join_cols(g)
        else:
            full[n] = g.reshape(N_DEV * g.shape[1], g.shape[2])

    w16 = dict(zip(GATHER_FIRST, _to_bf16([wts[n] for n in GATHER_FIRST], "cast_ffn1")))
    later = GATHER_MIX + GATHER_LAST
    w16.update(zip(later, _to_bf16([wts[n] for n in later], "cast_rest")))
    for n, g in zip(GATHER_FIRST, _all_gather([w16[n] for n in GATHER_FIRST], "gather_ffn1")):
        place(n, g)
    gather_mix, gather_mix_token = _exchange_start([w16[n][None] for n in GATHER_MIX], full["ffn1_w_out"],
                                                   "gather_mix_start")

    def row(v):
        return v.reshape(1, -1)

    _, n_groups, n_state = wts["ssm_lambda_re"].shape
    n_ch = wts["ssm_b_re"].shape[3]
    disc_in = tuple(wts[n][0] for n in ("ssm_lambda_re", "ssm_lambda_im", "ssm_log_dt", "ssm_b_re", "ssm_b_im"))
    (ab_re, ab_im, bb_re, bb_im), disc_vjp = jax.vjp(_discretise, *disc_in)
    a_re, a_im = row(ab_re), row(ab_im)
    gl = S5_CHANNELS_PER_STEP // n_ch
    b_sg_re = _super_groups_in(bb_re, gl).astype(BF16)
    b_sg_im = _super_groups_in(bb_im, gl).astype(BF16)
    c_sg_re = _super_groups_out(wts["ssm_c_re"][0], gl).astype(BF16)
    c_sg_im = _super_groups_out(-wts["ssm_c_im"][0], gl).astype(BF16)

    ws = wts["gmlp_w_s"][0]
    n_heads = ws.shape[0]
    d_gmlp = wts["gmlp_ln_g"].shape[1]
    causal = jnp.tril(jnp.ones((CHUNK, CHUNK), dtype=bool))
    ws_masked = jnp.where(causal[None], ws, 0.0).astype(BF16)
    ws_masked_t = ws_masked.transpose(0, 2, 1)
    bs_cols = jnp.repeat(wts["gmlp_b_s"][0].T, d_gmlp // n_heads, axis=1)
    d_ssm = n_groups * n_ch
    assert d_ssm == d_gmlp and d_model == 2 * d_ssm

    h1, a1 = _ffn_in(x16, full["ffn1_w_in"], "ffn1_in", token=gather_mix_token)
    r1, x1, x1_16 = _ffn_out_ln(a1, full["ffn1_w_out"], x, row(wts["ln1_g"]), row(wts["ln1_b"]), alpha,
                                "ffn1_out_ln")

    for n, g in zip(GATHER_MIX, _exchange_wait(gather_mix, [x1_16], "gather_mix_wait")):
        place(n, g)
    gather_last, token = _exchange_start([w16[n][None] for n in GATHER_LAST], full["mix_w_in"],
                                         "gather_ffn2_start")
    proj = _mm(x1_16, full["mix_w_in"], name="mix_in", token=token)
    za_p = _perm_rows(proj[:, :d_ssm], n_seq)
    za_p16 = za_p.astype(BF16)
    h_re, h_im, yssm = _s5_fwd(za_p16, b_sg_re, b_sg_im, a_re, a_im, c_sg_re, c_sg_im, n_seq, "s5_fwd")
    s5out_p = _s5_post_fwd(yssm, za_p, row(wts["ssm_d"]), full["ssm_glu_w"], row(wts["ssm_glu_b"]),
                           "s5_post")
    s5out = _unperm_rows(s5out_p, n_seq)
    gm = _gmlp_fwd(proj, row(wts["gmlp_ln_g"]), row(wts["gmlp_ln_b"]), ws_masked, bs_cols, "gmlp")
    m_mix, y_a, y_b = _merge_fwd(s5out, gm, proj, full["up_a"], full["up_b"], "merge")
    mixed = _mm(m_mix, full["mix_w_out"], name="mix_out")
    r2, x2, x2_16 = _resid_ln_fwd(x1, mixed, row(wts["ln2_g"]), row(wts["ln2_b"]), alpha, 1.0, "ln2")

    for n, g in zip(GATHER_LAST, _exchange_wait(gather_last, [x2_16], "gather_ffn2_wait")):
        place(n, g)
    h2, a2 = _ffn_in(x2_16, full["ffn2_w_in"], "ffn2_in")
    r3, x3, _ = _ffn_out_ln(a2, full["ffn2_w_out"], x2, row(wts["ln3_g"]), row(wts["ln3_b"]), alpha,
                            "ffn2_out_ln")

    small = {}
    send = {}

    def lane_dense(n, v):
        return v.reshape(v.shape[:2] + (-1,)) if n in ("ssm_b_re", "ssm_b_im") else v
    loss_parts, dx3, dw_gate, dw_proj = _head(x3, p16, target, full["ple_w_gate"], full["ple_w_proj"], "head")
    loss = lax.psum(jnp.sum(loss_parts[::8, 0]), ("x", "y", "c"))
    send["ple_w_gate"] = dw_gate.astype(BF16).reshape(N_DEV, -1, d_model)
    send["ple_w_proj"] = _split_cols(dw_proj.astype(BF16))

    dr3, dr3_h, dg, db = _ln_bwd_call(r3, dx3, None, 1.0, row(wts["ln3_g"]), 0.5, "ln3_bwd")
    small["ln3_g"], small["ln3_b"] = dg.sum(0, keepdims=True), db.sum(0, keepdims=True)
    dh2 = _ffn_out_dx(dr3_h, full["ffn2_w_out"], h2, "ffn2_out_dx")
    dw = _ffn_out_dw(a2, dr3_h, "ffn2_out_dw")
    send["ffn2_w_out"] = dw.reshape(N_DEV, -1, d_model)
    dw = _ffn_in_dw(x2_16, dh2, "ffn2_in_dw")
    send["ffn2_w_in"] = dw.reshape((N_DEV,) + dw.shape[2:])
    group1 = ("ffn2_w_in", "ffn2_w_out", "ple_w_gate", "ple_w_proj")
    exchange1, token = _exchange_start([send[n] for n in group1], dh2, "grad_exchange1_start")
    dx2_f = _ffn_in_dx(dh2, full["ffn2_w_in"], None, 1.0, "ffn2_in_dx", token=token)

    dr2, dr2_h, dg, db = _ln_bwd_call(r2, dx2_f, dr3, alpha, row(wts["ln2_g"]), 1.0, "ln2_bwd")
    small["ln2_g"], small["ln2_b"] = dg.sum(0, keepdims=True), db.sum(0, keepdims=True)
    dm = _mm(dr2_h, full["mix_w_out"], tb=True, name="mix_out_dx")
    send["mix_w_out"] = _mm(m_mix, dr2_h, ta=True, name="mix_out_dw", out_dtype=BF16).reshape(
        N_DEV, -1, d_model)
    dga, dgb, ds5out, dgm, dw_a, dw_b = _merge_bwd(
        dm, y_a, y_b, s5out, gm, proj, full["up_a"], full["up_b"], "merge_bwd")
    send["up_a"] = _split_cols(dw_a.astype(BF16))
    send["up_b"] = _split_cols(dw_b.astype(BF16))

    dzu, dzv, dws, dbs_cols, dg, db = _gmlp_bwd(
        proj, dgm, row(wts["gmlp_ln_g"]), row(wts["gmlp_ln_b"]), ws_masked, ws_masked_t, bs_cols,
        "gmlp_bwd")
    small["gmlp_ln_g"], small["gmlp_ln_b"] = dg.sum(0, keepdims=True), db.sum(0, keepdims=True)
    small["gmlp_w_s"] = jnp.where(causal[None], dws, 0.0)[None]
    small["gmlp_b_s"] = dbs_cols.reshape(CHUNK, n_heads, -1).sum(-1).T[None]

    ds5out_p = _perm_rows(ds5out, n_seq)
    dy_p, dza_skip, dw_glu, dd, dgb_glu = _s5_post_bwd(
        yssm, za_p, ds5out_p, row(wts["ssm_d"]), full["ssm_glu_w"], row(wts["ssm_glu_b"]),
        "s5_post_bwd")
    send["ssm_glu_w"] = dw_glu.astype(BF16).reshape(N_DEV, -1, d_ssm)
    small["ssm_d"], small["ssm_glu_b"] = dd.sum(0, keepdims=True), dgb_glu.sum(0, keepdims=True)
    dza_p, dbb_re, dbb_im, dc_re, dc_im, da_re, da_im = _s5_bwd(
        dy_p, dza_skip, h_re, h_im, za_p16, b_sg_re, b_sg_im, a_re, a_im, c_sg_re, c_sg_im, n_seq, "s5_bwd")
    small["ssm_c_re"] = _super_groups_out_take(dc_re, gl, n_ch, n_state)[None]
    small["ssm_c_im"] = -_super_groups_out_take(dc_im, gl, n_ch, n_state)[None]
    dbb_re = _super_groups_in_take(dbb_re, gl, n_state, n_ch)
    dbb_im = _super_groups_in_take(dbb_im, gl, n_state, n_ch)
    dab_re = da_re.sum(0).reshape(n_groups, n_state)
    dab_im = da_im.sum(0).reshape(n_groups, n_state)
    for n, g in zip(("ssm_lambda_re", "ssm_lambda_im", "ssm_log_dt", "ssm_b_re", "ssm_b_im"),
                    disc_vjp((dab_re, dab_im, dbb_re, dbb_im))):
        small[n] = g[None]

    dproj = jnp.concatenate([_unperm_rows(dza_p, n_seq), dzu, dzv, dga, dgb], axis=1)
    send["mix_w_in"] = _split_cols(_mm(x1_16, dproj, ta=True, name="mix_in_dw", out_dtype=BF16))
    group2 = ("mix_w_in", "mix_w_out", "up_a", "up_b", "ssm_glu_w")
    exchange2, token = _exchange_start(
        [send[n] for n in group2] + [lane_dense(n, small[n])[None] for n in SMALL_EARLY], dproj, "grad_exchange2_start")
    dx1_f = _mm(dproj, full["mix_w_in"], tb=True, name="mix_in_dx", token=token)

    dr1, dr1_h, dg, db = _ln_bwd_call(r1, dx1_f, dr2, alpha, row(wts["ln1_g"]), 0.5, "ln1_bwd")
    small["ln1_g"], small["ln1_b"] = dg.sum(0, keepdims=True), db.sum(0, keepdims=True)
    dw = _ffn_out_dw(a1, dr1_h, "ffn1_out_dw")
    send["ffn1_w_out"] = dw.reshape(N_DEV, -1, d_model)
    exchange3, token = _exchange_start([send["ffn1_w_out"]] + [small[n][None] for n in SMALL_LATE], dr1_h,
                                       "grad_exchange3_start")
    dh1 = _ffn_out_dx(dr1_h, full["ffn1_w_out"], h1, "ffn1_out_dx", token=token)
    dw = _ffn_in_dw(x16, dh1, "ffn1_in_dw")
    send["ffn1_w_in"] = dw.reshape((N_DEV,) + dw.shape[2:])
    exchange4, token = _exchange_start([send["ffn1_w_in"]], dh1, "grad_exchange4_start")
    grad_x = _ffn_in_dx(dh1, full["ffn1_w_in"], dr1, alpha, "ffn1_in_dx", token=token)

    results = {}

    def update(names, recv, prefix):
        for n, r in zip(names, recv):
            results[n] = _adamw(r, wts[n], mom[n], var[n], prefix + n)

    def update_small(names, recv, name):
        items = [(r, lane_dense(n, wts[n]), lane_dense(n, mom[n]), lane_dense(n, var[n])) for n, r in zip(names, recv)]
        for n, outs in zip(names, _adamw_small(items, name)):
            results[n] = tuple(o.reshape(wts[n].shape) for o in outs)

    def done(names):
        return [results[n][3] for n in names]

    update(group1, _exchange_wait(exchange1, [grad_x], "grad_exchange1_wait"), "adamw_")
    recv = _exchange_wait(exchange2, done(group1), "grad_exchange2_wait")
    update(group2, recv[:len(group2)], "adamw_")
    update_small(SMALL_EARLY, recv[len(group2):], "adamw_small")
    recv = _exchange_wait(exchange3, done(group2 + SMALL_EARLY), "grad_exchange3_wait")
    update(("ffn1_w_out",), recv[:1], "adamw_")
    update_small(SMALL_LATE, recv[1:], "adamw_ln1")
    recv = _exchange_wait(exchange4, done(("ffn1_w_out",) + SMALL_LATE), "grad_exchange4_wait")
    update(("ffn1_w_in",), recv, "adamw_")

    outs = [loss, grad_x.reshape(n_seq, seq, d_model)]
    for k in range(4):
        outs += [results[n][k][None] if n in SHARDED else results[n][k] for n in WEIGHTS]
    return tuple(outs)
```

```python
import functools
import math

import jax
import jax.numpy as jnp
from jax import lax
from jax.experimental import pallas as pl
from jax.experimental.pallas import tpu as pltpu

F32 = jnp.float32
BF16 = jnp.bfloat16

N_DEV = 8
LN_EPS = 1e-5
CHUNK = 128
N_SEG = 8
S5_CHANNELS_PER_STEP = 128
VMEM_LIMIT_BYTES = 56 * 1024 * 1024
MM_OPERAND_BLOCK_BYTES = 8 * 1024 * 1024
ADAM_BLOCK_BYTES = 6 * 1024 * 1024

ADAM_LR = 0.001
ADAM_B1 = 0.9
ADAM_B2 = 0.999
ADAM_EPS = 1e-08
ADAM_WD = 0.01
ADAM_STEP = 10

COL_SHARDED = ("ffn1_w_in", "mix_w_in", "up_a", "up_b", "ffn2_w_in", "ple_w_proj")
ROW_SHARDED = ("ffn1_w_out", "ssm_glu_w", "mix_w_out", "ffn2_w_out", "ple_w_gate")
SHARDED = ("ffn1_w_in", "ffn1_w_out", "mix_w_in", "ssm_glu_w", "up_a", "up_b", "mix_w_out",
           "ffn2_w_in", "ffn2_w_out", "ple_w_proj", "ple_w_gate")
WEIGHTS = ("ffn1_w_in", "ffn1_w_out", "ln1_g", "ln1_b", "mix_w_in", "ssm_lambda_re", "ssm_lambda_im",
           "ssm_log_dt", "ssm_b_re", "ssm_b_im", "ssm_c_re", "ssm_c_im", "ssm_d", "ssm_glu_w",
           "ssm_glu_b", "gmlp_ln_g", "gmlp_ln_b", "gmlp_w_s", "gmlp_b_s", "up_a", "up_b", "mix_w_out",
           "ln2_g", "ln2_b", "ffn2_w_in", "ffn2_w_out", "ln3_g", "ln3_b", "ple_w_proj", "ple_w_gate")
GATHER_FIRST = ("ffn1_w_in", "ffn1_w_out")
GATHER_MIX = ("mix_w_in", "ssm_glu_w", "up_a", "up_b", "mix_w_out")
GATHER_LAST = ("ffn2_w_in", "ffn2_w_out", "ple_w_proj", "ple_w_gate")
SMALL = tuple(n for n in WEIGHTS if n not in SHARDED)
SMALL_HEAD = ("ln3_g", "ln3_b")
SMALL_LATE = ("ln1_g", "ln1_b")
SMALL_MID = tuple(n for n in SMALL if n not in SMALL_HEAD + SMALL_LATE)
SMALL_BF16 = ("gmlp_w_s", "ssm_b_re", "ssm_b_im", "ssm_c_re", "ssm_c_im")
INPUT_NAMES = ("x", "p") + WEIGHTS + ("loss_target",) + tuple("m_" + n for n in WEIGHTS) + tuple(
    "v_" + n for n in WEIGHTS)


def _pick(dim, pref, mult=128):
    if dim <= pref:
        return dim
    d = (pref // mult) * mult
    while d >= mult:
        if dim % d == 0:
            return d
        d -= mult
    return dim


def _params(n_axes=1):
    return pltpu.CompilerParams(dimension_semantics=("arbitrary",) * n_axes,
                                vmem_limit_bytes=VMEM_LIMIT_BYTES)


def _sigmoid(v):
    return 1.0 / (1.0 + jnp.exp(-v))


_GELU_C = math.sqrt(2.0 / math.pi)


def _gelu(v):
    return 0.5 * v * (1.0 + jnp.tanh(_GELU_C * (v + 0.044715 * (v * v * v))))


def _gelu_grad(v):
    t = jnp.tanh(_GELU_C * (v + 0.044715 * (v * v * v)))
    return 0.5 * (1.0 + t) + 0.5 * v * (1.0 - t * t) * (_GELU_C * (1.0 + 3.0 * 0.044715 * v * v))


def _ln_stats(r):
    mu = jnp.mean(r, axis=-1, keepdims=True)
    xc = r - mu
    var = jnp.mean(xc * xc, axis=-1, keepdims=True)
    rstd = lax.rsqrt(var + LN_EPS)
    return xc * rstd, rstd


def _ln_bwd(dy, xhat, rstd, g):
    dxh = dy * g
    m1 = jnp.mean(dxh, axis=-1, keepdims=True)
    m2 = jnp.mean(dxh * xhat, axis=-1, keepdims=True)
    return rstd * (dxh - m1 - xhat * m2)


def _fold8(v):
    rows, w = v.shape
    return jnp.sum(v.reshape(rows // 8, 8, w), axis=0)


def _dot(a, b, ta=False, tb=False):
    dn = (((0 if ta else 1,), (1 if tb else 0,)), ((), ()))
    return lax.dot_general(a.astype(BF16), b.astype(BF16), dn, preferred_element_type=F32)


_TOKEN = pl.BlockSpec((8, 128), lambda *_: (0, 0))


def _mm(a, b, *, name, ta=False, tb=False, out_dtype=F32, add=None, add_scale=1.0, token=None,
        tm=2048, tn=512, tk=4096):
    m_dim = a.shape[1] if ta else a.shape[0]
    k_dim = a.shape[0] if ta else a.shape[1]
    n_dim = b.shape[0] if tb else b.shape[1]
    assert (b.shape[1] if tb else b.shape[0]) == k_dim, (name, a.shape, b.shape)
    tk = _pick(k_dim, tk)
    tm = min(tm, max(256, MM_OPERAND_BLOCK_BYTES // (tk * a.dtype.itemsize)))
    tm, tn = _pick(m_dim, tm), _pick(n_dim, tn)
    nk = k_dim // tk
    has_add = add is not None

    def finish(r, add_ref, o_ref):
        if has_add:
            r = r + add_scale * add_ref[...].astype(F32)
        o_ref[...] = r.astype(out_dtype)

    def body(*refs):
        a_ref, b_ref = refs[:2]
        add_ref = refs[2] if has_add else None
        o_ref = refs[n_in]
        if nk == 1:
            finish(_dot(a_ref[...], b_ref[...], ta, tb), add_ref, o_ref)
            return
        acc_ref = refs[-1]
        k = pl.program_id(2)

        @pl.when(k == 0)
        def _():
            acc_ref[...] = jnp.zeros_like(acc_ref)

        acc_ref[...] += _dot(a_ref[...], b_ref[...], ta, tb)

        @pl.when(k == nk - 1)
        def _():
            finish(acc_ref[...], add_ref, o_ref)

    a_spec = (pl.BlockSpec((tk, tm), lambda i, j, k: (k, i)) if ta
              else pl.BlockSpec((tm, tk), lambda i, j, k: (i, k)))
    b_spec = (pl.BlockSpec((tn, tk), lambda i, j, k: (j, k)) if tb
              else pl.BlockSpec((tk, tn), lambda i, j, k: (k, j)))
    in_specs = [a_spec, b_spec]
    operands = [a, b]
    if has_add:
        in_specs.append(pl.BlockSpec((tm, tn), lambda i, j, k: (i, j)))
        operands.append(add)
    if token is not None:
        in_specs.append(_TOKEN)
        operands.append(token)
    n_in = len(operands)
    return pl.pallas_call(
        body, name=name,
        out_shape=jax.ShapeDtypeStruct((m_dim, n_dim), out_dtype),
        grid=(m_dim // tm, n_dim // tn, nk),
        in_specs=in_specs,
        out_specs=pl.BlockSpec((tm, tn), lambda i, j, k: (i, j)),
        scratch_shapes=[pltpu.VMEM((tm, tn), F32)] if nk > 1 else [],
        compiler_params=_params(3),
    )(*operands)


def _to_bf16(arrays, name):
    n = len(arrays)

    def body(*refs):
        for src, dst in zip(refs[:n], refs[n:]):
            dst[...] = src[...].astype(BF16)

    vmem = pl.BlockSpec(memory_space=pltpu.VMEM)
    return pl.pallas_call(
        body, name=name, out_shape=tuple(jax.ShapeDtypeStruct(a.shape, BF16) for a in arrays),
        in_specs=[vmem] * n, out_specs=tuple([vmem] * n),
        compiler_params=pltpu.CompilerParams(vmem_limit_bytes=VMEM_LIMIT_BYTES))(*arrays)


def _rows(tr, w, cb=0):
    return pl.BlockSpec((tr, w), lambda i: (i, cb))


def _whole(shape):
    nd = len(shape)
    return pl.BlockSpec(tuple(shape), lambda i: (0,) * nd)


def _ffn_in(x16, w_in, name, token=None):
    t, d = x16.shape
    _, nb, _, fb = w_in.shape
    tm = _pick(t, 1024, 8)
    extra = [] if token is None else [token]

    def body(*refs):
        x_ref, wg_ref, wu_ref = refs[:3]
        h_ref, a_ref = refs[-2:]
        xv = x_ref[...]
        g = _dot(xv, wg_ref[0, 0])
        u = _dot(xv, wu_ref[0, 0])
        h_ref[0, 0] = g.astype(BF16)
        h_ref[0, 1] = u.astype(BF16)
        a_ref[0] = (g * _sigmoid(g) * u).astype(BF16)

    return pl.pallas_call(
        body, name=name,
        out_shape=(jax.ShapeDtypeStruct((nb, 2, t, fb), BF16), jax.ShapeDtypeStruct((nb, t, fb), BF16)),
        grid=(t // tm, nb),
        in_specs=[pl.BlockSpec((tm, d), lambda i, j: (i, 0)),
                  pl.BlockSpec((1, 1, d, fb), lambda i, j: (0, j, 0, 0)),
                  pl.BlockSpec((1, 1, d, fb), lambda i, j: (1, j, 0, 0))] + [_TOKEN] * len(extra),
        out_specs=(pl.BlockSpec((1, 2, tm, fb), lambda i, j: (j, 0, i, 0)),
                   pl.BlockSpec((1, tm, fb), lambda i, j: (j, i, 0))),
        compiler_params=_params(2))(x16, w_in, w_in, *extra)


def _ffn_out_ln(a, w_out, xin, g, b, alpha, name):
    nb, t, fb = a.shape
    d = w_out.shape[2]
    tm = _pick(t, 512, 8)

    def body(a_ref, w_ref, x_ref, g_ref, b_ref, r_ref, y_ref, y16_ref):
        f = _dot(a_ref[0], w_ref[0])
        for jb in range(1, nb):
            f = f + _dot(a_ref[jb], w_ref[jb])
        r = alpha * x_ref[...] + 0.5 * f
        xhat, _ = _ln_stats(r)
        y = xhat * g_ref[...] + b_ref[...]
        r_ref[...] = r
        y_ref[...] = y
        y16_ref[...] = y.astype(BF16)

    return pl.pallas_call(
        body, name=name,
        out_shape=(jax.ShapeDtypeStruct((t, d), F32), jax.ShapeDtypeStruct((t, d), F32),
                   jax.ShapeDtypeStruct((t, d), BF16)),
        grid=(t // tm,),
        in_specs=[pl.BlockSpec((nb, tm, fb), lambda i: (0, i, 0)), _whole(w_out.shape), _rows(tm, d),
                  _whole((1, d)), _whole((1, d))],
        out_specs=(_rows(tm, d), _rows(tm, d), _rows(tm, d)),
        compiler_params=_params())(a, w_out, xin, g, b)


def _ffn_out_dx(drs, w_out, h, name, token=None):
    nb, _, t, fb = h.shape
    d = w_out.shape[2]
    tm = _pick(t, 1024, 8)
    extra = [] if token is None else [token]

    def body(*refs):
        dr_ref, w_ref, h_ref, dh_ref = refs[0], refs[1], refs[2], refs[-1]
        da = _dot(dr_ref[...], w_ref[0], tb=True)
        g, u = h_ref[0, 0].astype(F32), h_ref[0, 1].astype(F32)
        sg = _sigmoid(g)
        dh_ref[0, 0] = (da * u * (sg * (1.0 + g * (1.0 - sg)))).astype(BF16)
        dh_ref[0, 1] = (da * (g * sg)).astype(BF16)

    return pl.pallas_call(
        body, name=name, out_shape=jax.ShapeDtypeStruct((nb, 2, t, fb), BF16), grid=(t // tm, nb),
        in_specs=[pl.BlockSpec((tm, d), lambda i, j: (i, 0)),
                  pl.BlockSpec((1, fb, d), lambda i, j: (j, 0, 0)),
                  pl.BlockSpec((1, 2, tm, fb), lambda i, j: (j, 0, i, 0))] + [_TOKEN] * len(extra),
        out_specs=pl.BlockSpec((1, 2, tm, fb), lambda i, j: (j, 0, i, 0)),
        compiler_params=_params(2))(drs, w_out, h, *extra)


def _ffn_out_dw(a, drs, name):
    nb, t, fb = a.shape
    d = drs.shape[1]

    def body(a_ref, dr_ref, o_ref):
        o_ref[0] = _dot(a_ref[0], dr_ref[...], ta=True).astype(BF16)

    return pl.pallas_call(
        body, name=name, out_shape=jax.ShapeDtypeStruct((nb, fb, d), BF16), grid=(nb,),
        in_specs=[pl.BlockSpec((1, t, fb), lambda j: (j, 0, 0)), pl.BlockSpec((t, d), lambda j: (0, 0))],
        out_specs=pl.BlockSpec((1, fb, d), lambda j: (j, 0, 0)),
        compiler_params=_params())(a, drs)


def _ffn_in_dw(x16, dh, name, token=None):
    t, d = x16.shape
    nb, _, _, fb = dh.shape
    extra = [] if token is None else [token]

    def body(*refs):
        x_ref, dh_ref, o_ref = refs[0], refs[1], refs[-1]
        o_ref[0, 0] = _dot(x_ref[...], dh_ref[0, 0], ta=True).astype(BF16)

    return pl.pallas_call(
        body, name=name, out_shape=jax.ShapeDtypeStruct((2, nb, d, fb), BF16), grid=(nb, 2),
        in_specs=[pl.BlockSpec((t, d), lambda j, s: (0, 0)),
                  pl.BlockSpec((1, 1, t, fb), lambda j, s: (j, s, 0, 0))] + [_TOKEN] * len(extra),
        out_specs=pl.BlockSpec((1, 1, d, fb), lambda j, s: (s, j, 0, 0)),
        compiler_params=_params(2))(x16, dh, *extra)


def _ffn_in_dx(dh, w_in, add, add_scale, name, token=None):
    nb, _, t, fb = dh.shape
    d = w_in.shape[2]
    tm = _pick(t, 512, 8)
    has_add = add is not None
    extra = [] if token is None else [token]

    def body(*refs):
        dh_ref, w_ref = refs[:2]
        o_ref = refs[-1]
        acc = None
        for jb in range(nb):
            for s in range(2):
                part = _dot(dh_ref[jb, s], w_ref[s, jb], tb=True)
                acc = part if acc is None else acc + part
        if has_add:
            acc = acc + add_scale * refs[2][...]
        o_ref[...] = acc

    return pl.pallas_call(
        body, name=name, out_shape=jax.ShapeDtypeStruct((t, d), F32), grid=(t // tm,),
        in_specs=[pl.BlockSpec((nb, 2, tm, fb), lambda i: (0, 0, i, 0)), _whole(w_in.shape)]
        + ([_rows(tm, d)] if has_add else []) + [_TOKEN] * len(extra),
        out_specs=_rows(tm, d),
        compiler_params=_params())(*([dh, w_in] + ([add] if has_add else []) + extra))


def _ffn_in_dx_ln(dh, w_in, r, dy_b, b_scale, ln_g, out_scale, name, token=None):
    nb, _, t, fb = dh.shape
    d = w_in.shape[2]
    tm = _pick(t, 512, 8)
    extra = [] if token is None else [token]

    def body(*refs):
        dh_ref, w_ref, r_ref, dyb_ref, g_ref = refs[:5]
        dr_ref, drs_ref, dg_ref, dbeta_ref = refs[-4:]
        dy = b_scale * dyb_ref[...]
        for jb in range(nb):
            for s in range(2):
                dy = dy + _dot(dh_ref[jb, s], w_ref[s, jb], tb=True)
        xhat, rstd = _ln_stats(r_ref[...])
        dr = _ln_bwd(dy, xhat, rstd, g_ref[...])
        dr_ref[...] = dr
        drs_ref[...] = (out_scale * dr).astype(BF16)

        @pl.when(pl.program_id(0) == 0)
        def _():
            dg_ref[...] = jnp.zeros_like(dg_ref)
            dbeta_ref[...] = jnp.zeros_like(dbeta_ref)

        dg_ref[...] += _fold8(dy * xhat)
        dbeta_ref[...] += _fold8(dy)

    return pl.pallas_call(
        body, name=name,
        out_shape=(jax.ShapeDtypeStruct((t, d), F32), jax.ShapeDtypeStruct((t, d), BF16),
                   jax.ShapeDtypeStruct((8, d), F32), jax.ShapeDtypeStruct((8, d), F32)),
        grid=(t // tm,),
        in_specs=[pl.BlockSpec((nb, 2, tm, fb), lambda i: (0, 0, i, 0)), _whole(w_in.shape), _rows(tm, d),
                  _rows(tm, d), _whole((1, d))] + [_TOKEN] * len(extra),
        out_specs=(_rows(tm, d), _rows(tm, d), _whole((8, d)), _whole((8, d))),
        compiler_params=_params())(dh, w_in, r, dy_b, ln_g, *extra)


def _resid_ln_fwd(xin, f, g, b, alpha, scale, name):
    t, d = xin.shape
    tr = _pick(t, 256, 8)

    def body(x_ref, f_ref, g_ref, b_ref, r_ref, y_ref, y16_ref):
        r = alpha * x_ref[...] + scale * f_ref[...]
        xhat, _ = _ln_stats(r)
        y = xhat * g_ref[...] + b_ref[...]
        r_ref[...] = r
        y_ref[...] = y
        y16_ref[...] = y.astype(BF16)

    return pl.pallas_call(
        body, name=name,
        out_shape=(jax.ShapeDtypeStruct((t, d), F32), jax.ShapeDtypeStruct((t, d), F32),
                   jax.ShapeDtypeStruct((t, d), BF16)),
        grid=(t // tr,),
        in_specs=[_rows(tr, d), _rows(tr, d), _whole((1, d)), _whole((1, d))],
        out_specs=(_rows(tr, d), _rows(tr, d), _rows(tr, d)),
        compiler_params=_params())(xin, f, g, b)


def _ln_bwd_call(r, dy_a, dy_b, b_scale, g, out_scale, name):
    t, d = r.shape
    tr = _pick(t, 256, 8)
    has_b = dy_b is not None

    def body(*refs):
        if has_b:
            r_ref, da_ref, db_ref, g_ref, dr_ref, drs_ref, dg_ref, dbeta_ref = refs
            dy = da_ref[...] + b_scale * db_ref[...]
        else:
            r_ref, da_ref, g_ref, dr_ref, drs_ref, dg_ref, dbeta_ref = refs
            dy = da_ref[...]
        xhat, rstd = _ln_stats(r_ref[...])
        dr = _ln_bwd(dy, xhat, rstd, g_ref[...])
        dr_ref[...] = dr
        drs_ref[...] = (out_scale * dr).astype(BF16)

        @pl.when(pl.program_id(0) == 0)
        def _():
            dg_ref[...] = jnp.zeros_like(dg_ref)
            dbeta_ref[...] = jnp.zeros_like(dbeta_ref)

        dg_ref[...] += _fold8(dy * xhat)
        dbeta_ref[...] += _fold8(dy)

    ins = [r, dy_a] + ([dy_b] if has_b else []) + [g]
    in_specs = [_rows(tr, d)] * (3 if has_b else 2) + [_whole((1, d))]
    return pl.pallas_call(
        body, name=name,
        out_shape=(jax.ShapeDtypeStruct((t, d), F32), jax.ShapeDtypeStruct((t, d), BF16),
                   jax.ShapeDtypeStruct((8, d), F32), jax.ShapeDtypeStruct((8, d), F32)),
        grid=(t // tr,), in_specs=in_specs,
        out_specs=(_rows(tr, d), _rows(tr, d), _whole((8, d)), _whole((8, d))),
        compiler_params=_params())(*ins)


def _take_row(v, row_id, s):
    return jnp.sum(jnp.where(row_id == s, v, 0.0), axis=0, keepdims=True)


def _complex_power(ar, ai, n):
    out = None
    while n:
        if n & 1:
            out = (ar, ai) if out is None else (out[0] * ar - out[1] * ai, out[0] * ai + out[1] * ar)
        ar, ai = ar * ar - ai * ai, 2.0 * ar * ai
        n >>= 1
    return out


def _seg_carries(f_re, f_im, p_re, p_im, reverse):
    row_id = lax.broadcasted_iota(jnp.int32, f_re.shape, 0)
    p_re, p_im = _take_row(p_re, row_id, 0), _take_row(p_im, row_id, 0)
    out_re, out_im = jnp.zeros_like(f_re), jnp.zeros_like(f_im)
    c_re, c_im = jnp.zeros_like(p_re), jnp.zeros_like(p_im)
    order = range(N_SEG - 1, -1, -1) if reverse else range(N_SEG)
    for s in order:
        out_re = jnp.where(row_id == s, c_re, out_re)
        out_im = jnp.where(row_id == s, c_im, out_im)
        fr, fi = _take_row(f_re, row_id, s), _take_row(f_im, row_id, s)
        c_re, c_im = fr + p_re * c_re - p_im * c_im, fi + p_re * c_im + p_im * c_re
    return out_re, out_im


def _s5_fwd(za16, b_re, b_im, a_re, a_im, c_re, c_im, n_seq, name):
    t = za16.shape[0]
    n_sg, ch, st = b_re.shape
    seq = t // n_seq
    steps = seq // N_SEG

    def body(za_ref, bre_ref, bim_ref, are, aim, cre_ref, cim_ref, hre, him, y_ref):
        za = za_ref[...]
        hre[...] = _dot(za, bre_ref[0])
        him[...] = _dot(za, bim_ref[0])
        ar = jnp.broadcast_to(are[...], (N_SEG, st))
        ai = jnp.broadcast_to(aim[...], (N_SEG, st))
        zero = jnp.zeros((N_SEG, st), F32)

        def local(k, carry):
            lr, li = carry
            rows = pl.ds(pl.multiple_of(k * N_SEG, N_SEG), N_SEG)
            nr = ar * lr - ai * li + hre[rows, :]
            ni = ar * li + ai * lr + him[rows, :]
            hre[rows, :] = nr
            him[rows, :] = ni
            return nr, ni

        f_re, f_im = lax.fori_loop(0, steps, local, (zero, zero))
        c_re, c_im = _seg_carries(f_re, f_im, *_complex_power(ar, ai, steps), reverse=False)

        def fix(k, carry):
            qr, qi = carry
            rows = pl.ds(pl.multiple_of(k * N_SEG, N_SEG), N_SEG)
            hre[rows, :] = hre[rows, :] + qr
            him[rows, :] = him[rows, :] + qi
            return ar * qr - ai * qi, ar * qi + ai * qr

        lax.fori_loop(0, steps, fix, (ar * c_re - ai * c_im, ar * c_im + ai * c_re))
        y_ref[...] = _dot(hre[...], cre_ref[0]) + _dot(him[...], cim_ref[0])

    rows_ch = pl.BlockSpec((seq, ch), lambda s, j: (s, j))
    rows_st = pl.BlockSpec((seq, st), lambda s, j: (s, j))
    vec = pl.BlockSpec((1, st), lambda s, j: (0, j))
    b_blk = pl.BlockSpec((1, ch, st), lambda s, j: (j, 0, 0))
    c_blk = pl.BlockSpec((1, st, ch), lambda s, j: (j, 0, 0))
    return pl.pallas_call(
        body, name=name,
        out_shape=(jax.ShapeDtypeStruct((t, n_sg * st), F32), jax.ShapeDtypeStruct((t, n_sg * st), F32),
                   jax.ShapeDtypeStruct((t, n_sg * ch), F32)),
        grid=(n_seq, n_sg), in_specs=[rows_ch, b_blk, b_blk, vec, vec, c_blk, c_blk],
        out_specs=(rows_st, rows_st, rows_ch),
        compiler_params=_params(2))(za16, b_re, b_im, a_re, a_im, c_re, c_im)


def _s5_bwd(dy16, dza_skip, h_re, h_im, za16, b_re, b_im, a_re, a_im, c_re, c_im, n_seq, name):
    t = dy16.shape[0]
    n_sg, ch, st = b_re.shape
    seq = t // n_seq
    steps = seq // N_SEG

    def body(dy_ref, skip_ref, hre, him, za_ref, bre_ref, bim_ref, are, aim, cre_ref, cim_ref,
             dza_ref, dbre_ref, dbim_ref, dcre_ref, dcim_ref, dare, daim, lre, lim):
        dy = dy_ref[...]
        lre[...] = _dot(dy, cre_ref[0], tb=True)
        lim[...] = _dot(dy, cim_ref[0], tb=True)
        ar = jnp.broadcast_to(are[...], (N_SEG, st))
        ai = -jnp.broadcast_to(aim[...], (N_SEG, st))
        zero = jnp.zeros((N_SEG, st), F32)

        def local(i, carry):
            lr, li = carry
            k = steps - 1 - i
            rows = pl.ds(pl.multiple_of(k * N_SEG, N_SEG), N_SEG)
            nr = ar * lr - ai * li + lre[rows, :]
            ni = ar * li + ai * lr + lim[rows, :]
            lre[rows, :] = nr
            lim[rows, :] = ni
            return nr, ni

        f_re, f_im = lax.fori_loop(0, steps, local, (zero, zero))
        c_re, c_im = _seg_carries(f_re, f_im, *_complex_power(ar, ai, steps), reverse=True)

        def accumulate(lam_r, lam_i, hp_r, hp_i, acc_r, acc_i):
            return acc_r + (lam_r * hp_r + lam_i * hp_i), acc_i + (lam_i * hp_r - lam_r * hp_i)

        def fix(i, carry):
            qr, qi, acc_r, acc_i = carry
            k = steps - 1 - i
            rows = pl.ds(pl.multiple_of(k * N_SEG, N_SEG), N_SEG)
            prev = pl.ds(pl.multiple_of((k - 1) * N_SEG, N_SEG), N_SEG)
            lam_r = lre[rows, :] + qr
            lam_i = lim[rows, :] + qi
            lre[rows, :] = lam_r
            lim[rows, :] = lam_i
            acc_r, acc_i = accumulate(lam_r, lam_i, hre[prev, :], him[prev, :], acc_r, acc_i)
            return ar * qr - ai * qi, ar * qi + ai * qr, acc_r, acc_i

        qr, qi, acc_r, acc_i = lax.fori_loop(
            0, steps - 1, fix, (ar * c_re - ai * c_im, ar * c_im + ai * c_re, zero, zero))
        first = pl.ds(0, N_SEG)
        last = pl.ds((steps - 1) * N_SEG, N_SEG)
        lam_r = lre[first, :] + qr
        lam_i = lim[first, :] + qi
        lre[first, :] = lam_r
        lim[first, :] = lam_i
        row_id = lax.broadcasted_iota(jnp.int32, (N_SEG, st), 0)
        hp_r = jnp.where(row_id == 0, 0.0, pltpu.roll(hre[last, :], 1, 0))
        hp_i = jnp.where(row_id == 0, 0.0, pltpu.roll(him[last, :], 1, 0))
        acc_r, acc_i = accumulate(lam_r, lam_i, hp_r, hp_i, acc_r, acc_i)

        lam_re16 = lre[...].astype(BF16)
        lam_im16 = lim[...].astype(BF16)
        dza_ref[...] = (_dot(lam_re16, bre_ref[0], tb=True) + _dot(lam_im16, bim_ref[0], tb=True)
                        + skip_ref[...]).astype(BF16)

        @pl.when(pl.program_id(1) == 0)
        def _():
            for ref in (dbre_ref, dbim_ref, dcre_ref, dcim_ref, dare, daim):
                ref[...] = jnp.zeros_like(ref)

        za = za_ref[...]
        dbre_ref[0] += _dot(za, lam_re16, ta=True)
        dbim_ref[0] += _dot(za, lam_im16, ta=True)
        dcre_ref[0] += _dot(hre[...], dy, ta=True)
        dcim_ref[0] += _dot(him[...], dy, ta=True)
        dare[...] += acc_r
        daim[...] += acc_i

    rows_ch = pl.BlockSpec((seq, ch), lambda j, s: (s, j))
    rows_st = pl.BlockSpec((seq, st), lambda j, s: (s, j))
    vec = pl.BlockSpec((1, st), lambda j, s: (0, j))
    b_blk = pl.BlockSpec((1, ch, st), lambda j, s: (j, 0, 0))
    c_blk = pl.BlockSpec((1, st, ch), lambda j, s: (j, 0, 0))
    acc = pl.BlockSpec((N_SEG, st), lambda j, s: (0, j))
    return pl.pallas_call(
        body, name=name,
        out_shape=(jax.ShapeDtypeStruct((t, n_sg * ch), BF16),
                   jax.ShapeDtypeStruct((n_sg, ch, st), F32), jax.ShapeDtypeStruct((n_sg, ch, st), F32),
                   jax.ShapeDtypeStruct((n_sg, st, ch), F32), jax.ShapeDtypeStruct((n_sg, st, ch), F32),
                   jax.ShapeDtypeStruct((N_SEG, n_sg * st), F32), jax.ShapeDtypeStruct((N_SEG, n_sg * st), F32)),
        grid=(n_sg, n_seq),
        in_specs=[rows_ch, rows_ch, rows_st, rows_st, rows_ch, b_blk, b_blk, vec, vec, c_blk, c_blk],
        out_specs=(rows_ch, b_blk, b_blk, c_blk, c_blk, acc, acc),
        scratch_shapes=[pltpu.VMEM((seq, st), F32), pltpu.VMEM((seq, st), F32)],
        compiler_params=_params(2))(dy16, dza_skip, h_re, h_im, za16, b_re, b_im, a_re, a_im, c_re, c_im)


def _s5_post_fwd(yssm, u, d_skip, glu_w, glu_b, name):
    t, w = yssm.shape
    tr = _pick(t, 512, 8)

    def body(y_ref, u_ref, d_ref, w_ref, b_ref, o_ref):
        yg = _gelu(y_ref[...] + d_ref[...] * u_ref[...])
        pre = _dot(yg, w_ref[...]) + b_ref[...]
        o_ref[...] = yg * _sigmoid(pre)

    return pl.pallas_call(
        body, name=name, out_shape=jax.ShapeDtypeStruct((t, w), F32), grid=(t // tr,),
        in_specs=[_rows(tr, w), _rows(tr, w), _whole((1, w)), _whole((w, w)), _whole((1, w))],
        out_specs=_rows(tr, w), compiler_params=_params())(yssm, u, d_skip, glu_w, glu_b)


def _s5_post_bwd(yssm, u, dout, d_skip, glu_w, glu_b, name):
    t, w = yssm.shape
    tr = _pick(t, 512, 8)

    def body(y_ref, u_ref, do_ref, d_ref, w_ref, b_ref, dy_ref, du_ref, dw_ref, dd_ref, db_ref):
        uu = u_ref[...]
        y = y_ref[...] + d_ref[...] * uu
        yg = _gelu(y)
        sg = _sigmoid(_dot(yg, w_ref[...]) + b_ref[...])
        do = do_ref[...]
        dpre = do * yg * sg * (1.0 - sg)
        dyg = do * sg + _dot(dpre, w_ref[...], tb=True)
        dy = dyg * _gelu_grad(y)
        dy_ref[...] = dy.astype(BF16)
        du_ref[...] = dy * d_ref[...]

        @pl.when(pl.program_id(0) == 0)
        def _():
            dw_ref[...] = jnp.zeros_like(dw_ref)
            dd_ref[...] = jnp.zeros_like(dd_ref)
            db_ref[...] = jnp.zeros_like(db_ref)

        dw_ref[...] += _dot(yg, dpre, ta=True)
        dd_ref[...] += _fold8(dy * uu)
        db_ref[...] += _fold8(dpre)

    return pl.pallas_call(
        body, name=name,
        out_shape=(jax.ShapeDtypeStruct((t, w), BF16), jax.ShapeDtypeStruct((t, w), F32),
                   jax.ShapeDtypeStruct((w, w), F32), jax.ShapeDtypeStruct((8, w), F32),
                   jax.ShapeDtypeStruct((8, w), F32)),
        grid=(t // tr,),
        in_specs=[_rows(tr, w), _rows(tr, w), _rows(tr, w), _whole((1, w)), _whole((w, w)),
                  _whole((1, w))],
        out_specs=(_rows(tr, w), _rows(tr, w), _whole((w, w)), _whole((8, w)), _whole((8, w))),
        compiler_params=_params())(yssm, u, dout, d_skip, glu_w, glu_b)


def _head_select(parts, head_dim):
    col_head = lax.broadcasted_iota(jnp.int32, parts[0].shape, 1) // head_dim
    out = jnp.zeros_like(parts[0])
    for h, part in enumerate(parts):
        out = jnp.where(col_head == h, part, out)
    return out


def _gmlp_fwd(proj, ln_g, ln_b, ws, bs_cols, name):
    t = proj.shape[0]
    n_heads = ws.shape[0]
    w = bs_cols.shape[1]
    head_dim = w // n_heads
    cpb = _pick(t // CHUNK, 4, 1)
    tr = cpb * CHUNK

    def body(zu_ref, zv_ref, g_ref, b_ref, ws_ref, bs_ref, o_ref):
        for c in range(cpb):
            rows = pl.ds(c * CHUNK, CHUNK)
            xhat, _ = _ln_stats(_gelu(zv_ref[rows, :]))
            vn = (xhat * g_ref[...] + b_ref[...]).astype(BF16)
            s = _head_select([_dot(ws_ref[h], vn) for h in range(n_heads)], head_dim) + bs_ref[...]
            o_ref[rows, :] = _gelu(zu_ref[rows, :]) * s

    return pl.pallas_call(
        body, name=name, out_shape=jax.ShapeDtypeStruct((t, w), F32), grid=(t // tr,),
        in_specs=[_rows(tr, w, 1), _rows(tr, w, 2), _whole((1, w)), _whole((1, w)),
                  _whole(ws.shape), _whole(bs_cols.shape)],
        out_specs=_rows(tr, w), compiler_params=_params())(proj, proj, ln_g, ln_b, ws, bs_cols)


def _gmlp_bwd(proj, dout, ln_g, ln_b, ws, ws_t, bs_cols, name):
    t = proj.shape[0]
    n_heads = ws.shape[0]
    w = bs_cols.shape[1]
    head_dim = w // n_heads
    cpb = _pick(t // CHUNK, 4, 1)
    tr = cpb * CHUNK

    def body(zu_ref, zv_ref, do_ref, g_ref, b_ref, ws_ref, wst_ref, bs_ref,
             dzu_ref, dzv_ref, dws_ref, dbs_ref, dg_ref, dbeta_ref):
        @pl.when(pl.program_id(0) == 0)
        def _():
            dws_ref[...] = jnp.zeros_like(dws_ref)
            dbs_ref[...] = jnp.zeros_like(dbs_ref)
            dg_ref[...] = jnp.zeros_like(dg_ref)
            dbeta_ref[...] = jnp.zeros_like(dbeta_ref)

        col_head = lax.broadcasted_iota(jnp.int32, (CHUNK, w), 1) // head_dim
        for c in range(cpb):
            rows = pl.ds(c * CHUNK, CHUNK)
            zu, zv, do = zu_ref[rows, :], zv_ref[rows, :], do_ref[rows, :]
            xhat, rstd = _ln_stats(_gelu(zv))
            vn = (xhat * g_ref[...] + b_ref[...]).astype(BF16)
            s = _head_select([_dot(ws_ref[h], vn) for h in range(n_heads)], head_dim) + bs_ref[...]
            dzu_ref[rows, :] = (do * s * _gelu_grad(zu)).astype(BF16)
            ds = do * _gelu(zu)
            ds16 = ds.astype(BF16)
            dbs_ref[...] += ds
            for h in range(n_heads):
                dws_ref[h] += _dot(jnp.where(col_head == h, ds16, jnp.zeros_like(ds16)), vn, tb=True)
            dvn = _head_select([_dot(wst_ref[h], ds16) for h in range(n_heads)], head_dim)
            dg_ref[...] += _fold8(dvn * xhat)
            dbeta_ref[...] += _fold8(dvn)
            dgv = _ln_bwd(dvn, xhat, rstd, g_ref[...])
            dzv_ref[rows, :] = (dgv * _gelu_grad(zv)).astype(BF16)

    return pl.pallas_call(
        body, name=name,
        out_shape=(jax.ShapeDtypeStruct((t, w), BF16), jax.ShapeDtypeStruct((t, w), BF16),
                   jax.ShapeDtypeStruct(ws.shape, F32), jax.ShapeDtypeStruct((CHUNK, w), F32),
                   jax.ShapeDtypeStruct((8, w), F32), jax.ShapeDtypeStruct((8, w), F32)),
        grid=(t // tr,),
        in_specs=[_rows(tr, w, 1), _rows(tr, w, 2), _rows(tr, w), _whole((1, w)), _whole((1, w)),
                  _whole(ws.shape), _whole(ws.shape), _whole(bs_cols.shape)],
        out_specs=(_rows(tr, w), _rows(tr, w), _whole(ws.shape), _whole((CHUNK, w)),
                   _whole((8, w)), _whole((8, w))),
        compiler_params=_params())(proj, proj, dout, ln_g, ln_b, ws, ws_t, bs_cols)


def _merge_fwd(s5out, gm, proj, up_a, up_b, name):
    t, w = s5out.shape
    d = up_a.shape[1]
    assert d == 2 * w
    tr = _pick(t, 256, 8)

    def body(s_ref, gm_ref, ga0, ga1, gb0, gb1, ua_ref, ub_ref, m_ref, ya_ref, yb_ref):
        ya = _dot(s_ref[...], ua_ref[...])
        yb = _dot(gm_ref[...], ub_ref[...])
        ya_ref[...] = ya
        yb_ref[...] = yb
        for half, (ga, gb) in enumerate(((ga0, gb0), (ga1, gb1))):
            cols = slice(half * w, (half + 1) * w)
            m_ref[:, cols] = (_sigmoid(ga[...]) * ya[:, cols] + _sigmoid(gb[...]) * yb[:, cols]).astype(BF16)

    return pl.pallas_call(
        body, name=name,
        out_shape=(jax.ShapeDtypeStruct((t, d), BF16), jax.ShapeDtypeStruct((t, d), F32),
                   jax.ShapeDtypeStruct((t, d), F32)),
        grid=(t // tr,),
        in_specs=[_rows(tr, w), _rows(tr, w), _rows(tr, w, 3), _rows(tr, w, 4), _rows(tr, w, 5),
                  _rows(tr, w, 6), _whole(up_a.shape), _whole(up_b.shape)],
        out_specs=(_rows(tr, d), _rows(tr, d), _rows(tr, d)),
        compiler_params=_params())(s5out, gm, proj, proj, proj, proj, up_a, up_b)


def _merge_bwd(dm, ya, yb, s5out, gm, proj, up_a, up_b, name):
    t, d = dm.shape
    w = d // 2
    tr = _pick(t, 256, 8)

    def body(dm_ref, ya_ref, yb_ref, s_ref, gm_ref, ga0, ga1, gb0, gb1, ua_ref, ub_ref,
             dga_ref, dgb_ref, ds_ref, dgm_ref, dua_ref, dub_ref):
        dm_v, ya, yb = dm_ref[...], ya_ref[...], yb_ref[...]
        dya, dyb = [], []
        for half, (ga, gb) in enumerate(((ga0, gb0), (ga1, gb1))):
            cols = slice(half * w, (half + 1) * w)
            sa, sb = _sigmoid(ga[...]), _sigmoid(gb[...])
            dmh = dm_v[:, cols]
            dga_ref[:, cols] = (dmh * ya[:, cols] * sa * (1.0 - sa)).astype(BF16)
            dgb_ref[:, cols] = (dmh * yb[:, cols] * sb * (1.0 - sb)).astype(BF16)
            dya.append((dmh * sa).astype(BF16))
            dyb.append((dmh * sb).astype(BF16))
        dya = jnp.concatenate(dya, axis=1)
        dyb = jnp.concatenate(dyb, axis=1)
        ds_ref[...] = _dot(dya, ua_ref[...], tb=True)
        dgm_ref[...] = _dot(dyb, ub_ref[...], tb=True)

        @pl.when(pl.program_id(0) == 0)
        def _():
            dua_ref[...] = jnp.zeros_like(dua_ref)
            dub_ref[...] = jnp.zeros_like(dub_ref)

        dua_ref[...] += _dot(s_ref[...], dya, ta=True)
        dub_ref[...] += _dot(gm_ref[...], dyb, ta=True)

    return pl.pallas_call(
        body, name=name,
        out_shape=(jax.ShapeDtypeStruct((t, d), BF16), jax.ShapeDtypeStruct((t, d), BF16),
                   jax.ShapeDtypeStruct((t, w), F32), jax.ShapeDtypeStruct((t, w), F32),
                   jax.ShapeDtypeStruct(up_a.shape, F32), jax.ShapeDtypeStruct(up_b.shape, F32)),
        grid=(t // tr,),
        in_specs=[_rows(tr, d), _rows(tr, d), _rows(tr, d), _rows(tr, w), _rows(tr, w),
                  _rows(tr, w, 3), _rows(tr, w, 4), _rows(tr, w, 5), _rows(tr, w, 6),
                  _whole(up_a.shape), _whole(up_b.shape)],
        out_specs=(_rows(tr, d), _rows(tr, d), _rows(tr, w), _rows(tr, w), _whole(up_a.shape),
                   _whole(up_b.shape)),
        compiler_params=_params())(dm, ya, yb, s5out, gm, proj, proj, proj, proj, up_a, up_b)


def _head(x3, r3, ln_g, p, target, w_gate, w_proj, out_scale, name):
    t, d = x3.shape
    pd = p.shape[1]
    tr = _pick(t, 256, 8)
    nb = t // tr

    def body(x_ref, r_ref, g_ref, p_ref, t_ref, wg_ref, wp_ref,
             loss_ref, dr_ref, drs_ref, dwg_ref, dwp_ref, dg_ref, dbeta_ref):
        xv = x_ref[...]
        sg = _sigmoid(_dot(xv, wg_ref[...]))
        pp = _dot(p_ref[...], wp_ref[...])
        err = xv + sg * pp - t_ref[...]
        loss_ref[...] = jnp.full((8, 128), 0.5 * jnp.sum(jnp.mean(err * err, axis=-1)), F32)
        dout = err * (1.0 / d)
        dgpre = dout * pp * sg * (1.0 - sg)
        dpp = dout * sg
        dx = dout + _dot(dgpre, wg_ref[...], tb=True)
        xhat, rstd = _ln_stats(r_ref[...])
        dr = _ln_bwd(dx, xhat, rstd, g_ref[...])
        dr_ref[...] = dr
        drs_ref[...] = (out_scale * dr).astype(BF16)

        @pl.when(pl.program_id(0) == 0)
        def _():
            for ref in (dwg_ref, dwp_ref, dg_ref, dbeta_ref):
                ref[...] = jnp.zeros_like(ref)

        dwg_ref[...] += _dot(xv, dgpre, ta=True)
        dwp_ref[...] += _dot(p_ref[...], dpp, ta=True)
        dg_ref[...] += _fold8(dx * xhat)
        dbeta_ref[...] += _fold8(dx)

    return pl.pallas_call(
        body, name=name,
        out_shape=(jax.ShapeDtypeStruct((nb * 8, 128), F32), jax.ShapeDtypeStruct((t, d), F32),
                   jax.ShapeDtypeStruct((t, d), BF16), jax.ShapeDtypeStruct((d, d), F32),
                   jax.ShapeDtypeStruct((pd, d), F32), jax.ShapeDtypeStruct((8, d), F32),
                   jax.ShapeDtypeStruct((8, d), F32)),
        grid=(nb,),
        in_specs=[_rows(tr, d), _rows(tr, d), _whole((1, d)), _rows(tr, pd), _rows(tr, d), _whole((d, d)),
                  _whole((pd, d))],
        out_specs=(pl.BlockSpec((8, 128), lambda i: (i, 0)), _rows(tr, d), _rows(tr, d), _whole((d, d)),
                   _whole((pd, d)), _whole((8, d)), _whole((8, d))),
        compiler_params=_params())(x3, r3, ln_g, p, target, w_gate, w_proj)


_HBM = pl.BlockSpec(memory_space=pltpu.HBM)


def _all_gather(shards, name):
    n = len(shards)

    def body(*refs):
        x_refs, out_refs = refs[:n], refs[n:2 * n]
        send_sems, recv_sems, local_sems = refs[2 * n:]
        x, y, c = lax.axis_index("x"), lax.axis_index("y"), lax.axis_index("c")
        me, sibling = (x, y, c), (x, y, 1 - c)
        chips = [(1 - x, y), (x, 1 - y), (1 - x, 1 - y)]

        def copy(a, k, block, to, own=False):
            slot = out_refs[a].at[4 * block[0] + 2 * block[1] + block[2]]
            return pltpu.make_async_remote_copy(
                src_ref=x_refs[a] if own else slot, dst_ref=slot,
                send_sem=send_sems.at[7 * a + k], recv_sem=recv_sems.at[7 * a + k],
                device_id=to, device_id_type=pl.DeviceIdType.MESH)

        mine = [pltpu.make_async_copy(x_refs[a], out_refs[a].at[4 * x + 2 * y + c], local_sems.at[a])
                for a in range(n)]
        sends = []
        for a in range(n):
            mine[a].start()
            first = [copy(a, 0, me, sibling, own=True)]
            first += [copy(a, 1 + j, me, (*chip, c), own=True) for j, chip in enumerate(chips)]
            for cp in first:
                cp.start()
            sends += first
        for a in range(n):
            for j, chip in enumerate(chips):
                copy(a, 1 + j, (*chip, c), me).wait_recv()
                passed = copy(a, 4 + j, (*chip, c), sibling)
                passed.start()
                sends.append(passed)
        for a in range(n):
            copy(a, 0, sibling, me).wait_recv()
            for j, chip in enumerate(chips):
                copy(a, 4 + j, (*chip, 1 - c), me).wait_recv()
        for cp in sends:
            cp.wait_send()
        for cp in mine:
            cp.wait()

    return pl.pallas_call(
        body, name=name,
        out_shape=tuple(jax.ShapeDtypeStruct((N_DEV,) + s.shape, s.dtype) for s in shards),
        in_specs=[_HBM] * n, out_specs=tuple([_HBM] * n),
        scratch_shapes=[pltpu.SemaphoreType.DMA((7 * n,)), pltpu.SemaphoreType.DMA((7 * n,)),
                        pltpu.SemaphoreType.DMA((n,))],
    )(*shards)


_SEM = pl.BlockSpec(memory_space=pltpu.SEMAPHORE)
_ANY = pl.BlockSpec(memory_space=pl.ANY)
_DATAFLOW = pltpu.SideEffectType.DATAFLOW_SIDE_EFFECTING


def _peer(k, x, y, c):
    return (1 - x if k & 4 else x, 1 - y if k & 2 else y, 1 - c if k & 1 else c)


def _exchange_start(sends, after, name):
    n = len(sends)
    me = 4 * lax.axis_index("x") + 2 * lax.axis_index("y") + lax.axis_index("c")
    lands = []
    for s in sends:
        own = s[0] if s.shape[0] == 1 else lax.dynamic_index_in_dim(s, me, 0, keepdims=False)
        land = lax.empty((N_DEV,) + s.shape[1:], s.dtype)
        lands.append(lax.dynamic_update_index_in_dim(land, own, me, 0))

    def body(*refs):
        s_refs, l_refs = refs[:n], refs[n:2 * n]
        send_sems, recv_sems, token = refs[2 * n + 1], refs[2 * n + 2], refs[-1]
        x, y, c = lax.axis_index("x"), lax.axis_index("y"), lax.axis_index("c")
        for a in range(n):
            same = sends[a].shape[0] == 1
            for k in range(1, N_DEV):
                px, py, pc = _peer(k, x, y, c)
                pltpu.make_async_remote_copy(
                    src_ref=s_refs[a].at[0 if same else 4 * px + 2 * py + pc],
                    dst_ref=l_refs[a].at[4 * x + 2 * y + c],
                    send_sem=send_sems.at[7 * a + k - 1], recv_sem=recv_sems.at[7 * a + k - 1],
                    device_id=(px, py, pc), device_id_type=pl.DeviceIdType.MESH).start()
        token[...] = jnp.zeros_like(token)

    outs = pl.pallas_call(
        body, name=name,
        out_shape=(pltpu.SemaphoreType.DMA((7 * n,)), pltpu.SemaphoreType.DMA((7 * n,)),
                   *[pltpu.HBM(s.shape, s.dtype) for s in sends],
                   *[pltpu.HBM(l.shape, l.dtype) for l in lands],
                   jax.ShapeDtypeStruct((8, 128), F32)),
        in_specs=[_HBM] * (2 * n) + [_ANY],
        out_specs=(_SEM, _SEM, *([_HBM] * (2 * n)), pl.BlockSpec(memory_space=pltpu.VMEM)),
        input_output_aliases={i: 2 + i for i in range(2 * n)},
        compiler_params=pltpu.CompilerParams(has_side_effects=_DATAFLOW),
    )(*[pltpu.with_memory_space_constraint(v, pltpu.HBM) for v in list(sends) + lands], after)
    return (outs[0], outs[1], list(outs[2:2 + n]), list(outs[2 + n:2 + 2 * n])), outs[-1]


def _exchange_wait(handle, after, name):
    send_sems, recv_sems, sends, lands = handle
    n = len(sends)

    def body(*refs):
        s_refs, l_refs = refs[:n], refs[n:2 * n]
        send_sems, recv_sems = refs[2 * n], refs[2 * n + 1]
        x, y, c = lax.axis_index("x"), lax.axis_index("y"), lax.axis_index("c")
        for a in range(n):
            for k in range(1, N_DEV):
                copy = pltpu.make_async_remote_copy(
                    src_ref=s_refs[a].at[0], dst_ref=l_refs[a].at[0],
                    send_sem=send_sems.at[7 * a + k - 1], recv_sem=recv_sems.at[7 * a + k - 1],
                    device_id=_peer(k, x, y, c), device_id_type=pl.DeviceIdType.MESH)
                copy.wait_send()
                copy.wait_recv()

    outs = pl.pallas_call(
        body, name=name,
        out_shape=(*[pltpu.HBM(s.shape, s.dtype) for s in sends], *[pltpu.HBM(l.shape, l.dtype) for l in lands]),
        in_specs=[_HBM] * (2 * n) + [_SEM, _SEM] + [_ANY] * len(after),
        out_specs=tuple([_HBM] * (2 * n)),
        input_output_aliases={i: i for i in range(2 * n)},
        compiler_params=pltpu.CompilerParams(has_side_effects=_DATAFLOW),
    )(*sends, *lands, send_sems, recv_sems, *after)
    return list(outs[n:])


def _adamw(recv, w, m, v, name):
    n, rows, width = recv.shape
    tr = _pick(rows, max(8, ADAM_BLOCK_BYTES // (n * width * recv.dtype.itemsize)), 8)
    c_m = 1.0 - ADAM_B1 ** ADAM_STEP
    c_v = 1.0 - ADAM_B2 ** ADAM_STEP

    def body(r_ref, w_ref, m_ref, v_ref, g_ref, d_ref, nm_ref, nv_ref):
        g = r_ref[0].astype(F32)
        for i in range(1, n):
            g = g + r_ref[i].astype(F32)
        m2 = ADAM_B1 * m_ref[...] + (1.0 - ADAM_B1) * g
        v2 = ADAM_B2 * v_ref[...] + (1.0 - ADAM_B2) * (g * g)
        m_hat = m2 / c_m
        v_hat = v2 / c_v
        g_ref[...] = g
        d_ref[...] = -ADAM_LR * (m_hat / (jnp.sqrt(v_hat) + ADAM_EPS) + ADAM_WD * w_ref[...])
        nm_ref[...] = m2
        nv_ref[...] = v2

    blk = _rows(tr, width)
    shp = jax.ShapeDtypeStruct((rows, width), F32)
    return pl.pallas_call(
        body, name=name, out_shape=(shp, shp, shp, shp), grid=(rows // tr,),
        in_specs=[pl.BlockSpec((n, tr, width), lambda i: (0, i, 0)), blk, blk, blk],
        out_specs=(blk, blk, blk, blk), compiler_params=_params())(recv, w, m, v)


def _join_cols(gathered):
    n, r, c = gathered.shape
    return gathered.transpose(1, 0, 2).reshape(r, n * c)


def _split_cols(full):
    r, c = full.shape
    return full.reshape(r, N_DEV, c // N_DEV).transpose(1, 0, 2)


def _adamw_small(items, name):
    n = len(items)
    c_m = 1.0 - ADAM_B1 ** ADAM_STEP
    c_v = 1.0 - ADAM_B2 ** ADAM_STEP

    def body(*refs):
        ins, outs = refs[:4 * n], refs[4 * n:]
        for k in range(n):
            r_ref, w_ref, m_ref, v_ref = ins[4 * k:4 * k + 4]
            g = r_ref[0].astype(F32)
            for i in range(1, N_DEV):
                g = g + r_ref[i].astype(F32)
            m2 = ADAM_B1 * m_ref[...] + (1.0 - ADAM_B1) * g
            v2 = ADAM_B2 * v_ref[...] + (1.0 - ADAM_B2) * (g * g)
            outs[4 * k][...] = g
            outs[4 * k + 1][...] = -ADAM_LR * ((m2 / c_m) / (jnp.sqrt(v2 / c_v) + ADAM_EPS) + ADAM_WD * w_ref[...])
            outs[4 * k + 2][...] = m2
            outs[4 * k + 3][...] = v2

    vmem = pl.BlockSpec(memory_space=pltpu.VMEM)
    flat = pl.pallas_call(
        body, name=name,
        out_shape=tuple(jax.ShapeDtypeStruct(w.shape, F32) for _, w, _, _ in items for _ in range(4)),
        in_specs=[vmem] * (4 * n), out_specs=tuple([vmem] * (4 * n)),
        compiler_params=pltpu.CompilerParams(vmem_limit_bytes=VMEM_LIMIT_BYTES),
    )(*[a for item in items for a in item])
    return [tuple(flat[4 * k:4 * k + 4]) for k in range(n)]


def _discretise(lam_re, lam_im, log_dt, b_re, b_im):
    dt = jnp.exp(log_dt)[:, None]
    mag = jnp.exp(lam_re * dt)
    ab_re = mag * jnp.cos(lam_im * dt)
    ab_im = mag * jnp.sin(lam_im * dt)
    nr = ab_re - 1.0
    ni = ab_im
    den = lam_re * lam_re + lam_im * lam_im
    coef_re = ((nr * lam_re + ni * lam_im) / den)[..., None]
    coef_im = ((ni * lam_re - nr * lam_im) / den)[..., None]
    bb_re = coef_re * b_re - coef_im * b_im
    bb_im = coef_re * b_im + coef_im * b_re
    return ab_re, ab_im, bb_re, bb_im


def _same_group(gl):
    return jnp.eye(gl, dtype=bool)[None, :, None, :, None]


def _super_groups_in(bb, gl):
    g, p, i = bb.shape
    blocks = bb.reshape(g // gl, gl, p, i).transpose(0, 1, 3, 2)[:, :, :, None, :]
    return jnp.where(_same_group(gl), blocks, 0.0).reshape(g // gl, gl * i, gl * p)


def _super_groups_in_take(dense, gl, p, i):
    n_sg = dense.shape[0]
    blocks = jnp.where(_same_group(gl), dense.reshape(n_sg, gl, i, gl, p), 0.0).sum(axis=3)
    return blocks.transpose(0, 1, 3, 2).reshape(n_sg * gl, p, i)


def _super_groups_out(cc, gl):
    g, i, p = cc.shape
    blocks = cc.reshape(g // gl, gl, i, p).transpose(0, 1, 3, 2)[:, :, :, None, :]
    return jnp.where(_same_group(gl), blocks, 0.0).reshape(g // gl, gl * p, gl * i)


def _super_groups_out_take(dense, gl, i, p):
    n_sg = dense.shape[0]
    blocks = jnp.where(_same_group(gl), dense.reshape(n_sg, gl, p, gl, i), 0.0).sum(axis=3)
    return blocks.transpose(0, 1, 3, 2).reshape(n_sg * gl, i, p)


def _perm_rows(z, n_seq):
    t, w = z.shape
    steps = t // n_seq // N_SEG
    return z.reshape(n_seq, N_SEG, steps, w).transpose(0, 2, 1, 3).reshape(t, w)


def _unperm_rows(z, n_seq):
    t, w = z.shape
    steps = t // n_seq // N_SEG
    return z.reshape(n_seq, steps, N_SEG, w).transpose(0, 2, 1, 3).reshape(t, w)


def kernel(*args):
    assert len(args) == len(INPUT_NAMES)
    inp = dict(zip(INPUT_NAMES, args))
    depth = inp["ffn1_w_in"].shape[0]
    assert depth == 1
    alpha = (2.0 * depth) ** 0.25

    n_seq, seq, d_model = inp["x"].shape
    t = n_seq * seq
    x = inp["x"].reshape(t, d_model)
    x16 = x.astype(BF16)
    p16 = inp["p"][0].reshape(t, -1).astype(BF16)
    target = inp["loss_target"].reshape(t, d_model)
    wts = {n: inp[n][0] if n in SHARDED else inp[n] for n in WEIGHTS}
    mom = {n: inp["m_" + n][0] if n in SHARDED else inp["m_" + n] for n in WEIGHTS}
    var = {n: inp["v_" + n][0] if n in SHARDED else inp["v_" + n] for n in WEIGHTS}

    full = {}

    def place(n, g):
        if n in ("ffn1_w_in", "ffn2_w_in"):
            full[n] = g.reshape((2, N_DEV // 2) + g.shape[1:])
        elif n in ("ffn1_w_out", "ffn2_w_out"):
            full[n] = g.reshape(N_DEV // 2, 2 * g.shape[1], g.shape[2])
        elif n in COL_SHARDED:
            full[n] = _join_cols(g)
        else:
            full[n] = g.reshape(N_DEV * g.shape[1], g.shape[2])

    w16 = dict(zip(GATHER_FIRST, _to_bf16([wts[n] for n in GATHER_FIRST], "cast_ffn1")))
    later = GATHER_MIX + GATHER_LAST
    w16.update(zip(later, _to_bf16([wts[n] for n in later], "cast_rest")))
    for n, g in zip(GATHER_FIRST, _all_gather([w16[n] for n in GATHER_FIRST], "gather_ffn1")):
        place(n, g)
    gather_mix, gather_mix_token = _exchange_start([w16[n][None] for n in GATHER_MIX], full["ffn1_w_out"],
                                                   "gather_mix_start")

    def row(v):
        return v.reshape(1, -1)

    _, n_groups, n_state = wts["ssm_lambda_re"].shape
    n_ch = wts["ssm_b_re"].shape[3]
    disc_in = tuple(wts[n][0] for n in ("ssm_lambda_re", "ssm_lambda_im", "ssm_log_dt", "ssm_b_re", "ssm_b_im"))
    (ab_re, ab_im, bb_re, bb_im), disc_vjp = jax.vjp(_discretise, *disc_in)
    a_re, a_im = row(ab_re), row(ab_im)
    gl = S5_CHANNELS_PER_STEP // n_ch
    b_sg_re = _super_groups_in(bb_re, gl).astype(BF16)
    b_sg_im = _super_groups_in(bb_im, gl).astype(BF16)
    c_sg_re = _super_groups_out(wts["ssm_c_re"][0], gl).astype(BF16)
    c_sg_im = _super_groups_out(-wts["ssm_c_im"][0], gl).astype(BF16)

    ws = wts["gmlp_w_s"][0]
    n_heads = ws.shape[0]
    d_gmlp = wts["gmlp_ln_g"].shape[1]
    causal = jnp.tril(jnp.ones((CHUNK, CHUNK), dtype=bool))
    ws_masked = jnp.where(causal[None], ws, 0.0).astype(BF16)
    ws_masked_t = ws_masked.transpose(0, 2, 1)
    bs_cols = jnp.repeat(wts["gmlp_b_s"][0].T, d_gmlp // n_heads, axis=1)
    d_ssm = n_groups * n_ch
    assert d_ssm == d_gmlp and d_model == 2 * d_ssm

    h1, a1 = _ffn_in(x16, full["ffn1_w_in"], "ffn1_in", token=gather_mix_token)
    r1, x1, x1_16 = _ffn_out_ln(a1, full["ffn1_w_out"], x, row(wts["ln1_g"]), row(wts["ln1_b"]), alpha,
                                "ffn1_out_ln")

    for n, g in zip(GATHER_MIX, _exchange_wait(gather_mix, [x1_16], "gather_mix_wait")):
        place(n, g)
    gather_last, token = _exchange_start([w16[n][None] for n in GATHER_LAST], full["mix_w_in"],
                                         "gather_ffn2_start")
    proj = _mm(x1_16, full["mix_w_in"], name="mix_in", token=token)
    za_p = _perm_rows(proj[:, :d_ssm], n_seq)
    za_p16 = za_p.astype(BF16)
    h_re, h_im, yssm = _s5_fwd(za_p16, b_sg_re, b_sg_im, a_re, a_im, c_sg_re, c_sg_im, n_seq, "s5_fwd")
    s5out_p = _s5_post_fwd(yssm, za_p, row(wts["ssm_d"]), full["ssm_glu_w"], row(wts["ssm_glu_b"]),
                           "s5_post")
    s5out = _unperm_rows(s5out_p, n_seq)
    gm = _gmlp_fwd(proj, row(wts["gmlp_ln_g"]), row(wts["gmlp_ln_b"]), ws_masked, bs_cols, "gmlp")
    m_mix, y_a, y_b = _merge_fwd(s5out, gm, proj, full["up_a"], full["up_b"], "merge")
    mixed = _mm(m_mix, full["mix_w_out"], name="mix_out")
    r2, x2, x2_16 = _resid_ln_fwd(x1, mixed, row(wts["ln2_g"]), row(wts["ln2_b"]), alpha, 1.0, "ln2")

    for n, g in zip(GATHER_LAST, _exchange_wait(gather_last, [x2_16], "gather_ffn2_wait")):
        place(n, g)
    h2, a2 = _ffn_in(x2_16, full["ffn2_w_in"], "ffn2_in")
    r3, x3, _ = _ffn_out_ln(a2, full["ffn2_w_out"], x2, row(wts["ln3_g"]), row(wts["ln3_b"]), alpha,
                            "ffn2_out_ln")

    small = {}
    send = {}

    def lane_dense(n, v):
        return v.reshape(v.shape[:2] + (-1,)) if n in ("ssm_b_re", "ssm_b_im") else v

    def on_wire(n, g):
        g = lane_dense(n, g)
        return (g.astype(BF16) if n in SMALL_BF16 else g)[None]
    loss_parts, dr3, dr3_h, dw_gate, dw_proj, dg, db = _head(
        x3, r3, row(wts["ln3_g"]), p16, target, full["ple_w_gate"], full["ple_w_proj"], 0.5, "head")
    loss = lax.psum(jnp.sum(loss_parts[::8, 0]), ("x", "y", "c"))
    send["ple_w_gate"] = dw_gate.astype(BF16).reshape(N_DEV, -1, d_model)
    send["ple_w_proj"] = _split_cols(dw_proj.astype(BF16))
    small["ln3_g"], small["ln3_b"] = dg.sum(0, keepdims=True), db.sum(0, keepdims=True)
    dh2 = _ffn_out_dx(dr3_h, full["ffn2_w_out"], h2, "ffn2_out_dx")
    dw = _ffn_out_dw(a2, dr3_h, "ffn2_out_dw")
    send["ffn2_w_out"] = dw.reshape(N_DEV, -1, d_model)
    dw = _ffn_in_dw(x2_16, dh2, "ffn2_in_dw")
    send["ffn2_w_in"] = dw.reshape((N_DEV,) + dw.shape[2:])
    group1 = ("ffn2_w_in", "ffn2_w_out", "ple_w_gate", "ple_w_proj")
    exchange1, token = _exchange_start([send[n] for n in group1] + [on_wire(n, small[n]) for n in SMALL_HEAD], dh2,
                                       "grad_exchange1_start")
    dr2, dr2_h, dg, db = _ffn_in_dx_ln(dh2, full["ffn2_w_in"], r2, dr3, alpha, row(wts["ln2_g"]), 1.0,
                                       "ffn2_in_dx_ln2_bwd", token=token)
    small["ln2_g"], small["ln2_b"] = dg.sum(0, keepdims=True), db.sum(0, keepdims=True)
    dm = _mm(dr2_h, full["mix_w_out"], tb=True, name="mix_out_dx")
    send["mix_w_out"] = _mm(m_mix, dr2_h, ta=True, name="mix_out_dw", out_dtype=BF16).reshape(
        N_DEV, -1, d_model)
    dga, dgb, ds5out, dgm, dw_a, dw_b = _merge_bwd(
        dm, y_a, y_b, s5out, gm, proj, full["up_a"], full["up_b"], "merge_bwd")
    send["up_a"] = _split_cols(dw_a.astype(BF16))
    send["up_b"] = _split_cols(dw_b.astype(BF16))

    dzu, dzv, dws, dbs_cols, dg, db = _gmlp_bwd(
        proj, dgm, row(wts["gmlp_ln_g"]), row(wts["gmlp_ln_b"]), ws_masked, ws_masked_t, bs_cols,
        "gmlp_bwd")
    small["gmlp_ln_g"], small["gmlp_ln_b"] = dg.sum(0, keepdims=True), db.sum(0, keepdims=True)
    small["gmlp_w_s"] = jnp.where(causal[None], dws, 0.0)[None]
    small["gmlp_b_s"] = dbs_cols.reshape(CHUNK, n_heads, -1).sum(-1).T[None]

    ds5out_p = _perm_rows(ds5out, n_seq)
    dy_p, dza_skip, dw_glu, dd, dgb_glu = _s5_post_bwd(
        yssm, za_p, ds5out_p, row(wts["ssm_d"]), full["ssm_glu_w"], row(wts["ssm_glu_b"]),
        "s5_post_bwd")
    send["ssm_glu_w"] = dw_glu.astype(BF16).reshape(N_DEV, -1, d_ssm)
    small["ssm_d"], small["ssm_glu_b"] = dd.sum(0, keepdims=True), dgb_glu.sum(0, keepdims=True)
    dza_p, dbb_re, dbb_im, dc_re, dc_im, da_re, da_im = _s5_bwd(
        dy_p, dza_skip, h_re, h_im, za_p16, b_sg_re, b_sg_im, a_re, a_im, c_sg_re, c_sg_im, n_seq, "s5_bwd")
    small["ssm_c_re"] = _super_groups_out_take(dc_re, gl, n_ch, n_state)[None]
    small["ssm_c_im"] = -_super_groups_out_take(dc_im, gl, n_ch, n_state)[None]
    dbb_re = _super_groups_in_take(dbb_re, gl, n_state, n_ch)
    dbb_im = _super_groups_in_take(dbb_im, gl, n_state, n_ch)
    dab_re = da_re.sum(0).reshape(n_groups, n_state)
    dab_im = da_im.sum(0).reshape(n_groups, n_state)
    for n, g in zip(("ssm_lambda_re", "ssm_lambda_im", "ssm_log_dt", "ssm_b_re", "ssm_b_im"),
                    disc_vjp((dab_re, dab_im, dbb_re, dbb_im))):
        small[n] = g[None]

    dproj = jnp.concatenate([_unperm_rows(dza_p, n_seq), dzu, dzv, dga, dgb], axis=1)
    group2 = ("mix_w_out", "up_a", "up_b", "ssm_glu_w")
    exchange2, token = _exchange_start(
        [send[n] for n in group2] + [on_wire(n, small[n]) for n in SMALL_MID], dproj, "grad_exchange2_start")
    send["mix_w_in"] = _split_cols(_mm(x1_16, dproj, ta=True, name="mix_in_dw", out_dtype=BF16, token=token))
    exchange3, token = _exchange_start([send["mix_w_in"]], dproj, "grad_exchange3_start")
    dx1_f = _mm(dproj, full["mix_w_in"], tb=True, name="mix_in_dx", token=token)

    dr1, dr1_h, dg, db = _ln_bwd_call(r1, dx1_f, dr2, alpha, row(wts["ln1_g"]), 0.5, "ln1_bwd")
    small["ln1_g"], small["ln1_b"] = dg.sum(0, keepdims=True), db.sum(0, keepdims=True)
    dw = _ffn_out_dw(a1, dr1_h, "ffn1_out_dw")
    send["ffn1_w_out"] = dw.reshape(N_DEV, -1, d_model)
    exchange4, token = _exchange_start([send["ffn1_w_out"]] + [on_wire(n, small[n]) for n in SMALL_LATE], dr1_h,
                                       "grad_exchange4_start")
    dh1 = _ffn_out_dx(dr1_h, full["ffn1_w_out"], h1, "ffn1_out_dx", token=token)
    dw = _ffn_in_dw(x16, dh1, "ffn1_in_dw")
    send["ffn1_w_in"] = dw.reshape((N_DEV,) + dw.shape[2:])
    exchange5, token = _exchange_start([send["ffn1_w_in"]], dh1, "grad_exchange5_start")
    grad_x = _ffn_in_dx(dh1, full["ffn1_w_in"], dr1, alpha, "ffn1_in_dx", token=token)

    results = {}

    def update(names, recv, prefix):
        for n, r in zip(names, recv):
            results[n] = _adamw(r, wts[n], mom[n], var[n], prefix + n)

    def update_small(names, recv, name):
        items = [(r, lane_dense(n, wts[n]), lane_dense(n, mom[n]), lane_dense(n, var[n])) for n, r in zip(names, recv)]
        for n, outs in zip(names, _adamw_small(items, name)):
            results[n] = tuple(o.reshape(wts[n].shape) for o in outs)

    def done(names):
        return [results[n][3] for n in names]

    recv = _exchange_wait(exchange1, [grad_x], "grad_exchange1_wait")
    update(group1, recv[:len(group1)], "adamw_")
    update_small(SMALL_HEAD, recv[len(group1):], "adamw_ln3")
    recv = _exchange_wait(exchange2, done(group1 + SMALL_HEAD), "grad_exchange2_wait")
    update(group2, recv[:len(group2)], "adamw_")
    update_small(SMALL_MID, recv[len(group2):], "adamw_small")
    recv = _exchange_wait(exchange3, done(group2 + SMALL_MID), "grad_exchange3_wait")
    update(("mix_w_in",), recv, "adamw_")
    recv = _exchange_wait(exchange4, done(("mix_w_in",)), "grad_exchange4_wait")
    update(("ffn1_w_out",), recv[:1], "adamw_")
    update_small(SMALL_LATE, recv[1:], "adamw_ln1")
    recv = _exchange_wait(exchange5, done(("ffn1_w_out",) + SMALL_LATE), "grad_exchange5_wait")
    update(("ffn1_w_in",), recv, "adamw_")

    outs = [loss, grad_x.reshape(n_seq, seq, d_model)]
    for k in range(4):
        outs += [results[n][k][None] if n in SHARDED else results[n][k] for n in WEIGHTS]
    return tuple(outs)
```

```python
import functools
import math

import jax
import jax.numpy as jnp
from jax import lax
from jax.experimental import pallas as pl
from jax.experimental.pallas import tpu as pltpu

F32 = jnp.float32
BF16 = jnp.bfloat16

N_DEV = 8
LN_EPS = 1e-5
CHUNK = 128
N_SEG = 8
S5_CHANNELS_PER_STEP = 128
VMEM_LIMIT_BYTES = 56 * 1024 * 1024
MM_OPERAND_BLOCK_BYTES = 8 * 1024 * 1024
ADAM_BLOCK_BYTES = 6 * 1024 * 1024

ADAM_LR = 0.001
ADAM_B1 = 0.9
ADAM_B2 = 0.999
ADAM_EPS = 1e-08
ADAM_WD = 0.01
ADAM_STEP = 10

COL_SHARDED = ("ffn1_w_in", "mix_w_in", "up_a", "up_b", "ffn2_w_in", "ple_w_proj")
ROW_SHARDED = ("ffn1_w_out", "ssm_glu_w", "mix_w_out", "ffn2_w_out", "ple_w_gate")
SHARDED = ("ffn1_w_in", "ffn1_w_out", "mix_w_in", "ssm_glu_w", "up_a", "up_b", "mix_w_out",
           "ffn2_w_in", "ffn2_w_out", "ple_w_proj", "ple_w_gate")
WEIGHTS = ("ffn1_w_in", "ffn1_w_out", "ln1_g", "ln1_b", "mix_w_in", "ssm_lambda_re", "ssm_lambda_im",
           "ssm_log_dt", "ssm_b_re", "ssm_b_im", "ssm_c_re", "ssm_c_im", "ssm_d", "ssm_glu_w",
           "ssm_glu_b", "gmlp_ln_g", "gmlp_ln_b", "gmlp_w_s", "gmlp_b_s", "up_a", "up_b", "mix_w_out",
           "ln2_g", "ln2_b", "ffn2_w_in", "ffn2_w_out", "ln3_g", "ln3_b", "ple_w_proj", "ple_w_gate")
GATHER_FIRST = ("ffn1_w_in", "ffn1_w_out")
GATHER_MIX = ("mix_w_in", "ssm_glu_w", "up_a", "up_b", "mix_w_out")
GATHER_LAST = ("ffn2_w_in", "ffn2_w_out", "ple_w_proj", "ple_w_gate")
SMALL = tuple(n for n in WEIGHTS if n not in SHARDED)
SMALL_HEAD = ("ln3_g", "ln3_b")
SMALL_LATE = ("ln1_g", "ln1_b")
SMALL_MID = tuple(n for n in SMALL if n not in SMALL_HEAD + SMALL_LATE)
SMALL_BF16 = ("gmlp_w_s", "ssm_b_re", "ssm_b_im", "ssm_c_re", "ssm_c_im")
INPUT_NAMES = ("x", "p") + WEIGHTS + ("loss_target",) + tuple("m_" + n for n in WEIGHTS) + tuple(
    "v_" + n for n in WEIGHTS)


def _pick(dim, pref, mult=128):
    if dim <= pref:
        return dim
    d = (pref // mult) * mult
    while d >= mult:
        if dim % d == 0:
            return d
        d -= mult
    return dim


def _params(n_axes=1):
    return pltpu.CompilerParams(dimension_semantics=("arbitrary",) * n_axes,
                                vmem_limit_bytes=VMEM_LIMIT_BYTES)


def _sigmoid(v):
    return 1.0 / (1.0 + jnp.exp(-v))


_GELU_C = math.sqrt(2.0 / math.pi)


def _gelu(v):
    return 0.5 * v * (1.0 + jnp.tanh(_GELU_C * (v + 0.044715 * (v * v * v))))


def _gelu_grad(v):
    t = jnp.tanh(_GELU_C * (v + 0.044715 * (v * v * v)))
    return 0.5 * (1.0 + t) + 0.5 * v * (1.0 - t * t) * (_GELU_C * (1.0 + 3.0 * 0.044715 * v * v))


def _ln_stats(r):
    mu = jnp.mean(r, axis=-1, keepdims=True)
    xc = r - mu
    var = jnp.mean(xc * xc, axis=-1, keepdims=True)
    rstd = lax.rsqrt(var + LN_EPS)
    return xc * rstd, rstd


def _ln_bwd(dy, xhat, rstd, g):
    dxh = dy * g
    m1 = jnp.mean(dxh, axis=-1, keepdims=True)
    m2 = jnp.mean(dxh * xhat, axis=-1, keepdims=True)
    return rstd * (dxh - m1 - xhat * m2)


def _fold8(v):
    rows, w = v.shape
    return jnp.sum(v.reshape(rows // 8, 8, w), axis=0)


def _dot(a, b, ta=False, tb=False):
    dn = (((0 if ta else 1,), (1 if tb else 0,)), ((), ()))
    return lax.dot_general(a.astype(BF16), b.astype(BF16), dn, preferred_element_type=F32)


_TOKEN = pl.BlockSpec((8, 128), lambda *_: (0, 0))


def _mm(a, b, *, name, ta=False, tb=False, out_dtype=F32, add=None, add_scale=1.0, token=None,
        tm=2048, tn=512, tk=4096):
    m_dim = a.shape[1] if ta else a.shape[0]
    k_dim = a.shape[0] if ta else a.shape[1]
    n_dim = b.shape[0] if tb else b.shape[1]
    assert (b.shape[1] if tb else b.shape[0]) == k_dim, (name, a.shape, b.shape)
    tk = _pick(k_dim, tk)
    tm = min(tm, max(256, MM_OPERAND_BLOCK_BYTES // (tk * a.dtype.itemsize)))
    tm, tn = _pick(m_dim, tm), _pick(n_dim, tn)
    nk = k_dim // tk
    has_add = add is not None

    def finish(r, add_ref, o_ref):
        if has_add:
            r = r + add_scale * add_ref[...].astype(F32)
        o_ref[...] = r.astype(out_dtype)

    def body(*refs):
        a_ref, b_ref = refs[:2]
        add_ref = refs[2] if has_add else None
        o_ref = refs[n_in]
        if nk == 1:
            finish(_dot(a_ref[...], b_ref[...], ta, tb), add_ref, o_ref)
            return
        acc_ref = refs[-1]
        k = pl.program_id(2)

        @pl.when(k == 0)
        def _():
            acc_ref[...] = jnp.zeros_like(acc_ref)

        acc_ref[...] += _dot(a_ref[...], b_ref[...], ta, tb)

        @pl.when(k == nk - 1)
        def _():
            finish(acc_ref[...], add_ref, o_ref)

    a_spec = (pl.BlockSpec((tk, tm), lambda i, j, k: (k, i)) if ta
              else pl.BlockSpec((tm, tk), lambda i, j, k: (i, k)))
    b_spec = (pl.BlockSpec((tn, tk), lambda i, j, k: (j, k)) if tb
              else pl.BlockSpec((tk, tn), lambda i, j, k: (k, j)))
    in_specs = [a_spec, b_spec]
    operands = [a, b]
    if has_add:
        in_specs.append(pl.BlockSpec((tm, tn), lambda i, j, k: (i, j)))
        operands.append(add)
    if token is not None:
        in_specs.append(_TOKEN)
        operands.append(token)
    n_in = len(operands)
    return pl.pallas_call(
        body, name=name,
        out_shape=jax.ShapeDtypeStruct((m_dim, n_dim), out_dtype),
        grid=(m_dim // tm, n_dim // tn, nk),
        in_specs=in_specs,
        out_specs=pl.BlockSpec((tm, tn), lambda i, j, k: (i, j)),
        scratch_shapes=[pltpu.VMEM((tm, tn), F32)] if nk > 1 else [],
        compiler_params=_params(3),
    )(*operands)


def _to_bf16(arrays, name):
    n = len(arrays)

    def body(*refs):
        for src, dst in zip(refs[:n], refs[n:]):
            dst[...] = src[...].astype(BF16)

    vmem = pl.BlockSpec(memory_space=pltpu.VMEM)
    return pl.pallas_call(
        body, name=name, out_shape=tuple(jax.ShapeDtypeStruct(a.shape, BF16) for a in arrays),
        in_specs=[vmem] * n, out_specs=tuple([vmem] * n),
        compiler_params=pltpu.CompilerParams(vmem_limit_bytes=VMEM_LIMIT_BYTES))(*arrays)


def _rows(tr, w, cb=0):
    return pl.BlockSpec((tr, w), lambda i: (i, cb))


def _whole(shape):
    nd = len(shape)
    return pl.BlockSpec(tuple(shape), lambda i: (0,) * nd)


def _ffn_in(x16, w_in, name, token=None):
    t, d = x16.shape
    _, nb, _, fb = w_in.shape
    tm = _pick(t, 1024, 8)
    extra = [] if token is None else [token]

    def body(*refs):
        x_ref, wg_ref, wu_ref = refs[:3]
        h_ref, a_ref = refs[-2:]
        xv = x_ref[...]
        g = _dot(xv, wg_ref[0, 0])
        u = _dot(xv, wu_ref[0, 0])
        h_ref[0, 0] = g.astype(BF16)
        h_ref[0, 1] = u.astype(BF16)
        a_ref[0] = (g * _sigmoid(g) * u).astype(BF16)

    return pl.pallas_call(
        body, name=name,
        out_shape=(jax.ShapeDtypeStruct((nb, 2, t, fb), BF16), jax.ShapeDtypeStruct((nb, t, fb), BF16)),
        grid=(t // tm, nb),
        in_specs=[pl.BlockSpec((tm, d), lambda i, j: (i, 0)),
                  pl.BlockSpec((1, 1, d, fb), lambda i, j: (0, j, 0, 0)),
                  pl.BlockSpec((1, 1, d, fb), lambda i, j: (1, j, 0, 0))] + [_TOKEN] * len(extra),
        out_specs=(pl.BlockSpec((1, 2, tm, fb), lambda i, j: (j, 0, i, 0)),
                   pl.BlockSpec((1, tm, fb), lambda i, j: (j, i, 0))),
        compiler_params=_params(2))(x16, w_in, w_in, *extra)


def _ffn_out_ln(a, w_out, xin, g, b, alpha, name, scale=0.5):
    nb, t, fb = a.shape
    d = w_out.shape[2]
    tm = _pick(t, 512, 8)

    def body(a_ref, w_ref, x_ref, g_ref, b_ref, r_ref, y_ref, y16_ref):
        f = _dot(a_ref[0], w_ref[0])
        for jb in range(1, nb):
            f = f + _dot(a_ref[jb], w_ref[jb])
        r = alpha * x_ref[...] + scale * f
        xhat, _ = _ln_stats(r)
        y = xhat * g_ref[...] + b_ref[...]
        r_ref[...] = r
        y_ref[...] = y
        y16_ref[...] = y.astype(BF16)

    return pl.pallas_call(
        body, name=name,
        out_shape=(jax.ShapeDtypeStruct((t, d), F32), jax.ShapeDtypeStruct((t, d), F32),
                   jax.ShapeDtypeStruct((t, d), BF16)),
        grid=(t // tm,),
        in_specs=[pl.BlockSpec((nb, tm, fb), lambda i: (0, i, 0)), _whole(w_out.shape), _rows(tm, d),
                  _whole((1, d)), _whole((1, d))],
        out_specs=(_rows(tm, d), _rows(tm, d), _rows(tm, d)),
        compiler_params=_params())(a, w_out, xin, g, b)


def _ffn_out_dx(drs, w_out, h, name, token=None):
    nb, _, t, fb = h.shape
    d = w_out.shape[2]
    tm = _pick(t, 1024, 8)
    extra = [] if token is None else [token]

    def body(*refs):
        dr_ref, w_ref, h_ref, dh_ref = refs[0], refs[1], refs[2], refs[-1]
        da = _dot(dr_ref[...], w_ref[0], tb=True)
        g, u = h_ref[0, 0].astype(F32), h_ref[0, 1].astype(F32)
        sg = _sigmoid(g)
        dh_ref[0, 0] = (da * u * (sg * (1.0 + g * (1.0 - sg)))).astype(BF16)
        dh_ref[0, 1] = (da * (g * sg)).astype(BF16)

    return pl.pallas_call(
        body, name=name, out_shape=jax.ShapeDtypeStruct((nb, 2, t, fb), BF16), grid=(t // tm, nb),
        in_specs=[pl.BlockSpec((tm, d), lambda i, j: (i, 0)),
                  pl.BlockSpec((1, fb, d), lambda i, j: (j, 0, 0)),
                  pl.BlockSpec((1, 2, tm, fb), lambda i, j: (j, 0, i, 0))] + [_TOKEN] * len(extra),
        out_specs=pl.BlockSpec((1, 2, tm, fb), lambda i, j: (j, 0, i, 0)),
        compiler_params=_params(2))(drs, w_out, h, *extra)


def _ffn_out_dw(a, drs, name):
    nb, t, fb = a.shape
    d = drs.shape[1]

    def body(a_ref, dr_ref, o_ref):
        o_ref[0] = _dot(a_ref[0], dr_ref[...], ta=True).astype(BF16)

    return pl.pallas_call(
        body, name=name, out_shape=jax.ShapeDtypeStruct((nb, fb, d), BF16), grid=(nb,),
        in_specs=[pl.BlockSpec((1, t, fb), lambda j: (j, 0, 0)), pl.BlockSpec((t, d), lambda j: (0, 0))],
        out_specs=pl.BlockSpec((1, fb, d), lambda j: (j, 0, 0)),
        compiler_params=_params())(a, drs)


def _ffn_in_dw(x16, dh, name, token=None):
    t, d = x16.shape
    nb, _, _, fb = dh.shape
    extra = [] if token is None else [token]

    def body(*refs):
        x_ref, dh_ref, o_ref = refs[0], refs[1], refs[-1]
        o_ref[0, 0] = _dot(x_ref[...], dh_ref[0, 0], ta=True).astype(BF16)

    return pl.pallas_call(
        body, name=name, out_shape=jax.ShapeDtypeStruct((2, nb, d, fb), BF16), grid=(nb, 2),
        in_specs=[pl.BlockSpec((t, d), lambda j, s: (0, 0)),
                  pl.BlockSpec((1, 1, t, fb), lambda j, s: (j, s, 0, 0))] + [_TOKEN] * len(extra),
        out_specs=pl.BlockSpec((1, 1, d, fb), lambda j, s: (s, j, 0, 0)),
        compiler_params=_params(2))(x16, dh, *extra)


def _ffn_in_dx(dh, w_in, add, add_scale, name, token=None):
    nb, _, t, fb = dh.shape
    d = w_in.shape[2]
    tm = _pick(t, 512, 8)
    has_add = add is not None
    extra = [] if token is None else [token]

    def body(*refs):
        dh_ref, w_ref = refs[:2]
        o_ref = refs[-1]
        acc = None
        for jb in range(nb):
            for s in range(2):
                part = _dot(dh_ref[jb, s], w_ref[s, jb], tb=True)
                acc = part if acc is None else acc + part
        if has_add:
            acc = acc + add_scale * refs[2][...]
        o_ref[...] = acc

    return pl.pallas_call(
        body, name=name, out_shape=jax.ShapeDtypeStruct((t, d), F32), grid=(t // tm,),
        in_specs=[pl.BlockSpec((nb, 2, tm, fb), lambda i: (0, 0, i, 0)), _whole(w_in.shape)]
        + ([_rows(tm, d)] if has_add else []) + [_TOKEN] * len(extra),
        out_specs=_rows(tm, d),
        compiler_params=_params())(*([dh, w_in] + ([add] if has_add else []) + extra))


def _ffn_in_dx_ln(dh, w_in, r, dy_b, b_scale, ln_g, out_scale, name, token=None):
    nb, _, t, fb = dh.shape
    d = w_in.shape[2]
    tm = _pick(t, 512, 8)
    extra = [] if token is None else [token]

    def body(*refs):
        dh_ref, w_ref, r_ref, dyb_ref, g_ref = refs[:5]
        dr_ref, drs_ref, dg_ref, dbeta_ref = refs[-4:]
        dy = b_scale * dyb_ref[...]
        for jb in range(nb):
            for s in range(2):
                dy = dy + _dot(dh_ref[jb, s], w_ref[s, jb], tb=True)
        xhat, rstd = _ln_stats(r_ref[...])
        dr = _ln_bwd(dy, xhat, rstd, g_ref[...])
        dr_ref[...] = dr
        drs_ref[...] = (out_scale * dr).astype(BF16)

        @pl.when(pl.program_id(0) == 0)
        def _():
            dg_ref[...] = jnp.zeros_like(dg_ref)
            dbeta_ref[...] = jnp.zeros_like(dbeta_ref)

        dg_ref[...] += _fold8(dy * xhat)
        dbeta_ref[...] += _fold8(dy)

    return pl.pallas_call(
        body, name=name,
        out_shape=(jax.ShapeDtypeStruct((t, d), F32), jax.ShapeDtypeStruct((t, d), BF16),
                   jax.ShapeDtypeStruct((8, d), F32), jax.ShapeDtypeStruct((8, d), F32)),
        grid=(t // tm,),
        in_specs=[pl.BlockSpec((nb, 2, tm, fb), lambda i: (0, 0, i, 0)), _whole(w_in.shape), _rows(tm, d),
                  _rows(tm, d), _whole((1, d))] + [_TOKEN] * len(extra),
        out_specs=(_rows(tm, d), _rows(tm, d), _whole((8, d)), _whole((8, d))),
        compiler_params=_params())(dh, w_in, r, dy_b, ln_g, *extra)


def _mm_ln_bwd(a, w, r, dy_b, b_scale, ln_g, out_scale, name, token=None):
    t, k_dim = a.shape
    d = w.shape[0]
    tm = _pick(t, 512, 8)
    extra = [] if token is None else [token]

    def body(*refs):
        a_ref, w_ref, r_ref, dyb_ref, g_ref = refs[:5]
        dr_ref, drs_ref, dg_ref, dbeta_ref = refs[-4:]
        dy = _dot(a_ref[...], w_ref[...], tb=True) + b_scale * dyb_ref[...]
        xhat, rstd = _ln_stats(r_ref[...])
        dr = _ln_bwd(dy, xhat, rstd, g_ref[...])
        dr_ref[...] = dr
        drs_ref[...] = (out_scale * dr).astype(BF16)

        @pl.when(pl.program_id(0) == 0)
        def _():
            dg_ref[...] = jnp.zeros_like(dg_ref)
            dbeta_ref[...] = jnp.zeros_like(dbeta_ref)

        dg_ref[...] += _fold8(dy * xhat)
        dbeta_ref[...] += _fold8(dy)

    return pl.pallas_call(
        body, name=name,
        out_shape=(jax.ShapeDtypeStruct((t, d), F32), jax.ShapeDtypeStruct((t, d), BF16),
                   jax.ShapeDtypeStruct((8, d), F32), jax.ShapeDtypeStruct((8, d), F32)),
        grid=(t // tm,),
        in_specs=[_rows(tm, k_dim), _whole(w.shape), _rows(tm, d), _rows(tm, d), _whole((1, d))]
        + [_TOKEN] * len(extra),
        out_specs=(_rows(tm, d), _rows(tm, d), _whole((8, d)), _whole((8, d))),
        compiler_params=_params())(a, w, r, dy_b, ln_g, *extra)


def _take_row(v, row_id, s):
    return jnp.sum(jnp.where(row_id == s, v, 0.0), axis=0, keepdims=True)


def _complex_power(ar, ai, n):
    out = None
    while n:
        if n & 1:
            out = (ar, ai) if out is None else (out[0] * ar - out[1] * ai, out[0] * ai + out[1] * ar)
        ar, ai = ar * ar - ai * ai, 2.0 * ar * ai
        n >>= 1
    return out


def _seg_carries(f_re, f_im, p_re, p_im, reverse):
    row_id = lax.broadcasted_iota(jnp.int32, f_re.shape, 0)
    p_re, p_im = _take_row(p_re, row_id, 0), _take_row(p_im, row_id, 0)
    out_re, out_im = jnp.zeros_like(f_re), jnp.zeros_like(f_im)
    c_re, c_im = jnp.zeros_like(p_re), jnp.zeros_like(p_im)
    order = range(N_SEG - 1, -1, -1) if reverse else range(N_SEG)
    for s in order:
        out_re = jnp.where(row_id == s, c_re, out_re)
        out_im = jnp.where(row_id == s, c_im, out_im)
        fr, fi = _take_row(f_re, row_id, s), _take_row(f_im, row_id, s)
        c_re, c_im = fr + p_re * c_re - p_im * c_im, fi + p_re * c_im + p_im * c_re
    return out_re, out_im


def _s5_fwd(za16, b_re, b_im, a_re, a_im, c_re, c_im, n_seq, name):
    t = za16.shape[0]
    n_sg, ch, st = b_re.shape
    seq = t // n_seq
    steps = seq // N_SEG

    def body(za_ref, bre_ref, bim_ref, are, aim, cre_ref, cim_ref, hre, him, y_ref):
        za = za_ref[...]
        hre[...] = _dot(za, bre_ref[0])
        him[...] = _dot(za, bim_ref[0])
        ar = jnp.broadcast_to(are[...], (N_SEG, st))
        ai = jnp.broadcast_to(aim[...], (N_SEG, st))
        zero = jnp.zeros((N_SEG, st), F32)

        def local(k, carry):
            lr, li = carry
            rows = pl.ds(pl.multiple_of(k * N_SEG, N_SEG), N_SEG)
            nr = ar * lr - ai * li + hre[rows, :]
            ni = ar * li + ai * lr + him[rows, :]
            hre[rows, :] = nr
            him[rows, :] = ni
            return nr, ni

        f_re, f_im = lax.fori_loop(0, steps, local, (zero, zero))
        c_re, c_im = _seg_carries(f_re, f_im, *_complex_power(ar, ai, steps), reverse=False)

        def fix(k, carry):
            qr, qi = carry
            rows = pl.ds(pl.multiple_of(k * N_SEG, N_SEG), N_SEG)
            hre[rows, :] = hre[rows, :] + qr
            him[rows, :] = him[rows, :] + qi
            return ar * qr - ai * qi, ar * qi + ai * qr

        lax.fori_loop(0, steps, fix, (ar * c_re - ai * c_im, ar * c_im + ai * c_re))
        y_ref[...] = _dot(hre[...], cre_ref[0]) + _dot(him[...], cim_ref[0])

    rows_ch = pl.BlockSpec((seq, ch), lambda s, j: (s, j))
    rows_st = pl.BlockSpec((seq, st), lambda s, j: (s, j))
    vec = pl.BlockSpec((1, st), lambda s, j: (0, j))
    b_blk = pl.BlockSpec((1, ch, st), lambda s, j: (j, 0, 0))
    c_blk = pl.BlockSpec((1, st, ch), lambda s, j: (j, 0, 0))
    return pl.pallas_call(
        body, name=name,
        out_shape=(jax.ShapeDtypeStruct((t, n_sg * st), F32), jax.ShapeDtypeStruct((t, n_sg * st), F32),
                   jax.ShapeDtypeStruct((t, n_sg * ch), F32)),
        grid=(n_seq, n_sg), in_specs=[rows_ch, b_blk, b_blk, vec, vec, c_blk, c_blk],
        out_specs=(rows_st, rows_st, rows_ch),
        compiler_params=_params(2))(za16, b_re, b_im, a_re, a_im, c_re, c_im)


def _s5_bwd(dy16, dza_skip, h_re, h_im, za16, b_re, b_im, a_re, a_im, c_re, c_im, n_seq, name):
    t = dy16.shape[0]
    n_sg, ch, st = b_re.shape
    seq = t // n_seq
    steps = seq // N_SEG

    def body(dy_ref, skip_ref, hre, him, za_ref, bre_ref, bim_ref, are, aim, cre_ref, cim_ref,
             dza_ref, dbre_ref, dbim_ref, dcre_ref, dcim_ref, dare, daim, lre, lim):
        dy = dy_ref[...]
        lre[...] = _dot(dy, cre_ref[0], tb=True)
        lim[...] = _dot(dy, cim_ref[0], tb=True)
        ar = jnp.broadcast_to(are[...], (N_SEG, st))
        ai = -jnp.broadcast_to(aim[...], (N_SEG, st))
        zero = jnp.zeros((N_SEG, st), F32)

        def local(i, carry):
            lr, li = carry
            k = steps - 1 - i
            rows = pl.ds(pl.multiple_of(k * N_SEG, N_SEG), N_SEG)
            nr = ar * lr - ai * li + lre[rows, :]
            ni = ar * li + ai * lr + lim[rows, :]
            lre[rows, :] = nr
            lim[rows, :] = ni
            return nr, ni

        f_re, f_im = lax.fori_loop(0, steps, local, (zero, zero))
        c_re, c_im = _seg_carries(f_re, f_im, *_complex_power(ar, ai, steps), reverse=True)

        def accumulate(lam_r, lam_i, hp_r, hp_i, acc_r, acc_i):
            return acc_r + (lam_r * hp_r + lam_i * hp_i), acc_i + (lam_i * hp_r - lam_r * hp_i)

        def fix(i, carry):
            qr, qi, acc_r, acc_i = carry
            k = steps - 1 - i
            rows = pl.ds(pl.multiple_of(k * N_SEG, N_SEG), N_SEG)
            prev = pl.ds(pl.multiple_of((k - 1) * N_SEG, N_SEG), N_SEG)
            lam_r = lre[rows, :] + qr
            lam_i = lim[rows, :] + qi
            lre[rows, :] = lam_r
            lim[rows, :] = lam_i
            acc_r, acc_i = accumulate(lam_r, lam_i, hre[prev, :], him[prev, :], acc_r, acc_i)
            return ar * qr - ai * qi, ar * qi + ai * qr, acc_r, acc_i

        qr, qi, acc_r, acc_i = lax.fori_loop(
            0, steps - 1, fix, (ar * c_re - ai * c_im, ar * c_im + ai * c_re, zero, zero))
        first = pl.ds(0, N_SEG)
        last = pl.ds((steps - 1) * N_SEG, N_SEG)
        lam_r = lre[first, :] + qr
        lam_i = lim[first, :] + qi
        lre[first, :] = lam_r
        lim[first, :] = lam_i
        row_id = lax.broadcasted_iota(jnp.int32, (N_SEG, st), 0)
        hp_r = jnp.where(row_id == 0, 0.0, pltpu.roll(hre[last, :], 1, 0))
        hp_i = jnp.where(row_id == 0, 0.0, pltpu.roll(him[last, :], 1, 0))
        acc_r, acc_i = accumulate(lam_r, lam_i, hp_r, hp_i, acc_r, acc_i)

        lam_re16 = lre[...].astype(BF16)
        lam_im16 = lim[...].astype(BF16)
        dza_ref[...] = (_dot(lam_re16, bre_ref[0], tb=True) + _dot(lam_im16, bim_ref[0], tb=True)
                        + skip_ref[...]).astype(BF16)

        @pl.when(pl.program_id(1) == 0)
        def _():
            for ref in (dbre_ref, dbim_ref, dcre_ref, dcim_ref, dare, daim):
                ref[...] = jnp.zeros_like(ref)

        za = za_ref[...]
        dbre_ref[0] += _dot(za, lam_re16, ta=True)
        dbim_ref[0] += _dot(za, lam_im16, ta=True)
        dcre_ref[0] += _dot(hre[...], dy, ta=True)
        dcim_ref[0] += _dot(him[...], dy, ta=True)
        dare[...] += acc_r
        daim[...] += acc_i

    rows_ch = pl.BlockSpec((seq, ch), lambda j, s: (s, j))
    rows_st = pl.BlockSpec((seq, st), lambda j, s: (s, j))
    vec = pl.BlockSpec((1, st), lambda j, s: (0, j))
    b_blk = pl.BlockSpec((1, ch, st), lambda j, s: (j, 0, 0))
    c_blk = pl.BlockSpec((1, st, ch), lambda j, s: (j, 0, 0))
    acc = pl.BlockSpec((N_SEG, st), lambda j, s: (0, j))
    return pl.pallas_call(
        body, name=name,
        out_shape=(jax.ShapeDtypeStruct((t, n_sg * ch), BF16),
                   jax.ShapeDtypeStruct((n_sg, ch, st), F32), jax.ShapeDtypeStruct((n_sg, ch, st), F32),
                   jax.ShapeDtypeStruct((n_sg, st, ch), F32), jax.ShapeDtypeStruct((n_sg, st, ch), F32),
                   jax.ShapeDtypeStruct((N_SEG, n_sg * st), F32), jax.ShapeDtypeStruct((N_SEG, n_sg * st), F32)),
        grid=(n_sg, n_seq),
        in_specs=[rows_ch, rows_ch, rows_st, rows_st, rows_ch, b_blk, b_blk, vec, vec, c_blk, c_blk],
        out_specs=(rows_ch, b_blk, b_blk, c_blk, c_blk, acc, acc),
        scratch_shapes=[pltpu.VMEM((seq, st), F32), pltpu.VMEM((seq, st), F32)],
        compiler_params=_params(2))(dy16, dza_skip, h_re, h_im, za16, b_re, b_im, a_re, a_im, c_re, c_im)


def _s5_post_fwd(yssm, u, d_skip, glu_w, glu_b, name):
    t, w = yssm.shape
    tr = _pick(t, 512, 8)

    def body(y_ref, u_ref, d_ref, w_ref, b_ref, o_ref):
        yg = _gelu(y_ref[...] + d_ref[...] * u_ref[...])
        pre = _dot(yg, w_ref[...]) + b_ref[...]
        o_ref[...] = yg * _sigmoid(pre)

    return pl.pallas_call(
        body, name=name, out_shape=jax.ShapeDtypeStruct((t, w), F32), grid=(t // tr,),
        in_specs=[_rows(tr, w), _rows(tr, w), _whole((1, w)), _whole((w, w)), _whole((1, w))],
        out_specs=_rows(tr, w), compiler_params=_params())(yssm, u, d_skip, glu_w, glu_b)


def _s5_post_bwd(yssm, u, dout, d_skip, glu_w, glu_b, name):
    t, w = yssm.shape
    tr = _pick(t, 512, 8)

    def body(y_ref, u_ref, do_ref, d_ref, w_ref, b_ref, dy_ref, du_ref, dw_ref, dd_ref, db_ref):
        uu = u_ref[...]
        y = y_ref[...] + d_ref[...] * uu
        yg = _gelu(y)
        sg = _sigmoid(_dot(yg, w_ref[...]) + b_ref[...])
        do = do_ref[...]
        dpre = do * yg * sg * (1.0 - sg)
        dyg = do * sg + _dot(dpre, w_ref[...], tb=True)
        dy = dyg * _gelu_grad(y)
        dy_ref[...] = dy.astype(BF16)
        du_ref[...] = dy * d_ref[...]

        @pl.when(pl.program_id(0) == 0)
        def _():
            dw_ref[...] = jnp.zeros_like(dw_ref)
            dd_ref[...] = jnp.zeros_like(dd_ref)
            db_ref[...] = jnp.zeros_like(db_ref)

        dw_ref[...] += _dot(yg, dpre, ta=True)
        dd_ref[...] += _fold8(dy * uu)
        db_ref[...] += _fold8(dpre)

    return pl.pallas_call(
        body, name=name,
        out_shape=(jax.ShapeDtypeStruct((t, w), BF16), jax.ShapeDtypeStruct((t, w), F32),
                   jax.ShapeDtypeStruct((w, w), F32), jax.ShapeDtypeStruct((8, w), F32),
                   jax.ShapeDtypeStruct((8, w), F32)),
        grid=(t // tr,),
        in_specs=[_rows(tr, w), _rows(tr, w), _rows(tr, w), _whole((1, w)), _whole((w, w)),
                  _whole((1, w))],
        out_specs=(_rows(tr, w), _rows(tr, w), _whole((w, w)), _whole((8, w)), _whole((8, w))),
        compiler_params=_params())(yssm, u, dout, d_skip, glu_w, glu_b)


def _head_select(parts, head_dim):
    col_head = lax.broadcasted_iota(jnp.int32, parts[0].shape, 1) // head_dim
    out = jnp.zeros_like(parts[0])
    for h, part in enumerate(parts):
        out = jnp.where(col_head == h, part, out)
    return out


def _gmlp_fwd(proj, ln_g, ln_b, ws, bs_cols, name):
    t = proj.shape[0]
    n_heads = ws.shape[0]
    w = bs_cols.shape[1]
    head_dim = w // n_heads
    cpb = _pick(t // CHUNK, 4, 1)
    tr = cpb * CHUNK

    def body(zu_ref, zv_ref, g_ref, b_ref, ws_ref, bs_ref, o_ref):
        for c in range(cpb):
            rows = pl.ds(c * CHUNK, CHUNK)
            xhat, _ = _ln_stats(_gelu(zv_ref[rows, :]))
            vn = (xhat * g_ref[...] + b_ref[...]).astype(BF16)
            s = _head_select([_dot(ws_ref[h], vn) for h in range(n_heads)], head_dim) + bs_ref[...]
            o_ref[rows, :] = _gelu(zu_ref[rows, :]) * s

    return pl.pallas_call(
        body, name=name, out_shape=jax.ShapeDtypeStruct((t, w), F32), grid=(t // tr,),
        in_specs=[_rows(tr, w, 1), _rows(tr, w, 2), _whole((1, w)), _whole((1, w)),
                  _whole(ws.shape), _whole(bs_cols.shape)],
        out_specs=_rows(tr, w), compiler_params=_params())(proj, proj, ln_g, ln_b, ws, bs_cols)


def _gmlp_bwd(proj, dout, ln_g, ln_b, ws, ws_t, bs_cols, name):
    t = proj.shape[0]
    n_heads = ws.shape[0]
    w = bs_cols.shape[1]
    head_dim = w // n_heads
    cpb = _pick(t // CHUNK, 4, 1)
    tr = cpb * CHUNK

    def body(zu_ref, zv_ref, do_ref, g_ref, b_ref, ws_ref, wst_ref, bs_ref,
             dzu_ref, dzv_ref, dws_ref, dbs_ref, dg_ref, dbeta_ref):
        @pl.when(pl.program_id(0) == 0)
        def _():
            dws_ref[...] = jnp.zeros_like(dws_ref)
            dbs_ref[...] = jnp.zeros_like(dbs_ref)
            dg_ref[...] = jnp.zeros_like(dg_ref)
            dbeta_ref[...] = jnp.zeros_like(dbeta_ref)

        col_head = lax.broadcasted_iota(jnp.int32, (CHUNK, w), 1) // head_dim
        for c in range(cpb):
            rows = pl.ds(c * CHUNK, CHUNK)
            zu, zv, do = zu_ref[rows, :], zv_ref[rows, :], do_ref[rows, :]
            xhat, rstd = _ln_stats(_gelu(zv))
            vn = (xhat * g_ref[...] + b_ref[...]).astype(BF16)
            s = _head_select([_dot(ws_ref[h], vn) for h in range(n_heads)], head_dim) + bs_ref[...]
            dzu_ref[rows, :] = (do * s * _gelu_grad(zu)).astype(BF16)
            ds = do * _gelu(zu)
            ds16 = ds.astype(BF16)
            dbs_ref[...] += ds
            for h in range(n_heads):
                dws_ref[h] += _dot(jnp.where(col_head == h, ds16, jnp.zeros_like(ds16)), vn, tb=True)
            dvn = _head_select([_dot(wst_ref[h], ds16) for h in range(n_heads)], head_dim)
            dg_ref[...] += _fold8(dvn * xhat)
            dbeta_ref[...] += _fold8(dvn)
            dgv = _ln_bwd(dvn, xhat, rstd, g_ref[...])
            dzv_ref[rows, :] = (dgv * _gelu_grad(zv)).astype(BF16)

    return pl.pallas_call(
        body, name=name,
        out_shape=(jax.ShapeDtypeStruct((t, w), BF16), jax.ShapeDtypeStruct((t, w), BF16),
                   jax.ShapeDtypeStruct(ws.shape, F32), jax.ShapeDtypeStruct((CHUNK, w), F32),
                   jax.ShapeDtypeStruct((8, w), F32), jax.ShapeDtypeStruct((8, w), F32)),
        grid=(t // tr,),
        in_specs=[_rows(tr, w, 1), _rows(tr, w, 2), _rows(tr, w), _whole((1, w)), _whole((1, w)),
                  _whole(ws.shape), _whole(ws.shape), _whole(bs_cols.shape)],
        out_specs=(_rows(tr, w), _rows(tr, w), _whole(ws.shape), _whole((CHUNK, w)),
                   _whole((8, w)), _whole((8, w))),
        compiler_params=_params())(proj, proj, dout, ln_g, ln_b, ws, ws_t, bs_cols)


def _merge_fwd(s5out, gm, proj, up_a, up_b, name):
    t, w = s5out.shape
    d = up_a.shape[1]
    assert d == 2 * w
    tr = _pick(t, 256, 8)

    def body(s_ref, gm_ref, ga0, ga1, gb0, gb1, ua_ref, ub_ref, m_ref, ya_ref, yb_ref):
        ya = _dot(s_ref[...], ua_ref[...])
        yb = _dot(gm_ref[...], ub_ref[...])
        ya_ref[...] = ya
        yb_ref[...] = yb
        for half, (ga, gb) in enumerate(((ga0, gb0), (ga1, gb1))):
            cols = slice(half * w, (half + 1) * w)
            m_ref[:, cols] = (_sigmoid(ga[...]) * ya[:, cols] + _sigmoid(gb[...]) * yb[:, cols]).astype(BF16)

    return pl.pallas_call(
        body, name=name,
        out_shape=(jax.ShapeDtypeStruct((t, d), BF16), jax.ShapeDtypeStruct((t, d), F32),
                   jax.ShapeDtypeStruct((t, d), F32)),
        grid=(t // tr,),
        in_specs=[_rows(tr, w), _rows(tr, w), _rows(tr, w, 3), _rows(tr, w, 4), _rows(tr, w, 5),
                  _rows(tr, w, 6), _whole(up_a.shape), _whole(up_b.shape)],
        out_specs=(_rows(tr, d), _rows(tr, d), _rows(tr, d)),
        compiler_params=_params())(s5out, gm, proj, proj, proj, proj, up_a, up_b)


def _merge_bwd(dm, ya, yb, s5out, gm, proj, up_a, up_b, name):
    t, d = dm.shape
    w = d // 2
    tr = _pick(t, 256, 8)

    def body(dm_ref, ya_ref, yb_ref, s_ref, gm_ref, ga0, ga1, gb0, gb1, ua_ref, ub_ref,
             dga_ref, dgb_ref, ds_ref, dgm_ref, dua_ref, dub_ref):
        dm_v, ya, yb = dm_ref[...], ya_ref[...], yb_ref[...]
        dya, dyb = [], []
        for half, (ga, gb) in enumerate(((ga0, gb0), (ga1, gb1))):
            cols = slice(half * w, (half + 1) * w)
            sa, sb = _sigmoid(ga[...]), _sigmoid(gb[...])
            dmh = dm_v[:, cols]
            dga_ref[:, cols] = (dmh * ya[:, cols] * sa * (1.0 - sa)).astype(BF16)
            dgb_ref[:, cols] = (dmh * yb[:, cols] * sb * (1.0 - sb)).astype(BF16)
            dya.append((dmh * sa).astype(BF16))
            dyb.append((dmh * sb).astype(BF16))
        dya = jnp.concatenate(dya, axis=1)
        dyb = jnp.concatenate(dyb, axis=1)
        ds_ref[...] = _dot(dya, ua_ref[...], tb=True)
        dgm_ref[...] = _dot(dyb, ub_ref[...], tb=True)

        @pl.when(pl.program_id(0) == 0)
        def _():
            dua_ref[...] = jnp.zeros_like(dua_ref)
            dub_ref[...] = jnp.zeros_like(dub_ref)

        dua_ref[...] += _dot(s_ref[...], dya, ta=True)
        dub_ref[...] += _dot(gm_ref[...], dyb, ta=True)

    return pl.pallas_call(
        body, name=name,
        out_shape=(jax.ShapeDtypeStruct((t, d), BF16), jax.ShapeDtypeStruct((t, d), BF16),
                   jax.ShapeDtypeStruct((t, w), F32), jax.ShapeDtypeStruct((t, w), F32),
                   jax.ShapeDtypeStruct(up_a.shape, F32), jax.ShapeDtypeStruct(up_b.shape, F32)),
        grid=(t // tr,),
        in_specs=[_rows(tr, d), _rows(tr, d), _rows(tr, d), _rows(tr, w), _rows(tr, w),
                  _rows(tr, w, 3), _rows(tr, w, 4), _rows(tr, w, 5), _rows(tr, w, 6),
                  _whole(up_a.shape), _whole(up_b.shape)],
        out_specs=(_rows(tr, d), _rows(tr, d), _rows(tr, w), _rows(tr, w), _whole(up_a.shape),
                   _whole(up_b.shape)),
        compiler_params=_params())(dm, ya, yb, s5out, gm, proj, proj, proj, proj, up_a, up_b)


def _head(x3, r3, ln_g, p, target, w_gate, w_proj, out_scale, name):
    t, d = x3.shape
    pd = p.shape[1]
    tr = _pick(t, 256, 8)
    nb = t // tr

    def body(x_ref, r_ref, g_ref, p_ref, t_ref, wg_ref, wp_ref,
             loss_ref, dr_ref, drs_ref, dwg_ref, dwp_ref, dg_ref, dbeta_ref):
        xv = x_ref[...]
        sg = _sigmoid(_dot(xv, wg_ref[...]))
        pp = _dot(p_ref[...], wp_ref[...])
        err = xv + sg * pp - t_ref[...]
        loss_ref[...] = jnp.full((8, 128), 0.5 * jnp.sum(jnp.mean(err * err, axis=-1)), F32)
        dout = err * (1.0 / d)
        dgpre = dout * pp * sg * (1.0 - sg)
        dpp = dout * sg
        dx = dout + _dot(dgpre, wg_ref[...], tb=True)
        xhat, rstd = _ln_stats(r_ref[...])
        dr = _ln_bwd(dx, xhat, rstd, g_ref[...])
        dr_ref[...] = dr
        drs_ref[...] = (out_scale * dr).astype(BF16)

        @pl.when(pl.program_id(0) == 0)
        def _():
            for ref in (dwg_ref, dwp_ref, dg_ref, dbeta_ref):
                ref[...] = jnp.zeros_like(ref)

        dwg_ref[...] += _dot(xv, dgpre, ta=True)
        dwp_ref[...] += _dot(p_ref[...], dpp, ta=True)
        dg_ref[...] += _fold8(dx * xhat)
        dbeta_ref[...] += _fold8(dx)

    return pl.pallas_call(
        body, name=name,
        out_shape=(jax.ShapeDtypeStruct((nb * 8, 128), F32), jax.ShapeDtypeStruct((t, d), F32),
                   jax.ShapeDtypeStruct((t, d), BF16), jax.ShapeDtypeStruct((d, d), F32),
                   jax.ShapeDtypeStruct((pd, d), F32), jax.ShapeDtypeStruct((8, d), F32),
                   jax.ShapeDtypeStruct((8, d), F32)),
        grid=(nb,),
        in_specs=[_rows(tr, d), _rows(tr, d), _whole((1, d)), _rows(tr, pd), _rows(tr, d), _whole((d, d)),
                  _whole((pd, d))],
        out_specs=(pl.BlockSpec((8, 128), lambda i: (i, 0)), _rows(tr, d), _rows(tr, d), _whole((d, d)),
                   _whole((pd, d)), _whole((8, d)), _whole((8, d))),
        compiler_params=_params())(x3, r3, ln_g, p, target, w_gate, w_proj)


_HBM = pl.BlockSpec(memory_space=pltpu.HBM)


def _all_gather(shards, name):
    n = len(shards)

    def body(*refs):
        x_refs, out_refs = refs[:n], refs[n:2 * n]
        send_sems, recv_sems, local_sems = refs[2 * n:]
        x, y, c = lax.axis_index("x"), lax.axis_index("y"), lax.axis_index("c")
        me, sibling = (x, y, c), (x, y, 1 - c)
        chips = [(1 - x, y), (x, 1 - y), (1 - x, 1 - y)]

        def copy(a, k, block, to, own=False):
            slot = out_refs[a].at[4 * block[0] + 2 * block[1] + block[2]]
            return pltpu.make_async_remote_copy(
                src_ref=x_refs[a] if own else slot, dst_ref=slot,
                send_sem=send_sems.at[7 * a + k], recv_sem=recv_sems.at[7 * a + k],
                device_id=to, device_id_type=pl.DeviceIdType.MESH)

        mine = [pltpu.make_async_copy(x_refs[a], out_refs[a].at[4 * x + 2 * y + c], local_sems.at[a])
                for a in range(n)]
        sends = []
        for a in range(n):
            mine[a].start()
            first = [copy(a, 0, me, sibling, own=True)]
            first += [copy(a, 1 + j, me, (*chip, c), own=True) for j, chip in enumerate(chips)]
            for cp in first:
                cp.start()
            sends += first
        for a in range(n):
            for j, chip in enumerate(chips):
                copy(a, 1 + j, (*chip, c), me).wait_recv()
                passed = copy(a, 4 + j, (*chip, c), sibling)
                passed.start()
                sends.append(passed)
        for a in range(n):
            copy(a, 0, sibling, me).wait_recv()
            for j, chip in enumerate(chips):
                copy(a, 4 + j, (*chip, 1 - c), me).wait_recv()
        for cp in sends:
            cp.wait_send()
        for cp in mine:
            cp.wait()

    return pl.pallas_call(
        body, name=name,
        out_shape=tuple(jax.ShapeDtypeStruct((N_DEV,) + s.shape, s.dtype) for s in shards),
        in_specs=[_HBM] * n, out_specs=tuple([_HBM] * n),
        scratch_shapes=[pltpu.SemaphoreType.DMA((7 * n,)), pltpu.SemaphoreType.DMA((7 * n,)),
                        pltpu.SemaphoreType.DMA((n,))],
    )(*shards)


_SEM = pl.BlockSpec(memory_space=pltpu.SEMAPHORE)
_ANY = pl.BlockSpec(memory_space=pl.ANY)
_DATAFLOW = pltpu.SideEffectType.DATAFLOW_SIDE_EFFECTING


def _peer(k, x, y, c):
    return (1 - x if k & 4 else x, 1 - y if k & 2 else y, 1 - c if k & 1 else c)


def _exchange_start(sends, after, name):
    n = len(sends)
    lands = [lax.empty((N_DEV,) + s.shape[1:], s.dtype) for s in sends]

    def body(*refs):
        s_refs, l_refs = refs[:n], refs[n:2 * n]
        send_sems, recv_sems, local_sems, token = refs[2 * n + 1], refs[2 * n + 2], refs[2 * n + 3], refs[-1]
        x, y, c = lax.axis_index("x"), lax.axis_index("y"), lax.axis_index("c")
        me = 4 * x + 2 * y + c
        for a in range(n):
            same = sends[a].shape[0] == 1
            pltpu.make_async_copy(s_refs[a].at[0 if same else me], l_refs[a].at[me], local_sems.at[a]).start()
            for k in range(1, N_DEV):
                px, py, pc = _peer(k, x, y, c)
                pltpu.make_async_remote_copy(
                    src_ref=s_refs[a].at[0 if same else 4 * px + 2 * py + pc], dst_ref=l_refs[a].at[me],
                    send_sem=send_sems.at[7 * a + k - 1], recv_sem=recv_sems.at[7 * a + k - 1],
                    device_id=(px, py, pc), device_id_type=pl.DeviceIdType.MESH).start()
        token[...] = jnp.zeros_like(token)

    outs = pl.pallas_call(
        body, name=name,
        out_shape=(pltpu.SemaphoreType.DMA((7 * n,)), pltpu.SemaphoreType.DMA((7 * n,)),
                   pltpu.SemaphoreType.DMA((n,)),
                   *[pltpu.HBM(s.shape, s.dtype) for s in sends],
                   *[pltpu.HBM(l.shape, l.dtype) for l in lands],
                   jax.ShapeDtypeStruct((8, 128), F32)),
        in_specs=[_HBM] * (2 * n) + [_ANY],
        out_specs=(_SEM, _SEM, _SEM, *([_HBM] * (2 * n)), pl.BlockSpec(memory_space=pltpu.VMEM)),
        input_output_aliases={i: 3 + i for i in range(2 * n)},
        compiler_params=pltpu.CompilerParams(has_side_effects=_DATAFLOW),
    )(*[pltpu.with_memory_space_constraint(v, pltpu.HBM) for v in list(sends) + lands], after)
    return (outs[0], outs[1], outs[2], list(outs[3:3 + n]), list(outs[3 + n:3 + 2 * n])), outs[-1]


def _exchange_wait(handle, after, name):
    send_sems, recv_sems, local_sems, sends, lands = handle
    n = len(sends)

    def body(*refs):
        s_refs, l_refs = refs[:n], refs[n:2 * n]
        send_sems, recv_sems, local_sems = refs[2 * n], refs[2 * n + 1], refs[2 * n + 2]
        x, y, c = lax.axis_index("x"), lax.axis_index("y"), lax.axis_index("c")
        for a in range(n):
            pltpu.make_async_copy(s_refs[a].at[0], l_refs[a].at[0], local_sems.at[a]).wait()
            for k in range(1, N_DEV):
                copy = pltpu.make_async_remote_copy(
                    src_ref=s_refs[a].at[0], dst_ref=l_refs[a].at[0],
                    send_sem=send_sems.at[7 * a + k - 1], recv_sem=recv_sems.at[7 * a + k - 1],
                    device_id=_peer(k, x, y, c), device_id_type=pl.DeviceIdType.MESH)
                copy.wait_send()
                copy.wait_recv()

    outs = pl.pallas_call(
        body, name=name,
        out_shape=(*[pltpu.HBM(s.shape, s.dtype) for s in sends], *[pltpu.HBM(l.shape, l.dtype) for l in lands]),
        in_specs=[_HBM] * (2 * n) + [_SEM, _SEM, _SEM] + [_ANY] * len(after),
        out_specs=tuple([_HBM] * (2 * n)),
        input_output_aliases={i: i for i in range(2 * n)},
        compiler_params=pltpu.CompilerParams(has_side_effects=_DATAFLOW),
    )(*sends, *lands, send_sems, recv_sems, local_sems, *after)
    return list(outs[n:])


def _adamw(recv, w, m, v, name):
    n, rows, width = recv.shape
    tr = _pick(rows, max(8, ADAM_BLOCK_BYTES // (n * width * recv.dtype.itemsize)), 8)
    c_m = 1.0 - ADAM_B1 ** ADAM_STEP
    c_v = 1.0 - ADAM_B2 ** ADAM_STEP

    def body(r_ref, w_ref, m_ref, v_ref, g_ref, d_ref, nm_ref, nv_ref):
        g = r_ref[0].astype(F32)
        for i in range(1, n):
            g = g + r_ref[i].astype(F32)
        m2 = ADAM_B1 * m_ref[...] + (1.0 - ADAM_B1) * g
        v2 = ADAM_B2 * v_ref[...] + (1.0 - ADAM_B2) * (g * g)
        m_hat = m2 / c_m
        v_hat = v2 / c_v
        g_ref[...] = g
        d_ref[...] = -ADAM_LR * (m_hat / (jnp.sqrt(v_hat) + ADAM_EPS) + ADAM_WD * w_ref[...])
        nm_ref[...] = m2
        nv_ref[...] = v2

    blk = _rows(tr, width)
    shp = jax.ShapeDtypeStruct((rows, width), F32)
    return pl.pallas_call(
        body, name=name, out_shape=(shp, shp, shp, shp), grid=(rows // tr,),
        in_specs=[pl.BlockSpec((n, tr, width), lambda i: (0, i, 0)), blk, blk, blk],
        out_specs=(blk, blk, blk, blk), compiler_params=_params())(recv, w, m, v)


def _join_cols(gathered):
    n, r, c = gathered.shape
    return gathered.transpose(1, 0, 2).reshape(r, n * c)


def _split_cols(full):
    r, c = full.shape
    return full.reshape(r, N_DEV, c // N_DEV).transpose(1, 0, 2)


def _adamw_small(items, name):
    n = len(items)
    c_m = 1.0 - ADAM_B1 ** ADAM_STEP
    c_v = 1.0 - ADAM_B2 ** ADAM_STEP

    def body(*refs):
        ins, outs = refs[:4 * n], refs[4 * n:]
        for k in range(n):
            r_ref, w_ref, m_ref, v_ref = ins[4 * k:4 * k + 4]
            g = r_ref[0].astype(F32)
            for i in range(1, N_DEV):
                g = g + r_ref[i].astype(F32)
            m2 = ADAM_B1 * m_ref[...] + (1.0 - ADAM_B1) * g
            v2 = ADAM_B2 * v_ref[...] + (1.0 - ADAM_B2) * (g * g)
            outs[4 * k][...] = g
            outs[4 * k + 1][...] = -ADAM_LR * ((m2 / c_m) / (jnp.sqrt(v2 / c_v) + ADAM_EPS) + ADAM_WD * w_ref[...])
            outs[4 * k + 2][...] = m2
            outs[4 * k + 3][...] = v2

    vmem = pl.BlockSpec(memory_space=pltpu.VMEM)
    flat = pl.pallas_call(
        body, name=name,
        out_shape=tuple(jax.ShapeDtypeStruct(w.shape, F32) for _, w, _, _ in items for _ in range(4)),
        in_specs=[vmem] * (4 * n), out_specs=tuple([vmem] * (4 * n)),
        compiler_params=pltpu.CompilerParams(vmem_limit_bytes=VMEM_LIMIT_BYTES),
    )(*[a for item in items for a in item])
    return [tuple(flat[4 * k:4 * k + 4]) for k in range(n)]


def _discretise(lam_re, lam_im, log_dt, b_re, b_im):
    dt = jnp.exp(log_dt)[:, None]
    mag = jnp.exp(lam_re * dt)
    ab_re = mag * jnp.cos(lam_im * dt)
    ab_im = mag * jnp.sin(lam_im * dt)
    nr = ab_re - 1.0
    ni = ab_im
    den = lam_re * lam_re + lam_im * lam_im
    coef_re = ((nr * lam_re + ni * lam_im) / den)[..., None]
    coef_im = ((ni * lam_re - nr * lam_im) / den)[..., None]
    bb_re = coef_re * b_re - coef_im * b_im
    bb_im = coef_re * b_im + coef_im * b_re
    return ab_re, ab_im, bb_re, bb_im


def _same_group(gl):
    return jnp.eye(gl, dtype=bool)[None, :, None, :, None]


def _super_groups_in(bb, gl):
    g, p, i = bb.shape
    blocks = bb.reshape(g // gl, gl, p, i).transpose(0, 1, 3, 2)[:, :, :, None, :]
    return jnp.where(_same_group(gl), blocks, 0.0).reshape(g // gl, gl * i, gl * p)


def _super_groups_in_take(dense, gl, p, i):
    n_sg = dense.shape[0]
    blocks = jnp.where(_same_group(gl), dense.reshape(n_sg, gl, i, gl, p), 0.0).sum(axis=3)
    return blocks.transpose(0, 1, 3, 2).reshape(n_sg * gl, p, i)


def _super_groups_out(cc, gl):
    g, i, p = cc.shape
    blocks = cc.reshape(g // gl, gl, i, p).transpose(0, 1, 3, 2)[:, :, :, None, :]
    return jnp.where(_same_group(gl), blocks, 0.0).reshape(g // gl, gl * p, gl * i)


def _super_groups_out_take(dense, gl, i, p):
    n_sg = dense.shape[0]
    blocks = jnp.where(_same_group(gl), dense.reshape(n_sg, gl, p, gl, i), 0.0).sum(axis=3)
    return blocks.transpose(0, 1, 3, 2).reshape(n_sg * gl, i, p)


def _perm_rows(z, n_seq):
    t, w = z.shape
    steps = t // n_seq // N_SEG
    return z.reshape(n_seq, N_SEG, steps, w).transpose(0, 2, 1, 3).reshape(t, w)


def _unperm_rows(z, n_seq):
    t, w = z.shape
    steps = t // n_seq // N_SEG
    return z.reshape(n_seq, steps, N_SEG, w).transpose(0, 2, 1, 3).reshape(t, w)


def kernel(*args):
    assert len(args) == len(INPUT_NAMES)
    inp = dict(zip(INPUT_NAMES, args))
    depth = inp["ffn1_w_in"].shape[0]
    assert depth == 1
    alpha = (2.0 * depth) ** 0.25

    n_seq, seq, d_model = inp["x"].shape
    t = n_seq * seq
    x = inp["x"].reshape(t, d_model)
    x16 = x.astype(BF16)
    p16 = inp["p"][0].reshape(t, -1).astype(BF16)
    target = inp["loss_target"].reshape(t, d_model)
    wts = {n: inp[n][0] if n in SHARDED else inp[n] for n in WEIGHTS}
    mom = {n: inp["m_" + n][0] if n in SHARDED else inp["m_" + n] for n in WEIGHTS}
    var = {n: inp["v_" + n][0] if n in SHARDED else inp["v_" + n] for n in WEIGHTS}

    full = {}

    def place(n, g):
        if n in ("ffn1_w_in", "ffn2_w_in"):
            full[n] = g.reshape((2, N_DEV // 2) + g.shape[1:])
        elif n in ("ffn1_w_out", "ffn2_w_out"):
            full[n] = g.reshape(N_DEV // 2, 2 * g.shape[1], g.shape[2])
        elif n in COL_SHARDED:
            full[n] = _join_cols(g)
        else:
            full[n] = g.reshape(N_DEV * g.shape[1], g.shape[2])

    w16 = dict(zip(GATHER_FIRST, _to_bf16([wts[n] for n in GATHER_FIRST], "cast_ffn1")))
    later = GATHER_MIX + GATHER_LAST
    w16.update(zip(later, _to_bf16([wts[n] for n in later], "cast_rest")))
    for n, g in zip(GATHER_FIRST, _all_gather([w16[n] for n in GATHER_FIRST], "gather_ffn1")):
        place(n, g)
    gather_mix, gather_mix_token = _exchange_start([w16[n][None] for n in GATHER_MIX], full["ffn1_w_out"],
                                                   "gather_mix_start")

    def row(v):
        return v.reshape(1, -1)

    _, n_groups, n_state = wts["ssm_lambda_re"].shape
    n_ch = wts["ssm_b_re"].shape[3]
    disc_in = tuple(wts[n][0] for n in ("ssm_lambda_re", "ssm_lambda_im", "ssm_log_dt", "ssm_b_re", "ssm_b_im"))
    (ab_re, ab_im, bb_re, bb_im), disc_vjp = jax.vjp(_discretise, *disc_in)
    a_re, a_im = row(ab_re), row(ab_im)
    gl = S5_CHANNELS_PER_STEP // n_ch
    b_sg_re = _super_groups_in(bb_re, gl).astype(BF16)
    b_sg_im = _super_groups_in(bb_im, gl).astype(BF16)
    c_sg_re = _super_groups_out(wts["ssm_c_re"][0], gl).astype(BF16)
    c_sg_im = _super_groups_out(-wts["ssm_c_im"][0], gl).astype(BF16)

    ws = wts["gmlp_w_s"][0]
    n_heads = ws.shape[0]
    d_gmlp = wts["gmlp_ln_g"].shape[1]
    causal = jnp.tril(jnp.ones((CHUNK, CHUNK), dtype=bool))
    ws_masked = jnp.where(causal[None], ws, 0.0).astype(BF16)
    ws_masked_t = ws_masked.transpose(0, 2, 1)
    bs_cols = jnp.repeat(wts["gmlp_b_s"][0].T, d_gmlp // n_heads, axis=1)
    d_ssm = n_groups * n_ch
    assert d_ssm == d_gmlp and d_model == 2 * d_ssm

    h1, a1 = _ffn_in(x16, full["ffn1_w_in"], "ffn1_in", token=gather_mix_token)
    r1, x1, x1_16 = _ffn_out_ln(a1, full["ffn1_w_out"], x, row(wts["ln1_g"]), row(wts["ln1_b"]), alpha,
                                "ffn1_out_ln")

    for n, g in zip(GATHER_MIX, _exchange_wait(gather_mix, [x1_16], "gather_mix_wait")):
        place(n, g)
    gather_last, token = _exchange_start([w16[n][None] for n in GATHER_LAST], full["mix_w_in"],
                                         "gather_ffn2_start")
    proj = _mm(x1_16, full["mix_w_in"], name="mix_in", token=token)
    za_p = _perm_rows(proj[:, :d_ssm], n_seq)
    za_p16 = za_p.astype(BF16)
    h_re, h_im, yssm = _s5_fwd(za_p16, b_sg_re, b_sg_im, a_re, a_im, c_sg_re, c_sg_im, n_seq, "s5_fwd")
    s5out_p = _s5_post_fwd(yssm, za_p, row(wts["ssm_d"]), full["ssm_glu_w"], row(wts["ssm_glu_b"]),
                           "s5_post")
    s5out = _unperm_rows(s5out_p, n_seq)
    gm = _gmlp_fwd(proj, row(wts["gmlp_ln_g"]), row(wts["gmlp_ln_b"]), ws_masked, bs_cols, "gmlp")
    m_mix, y_a, y_b = _merge_fwd(s5out, gm, proj, full["up_a"], full["up_b"], "merge")
    r2, x2, x2_16 = _ffn_out_ln(m_mix[None], full["mix_w_out"][None], x1, row(wts["ln2_g"]), row(wts["ln2_b"]),
                                alpha, "mix_out_ln2", scale=1.0)

    for n, g in zip(GATHER_LAST, _exchange_wait(gather_last, [x2_16], "gather_ffn2_wait")):
        place(n, g)
    h2, a2 = _ffn_in(x2_16, full["ffn2_w_in"], "ffn2_in")
    r3, x3, _ = _ffn_out_ln(a2, full["ffn2_w_out"], x2, row(wts["ln3_g"]), row(wts["ln3_b"]), alpha,
                            "ffn2_out_ln")

    small = {}
    send = {}

    def lane_dense(n, v):
        return v.reshape(v.shape[:2] + (-1,)) if n in ("ssm_b_re", "ssm_b_im") else v

    def on_wire(n, g):
        g = lane_dense(n, g)
        return (g.astype(BF16) if n in SMALL_BF16 else g)[None]
    loss_parts, dr3, dr3_h, dw_gate, dw_proj, dg, db = _head(
        x3, r3, row(wts["ln3_g"]), p16, target, full["ple_w_gate"], full["ple_w_proj"], 0.5, "head")
    loss_mine = jnp.full((1, 1, 8, 128), jnp.sum(loss_parts[::8, 0]), F32)
    send["ple_w_gate"] = dw_gate.astype(BF16).reshape(N_DEV, -1, d_model)
    send["ple_w_proj"] = _split_cols(dw_proj.astype(BF16))
    small["ln3_g"], small["ln3_b"] = dg.sum(0, keepdims=True), db.sum(0, keepdims=True)
    dh2 = _ffn_out_dx(dr3_h, full["ffn2_w_out"], h2, "ffn2_out_dx")
    dw = _ffn_out_dw(a2, dr3_h, "ffn2_out_dw")
    send["ffn2_w_out"] = dw.reshape(N_DEV, -1, d_model)
    dw = _ffn_in_dw(x2_16, dh2, "ffn2_in_dw")
    send["ffn2_w_in"] = dw.reshape((N_DEV,) + dw.shape[2:])
    group1 = ("ffn2_w_in", "ffn2_w_out", "ple_w_gate", "ple_w_proj")
    exchange1, token = _exchange_start(
        [send[n] for n in group1] + [on_wire(n, small[n]) for n in SMALL_HEAD] + [loss_mine], dh2,
        "grad_exchange1_start")
    dr2, dr2_h, dg, db = _ffn_in_dx_ln(dh2, full["ffn2_w_in"], r2, dr3, alpha, row(wts["ln2_g"]), 1.0,
                                       "ffn2_in_dx_ln2_bwd", token=token)
    small["ln2_g"], small["ln2_b"] = dg.sum(0, keepdims=True), db.sum(0, keepdims=True)
    dm = _mm(dr2_h, full["mix_w_out"], tb=True, name="mix_out_dx")
    send["mix_w_out"] = _mm(m_mix, dr2_h, ta=True, name="mix_out_dw", out_dtype=BF16).reshape(
        N_DEV, -1, d_model)
    dga, dgb, ds5out, dgm, dw_a, dw_b = _merge_bwd(
        dm, y_a, y_b, s5out, gm, proj, full["up_a"], full["up_b"], "merge_bwd")
    send["up_a"] = _split_cols(dw_a.astype(BF16))
    send["up_b"] = _split_cols(dw_b.astype(BF16))

    dzu, dzv, dws, dbs_cols, dg, db = _gmlp_bwd(
        proj, dgm, row(wts["gmlp_ln_g"]), row(wts["gmlp_ln_b"]), ws_masked, ws_masked_t, bs_cols,
        "gmlp_bwd")
    small["gmlp_ln_g"], small["gmlp_ln_b"] = dg.sum(0, keepdims=True), db.sum(0, keepdims=True)
    small["gmlp_w_s"] = jnp.where(causal[None], dws, 0.0)[None]
    small["gmlp_b_s"] = dbs_cols.reshape(CHUNK, n_heads, -1).sum(-1).T[None]

    ds5out_p = _perm_rows(ds5out, n_seq)
    dy_p, dza_skip, dw_glu, dd, dgb_glu = _s5_post_bwd(
        yssm, za_p, ds5out_p, row(wts["ssm_d"]), full["ssm_glu_w"], row(wts["ssm_glu_b"]),
        "s5_post_bwd")
    send["ssm_glu_w"] = dw_glu.astype(BF16).reshape(N_DEV, -1, d_ssm)
    small["ssm_d"], small["ssm_glu_b"] = dd.sum(0, keepdims=True), dgb_glu.sum(0, keepdims=True)
    dza_p, dbb_re, dbb_im, dc_re, dc_im, da_re, da_im = _s5_bwd(
        dy_p, dza_skip, h_re, h_im, za_p16, b_sg_re, b_sg_im, a_re, a_im, c_sg_re, c_sg_im, n_seq, "s5_bwd")
    small["ssm_c_re"] = _super_groups_out_take(dc_re, gl, n_ch, n_state)[None]
    small["ssm_c_im"] = -_super_groups_out_take(dc_im, gl, n_ch, n_state)[None]
    dbb_re = _super_groups_in_take(dbb_re, gl, n_state, n_ch)
    dbb_im = _super_groups_in_take(dbb_im, gl, n_state, n_ch)
    dab_re = da_re.sum(0).reshape(n_groups, n_state)
    dab_im = da_im.sum(0).reshape(n_groups, n_state)
    for n, g in zip(("ssm_lambda_re", "ssm_lambda_im", "ssm_log_dt", "ssm_b_re", "ssm_b_im"),
                    disc_vjp((dab_re, dab_im, dbb_re, dbb_im))):
        small[n] = g[None]

    dproj = jnp.concatenate([_unperm_rows(dza_p, n_seq), dzu, dzv, dga, dgb], axis=1)
    group2 = ("mix_w_out", "up_a", "up_b", "ssm_glu_w")
    exchange2, token = _exchange_start(
        [send[n] for n in group2] + [on_wire(n, small[n]) for n in SMALL_MID], dproj, "grad_exchange2_start")
    send["mix_w_in"] = _split_cols(_mm(x1_16, dproj, ta=True, name="mix_in_dw", out_dtype=BF16, token=token))
    exchange3, token = _exchange_start([send["mix_w_in"]], dproj, "grad_exchange3_start")
    dr1, dr1_h, dg, db = _mm_ln_bwd(dproj, full["mix_w_in"], r1, dr2, alpha, row(wts["ln1_g"]), 0.5,
                                    "mix_in_dx_ln1_bwd", token=token)
    small["ln1_g"], small["ln1_b"] = dg.sum(0, keepdims=True), db.sum(0, keepdims=True)
    dw = _ffn_out_dw(a1, dr1_h, "ffn1_out_dw")
    send["ffn1_w_out"] = dw.reshape(N_DEV, -1, d_model)
    exchange4, token = _exchange_start([send["ffn1_w_out"]] + [on_wire(n, small[n]) for n in SMALL_LATE], dr1_h,
                                       "grad_exchange4_start")
    dh1 = _ffn_out_dx(dr1_h, full["ffn1_w_out"], h1, "ffn1_out_dx", token=token)
    dw = _ffn_in_dw(x16, dh1, "ffn1_in_dw")
    send["ffn1_w_in"] = dw.reshape((N_DEV,) + dw.shape[2:])
    exchange5, token = _exchange_start([send["ffn1_w_in"]], dh1, "grad_exchange5_start")
    grad_x = _ffn_in_dx(dh1, full["ffn1_w_in"], dr1, alpha, "ffn1_in_dx", token=token)

    results = {}

    def update(names, recv, prefix):
        for n, r in zip(names, recv):
            results[n] = _adamw(r, wts[n], mom[n], var[n], prefix + n)

    def update_small(names, recv, name):
        items = [(r, lane_dense(n, wts[n]), lane_dense(n, mom[n]), lane_dense(n, var[n])) for n, r in zip(names, recv)]
        for n, outs in zip(names, _adamw_small(items, name)):
            results[n] = tuple(o.reshape(wts[n].shape) for o in outs)

    def done(names):
        return [results[n][3] for n in names]

    recv = _exchange_wait(exchange1, [grad_x], "grad_exchange1_wait")
    update(group1, recv[:len(group1)], "adamw_")
    update_small(SMALL_HEAD, recv[len(group1):-1], "adamw_ln3")
    loss = jnp.sum(recv[-1][:, 0, 0, 0])
    recv = _exchange_wait(exchange2, done(group1 + SMALL_HEAD), "grad_exchange2_wait")
    update(group2, recv[:len(group2)], "adamw_")
    update_small(SMALL_MID, recv[len(group2):], "adamw_small")
    recv = _exchange_wait(exchange3, done(group2 + SMALL_MID), "grad_exchange3_wait")
    update(("mix_w_in",), recv, "adamw_")
    recv = _exchange_wait(exchange4, done(("mix_w_in",)), "grad_exchange4_wait")
    update(("ffn1_w_out",), recv[:1], "adamw_")
    update_small(SMALL_LATE, recv[1:], "adamw_ln1")
    recv = _exchange_wait(exchange5, done(("ffn1_w_out",) + SMALL_LATE), "grad_exchange5_wait")
    update(("ffn1_w_in",), recv, "adamw_")

    outs = [loss, grad_x.reshape(n_seq, seq, d_model)]
    for k in range(4):
        outs += [results[n][k][None] if n in SHARDED else results[n][k] for n in WEIGHTS]
    return tuple(outs)
```

```python
import functools
import math

import jax
import jax.numpy as jnp
from jax import lax
from jax.experimental import pallas as pl
from jax.experimental.pallas import tpu as pltpu

F32 = jnp.float32
BF16 = jnp.bfloat16

N_DEV = 8
LN_EPS = 1e-5
CHUNK = 128
N_SEG = 8
S5_CHANNELS_PER_STEP = 128
VMEM_LIMIT_BYTES = 56 * 1024 * 1024
MM_OPERAND_BLOCK_BYTES = 8 * 1024 * 1024
ADAM_BLOCK_BYTES = 6 * 1024 * 1024

ADAM_LR = 0.001
ADAM_B1 = 0.9
ADAM_B2 = 0.999
ADAM_EPS = 1e-08
ADAM_WD = 0.01
ADAM_STEP = 10

COL_SHARDED = ("ffn1_w_in", "mix_w_in", "up_a", "up_b", "ffn2_w_in", "ple_w_proj")
ROW_SHARDED = ("ffn1_w_out", "ssm_glu_w", "mix_w_out", "ffn2_w_out", "ple_w_gate")
SHARDED = ("ffn1_w_in", "ffn1_w_out", "mix_w_in", "ssm_glu_w", "up_a", "up_b", "mix_w_out",
           "ffn2_w_in", "ffn2_w_out", "ple_w_proj", "ple_w_gate")
WEIGHTS = ("ffn1_w_in", "ffn1_w_out", "ln1_g", "ln1_b", "mix_w_in", "ssm_lambda_re", "ssm_lambda_im",
           "ssm_log_dt", "ssm_b_re", "ssm_b_im", "ssm_c_re", "ssm_c_im", "ssm_d", "ssm_glu_w",
           "ssm_glu_b", "gmlp_ln_g", "gmlp_ln_b", "gmlp_w_s", "gmlp_b_s", "up_a", "up_b", "mix_w_out",
           "ln2_g", "ln2_b", "ffn2_w_in", "ffn2_w_out", "ln3_g", "ln3_b", "ple_w_proj", "ple_w_gate")
GATHER_FIRST = ("ffn1_w_in",)
GATHER_FFN1_OUT = ("ffn1_w_out",)
GATHER_MIX = ("mix_w_in",)
GATHER_MID = ("ssm_glu_w", "up_a", "up_b", "mix_w_out")
GATHER_LAST = ("ffn2_w_in", "ffn2_w_out", "ple_w_proj", "ple_w_gate")
SMALL = tuple(n for n in WEIGHTS if n not in SHARDED)
SMALL_HEAD = ("ln3_g", "ln3_b")
SMALL_LATE = ("ln1_g", "ln1_b")
SMALL_MID = tuple(n for n in SMALL if n not in SMALL_HEAD + SMALL_LATE)
SMALL_BF16 = ("gmlp_w_s", "ssm_b_re", "ssm_b_im", "ssm_c_re", "ssm_c_im")
INPUT_NAMES = ("x", "p") + WEIGHTS + ("loss_target",) + tuple("m_" + n for n in WEIGHTS) + tuple(
    "v_" + n for n in WEIGHTS)


def _pick(dim, pref, mult=128):
    if dim <= pref:
        return dim
    d = (pref // mult) * mult
    while d >= mult:
        if dim % d == 0:
            return d
        d -= mult
    return dim


def _params(n_axes=1):
    return pltpu.CompilerParams(dimension_semantics=("arbitrary",) * n_axes,
                                vmem_limit_bytes=VMEM_LIMIT_BYTES)


def _sigmoid(v):
    return 1.0 / (1.0 + jnp.exp(-v))


_GELU_C = math.sqrt(2.0 / math.pi)


def _gelu(v):
    return 0.5 * v * (1.0 + jnp.tanh(_GELU_C * (v + 0.044715 * (v * v * v))))


def _gelu_grad(v):
    t = jnp.tanh(_GELU_C * (v + 0.044715 * (v * v * v)))
    return 0.5 * (1.0 + t) + 0.5 * v * (1.0 - t * t) * (_GELU_C * (1.0 + 3.0 * 0.044715 * v * v))


def _ln_stats(r):
    mu = jnp.mean(r, axis=-1, keepdims=True)
    xc = r - mu
    var = jnp.mean(xc * xc, axis=-1, keepdims=True)
    rstd = lax.rsqrt(var + LN_EPS)
    return xc * rstd, rstd


def _ln_bwd(dy, xhat, rstd, g):
    dxh = dy * g
    m1 = jnp.mean(dxh, axis=-1, keepdims=True)
    m2 = jnp.mean(dxh * xhat, axis=-1, keepdims=True)
    return rstd * (dxh - m1 - xhat * m2)


def _fold8(v):
    rows, w = v.shape
    return jnp.sum(v.reshape(rows // 8, 8, w), axis=0)


def _dot(a, b, ta=False, tb=False):
    dn = (((0 if ta else 1,), (1 if tb else 0,)), ((), ()))
    return lax.dot_general(a.astype(BF16), b.astype(BF16), dn, preferred_element_type=F32)


_TOKEN = pl.BlockSpec((8, 128), lambda *_: (0, 0))


def _mm(a, b, *, name, ta=False, tb=False, out_dtype=F32, add=None, add_scale=1.0, token=None,
        tm=2048, tn=512, tk=4096):
    m_dim = a.shape[1] if ta else a.shape[0]
    k_dim = a.shape[0] if ta else a.shape[1]
    n_dim = b.shape[0] if tb else b.shape[1]
    assert (b.shape[1] if tb else b.shape[0]) == k_dim, (name, a.shape, b.shape)
    tk = _pick(k_dim, tk)
    tm = min(tm, max(256, MM_OPERAND_BLOCK_BYTES // (tk * a.dtype.itemsize)))
    tm, tn = _pick(m_dim, tm), _pick(n_dim, tn)
    nk = k_dim // tk
    has_add = add is not None

    def finish(r, add_ref, o_ref):
        if has_add:
            r = r + add_scale * add_ref[...].astype(F32)
        o_ref[...] = r.astype(out_dtype)

    def body(*refs):
        a_ref, b_ref = refs[:2]
        add_ref = refs[2] if has_add else None
        o_ref = refs[n_in]
        if nk == 1:
            finish(_dot(a_ref[...], b_ref[...], ta, tb), add_ref, o_ref)
            return
        acc_ref = refs[-1]
        k = pl.program_id(2)

        @pl.when(k == 0)
        def _():
            acc_ref[...] = jnp.zeros_like(acc_ref)

        acc_ref[...] += _dot(a_ref[...], b_ref[...], ta, tb)

        @pl.when(k == nk - 1)
        def _():
            finish(acc_ref[...], add_ref, o_ref)

    a_spec = (pl.BlockSpec((tk, tm), lambda i, j, k: (k, i)) if ta
              else pl.BlockSpec((tm, tk), lambda i, j, k: (i, k)))
    b_spec = (pl.BlockSpec((tn, tk), lambda i, j, k: (j, k)) if tb
              else pl.BlockSpec((tk, tn), lambda i, j, k: (k, j)))
    in_specs = [a_spec, b_spec]
    operands = [a, b]
    if has_add:
        in_specs.append(pl.BlockSpec((tm, tn), lambda i, j, k: (i, j)))
        operands.append(add)
    if token is not None:
        in_specs.append(_TOKEN)
        operands.append(token)
    n_in = len(operands)
    return pl.pallas_call(
        body, name=name,
        out_shape=jax.ShapeDtypeStruct((m_dim, n_dim), out_dtype),
        grid=(m_dim // tm, n_dim // tn, nk),
        in_specs=in_specs,
        out_specs=pl.BlockSpec((tm, tn), lambda i, j, k: (i, j)),
        scratch_shapes=[pltpu.VMEM((tm, tn), F32)] if nk > 1 else [],
        compiler_params=_params(3),
    )(*operands)


def _to_bf16(arrays, name):
    n = len(arrays)

    def body(*refs):
        for src, dst in zip(refs[:n], refs[n:]):
            dst[...] = src[...].astype(BF16)

    vmem = pl.BlockSpec(memory_space=pltpu.VMEM)
    return pl.pallas_call(
        body, name=name, out_shape=tuple(jax.ShapeDtypeStruct(a.shape, BF16) for a in arrays),
        in_specs=[vmem] * n, out_specs=tuple([vmem] * n),
        compiler_params=pltpu.CompilerParams(vmem_limit_bytes=VMEM_LIMIT_BYTES))(*arrays)


def _rows(tr, w, cb=0):
    return pl.BlockSpec((tr, w), lambda i: (i, cb))


def _whole(shape):
    nd = len(shape)
    return pl.BlockSpec(tuple(shape), lambda i: (0,) * nd)


def _ffn_in(x16, w_in, name, token=None):
    t, d = x16.shape
    _, nb, _, fb = w_in.shape
    tm = _pick(t, 1024, 8)
    extra = [] if token is None else [token]

    def body(*refs):
        x_ref, wg_ref, wu_ref = refs[:3]
        h_ref, a_ref = refs[-2:]
        xv = x_ref[...]
        g = _dot(xv, wg_ref[0, 0])
        u = _dot(xv, wu_ref[0, 0])
        h_ref[0, 0] = g.astype(BF16)
        h_ref[0, 1] = u.astype(BF16)
        a_ref[0] = (g * _sigmoid(g) * u).astype(BF16)

    return pl.pallas_call(
        body, name=name,
        out_shape=(jax.ShapeDtypeStruct((nb, 2, t, fb), BF16), jax.ShapeDtypeStruct((nb, t, fb), BF16)),
        grid=(t // tm, nb),
        in_specs=[pl.BlockSpec((tm, d), lambda i, j: (i, 0)),
                  pl.BlockSpec((1, 1, d, fb), lambda i, j: (0, j, 0, 0)),
                  pl.BlockSpec((1, 1, d, fb), lambda i, j: (1, j, 0, 0))] + [_TOKEN] * len(extra),
        out_specs=(pl.BlockSpec((1, 2, tm, fb), lambda i, j: (j, 0, i, 0)),
                   pl.BlockSpec((1, tm, fb), lambda i, j: (j, i, 0))),
        compiler_params=_params(2))(x16, w_in, w_in, *extra)


def _ffn_out_ln(a, w_out, xin, g, b, alpha, name, scale=0.5):
    nb, t, fb = a.shape
    d = w_out.shape[2]
    tm = _pick(t, 512, 8)

    def body(a_ref, w_ref, x_ref, g_ref, b_ref, r_ref, y_ref, y16_ref):
        f = _dot(a_ref[0], w_ref[0])
        for jb in range(1, nb):
            f = f + _dot(a_ref[jb], w_ref[jb])
        r = alpha * x_ref[...] + scale * f
        xhat, _ = _ln_stats(r)
        y = xhat * g_ref[...] + b_ref[...]
        r_ref[...] = r
        y_ref[...] = y
        y16_ref[...] = y.astype(BF16)

    return pl.pallas_call(
        body, name=name,
        out_shape=(jax.ShapeDtypeStruct((t, d), F32), jax.ShapeDtypeStruct((t, d), F32),
                   jax.ShapeDtypeStruct((t, d), BF16)),
        grid=(t // tm,),
        in_specs=[pl.BlockSpec((nb, tm, fb), lambda i: (0, i, 0)), _whole(w_out.shape), _rows(tm, d),
                  _whole((1, d)), _whole((1, d))],
        out_specs=(_rows(tm, d), _rows(tm, d), _rows(tm, d)),
        compiler_params=_params())(a, w_out, xin, g, b)


def _ffn_out_dx(drs, w_out, h, name, token=None):
    nb, _, t, fb = h.shape
    d = w_out.shape[2]
    tm = _pick(t, 1024, 8)
    extra = [] if token is None else [token]

    def body(*refs):
        dr_ref, w_ref, h_ref, dh_ref = refs[0], refs[1], refs[2], refs[-1]
        da = _dot(dr_ref[...], w_ref[0], tb=True)
        g, u = h_ref[0, 0].astype(F32), h_ref[0, 1].astype(F32)
        sg = _sigmoid(g)
        dh_ref[0, 0] = (da * u * (sg * (1.0 + g * (1.0 - sg)))).astype(BF16)
        dh_ref[0, 1] = (da * (g * sg)).astype(BF16)

    return pl.pallas_call(
        body, name=name, out_shape=jax.ShapeDtypeStruct((nb, 2, t, fb), BF16), grid=(t // tm, nb),
        in_specs=[pl.BlockSpec((tm, d), lambda i, j: (i, 0)),
                  pl.BlockSpec((1, fb, d), lambda i, j: (j, 0, 0)),
                  pl.BlockSpec((1, 2, tm, fb), lambda i, j: (j, 0, i, 0))] + [_TOKEN] * len(extra),
        out_specs=pl.BlockSpec((1, 2, tm, fb), lambda i, j: (j, 0, i, 0)),
        compiler_params=_params(2))(drs, w_out, h, *extra)


def _ffn_out_dw(a, drs, name):
    nb, t, fb = a.shape
    d = drs.shape[1]

    def body(a_ref, dr_ref, o_ref):
        o_ref[0] = _dot(a_ref[0], dr_ref[...], ta=True).astype(BF16)

    return pl.pallas_call(
        body, name=name, out_shape=jax.ShapeDtypeStruct((nb, fb, d), BF16), grid=(nb,),
        in_specs=[pl.BlockSpec((1, t, fb), lambda j: (j, 0, 0)), pl.BlockSpec((t, d), lambda j: (0, 0))],
        out_specs=pl.BlockSpec((1, fb, d), lambda j: (j, 0, 0)),
        compiler_params=_params())(a, drs)


def _ffn_in_dw(x16, dh, name, token=None):
    t, d = x16.shape
    nb, _, _, fb = dh.shape
    extra = [] if token is None else [token]

    def body(*refs):
        x_ref, dh_ref, o_ref = refs[0], refs[1], refs[-1]
        o_ref[0, 0] = _dot(x_ref[...], dh_ref[0, 0], ta=True).astype(BF16)

    return pl.pallas_call(
        body, name=name, out_shape=jax.ShapeDtypeStruct((2, nb, d, fb), BF16), grid=(nb, 2),
        in_specs=[pl.BlockSpec((t, d), lambda j, s: (0, 0)),
                  pl.BlockSpec((1, 1, t, fb), lambda j, s: (j, s, 0, 0))] + [_TOKEN] * len(extra),
        out_specs=pl.BlockSpec((1, 1, d, fb), lambda j, s: (s, j, 0, 0)),
        compiler_params=_params(2))(x16, dh, *extra)


def _ffn_in_dx(dh, w_in, add, add_scale, name, token=None):
    nb, _, t, fb = dh.shape
    d = w_in.shape[2]
    tm = _pick(t, 512, 8)
    has_add = add is not None
    extra = [] if token is None else [token]

    def body(*refs):
        dh_ref, w_ref = refs[:2]
        o_ref = refs[-1]
        acc = None
        for jb in range(nb):
            for s in range(2):
                part = _dot(dh_ref[jb, s], w_ref[s, jb], tb=True)
                acc = part if acc is None else acc + part
        if has_add:
            acc = acc + add_scale * refs[2][...]
        o_ref[...] = acc

    return pl.pallas_call(
        body, name=name, out_shape=jax.ShapeDtypeStruct((t, d), F32), grid=(t // tm,),
        in_specs=[pl.BlockSpec((nb, 2, tm, fb), lambda i: (0, 0, i, 0)), _whole(w_in.shape)]
        + ([_rows(tm, d)] if has_add else []) + [_TOKEN] * len(extra),
        out_specs=_rows(tm, d),
        compiler_params=_params())(*([dh, w_in] + ([add] if has_add else []) + extra))


def _ffn_in_dx_ln(dh, w_in, r, dy_b, b_scale, ln_g, out_scale, name, token=None):
    nb, _, t, fb = dh.shape
    d = w_in.shape[2]
    tm = _pick(t, 512, 8)
    extra = [] if token is None else [token]

    def body(*refs):
        dh_ref, w_ref, r_ref, dyb_ref, g_ref = refs[:5]
        dr_ref, drs_ref, dg_ref, dbeta_ref = refs[-4:]
        dy = b_scale * dyb_ref[...]
        for jb in range(nb):
            for s in range(2):
                dy = dy + _dot(dh_ref[jb, s], w_ref[s, jb], tb=True)
        xhat, rstd = _ln_stats(r_ref[...])
        dr = _ln_bwd(dy, xhat, rstd, g_ref[...])
        dr_ref[...] = dr
        drs_ref[...] = (out_scale * dr).astype(BF16)

        @pl.when(pl.program_id(0) == 0)
        def _():
            dg_ref[...] = jnp.zeros_like(dg_ref)
            dbeta_ref[...] = jnp.zeros_like(dbeta_ref)

        dg_ref[...] += _fold8(dy * xhat)
        dbeta_ref[...] += _fold8(dy)

    return pl.pallas_call(
        body, name=name,
        out_shape=(jax.ShapeDtypeStruct((t, d), F32), jax.ShapeDtypeStruct((t, d), BF16),
                   jax.ShapeDtypeStruct((8, d), F32), jax.ShapeDtypeStruct((8, d), F32)),
        grid=(t // tm,),
        in_specs=[pl.BlockSpec((nb, 2, tm, fb), lambda i: (0, 0, i, 0)), _whole(w_in.shape), _rows(tm, d),
                  _rows(tm, d), _whole((1, d))] + [_TOKEN] * len(extra),
        out_specs=(_rows(tm, d), _rows(tm, d), _whole((8, d)), _whole((8, d))),
        compiler_params=_params())(dh, w_in, r, dy_b, ln_g, *extra)


def _mm_ln_bwd(a, w, r, dy_b, b_scale, ln_g, out_scale, name, token=None):
    t, k_dim = a.shape
    d = w.shape[0]
    tm = _pick(t, 512, 8)
    extra = [] if token is None else [token]

    def body(*refs):
        a_ref, w_ref, r_ref, dyb_ref, g_ref = refs[:5]
        dr_ref, drs_ref, dg_ref, dbeta_ref = refs[-4:]
        dy = _dot(a_ref[...], w_ref[...], tb=True) + b_scale * dyb_ref[...]
        xhat, rstd = _ln_stats(r_ref[...])
        dr = _ln_bwd(dy, xhat, rstd, g_ref[...])
        dr_ref[...] = dr
        drs_ref[...] = (out_scale * dr).astype(BF16)

        @pl.when(pl.program_id(0) == 0)
        def _():
            dg_ref[...] = jnp.zeros_like(dg_ref)
            dbeta_ref[...] = jnp.zeros_like(dbeta_ref)

        dg_ref[...] += _fold8(dy * xhat)
        dbeta_ref[...] += _fold8(dy)

    return pl.pallas_call(
        body, name=name,
        out_shape=(jax.ShapeDtypeStruct((t, d), F32), jax.ShapeDtypeStruct((t, d), BF16),
                   jax.ShapeDtypeStruct((8, d), F32), jax.ShapeDtypeStruct((8, d), F32)),
        grid=(t // tm,),
        in_specs=[_rows(tm, k_dim), _whole(w.shape), _rows(tm, d), _rows(tm, d), _whole((1, d))]
        + [_TOKEN] * len(extra),
        out_specs=(_rows(tm, d), _rows(tm, d), _whole((8, d)), _whole((8, d))),
        compiler_params=_params())(a, w, r, dy_b, ln_g, *extra)


def _take_row(v, row_id, s):
    return jnp.sum(jnp.where(row_id == s, v, 0.0), axis=0, keepdims=True)


def _complex_power(ar, ai, n):
    out = None
    while n:
        if n & 1:
            out = (ar, ai) if out is None else (out[0] * ar - out[1] * ai, out[0] * ai + out[1] * ar)
        ar, ai = ar * ar - ai * ai, 2.0 * ar * ai
        n >>= 1
    return out


def _seg_carries(f_re, f_im, p_re, p_im, reverse):
    row_id = lax.broadcasted_iota(jnp.int32, f_re.shape, 0)
    p_re, p_im = _take_row(p_re, row_id, 0), _take_row(p_im, row_id, 0)
    out_re, out_im = jnp.zeros_like(f_re), jnp.zeros_like(f_im)
    c_re, c_im = jnp.zeros_like(p_re), jnp.zeros_like(p_im)
    order = range(N_SEG - 1, -1, -1) if reverse else range(N_SEG)
    for s in order:
        out_re = jnp.where(row_id == s, c_re, out_re)
        out_im = jnp.where(row_id == s, c_im, out_im)
        fr, fi = _take_row(f_re, row_id, s), _take_row(f_im, row_id, s)
        c_re, c_im = fr + p_re * c_re - p_im * c_im, fi + p_re * c_im + p_im * c_re
    return out_re, out_im


def _s5_fwd(za16, b_re, b_im, a_re, a_im, c_re, c_im, n_seq, name):
    t = za16.shape[0]
    n_sg, ch, st = b_re.shape
    seq = t // n_seq
    steps = seq // N_SEG

    def body(za_ref, bre_ref, bim_ref, are, aim, cre_ref, cim_ref, hre, him, y_ref):
        za = za_ref[...]
        hre[...] = _dot(za, bre_ref[0])
        him[...] = _dot(za, bim_ref[0])
        ar = jnp.broadcast_to(are[...], (N_SEG, st))
        ai = jnp.broadcast_to(aim[...], (N_SEG, st))
        zero = jnp.zeros((N_SEG, st), F32)

        def local(k, carry):
            lr, li = carry
            rows = pl.ds(pl.multiple_of(k * N_SEG, N_SEG), N_SEG)
            nr = ar * lr - ai * li + hre[rows, :]
            ni = ar * li + ai * lr + him[rows, :]
            hre[rows, :] = nr
            him[rows, :] = ni
            return nr, ni

        f_re, f_im = lax.fori_loop(0, steps, local, (zero, zero))
        c_re, c_im = _seg_carries(f_re, f_im, *_complex_power(ar, ai, steps), reverse=False)

        def fix(k, carry):
            qr, qi = carry
            rows = pl.ds(pl.multiple_of(k * N_SEG, N_SEG), N_SEG)
            hre[rows, :] = hre[rows, :] + qr
            him[rows, :] = him[rows, :] + qi
            return ar * qr - ai * qi, ar * qi + ai * qr

        lax.fori_loop(0, steps, fix, (ar * c_re - ai * c_im, ar * c_im + ai * c_re))
        y_ref[...] = _dot(hre[...], cre_ref[0]) + _dot(him[...], cim_ref[0])

    rows_ch = pl.BlockSpec((seq, ch), lambda s, j: (s, j))
    rows_st = pl.BlockSpec((seq, st), lambda s, j: (s, j))
    vec = pl.BlockSpec((1, st), lambda s, j: (0, j))
    b_blk = pl.BlockSpec((1, ch, st), lambda s, j: (j, 0, 0))
    c_blk = pl.BlockSpec((1, st, ch), lambda s, j: (j, 0, 0))
    return pl.pallas_call(
        body, name=name,
        out_shape=(jax.ShapeDtypeStruct((t, n_sg * st), F32), jax.ShapeDtypeStruct((t, n_sg * st), F32),
                   jax.ShapeDtypeStruct((t, n_sg * ch), F32)),
        grid=(n_seq, n_sg), in_specs=[rows_ch, b_blk, b_blk, vec, vec, c_blk, c_blk],
        out_specs=(rows_st, rows_st, rows_ch),
        compiler_params=_params(2))(za16, b_re, b_im, a_re, a_im, c_re, c_im)


def _s5_bwd(dy16, dza_skip, h_re, h_im, za16, b_re, b_im, a_re, a_im, c_re, c_im, n_seq, name):
    t = dy16.shape[0]
    n_sg, ch, st = b_re.shape
    seq = t // n_seq
    steps = seq // N_SEG

    def body(dy_ref, skip_ref, hre, him, za_ref, bre_ref, bim_ref, are, aim, cre_ref, cim_ref,
             dza_ref, dbre_ref, dbim_ref, dcre_ref, dcim_ref, dare, daim, lre, lim):
        dy = dy_ref[...]
        lre[...] = _dot(dy, cre_ref[0], tb=True)
        lim[...] = _dot(dy, cim_ref[0], tb=True)
        ar = jnp.broadcast_to(are[...], (N_SEG, st))
        ai = -jnp.broadcast_to(aim[...], (N_SEG, st))
        zero = jnp.zeros((N_SEG, st), F32)

        def local(i, carry):
            lr, li = carry
            k = steps - 1 - i
            rows = pl.ds(pl.multiple_of(k * N_SEG, N_SEG), N_SEG)
            nr = ar * lr - ai * li + lre[rows, :]
            ni = ar * li + ai * lr + lim[rows, :]
            lre[rows, :] = nr
            lim[rows, :] = ni
            return nr, ni

        f_re, f_im = lax.fori_loop(0, steps, local, (zero, zero))
        c_re, c_im = _seg_carries(f_re, f_im, *_complex_power(ar, ai, steps), reverse=True)

        def accumulate(lam_r, lam_i, hp_r, hp_i, acc_r, acc_i):
            return acc_r + (lam_r * hp_r + lam_i * hp_i), acc_i + (lam_i * hp_r - lam_r * hp_i)

        def fix(i, carry):
            qr, qi, acc_r, acc_i = carry
            k = steps - 1 - i
            rows = pl.ds(pl.multiple_of(k * N_SEG, N_SEG), N_SEG)
            prev = pl.ds(pl.multiple_of((k - 1) * N_SEG, N_SEG), N_SEG)
            lam_r = lre[rows, :] + qr
            lam_i = lim[rows, :] + qi
            lre[rows, :] = lam_r
            lim[rows, :] = lam_i
            acc_r, acc_i = accumulate(lam_r, lam_i, hre[prev, :], him[prev, :], acc_r, acc_i)
            return ar * qr - ai * qi, ar * qi + ai * qr, acc_r, acc_i

        qr, qi, acc_r, acc_i = lax.fori_loop(
            0, steps - 1, fix, (ar * c_re - ai * c_im, ar * c_im + ai * c_re, zero, zero))
        first = pl.ds(0, N_SEG)
        last = pl.ds((steps - 1) * N_SEG, N_SEG)
        lam_r = lre[first, :] + qr
        lam_i = lim[first, :] + qi
        lre[first, :] = lam_r
        lim[first, :] = lam_i
        row_id = lax.broadcasted_iota(jnp.int32, (N_SEG, st), 0)
        hp_r = jnp.where(row_id == 0, 0.0, pltpu.roll(hre[last, :], 1, 0))
        hp_i = jnp.where(row_id == 0, 0.0, pltpu.roll(him[last, :], 1, 0))
        acc_r, acc_i = accumulate(lam_r, lam_i, hp_r, hp_i, acc_r, acc_i)

        lam_re16 = lre[...].astype(BF16)
        lam_im16 = lim[...].astype(BF16)
        dza_ref[...] = (_dot(lam_re16, bre_ref[0], tb=True) + _dot(lam_im16, bim_ref[0], tb=True)
                        + skip_ref[...]).astype(BF16)

        @pl.when(pl.program_id(1) == 0)
        def _():
            for ref in (dbre_ref, dbim_ref, dcre_ref, dcim_ref, dare, daim):
                ref[...] = jnp.zeros_like(ref)

        za = za_ref[...]
        dbre_ref[0] += _dot(za, lam_re16, ta=True)
        dbim_ref[0] += _dot(za, lam_im16, ta=True)
        dcre_ref[0] += _dot(hre[...], dy, ta=True)
        dcim_ref[0] += _dot(him[...], dy, ta=True)
        dare[...] += acc_r
        daim[...] += acc_i

    rows_ch = pl.BlockSpec((seq, ch), lambda j, s: (s, j))
    rows_st = pl.BlockSpec((seq, st), lambda j, s: (s, j))
    vec = pl.BlockSpec((1, st), lambda j, s: (0, j))
    b_blk = pl.BlockSpec((1, ch, st), lambda j, s: (j, 0, 0))
    c_blk = pl.BlockSpec((1, st, ch), lambda j, s: (j, 0, 0))
    acc = pl.BlockSpec((N_SEG, st), lambda j, s: (0, j))
    return pl.pallas_call(
        body, name=name,
        out_shape=(jax.ShapeDtypeStruct((t, n_sg * ch), BF16),
                   jax.ShapeDtypeStruct((n_sg, ch, st), F32), jax.ShapeDtypeStruct((n_sg, ch, st), F32),
                   jax.ShapeDtypeStruct((n_sg, st, ch), F32), jax.ShapeDtypeStruct((n_sg, st, ch), F32),
                   jax.ShapeDtypeStruct((N_SEG, n_sg * st), F32), jax.ShapeDtypeStruct((N_SEG, n_sg * st), F32)),
        grid=(n_sg, n_seq),
        in_specs=[rows_ch, rows_ch, rows_st, rows_st, rows_ch, b_blk, b_blk, vec, vec, c_blk, c_blk],
        out_specs=(rows_ch, b_blk, b_blk, c_blk, c_blk, acc, acc),
        scratch_shapes=[pltpu.VMEM((seq, st), F32), pltpu.VMEM((seq, st), F32)],
        compiler_params=_params(2))(dy16, dza_skip, h_re, h_im, za16, b_re, b_im, a_re, a_im, c_re, c_im)


def _s5_post_fwd(yssm, u, d_skip, glu_w, glu_b, name):
    t, w = yssm.shape
    tr = _pick(t, 512, 8)

    def body(y_ref, u_ref, d_ref, w_ref, b_ref, o_ref):
        yg = _gelu(y_ref[...] + d_ref[...] * u_ref[...])
        pre = _dot(yg, w_ref[...]) + b_ref[...]
        o_ref[...] = yg * _sigmoid(pre)

    return pl.pallas_call(
        body, name=name, out_shape=jax.ShapeDtypeStruct((t, w), F32), grid=(t // tr,),
        in_specs=[_rows(tr, w), _rows(tr, w), _whole((1, w)), _whole((w, w)), _whole((1, w))],
        out_specs=_rows(tr, w), compiler_params=_params())(yssm, u, d_skip, glu_w, glu_b)


def _s5_post_bwd(yssm, u, dout, d_skip, glu_w, glu_b, name):
    t, w = yssm.shape
    tr = _pick(t, 512, 8)

    def body(y_ref, u_ref, do_ref, d_ref, w_ref, b_ref, dy_ref, du_ref, dw_ref, dd_ref, db_ref):
        uu = u_ref[...]
        y = y_ref[...] + d_ref[...] * uu
        yg = _gelu(y)
        sg = _sigmoid(_dot(yg, w_ref[...]) + b_ref[...])
        do = do_ref[...]
        dpre = do * yg * sg * (1.0 - sg)
        dyg = do * sg + _dot(dpre, w_ref[...], tb=True)
        dy = dyg * _gelu_grad(y)
        dy_ref[...] = dy.astype(BF16)
        du_ref[...] = dy * d_ref[...]

        @pl.when(pl.program_id(0) == 0)
        def _():
            dw_ref[...] = jnp.zeros_like(dw_ref)
            dd_ref[...] = jnp.zeros_like(dd_ref)
            db_ref[...] = jnp.zeros_like(db_ref)

        dw_ref[...] += _dot(yg, dpre, ta=True)
        dd_ref[...] += _fold8(dy * uu)
        db_ref[...] += _fold8(dpre)

    return pl.pallas_call(
        body, name=name,
        out_shape=(jax.ShapeDtypeStruct((t, w), BF16), jax.ShapeDtypeStruct((t, w), F32),
                   jax.ShapeDtypeStruct((w, w), F32), jax.ShapeDtypeStruct((8, w), F32),
                   jax.ShapeDtypeStruct((8, w), F32)),
        grid=(t // tr,),
        in_specs=[_rows(tr, w), _rows(tr, w), _rows(tr, w), _whole((1, w)), _whole((w, w)),
                  _whole((1, w))],
        out_specs=(_rows(tr, w), _rows(tr, w), _whole((w, w)), _whole((8, w)), _whole((8, w))),
        compiler_params=_params())(yssm, u, dout, d_skip, glu_w, glu_b)


def _head_select(parts, head_dim):
    col_head = lax.broadcasted_iota(jnp.int32, parts[0].shape, 1) // head_dim
    out = jnp.zeros_like(parts[0])
    for h, part in enumerate(parts):
        out = jnp.where(col_head == h, part, out)
    return out


def _gmlp_fwd(proj, ln_g, ln_b, ws, bs_cols, name):
    t = proj.shape[0]
    n_heads = ws.shape[0]
    w = bs_cols.shape[1]
    head_dim = w // n_heads
    cpb = _pick(t // CHUNK, 4, 1)
    tr = cpb * CHUNK

    def body(zu_ref, zv_ref, g_ref, b_ref, ws_ref, bs_ref, o_ref):
        for c in range(cpb):
            rows = pl.ds(c * CHUNK, CHUNK)
            xhat, _ = _ln_stats(_gelu(zv_ref[rows, :]))
            vn = (xhat * g_ref[...] + b_ref[...]).astype(BF16)
            s = _head_select([_dot(ws_ref[h], vn) for h in range(n_heads)], head_dim) + bs_ref[...]
            o_ref[rows, :] = _gelu(zu_ref[rows, :]) * s

    return pl.pallas_call(
        body, name=name, out_shape=jax.ShapeDtypeStruct((t, w), F32), grid=(t // tr,),
        in_specs=[_rows(tr, w, 1), _rows(tr, w, 2), _whole((1, w)), _whole((1, w)),
                  _whole(ws.shape), _whole(bs_cols.shape)],
        out_specs=_rows(tr, w), compiler_params=_params())(proj, proj, ln_g, ln_b, ws, bs_cols)


def _gmlp_bwd(proj, dout, ln_g, ln_b, ws, ws_t, bs_cols, name):
    t = proj.shape[0]
    n_heads = ws.shape[0]
    w = bs_cols.shape[1]
    head_dim = w // n_heads
    cpb = _pick(t // CHUNK, 4, 1)
    tr = cpb * CHUNK

    def body(zu_ref, zv_ref, do_ref, g_ref, b_ref, ws_ref, wst_ref, bs_ref,
             dzu_ref, dzv_ref, dws_ref, dbs_ref, dg_ref, dbeta_ref):
        @pl.when(pl.program_id(0) == 0)
        def _():
            dws_ref[...] = jnp.zeros_like(dws_ref)
            dbs_ref[...] = jnp.zeros_like(dbs_ref)
            dg_ref[...] = jnp.zeros_like(dg_ref)
            dbeta_ref[...] = jnp.zeros_like(dbeta_ref)

        col_head = lax.broadcasted_iota(jnp.int32, (CHUNK, w), 1) // head_dim
        for c in range(cpb):
            rows = pl.ds(c * CHUNK, CHUNK)
            zu, zv, do = zu_ref[rows, :], zv_ref[rows, :], do_ref[rows, :]
            xhat, rstd = _ln_stats(_gelu(zv))
            vn = (xhat * g_ref[...] + b_ref[...]).astype(BF16)
            s = _head_select([_dot(ws_ref[h], vn) for h in range(n_heads)], head_dim) + bs_ref[...]
            dzu_ref[rows, :] = (do * s * _gelu_grad(zu)).astype(BF16)
            ds = do * _gelu(zu)
            ds16 = ds.astype(BF16)
            dbs_ref[...] += ds
            for h in range(n_heads):
                dws_ref[h] += _dot(jnp.where(col_head == h, ds16, jnp.zeros_like(ds16)), vn, tb=True)
            dvn = _head_select([_dot(wst_ref[h], ds16) for h in range(n_heads)], head_dim)
            dg_ref[...] += _fold8(dvn * xhat)
            dbeta_ref[...] += _fold8(dvn)
            dgv = _ln_bwd(dvn, xhat, rstd, g_ref[...])
            dzv_ref[rows, :] = (dgv * _gelu_grad(zv)).astype(BF16)

    return pl.pallas_call(
        body, name=name,
        out_shape=(jax.ShapeDtypeStruct((t, w), BF16), jax.ShapeDtypeStruct((t, w), BF16),
                   jax.ShapeDtypeStruct(ws.shape, F32), jax.ShapeDtypeStruct((CHUNK, w), F32),
                   jax.ShapeDtypeStruct((8, w), F32), jax.ShapeDtypeStruct((8, w), F32)),
        grid=(t // tr,),
        in_specs=[_rows(tr, w, 1), _rows(tr, w, 2), _rows(tr, w), _whole((1, w)), _whole((1, w)),
                  _whole(ws.shape), _whole(ws.shape), _whole(bs_cols.shape)],
        out_specs=(_rows(tr, w), _rows(tr, w), _whole(ws.shape), _whole((CHUNK, w)),
                   _whole((8, w)), _whole((8, w))),
        compiler_params=_params())(proj, proj, dout, ln_g, ln_b, ws, ws_t, bs_cols)


def _merge_fwd(s5out, gm, proj, up_a, up_b, name):
    t, w = s5out.shape
    d = up_a.shape[1]
    assert d == 2 * w
    tr = _pick(t, 512, 8)

    def body(s_ref, gm_ref, ga0, ga1, gb0, gb1, ua_ref, ub_ref, m_ref, ya_ref, yb_ref):
        ya = _dot(s_ref[...], ua_ref[...])
        yb = _dot(gm_ref[...], ub_ref[...])
        ya_ref[...] = ya.astype(BF16)
        yb_ref[...] = yb.astype(BF16)
        for half, (ga, gb) in enumerate(((ga0, gb0), (ga1, gb1))):
            cols = slice(half * w, (half + 1) * w)
            m_ref[:, cols] = (_sigmoid(ga[...]) * ya[:, cols] + _sigmoid(gb[...]) * yb[:, cols]).astype(BF16)

    return pl.pallas_call(
        body, name=name,
        out_shape=(jax.ShapeDtypeStruct((t, d), BF16), jax.ShapeDtypeStruct((t, d), BF16),
                   jax.ShapeDtypeStruct((t, d), BF16)),
        grid=(t // tr,),
        in_specs=[_rows(tr, w), _rows(tr, w), _rows(tr, w, 3), _rows(tr, w, 4), _rows(tr, w, 5),
                  _rows(tr, w, 6), _whole(up_a.shape), _whole(up_b.shape)],
        out_specs=(_rows(tr, d), _rows(tr, d), _rows(tr, d)),
        compiler_params=_params())(s5out, gm, proj, proj, proj, proj, up_a, up_b)


def _merge_bwd(dm, ya, yb, s5out, gm, proj, up_a, up_b, name):
    t, d = dm.shape
    w = d // 2
    tr = _pick(t, 512, 8)

    def body(dm_ref, ya_ref, yb_ref, s_ref, gm_ref, ga0, ga1, gb0, gb1, ua_ref, ub_ref,
             dga_ref, dgb_ref, ds_ref, dgm_ref, dua_ref, dub_ref):
        dm_v, ya, yb = dm_ref[...], ya_ref[...].astype(F32), yb_ref[...].astype(F32)
        dya, dyb = [], []
        for half, (ga, gb) in enumerate(((ga0, gb0), (ga1, gb1))):
            cols = slice(half * w, (half + 1) * w)
            sa, sb = _sigmoid(ga[...]), _sigmoid(gb[...])
            dmh = dm_v[:, cols]
            dga_ref[:, cols] = (dmh * ya[:, cols] * sa * (1.0 - sa)).astype(BF16)
            dgb_ref[:, cols] = (dmh * yb[:, cols] * sb * (1.0 - sb)).astype(BF16)
            dya.append((dmh * sa).astype(BF16))
            dyb.append((dmh * sb).astype(BF16))
        dya = jnp.concatenate(dya, axis=1)
        dyb = jnp.concatenate(dyb, axis=1)
        ds_ref[...] = _dot(dya, ua_ref[...], tb=True)
        dgm_ref[...] = _dot(dyb, ub_ref[...], tb=True)

        @pl.when(pl.program_id(0) == 0)
        def _():
            dua_ref[...] = jnp.zeros_like(dua_ref)
            dub_ref[...] = jnp.zeros_like(dub_ref)

        dua_ref[...] += _dot(s_ref[...], dya, ta=True)
        dub_ref[...] += _dot(gm_ref[...], dyb, ta=True)

    return pl.pallas_call(
        body, name=name,
        out_shape=(jax.ShapeDtypeStruct((t, d), BF16), jax.ShapeDtypeStruct((t, d), BF16),
                   jax.ShapeDtypeStruct((t, w), F32), jax.ShapeDtypeStruct((t, w), F32),
                   jax.ShapeDtypeStruct(up_a.shape, F32), jax.ShapeDtypeStruct(up_b.shape, F32)),
        grid=(t // tr,),
        in_specs=[_rows(tr, d), _rows(tr, d), _rows(tr, d), _rows(tr, w), _rows(tr, w),
                  _rows(tr, w, 3), _rows(tr, w, 4), _rows(tr, w, 5), _rows(tr, w, 6),
                  _whole(up_a.shape), _whole(up_b.shape)],
        out_specs=(_rows(tr, d), _rows(tr, d), _rows(tr, w), _rows(tr, w), _whole(up_a.shape),
                   _whole(up_b.shape)),
        compiler_params=_params())(dm, ya, yb, s5out, gm, proj, proj, proj, proj, up_a, up_b)


def _head(x3, r3, ln_g, p, target, w_gate, w_proj, out_scale, name):
    t, d = x3.shape
    pd = p.shape[1]
    tr = _pick(t, 512, 8)
    nb = t // tr

    def body(x_ref, r_ref, g_ref, p_ref, t_ref, wg_ref, wp_ref,
             loss_ref, dr_ref, drs_ref, dwg_ref, dwp_ref, dg_ref, dbeta_ref):
        xv = x_ref[...]
        sg = _sigmoid(_dot(xv, wg_ref[...]))
        pp = _dot(p_ref[...], wp_ref[...])
        err = xv + sg * pp - t_ref[...]
        loss_ref[...] = jnp.full((8, 128), 0.5 * jnp.sum(jnp.mean(err * err, axis=-1)), F32)
        dout = err * (1.0 / d)
        dgpre = dout * pp * sg * (1.0 - sg)
        dpp = dout * sg
        dx = dout + _dot(dgpre, wg_ref[...], tb=True)
        xhat, rstd = _ln_stats(r_ref[...])
        dr = _ln_bwd(dx, xhat, rstd, g_ref[...])
        dr_ref[...] = dr
        drs_ref[...] = (out_scale * dr).astype(BF16)

        @pl.when(pl.program_id(0) == 0)
        def _():
            for ref in (dwg_ref, dwp_ref, dg_ref, dbeta_ref):
                ref[...] = jnp.zeros_like(ref)

        dwg_ref[...] += _dot(xv, dgpre, ta=True)
        dwp_ref[...] += _dot(p_ref[...], dpp, ta=True)
        dg_ref[...] += _fold8(dx * xhat)
        dbeta_ref[...] += _fold8(dx)

    return pl.pallas_call(
        body, name=name,
        out_shape=(jax.ShapeDtypeStruct((nb * 8, 128), F32), jax.ShapeDtypeStruct((t, d), F32),
                   jax.ShapeDtypeStruct((t, d), BF16), jax.ShapeDtypeStruct((d, d), F32),
                   jax.ShapeDtypeStruct((pd, d), F32), jax.ShapeDtypeStruct((8, d), F32),
                   jax.ShapeDtypeStruct((8, d), F32)),
        grid=(nb,),
        in_specs=[_rows(tr, d), _rows(tr, d), _whole((1, d)), _rows(tr, pd), _rows(tr, d), _whole((d, d)),
                  _whole((pd, d))],
        out_specs=(pl.BlockSpec((8, 128), lambda i: (i, 0)), _rows(tr, d), _rows(tr, d), _whole((d, d)),
                   _whole((pd, d)), _whole((8, d)), _whole((8, d))),
        compiler_params=_params())(x3, r3, ln_g, p, target, w_gate, w_proj)


_HBM = pl.BlockSpec(memory_space=pltpu.HBM)


def _all_gather(shards, name):
    n = len(shards)

    def body(*refs):
        x_refs, out_refs = refs[:n], refs[n:2 * n]
        send_sems, recv_sems, local_sems = refs[2 * n:]
        x, y, c = lax.axis_index("x"), lax.axis_index("y"), lax.axis_index("c")
        me, sibling = (x, y, c), (x, y, 1 - c)
        chips = [(1 - x, y), (x, 1 - y), (1 - x, 1 - y)]

        def copy(a, k, block, to, own=False):
            slot = out_refs[a].at[4 * block[0] + 2 * block[1] + block[2]]
            return pltpu.make_async_remote_copy(
                src_ref=x_refs[a] if own else slot, dst_ref=slot,
                send_sem=send_sems.at[7 * a + k], recv_sem=recv_sems.at[7 * a + k],
                device_id=to, device_id_type=pl.DeviceIdType.MESH)

        mine = [pltpu.make_async_copy(x_refs[a], out_refs[a].at[4 * x + 2 * y + c], local_sems.at[a])
                for a in range(n)]
        sends = []
        for a in range(n):
            mine[a].start()
            first = [copy(a, 0, me, sibling, own=True)]
            first += [copy(a, 1 + j, me, (*chip, c), own=True) for j, chip in enumerate(chips)]
            for cp in first:
                cp.start()
            sends += first
        for a in range(n):
            for j, chip in enumerate(chips):
                copy(a, 1 + j, (*chip, c), me).wait_recv()
                passed = copy(a, 4 + j, (*chip, c), sibling)
                passed.start()
                sends.append(passed)
        for a in range(n):
            copy(a, 0, sibling, me).wait_recv()
            for j, chip in enumerate(chips):
                copy(a, 4 + j, (*chip, 1 - c), me).wait_recv()
        for cp in sends:
            cp.wait_send()
        for cp in mine:
            cp.wait()

    return pl.pallas_call(
        body, name=name,
        out_shape=tuple(jax.ShapeDtypeStruct((N_DEV,) + s.shape, s.dtype) for s in shards),
        in_specs=[_HBM] * n, out_specs=tuple([_HBM] * n),
        scratch_shapes=[pltpu.SemaphoreType.DMA((7 * n,)), pltpu.SemaphoreType.DMA((7 * n,)),
                        pltpu.SemaphoreType.DMA((n,))],
    )(*shards)


_SEM = pl.BlockSpec(memory_space=pltpu.SEMAPHORE)
_ANY = pl.BlockSpec(memory_space=pl.ANY)
_DATAFLOW = pltpu.SideEffectType.DATAFLOW_SIDE_EFFECTING


def _peer(k, x, y, c):
    return (1 - x if k & 4 else x, 1 - y if k & 2 else y, 1 - c if k & 1 else c)


def _exchange_start(sends, after, name):
    n = len(sends)
    lands = [lax.empty((N_DEV,) + s.shape[1:], s.dtype) for s in sends]

    def body(*refs):
        s_refs, l_refs = refs[:n], refs[n:2 * n]
        send_sems, recv_sems, local_sems, token = refs[2 * n + 1], refs[2 * n + 2], refs[2 * n + 3], refs[-1]
        x, y, c = lax.axis_index("x"), lax.axis_index("y"), lax.axis_index("c")
        me = 4 * x + 2 * y + c
        for a in range(n):
            same = sends[a].shape[0] == 1
            pltpu.make_async_copy(s_refs[a].at[0 if same else me], l_refs[a].at[me], local_sems.at[a]).start()
            for k in range(1, N_DEV):
                px, py, pc = _peer(k, x, y, c)
                pltpu.make_async_remote_copy(
                    src_ref=s_refs[a].at[0 if same else 4 * px + 2 * py + pc], dst_ref=l_refs[a].at[me],
                    send_sem=send_sems.at[7 * a + k - 1], recv_sem=recv_sems.at[7 * a + k - 1],
                    device_id=(px, py, pc), device_id_type=pl.DeviceIdType.MESH).start()
        token[...] = jnp.zeros_like(token)

    outs = pl.pallas_call(
        body, name=name,
        out_shape=(pltpu.SemaphoreType.DMA((7 * n,)), pltpu.SemaphoreType.DMA((7 * n,)),
                   pltpu.SemaphoreType.DMA((n,)),
                   *[pltpu.HBM(s.shape, s.dtype) for s in sends],
                   *[pltpu.HBM(l.shape, l.dtype) for l in lands],
                   jax.ShapeDtypeStruct((8, 128), F32)),
        in_specs=[_HBM] * (2 * n) + [_ANY],
        out_specs=(_SEM, _SEM, _SEM, *([_HBM] * (2 * n)), pl.BlockSpec(memory_space=pltpu.VMEM)),
        input_output_aliases={i: 3 + i for i in range(2 * n)},
        compiler_params=pltpu.CompilerParams(has_side_effects=_DATAFLOW),
    )(*[pltpu.with_memory_space_constraint(v, pltpu.HBM) for v in list(sends) + lands], after)
    return (outs[0], outs[1], outs[2], list(outs[3:3 + n]), list(outs[3 + n:3 + 2 * n])), outs[-1]


def _exchange_wait(handle, after, name):
    send_sems, recv_sems, local_sems, sends, lands = handle
    n = len(sends)

    def body(*refs):
        s_refs, l_refs = refs[:n], refs[n:2 * n]
        send_sems, recv_sems, local_sems = refs[2 * n], refs[2 * n + 1], refs[2 * n + 2]
        x, y, c = lax.axis_index("x"), lax.axis_index("y"), lax.axis_index("c")
        for a in range(n):
            pltpu.make_async_copy(s_refs[a].at[0], l_refs[a].at[0], local_sems.at[a]).wait()
            for k in range(1, N_DEV):
                copy = pltpu.make_async_remote_copy(
                    src_ref=s_refs[a].at[0], dst_ref=l_refs[a].at[0],
                    send_sem=send_sems.at[7 * a + k - 1], recv_sem=recv_sems.at[7 * a + k - 1],
                    device_id=_peer(k, x, y, c), device_id_type=pl.DeviceIdType.MESH)
                copy.wait_send()
                copy.wait_recv()

    outs = pl.pallas_call(
        body, name=name,
        out_shape=(*[pltpu.HBM(s.shape, s.dtype) for s in sends], *[pltpu.HBM(l.shape, l.dtype) for l in lands]),
        in_specs=[_HBM] * (2 * n) + [_SEM, _SEM, _SEM] + [_ANY] * len(after),
        out_specs=tuple([_HBM] * (2 * n)),
        input_output_aliases={i: i for i in range(2 * n)},
        compiler_params=pltpu.CompilerParams(has_side_effects=_DATAFLOW),
    )(*sends, *lands, send_sems, recv_sems, local_sems, *after)
    return list(outs[n:])


def _adamw(recv, w, m, v, name):
    n, rows, width = recv.shape
    tr = _pick(rows, max(8, ADAM_BLOCK_BYTES // (n * width * recv.dtype.itemsize)), 8)
    c_m = 1.0 - ADAM_B1 ** ADAM_STEP
    c_v = 1.0 - ADAM_B2 ** ADAM_STEP

    def body(r_ref, w_ref, m_ref, v_ref, g_ref, d_ref, nm_ref, nv_ref):
        g = r_ref[0].astype(F32)
        for i in range(1, n):
            g = g + r_ref[i].astype(F32)
        m2 = ADAM_B1 * m_ref[...] + (1.0 - ADAM_B1) * g
        v2 = ADAM_B2 * v_ref[...] + (1.0 - ADAM_B2) * (g * g)
        m_hat = m2 / c_m
        v_hat = v2 / c_v
        g_ref[...] = g
        d_ref[...] = -ADAM_LR * (m_hat / (jnp.sqrt(v_hat) + ADAM_EPS) + ADAM_WD * w_ref[...])
        nm_ref[...] = m2
        nv_ref[...] = v2

    blk = _rows(tr, width)
    shp = jax.ShapeDtypeStruct((rows, width), F32)
    return pl.pallas_call(
        body, name=name, out_shape=(shp, shp, shp, shp), grid=(rows // tr,),
        in_specs=[pl.BlockSpec((n, tr, width), lambda i: (0, i, 0)), blk, blk, blk],
        out_specs=(blk, blk, blk, blk), compiler_params=_params())(recv, w, m, v)


def _join_cols(gathered):
    n, r, c = gathered.shape
    return gathered.transpose(1, 0, 2).reshape(r, n * c)


def _split_cols(full):
    r, c = full.shape
    return full.reshape(r, N_DEV, c // N_DEV).transpose(1, 0, 2)


def _adamw_small(items, name):
    n = len(items)
    c_m = 1.0 - ADAM_B1 ** ADAM_STEP
    c_v = 1.0 - ADAM_B2 ** ADAM_STEP

    def body(*refs):
        ins, outs = refs[:4 * n], refs[4 * n:]
        for k in range(n):
            r_ref, w_ref, m_ref, v_ref = ins[4 * k:4 * k + 4]
            g = r_ref[0].astype(F32)
            for i in range(1, N_DEV):
                g = g + r_ref[i].astype(F32)
            m2 = ADAM_B1 * m_ref[...] + (1.0 - ADAM_B1) * g
            v2 = ADAM_B2 * v_ref[...] + (1.0 - ADAM_B2) * (g * g)
            outs[4 * k][...] = g
            outs[4 * k + 1][...] = -ADAM_LR * ((m2 / c_m) / (jnp.sqrt(v2 / c_v) + ADAM_EPS) + ADAM_WD * w_ref[...])
            outs[4 * k + 2][...] = m2
            outs[4 * k + 3][...] = v2

    vmem = pl.BlockSpec(memory_space=pltpu.VMEM)
    flat = pl.pallas_call(
        body, name=name,
        out_shape=tuple(jax.ShapeDtypeStruct(w.shape, F32) for _, w, _, _ in items for _ in range(4)),
        in_specs=[vmem] * (4 * n), out_specs=tuple([vmem] * (4 * n)),
        compiler_params=pltpu.CompilerParams(vmem_limit_bytes=VMEM_LIMIT_BYTES),
    )(*[a for item in items for a in item])
    return [tuple(flat[4 * k:4 * k + 4]) for k in range(n)]


def _discretise(lam_re, lam_im, log_dt, b_re, b_im):
    dt = jnp.exp(log_dt)[:, None]
    mag = jnp.exp(lam_re * dt)
    ab_re = mag * jnp.cos(lam_im * dt)
    ab_im = mag * jnp.sin(lam_im * dt)
    nr = ab_re - 1.0
    ni = ab_im
    den = lam_re * lam_re + lam_im * lam_im
    coef_re = ((nr * lam_re + ni * lam_im) / den)[..., None]
    coef_im = ((ni * lam_re - nr * lam_im) / den)[..., None]
    bb_re = coef_re * b_re - coef_im * b_im
    bb_im = coef_re * b_im + coef_im * b_re
    return ab_re, ab_im, bb_re, bb_im


def _same_group(gl):
    return jnp.eye(gl, dtype=bool)[None, :, None, :, None]


def _super_groups_in(bb, gl):
    g, p, i = bb.shape
    blocks = bb.reshape(g // gl, gl, p, i).transpose(0, 1, 3, 2)[:, :, :, None, :]
    return jnp.where(_same_group(gl), blocks, 0.0).reshape(g // gl, gl * i, gl * p)


def _super_groups_in_take(dense, gl, p, i):
    n_sg = dense.shape[0]
    blocks = jnp.where(_same_group(gl), dense.reshape(n_sg, gl, i, gl, p), 0.0).sum(axis=3)
    return blocks.transpose(0, 1, 3, 2).reshape(n_sg * gl, p, i)


def _super_groups_out(cc, gl):
    g, i, p = cc.shape
    blocks = cc.reshape(g // gl, gl, i, p).transpose(0, 1, 3, 2)[:, :, :, None, :]
    return jnp.where(_same_group(gl), blocks, 0.0).reshape(g // gl, gl * p, gl * i)


def _super_groups_out_take(dense, gl, i, p):
    n_sg = dense.shape[0]
    blocks = jnp.where(_same_group(gl), dense.reshape(n_sg, gl, p, gl, i), 0.0).sum(axis=3)
    return blocks.transpose(0, 1, 3, 2).reshape(n_sg * gl, i, p)


def _perm_rows(z, n_seq):
    t, w = z.shape
    steps = t // n_seq // N_SEG
    return z.reshape(n_seq, N_SEG, steps, w).transpose(0, 2, 1, 3).reshape(t, w)


def _unperm_rows(z, n_seq):
    t, w = z.shape
    steps = t // n_seq // N_SEG
    return z.reshape(n_seq, steps, N_SEG, w).transpose(0, 2, 1, 3).reshape(t, w)


def kernel(*args):
    assert len(args) == len(INPUT_NAMES)
    inp = dict(zip(INPUT_NAMES, args))
    depth = inp["ffn1_w_in"].shape[0]
    assert depth == 1
    alpha = (2.0 * depth) ** 0.25

    n_seq, seq, d_model = inp["x"].shape
    t = n_seq * seq
    x = inp["x"].reshape(t, d_model)
    x16 = x.astype(BF16)
    p16 = inp["p"][0].reshape(t, -1).astype(BF16)
    target = inp["loss_target"].reshape(t, d_model)
    wts = {n: inp[n][0] if n in SHARDED else inp[n] for n in WEIGHTS}
    mom = {n: inp["m_" + n][0] if n in SHARDED else inp["m_" + n] for n in WEIGHTS}
    var = {n: inp["v_" + n][0] if n in SHARDED else inp["v_" + n] for n in WEIGHTS}

    full = {}

    def place(n, g):
        if n in ("ffn1_w_in", "ffn2_w_in"):
            full[n] = g.reshape((2, N_DEV // 2) + g.shape[1:])
        elif n in ("ffn1_w_out", "ffn2_w_out"):
            full[n] = g.reshape(N_DEV // 2, 2 * g.shape[1], g.shape[2])
        elif n in COL_SHARDED:
            full[n] = _join_cols(g)
        else:
            full[n] = g.reshape(N_DEV * g.shape[1], g.shape[2])

    w16 = dict(zip(GATHER_FIRST, _to_bf16([wts[n] for n in GATHER_FIRST], "cast_ffn1")))
    later = tuple(n for n in SHARDED if n not in GATHER_FIRST)
    w16.update(zip(later, _to_bf16([wts[n] for n in later], "cast_rest")))
    for n, g in zip(GATHER_FIRST, _all_gather([w16[n] for n in GATHER_FIRST], "gather_ffn1")):
        place(n, g)

    def gather_start(names, after, name):
        return _exchange_start([w16[n][None] for n in names], after, name)

    def gather_wait(names, handle, after, name):
        for n, g in zip(names, _exchange_wait(handle, after, name)):
            place(n, g)

    gather_ffn1_out, token = gather_start(GATHER_FFN1_OUT, full["ffn1_w_in"], "gather_ffn1_out_start")
    gather_mix, gather_mix_token = gather_start(GATHER_MIX, token, "gather_mix_start")

    def row(v):
        return v.reshape(1, -1)

    _, n_groups, n_state = wts["ssm_lambda_re"].shape
    n_ch = wts["ssm_b_re"].shape[3]
    disc_in = tuple(wts[n][0] for n in ("ssm_lambda_re", "ssm_lambda_im", "ssm_log_dt", "ssm_b_re", "ssm_b_im"))
    (ab_re, ab_im, bb_re, bb_im), disc_vjp = jax.vjp(_discretise, *disc_in)
    a_re, a_im = row(ab_re), row(ab_im)
    gl = S5_CHANNELS_PER_STEP // n_ch
    b_sg_re = _super_groups_in(bb_re, gl).astype(BF16)
    b_sg_im = _super_groups_in(bb_im, gl).astype(BF16)
    c_sg_re = _super_groups_out(wts["ssm_c_re"][0], gl).astype(BF16)
    c_sg_im = _super_groups_out(-wts["ssm_c_im"][0], gl).astype(BF16)

    ws = wts["gmlp_w_s"][0]
    n_heads = ws.shape[0]
    d_gmlp = wts["gmlp_ln_g"].shape[1]
    causal = jnp.tril(jnp.ones((CHUNK, CHUNK), dtype=bool))
    ws_masked = jnp.where(causal[None], ws, 0.0).astype(BF16)
    ws_masked_t = ws_masked.transpose(0, 2, 1)
    bs_cols = jnp.repeat(wts["gmlp_b_s"][0].T, d_gmlp // n_heads, axis=1)
    d_ssm = n_groups * n_ch
    assert d_ssm == d_gmlp and d_model == 2 * d_ssm

    h1, a1 = _ffn_in(x16, full["ffn1_w_in"], "ffn1_in", token=gather_mix_token)
    gather_wait(GATHER_FFN1_OUT, gather_ffn1_out, [a1], "gather_ffn1_out_wait")
    r1, x1, x1_16 = _ffn_out_ln(a1, full["ffn1_w_out"], x, row(wts["ln1_g"]), row(wts["ln1_b"]), alpha,
                                "ffn1_out_ln")

    gather_wait(GATHER_MIX, gather_mix, [x1_16], "gather_mix_wait")
    gather_mid, token = gather_start(GATHER_MID, full["mix_w_in"], "gather_mid_start")
    gather_last, token = gather_start(GATHER_LAST, token, "gather_ffn2_start")
    proj = _mm(x1_16, full["mix_w_in"], name="mix_in", token=token)
    za_p = _perm_rows(proj[:, :d_ssm], n_seq)
    za_p16 = za_p.astype(BF16)
    h_re, h_im, yssm = _s5_fwd(za_p16, b_sg_re, b_sg_im, a_re, a_im, c_sg_re, c_sg_im, n_seq, "s5_fwd")
    gather_wait(GATHER_MID, gather_mid, [yssm], "gather_mid_wait")
    s5out_p = _s5_post_fwd(yssm, za_p, row(wts["ssm_d"]), full["ssm_glu_w"], row(wts["ssm_glu_b"]),
                           "s5_post")
    s5out = _unperm_rows(s5out_p, n_seq)
    gm = _gmlp_fwd(proj, row(wts["gmlp_ln_g"]), row(wts["gmlp_ln_b"]), ws_masked, bs_cols, "gmlp")
    m_mix, y_a, y_b = _merge_fwd(s5out, gm, proj, full["up_a"], full["up_b"], "merge")
    r2, x2, x2_16 = _ffn_out_ln(m_mix[None], full["mix_w_out"][None], x1, row(wts["ln2_g"]), row(wts["ln2_b"]),
                                alpha, "mix_out_ln2", scale=1.0)

    gather_wait(GATHER_LAST, gather_last, [x2_16], "gather_ffn2_wait")
    h2, a2 = _ffn_in(x2_16, full["ffn2_w_in"], "ffn2_in")
    r3, x3, _ = _ffn_out_ln(a2, full["ffn2_w_out"], x2, row(wts["ln3_g"]), row(wts["ln3_b"]), alpha,
                            "ffn2_out_ln")

    small = {}
    send = {}

    def lane_dense(n, v):
        return v.reshape(v.shape[:2] + (-1,)) if n in ("ssm_b_re", "ssm_b_im") else v

    def on_wire(n, g):
        g = lane_dense(n, g)
        return (g.astype(BF16) if n in SMALL_BF16 else g)[None]
    loss_parts, dr3, dr3_h, dw_gate, dw_proj, dg, db = _head(
        x3, r3, row(wts["ln3_g"]), p16, target, full["ple_w_gate"], full["ple_w_proj"], 0.5, "head")
    loss_mine = jnp.full((1, 1, 8, 128), jnp.sum(loss_parts[::8, 0]), F32)
    send["ple_w_gate"] = dw_gate.astype(BF16).reshape(N_DEV, -1, d_model)
    send["ple_w_proj"] = _split_cols(dw_proj.astype(BF16))
    small["ln3_g"], small["ln3_b"] = dg.sum(0, keepdims=True), db.sum(0, keepdims=True)
    dh2 = _ffn_out_dx(dr3_h, full["ffn2_w_out"], h2, "ffn2_out_dx")
    dw = _ffn_out_dw(a2, dr3_h, "ffn2_out_dw")
    send["ffn2_w_out"] = dw.reshape(N_DEV, -1, d_model)
    dw = _ffn_in_dw(x2_16, dh2, "ffn2_in_dw")
    send["ffn2_w_in"] = dw.reshape((N_DEV,) + dw.shape[2:])
    group1 = ("ffn2_w_in", "ffn2_w_out", "ple_w_gate", "ple_w_proj")
    exchange1, token = _exchange_start(
        [send[n] for n in group1] + [on_wire(n, small[n]) for n in SMALL_HEAD] + [loss_mine], dh2,
        "grad_exchange1_start")
    dr2, dr2_h, dg, db = _ffn_in_dx_ln(dh2, full["ffn2_w_in"], r2, dr3, alpha, row(wts["ln2_g"]), 1.0,
                                       "ffn2_in_dx_ln2_bwd", token=token)
    small["ln2_g"], small["ln2_b"] = dg.sum(0, keepdims=True), db.sum(0, keepdims=True)
    dm = _mm(dr2_h, full["mix_w_out"], tb=True, name="mix_out_dx")
    send["mix_w_out"] = _mm(m_mix, dr2_h, ta=True, name="mix_out_dw", out_dtype=BF16).reshape(
        N_DEV, -1, d_model)
    dga, dgb, ds5out, dgm, dw_a, dw_b = _merge_bwd(
        dm, y_a, y_b, s5out, gm, proj, full["up_a"], full["up_b"], "merge_bwd")
    send["up_a"] = _split_cols(dw_a.astype(BF16))
    send["up_b"] = _split_cols(dw_b.astype(BF16))

    dzu, dzv, dws, dbs_cols, dg, db = _gmlp_bwd(
        proj, dgm, row(wts["gmlp_ln_g"]), row(wts["gmlp_ln_b"]), ws_masked, ws_masked_t, bs_cols,
        "gmlp_bwd")
    small["gmlp_ln_g"], small["gmlp_ln_b"] = dg.sum(0, keepdims=True), db.sum(0, keepdims=True)
    small["gmlp_w_s"] = jnp.where(causal[None], dws, 0.0)[None]
    small["gmlp_b_s"] = dbs_cols.reshape(CHUNK, n_heads, -1).sum(-1).T[None]

    ds5out_p = _perm_rows(ds5out, n_seq)
    dy_p, dza_skip, dw_glu, dd, dgb_glu = _s5_post_bwd(
        yssm, za_p, ds5out_p, row(wts["ssm_d"]), full["ssm_glu_w"], row(wts["ssm_glu_b"]),
        "s5_post_bwd")
    send["ssm_glu_w"] = dw_glu.astype(BF16).reshape(N_DEV, -1, d_ssm)
    small["ssm_d"], small["ssm_glu_b"] = dd.sum(0, keepdims=True), dgb_glu.sum(0, keepdims=True)
    dza_p, dbb_re, dbb_im, dc_re, dc_im, da_re, da_im = _s5_bwd(
        dy_p, dza_skip, h_re, h_im, za_p16, b_sg_re, b_sg_im, a_re, a_im, c_sg_re, c_sg_im, n_seq, "s5_bwd")
    small["ssm_c_re"] = _super_groups_out_take(dc_re, gl, n_ch, n_state)[None]
    small["ssm_c_im"] = -_super_groups_out_take(dc_im, gl, n_ch, n_state)[None]
    dbb_re = _super_groups_in_take(dbb_re, gl, n_state, n_ch)
    dbb_im = _super_groups_in_take(dbb_im, gl, n_state, n_ch)
    dab_re = da_re.sum(0).reshape(n_groups, n_state)
    dab_im = da_im.sum(0).reshape(n_groups, n_state)
    for n, g in zip(("ssm_lambda_re", "ssm_lambda_im", "ssm_log_dt", "ssm_b_re", "ssm_b_im"),
                    disc_vjp((dab_re, dab_im, dbb_re, dbb_im))):
        small[n] = g[None]

    dproj = jnp.concatenate([_unperm_rows(dza_p, n_seq), dzu, dzv, dga, dgb], axis=1)
    group2 = ("mix_w_out", "up_a", "up_b", "ssm_glu_w")
    exchange2, token = _exchange_start(
        [send[n] for n in group2] + [on_wire(n, small[n]) for n in SMALL_MID], dproj, "grad_exchange2_start")
    send["mix_w_in"] = _split_cols(_mm(x1_16, dproj, ta=True, name="mix_in_dw", out_dtype=BF16, token=token))
    exchange3, token = _exchange_start([send["mix_w_in"]], dproj, "grad_exchange3_start")
    dr1, dr1_h, dg, db = _mm_ln_bwd(dproj, full["mix_w_in"], r1, dr2, alpha, row(wts["ln1_g"]), 0.5,
                                    "mix_in_dx_ln1_bwd", token=token)
    small["ln1_g"], small["ln1_b"] = dg.sum(0, keepdims=True), db.sum(0, keepdims=True)
    dw = _ffn_out_dw(a1, dr1_h, "ffn1_out_dw")
    send["ffn1_w_out"] = dw.reshape(N_DEV, -1, d_model)
    exchange4, token = _exchange_start([send["ffn1_w_out"]] + [on_wire(n, small[n]) for n in SMALL_LATE], dr1_h,
                                       "grad_exchange4_start")
    dh1 = _ffn_out_dx(dr1_h, full["ffn1_w_out"], h1, "ffn1_out_dx", token=token)
    dw = _ffn_in_dw(x16, dh1, "ffn1_in_dw")
    send["ffn1_w_in"] = dw.reshape((N_DEV,) + dw.shape[2:])
    exchange5, token = _exchange_start([send["ffn1_w_in"]], dh1, "grad_exchange5_start")
    grad_x = _ffn_in_dx(dh1, full["ffn1_w_in"], dr1, alpha, "ffn1_in_dx", token=token)

    results = {}

    def update(names, recv, prefix):
        for n, r in zip(names, recv):
            results[n] = _adamw(r, wts[n], mom[n], var[n], prefix + n)

    def update_small(names, recv, name):
        items = [(r, lane_dense(n, wts[n]), lane_dense(n, mom[n]), lane_dense(n, var[n])) for n, r in zip(names, recv)]
        for n, outs in zip(names, _adamw_small(items, name)):
            results[n] = tuple(o.reshape(wts[n].shape) for o in outs)

    def done(names):
        return [results[n][3] for n in names]

    recv = _exchange_wait(exchange1, [grad_x], "grad_exchange1_wait")
    update(group1, recv[:len(group1)], "adamw_")
    update_small(SMALL_HEAD, recv[len(group1):-1], "adamw_ln3")
    loss = jnp.sum(recv[-1][:, 0, 0, 0])
    recv = _exchange_wait(exchange2, done(group1 + SMALL_HEAD), "grad_exchange2_wait")
    update(group2, recv[:len(group2)], "adamw_")
    update_small(SMALL_MID, recv[len(group2):], "adamw_small")
    recv = _exchange_wait(exchange3, done(group2 + SMALL_MID), "grad_exchange3_wait")
    update(("mix_w_in",), recv, "adamw_")
    recv = _exchange_wait(exchange4, done(("mix_w_in",)), "grad_exchange4_wait")
    update(("ffn1_w_out",), recv[:1], "adamw_")
    update_small(SMALL_LATE, recv[1:], "adamw_ln1")
    recv = _exchange_wait(exchange5, done(("ffn1_w_out",) + SMALL_LATE), "grad_exchange5_wait")
    update(("ffn1_w_in",), recv, "adamw_")

    outs = [loss, grad_x.reshape(n_seq, seq, d_model)]
    for k in range(4):
        outs += [results[n][k][None] if n in SHARDED else results[n][k] for n in WEIGHTS]
    return tuple(outs)
```

```python
import functools
import math

import jax
import jax.numpy as jnp
from jax import lax
from jax.experimental import pallas as pl
from jax.experimental.pallas import tpu as pltpu

F32 = jnp.float32
BF16 = jnp.bfloat16

N_DEV = 8
LN_EPS = 1e-5
CHUNK = 128
N_SEG = 8
S5_CHANNELS_PER_STEP = 128
VMEM_LIMIT_BYTES = 56 * 1024 * 1024
MM_OPERAND_BLOCK_BYTES = 8 * 1024 * 1024
ADAM_BLOCK_BYTES = 6 * 1024 * 1024

ADAM_LR = 0.001
ADAM_B1 = 0.9
ADAM_B2 = 0.999
ADAM_EPS = 1e-08
ADAM_WD = 0.01
ADAM_STEP = 10

COL_SHARDED = ("ffn1_w_in", "mix_w_in", "up_a", "up_b", "ffn2_w_in", "ple_w_proj")
ROW_SHARDED = ("ffn1_w_out", "ssm_glu_w", "mix_w_out", "ffn2_w_out", "ple_w_gate")
SHARDED = ("ffn1_w_in", "ffn1_w_out", "mix_w_in", "ssm_glu_w", "up_a", "up_b", "mix_w_out",
           "ffn2_w_in", "ffn2_w_out", "ple_w_proj", "ple_w_gate")
WEIGHTS = ("ffn1_w_in", "ffn1_w_out", "ln1_g", "ln1_b", "mix_w_in", "ssm_lambda_re", "ssm_lambda_im",
           "ssm_log_dt", "ssm_b_re", "ssm_b_im", "ssm_c_re", "ssm_c_im", "ssm_d", "ssm_glu_w",
           "ssm_glu_b", "gmlp_ln_g", "gmlp_ln_b", "gmlp_w_s", "gmlp_b_s", "up_a", "up_b", "mix_w_out",
           "ln2_g", "ln2_b", "ffn2_w_in", "ffn2_w_out", "ln3_g", "ln3_b", "ple_w_proj", "ple_w_gate")
GATHER_FIRST = ("ffn1_w_in",)
GATHER_EARLY = ("ffn1_w_out", "mix_w_in")
GATHER_MID = ("ssm_glu_w", "up_a", "up_b", "mix_w_out")
GATHER_LAST = ("ffn2_w_in", "ffn2_w_out", "ple_w_proj", "ple_w_gate")
SMALL = tuple(n for n in WEIGHTS if n not in SHARDED)
SMALL_HEAD = ("ln3_g", "ln3_b")
SMALL_LATE = ("ln1_g", "ln1_b")
SMALL_MID = tuple(n for n in SMALL if n not in SMALL_HEAD + SMALL_LATE)
SMALL_BF16 = ("gmlp_w_s", "ssm_b_re", "ssm_b_im", "ssm_c_re", "ssm_c_im")
INPUT_NAMES = ("x", "p") + WEIGHTS + ("loss_target",) + tuple("m_" + n for n in WEIGHTS) + tuple(
    "v_" + n for n in WEIGHTS)


def _pick(dim, pref, mult=128):
    if dim <= pref:
        return dim
    d = (pref // mult) * mult
    while d >= mult:
        if dim % d == 0:
            return d
        d -= mult
    return dim


def _params(n_axes=1):
    return pltpu.CompilerParams(dimension_semantics=("arbitrary",) * n_axes,
                                vmem_limit_bytes=VMEM_LIMIT_BYTES)


def _sigmoid(v):
    return 1.0 / (1.0 + jnp.exp(-v))


_GELU_C = math.sqrt(2.0 / math.pi)


def _gelu(v):
    return 0.5 * v * (1.0 + jnp.tanh(_GELU_C * (v + 0.044715 * (v * v * v))))


def _gelu_grad(v):
    t = jnp.tanh(_GELU_C * (v + 0.044715 * (v * v * v)))
    return 0.5 * (1.0 + t) + 0.5 * v * (1.0 - t * t) * (_GELU_C * (1.0 + 3.0 * 0.044715 * v * v))


def _ln_stats(r):
    mu = jnp.mean(r, axis=-1, keepdims=True)
    xc = r - mu
    var = jnp.mean(xc * xc, axis=-1, keepdims=True)
    rstd = lax.rsqrt(var + LN_EPS)
    return xc * rstd, rstd


def _ln_bwd(dy, xhat, rstd, g):
    dxh = dy * g
    m1 = jnp.mean(dxh, axis=-1, keepdims=True)
    m2 = jnp.mean(dxh * xhat, axis=-1, keepdims=True)
    return rstd * (dxh - m1 - xhat * m2)


def _fold8(v):
    rows, w = v.shape
    return jnp.sum(v.reshape(rows // 8, 8, w), axis=0)


def _dot(a, b, ta=False, tb=False):
    dn = (((0 if ta else 1,), (1 if tb else 0,)), ((), ()))
    return lax.dot_general(a.astype(BF16), b.astype(BF16), dn, preferred_element_type=F32)


_TOKEN = pl.BlockSpec((8, 128), lambda *_: (0, 0))


def _mm(a, b, *, name, ta=False, tb=False, out_dtype=F32, add=None, add_scale=1.0, token=None,
        tm=2048, tn=512, tk=4096):
    m_dim = a.shape[1] if ta else a.shape[0]
    k_dim = a.shape[0] if ta else a.shape[1]
    n_dim = b.shape[0] if tb else b.shape[1]
    assert (b.shape[1] if tb else b.shape[0]) == k_dim, (name, a.shape, b.shape)
    tk = _pick(k_dim, tk)
    tm = min(tm, max(256, MM_OPERAND_BLOCK_BYTES // (tk * a.dtype.itemsize)))
    tm, tn = _pick(m_dim, tm), _pick(n_dim, tn)
    nk = k_dim // tk
    has_add = add is not None

    def finish(r, add_ref, o_ref):
        if has_add:
            r = r + add_scale * add_ref[...].astype(F32)
        o_ref[...] = r.astype(out_dtype)

    def body(*refs):
        a_ref, b_ref = refs[:2]
        add_ref = refs[2] if has_add else None
        o_ref = refs[n_in]
        if nk == 1:
            finish(_dot(a_ref[...], b_ref[...], ta, tb), add_ref, o_ref)
            return
        acc_ref = refs[-1]
        k = pl.program_id(2)

        @pl.when(k == 0)
        def _():
            acc_ref[...] = jnp.zeros_like(acc_ref)

        acc_ref[...] += _dot(a_ref[...], b_ref[...], ta, tb)

        @pl.when(k == nk - 1)
        def _():
            finish(acc_ref[...], add_ref, o_ref)

    a_spec = (pl.BlockSpec((tk, tm), lambda i, j, k: (k, i)) if ta
              else pl.BlockSpec((tm, tk), lambda i, j, k: (i, k)))
    b_spec = (pl.BlockSpec((tn, tk), lambda i, j, k: (j, k)) if tb
              else pl.BlockSpec((tk, tn), lambda i, j, k: (k, j)))
    in_specs = [a_spec, b_spec]
    operands = [a, b]
    if has_add:
        in_specs.append(pl.BlockSpec((tm, tn), lambda i, j, k: (i, j)))
        operands.append(add)
    if token is not None:
        in_specs.append(_TOKEN)
        operands.append(token)
    n_in = len(operands)
    return pl.pallas_call(
        body, name=name,
        out_shape=jax.ShapeDtypeStruct((m_dim, n_dim), out_dtype),
        grid=(m_dim // tm, n_dim // tn, nk),
        in_specs=in_specs,
        out_specs=pl.BlockSpec((tm, tn), lambda i, j, k: (i, j)),
        scratch_shapes=[pltpu.VMEM((tm, tn), F32)] if nk > 1 else [],
        compiler_params=_params(3),
    )(*operands)


def _to_bf16(arrays, name):
    n = len(arrays)

    def body(*refs):
        for src, dst in zip(refs[:n], refs[n:]):
            dst[...] = src[...].astype(BF16)

    vmem = pl.BlockSpec(memory_space=pltpu.VMEM)
    return pl.pallas_call(
        body, name=name, out_shape=tuple(jax.ShapeDtypeStruct(a.shape, BF16) for a in arrays),
        in_specs=[vmem] * n, out_specs=tuple([vmem] * n),
        compiler_params=pltpu.CompilerParams(vmem_limit_bytes=VMEM_LIMIT_BYTES))(*arrays)


def _rows(tr, w, cb=0):
    return pl.BlockSpec((tr, w), lambda i: (i, cb))


def _whole(shape):
    nd = len(shape)
    return pl.BlockSpec(tuple(shape), lambda i: (0,) * nd)


def _ffn_in(x16, w_in, name, token=None):
    t, d = x16.shape
    _, nb, _, fb = w_in.shape
    tm = _pick(t, 1024, 8)
    extra = [] if token is None else [token]

    def body(*refs):
        x_ref, wg_ref, wu_ref = refs[:3]
        h_ref, a_ref = refs[-2:]
        xv = x_ref[...]
        g = _dot(xv, wg_ref[0, 0])
        u = _dot(xv, wu_ref[0, 0])
        h_ref[0, 0] = g.astype(BF16)
        h_ref[0, 1] = u.astype(BF16)
        a_ref[0] = (g * _sigmoid(g) * u).astype(BF16)

    return pl.pallas_call(
        body, name=name,
        out_shape=(jax.ShapeDtypeStruct((nb, 2, t, fb), BF16), jax.ShapeDtypeStruct((nb, t, fb), BF16)),
        grid=(t // tm, nb),
        in_specs=[pl.BlockSpec((tm, d), lambda i, j: (i, 0)),
                  pl.BlockSpec((1, 1, d, fb), lambda i, j: (0, j, 0, 0)),
                  pl.BlockSpec((1, 1, d, fb), lambda i, j: (1, j, 0, 0))] + [_TOKEN] * len(extra),
        out_specs=(pl.BlockSpec((1, 2, tm, fb), lambda i, j: (j, 0, i, 0)),
                   pl.BlockSpec((1, tm, fb), lambda i, j: (j, i, 0))),
        compiler_params=_params(2))(x16, w_in, w_in, *extra)


def _ffn_out_ln(a, w_out, xin, g, b, alpha, name, scale=0.5):
    nb, t, fb = a.shape
    d = w_out.shape[2]
    tm = _pick(t, 512, 8)

    def body(a_ref, w_ref, x_ref, g_ref, b_ref, r_ref, y_ref, y16_ref):
        f = _dot(a_ref[0], w_ref[0])
        for jb in range(1, nb):
            f = f + _dot(a_ref[jb], w_ref[jb])
        r = alpha * x_ref[...] + scale * f
        xhat, _ = _ln_stats(r)
        y = xhat * g_ref[...] + b_ref[...]
        r_ref[...] = r
        y_ref[...] = y
        y16_ref[...] = y.astype(BF16)

    return pl.pallas_call(
        body, name=name,
        out_shape=(jax.ShapeDtypeStruct((t, d), F32), jax.ShapeDtypeStruct((t, d), F32),
                   jax.ShapeDtypeStruct((t, d), BF16)),
        grid=(t // tm,),
        in_specs=[pl.BlockSpec((nb, tm, fb), lambda i: (0, i, 0)), _whole(w_out.shape), _rows(tm, d),
                  _whole((1, d)), _whole((1, d))],
        out_specs=(_rows(tm, d), _rows(tm, d), _rows(tm, d)),
        compiler_params=_params())(a, w_out, xin, g, b)


def _ffn_out_dx(drs, w_out, h, name, token=None):
    nb, _, t, fb = h.shape
    d = w_out.shape[2]
    tm = _pick(t, 1024, 8)
    extra = [] if token is None else [token]

    def body(*refs):
        dr_ref, w_ref, h_ref, dh_ref = refs[0], refs[1], refs[2], refs[-1]
        da = _dot(dr_ref[...], w_ref[0], tb=True)
        g, u = h_ref[0, 0].astype(F32), h_ref[0, 1].astype(F32)
        sg = _sigmoid(g)
        dh_ref[0, 0] = (da * u * (sg * (1.0 + g * (1.0 - sg)))).astype(BF16)
        dh_ref[0, 1] = (da * (g * sg)).astype(BF16)

    return pl.pallas_call(
        body, name=name, out_shape=jax.ShapeDtypeStruct((nb, 2, t, fb), BF16), grid=(t // tm, nb),
        in_specs=[pl.BlockSpec((tm, d), lambda i, j: (i, 0)),
                  pl.BlockSpec((1, fb, d), lambda i, j: (j, 0, 0)),
                  pl.BlockSpec((1, 2, tm, fb), lambda i, j: (j, 0, i, 0))] + [_TOKEN] * len(extra),
        out_specs=pl.BlockSpec((1, 2, tm, fb), lambda i, j: (j, 0, i, 0)),
        compiler_params=_params(2))(drs, w_out, h, *extra)


def _ffn_out_dw(a, drs, name):
    nb, t, fb = a.shape
    d = drs.shape[1]

    def body(a_ref, dr_ref, o_ref):
        o_ref[0] = _dot(a_ref[0], dr_ref[...], ta=True).astype(BF16)

    return pl.pallas_call(
        body, name=name, out_shape=jax.ShapeDtypeStruct((nb, fb, d), BF16), grid=(nb,),
        in_specs=[pl.BlockSpec((1, t, fb), lambda j: (j, 0, 0)), pl.BlockSpec((t, d), lambda j: (0, 0))],
        out_specs=pl.BlockSpec((1, fb, d), lambda j: (j, 0, 0)),
        compiler_params=_params())(a, drs)


def _ffn_in_dw(x16, dh, name, token=None):
    t, d = x16.shape
    nb, _, _, fb = dh.shape
    extra = [] if token is None else [token]

    def body(*refs):
        x_ref, dh_ref, o_ref = refs[0], refs[1], refs[-1]
        o_ref[0, 0] = _dot(x_ref[...], dh_ref[0, 0], ta=True).astype(BF16)

    return pl.pallas_call(
        body, name=name, out_shape=jax.ShapeDtypeStruct((2, nb, d, fb), BF16), grid=(nb, 2),
        in_specs=[pl.BlockSpec((t, d), lambda j, s: (0, 0)),
                  pl.BlockSpec((1, 1, t, fb), lambda j, s: (j, s, 0, 0))] + [_TOKEN] * len(extra),
        out_specs=pl.BlockSpec((1, 1, d, fb), lambda j, s: (s, j, 0, 0)),
        compiler_params=_params(2))(x16, dh, *extra)


def _ffn_in_dx(dh, w_in, add, add_scale, name, token=None):
    nb, _, t, fb = dh.shape
    d = w_in.shape[2]
    tm = _pick(t, 512, 8)
    has_add = add is not None
    extra = [] if token is None else [token]

    def body(*refs):
        dh_ref, w_ref = refs[:2]
        o_ref = refs[-1]
        acc = None
        for jb in range(nb):
            for s in range(2):
                part = _dot(dh_ref[jb, s], w_ref[s, jb], tb=True)
                acc = part if acc is None else acc + part
        if has_add:
            acc = acc + add_scale * refs[2][...]
        o_ref[...] = acc

    return pl.pallas_call(
        body, name=name, out_shape=jax.ShapeDtypeStruct((t, d), F32), grid=(t // tm,),
        in_specs=[pl.BlockSpec((nb, 2, tm, fb), lambda i: (0, 0, i, 0)), _whole(w_in.shape)]
        + ([_rows(tm, d)] if has_add else []) + [_TOKEN] * len(extra),
        out_specs=_rows(tm, d),
        compiler_params=_params())(*([dh, w_in] + ([add] if has_add else []) + extra))


def _ffn_in_dx_ln(dh, w_in, r, dy_b, b_scale, ln_g, out_scale, name, token=None):
    nb, _, t, fb = dh.shape
    d = w_in.shape[2]
    tm = _pick(t, 512, 8)
    extra = [] if token is None else [token]

    def body(*refs):
        dh_ref, w_ref, r_ref, dyb_ref, g_ref = refs[:5]
        dr_ref, drs_ref, dg_ref, dbeta_ref = refs[-4:]
        dy = b_scale * dyb_ref[...]
        for jb in range(nb):
            for s in range(2):
                dy = dy + _dot(dh_ref[jb, s], w_ref[s, jb], tb=True)
        xhat, rstd = _ln_stats(r_ref[...])
        dr = _ln_bwd(dy, xhat, rstd, g_ref[...])
        dr_ref[...] = dr
        drs_ref[...] = (out_scale * dr).astype(BF16)

        @pl.when(pl.program_id(0) == 0)
        def _():
            dg_ref[...] = jnp.zeros_like(dg_ref)
            dbeta_ref[...] = jnp.zeros_like(dbeta_ref)

        dg_ref[...] += _fold8(dy * xhat)
        dbeta_ref[...] += _fold8(dy)

    return pl.pallas_call(
        body, name=name,
        out_shape=(jax.ShapeDtypeStruct((t, d), F32), jax.ShapeDtypeStruct((t, d), BF16),
                   jax.ShapeDtypeStruct((8, d), F32), jax.ShapeDtypeStruct((8, d), F32)),
        grid=(t // tm,),
        in_specs=[pl.BlockSpec((nb, 2, tm, fb), lambda i: (0, 0, i, 0)), _whole(w_in.shape), _rows(tm, d),
                  _rows(tm, d), _whole((1, d))] + [_TOKEN] * len(extra),
        out_specs=(_rows(tm, d), _rows(tm, d), _whole((8, d)), _whole((8, d))),
        compiler_params=_params())(dh, w_in, r, dy_b, ln_g, *extra)


def _mm_ln_bwd(a, w, r, dy_b, b_scale, ln_g, out_scale, name, token=None):
    t, k_dim = a.shape
    d = w.shape[0]
    tm = _pick(t, 512, 8)
    extra = [] if token is None else [token]

    def body(*refs):
        a_ref, w_ref, r_ref, dyb_ref, g_ref = refs[:5]
        dr_ref, drs_ref, dg_ref, dbeta_ref = refs[-4:]
        dy = _dot(a_ref[...], w_ref[...], tb=True) + b_scale * dyb_ref[...]
        xhat, rstd = _ln_stats(r_ref[...])
        dr = _ln_bwd(dy, xhat, rstd, g_ref[...])
        dr_ref[...] = dr
        drs_ref[...] = (out_scale * dr).astype(BF16)

        @pl.when(pl.program_id(0) == 0)
        def _():
            dg_ref[...] = jnp.zeros_like(dg_ref)
            dbeta_ref[...] = jnp.zeros_like(dbeta_ref)

        dg_ref[...] += _fold8(dy * xhat)
        dbeta_ref[...] += _fold8(dy)

    return pl.pallas_call(
        body, name=name,
        out_shape=(jax.ShapeDtypeStruct((t, d), F32), jax.ShapeDtypeStruct((t, d), BF16),
                   jax.ShapeDtypeStruct((8, d), F32), jax.ShapeDtypeStruct((8, d), F32)),
        grid=(t // tm,),
        in_specs=[_rows(tm, k_dim), _whole(w.shape), _rows(tm, d), _rows(tm, d), _whole((1, d))]
        + [_TOKEN] * len(extra),
        out_specs=(_rows(tm, d), _rows(tm, d), _whole((8, d)), _whole((8, d))),
        compiler_params=_params())(a, w, r, dy_b, ln_g, *extra)


def _take_row(v, row_id, s):
    return jnp.sum(jnp.where(row_id == s, v, 0.0), axis=0, keepdims=True)


def _complex_power(ar, ai, n):
    out = None
    while n:
        if n & 1:
            out = (ar, ai) if out is None else (out[0] * ar - out[1] * ai, out[0] * ai + out[1] * ar)
        ar, ai = ar * ar - ai * ai, 2.0 * ar * ai
        n >>= 1
    return out


def _seg_carries(f_re, f_im, p_re, p_im, reverse):
    row_id = lax.broadcasted_iota(jnp.int32, f_re.shape, 0)
    p_re, p_im = _take_row(p_re, row_id, 0), _take_row(p_im, row_id, 0)
    out_re, out_im = jnp.zeros_like(f_re), jnp.zeros_like(f_im)
    c_re, c_im = jnp.zeros_like(p_re), jnp.zeros_like(p_im)
    order = range(N_SEG - 1, -1, -1) if reverse else range(N_SEG)
    for s in order:
        out_re = jnp.where(row_id == s, c_re, out_re)
        out_im = jnp.where(row_id == s, c_im, out_im)
        fr, fi = _take_row(f_re, row_id, s), _take_row(f_im, row_id, s)
        c_re, c_im = fr + p_re * c_re - p_im * c_im, fi + p_re * c_im + p_im * c_re
    return out_re, out_im


def _s5_fwd(za16, b_re, b_im, a_re, a_im, c_re, c_im, n_seq, name):
    t = za16.shape[0]
    n_sg, ch, st = b_re.shape
    seq = t // n_seq
    steps = seq // N_SEG

    def body(za_ref, bre_ref, bim_ref, are, aim, cre_ref, cim_ref, hre, him, y_ref):
        za = za_ref[...]
        hre[...] = _dot(za, bre_ref[0])
        him[...] = _dot(za, bim_ref[0])
        ar = jnp.broadcast_to(are[...], (N_SEG, st))
        ai = jnp.broadcast_to(aim[...], (N_SEG, st))
        zero = jnp.zeros((N_SEG, st), F32)

        def local(k, carry):
            lr, li = carry
            rows = pl.ds(pl.multiple_of(k * N_SEG, N_SEG), N_SEG)
            nr = ar * lr - ai * li + hre[rows, :]
            ni = ar * li + ai * lr + him[rows, :]
            hre[rows, :] = nr
            him[rows, :] = ni
            return nr, ni

        f_re, f_im = lax.fori_loop(0, steps, local, (zero, zero))
        c_re, c_im = _seg_carries(f_re, f_im, *_complex_power(ar, ai, steps), reverse=False)

        def fix(k, carry):
            qr, qi = carry
            rows = pl.ds(pl.multiple_of(k * N_SEG, N_SEG), N_SEG)
            hre[rows, :] = hre[rows, :] + qr
            him[rows, :] = him[rows, :] + qi
            return ar * qr - ai * qi, ar * qi + ai * qr

        lax.fori_loop(0, steps, fix, (ar * c_re - ai * c_im, ar * c_im + ai * c_re))
        y_ref[...] = _dot(hre[...], cre_ref[0]) + _dot(him[...], cim_ref[0])

    rows_ch = pl.BlockSpec((seq, ch), lambda s, j: (s, j))
    rows_st = pl.BlockSpec((seq, st), lambda s, j: (s, j))
    vec = pl.BlockSpec((1, st), lambda s, j: (0, j))
    b_blk = pl.BlockSpec((1, ch, st), lambda s, j: (j, 0, 0))
    c_blk = pl.BlockSpec((1, st, ch), lambda s, j: (j, 0, 0))
    return pl.pallas_call(
        body, name=name,
        out_shape=(jax.ShapeDtypeStruct((t, n_sg * st), F32), jax.ShapeDtypeStruct((t, n_sg * st), F32),
                   jax.ShapeDtypeStruct((t, n_sg * ch), F32)),
        grid=(n_seq, n_sg), in_specs=[rows_ch, b_blk, b_blk, vec, vec, c_blk, c_blk],
        out_specs=(rows_st, rows_st, rows_ch),
        compiler_params=_params(2))(za16, b_re, b_im, a_re, a_im, c_re, c_im)


def _s5_bwd(dy16, dza_skip, h_re, h_im, za16, b_re, b_im, a_re, a_im, c_re, c_im, n_seq, name):
    t = dy16.shape[0]
    n_sg, ch, st = b_re.shape
    seq = t // n_seq
    steps = seq // N_SEG

    def body(dy_ref, skip_ref, hre, him, za_ref, bre_ref, bim_ref, are, aim, cre_ref, cim_ref,
             dza_ref, dbre_ref, dbim_ref, dcre_ref, dcim_ref, dare, daim, lre, lim):
        dy = dy_ref[...]
        lre[...] = _dot(dy, cre_ref[0], tb=True)
        lim[...] = _dot(dy, cim_ref[0], tb=True)
        ar = jnp.broadcast_to(are[...], (N_SEG, st))
        ai = -jnp.broadcast_to(aim[...], (N_SEG, st))
        zero = jnp.zeros((N_SEG, st), F32)

        def local(i, carry):
            lr, li = carry
            k = steps - 1 - i
            rows = pl.ds(pl.multiple_of(k * N_SEG, N_SEG), N_SEG)
            nr = ar * lr - ai * li + lre[rows, :]
            ni = ar * li + ai * lr + lim[rows, :]
            lre[rows, :] = nr
            lim[rows, :] = ni
            return nr, ni

        f_re, f_im = lax.fori_loop(0, steps, local, (zero, zero))
        c_re, c_im = _seg_carries(f_re, f_im, *_complex_power(ar, ai, steps), reverse=True)

        def accumulate(lam_r, lam_i, hp_r, hp_i, acc_r, acc_i):
            return acc_r + (lam_r * hp_r + lam_i * hp_i), acc_i + (lam_i * hp_r - lam_r * hp_i)

        def fix(i, carry):
            qr, qi, acc_r, acc_i = carry
            k = steps - 1 - i
            rows = pl.ds(pl.multiple_of(k * N_SEG, N_SEG), N_SEG)
            prev = pl.ds(pl.multiple_of((k - 1) * N_SEG, N_SEG), N_SEG)
            lam_r = lre[rows, :] + qr
            lam_i = lim[rows, :] + qi
            lre[rows, :] = lam_r
            lim[rows, :] = lam_i
            acc_r, acc_i = accumulate(lam_r, lam_i, hre[prev, :], him[prev, :], acc_r, acc_i)
            return ar * qr - ai * qi, ar * qi + ai * qr, acc_r, acc_i

        qr, qi, acc_r, acc_i = lax.fori_loop(
            0, steps - 1, fix, (ar * c_re - ai * c_im, ar * c_im + ai * c_re, zero, zero))
        first = pl.ds(0, N_SEG)
        last = pl.ds((steps - 1) * N_SEG, N_SEG)
        lam_r = lre[first, :] + qr
        lam_i = lim[first, :] + qi
        lre[first, :] = lam_r
        lim[first, :] = lam_i
        row_id = lax.broadcasted_iota(jnp.int32, (N_SEG, st), 0)
        hp_r = jnp.where(row_id == 0, 0.0, pltpu.roll(hre[last, :], 1, 0))
        hp_i = jnp.where(row_id == 0, 0.0, pltpu.roll(him[last, :], 1, 0))
        acc_r, acc_i = accumulate(lam_r, lam_i, hp_r, hp_i, acc_r, acc_i)

        lam_re16 = lre[...].astype(BF16)
        lam_im16 = lim[...].astype(BF16)
        dza_ref[...] = (_dot(lam_re16, bre_ref[0], tb=True) + _dot(lam_im16, bim_ref[0], tb=True)
                        + skip_ref[...]).astype(BF16)

        @pl.when(pl.program_id(1) == 0)
        def _():
            for ref in (dbre_ref, dbim_ref, dcre_ref, dcim_ref, dare, daim):
                ref[...] = jnp.zeros_like(ref)

        za = za_ref[...]
        dbre_ref[0] += _dot(za, lam_re16, ta=True)
        dbim_ref[0] += _dot(za, lam_im16, ta=True)
        dcre_ref[0] += _dot(hre[...], dy, ta=True)
        dcim_ref[0] += _dot(him[...], dy, ta=True)
        dare[...] += acc_r
        daim[...] += acc_i

    rows_ch = pl.BlockSpec((seq, ch), lambda j, s: (s, j))
    rows_st = pl.BlockSpec((seq, st), lambda j, s: (s, j))
    vec = pl.BlockSpec((1, st), lambda j, s: (0, j))
    b_blk = pl.BlockSpec((1, ch, st), lambda j, s: (j, 0, 0))
    c_blk = pl.BlockSpec((1, st, ch), lambda j, s: (j, 0, 0))
    acc = pl.BlockSpec((N_SEG, st), lambda j, s: (0, j))
    return pl.pallas_call(
        body, name=name,
        out_shape=(jax.ShapeDtypeStruct((t, n_sg * ch), BF16),
                   jax.ShapeDtypeStruct((n_sg, ch, st), F32), jax.ShapeDtypeStruct((n_sg, ch, st), F32),
                   jax.ShapeDtypeStruct((n_sg, st, ch), F32), jax.ShapeDtypeStruct((n_sg, st, ch), F32),
                   jax.ShapeDtypeStruct((N_SEG, n_sg * st), F32), jax.ShapeDtypeStruct((N_SEG, n_sg * st), F32)),
        grid=(n_sg, n_seq),
        in_specs=[rows_ch, rows_ch, rows_st, rows_st, rows_ch, b_blk, b_blk, vec, vec, c_blk, c_blk],
        out_specs=(rows_ch, b_blk, b_blk, c_blk, c_blk, acc, acc),
        scratch_shapes=[pltpu.VMEM((seq, st), F32), pltpu.VMEM((seq, st), F32)],
        compiler_params=_params(2))(dy16, dza_skip, h_re, h_im, za16, b_re, b_im, a_re, a_im, c_re, c_im)


def _s5_post_fwd(yssm, u, d_skip, glu_w, glu_b, name):
    t, w = yssm.shape
    tr = _pick(t, 512, 8)

    def body(y_ref, u_ref, d_ref, w_ref, b_ref, o_ref):
        yg = _gelu(y_ref[...] + d_ref[...] * u_ref[...])
        pre = _dot(yg, w_ref[...]) + b_ref[...]
        o_ref[...] = yg * _sigmoid(pre)

    return pl.pallas_call(
        body, name=name, out_shape=jax.ShapeDtypeStruct((t, w), F32), grid=(t // tr,),
        in_specs=[_rows(tr, w), _rows(tr, w), _whole((1, w)), _whole((w, w)), _whole((1, w))],
        out_specs=_rows(tr, w), compiler_params=_params())(yssm, u, d_skip, glu_w, glu_b)


def _s5_post_bwd(yssm, u, dout, d_skip, glu_w, glu_b, name):
    t, w = yssm.shape
    tr = _pick(t, 512, 8)

    def body(y_ref, u_ref, do_ref, d_ref, w_ref, b_ref, dy_ref, du_ref, dw_ref, dd_ref, db_ref):
        uu = u_ref[...]
        y = y_ref[...] + d_ref[...] * uu
        yg = _gelu(y)
        sg = _sigmoid(_dot(yg, w_ref[...]) + b_ref[...])
        do = do_ref[...]
        dpre = do * yg * sg * (1.0 - sg)
        dyg = do * sg + _dot(dpre, w_ref[...], tb=True)
        dy = dyg * _gelu_grad(y)
        dy_ref[...] = dy.astype(BF16)
        du_ref[...] = dy * d_ref[...]

        @pl.when(pl.program_id(0) == 0)
        def _():
            dw_ref[...] = jnp.zeros_like(dw_ref)
            dd_ref[...] = jnp.zeros_like(dd_ref)
            db_ref[...] = jnp.zeros_like(db_ref)

        dw_ref[...] += _dot(yg, dpre, ta=True)
        dd_ref[...] += _fold8(dy * uu)
        db_ref[...] += _fold8(dpre)

    return pl.pallas_call(
        body, name=name,
        out_shape=(jax.ShapeDtypeStruct((t, w), BF16), jax.ShapeDtypeStruct((t, w), F32),
                   jax.ShapeDtypeStruct((w, w), F32), jax.ShapeDtypeStruct((8, w), F32),
                   jax.ShapeDtypeStruct((8, w), F32)),
        grid=(t // tr,),
        in_specs=[_rows(tr, w), _rows(tr, w), _rows(tr, w), _whole((1, w)), _whole((w, w)),
                  _whole((1, w))],
        out_specs=(_rows(tr, w), _rows(tr, w), _whole((w, w)), _whole((8, w)), _whole((8, w))),
        compiler_params=_params())(yssm, u, dout, d_skip, glu_w, glu_b)


def _head_select(parts, head_dim):
    col_head = lax.broadcasted_iota(jnp.int32, parts[0].shape, 1) // head_dim
    out = jnp.zeros_like(parts[0])
    for h, part in enumerate(parts):
        out = jnp.where(col_head == h, part, out)
    return out


def _gmlp_fwd(proj, ln_g, ln_b, ws, bs_cols, name):
    t = proj.shape[0]
    n_heads = ws.shape[0]
    w = bs_cols.shape[1]
    head_dim = w // n_heads
    cpb = _pick(t // CHUNK, 4, 1)
    tr = cpb * CHUNK

    def body(zu_ref, zv_ref, g_ref, b_ref, ws_ref, bs_ref, o_ref):
        for c in range(cpb):
            rows = pl.ds(c * CHUNK, CHUNK)
            xhat, _ = _ln_stats(_gelu(zv_ref[rows, :]))
            vn = (xhat * g_ref[...] + b_ref[...]).astype(BF16)
            s = _head_select([_dot(ws_ref[h], vn) for h in range(n_heads)], head_dim) + bs_ref[...]
            o_ref[rows, :] = _gelu(zu_ref[rows, :]) * s

    return pl.pallas_call(
        body, name=name, out_shape=jax.ShapeDtypeStruct((t, w), F32), grid=(t // tr,),
        in_specs=[_rows(tr, w, 1), _rows(tr, w, 2), _whole((1, w)), _whole((1, w)),
                  _whole(ws.shape), _whole(bs_cols.shape)],
        out_specs=_rows(tr, w), compiler_params=_params())(proj, proj, ln_g, ln_b, ws, bs_cols)


def _gmlp_bwd(proj, dout, ln_g, ln_b, ws, ws_t, bs_cols, name):
    t = proj.shape[0]
    n_heads = ws.shape[0]
    w = bs_cols.shape[1]
    head_dim = w // n_heads
    cpb = _pick(t // CHUNK, 4, 1)
    tr = cpb * CHUNK

    def body(zu_ref, zv_ref, do_ref, g_ref, b_ref, ws_ref, wst_ref, bs_ref,
             dzu_ref, dzv_ref, dws_ref, dbs_ref, dg_ref, dbeta_ref):
        @pl.when(pl.program_id(0) == 0)
        def _():
            dws_ref[...] = jnp.zeros_like(dws_ref)
            dbs_ref[...] = jnp.zeros_like(dbs_ref)
            dg_ref[...] = jnp.zeros_like(dg_ref)
            dbeta_ref[...] = jnp.zeros_like(dbeta_ref)

        col_head = lax.broadcasted_iota(jnp.int32, (CHUNK, w), 1) // head_dim
        for c in range(cpb):
            rows = pl.ds(c * CHUNK, CHUNK)
            zu, zv, do = zu_ref[rows, :], zv_ref[rows, :], do_ref[rows, :]
            xhat, rstd = _ln_stats(_gelu(zv))
            vn = (xhat * g_ref[...] + b_ref[...]).astype(BF16)
            s = _head_select([_dot(ws_ref[h], vn) for h in range(n_heads)], head_dim) + bs_ref[...]
            dzu_ref[rows, :] = (do * s * _gelu_grad(zu)).astype(BF16)
            ds = do * _gelu(zu)
            ds16 = ds.astype(BF16)
            dbs_ref[...] += ds
            for h in range(n_heads):
                dws_ref[h] += _dot(jnp.where(col_head == h, ds16, jnp.zeros_like(ds16)), vn, tb=True)
            dvn = _head_select([_dot(wst_ref[h], ds16) for h in range(n_heads)], head_dim)
            dg_ref[...] += _fold8(dvn * xhat)
            dbeta_ref[...] += _fold8(dvn)
            dgv = _ln_bwd(dvn, xhat, rstd, g_ref[...])
            dzv_ref[rows, :] = (dgv * _gelu_grad(zv)).astype(BF16)

    return pl.pallas_call(
        body, name=name,
        out_shape=(jax.ShapeDtypeStruct((t, w), BF16), jax.ShapeDtypeStruct((t, w), BF16),
                   jax.ShapeDtypeStruct(ws.shape, F32), jax.ShapeDtypeStruct((CHUNK, w), F32),
                   jax.ShapeDtypeStruct((8, w), F32), jax.ShapeDtypeStruct((8, w), F32)),
        grid=(t // tr,),
        in_specs=[_rows(tr, w, 1), _rows(tr, w, 2), _rows(tr, w), _whole((1, w)), _whole((1, w)),
                  _whole(ws.shape), _whole(ws.shape), _whole(bs_cols.shape)],
        out_specs=(_rows(tr, w), _rows(tr, w), _whole(ws.shape), _whole((CHUNK, w)),
                   _whole((8, w)), _whole((8, w))),
        compiler_params=_params())(proj, proj, dout, ln_g, ln_b, ws, ws_t, bs_cols)


def _merge_fwd(s5out, gm, proj, up_a, up_b, name):
    t, w = s5out.shape
    d = up_a.shape[1]
    assert d == 2 * w
    tr = _pick(t, 512, 8)

    def body(s_ref, gm_ref, ga0, ga1, gb0, gb1, ua_ref, ub_ref, m_ref, ya_ref, yb_ref):
        ya = _dot(s_ref[...], ua_ref[...])
        yb = _dot(gm_ref[...], ub_ref[...])
        ya_ref[...] = ya.astype(BF16)
        yb_ref[...] = yb.astype(BF16)
        for half, (ga, gb) in enumerate(((ga0, gb0), (ga1, gb1))):
            cols = slice(half * w, (half + 1) * w)
            m_ref[:, cols] = (_sigmoid(ga[...]) * ya[:, cols] + _sigmoid(gb[...]) * yb[:, cols]).astype(BF16)

    return pl.pallas_call(
        body, name=name,
        out_shape=(jax.ShapeDtypeStruct((t, d), BF16), jax.ShapeDtypeStruct((t, d), BF16),
                   jax.ShapeDtypeStruct((t, d), BF16)),
        grid=(t // tr,),
        in_specs=[_rows(tr, w), _rows(tr, w), _rows(tr, w, 3), _rows(tr, w, 4), _rows(tr, w, 5),
                  _rows(tr, w, 6), _whole(up_a.shape), _whole(up_b.shape)],
        out_specs=(_rows(tr, d), _rows(tr, d), _rows(tr, d)),
        compiler_params=_params())(s5out, gm, proj, proj, proj, proj, up_a, up_b)


def _merge_bwd(dm, ya, yb, s5out, gm, proj, up_a, up_b, name):
    t, d = dm.shape
    w = d // 2
    tr = _pick(t, 512, 8)

    def body(dm_ref, ya_ref, yb_ref, s_ref, gm_ref, ga0, ga1, gb0, gb1, ua_ref, ub_ref,
             dga_ref, dgb_ref, ds_ref, dgm_ref, dua_ref, dub_ref):
        dm_v, ya, yb = dm_ref[...], ya_ref[...].astype(F32), yb_ref[...].astype(F32)
        dya, dyb = [], []
        for half, (ga, gb) in enumerate(((ga0, gb0), (ga1, gb1))):
            cols = slice(half * w, (half + 1) * w)
            sa, sb = _sigmoid(ga[...]), _sigmoid(gb[...])
            dmh = dm_v[:, cols]
            dga_ref[:, cols] = (dmh * ya[:, cols] * sa * (1.0 - sa)).astype(BF16)
            dgb_ref[:, cols] = (dmh * yb[:, cols] * sb * (1.0 - sb)).astype(BF16)
            dya.append((dmh * sa).astype(BF16))
            dyb.append((dmh * sb).astype(BF16))
        dya = jnp.concatenate(dya, axis=1)
        dyb = jnp.concatenate(dyb, axis=1)
        ds_ref[...] = _dot(dya, ua_ref[...], tb=True)
        dgm_ref[...] = _dot(dyb, ub_ref[...], tb=True)

        @pl.when(pl.program_id(0) == 0)
        def _():
            dua_ref[...] = jnp.zeros_like(dua_ref)
            dub_ref[...] = jnp.zeros_like(dub_ref)

        dua_ref[...] += _dot(s_ref[...], dya, ta=True)
        dub_ref[...] += _dot(gm_ref[...], dyb, ta=True)

    return pl.pallas_call(
        body, name=name,
        out_shape=(jax.ShapeDtypeStruct((t, d), BF16), jax.ShapeDtypeStruct((t, d), BF16),
                   jax.ShapeDtypeStruct((t, w), F32), jax.ShapeDtypeStruct((t, w), F32),
                   jax.ShapeDtypeStruct(up_a.shape, F32), jax.ShapeDtypeStruct(up_b.shape, F32)),
        grid=(t // tr,),
        in_specs=[_rows(tr, d), _rows(tr, d), _rows(tr, d), _rows(tr, w), _rows(tr, w),
                  _rows(tr, w, 3), _rows(tr, w, 4), _rows(tr, w, 5), _rows(tr, w, 6),
                  _whole(up_a.shape), _whole(up_b.shape)],
        out_specs=(_rows(tr, d), _rows(tr, d), _rows(tr, w), _rows(tr, w), _whole(up_a.shape),
                   _whole(up_b.shape)),
        compiler_params=_params())(dm, ya, yb, s5out, gm, proj, proj, proj, proj, up_a, up_b)


def _head(x3, r3, ln_g, p, target, w_gate, w_proj, out_scale, name):
    t, d = x3.shape
    pd = p.shape[1]
    tr = _pick(t, 512, 8)
    nb = t // tr

    def body(x_ref, r_ref, g_ref, p_ref, t_ref, wg_ref, wp_ref,
             loss_ref, dr_ref, drs_ref, dwg_ref, dwp_ref, dg_ref, dbeta_ref):
        xv = x_ref[...]
        sg = _sigmoid(_dot(xv, wg_ref[...]))
        pp = _dot(p_ref[...], wp_ref[...])
        err = xv + sg * pp - t_ref[...]
        loss_ref[...] = jnp.full((8, 128), 0.5 * jnp.sum(jnp.mean(err * err, axis=-1)), F32)
        dout = err * (1.0 / d)
        dgpre = dout * pp * sg * (1.0 - sg)
        dpp = dout * sg
        dx = dout + _dot(dgpre, wg_ref[...], tb=True)
        xhat, rstd = _ln_stats(r_ref[...])
        dr = _ln_bwd(dx, xhat, rstd, g_ref[...])
        dr_ref[...] = dr
        drs_ref[...] = (out_scale * dr).astype(BF16)

        @pl.when(pl.program_id(0) == 0)
        def _():
            for ref in (dwg_ref, dwp_ref, dg_ref, dbeta_ref):
                ref[...] = jnp.zeros_like(ref)

        dwg_ref[...] += _dot(xv, dgpre, ta=True)
        dwp_ref[...] += _dot(p_ref[...], dpp, ta=True)
        dg_ref[...] += _fold8(dx * xhat)
        dbeta_ref[...] += _fold8(dx)

    return pl.pallas_call(
        body, name=name,
        out_shape=(jax.ShapeDtypeStruct((nb * 8, 128), F32), jax.ShapeDtypeStruct((t, d), F32),
                   jax.ShapeDtypeStruct((t, d), BF16), jax.ShapeDtypeStruct((d, d), F32),
                   jax.ShapeDtypeStruct((pd, d), F32), jax.ShapeDtypeStruct((8, d), F32),
                   jax.ShapeDtypeStruct((8, d), F32)),
        grid=(nb,),
        in_specs=[_rows(tr, d), _rows(tr, d), _whole((1, d)), _rows(tr, pd), _rows(tr, d), _whole((d, d)),
                  _whole((pd, d))],
        out_specs=(pl.BlockSpec((8, 128), lambda i: (i, 0)), _rows(tr, d), _rows(tr, d), _whole((d, d)),
                   _whole((pd, d)), _whole((8, d)), _whole((8, d))),
        compiler_params=_params())(x3, r3, ln_g, p, target, w_gate, w_proj)


_HBM = pl.BlockSpec(memory_space=pltpu.HBM)


def _all_gather(shards, name):
    n = len(shards)

    def body(*refs):
        x_refs, out_refs = refs[:n], refs[n:2 * n]
        send_sems, recv_sems, local_sems = refs[2 * n:]
        x, y, c = lax.axis_index("x"), lax.axis_index("y"), lax.axis_index("c")
        me, sibling = (x, y, c), (x, y, 1 - c)
        chips = [(1 - x, y), (x, 1 - y), (1 - x, 1 - y)]

        def copy(a, k, block, to, own=False):
            slot = out_refs[a].at[4 * block[0] + 2 * block[1] + block[2]]
            return pltpu.make_async_remote_copy(
                src_ref=x_refs[a] if own else slot, dst_ref=slot,
                send_sem=send_sems.at[7 * a + k], recv_sem=recv_sems.at[7 * a + k],
                device_id=to, device_id_type=pl.DeviceIdType.MESH)

        mine = [pltpu.make_async_copy(x_refs[a], out_refs[a].at[4 * x + 2 * y + c], local_sems.at[a])
                for a in range(n)]
        sends = []
        for a in range(n):
            mine[a].start()
            first = [copy(a, 0, me, sibling, own=True)]
            first += [copy(a, 1 + j, me, (*chip, c), own=True) for j, chip in enumerate(chips)]
            for cp in first:
                cp.start()
            sends += first
        for a in range(n):
            for j, chip in enumerate(chips):
                copy(a, 1 + j, (*chip, c), me).wait_recv()
                passed = copy(a, 4 + j, (*chip, c), sibling)
                passed.start()
                sends.append(passed)
        for a in range(n):
            copy(a, 0, sibling, me).wait_recv()
            for j, chip in enumerate(chips):
                copy(a, 4 + j, (*chip, 1 - c), me).wait_recv()
        for cp in sends:
            cp.wait_send()
        for cp in mine:
            cp.wait()

    return pl.pallas_call(
        body, name=name,
        out_shape=tuple(jax.ShapeDtypeStruct((N_DEV,) + s.shape, s.dtype) for s in shards),
        in_specs=[_HBM] * n, out_specs=tuple([_HBM] * n),
        scratch_shapes=[pltpu.SemaphoreType.DMA((7 * n,)), pltpu.SemaphoreType.DMA((7 * n,)),
                        pltpu.SemaphoreType.DMA((n,))],
    )(*shards)


_SEM = pl.BlockSpec(memory_space=pltpu.SEMAPHORE)
_ANY = pl.BlockSpec(memory_space=pl.ANY)
_DATAFLOW = pltpu.SideEffectType.DATAFLOW_SIDE_EFFECTING


def _peer(k, x, y, c):
    return (1 - x if k & 4 else x, 1 - y if k & 2 else y, 1 - c if k & 1 else c)


def _exchange_start(sends, after, name):
    n = len(sends)
    lands = [lax.empty((N_DEV,) + s.shape[1:], s.dtype) for s in sends]

    def body(*refs):
        s_refs, l_refs = refs[:n], refs[n:2 * n]
        send_sems, recv_sems, local_sems, token = refs[2 * n + 1], refs[2 * n + 2], refs[2 * n + 3], refs[-1]
        x, y, c = lax.axis_index("x"), lax.axis_index("y"), lax.axis_index("c")
        me = 4 * x + 2 * y + c
        for a in range(n):
            same = sends[a].shape[0] == 1
            pltpu.make_async_copy(s_refs[a].at[0 if same else me], l_refs[a].at[me], local_sems.at[a]).start()
            for k in range(1, N_DEV):
                px, py, pc = _peer(k, x, y, c)
                pltpu.make_async_remote_copy(
                    src_ref=s_refs[a].at[0 if same else 4 * px + 2 * py + pc], dst_ref=l_refs[a].at[me],
                    send_sem=send_sems.at[7 * a + k - 1], recv_sem=recv_sems.at[7 * a + k - 1],
                    device_id=(px, py, pc), device_id_type=pl.DeviceIdType.MESH).start()
        token[...] = jnp.zeros_like(token)

    outs = pl.pallas_call(
        body, name=name,
        out_shape=(pltpu.SemaphoreType.DMA((7 * n,)), pltpu.SemaphoreType.DMA((7 * n,)),
                   pltpu.SemaphoreType.DMA((n,)),
                   *[pltpu.HBM(s.shape, s.dtype) for s in sends],
                   *[pltpu.HBM(l.shape, l.dtype) for l in lands],
                   jax.ShapeDtypeStruct((8, 128), F32)),
        in_specs=[_HBM] * (2 * n) + [_ANY],
        out_specs=(_SEM, _SEM, _SEM, *([_HBM] * (2 * n)), pl.BlockSpec(memory_space=pltpu.VMEM)),
        input_output_aliases={i: 3 + i for i in range(2 * n)},
        compiler_params=pltpu.CompilerParams(has_side_effects=_DATAFLOW),
    )(*[pltpu.with_memory_space_constraint(v, pltpu.HBM) for v in list(sends) + lands], after)
    return (outs[0], outs[1], outs[2], list(outs[3:3 + n]), list(outs[3 + n:3 + 2 * n])), outs[-1]


def _exchange_wait(handle, after, name):
    send_sems, recv_sems, local_sems, sends, lands = handle
    n = len(sends)

    def body(*refs):
        s_refs, l_refs = refs[:n], refs[n:2 * n]
        send_sems, recv_sems, local_sems = refs[2 * n], refs[2 * n + 1], refs[2 * n + 2]
        x, y, c = lax.axis_index("x"), lax.axis_index("y"), lax.axis_index("c")
        for a in range(n):
            pltpu.make_async_copy(s_refs[a].at[0], l_refs[a].at[0], local_sems.at[a]).wait()
            for k in range(1, N_DEV):
                copy = pltpu.make_async_remote_copy(
                    src_ref=s_refs[a].at[0], dst_ref=l_refs[a].at[0],
                    send_sem=send_sems.at[7 * a + k - 1], recv_sem=recv_sems.at[7 * a + k - 1],
                    device_id=_peer(k, x, y, c), device_id_type=pl.DeviceIdType.MESH)
                copy.wait_send()
                copy.wait_recv()

    outs = pl.pallas_call(
        body, name=name,
        out_shape=(*[pltpu.HBM(s.shape, s.dtype) for s in sends], *[pltpu.HBM(l.shape, l.dtype) for l in lands]),
        in_specs=[_HBM] * (2 * n) + [_SEM, _SEM, _SEM] + [_ANY] * len(after),
        out_specs=tuple([_HBM] * (2 * n)),
        input_output_aliases={i: i for i in range(2 * n)},
        compiler_params=pltpu.CompilerParams(has_side_effects=_DATAFLOW),
    )(*sends, *lands, send_sems, recv_sems, local_sems, *after)
    return list(outs[n:])


def _chips_of(x, y):
    return [(1 - x, y), (x, 1 - y), (1 - x, 1 - y)]


def _gather2_start(shards, after, name):
    n = len(shards)
    lands = [lax.empty((N_DEV,) + s.shape, s.dtype) for s in shards]

    def body(*refs):
        x_refs, l_refs = refs[:n], refs[n:2 * n]
        send_sems, recv_sems, local_sems, token = refs[2 * n + 1], refs[2 * n + 2], refs[2 * n + 3], refs[-1]
        x, y, c = lax.axis_index("x"), lax.axis_index("y"), lax.axis_index("c")
        me = 4 * x + 2 * y + c
        peers = [(x, y, 1 - c)] + [(*chip, c) for chip in _chips_of(x, y)]
        for a in range(n):
            pltpu.make_async_copy(x_refs[a], l_refs[a].at[me], local_sems.at[a]).start()
            for k, peer in enumerate(peers):
                pltpu.make_async_remote_copy(
                    src_ref=x_refs[a], dst_ref=l_refs[a].at[me],
                    send_sem=send_sems.at[4 * a + k], recv_sem=recv_sems.at[4 * a + k],
                    device_id=peer, device_id_type=pl.DeviceIdType.MESH).start()
        token[...] = jnp.zeros_like(token)

    outs = pl.pallas_call(
        body, name=name,
        out_shape=(pltpu.SemaphoreType.DMA((4 * n,)), pltpu.SemaphoreType.DMA((4 * n,)),
                   pltpu.SemaphoreType.DMA((n,)),
                   *[pltpu.HBM(s.shape, s.dtype) for s in shards],
                   *[pltpu.HBM(l.shape, l.dtype) for l in lands],
                   jax.ShapeDtypeStruct((8, 128), F32)),
        in_specs=[_HBM] * (2 * n) + [_ANY],
        out_specs=(_SEM, _SEM, _SEM, *([_HBM] * (2 * n)), pl.BlockSpec(memory_space=pltpu.VMEM)),
        input_output_aliases={i: 3 + i for i in range(2 * n)},
        compiler_params=pltpu.CompilerParams(has_side_effects=_DATAFLOW),
    )(*[pltpu.with_memory_space_constraint(v, pltpu.HBM) for v in list(shards) + lands], after)
    return (outs[0], outs[1], outs[2], list(outs[3:3 + n]), list(outs[3 + n:3 + 2 * n])), outs[-1]


def _gather2_forward(handle, after, name):
    send_sems, recv_sems, local_sems, shards, lands = handle
    n = len(shards)

    def body(*refs):
        x_refs, l_refs = refs[:n], refs[n:2 * n]
        send_sems, recv_sems = refs[2 * n], refs[2 * n + 1]
        out0 = 2 * n + 2 + len(after)
        fwd_send, fwd_recv = refs[out0], refs[out0 + 1]
        x, y, c = lax.axis_index("x"), lax.axis_index("y"), lax.axis_index("c")
        for a in range(n):
            for j, chip in enumerate(_chips_of(x, y)):
                block = l_refs[a].at[4 * chip[0] + 2 * chip[1] + c]
                pltpu.make_async_remote_copy(
                    src_ref=x_refs[a], dst_ref=block, send_sem=send_sems.at[4 * a + 1 + j],
                    recv_sem=recv_sems.at[4 * a + 1 + j], device_id=(*chip, c),
                    device_id_type=pl.DeviceIdType.MESH).wait_recv()
                pltpu.make_async_remote_copy(
                    src_ref=block, dst_ref=block, send_sem=fwd_send.at[3 * a + j], recv_sem=fwd_recv.at[3 * a + j],
                    device_id=(x, y, 1 - c), device_id_type=pl.DeviceIdType.MESH).start()

    outs = pl.pallas_call(
        body, name=name,
        out_shape=(pltpu.SemaphoreType.DMA((3 * n,)), pltpu.SemaphoreType.DMA((3 * n,)),
                   *[pltpu.HBM(s.shape, s.dtype) for s in shards], *[pltpu.HBM(l.shape, l.dtype) for l in lands]),
        in_specs=[_HBM] * (2 * n) + [_SEM, _SEM] + [_ANY] * len(after),
        out_specs=(_SEM, _SEM, *([_HBM] * (2 * n))),
        input_output_aliases={i: 2 + i for i in range(2 * n)},
        compiler_params=pltpu.CompilerParams(has_side_effects=_DATAFLOW),
    )(*shards, *lands, send_sems, recv_sems, *after)
    return (send_sems, recv_sems, local_sems, outs[0], outs[1], list(outs[2:2 + n]), list(outs[2 + n:]))


def _gather2_wait(handle, after, name):
    send_sems, recv_sems, local_sems, fwd_send, fwd_recv, shards, lands = handle
    n = len(shards)

    def body(*refs):
        x_refs, l_refs = refs[:n], refs[n:2 * n]
        send_sems, recv_sems, local_sems, fwd_send, fwd_recv = refs[2 * n:2 * n + 5]
        x, y, c = lax.axis_index("x"), lax.axis_index("y"), lax.axis_index("c")
        sibling = (x, y, 1 - c)

        def copy(a, send_sem, recv_sem):
            return pltpu.make_async_remote_copy(
                src_ref=x_refs[a], dst_ref=l_refs[a].at[0], send_sem=send_sem, recv_sem=recv_sem,
                device_id=sibling, device_id_type=pl.DeviceIdType.MESH)

        for a in range(n):
            pltpu.make_async_copy(x_refs[a], l_refs[a].at[0], local_sems.at[a]).wait()
            for k in range(4):
                copy(a, send_sems.at[4 * a + k], recv_sems.at[4 * a + k]).wait_send()
            copy(a, send_sems.at[4 * a], recv_sems.at[4 * a]).wait_recv()
            for j in range(3):
                passed = copy(a, fwd_send.at[3 * a + j], fwd_recv.at[3 * a + j])
                passed.wait_send()
                passed.wait_recv()

    outs = pl.pallas_call(
        body, name=name,
        out_shape=(*[pltpu.HBM(s.shape, s.dtype) for s in shards], *[pltpu.HBM(l.shape, l.dtype) for l in lands]),
        in_specs=[_HBM] * (2 * n) + [_SEM] * 5 + [_ANY] * len(after),
        out_specs=tuple([_HBM] * (2 * n)),
        input_output_aliases={i: i for i in range(2 * n)},
        compiler_params=pltpu.CompilerParams(has_side_effects=_DATAFLOW),
    )(*shards, *lands, send_sems, recv_sems, local_sems, fwd_send, fwd_recv, *after)
    return list(outs[n:])


def _adamw(recv, w, m, v, name):
    n, rows, width = recv.shape
    tr = _pick(rows, max(8, ADAM_BLOCK_BYTES // (n * width * recv.dtype.itemsize)), 8)
    c_m = 1.0 - ADAM_B1 ** ADAM_STEP
    c_v = 1.0 - ADAM_B2 ** ADAM_STEP

    def body(r_ref, w_ref, m_ref, v_ref, g_ref, d_ref, nm_ref, nv_ref):
        g = r_ref[0].astype(F32)
        for i in range(1, n):
            g = g + r_ref[i].astype(F32)
        m2 = ADAM_B1 * m_ref[...] + (1.0 - ADAM_B1) * g
        v2 = ADAM_B2 * v_ref[...] + (1.0 - ADAM_B2) * (g * g)
        m_hat = m2 / c_m
        v_hat = v2 / c_v
        g_ref[...] = g
        d_ref[...] = -ADAM_LR * (m_hat / (jnp.sqrt(v_hat) + ADAM_EPS) + ADAM_WD * w_ref[...])
        nm_ref[...] = m2
        nv_ref[...] = v2

    blk = _rows(tr, width)
    shp = jax.ShapeDtypeStruct((rows, width), F32)
    return pl.pallas_call(
        body, name=name, out_shape=(shp, shp, shp, shp), grid=(rows // tr,),
        in_specs=[pl.BlockSpec((n, tr, width), lambda i: (0, i, 0)), blk, blk, blk],
        out_specs=(blk, blk, blk, blk), compiler_params=_params())(recv, w, m, v)


def _join_cols(gathered):
    n, r, c = gathered.shape
    return gathered.transpose(1, 0, 2).reshape(r, n * c)


def _split_cols(full):
    r, c = full.shape
    return full.reshape(r, N_DEV, c // N_DEV).transpose(1, 0, 2)


def _adamw_small(items, name):
    n = len(items)
    c_m = 1.0 - ADAM_B1 ** ADAM_STEP
    c_v = 1.0 - ADAM_B2 ** ADAM_STEP

    def body(*refs):
        ins, outs = refs[:4 * n], refs[4 * n:]
        for k in range(n):
            r_ref, w_ref, m_ref, v_ref = ins[4 * k:4 * k + 4]
            g = r_ref[0].astype(F32)
            for i in range(1, N_DEV):
                g = g + r_ref[i].astype(F32)
            m2 = ADAM_B1 * m_ref[...] + (1.0 - ADAM_B1) * g
            v2 = ADAM_B2 * v_ref[...] + (1.0 - ADAM_B2) * (g * g)
            outs[4 * k][...] = g
            outs[4 * k + 1][...] = -ADAM_LR * ((m2 / c_m) / (jnp.sqrt(v2 / c_v) + ADAM_EPS) + ADAM_WD * w_ref[...])
            outs[4 * k + 2][...] = m2
            outs[4 * k + 3][...] = v2

    vmem = pl.BlockSpec(memory_space=pltpu.VMEM)
    flat = pl.pallas_call(
        body, name=name,
        out_shape=tuple(jax.ShapeDtypeStruct(w.shape, F32) for _, w, _, _ in items for _ in range(4)),
        in_specs=[vmem] * (4 * n), out_specs=tuple([vmem] * (4 * n)),
        compiler_params=pltpu.CompilerParams(vmem_limit_bytes=VMEM_LIMIT_BYTES),
    )(*[a for item in items for a in item])
    return [tuple(flat[4 * k:4 * k + 4]) for k in range(n)]


def _discretise(lam_re, lam_im, log_dt, b_re, b_im):
    dt = jnp.exp(log_dt)[:, None]
    mag = jnp.exp(lam_re * dt)
    ab_re = mag * jnp.cos(lam_im * dt)
    ab_im = mag * jnp.sin(lam_im * dt)
    nr = ab_re - 1.0
    ni = ab_im
    den = lam_re * lam_re + lam_im * lam_im
    coef_re = ((nr * lam_re + ni * lam_im) / den)[..., None]
    coef_im = ((ni * lam_re - nr * lam_im) / den)[..., None]
    bb_re = coef_re * b_re - coef_im * b_im
    bb_im = coef_re * b_im + coef_im * b_re
    return ab_re, ab_im, bb_re, bb_im


def _same_group(gl):
    return jnp.eye(gl, dtype=bool)[None, :, None, :, None]


def _super_groups_in(bb, gl):
    g, p, i = bb.shape
    blocks = bb.reshape(g // gl, gl, p, i).transpose(0, 1, 3, 2)[:, :, :, None, :]
    return jnp.where(_same_group(gl), blocks, 0.0).reshape(g // gl, gl * i, gl * p)


def _super_groups_in_take(dense, gl, p, i):
    n_sg = dense.shape[0]
    blocks = jnp.where(_same_group(gl), dense.reshape(n_sg, gl, i, gl, p), 0.0).sum(axis=3)
    return blocks.transpose(0, 1, 3, 2).reshape(n_sg * gl, p, i)


def _super_groups_out(cc, gl):
    g, i, p = cc.shape
    blocks = cc.reshape(g // gl, gl, i, p).transpose(0, 1, 3, 2)[:, :, :, None, :]
    return jnp.where(_same_group(gl), blocks, 0.0).reshape(g // gl, gl * p, gl * i)


def _super_groups_out_take(dense, gl, i, p):
    n_sg = dense.shape[0]
    blocks = jnp.where(_same_group(gl), dense.reshape(n_sg, gl, p, gl, i), 0.0).sum(axis=3)
    return blocks.transpose(0, 1, 3, 2).reshape(n_sg * gl, i, p)


def _perm_rows(z, n_seq):
    t, w = z.shape
    steps = t // n_seq // N_SEG
    return z.reshape(n_seq, N_SEG, steps, w).transpose(0, 2, 1, 3).reshape(t, w)


def _unperm_rows(z, n_seq):
    t, w = z.shape
    steps = t // n_seq // N_SEG
    return z.reshape(n_seq, steps, N_SEG, w).transpose(0, 2, 1, 3).reshape(t, w)


def kernel(*args):
    assert len(args) == len(INPUT_NAMES)
    inp = dict(zip(INPUT_NAMES, args))
    depth = inp["ffn1_w_in"].shape[0]
    assert depth == 1
    alpha = (2.0 * depth) ** 0.25

    n_seq, seq, d_model = inp["x"].shape
    t = n_seq * seq
    x = inp["x"].reshape(t, d_model)
    x16 = x.astype(BF16)
    p16 = inp["p"][0].reshape(t, -1).astype(BF16)
    target = inp["loss_target"].reshape(t, d_model)
    wts = {n: inp[n][0] if n in SHARDED else inp[n] for n in WEIGHTS}
    mom = {n: inp["m_" + n][0] if n in SHARDED else inp["m_" + n] for n in WEIGHTS}
    var = {n: inp["v_" + n][0] if n in SHARDED else inp["v_" + n] for n in WEIGHTS}

    full = {}

    def place(n, g):
        if n in ("ffn1_w_in", "ffn2_w_in"):
            full[n] = g.reshape((2, N_DEV // 2) + g.shape[1:])
        elif n in ("ffn1_w_out", "ffn2_w_out"):
            full[n] = g.reshape(N_DEV // 2, 2 * g.shape[1], g.shape[2])
        elif n in COL_SHARDED:
            full[n] = _join_cols(g)
        else:
            full[n] = g.reshape(N_DEV * g.shape[1], g.shape[2])

    w16 = dict(zip(GATHER_FIRST, _to_bf16([wts[n] for n in GATHER_FIRST], "cast_ffn1")))
    later = tuple(n for n in SHARDED if n not in GATHER_FIRST)
    w16.update(zip(later, _to_bf16([wts[n] for n in later], "cast_rest")))
    for n, g in zip(GATHER_FIRST, _all_gather([w16[n] for n in GATHER_FIRST], "gather_ffn1")):
        place(n, g)

    def gather_start(names, after, name):
        return _exchange_start([w16[n][None] for n in names], after, name)

    def gather_wait(names, handle, after, name):
        for n, g in zip(names, _exchange_wait(handle, after, name)):
            place(n, g)

    def gather2_wait(names, handle, after, name):
        for n, g in zip(names, _gather2_wait(handle, after, name)):
            place(n, g)

    gather_early, gather_early_token = _gather2_start([w16[n] for n in GATHER_EARLY], full["ffn1_w_in"],
                                                      "gather_early_start")

    def row(v):
        return v.reshape(1, -1)

    _, n_groups, n_state = wts["ssm_lambda_re"].shape
    n_ch = wts["ssm_b_re"].shape[3]
    disc_in = tuple(wts[n][0] for n in ("ssm_lambda_re", "ssm_lambda_im", "ssm_log_dt", "ssm_b_re", "ssm_b_im"))
    (ab_re, ab_im, bb_re, bb_im), disc_vjp = jax.vjp(_discretise, *disc_in)
    a_re, a_im = row(ab_re), row(ab_im)
    gl = S5_CHANNELS_PER_STEP // n_ch
    b_sg_re = _super_groups_in(bb_re, gl).astype(BF16)
    b_sg_im = _super_groups_in(bb_im, gl).astype(BF16)
    c_sg_re = _super_groups_out(wts["ssm_c_re"][0], gl).astype(BF16)
    c_sg_im = _super_groups_out(-wts["ssm_c_im"][0], gl).astype(BF16)

    ws = wts["gmlp_w_s"][0]
    n_heads = ws.shape[0]
    d_gmlp = wts["gmlp_ln_g"].shape[1]
    causal = jnp.tril(jnp.ones((CHUNK, CHUNK), dtype=bool))
    ws_masked = jnp.where(causal[None], ws, 0.0).astype(BF16)
    ws_masked_t = ws_masked.transpose(0, 2, 1)
    bs_cols = jnp.repeat(wts["gmlp_b_s"][0].T, d_gmlp // n_heads, axis=1)
    d_ssm = n_groups * n_ch
    assert d_ssm == d_gmlp and d_model == 2 * d_ssm

    h1, a1 = _ffn_in(x16, full["ffn1_w_in"], "ffn1_in", token=gather_early_token)
    gather_early = _gather2_forward(gather_early, [a1], "gather_early_forward")
    gather2_wait(GATHER_EARLY, gather_early, [a1], "gather_early_wait")
    r1, x1, x1_16 = _ffn_out_ln(a1, full["ffn1_w_out"], x, row(wts["ln1_g"]), row(wts["ln1_b"]), alpha,
                                "ffn1_out_ln")

    gather_mid, token = gather_start(GATHER_MID, x1_16, "gather_mid_start")
    gather_last, token = _gather2_start([w16[n] for n in GATHER_LAST], token, "gather_last_start")
    proj = _mm(x1_16, full["mix_w_in"], name="mix_in", token=token)
    za_p = _perm_rows(proj[:, :d_ssm], n_seq)
    za_p16 = za_p.astype(BF16)
    h_re, h_im, yssm = _s5_fwd(za_p16, b_sg_re, b_sg_im, a_re, a_im, c_sg_re, c_sg_im, n_seq, "s5_fwd")
    gather_wait(GATHER_MID, gather_mid, [yssm], "gather_mid_wait")
    s5out_p = _s5_post_fwd(yssm, za_p, row(wts["ssm_d"]), full["ssm_glu_w"], row(wts["ssm_glu_b"]),
                           "s5_post")
    s5out = _unperm_rows(s5out_p, n_seq)
    gm = _gmlp_fwd(proj, row(wts["gmlp_ln_g"]), row(wts["gmlp_ln_b"]), ws_masked, bs_cols, "gmlp")
    m_mix, y_a, y_b = _merge_fwd(s5out, gm, proj, full["up_a"], full["up_b"], "merge")
    gather_last = _gather2_forward(gather_last, [m_mix], "gather_last_forward")
    r2, x2, x2_16 = _ffn_out_ln(m_mix[None], full["mix_w_out"][None], x1, row(wts["ln2_g"]), row(wts["ln2_b"]),
                                alpha, "mix_out_ln2", scale=1.0)

    gather2_wait(GATHER_LAST, gather_last, [x2_16], "gather_last_wait")
    h2, a2 = _ffn_in(x2_16, full["ffn2_w_in"], "ffn2_in")
    r3, x3, _ = _ffn_out_ln(a2, full["ffn2_w_out"], x2, row(wts["ln3_g"]), row(wts["ln3_b"]), alpha,
                            "ffn2_out_ln")

    small = {}
    send = {}

    def lane_dense(n, v):
        return v.reshape(v.shape[:2] + (-1,)) if n in ("ssm_b_re", "ssm_b_im") else v

    def on_wire(n, g):
        g = lane_dense(n, g)
        return (g.astype(BF16) if n in SMALL_BF16 else g)[None]
    loss_parts, dr3, dr3_h, dw_gate, dw_proj, dg, db = _head(
        x3, r3, row(wts["ln3_g"]), p16, target, full["ple_w_gate"], full["ple_w_proj"], 0.5, "head")
    loss_mine = jnp.full((1, 1, 8, 128), jnp.sum(loss_parts[::8, 0]), F32)
    send["ple_w_gate"] = dw_gate.astype(BF16).reshape(N_DEV, -1, d_model)
    send["ple_w_proj"] = _split_cols(dw_proj.astype(BF16))
    small["ln3_g"], small["ln3_b"] = dg.sum(0, keepdims=True), db.sum(0, keepdims=True)
    dh2 = _ffn_out_dx(dr3_h, full["ffn2_w_out"], h2, "ffn2_out_dx")
    dw = _ffn_out_dw(a2, dr3_h, "ffn2_out_dw")
    send["ffn2_w_out"] = dw.reshape(N_DEV, -1, d_model)
    dw = _ffn_in_dw(x2_16, dh2, "ffn2_in_dw")
    send["ffn2_w_in"] = dw.reshape((N_DEV,) + dw.shape[2:])
    group1 = ("ffn2_w_in", "ffn2_w_out", "ple_w_gate", "ple_w_proj")
    exchange1, token = _exchange_start(
        [send[n] for n in group1] + [on_wire(n, small[n]) for n in SMALL_HEAD] + [loss_mine], dh2,
        "grad_exchange1_start")
    dr2, dr2_h, dg, db = _ffn_in_dx_ln(dh2, full["ffn2_w_in"], r2, dr3, alpha, row(wts["ln2_g"]), 1.0,
                                       "ffn2_in_dx_ln2_bwd", token=token)
    small["ln2_g"], small["ln2_b"] = dg.sum(0, keepdims=True), db.sum(0, keepdims=True)
    dm = _mm(dr2_h, full["mix_w_out"], tb=True, name="mix_out_dx")
    send["mix_w_out"] = _mm(m_mix, dr2_h, ta=True, name="mix_out_dw", out_dtype=BF16).reshape(
        N_DEV, -1, d_model)
    dga, dgb, ds5out, dgm, dw_a, dw_b = _merge_bwd(
        dm, y_a, y_b, s5out, gm, proj, full["up_a"], full["up_b"], "merge_bwd")
    send["up_a"] = _split_cols(dw_a.astype(BF16))
    send["up_b"] = _split_cols(dw_b.astype(BF16))

    dzu, dzv, dws, dbs_cols, dg, db = _gmlp_bwd(
        proj, dgm, row(wts["gmlp_ln_g"]), row(wts["gmlp_ln_b"]), ws_masked, ws_masked_t, bs_cols,
        "gmlp_bwd")
    small["gmlp_ln_g"], small["gmlp_ln_b"] = dg.sum(0, keepdims=True), db.sum(0, keepdims=True)
    small["gmlp_w_s"] = jnp.where(causal[None], dws, 0.0)[None]
    small["gmlp_b_s"] = dbs_cols.reshape(CHUNK, n_heads, -1).sum(-1).T[None]

    ds5out_p = _perm_rows(ds5out, n_seq)
    dy_p, dza_skip, dw_glu, dd, dgb_glu = _s5_post_bwd(
        yssm, za_p, ds5out_p, row(wts["ssm_d"]), full["ssm_glu_w"], row(wts["ssm_glu_b"]),
        "s5_post_bwd")
    send["ssm_glu_w"] = dw_glu.astype(BF16).reshape(N_DEV, -1, d_ssm)
    small["ssm_d"], small["ssm_glu_b"] = dd.sum(0, keepdims=True), dgb_glu.sum(0, keepdims=True)
    dza_p, dbb_re, dbb_im, dc_re, dc_im, da_re, da_im = _s5_bwd(
        dy_p, dza_skip, h_re, h_im, za_p16, b_sg_re, b_sg_im, a_re, a_im, c_sg_re, c_sg_im, n_seq, "s5_bwd")
    small["ssm_c_re"] = _super_groups_out_take(dc_re, gl, n_ch, n_state)[None]
    small["ssm_c_im"] = -_super_groups_out_take(dc_im, gl, n_ch, n_state)[None]
    dbb_re = _super_groups_in_take(dbb_re, gl, n_state, n_ch)
    dbb_im = _super_groups_in_take(dbb_im, gl, n_state, n_ch)
    dab_re = da_re.sum(0).reshape(n_groups, n_state)
    dab_im = da_im.sum(0).reshape(n_groups, n_state)
    for n, g in zip(("ssm_lambda_re", "ssm_lambda_im", "ssm_log_dt", "ssm_b_re", "ssm_b_im"),
                    disc_vjp((dab_re, dab_im, dbb_re, dbb_im))):
        small[n] = g[None]

    dproj = jnp.concatenate([_unperm_rows(dza_p, n_seq), dzu, dzv, dga, dgb], axis=1)
    group2 = ("mix_w_out", "up_a", "up_b", "ssm_glu_w")
    exchange2, token = _exchange_start(
        [send[n] for n in group2] + [on_wire(n, small[n]) for n in SMALL_MID], dproj, "grad_exchange2_start")
    send["mix_w_in"] = _split_cols(_mm(x1_16, dproj, ta=True, name="mix_in_dw", out_dtype=BF16, token=token))
    exchange3, token = _exchange_start([send["mix_w_in"]], dproj, "grad_exchange3_start")
    dr1, dr1_h, dg, db = _mm_ln_bwd(dproj, full["mix_w_in"], r1, dr2, alpha, row(wts["ln1_g"]), 0.5,
                                    "mix_in_dx_ln1_bwd", token=token)
    small["ln1_g"], small["ln1_b"] = dg.sum(0, keepdims=True), db.sum(0, keepdims=True)
    dw = _ffn_out_dw(a1, dr1_h, "ffn1_out_dw")
    send["ffn1_w_out"] = dw.reshape(N_DEV, -1, d_model)
    exchange4, token = _exchange_start([send["ffn1_w_out"]] + [on_wire(n, small[n]) for n in SMALL_LATE], dr1_h,
                                       "grad_exchange4_start")
    dh1 = _ffn_out_dx(dr1_h, full["ffn1_w_out"], h1, "ffn1_out_dx", token=token)
    dw = _ffn_in_dw(x16, dh1, "ffn1_in_dw")
    send["ffn1_w_in"] = dw.reshape((N_DEV,) + dw.shape[2:])
    exchange5, token = _exchange_start([send["ffn1_w_in"]], dh1, "grad_exchange5_start")
    grad_x = _ffn_in_dx(dh1, full["ffn1_w_in"], dr1, alpha, "ffn1_in_dx", token=token)

    results = {}

    def update(names, recv, prefix):
        for n, r in zip(names, recv):
            results[n] = _adamw(r, wts[n], mom[n], var[n], prefix + n)

    def update_small(names, recv, name):
        items = [(r, lane_dense(n, wts[n]), lane_dense(n, mom[n]), lane_dense(n, var[n])) for n, r in zip(names, recv)]
        for n, outs in zip(names, _adamw_small(items, name)):
            results[n] = tuple(o.reshape(wts[n].shape) for o in outs)

    def done(names):
        return [results[n][3] for n in names]

    recv = _exchange_wait(exchange1, [grad_x], "grad_exchange1_wait")
    update(group1, recv[:len(group1)], "adamw_")
    update_small(SMALL_HEAD, recv[len(group1):-1], "adamw_ln3")
    loss = jnp.sum(recv[-1][:, 0, 0, 0])
    recv = _exchange_wait(exchange2, done(group1 + SMALL_HEAD), "grad_exchange2_wait")
    update(group2, recv[:len(group2)], "adamw_")
    update_small(SMALL_MID, recv[len(group2):], "adamw_small")
    recv = _exchange_wait(exchange3, done(group2 + SMALL_MID), "grad_exchange3_wait")
    update(("mix_w_in",), recv, "adamw_")
    recv = _exchange_wait(exchange4, done(("mix_w_in",)), "grad_exchange4_wait")
    update(("ffn1_w_out",), recv[:1], "adamw_")
    update_small(SMALL_LATE, recv[1:], "adamw_ln1")
    recv = _exchange_wait(exchange5, done(("ffn1_w_out",) + SMALL_LATE), "grad_exchange5_wait")
    update(("ffn1_w_in",), recv, "adamw_")

    outs = [loss, grad_x.reshape(n_seq, seq, d_model)]
    for k in range(4):
        outs += [results[n][k][None] if n in SHARDED else results[n][k] for n in WEIGHTS]
    return tuple(outs)
```

```python
import functools
import math

import jax
import jax.numpy as jnp
from jax import lax
from jax.experimental import pallas as pl
from jax.experimental.pallas import tpu as pltpu

F32 = jnp.float32
BF16 = jnp.bfloat16

N_DEV = 8
LN_EPS = 1e-5
CHUNK = 128
N_SEG = 8
S5_CHANNELS_PER_STEP = 128
VMEM_LIMIT_BYTES = 56 * 1024 * 1024
MM_OPERAND_BLOCK_BYTES = 8 * 1024 * 1024
ADAM_BLOCK_BYTES = 6 * 1024 * 1024

ADAM_LR = 0.001
ADAM_B1 = 0.9
ADAM_B2 = 0.999
ADAM_EPS = 1e-08
ADAM_WD = 0.01
ADAM_STEP = 10

COL_SHARDED = ("ffn1_w_in", "mix_w_in", "up_a", "up_b", "ffn2_w_in", "ple_w_proj")
ROW_SHARDED = ("ffn1_w_out", "ssm_glu_w", "mix_w_out", "ffn2_w_out", "ple_w_gate")
SHARDED = ("ffn1_w_in", "ffn1_w_out", "mix_w_in", "ssm_glu_w", "up_a", "up_b", "mix_w_out",
           "ffn2_w_in", "ffn2_w_out", "ple_w_proj", "ple_w_gate")
WEIGHTS = ("ffn1_w_in", "ffn1_w_out", "ln1_g", "ln1_b", "mix_w_in", "ssm_lambda_re", "ssm_lambda_im",
           "ssm_log_dt", "ssm_b_re", "ssm_b_im", "ssm_c_re", "ssm_c_im", "ssm_d", "ssm_glu_w",
           "ssm_glu_b", "gmlp_ln_g", "gmlp_ln_b", "gmlp_w_s", "gmlp_b_s", "up_a", "up_b", "mix_w_out",
           "ln2_g", "ln2_b", "ffn2_w_in", "ffn2_w_out", "ln3_g", "ln3_b", "ple_w_proj", "ple_w_gate")
GATHER_FIRST = ("ffn1_w_in",)
GATHER_EARLY = ("ffn1_w_out", "mix_w_in")
GATHER_MID = ("ssm_glu_w", "up_a", "up_b", "mix_w_out")
GATHER_LAST = ("ffn2_w_in", "ffn2_w_out", "ple_w_proj", "ple_w_gate")
SMALL = tuple(n for n in WEIGHTS if n not in SHARDED)
SMALL_HEAD = ("ln3_g", "ln3_b")
SMALL_LATE = ("ln1_g", "ln1_b")
SMALL_MID = tuple(n for n in SMALL if n not in SMALL_HEAD + SMALL_LATE)
SMALL_BF16 = ("gmlp_w_s", "ssm_b_re", "ssm_b_im", "ssm_c_re", "ssm_c_im")
INPUT_NAMES = ("x", "p") + WEIGHTS + ("loss_target",) + tuple("m_" + n for n in WEIGHTS) + tuple(
    "v_" + n for n in WEIGHTS)


def _pick(dim, pref, mult=128):
    if dim <= pref:
        return dim
    d = (pref // mult) * mult
    while d >= mult:
        if dim % d == 0:
            return d
        d -= mult
    return dim


def _params(n_axes=1):
    return pltpu.CompilerParams(dimension_semantics=("arbitrary",) * n_axes,
                                vmem_limit_bytes=VMEM_LIMIT_BYTES)


def _sigmoid(v):
    return 1.0 / (1.0 + jnp.exp(-v))


_GELU_C = math.sqrt(2.0 / math.pi)


def _gelu(v):
    return 0.5 * v * (1.0 + jnp.tanh(_GELU_C * (v + 0.044715 * (v * v * v))))


def _gelu_grad(v):
    t = jnp.tanh(_GELU_C * (v + 0.044715 * (v * v * v)))
    return 0.5 * (1.0 + t) + 0.5 * v * (1.0 - t * t) * (_GELU_C * (1.0 + 3.0 * 0.044715 * v * v))


def _ln_stats(r):
    mu = jnp.mean(r, axis=-1, keepdims=True)
    xc = r - mu
    var = jnp.mean(xc * xc, axis=-1, keepdims=True)
    rstd = lax.rsqrt(var + LN_EPS)
    return xc * rstd, rstd


def _ln_bwd(dy, xhat, rstd, g):
    dxh = dy * g
    m1 = jnp.mean(dxh, axis=-1, keepdims=True)
    m2 = jnp.mean(dxh * xhat, axis=-1, keepdims=True)
    return rstd * (dxh - m1 - xhat * m2)


def _fold8(v):
    rows, w = v.shape
    return jnp.sum(v.reshape(rows // 8, 8, w), axis=0)


def _dot(a, b, ta=False, tb=False):
    dn = (((0 if ta else 1,), (1 if tb else 0,)), ((), ()))
    return lax.dot_general(a.astype(BF16), b.astype(BF16), dn, preferred_element_type=F32)


_TOKEN = pl.BlockSpec((8, 128), lambda *_: (0, 0))


def _mm(a, b, *, name, ta=False, tb=False, out_dtype=F32, add=None, add_scale=1.0, token=None,
        tm=2048, tn=512, tk=4096):
    m_dim = a.shape[1] if ta else a.shape[0]
    k_dim = a.shape[0] if ta else a.shape[1]
    n_dim = b.shape[0] if tb else b.shape[1]
    assert (b.shape[1] if tb else b.shape[0]) == k_dim, (name, a.shape, b.shape)
    tk = _pick(k_dim, tk)
    tm = min(tm, max(256, MM_OPERAND_BLOCK_BYTES // (tk * a.dtype.itemsize)))
    tm, tn = _pick(m_dim, tm), _pick(n_dim, tn)
    nk = k_dim // tk
    has_add = add is not None

    def finish(r, add_ref, o_ref):
        if has_add:
            r = r + add_scale * add_ref[...].astype(F32)
        o_ref[...] = r.astype(out_dtype)

    def body(*refs):
        a_ref, b_ref = refs[:2]
        add_ref = refs[2] if has_add else None
        o_ref = refs[n_in]
        if nk == 1:
            finish(_dot(a_ref[...], b_ref[...], ta, tb), add_ref, o_ref)
            return
        acc_ref = refs[-1]
        k = pl.program_id(2)

        @pl.when(k == 0)
        def _():
            acc_ref[...] = jnp.zeros_like(acc_ref)

        acc_ref[...] += _dot(a_ref[...], b_ref[...], ta, tb)

        @pl.when(k == nk - 1)
        def _():
            finish(acc_ref[...], add_ref, o_ref)

    a_spec = (pl.BlockSpec((tk, tm), lambda i, j, k: (k, i)) if ta
              else pl.BlockSpec((tm, tk), lambda i, j, k: (i, k)))
    b_spec = (pl.BlockSpec((tn, tk), lambda i, j, k: (j, k)) if tb
              else pl.BlockSpec((tk, tn), lambda i, j, k: (k, j)))
    in_specs = [a_spec, b_spec]
    operands = [a, b]
    if has_add:
        in_specs.append(pl.BlockSpec((tm, tn), lambda i, j, k: (i, j)))
        operands.append(add)
    if token is not None:
        in_specs.append(_TOKEN)
        operands.append(token)
    n_in = len(operands)
    return pl.pallas_call(
        body, name=name,
        out_shape=jax.ShapeDtypeStruct((m_dim, n_dim), out_dtype),
        grid=(m_dim // tm, n_dim // tn, nk),
        in_specs=in_specs,
        out_specs=pl.BlockSpec((tm, tn), lambda i, j, k: (i, j)),
        scratch_shapes=[pltpu.VMEM((tm, tn), F32)] if nk > 1 else [],
        compiler_params=_params(3),
    )(*operands)


def _to_bf16(arrays, name, token=None):
    n = len(arrays)
    extra = [] if token is None else [token]

    def body(*refs):
        for src, dst in zip(refs[:n], refs[n + len(extra):]):
            dst[...] = src[...].astype(BF16)

    vmem = pl.BlockSpec(memory_space=pltpu.VMEM)
    return pl.pallas_call(
        body, name=name, out_shape=tuple(jax.ShapeDtypeStruct(a.shape, BF16) for a in arrays),
        in_specs=[vmem] * (n + len(extra)), out_specs=tuple([vmem] * n),
        compiler_params=pltpu.CompilerParams(vmem_limit_bytes=VMEM_LIMIT_BYTES))(*arrays, *extra)


def _rows(tr, w, cb=0):
    return pl.BlockSpec((tr, w), lambda i: (i, cb))


def _whole(shape):
    nd = len(shape)
    return pl.BlockSpec(tuple(shape), lambda i: (0,) * nd)


def _ffn_in(x16, w_in, name, token=None):
    t, d = x16.shape
    _, nb, _, fb = w_in.shape
    tm = _pick(t, 1024, 8)
    extra = [] if token is None else [token]

    def body(*refs):
        x_ref, wg_ref, wu_ref = refs[:3]
        h_ref, a_ref = refs[-2:]
        xv = x_ref[...]
        g = _dot(xv, wg_ref[0, 0])
        u = _dot(xv, wu_ref[0, 0])
        h_ref[0, 0] = g.astype(BF16)
        h_ref[0, 1] = u.astype(BF16)
        a_ref[0] = (g * _sigmoid(g) * u).astype(BF16)

    return pl.pallas_call(
        body, name=name,
        out_shape=(jax.ShapeDtypeStruct((nb, 2, t, fb), BF16), jax.ShapeDtypeStruct((nb, t, fb), BF16)),
        grid=(t // tm, nb),
        in_specs=[pl.BlockSpec((tm, d), lambda i, j: (i, 0)),
                  pl.BlockSpec((1, 1, d, fb), lambda i, j: (0, j, 0, 0)),
                  pl.BlockSpec((1, 1, d, fb), lambda i, j: (1, j, 0, 0))] + [_TOKEN] * len(extra),
        out_specs=(pl.BlockSpec((1, 2, tm, fb), lambda i, j: (j, 0, i, 0)),
                   pl.BlockSpec((1, tm, fb), lambda i, j: (j, i, 0))),
        compiler_params=_params(2))(x16, w_in, w_in, *extra)


def _ffn_out_ln(a, w_out, xin, g, b, alpha, name, scale=0.5, token=None):
    nb, t, fb = a.shape
    d = w_out.shape[2]
    tm = _pick(t, 512, 8)
    extra = [] if token is None else [token]

    def body(*refs):
        a_ref, w_ref, x_ref, g_ref, b_ref = refs[:5]
        r_ref, y_ref, y16_ref = refs[-3:]
        f = _dot(a_ref[0], w_ref[0])
        for jb in range(1, nb):
            f = f + _dot(a_ref[jb], w_ref[jb])
        r = alpha * x_ref[...] + scale * f
        xhat, _ = _ln_stats(r)
        y = xhat * g_ref[...] + b_ref[...]
        r_ref[...] = r
        y_ref[...] = y
        y16_ref[...] = y.astype(BF16)

    return pl.pallas_call(
        body, name=name,
        out_shape=(jax.ShapeDtypeStruct((t, d), F32), jax.ShapeDtypeStruct((t, d), F32),
                   jax.ShapeDtypeStruct((t, d), BF16)),
        grid=(t // tm,),
        in_specs=[pl.BlockSpec((nb, tm, fb), lambda i: (0, i, 0)), _whole(w_out.shape), _rows(tm, d),
                  _whole((1, d)), _whole((1, d))] + [_TOKEN] * len(extra),
        out_specs=(_rows(tm, d), _rows(tm, d), _rows(tm, d)),
        compiler_params=_params())(a, w_out, xin, g, b, *extra)


def _ffn_out_dx(drs, w_out, h, name, token=None):
    nb, _, t, fb = h.shape
    d = w_out.shape[2]
    tm = _pick(t, 1024, 8)
    extra = [] if token is None else [token]

    def body(*refs):
        dr_ref, w_ref, h_ref, dh_ref = refs[0], refs[1], refs[2], refs[-1]
        da = _dot(dr_ref[...], w_ref[0], tb=True)
        g, u = h_ref[0, 0].astype(F32), h_ref[0, 1].astype(F32)
        sg = _sigmoid(g)
        dh_ref[0, 0] = (da * u * (sg * (1.0 + g * (1.0 - sg)))).astype(BF16)
        dh_ref[0, 1] = (da * (g * sg)).astype(BF16)

    return pl.pallas_call(
        body, name=name, out_shape=jax.ShapeDtypeStruct((nb, 2, t, fb), BF16), grid=(t // tm, nb),
        in_specs=[pl.BlockSpec((tm, d), lambda i, j: (i, 0)),
                  pl.BlockSpec((1, fb, d), lambda i, j: (j, 0, 0)),
                  pl.BlockSpec((1, 2, tm, fb), lambda i, j: (j, 0, i, 0))] + [_TOKEN] * len(extra),
        out_specs=pl.BlockSpec((1, 2, tm, fb), lambda i, j: (j, 0, i, 0)),
        compiler_params=_params(2))(drs, w_out, h, *extra)


def _ffn_out_dw(a, drs, name):
    nb, t, fb = a.shape
    d = drs.shape[1]

    def body(a_ref, dr_ref, o_ref):
        o_ref[0] = _dot(a_ref[0], dr_ref[...], ta=True).astype(BF16)

    return pl.pallas_call(
        body, name=name, out_shape=jax.ShapeDtypeStruct((nb, fb, d), BF16), grid=(nb,),
        in_specs=[pl.BlockSpec((1, t, fb), lambda j: (j, 0, 0)), pl.BlockSpec((t, d), lambda j: (0, 0))],
        out_specs=pl.BlockSpec((1, fb, d), lambda j: (j, 0, 0)),
        compiler_params=_params())(a, drs)


def _ffn_in_dw(x16, dh, name, token=None):
    t, d = x16.shape
    nb, _, _, fb = dh.shape
    extra = [] if token is None else [token]

    def body(*refs):
        x_ref, dh_ref, o_ref = refs[0], refs[1], refs[-1]
        o_ref[0, 0] = _dot(x_ref[...], dh_ref[0, 0], ta=True).astype(BF16)

    return pl.pallas_call(
        body, name=name, out_shape=jax.ShapeDtypeStruct((2, nb, d, fb), BF16), grid=(nb, 2),
        in_specs=[pl.BlockSpec((t, d), lambda j, s: (0, 0)),
                  pl.BlockSpec((1, 1, t, fb), lambda j, s: (j, s, 0, 0))] + [_TOKEN] * len(extra),
        out_specs=pl.BlockSpec((1, 1, d, fb), lambda j, s: (s, j, 0, 0)),
        compiler_params=_params(2))(x16, dh, *extra)


def _ffn_in_dx(dh, w_in, add, add_scale, name, token=None):
    nb, _, t, fb = dh.shape
    d = w_in.shape[2]
    tm = _pick(t, 512, 8)
    has_add = add is not None
    extra = [] if token is None else [token]

    def body(*refs):
        dh_ref, w_ref = refs[:2]
        o_ref = refs[-1]
        acc = None
        for jb in range(nb):
            for s in range(2):
                part = _dot(dh_ref[jb, s], w_ref[s, jb], tb=True)
                acc = part if acc is None else acc + part
        if has_add:
            acc = acc + add_scale * refs[2][...]
        o_ref[...] = acc

    return pl.pallas_call(
        body, name=name, out_shape=jax.ShapeDtypeStruct((t, d), F32), grid=(t // tm,),
        in_specs=[pl.BlockSpec((nb, 2, tm, fb), lambda i: (0, 0, i, 0)), _whole(w_in.shape)]
        + ([_rows(tm, d)] if has_add else []) + [_TOKEN] * len(extra),
        out_specs=_rows(tm, d),
        compiler_params=_params())(*([dh, w_in] + ([add] if has_add else []) + extra))


def _ffn_in_dx_ln(dh, w_in, r, dy_b, b_scale, ln_g, out_scale, name, token=None):
    nb, _, t, fb = dh.shape
    d = w_in.shape[2]
    tm = _pick(t, 512, 8)
    extra = [] if token is None else [token]

    def body(*refs):
        dh_ref, w_ref, r_ref, dyb_ref, g_ref = refs[:5]
        dr_ref, drs_ref, dg_ref, dbeta_ref = refs[-4:]
        dy = b_scale * dyb_ref[...]
        for jb in range(nb):
            for s in range(2):
                dy = dy + _dot(dh_ref[jb, s], w_ref[s, jb], tb=True)
        xhat, rstd = _ln_stats(r_ref[...])
        dr = _ln_bwd(dy, xhat, rstd, g_ref[...])
        dr_ref[...] = dr
        drs_ref[...] = (out_scale * dr).astype(BF16)

        @pl.when(pl.program_id(0) == 0)
        def _():
            dg_ref[...] = jnp.zeros_like(dg_ref)
            dbeta_ref[...] = jnp.zeros_like(dbeta_ref)

        dg_ref[...] += _fold8(dy * xhat)
        dbeta_ref[...] += _fold8(dy)

    return pl.pallas_call(
        body, name=name,
        out_shape=(jax.ShapeDtypeStruct((t, d), F32), jax.ShapeDtypeStruct((t, d), BF16),
                   jax.ShapeDtypeStruct((8, d), F32), jax.ShapeDtypeStruct((8, d), F32)),
        grid=(t // tm,),
        in_specs=[pl.BlockSpec((nb, 2, tm, fb), lambda i: (0, 0, i, 0)), _whole(w_in.shape), _rows(tm, d),
                  _rows(tm, d), _whole((1, d))] + [_TOKEN] * len(extra),
        out_specs=(_rows(tm, d), _rows(tm, d), _whole((8, d)), _whole((8, d))),
        compiler_params=_params())(dh, w_in, r, dy_b, ln_g, *extra)


def _mm_ln_bwd(a, w, r, dy_b, b_scale, ln_g, out_scale, name, token=None):
    t, k_dim = a.shape
    d = w.shape[0]
    tm = _pick(t, 512, 8)
    extra = [] if token is None else [token]

    def body(*refs):
        a_ref, w_ref, r_ref, dyb_ref, g_ref = refs[:5]
        dr_ref, drs_ref, dg_ref, dbeta_ref = refs[-4:]
        dy = _dot(a_ref[...], w_ref[...], tb=True) + b_scale * dyb_ref[...]
        xhat, rstd = _ln_stats(r_ref[...])
        dr = _ln_bwd(dy, xhat, rstd, g_ref[...])
        dr_ref[...] = dr
        drs_ref[...] = (out_scale * dr).astype(BF16)

        @pl.when(pl.program_id(0) == 0)
        def _():
            dg_ref[...] = jnp.zeros_like(dg_ref)
            dbeta_ref[...] = jnp.zeros_like(dbeta_ref)

        dg_ref[...] += _fold8(dy * xhat)
        dbeta_ref[...] += _fold8(dy)

    return pl.pallas_call(
        body, name=name,
        out_shape=(jax.ShapeDtypeStruct((t, d), F32), jax.ShapeDtypeStruct((t, d), BF16),
                   jax.ShapeDtypeStruct((8, d), F32), jax.ShapeDtypeStruct((8, d), F32)),
        grid=(t // tm,),
        in_specs=[_rows(tm, k_dim), _whole(w.shape), _rows(tm, d), _rows(tm, d), _whole((1, d))]
        + [_TOKEN] * len(extra),
        out_specs=(_rows(tm, d), _rows(tm, d), _whole((8, d)), _whole((8, d))),
        compiler_params=_params())(a, w, r, dy_b, ln_g, *extra)


def _take_row(v, row_id, s):
    return jnp.sum(jnp.where(row_id == s, v, 0.0), axis=0, keepdims=True)


def _complex_power(ar, ai, n):
    out = None
    while n:
        if n & 1:
            out = (ar, ai) if out is None else (out[0] * ar - out[1] * ai, out[0] * ai + out[1] * ar)
        ar, ai = ar * ar - ai * ai, 2.0 * ar * ai
        n >>= 1
    return out


def _seg_carries(f_re, f_im, p_re, p_im, reverse):
    row_id = lax.broadcasted_iota(jnp.int32, f_re.shape, 0)
    p_re, p_im = _take_row(p_re, row_id, 0), _take_row(p_im, row_id, 0)
    out_re, out_im = jnp.zeros_like(f_re), jnp.zeros_like(f_im)
    c_re, c_im = jnp.zeros_like(p_re), jnp.zeros_like(p_im)
    order = range(N_SEG - 1, -1, -1) if reverse else range(N_SEG)
    for s in order:
        out_re = jnp.where(row_id == s, c_re, out_re)
        out_im = jnp.where(row_id == s, c_im, out_im)
        fr, fi = _take_row(f_re, row_id, s), _take_row(f_im, row_id, s)
        c_re, c_im = fr + p_re * c_re - p_im * c_im, fi + p_re * c_im + p_im * c_re
    return out_re, out_im


def _s5_fwd(za16, b_re, b_im, a_re, a_im, c_re, c_im, n_seq, name):
    t = za16.shape[0]
    n_sg, ch, st = b_re.shape
    seq = t // n_seq
    steps = seq // N_SEG

    def body(za_ref, bre_ref, bim_ref, are, aim, cre_ref, cim_ref, hre, him, y_ref):
        za = za_ref[...]
        hre[...] = _dot(za, bre_ref[0])
        him[...] = _dot(za, bim_ref[0])
        ar = jnp.broadcast_to(are[...], (N_SEG, st))
        ai = jnp.broadcast_to(aim[...], (N_SEG, st))
        zero = jnp.zeros((N_SEG, st), F32)

        def local(k, carry):
            lr, li = carry
            rows = pl.ds(pl.multiple_of(k * N_SEG, N_SEG), N_SEG)
            nr = ar * lr - ai * li + hre[rows, :]
            ni = ar * li + ai * lr + him[rows, :]
            hre[rows, :] = nr
            him[rows, :] = ni
            return nr, ni

        f_re, f_im = lax.fori_loop(0, steps, local, (zero, zero))
        c_re, c_im = _seg_carries(f_re, f_im, *_complex_power(ar, ai, steps), reverse=False)

        def fix(k, carry):
            qr, qi = carry
            rows = pl.ds(pl.multiple_of(k * N_SEG, N_SEG), N_SEG)
            hre[rows, :] = hre[rows, :] + qr
            him[rows, :] = him[rows, :] + qi
            return ar * qr - ai * qi, ar * qi + ai * qr

        lax.fori_loop(0, steps, fix, (ar * c_re - ai * c_im, ar * c_im + ai * c_re))
        y_ref[...] = _dot(hre[...], cre_ref[0]) + _dot(him[...], cim_ref[0])

    rows_ch = pl.BlockSpec((seq, ch), lambda s, j: (s, j))
    rows_st = pl.BlockSpec((seq, st), lambda s, j: (s, j))
    vec = pl.BlockSpec((1, st), lambda s, j: (0, j))
    b_blk = pl.BlockSpec((1, ch, st), lambda s, j: (j, 0, 0))
    c_blk = pl.BlockSpec((1, st, ch), lambda s, j: (j, 0, 0))
    return pl.pallas_call(
        body, name=name,
        out_shape=(jax.ShapeDtypeStruct((t, n_sg * st), F32), jax.ShapeDtypeStruct((t, n_sg * st), F32),
                   jax.ShapeDtypeStruct((t, n_sg * ch), F32)),
        grid=(n_seq, n_sg), in_specs=[rows_ch, b_blk, b_blk, vec, vec, c_blk, c_blk],
        out_specs=(rows_st, rows_st, rows_ch),
        compiler_params=_params(2))(za16, b_re, b_im, a_re, a_im, c_re, c_im)


def _s5_bwd(dy16, dza_skip, h_re, h_im, za16, b_re, b_im, a_re, a_im, c_re, c_im, n_seq, name):
    t = dy16.shape[0]
    n_sg, ch, st = b_re.shape
    seq = t // n_seq
    steps = seq // N_SEG

    def body(dy_ref, skip_ref, hre, him, za_ref, bre_ref, bim_ref, are, aim, cre_ref, cim_ref,
             dza_ref, dbre_ref, dbim_ref, dcre_ref, dcim_ref, dare, daim, lre, lim):
        dy = dy_ref[...]
        lre[...] = _dot(dy, cre_ref[0], tb=True)
        lim[...] = _dot(dy, cim_ref[0], tb=True)
        ar = jnp.broadcast_to(are[...], (N_SEG, st))
        ai = -jnp.broadcast_to(aim[...], (N_SEG, st))
        zero = jnp.zeros((N_SEG, st), F32)

        def local(i, carry):
            lr, li = carry
            k = steps - 1 - i
            rows = pl.ds(pl.multiple_of(k * N_SEG, N_SEG), N_SEG)
            nr = ar * lr - ai * li + lre[rows, :]
            ni = ar * li + ai * lr + lim[rows, :]
            lre[rows, :] = nr
            lim[rows, :] = ni
            return nr, ni

        f_re, f_im = lax.fori_loop(0, steps, local, (zero, zero))
        c_re, c_im = _seg_carries(f_re, f_im, *_complex_power(ar, ai, steps), reverse=True)

        def accumulate(lam_r, lam_i, hp_r, hp_i, acc_r, acc_i):
            return acc_r + (lam_r * hp_r + lam_i * hp_i), acc_i + (lam_i * hp_r - lam_r * hp_i)

        def fix(i, carry):
            qr, qi, acc_r, acc_i = carry
            k = steps - 1 - i
            rows = pl.ds(pl.multiple_of(k * N_SEG, N_SEG), N_SEG)
            prev = pl.ds(pl.multiple_of((k - 1) * N_SEG, N_SEG), N_SEG)
            lam_r = lre[rows, :] + qr
            lam_i = lim[rows, :] + qi
            lre[rows, :] = lam_r
            lim[rows, :] = lam_i
            acc_r, acc_i = accumulate(lam_r, lam_i, hre[prev, :], him[prev, :], acc_r, acc_i)
            return ar * qr - ai * qi, ar * qi + ai * qr, acc_r, acc_i

        qr, qi, acc_r, acc_i = lax.fori_loop(
            0, steps - 1, fix, (ar * c_re - ai * c_im, ar * c_im + ai * c_re, zero, zero))
        first = pl.ds(0, N_SEG)
        last = pl.ds((steps - 1) * N_SEG, N_SEG)
        lam_r = lre[first, :] + qr
        lam_i = lim[first, :] + qi
        lre[first, :] = lam_r
        lim[first, :] = lam_i
        row_id = lax.broadcasted_iota(jnp.int32, (N_SEG, st), 0)
        hp_r = jnp.where(row_id == 0, 0.0, pltpu.roll(hre[last, :], 1, 0))
        hp_i = jnp.where(row_id == 0, 0.0, pltpu.roll(him[last, :], 1, 0))
        acc_r, acc_i = accumulate(lam_r, lam_i, hp_r, hp_i, acc_r, acc_i)

        lam_re16 = lre[...].astype(BF16)
        lam_im16 = lim[...].astype(BF16)
        dza_ref[...] = (_dot(lam_re16, bre_ref[0], tb=True) + _dot(lam_im16, bim_ref[0], tb=True)
                        + skip_ref[...]).astype(BF16)

        @pl.when(pl.program_id(1) == 0)
        def _():
            for ref in (dbre_ref, dbim_ref, dcre_ref, dcim_ref, dare, daim):
                ref[...] = jnp.zeros_like(ref)

        za = za_ref[...]
        dbre_ref[0] += _dot(za, lam_re16, ta=True)
        dbim_ref[0] += _dot(za, lam_im16, ta=True)
        dcre_ref[0] += _dot(hre[...], dy, ta=True)
        dcim_ref[0] += _dot(him[...], dy, ta=True)
        dare[...] += acc_r
        daim[...] += acc_i

    rows_ch = pl.BlockSpec((seq, ch), lambda j, s: (s, j))
    rows_st = pl.BlockSpec((seq, st), lambda j, s: (s, j))
    vec = pl.BlockSpec((1, st), lambda j, s: (0, j))
    b_blk = pl.BlockSpec((1, ch, st), lambda j, s: (j, 0, 0))
    c_blk = pl.BlockSpec((1, st, ch), lambda j, s: (j, 0, 0))
    acc = pl.BlockSpec((N_SEG, st), lambda j, s: (0, j))
    return pl.pallas_call(
        body, name=name,
        out_shape=(jax.ShapeDtypeStruct((t, n_sg * ch), BF16),
                   jax.ShapeDtypeStruct((n_sg, ch, st), F32), jax.ShapeDtypeStruct((n_sg, ch, st), F32),
                   jax.ShapeDtypeStruct((n_sg, st, ch), F32), jax.ShapeDtypeStruct((n_sg, st, ch), F32),
                   jax.ShapeDtypeStruct((N_SEG, n_sg * st), F32), jax.ShapeDtypeStruct((N_SEG, n_sg * st), F32)),
        grid=(n_sg, n_seq),
        in_specs=[rows_ch, rows_ch, rows_st, rows_st, rows_ch, b_blk, b_blk, vec, vec, c_blk, c_blk],
        out_specs=(rows_ch, b_blk, b_blk, c_blk, c_blk, acc, acc),
        scratch_shapes=[pltpu.VMEM((seq, st), F32), pltpu.VMEM((seq, st), F32)],
        compiler_params=_params(2))(dy16, dza_skip, h_re, h_im, za16, b_re, b_im, a_re, a_im, c_re, c_im)


def _s5_post_fwd(yssm, u, d_skip, glu_w, glu_b, name):
    t, w = yssm.shape
    tr = _pick(t, 512, 8)

    def body(y_ref, u_ref, d_ref, w_ref, b_ref, o_ref):
        yg = _gelu(y_ref[...] + d_ref[...] * u_ref[...])
        pre = _dot(yg, w_ref[...]) + b_ref[...]
        o_ref[...] = yg * _sigmoid(pre)

    return pl.pallas_call(
        body, name=name, out_shape=jax.ShapeDtypeStruct((t, w), F32), grid=(t // tr,),
        in_specs=[_rows(tr, w), _rows(tr, w), _whole((1, w)), _whole((w, w)), _whole((1, w))],
        out_specs=_rows(tr, w), compiler_params=_params())(yssm, u, d_skip, glu_w, glu_b)


def _s5_post_bwd(yssm, u, dout, d_skip, glu_w, glu_b, name):
    t, w = yssm.shape
    tr = _pick(t, 512, 8)

    def body(y_ref, u_ref, do_ref, d_ref, w_ref, b_ref, dy_ref, du_ref, dw_ref, dd_ref, db_ref):
        uu = u_ref[...]
        y = y_ref[...] + d_ref[...] * uu
        yg = _gelu(y)
        sg = _sigmoid(_dot(yg, w_ref[...]) + b_ref[...])
        do = do_ref[...]
        dpre = do * yg * sg * (1.0 - sg)
        dyg = do * sg + _dot(dpre, w_ref[...], tb=True)
        dy = dyg * _gelu_grad(y)
        dy_ref[...] = dy.astype(BF16)
        du_ref[...] = dy * d_ref[...]

        @pl.when(pl.program_id(0) == 0)
        def _():
            dw_ref[...] = jnp.zeros_like(dw_ref)
            dd_ref[...] = jnp.zeros_like(dd_ref)
            db_ref[...] = jnp.zeros_like(db_ref)

        dw_ref[...] += _dot(yg, dpre, ta=True)
        dd_ref[...] += _fold8(dy * uu)
        db_ref[...] += _fold8(dpre)

    return pl.pallas_call(
        body, name=name,
        out_shape=(jax.ShapeDtypeStruct((t, w), BF16), jax.ShapeDtypeStruct((t, w), F32),
                   jax.ShapeDtypeStruct((w, w), F32), jax.ShapeDtypeStruct((8, w), F32),
                   jax.ShapeDtypeStruct((8, w), F32)),
        grid=(t // tr,),
        in_specs=[_rows(tr, w), _rows(tr, w), _rows(tr, w), _whole((1, w)), _whole((w, w)),
                  _whole((1, w))],
        out_specs=(_rows(tr, w), _rows(tr, w), _whole((w, w)), _whole((8, w)), _whole((8, w))),
        compiler_params=_params())(yssm, u, dout, d_skip, glu_w, glu_b)


def _head_select(parts, head_dim):
    col_head = lax.broadcasted_iota(jnp.int32, parts[0].shape, 1) // head_dim
    out = jnp.zeros_like(parts[0])
    for h, part in enumerate(parts):
        out = jnp.where(col_head == h, part, out)
    return out


def _gmlp_fwd(proj, ln_g, ln_b, ws, bs_cols, name):
    t = proj.shape[0]
    n_heads = ws.shape[0]
    w = bs_cols.shape[1]
    head_dim = w // n_heads
    cpb = _pick(t // CHUNK, 4, 1)
    tr = cpb * CHUNK

    def body(zu_ref, zv_ref, g_ref, b_ref, ws_ref, bs_ref, o_ref):
        for c in range(cpb):
            rows = pl.ds(c * CHUNK, CHUNK)
            xhat, _ = _ln_stats(_gelu(zv_ref[rows, :]))
            vn = (xhat * g_ref[...] + b_ref[...]).astype(BF16)
            s = _head_select([_dot(ws_ref[h], vn) for h in range(n_heads)], head_dim) + bs_ref[...]
            o_ref[rows, :] = _gelu(zu_ref[rows, :]) * s

    return pl.pallas_call(
        body, name=name, out_shape=jax.ShapeDtypeStruct((t, w), F32), grid=(t // tr,),
        in_specs=[_rows(tr, w, 1), _rows(tr, w, 2), _whole((1, w)), _whole((1, w)),
                  _whole(ws.shape), _whole(bs_cols.shape)],
        out_specs=_rows(tr, w), compiler_params=_params())(proj, proj, ln_g, ln_b, ws, bs_cols)


def _gmlp_bwd(proj, dout, ln_g, ln_b, ws, ws_t, bs_cols, name):
    t = proj.shape[0]
    n_heads = ws.shape[0]
    w = bs_cols.shape[1]
    head_dim = w // n_heads
    cpb = _pick(t // CHUNK, 4, 1)
    tr = cpb * CHUNK

    def body(zu_ref, zv_ref, do_ref, g_ref, b_ref, ws_ref, wst_ref, bs_ref,
             dzu_ref, dzv_ref, dws_ref, dbs_ref, dg_ref, dbeta_ref):
        @pl.when(pl.program_id(0) == 0)
        def _():
            dws_ref[...] = jnp.zeros_like(dws_ref)
            dbs_ref[...] = jnp.zeros_like(dbs_ref)
            dg_ref[...] = jnp.zeros_like(dg_ref)
            dbeta_ref[...] = jnp.zeros_like(dbeta_ref)

        col_head = lax.broadcasted_iota(jnp.int32, (CHUNK, w), 1) // head_dim
        for c in range(cpb):
            rows = pl.ds(c * CHUNK, CHUNK)
            zu, zv, do = zu_ref[rows, :], zv_ref[rows, :], do_ref[rows, :]
            xhat, rstd = _ln_stats(_gelu(zv))
            vn = (xhat * g_ref[...] + b_ref[...]).astype(BF16)
            s = _head_select([_dot(ws_ref[h], vn) for h in range(n_heads)], head_dim) + bs_ref[...]
            dzu_ref[rows, :] = (do * s * _gelu_grad(zu)).astype(BF16)
            ds = do * _gelu(zu)
            ds16 = ds.astype(BF16)
            dbs_ref[...] += ds
            for h in range(n_heads):
                dws_ref[h] += _dot(jnp.where(col_head == h, ds16, jnp.zeros_like(ds16)), vn, tb=True)
            dvn = _head_select([_dot(wst_ref[h], ds16) for h in range(n_heads)], head_dim)
            dg_ref[...] += _fold8(dvn * xhat)
            dbeta_ref[...] += _fold8(dvn)
            dgv = _ln_bwd(dvn, xhat, rstd, g_ref[...])
            dzv_ref[rows, :] = (dgv * _gelu_grad(zv)).astype(BF16)

    return pl.pallas_call(
        body, name=name,
        out_shape=(jax.ShapeDtypeStruct((t, w), BF16), jax.ShapeDtypeStruct((t, w), BF16),
                   jax.ShapeDtypeStruct(ws.shape, F32), jax.ShapeDtypeStruct((CHUNK, w), F32),
                   jax.ShapeDtypeStruct((8, w), F32), jax.ShapeDtypeStruct((8, w), F32)),
        grid=(t // tr,),
        in_specs=[_rows(tr, w, 1), _rows(tr, w, 2), _rows(tr, w), _whole((1, w)), _whole((1, w)),
                  _whole(ws.shape), _whole(ws.shape), _whole(bs_cols.shape)],
        out_specs=(_rows(tr, w), _rows(tr, w), _whole(ws.shape), _whole((CHUNK, w)),
                   _whole((8, w)), _whole((8, w))),
        compiler_params=_params())(proj, proj, dout, ln_g, ln_b, ws, ws_t, bs_cols)


def _merge_fwd(s5out, gm, proj, up_a, up_b, name):
    t, w = s5out.shape
    d = up_a.shape[1]
    assert d == 2 * w
    tr = _pick(t, 512, 8)

    def body(s_ref, gm_ref, ga0, ga1, gb0, gb1, ua_ref, ub_ref, m_ref, ya_ref, yb_ref):
        ya = _dot(s_ref[...], ua_ref[...])
        yb = _dot(gm_ref[...], ub_ref[...])
        ya_ref[...] = ya.astype(BF16)
        yb_ref[...] = yb.astype(BF16)
        for half, (ga, gb) in enumerate(((ga0, gb0), (ga1, gb1))):
            cols = slice(half * w, (half + 1) * w)
            m_ref[:, cols] = (_sigmoid(ga[...]) * ya[:, cols] + _sigmoid(gb[...]) * yb[:, cols]).astype(BF16)

    return pl.pallas_call(
        body, name=name,
        out_shape=(jax.ShapeDtypeStruct((t, d), BF16), jax.ShapeDtypeStruct((t, d), BF16),
                   jax.ShapeDtypeStruct((t, d), BF16)),
        grid=(t // tr,),
        in_specs=[_rows(tr, w), _rows(tr, w), _rows(tr, w, 3), _rows(tr, w, 4), _rows(tr, w, 5),
                  _rows(tr, w, 6), _whole(up_a.shape), _whole(up_b.shape)],
        out_specs=(_rows(tr, d), _rows(tr, d), _rows(tr, d)),
        compiler_params=_params())(s5out, gm, proj, proj, proj, proj, up_a, up_b)


def _merge_bwd(dm, ya, yb, s5out, gm, proj, up_a, up_b, name):
    t, d = dm.shape
    w = d // 2
    tr = _pick(t, 512, 8)

    def body(dm_ref, ya_ref, yb_ref, s_ref, gm_ref, ga0, ga1, gb0, gb1, ua_ref, ub_ref,
             dga_ref, dgb_ref, ds_ref, dgm_ref, dua_ref, dub_ref):
        dm_v, ya, yb = dm_ref[...], ya_ref[...].astype(F32), yb_ref[...].astype(F32)
        dya, dyb = [], []
        for half, (ga, gb) in enumerate(((ga0, gb0), (ga1, gb1))):
            cols = slice(half * w, (half + 1) * w)
            sa, sb = _sigmoid(ga[...]), _sigmoid(gb[...])
            dmh = dm_v[:, cols]
            dga_ref[:, cols] = (dmh * ya[:, cols] * sa * (1.0 - sa)).astype(BF16)
            dgb_ref[:, cols] = (dmh * yb[:, cols] * sb * (1.0 - sb)).astype(BF16)
            dya.append((dmh * sa).astype(BF16))
            dyb.append((dmh * sb).astype(BF16))
        dya = jnp.concatenate(dya, axis=1)
        dyb = jnp.concatenate(dyb, axis=1)
        ds_ref[...] = _dot(dya, ua_ref[...], tb=True)
        dgm_ref[...] = _dot(dyb, ub_ref[...], tb=True)

        @pl.when(pl.program_id(0) == 0)
        def _():
            dua_ref[...] = jnp.zeros_like(dua_ref)
            dub_ref[...] = jnp.zeros_like(dub_ref)

        dua_ref[...] += _dot(s_ref[...], dya, ta=True)
        dub_ref[...] += _dot(gm_ref[...], dyb, ta=True)

    return pl.pallas_call(
        body, name=name,
        out_shape=(jax.ShapeDtypeStruct((t, d), BF16), jax.ShapeDtypeStruct((t, d), BF16),
                   jax.ShapeDtypeStruct((t, w), F32), jax.ShapeDtypeStruct((t, w), F32),
                   jax.ShapeDtypeStruct(up_a.shape, F32), jax.ShapeDtypeStruct(up_b.shape, F32)),
        grid=(t // tr,),
        in_specs=[_rows(tr, d), _rows(tr, d), _rows(tr, d), _rows(tr, w), _rows(tr, w),
                  _rows(tr, w, 3), _rows(tr, w, 4), _rows(tr, w, 5), _rows(tr, w, 6),
                  _whole(up_a.shape), _whole(up_b.shape)],
        out_specs=(_rows(tr, d), _rows(tr, d), _rows(tr, w), _rows(tr, w), _whole(up_a.shape),
                   _whole(up_b.shape)),
        compiler_params=_params())(dm, ya, yb, s5out, gm, proj, proj, proj, proj, up_a, up_b)


def _head(x3, r3, ln_g, p, target, w_gate, w_proj, out_scale, name):
    t, d = x3.shape
    pd = p.shape[1]
    tr = _pick(t, 512, 8)
    nb = t // tr

    def body(x_ref, r_ref, g_ref, p_ref, t_ref, wg_ref, wp_ref,
             loss_ref, dr_ref, drs_ref, dwg_ref, dwp_ref, dg_ref, dbeta_ref):
        xv = x_ref[...]
        sg = _sigmoid(_dot(xv, wg_ref[...]))
        pp = _dot(p_ref[...], wp_ref[...])
        err = xv + sg * pp - t_ref[...]
        loss_ref[...] = jnp.full((8, 128), 0.5 * jnp.sum(jnp.mean(err * err, axis=-1)), F32)
        dout = err * (1.0 / d)
        dgpre = dout * pp * sg * (1.0 - sg)
        dpp = dout * sg
        dx = dout + _dot(dgpre, wg_ref[...], tb=True)
        xhat, rstd = _ln_stats(r_ref[...])
        dr = _ln_bwd(dx, xhat, rstd, g_ref[...])
        dr_ref[...] = dr
        drs_ref[...] = (out_scale * dr).astype(BF16)

        @pl.when(pl.program_id(0) == 0)
        def _():
            for ref in (dwg_ref, dwp_ref, dg_ref, dbeta_ref):
                ref[...] = jnp.zeros_like(ref)

        dwg_ref[...] += _dot(xv, dgpre, ta=True)
        dwp_ref[...] += _dot(p_ref[...], dpp, ta=True)
        dg_ref[...] += _fold8(dx * xhat)
        dbeta_ref[...] += _fold8(dx)

    return pl.pallas_call(
        body, name=name,
        out_shape=(jax.ShapeDtypeStruct((nb * 8, 128), F32), jax.ShapeDtypeStruct((t, d), F32),
                   jax.ShapeDtypeStruct((t, d), BF16), jax.ShapeDtypeStruct((d, d), F32),
                   jax.ShapeDtypeStruct((pd, d), F32), jax.ShapeDtypeStruct((8, d), F32),
                   jax.ShapeDtypeStruct((8, d), F32)),
        grid=(nb,),
        in_specs=[_rows(tr, d), _rows(tr, d), _whole((1, d)), _rows(tr, pd), _rows(tr, d), _whole((d, d)),
                  _whole((pd, d))],
        out_specs=(pl.BlockSpec((8, 128), lambda i: (i, 0)), _rows(tr, d), _rows(tr, d), _whole((d, d)),
                   _whole((pd, d)), _whole((8, d)), _whole((8, d))),
        compiler_params=_params())(x3, r3, ln_g, p, target, w_gate, w_proj)


_HBM = pl.BlockSpec(memory_space=pltpu.HBM)


_SEM = pl.BlockSpec(memory_space=pltpu.SEMAPHORE)
_ANY = pl.BlockSpec(memory_space=pl.ANY)
_DATAFLOW = pltpu.SideEffectType.DATAFLOW_SIDE_EFFECTING


def _peer(k, x, y, c):
    return (1 - x if k & 4 else x, 1 - y if k & 2 else y, 1 - c if k & 1 else c)


def _exchange_start(sends, after, name):
    n = len(sends)
    lands = [lax.empty((N_DEV,) + s.shape[1:], s.dtype) for s in sends]

    def body(*refs):
        s_refs, l_refs = refs[:n], refs[n:2 * n]
        send_sems, recv_sems, local_sems, token = refs[2 * n + 1], refs[2 * n + 2], refs[2 * n + 3], refs[-1]
        x, y, c = lax.axis_index("x"), lax.axis_index("y"), lax.axis_index("c")
        me = 4 * x + 2 * y + c
        for a in range(n):
            same = sends[a].shape[0] == 1
            pltpu.make_async_copy(s_refs[a].at[0 if same else me], l_refs[a].at[me], local_sems.at[a]).start()
            for k in range(1, N_DEV):
                px, py, pc = _peer(k, x, y, c)
                pltpu.make_async_remote_copy(
                    src_ref=s_refs[a].at[0 if same else 4 * px + 2 * py + pc], dst_ref=l_refs[a].at[me],
                    send_sem=send_sems.at[7 * a + k - 1], recv_sem=recv_sems.at[7 * a + k - 1],
                    device_id=(px, py, pc), device_id_type=pl.DeviceIdType.MESH).start()
        token[...] = jnp.zeros_like(token)

    outs = pl.pallas_call(
        body, name=name,
        out_shape=(pltpu.SemaphoreType.DMA((7 * n,)), pltpu.SemaphoreType.DMA((7 * n,)),
                   pltpu.SemaphoreType.DMA((n,)),
                   *[pltpu.HBM(s.shape, s.dtype) for s in sends],
                   *[pltpu.HBM(l.shape, l.dtype) for l in lands],
                   jax.ShapeDtypeStruct((8, 128), F32)),
        in_specs=[_HBM] * (2 * n) + [_ANY],
        out_specs=(_SEM, _SEM, _SEM, *([_HBM] * (2 * n)), pl.BlockSpec(memory_space=pltpu.VMEM)),
        input_output_aliases={i: 3 + i for i in range(2 * n)},
        compiler_params=pltpu.CompilerParams(has_side_effects=_DATAFLOW),
    )(*[pltpu.with_memory_space_constraint(v, pltpu.HBM) for v in list(sends) + lands], after)
    return (outs[0], outs[1], outs[2], list(outs[3:3 + n]), list(outs[3 + n:3 + 2 * n])), outs[-1]


def _exchange_wait(handle, after, name):
    send_sems, recv_sems, local_sems, sends, lands = handle
    n = len(sends)

    def body(*refs):
        s_refs, l_refs = refs[:n], refs[n:2 * n]
        send_sems, recv_sems, local_sems = refs[2 * n], refs[2 * n + 1], refs[2 * n + 2]
        x, y, c = lax.axis_index("x"), lax.axis_index("y"), lax.axis_index("c")
        for a in range(n):
            pltpu.make_async_copy(s_refs[a].at[0], l_refs[a].at[0], local_sems.at[a]).wait()
            for k in range(1, N_DEV):
                copy = pltpu.make_async_remote_copy(
                    src_ref=s_refs[a].at[0], dst_ref=l_refs[a].at[0],
                    send_sem=send_sems.at[7 * a + k - 1], recv_sem=recv_sems.at[7 * a + k - 1],
                    device_id=_peer(k, x, y, c), device_id_type=pl.DeviceIdType.MESH)
                copy.wait_send()
                copy.wait_recv()

    outs = pl.pallas_call(
        body, name=name,
        out_shape=(*[pltpu.HBM(s.shape, s.dtype) for s in sends], *[pltpu.HBM(l.shape, l.dtype) for l in lands]),
        in_specs=[_HBM] * (2 * n) + [_SEM, _SEM, _SEM] + [_ANY] * len(after),
        out_specs=tuple([_HBM] * (2 * n)),
        input_output_aliases={i: i for i in range(2 * n)},
        compiler_params=pltpu.CompilerParams(has_side_effects=_DATAFLOW),
    )(*sends, *lands, send_sems, recv_sems, local_sems, *after)
    return list(outs[n:])


def _chips_of(x, y):
    return [(1 - x, y), (x, 1 - y), (1 - x, 1 - y)]


def _gather2_start(shards, after, name):
    n = len(shards)
    lands = [lax.empty((N_DEV,) + s.shape, s.dtype) for s in shards]

    def body(*refs):
        x_refs, l_refs = refs[:n], refs[n:2 * n]
        send_sems, recv_sems, local_sems, token = refs[2 * n + 1], refs[2 * n + 2], refs[2 * n + 3], refs[-1]
        x, y, c = lax.axis_index("x"), lax.axis_index("y"), lax.axis_index("c")
        me = 4 * x + 2 * y + c
        peers = [(x, y, 1 - c)] + [(*chip, c) for chip in _chips_of(x, y)]
        for a in range(n):
            pltpu.make_async_copy(x_refs[a], l_refs[a].at[me], local_sems.at[a]).start()
            for k, peer in enumerate(peers):
                pltpu.make_async_remote_copy(
                    src_ref=x_refs[a], dst_ref=l_refs[a].at[me],
                    send_sem=send_sems.at[4 * a + k], recv_sem=recv_sems.at[4 * a + k],
                    device_id=peer, device_id_type=pl.DeviceIdType.MESH).start()
        token[...] = jnp.zeros_like(token)

    outs = pl.pallas_call(
        body, name=name,
        out_shape=(pltpu.SemaphoreType.DMA((4 * n,)), pltpu.SemaphoreType.DMA((4 * n,)),
                   pltpu.SemaphoreType.DMA((n,)),
                   *[pltpu.HBM(s.shape, s.dtype) for s in shards],
                   *[pltpu.HBM(l.shape, l.dtype) for l in lands],
                   jax.ShapeDtypeStruct((8, 128), F32)),
        in_specs=[_HBM] * (2 * n) + [_ANY],
        out_specs=(_SEM, _SEM, _SEM, *([_HBM] * (2 * n)), pl.BlockSpec(memory_space=pltpu.VMEM)),
        input_output_aliases={i: 3 + i for i in range(2 * n)},
        compiler_params=pltpu.CompilerParams(has_side_effects=_DATAFLOW),
    )(*[pltpu.with_memory_space_constraint(v, pltpu.HBM) for v in list(shards) + lands], after)
    return (outs[0], outs[1], outs[2], list(outs[3:3 + n]), list(outs[3 + n:3 + 2 * n])), outs[-1]


def _gather2_forward(handle, after, name):
    send_sems, recv_sems, local_sems, shards, lands = handle
    n = len(shards)

    def body(*refs):
        x_refs, l_refs = refs[:n], refs[n:2 * n]
        send_sems, recv_sems = refs[2 * n], refs[2 * n + 1]
        out0 = 2 * n + 2 + len(after)
        fwd_send, fwd_recv, token = refs[out0], refs[out0 + 1], refs[-1]
        x, y, c = lax.axis_index("x"), lax.axis_index("y"), lax.axis_index("c")
        for a in range(n):
            for j, chip in enumerate(_chips_of(x, y)):
                block = l_refs[a].at[4 * chip[0] + 2 * chip[1] + c]
                pltpu.make_async_remote_copy(
                    src_ref=x_refs[a], dst_ref=block, send_sem=send_sems.at[4 * a + 1 + j],
                    recv_sem=recv_sems.at[4 * a + 1 + j], device_id=(*chip, c),
                    device_id_type=pl.DeviceIdType.MESH).wait_recv()
                pltpu.make_async_remote_copy(
                    src_ref=block, dst_ref=block, send_sem=fwd_send.at[3 * a + j], recv_sem=fwd_recv.at[3 * a + j],
                    device_id=(x, y, 1 - c), device_id_type=pl.DeviceIdType.MESH).start()
        token[...] = jnp.zeros_like(token)

    outs = pl.pallas_call(
        body, name=name,
        out_shape=(pltpu.SemaphoreType.DMA((3 * n,)), pltpu.SemaphoreType.DMA((3 * n,)),
                   *[pltpu.HBM(s.shape, s.dtype) for s in shards], *[pltpu.HBM(l.shape, l.dtype) for l in lands],
                   jax.ShapeDtypeStruct((8, 128), F32)),
        in_specs=[_HBM] * (2 * n) + [_SEM, _SEM] + [_ANY] * len(after),
        out_specs=(_SEM, _SEM, *([_HBM] * (2 * n)), pl.BlockSpec(memory_space=pltpu.VMEM)),
        input_output_aliases={i: 2 + i for i in range(2 * n)},
        compiler_params=pltpu.CompilerParams(has_side_effects=_DATAFLOW),
    )(*shards, *lands, send_sems, recv_sems, *after)
    handle = (send_sems, recv_sems, local_sems, outs[0], outs[1], list(outs[2:2 + n]), list(outs[2 + n:2 + 2 * n]))
    return handle, outs[-1]


def _gather2_wait(handle, after, name):
    send_sems, recv_sems, local_sems, fwd_send, fwd_recv, shards, lands = handle
    n = len(shards)

    def body(*refs):
        x_refs, l_refs = refs[:n], refs[n:2 * n]
        send_sems, recv_sems, local_sems, fwd_send, fwd_recv = refs[2 * n:2 * n + 5]
        x, y, c = lax.axis_index("x"), lax.axis_index("y"), lax.axis_index("c")
        sibling = (x, y, 1 - c)

        def copy(a, send_sem, recv_sem):
            return pltpu.make_async_remote_copy(
                src_ref=x_refs[a], dst_ref=l_refs[a].at[0], send_sem=send_sem, recv_sem=recv_sem,
                device_id=sibling, device_id_type=pl.DeviceIdType.MESH)

        for a in range(n):
            pltpu.make_async_copy(x_refs[a], l_refs[a].at[0], local_sems.at[a]).wait()
            for k in range(4):
                copy(a, send_sems.at[4 * a + k], recv_sems.at[4 * a + k]).wait_send()
            copy(a, send_sems.at[4 * a], recv_sems.at[4 * a]).wait_recv()
            for j in range(3):
                passed = copy(a, fwd_send.at[3 * a + j], fwd_recv.at[3 * a + j])
                passed.wait_send()
                passed.wait_recv()

    outs = pl.pallas_call(
        body, name=name,
        out_shape=(*[pltpu.HBM(s.shape, s.dtype) for s in shards], *[pltpu.HBM(l.shape, l.dtype) for l in lands]),
        in_specs=[_HBM] * (2 * n) + [_SEM] * 5 + [_ANY] * len(after),
        out_specs=tuple([_HBM] * (2 * n)),
        input_output_aliases={i: i for i in range(2 * n)},
        compiler_params=pltpu.CompilerParams(has_side_effects=_DATAFLOW),
    )(*shards, *lands, send_sems, recv_sems, local_sems, fwd_send, fwd_recv, *after)
    return list(outs[n:])


def _adamw(recv, w, m, v, name):
    n, rows, width = recv.shape
    tr = _pick(rows, max(8, ADAM_BLOCK_BYTES // (n * width * recv.dtype.itemsize)), 8)
    c_m = 1.0 - ADAM_B1 ** ADAM_STEP
    c_v = 1.0 - ADAM_B2 ** ADAM_STEP

    def body(r_ref, w_ref, m_ref, v_ref, g_ref, d_ref, nm_ref, nv_ref):
        g = r_ref[0].astype(F32)
        for i in range(1, n):
            g = g + r_ref[i].astype(F32)
        m2 = ADAM_B1 * m_ref[...] + (1.0 - ADAM_B1) * g
        v2 = ADAM_B2 * v_ref[...] + (1.0 - ADAM_B2) * (g * g)
        m_hat = m2 / c_m
        v_hat = v2 / c_v
        g_ref[...] = g
        d_ref[...] = -ADAM_LR * (m_hat / (jnp.sqrt(v_hat) + ADAM_EPS) + ADAM_WD * w_ref[...])
        nm_ref[...] = m2
        nv_ref[...] = v2

    blk = _rows(tr, width)
    shp = jax.ShapeDtypeStruct((rows, width), F32)
    return pl.pallas_call(
        body, name=name, out_shape=(shp, shp, shp, shp), grid=(rows // tr,),
        in_specs=[pl.BlockSpec((n, tr, width), lambda i: (0, i, 0)), blk, blk, blk],
        out_specs=(blk, blk, blk, blk), compiler_params=_params())(recv, w, m, v)


def _join_cols(gathered):
    n, r, c = gathered.shape
    return gathered.transpose(1, 0, 2).reshape(r, n * c)


def _split_cols(full):
    r, c = full.shape
    return full.reshape(r, N_DEV, c // N_DEV).transpose(1, 0, 2)


def _adamw_small(items, name):
    n = len(items)
    c_m = 1.0 - ADAM_B1 ** ADAM_STEP
    c_v = 1.0 - ADAM_B2 ** ADAM_STEP

    def body(*refs):
        ins, outs = refs[:4 * n], refs[4 * n:]
        for k in range(n):
            r_ref, w_ref, m_ref, v_ref = ins[4 * k:4 * k + 4]
            g = r_ref[0].astype(F32)
            for i in range(1, N_DEV):
                g = g + r_ref[i].astype(F32)
            m2 = ADAM_B1 * m_ref[...] + (1.0 - ADAM_B1) * g
            v2 = ADAM_B2 * v_ref[...] + (1.0 - ADAM_B2) * (g * g)
            outs[4 * k][...] = g
            outs[4 * k + 1][...] = -ADAM_LR * ((m2 / c_m) / (jnp.sqrt(v2 / c_v) + ADAM_EPS) + ADAM_WD * w_ref[...])
            outs[4 * k + 2][...] = m2
            outs[4 * k + 3][...] = v2

    vmem = pl.BlockSpec(memory_space=pltpu.VMEM)
    flat = pl.pallas_call(
        body, name=name,
        out_shape=tuple(jax.ShapeDtypeStruct(w.shape, F32) for _, w, _, _ in items for _ in range(4)),
        in_specs=[vmem] * (4 * n), out_specs=tuple([vmem] * (4 * n)),
        compiler_params=pltpu.CompilerParams(vmem_limit_bytes=VMEM_LIMIT_BYTES),
    )(*[a for item in items for a in item])
    return [tuple(flat[4 * k:4 * k + 4]) for k in range(n)]


def _discretise(lam_re, lam_im, log_dt, b_re, b_im):
    dt = jnp.exp(log_dt)
    mag = jnp.exp(lam_re * dt)
    ab_re = mag * jnp.cos(lam_im * dt)
    ab_im = mag * jnp.sin(lam_im * dt)
    nr = ab_re - 1.0
    ni = ab_im
    den = lam_re * lam_re + lam_im * lam_im
    coef_re = (nr * lam_re + ni * lam_im) / den
    coef_im = (ni * lam_re - nr * lam_im) / den
    return ab_re, ab_im, coef_re * b_re - coef_im * b_im, coef_re * b_im + coef_im * b_re


def _s5_discretise(operands, name):
    def body(*refs):
        for out_ref, v in zip(refs[5:], _discretise(*[r[...] for r in refs[:5]])):
            out_ref[...] = v

    vmem = pl.BlockSpec(memory_space=pltpu.VMEM)
    shape = jax.ShapeDtypeStruct(operands[0].shape, F32)
    return pl.pallas_call(body, name=name, out_shape=(shape,) * 4, in_specs=[vmem] * 5,
                          out_specs=(vmem,) * 4)(*operands)


def _s5_discretise_bwd(operands, cotangents, name):
    def body(*refs):
        _, vjp = jax.vjp(_discretise, *[r[...] for r in refs[:5]])
        for out_ref, v in zip(refs[9:], vjp(tuple(r[...] for r in refs[5:9]))):
            out_ref[...] = v

    vmem = pl.BlockSpec(memory_space=pltpu.VMEM)
    shape = jax.ShapeDtypeStruct(operands[0].shape, F32)
    return pl.pallas_call(body, name=name, out_shape=(shape,) * 5, in_specs=[vmem] * 9,
                          out_specs=(vmem,) * 5)(*operands, *cotangents)


def _same_group(gl):
    return jnp.eye(gl, dtype=bool)[None, :, None, :, None]


def _super_groups_in(bb, gl):
    g, p, i = bb.shape
    blocks = bb.reshape(g // gl, gl, p, i).transpose(0, 1, 3, 2)[:, :, :, None, :]
    return jnp.where(_same_group(gl), blocks, 0.0).reshape(g // gl, gl * i, gl * p)


def _super_groups_in_take(dense, gl, p, i):
    n_sg = dense.shape[0]
    blocks = jnp.where(_same_group(gl), dense.reshape(n_sg, gl, i, gl, p), 0.0).sum(axis=3)
    return blocks.transpose(0, 1, 3, 2).reshape(n_sg * gl, p, i)


def _super_groups_out(cc, gl):
    g, i, p = cc.shape
    blocks = cc.reshape(g // gl, gl, i, p).transpose(0, 1, 3, 2)[:, :, :, None, :]
    return jnp.where(_same_group(gl), blocks, 0.0).reshape(g // gl, gl * p, gl * i)


def _super_groups_out_take(dense, gl, i, p):
    n_sg = dense.shape[0]
    blocks = jnp.where(_same_group(gl), dense.reshape(n_sg, gl, p, gl, i), 0.0).sum(axis=3)
    return blocks.transpose(0, 1, 3, 2).reshape(n_sg * gl, i, p)


def _perm_rows(z, n_seq):
    t, w = z.shape
    steps = t // n_seq // N_SEG
    return z.reshape(n_seq, N_SEG, steps, w).transpose(0, 2, 1, 3).reshape(t, w)


def _unperm_rows(z, n_seq):
    t, w = z.shape
    steps = t // n_seq // N_SEG
    return z.reshape(n_seq, steps, N_SEG, w).transpose(0, 2, 1, 3).reshape(t, w)


def kernel(*args):
    assert len(args) == len(INPUT_NAMES)
    inp = dict(zip(INPUT_NAMES, args))
    depth = inp["ffn1_w_in"].shape[0]
    assert depth == 1
    alpha = (2.0 * depth) ** 0.25

    n_seq, seq, d_model = inp["x"].shape
    t = n_seq * seq
    x = inp["x"].reshape(t, d_model)
    target = inp["loss_target"].reshape(t, d_model)
    wts = {n: inp[n][0] if n in SHARDED else inp[n] for n in WEIGHTS}
    mom = {n: inp["m_" + n][0] if n in SHARDED else inp["m_" + n] for n in WEIGHTS}
    var = {n: inp["v_" + n][0] if n in SHARDED else inp["v_" + n] for n in WEIGHTS}

    full = {}

    def place(n, g):
        if n in ("ffn1_w_in", "ffn2_w_in"):
            full[n] = g.reshape((2, N_DEV // 2) + g.shape[1:])
        elif n in ("ffn1_w_out", "ffn2_w_out"):
            full[n] = g.reshape(N_DEV // 2, 2 * g.shape[1], g.shape[2])
        elif n in COL_SHARDED:
            full[n] = _join_cols(g)
        else:
            full[n] = g.reshape(N_DEV * g.shape[1], g.shape[2])

    w16 = dict(zip(GATHER_FIRST, _to_bf16([wts[n] for n in GATHER_FIRST], "cast_ffn1")))
    gather_first, token = _gather2_start([w16[n] for n in GATHER_FIRST], w16[GATHER_FIRST[0]], "gather_first_start")
    later = tuple(n for n in SHARDED if n not in GATHER_FIRST)
    casts = _to_bf16([wts[n] for n in later] + [x, inp["p"][0].reshape(t, -1)], "cast_rest", token=token)
    w16.update(zip(later, casts))
    x16, p16 = casts[-2:]

    def gather_start(names, after, name):
        return _exchange_start([w16[n][None] for n in names], after, name)

    def gather_wait(names, handle, after, name):
        for n, g in zip(names, _exchange_wait(handle, after, name)):
            place(n, g)

    def gather2_wait(names, handle, after, name):
        for n, g in zip(names, _gather2_wait(handle, after, name)):
            place(n, g)


    def row(v):
        return v.reshape(1, -1)

    _, n_groups, n_state = wts["ssm_lambda_re"].shape
    n_ch = wts["ssm_b_re"].shape[3]
    disc_in = (jnp.repeat(wts["ssm_lambda_re"][0], n_ch, axis=1), jnp.repeat(wts["ssm_lambda_im"][0], n_ch, axis=1),
               jnp.broadcast_to(wts["ssm_log_dt"][0][:, None], (n_groups, n_state * n_ch)),
               wts["ssm_b_re"][0].reshape(n_groups, -1), wts["ssm_b_im"][0].reshape(n_groups, -1))
    ab_re, ab_im, bb_re, bb_im = _s5_discretise(disc_in, "s5_discretise")
    a_re, a_im = row(ab_re[:, ::n_ch]), row(ab_im[:, ::n_ch])
    bb_re, bb_im = bb_re.reshape(n_groups, n_state, n_ch), bb_im.reshape(n_groups, n_state, n_ch)
    gl = S5_CHANNELS_PER_STEP // n_ch
    b_sg_re = _super_groups_in(bb_re, gl).astype(BF16)
    b_sg_im = _super_groups_in(bb_im, gl).astype(BF16)
    c_sg_re = _super_groups_out(wts["ssm_c_re"][0], gl).astype(BF16)
    c_sg_im = _super_groups_out(-wts["ssm_c_im"][0], gl).astype(BF16)

    ws = wts["gmlp_w_s"][0]
    n_heads = ws.shape[0]
    d_gmlp = wts["gmlp_ln_g"].shape[1]
    causal = jnp.tril(jnp.ones((CHUNK, CHUNK), dtype=bool))
    ws_masked = jnp.where(causal[None], ws, 0.0).astype(BF16)
    ws_masked_t = ws_masked.transpose(0, 2, 1)
    bs_cols = jnp.repeat(wts["gmlp_b_s"][0].T, d_gmlp // n_heads, axis=1)
    d_ssm = n_groups * n_ch
    assert d_ssm == d_gmlp and d_model == 2 * d_ssm

    prepared = [x16, b_sg_re, b_sg_im, c_sg_re, c_sg_im, ws_masked_t, bs_cols]
    gather_first, _ = _gather2_forward(gather_first, prepared, "gather_first_forward")
    gather2_wait(GATHER_FIRST, gather_first, prepared, "gather_first_wait")
    gather_early, gather_early_token = _gather2_start([w16[n] for n in GATHER_EARLY], full["ffn1_w_in"],
                                                      "gather_early_start")
    h1, a1 = _ffn_in(x16, full["ffn1_w_in"], "ffn1_in", token=gather_early_token)
    gather_early, _ = _gather2_forward(gather_early, [a1], "gather_early_forward")
    gather2_wait(GATHER_EARLY, gather_early, [a1], "gather_early_wait")
    r1, x1, x1_16 = _ffn_out_ln(a1, full["ffn1_w_out"], x, row(wts["ln1_g"]), row(wts["ln1_b"]), alpha,
                                "ffn1_out_ln")

    gather_mid, token = gather_start(GATHER_MID, x1_16, "gather_mid_start")
    gather_last, token = _gather2_start([w16[n] for n in GATHER_LAST], token, "gather_last_start")
    proj = _mm(x1_16, full["mix_w_in"], name="mix_in", token=token)
    za_p = _perm_rows(proj[:, :d_ssm], n_seq)
    za_p16 = za_p.astype(BF16)
    h_re, h_im, yssm = _s5_fwd(za_p16, b_sg_re, b_sg_im, a_re, a_im, c_sg_re, c_sg_im, n_seq, "s5_fwd")
    gather_wait(GATHER_MID, gather_mid, [yssm], "gather_mid_wait")
    s5out_p = _s5_post_fwd(yssm, za_p, row(wts["ssm_d"]), full["ssm_glu_w"], row(wts["ssm_glu_b"]),
                           "s5_post")
    s5out = _unperm_rows(s5out_p, n_seq)
    gm = _gmlp_fwd(proj, row(wts["gmlp_ln_g"]), row(wts["gmlp_ln_b"]), ws_masked, bs_cols, "gmlp")
    m_mix, y_a, y_b = _merge_fwd(s5out, gm, proj, full["up_a"], full["up_b"], "merge")
    gather_last, token = _gather2_forward(gather_last, [m_mix], "gather_last_forward")
    r2, x2, x2_16 = _ffn_out_ln(m_mix[None], full["mix_w_out"][None], x1, row(wts["ln2_g"]), row(wts["ln2_b"]),
                                alpha, "mix_out_ln2", scale=1.0, token=token)

    gather2_wait(GATHER_LAST, gather_last, [x2_16], "gather_last_wait")
    h2, a2 = _ffn_in(x2_16, full["ffn2_w_in"], "ffn2_in")
    r3, x3, _ = _ffn_out_ln(a2, full["ffn2_w_out"], x2, row(wts["ln3_g"]), row(wts["ln3_b"]), alpha,
                            "ffn2_out_ln")

    small = {}
    send = {}

    def lane_dense(n, v):
        return v.reshape(v.shape[:2] + (-1,)) if n in ("ssm_b_re", "ssm_b_im") else v

    def on_wire(n, g):
        g = lane_dense(n, g)
        return (g.astype(BF16) if n in SMALL_BF16 else g)[None]
    loss_parts, dr3, dr3_h, dw_gate, dw_proj, dg, db = _head(
        x3, r3, row(wts["ln3_g"]), p16, target, full["ple_w_gate"], full["ple_w_proj"], 0.5, "head")
    loss_mine = jnp.full((1, 1, 8, 128), jnp.sum(loss_parts[::8, 0]), F32)
    send["ple_w_gate"] = dw_gate.astype(BF16).reshape(N_DEV, -1, d_model)
    send["ple_w_proj"] = _split_cols(dw_proj.astype(BF16))
    small["ln3_g"], small["ln3_b"] = dg.sum(0, keepdims=True), db.sum(0, keepdims=True)
    dh2 = _ffn_out_dx(dr3_h, full["ffn2_w_out"], h2, "ffn2_out_dx")
    dw = _ffn_out_dw(a2, dr3_h, "ffn2_out_dw")
    send["ffn2_w_out"] = dw.reshape(N_DEV, -1, d_model)
    dw = _ffn_in_dw(x2_16, dh2, "ffn2_in_dw")
    send["ffn2_w_in"] = dw.reshape((N_DEV,) + dw.shape[2:])
    group1 = ("ffn2_w_in", "ffn2_w_out", "ple_w_gate", "ple_w_proj")
    exchange1, token = _exchange_start(
        [send[n] for n in group1] + [on_wire(n, small[n]) for n in SMALL_HEAD] + [loss_mine], dh2,
        "grad_exchange1_start")
    dr2, dr2_h, dg, db = _ffn_in_dx_ln(dh2, full["ffn2_w_in"], r2, dr3, alpha, row(wts["ln2_g"]), 1.0,
                                       "ffn2_in_dx_ln2_bwd", token=token)
    small["ln2_g"], small["ln2_b"] = dg.sum(0, keepdims=True), db.sum(0, keepdims=True)
    dm = _mm(dr2_h, full["mix_w_out"], tb=True, name="mix_out_dx")
    send["mix_w_out"] = _mm(m_mix, dr2_h, ta=True, name="mix_out_dw", out_dtype=BF16).reshape(
        N_DEV, -1, d_model)
    dga, dgb, ds5out, dgm, dw_a, dw_b = _merge_bwd(
        dm, y_a, y_b, s5out, gm, proj, full["up_a"], full["up_b"], "merge_bwd")
    send["up_a"] = _split_cols(dw_a.astype(BF16))
    send["up_b"] = _split_cols(dw_b.astype(BF16))

    dzu, dzv, dws, dbs_cols, dg, db = _gmlp_bwd(
        proj, dgm, row(wts["gmlp_ln_g"]), row(wts["gmlp_ln_b"]), ws_masked, ws_masked_t, bs_cols,
        "gmlp_bwd")
    small["gmlp_ln_g"], small["gmlp_ln_b"] = dg.sum(0, keepdims=True), db.sum(0, keepdims=True)
    small["gmlp_w_s"] = jnp.where(causal[None], dws, 0.0)[None]
    small["gmlp_b_s"] = dbs_cols.reshape(CHUNK, n_heads, -1).sum(-1).T[None]

    ds5out_p = _perm_rows(ds5out, n_seq)
    dy_p, dza_skip, dw_glu, dd, dgb_glu = _s5_post_bwd(
        yssm, za_p, ds5out_p, row(wts["ssm_d"]), full["ssm_glu_w"], row(wts["ssm_glu_b"]),
        "s5_post_bwd")
    send["ssm_glu_w"] = dw_glu.astype(BF16).reshape(N_DEV, -1, d_ssm)
    small["ssm_d"], small["ssm_glu_b"] = dd.sum(0, keepdims=True), dgb_glu.sum(0, keepdims=True)
    dza_p, dbb_re, dbb_im, dc_re, dc_im, da_re, da_im = _s5_bwd(
        dy_p, dza_skip, h_re, h_im, za_p16, b_sg_re, b_sg_im, a_re, a_im, c_sg_re, c_sg_im, n_seq, "s5_bwd")
    small["ssm_c_re"] = _super_groups_out_take(dc_re, gl, n_ch, n_state)[None]
    small["ssm_c_im"] = -_super_groups_out_take(dc_im, gl, n_ch, n_state)[None]
    dbb_re = _super_groups_in_take(dbb_re, gl, n_state, n_ch)
    dbb_im = _super_groups_in_take(dbb_im, gl, n_state, n_ch)
    def first_channel(v):
        placed = jnp.pad(v.sum(0).reshape(n_groups, n_state, 1), ((0, 0), (0, 0), (0, n_ch - 1)))
        return placed.reshape(n_groups, -1)

    cotangents = (first_channel(da_re), first_channel(da_im), dbb_re.reshape(n_groups, -1),
                  dbb_im.reshape(n_groups, -1))
    d_lre, d_lim, d_ldt, d_bre, d_bim = _s5_discretise_bwd(disc_in, cotangents, "s5_discretise_bwd")
    per_state = (n_groups, n_state, n_ch)
    small["ssm_lambda_re"] = d_lre.reshape(per_state).sum(-1)[None]
    small["ssm_lambda_im"] = d_lim.reshape(per_state).sum(-1)[None]
    small["ssm_log_dt"] = d_ldt.sum(-1)[None]
    small["ssm_b_re"] = d_bre.reshape((1,) + per_state)
    small["ssm_b_im"] = d_bim.reshape((1,) + per_state)

    dproj = jnp.concatenate([_unperm_rows(dza_p, n_seq), dzu, dzv, dga, dgb], axis=1)
    group2 = ("mix_w_out", "up_a", "up_b", "ssm_glu_w")
    exchange2, token = _exchange_start(
        [send[n] for n in group2] + [on_wire(n, small[n]) for n in SMALL_MID], dproj, "grad_exchange2_start")
    send["mix_w_in"] = _split_cols(_mm(x1_16, dproj, ta=True, name="mix_in_dw", out_dtype=BF16, token=token))
    exchange3, token = _exchange_start([send["mix_w_in"]], dproj, "grad_exchange3_start")
    dr1, dr1_h, dg, db = _mm_ln_bwd(dproj, full["mix_w_in"], r1, dr2, alpha, row(wts["ln1_g"]), 0.5,
                                    "mix_in_dx_ln1_bwd", token=token)
    small["ln1_g"], small["ln1_b"] = dg.sum(0, keepdims=True), db.sum(0, keepdims=True)
    dw = _ffn_out_dw(a1, dr1_h, "ffn1_out_dw")
    send["ffn1_w_out"] = dw.reshape(N_DEV, -1, d_model)
    exchange4, token = _exchange_start([send["ffn1_w_out"]] + [on_wire(n, small[n]) for n in SMALL_LATE], dr1_h,
                                       "grad_exchange4_start")
    dh1 = _ffn_out_dx(dr1_h, full["ffn1_w_out"], h1, "ffn1_out_dx", token=token)
    dw = _ffn_in_dw(x16, dh1, "ffn1_in_dw")
    send["ffn1_w_in"] = dw.reshape((N_DEV,) + dw.shape[2:])
    exchange5, token = _exchange_start([send["ffn1_w_in"]], dh1, "grad_exchange5_start")
    grad_x = _ffn_in_dx(dh1, full["ffn1_w_in"], dr1, alpha, "ffn1_in_dx", token=token)

    results = {}

    def update(names, recv, prefix):
        for n, r in zip(names, recv):
            results[n] = _adamw(r, wts[n], mom[n], var[n], prefix + n)

    def update_small(names, recv, name):
        items = [(r, lane_dense(n, wts[n]), lane_dense(n, mom[n]), lane_dense(n, var[n])) for n, r in zip(names, recv)]
        for n, outs in zip(names, _adamw_small(items, name)):
            results[n] = tuple(o.reshape(wts[n].shape) for o in outs)

    def done(names):
        return [results[n][3] for n in names]

    recv = _exchange_wait(exchange1, [grad_x], "grad_exchange1_wait")
    update(group1, recv[:len(group1)], "adamw_")
    update_small(SMALL_HEAD, recv[len(group1):-1], "adamw_ln3")
    loss = jnp.sum(recv[-1][:, 0, 0, 0])
    recv = _exchange_wait(exchange2, done(group1 + SMALL_HEAD), "grad_exchange2_wait")
    update(group2, recv[:len(group2)], "adamw_")
    update_small(SMALL_MID, recv[len(group2):], "adamw_small")
    recv = _exchange_wait(exchange3, done(group2 + SMALL_MID), "grad_exchange3_wait")
    update(("mix_w_in",), recv, "adamw_")
    recv = _exchange_wait(exchange4, done(("mix_w_in",)), "grad_exchange4_wait")
    update(("ffn1_w_out",), recv[:1], "adamw_")
    update_small(SMALL_LATE, recv[1:], "adamw_ln1")
    recv = _exchange_wait(exchange5, done(("ffn1_w_out",) + SMALL_LATE), "grad_exchange5_wait")
    update(("ffn1_w_in",), recv, "adamw_")

    outs = [loss, grad_x.reshape(n_seq, seq, d_model)]
    for k in range(4):
        outs += [results[n][k][None] if n in SHARDED else results[n][k] for n in WEIGHTS]
    return tuple(outs)
```

```python
import math

import jax
import jax.numpy as jnp
from jax import lax
from jax.experimental import pallas as pl
from jax.experimental.pallas import tpu as pltpu

F32 = jnp.float32
BF16 = jnp.bfloat16

N_DEV = 8
LN_EPS = 1e-5
CHUNK = 128
N_SEG = 8
S5_CHANNELS_PER_STEP = 128
VMEM_LIMIT_BYTES = 56 * 1024 * 1024
MM_OPERAND_BLOCK_BYTES = 8 * 1024 * 1024
ADAM_BLOCK_BYTES = 6 * 1024 * 1024

ADAM_LR = 0.001
ADAM_B1 = 0.9
ADAM_B2 = 0.999
ADAM_EPS = 1e-08
ADAM_WD = 0.01
ADAM_STEP = 10

COL_SHARDED = ("ffn1_w_in", "mix_w_in", "up_a", "up_b", "ffn2_w_in", "ple_w_proj")
SHARDED = ("ffn1_w_in", "ffn1_w_out", "mix_w_in", "ssm_glu_w", "up_a", "up_b", "mix_w_out",
           "ffn2_w_in", "ffn2_w_out", "ple_w_proj", "ple_w_gate")
WEIGHTS = ("ffn1_w_in", "ffn1_w_out", "ln1_g", "ln1_b", "mix_w_in", "ssm_lambda_re", "ssm_lambda_im",
           "ssm_log_dt", "ssm_b_re", "ssm_b_im", "ssm_c_re", "ssm_c_im", "ssm_d", "ssm_glu_w",
           "ssm_glu_b", "gmlp_ln_g", "gmlp_ln_b", "gmlp_w_s", "gmlp_b_s", "up_a", "up_b", "mix_w_out",
           "ln2_g", "ln2_b", "ffn2_w_in", "ffn2_w_out", "ln3_g", "ln3_b", "ple_w_proj", "ple_w_gate")
GATHER_FIRST = ("ffn1_w_in",)
GATHER_EARLY = ("ffn1_w_out", "mix_w_in")
GATHER_MID = ("ssm_glu_w", "up_a", "up_b", "mix_w_out")
GATHER_LAST = ("ffn2_w_in", "ffn2_w_out", "ple_w_proj", "ple_w_gate")
SMALL = tuple(n for n in WEIGHTS if n not in SHARDED)
SMALL_HEAD = ("ln3_g", "ln3_b")
SMALL_LATE = ("ln1_g", "ln1_b")
SMALL_MID = tuple(n for n in SMALL if n not in SMALL_HEAD + SMALL_LATE)
SMALL_BF16 = ("gmlp_w_s", "ssm_b_re", "ssm_b_im", "ssm_c_re", "ssm_c_im")
INPUT_NAMES = ("x", "p") + WEIGHTS + ("loss_target",) + tuple("m_" + n for n in WEIGHTS) + tuple(
    "v_" + n for n in WEIGHTS)


def _pick(dim, pref, mult=128):
    if dim <= pref:
        return dim
    d = (pref // mult) * mult
    while d >= mult:
        if dim % d == 0:
            return d
        d -= mult
    return dim


def _params(n_axes=1):
    return pltpu.CompilerParams(dimension_semantics=("arbitrary",) * n_axes,
                                vmem_limit_bytes=VMEM_LIMIT_BYTES)


def _sigmoid(v):
    return 1.0 / (1.0 + jnp.exp(-v))


_GELU_C = math.sqrt(2.0 / math.pi)


def _gelu(v):
    return 0.5 * v * (1.0 + jnp.tanh(_GELU_C * (v + 0.044715 * (v * v * v))))


def _gelu_grad(v):
    t = jnp.tanh(_GELU_C * (v + 0.044715 * (v * v * v)))
    return 0.5 * (1.0 + t) + 0.5 * v * (1.0 - t * t) * (_GELU_C * (1.0 + 3.0 * 0.044715 * v * v))


def _ln_stats(r):
    mu = jnp.mean(r, axis=-1, keepdims=True)
    xc = r - mu
    var = jnp.mean(xc * xc, axis=-1, keepdims=True)
    rstd = lax.rsqrt(var + LN_EPS)
    return xc * rstd, rstd


def _ln_bwd(dy, xhat, rstd, g):
    dxh = dy * g
    m1 = jnp.mean(dxh, axis=-1, keepdims=True)
    m2 = jnp.mean(dxh * xhat, axis=-1, keepdims=True)
    return rstd * (dxh - m1 - xhat * m2)


def _fold8(v):
    rows, w = v.shape
    return jnp.sum(v.reshape(rows // 8, 8, w), axis=0)


def _dot(a, b, ta=False, tb=False):
    dn = (((0 if ta else 1,), (1 if tb else 0,)), ((), ()))
    return lax.dot_general(a.astype(BF16), b.astype(BF16), dn, preferred_element_type=F32)


_TOKEN = pl.BlockSpec((8, 128), lambda *_: (0, 0))


def _mm(a, b, *, name, ta=False, tb=False, out_dtype=F32, add=None, add_scale=1.0, token=None,
        tm=2048, tn=512, tk=4096):
    m_dim = a.shape[1] if ta else a.shape[0]
    k_dim = a.shape[0] if ta else a.shape[1]
    n_dim = b.shape[0] if tb else b.shape[1]
    assert (b.shape[1] if tb else b.shape[0]) == k_dim, (name, a.shape, b.shape)
    tk = _pick(k_dim, tk)
    tm = min(tm, max(256, MM_OPERAND_BLOCK_BYTES // (tk * a.dtype.itemsize)))
    tm, tn = _pick(m_dim, tm), _pick(n_dim, tn)
    nk = k_dim // tk
    has_add = add is not None

    def finish(r, add_ref, o_ref):
        if has_add:
            r = r + add_scale * add_ref[...].astype(F32)
        o_ref[...] = r.astype(out_dtype)

    def body(*refs):
        a_ref, b_ref = refs[:2]
        add_ref = refs[2] if has_add else None
        o_ref = refs[n_in]
        if nk == 1:
            finish(_dot(a_ref[...], b_ref[...], ta, tb), add_ref, o_ref)
            return
        acc_ref = refs[-1]
        k = pl.program_id(2)

        @pl.when(k == 0)
        def _():
            acc_ref[...] = jnp.zeros_like(acc_ref)

        acc_ref[...] += _dot(a_ref[...], b_ref[...], ta, tb)

        @pl.when(k == nk - 1)
        def _():
            finish(acc_ref[...], add_ref, o_ref)

    a_spec = (pl.BlockSpec((tk, tm), lambda i, j, k: (k, i)) if ta
              else pl.BlockSpec((tm, tk), lambda i, j, k: (i, k)))
    b_spec = (pl.BlockSpec((tn, tk), lambda i, j, k: (j, k)) if tb
              else pl.BlockSpec((tk, tn), lambda i, j, k: (k, j)))
    in_specs = [a_spec, b_spec]
    operands = [a, b]
    if has_add:
        in_specs.append(pl.BlockSpec((tm, tn), lambda i, j, k: (i, j)))
        operands.append(add)
    if token is not None:
        in_specs.append(_TOKEN)
        operands.append(token)
    n_in = len(operands)
    return pl.pallas_call(
        body, name=name,
        out_shape=jax.ShapeDtypeStruct((m_dim, n_dim), out_dtype),
        grid=(m_dim // tm, n_dim // tn, nk),
        in_specs=in_specs,
        out_specs=pl.BlockSpec((tm, tn), lambda i, j, k: (i, j)),
        scratch_shapes=[pltpu.VMEM((tm, tn), F32)] if nk > 1 else [],
        compiler_params=_params(3),
    )(*operands)


def _to_bf16(arrays, name, token=None):
    n = len(arrays)
    extra = [] if token is None else [token]

    def body(*refs):
        for src, dst in zip(refs[:n], refs[n + len(extra):]):
            dst[...] = src[...].astype(BF16)

    vmem = pl.BlockSpec(memory_space=pltpu.VMEM)
    return pl.pallas_call(
        body, name=name, out_shape=tuple(jax.ShapeDtypeStruct(a.shape, BF16) for a in arrays),
        in_specs=[vmem] * (n + len(extra)), out_specs=tuple([vmem] * n),
        compiler_params=pltpu.CompilerParams(vmem_limit_bytes=VMEM_LIMIT_BYTES))(*arrays, *extra)


def _rows(tr, w, cb=0):
    return pl.BlockSpec((tr, w), lambda i: (i, cb))


def _whole(shape):
    nd = len(shape)
    return pl.BlockSpec(tuple(shape), lambda i: (0,) * nd)


def _ffn_in(x16, w_in, name, token=None):
    t, d = x16.shape
    _, nb, _, fb = w_in.shape
    tm = _pick(t, 1024, 8)
    extra = [] if token is None else [token]

    def body(*refs):
        x_ref, wg_ref, wu_ref = refs[:3]
        h_ref, a_ref = refs[-2:]
        xv = x_ref[...]
        g = _dot(xv, wg_ref[0, 0])
        u = _dot(xv, wu_ref[0, 0])
        h_ref[0, 0] = g.astype(BF16)
        h_ref[0, 1] = u.astype(BF16)
        a_ref[0] = (g * _sigmoid(g) * u).astype(BF16)

    return pl.pallas_call(
        body, name=name,
        out_shape=(jax.ShapeDtypeStruct((nb, 2, t, fb), BF16), jax.ShapeDtypeStruct((nb, t, fb), BF16)),
        grid=(t // tm, nb),
        in_specs=[pl.BlockSpec((tm, d), lambda i, j: (i, 0)),
                  pl.BlockSpec((1, 1, d, fb), lambda i, j: (0, j, 0, 0)),
                  pl.BlockSpec((1, 1, d, fb), lambda i, j: (1, j, 0, 0))] + [_TOKEN] * len(extra),
        out_specs=(pl.BlockSpec((1, 2, tm, fb), lambda i, j: (j, 0, i, 0)),
                   pl.BlockSpec((1, tm, fb), lambda i, j: (j, i, 0))),
        compiler_params=_params(2))(x16, w_in, w_in, *extra)


def _ffn_out_ln(a, w_out, xin, g, b, alpha, name, scale=0.5, token=None):
    nb, t, fb = a.shape
    d = w_out.shape[2]
    tm = _pick(t, 512, 8)
    extra = [] if token is None else [token]

    def body(*refs):
        a_ref, w_ref, x_ref, g_ref, b_ref = refs[:5]
        r_ref, y_ref, y16_ref = refs[-3:]
        f = _dot(a_ref[0], w_ref[0])
        for jb in range(1, nb):
            f = f + _dot(a_ref[jb], w_ref[jb])
        r = alpha * x_ref[...] + scale * f
        xhat, _ = _ln_stats(r)
        y = xhat * g_ref[...] + b_ref[...]
        r_ref[...] = r
        y_ref[...] = y
        y16_ref[...] = y.astype(BF16)

    return pl.pallas_call(
        body, name=name,
        out_shape=(jax.ShapeDtypeStruct((t, d), F32), jax.ShapeDtypeStruct((t, d), F32),
                   jax.ShapeDtypeStruct((t, d), BF16)),
        grid=(t // tm,),
        in_specs=[pl.BlockSpec((nb, tm, fb), lambda i: (0, i, 0)), _whole(w_out.shape), _rows(tm, d),
                  _whole((1, d)), _whole((1, d))] + [_TOKEN] * len(extra),
        out_specs=(_rows(tm, d), _rows(tm, d), _rows(tm, d)),
        compiler_params=_params())(a, w_out, xin, g, b, *extra)


def _ffn_out_dx(drs, w_out, h, name, token=None):
    nb, _, t, fb = h.shape
    d = w_out.shape[2]
    tm = _pick(t, 1024, 8)
    extra = [] if token is None else [token]

    def body(*refs):
        dr_ref, w_ref, h_ref, dh_ref = refs[0], refs[1], refs[2], refs[-1]
        da = _dot(dr_ref[...], w_ref[0], tb=True)
        g, u = h_ref[0, 0].astype(F32), h_ref[0, 1].astype(F32)
        sg = _sigmoid(g)
        dh_ref[0, 0] = (da * u * (sg * (1.0 + g * (1.0 - sg)))).astype(BF16)
        dh_ref[0, 1] = (da * (g * sg)).astype(BF16)

    return pl.pallas_call(
        body, name=name, out_shape=jax.ShapeDtypeStruct((nb, 2, t, fb), BF16), grid=(t // tm, nb),
        in_specs=[pl.BlockSpec((tm, d), lambda i, j: (i, 0)),
                  pl.BlockSpec((1, fb, d), lambda i, j: (j, 0, 0)),
                  pl.BlockSpec((1, 2, tm, fb), lambda i, j: (j, 0, i, 0))] + [_TOKEN] * len(extra),
        out_specs=pl.BlockSpec((1, 2, tm, fb), lambda i, j: (j, 0, i, 0)),
        compiler_params=_params(2))(drs, w_out, h, *extra)


def _ffn_out_dw(a, drs, name):
    nb, t, fb = a.shape
    d = drs.shape[1]

    def body(a_ref, dr_ref, o_ref):
        o_ref[0] = _dot(a_ref[0], dr_ref[...], ta=True).astype(BF16)

    return pl.pallas_call(
        body, name=name, out_shape=jax.ShapeDtypeStruct((nb, fb, d), BF16), grid=(nb,),
        in_specs=[pl.BlockSpec((1, t, fb), lambda j: (j, 0, 0)), pl.BlockSpec((t, d), lambda j: (0, 0))],
        out_specs=pl.BlockSpec((1, fb, d), lambda j: (j, 0, 0)),
        compiler_params=_params())(a, drs)


def _ffn_in_dw(x16, dh, name, token=None):
    t, d = x16.shape
    nb, _, _, fb = dh.shape
    extra = [] if token is None else [token]

    def body(*refs):
        x_ref, dh_ref, o_ref = refs[0], refs[1], refs[-1]
        o_ref[0, 0] = _dot(x_ref[...], dh_ref[0, 0], ta=True).astype(BF16)

    return pl.pallas_call(
        body, name=name, out_shape=jax.ShapeDtypeStruct((2, nb, d, fb), BF16), grid=(nb, 2),
        in_specs=[pl.BlockSpec((t, d), lambda j, s: (0, 0)),
                  pl.BlockSpec((1, 1, t, fb), lambda j, s: (j, s, 0, 0))] + [_TOKEN] * len(extra),
        out_specs=pl.BlockSpec((1, 1, d, fb), lambda j, s: (s, j, 0, 0)),
        compiler_params=_params(2))(x16, dh, *extra)


def _ffn_in_dx(dh, w_in, add, add_scale, name, token=None):
    nb, _, t, fb = dh.shape
    d = w_in.shape[2]
    tm = _pick(t, 512, 8)
    has_add = add is not None
    extra = [] if token is None else [token]

    def body(*refs):
        dh_ref, w_ref = refs[:2]
        o_ref = refs[-1]
        acc = None
        for jb in range(nb):
            for s in range(2):
                part = _dot(dh_ref[jb, s], w_ref[s, jb], tb=True)
                acc = part if acc is None else acc + part
        if has_add:
            acc = acc + add_scale * refs[2][...]
        o_ref[...] = acc

    return pl.pallas_call(
        body, name=name, out_shape=jax.ShapeDtypeStruct((t, d), F32), grid=(t // tm,),
        in_specs=[pl.BlockSpec((nb, 2, tm, fb), lambda i: (0, 0, i, 0)), _whole(w_in.shape)]
        + ([_rows(tm, d)] if has_add else []) + [_TOKEN] * len(extra),
        out_specs=_rows(tm, d),
        compiler_params=_params())(*([dh, w_in] + ([add] if has_add else []) + extra))


def _ffn_in_dx_ln(dh, w_in, r, dy_b, b_scale, ln_g, out_scale, name, token=None):
    nb, _, t, fb = dh.shape
    d = w_in.shape[2]
    tm = _pick(t, 512, 8)
    extra = [] if token is None else [token]

    def body(*refs):
        dh_ref, w_ref, r_ref, dyb_ref, g_ref = refs[:5]
        dr_ref, drs_ref, dg_ref, dbeta_ref = refs[-4:]
        dy = b_scale * dyb_ref[...]
        for jb in range(nb):
            for s in range(2):
                dy = dy + _dot(dh_ref[jb, s], w_ref[s, jb], tb=True)
        xhat, rstd = _ln_stats(r_ref[...])
        dr = _ln_bwd(dy, xhat, rstd, g_ref[...])
        dr_ref[...] = dr
        drs_ref[...] = (out_scale * dr).astype(BF16)

        @pl.when(pl.program_id(0) == 0)
        def _():
            dg_ref[...] = jnp.zeros_like(dg_ref)
            dbeta_ref[...] = jnp.zeros_like(dbeta_ref)

        dg_ref[...] += _fold8(dy * xhat)
        dbeta_ref[...] += _fold8(dy)

    return pl.pallas_call(
        body, name=name,
        out_shape=(jax.ShapeDtypeStruct((t, d), F32), jax.ShapeDtypeStruct((t, d), BF16),
                   jax.ShapeDtypeStruct((8, d), F32), jax.ShapeDtypeStruct((8, d), F32)),
        grid=(t // tm,),
        in_specs=[pl.BlockSpec((nb, 2, tm, fb), lambda i: (0, 0, i, 0)), _whole(w_in.shape), _rows(tm, d),
                  _rows(tm, d), _whole((1, d))] + [_TOKEN] * len(extra),
        out_specs=(_rows(tm, d), _rows(tm, d), _whole((8, d)), _whole((8, d))),
        compiler_params=_params())(dh, w_in, r, dy_b, ln_g, *extra)


def _mm_ln_bwd(a, w, r, dy_b, b_scale, ln_g, out_scale, name, token=None):
    t, k_dim = a.shape
    d = w.shape[0]
    tm = _pick(t, 512, 8)
    extra = [] if token is None else [token]

    def body(*refs):
        a_ref, w_ref, r_ref, dyb_ref, g_ref = refs[:5]
        dr_ref, drs_ref, dg_ref, dbeta_ref = refs[-4:]
        dy = _dot(a_ref[...], w_ref[...], tb=True) + b_scale * dyb_ref[...]
        xhat, rstd = _ln_stats(r_ref[...])
        dr = _ln_bwd(dy, xhat, rstd, g_ref[...])
        dr_ref[...] = dr
        drs_ref[...] = (out_scale * dr).astype(BF16)

        @pl.when(pl.program_id(0) == 0)
        def _():
            dg_ref[...] = jnp.zeros_like(dg_ref)
            dbeta_ref[...] = jnp.zeros_like(dbeta_ref)

        dg_ref[...] += _fold8(dy * xhat)
        dbeta_ref[...] += _fold8(dy)

    return pl.pallas_call(
        body, name=name,
        out_shape=(jax.ShapeDtypeStruct((t, d), F32), jax.ShapeDtypeStruct((t, d), BF16),
                   jax.ShapeDtypeStruct((8, d), F32), jax.ShapeDtypeStruct((8, d), F32)),
        grid=(t // tm,),
        in_specs=[_rows(tm, k_dim), _whole(w.shape), _rows(tm, d), _rows(tm, d), _whole((1, d))]
        + [_TOKEN] * len(extra),
        out_specs=(_rows(tm, d), _rows(tm, d), _whole((8, d)), _whole((8, d))),
        compiler_params=_params())(a, w, r, dy_b, ln_g, *extra)


def _take_row(v, row_id, s):
    return jnp.sum(jnp.where(row_id == s, v, 0.0), axis=0, keepdims=True)


def _complex_power(ar, ai, n):
    out = None
    while n:
        if n & 1:
            out = (ar, ai) if out is None else (out[0] * ar - out[1] * ai, out[0] * ai + out[1] * ar)
        ar, ai = ar * ar - ai * ai, 2.0 * ar * ai
        n >>= 1
    return out


def _seg_carries(f_re, f_im, p_re, p_im, reverse):
    row_id = lax.broadcasted_iota(jnp.int32, f_re.shape, 0)
    p_re, p_im = _take_row(p_re, row_id, 0), _take_row(p_im, row_id, 0)
    out_re, out_im = jnp.zeros_like(f_re), jnp.zeros_like(f_im)
    c_re, c_im = jnp.zeros_like(p_re), jnp.zeros_like(p_im)
    order = range(N_SEG - 1, -1, -1) if reverse else range(N_SEG)
    for s in order:
        out_re = jnp.where(row_id == s, c_re, out_re)
        out_im = jnp.where(row_id == s, c_im, out_im)
        fr, fi = _take_row(f_re, row_id, s), _take_row(f_im, row_id, s)
        c_re, c_im = fr + p_re * c_re - p_im * c_im, fi + p_re * c_im + p_im * c_re
    return out_re, out_im


def _s5_fwd(za16, b_re, b_im, a_re, a_im, c_re, c_im, n_seq, name):
    t = za16.shape[0]
    n_sg, ch, st = b_re.shape
    seq = t // n_seq
    steps = seq // N_SEG

    def body(za_ref, bre_ref, bim_ref, are, aim, cre_ref, cim_ref, hre, him, y_ref):
        za = za_ref[...]
        hre[...] = _dot(za, bre_ref[0])
        him[...] = _dot(za, bim_ref[0])
        ar = jnp.broadcast_to(are[...], (N_SEG, st))
        ai = jnp.broadcast_to(aim[...], (N_SEG, st))
        zero = jnp.zeros((N_SEG, st), F32)

        def local(k, carry):
            lr, li = carry
            rows = pl.ds(pl.multiple_of(k * N_SEG, N_SEG), N_SEG)
            nr = ar * lr - ai * li + hre[rows, :]
            ni = ar * li + ai * lr + him[rows, :]
            hre[rows, :] = nr
            him[rows, :] = ni
            return nr, ni

        f_re, f_im = lax.fori_loop(0, steps, local, (zero, zero))
        c_re, c_im = _seg_carries(f_re, f_im, *_complex_power(ar, ai, steps), reverse=False)

        def fix(k, carry):
            qr, qi = carry
            rows = pl.ds(pl.multiple_of(k * N_SEG, N_SEG), N_SEG)
            hre[rows, :] = hre[rows, :] + qr
            him[rows, :] = him[rows, :] + qi
            return ar * qr - ai * qi, ar * qi + ai * qr

        lax.fori_loop(0, steps, fix, (ar * c_re - ai * c_im, ar * c_im + ai * c_re))
        y_ref[...] = _dot(hre[...], cre_ref[0]) + _dot(him[...], cim_ref[0])

    rows_ch = pl.BlockSpec((seq, ch), lambda s, j: (s, j))
    rows_st = pl.BlockSpec((seq, st), lambda s, j: (s, j))
    vec = pl.BlockSpec((1, st), lambda s, j: (0, j))
    b_blk = pl.BlockSpec((1, ch, st), lambda s, j: (j, 0, 0))
    c_blk = pl.BlockSpec((1, st, ch), lambda s, j: (j, 0, 0))
    return pl.pallas_call(
        body, name=name,
        out_shape=(jax.ShapeDtypeStruct((t, n_sg * st), F32), jax.ShapeDtypeStruct((t, n_sg * st), F32),
                   jax.ShapeDtypeStruct((t, n_sg * ch), F32)),
        grid=(n_seq, n_sg), in_specs=[rows_ch, b_blk, b_blk, vec, vec, c_blk, c_blk],
        out_specs=(rows_st, rows_st, rows_ch),
        compiler_params=_params(2))(za16, b_re, b_im, a_re, a_im, c_re, c_im)


def _s5_bwd(dy16, dza_skip, h_re, h_im, za16, b_re, b_im, a_re, a_im, c_re, c_im, n_seq, name):
    t = dy16.shape[0]
    n_sg, ch, st = b_re.shape
    seq = t // n_seq
    steps = seq // N_SEG

    def body(dy_ref, skip_ref, hre, him, za_ref, bre_ref, bim_ref, are, aim, cre_ref, cim_ref,
             dza_ref, dbre_ref, dbim_ref, dcre_ref, dcim_ref, dare, daim, lre, lim):
        dy = dy_ref[...]
        lre[...] = _dot(dy, cre_ref[0], tb=True)
        lim[...] = _dot(dy, cim_ref[0], tb=True)
        ar = jnp.broadcast_to(are[...], (N_SEG, st))
        ai = -jnp.broadcast_to(aim[...], (N_SEG, st))
        zero = jnp.zeros((N_SEG, st), F32)

        def local(i, carry):
            lr, li = carry
            k = steps - 1 - i
            rows = pl.ds(pl.multiple_of(k * N_SEG, N_SEG), N_SEG)
            nr = ar * lr - ai * li + lre[rows, :]
            ni = ar * li + ai * lr + lim[rows, :]
            lre[rows, :] = nr
            lim[rows, :] = ni
            return nr, ni

        f_re, f_im = lax.fori_loop(0, steps, local, (zero, zero))
        c_re, c_im = _seg_carries(f_re, f_im, *_complex_power(ar, ai, steps), reverse=True)

        def accumulate(lam_r, lam_i, hp_r, hp_i, acc_r, acc_i):
            return acc_r + (lam_r * hp_r + lam_i * hp_i), acc_i + (lam_i * hp_r - lam_r * hp_i)

        def fix(i, carry):
            qr, qi, acc_r, acc_i = carry
            k = steps - 1 - i
            rows = pl.ds(pl.multiple_of(k * N_SEG, N_SEG), N_SEG)
            prev = pl.ds(pl.multiple_of((k - 1) * N_SEG, N_SEG), N_SEG)
            lam_r = lre[rows, :] + qr
            lam_i = lim[rows, :] + qi
            lre[rows, :] = lam_r
            lim[rows, :] = lam_i
            acc_r, acc_i = accumulate(lam_r, lam_i, hre[prev, :], him[prev, :], acc_r, acc_i)
            return ar * qr - ai * qi, ar * qi + ai * qr, acc_r, acc_i

        qr, qi, acc_r, acc_i = lax.fori_loop(
            0, steps - 1, fix, (ar * c_re - ai * c_im, ar * c_im + ai * c_re, zero, zero))
        first = pl.ds(0, N_SEG)
        last = pl.ds((steps - 1) * N_SEG, N_SEG)
        lam_r = lre[first, :] + qr
        lam_i = lim[first, :] + qi
        lre[first, :] = lam_r
        lim[first, :] = lam_i
        row_id = lax.broadcasted_iota(jnp.int32, (N_SEG, st), 0)
        hp_r = jnp.where(row_id == 0, 0.0, pltpu.roll(hre[last, :], 1, 0))
        hp_i = jnp.where(row_id == 0, 0.0, pltpu.roll(him[last, :], 1, 0))
        acc_r, acc_i = accumulate(lam_r, lam_i, hp_r, hp_i, acc_r, acc_i)

        lam_re16 = lre[...].astype(BF16)
        lam_im16 = lim[...].astype(BF16)
        dza_ref[...] = (_dot(lam_re16, bre_ref[0], tb=True) + _dot(lam_im16, bim_ref[0], tb=True)
                        + skip_ref[...]).astype(BF16)

        @pl.when(pl.program_id(1) == 0)
        def _():
            for ref in (dbre_ref, dbim_ref, dcre_ref, dcim_ref, dare, daim):
                ref[...] = jnp.zeros_like(ref)

        za = za_ref[...]
        dbre_ref[0] += _dot(za, lam_re16, ta=True)
        dbim_ref[0] += _dot(za, lam_im16, ta=True)
        dcre_ref[0] += _dot(hre[...], dy, ta=True)
        dcim_ref[0] += _dot(him[...], dy, ta=True)
        dare[...] += acc_r
        daim[...] += acc_i

    rows_ch = pl.BlockSpec((seq, ch), lambda j, s: (s, j))
    rows_st = pl.BlockSpec((seq, st), lambda j, s: (s, j))
    vec = pl.BlockSpec((1, st), lambda j, s: (0, j))
    b_blk = pl.BlockSpec((1, ch, st), lambda j, s: (j, 0, 0))
    c_blk = pl.BlockSpec((1, st, ch), lambda j, s: (j, 0, 0))
    acc = pl.BlockSpec((N_SEG, st), lambda j, s: (0, j))
    return pl.pallas_call(
        body, name=name,
        out_shape=(jax.ShapeDtypeStruct((t, n_sg * ch), BF16),
                   jax.ShapeDtypeStruct((n_sg, ch, st), F32), jax.ShapeDtypeStruct((n_sg, ch, st), F32),
                   jax.ShapeDtypeStruct((n_sg, st, ch), F32), jax.ShapeDtypeStruct((n_sg, st, ch), F32),
                   jax.ShapeDtypeStruct((N_SEG, n_sg * st), F32), jax.ShapeDtypeStruct((N_SEG, n_sg * st), F32)),
        grid=(n_sg, n_seq),
        in_specs=[rows_ch, rows_ch, rows_st, rows_st, rows_ch, b_blk, b_blk, vec, vec, c_blk, c_blk],
        out_specs=(rows_ch, b_blk, b_blk, c_blk, c_blk, acc, acc),
        scratch_shapes=[pltpu.VMEM((seq, st), F32), pltpu.VMEM((seq, st), F32)],
        compiler_params=_params(2))(dy16, dza_skip, h_re, h_im, za16, b_re, b_im, a_re, a_im, c_re, c_im)


def _s5_post_fwd(yssm, u, d_skip, glu_w, glu_b, name):
    t, w = yssm.shape
    tr = _pick(t, 512, 8)

    def body(y_ref, u_ref, d_ref, w_ref, b_ref, o_ref):
        yg = _gelu(y_ref[...] + d_ref[...] * u_ref[...])
        pre = _dot(yg, w_ref[...]) + b_ref[...]
        o_ref[...] = yg * _sigmoid(pre)

    return pl.pallas_call(
        body, name=name, out_shape=jax.ShapeDtypeStruct((t, w), F32), grid=(t // tr,),
        in_specs=[_rows(tr, w), _rows(tr, w), _whole((1, w)), _whole((w, w)), _whole((1, w))],
        out_specs=_rows(tr, w), compiler_params=_params())(yssm, u, d_skip, glu_w, glu_b)


def _s5_post_bwd(yssm, u, dout, d_skip, glu_w, glu_b, name):
    t, w = yssm.shape
    tr = _pick(t, 512, 8)

    def body(y_ref, u_ref, do_ref, d_ref, w_ref, b_ref, dy_ref, du_ref, dw_ref, dd_ref, db_ref):
        uu = u_ref[...]
        y = y_ref[...] + d_ref[...] * uu
        yg = _gelu(y)
        sg = _sigmoid(_dot(yg, w_ref[...]) + b_ref[...])
        do = do_ref[...]
        dpre = do * yg * sg * (1.0 - sg)
        dyg = do * sg + _dot(dpre, w_ref[...], tb=True)
        dy = dyg * _gelu_grad(y)
        dy_ref[...] = dy.astype(BF16)
        du_ref[...] = dy * d_ref[...]

        @pl.when(pl.program_id(0) == 0)
        def _():
            dw_ref[...] = jnp.zeros_like(dw_ref)
            dd_ref[...] = jnp.zeros_like(dd_ref)
            db_ref[...] = jnp.zeros_like(db_ref)

        dw_ref[...] += _dot(yg, dpre, ta=True)
        dd_ref[...] += _fold8(dy * uu)
        db_ref[...] += _fold8(dpre)

    return pl.pallas_call(
        body, name=name,
        out_shape=(jax.ShapeDtypeStruct((t, w), BF16), jax.ShapeDtypeStruct((t, w), F32),
                   jax.ShapeDtypeStruct((w, w), F32), jax.ShapeDtypeStruct((8, w), F32),
                   jax.ShapeDtypeStruct((8, w), F32)),
        grid=(t // tr,),
        in_specs=[_rows(tr, w), _rows(tr, w), _rows(tr, w), _whole((1, w)), _whole((w, w)),
                  _whole((1, w))],
        out_specs=(_rows(tr, w), _rows(tr, w), _whole((w, w)), _whole((8, w)), _whole((8, w))),
        compiler_params=_params())(yssm, u, dout, d_skip, glu_w, glu_b)


def _head_select(parts, head_dim):
    col_head = lax.broadcasted_iota(jnp.int32, parts[0].shape, 1) // head_dim
    out = jnp.zeros_like(parts[0])
    for h, part in enumerate(parts):
        out = jnp.where(col_head == h, part, out)
    return out


def _gmlp_fwd(proj, ln_g, ln_b, ws, bs_cols, name):
    t = proj.shape[0]
    n_heads = ws.shape[0]
    w = bs_cols.shape[1]
    head_dim = w // n_heads
    cpb = _pick(t // CHUNK, 4, 1)
    tr = cpb * CHUNK

    def body(zu_ref, zv_ref, g_ref, b_ref, ws_ref, bs_ref, o_ref):
        for c in range(cpb):
            rows = pl.ds(c * CHUNK, CHUNK)
            xhat, _ = _ln_stats(_gelu(zv_ref[rows, :]))
            vn = (xhat * g_ref[...] + b_ref[...]).astype(BF16)
            s = _head_select([_dot(ws_ref[h], vn) for h in range(n_heads)], head_dim) + bs_ref[...]
            o_ref[rows, :] = _gelu(zu_ref[rows, :]) * s

    return pl.pallas_call(
        body, name=name, out_shape=jax.ShapeDtypeStruct((t, w), F32), grid=(t // tr,),
        in_specs=[_rows(tr, w, 1), _rows(tr, w, 2), _whole((1, w)), _whole((1, w)),
                  _whole(ws.shape), _whole(bs_cols.shape)],
        out_specs=_rows(tr, w), compiler_params=_params())(proj, proj, ln_g, ln_b, ws, bs_cols)


def _gmlp_bwd(proj, dout, ln_g, ln_b, ws, ws_t, bs_cols, name):
    t = proj.shape[0]
    n_heads = ws.shape[0]
    w = bs_cols.shape[1]
    head_dim = w // n_heads
    cpb = _pick(t // CHUNK, 4, 1)
    tr = cpb * CHUNK

    def body(zu_ref, zv_ref, do_ref, g_ref, b_ref, ws_ref, wst_ref, bs_ref,
             dzu_ref, dzv_ref, dws_ref, dbs_ref, dg_ref, dbeta_ref):
        @pl.when(pl.program_id(0) == 0)
        def _():
            dws_ref[...] = jnp.zeros_like(dws_ref)
            dbs_ref[...] = jnp.zeros_like(dbs_ref)
            dg_ref[...] = jnp.zeros_like(dg_ref)
            dbeta_ref[...] = jnp.zeros_like(dbeta_ref)

        col_head = lax.broadcasted_iota(jnp.int32, (CHUNK, w), 1) // head_dim
        for c in range(cpb):
            rows = pl.ds(c * CHUNK, CHUNK)
            zu, zv, do = zu_ref[rows, :], zv_ref[rows, :], do_ref[rows, :]
            xhat, rstd = _ln_stats(_gelu(zv))
            vn = (xhat * g_ref[...] + b_ref[...]).astype(BF16)
            s = _head_select([_dot(ws_ref[h], vn) for h in range(n_heads)], head_dim) + bs_ref[...]
            dzu_ref[rows, :] = (do * s * _gelu_grad(zu)).astype(BF16)
            ds = do * _gelu(zu)
            ds16 = ds.astype(BF16)
            dbs_ref[...] += ds
            for h in range(n_heads):
                dws_ref[h] += _dot(jnp.where(col_head == h, ds16, jnp.zeros_like(ds16)), vn, tb=True)
            dvn = _head_select([_dot(wst_ref[h], ds16) for h in range(n_heads)], head_dim)
            dg_ref[...] += _fold8(dvn * xhat)
            dbeta_ref[...] += _fold8(dvn)
            dgv = _ln_bwd(dvn, xhat, rstd, g_ref[...])
            dzv_ref[rows, :] = (dgv * _gelu_grad(zv)).astype(BF16)

    return pl.pallas_call(
        body, name=name,
        out_shape=(jax.ShapeDtypeStruct((t, w), BF16), jax.ShapeDtypeStruct((t, w), BF16),
                   jax.ShapeDtypeStruct(ws.shape, F32), jax.ShapeDtypeStruct((CHUNK, w), F32),
                   jax.ShapeDtypeStruct((8, w), F32), jax.ShapeDtypeStruct((8, w), F32)),
        grid=(t // tr,),
        in_specs=[_rows(tr, w, 1), _rows(tr, w, 2), _rows(tr, w), _whole((1, w)), _whole((1, w)),
                  _whole(ws.shape), _whole(ws.shape), _whole(bs_cols.shape)],
        out_specs=(_rows(tr, w), _rows(tr, w), _whole(ws.shape), _whole((CHUNK, w)),
                   _whole((8, w)), _whole((8, w))),
        compiler_params=_params())(proj, proj, dout, ln_g, ln_b, ws, ws_t, bs_cols)


def _merge_fwd(s5out, gm, proj, up_a, up_b, name):
    t, w = s5out.shape
    d = up_a.shape[1]
    assert d == 2 * w
    tr = _pick(t, 512, 8)

    def body(s_ref, gm_ref, ga0, ga1, gb0, gb1, ua_ref, ub_ref, m_ref, ya_ref, yb_ref):
        ya = _dot(s_ref[...], ua_ref[...])
        yb = _dot(gm_ref[...], ub_ref[...])
        ya_ref[...] = ya.astype(BF16)
        yb_ref[...] = yb.astype(BF16)
        for half, (ga, gb) in enumerate(((ga0, gb0), (ga1, gb1))):
            cols = slice(half * w, (half + 1) * w)
            m_ref[:, cols] = (_sigmoid(ga[...]) * ya[:, cols] + _sigmoid(gb[...]) * yb[:, cols]).astype(BF16)

    return pl.pallas_call(
        body, name=name,
        out_shape=(jax.ShapeDtypeStruct((t, d), BF16), jax.ShapeDtypeStruct((t, d), BF16),
                   jax.ShapeDtypeStruct((t, d), BF16)),
        grid=(t // tr,),
        in_specs=[_rows(tr, w), _rows(tr, w), _rows(tr, w, 3), _rows(tr, w, 4), _rows(tr, w, 5),
                  _rows(tr, w, 6), _whole(up_a.shape), _whole(up_b.shape)],
        out_specs=(_rows(tr, d), _rows(tr, d), _rows(tr, d)),
        compiler_params=_params())(s5out, gm, proj, proj, proj, proj, up_a, up_b)


def _merge_bwd(dm, ya, yb, s5out, gm, proj, up_a, up_b, name):
    t, d = dm.shape
    w = d // 2
    tr = _pick(t, 512, 8)

    def body(dm_ref, ya_ref, yb_ref, s_ref, gm_ref, ga0, ga1, gb0, gb1, ua_ref, ub_ref,
             dga_ref, dgb_ref, ds_ref, dgm_ref, dua_ref, dub_ref):
        dm_v, ya, yb = dm_ref[...], ya_ref[...].astype(F32), yb_ref[...].astype(F32)
        dya, dyb = [], []
        for half, (ga, gb) in enumerate(((ga0, gb0), (ga1, gb1))):
            cols = slice(half * w, (half + 1) * w)
            sa, sb = _sigmoid(ga[...]), _sigmoid(gb[...])
            dmh = dm_v[:, cols]
            dga_ref[:, cols] = (dmh * ya[:, cols] * sa * (1.0 - sa)).astype(BF16)
            dgb_ref[:, cols] = (dmh * yb[:, cols] * sb * (1.0 - sb)).astype(BF16)
            dya.append((dmh * sa).astype(BF16))
            dyb.append((dmh * sb).astype(BF16))
        dya = jnp.concatenate(dya, axis=1)
        dyb = jnp.concatenate(dyb, axis=1)
        ds_ref[...] = _dot(dya, ua_ref[...], tb=True)
        dgm_ref[...] = _dot(dyb, ub_ref[...], tb=True)

        @pl.when(pl.program_id(0) == 0)
        def _():
            dua_ref[...] = jnp.zeros_like(dua_ref)
            dub_ref[...] = jnp.zeros_like(dub_ref)

        dua_ref[...] += _dot(s_ref[...], dya, ta=True)
        dub_ref[...] += _dot(gm_ref[...], dyb, ta=True)

    return pl.pallas_call(
        body, name=name,
        out_shape=(jax.ShapeDtypeStruct((t, d), BF16), jax.ShapeDtypeStruct((t, d), BF16),
                   jax.ShapeDtypeStruct((t, w), F32), jax.ShapeDtypeStruct((t, w), F32),
                   jax.ShapeDtypeStruct(up_a.shape, F32), jax.ShapeDtypeStruct(up_b.shape, F32)),
        grid=(t // tr,),
        in_specs=[_rows(tr, d), _rows(tr, d), _rows(tr, d), _rows(tr, w), _rows(tr, w),
                  _rows(tr, w, 3), _rows(tr, w, 4), _rows(tr, w, 5), _rows(tr, w, 6),
                  _whole(up_a.shape), _whole(up_b.shape)],
        out_specs=(_rows(tr, d), _rows(tr, d), _rows(tr, w), _rows(tr, w), _whole(up_a.shape),
                   _whole(up_b.shape)),
        compiler_params=_params())(dm, ya, yb, s5out, gm, proj, proj, proj, proj, up_a, up_b)


def _head(x3, r3, ln_g, p, target, w_gate, w_proj, out_scale, name):
    t, d = x3.shape
    pd = p.shape[1]
    tr = _pick(t, 512, 8)
    nb = t // tr

    def body(x_ref, r_ref, g_ref, p_ref, t_ref, wg_ref, wp_ref,
             loss_ref, dr_ref, drs_ref, dwg_ref, dwp_ref, dg_ref, dbeta_ref):
        xv = x_ref[...]
        sg = _sigmoid(_dot(xv, wg_ref[...]))
        pp = _dot(p_ref[...], wp_ref[...])
        err = xv + sg * pp - t_ref[...]
        loss_ref[...] = jnp.full((8, 128), 0.5 * jnp.sum(jnp.mean(err * err, axis=-1)), F32)
        dout = err * (1.0 / d)
        dgpre = dout * pp * sg * (1.0 - sg)
        dpp = dout * sg
        dx = dout + _dot(dgpre, wg_ref[...], tb=True)
        xhat, rstd = _ln_stats(r_ref[...])
        dr = _ln_bwd(dx, xhat, rstd, g_ref[...])
        dr_ref[...] = dr
        drs_ref[...] = (out_scale * dr).astype(BF16)

        @pl.when(pl.program_id(0) == 0)
        def _():
            for ref in (dwg_ref, dwp_ref, dg_ref, dbeta_ref):
                ref[...] = jnp.zeros_like(ref)

        dwg_ref[...] += _dot(xv, dgpre, ta=True)
        dwp_ref[...] += _dot(p_ref[...], dpp, ta=True)
        dg_ref[...] += _fold8(dx * xhat)
        dbeta_ref[...] += _fold8(dx)

    return pl.pallas_call(
        body, name=name,
        out_shape=(jax.ShapeDtypeStruct((nb * 8, 128), F32), jax.ShapeDtypeStruct((t, d), F32),
                   jax.ShapeDtypeStruct((t, d), BF16), jax.ShapeDtypeStruct((d, d), F32),
                   jax.ShapeDtypeStruct((pd, d), F32), jax.ShapeDtypeStruct((8, d), F32),
                   jax.ShapeDtypeStruct((8, d), F32)),
        grid=(nb,),
        in_specs=[_rows(tr, d), _rows(tr, d), _whole((1, d)), _rows(tr, pd), _rows(tr, d), _whole((d, d)),
                  _whole((pd, d))],
        out_specs=(pl.BlockSpec((8, 128), lambda i: (i, 0)), _rows(tr, d), _rows(tr, d), _whole((d, d)),
                   _whole((pd, d)), _whole((8, d)), _whole((8, d))),
        compiler_params=_params())(x3, r3, ln_g, p, target, w_gate, w_proj)


_HBM = pl.BlockSpec(memory_space=pltpu.HBM)


_SEM = pl.BlockSpec(memory_space=pltpu.SEMAPHORE)
_ANY = pl.BlockSpec(memory_space=pl.ANY)
_DATAFLOW = pltpu.SideEffectType.DATAFLOW_SIDE_EFFECTING


def _peer(k, x, y, c):
    return (1 - x if k & 4 else x, 1 - y if k & 2 else y, 1 - c if k & 1 else c)


def _exchange_start(sends, after, name):
    n = len(sends)
    lands = [lax.empty((N_DEV,) + s.shape[1:], s.dtype) for s in sends]

    def body(*refs):
        s_refs, l_refs = refs[:n], refs[n:2 * n]
        send_sems, recv_sems, local_sems, token = refs[2 * n + 1], refs[2 * n + 2], refs[2 * n + 3], refs[-1]
        x, y, c = lax.axis_index("x"), lax.axis_index("y"), lax.axis_index("c")
        me = 4 * x + 2 * y + c
        for a in range(n):
            same = sends[a].shape[0] == 1
            pltpu.make_async_copy(s_refs[a].at[0 if same else me], l_refs[a].at[me], local_sems.at[a]).start()
            for k in range(1, N_DEV):
                px, py, pc = _peer(k, x, y, c)
                pltpu.make_async_remote_copy(
                    src_ref=s_refs[a].at[0 if same else 4 * px + 2 * py + pc], dst_ref=l_refs[a].at[me],
                    send_sem=send_sems.at[7 * a + k - 1], recv_sem=recv_sems.at[7 * a + k - 1],
                    device_id=(px, py, pc), device_id_type=pl.DeviceIdType.MESH).start()
        token[...] = jnp.zeros_like(token)

    outs = pl.pallas_call(
        body, name=name,
        out_shape=(pltpu.SemaphoreType.DMA((7 * n,)), pltpu.SemaphoreType.DMA((7 * n,)),
                   pltpu.SemaphoreType.DMA((n,)),
                   *[pltpu.HBM(s.shape, s.dtype) for s in sends],
                   *[pltpu.HBM(l.shape, l.dtype) for l in lands],
                   jax.ShapeDtypeStruct((8, 128), F32)),
        in_specs=[_HBM] * (2 * n) + [_ANY],
        out_specs=(_SEM, _SEM, _SEM, *([_HBM] * (2 * n)), pl.BlockSpec(memory_space=pltpu.VMEM)),
        input_output_aliases={i: 3 + i for i in range(2 * n)},
        compiler_params=pltpu.CompilerParams(has_side_effects=_DATAFLOW),
    )(*[pltpu.with_memory_space_constraint(v, pltpu.HBM) for v in list(sends) + lands], after)
    return (outs[0], outs[1], outs[2], list(outs[3:3 + n]), list(outs[3 + n:3 + 2 * n])), outs[-1]


def _exchange_wait(handle, after, name):
    send_sems, recv_sems, local_sems, sends, lands = handle
    n = len(sends)

    def body(*refs):
        s_refs, l_refs = refs[:n], refs[n:2 * n]
        send_sems, recv_sems, local_sems = refs[2 * n], refs[2 * n + 1], refs[2 * n + 2]
        x, y, c = lax.axis_index("x"), lax.axis_index("y"), lax.axis_index("c")
        for a in range(n):
            pltpu.make_async_copy(s_refs[a].at[0], l_refs[a].at[0], local_sems.at[a]).wait()
            for k in range(1, N_DEV):
                copy = pltpu.make_async_remote_copy(
                    src_ref=s_refs[a].at[0], dst_ref=l_refs[a].at[0],
                    send_sem=send_sems.at[7 * a + k - 1], recv_sem=recv_sems.at[7 * a + k - 1],
                    device_id=_peer(k, x, y, c), device_id_type=pl.DeviceIdType.MESH)
                copy.wait_send()
                copy.wait_recv()

    outs = pl.pallas_call(
        body, name=name,
        out_shape=(*[pltpu.HBM(s.shape, s.dtype) for s in sends], *[pltpu.HBM(l.shape, l.dtype) for l in lands]),
        in_specs=[_HBM] * (2 * n) + [_SEM, _SEM, _SEM] + [_ANY] * len(after),
        out_specs=tuple([_HBM] * (2 * n)),
        input_output_aliases={i: i for i in range(2 * n)},
        compiler_params=pltpu.CompilerParams(has_side_effects=_DATAFLOW),
    )(*sends, *lands, send_sems, recv_sems, local_sems, *after)
    return list(outs[n:])


def _chips_of(x, y):
    return [(1 - x, y), (x, 1 - y), (1 - x, 1 - y)]


def _gather2_start(shards, after, name):
    n = len(shards)
    lands = [lax.empty((N_DEV,) + s.shape, s.dtype) for s in shards]

    def body(*refs):
        x_refs, l_refs = refs[:n], refs[n:2 * n]
        send_sems, recv_sems, local_sems, token = refs[2 * n + 1], refs[2 * n + 2], refs[2 * n + 3], refs[-1]
        x, y, c = lax.axis_index("x"), lax.axis_index("y"), lax.axis_index("c")
        me = 4 * x + 2 * y + c
        peers = [(x, y, 1 - c)] + [(*chip, c) for chip in _chips_of(x, y)]
        for a in range(n):
            pltpu.make_async_copy(x_refs[a], l_refs[a].at[me], local_sems.at[a]).start()
            for k, peer in enumerate(peers):
                pltpu.make_async_remote_copy(
                    src_ref=x_refs[a], dst_ref=l_refs[a].at[me],
                    send_sem=send_sems.at[4 * a + k], recv_sem=recv_sems.at[4 * a + k],
                    device_id=peer, device_id_type=pl.DeviceIdType.MESH).start()
        token[...] = jnp.zeros_like(token)

    outs = pl.pallas_call(
        body, name=name,
        out_shape=(pltpu.SemaphoreType.DMA((4 * n,)), pltpu.SemaphoreType.DMA((4 * n,)),
                   pltpu.SemaphoreType.DMA((n,)),
                   *[pltpu.HBM(s.shape, s.dtype) for s in shards],
                   *[pltpu.HBM(l.shape, l.dtype) for l in lands],
                   jax.ShapeDtypeStruct((8, 128), F32)),
        in_specs=[_HBM] * (2 * n) + [_ANY],
        out_specs=(_SEM, _SEM, _SEM, *([_HBM] * (2 * n)), pl.BlockSpec(memory_space=pltpu.VMEM)),
        input_output_aliases={i: 3 + i for i in range(2 * n)},
        compiler_params=pltpu.CompilerParams(has_side_effects=_DATAFLOW),
    )(*[pltpu.with_memory_space_constraint(v, pltpu.HBM) for v in list(shards) + lands], after)
    return (outs[0], outs[1], outs[2], list(outs[3:3 + n]), list(outs[3 + n:3 + 2 * n])), outs[-1]


def _gather2_forward(handle, after, name):
    send_sems, recv_sems, local_sems, shards, lands = handle
    n = len(shards)

    def body(*refs):
        x_refs, l_refs = refs[:n], refs[n:2 * n]
        send_sems, recv_sems = refs[2 * n], refs[2 * n + 1]
        out0 = 2 * n + 2 + len(after)
        fwd_send, fwd_recv, token = refs[out0], refs[out0 + 1], refs[-1]
        x, y, c = lax.axis_index("x"), lax.axis_index("y"), lax.axis_index("c")
        for a in range(n):
            for j, chip in enumerate(_chips_of(x, y)):
                block = l_refs[a].at[4 * chip[0] + 2 * chip[1] + c]
                pltpu.make_async_remote_copy(
                    src_ref=x_refs[a], dst_ref=block, send_sem=send_sems.at[4 * a + 1 + j],
                    recv_sem=recv_sems.at[4 * a + 1 + j], device_id=(*chip, c),
                    device_id_type=pl.DeviceIdType.MESH).wait_recv()
                pltpu.make_async_remote_copy(
                    src_ref=block, dst_ref=block, send_sem=fwd_send.at[3 * a + j], recv_sem=fwd_recv.at[3 * a + j],
                    device_id=(x, y, 1 - c), device_id_type=pl.DeviceIdType.MESH).start()
        token[...] = jnp.zeros_like(token)

    outs = pl.pallas_call(
        body, name=name,
        out_shape=(pltpu.SemaphoreType.DMA((3 * n,)), pltpu.SemaphoreType.DMA((3 * n,)),
                   *[pltpu.HBM(s.shape, s.dtype) for s in shards], *[pltpu.HBM(l.shape, l.dtype) for l in lands],
                   jax.ShapeDtypeStruct((8, 128), F32)),
        in_specs=[_HBM] * (2 * n) + [_SEM, _SEM] + [_ANY] * len(after),
        out_specs=(_SEM, _SEM, *([_HBM] * (2 * n)), pl.BlockSpec(memory_space=pltpu.VMEM)),
        input_output_aliases={i: 2 + i for i in range(2 * n)},
        compiler_params=pltpu.CompilerParams(has_side_effects=_DATAFLOW),
    )(*shards, *lands, send_sems, recv_sems, *after)
    handle = (send_sems, recv_sems, local_sems, outs[0], outs[1], list(outs[2:2 + n]), list(outs[2 + n:2 + 2 * n]))
    return handle, outs[-1]


def _gather2_wait(handle, after, name):
    send_sems, recv_sems, local_sems, fwd_send, fwd_recv, shards, lands = handle
    n = len(shards)

    def body(*refs):
        x_refs, l_refs = refs[:n], refs[n:2 * n]
        send_sems, recv_sems, local_sems, fwd_send, fwd_recv = refs[2 * n:2 * n + 5]
        x, y, c = lax.axis_index("x"), lax.axis_index("y"), lax.axis_index("c")
        sibling = (x, y, 1 - c)

        def copy(a, send_sem, recv_sem):
            return pltpu.make_async_remote_copy(
                src_ref=x_refs[a], dst_ref=l_refs[a].at[0], send_sem=send_sem, recv_sem=recv_sem,
                device_id=sibling, device_id_type=pl.DeviceIdType.MESH)

        for a in range(n):
            pltpu.make_async_copy(x_refs[a], l_refs[a].at[0], local_sems.at[a]).wait()
            for k in range(4):
                copy(a, send_sems.at[4 * a + k], recv_sems.at[4 * a + k]).wait_send()
            copy(a, send_sems.at[4 * a], recv_sems.at[4 * a]).wait_recv()
            for j in range(3):
                passed = copy(a, fwd_send.at[3 * a + j], fwd_recv.at[3 * a + j])
                passed.wait_send()
                passed.wait_recv()

    outs = pl.pallas_call(
        body, name=name,
        out_shape=(*[pltpu.HBM(s.shape, s.dtype) for s in shards], *[pltpu.HBM(l.shape, l.dtype) for l in lands]),
        in_specs=[_HBM] * (2 * n) + [_SEM] * 5 + [_ANY] * len(after),
        out_specs=tuple([_HBM] * (2 * n)),
        input_output_aliases={i: i for i in range(2 * n)},
        compiler_params=pltpu.CompilerParams(has_side_effects=_DATAFLOW),
    )(*shards, *lands, send_sems, recv_sems, local_sems, fwd_send, fwd_recv, *after)
    return list(outs[n:])


def _adamw(recv, w, m, v, name):
    n, rows, width = recv.shape
    tr = _pick(rows, max(8, ADAM_BLOCK_BYTES // (n * width * recv.dtype.itemsize)), 8)
    c_m = 1.0 - ADAM_B1 ** ADAM_STEP
    c_v = 1.0 - ADAM_B2 ** ADAM_STEP

    def body(r_ref, w_ref, m_ref, v_ref, g_ref, d_ref, nm_ref, nv_ref):
        g = r_ref[0].astype(F32)
        for i in range(1, n):
            g = g + r_ref[i].astype(F32)
        m2 = ADAM_B1 * m_ref[...] + (1.0 - ADAM_B1) * g
        v2 = ADAM_B2 * v_ref[...] + (1.0 - ADAM_B2) * (g * g)
        m_hat = m2 / c_m
        v_hat = v2 / c_v
        g_ref[...] = g
        d_ref[...] = -ADAM_LR * (m_hat / (jnp.sqrt(v_hat) + ADAM_EPS) + ADAM_WD * w_ref[...])
        nm_ref[...] = m2
        nv_ref[...] = v2

    blk = _rows(tr, width)
    shp = jax.ShapeDtypeStruct((rows, width), F32)
    return pl.pallas_call(
        body, name=name, out_shape=(shp, shp, shp, shp), grid=(rows // tr,),
        in_specs=[pl.BlockSpec((n, tr, width), lambda i: (0, i, 0)), blk, blk, blk],
        out_specs=(blk, blk, blk, blk), compiler_params=_params())(recv, w, m, v)


def _join_cols(gathered):
    n, r, c = gathered.shape
    return gathered.transpose(1, 0, 2).reshape(r, n * c)


def _split_cols(full):
    r, c = full.shape
    return full.reshape(r, N_DEV, c // N_DEV).transpose(1, 0, 2)


def _adamw_small(items, name):
    n = len(items)
    c_m = 1.0 - ADAM_B1 ** ADAM_STEP
    c_v = 1.0 - ADAM_B2 ** ADAM_STEP

    def body(*refs):
        ins, outs = refs[:4 * n], refs[4 * n:]
        for k in range(n):
            r_ref, w_ref, m_ref, v_ref = ins[4 * k:4 * k + 4]
            g = r_ref[0].astype(F32)
            for i in range(1, N_DEV):
                g = g + r_ref[i].astype(F32)
            m2 = ADAM_B1 * m_ref[...] + (1.0 - ADAM_B1) * g
            v2 = ADAM_B2 * v_ref[...] + (1.0 - ADAM_B2) * (g * g)
            outs[4 * k][...] = g
            outs[4 * k + 1][...] = -ADAM_LR * ((m2 / c_m) / (jnp.sqrt(v2 / c_v) + ADAM_EPS) + ADAM_WD * w_ref[...])
            outs[4 * k + 2][...] = m2
            outs[4 * k + 3][...] = v2

    vmem = pl.BlockSpec(memory_space=pltpu.VMEM)
    flat = pl.pallas_call(
        body, name=name,
        out_shape=tuple(jax.ShapeDtypeStruct(w.shape, F32) for _, w, _, _ in items for _ in range(4)),
        in_specs=[vmem] * (4 * n), out_specs=tuple([vmem] * (4 * n)),
        compiler_params=pltpu.CompilerParams(vmem_limit_bytes=VMEM_LIMIT_BYTES),
    )(*[a for item in items for a in item])
    return [tuple(flat[4 * k:4 * k + 4]) for k in range(n)]


def _discretise(lam_re, lam_im, log_dt, b_re, b_im):
    dt = jnp.exp(log_dt)
    mag = jnp.exp(lam_re * dt)
    ab_re = mag * jnp.cos(lam_im * dt)
    ab_im = mag * jnp.sin(lam_im * dt)
    nr = ab_re - 1.0
    ni = ab_im
    den = lam_re * lam_re + lam_im * lam_im
    coef_re = (nr * lam_re + ni * lam_im) / den
    coef_im = (ni * lam_re - nr * lam_im) / den
    return ab_re, ab_im, coef_re * b_re - coef_im * b_im, coef_re * b_im + coef_im * b_re


def _s5_discretise(operands, name, token):
    def body(*refs):
        for out_ref, v in zip(refs[6:], _discretise(*[r[...] for r in refs[:5]])):
            out_ref[...] = v

    vmem = pl.BlockSpec(memory_space=pltpu.VMEM)
    shape = jax.ShapeDtypeStruct(operands[0].shape, F32)
    return pl.pallas_call(body, name=name, out_shape=(shape,) * 4, in_specs=[vmem] * 6,
                          out_specs=(vmem,) * 4)(*operands, token)


def _s5_discretise_bwd(operands, cotangents, name):
    def body(*refs):
        _, vjp = jax.vjp(_discretise, *[r[...] for r in refs[:5]])
        for out_ref, v in zip(refs[9:], vjp(tuple(r[...] for r in refs[5:9]))):
            out_ref[...] = v

    vmem = pl.BlockSpec(memory_space=pltpu.VMEM)
    shape = jax.ShapeDtypeStruct(operands[0].shape, F32)
    return pl.pallas_call(body, name=name, out_shape=(shape,) * 5, in_specs=[vmem] * 9,
                          out_specs=(vmem,) * 5)(*operands, *cotangents)


def _same_group(gl):
    return jnp.eye(gl, dtype=bool)[None, :, None, :, None]


def _super_groups_in(bb, gl):
    g, p, i = bb.shape
    blocks = bb.reshape(g // gl, gl, p, i).transpose(0, 1, 3, 2)[:, :, :, None, :]
    return jnp.where(_same_group(gl), blocks, 0.0).reshape(g // gl, gl * i, gl * p)


def _super_groups_in_take(dense, gl, p, i):
    n_sg = dense.shape[0]
    blocks = jnp.where(_same_group(gl), dense.reshape(n_sg, gl, i, gl, p), 0.0).sum(axis=3)
    return blocks.transpose(0, 1, 3, 2).reshape(n_sg * gl, p, i)


def _super_groups_out(cc, gl):
    g, i, p = cc.shape
    blocks = cc.reshape(g // gl, gl, i, p).transpose(0, 1, 3, 2)[:, :, :, None, :]
    return jnp.where(_same_group(gl), blocks, 0.0).reshape(g // gl, gl * p, gl * i)


def _super_groups_out_take(dense, gl, i, p):
    n_sg = dense.shape[0]
    blocks = jnp.where(_same_group(gl), dense.reshape(n_sg, gl, p, gl, i), 0.0).sum(axis=3)
    return blocks.transpose(0, 1, 3, 2).reshape(n_sg * gl, i, p)


def _perm_rows(z, n_seq):
    t, w = z.shape
    steps = t // n_seq // N_SEG
    return z.reshape(n_seq, N_SEG, steps, w).transpose(0, 2, 1, 3).reshape(t, w)


def _unperm_rows(z, n_seq):
    t, w = z.shape
    steps = t // n_seq // N_SEG
    return z.reshape(n_seq, steps, N_SEG, w).transpose(0, 2, 1, 3).reshape(t, w)


def kernel(*args):
    assert len(args) == len(INPUT_NAMES)
    inp = dict(zip(INPUT_NAMES, args))
    depth = inp["ffn1_w_in"].shape[0]
    assert depth == 1
    alpha = (2.0 * depth) ** 0.25

    n_seq, seq, d_model = inp["x"].shape
    t = n_seq * seq
    x = inp["x"].reshape(t, d_model)
    target = inp["loss_target"].reshape(t, d_model)
    wts = {n: inp[n][0] if n in SHARDED else inp[n] for n in WEIGHTS}
    mom = {n: inp["m_" + n][0] if n in SHARDED else inp["m_" + n] for n in WEIGHTS}
    var = {n: inp["v_" + n][0] if n in SHARDED else inp["v_" + n] for n in WEIGHTS}

    full = {}

    def place(n, g):
        if n in ("ffn1_w_in", "ffn2_w_in"):
            full[n] = g.reshape((2, N_DEV // 2) + g.shape[1:])
        elif n in ("ffn1_w_out", "ffn2_w_out"):
            full[n] = g.reshape(N_DEV // 2, 2 * g.shape[1], g.shape[2])
        elif n in COL_SHARDED:
            full[n] = _join_cols(g)
        else:
            full[n] = g.reshape(N_DEV * g.shape[1], g.shape[2])

    w16 = dict(zip(GATHER_FIRST, _to_bf16([wts[n] for n in GATHER_FIRST], "cast_ffn1")))
    gather_first, gather_first_token = _gather2_start([w16[n] for n in GATHER_FIRST], w16[GATHER_FIRST[0]],
                                                      "gather_first_start")
    later = tuple(n for n in SHARDED if n not in GATHER_FIRST)
    casts = _to_bf16([wts[n] for n in later] + [x, inp["p"][0].reshape(t, -1)], "cast_rest",
                     token=gather_first_token)
    w16.update(zip(later, casts))
    x16, p16 = casts[-2:]

    def gather_start(names, after, name):
        return _exchange_start([w16[n][None] for n in names], after, name)

    def gather_wait(names, handle, after, name):
        for n, g in zip(names, _exchange_wait(handle, after, name)):
            place(n, g)

    def gather2_wait(names, handle, after, name):
        for n, g in zip(names, _gather2_wait(handle, after, name)):
            place(n, g)


    def row(v):
        return v.reshape(1, -1)

    _, n_groups, n_state = wts["ssm_lambda_re"].shape
    n_ch = wts["ssm_b_re"].shape[3]
    disc_in = (jnp.repeat(wts["ssm_lambda_re"][0], n_ch, axis=1), jnp.repeat(wts["ssm_lambda_im"][0], n_ch, axis=1),
               jnp.broadcast_to(wts["ssm_log_dt"][0][:, None], (n_groups, n_state * n_ch)),
               wts["ssm_b_re"][0].reshape(n_groups, -1), wts["ssm_b_im"][0].reshape(n_groups, -1))
    ab_re, ab_im, bb_re, bb_im = _s5_discretise(disc_in, "s5_discretise", gather_first_token)
    a_re, a_im = row(ab_re[:, ::n_ch]), row(ab_im[:, ::n_ch])
    bb_re, bb_im = bb_re.reshape(n_groups, n_state, n_ch), bb_im.reshape(n_groups, n_state, n_ch)
    gl = S5_CHANNELS_PER_STEP // n_ch
    b_sg_re = _super_groups_in(bb_re, gl).astype(BF16)
    b_sg_im = _super_groups_in(bb_im, gl).astype(BF16)
    c_sg_re = _super_groups_out(wts["ssm_c_re"][0], gl).astype(BF16)
    c_sg_im = _super_groups_out(-wts["ssm_c_im"][0], gl).astype(BF16)

    ws = wts["gmlp_w_s"][0]
    n_heads = ws.shape[0]
    d_gmlp = wts["gmlp_ln_g"].shape[1]
    causal = jnp.tril(jnp.ones((CHUNK, CHUNK), dtype=bool))
    ws_masked = jnp.where(causal[None], ws, 0.0).astype(BF16)
    ws_masked_t = ws_masked.transpose(0, 2, 1)
    bs_cols = jnp.repeat(wts["gmlp_b_s"][0].T, d_gmlp // n_heads, axis=1)
    d_ssm = n_groups * n_ch
    assert d_ssm == d_gmlp and d_model == 2 * d_ssm

    prepared = [x16, b_sg_re, b_sg_im, c_sg_re, c_sg_im, ws_masked_t, bs_cols]
    gather_first, _ = _gather2_forward(gather_first, prepared, "gather_first_forward")
    gather2_wait(GATHER_FIRST, gather_first, prepared, "gather_first_wait")
    gather_early, gather_early_token = _gather2_start([w16[n] for n in GATHER_EARLY], full["ffn1_w_in"],
                                                      "gather_early_start")
    h1, a1 = _ffn_in(x16, full["ffn1_w_in"], "ffn1_in", token=gather_early_token)
    gather_early, _ = _gather2_forward(gather_early, [a1], "gather_early_forward")
    gather2_wait(GATHER_EARLY, gather_early, [a1], "gather_early_wait")
    r1, x1, x1_16 = _ffn_out_ln(a1, full["ffn1_w_out"], x, row(wts["ln1_g"]), row(wts["ln1_b"]), alpha,
                                "ffn1_out_ln")

    gather_mid, token = gather_start(GATHER_MID, x1_16, "gather_mid_start")
    gather_last, token = _gather2_start([w16[n] for n in GATHER_LAST], token, "gather_last_start")
    proj = _mm(x1_16, full["mix_w_in"], name="mix_in", token=token)
    za_p = _perm_rows(proj[:, :d_ssm], n_seq)
    za_p16 = za_p.astype(BF16)
    h_re, h_im, yssm = _s5_fwd(za_p16, b_sg_re, b_sg_im, a_re, a_im, c_sg_re, c_sg_im, n_seq, "s5_fwd")
    gather_wait(GATHER_MID, gather_mid, [yssm], "gather_mid_wait")
    s5out_p = _s5_post_fwd(yssm, za_p, row(wts["ssm_d"]), full["ssm_glu_w"], row(wts["ssm_glu_b"]),
                           "s5_post")
    s5out = _unperm_rows(s5out_p, n_seq)
    gm = _gmlp_fwd(proj, row(wts["gmlp_ln_g"]), row(wts["gmlp_ln_b"]), ws_masked, bs_cols, "gmlp")
    m_mix, y_a, y_b = _merge_fwd(s5out, gm, proj, full["up_a"], full["up_b"], "merge")
    gather_last, token = _gather2_forward(gather_last, [m_mix], "gather_last_forward")
    r2, x2, x2_16 = _ffn_out_ln(m_mix[None], full["mix_w_out"][None], x1, row(wts["ln2_g"]), row(wts["ln2_b"]),
                                alpha, "mix_out_ln2", scale=1.0, token=token)

    gather2_wait(GATHER_LAST, gather_last, [x2_16], "gather_last_wait")
    h2, a2 = _ffn_in(x2_16, full["ffn2_w_in"], "ffn2_in")
    r3, x3, _ = _ffn_out_ln(a2, full["ffn2_w_out"], x2, row(wts["ln3_g"]), row(wts["ln3_b"]), alpha,
                            "ffn2_out_ln")

    small = {}
    send = {}

    def lane_dense(n, v):
        return v.reshape(v.shape[:2] + (-1,)) if n in ("ssm_b_re", "ssm_b_im") else v

    def on_wire(n, g):
        g = lane_dense(n, g)
        return (g.astype(BF16) if n in SMALL_BF16 else g)[None]
    loss_parts, dr3, dr3_h, dw_gate, dw_proj, dg, db = _head(
        x3, r3, row(wts["ln3_g"]), p16, target, full["ple_w_gate"], full["ple_w_proj"], 0.5, "head")
    loss_mine = jnp.full((1, 1, 8, 128), jnp.sum(loss_parts[::8, 0]), F32)
    send["ple_w_gate"] = dw_gate.astype(BF16).reshape(N_DEV, -1, d_model)
    send["ple_w_proj"] = _split_cols(dw_proj.astype(BF16))
    small["ln3_g"], small["ln3_b"] = dg.sum(0, keepdims=True), db.sum(0, keepdims=True)
    dh2 = _ffn_out_dx(dr3_h, full["ffn2_w_out"], h2, "ffn2_out_dx")
    dw = _ffn_out_dw(a2, dr3_h, "ffn2_out_dw")
    send["ffn2_w_out"] = dw.reshape(N_DEV, -1, d_model)
    dw = _ffn_in_dw(x2_16, dh2, "ffn2_in_dw")
    send["ffn2_w_in"] = dw.reshape((N_DEV,) + dw.shape[2:])
    group1 = ("ffn2_w_in", "ffn2_w_out", "ple_w_gate", "ple_w_proj")
    exchange1, token = _exchange_start(
        [send[n] for n in group1] + [on_wire(n, small[n]) for n in SMALL_HEAD] + [loss_mine], dh2,
        "grad_exchange1_start")
    dr2, dr2_h, dg, db = _ffn_in_dx_ln(dh2, full["ffn2_w_in"], r2, dr3, alpha, row(wts["ln2_g"]), 1.0,
                                       "ffn2_in_dx_ln2_bwd", token=token)
    small["ln2_g"], small["ln2_b"] = dg.sum(0, keepdims=True), db.sum(0, keepdims=True)
    dm = _mm(dr2_h, full["mix_w_out"], tb=True, name="mix_out_dx")
    send["mix_w_out"] = _mm(m_mix, dr2_h, ta=True, name="mix_out_dw", out_dtype=BF16).reshape(
        N_DEV, -1, d_model)
    dga, dgb, ds5out, dgm, dw_a, dw_b = _merge_bwd(
        dm, y_a, y_b, s5out, gm, proj, full["up_a"], full["up_b"], "merge_bwd")
    send["up_a"] = _split_cols(dw_a.astype(BF16))
    send["up_b"] = _split_cols(dw_b.astype(BF16))

    dzu, dzv, dws, dbs_cols, dg, db = _gmlp_bwd(
        proj, dgm, row(wts["gmlp_ln_g"]), row(wts["gmlp_ln_b"]), ws_masked, ws_masked_t, bs_cols,
        "gmlp_bwd")
    small["gmlp_ln_g"], small["gmlp_ln_b"] = dg.sum(0, keepdims=True), db.sum(0, keepdims=True)
    small["gmlp_w_s"] = jnp.where(causal[None], dws, 0.0)[None]
    small["gmlp_b_s"] = dbs_cols.reshape(CHUNK, n_heads, -1).sum(-1).T[None]

    ds5out_p = _perm_rows(ds5out, n_seq)
    dy_p, dza_skip, dw_glu, dd, dgb_glu = _s5_post_bwd(
        yssm, za_p, ds5out_p, row(wts["ssm_d"]), full["ssm_glu_w"], row(wts["ssm_glu_b"]),
        "s5_post_bwd")
    send["ssm_glu_w"] = dw_glu.astype(BF16).reshape(N_DEV, -1, d_ssm)
    small["ssm_d"], small["ssm_glu_b"] = dd.sum(0, keepdims=True), dgb_glu.sum(0, keepdims=True)
    dza_p, dbb_re, dbb_im, dc_re, dc_im, da_re, da_im = _s5_bwd(
        dy_p, dza_skip, h_re, h_im, za_p16, b_sg_re, b_sg_im, a_re, a_im, c_sg_re, c_sg_im, n_seq, "s5_bwd")
    small["ssm_c_re"] = _super_groups_out_take(dc_re, gl, n_ch, n_state)[None]
    small["ssm_c_im"] = -_super_groups_out_take(dc_im, gl, n_ch, n_state)[None]
    dbb_re = _super_groups_in_take(dbb_re, gl, n_state, n_ch)
    dbb_im = _super_groups_in_take(dbb_im, gl, n_state, n_ch)
    def first_channel(v):
        placed = jnp.pad(v.sum(0).reshape(n_groups, n_state, 1), ((0, 0), (0, 0), (0, n_ch - 1)))
        return placed.reshape(n_groups, -1)

    cotangents = (first_channel(da_re), first_channel(da_im), dbb_re.reshape(n_groups, -1),
                  dbb_im.reshape(n_groups, -1))
    d_lre, d_lim, d_ldt, d_bre, d_bim = _s5_discretise_bwd(disc_in, cotangents, "s5_discretise_bwd")
    per_state = (n_groups, n_state, n_ch)
    small["ssm_lambda_re"] = d_lre.reshape(per_state).sum(-1)[None]
    small["ssm_lambda_im"] = d_lim.reshape(per_state).sum(-1)[None]
    small["ssm_log_dt"] = d_ldt.sum(-1)[None]
    small["ssm_b_re"] = d_bre.reshape((1,) + per_state)
    small["ssm_b_im"] = d_bim.reshape((1,) + per_state)

    dproj = jnp.concatenate([_unperm_rows(dza_p, n_seq), dzu, dzv, dga, dgb], axis=1)
    group2 = ("mix_w_out", "up_a", "up_b", "ssm_glu_w")
    exchange2, token = _exchange_start(
        [send[n] for n in group2] + [on_wire(n, small[n]) for n in SMALL_MID], dproj, "grad_exchange2_start")
    send["mix_w_in"] = _split_cols(_mm(x1_16, dproj, ta=True, name="mix_in_dw", out_dtype=BF16, token=token))
    exchange3, token = _exchange_start([send["mix_w_in"]], dproj, "grad_exchange3_start")
    dr1, dr1_h, dg, db = _mm_ln_bwd(dproj, full["mix_w_in"], r1, dr2, alpha, row(wts["ln1_g"]), 0.5,
                                    "mix_in_dx_ln1_bwd", token=token)
    small["ln1_g"], small["ln1_b"] = dg.sum(0, keepdims=True), db.sum(0, keepdims=True)
    dw = _ffn_out_dw(a1, dr1_h, "ffn1_out_dw")
    send["ffn1_w_out"] = dw.reshape(N_DEV, -1, d_model)
    exchange4, token = _exchange_start([send["ffn1_w_out"]] + [on_wire(n, small[n]) for n in SMALL_LATE], dr1_h,
                                       "grad_exchange4_start")
    dh1 = _ffn_out_dx(dr1_h, full["ffn1_w_out"], h1, "ffn1_out_dx", token=token)
    dw = _ffn_in_dw(x16, dh1, "ffn1_in_dw")
    send["ffn1_w_in"] = dw.reshape((N_DEV,) + dw.shape[2:])
    exchange5, token = _exchange_start([send["ffn1_w_in"]], dh1, "grad_exchange5_start")
    grad_x = _ffn_in_dx(dh1, full["ffn1_w_in"], dr1, alpha, "ffn1_in_dx", token=token)

    results = {}

    def update(names, recv, prefix):
        for n, r in zip(names, recv):
            results[n] = _adamw(r, wts[n], mom[n], var[n], prefix + n)

    def update_small(names, recv, name):
        items = [(r, lane_dense(n, wts[n]), lane_dense(n, mom[n]), lane_dense(n, var[n])) for n, r in zip(names, recv)]
        for n, outs in zip(names, _adamw_small(items, name)):
            results[n] = tuple(o.reshape(wts[n].shape) for o in outs)

    def done(names):
        return [results[n][3] for n in names]

    recv = _exchange_wait(exchange1, [grad_x], "grad_exchange1_wait")
    update(group1, recv[:len(group1)], "adamw_")
    update_small(SMALL_HEAD, recv[len(group1):-1], "adamw_ln3")
    loss = jnp.sum(recv[-1][:, 0, 0, 0])
    recv = _exchange_wait(exchange2, done(group1 + SMALL_HEAD), "grad_exchange2_wait")
    update(group2, recv[:len(group2)], "adamw_")
    update_small(SMALL_MID, recv[len(group2):], "adamw_small")
    recv = _exchange_wait(exchange3, done(group2 + SMALL_MID), "grad_exchange3_wait")
    update(("mix_w_in",), recv, "adamw_")
    recv = _exchange_wait(exchange4, done(("mix_w_in",)), "grad_exchange4_wait")
    update(("ffn1_w_out",), recv[:1], "adamw_")
    update_small(SMALL_LATE, recv[1:], "adamw_ln1")
    recv = _exchange_wait(exchange5, done(("ffn1_w_out",) + SMALL_LATE), "grad_exchange5_wait")
    update(("ffn1_w_in",), recv, "adamw_")

    outs = [loss, grad_x.reshape(n_seq, seq, d_model)]
    for k in range(4):
        outs += [results[n][k][None] if n in SHARDED else results[n][k] for n in WEIGHTS]
    return tuple(outs)
```

```python
import math

import jax
import jax.numpy as jnp
from jax import lax
from jax.experimental import pallas as pl
from jax.experimental.pallas import tpu as pltpu

F32 = jnp.float32
BF16 = jnp.bfloat16

N_DEV = 8
LN_EPS = 1e-5
CHUNK = 128
N_SEG = 8
S5_CHANNELS_PER_STEP = 128
VMEM_LIMIT_BYTES = 56 * 1024 * 1024
MM_OPERAND_BLOCK_BYTES = 8 * 1024 * 1024
ADAM_BLOCK_BYTES = 6 * 1024 * 1024

ADAM_LR = 0.001
ADAM_B1 = 0.9
ADAM_B2 = 0.999
ADAM_EPS = 1e-08
ADAM_WD = 0.01
ADAM_STEP = 10

COL_SHARDED = ("ffn1_w_in", "mix_w_in", "up_a", "up_b", "ffn2_w_in", "ple_w_proj")
SHARDED = ("ffn1_w_in", "ffn1_w_out", "mix_w_in", "ssm_glu_w", "up_a", "up_b", "mix_w_out",
           "ffn2_w_in", "ffn2_w_out", "ple_w_proj", "ple_w_gate")
WEIGHTS = ("ffn1_w_in", "ffn1_w_out", "ln1_g", "ln1_b", "mix_w_in", "ssm_lambda_re", "ssm_lambda_im",
           "ssm_log_dt", "ssm_b_re", "ssm_b_im", "ssm_c_re", "ssm_c_im", "ssm_d", "ssm_glu_w",
           "ssm_glu_b", "gmlp_ln_g", "gmlp_ln_b", "gmlp_w_s", "gmlp_b_s", "up_a", "up_b", "mix_w_out",
           "ln2_g", "ln2_b", "ffn2_w_in", "ffn2_w_out", "ln3_g", "ln3_b", "ple_w_proj", "ple_w_gate")
GATHER_FIRST = ("ffn1_w_in",)
GATHER_EARLY = ("ffn1_w_out", "mix_w_in")
GATHER_MID = ("ssm_glu_w", "up_a", "up_b", "mix_w_out")
GATHER_LAST = ("ffn2_w_in", "ffn2_w_out", "ple_w_proj", "ple_w_gate")
SMALL = tuple(n for n in WEIGHTS if n not in SHARDED)
SMALL_HEAD = ("ln3_g", "ln3_b")
SMALL_LATE = ("ln1_g", "ln1_b")
SMALL_MID = tuple(n for n in SMALL if n not in SMALL_HEAD + SMALL_LATE)
SMALL_BF16 = ("gmlp_w_s", "ssm_b_re", "ssm_b_im", "ssm_c_re", "ssm_c_im")
INPUT_NAMES = ("x", "p") + WEIGHTS + ("loss_target",) + tuple("m_" + n for n in WEIGHTS) + tuple(
    "v_" + n for n in WEIGHTS)


def _pick(dim, pref, mult=128):
    if dim <= pref:
        return dim
    d = (pref // mult) * mult
    while d >= mult:
        if dim % d == 0:
            return d
        d -= mult
    return dim


def _params(n_axes=1):
    return pltpu.CompilerParams(dimension_semantics=("arbitrary",) * n_axes,
                                vmem_limit_bytes=VMEM_LIMIT_BYTES)


def _sigmoid(v):
    return 1.0 / (1.0 + jnp.exp(-v))


_GELU_C = math.sqrt(2.0 / math.pi)


def _gelu(v):
    return 0.5 * v * (1.0 + jnp.tanh(_GELU_C * (v + 0.044715 * (v * v * v))))


def _gelu_grad(v):
    t = jnp.tanh(_GELU_C * (v + 0.044715 * (v * v * v)))
    return 0.5 * (1.0 + t) + 0.5 * v * (1.0 - t * t) * (_GELU_C * (1.0 + 3.0 * 0.044715 * v * v))


def _ln_stats(r):
    mu = jnp.mean(r, axis=-1, keepdims=True)
    xc = r - mu
    var = jnp.mean(xc * xc, axis=-1, keepdims=True)
    rstd = lax.rsqrt(var + LN_EPS)
    return xc * rstd, rstd


def _ln_bwd(dy, xhat, rstd, g):
    dxh = dy * g
    m1 = jnp.mean(dxh, axis=-1, keepdims=True)
    m2 = jnp.mean(dxh * xhat, axis=-1, keepdims=True)
    return rstd * (dxh - m1 - xhat * m2)


def _fold8(v):
    rows, w = v.shape
    return jnp.sum(v.reshape(rows // 8, 8, w), axis=0)


def _dot(a, b, ta=False, tb=False):
    dn = (((0 if ta else 1,), (1 if tb else 0,)), ((), ()))
    return lax.dot_general(a.astype(BF16), b.astype(BF16), dn, preferred_element_type=F32)


_TOKEN = pl.BlockSpec((8, 128), lambda *_: (0, 0))


def _mm(a, b, *, name, ta=False, tb=False, out_dtype=F32, add=None, add_scale=1.0, token=None,
        tm=2048, tn=512, tk=4096):
    m_dim = a.shape[1] if ta else a.shape[0]
    k_dim = a.shape[0] if ta else a.shape[1]
    n_dim = b.shape[0] if tb else b.shape[1]
    assert (b.shape[1] if tb else b.shape[0]) == k_dim, (name, a.shape, b.shape)
    tk = _pick(k_dim, tk)
    tm = min(tm, max(256, MM_OPERAND_BLOCK_BYTES // (tk * a.dtype.itemsize)))
    tm, tn = _pick(m_dim, tm), _pick(n_dim, tn)
    nk = k_dim // tk
    has_add = add is not None

    def finish(r, add_ref, o_ref):
        if has_add:
            r = r + add_scale * add_ref[...].astype(F32)
        o_ref[...] = r.astype(out_dtype)

    def body(*refs):
        a_ref, b_ref = refs[:2]
        add_ref = refs[2] if has_add else None
        o_ref = refs[n_in]
        if nk == 1:
            finish(_dot(a_ref[...], b_ref[...], ta, tb), add_ref, o_ref)
            return
        acc_ref = refs[-1]
        k = pl.program_id(2)

        @pl.when(k == 0)
        def _():
            acc_ref[...] = jnp.zeros_like(acc_ref)

        acc_ref[...] += _dot(a_ref[...], b_ref[...], ta, tb)

        @pl.when(k == nk - 1)
        def _():
            finish(acc_ref[...], add_ref, o_ref)

    a_spec = (pl.BlockSpec((tk, tm), lambda i, j, k: (k, i)) if ta
              else pl.BlockSpec((tm, tk), lambda i, j, k: (i, k)))
    b_spec = (pl.BlockSpec((tn, tk), lambda i, j, k: (j, k)) if tb
              else pl.BlockSpec((tk, tn), lambda i, j, k: (k, j)))
    in_specs = [a_spec, b_spec]
    operands = [a, b]
    if has_add:
        in_specs.append(pl.BlockSpec((tm, tn), lambda i, j, k: (i, j)))
        operands.append(add)
    if token is not None:
        in_specs.append(_TOKEN)
        operands.append(token)
    n_in = len(operands)
    return pl.pallas_call(
        body, name=name,
        out_shape=jax.ShapeDtypeStruct((m_dim, n_dim), out_dtype),
        grid=(m_dim // tm, n_dim // tn, nk),
        in_specs=in_specs,
        out_specs=pl.BlockSpec((tm, tn), lambda i, j, k: (i, j)),
        scratch_shapes=[pltpu.VMEM((tm, tn), F32)] if nk > 1 else [],
        compiler_params=_params(3),
    )(*operands)


def _to_bf16(arrays, name, token=None):
    n = len(arrays)
    extra = [] if token is None else [token]

    def body(*refs):
        for src, dst in zip(refs[:n], refs[n + len(extra):]):
            dst[...] = src[...].astype(BF16)

    vmem = pl.BlockSpec(memory_space=pltpu.VMEM)
    return pl.pallas_call(
        body, name=name, out_shape=tuple(jax.ShapeDtypeStruct(a.shape, BF16) for a in arrays),
        in_specs=[vmem] * (n + len(extra)), out_specs=tuple([vmem] * n),
        compiler_params=pltpu.CompilerParams(vmem_limit_bytes=VMEM_LIMIT_BYTES))(*arrays, *extra)


def _rows(tr, w, cb=0):
    return pl.BlockSpec((tr, w), lambda i: (i, cb))


def _whole(shape):
    nd = len(shape)
    return pl.BlockSpec(tuple(shape), lambda i: (0,) * nd)


def _ffn_in(x16, w_in, name, token=None):
    t, d = x16.shape
    _, nb, _, fb = w_in.shape
    tm = _pick(t, 1024, 8)
    extra = [] if token is None else [token]

    def body(*refs):
        x_ref, wg_ref, wu_ref = refs[:3]
        h_ref, a_ref = refs[-2:]
        xv = x_ref[...]
        g = _dot(xv, wg_ref[0, 0])
        u = _dot(xv, wu_ref[0, 0])
        h_ref[0, 0] = g.astype(BF16)
        h_ref[0, 1] = u.astype(BF16)
        a_ref[0] = (g * _sigmoid(g) * u).astype(BF16)

    return pl.pallas_call(
        body, name=name,
        out_shape=(jax.ShapeDtypeStruct((nb, 2, t, fb), BF16), jax.ShapeDtypeStruct((nb, t, fb), BF16)),
        grid=(t // tm, nb),
        in_specs=[pl.BlockSpec((tm, d), lambda i, j: (i, 0)),
                  pl.BlockSpec((1, 1, d, fb), lambda i, j: (0, j, 0, 0)),
                  pl.BlockSpec((1, 1, d, fb), lambda i, j: (1, j, 0, 0))] + [_TOKEN] * len(extra),
        out_specs=(pl.BlockSpec((1, 2, tm, fb), lambda i, j: (j, 0, i, 0)),
                   pl.BlockSpec((1, tm, fb), lambda i, j: (j, i, 0))),
        compiler_params=_params(2))(x16, w_in, w_in, *extra)


def _ffn_out_ln(a, w_out, xin, g, b, alpha, name, scale=0.5, token=None):
    nb, t, fb = a.shape
    d = w_out.shape[2]
    tm = _pick(t, 512, 8)
    extra = [] if token is None else [token]

    def body(*refs):
        a_ref, w_ref, x_ref, g_ref, b_ref = refs[:5]
        r_ref, y_ref, y16_ref = refs[-3:]
        f = _dot(a_ref[0], w_ref[0])
        for jb in range(1, nb):
            f = f + _dot(a_ref[jb], w_ref[jb])
        r = alpha * x_ref[...] + scale * f
        xhat, _ = _ln_stats(r)
        y = xhat * g_ref[...] + b_ref[...]
        r_ref[...] = r
        y_ref[...] = y
        y16_ref[...] = y.astype(BF16)

    return pl.pallas_call(
        body, name=name,
        out_shape=(jax.ShapeDtypeStruct((t, d), F32), jax.ShapeDtypeStruct((t, d), F32),
                   jax.ShapeDtypeStruct((t, d), BF16)),
        grid=(t // tm,),
        in_specs=[pl.BlockSpec((nb, tm, fb), lambda i: (0, i, 0)), _whole(w_out.shape), _rows(tm, d),
                  _whole((1, d)), _whole((1, d))] + [_TOKEN] * len(extra),
        out_specs=(_rows(tm, d), _rows(tm, d), _rows(tm, d)),
        compiler_params=_params())(a, w_out, xin, g, b, *extra)


def _ffn_out_dx(drs, w_out, h, name, token=None):
    nb, _, t, fb = h.shape
    d = w_out.shape[2]
    tm = _pick(t, 1024, 8)
    extra = [] if token is None else [token]

    def body(*refs):
        dr_ref, w_ref, h_ref, dh_ref = refs[0], refs[1], refs[2], refs[-1]
        da = _dot(dr_ref[...], w_ref[0], tb=True)
        g, u = h_ref[0, 0].astype(F32), h_ref[0, 1].astype(F32)
        sg = _sigmoid(g)
        dh_ref[0, 0] = (da * u * (sg * (1.0 + g * (1.0 - sg)))).astype(BF16)
        dh_ref[0, 1] = (da * (g * sg)).astype(BF16)

    return pl.pallas_call(
        body, name=name, out_shape=jax.ShapeDtypeStruct((nb, 2, t, fb), BF16), grid=(t // tm, nb),
        in_specs=[pl.BlockSpec((tm, d), lambda i, j: (i, 0)),
                  pl.BlockSpec((1, fb, d), lambda i, j: (j, 0, 0)),
                  pl.BlockSpec((1, 2, tm, fb), lambda i, j: (j, 0, i, 0))] + [_TOKEN] * len(extra),
        out_specs=pl.BlockSpec((1, 2, tm, fb), lambda i, j: (j, 0, i, 0)),
        compiler_params=_params(2))(drs, w_out, h, *extra)


def _ffn_out_dw(a, drs, name):
    nb, t, fb = a.shape
    d = drs.shape[1]

    def body(a_ref, dr_ref, o_ref):
        o_ref[0] = _dot(a_ref[0], dr_ref[...], ta=True).astype(BF16)

    return pl.pallas_call(
        body, name=name, out_shape=jax.ShapeDtypeStruct((nb, fb, d), BF16), grid=(nb,),
        in_specs=[pl.BlockSpec((1, t, fb), lambda j: (j, 0, 0)), pl.BlockSpec((t, d), lambda j: (0, 0))],
        out_specs=pl.BlockSpec((1, fb, d), lambda j: (j, 0, 0)),
        compiler_params=_params())(a, drs)


def _ffn_in_dw(x16, dh, name, token=None):
    t, d = x16.shape
    nb, _, _, fb = dh.shape
    extra = [] if token is None else [token]

    def body(*refs):
        x_ref, dh_ref, o_ref = refs[0], refs[1], refs[-1]
        o_ref[0, 0] = _dot(x_ref[...], dh_ref[0, 0], ta=True).astype(BF16)

    return pl.pallas_call(
        body, name=name, out_shape=jax.ShapeDtypeStruct((2, nb, d, fb), BF16), grid=(nb, 2),
        in_specs=[pl.BlockSpec((t, d), lambda j, s: (0, 0)),
                  pl.BlockSpec((1, 1, t, fb), lambda j, s: (j, s, 0, 0))] + [_TOKEN] * len(extra),
        out_specs=pl.BlockSpec((1, 1, d, fb), lambda j, s: (s, j, 0, 0)),
        compiler_params=_params(2))(x16, dh, *extra)


def _ffn_in_dx(dh, w_in, add, add_scale, name, token=None):
    nb, _, t, fb = dh.shape
    d = w_in.shape[2]
    tm = _pick(t, 512, 8)
    has_add = add is not None
    extra = [] if token is None else [token]

    def body(*refs):
        dh_ref, w_ref = refs[:2]
        o_ref = refs[-1]
        acc = None
        for jb in range(nb):
            for s in range(2):
                part = _dot(dh_ref[jb, s], w_ref[s, jb], tb=True)
                acc = part if acc is None else acc + part
        if has_add:
            acc = acc + add_scale * refs[2][...]
        o_ref[...] = acc

    return pl.pallas_call(
        body, name=name, out_shape=jax.ShapeDtypeStruct((t, d), F32), grid=(t // tm,),
        in_specs=[pl.BlockSpec((nb, 2, tm, fb), lambda i: (0, 0, i, 0)), _whole(w_in.shape)]
        + ([_rows(tm, d)] if has_add else []) + [_TOKEN] * len(extra),
        out_specs=_rows(tm, d),
        compiler_params=_params())(*([dh, w_in] + ([add] if has_add else []) + extra))


def _ffn_in_dx_ln(dh, w_in, r, dy_b, b_scale, ln_g, out_scale, name, token=None):
    nb, _, t, fb = dh.shape
    d = w_in.shape[2]
    tm = _pick(t, 512, 8)
    extra = [] if token is None else [token]

    def body(*refs):
        dh_ref, w_ref, r_ref, dyb_ref, g_ref = refs[:5]
        dr_ref, drs_ref, dg_ref, dbeta_ref = refs[-4:]
        dy = b_scale * dyb_ref[...]
        for jb in range(nb):
            for s in range(2):
                dy = dy + _dot(dh_ref[jb, s], w_ref[s, jb], tb=True)
        xhat, rstd = _ln_stats(r_ref[...])
        dr = _ln_bwd(dy, xhat, rstd, g_ref[...])
        dr_ref[...] = dr
        drs_ref[...] = (out_scale * dr).astype(BF16)

        @pl.when(pl.program_id(0) == 0)
        def _():
            dg_ref[...] = jnp.zeros_like(dg_ref)
            dbeta_ref[...] = jnp.zeros_like(dbeta_ref)

        dg_ref[...] += _fold8(dy * xhat)
        dbeta_ref[...] += _fold8(dy)

    return pl.pallas_call(
        body, name=name,
        out_shape=(jax.ShapeDtypeStruct((t, d), F32), jax.ShapeDtypeStruct((t, d), BF16),
                   jax.ShapeDtypeStruct((8, d), F32), jax.ShapeDtypeStruct((8, d), F32)),
        grid=(t // tm,),
        in_specs=[pl.BlockSpec((nb, 2, tm, fb), lambda i: (0, 0, i, 0)), _whole(w_in.shape), _rows(tm, d),
                  _rows(tm, d), _whole((1, d))] + [_TOKEN] * len(extra),
        out_specs=(_rows(tm, d), _rows(tm, d), _whole((8, d)), _whole((8, d))),
        compiler_params=_params())(dh, w_in, r, dy_b, ln_g, *extra)


def _mm_ln_bwd(a, w, r, dy_b, b_scale, ln_g, out_scale, name, token=None):
    t, k_dim = a.shape
    d = w.shape[0]
    tm = _pick(t, 512, 8)
    extra = [] if token is None else [token]

    def body(*refs):
        a_ref, w_ref, r_ref, dyb_ref, g_ref = refs[:5]
        dr_ref, drs_ref, dg_ref, dbeta_ref = refs[-4:]
        dy = _dot(a_ref[...], w_ref[...], tb=True) + b_scale * dyb_ref[...]
        xhat, rstd = _ln_stats(r_ref[...])
        dr = _ln_bwd(dy, xhat, rstd, g_ref[...])
        dr_ref[...] = dr
        drs_ref[...] = (out_scale * dr).astype(BF16)

        @pl.when(pl.program_id(0) == 0)
        def _():
            dg_ref[...] = jnp.zeros_like(dg_ref)
            dbeta_ref[...] = jnp.zeros_like(dbeta_ref)

        dg_ref[...] += _fold8(dy * xhat)
        dbeta_ref[...] += _fold8(dy)

    return pl.pallas_call(
        body, name=name,
        out_shape=(jax.ShapeDtypeStruct((t, d), F32), jax.ShapeDtypeStruct((t, d), BF16),
                   jax.ShapeDtypeStruct((8, d), F32), jax.ShapeDtypeStruct((8, d), F32)),
        grid=(t // tm,),
        in_specs=[_rows(tm, k_dim), _whole(w.shape), _rows(tm, d), _rows(tm, d), _whole((1, d))]
        + [_TOKEN] * len(extra),
        out_specs=(_rows(tm, d), _rows(tm, d), _whole((8, d)), _whole((8, d))),
        compiler_params=_params())(a, w, r, dy_b, ln_g, *extra)


def _take_row(v, row_id, s):
    return jnp.sum(jnp.where(row_id == s, v, 0.0), axis=0, keepdims=True)


def _complex_power(ar, ai, n):
    out = None
    while n:
        if n & 1:
            out = (ar, ai) if out is None else (out[0] * ar - out[1] * ai, out[0] * ai + out[1] * ar)
        ar, ai = ar * ar - ai * ai, 2.0 * ar * ai
        n >>= 1
    return out


def _seg_carries(f_re, f_im, p_re, p_im, reverse):
    row_id = lax.broadcasted_iota(jnp.int32, f_re.shape, 0)
    p_re, p_im = _take_row(p_re, row_id, 0), _take_row(p_im, row_id, 0)
    out_re, out_im = jnp.zeros_like(f_re), jnp.zeros_like(f_im)
    c_re, c_im = jnp.zeros_like(p_re), jnp.zeros_like(p_im)
    order = range(N_SEG - 1, -1, -1) if reverse else range(N_SEG)
    for s in order:
        out_re = jnp.where(row_id == s, c_re, out_re)
        out_im = jnp.where(row_id == s, c_im, out_im)
        fr, fi = _take_row(f_re, row_id, s), _take_row(f_im, row_id, s)
        c_re, c_im = fr + p_re * c_re - p_im * c_im, fi + p_re * c_im + p_im * c_re
    return out_re, out_im


def _s5_fwd(za16, b_re, b_im, a_re, a_im, c_re, c_im, n_seq, name):
    t = za16.shape[0]
    n_sg, ch, st = b_re.shape
    seq = t // n_seq
    steps = seq // N_SEG

    def body(za_ref, bre_ref, bim_ref, are, aim, cre_ref, cim_ref, hre, him, y_ref):
        za = za_ref[...]
        hre[...] = _dot(za, bre_ref[0])
        him[...] = _dot(za, bim_ref[0])
        ar = jnp.broadcast_to(are[...], (N_SEG, st))
        ai = jnp.broadcast_to(aim[...], (N_SEG, st))
        zero = jnp.zeros((N_SEG, st), F32)
        p_re, p_im = _complex_power(ar, ai, steps)

        def local(k, carry):
            lr, li = carry
            rows = pl.ds(pl.multiple_of(k * N_SEG, N_SEG), N_SEG)
            nr = ar * lr - ai * li + hre[rows, :]
            ni = ar * li + ai * lr + him[rows, :]
            hre[rows, :] = nr
            him[rows, :] = ni
            return nr, ni

        f_re, f_im = lax.fori_loop(0, steps, local, (zero, zero))
        c_re, c_im = _seg_carries(f_re, f_im, p_re, p_im, reverse=False)

        def fix(k, carry):
            qr, qi = carry
            rows = pl.ds(pl.multiple_of(k * N_SEG, N_SEG), N_SEG)
            hre[rows, :] = hre[rows, :] + qr
            him[rows, :] = him[rows, :] + qi
            return ar * qr - ai * qi, ar * qi + ai * qr

        lax.fori_loop(0, steps, fix, (ar * c_re - ai * c_im, ar * c_im + ai * c_re))
        y_ref[...] = _dot(hre[...], cre_ref[0]) + _dot(him[...], cim_ref[0])

    rows_ch = pl.BlockSpec((seq, ch), lambda s, j: (s, j))
    rows_st = pl.BlockSpec((seq, st), lambda s, j: (s, j))
    vec = pl.BlockSpec((1, st), lambda s, j: (0, j))
    b_blk = pl.BlockSpec((1, ch, st), lambda s, j: (j, 0, 0))
    c_blk = pl.BlockSpec((1, st, ch), lambda s, j: (j, 0, 0))
    return pl.pallas_call(
        body, name=name,
        out_shape=(jax.ShapeDtypeStruct((t, n_sg * st), F32), jax.ShapeDtypeStruct((t, n_sg * st), F32),
                   jax.ShapeDtypeStruct((t, n_sg * ch), F32)),
        grid=(n_seq, n_sg), in_specs=[rows_ch, b_blk, b_blk, vec, vec, c_blk, c_blk],
        out_specs=(rows_st, rows_st, rows_ch),
        compiler_params=_params(2))(za16, b_re, b_im, a_re, a_im, c_re, c_im)


def _s5_bwd(dy16, dza_skip, h_re, h_im, za16, b_re, b_im, a_re, a_im, c_re, c_im, n_seq, name):
    t = dy16.shape[0]
    n_sg, ch, st = b_re.shape
    seq = t // n_seq
    steps = seq // N_SEG

    def body(dy_ref, skip_ref, hre, him, za_ref, bre_ref, bim_ref, are, aim, cre_ref, cim_ref,
             dza_ref, dbre_ref, dbim_ref, dcre_ref, dcim_ref, dare, daim, lre, lim):
        dy = dy_ref[...]
        lre[...] = _dot(dy, cre_ref[0], tb=True)
        lim[...] = _dot(dy, cim_ref[0], tb=True)
        ar = jnp.broadcast_to(are[...], (N_SEG, st))
        ai = -jnp.broadcast_to(aim[...], (N_SEG, st))
        zero = jnp.zeros((N_SEG, st), F32)
        p_re, p_im = _complex_power(ar, ai, steps)

        def local(i, carry):
            lr, li = carry
            k = steps - 1 - i
            rows = pl.ds(pl.multiple_of(k * N_SEG, N_SEG), N_SEG)
            nr = ar * lr - ai * li + lre[rows, :]
            ni = ar * li + ai * lr + lim[rows, :]
            lre[rows, :] = nr
            lim[rows, :] = ni
            return nr, ni

        f_re, f_im = lax.fori_loop(0, steps, local, (zero, zero))
        c_re, c_im = _seg_carries(f_re, f_im, p_re, p_im, reverse=True)

        def accumulate(lam_r, lam_i, hp_r, hp_i, acc_r, acc_i):
            return acc_r + (lam_r * hp_r + lam_i * hp_i), acc_i + (lam_i * hp_r - lam_r * hp_i)

        def fix(i, carry):
            qr, qi, acc_r, acc_i = carry
            k = steps - 1 - i
            rows = pl.ds(pl.multiple_of(k * N_SEG, N_SEG), N_SEG)
            prev = pl.ds(pl.multiple_of((k - 1) * N_SEG, N_SEG), N_SEG)
            lam_r = lre[rows, :] + qr
            lam_i = lim[rows, :] + qi
            lre[rows, :] = lam_r
            lim[rows, :] = lam_i
            acc_r, acc_i = accumulate(lam_r, lam_i, hre[prev, :], him[prev, :], acc_r, acc_i)
            return ar * qr - ai * qi, ar * qi + ai * qr, acc_r, acc_i

        qr, qi, acc_r, acc_i = lax.fori_loop(
            0, steps - 1, fix, (ar * c_re - ai * c_im, ar * c_im + ai * c_re, zero, zero))
        first = pl.ds(0, N_SEG)
        last = pl.ds((steps - 1) * N_SEG, N_SEG)
        lam_r = lre[first, :] + qr
        lam_i = lim[first, :] + qi
        lre[first, :] = lam_r
        lim[first, :] = lam_i
        row_id = lax.broadcasted_iota(jnp.int32, (N_SEG, st), 0)
        hp_r = jnp.where(row_id == 0, 0.0, pltpu.roll(hre[last, :], 1, 0))
        hp_i = jnp.where(row_id == 0, 0.0, pltpu.roll(him[last, :], 1, 0))
        acc_r, acc_i = accumulate(lam_r, lam_i, hp_r, hp_i, acc_r, acc_i)

        lam_re16 = lre[...].astype(BF16)
        lam_im16 = lim[...].astype(BF16)
        dza_ref[...] = (_dot(lam_re16, bre_ref[0], tb=True) + _dot(lam_im16, bim_ref[0], tb=True)
                        + skip_ref[...]).astype(BF16)

        @pl.when(pl.program_id(1) == 0)
        def _():
            for ref in (dbre_ref, dbim_ref, dcre_ref, dcim_ref, dare, daim):
                ref[...] = jnp.zeros_like(ref)

        za = za_ref[...]
        dbre_ref[0] += _dot(za, lam_re16, ta=True)
        dbim_ref[0] += _dot(za, lam_im16, ta=True)
        dcre_ref[0] += _dot(hre[...], dy, ta=True)
        dcim_ref[0] += _dot(him[...], dy, ta=True)
        dare[...] += acc_r
        daim[...] += acc_i

    rows_ch = pl.BlockSpec((seq, ch), lambda j, s: (s, j))
    rows_st = pl.BlockSpec((seq, st), lambda j, s: (s, j))
    vec = pl.BlockSpec((1, st), lambda j, s: (0, j))
    b_blk = pl.BlockSpec((1, ch, st), lambda j, s: (j, 0, 0))
    c_blk = pl.BlockSpec((1, st, ch), lambda j, s: (j, 0, 0))
    acc = pl.BlockSpec((N_SEG, st), lambda j, s: (0, j))
    return pl.pallas_call(
        body, name=name,
        out_shape=(jax.ShapeDtypeStruct((t, n_sg * ch), BF16),
                   jax.ShapeDtypeStruct((n_sg, ch, st), F32), jax.ShapeDtypeStruct((n_sg, ch, st), F32),
                   jax.ShapeDtypeStruct((n_sg, st, ch), F32), jax.ShapeDtypeStruct((n_sg, st, ch), F32),
                   jax.ShapeDtypeStruct((N_SEG, n_sg * st), F32), jax.ShapeDtypeStruct((N_SEG, n_sg * st), F32)),
        grid=(n_sg, n_seq),
        in_specs=[rows_ch, rows_ch, rows_st, rows_st, rows_ch, b_blk, b_blk, vec, vec, c_blk, c_blk],
        out_specs=(rows_ch, b_blk, b_blk, c_blk, c_blk, acc, acc),
        scratch_shapes=[pltpu.VMEM((seq, st), F32), pltpu.VMEM((seq, st), F32)],
        compiler_params=_params(2))(dy16, dza_skip, h_re, h_im, za16, b_re, b_im, a_re, a_im, c_re, c_im)


def _s5_post_fwd(yssm, u, d_skip, glu_w, glu_b, name):
    t, w = yssm.shape
    tr = _pick(t, 512, 8)

    def body(y_ref, u_ref, d_ref, w_ref, b_ref, o_ref):
        yg = _gelu(y_ref[...] + d_ref[...] * u_ref[...])
        pre = _dot(yg, w_ref[...]) + b_ref[...]
        o_ref[...] = yg * _sigmoid(pre)

    return pl.pallas_call(
        body, name=name, out_shape=jax.ShapeDtypeStruct((t, w), F32), grid=(t // tr,),
        in_specs=[_rows(tr, w), _rows(tr, w), _whole((1, w)), _whole((w, w)), _whole((1, w))],
        out_specs=_rows(tr, w), compiler_params=_params())(yssm, u, d_skip, glu_w, glu_b)


def _s5_post_bwd(yssm, u, dout, d_skip, glu_w, glu_b, name):
    t, w = yssm.shape
    tr = _pick(t, 512, 8)

    def body(y_ref, u_ref, do_ref, d_ref, w_ref, b_ref, dy_ref, du_ref, dw_ref, dd_ref, db_ref):
        uu = u_ref[...]
        y = y_ref[...] + d_ref[...] * uu
        yg = _gelu(y)
        sg = _sigmoid(_dot(yg, w_ref[...]) + b_ref[...])
        do = do_ref[...]
        dpre = do * yg * sg * (1.0 - sg)
        dyg = do * sg + _dot(dpre, w_ref[...], tb=True)
        dy = dyg * _gelu_grad(y)
        dy_ref[...] = dy.astype(BF16)
        du_ref[...] = dy * d_ref[...]

        @pl.when(pl.program_id(0) == 0)
        def _():
            dw_ref[...] = jnp.zeros_like(dw_ref)
            dd_ref[...] = jnp.zeros_like(dd_ref)
            db_ref[...] = jnp.zeros_like(db_ref)

        dw_ref[...] += _dot(yg, dpre, ta=True)
        dd_ref[...] += _fold8(dy * uu)
        db_ref[...] += _fold8(dpre)

    return pl.pallas_call(
        body, name=name,
        out_shape=(jax.ShapeDtypeStruct((t, w), BF16), jax.ShapeDtypeStruct((t, w), F32),
                   jax.ShapeDtypeStruct((w, w), F32), jax.ShapeDtypeStruct((8, w), F32),
                   jax.ShapeDtypeStruct((8, w), F32)),
        grid=(t // tr,),
        in_specs=[_rows(tr, w), _rows(tr, w), _rows(tr, w), _whole((1, w)), _whole((w, w)),
                  _whole((1, w))],
        out_specs=(_rows(tr, w), _rows(tr, w), _whole((w, w)), _whole((8, w)), _whole((8, w))),
        compiler_params=_params())(yssm, u, dout, d_skip, glu_w, glu_b)


def _head_select(parts, head_dim):
    col_head = lax.broadcasted_iota(jnp.int32, parts[0].shape, 1) // head_dim
    out = jnp.zeros_like(parts[0])
    for h, part in enumerate(parts):
        out = jnp.where(col_head == h, part, out)
    return out


def _gmlp_fwd(proj, ln_g, ln_b, ws, bs_cols, name):
    t = proj.shape[0]
    n_heads = ws.shape[0]
    w = bs_cols.shape[1]
    head_dim = w // n_heads
    cpb = _pick(t // CHUNK, 4, 1)
    tr = cpb * CHUNK

    def body(zu_ref, zv_ref, g_ref, b_ref, ws_ref, bs_ref, o_ref):
        for c in range(cpb):
            rows = pl.ds(c * CHUNK, CHUNK)
            xhat, _ = _ln_stats(_gelu(zv_ref[rows, :]))
            vn = (xhat * g_ref[...] + b_ref[...]).astype(BF16)
            s = _head_select([_dot(ws_ref[h], vn) for h in range(n_heads)], head_dim) + bs_ref[...]
            o_ref[rows, :] = _gelu(zu_ref[rows, :]) * s

    return pl.pallas_call(
        body, name=name, out_shape=jax.ShapeDtypeStruct((t, w), F32), grid=(t // tr,),
        in_specs=[_rows(tr, w, 1), _rows(tr, w, 2), _whole((1, w)), _whole((1, w)),
                  _whole(ws.shape), _whole(bs_cols.shape)],
        out_specs=_rows(tr, w), compiler_params=_params())(proj, proj, ln_g, ln_b, ws, bs_cols)


def _gmlp_bwd(proj, dout, ln_g, ln_b, ws, ws_t, bs_cols, name):
    t = proj.shape[0]
    n_heads = ws.shape[0]
    w = bs_cols.shape[1]
    head_dim = w // n_heads
    cpb = _pick(t // CHUNK, 4, 1)
    tr = cpb * CHUNK

    def body(zu_ref, zv_ref, do_ref, g_ref, b_ref, ws_ref, wst_ref, bs_ref,
             dzu_ref, dzv_ref, dws_ref, dbs_ref, dg_ref, dbeta_ref):
        @pl.when(pl.program_id(0) == 0)
        def _():
            dws_ref[...] = jnp.zeros_like(dws_ref)
            dbs_ref[...] = jnp.zeros_like(dbs_ref)
            dg_ref[...] = jnp.zeros_like(dg_ref)
            dbeta_ref[...] = jnp.zeros_like(dbeta_ref)

        col_head = lax.broadcasted_iota(jnp.int32, (CHUNK, w), 1) // head_dim
        for c in range(cpb):
            rows = pl.ds(c * CHUNK, CHUNK)
            zu, zv, do = zu_ref[rows, :], zv_ref[rows, :], do_ref[rows, :]
            xhat, rstd = _ln_stats(_gelu(zv))
            vn = (xhat * g_ref[...] + b_ref[...]).astype(BF16)
            s = _head_select([_dot(ws_ref[h], vn) for h in range(n_heads)], head_dim) + bs_ref[...]
            dzu_ref[rows, :] = (do * s * _gelu_grad(zu)).astype(BF16)
            ds = do * _gelu(zu)
            ds16 = ds.astype(BF16)
            dbs_ref[...] += ds
            for h in range(n_heads):
                dws_ref[h] += _dot(jnp.where(col_head == h, ds16, jnp.zeros_like(ds16)), vn, tb=True)
            dvn = _head_select([_dot(wst_ref[h], ds16) for h in range(n_heads)], head_dim)
            dg_ref[...] += _fold8(dvn * xhat)
            dbeta_ref[...] += _fold8(dvn)
            dgv = _ln_bwd(dvn, xhat, rstd, g_ref[...])
            dzv_ref[rows, :] = (dgv * _gelu_grad(zv)).astype(BF16)

    return pl.pallas_call(
        body, name=name,
        out_shape=(jax.ShapeDtypeStruct((t, w), BF16), jax.ShapeDtypeStruct((t, w), BF16),
                   jax.ShapeDtypeStruct(ws.shape, F32), jax.ShapeDtypeStruct((CHUNK, w), F32),
                   jax.ShapeDtypeStruct((8, w), F32), jax.ShapeDtypeStruct((8, w), F32)),
        grid=(t // tr,),
        in_specs=[_rows(tr, w, 1), _rows(tr, w, 2), _rows(tr, w), _whole((1, w)), _whole((1, w)),
                  _whole(ws.shape), _whole(ws.shape), _whole(bs_cols.shape)],
        out_specs=(_rows(tr, w), _rows(tr, w), _whole(ws.shape), _whole((CHUNK, w)),
                   _whole((8, w)), _whole((8, w))),
        compiler_params=_params())(proj, proj, dout, ln_g, ln_b, ws, ws_t, bs_cols)


def _merge_fwd(s5out, gm, proj, up_a, up_b, name):
    t, w = s5out.shape
    d = up_a.shape[1]
    assert d == 2 * w
    tr = _pick(t, 512, 8)

    def body(s_ref, gm_ref, ga0, ga1, gb0, gb1, ua_ref, ub_ref, m_ref, ya_ref, yb_ref):
        ya = _dot(s_ref[...], ua_ref[...])
        yb = _dot(gm_ref[...], ub_ref[...])
        ya_ref[...] = ya.astype(BF16)
        yb_ref[...] = yb.astype(BF16)
        for half, (ga, gb) in enumerate(((ga0, gb0), (ga1, gb1))):
            cols = slice(half * w, (half + 1) * w)
            m_ref[:, cols] = (_sigmoid(ga[...]) * ya[:, cols] + _sigmoid(gb[...]) * yb[:, cols]).astype(BF16)

    return pl.pallas_call(
        body, name=name,
        out_shape=(jax.ShapeDtypeStruct((t, d), BF16), jax.ShapeDtypeStruct((t, d), BF16),
                   jax.ShapeDtypeStruct((t, d), BF16)),
        grid=(t // tr,),
        in_specs=[_rows(tr, w), _rows(tr, w), _rows(tr, w, 3), _rows(tr, w, 4), _rows(tr, w, 5),
                  _rows(tr, w, 6), _whole(up_a.shape), _whole(up_b.shape)],
        out_specs=(_rows(tr, d), _rows(tr, d), _rows(tr, d)),
        compiler_params=_params())(s5out, gm, proj, proj, proj, proj, up_a, up_b)


def _merge_bwd(dm, ya, yb, s5out, gm, proj, up_a, up_b, name):
    t, d = dm.shape
    w = d // 2
    tr = _pick(t, 512, 8)

    def body(dm_ref, ya_ref, yb_ref, s_ref, gm_ref, ga0, ga1, gb0, gb1, ua_ref, ub_ref,
             dga_ref, dgb_ref, ds_ref, dgm_ref, dua_ref, dub_ref):
        dm_v, ya, yb = dm_ref[...], ya_ref[...].astype(F32), yb_ref[...].astype(F32)
        dya, dyb = [], []
        for half, (ga, gb) in enumerate(((ga0, gb0), (ga1, gb1))):
            cols = slice(half * w, (half + 1) * w)
            sa, sb = _sigmoid(ga[...]), _sigmoid(gb[...])
            dmh = dm_v[:, cols]
            dga_ref[:, cols] = (dmh * ya[:, cols] * sa * (1.0 - sa)).astype(BF16)
            dgb_ref[:, cols] = (dmh * yb[:, cols] * sb * (1.0 - sb)).astype(BF16)
            dya.append((dmh * sa).astype(BF16))
            dyb.append((dmh * sb).astype(BF16))
        dya = jnp.concatenate(dya, axis=1)
        dyb = jnp.concatenate(dyb, axis=1)
        ds_ref[...] = _dot(dya, ua_ref[...], tb=True)
        dgm_ref[...] = _dot(dyb, ub_ref[...], tb=True)

        @pl.when(pl.program_id(0) == 0)
        def _():
            dua_ref[...] = jnp.zeros_like(dua_ref)
            dub_ref[...] = jnp.zeros_like(dub_ref)

        dua_ref[...] += _dot(s_ref[...], dya, ta=True)
        dub_ref[...] += _dot(gm_ref[...], dyb, ta=True)

    return pl.pallas_call(
        body, name=name,
        out_shape=(jax.ShapeDtypeStruct((t, d), BF16), jax.ShapeDtypeStruct((t, d), BF16),
                   jax.ShapeDtypeStruct((t, w), F32), jax.ShapeDtypeStruct((t, w), F32),
                   jax.ShapeDtypeStruct(up_a.shape, F32), jax.ShapeDtypeStruct(up_b.shape, F32)),
        grid=(t // tr,),
        in_specs=[_rows(tr, d), _rows(tr, d), _rows(tr, d), _rows(tr, w), _rows(tr, w),
                  _rows(tr, w, 3), _rows(tr, w, 4), _rows(tr, w, 5), _rows(tr, w, 6),
                  _whole(up_a.shape), _whole(up_b.shape)],
        out_specs=(_rows(tr, d), _rows(tr, d), _rows(tr, w), _rows(tr, w), _whole(up_a.shape),
                   _whole(up_b.shape)),
        compiler_params=_params())(dm, ya, yb, s5out, gm, proj, proj, proj, proj, up_a, up_b)


def _head(x3, r3, ln_g, p, target, w_gate, w_proj, out_scale, name):
    t, d = x3.shape
    pd = p.shape[1]
    tr = _pick(t, 512, 8)
    nb = t // tr

    def body(x_ref, r_ref, g_ref, p_ref, t_ref, wg_ref, wp_ref,
             loss_ref, dr_ref, drs_ref, dwg_ref, dwp_ref, dg_ref, dbeta_ref):
        xv = x_ref[...]
        sg = _sigmoid(_dot(xv, wg_ref[...]))
        pp = _dot(p_ref[...], wp_ref[...])
        err = xv + sg * pp - t_ref[...]
        loss_ref[...] = jnp.full((8, 128), 0.5 * jnp.sum(jnp.mean(err * err, axis=-1)), F32)
        dout = err * (1.0 / d)
        dgpre = dout * pp * sg * (1.0 - sg)
        dpp = dout * sg
        dx = dout + _dot(dgpre, wg_ref[...], tb=True)
        xhat, rstd = _ln_stats(r_ref[...])
        dr = _ln_bwd(dx, xhat, rstd, g_ref[...])
        dr_ref[...] = dr
        drs_ref[...] = (out_scale * dr).astype(BF16)

        @pl.when(pl.program_id(0) == 0)
        def _():
            for ref in (dwg_ref, dwp_ref, dg_ref, dbeta_ref):
                ref[...] = jnp.zeros_like(ref)

        dwg_ref[...] += _dot(xv, dgpre, ta=True)
        dwp_ref[...] += _dot(p_ref[...], dpp, ta=True)
        dg_ref[...] += _fold8(dx * xhat)
        dbeta_ref[...] += _fold8(dx)

    return pl.pallas_call(
        body, name=name,
        out_shape=(jax.ShapeDtypeStruct((nb * 8, 128), F32), jax.ShapeDtypeStruct((t, d), F32),
                   jax.ShapeDtypeStruct((t, d), BF16), jax.ShapeDtypeStruct((d, d), F32),
                   jax.ShapeDtypeStruct((pd, d), F32), jax.ShapeDtypeStruct((8, d), F32),
                   jax.ShapeDtypeStruct((8, d), F32)),
        grid=(nb,),
        in_specs=[_rows(tr, d), _rows(tr, d), _whole((1, d)), _rows(tr, pd), _rows(tr, d), _whole((d, d)),
                  _whole((pd, d))],
        out_specs=(pl.BlockSpec((8, 128), lambda i: (i, 0)), _rows(tr, d), _rows(tr, d), _whole((d, d)),
                   _whole((pd, d)), _whole((8, d)), _whole((8, d))),
        compiler_params=_params())(x3, r3, ln_g, p, target, w_gate, w_proj)


_HBM = pl.BlockSpec(memory_space=pltpu.HBM)


_SEM = pl.BlockSpec(memory_space=pltpu.SEMAPHORE)
_ANY = pl.BlockSpec(memory_space=pl.ANY)
_DATAFLOW = pltpu.SideEffectType.DATAFLOW_SIDE_EFFECTING


def _peer(k, x, y, c):
    return (1 - x if k & 4 else x, 1 - y if k & 2 else y, 1 - c if k & 1 else c)


def _exchange_start(sends, after, name):
    n = len(sends)
    lands = [lax.empty((N_DEV,) + s.shape[1:], s.dtype) for s in sends]

    def body(*refs):
        s_refs, l_refs = refs[:n], refs[n:2 * n]
        send_sems, recv_sems, local_sems, token = refs[2 * n + 1], refs[2 * n + 2], refs[2 * n + 3], refs[-1]
        x, y, c = lax.axis_index("x"), lax.axis_index("y"), lax.axis_index("c")
        me = 4 * x + 2 * y + c
        for a in range(n):
            same = sends[a].shape[0] == 1
            pltpu.make_async_copy(s_refs[a].at[0 if same else me], l_refs[a].at[me], local_sems.at[a]).start()
            for k in range(1, N_DEV):
                px, py, pc = _peer(k, x, y, c)
                pltpu.make_async_remote_copy(
                    src_ref=s_refs[a].at[0 if same else 4 * px + 2 * py + pc], dst_ref=l_refs[a].at[me],
                    send_sem=send_sems.at[7 * a + k - 1], recv_sem=recv_sems.at[7 * a + k - 1],
                    device_id=(px, py, pc), device_id_type=pl.DeviceIdType.MESH).start()
        token[...] = jnp.zeros_like(token)

    outs = pl.pallas_call(
        body, name=name,
        out_shape=(pltpu.SemaphoreType.DMA((7 * n,)), pltpu.SemaphoreType.DMA((7 * n,)),
                   pltpu.SemaphoreType.DMA((n,)),
                   *[pltpu.HBM(s.shape, s.dtype) for s in sends],
                   *[pltpu.HBM(l.shape, l.dtype) for l in lands],
                   jax.ShapeDtypeStruct((8, 128), F32)),
        in_specs=[_HBM] * (2 * n) + [_ANY],
        out_specs=(_SEM, _SEM, _SEM, *([_HBM] * (2 * n)), pl.BlockSpec(memory_space=pltpu.VMEM)),
        input_output_aliases={i: 3 + i for i in range(2 * n)},
        compiler_params=pltpu.CompilerParams(has_side_effects=_DATAFLOW),
    )(*[pltpu.with_memory_space_constraint(v, pltpu.HBM) for v in list(sends) + lands], after)
    return (outs[0], outs[1], outs[2], list(outs[3:3 + n]), list(outs[3 + n:3 + 2 * n])), outs[-1]


def _exchange_wait(handle, after, name):
    send_sems, recv_sems, local_sems, sends, lands = handle
    n = len(sends)

    def body(*refs):
        s_refs, l_refs = refs[:n], refs[n:2 * n]
        send_sems, recv_sems, local_sems = refs[2 * n], refs[2 * n + 1], refs[2 * n + 2]
        x, y, c = lax.axis_index("x"), lax.axis_index("y"), lax.axis_index("c")
        for a in range(n):
            pltpu.make_async_copy(s_refs[a].at[0], l_refs[a].at[0], local_sems.at[a]).wait()
            for k in range(1, N_DEV):
                copy = pltpu.make_async_remote_copy(
                    src_ref=s_refs[a].at[0], dst_ref=l_refs[a].at[0],
                    send_sem=send_sems.at[7 * a + k - 1], recv_sem=recv_sems.at[7 * a + k - 1],
                    device_id=_peer(k, x, y, c), device_id_type=pl.DeviceIdType.MESH)
                copy.wait_send()
                copy.wait_recv()

    outs = pl.pallas_call(
        body, name=name,
        out_shape=(*[pltpu.HBM(s.shape, s.dtype) for s in sends], *[pltpu.HBM(l.shape, l.dtype) for l in lands]),
        in_specs=[_HBM] * (2 * n) + [_SEM, _SEM, _SEM] + [_ANY] * len(after),
        out_specs=tuple([_HBM] * (2 * n)),
        input_output_aliases={i: i for i in range(2 * n)},
        compiler_params=pltpu.CompilerParams(has_side_effects=_DATAFLOW),
    )(*sends, *lands, send_sems, recv_sems, local_sems, *after)
    return list(outs[n:])


def _chips_of(x, y):
    return [(1 - x, y), (x, 1 - y), (1 - x, 1 - y)]


def _gather2_start(shards, after, name):
    n = len(shards)
    lands = [lax.empty((N_DEV,) + s.shape, s.dtype) for s in shards]

    def body(*refs):
        x_refs, l_refs = refs[:n], refs[n:2 * n]
        send_sems, recv_sems, local_sems, token = refs[2 * n + 1], refs[2 * n + 2], refs[2 * n + 3], refs[-1]
        x, y, c = lax.axis_index("x"), lax.axis_index("y"), lax.axis_index("c")
        me = 4 * x + 2 * y + c
        peers = [(x, y, 1 - c)] + [(*chip, c) for chip in _chips_of(x, y)]
        for a in range(n):
            pltpu.make_async_copy(x_refs[a], l_refs[a].at[me], local_sems.at[a]).start()
            for k, peer in enumerate(peers):
                pltpu.make_async_remote_copy(
                    src_ref=x_refs[a], dst_ref=l_refs[a].at[me],
                    send_sem=send_sems.at[4 * a + k], recv_sem=recv_sems.at[4 * a + k],
                    device_id=peer, device_id_type=pl.DeviceIdType.MESH).start()
        token[...] = jnp.zeros_like(token)

    outs = pl.pallas_call(
        body, name=name,
        out_shape=(pltpu.SemaphoreType.DMA((4 * n,)), pltpu.SemaphoreType.DMA((4 * n,)),
                   pltpu.SemaphoreType.DMA((n,)),
                   *[pltpu.HBM(s.shape, s.dtype) for s in shards],
                   *[pltpu.HBM(l.shape, l.dtype) for l in lands],
                   jax.ShapeDtypeStruct((8, 128), F32)),
        in_specs=[_HBM] * (2 * n) + [_ANY],
        out_specs=(_SEM, _SEM, _SEM, *([_HBM] * (2 * n)), pl.BlockSpec(memory_space=pltpu.VMEM)),
        input_output_aliases={i: 3 + i for i in range(2 * n)},
        compiler_params=pltpu.CompilerParams(has_side_effects=_DATAFLOW),
    )(*[pltpu.with_memory_space_constraint(v, pltpu.HBM) for v in list(shards) + lands], after)
    return (outs[0], outs[1], outs[2], list(outs[3:3 + n]), list(outs[3 + n:3 + 2 * n])), outs[-1]


def _gather2_forward(handle, after, name):
    send_sems, recv_sems, local_sems, shards, lands = handle
    n = len(shards)

    def body(*refs):
        x_refs, l_refs = refs[:n], refs[n:2 * n]
        send_sems, recv_sems = refs[2 * n], refs[2 * n + 1]
        out0 = 2 * n + 2 + len(after)
        fwd_send, fwd_recv, token = refs[out0], refs[out0 + 1], refs[-1]
        x, y, c = lax.axis_index("x"), lax.axis_index("y"), lax.axis_index("c")
        for a in range(n):
            for j, chip in enumerate(_chips_of(x, y)):
                block = l_refs[a].at[4 * chip[0] + 2 * chip[1] + c]
                pltpu.make_async_remote_copy(
                    src_ref=x_refs[a], dst_ref=block, send_sem=send_sems.at[4 * a + 1 + j],
                    recv_sem=recv_sems.at[4 * a + 1 + j], device_id=(*chip, c),
                    device_id_type=pl.DeviceIdType.MESH).wait_recv()
                pltpu.make_async_remote_copy(
                    src_ref=block, dst_ref=block, send_sem=fwd_send.at[3 * a + j], recv_sem=fwd_recv.at[3 * a + j],
                    device_id=(x, y, 1 - c), device_id_type=pl.DeviceIdType.MESH).start()
        token[...] = jnp.zeros_like(token)

    outs = pl.pallas_call(
        body, name=name,
        out_shape=(pltpu.SemaphoreType.DMA((3 * n,)), pltpu.SemaphoreType.DMA((3 * n,)),
                   *[pltpu.HBM(s.shape, s.dtype) for s in shards], *[pltpu.HBM(l.shape, l.dtype) for l in lands],
                   jax.ShapeDtypeStruct((8, 128), F32)),
        in_specs=[_HBM] * (2 * n) + [_SEM, _SEM] + [_ANY] * len(after),
        out_specs=(_SEM, _SEM, *([_HBM] * (2 * n)), pl.BlockSpec(memory_space=pltpu.VMEM)),
        input_output_aliases={i: 2 + i for i in range(2 * n)},
        compiler_params=pltpu.CompilerParams(has_side_effects=_DATAFLOW),
    )(*shards, *lands, send_sems, recv_sems, *after)
    handle = (send_sems, recv_sems, local_sems, outs[0], outs[1], list(outs[2:2 + n]), list(outs[2 + n:2 + 2 * n]))
    return handle, outs[-1]


def _gather2_wait(handle, after, name):
    send_sems, recv_sems, local_sems, fwd_send, fwd_recv, shards, lands = handle
    n = len(shards)

    def body(*refs):
        x_refs, l_refs = refs[:n], refs[n:2 * n]
        send_sems, recv_sems, local_sems, fwd_send, fwd_recv = refs[2 * n:2 * n + 5]
        x, y, c = lax.axis_index("x"), lax.axis_index("y"), lax.axis_index("c")
        sibling = (x, y, 1 - c)

        def copy(a, send_sem, recv_sem):
            return pltpu.make_async_remote_copy(
                src_ref=x_refs[a], dst_ref=l_refs[a].at[0], send_sem=send_sem, recv_sem=recv_sem,
                device_id=sibling, device_id_type=pl.DeviceIdType.MESH)

        for a in range(n):
            pltpu.make_async_copy(x_refs[a], l_refs[a].at[0], local_sems.at[a]).wait()
            for k in range(4):
                copy(a, send_sems.at[4 * a + k], recv_sems.at[4 * a + k]).wait_send()
            copy(a, send_sems.at[4 * a], recv_sems.at[4 * a]).wait_recv()
            for j in range(3):
                passed = copy(a, fwd_send.at[3 * a + j], fwd_recv.at[3 * a + j])
                passed.wait_send()
                passed.wait_recv()

    outs = pl.pallas_call(
        body, name=name,
        out_shape=(*[pltpu.HBM(s.shape, s.dtype) for s in shards], *[pltpu.HBM(l.shape, l.dtype) for l in lands]),
        in_specs=[_HBM] * (2 * n) + [_SEM] * 5 + [_ANY] * len(after),
        out_specs=tuple([_HBM] * (2 * n)),
        input_output_aliases={i: i for i in range(2 * n)},
        compiler_params=pltpu.CompilerParams(has_side_effects=_DATAFLOW),
    )(*shards, *lands, send_sems, recv_sems, local_sems, fwd_send, fwd_recv, *after)
    return list(outs[n:])


def _adamw(recv, w, m, v, name):
    n, rows, width = recv.shape
    tr = _pick(rows, max(8, ADAM_BLOCK_BYTES // (n * width * recv.dtype.itemsize)), 8)
    c_m = 1.0 - ADAM_B1 ** ADAM_STEP
    c_v = 1.0 - ADAM_B2 ** ADAM_STEP

    def body(r_ref, w_ref, m_ref, v_ref, g_ref, d_ref, nm_ref, nv_ref):
        g = r_ref[0].astype(F32)
        for i in range(1, n):
            g = g + r_ref[i].astype(F32)
        m2 = ADAM_B1 * m_ref[...] + (1.0 - ADAM_B1) * g
        v2 = ADAM_B2 * v_ref[...] + (1.0 - ADAM_B2) * (g * g)
        m_hat = m2 / c_m
        v_hat = v2 / c_v
        g_ref[...] = g
        d_ref[...] = -ADAM_LR * (m_hat / (jnp.sqrt(v_hat) + ADAM_EPS) + ADAM_WD * w_ref[...])
        nm_ref[...] = m2
        nv_ref[...] = v2

    blk = _rows(tr, width)
    shp = jax.ShapeDtypeStruct((rows, width), F32)
    return pl.pallas_call(
        body, name=name, out_shape=(shp, shp, shp, shp), grid=(rows // tr,),
        in_specs=[pl.BlockSpec((n, tr, width), lambda i: (0, i, 0)), blk, blk, blk],
        out_specs=(blk, blk, blk, blk), compiler_params=_params())(recv, w, m, v)


def _join_cols(gathered):
    n, r, c = gathered.shape
    return gathered.transpose(1, 0, 2).reshape(r, n * c)


def _split_cols(full):
    r, c = full.shape
    return full.reshape(r, N_DEV, c // N_DEV).transpose(1, 0, 2)


def _adamw_small(items, name):
    n = len(items)
    c_m = 1.0 - ADAM_B1 ** ADAM_STEP
    c_v = 1.0 - ADAM_B2 ** ADAM_STEP

    def body(*refs):
        ins, outs = refs[:4 * n], refs[4 * n:]
        for k in range(n):
            r_ref, w_ref, m_ref, v_ref = ins[4 * k:4 * k + 4]
            g = r_ref[0].astype(F32)
            for i in range(1, N_DEV):
                g = g + r_ref[i].astype(F32)
            m2 = ADAM_B1 * m_ref[...] + (1.0 - ADAM_B1) * g
            v2 = ADAM_B2 * v_ref[...] + (1.0 - ADAM_B2) * (g * g)
            outs[4 * k][...] = g
            outs[4 * k + 1][...] = -ADAM_LR * ((m2 / c_m) / (jnp.sqrt(v2 / c_v) + ADAM_EPS) + ADAM_WD * w_ref[...])
            outs[4 * k + 2][...] = m2
            outs[4 * k + 3][...] = v2

    vmem = pl.BlockSpec(memory_space=pltpu.VMEM)
    flat = pl.pallas_call(
        body, name=name,
        out_shape=tuple(jax.ShapeDtypeStruct(w.shape, F32) for _, w, _, _ in items for _ in range(4)),
        in_specs=[vmem] * (4 * n), out_specs=tuple([vmem] * (4 * n)),
        compiler_params=pltpu.CompilerParams(vmem_limit_bytes=VMEM_LIMIT_BYTES),
    )(*[a for item in items for a in item])
    return [tuple(flat[4 * k:4 * k + 4]) for k in range(n)]


def _discretise(lam_re, lam_im, log_dt, b_re, b_im):
    dt = jnp.exp(log_dt)
    mag = jnp.exp(lam_re * dt)
    ab_re = mag * jnp.cos(lam_im * dt)
    ab_im = mag * jnp.sin(lam_im * dt)
    nr = ab_re - 1.0
    ni = ab_im
    den = lam_re * lam_re + lam_im * lam_im
    coef_re = (nr * lam_re + ni * lam_im) / den
    coef_im = (ni * lam_re - nr * lam_im) / den
    return ab_re, ab_im, coef_re * b_re - coef_im * b_im, coef_re * b_im + coef_im * b_re


def _s5_discretise(operands, name, token):
    def body(*refs):
        for out_ref, v in zip(refs[6:], _discretise(*[r[...] for r in refs[:5]])):
            out_ref[...] = v

    vmem = pl.BlockSpec(memory_space=pltpu.VMEM)
    shape = jax.ShapeDtypeStruct(operands[0].shape, F32)
    return pl.pallas_call(body, name=name, out_shape=(shape,) * 4, in_specs=[vmem] * 6,
                          out_specs=(vmem,) * 4)(*operands, token)


def _s5_discretise_bwd(operands, cotangents, name):
    def body(*refs):
        _, vjp = jax.vjp(_discretise, *[r[...] for r in refs[:5]])
        for out_ref, v in zip(refs[9:], vjp(tuple(r[...] for r in refs[5:9]))):
            out_ref[...] = v

    vmem = pl.BlockSpec(memory_space=pltpu.VMEM)
    shape = jax.ShapeDtypeStruct(operands[0].shape, F32)
    return pl.pallas_call(body, name=name, out_shape=(shape,) * 5, in_specs=[vmem] * 9,
                          out_specs=(vmem,) * 5)(*operands, *cotangents)


def _same_group(gl):
    return jnp.eye(gl, dtype=bool)[None, :, None, :, None]


def _super_groups_in(bb, gl):
    g, p, i = bb.shape
    blocks = bb.reshape(g // gl, gl, p, i).transpose(0, 1, 3, 2)[:, :, :, None, :]
    return jnp.where(_same_group(gl), blocks, 0.0).reshape(g // gl, gl * i, gl * p)


def _super_groups_in_take(dense, gl, p, i):
    n_sg = dense.shape[0]
    blocks = jnp.where(_same_group(gl), dense.reshape(n_sg, gl, i, gl, p), 0.0).sum(axis=3)
    return blocks.transpose(0, 1, 3, 2).reshape(n_sg * gl, p, i)


def _super_groups_out(cc, gl):
    g, i, p = cc.shape
    blocks = cc.reshape(g // gl, gl, i, p).transpose(0, 1, 3, 2)[:, :, :, None, :]
    return jnp.where(_same_group(gl), blocks, 0.0).reshape(g // gl, gl * p, gl * i)


def _super_groups_out_take(dense, gl, i, p):
    n_sg = dense.shape[0]
    blocks = jnp.where(_same_group(gl), dense.reshape(n_sg, gl, p, gl, i), 0.0).sum(axis=3)
    return blocks.transpose(0, 1, 3, 2).reshape(n_sg * gl, i, p)


def _perm_rows(z, n_seq):
    t, w = z.shape
    steps = t // n_seq // N_SEG
    return z.reshape(n_seq, N_SEG, steps, w).transpose(0, 2, 1, 3).reshape(t, w)


def _unperm_rows(z, n_seq):
    t, w = z.shape
    steps = t // n_seq // N_SEG
    return z.reshape(n_seq, steps, N_SEG, w).transpose(0, 2, 1, 3).reshape(t, w)


def kernel(*args):
    assert len(args) == len(INPUT_NAMES)
    inp = dict(zip(INPUT_NAMES, args))
    depth = inp["ffn1_w_in"].shape[0]
    assert depth == 1
    alpha = (2.0 * depth) ** 0.25

    n_seq, seq, d_model = inp["x"].shape
    t = n_seq * seq
    x = inp["x"].reshape(t, d_model)
    target = inp["loss_target"].reshape(t, d_model)
    wts = {n: inp[n][0] if n in SHARDED else inp[n] for n in WEIGHTS}
    mom = {n: inp["m_" + n][0] if n in SHARDED else inp["m_" + n] for n in WEIGHTS}
    var = {n: inp["v_" + n][0] if n in SHARDED else inp["v_" + n] for n in WEIGHTS}

    full = {}

    def place(n, g):
        if n in ("ffn1_w_in", "ffn2_w_in"):
            full[n] = g.reshape((2, N_DEV // 2) + g.shape[1:])
        elif n in ("ffn1_w_out", "ffn2_w_out"):
            full[n] = g.reshape(N_DEV // 2, 2 * g.shape[1], g.shape[2])
        elif n in COL_SHARDED:
            full[n] = _join_cols(g)
        else:
            full[n] = g.reshape(N_DEV * g.shape[1], g.shape[2])

    w16 = dict(zip(GATHER_FIRST, _to_bf16([wts[n] for n in GATHER_FIRST], "cast_ffn1")))
    gather_first, gather_first_token = _gather2_start([w16[n] for n in GATHER_FIRST], w16[GATHER_FIRST[0]],
                                                      "gather_first_start")
    later = tuple(n for n in SHARDED if n not in GATHER_FIRST)
    casts = _to_bf16([wts[n] for n in later] + [x, inp["p"][0].reshape(t, -1)], "cast_rest",
                     token=gather_first_token)
    w16.update(zip(later, casts))
    x16, p16 = casts[-2:]

    def gather_start(names, after, name):
        return _exchange_start([w16[n][None] for n in names], after, name)

    def gather_wait(names, handle, after, name):
        for n, g in zip(names, _exchange_wait(handle, after, name)):
            place(n, g)

    def gather2_wait(names, handle, after, name):
        for n, g in zip(names, _gather2_wait(handle, after, name)):
            place(n, g)


    def row(v):
        return v.reshape(1, -1)

    _, n_groups, n_state = wts["ssm_lambda_re"].shape
    n_ch = wts["ssm_b_re"].shape[3]
    disc_in = (jnp.repeat(wts["ssm_lambda_re"][0], n_ch, axis=1), jnp.repeat(wts["ssm_lambda_im"][0], n_ch, axis=1),
               jnp.broadcast_to(wts["ssm_log_dt"][0][:, None], (n_groups, n_state * n_ch)),
               wts["ssm_b_re"][0].reshape(n_groups, -1), wts["ssm_b_im"][0].reshape(n_groups, -1))
    ab_re, ab_im, bb_re, bb_im = _s5_discretise(disc_in, "s5_discretise", gather_first_token)
    a_re, a_im = row(ab_re[:, ::n_ch]), row(ab_im[:, ::n_ch])
    bb_re, bb_im = bb_re.reshape(n_groups, n_state, n_ch), bb_im.reshape(n_groups, n_state, n_ch)
    gl = S5_CHANNELS_PER_STEP // n_ch
    b_sg_re = _super_groups_in(bb_re, gl).astype(BF16)
    b_sg_im = _super_groups_in(bb_im, gl).astype(BF16)
    c_sg_re = _super_groups_out(wts["ssm_c_re"][0], gl).astype(BF16)
    c_sg_im = _super_groups_out(-wts["ssm_c_im"][0], gl).astype(BF16)

    ws = wts["gmlp_w_s"][0]
    n_heads = ws.shape[0]
    d_gmlp = wts["gmlp_ln_g"].shape[1]
    causal = jnp.tril(jnp.ones((CHUNK, CHUNK), dtype=bool))
    ws_masked = jnp.where(causal[None], ws, 0.0).astype(BF16)
    ws_masked_t = ws_masked.transpose(0, 2, 1)
    bs_cols = jnp.repeat(wts["gmlp_b_s"][0].T, d_gmlp // n_heads, axis=1)
    d_ssm = n_groups * n_ch
    assert d_ssm == d_gmlp and d_model == 2 * d_ssm

    prepared = [x16, b_sg_re, b_sg_im, c_sg_re, c_sg_im, ws_masked_t, bs_cols]
    gather_first, _ = _gather2_forward(gather_first, prepared, "gather_first_forward")
    gather2_wait(GATHER_FIRST, gather_first, prepared, "gather_first_wait")
    gather_early, gather_early_token = _gather2_start([w16[n] for n in GATHER_EARLY], full["ffn1_w_in"],
                                                      "gather_early_start")
    h1, a1 = _ffn_in(x16, full["ffn1_w_in"], "ffn1_in", token=gather_early_token)
    gather_early, _ = _gather2_forward(gather_early, [a1], "gather_early_forward")
    gather2_wait(GATHER_EARLY, gather_early, [a1], "gather_early_wait")
    r1, x1, x1_16 = _ffn_out_ln(a1, full["ffn1_w_out"], x, row(wts["ln1_g"]), row(wts["ln1_b"]), alpha,
                                "ffn1_out_ln")

    gather_mid, token = gather_start(GATHER_MID, x1_16, "gather_mid_start")
    gather_last, token = _gather2_start([w16[n] for n in GATHER_LAST], token, "gather_last_start")
    proj = _mm(x1_16, full["mix_w_in"], name="mix_in", token=token)
    za_p = _perm_rows(proj[:, :d_ssm], n_seq)
    h_re, h_im, yssm = _s5_fwd(za_p, b_sg_re, b_sg_im, a_re, a_im, c_sg_re, c_sg_im, n_seq, "s5_fwd")
    gather_wait(GATHER_MID, gather_mid, [yssm], "gather_mid_wait")
    s5out_p = _s5_post_fwd(yssm, za_p, row(wts["ssm_d"]), full["ssm_glu_w"], row(wts["ssm_glu_b"]),
                           "s5_post")
    s5out = _unperm_rows(s5out_p, n_seq)
    gm = _gmlp_fwd(proj, row(wts["gmlp_ln_g"]), row(wts["gmlp_ln_b"]), ws_masked, bs_cols, "gmlp")
    m_mix, y_a, y_b = _merge_fwd(s5out, gm, proj, full["up_a"], full["up_b"], "merge")
    gather_last, token = _gather2_forward(gather_last, [m_mix], "gather_last_forward")
    r2, x2, x2_16 = _ffn_out_ln(m_mix[None], full["mix_w_out"][None], x1, row(wts["ln2_g"]), row(wts["ln2_b"]),
                                alpha, "mix_out_ln2", scale=1.0, token=token)

    gather2_wait(GATHER_LAST, gather_last, [x2_16], "gather_last_wait")
    h2, a2 = _ffn_in(x2_16, full["ffn2_w_in"], "ffn2_in")
    r3, x3, _ = _ffn_out_ln(a2, full["ffn2_w_out"], x2, row(wts["ln3_g"]), row(wts["ln3_b"]), alpha,
                            "ffn2_out_ln")

    small = {}
    send = {}

    def lane_dense(n, v):
        return v.reshape(v.shape[:2] + (-1,)) if n in ("ssm_b_re", "ssm_b_im") else v

    def on_wire(n, g):
        g = lane_dense(n, g)
        return (g.astype(BF16) if n in SMALL_BF16 else g)[None]
    loss_parts, dr3, dr3_h, dw_gate, dw_proj, dg, db = _head(
        x3, r3, row(wts["ln3_g"]), p16, target, full["ple_w_gate"], full["ple_w_proj"], 0.5, "head")
    loss_mine = jnp.full((1, 1, 8, 128), jnp.sum(loss_parts[::8, 0]), F32)
    send["ple_w_gate"] = dw_gate.astype(BF16).reshape(N_DEV, -1, d_model)
    send["ple_w_proj"] = _split_cols(dw_proj.astype(BF16))
    small["ln3_g"], small["ln3_b"] = dg.sum(0, keepdims=True), db.sum(0, keepdims=True)
    dh2 = _ffn_out_dx(dr3_h, full["ffn2_w_out"], h2, "ffn2_out_dx")
    dw = _ffn_out_dw(a2, dr3_h, "ffn2_out_dw")
    send["ffn2_w_out"] = dw.reshape(N_DEV, -1, d_model)
    dw = _ffn_in_dw(x2_16, dh2, "ffn2_in_dw")
    send["ffn2_w_in"] = dw.reshape((N_DEV,) + dw.shape[2:])
    group1 = ("ffn2_w_in", "ffn2_w_out", "ple_w_gate", "ple_w_proj")
    exchange1, token = _exchange_start(
        [send[n] for n in group1] + [on_wire(n, small[n]) for n in SMALL_HEAD] + [loss_mine], dh2,
        "grad_exchange1_start")
    dr2, dr2_h, dg, db = _ffn_in_dx_ln(dh2, full["ffn2_w_in"], r2, dr3, alpha, row(wts["ln2_g"]), 1.0,
                                       "ffn2_in_dx_ln2_bwd", token=token)
    small["ln2_g"], small["ln2_b"] = dg.sum(0, keepdims=True), db.sum(0, keepdims=True)
    dm = _mm(dr2_h, full["mix_w_out"], tb=True, name="mix_out_dx")
    send["mix_w_out"] = _mm(m_mix, dr2_h, ta=True, name="mix_out_dw", out_dtype=BF16).reshape(
        N_DEV, -1, d_model)
    dga, dgb, ds5out, dgm, dw_a, dw_b = _merge_bwd(
        dm, y_a, y_b, s5out, gm, proj, full["up_a"], full["up_b"], "merge_bwd")
    send["up_a"] = _split_cols(dw_a.astype(BF16))
    send["up_b"] = _split_cols(dw_b.astype(BF16))

    dzu, dzv, dws, dbs_cols, dg, db = _gmlp_bwd(
        proj, dgm, row(wts["gmlp_ln_g"]), row(wts["gmlp_ln_b"]), ws_masked, ws_masked_t, bs_cols,
        "gmlp_bwd")
    small["gmlp_ln_g"], small["gmlp_ln_b"] = dg.sum(0, keepdims=True), db.sum(0, keepdims=True)
    small["gmlp_w_s"] = jnp.where(causal[None], dws, 0.0)[None]
    small["gmlp_b_s"] = dbs_cols.reshape(CHUNK, n_heads, -1).sum(-1).T[None]

    dy_p, dza_skip, dw_glu, dd, dgb_glu = _s5_post_bwd(
        yssm, za_p, _perm_rows(ds5out, n_seq), row(wts["ssm_d"]), full["ssm_glu_w"], row(wts["ssm_glu_b"]),
        "s5_post_bwd")
    send["ssm_glu_w"] = dw_glu.astype(BF16).reshape(N_DEV, -1, d_ssm)
    small["ssm_d"], small["ssm_glu_b"] = dd.sum(0, keepdims=True), dgb_glu.sum(0, keepdims=True)
    dza_p, dbb_re, dbb_im, dc_re, dc_im, da_re, da_im = _s5_bwd(
        dy_p, dza_skip, h_re, h_im, za_p, b_sg_re, b_sg_im, a_re, a_im, c_sg_re, c_sg_im, n_seq, "s5_bwd")
    small["ssm_c_re"] = _super_groups_out_take(dc_re, gl, n_ch, n_state)[None]
    small["ssm_c_im"] = -_super_groups_out_take(dc_im, gl, n_ch, n_state)[None]
    dbb_re = _super_groups_in_take(dbb_re, gl, n_state, n_ch)
    dbb_im = _super_groups_in_take(dbb_im, gl, n_state, n_ch)
    def first_channel(v):
        placed = jnp.pad(v.sum(0).reshape(n_groups, n_state, 1), ((0, 0), (0, 0), (0, n_ch - 1)))
        return placed.reshape(n_groups, -1)

    cotangents = (first_channel(da_re), first_channel(da_im), dbb_re.reshape(n_groups, -1),
                  dbb_im.reshape(n_groups, -1))
    d_lre, d_lim, d_ldt, d_bre, d_bim = _s5_discretise_bwd(disc_in, cotangents, "s5_discretise_bwd")
    per_state = (n_groups, n_state, n_ch)
    small["ssm_lambda_re"] = d_lre.reshape(per_state).sum(-1)[None]
    small["ssm_lambda_im"] = d_lim.reshape(per_state).sum(-1)[None]
    small["ssm_log_dt"] = d_ldt.sum(-1)[None]
    small["ssm_b_re"] = d_bre.reshape((1,) + per_state)
    small["ssm_b_im"] = d_bim.reshape((1,) + per_state)

    dproj = jnp.concatenate([_unperm_rows(dza_p, n_seq), dzu, dzv, dga, dgb], axis=1)
    group2 = ("mix_w_out", "up_a", "up_b", "ssm_glu_w")
    exchange2, token = _exchange_start(
        [send[n] for n in group2] + [on_wire(n, small[n]) for n in SMALL_MID], dproj, "grad_exchange2_start")
    send["mix_w_in"] = _split_cols(_mm(x1_16, dproj, ta=True, name="mix_in_dw", out_dtype=BF16, token=token))
    exchange3, token = _exchange_start([send["mix_w_in"]], dproj, "grad_exchange3_start")
    dr1, dr1_h, dg, db = _mm_ln_bwd(dproj, full["mix_w_in"], r1, dr2, alpha, row(wts["ln1_g"]), 0.5,
                                    "mix_in_dx_ln1_bwd", token=token)
    small["ln1_g"], small["ln1_b"] = dg.sum(0, keepdims=True), db.sum(0, keepdims=True)
    dw = _ffn_out_dw(a1, dr1_h, "ffn1_out_dw")
    send["ffn1_w_out"] = dw.reshape(N_DEV, -1, d_model)
    exchange4, token = _exchange_start([send["ffn1_w_out"]] + [on_wire(n, small[n]) for n in SMALL_LATE], dr1_h,
                                       "grad_exchange4_start")
    dh1 = _ffn_out_dx(dr1_h, full["ffn1_w_out"], h1, "ffn1_out_dx", token=token)
    dw = _ffn_in_dw(x16, dh1, "ffn1_in_dw")
    send["ffn1_w_in"] = dw.reshape((N_DEV,) + dw.shape[2:])
    exchange5, token = _exchange_start([send["ffn1_w_in"]], dh1, "grad_exchange5_start")
    grad_x = _ffn_in_dx(dh1, full["ffn1_w_in"], dr1, alpha, "ffn1_in_dx", token=token)

    results = {}

    def update(names, recv, prefix):
        for n, r in zip(names, recv):
            results[n] = _adamw(r, wts[n], mom[n], var[n], prefix + n)

    def update_small(names, recv, name):
        items = [(r, lane_dense(n, wts[n]), lane_dense(n, mom[n]), lane_dense(n, var[n])) for n, r in zip(names, recv)]
        for n, outs in zip(names, _adamw_small(items, name)):
            results[n] = tuple(o.reshape(wts[n].shape) for o in outs)

    def done(names):
        return [results[n][3] for n in names]

    recv = _exchange_wait(exchange1, [grad_x], "grad_exchange1_wait")
    update(group1, recv[:len(group1)], "adamw_")
    update_small(SMALL_HEAD, recv[len(group1):-1], "adamw_ln3")
    loss = jnp.sum(recv[-1][:, 0, 0, 0])
    recv = _exchange_wait(exchange2, done(group1 + SMALL_HEAD), "grad_exchange2_wait")
    update(group2, recv[:len(group2)], "adamw_")
    update_small(SMALL_MID, recv[len(group2):], "adamw_small")
    recv = _exchange_wait(exchange3, done(group2 + SMALL_MID), "grad_exchange3_wait")
    update(("mix_w_in",), recv, "adamw_")
    recv = _exchange_wait(exchange4, done(("mix_w_in",)), "grad_exchange4_wait")
    update(("ffn1_w_out",), recv[:1], "adamw_")
    update_small(SMALL_LATE, recv[1:], "adamw_ln1")
    recv = _exchange_wait(exchange5, done(("ffn1_w_out",) + SMALL_LATE), "grad_exchange5_wait")
    update(("ffn1_w_in",), recv, "adamw_")

    outs = [loss, grad_x.reshape(n_seq, seq, d_model)]
    for k in range(4):
        outs += [results[n][k][None] if n in SHARDED else results[n][k] for n in WEIGHTS]
    return tuple(outs)
```

```python
import math

import jax
import jax.numpy as jnp
from jax import lax
from jax.experimental import pallas as pl
from jax.experimental.pallas import tpu as pltpu

F32 = jnp.float32
BF16 = jnp.bfloat16

N_DEV = 8
LN_EPS = 1e-5
CHUNK = 128
N_SEG = 8
S5_CHANNELS_PER_STEP = 128
VMEM_LIMIT_BYTES = 56 * 1024 * 1024
MM_OPERAND_BLOCK_BYTES = 8 * 1024 * 1024
ADAM_BLOCK_BYTES = 6 * 1024 * 1024

ADAM_LR = 0.001
ADAM_B1 = 0.9
ADAM_B2 = 0.999
ADAM_EPS = 1e-08
ADAM_WD = 0.01
ADAM_STEP = 10

COL_SHARDED = ("ffn1_w_in", "mix_w_in", "up_a", "up_b", "ffn2_w_in", "ple_w_proj")
SHARDED = ("ffn1_w_in", "ffn1_w_out", "mix_w_in", "ssm_glu_w", "up_a", "up_b", "mix_w_out",
           "ffn2_w_in", "ffn2_w_out", "ple_w_proj", "ple_w_gate")
WEIGHTS = ("ffn1_w_in", "ffn1_w_out", "ln1_g", "ln1_b", "mix_w_in", "ssm_lambda_re", "ssm_lambda_im",
           "ssm_log_dt", "ssm_b_re", "ssm_b_im", "ssm_c_re", "ssm_c_im", "ssm_d", "ssm_glu_w",
           "ssm_glu_b", "gmlp_ln_g", "gmlp_ln_b", "gmlp_w_s", "gmlp_b_s", "up_a", "up_b", "mix_w_out",
           "ln2_g", "ln2_b", "ffn2_w_in", "ffn2_w_out", "ln3_g", "ln3_b", "ple_w_proj", "ple_w_gate")
GATHER_FIRST = ("ffn1_w_in",)
GATHER_EARLY = ("ffn1_w_out", "mix_w_in")
GATHER_MID = ("ssm_glu_w", "up_a", "up_b", "mix_w_out")
GATHER_LAST = ("ffn2_w_in", "ffn2_w_out", "ple_w_proj", "ple_w_gate")
SMALL = tuple(n for n in WEIGHTS if n not in SHARDED)
SMALL_HEAD = ("ln3_g", "ln3_b")
SMALL_LATE = ("ln1_g", "ln1_b")
SMALL_MID = tuple(n for n in SMALL if n not in SMALL_HEAD + SMALL_LATE)
SMALL_BF16 = ("gmlp_w_s", "ssm_b_re", "ssm_b_im", "ssm_c_re", "ssm_c_im")
INPUT_NAMES = ("x", "p") + WEIGHTS + ("loss_target",) + tuple("m_" + n for n in WEIGHTS) + tuple(
    "v_" + n for n in WEIGHTS)


def _pick(dim, pref, mult=128):
    if dim <= pref:
        return dim
    d = (pref // mult) * mult
    while d >= mult:
        if dim % d == 0:
            return d
        d -= mult
    return dim


def _params(n_axes=1):
    return pltpu.CompilerParams(dimension_semantics=("arbitrary",) * n_axes,
                                vmem_limit_bytes=VMEM_LIMIT_BYTES)


def _sigmoid(v):
    return 0.5 * jnp.tanh(0.5 * v) + 0.5


_GELU_C = math.sqrt(2.0 / math.pi)


def _gelu(v):
    return 0.5 * v * (1.0 + jnp.tanh(_GELU_C * (v + 0.044715 * (v * v * v))))


def _gelu_grad(v):
    t = jnp.tanh(_GELU_C * (v + 0.044715 * (v * v * v)))
    return 0.5 * (1.0 + t) + 0.5 * v * (1.0 - t * t) * (_GELU_C * (1.0 + 3.0 * 0.044715 * v * v))


def _ln_stats(r):
    mu = jnp.mean(r, axis=-1, keepdims=True)
    xc = r - mu
    var = jnp.mean(xc * xc, axis=-1, keepdims=True)
    rstd = lax.rsqrt(var + LN_EPS)
    return xc * rstd, rstd


def _ln_bwd(dy, xhat, rstd, g):
    dxh = dy * g
    m1 = jnp.mean(dxh, axis=-1, keepdims=True)
    m2 = jnp.mean(dxh * xhat, axis=-1, keepdims=True)
    return rstd * (dxh - m1 - xhat * m2)


def _fold8(v):
    rows, w = v.shape
    return jnp.sum(v.reshape(rows // 8, 8, w), axis=0)


def _dot(a, b, ta=False, tb=False):
    dn = (((0 if ta else 1,), (1 if tb else 0,)), ((), ()))
    return lax.dot_general(a.astype(BF16), b.astype(BF16), dn, preferred_element_type=F32)


_TOKEN = pl.BlockSpec((8, 128), lambda *_: (0, 0))


def _mm(a, b, *, name, ta=False, tb=False, out_dtype=F32, add=None, add_scale=1.0, token=None,
        tm=2048, tn=512, tk=4096):
    m_dim = a.shape[1] if ta else a.shape[0]
    k_dim = a.shape[0] if ta else a.shape[1]
    n_dim = b.shape[0] if tb else b.shape[1]
    assert (b.shape[1] if tb else b.shape[0]) == k_dim, (name, a.shape, b.shape)
    tk = _pick(k_dim, tk)
    tm = min(tm, max(256, MM_OPERAND_BLOCK_BYTES // (tk * a.dtype.itemsize)))
    tm, tn = _pick(m_dim, tm), _pick(n_dim, tn)
    nk = k_dim // tk
    has_add = add is not None

    def finish(r, add_ref, o_ref):
        if has_add:
            r = r + add_scale * add_ref[...].astype(F32)
        o_ref[...] = r.astype(out_dtype)

    def body(*refs):
        a_ref, b_ref = refs[:2]
        add_ref = refs[2] if has_add else None
        o_ref = refs[n_in]
        if nk == 1:
            finish(_dot(a_ref[...], b_ref[...], ta, tb), add_ref, o_ref)
            return
        acc_ref = refs[-1]
        k = pl.program_id(2)

        @pl.when(k == 0)
        def _():
            acc_ref[...] = jnp.zeros_like(acc_ref)

        acc_ref[...] += _dot(a_ref[...], b_ref[...], ta, tb)

        @pl.when(k == nk - 1)
        def _():
            finish(acc_ref[...], add_ref, o_ref)

    a_spec = (pl.BlockSpec((tk, tm), lambda i, j, k: (k, i)) if ta
              else pl.BlockSpec((tm, tk), lambda i, j, k: (i, k)))
    b_spec = (pl.BlockSpec((tn, tk), lambda i, j, k: (j, k)) if tb
              else pl.BlockSpec((tk, tn), lambda i, j, k: (k, j)))
    in_specs = [a_spec, b_spec]
    operands = [a, b]
    if has_add:
        in_specs.append(pl.BlockSpec((tm, tn), lambda i, j, k: (i, j)))
        operands.append(add)
    if token is not None:
        in_specs.append(_TOKEN)
        operands.append(token)
    n_in = len(operands)
    return pl.pallas_call(
        body, name=name,
        out_shape=jax.ShapeDtypeStruct((m_dim, n_dim), out_dtype),
        grid=(m_dim // tm, n_dim // tn, nk),
        in_specs=in_specs,
        out_specs=pl.BlockSpec((tm, tn), lambda i, j, k: (i, j)),
        scratch_shapes=[pltpu.VMEM((tm, tn), F32)] if nk > 1 else [],
        compiler_params=_params(3),
    )(*operands)


def _to_bf16(arrays, name, token=None):
    n = len(arrays)
    extra = [] if token is None else [token]

    def body(*refs):
        for src, dst in zip(refs[:n], refs[n + len(extra):]):
            dst[...] = src[...].astype(BF16)

    vmem = pl.BlockSpec(memory_space=pltpu.VMEM)
    return pl.pallas_call(
        body, name=name, out_shape=tuple(jax.ShapeDtypeStruct(a.shape, BF16) for a in arrays),
        in_specs=[vmem] * (n + len(extra)), out_specs=tuple([vmem] * n),
        compiler_params=pltpu.CompilerParams(vmem_limit_bytes=VMEM_LIMIT_BYTES))(*arrays, *extra)


def _rows(tr, w, cb=0):
    return pl.BlockSpec((tr, w), lambda i: (i, cb))


def _whole(shape):
    nd = len(shape)
    return pl.BlockSpec(tuple(shape), lambda i: (0,) * nd)


def _ffn_in(x16, w_in, name, token=None):
    t, d = x16.shape
    _, nb, _, fb = w_in.shape
    tm = _pick(t, 1024, 8)
    extra = [] if token is None else [token]

    def body(*refs):
        x_ref, wg_ref, wu_ref = refs[:3]
        h_ref, a_ref = refs[-2:]
        xv = x_ref[...]
        g = _dot(xv, wg_ref[0, 0])
        u = _dot(xv, wu_ref[0, 0])
        h_ref[0, 0] = g.astype(BF16)
        h_ref[0, 1] = u.astype(BF16)
        a_ref[0] = (g * _sigmoid(g) * u).astype(BF16)

    return pl.pallas_call(
        body, name=name,
        out_shape=(jax.ShapeDtypeStruct((nb, 2, t, fb), BF16), jax.ShapeDtypeStruct((nb, t, fb), BF16)),
        grid=(t // tm, nb),
        in_specs=[pl.BlockSpec((tm, d), lambda i, j: (i, 0)),
                  pl.BlockSpec((1, 1, d, fb), lambda i, j: (0, j, 0, 0)),
                  pl.BlockSpec((1, 1, d, fb), lambda i, j: (1, j, 0, 0))] + [_TOKEN] * len(extra),
        out_specs=(pl.BlockSpec((1, 2, tm, fb), lambda i, j: (j, 0, i, 0)),
                   pl.BlockSpec((1, tm, fb), lambda i, j: (j, i, 0))),
        compiler_params=_params(2))(x16, w_in, w_in, *extra)


def _ffn_out_ln(a, w_out, xin, g, b, alpha, name, scale=0.5, token=None):
    nb, t, fb = a.shape
    d = w_out.shape[2]
    tm = _pick(t, 512, 8)
    extra = [] if token is None else [token]

    def body(*refs):
        a_ref, w_ref, x_ref, g_ref, b_ref = refs[:5]
        r_ref, y_ref, y16_ref = refs[-3:]
        f = _dot(a_ref[0], w_ref[0])
        for jb in range(1, nb):
            f = f + _dot(a_ref[jb], w_ref[jb])
        r = alpha * x_ref[...] + scale * f
        xhat, _ = _ln_stats(r)
        y = xhat * g_ref[...] + b_ref[...]
        r_ref[...] = r
        y_ref[...] = y
        y16_ref[...] = y.astype(BF16)

    return pl.pallas_call(
        body, name=name,
        out_shape=(jax.ShapeDtypeStruct((t, d), F32), jax.ShapeDtypeStruct((t, d), F32),
                   jax.ShapeDtypeStruct((t, d), BF16)),
        grid=(t // tm,),
        in_specs=[pl.BlockSpec((nb, tm, fb), lambda i: (0, i, 0)), _whole(w_out.shape), _rows(tm, d),
                  _whole((1, d)), _whole((1, d))] + [_TOKEN] * len(extra),
        out_specs=(_rows(tm, d), _rows(tm, d), _rows(tm, d)),
        compiler_params=_params())(a, w_out, xin, g, b, *extra)


def _ffn_out_dx(drs, w_out, h, name, token=None):
    nb, _, t, fb = h.shape
    d = w_out.shape[2]
    tm = _pick(t, 1024, 8)
    extra = [] if token is None else [token]

    def body(*refs):
        dr_ref, w_ref, h_ref, dh_ref = refs[0], refs[1], refs[2], refs[-1]
        da = _dot(dr_ref[...], w_ref[0], tb=True)
        g, u = h_ref[0, 0].astype(F32), h_ref[0, 1].astype(F32)
        sg = _sigmoid(g)
        silu = g * sg
        dh_ref[0, 0] = ((da * sg) * u * (1.0 + (g - silu))).astype(BF16)
        dh_ref[0, 1] = (da * silu).astype(BF16)

    return pl.pallas_call(
        body, name=name, out_shape=jax.ShapeDtypeStruct((nb, 2, t, fb), BF16), grid=(t // tm, nb),
        in_specs=[pl.BlockSpec((tm, d), lambda i, j: (i, 0)),
                  pl.BlockSpec((1, fb, d), lambda i, j: (j, 0, 0)),
                  pl.BlockSpec((1, 2, tm, fb), lambda i, j: (j, 0, i, 0))] + [_TOKEN] * len(extra),
        out_specs=pl.BlockSpec((1, 2, tm, fb), lambda i, j: (j, 0, i, 0)),
        compiler_params=_params(2))(drs, w_out, h, *extra)


def _ffn_out_dw(a, drs, name):
    nb, t, fb = a.shape
    d = drs.shape[1]

    def body(a_ref, dr_ref, o_ref):
        o_ref[0] = _dot(a_ref[0], dr_ref[...], ta=True).astype(BF16)

    return pl.pallas_call(
        body, name=name, out_shape=jax.ShapeDtypeStruct((nb, fb, d), BF16), grid=(nb,),
        in_specs=[pl.BlockSpec((1, t, fb), lambda j: (j, 0, 0)), pl.BlockSpec((t, d), lambda j: (0, 0))],
        out_specs=pl.BlockSpec((1, fb, d), lambda j: (j, 0, 0)),
        compiler_params=_params())(a, drs)


def _ffn_in_dw(x16, dh, name, token=None):
    t, d = x16.shape
    nb, _, _, fb = dh.shape
    extra = [] if token is None else [token]

    def body(*refs):
        x_ref, dh_ref, o_ref = refs[0], refs[1], refs[-1]
        o_ref[0, 0] = _dot(x_ref[...], dh_ref[0, 0], ta=True).astype(BF16)

    return pl.pallas_call(
        body, name=name, out_shape=jax.ShapeDtypeStruct((2, nb, d, fb), BF16), grid=(nb, 2),
        in_specs=[pl.BlockSpec((t, d), lambda j, s: (0, 0)),
                  pl.BlockSpec((1, 1, t, fb), lambda j, s: (j, s, 0, 0))] + [_TOKEN] * len(extra),
        out_specs=pl.BlockSpec((1, 1, d, fb), lambda j, s: (s, j, 0, 0)),
        compiler_params=_params(2))(x16, dh, *extra)


def _ffn_in_dx(dh, w_in, add, add_scale, name, token=None):
    nb, _, t, fb = dh.shape
    d = w_in.shape[2]
    tm = _pick(t, 512, 8)
    has_add = add is not None
    extra = [] if token is None else [token]

    def body(*refs):
        dh_ref, w_ref = refs[:2]
        o_ref = refs[-1]
        acc = None
        for jb in range(nb):
            for s in range(2):
                part = _dot(dh_ref[jb, s], w_ref[s, jb], tb=True)
                acc = part if acc is None else acc + part
        if has_add:
            acc = acc + add_scale * refs[2][...]
        o_ref[...] = acc

    return pl.pallas_call(
        body, name=name, out_shape=jax.ShapeDtypeStruct((t, d), F32), grid=(t // tm,),
        in_specs=[pl.BlockSpec((nb, 2, tm, fb), lambda i: (0, 0, i, 0)), _whole(w_in.shape)]
        + ([_rows(tm, d)] if has_add else []) + [_TOKEN] * len(extra),
        out_specs=_rows(tm, d),
        compiler_params=_params())(*([dh, w_in] + ([add] if has_add else []) + extra))


def _ffn_in_dx_ln(dh, w_in, r, dy_b, b_scale, ln_g, out_scale, name, token=None):
    nb, _, t, fb = dh.shape
    d = w_in.shape[2]
    tm = _pick(t, 512, 8)
    extra = [] if token is None else [token]

    def body(*refs):
        dh_ref, w_ref, r_ref, dyb_ref, g_ref = refs[:5]
        dr_ref, drs_ref, dg_ref, dbeta_ref = refs[-4:]
        dy = b_scale * dyb_ref[...]
        for jb in range(nb):
            for s in range(2):
                dy = dy + _dot(dh_ref[jb, s], w_ref[s, jb], tb=True)
        xhat, rstd = _ln_stats(r_ref[...])
        dr = _ln_bwd(dy, xhat, rstd, g_ref[...])
        dr_ref[...] = dr
        drs_ref[...] = (out_scale * dr).astype(BF16)

        @pl.when(pl.program_id(0) == 0)
        def _():
            dg_ref[...] = jnp.zeros_like(dg_ref)
            dbeta_ref[...] = jnp.zeros_like(dbeta_ref)

        dg_ref[...] += _fold8(dy * xhat)
        dbeta_ref[...] += _fold8(dy)

    return pl.pallas_call(
        body, name=name,
        out_shape=(jax.ShapeDtypeStruct((t, d), F32), jax.ShapeDtypeStruct((t, d), BF16),
                   jax.ShapeDtypeStruct((8, d), F32), jax.ShapeDtypeStruct((8, d), F32)),
        grid=(t // tm,),
        in_specs=[pl.BlockSpec((nb, 2, tm, fb), lambda i: (0, 0, i, 0)), _whole(w_in.shape), _rows(tm, d),
                  _rows(tm, d), _whole((1, d))] + [_TOKEN] * len(extra),
        out_specs=(_rows(tm, d), _rows(tm, d), _whole((8, d)), _whole((8, d))),
        compiler_params=_params())(dh, w_in, r, dy_b, ln_g, *extra)


def _mm_ln_bwd(a, w, r, dy_b, b_scale, ln_g, out_scale, name, token=None):
    t, k_dim = a.shape
    d = w.shape[0]
    tm = _pick(t, 512, 8)
    extra = [] if token is None else [token]

    def body(*refs):
        a_ref, w_ref, r_ref, dyb_ref, g_ref = refs[:5]
        dr_ref, drs_ref, dg_ref, dbeta_ref = refs[-4:]
        dy = _dot(a_ref[...], w_ref[...], tb=True) + b_scale * dyb_ref[...]
        xhat, rstd = _ln_stats(r_ref[...])
        dr = _ln_bwd(dy, xhat, rstd, g_ref[...])
        dr_ref[...] = dr
        drs_ref[...] = (out_scale * dr).astype(BF16)

        @pl.when(pl.program_id(0) == 0)
        def _():
            dg_ref[...] = jnp.zeros_like(dg_ref)
            dbeta_ref[...] = jnp.zeros_like(dbeta_ref)

        dg_ref[...] += _fold8(dy * xhat)
        dbeta_ref[...] += _fold8(dy)

    return pl.pallas_call(
        body, name=name,
        out_shape=(jax.ShapeDtypeStruct((t, d), F32), jax.ShapeDtypeStruct((t, d), BF16),
                   jax.ShapeDtypeStruct((8, d), F32), jax.ShapeDtypeStruct((8, d), F32)),
        grid=(t // tm,),
        in_specs=[_rows(tm, k_dim), _whole(w.shape), _rows(tm, d), _rows(tm, d), _whole((1, d))]
        + [_TOKEN] * len(extra),
        out_specs=(_rows(tm, d), _rows(tm, d), _whole((8, d)), _whole((8, d))),
        compiler_params=_params())(a, w, r, dy_b, ln_g, *extra)


def _take_row(v, row_id, s):
    return jnp.sum(jnp.where(row_id == s, v, 0.0), axis=0, keepdims=True)


def _complex_power(ar, ai, n):
    out = None
    while n:
        if n & 1:
            out = (ar, ai) if out is None else (out[0] * ar - out[1] * ai, out[0] * ai + out[1] * ar)
        ar, ai = ar * ar - ai * ai, 2.0 * ar * ai
        n >>= 1
    return out


def _seg_carries(f_re, f_im, p_re, p_im, reverse):
    row_id = lax.broadcasted_iota(jnp.int32, f_re.shape, 0)
    p_re, p_im = _take_row(p_re, row_id, 0), _take_row(p_im, row_id, 0)
    out_re, out_im = jnp.zeros_like(f_re), jnp.zeros_like(f_im)
    c_re, c_im = jnp.zeros_like(p_re), jnp.zeros_like(p_im)
    order = range(N_SEG - 1, -1, -1) if reverse else range(N_SEG)
    for s in order:
        out_re = jnp.where(row_id == s, c_re, out_re)
        out_im = jnp.where(row_id == s, c_im, out_im)
        fr, fi = _take_row(f_re, row_id, s), _take_row(f_im, row_id, s)
        c_re, c_im = fr + p_re * c_re - p_im * c_im, fi + p_re * c_im + p_im * c_re
    return out_re, out_im


def _s5_fwd(za16, b_re, b_im, a_re, a_im, c_re, c_im, n_seq, name):
    t = za16.shape[0]
    n_sg, ch, st = b_re.shape
    seq = t // n_seq
    steps = seq // N_SEG

    def body(za_ref, bre_ref, bim_ref, are, aim, cre_ref, cim_ref, hre, him, y_ref):
        za = za_ref[...]
        hre[...] = _dot(za, bre_ref[0])
        him[...] = _dot(za, bim_ref[0])
        ar = jnp.broadcast_to(are[...], (N_SEG, st))
        ai = jnp.broadcast_to(aim[...], (N_SEG, st))
        zero = jnp.zeros((N_SEG, st), F32)
        p_re, p_im = _complex_power(ar, ai, steps)

        def local(k, carry):
            lr, li = carry
            rows = pl.ds(pl.multiple_of(k * N_SEG, N_SEG), N_SEG)
            nr = ar * lr - ai * li + hre[rows, :]
            ni = ar * li + ai * lr + him[rows, :]
            hre[rows, :] = nr
            him[rows, :] = ni
            return nr, ni

        f_re, f_im = lax.fori_loop(0, steps, local, (zero, zero))
        c_re, c_im = _seg_carries(f_re, f_im, p_re, p_im, reverse=False)

        def fix(k, carry):
            qr, qi = carry
            rows = pl.ds(pl.multiple_of(k * N_SEG, N_SEG), N_SEG)
            hre[rows, :] = hre[rows, :] + qr
            him[rows, :] = him[rows, :] + qi
            return ar * qr - ai * qi, ar * qi + ai * qr

        lax.fori_loop(0, steps, fix, (ar * c_re - ai * c_im, ar * c_im + ai * c_re))
        y_ref[...] = _dot(hre[...], cre_ref[0]) + _dot(him[...], cim_ref[0])

    rows_ch = pl.BlockSpec((seq, ch), lambda s, j: (s, j))
    rows_st = pl.BlockSpec((seq, st), lambda s, j: (s, j))
    vec = pl.BlockSpec((1, st), lambda s, j: (0, j))
    b_blk = pl.BlockSpec((1, ch, st), lambda s, j: (j, 0, 0))
    c_blk = pl.BlockSpec((1, st, ch), lambda s, j: (j, 0, 0))
    return pl.pallas_call(
        body, name=name,
        out_shape=(jax.ShapeDtypeStruct((t, n_sg * st), F32), jax.ShapeDtypeStruct((t, n_sg * st), F32),
                   jax.ShapeDtypeStruct((t, n_sg * ch), F32)),
        grid=(n_seq, n_sg), in_specs=[rows_ch, b_blk, b_blk, vec, vec, c_blk, c_blk],
        out_specs=(rows_st, rows_st, rows_ch),
        compiler_params=_params(2))(za16, b_re, b_im, a_re, a_im, c_re, c_im)


def _s5_bwd(dy16, dza_skip, h_re, h_im, za16, b_re, b_im, a_re, a_im, c_re, c_im, n_seq, name):
    t = dy16.shape[0]
    n_sg, ch, st = b_re.shape
    seq = t // n_seq
    steps = seq // N_SEG

    def body(dy_ref, skip_ref, hre, him, za_ref, bre_ref, bim_ref, are, aim, cre_ref, cim_ref,
             dza_ref, dbre_ref, dbim_ref, dcre_ref, dcim_ref, dare, daim, lre, lim):
        dy = dy_ref[...]
        lre[...] = _dot(dy, cre_ref[0], tb=True)
        lim[...] = _dot(dy, cim_ref[0], tb=True)
        ar = jnp.broadcast_to(are[...], (N_SEG, st))
        ai = -jnp.broadcast_to(aim[...], (N_SEG, st))
        zero = jnp.zeros((N_SEG, st), F32)
        p_re, p_im = _complex_power(ar, ai, steps)

        def local(i, carry):
            lr, li = carry
            k = steps - 1 - i
            rows = pl.ds(pl.multiple_of(k * N_SEG, N_SEG), N_SEG)
            nr = ar * lr - ai * li + lre[rows, :]
            ni = ar * li + ai * lr + lim[rows, :]
            lre[rows, :] = nr
            lim[rows, :] = ni
            return nr, ni

        f_re, f_im = lax.fori_loop(0, steps, local, (zero, zero))
        c_re, c_im = _seg_carries(f_re, f_im, p_re, p_im, reverse=True)

        def accumulate(lam_r, lam_i, hp_r, hp_i, acc_r, acc_i):
            return acc_r + (lam_r * hp_r + lam_i * hp_i), acc_i + (lam_i * hp_r - lam_r * hp_i)

        def fix(i, carry):
            qr, qi, acc_r, acc_i = carry
            k = steps - 1 - i
            rows = pl.ds(pl.multiple_of(k * N_SEG, N_SEG), N_SEG)
            prev = pl.ds(pl.multiple_of((k - 1) * N_SEG, N_SEG), N_SEG)
            lam_r = lre[rows, :] + qr
            lam_i = lim[rows, :] + qi
            lre[rows, :] = lam_r
            lim[rows, :] = lam_i
            acc_r, acc_i = accumulate(lam_r, lam_i, hre[prev, :], him[prev, :], acc_r, acc_i)
            return ar * qr - ai * qi, ar * qi + ai * qr, acc_r, acc_i

        qr, qi, acc_r, acc_i = lax.fori_loop(
            0, steps - 1, fix, (ar * c_re - ai * c_im, ar * c_im + ai * c_re, zero, zero))
        first = pl.ds(0, N_SEG)
        last = pl.ds((steps - 1) * N_SEG, N_SEG)
        lam_r = lre[first, :] + qr
        lam_i = lim[first, :] + qi
        lre[first, :] = lam_r
        lim[first, :] = lam_i
        row_id = lax.broadcasted_iota(jnp.int32, (N_SEG, st), 0)
        hp_r = jnp.where(row_id == 0, 0.0, pltpu.roll(hre[last, :], 1, 0))
        hp_i = jnp.where(row_id == 0, 0.0, pltpu.roll(him[last, :], 1, 0))
        acc_r, acc_i = accumulate(lam_r, lam_i, hp_r, hp_i, acc_r, acc_i)

        lam_re16 = lre[...].astype(BF16)
        lam_im16 = lim[...].astype(BF16)
        dza_ref[...] = (_dot(lam_re16, bre_ref[0], tb=True) + _dot(lam_im16, bim_ref[0], tb=True)
                        + skip_ref[...]).astype(BF16)

        @pl.when(pl.program_id(1) == 0)
        def _():
            for ref in (dbre_ref, dbim_ref, dcre_ref, dcim_ref, dare, daim):
                ref[...] = jnp.zeros_like(ref)

        za = za_ref[...]
        dbre_ref[0] += _dot(za, lam_re16, ta=True)
        dbim_ref[0] += _dot(za, lam_im16, ta=True)
        dcre_ref[0] += _dot(hre[...], dy, ta=True)
        dcim_ref[0] += _dot(him[...], dy, ta=True)
        dare[...] += acc_r
        daim[...] += acc_i

    rows_ch = pl.BlockSpec((seq, ch), lambda j, s: (s, j))
    rows_st = pl.BlockSpec((seq, st), lambda j, s: (s, j))
    vec = pl.BlockSpec((1, st), lambda j, s: (0, j))
    b_blk = pl.BlockSpec((1, ch, st), lambda j, s: (j, 0, 0))
    c_blk = pl.BlockSpec((1, st, ch), lambda j, s: (j, 0, 0))
    acc = pl.BlockSpec((N_SEG, st), lambda j, s: (0, j))
    return pl.pallas_call(
        body, name=name,
        out_shape=(jax.ShapeDtypeStruct((t, n_sg * ch), BF16),
                   jax.ShapeDtypeStruct((n_sg, ch, st), F32), jax.ShapeDtypeStruct((n_sg, ch, st), F32),
                   jax.ShapeDtypeStruct((n_sg, st, ch), F32), jax.ShapeDtypeStruct((n_sg, st, ch), F32),
                   jax.ShapeDtypeStruct((N_SEG, n_sg * st), F32), jax.ShapeDtypeStruct((N_SEG, n_sg * st), F32)),
        grid=(n_sg, n_seq),
        in_specs=[rows_ch, rows_ch, rows_st, rows_st, rows_ch, b_blk, b_blk, vec, vec, c_blk, c_blk],
        out_specs=(rows_ch, b_blk, b_blk, c_blk, c_blk, acc, acc),
        scratch_shapes=[pltpu.VMEM((seq, st), F32), pltpu.VMEM((seq, st), F32)],
        compiler_params=_params(2))(dy16, dza_skip, h_re, h_im, za16, b_re, b_im, a_re, a_im, c_re, c_im)


def _s5_post_fwd(yssm, u, d_skip, glu_w, glu_b, name):
    t, w = yssm.shape
    tr = _pick(t, 512, 8)

    def body(y_ref, u_ref, d_ref, w_ref, b_ref, o_ref):
        yg = _gelu(y_ref[...] + d_ref[...] * u_ref[...])
        pre = _dot(yg, w_ref[...]) + b_ref[...]
        o_ref[...] = yg * _sigmoid(pre)

    return pl.pallas_call(
        body, name=name, out_shape=jax.ShapeDtypeStruct((t, w), F32), grid=(t // tr,),
        in_specs=[_rows(tr, w), _rows(tr, w), _whole((1, w)), _whole((w, w)), _whole((1, w))],
        out_specs=_rows(tr, w), compiler_params=_params())(yssm, u, d_skip, glu_w, glu_b)


def _s5_post_bwd(yssm, u, dout, d_skip, glu_w, glu_b, name):
    t, w = yssm.shape
    tr = _pick(t, 512, 8)

    def body(y_ref, u_ref, do_ref, d_ref, w_ref, b_ref, dy_ref, du_ref, dw_ref, dd_ref, db_ref):
        uu = u_ref[...]
        y = y_ref[...] + d_ref[...] * uu
        yg = _gelu(y)
        sg = _sigmoid(_dot(yg, w_ref[...]) + b_ref[...])
        do = do_ref[...]
        dpre = do * yg * sg * (1.0 - sg)
        dyg = do * sg + _dot(dpre, w_ref[...], tb=True)
        dy = dyg * _gelu_grad(y)
        dy_ref[...] = dy.astype(BF16)
        du_ref[...] = dy * d_ref[...]

        @pl.when(pl.program_id(0) == 0)
        def _():
            dw_ref[...] = jnp.zeros_like(dw_ref)
            dd_ref[...] = jnp.zeros_like(dd_ref)
            db_ref[...] = jnp.zeros_like(db_ref)

        dw_ref[...] += _dot(yg, dpre, ta=True)
        dd_ref[...] += _fold8(dy * uu)
        db_ref[...] += _fold8(dpre)

    return pl.pallas_call(
        body, name=name,
        out_shape=(jax.ShapeDtypeStruct((t, w), BF16), jax.ShapeDtypeStruct((t, w), F32),
                   jax.ShapeDtypeStruct((w, w), F32), jax.ShapeDtypeStruct((8, w), F32),
                   jax.ShapeDtypeStruct((8, w), F32)),
        grid=(t // tr,),
        in_specs=[_rows(tr, w), _rows(tr, w), _rows(tr, w), _whole((1, w)), _whole((w, w)),
                  _whole((1, w))],
        out_specs=(_rows(tr, w), _rows(tr, w), _whole((w, w)), _whole((8, w)), _whole((8, w))),
        compiler_params=_params())(yssm, u, dout, d_skip, glu_w, glu_b)


def _head_select(parts, head_dim):
    col_head = lax.broadcasted_iota(jnp.int32, parts[0].shape, 1) // head_dim
    out = jnp.zeros_like(parts[0])
    for h, part in enumerate(parts):
        out = jnp.where(col_head == h, part, out)
    return out


def _gmlp_fwd(proj, ln_g, ln_b, ws, bs_cols, name):
    t = proj.shape[0]
    n_heads = ws.shape[0]
    w = bs_cols.shape[1]
    head_dim = w // n_heads
    cpb = _pick(t // CHUNK, 4, 1)
    tr = cpb * CHUNK

    def body(zu_ref, zv_ref, g_ref, b_ref, ws_ref, bs_ref, o_ref):
        for c in range(cpb):
            rows = pl.ds(c * CHUNK, CHUNK)
            xhat, _ = _ln_stats(_gelu(zv_ref[rows, :]))
            vn = (xhat * g_ref[...] + b_ref[...]).astype(BF16)
            s = _head_select([_dot(ws_ref[h], vn) for h in range(n_heads)], head_dim) + bs_ref[...]
            o_ref[rows, :] = _gelu(zu_ref[rows, :]) * s

    return pl.pallas_call(
        body, name=name, out_shape=jax.ShapeDtypeStruct((t, w), F32), grid=(t // tr,),
        in_specs=[_rows(tr, w, 1), _rows(tr, w, 2), _whole((1, w)), _whole((1, w)),
                  _whole(ws.shape), _whole(bs_cols.shape)],
        out_specs=_rows(tr, w), compiler_params=_params())(proj, proj, ln_g, ln_b, ws, bs_cols)


def _gmlp_bwd(proj, dout, ln_g, ln_b, ws, ws_t, bs_cols, name):
    t = proj.shape[0]
    n_heads = ws.shape[0]
    w = bs_cols.shape[1]
    head_dim = w // n_heads
    cpb = _pick(t // CHUNK, 4, 1)
    tr = cpb * CHUNK

    def body(zu_ref, zv_ref, do_ref, g_ref, b_ref, ws_ref, wst_ref, bs_ref,
             dzu_ref, dzv_ref, dws_ref, dbs_ref, dg_ref, dbeta_ref):
        @pl.when(pl.program_id(0) == 0)
        def _():
            dws_ref[...] = jnp.zeros_like(dws_ref)
            dbs_ref[...] = jnp.zeros_like(dbs_ref)
            dg_ref[...] = jnp.zeros_like(dg_ref)
            dbeta_ref[...] = jnp.zeros_like(dbeta_ref)

        col_head = lax.broadcasted_iota(jnp.int32, (CHUNK, w), 1) // head_dim
        for c in range(cpb):
            rows = pl.ds(c * CHUNK, CHUNK)
            zu, zv, do = zu_ref[rows, :], zv_ref[rows, :], do_ref[rows, :]
            xhat, rstd = _ln_stats(_gelu(zv))
            vn = (xhat * g_ref[...] + b_ref[...]).astype(BF16)
            s = _head_select([_dot(ws_ref[h], vn) for h in range(n_heads)], head_dim) + bs_ref[...]
            dzu_ref[rows, :] = (do * s * _gelu_grad(zu)).astype(BF16)
            ds = do * _gelu(zu)
            ds16 = ds.astype(BF16)
            dbs_ref[...] += ds
            for h in range(n_heads):
                dws_ref[h] += _dot(jnp.where(col_head == h, ds16, jnp.zeros_like(ds16)), vn, tb=True)
            dvn = _head_select([_dot(wst_ref[h], ds16) for h in range(n_heads)], head_dim)
            dg_ref[...] += _fold8(dvn * xhat)
            dbeta_ref[...] += _fold8(dvn)
            dgv = _ln_bwd(dvn, xhat, rstd, g_ref[...])
            dzv_ref[rows, :] = (dgv * _gelu_grad(zv)).astype(BF16)

    return pl.pallas_call(
        body, name=name,
        out_shape=(jax.ShapeDtypeStruct((t, w), BF16), jax.ShapeDtypeStruct((t, w), BF16),
                   jax.ShapeDtypeStruct(ws.shape, F32), jax.ShapeDtypeStruct((CHUNK, w), F32),
                   jax.ShapeDtypeStruct((8, w), F32), jax.ShapeDtypeStruct((8, w), F32)),
        grid=(t // tr,),
        in_specs=[_rows(tr, w, 1), _rows(tr, w, 2), _rows(tr, w), _whole((1, w)), _whole((1, w)),
                  _whole(ws.shape), _whole(ws.shape), _whole(bs_cols.shape)],
        out_specs=(_rows(tr, w), _rows(tr, w), _whole(ws.shape), _whole((CHUNK, w)),
                   _whole((8, w)), _whole((8, w))),
        compiler_params=_params())(proj, proj, dout, ln_g, ln_b, ws, ws_t, bs_cols)


def _merge_fwd(s5out, gm, proj, up_a, up_b, name):
    t, w = s5out.shape
    d = up_a.shape[1]
    assert d == 2 * w
    tr = _pick(t, 512, 8)

    def body(s_ref, gm_ref, ga0, ga1, gb0, gb1, ua_ref, ub_ref, m_ref, ya_ref, yb_ref):
        ya = _dot(s_ref[...], ua_ref[...])
        yb = _dot(gm_ref[...], ub_ref[...])
        ya_ref[...] = ya.astype(BF16)
        yb_ref[...] = yb.astype(BF16)
        for half, (ga, gb) in enumerate(((ga0, gb0), (ga1, gb1))):
            cols = slice(half * w, (half + 1) * w)
            m_ref[:, cols] = (_sigmoid(ga[...]) * ya[:, cols] + _sigmoid(gb[...]) * yb[:, cols]).astype(BF16)

    return pl.pallas_call(
        body, name=name,
        out_shape=(jax.ShapeDtypeStruct((t, d), BF16), jax.ShapeDtypeStruct((t, d), BF16),
                   jax.ShapeDtypeStruct((t, d), BF16)),
        grid=(t // tr,),
        in_specs=[_rows(tr, w), _rows(tr, w), _rows(tr, w, 3), _rows(tr, w, 4), _rows(tr, w, 5),
                  _rows(tr, w, 6), _whole(up_a.shape), _whole(up_b.shape)],
        out_specs=(_rows(tr, d), _rows(tr, d), _rows(tr, d)),
        compiler_params=_params())(s5out, gm, proj, proj, proj, proj, up_a, up_b)


def _merge_bwd(dm, ya, yb, s5out, gm, proj, up_a, up_b, name):
    t, d = dm.shape
    w = d // 2
    tr = _pick(t, 512, 8)

    def body(dm_ref, ya_ref, yb_ref, s_ref, gm_ref, ga0, ga1, gb0, gb1, ua_ref, ub_ref,
             dga_ref, dgb_ref, ds_ref, dgm_ref, dua_ref, dub_ref):
        dm_v, ya, yb = dm_ref[...], ya_ref[...].astype(F32), yb_ref[...].astype(F32)
        dya, dyb = [], []
        for half, (ga, gb) in enumerate(((ga0, gb0), (ga1, gb1))):
            cols = slice(half * w, (half + 1) * w)
            sa, sb = _sigmoid(ga[...]), _sigmoid(gb[...])
            dmh = dm_v[:, cols]
            dga_ref[:, cols] = (dmh * ya[:, cols] * sa * (1.0 - sa)).astype(BF16)
            dgb_ref[:, cols] = (dmh * yb[:, cols] * sb * (1.0 - sb)).astype(BF16)
            dya.append((dmh * sa).astype(BF16))
            dyb.append((dmh * sb).astype(BF16))
        dya = jnp.concatenate(dya, axis=1)
        dyb = jnp.concatenate(dyb, axis=1)
        ds_ref[...] = _dot(dya, ua_ref[...], tb=True)
        dgm_ref[...] = _dot(dyb, ub_ref[...], tb=True)

        @pl.when(pl.program_id(0) == 0)
        def _():
            dua_ref[...] = jnp.zeros_like(dua_ref)
            dub_ref[...] = jnp.zeros_like(dub_ref)

        dua_ref[...] += _dot(s_ref[...], dya, ta=True)
        dub_ref[...] += _dot(gm_ref[...], dyb, ta=True)

    return pl.pallas_call(
        body, name=name,
        out_shape=(jax.ShapeDtypeStruct((t, d), BF16), jax.ShapeDtypeStruct((t, d), BF16),
                   jax.ShapeDtypeStruct((t, w), F32), jax.ShapeDtypeStruct((t, w), F32),
                   jax.ShapeDtypeStruct(up_a.shape, F32), jax.ShapeDtypeStruct(up_b.shape, F32)),
        grid=(t // tr,),
        in_specs=[_rows(tr, d), _rows(tr, d), _rows(tr, d), _rows(tr, w), _rows(tr, w),
                  _rows(tr, w, 3), _rows(tr, w, 4), _rows(tr, w, 5), _rows(tr, w, 6),
                  _whole(up_a.shape), _whole(up_b.shape)],
        out_specs=(_rows(tr, d), _rows(tr, d), _rows(tr, w), _rows(tr, w), _whole(up_a.shape),
                   _whole(up_b.shape)),
        compiler_params=_params())(dm, ya, yb, s5out, gm, proj, proj, proj, proj, up_a, up_b)


def _head(x3, r3, ln_g, p, target, w_gate, w_proj, out_scale, name):
    t, d = x3.shape
    pd = p.shape[1]
    tr = _pick(t, 512, 8)
    nb = t // tr

    def body(x_ref, r_ref, g_ref, p_ref, t_ref, wg_ref, wp_ref,
             loss_ref, dr_ref, drs_ref, dwg_ref, dwp_ref, dg_ref, dbeta_ref):
        xv = x_ref[...]
        sg = _sigmoid(_dot(xv, wg_ref[...]))
        pp = _dot(p_ref[...], wp_ref[...])
        err = xv + sg * pp - t_ref[...]
        loss_ref[...] = jnp.full((8, 128), 0.5 * jnp.sum(jnp.mean(err * err, axis=-1)), F32)
        dout = err * (1.0 / d)
        dgpre = dout * pp * sg * (1.0 - sg)
        dpp = dout * sg
        dx = dout + _dot(dgpre, wg_ref[...], tb=True)
        xhat, rstd = _ln_stats(r_ref[...])
        dr = _ln_bwd(dx, xhat, rstd, g_ref[...])
        dr_ref[...] = dr
        drs_ref[...] = (out_scale * dr).astype(BF16)

        @pl.when(pl.program_id(0) == 0)
        def _():
            for ref in (dwg_ref, dwp_ref, dg_ref, dbeta_ref):
                ref[...] = jnp.zeros_like(ref)

        dwg_ref[...] += _dot(xv, dgpre, ta=True)
        dwp_ref[...] += _dot(p_ref[...], dpp, ta=True)
        dg_ref[...] += _fold8(dx * xhat)
        dbeta_ref[...] += _fold8(dx)

    return pl.pallas_call(
        body, name=name,
        out_shape=(jax.ShapeDtypeStruct((nb * 8, 128), F32), jax.ShapeDtypeStruct((t, d), F32),
                   jax.ShapeDtypeStruct((t, d), BF16), jax.ShapeDtypeStruct((d, d), F32),
                   jax.ShapeDtypeStruct((pd, d), F32), jax.ShapeDtypeStruct((8, d), F32),
                   jax.ShapeDtypeStruct((8, d), F32)),
        grid=(nb,),
        in_specs=[_rows(tr, d), _rows(tr, d), _whole((1, d)), _rows(tr, pd), _rows(tr, d), _whole((d, d)),
                  _whole((pd, d))],
        out_specs=(pl.BlockSpec((8, 128), lambda i: (i, 0)), _rows(tr, d), _rows(tr, d), _whole((d, d)),
                   _whole((pd, d)), _whole((8, d)), _whole((8, d))),
        compiler_params=_params())(x3, r3, ln_g, p, target, w_gate, w_proj)


_HBM = pl.BlockSpec(memory_space=pltpu.HBM)


_SEM = pl.BlockSpec(memory_space=pltpu.SEMAPHORE)
_ANY = pl.BlockSpec(memory_space=pl.ANY)
_DATAFLOW = pltpu.SideEffectType.DATAFLOW_SIDE_EFFECTING


def _peer(k, x, y, c):
    return (1 - x if k & 4 else x, 1 - y if k & 2 else y, 1 - c if k & 1 else c)


def _exchange_start(sends, after, name):
    n = len(sends)
    lands = [lax.empty((N_DEV,) + s.shape[1:], s.dtype) for s in sends]

    def body(*refs):
        s_refs, l_refs = refs[:n], refs[n:2 * n]
        send_sems, recv_sems, local_sems, token = refs[2 * n + 1], refs[2 * n + 2], refs[2 * n + 3], refs[-1]
        x, y, c = lax.axis_index("x"), lax.axis_index("y"), lax.axis_index("c")
        me = 4 * x + 2 * y + c
        for a in range(n):
            same = sends[a].shape[0] == 1
            pltpu.make_async_copy(s_refs[a].at[0 if same else me], l_refs[a].at[me], local_sems.at[a]).start()
            for k in range(1, N_DEV):
                px, py, pc = _peer(k, x, y, c)
                pltpu.make_async_remote_copy(
                    src_ref=s_refs[a].at[0 if same else 4 * px + 2 * py + pc], dst_ref=l_refs[a].at[me],
                    send_sem=send_sems.at[7 * a + k - 1], recv_sem=recv_sems.at[7 * a + k - 1],
                    device_id=(px, py, pc), device_id_type=pl.DeviceIdType.MESH).start()
        token[...] = jnp.zeros_like(token)

    outs = pl.pallas_call(
        body, name=name,
        out_shape=(pltpu.SemaphoreType.DMA((7 * n,)), pltpu.SemaphoreType.DMA((7 * n,)),
                   pltpu.SemaphoreType.DMA((n,)),
                   *[pltpu.HBM(s.shape, s.dtype) for s in sends],
                   *[pltpu.HBM(l.shape, l.dtype) for l in lands],
                   jax.ShapeDtypeStruct((8, 128), F32)),
        in_specs=[_HBM] * (2 * n) + [_ANY],
        out_specs=(_SEM, _SEM, _SEM, *([_HBM] * (2 * n)), pl.BlockSpec(memory_space=pltpu.VMEM)),
        input_output_aliases={i: 3 + i for i in range(2 * n)},
        compiler_params=pltpu.CompilerParams(has_side_effects=_DATAFLOW),
    )(*[pltpu.with_memory_space_constraint(v, pltpu.HBM) for v in list(sends) + lands], after)
    return (outs[0], outs[1], outs[2], list(outs[3:3 + n]), list(outs[3 + n:3 + 2 * n])), outs[-1]


def _exchange_wait(handle, after, name):
    send_sems, recv_sems, local_sems, sends, lands = handle
    n = len(sends)

    def body(*refs):
        s_refs, l_refs = refs[:n], refs[n:2 * n]
        send_sems, recv_sems, local_sems = refs[2 * n], refs[2 * n + 1], refs[2 * n + 2]
        x, y, c = lax.axis_index("x"), lax.axis_index("y"), lax.axis_index("c")
        for a in range(n):
            pltpu.make_async_copy(s_refs[a].at[0], l_refs[a].at[0], local_sems.at[a]).wait()
            for k in range(1, N_DEV):
                copy = pltpu.make_async_remote_copy(
                    src_ref=s_refs[a].at[0], dst_ref=l_refs[a].at[0],
                    send_sem=send_sems.at[7 * a + k - 1], recv_sem=recv_sems.at[7 * a + k - 1],
                    device_id=_peer(k, x, y, c), device_id_type=pl.DeviceIdType.MESH)
                copy.wait_send()
                copy.wait_recv()

    outs = pl.pallas_call(
        body, name=name,
        out_shape=(*[pltpu.HBM(s.shape, s.dtype) for s in sends], *[pltpu.HBM(l.shape, l.dtype) for l in lands]),
        in_specs=[_HBM] * (2 * n) + [_SEM, _SEM, _SEM] + [_ANY] * len(after),
        out_specs=tuple([_HBM] * (2 * n)),
        input_output_aliases={i: i for i in range(2 * n)},
        compiler_params=pltpu.CompilerParams(has_side_effects=_DATAFLOW),
    )(*sends, *lands, send_sems, recv_sems, local_sems, *after)
    return list(outs[n:])


def _chips_of(x, y):
    return [(1 - x, y), (x, 1 - y), (1 - x, 1 - y)]


def _gather2_start(shards, after, name):
    n = len(shards)
    lands = [lax.empty((N_DEV,) + s.shape, s.dtype) for s in shards]

    def body(*refs):
        x_refs, l_refs = refs[:n], refs[n:2 * n]
        send_sems, recv_sems, local_sems, token = refs[2 * n + 1], refs[2 * n + 2], refs[2 * n + 3], refs[-1]
        x, y, c = lax.axis_index("x"), lax.axis_index("y"), lax.axis_index("c")
        me = 4 * x + 2 * y + c
        peers = [(x, y, 1 - c)] + [(*chip, c) for chip in _chips_of(x, y)]
        for a in range(n):
            pltpu.make_async_copy(x_refs[a], l_refs[a].at[me], local_sems.at[a]).start()
            for k, peer in enumerate(peers):
                pltpu.make_async_remote_copy(
                    src_ref=x_refs[a], dst_ref=l_refs[a].at[me],
                    send_sem=send_sems.at[4 * a + k], recv_sem=recv_sems.at[4 * a + k],
                    device_id=peer, device_id_type=pl.DeviceIdType.MESH).start()
        token[...] = jnp.zeros_like(token)

    outs = pl.pallas_call(
        body, name=name,
        out_shape=(pltpu.SemaphoreType.DMA((4 * n,)), pltpu.SemaphoreType.DMA((4 * n,)),
                   pltpu.SemaphoreType.DMA((n,)),
                   *[pltpu.HBM(s.shape, s.dtype) for s in shards],
                   *[pltpu.HBM(l.shape, l.dtype) for l in lands],
                   jax.ShapeDtypeStruct((8, 128), F32)),
        in_specs=[_HBM] * (2 * n) + [_ANY],
        out_specs=(_SEM, _SEM, _SEM, *([_HBM] * (2 * n)), pl.BlockSpec(memory_space=pltpu.VMEM)),
        input_output_aliases={i: 3 + i for i in range(2 * n)},
        compiler_params=pltpu.CompilerParams(has_side_effects=_DATAFLOW),
    )(*[pltpu.with_memory_space_constraint(v, pltpu.HBM) for v in list(shards) + lands], after)
    return (outs[0], outs[1], outs[2], list(outs[3:3 + n]), list(outs[3 + n:3 + 2 * n])), outs[-1]


def _gather2_forward(handle, after, name):
    send_sems, recv_sems, local_sems, shards, lands = handle
    n = len(shards)

    def body(*refs):
        x_refs, l_refs = refs[:n], refs[n:2 * n]
        send_sems, recv_sems = refs[2 * n], refs[2 * n + 1]
        out0 = 2 * n + 2 + len(after)
        fwd_send, fwd_recv, token = refs[out0], refs[out0 + 1], refs[-1]
        x, y, c = lax.axis_index("x"), lax.axis_index("y"), lax.axis_index("c")
        for a in range(n):
            for j, chip in enumerate(_chips_of(x, y)):
                block = l_refs[a].at[4 * chip[0] + 2 * chip[1] + c]
                pltpu.make_async_remote_copy(
                    src_ref=x_refs[a], dst_ref=block, send_sem=send_sems.at[4 * a + 1 + j],
                    recv_sem=recv_sems.at[4 * a + 1 + j], device_id=(*chip, c),
                    device_id_type=pl.DeviceIdType.MESH).wait_recv()
                pltpu.make_async_remote_copy(
                    src_ref=block, dst_ref=block, send_sem=fwd_send.at[3 * a + j], recv_sem=fwd_recv.at[3 * a + j],
                    device_id=(x, y, 1 - c), device_id_type=pl.DeviceIdType.MESH).start()
        token[...] = jnp.zeros_like(token)

    outs = pl.pallas_call(
        body, name=name,
        out_shape=(pltpu.SemaphoreType.DMA((3 * n,)), pltpu.SemaphoreType.DMA((3 * n,)),
                   *[pltpu.HBM(s.shape, s.dtype) for s in shards], *[pltpu.HBM(l.shape, l.dtype) for l in lands],
                   jax.ShapeDtypeStruct((8, 128), F32)),
        in_specs=[_HBM] * (2 * n) + [_SEM, _SEM] + [_ANY] * len(after),
        out_specs=(_SEM, _SEM, *([_HBM] * (2 * n)), pl.BlockSpec(memory_space=pltpu.VMEM)),
        input_output_aliases={i: 2 + i for i in range(2 * n)},
        compiler_params=pltpu.CompilerParams(has_side_effects=_DATAFLOW),
    )(*shards, *lands, send_sems, recv_sems, *after)
    handle = (send_sems, recv_sems, local_sems, outs[0], outs[1], list(outs[2:2 + n]), list(outs[2 + n:2 + 2 * n]))
    return handle, outs[-1]


def _gather2_wait(handle, after, name):
    send_sems, recv_sems, local_sems, fwd_send, fwd_recv, shards, lands = handle
    n = len(shards)

    def body(*refs):
        x_refs, l_refs = refs[:n], refs[n:2 * n]
        send_sems, recv_sems, local_sems, fwd_send, fwd_recv = refs[2 * n:2 * n + 5]
        x, y, c = lax.axis_index("x"), lax.axis_index("y"), lax.axis_index("c")
        sibling = (x, y, 1 - c)

        def copy(a, send_sem, recv_sem):
            return pltpu.make_async_remote_copy(
                src_ref=x_refs[a], dst_ref=l_refs[a].at[0], send_sem=send_sem, recv_sem=recv_sem,
                device_id=sibling, device_id_type=pl.DeviceIdType.MESH)

        for a in range(n):
            pltpu.make_async_copy(x_refs[a], l_refs[a].at[0], local_sems.at[a]).wait()
            for k in range(4):
                copy(a, send_sems.at[4 * a + k], recv_sems.at[4 * a + k]).wait_send()
            copy(a, send_sems.at[4 * a], recv_sems.at[4 * a]).wait_recv()
            for j in range(3):
                passed = copy(a, fwd_send.at[3 * a + j], fwd_recv.at[3 * a + j])
                passed.wait_send()
                passed.wait_recv()

    outs = pl.pallas_call(
        body, name=name,
        out_shape=(*[pltpu.HBM(s.shape, s.dtype) for s in shards], *[pltpu.HBM(l.shape, l.dtype) for l in lands]),
        in_specs=[_HBM] * (2 * n) + [_SEM] * 5 + [_ANY] * len(after),
        out_specs=tuple([_HBM] * (2 * n)),
        input_output_aliases={i: i for i in range(2 * n)},
        compiler_params=pltpu.CompilerParams(has_side_effects=_DATAFLOW),
    )(*shards, *lands, send_sems, recv_sems, local_sems, fwd_send, fwd_recv, *after)
    return list(outs[n:])


def _adamw(recv, w, m, v, name):
    n, rows, width = recv.shape
    tr = _pick(rows, max(8, ADAM_BLOCK_BYTES // (n * width * recv.dtype.itemsize)), 8)
    c_m = 1.0 - ADAM_B1 ** ADAM_STEP
    c_v = 1.0 - ADAM_B2 ** ADAM_STEP

    def body(r_ref, w_ref, m_ref, v_ref, g_ref, d_ref, nm_ref, nv_ref):
        g = r_ref[0].astype(F32)
        for i in range(1, n):
            g = g + r_ref[i].astype(F32)
        m2 = ADAM_B1 * m_ref[...] + (1.0 - ADAM_B1) * g
        v2 = ADAM_B2 * v_ref[...] + (1.0 - ADAM_B2) * (g * g)
        m_hat = m2 / c_m
        v_hat = v2 / c_v
        g_ref[...] = g
        d_ref[...] = -ADAM_LR * (m_hat / (jnp.sqrt(v_hat) + ADAM_EPS) + ADAM_WD * w_ref[...])
        nm_ref[...] = m2
        nv_ref[...] = v2

    blk = _rows(tr, width)
    shp = jax.ShapeDtypeStruct((rows, width), F32)
    return pl.pallas_call(
        body, name=name, out_shape=(shp, shp, shp, shp), grid=(rows // tr,),
        in_specs=[pl.BlockSpec((n, tr, width), lambda i: (0, i, 0)), blk, blk, blk],
        out_specs=(blk, blk, blk, blk), compiler_params=_params())(recv, w, m, v)


def _join_cols(gathered):
    n, r, c = gathered.shape
    return gathered.transpose(1, 0, 2).reshape(r, n * c)


def _split_cols(full):
    r, c = full.shape
    return full.reshape(r, N_DEV, c // N_DEV).transpose(1, 0, 2)


def _adamw_small(items, name):
    n = len(items)
    c_m = 1.0 - ADAM_B1 ** ADAM_STEP
    c_v = 1.0 - ADAM_B2 ** ADAM_STEP

    def body(*refs):
        ins, outs = refs[:4 * n], refs[4 * n:]
        for k in range(n):
            r_ref, w_ref, m_ref, v_ref = ins[4 * k:4 * k + 4]
            g = r_ref[0].astype(F32)
            for i in range(1, N_DEV):
                g = g + r_ref[i].astype(F32)
            m2 = ADAM_B1 * m_ref[...] + (1.0 - ADAM_B1) * g
            v2 = ADAM_B2 * v_ref[...] + (1.0 - ADAM_B2) * (g * g)
            outs[4 * k][...] = g
            outs[4 * k + 1][...] = -ADAM_LR * ((m2 / c_m) / (jnp.sqrt(v2 / c_v) + ADAM_EPS) + ADAM_WD * w_ref[...])
            outs[4 * k + 2][...] = m2
            outs[4 * k + 3][...] = v2

    vmem = pl.BlockSpec(memory_space=pltpu.VMEM)
    flat = pl.pallas_call(
        body, name=name,
        out_shape=tuple(jax.ShapeDtypeStruct(w.shape, F32) for _, w, _, _ in items for _ in range(4)),
        in_specs=[vmem] * (4 * n), out_specs=tuple([vmem] * (4 * n)),
        compiler_params=pltpu.CompilerParams(vmem_limit_bytes=VMEM_LIMIT_BYTES),
    )(*[a for item in items for a in item])
    return [tuple(flat[4 * k:4 * k + 4]) for k in range(n)]


def _discretise(lam_re, lam_im, log_dt, b_re, b_im):
    dt = jnp.exp(log_dt)
    mag = jnp.exp(lam_re * dt)
    ab_re = mag * jnp.cos(lam_im * dt)
    ab_im = mag * jnp.sin(lam_im * dt)
    nr = ab_re - 1.0
    ni = ab_im
    den = lam_re * lam_re + lam_im * lam_im
    coef_re = (nr * lam_re + ni * lam_im) / den
    coef_im = (ni * lam_re - nr * lam_im) / den
    return ab_re, ab_im, coef_re * b_re - coef_im * b_im, coef_re * b_im + coef_im * b_re


def _s5_discretise(operands, name, token):
    def body(*refs):
        for out_ref, v in zip(refs[6:], _discretise(*[r[...] for r in refs[:5]])):
            out_ref[...] = v

    vmem = pl.BlockSpec(memory_space=pltpu.VMEM)
    shape = jax.ShapeDtypeStruct(operands[0].shape, F32)
    return pl.pallas_call(body, name=name, out_shape=(shape,) * 4, in_specs=[vmem] * 6,
                          out_specs=(vmem,) * 4)(*operands, token)


def _s5_discretise_bwd(operands, cotangents, name):
    def body(*refs):
        _, vjp = jax.vjp(_discretise, *[r[...] for r in refs[:5]])
        for out_ref, v in zip(refs[9:], vjp(tuple(r[...] for r in refs[5:9]))):
            out_ref[...] = v

    vmem = pl.BlockSpec(memory_space=pltpu.VMEM)
    shape = jax.ShapeDtypeStruct(operands[0].shape, F32)
    return pl.pallas_call(body, name=name, out_shape=(shape,) * 5, in_specs=[vmem] * 9,
                          out_specs=(vmem,) * 5)(*operands, *cotangents)


def _same_group(gl):
    return jnp.eye(gl, dtype=bool)[None, :, None, :, None]


def _super_groups_in(bb, gl):
    g, p, i = bb.shape
    blocks = bb.reshape(g // gl, gl, p, i).transpose(0, 1, 3, 2)[:, :, :, None, :]
    return jnp.where(_same_group(gl), blocks, 0.0).reshape(g // gl, gl * i, gl * p)


def _super_groups_in_take(dense, gl, p, i):
    n_sg = dense.shape[0]
    blocks = jnp.where(_same_group(gl), dense.reshape(n_sg, gl, i, gl, p), 0.0).sum(axis=3)
    return blocks.transpose(0, 1, 3, 2).reshape(n_sg * gl, p, i)


def _super_groups_out(cc, gl):
    g, i, p = cc.shape
    blocks = cc.reshape(g // gl, gl, i, p).transpose(0, 1, 3, 2)[:, :, :, None, :]
    return jnp.where(_same_group(gl), blocks, 0.0).reshape(g // gl, gl * p, gl * i)


def _super_groups_out_take(dense, gl, i, p):
    n_sg = dense.shape[0]
    blocks = jnp.where(_same_group(gl), dense.reshape(n_sg, gl, p, gl, i), 0.0).sum(axis=3)
    return blocks.transpose(0, 1, 3, 2).reshape(n_sg * gl, i, p)


def _perm_rows(z, n_seq):
    t, w = z.shape
    steps = t // n_seq // N_SEG
    return z.reshape(n_seq, N_SEG, steps, w).transpose(0, 2, 1, 3).reshape(t, w)


def _unperm_rows(z, n_seq):
    t, w = z.shape
    steps = t // n_seq // N_SEG
    return z.reshape(n_seq, steps, N_SEG, w).transpose(0, 2, 1, 3).reshape(t, w)


def kernel(*args):
    assert len(args) == len(INPUT_NAMES)
    inp = dict(zip(INPUT_NAMES, args))
    depth = inp["ffn1_w_in"].shape[0]
    assert depth == 1
    alpha = (2.0 * depth) ** 0.25

    n_seq, seq, d_model = inp["x"].shape
    t = n_seq * seq
    x = inp["x"].reshape(t, d_model)
    target = inp["loss_target"].reshape(t, d_model)
    wts = {n: inp[n][0] if n in SHARDED else inp[n] for n in WEIGHTS}
    mom = {n: inp["m_" + n][0] if n in SHARDED else inp["m_" + n] for n in WEIGHTS}
    var = {n: inp["v_" + n][0] if n in SHARDED else inp["v_" + n] for n in WEIGHTS}

    full = {}

    def place(n, g):
        if n in ("ffn1_w_in", "ffn2_w_in"):
            full[n] = g.reshape((2, N_DEV // 2) + g.shape[1:])
        elif n in ("ffn1_w_out", "ffn2_w_out"):
            full[n] = g.reshape(N_DEV // 2, 2 * g.shape[1], g.shape[2])
        elif n in COL_SHARDED:
            full[n] = _join_cols(g)
        else:
            full[n] = g.reshape(N_DEV * g.shape[1], g.shape[2])

    w16 = dict(zip(GATHER_FIRST, _to_bf16([wts[n] for n in GATHER_FIRST], "cast_ffn1")))
    gather_first, gather_first_token = _gather2_start([w16[n] for n in GATHER_FIRST], w16[GATHER_FIRST[0]],
                                                      "gather_first_start")
    later = tuple(n for n in SHARDED if n not in GATHER_FIRST)
    casts = _to_bf16([wts[n] for n in later] + [x, inp["p"][0].reshape(t, -1)], "cast_rest",
                     token=gather_first_token)
    w16.update(zip(later, casts))
    x16, p16 = casts[-2:]

    def gather_start(names, after, name):
        return _exchange_start([w16[n][None] for n in names], after, name)

    def gather_wait(names, handle, after, name):
        for n, g in zip(names, _exchange_wait(handle, after, name)):
            place(n, g)

    def gather2_wait(names, handle, after, name):
        for n, g in zip(names, _gather2_wait(handle, after, name)):
            place(n, g)


    def row(v):
        return v.reshape(1, -1)

    _, n_groups, n_state = wts["ssm_lambda_re"].shape
    n_ch = wts["ssm_b_re"].shape[3]
    disc_in = (jnp.repeat(wts["ssm_lambda_re"][0], n_ch, axis=1), jnp.repeat(wts["ssm_lambda_im"][0], n_ch, axis=1),
               jnp.broadcast_to(wts["ssm_log_dt"][0][:, None], (n_groups, n_state * n_ch)),
               wts["ssm_b_re"][0].reshape(n_groups, -1), wts["ssm_b_im"][0].reshape(n_groups, -1))
    ab_re, ab_im, bb_re, bb_im = _s5_discretise(disc_in, "s5_discretise", gather_first_token)
    a_re, a_im = row(ab_re[:, ::n_ch]), row(ab_im[:, ::n_ch])
    bb_re, bb_im = bb_re.reshape(n_groups, n_state, n_ch), bb_im.reshape(n_groups, n_state, n_ch)
    gl = S5_CHANNELS_PER_STEP // n_ch
    b_sg_re = _super_groups_in(bb_re, gl).astype(BF16)
    b_sg_im = _super_groups_in(bb_im, gl).astype(BF16)
    c_sg_re = _super_groups_out(wts["ssm_c_re"][0], gl).astype(BF16)
    c_sg_im = _super_groups_out(-wts["ssm_c_im"][0], gl).astype(BF16)

    ws = wts["gmlp_w_s"][0]
    n_heads = ws.shape[0]
    d_gmlp = wts["gmlp_ln_g"].shape[1]
    causal = jnp.tril(jnp.ones((CHUNK, CHUNK), dtype=bool))
    ws_masked = jnp.where(causal[None], ws, 0.0).astype(BF16)
    ws_masked_t = ws_masked.transpose(0, 2, 1)
    bs_cols = jnp.repeat(wts["gmlp_b_s"][0].T, d_gmlp // n_heads, axis=1)
    d_ssm = n_groups * n_ch
    assert d_ssm == d_gmlp and d_model == 2 * d_ssm

    prepared = [x16, b_sg_re, b_sg_im, c_sg_re, c_sg_im, ws_masked_t, bs_cols]
    gather_first, _ = _gather2_forward(gather_first, prepared, "gather_first_forward")
    gather2_wait(GATHER_FIRST, gather_first, prepared, "gather_first_wait")
    gather_early, gather_early_token = _gather2_start([w16[n] for n in GATHER_EARLY], full["ffn1_w_in"],
                                                      "gather_early_start")
    h1, a1 = _ffn_in(x16, full["ffn1_w_in"], "ffn1_in", token=gather_early_token)
    gather_early, _ = _gather2_forward(gather_early, [a1], "gather_early_forward")
    gather2_wait(GATHER_EARLY, gather_early, [a1], "gather_early_wait")
    r1, x1, x1_16 = _ffn_out_ln(a1, full["ffn1_w_out"], x, row(wts["ln1_g"]), row(wts["ln1_b"]), alpha,
                                "ffn1_out_ln")

    gather_mid, token = gather_start(GATHER_MID, x1_16, "gather_mid_start")
    gather_last, token = _gather2_start([w16[n] for n in GATHER_LAST], token, "gather_last_start")
    proj = _mm(x1_16, full["mix_w_in"], name="mix_in", token=token)
    za_p = _perm_rows(proj[:, :d_ssm], n_seq)
    h_re, h_im, yssm = _s5_fwd(za_p, b_sg_re, b_sg_im, a_re, a_im, c_sg_re, c_sg_im, n_seq, "s5_fwd")
    gather_wait(GATHER_MID, gather_mid, [yssm], "gather_mid_wait")
    s5out_p = _s5_post_fwd(yssm, za_p, row(wts["ssm_d"]), full["ssm_glu_w"], row(wts["ssm_glu_b"]),
                           "s5_post")
    s5out = _unperm_rows(s5out_p, n_seq)
    gm = _gmlp_fwd(proj, row(wts["gmlp_ln_g"]), row(wts["gmlp_ln_b"]), ws_masked, bs_cols, "gmlp")
    m_mix, y_a, y_b = _merge_fwd(s5out, gm, proj, full["up_a"], full["up_b"], "merge")
    gather_last, token = _gather2_forward(gather_last, [m_mix], "gather_last_forward")
    r2, x2, x2_16 = _ffn_out_ln(m_mix[None], full["mix_w_out"][None], x1, row(wts["ln2_g"]), row(wts["ln2_b"]),
                                alpha, "mix_out_ln2", scale=1.0, token=token)

    gather2_wait(GATHER_LAST, gather_last, [x2_16], "gather_last_wait")
    h2, a2 = _ffn_in(x2_16, full["ffn2_w_in"], "ffn2_in")
    r3, x3, _ = _ffn_out_ln(a2, full["ffn2_w_out"], x2, row(wts["ln3_g"]), row(wts["ln3_b"]), alpha,
                            "ffn2_out_ln")

    small = {}
    send = {}

    def lane_dense(n, v):
        return v.reshape(v.shape[:2] + (-1,)) if n in ("ssm_b_re", "ssm_b_im") else v

    def on_wire(n, g):
        g = lane_dense(n, g)
        return (g.astype(BF16) if n in SMALL_BF16 else g)[None]
    loss_parts, dr3, dr3_h, dw_gate, dw_proj, dg, db = _head(
        x3, r3, row(wts["ln3_g"]), p16, target, full["ple_w_gate"], full["ple_w_proj"], 0.5, "head")
    loss_mine = jnp.full((1, 1, 8, 128), jnp.sum(loss_parts[::8, 0]), F32)
    send["ple_w_gate"] = dw_gate.astype(BF16).reshape(N_DEV, -1, d_model)
    send["ple_w_proj"] = _split_cols(dw_proj.astype(BF16))
    small["ln3_g"], small["ln3_b"] = dg.sum(0, keepdims=True), db.sum(0, keepdims=True)
    dh2 = _ffn_out_dx(dr3_h, full["ffn2_w_out"], h2, "ffn2_out_dx")
    dw = _ffn_out_dw(a2, dr3_h, "ffn2_out_dw")
    send["ffn2_w_out"] = dw.reshape(N_DEV, -1, d_model)
    dw = _ffn_in_dw(x2_16, dh2, "ffn2_in_dw")
    send["ffn2_w_in"] = dw.reshape((N_DEV,) + dw.shape[2:])
    group1 = ("ffn2_w_in", "ffn2_w_out", "ple_w_gate", "ple_w_proj")
    exchange1, token = _exchange_start(
        [send[n] for n in group1] + [on_wire(n, small[n]) for n in SMALL_HEAD] + [loss_mine], dh2,
        "grad_exchange1_start")
    dr2, dr2_h, dg, db = _ffn_in_dx_ln(dh2, full["ffn2_w_in"], r2, dr3, alpha, row(wts["ln2_g"]), 1.0,
                                       "ffn2_in_dx_ln2_bwd", token=token)
    small["ln2_g"], small["ln2_b"] = dg.sum(0, keepdims=True), db.sum(0, keepdims=True)
    dm = _mm(dr2_h, full["mix_w_out"], tb=True, name="mix_out_dx")
    send["mix_w_out"] = _mm(m_mix, dr2_h, ta=True, name="mix_out_dw", out_dtype=BF16).reshape(
        N_DEV, -1, d_model)
    dga, dgb, ds5out, dgm, dw_a, dw_b = _merge_bwd(
        dm, y_a, y_b, s5out, gm, proj, full["up_a"], full["up_b"], "merge_bwd")
    send["up_a"] = _split_cols(dw_a.astype(BF16))
    send["up_b"] = _split_cols(dw_b.astype(BF16))

    dzu, dzv, dws, dbs_cols, dg, db = _gmlp_bwd(
        proj, dgm, row(wts["gmlp_ln_g"]), row(wts["gmlp_ln_b"]), ws_masked, ws_masked_t, bs_cols,
        "gmlp_bwd")
    small["gmlp_ln_g"], small["gmlp_ln_b"] = dg.sum(0, keepdims=True), db.sum(0, keepdims=True)
    small["gmlp_w_s"] = jnp.where(causal[None], dws, 0.0)[None]
    small["gmlp_b_s"] = dbs_cols.reshape(CHUNK, n_heads, -1).sum(-1).T[None]

    dy_p, dza_skip, dw_glu, dd, dgb_glu = _s5_post_bwd(
        yssm, za_p, _perm_rows(ds5out, n_seq), row(wts["ssm_d"]), full["ssm_glu_w"], row(wts["ssm_glu_b"]),
        "s5_post_bwd")
    send["ssm_glu_w"] = dw_glu.astype(BF16).reshape(N_DEV, -1, d_ssm)
    small["ssm_d"], small["ssm_glu_b"] = dd.sum(0, keepdims=True), dgb_glu.sum(0, keepdims=True)
    dza_p, dbb_re, dbb_im, dc_re, dc_im, da_re, da_im = _s5_bwd(
        dy_p, dza_skip, h_re, h_im, za_p, b_sg_re, b_sg_im, a_re, a_im, c_sg_re, c_sg_im, n_seq, "s5_bwd")
    small["ssm_c_re"] = _super_groups_out_take(dc_re, gl, n_ch, n_state)[None]
    small["ssm_c_im"] = -_super_groups_out_take(dc_im, gl, n_ch, n_state)[None]
    dbb_re = _super_groups_in_take(dbb_re, gl, n_state, n_ch)
    dbb_im = _super_groups_in_take(dbb_im, gl, n_state, n_ch)
    def first_channel(v):
        placed = jnp.pad(v.sum(0).reshape(n_groups, n_state, 1), ((0, 0), (0, 0), (0, n_ch - 1)))
        return placed.reshape(n_groups, -1)

    cotangents = (first_channel(da_re), first_channel(da_im), dbb_re.reshape(n_groups, -1),
                  dbb_im.reshape(n_groups, -1))
    d_lre, d_lim, d_ldt, d_bre, d_bim = _s5_discretise_bwd(disc_in, cotangents, "s5_discretise_bwd")
    per_state = (n_groups, n_state, n_ch)
    small["ssm_lambda_re"] = d_lre.reshape(per_state).sum(-1)[None]
    small["ssm_lambda_im"] = d_lim.reshape(per_state).sum(-1)[None]
    small["ssm_log_dt"] = d_ldt.sum(-1)[None]
    small["ssm_b_re"] = d_bre.reshape((1,) + per_state)
    small["ssm_b_im"] = d_bim.reshape((1,) + per_state)

    dproj = jnp.concatenate([_unperm_rows(dza_p, n_seq), dzu, dzv, dga, dgb], axis=1)
    group2 = ("mix_w_out", "up_a", "up_b", "ssm_glu_w")
    exchange2, token = _exchange_start(
        [send[n] for n in group2] + [on_wire(n, small[n]) for n in SMALL_MID], dproj, "grad_exchange2_start")
    send["mix_w_in"] = _split_cols(_mm(x1_16, dproj, ta=True, name="mix_in_dw", out_dtype=BF16, token=token))
    exchange3, token = _exchange_start([send["mix_w_in"]], dproj, "grad_exchange3_start")
    dr1, dr1_h, dg, db = _mm_ln_bwd(dproj, full["mix_w_in"], r1, dr2, alpha, row(wts["ln1_g"]), 0.5,
                                    "mix_in_dx_ln1_bwd", token=token)
    small["ln1_g"], small["ln1_b"] = dg.sum(0, keepdims=True), db.sum(0, keepdims=True)
    dw = _ffn_out_dw(a1, dr1_h, "ffn1_out_dw")
    send["ffn1_w_out"] = dw.reshape(N_DEV, -1, d_model)
    exchange4, token = _exchange_start([send["ffn1_w_out"]] + [on_wire(n, small[n]) for n in SMALL_LATE], dr1_h,
                                       "grad_exchange4_start")
    dh1 = _ffn_out_dx(dr1_h, full["ffn1_w_out"], h1, "ffn1_out_dx", token=token)
    dw = _ffn_in_dw(x16, dh1, "ffn1_in_dw")
    send["ffn1_w_in"] = dw.reshape((N_DEV,) + dw.shape[2:])
    exchange5, token = _exchange_start([send["ffn1_w_in"]], dh1, "grad_exchange5_start")
    grad_x = _ffn_in_dx(dh1, full["ffn1_w_in"], dr1, alpha, "ffn1_in_dx", token=token)

    results = {}

    def update(names, recv, prefix):
        for n, r in zip(names, recv):
            results[n] = _adamw(r, wts[n], mom[n], var[n], prefix + n)

    def update_small(names, recv, name):
        items = [(r, lane_dense(n, wts[n]), lane_dense(n, mom[n]), lane_dense(n, var[n])) for n, r in zip(names, recv)]
        for n, outs in zip(names, _adamw_small(items, name)):
            results[n] = tuple(o.reshape(wts[n].shape) for o in outs)

    def done(names):
        return [results[n][3] for n in names]

    recv = _exchange_wait(exchange1, [grad_x], "grad_exchange1_wait")
    update(group1, recv[:len(group1)], "adamw_")
    update_small(SMALL_HEAD, recv[len(group1):-1], "adamw_ln3")
    loss = jnp.sum(recv[-1][:, 0, 0, 0])
    recv = _exchange_wait(exchange2, done(group1 + SMALL_HEAD), "grad_exchange2_wait")
    update(group2, recv[:len(group2)], "adamw_")
    update_small(SMALL_MID, recv[len(group2):], "adamw_small")
    recv = _exchange_wait(exchange3, done(group2 + SMALL_MID), "grad_exchange3_wait")
    update(("mix_w_in",), recv, "adamw_")
    recv = _exchange_wait(exchange4, done(("mix_w_in",)), "grad_exchange4_wait")
    update(("ffn1_w_out",), recv[:1], "adamw_")
    update_small(SMALL_LATE, recv[1:], "adamw_ln1")
    recv = _exchange_wait(exchange5, done(("ffn1_w_out",) + SMALL_LATE), "grad_exchange5_wait")
    update(("ffn1_w_in",), recv, "adamw_")

    outs = [loss, grad_x.reshape(n_seq, seq, d_model)]
    for k in range(4):
        outs += [results[n][k][None] if n in SHARDED else results[n][k] for n in WEIGHTS]
    return tuple(outs)
```

```python
import math

import jax
import jax.numpy as jnp
from jax import lax
from jax.experimental import pallas as pl
from jax.experimental.pallas import tpu as pltpu

F32 = jnp.float32
BF16 = jnp.bfloat16

N_DEV = 8
LN_EPS = 1e-5
CHUNK = 128
N_SEG = 8
S5_CHANNELS_PER_STEP = 128
VMEM_LIMIT_BYTES = 56 * 1024 * 1024
MM_OPERAND_BLOCK_BYTES = 8 * 1024 * 1024
ADAM_BLOCK_BYTES = 6 * 1024 * 1024
ADAM_GRIDLESS_ELEMS = 128 * 1024

ADAM_LR = 0.001
ADAM_B1 = 0.9
ADAM_B2 = 0.999
ADAM_EPS = 1e-08
ADAM_WD = 0.01
ADAM_STEP = 10

COL_SHARDED = ("ffn1_w_in", "mix_w_in", "up_a", "up_b", "ffn2_w_in", "ple_w_proj")
SHARDED = ("ffn1_w_in", "ffn1_w_out", "mix_w_in", "ssm_glu_w", "up_a", "up_b", "mix_w_out",
           "ffn2_w_in", "ffn2_w_out", "ple_w_proj", "ple_w_gate")
WEIGHTS = ("ffn1_w_in", "ffn1_w_out", "ln1_g", "ln1_b", "mix_w_in", "ssm_lambda_re", "ssm_lambda_im",
           "ssm_log_dt", "ssm_b_re", "ssm_b_im", "ssm_c_re", "ssm_c_im", "ssm_d", "ssm_glu_w",
           "ssm_glu_b", "gmlp_ln_g", "gmlp_ln_b", "gmlp_w_s", "gmlp_b_s", "up_a", "up_b", "mix_w_out",
           "ln2_g", "ln2_b", "ffn2_w_in", "ffn2_w_out", "ln3_g", "ln3_b", "ple_w_proj", "ple_w_gate")
GATHER_FIRST = ("ffn1_w_in",)
GATHER_EARLY = ("ffn1_w_out", "mix_w_in")
GATHER_MID = ("ssm_glu_w", "up_a", "up_b", "mix_w_out")
GATHER_LAST = ("ffn2_w_in", "ffn2_w_out", "ple_w_proj", "ple_w_gate")
SMALL = tuple(n for n in WEIGHTS if n not in SHARDED)
SMALL_HEAD = ("ln3_g", "ln3_b")
SMALL_LATE = ("ln1_g", "ln1_b")
SMALL_MID = tuple(n for n in SMALL if n not in SMALL_HEAD + SMALL_LATE)
SMALL_BF16 = ("gmlp_w_s", "ssm_b_re", "ssm_b_im", "ssm_c_re", "ssm_c_im")
INPUT_NAMES = ("x", "p") + WEIGHTS + ("loss_target",) + tuple("m_" + n for n in WEIGHTS) + tuple(
    "v_" + n for n in WEIGHTS)


def _pick(dim, pref, mult=128):
    if dim <= pref:
        return dim
    d = (pref // mult) * mult
    while d >= mult:
        if dim % d == 0:
            return d
        d -= mult
    return dim


def _params(n_axes=1):
    return pltpu.CompilerParams(dimension_semantics=("arbitrary",) * n_axes,
                                vmem_limit_bytes=VMEM_LIMIT_BYTES)


def _sigmoid(v):
    return 0.5 * jnp.tanh(0.5 * v) + 0.5


_GELU_C = math.sqrt(2.0 / math.pi)


def _gelu(v):
    return 0.5 * v * (1.0 + jnp.tanh(_GELU_C * (v + 0.044715 * (v * v * v))))


def _gelu_grad(v):
    t = jnp.tanh(_GELU_C * (v + 0.044715 * (v * v * v)))
    return 0.5 * (1.0 + t) + 0.5 * v * (1.0 - t * t) * (_GELU_C * (1.0 + 3.0 * 0.044715 * v * v))


def _ln_stats(r):
    mu = jnp.mean(r, axis=-1, keepdims=True)
    xc = r - mu
    var = jnp.mean(xc * xc, axis=-1, keepdims=True)
    rstd = lax.rsqrt(var + LN_EPS)
    return xc * rstd, rstd


def _ln_bwd(dy, xhat, rstd, g):
    dxh = dy * g
    m1 = jnp.mean(dxh, axis=-1, keepdims=True)
    m2 = jnp.mean(dxh * xhat, axis=-1, keepdims=True)
    return rstd * (dxh - m1 - xhat * m2)


def _fold8(v):
    rows, w = v.shape
    return jnp.sum(v.reshape(rows // 8, 8, w), axis=0)


def _dot(a, b, ta=False, tb=False):
    dn = (((0 if ta else 1,), (1 if tb else 0,)), ((), ()))
    return lax.dot_general(a.astype(BF16), b.astype(BF16), dn, preferred_element_type=F32)


_TOKEN = pl.BlockSpec((8, 128), lambda *_: (0, 0))


def _mm(a, b, *, name, ta=False, tb=False, out_dtype=F32, add=None, add_scale=1.0, token=None,
        tm=2048, tn=512, tk=4096):
    m_dim = a.shape[1] if ta else a.shape[0]
    k_dim = a.shape[0] if ta else a.shape[1]
    n_dim = b.shape[0] if tb else b.shape[1]
    assert (b.shape[1] if tb else b.shape[0]) == k_dim, (name, a.shape, b.shape)
    tk = _pick(k_dim, tk)
    tm = min(tm, max(256, MM_OPERAND_BLOCK_BYTES // (tk * a.dtype.itemsize)))
    tm, tn = _pick(m_dim, tm), _pick(n_dim, tn)
    nk = k_dim // tk
    has_add = add is not None

    def finish(r, add_ref, o_ref):
        if has_add:
            r = r + add_scale * add_ref[...].astype(F32)
        o_ref[...] = r.astype(out_dtype)

    def body(*refs):
        a_ref, b_ref = refs[:2]
        add_ref = refs[2] if has_add else None
        o_ref = refs[n_in]
        if nk == 1:
            finish(_dot(a_ref[...], b_ref[...], ta, tb), add_ref, o_ref)
            return
        acc_ref = refs[-1]
        k = pl.program_id(2)

        @pl.when(k == 0)
        def _():
            acc_ref[...] = jnp.zeros_like(acc_ref)

        acc_ref[...] += _dot(a_ref[...], b_ref[...], ta, tb)

        @pl.when(k == nk - 1)
        def _():
            finish(acc_ref[...], add_ref, o_ref)

    a_spec = (pl.BlockSpec((tk, tm), lambda i, j, k: (k, i)) if ta
              else pl.BlockSpec((tm, tk), lambda i, j, k: (i, k)))
    b_spec = (pl.BlockSpec((tn, tk), lambda i, j, k: (j, k)) if tb
              else pl.BlockSpec((tk, tn), lambda i, j, k: (k, j)))
    in_specs = [a_spec, b_spec]
    operands = [a, b]
    if has_add:
        in_specs.append(pl.BlockSpec((tm, tn), lambda i, j, k: (i, j)))
        operands.append(add)
    if token is not None:
        in_specs.append(_TOKEN)
        operands.append(token)
    n_in = len(operands)
    return pl.pallas_call(
        body, name=name,
        out_shape=jax.ShapeDtypeStruct((m_dim, n_dim), out_dtype),
        grid=(m_dim // tm, n_dim // tn, nk),
        in_specs=in_specs,
        out_specs=pl.BlockSpec((tm, tn), lambda i, j, k: (i, j)),
        scratch_shapes=[pltpu.VMEM((tm, tn), F32)] if nk > 1 else [],
        compiler_params=_params(3),
    )(*operands)


def _to_bf16(arrays, name, token=None):
    n = len(arrays)
    extra = [] if token is None else [token]

    def body(*refs):
        for src, dst in zip(refs[:n], refs[n + len(extra):]):
            dst[...] = src[...].astype(BF16)

    vmem = pl.BlockSpec(memory_space=pltpu.VMEM)
    return pl.pallas_call(
        body, name=name, out_shape=tuple(jax.ShapeDtypeStruct(a.shape, BF16) for a in arrays),
        in_specs=[vmem] * (n + len(extra)), out_specs=tuple([vmem] * n),
        compiler_params=pltpu.CompilerParams(vmem_limit_bytes=VMEM_LIMIT_BYTES))(*arrays, *extra)


def _rows(tr, w, cb=0):
    return pl.BlockSpec((tr, w), lambda i: (i, cb))


def _whole(shape):
    nd = len(shape)
    return pl.BlockSpec(tuple(shape), lambda i: (0,) * nd)


def _ffn_in(x16, w_in, name, token=None):
    t, d = x16.shape
    _, nb, _, fb = w_in.shape
    tm = _pick(t, 1024, 8)
    extra = [] if token is None else [token]

    def body(*refs):
        x_ref, wg_ref, wu_ref = refs[:3]
        h_ref, a_ref = refs[-2:]
        xv = x_ref[...]
        g = _dot(xv, wg_ref[0, 0])
        u = _dot(xv, wu_ref[0, 0])
        h_ref[0, 0] = g.astype(BF16)
        h_ref[0, 1] = u.astype(BF16)
        a_ref[0] = (g * _sigmoid(g) * u).astype(BF16)

    return pl.pallas_call(
        body, name=name,
        out_shape=(jax.ShapeDtypeStruct((nb, 2, t, fb), BF16), jax.ShapeDtypeStruct((nb, t, fb), BF16)),
        grid=(t // tm, nb),
        in_specs=[pl.BlockSpec((tm, d), lambda i, j: (i, 0)),
                  pl.BlockSpec((1, 1, d, fb), lambda i, j: (0, j, 0, 0)),
                  pl.BlockSpec((1, 1, d, fb), lambda i, j: (1, j, 0, 0))] + [_TOKEN] * len(extra),
        out_specs=(pl.BlockSpec((1, 2, tm, fb), lambda i, j: (j, 0, i, 0)),
                   pl.BlockSpec((1, tm, fb), lambda i, j: (j, i, 0))),
        compiler_params=_params(2))(x16, w_in, w_in, *extra)


def _ffn_out_ln(a, w_out, xin, g, b, alpha, name, scale=0.5, token=None):
    nb, t, fb = a.shape
    d = w_out.shape[2]
    tm = _pick(t, 512, 8)
    extra = [] if token is None else [token]

    def body(*refs):
        a_ref, w_ref, x_ref, g_ref, b_ref = refs[:5]
        r_ref, y_ref, y16_ref = refs[-3:]
        f = _dot(a_ref[0], w_ref[0])
        for jb in range(1, nb):
            f = f + _dot(a_ref[jb], w_ref[jb])
        r = alpha * x_ref[...] + scale * f
        xhat, _ = _ln_stats(r)
        y = xhat * g_ref[...] + b_ref[...]
        r_ref[...] = r
        y_ref[...] = y
        y16_ref[...] = y.astype(BF16)

    return pl.pallas_call(
        body, name=name,
        out_shape=(jax.ShapeDtypeStruct((t, d), F32), jax.ShapeDtypeStruct((t, d), F32),
                   jax.ShapeDtypeStruct((t, d), BF16)),
        grid=(t // tm,),
        in_specs=[pl.BlockSpec((nb, tm, fb), lambda i: (0, i, 0)), _whole(w_out.shape), _rows(tm, d),
                  _whole((1, d)), _whole((1, d))] + [_TOKEN] * len(extra),
        out_specs=(_rows(tm, d), _rows(tm, d), _rows(tm, d)),
        compiler_params=_params())(a, w_out, xin, g, b, *extra)


def _ffn_out_dx(drs, w_out, h, name, token=None):
    nb, _, t, fb = h.shape
    d = w_out.shape[2]
    tm = _pick(t, 1024, 8)
    extra = [] if token is None else [token]

    def body(*refs):
        dr_ref, w_ref, h_ref, dh_ref = refs[0], refs[1], refs[2], refs[-1]
        da = _dot(dr_ref[...], w_ref[0], tb=True)
        g, u = h_ref[0, 0].astype(F32), h_ref[0, 1].astype(F32)
        sg = _sigmoid(g)
        silu = g * sg
        dh_ref[0, 0] = ((da * sg) * u * (1.0 + (g - silu))).astype(BF16)
        dh_ref[0, 1] = (da * silu).astype(BF16)

    return pl.pallas_call(
        body, name=name, out_shape=jax.ShapeDtypeStruct((nb, 2, t, fb), BF16), grid=(t // tm, nb),
        in_specs=[pl.BlockSpec((tm, d), lambda i, j: (i, 0)),
                  pl.BlockSpec((1, fb, d), lambda i, j: (j, 0, 0)),
                  pl.BlockSpec((1, 2, tm, fb), lambda i, j: (j, 0, i, 0))] + [_TOKEN] * len(extra),
        out_specs=pl.BlockSpec((1, 2, tm, fb), lambda i, j: (j, 0, i, 0)),
        compiler_params=_params(2))(drs, w_out, h, *extra)


def _ffn_out_dw(a, drs, name):
    nb, t, fb = a.shape
    d = drs.shape[1]

    def body(a_ref, dr_ref, o_ref):
        o_ref[0] = _dot(a_ref[0], dr_ref[...], ta=True).astype(BF16)

    return pl.pallas_call(
        body, name=name, out_shape=jax.ShapeDtypeStruct((nb, fb, d), BF16), grid=(nb,),
        in_specs=[pl.BlockSpec((1, t, fb), lambda j: (j, 0, 0)), pl.BlockSpec((t, d), lambda j: (0, 0))],
        out_specs=pl.BlockSpec((1, fb, d), lambda j: (j, 0, 0)),
        compiler_params=_params())(a, drs)


def _ffn_in_dw(x16, dh, name, token=None):
    t, d = x16.shape
    nb, _, _, fb = dh.shape
    extra = [] if token is None else [token]

    def body(*refs):
        x_ref, dh_ref, o_ref = refs[0], refs[1], refs[-1]
        o_ref[0, 0] = _dot(x_ref[...], dh_ref[0, 0], ta=True).astype(BF16)

    return pl.pallas_call(
        body, name=name, out_shape=jax.ShapeDtypeStruct((2, nb, d, fb), BF16), grid=(nb, 2),
        in_specs=[pl.BlockSpec((t, d), lambda j, s: (0, 0)),
                  pl.BlockSpec((1, 1, t, fb), lambda j, s: (j, s, 0, 0))] + [_TOKEN] * len(extra),
        out_specs=pl.BlockSpec((1, 1, d, fb), lambda j, s: (s, j, 0, 0)),
        compiler_params=_params(2))(x16, dh, *extra)


def _ffn_in_dx(dh, w_in, add, add_scale, name, token=None):
    nb, _, t, fb = dh.shape
    d = w_in.shape[2]
    tm = _pick(t, 512, 8)
    has_add = add is not None
    extra = [] if token is None else [token]

    def body(*refs):
        dh_ref, w_ref = refs[:2]
        o_ref = refs[-1]
        acc = None
        for jb in range(nb):
            for s in range(2):
                part = _dot(dh_ref[jb, s], w_ref[s, jb], tb=True)
                acc = part if acc is None else acc + part
        if has_add:
            acc = acc + add_scale * refs[2][...]
        o_ref[...] = acc

    return pl.pallas_call(
        body, name=name, out_shape=jax.ShapeDtypeStruct((t, d), F32), grid=(t // tm,),
        in_specs=[pl.BlockSpec((nb, 2, tm, fb), lambda i: (0, 0, i, 0)), _whole(w_in.shape)]
        + ([_rows(tm, d)] if has_add else []) + [_TOKEN] * len(extra),
        out_specs=_rows(tm, d),
        compiler_params=_params())(*([dh, w_in] + ([add] if has_add else []) + extra))


def _ffn_in_dx_ln(dh, w_in, r, dy_b, b_scale, ln_g, out_scale, name, token=None):
    nb, _, t, fb = dh.shape
    d = w_in.shape[2]
    tm = _pick(t, 512, 8)
    extra = [] if token is None else [token]

    def body(*refs):
        dh_ref, w_ref, r_ref, dyb_ref, g_ref = refs[:5]
        dr_ref, drs_ref, dg_ref, dbeta_ref = refs[-4:]
        dy = b_scale * dyb_ref[...]
        for jb in range(nb):
            for s in range(2):
                dy = dy + _dot(dh_ref[jb, s], w_ref[s, jb], tb=True)
        xhat, rstd = _ln_stats(r_ref[...])
        dr = _ln_bwd(dy, xhat, rstd, g_ref[...])
        dr_ref[...] = dr
        drs_ref[...] = (out_scale * dr).astype(BF16)

        @pl.when(pl.program_id(0) == 0)
        def _():
            dg_ref[...] = jnp.zeros_like(dg_ref)
            dbeta_ref[...] = jnp.zeros_like(dbeta_ref)

        dg_ref[...] += _fold8(dy * xhat)
        dbeta_ref[...] += _fold8(dy)

    return pl.pallas_call(
        body, name=name,
        out_shape=(jax.ShapeDtypeStruct((t, d), F32), jax.ShapeDtypeStruct((t, d), BF16),
                   jax.ShapeDtypeStruct((8, d), F32), jax.ShapeDtypeStruct((8, d), F32)),
        grid=(t // tm,),
        in_specs=[pl.BlockSpec((nb, 2, tm, fb), lambda i: (0, 0, i, 0)), _whole(w_in.shape), _rows(tm, d),
                  _rows(tm, d), _whole((1, d))] + [_TOKEN] * len(extra),
        out_specs=(_rows(tm, d), _rows(tm, d), _whole((8, d)), _whole((8, d))),
        compiler_params=_params())(dh, w_in, r, dy_b, ln_g, *extra)


def _mm_ln_bwd(a, w, r, dy_b, b_scale, ln_g, out_scale, name, token=None):
    t, k_dim = a.shape
    d = w.shape[0]
    tm = _pick(t, 512, 8)
    extra = [] if token is None else [token]

    def body(*refs):
        a_ref, w_ref, r_ref, dyb_ref, g_ref = refs[:5]
        dr_ref, drs_ref, dg_ref, dbeta_ref = refs[-4:]
        dy = _dot(a_ref[...], w_ref[...], tb=True) + b_scale * dyb_ref[...]
        xhat, rstd = _ln_stats(r_ref[...])
        dr = _ln_bwd(dy, xhat, rstd, g_ref[...])
        dr_ref[...] = dr
        drs_ref[...] = (out_scale * dr).astype(BF16)

        @pl.when(pl.program_id(0) == 0)
        def _():
            dg_ref[...] = jnp.zeros_like(dg_ref)
            dbeta_ref[...] = jnp.zeros_like(dbeta_ref)

        dg_ref[...] += _fold8(dy * xhat)
        dbeta_ref[...] += _fold8(dy)

    return pl.pallas_call(
        body, name=name,
        out_shape=(jax.ShapeDtypeStruct((t, d), F32), jax.ShapeDtypeStruct((t, d), BF16),
                   jax.ShapeDtypeStruct((8, d), F32), jax.ShapeDtypeStruct((8, d), F32)),
        grid=(t // tm,),
        in_specs=[_rows(tm, k_dim), _whole(w.shape), _rows(tm, d), _rows(tm, d), _whole((1, d))]
        + [_TOKEN] * len(extra),
        out_specs=(_rows(tm, d), _rows(tm, d), _whole((8, d)), _whole((8, d))),
        compiler_params=_params())(a, w, r, dy_b, ln_g, *extra)


def _take_row(v, row_id, s):
    return jnp.sum(jnp.where(row_id == s, v, 0.0), axis=0, keepdims=True)


def _complex_power(ar, ai, n):
    out = None
    while n:
        if n & 1:
            out = (ar, ai) if out is None else (out[0] * ar - out[1] * ai, out[0] * ai + out[1] * ar)
        ar, ai = ar * ar - ai * ai, 2.0 * ar * ai
        n >>= 1
    return out


def _seg_carries(f_re, f_im, p_re, p_im, reverse):
    row_id = lax.broadcasted_iota(jnp.int32, f_re.shape, 0)
    p_re, p_im = _take_row(p_re, row_id, 0), _take_row(p_im, row_id, 0)
    out_re, out_im = jnp.zeros_like(f_re), jnp.zeros_like(f_im)
    c_re, c_im = jnp.zeros_like(p_re), jnp.zeros_like(p_im)
    order = range(N_SEG - 1, -1, -1) if reverse else range(N_SEG)
    for s in order:
        out_re = jnp.where(row_id == s, c_re, out_re)
        out_im = jnp.where(row_id == s, c_im, out_im)
        fr, fi = _take_row(f_re, row_id, s), _take_row(f_im, row_id, s)
        c_re, c_im = fr + p_re * c_re - p_im * c_im, fi + p_re * c_im + p_im * c_re
    return out_re, out_im


def _s5_fwd(za16, b_re, b_im, a_re, a_im, c_re, c_im, n_seq, name):
    t = za16.shape[0]
    n_sg, ch, st = b_re.shape
    seq = t // n_seq
    steps = seq // N_SEG

    def body(za_ref, bre_ref, bim_ref, are, aim, cre_ref, cim_ref, hre, him, y_ref):
        za = za_ref[...]
        hre[...] = _dot(za, bre_ref[0])
        him[...] = _dot(za, bim_ref[0])
        ar = jnp.broadcast_to(are[...], (N_SEG, st))
        ai = jnp.broadcast_to(aim[...], (N_SEG, st))
        zero = jnp.zeros((N_SEG, st), F32)
        p_re, p_im = _complex_power(ar, ai, steps)

        def local(k, carry):
            lr, li = carry
            rows = pl.ds(pl.multiple_of(k * N_SEG, N_SEG), N_SEG)
            nr = ar * lr - ai * li + hre[rows, :]
            ni = ar * li + ai * lr + him[rows, :]
            hre[rows, :] = nr
            him[rows, :] = ni
            return nr, ni

        f_re, f_im = lax.fori_loop(0, steps, local, (zero, zero))
        c_re, c_im = _seg_carries(f_re, f_im, p_re, p_im, reverse=False)

        def fix(k, carry):
            qr, qi = carry
            rows = pl.ds(pl.multiple_of(k * N_SEG, N_SEG), N_SEG)
            hre[rows, :] = hre[rows, :] + qr
            him[rows, :] = him[rows, :] + qi
            return ar * qr - ai * qi, ar * qi + ai * qr

        lax.fori_loop(0, steps, fix, (ar * c_re - ai * c_im, ar * c_im + ai * c_re))
        y_ref[...] = _dot(hre[...], cre_ref[0]) + _dot(him[...], cim_ref[0])

    rows_ch = pl.BlockSpec((seq, ch), lambda s, j: (s, j))
    rows_st = pl.BlockSpec((seq, st), lambda s, j: (s, j))
    vec = pl.BlockSpec((1, st), lambda s, j: (0, j))
    b_blk = pl.BlockSpec((1, ch, st), lambda s, j: (j, 0, 0))
    c_blk = pl.BlockSpec((1, st, ch), lambda s, j: (j, 0, 0))
    return pl.pallas_call(
        body, name=name,
        out_shape=(jax.ShapeDtypeStruct((t, n_sg * st), F32), jax.ShapeDtypeStruct((t, n_sg * st), F32),
                   jax.ShapeDtypeStruct((t, n_sg * ch), F32)),
        grid=(n_seq, n_sg), in_specs=[rows_ch, b_blk, b_blk, vec, vec, c_blk, c_blk],
        out_specs=(rows_st, rows_st, rows_ch),
        compiler_params=_params(2))(za16, b_re, b_im, a_re, a_im, c_re, c_im)


def _s5_bwd(dy16, dza_skip, h_re, h_im, za16, b_re, b_im, a_re, a_im, c_re, c_im, n_seq, name):
    t = dy16.shape[0]
    n_sg, ch, st = b_re.shape
    seq = t // n_seq
    steps = seq // N_SEG

    def body(dy_ref, skip_ref, hre, him, za_ref, bre_ref, bim_ref, are, aim, cre_ref, cim_ref,
             dza_ref, dbre_ref, dbim_ref, dcre_ref, dcim_ref, dare, daim, lre, lim):
        dy = dy_ref[...]
        lre[...] = _dot(dy, cre_ref[0], tb=True)
        lim[...] = _dot(dy, cim_ref[0], tb=True)
        ar = jnp.broadcast_to(are[...], (N_SEG, st))
        ai = -jnp.broadcast_to(aim[...], (N_SEG, st))
        zero = jnp.zeros((N_SEG, st), F32)
        p_re, p_im = _complex_power(ar, ai, steps)

        def local(i, carry):
            lr, li = carry
            k = steps - 1 - i
            rows = pl.ds(pl.multiple_of(k * N_SEG, N_SEG), N_SEG)
            nr = ar * lr - ai * li + lre[rows, :]
            ni = ar * li + ai * lr + lim[rows, :]
            lre[rows, :] = nr
            lim[rows, :] = ni
            return nr, ni

        f_re, f_im = lax.fori_loop(0, steps, local, (zero, zero))
        c_re, c_im = _seg_carries(f_re, f_im, p_re, p_im, reverse=True)

        def accumulate(lam_r, lam_i, hp_r, hp_i, acc_r, acc_i):
            return acc_r + (lam_r * hp_r + lam_i * hp_i), acc_i + (lam_i * hp_r - lam_r * hp_i)

        def fix(i, carry):
            qr, qi, acc_r, acc_i = carry
            k = steps - 1 - i
            rows = pl.ds(pl.multiple_of(k * N_SEG, N_SEG), N_SEG)
            prev = pl.ds(pl.multiple_of((k - 1) * N_SEG, N_SEG), N_SEG)
            lam_r = lre[rows, :] + qr
            lam_i = lim[rows, :] + qi
            lre[rows, :] = lam_r
            lim[rows, :] = lam_i
            acc_r, acc_i = accumulate(lam_r, lam_i, hre[prev, :], him[prev, :], acc_r, acc_i)
            return ar * qr - ai * qi, ar * qi + ai * qr, acc_r, acc_i

        qr, qi, acc_r, acc_i = lax.fori_loop(
            0, steps - 1, fix, (ar * c_re - ai * c_im, ar * c_im + ai * c_re, zero, zero))
        first = pl.ds(0, N_SEG)
        last = pl.ds((steps - 1) * N_SEG, N_SEG)
        lam_r = lre[first, :] + qr
        lam_i = lim[first, :] + qi
        lre[first, :] = lam_r
        lim[first, :] = lam_i
        row_id = lax.broadcasted_iota(jnp.int32, (N_SEG, st), 0)
        hp_r = jnp.where(row_id == 0, 0.0, pltpu.roll(hre[last, :], 1, 0))
        hp_i = jnp.where(row_id == 0, 0.0, pltpu.roll(him[last, :], 1, 0))
        acc_r, acc_i = accumulate(lam_r, lam_i, hp_r, hp_i, acc_r, acc_i)

        lam_re16 = lre[...].astype(BF16)
        lam_im16 = lim[...].astype(BF16)
        dza_ref[...] = (_dot(lam_re16, bre_ref[0], tb=True) + _dot(lam_im16, bim_ref[0], tb=True)
                        + skip_ref[...]).astype(BF16)

        @pl.when(pl.program_id(1) == 0)
        def _():
            for ref in (dbre_ref, dbim_ref, dcre_ref, dcim_ref, dare, daim):
                ref[...] = jnp.zeros_like(ref)

        za = za_ref[...]
        dbre_ref[0] += _dot(za, lam_re16, ta=True)
        dbim_ref[0] += _dot(za, lam_im16, ta=True)
        dcre_ref[0] += _dot(hre[...], dy, ta=True)
        dcim_ref[0] += _dot(him[...], dy, ta=True)
        dare[...] += acc_r
        daim[...] += acc_i

    rows_ch = pl.BlockSpec((seq, ch), lambda j, s: (s, j))
    rows_st = pl.BlockSpec((seq, st), lambda j, s: (s, j))
    vec = pl.BlockSpec((1, st), lambda j, s: (0, j))
    b_blk = pl.BlockSpec((1, ch, st), lambda j, s: (j, 0, 0))
    c_blk = pl.BlockSpec((1, st, ch), lambda j, s: (j, 0, 0))
    acc = pl.BlockSpec((N_SEG, st), lambda j, s: (0, j))
    return pl.pallas_call(
        body, name=name,
        out_shape=(jax.ShapeDtypeStruct((t, n_sg * ch), BF16),
                   jax.ShapeDtypeStruct((n_sg, ch, st), F32), jax.ShapeDtypeStruct((n_sg, ch, st), F32),
                   jax.ShapeDtypeStruct((n_sg, st, ch), F32), jax.ShapeDtypeStruct((n_sg, st, ch), F32),
                   jax.ShapeDtypeStruct((N_SEG, n_sg * st), F32), jax.ShapeDtypeStruct((N_SEG, n_sg * st), F32)),
        grid=(n_sg, n_seq),
        in_specs=[rows_ch, rows_ch, rows_st, rows_st, rows_ch, b_blk, b_blk, vec, vec, c_blk, c_blk],
        out_specs=(rows_ch, b_blk, b_blk, c_blk, c_blk, acc, acc),
        scratch_shapes=[pltpu.VMEM((seq, st), F32), pltpu.VMEM((seq, st), F32)],
        compiler_params=_params(2))(dy16, dza_skip, h_re, h_im, za16, b_re, b_im, a_re, a_im, c_re, c_im)


def _s5_post_fwd(yssm, u, d_skip, glu_w, glu_b, name):
    t, w = yssm.shape
    tr = _pick(t, 512, 8)

    def body(y_ref, u_ref, d_ref, w_ref, b_ref, o_ref):
        yg = _gelu(y_ref[...] + d_ref[...] * u_ref[...])
        pre = _dot(yg, w_ref[...]) + b_ref[...]
        o_ref[...] = yg * _sigmoid(pre)

    return pl.pallas_call(
        body, name=name, out_shape=jax.ShapeDtypeStruct((t, w), F32), grid=(t // tr,),
        in_specs=[_rows(tr, w), _rows(tr, w), _whole((1, w)), _whole((w, w)), _whole((1, w))],
        out_specs=_rows(tr, w), compiler_params=_params())(yssm, u, d_skip, glu_w, glu_b)


def _s5_post_bwd(yssm, u, dout, d_skip, glu_w, glu_b, name):
    t, w = yssm.shape
    tr = _pick(t, 512, 8)

    def body(y_ref, u_ref, do_ref, d_ref, w_ref, b_ref, dy_ref, du_ref, dw_ref, dd_ref, db_ref):
        uu = u_ref[...]
        y = y_ref[...] + d_ref[...] * uu
        yg = _gelu(y)
        sg = _sigmoid(_dot(yg, w_ref[...]) + b_ref[...])
        do = do_ref[...]
        dpre = do * yg * sg * (1.0 - sg)
        dyg = do * sg + _dot(dpre, w_ref[...], tb=True)
        dy = dyg * _gelu_grad(y)
        dy_ref[...] = dy.astype(BF16)
        du_ref[...] = dy * d_ref[...]

        @pl.when(pl.program_id(0) == 0)
        def _():
            dw_ref[...] = jnp.zeros_like(dw_ref)
            dd_ref[...] = jnp.zeros_like(dd_ref)
            db_ref[...] = jnp.zeros_like(db_ref)

        dw_ref[...] += _dot(yg, dpre, ta=True)
        dd_ref[...] += _fold8(dy * uu)
        db_ref[...] += _fold8(dpre)

    return pl.pallas_call(
        body, name=name,
        out_shape=(jax.ShapeDtypeStruct((t, w), BF16), jax.ShapeDtypeStruct((t, w), F32),
                   jax.ShapeDtypeStruct((w, w), F32), jax.ShapeDtypeStruct((8, w), F32),
                   jax.ShapeDtypeStruct((8, w), F32)),
        grid=(t // tr,),
        in_specs=[_rows(tr, w), _rows(tr, w), _rows(tr, w), _whole((1, w)), _whole((w, w)),
                  _whole((1, w))],
        out_specs=(_rows(tr, w), _rows(tr, w), _whole((w, w)), _whole((8, w)), _whole((8, w))),
        compiler_params=_params())(yssm, u, dout, d_skip, glu_w, glu_b)


def _head_select(parts, head_dim):
    col_head = lax.broadcasted_iota(jnp.int32, parts[0].shape, 1) // head_dim
    out = jnp.zeros_like(parts[0])
    for h, part in enumerate(parts):
        out = jnp.where(col_head == h, part, out)
    return out


def _gmlp_fwd(proj, ln_g, ln_b, ws, bs_cols, name):
    t = proj.shape[0]
    n_heads = ws.shape[0]
    w = bs_cols.shape[1]
    head_dim = w // n_heads
    cpb = _pick(t // CHUNK, 4, 1)
    tr = cpb * CHUNK

    def body(zu_ref, zv_ref, g_ref, b_ref, ws_ref, bs_ref, o_ref):
        for c in range(cpb):
            rows = pl.ds(c * CHUNK, CHUNK)
            xhat, _ = _ln_stats(_gelu(zv_ref[rows, :]))
            vn = (xhat * g_ref[...] + b_ref[...]).astype(BF16)
            s = _head_select([_dot(ws_ref[h], vn) for h in range(n_heads)], head_dim) + bs_ref[...]
            o_ref[rows, :] = _gelu(zu_ref[rows, :]) * s

    return pl.pallas_call(
        body, name=name, out_shape=jax.ShapeDtypeStruct((t, w), F32), grid=(t // tr,),
        in_specs=[_rows(tr, w, 1), _rows(tr, w, 2), _whole((1, w)), _whole((1, w)),
                  _whole(ws.shape), _whole(bs_cols.shape)],
        out_specs=_rows(tr, w), compiler_params=_params())(proj, proj, ln_g, ln_b, ws, bs_cols)


def _gmlp_bwd(proj, dout, ln_g, ln_b, ws, ws_t, bs_cols, name):
    t = proj.shape[0]
    n_heads = ws.shape[0]
    w = bs_cols.shape[1]
    head_dim = w // n_heads
    cpb = _pick(t // CHUNK, 4, 1)
    tr = cpb * CHUNK

    def body(zu_ref, zv_ref, do_ref, g_ref, b_ref, ws_ref, wst_ref, bs_ref,
             dzu_ref, dzv_ref, dws_ref, dbs_ref, dg_ref, dbeta_ref):
        @pl.when(pl.program_id(0) == 0)
        def _():
            dws_ref[...] = jnp.zeros_like(dws_ref)
            dbs_ref[...] = jnp.zeros_like(dbs_ref)
            dg_ref[...] = jnp.zeros_like(dg_ref)
            dbeta_ref[...] = jnp.zeros_like(dbeta_ref)

        col_head = lax.broadcasted_iota(jnp.int32, (CHUNK, w), 1) // head_dim
        for c in range(cpb):
            rows = pl.ds(c * CHUNK, CHUNK)
            zu, zv, do = zu_ref[rows, :], zv_ref[rows, :], do_ref[rows, :]
            xhat, rstd = _ln_stats(_gelu(zv))
            vn = (xhat * g_ref[...] + b_ref[...]).astype(BF16)
            s = _head_select([_dot(ws_ref[h], vn) for h in range(n_heads)], head_dim) + bs_ref[...]
            dzu_ref[rows, :] = (do * s * _gelu_grad(zu)).astype(BF16)
            ds = do * _gelu(zu)
            ds16 = ds.astype(BF16)
            dbs_ref[...] += ds
            for h in range(n_heads):
                dws_ref[h] += _dot(jnp.where(col_head == h, ds16, jnp.zeros_like(ds16)), vn, tb=True)
            dvn = _head_select([_dot(wst_ref[h], ds16) for h in range(n_heads)], head_dim)
            dg_ref[...] += _fold8(dvn * xhat)
            dbeta_ref[...] += _fold8(dvn)
            dgv = _ln_bwd(dvn, xhat, rstd, g_ref[...])
            dzv_ref[rows, :] = (dgv * _gelu_grad(zv)).astype(BF16)

    return pl.pallas_call(
        body, name=name,
        out_shape=(jax.ShapeDtypeStruct((t, w), BF16), jax.ShapeDtypeStruct((t, w), BF16),
                   jax.ShapeDtypeStruct(ws.shape, F32), jax.ShapeDtypeStruct((CHUNK, w), F32),
                   jax.ShapeDtypeStruct((8, w), F32), jax.ShapeDtypeStruct((8, w), F32)),
        grid=(t // tr,),
        in_specs=[_rows(tr, w, 1), _rows(tr, w, 2), _rows(tr, w), _whole((1, w)), _whole((1, w)),
                  _whole(ws.shape), _whole(ws.shape), _whole(bs_cols.shape)],
        out_specs=(_rows(tr, w), _rows(tr, w), _whole(ws.shape), _whole((CHUNK, w)),
                   _whole((8, w)), _whole((8, w))),
        compiler_params=_params())(proj, proj, dout, ln_g, ln_b, ws, ws_t, bs_cols)


def _merge_fwd(s5out, gm, proj, up_a, up_b, name):
    t, w = s5out.shape
    d = up_a.shape[1]
    assert d == 2 * w
    tr = _pick(t, 512, 8)

    def body(s_ref, gm_ref, ga0, ga1, gb0, gb1, ua_ref, ub_ref, m_ref, ya_ref, yb_ref):
        ya = _dot(s_ref[...], ua_ref[...])
        yb = _dot(gm_ref[...], ub_ref[...])
        ya_ref[...] = ya.astype(BF16)
        yb_ref[...] = yb.astype(BF16)
        for half, (ga, gb) in enumerate(((ga0, gb0), (ga1, gb1))):
            cols = slice(half * w, (half + 1) * w)
            m_ref[:, cols] = (_sigmoid(ga[...]) * ya[:, cols] + _sigmoid(gb[...]) * yb[:, cols]).astype(BF16)

    return pl.pallas_call(
        body, name=name,
        out_shape=(jax.ShapeDtypeStruct((t, d), BF16), jax.ShapeDtypeStruct((t, d), BF16),
                   jax.ShapeDtypeStruct((t, d), BF16)),
        grid=(t // tr,),
        in_specs=[_rows(tr, w), _rows(tr, w), _rows(tr, w, 3), _rows(tr, w, 4), _rows(tr, w, 5),
                  _rows(tr, w, 6), _whole(up_a.shape), _whole(up_b.shape)],
        out_specs=(_rows(tr, d), _rows(tr, d), _rows(tr, d)),
        compiler_params=_params())(s5out, gm, proj, proj, proj, proj, up_a, up_b)


def _merge_bwd(dm, ya, yb, s5out, gm, proj, up_a, up_b, name):
    t, d = dm.shape
    w = d // 2
    tr = _pick(t, 512, 8)

    def body(dm_ref, ya_ref, yb_ref, s_ref, gm_ref, ga0, ga1, gb0, gb1, ua_ref, ub_ref,
             dga_ref, dgb_ref, ds_ref, dgm_ref, dua_ref, dub_ref):
        dm_v, ya, yb = dm_ref[...], ya_ref[...].astype(F32), yb_ref[...].astype(F32)
        dya, dyb = [], []
        for half, (ga, gb) in enumerate(((ga0, gb0), (ga1, gb1))):
            cols = slice(half * w, (half + 1) * w)
            sa, sb = _sigmoid(ga[...]), _sigmoid(gb[...])
            dmh = dm_v[:, cols]
            dga_ref[:, cols] = (dmh * ya[:, cols] * sa * (1.0 - sa)).astype(BF16)
            dgb_ref[:, cols] = (dmh * yb[:, cols] * sb * (1.0 - sb)).astype(BF16)
            dya.append((dmh * sa).astype(BF16))
            dyb.append((dmh * sb).astype(BF16))
        dya = jnp.concatenate(dya, axis=1)
        dyb = jnp.concatenate(dyb, axis=1)
        ds_ref[...] = _dot(dya, ua_ref[...], tb=True)
        dgm_ref[...] = _dot(dyb, ub_ref[...], tb=True)

        @pl.when(pl.program_id(0) == 0)
        def _():
            dua_ref[...] = jnp.zeros_like(dua_ref)
            dub_ref[...] = jnp.zeros_like(dub_ref)

        dua_ref[...] += _dot(s_ref[...], dya, ta=True)
        dub_ref[...] += _dot(gm_ref[...], dyb, ta=True)

    return pl.pallas_call(
        body, name=name,
        out_shape=(jax.ShapeDtypeStruct((t, d), BF16), jax.ShapeDtypeStruct((t, d), BF16),
                   jax.ShapeDtypeStruct((t, w), F32), jax.ShapeDtypeStruct((t, w), F32),
                   jax.ShapeDtypeStruct(up_a.shape, F32), jax.ShapeDtypeStruct(up_b.shape, F32)),
        grid=(t // tr,),
        in_specs=[_rows(tr, d), _rows(tr, d), _rows(tr, d), _rows(tr, w), _rows(tr, w),
                  _rows(tr, w, 3), _rows(tr, w, 4), _rows(tr, w, 5), _rows(tr, w, 6),
                  _whole(up_a.shape), _whole(up_b.shape)],
        out_specs=(_rows(tr, d), _rows(tr, d), _rows(tr, w), _rows(tr, w), _whole(up_a.shape),
                   _whole(up_b.shape)),
        compiler_params=_params())(dm, ya, yb, s5out, gm, proj, proj, proj, proj, up_a, up_b)


def _head(x3, r3, ln_g, p, target, w_gate, w_proj, out_scale, name):
    t, d = x3.shape
    pd = p.shape[1]
    tr = _pick(t, 512, 8)
    nb = t // tr

    def body(x_ref, r_ref, g_ref, p_ref, t_ref, wg_ref, wp_ref,
             loss_ref, dr_ref, drs_ref, dwg_ref, dwp_ref, dg_ref, dbeta_ref):
        xv = x_ref[...]
        sg = _sigmoid(_dot(xv, wg_ref[...]))
        pp = _dot(p_ref[...], wp_ref[...])
        err = xv + sg * pp - t_ref[...]
        loss_ref[...] = jnp.full((8, 128), 0.5 * jnp.sum(jnp.mean(err * err, axis=-1)), F32)
        dout = err * (1.0 / d)
        dgpre = dout * pp * sg * (1.0 - sg)
        dpp = dout * sg
        dx = dout + _dot(dgpre, wg_ref[...], tb=True)
        xhat, rstd = _ln_stats(r_ref[...])
        dr = _ln_bwd(dx, xhat, rstd, g_ref[...])
        dr_ref[...] = dr
        drs_ref[...] = (out_scale * dr).astype(BF16)

        @pl.when(pl.program_id(0) == 0)
        def _():
            for ref in (dwg_ref, dwp_ref, dg_ref, dbeta_ref):
                ref[...] = jnp.zeros_like(ref)

        dwg_ref[...] += _dot(xv, dgpre, ta=True)
        dwp_ref[...] += _dot(p_ref[...], dpp, ta=True)
        dg_ref[...] += _fold8(dx * xhat)
        dbeta_ref[...] += _fold8(dx)

    return pl.pallas_call(
        body, name=name,
        out_shape=(jax.ShapeDtypeStruct((nb * 8, 128), F32), jax.ShapeDtypeStruct((t, d), F32),
                   jax.ShapeDtypeStruct((t, d), BF16), jax.ShapeDtypeStruct((d, d), F32),
                   jax.ShapeDtypeStruct((pd, d), F32), jax.ShapeDtypeStruct((8, d), F32),
                   jax.ShapeDtypeStruct((8, d), F32)),
        grid=(nb,),
        in_specs=[_rows(tr, d), _rows(tr, d), _whole((1, d)), _rows(tr, pd), _rows(tr, d), _whole((d, d)),
                  _whole((pd, d))],
        out_specs=(pl.BlockSpec((8, 128), lambda i: (i, 0)), _rows(tr, d), _rows(tr, d), _whole((d, d)),
                   _whole((pd, d)), _whole((8, d)), _whole((8, d))),
        compiler_params=_params())(x3, r3, ln_g, p, target, w_gate, w_proj)


_HBM = pl.BlockSpec(memory_space=pltpu.HBM)


_SEM = pl.BlockSpec(memory_space=pltpu.SEMAPHORE)
_ANY = pl.BlockSpec(memory_space=pl.ANY)
_DATAFLOW = pltpu.SideEffectType.DATAFLOW_SIDE_EFFECTING


def _peer(k, x, y, c):
    return (1 - x if k & 4 else x, 1 - y if k & 2 else y, 1 - c if k & 1 else c)


def _exchange_start(sends, after, name):
    n = len(sends)
    lands = [lax.empty((N_DEV,) + s.shape[1:], s.dtype) for s in sends]

    def body(*refs):
        s_refs, l_refs = refs[:n], refs[n:2 * n]
        send_sems, recv_sems, local_sems, token = refs[2 * n + 1], refs[2 * n + 2], refs[2 * n + 3], refs[-1]
        x, y, c = lax.axis_index("x"), lax.axis_index("y"), lax.axis_index("c")
        me = 4 * x + 2 * y + c
        for a in range(n):
            same = sends[a].shape[0] == 1
            pltpu.make_async_copy(s_refs[a].at[0 if same else me], l_refs[a].at[me], local_sems.at[a]).start()
            for k in range(1, N_DEV):
                px, py, pc = _peer(k, x, y, c)
                pltpu.make_async_remote_copy(
                    src_ref=s_refs[a].at[0 if same else 4 * px + 2 * py + pc], dst_ref=l_refs[a].at[me],
                    send_sem=send_sems.at[7 * a + k - 1], recv_sem=recv_sems.at[7 * a + k - 1],
                    device_id=(px, py, pc), device_id_type=pl.DeviceIdType.MESH).start()
        token[...] = jnp.zeros_like(token)

    outs = pl.pallas_call(
        body, name=name,
        out_shape=(pltpu.SemaphoreType.DMA((7 * n,)), pltpu.SemaphoreType.DMA((7 * n,)),
                   pltpu.SemaphoreType.DMA((n,)),
                   *[pltpu.HBM(s.shape, s.dtype) for s in sends],
                   *[pltpu.HBM(l.shape, l.dtype) for l in lands],
                   jax.ShapeDtypeStruct((8, 128), F32)),
        in_specs=[_HBM] * (2 * n) + [_ANY],
        out_specs=(_SEM, _SEM, _SEM, *([_HBM] * (2 * n)), pl.BlockSpec(memory_space=pltpu.VMEM)),
        input_output_aliases={i: 3 + i for i in range(2 * n)},
        compiler_params=pltpu.CompilerParams(has_side_effects=_DATAFLOW),
    )(*[pltpu.with_memory_space_constraint(v, pltpu.HBM) for v in list(sends) + lands], after)
    return (outs[0], outs[1], outs[2], list(outs[3:3 + n]), list(outs[3 + n:3 + 2 * n])), outs[-1]


def _exchange_wait(handle, after, name):
    send_sems, recv_sems, local_sems, sends, lands = handle
    n = len(sends)

    def body(*refs):
        s_refs, l_refs = refs[:n], refs[n:2 * n]
        send_sems, recv_sems, local_sems = refs[2 * n], refs[2 * n + 1], refs[2 * n + 2]
        x, y, c = lax.axis_index("x"), lax.axis_index("y"), lax.axis_index("c")
        for a in range(n):
            pltpu.make_async_copy(s_refs[a].at[0], l_refs[a].at[0], local_sems.at[a]).wait()
            for k in range(1, N_DEV):
                copy = pltpu.make_async_remote_copy(
                    src_ref=s_refs[a].at[0], dst_ref=l_refs[a].at[0],
                    send_sem=send_sems.at[7 * a + k - 1], recv_sem=recv_sems.at[7 * a + k - 1],
                    device_id=_peer(k, x, y, c), device_id_type=pl.DeviceIdType.MESH)
                copy.wait_send()
                copy.wait_recv()

    outs = pl.pallas_call(
        body, name=name,
        out_shape=(*[pltpu.HBM(s.shape, s.dtype) for s in sends], *[pltpu.HBM(l.shape, l.dtype) for l in lands]),
        in_specs=[_HBM] * (2 * n) + [_SEM, _SEM, _SEM] + [_ANY] * len(after),
        out_specs=tuple([_HBM] * (2 * n)),
        input_output_aliases={i: i for i in range(2 * n)},
        compiler_params=pltpu.CompilerParams(has_side_effects=_DATAFLOW),
    )(*sends, *lands, send_sems, recv_sems, local_sems, *after)
    return list(outs[n:])


def _chips_of(x, y):
    return [(1 - x, y), (x, 1 - y), (1 - x, 1 - y)]


def _gather2_start(shards, after, name):
    n = len(shards)
    lands = [lax.empty((N_DEV,) + s.shape, s.dtype) for s in shards]

    def body(*refs):
        x_refs, l_refs = refs[:n], refs[n:2 * n]
        send_sems, recv_sems, local_sems, token = refs[2 * n + 1], refs[2 * n + 2], refs[2 * n + 3], refs[-1]
        x, y, c = lax.axis_index("x"), lax.axis_index("y"), lax.axis_index("c")
        me = 4 * x + 2 * y + c
        peers = [(x, y, 1 - c)] + [(*chip, c) for chip in _chips_of(x, y)]
        for a in range(n):
            pltpu.make_async_copy(x_refs[a], l_refs[a].at[me], local_sems.at[a]).start()
            for k, peer in enumerate(peers):
                pltpu.make_async_remote_copy(
                    src_ref=x_refs[a], dst_ref=l_refs[a].at[me],
                    send_sem=send_sems.at[4 * a + k], recv_sem=recv_sems.at[4 * a + k],
                    device_id=peer, device_id_type=pl.DeviceIdType.MESH).start()
        token[...] = jnp.zeros_like(token)

    outs = pl.pallas_call(
        body, name=name,
        out_shape=(pltpu.SemaphoreType.DMA((4 * n,)), pltpu.SemaphoreType.DMA((4 * n,)),
                   pltpu.SemaphoreType.DMA((n,)),
                   *[pltpu.HBM(s.shape, s.dtype) for s in shards],
                   *[pltpu.HBM(l.shape, l.dtype) for l in lands],
                   jax.ShapeDtypeStruct((8, 128), F32)),
        in_specs=[_HBM] * (2 * n) + [_ANY],
        out_specs=(_SEM, _SEM, _SEM, *([_HBM] * (2 * n)), pl.BlockSpec(memory_space=pltpu.VMEM)),
        input_output_aliases={i: 3 + i for i in range(2 * n)},
        compiler_params=pltpu.CompilerParams(has_side_effects=_DATAFLOW),
    )(*[pltpu.with_memory_space_constraint(v, pltpu.HBM) for v in list(shards) + lands], after)
    return (outs[0], outs[1], outs[2], list(outs[3:3 + n]), list(outs[3 + n:3 + 2 * n])), outs[-1]


def _gather2_forward(handle, after, name):
    send_sems, recv_sems, local_sems, shards, lands = handle
    n = len(shards)

    def body(*refs):
        x_refs, l_refs = refs[:n], refs[n:2 * n]
        send_sems, recv_sems = refs[2 * n], refs[2 * n + 1]
        out0 = 2 * n + 2 + len(after)
        fwd_send, fwd_recv, token = refs[out0], refs[out0 + 1], refs[-1]
        x, y, c = lax.axis_index("x"), lax.axis_index("y"), lax.axis_index("c")
        for a in range(n):
            for j, chip in enumerate(_chips_of(x, y)):
                block = l_refs[a].at[4 * chip[0] + 2 * chip[1] + c]
                pltpu.make_async_remote_copy(
                    src_ref=x_refs[a], dst_ref=block, send_sem=send_sems.at[4 * a + 1 + j],
                    recv_sem=recv_sems.at[4 * a + 1 + j], device_id=(*chip, c),
                    device_id_type=pl.DeviceIdType.MESH).wait_recv()
                pltpu.make_async_remote_copy(
                    src_ref=block, dst_ref=block, send_sem=fwd_send.at[3 * a + j], recv_sem=fwd_recv.at[3 * a + j],
                    device_id=(x, y, 1 - c), device_id_type=pl.DeviceIdType.MESH).start()
        token[...] = jnp.zeros_like(token)

    outs = pl.pallas_call(
        body, name=name,
        out_shape=(pltpu.SemaphoreType.DMA((3 * n,)), pltpu.SemaphoreType.DMA((3 * n,)),
                   *[pltpu.HBM(s.shape, s.dtype) for s in shards], *[pltpu.HBM(l.shape, l.dtype) for l in lands],
                   jax.ShapeDtypeStruct((8, 128), F32)),
        in_specs=[_HBM] * (2 * n) + [_SEM, _SEM] + [_ANY] * len(after),
        out_specs=(_SEM, _SEM, *([_HBM] * (2 * n)), pl.BlockSpec(memory_space=pltpu.VMEM)),
        input_output_aliases={i: 2 + i for i in range(2 * n)},
        compiler_params=pltpu.CompilerParams(has_side_effects=_DATAFLOW),
    )(*shards, *lands, send_sems, recv_sems, *after)
    handle = (send_sems, recv_sems, local_sems, outs[0], outs[1], list(outs[2:2 + n]), list(outs[2 + n:2 + 2 * n]))
    return handle, outs[-1]


def _gather2_wait(handle, after, name):
    send_sems, recv_sems, local_sems, fwd_send, fwd_recv, shards, lands = handle
    n = len(shards)

    def body(*refs):
        x_refs, l_refs = refs[:n], refs[n:2 * n]
        send_sems, recv_sems, local_sems, fwd_send, fwd_recv = refs[2 * n:2 * n + 5]
        x, y, c = lax.axis_index("x"), lax.axis_index("y"), lax.axis_index("c")
        sibling = (x, y, 1 - c)

        def copy(a, send_sem, recv_sem):
            return pltpu.make_async_remote_copy(
                src_ref=x_refs[a], dst_ref=l_refs[a].at[0], send_sem=send_sem, recv_sem=recv_sem,
                device_id=sibling, device_id_type=pl.DeviceIdType.MESH)

        for a in range(n):
            pltpu.make_async_copy(x_refs[a], l_refs[a].at[0], local_sems.at[a]).wait()
            for k in range(4):
                copy(a, send_sems.at[4 * a + k], recv_sems.at[4 * a + k]).wait_send()
            copy(a, send_sems.at[4 * a], recv_sems.at[4 * a]).wait_recv()
            for j in range(3):
                passed = copy(a, fwd_send.at[3 * a + j], fwd_recv.at[3 * a + j])
                passed.wait_send()
                passed.wait_recv()

    outs = pl.pallas_call(
        body, name=name,
        out_shape=(*[pltpu.HBM(s.shape, s.dtype) for s in shards], *[pltpu.HBM(l.shape, l.dtype) for l in lands]),
        in_specs=[_HBM] * (2 * n) + [_SEM] * 5 + [_ANY] * len(after),
        out_specs=tuple([_HBM] * (2 * n)),
        input_output_aliases={i: i for i in range(2 * n)},
        compiler_params=pltpu.CompilerParams(has_side_effects=_DATAFLOW),
    )(*shards, *lands, send_sems, recv_sems, local_sems, fwd_send, fwd_recv, *after)
    return list(outs[n:])


def _adamw(recv, w, m, v, name):
    n, rows, width = recv.shape
    tr = _pick(rows, max(8, ADAM_BLOCK_BYTES // (n * width * recv.dtype.itemsize)), 8)
    c_m = 1.0 - ADAM_B1 ** ADAM_STEP
    c_v = 1.0 - ADAM_B2 ** ADAM_STEP

    def body(r_ref, w_ref, m_ref, v_ref, g_ref, d_ref, nm_ref, nv_ref):
        g = r_ref[0].astype(F32)
        for i in range(1, n):
            g = g + r_ref[i].astype(F32)
        m2 = ADAM_B1 * m_ref[...] + (1.0 - ADAM_B1) * g
        v2 = ADAM_B2 * v_ref[...] + (1.0 - ADAM_B2) * (g * g)
        m_hat = m2 / c_m
        v_hat = v2 / c_v
        g_ref[...] = g
        d_ref[...] = -ADAM_LR * (m_hat / (jnp.sqrt(v_hat) + ADAM_EPS) + ADAM_WD * w_ref[...])
        nm_ref[...] = m2
        nv_ref[...] = v2

    blk = _rows(tr, width)
    shp = jax.ShapeDtypeStruct((rows, width), F32)
    return pl.pallas_call(
        body, name=name, out_shape=(shp, shp, shp, shp), grid=(rows // tr,),
        in_specs=[pl.BlockSpec((n, tr, width), lambda i: (0, i, 0)), blk, blk, blk],
        out_specs=(blk, blk, blk, blk), compiler_params=_params())(recv, w, m, v)


def _join_cols(gathered):
    n, r, c = gathered.shape
    return gathered.transpose(1, 0, 2).reshape(r, n * c)


def _split_cols(full):
    r, c = full.shape
    return full.reshape(r, N_DEV, c // N_DEV).transpose(1, 0, 2)


def _adamw_small(items, name):
    n = len(items)
    c_m = 1.0 - ADAM_B1 ** ADAM_STEP
    c_v = 1.0 - ADAM_B2 ** ADAM_STEP

    def body(*refs):
        ins, outs = refs[:4 * n], refs[4 * n:]
        for k in range(n):
            r_ref, w_ref, m_ref, v_ref = ins[4 * k:4 * k + 4]
            g = r_ref[0].astype(F32)
            for i in range(1, N_DEV):
                g = g + r_ref[i].astype(F32)
            m2 = ADAM_B1 * m_ref[...] + (1.0 - ADAM_B1) * g
            v2 = ADAM_B2 * v_ref[...] + (1.0 - ADAM_B2) * (g * g)
            outs[4 * k][...] = g
            outs[4 * k + 1][...] = -ADAM_LR * ((m2 / c_m) / (jnp.sqrt(v2 / c_v) + ADAM_EPS) + ADAM_WD * w_ref[...])
            outs[4 * k + 2][...] = m2
            outs[4 * k + 3][...] = v2

    vmem = pl.BlockSpec(memory_space=pltpu.VMEM)
    flat = pl.pallas_call(
        body, name=name,
        out_shape=tuple(jax.ShapeDtypeStruct(w.shape, F32) for _, w, _, _ in items for _ in range(4)),
        in_specs=[vmem] * (4 * n), out_specs=tuple([vmem] * (4 * n)),
        compiler_params=pltpu.CompilerParams(vmem_limit_bytes=VMEM_LIMIT_BYTES),
    )(*[a for item in items for a in item])
    return [tuple(flat[4 * k:4 * k + 4]) for k in range(n)]


def _discretise(lam_re, lam_im, log_dt, b_re, b_im):
    dt = jnp.exp(log_dt)
    mag = jnp.exp(lam_re * dt)
    ab_re = mag * jnp.cos(lam_im * dt)
    ab_im = mag * jnp.sin(lam_im * dt)
    nr = ab_re - 1.0
    ni = ab_im
    den = lam_re * lam_re + lam_im * lam_im
    coef_re = (nr * lam_re + ni * lam_im) / den
    coef_im = (ni * lam_re - nr * lam_im) / den
    return ab_re, ab_im, coef_re * b_re - coef_im * b_im, coef_re * b_im + coef_im * b_re


def _s5_discretise(operands, name, token):
    def body(*refs):
        for out_ref, v in zip(refs[6:], _discretise(*[r[...] for r in refs[:5]])):
            out_ref[...] = v

    vmem = pl.BlockSpec(memory_space=pltpu.VMEM)
    shape = jax.ShapeDtypeStruct(operands[0].shape, F32)
    return pl.pallas_call(body, name=name, out_shape=(shape,) * 4, in_specs=[vmem] * 6,
                          out_specs=(vmem,) * 4)(*operands, token)


def _s5_discretise_bwd(operands, cotangents, name):
    def body(*refs):
        _, vjp = jax.vjp(_discretise, *[r[...] for r in refs[:5]])
        for out_ref, v in zip(refs[9:], vjp(tuple(r[...] for r in refs[5:9]))):
            out_ref[...] = v

    vmem = pl.BlockSpec(memory_space=pltpu.VMEM)
    shape = jax.ShapeDtypeStruct(operands[0].shape, F32)
    return pl.pallas_call(body, name=name, out_shape=(shape,) * 5, in_specs=[vmem] * 9,
                          out_specs=(vmem,) * 5)(*operands, *cotangents)


def _same_group(gl):
    return jnp.eye(gl, dtype=bool)[None, :, None, :, None]


def _super_groups_in(bb, gl):
    g, p, i = bb.shape
    blocks = bb.reshape(g // gl, gl, p, i).transpose(0, 1, 3, 2)[:, :, :, None, :]
    return jnp.where(_same_group(gl), blocks, 0.0).reshape(g // gl, gl * i, gl * p)


def _super_groups_in_take(dense, gl, p, i):
    n_sg = dense.shape[0]
    blocks = jnp.where(_same_group(gl), dense.reshape(n_sg, gl, i, gl, p), 0.0).sum(axis=3)
    return blocks.transpose(0, 1, 3, 2).reshape(n_sg * gl, p, i)


def _super_groups_out(cc, gl):
    g, i, p = cc.shape
    blocks = cc.reshape(g // gl, gl, i, p).transpose(0, 1, 3, 2)[:, :, :, None, :]
    return jnp.where(_same_group(gl), blocks, 0.0).reshape(g // gl, gl * p, gl * i)


def _super_groups_out_take(dense, gl, i, p):
    n_sg = dense.shape[0]
    blocks = jnp.where(_same_group(gl), dense.reshape(n_sg, gl, p, gl, i), 0.0).sum(axis=3)
    return blocks.transpose(0, 1, 3, 2).reshape(n_sg * gl, i, p)


def _perm_rows(z, n_seq):
    t, w = z.shape
    steps = t // n_seq // N_SEG
    return z.reshape(n_seq, N_SEG, steps, w).transpose(0, 2, 1, 3).reshape(t, w)


def _unperm_rows(z, n_seq):
    t, w = z.shape
    steps = t // n_seq // N_SEG
    return z.reshape(n_seq, steps, N_SEG, w).transpose(0, 2, 1, 3).reshape(t, w)


def kernel(*args):
    assert len(args) == len(INPUT_NAMES)
    inp = dict(zip(INPUT_NAMES, args))
    depth = inp["ffn1_w_in"].shape[0]
    assert depth == 1
    alpha = (2.0 * depth) ** 0.25

    n_seq, seq, d_model = inp["x"].shape
    t = n_seq * seq
    x = inp["x"].reshape(t, d_model)
    target = inp["loss_target"].reshape(t, d_model)
    wts = {n: inp[n][0] if n in SHARDED else inp[n] for n in WEIGHTS}
    mom = {n: inp["m_" + n][0] if n in SHARDED else inp["m_" + n] for n in WEIGHTS}
    var = {n: inp["v_" + n][0] if n in SHARDED else inp["v_" + n] for n in WEIGHTS}

    full = {}

    def place(n, g):
        if n in ("ffn1_w_in", "ffn2_w_in"):
            full[n] = g.reshape((2, N_DEV // 2) + g.shape[1:])
        elif n in ("ffn1_w_out", "ffn2_w_out"):
            full[n] = g.reshape(N_DEV // 2, 2 * g.shape[1], g.shape[2])
        elif n in COL_SHARDED:
            full[n] = _join_cols(g)
        else:
            full[n] = g.reshape(N_DEV * g.shape[1], g.shape[2])

    w16 = dict(zip(GATHER_FIRST, _to_bf16([wts[n] for n in GATHER_FIRST], "cast_ffn1")))
    gather_first, gather_first_token = _gather2_start([w16[n] for n in GATHER_FIRST], w16[GATHER_FIRST[0]],
                                                      "gather_first_start")
    later = tuple(n for n in SHARDED if n not in GATHER_FIRST)
    casts = _to_bf16([wts[n] for n in later] + [x, inp["p"][0].reshape(t, -1)], "cast_rest",
                     token=gather_first_token)
    w16.update(zip(later, casts))
    x16, p16 = casts[-2:]

    def gather_start(names, after, name):
        return _exchange_start([w16[n][None] for n in names], after, name)

    def gather_wait(names, handle, after, name):
        for n, g in zip(names, _exchange_wait(handle, after, name)):
            place(n, g)

    def gather2_wait(names, handle, after, name):
        for n, g in zip(names, _gather2_wait(handle, after, name)):
            place(n, g)


    def row(v):
        return v.reshape(1, -1)

    _, n_groups, n_state = wts["ssm_lambda_re"].shape
    n_ch = wts["ssm_b_re"].shape[3]
    disc_in = (jnp.repeat(wts["ssm_lambda_re"][0], n_ch, axis=1), jnp.repeat(wts["ssm_lambda_im"][0], n_ch, axis=1),
               jnp.broadcast_to(wts["ssm_log_dt"][0][:, None], (n_groups, n_state * n_ch)),
               wts["ssm_b_re"][0].reshape(n_groups, -1), wts["ssm_b_im"][0].reshape(n_groups, -1))
    ab_re, ab_im, bb_re, bb_im = _s5_discretise(disc_in, "s5_discretise", gather_first_token)
    a_re, a_im = row(ab_re[:, ::n_ch]), row(ab_im[:, ::n_ch])
    bb_re, bb_im = bb_re.reshape(n_groups, n_state, n_ch), bb_im.reshape(n_groups, n_state, n_ch)
    gl = S5_CHANNELS_PER_STEP // n_ch
    b_sg_re = _super_groups_in(bb_re, gl).astype(BF16)
    b_sg_im = _super_groups_in(bb_im, gl).astype(BF16)
    c_sg_re = _super_groups_out(wts["ssm_c_re"][0], gl).astype(BF16)
    c_sg_im = _super_groups_out(-wts["ssm_c_im"][0], gl).astype(BF16)

    ws = wts["gmlp_w_s"][0]
    n_heads = ws.shape[0]
    d_gmlp = wts["gmlp_ln_g"].shape[1]
    causal = jnp.tril(jnp.ones((CHUNK, CHUNK), dtype=bool))
    ws_masked = jnp.where(causal[None], ws, 0.0).astype(BF16)
    ws_masked_t = ws_masked.transpose(0, 2, 1)
    bs_cols = jnp.repeat(wts["gmlp_b_s"][0].T, d_gmlp // n_heads, axis=1)
    d_ssm = n_groups * n_ch
    assert d_ssm == d_gmlp and d_model == 2 * d_ssm

    prepared = [x16, b_sg_re, b_sg_im, c_sg_re, c_sg_im, ws_masked_t, bs_cols]
    gather_first, _ = _gather2_forward(gather_first, prepared, "gather_first_forward")
    gather2_wait(GATHER_FIRST, gather_first, prepared, "gather_first_wait")
    gather_early, gather_early_token = _gather2_start([w16[n] for n in GATHER_EARLY], full["ffn1_w_in"],
                                                      "gather_early_start")
    h1, a1 = _ffn_in(x16, full["ffn1_w_in"], "ffn1_in", token=gather_early_token)
    gather_early, _ = _gather2_forward(gather_early, [a1], "gather_early_forward")
    gather2_wait(GATHER_EARLY, gather_early, [a1], "gather_early_wait")
    r1, x1, x1_16 = _ffn_out_ln(a1, full["ffn1_w_out"], x, row(wts["ln1_g"]), row(wts["ln1_b"]), alpha,
                                "ffn1_out_ln")

    gather_mid, token = gather_start(GATHER_MID, x1_16, "gather_mid_start")
    gather_last, token = _gather2_start([w16[n] for n in GATHER_LAST], token, "gather_last_start")
    proj = _mm(x1_16, full["mix_w_in"], name="mix_in", token=token)
    za_p = _perm_rows(proj[:, :d_ssm], n_seq)
    h_re, h_im, yssm = _s5_fwd(za_p, b_sg_re, b_sg_im, a_re, a_im, c_sg_re, c_sg_im, n_seq, "s5_fwd")
    gather_wait(GATHER_MID, gather_mid, [yssm], "gather_mid_wait")
    s5out_p = _s5_post_fwd(yssm, za_p, row(wts["ssm_d"]), full["ssm_glu_w"], row(wts["ssm_glu_b"]),
                           "s5_post")
    s5out = _unperm_rows(s5out_p, n_seq)
    gm = _gmlp_fwd(proj, row(wts["gmlp_ln_g"]), row(wts["gmlp_ln_b"]), ws_masked, bs_cols, "gmlp")
    m_mix, y_a, y_b = _merge_fwd(s5out, gm, proj, full["up_a"], full["up_b"], "merge")
    gather_last, token = _gather2_forward(gather_last, [m_mix], "gather_last_forward")
    r2, x2, x2_16 = _ffn_out_ln(m_mix[None], full["mix_w_out"][None], x1, row(wts["ln2_g"]), row(wts["ln2_b"]),
                                alpha, "mix_out_ln2", scale=1.0, token=token)

    gather2_wait(GATHER_LAST, gather_last, [x2_16], "gather_last_wait")
    h2, a2 = _ffn_in(x2_16, full["ffn2_w_in"], "ffn2_in")
    r3, x3, _ = _ffn_out_ln(a2, full["ffn2_w_out"], x2, row(wts["ln3_g"]), row(wts["ln3_b"]), alpha,
                            "ffn2_out_ln")

    small = {}
    send = {}

    def lane_dense(n, v):
        return v.reshape(v.shape[:2] + (-1,)) if n in ("ssm_b_re", "ssm_b_im") else v

    def on_wire(n, g):
        g = lane_dense(n, g)
        return (g.astype(BF16) if n in SMALL_BF16 else g)[None]
    loss_parts, dr3, dr3_h, dw_gate, dw_proj, dg, db = _head(
        x3, r3, row(wts["ln3_g"]), p16, target, full["ple_w_gate"], full["ple_w_proj"], 0.5, "head")
    loss_mine = jnp.full((1, 1, 8, 128), jnp.sum(loss_parts[::8, 0]), F32)
    send["ple_w_gate"] = dw_gate.astype(BF16).reshape(N_DEV, -1, d_model)
    send["ple_w_proj"] = _split_cols(dw_proj.astype(BF16))
    small["ln3_g"], small["ln3_b"] = dg.sum(0, keepdims=True), db.sum(0, keepdims=True)
    dh2 = _ffn_out_dx(dr3_h, full["ffn2_w_out"], h2, "ffn2_out_dx")
    dw = _ffn_out_dw(a2, dr3_h, "ffn2_out_dw")
    send["ffn2_w_out"] = dw.reshape(N_DEV, -1, d_model)
    dw = _ffn_in_dw(x2_16, dh2, "ffn2_in_dw")
    send["ffn2_w_in"] = dw.reshape((N_DEV,) + dw.shape[2:])
    group1 = ("ffn2_w_in", "ffn2_w_out", "ple_w_gate", "ple_w_proj")
    exchange1, token = _exchange_start(
        [send[n] for n in group1] + [on_wire(n, small[n]) for n in SMALL_HEAD] + [loss_mine], dh2,
        "grad_exchange1_start")
    dr2, dr2_h, dg, db = _ffn_in_dx_ln(dh2, full["ffn2_w_in"], r2, dr3, alpha, row(wts["ln2_g"]), 1.0,
                                       "ffn2_in_dx_ln2_bwd", token=token)
    small["ln2_g"], small["ln2_b"] = dg.sum(0, keepdims=True), db.sum(0, keepdims=True)
    dm = _mm(dr2_h, full["mix_w_out"], tb=True, name="mix_out_dx")
    send["mix_w_out"] = _mm(m_mix, dr2_h, ta=True, name="mix_out_dw", out_dtype=BF16).reshape(
        N_DEV, -1, d_model)
    dga, dgb, ds5out, dgm, dw_a, dw_b = _merge_bwd(
        dm, y_a, y_b, s5out, gm, proj, full["up_a"], full["up_b"], "merge_bwd")
    send["up_a"] = _split_cols(dw_a.astype(BF16))
    send["up_b"] = _split_cols(dw_b.astype(BF16))

    dzu, dzv, dws, dbs_cols, dg, db = _gmlp_bwd(
        proj, dgm, row(wts["gmlp_ln_g"]), row(wts["gmlp_ln_b"]), ws_masked, ws_masked_t, bs_cols,
        "gmlp_bwd")
    small["gmlp_ln_g"], small["gmlp_ln_b"] = dg.sum(0, keepdims=True), db.sum(0, keepdims=True)
    small["gmlp_w_s"] = jnp.where(causal[None], dws, 0.0)[None]
    small["gmlp_b_s"] = dbs_cols.reshape(CHUNK, n_heads, -1).sum(-1).T[None]

    dy_p, dza_skip, dw_glu, dd, dgb_glu = _s5_post_bwd(
        yssm, za_p, _perm_rows(ds5out, n_seq), row(wts["ssm_d"]), full["ssm_glu_w"], row(wts["ssm_glu_b"]),
        "s5_post_bwd")
    send["ssm_glu_w"] = dw_glu.astype(BF16).reshape(N_DEV, -1, d_ssm)
    small["ssm_d"], small["ssm_glu_b"] = dd.sum(0, keepdims=True), dgb_glu.sum(0, keepdims=True)
    dza_p, dbb_re, dbb_im, dc_re, dc_im, da_re, da_im = _s5_bwd(
        dy_p, dza_skip, h_re, h_im, za_p, b_sg_re, b_sg_im, a_re, a_im, c_sg_re, c_sg_im, n_seq, "s5_bwd")
    small["ssm_c_re"] = _super_groups_out_take(dc_re, gl, n_ch, n_state)[None]
    small["ssm_c_im"] = -_super_groups_out_take(dc_im, gl, n_ch, n_state)[None]
    dbb_re = _super_groups_in_take(dbb_re, gl, n_state, n_ch)
    dbb_im = _super_groups_in_take(dbb_im, gl, n_state, n_ch)
    def first_channel(v):
        placed = jnp.pad(v.sum(0).reshape(n_groups, n_state, 1), ((0, 0), (0, 0), (0, n_ch - 1)))
        return placed.reshape(n_groups, -1)

    cotangents = (first_channel(da_re), first_channel(da_im), dbb_re.reshape(n_groups, -1),
                  dbb_im.reshape(n_groups, -1))
    d_lre, d_lim, d_ldt, d_bre, d_bim = _s5_discretise_bwd(disc_in, cotangents, "s5_discretise_bwd")
    per_state = (n_groups, n_state, n_ch)
    small["ssm_lambda_re"] = d_lre.reshape(per_state).sum(-1)[None]
    small["ssm_lambda_im"] = d_lim.reshape(per_state).sum(-1)[None]
    small["ssm_log_dt"] = d_ldt.sum(-1)[None]
    small["ssm_b_re"] = d_bre.reshape((1,) + per_state)
    small["ssm_b_im"] = d_bim.reshape((1,) + per_state)

    dproj = jnp.concatenate([_unperm_rows(dza_p, n_seq), dzu, dzv, dga, dgb], axis=1)
    group2 = ("mix_w_out", "up_a", "up_b", "ssm_glu_w")
    exchange2, token = _exchange_start(
        [send[n] for n in group2] + [on_wire(n, small[n]) for n in SMALL_MID], dproj, "grad_exchange2_start")
    send["mix_w_in"] = _split_cols(_mm(x1_16, dproj, ta=True, name="mix_in_dw", out_dtype=BF16, token=token))
    exchange3, token = _exchange_start([send["mix_w_in"]], dproj, "grad_exchange3_start")
    dr1, dr1_h, dg, db = _mm_ln_bwd(dproj, full["mix_w_in"], r1, dr2, alpha, row(wts["ln1_g"]), 0.5,
                                    "mix_in_dx_ln1_bwd", token=token)
    small["ln1_g"], small["ln1_b"] = dg.sum(0, keepdims=True), db.sum(0, keepdims=True)
    dw = _ffn_out_dw(a1, dr1_h, "ffn1_out_dw")
    send["ffn1_w_out"] = dw.reshape(N_DEV, -1, d_model)
    exchange4, token = _exchange_start([send["ffn1_w_out"]] + [on_wire(n, small[n]) for n in SMALL_LATE], dr1_h,
                                       "grad_exchange4_start")
    dh1 = _ffn_out_dx(dr1_h, full["ffn1_w_out"], h1, "ffn1_out_dx", token=token)
    dw = _ffn_in_dw(x16, dh1, "ffn1_in_dw")
    send["ffn1_w_in"] = dw.reshape((N_DEV,) + dw.shape[2:])
    exchange5, token = _exchange_start([send["ffn1_w_in"]], dh1, "grad_exchange5_start")
    grad_x = _ffn_in_dx(dh1, full["ffn1_w_in"], dr1, alpha, "ffn1_in_dx", token=token)

    results = {}

    def update(names, recv, prefix):
        light = [n for n in names if wts[n].size <= ADAM_GRIDLESS_ELEMS]
        for n, r in zip(names, recv):
            if n not in light:
                results[n] = _adamw(r, wts[n], mom[n], var[n], prefix + n)
        if light:
            items = [(r, wts[n], mom[n], var[n]) for n, r in zip(names, recv) if n in light]
            for n, outs in zip(light, _adamw_small(items, prefix + "light_" + light[0])):
                results[n] = tuple(outs)

    def update_small(names, recv, name):
        items = [(r, lane_dense(n, wts[n]), lane_dense(n, mom[n]), lane_dense(n, var[n])) for n, r in zip(names, recv)]
        for n, outs in zip(names, _adamw_small(items, name)):
            results[n] = tuple(o.reshape(wts[n].shape) for o in outs)

    def done(names):
        return [results[n][3] for n in names]

    recv = _exchange_wait(exchange1, [grad_x], "grad_exchange1_wait")
    update(group1, recv[:len(group1)], "adamw_")
    update_small(SMALL_HEAD, recv[len(group1):-1], "adamw_ln3")
    loss = jnp.sum(recv[-1][:, 0, 0, 0])
    recv = _exchange_wait(exchange2, done(group1 + SMALL_HEAD), "grad_exchange2_wait")
    update(group2, recv[:len(group2)], "adamw_")
    update_small(SMALL_MID, recv[len(group2):], "adamw_small")
    recv = _exchange_wait(exchange3, done(group2 + SMALL_MID), "grad_exchange3_wait")
    update(("mix_w_in",), recv, "adamw_")
    recv = _exchange_wait(exchange4, done(("mix_w_in",)), "grad_exchange4_wait")
    update(("ffn1_w_out",), recv[:1], "adamw_")
    update_small(SMALL_LATE, recv[1:], "adamw_ln1")
    recv = _exchange_wait(exchange5, done(("ffn1_w_out",) + SMALL_LATE), "grad_exchange5_wait")
    update(("ffn1_w_in",), recv, "adamw_")

    outs = [loss, grad_x.reshape(n_seq, seq, d_model)]
    for k in range(4):
        outs += [results[n][k][None] if n in SHARDED else results[n][k] for n in WEIGHTS]
    return tuple(outs)
```

```python
import math

import jax
import jax.numpy as jnp
from jax import lax
from jax.experimental import pallas as pl
from jax.experimental.pallas import tpu as pltpu

F32 = jnp.float32
BF16 = jnp.bfloat16

N_DEV = 8
LN_EPS = 1e-5
CHUNK = 128
N_SEG = 8
S5_CHANNELS_PER_STEP = 128
VMEM_LIMIT_BYTES = 56 * 1024 * 1024
MM_OPERAND_BLOCK_BYTES = 8 * 1024 * 1024
ADAM_BLOCK_BYTES = 6 * 1024 * 1024

ADAM_LR = 0.001
ADAM_B1 = 0.9
ADAM_B2 = 0.999
ADAM_EPS = 1e-08
ADAM_WD = 0.01
ADAM_STEP = 10

COL_SHARDED = ("ffn1_w_in", "mix_w_in", "up_a", "up_b", "ffn2_w_in", "ple_w_proj")
SHARDED = ("ffn1_w_in", "ffn1_w_out", "mix_w_in", "ssm_glu_w", "up_a", "up_b", "mix_w_out",
           "ffn2_w_in", "ffn2_w_out", "ple_w_proj", "ple_w_gate")
WEIGHTS = ("ffn1_w_in", "ffn1_w_out", "ln1_g", "ln1_b", "mix_w_in", "ssm_lambda_re", "ssm_lambda_im",
           "ssm_log_dt", "ssm_b_re", "ssm_b_im", "ssm_c_re", "ssm_c_im", "ssm_d", "ssm_glu_w",
           "ssm_glu_b", "gmlp_ln_g", "gmlp_ln_b", "gmlp_w_s", "gmlp_b_s", "up_a", "up_b", "mix_w_out",
           "ln2_g", "ln2_b", "ffn2_w_in", "ffn2_w_out", "ln3_g", "ln3_b", "ple_w_proj", "ple_w_gate")
GATHER_FIRST = ("ffn1_w_in",)
GATHER_EARLY = ("ffn1_w_out", "mix_w_in")
GATHER_MID = ("ssm_glu_w", "up_a", "up_b", "mix_w_out")
GATHER_LAST = ("ffn2_w_in", "ffn2_w_out", "ple_w_proj", "ple_w_gate")
SMALL = tuple(n for n in WEIGHTS if n not in SHARDED)
SMALL_HEAD = ("ln3_g", "ln3_b")
SMALL_LATE = ("ln1_g", "ln1_b")
SMALL_MID = tuple(n for n in SMALL if n not in SMALL_HEAD + SMALL_LATE)
SMALL_BF16 = ("gmlp_w_s", "ssm_b_re", "ssm_b_im", "ssm_c_re", "ssm_c_im")
INPUT_NAMES = ("x", "p") + WEIGHTS + ("loss_target",) + tuple("m_" + n for n in WEIGHTS) + tuple(
    "v_" + n for n in WEIGHTS)


def _pick(dim, pref, mult=128):
    if dim <= pref:
        return dim
    d = (pref // mult) * mult
    while d >= mult:
        if dim % d == 0:
            return d
        d -= mult
    return dim


def _params(n_axes=1):
    return pltpu.CompilerParams(dimension_semantics=("arbitrary",) * n_axes,
                                vmem_limit_bytes=VMEM_LIMIT_BYTES)


def _sigmoid(v):
    return 0.5 * jnp.tanh(0.5 * v) + 0.5


_GELU_C = math.sqrt(2.0 / math.pi)


def _gelu(v):
    return 0.5 * v * (1.0 + jnp.tanh(_GELU_C * (v + 0.044715 * (v * v * v))))


def _gelu_grad(v):
    t = jnp.tanh(_GELU_C * (v + 0.044715 * (v * v * v)))
    return 0.5 * (1.0 + t) + 0.5 * v * (1.0 - t * t) * (_GELU_C * (1.0 + 3.0 * 0.044715 * v * v))


def _ln_stats(r):
    mu = jnp.mean(r, axis=-1, keepdims=True)
    xc = r - mu
    var = jnp.mean(xc * xc, axis=-1, keepdims=True)
    rstd = lax.rsqrt(var + LN_EPS)
    return xc * rstd, rstd


def _ln_bwd(dy, xhat, rstd, g):
    dxh = dy * g
    m1 = jnp.mean(dxh, axis=-1, keepdims=True)
    m2 = jnp.mean(dxh * xhat, axis=-1, keepdims=True)
    return rstd * (dxh - m1 - xhat * m2)


def _fold8(v):
    rows, w = v.shape
    return jnp.sum(v.reshape(rows // 8, 8, w), axis=0)


def _dot(a, b, ta=False, tb=False):
    dn = (((0 if ta else 1,), (1 if tb else 0,)), ((), ()))
    return lax.dot_general(a.astype(BF16), b.astype(BF16), dn, preferred_element_type=F32)


_TOKEN = pl.BlockSpec((8, 128), lambda *_: (0, 0))


def _mm(a, b, *, name, ta=False, tb=False, out_dtype=F32, add=None, add_scale=1.0, token=None,
        tm=2048, tn=512, tk=4096):
    m_dim = a.shape[1] if ta else a.shape[0]
    k_dim = a.shape[0] if ta else a.shape[1]
    n_dim = b.shape[0] if tb else b.shape[1]
    assert (b.shape[1] if tb else b.shape[0]) == k_dim, (name, a.shape, b.shape)
    tk = _pick(k_dim, tk)
    tm = min(tm, max(256, MM_OPERAND_BLOCK_BYTES // (tk * a.dtype.itemsize)))
    tm, tn = _pick(m_dim, tm), _pick(n_dim, tn)
    nk = k_dim // tk
    has_add = add is not None

    def finish(r, add_ref, o_ref):
        if has_add:
            r = r + add_scale * add_ref[...].astype(F32)
        o_ref[...] = r.astype(out_dtype)

    def body(*refs):
        a_ref, b_ref = refs[:2]
        add_ref = refs[2] if has_add else None
        o_ref = refs[n_in]
        if nk == 1:
            finish(_dot(a_ref[...], b_ref[...], ta, tb), add_ref, o_ref)
            return
        acc_ref = refs[-1]
        k = pl.program_id(2)

        @pl.when(k == 0)
        def _():
            acc_ref[...] = jnp.zeros_like(acc_ref)

        acc_ref[...] += _dot(a_ref[...], b_ref[...], ta, tb)

        @pl.when(k == nk - 1)
        def _():
            finish(acc_ref[...], add_ref, o_ref)

    a_spec = (pl.BlockSpec((tk, tm), lambda i, j, k: (k, i)) if ta
              else pl.BlockSpec((tm, tk), lambda i, j, k: (i, k)))
    b_spec = (pl.BlockSpec((tn, tk), lambda i, j, k: (j, k)) if tb
              else pl.BlockSpec((tk, tn), lambda i, j, k: (k, j)))
    in_specs = [a_spec, b_spec]
    operands = [a, b]
    if has_add:
        in_specs.append(pl.BlockSpec((tm, tn), lambda i, j, k: (i, j)))
        operands.append(add)
    if token is not None:
        in_specs.append(_TOKEN)
        operands.append(token)
    n_in = len(operands)
    return pl.pallas_call(
        body, name=name,
        out_shape=jax.ShapeDtypeStruct((m_dim, n_dim), out_dtype),
        grid=(m_dim // tm, n_dim // tn, nk),
        in_specs=in_specs,
        out_specs=pl.BlockSpec((tm, tn), lambda i, j, k: (i, j)),
        scratch_shapes=[pltpu.VMEM((tm, tn), F32)] if nk > 1 else [],
        compiler_params=_params(3),
    )(*operands)


def _to_bf16(arrays, name, token=None):
    n = len(arrays)
    extra = [] if token is None else [token]

    def body(*refs):
        for src, dst in zip(refs[:n], refs[n + len(extra):]):
            dst[...] = src[...].astype(BF16)

    vmem = pl.BlockSpec(memory_space=pltpu.VMEM)
    return pl.pallas_call(
        body, name=name, out_shape=tuple(jax.ShapeDtypeStruct(a.shape, BF16) for a in arrays),
        in_specs=[vmem] * (n + len(extra)), out_specs=tuple([vmem] * n),
        compiler_params=pltpu.CompilerParams(vmem_limit_bytes=VMEM_LIMIT_BYTES))(*arrays, *extra)


def _rows(tr, w, cb=0):
    return pl.BlockSpec((tr, w), lambda i: (i, cb))


def _whole(shape):
    nd = len(shape)
    return pl.BlockSpec(tuple(shape), lambda i: (0,) * nd)


def _ffn_in(x16, w_in, name, token=None):
    t, d = x16.shape
    _, nb, _, fb = w_in.shape
    tm = _pick(t, 1024, 8)
    extra = [] if token is None else [token]

    def body(*refs):
        x_ref, wg_ref, wu_ref = refs[:3]
        h_ref, a_ref = refs[-2:]
        xv = x_ref[...]
        g = _dot(xv, wg_ref[0, 0])
        u = _dot(xv, wu_ref[0, 0])
        h_ref[0, 0] = g.astype(BF16)
        h_ref[0, 1] = u.astype(BF16)
        a_ref[0] = (g * _sigmoid(g) * u).astype(BF16)

    return pl.pallas_call(
        body, name=name,
        out_shape=(jax.ShapeDtypeStruct((nb, 2, t, fb), BF16), jax.ShapeDtypeStruct((nb, t, fb), BF16)),
        grid=(t // tm, nb),
        in_specs=[pl.BlockSpec((tm, d), lambda i, j: (i, 0)),
                  pl.BlockSpec((1, 1, d, fb), lambda i, j: (0, j, 0, 0)),
                  pl.BlockSpec((1, 1, d, fb), lambda i, j: (1, j, 0, 0))] + [_TOKEN] * len(extra),
        out_specs=(pl.BlockSpec((1, 2, tm, fb), lambda i, j: (j, 0, i, 0)),
                   pl.BlockSpec((1, tm, fb), lambda i, j: (j, i, 0))),
        compiler_params=_params(2))(x16, w_in, w_in, *extra)


def _ffn_out_ln(a, w_out, xin, g, b, alpha, name, scale=0.5, token=None):
    nb, t, fb = a.shape
    d = w_out.shape[2]
    tm = _pick(t, 512, 8)
    extra = [] if token is None else [token]

    def body(*refs):
        a_ref, w_ref, x_ref, g_ref, b_ref = refs[:5]
        r_ref, y_ref, y16_ref = refs[-3:]
        f = _dot(a_ref[0], w_ref[0])
        for jb in range(1, nb):
            f = f + _dot(a_ref[jb], w_ref[jb])
        r = alpha * x_ref[...] + scale * f
        xhat, _ = _ln_stats(r)
        y = xhat * g_ref[...] + b_ref[...]
        r_ref[...] = r
        y_ref[...] = y
        y16_ref[...] = y.astype(BF16)

    return pl.pallas_call(
        body, name=name,
        out_shape=(jax.ShapeDtypeStruct((t, d), F32), jax.ShapeDtypeStruct((t, d), F32),
                   jax.ShapeDtypeStruct((t, d), BF16)),
        grid=(t // tm,),
        in_specs=[pl.BlockSpec((nb, tm, fb), lambda i: (0, i, 0)), _whole(w_out.shape), _rows(tm, d),
                  _whole((1, d)), _whole((1, d))] + [_TOKEN] * len(extra),
        out_specs=(_rows(tm, d), _rows(tm, d), _rows(tm, d)),
        compiler_params=_params())(a, w_out, xin, g, b, *extra)


def _ffn_out_dx(drs, w_out, h, name, token=None):
    nb, _, t, fb = h.shape
    d = w_out.shape[2]
    tm = _pick(t, 1024, 8)
    extra = [] if token is None else [token]

    def body(*refs):
        dr_ref, w_ref, h_ref, dh_ref = refs[0], refs[1], refs[2], refs[-1]
        da = _dot(dr_ref[...], w_ref[0], tb=True)
        g, u = h_ref[0, 0].astype(F32), h_ref[0, 1].astype(F32)
        sg = _sigmoid(g)
        silu = g * sg
        dh_ref[0, 0] = ((da * sg) * u * (1.0 + (g - silu))).astype(BF16)
        dh_ref[0, 1] = (da * silu).astype(BF16)

    return pl.pallas_call(
        body, name=name, out_shape=jax.ShapeDtypeStruct((nb, 2, t, fb), BF16), grid=(t // tm, nb),
        in_specs=[pl.BlockSpec((tm, d), lambda i, j: (i, 0)),
                  pl.BlockSpec((1, fb, d), lambda i, j: (j, 0, 0)),
                  pl.BlockSpec((1, 2, tm, fb), lambda i, j: (j, 0, i, 0))] + [_TOKEN] * len(extra),
        out_specs=pl.BlockSpec((1, 2, tm, fb), lambda i, j: (j, 0, i, 0)),
        compiler_params=_params(2))(drs, w_out, h, *extra)


def _ffn_out_dw(a, drs, name):
    nb, t, fb = a.shape
    d = drs.shape[1]

    def body(a_ref, dr_ref, o_ref):
        o_ref[0] = _dot(a_ref[0], dr_ref[...], ta=True).astype(BF16)

    return pl.pallas_call(
        body, name=name, out_shape=jax.ShapeDtypeStruct((nb, fb, d), BF16), grid=(nb,),
        in_specs=[pl.BlockSpec((1, t, fb), lambda j: (j, 0, 0)), pl.BlockSpec((t, d), lambda j: (0, 0))],
        out_specs=pl.BlockSpec((1, fb, d), lambda j: (j, 0, 0)),
        compiler_params=_params())(a, drs)


def _ffn_in_dw(x16, dh, name, token=None):
    t, d = x16.shape
    nb, _, _, fb = dh.shape
    extra = [] if token is None else [token]

    def body(*refs):
        x_ref, dh_ref, o_ref = refs[0], refs[1], refs[-1]
        o_ref[0, 0] = _dot(x_ref[...], dh_ref[0, 0], ta=True).astype(BF16)

    return pl.pallas_call(
        body, name=name, out_shape=jax.ShapeDtypeStruct((2, nb, d, fb), BF16), grid=(nb, 2),
        in_specs=[pl.BlockSpec((t, d), lambda j, s: (0, 0)),
                  pl.BlockSpec((1, 1, t, fb), lambda j, s: (j, s, 0, 0))] + [_TOKEN] * len(extra),
        out_specs=pl.BlockSpec((1, 1, d, fb), lambda j, s: (s, j, 0, 0)),
        compiler_params=_params(2))(x16, dh, *extra)


def _ffn_in_dx(dh, w_in, add, add_scale, name, token=None):
    nb, _, t, fb = dh.shape
    d = w_in.shape[2]
    tm = _pick(t, 512, 8)
    has_add = add is not None
    extra = [] if token is None else [token]

    def body(*refs):
        dh_ref, w_ref = refs[:2]
        o_ref = refs[-1]
        acc = None
        for jb in range(nb):
            for s in range(2):
                part = _dot(dh_ref[jb, s], w_ref[s, jb], tb=True)
                acc = part if acc is None else acc + part
        if has_add:
            acc = acc + add_scale * refs[2][...]
        o_ref[...] = acc

    return pl.pallas_call(
        body, name=name, out_shape=jax.ShapeDtypeStruct((t, d), F32), grid=(t // tm,),
        in_specs=[pl.BlockSpec((nb, 2, tm, fb), lambda i: (0, 0, i, 0)), _whole(w_in.shape)]
        + ([_rows(tm, d)] if has_add else []) + [_TOKEN] * len(extra),
        out_specs=_rows(tm, d),
        compiler_params=_params())(*([dh, w_in] + ([add] if has_add else []) + extra))


def _ffn_in_dx_ln(dh, w_in, r, dy_b, b_scale, ln_g, out_scale, name, token=None):
    nb, _, t, fb = dh.shape
    d = w_in.shape[2]
    tm = _pick(t, 512, 8)
    extra = [] if token is None else [token]

    def body(*refs):
        dh_ref, w_ref, r_ref, dyb_ref, g_ref = refs[:5]
        dr_ref, drs_ref, dg_ref, dbeta_ref = refs[-4:]
        dy = b_scale * dyb_ref[...]
        for jb in range(nb):
            for s in range(2):
                dy = dy + _dot(dh_ref[jb, s], w_ref[s, jb], tb=True)
        xhat, rstd = _ln_stats(r_ref[...])
        dr = _ln_bwd(dy, xhat, rstd, g_ref[...])
        dr_ref[...] = dr
        drs_ref[...] = (out_scale * dr).astype(BF16)

        @pl.when(pl.program_id(0) == 0)
        def _():
            dg_ref[...] = jnp.zeros_like(dg_ref)
            dbeta_ref[...] = jnp.zeros_like(dbeta_ref)

        dg_ref[...] += _fold8(dy * xhat)
        dbeta_ref[...] += _fold8(dy)

    return pl.pallas_call(
        body, name=name,
        out_shape=(jax.ShapeDtypeStruct((t, d), F32), jax.ShapeDtypeStruct((t, d), BF16),
                   jax.ShapeDtypeStruct((8, d), F32), jax.ShapeDtypeStruct((8, d), F32)),
        grid=(t // tm,),
        in_specs=[pl.BlockSpec((nb, 2, tm, fb), lambda i: (0, 0, i, 0)), _whole(w_in.shape), _rows(tm, d),
                  _rows(tm, d), _whole((1, d))] + [_TOKEN] * len(extra),
        out_specs=(_rows(tm, d), _rows(tm, d), _whole((8, d)), _whole((8, d))),
        compiler_params=_params())(dh, w_in, r, dy_b, ln_g, *extra)


def _mm_ln_bwd(a, w, r, dy_b, b_scale, ln_g, out_scale, name, token=None):
    t, k_dim = a.shape
    d = w.shape[0]
    tm = _pick(t, 512, 8)
    extra = [] if token is None else [token]

    def body(*refs):
        a_ref, w_ref, r_ref, dyb_ref, g_ref = refs[:5]
        dr_ref, drs_ref, dg_ref, dbeta_ref = refs[-4:]
        dy = _dot(a_ref[...], w_ref[...], tb=True) + b_scale * dyb_ref[...]
        xhat, rstd = _ln_stats(r_ref[...])
        dr = _ln_bwd(dy, xhat, rstd, g_ref[...])
        dr_ref[...] = dr
        drs_ref[...] = (out_scale * dr).astype(BF16)

        @pl.when(pl.program_id(0) == 0)
        def _():
            dg_ref[...] = jnp.zeros_like(dg_ref)
            dbeta_ref[...] = jnp.zeros_like(dbeta_ref)

        dg_ref[...] += _fold8(dy * xhat)
        dbeta_ref[...] += _fold8(dy)

    return pl.pallas_call(
        body, name=name,
        out_shape=(jax.ShapeDtypeStruct((t, d), F32), jax.ShapeDtypeStruct((t, d), BF16),
                   jax.ShapeDtypeStruct((8, d), F32), jax.ShapeDtypeStruct((8, d), F32)),
        grid=(t // tm,),
        in_specs=[_rows(tm, k_dim), _whole(w.shape), _rows(tm, d), _rows(tm, d), _whole((1, d))]
        + [_TOKEN] * len(extra),
        out_specs=(_rows(tm, d), _rows(tm, d), _whole((8, d)), _whole((8, d))),
        compiler_params=_params())(a, w, r, dy_b, ln_g, *extra)


def _take_row(v, row_id, s):
    return jnp.sum(jnp.where(row_id == s, v, 0.0), axis=0, keepdims=True)


def _complex_power(ar, ai, n):
    out = None
    while n:
        if n & 1:
            out = (ar, ai) if out is None else (out[0] * ar - out[1] * ai, out[0] * ai + out[1] * ar)
        ar, ai = ar * ar - ai * ai, 2.0 * ar * ai
        n >>= 1
    return out


def _seg_carries(f_re, f_im, p_re, p_im, reverse):
    row_id = lax.broadcasted_iota(jnp.int32, f_re.shape, 0)
    p_re, p_im = _take_row(p_re, row_id, 0), _take_row(p_im, row_id, 0)
    out_re, out_im = jnp.zeros_like(f_re), jnp.zeros_like(f_im)
    c_re, c_im = jnp.zeros_like(p_re), jnp.zeros_like(p_im)
    order = range(N_SEG - 1, -1, -1) if reverse else range(N_SEG)
    for s in order:
        out_re = jnp.where(row_id == s, c_re, out_re)
        out_im = jnp.where(row_id == s, c_im, out_im)
        fr, fi = _take_row(f_re, row_id, s), _take_row(f_im, row_id, s)
        c_re, c_im = fr + p_re * c_re - p_im * c_im, fi + p_re * c_im + p_im * c_re
    return out_re, out_im


def _s5_fwd(za16, b_re, b_im, a_re, a_im, c_re, c_im, n_seq, name):
    t = za16.shape[0]
    n_sg, ch, st = b_re.shape
    seq = t // n_seq
    steps = seq // N_SEG

    def body(za_ref, bre_ref, bim_ref, are, aim, cre_ref, cim_ref, hre, him, y_ref):
        za = za_ref[...]
        hre[...] = _dot(za, bre_ref[0])
        him[...] = _dot(za, bim_ref[0])
        ar = jnp.broadcast_to(are[...], (N_SEG, st))
        ai = jnp.broadcast_to(aim[...], (N_SEG, st))
        zero = jnp.zeros((N_SEG, st), F32)
        p_re, p_im = _complex_power(ar, ai, steps)

        def local(k, carry):
            lr, li = carry
            rows = pl.ds(pl.multiple_of(k * N_SEG, N_SEG), N_SEG)
            nr = ar * lr - ai * li + hre[rows, :]
            ni = ar * li + ai * lr + him[rows, :]
            hre[rows, :] = nr
            him[rows, :] = ni
            return nr, ni

        f_re, f_im = lax.fori_loop(0, steps, local, (zero, zero))
        c_re, c_im = _seg_carries(f_re, f_im, p_re, p_im, reverse=False)

        def fix(k, carry):
            qr, qi = carry
            rows = pl.ds(pl.multiple_of(k * N_SEG, N_SEG), N_SEG)
            hre[rows, :] = hre[rows, :] + qr
            him[rows, :] = him[rows, :] + qi
            return ar * qr - ai * qi, ar * qi + ai * qr

        lax.fori_loop(0, steps, fix, (ar * c_re - ai * c_im, ar * c_im + ai * c_re))
        y_ref[...] = _dot(hre[...], cre_ref[0]) + _dot(him[...], cim_ref[0])

    rows_ch = pl.BlockSpec((seq, ch), lambda s, j: (s, j))
    rows_st = pl.BlockSpec((seq, st), lambda s, j: (s, j))
    vec = pl.BlockSpec((1, st), lambda s, j: (0, j))
    b_blk = pl.BlockSpec((1, ch, st), lambda s, j: (j, 0, 0))
    c_blk = pl.BlockSpec((1, st, ch), lambda s, j: (j, 0, 0))
    return pl.pallas_call(
        body, name=name,
        out_shape=(jax.ShapeDtypeStruct((t, n_sg * st), F32), jax.ShapeDtypeStruct((t, n_sg * st), F32),
                   jax.ShapeDtypeStruct((t, n_sg * ch), F32)),
        grid=(n_seq, n_sg), in_specs=[rows_ch, b_blk, b_blk, vec, vec, c_blk, c_blk],
        out_specs=(rows_st, rows_st, rows_ch),
        compiler_params=_params(2))(za16, b_re, b_im, a_re, a_im, c_re, c_im)


def _s5_bwd(dy16, dza_skip, h_re, h_im, za16, b_re, b_im, a_re, a_im, c_re, c_im, n_seq, name):
    t = dy16.shape[0]
    n_sg, ch, st = b_re.shape
    seq = t // n_seq
    steps = seq // N_SEG

    def body(dy_ref, skip_ref, hre, him, za_ref, bre_ref, bim_ref, are, aim, cre_ref, cim_ref,
             dza_ref, dbre_ref, dbim_ref, dcre_ref, dcim_ref, dare, daim, lre, lim):
        dy = dy_ref[...]
        lre[...] = _dot(dy, cre_ref[0], tb=True)
        lim[...] = _dot(dy, cim_ref[0], tb=True)
        ar = jnp.broadcast_to(are[...], (N_SEG, st))
        ai = -jnp.broadcast_to(aim[...], (N_SEG, st))
        zero = jnp.zeros((N_SEG, st), F32)
        p_re, p_im = _complex_power(ar, ai, steps)

        def local(i, carry):
            lr, li = carry
            k = steps - 1 - i
            rows = pl.ds(pl.multiple_of(k * N_SEG, N_SEG), N_SEG)
            nr = ar * lr - ai * li + lre[rows, :]
            ni = ar * li + ai * lr + lim[rows, :]
            lre[rows, :] = nr
            lim[rows, :] = ni
            return nr, ni

        f_re, f_im = lax.fori_loop(0, steps, local, (zero, zero))
        c_re, c_im = _seg_carries(f_re, f_im, p_re, p_im, reverse=True)

        def accumulate(lam_r, lam_i, hp_r, hp_i, acc_r, acc_i):
            return acc_r + (lam_r * hp_r + lam_i * hp_i), acc_i + (lam_i * hp_r - lam_r * hp_i)

        def fix(i, carry):
            qr, qi, acc_r, acc_i = carry
            k = steps - 1 - i
            rows = pl.ds(pl.multiple_of(k * N_SEG, N_SEG), N_SEG)
            prev = pl.ds(pl.multiple_of((k - 1) * N_SEG, N_SEG), N_SEG)
            lam_r = lre[rows, :] + qr
            lam_i = lim[rows, :] + qi
            lre[rows, :] = lam_r
            lim[rows, :] = lam_i
            acc_r, acc_i = accumulate(lam_r, lam_i, hre[prev, :], him[prev, :], acc_r, acc_i)
            return ar * qr - ai * qi, ar * qi + ai * qr, acc_r, acc_i

        qr, qi, acc_r, acc_i = lax.fori_loop(
            0, steps - 1, fix, (ar * c_re - ai * c_im, ar * c_im + ai * c_re, zero, zero))
        first = pl.ds(0, N_SEG)
        last = pl.ds((steps - 1) * N_SEG, N_SEG)
        lam_r = lre[first, :] + qr
        lam_i = lim[first, :] + qi
        lre[first, :] = lam_r
        lim[first, :] = lam_i
        row_id = lax.broadcasted_iota(jnp.int32, (N_SEG, st), 0)
        hp_r = jnp.where(row_id == 0, 0.0, pltpu.roll(hre[last, :], 1, 0))
        hp_i = jnp.where(row_id == 0, 0.0, pltpu.roll(him[last, :], 1, 0))
        acc_r, acc_i = accumulate(lam_r, lam_i, hp_r, hp_i, acc_r, acc_i)

        lam_re16 = lre[...].astype(BF16)
        lam_im16 = lim[...].astype(BF16)
        dza_ref[...] = (_dot(lam_re16, bre_ref[0], tb=True) + _dot(lam_im16, bim_ref[0], tb=True)
                        + skip_ref[...]).astype(BF16)

        @pl.when(pl.program_id(1) == 0)
        def _():
            for ref in (dbre_ref, dbim_ref, dcre_ref, dcim_ref, dare, daim):
                ref[...] = jnp.zeros_like(ref)

        za = za_ref[...]
        dbre_ref[0] += _dot(za, lam_re16, ta=True)
        dbim_ref[0] += _dot(za, lam_im16, ta=True)
        dcre_ref[0] += _dot(hre[...], dy, ta=True)
        dcim_ref[0] += _dot(him[...], dy, ta=True)
        dare[...] += acc_r
        daim[...] += acc_i

    rows_ch = pl.BlockSpec((seq, ch), lambda j, s: (s, j))
    rows_st = pl.BlockSpec((seq, st), lambda j, s: (s, j))
    vec = pl.BlockSpec((1, st), lambda j, s: (0, j))
    b_blk = pl.BlockSpec((1, ch, st), lambda j, s: (j, 0, 0))
    c_blk = pl.BlockSpec((1, st, ch), lambda j, s: (j, 0, 0))
    acc = pl.BlockSpec((N_SEG, st), lambda j, s: (0, j))
    return pl.pallas_call(
        body, name=name,
        out_shape=(jax.ShapeDtypeStruct((t, n_sg * ch), BF16),
                   jax.ShapeDtypeStruct((n_sg, ch, st), F32), jax.ShapeDtypeStruct((n_sg, ch, st), F32),
                   jax.ShapeDtypeStruct((n_sg, st, ch), F32), jax.ShapeDtypeStruct((n_sg, st, ch), F32),
                   jax.ShapeDtypeStruct((N_SEG, n_sg * st), F32), jax.ShapeDtypeStruct((N_SEG, n_sg * st), F32)),
        grid=(n_sg, n_seq),
        in_specs=[rows_ch, rows_ch, rows_st, rows_st, rows_ch, b_blk, b_blk, vec, vec, c_blk, c_blk],
        out_specs=(rows_ch, b_blk, b_blk, c_blk, c_blk, acc, acc),
        scratch_shapes=[pltpu.VMEM((seq, st), F32), pltpu.VMEM((seq, st), F32)],
        compiler_params=_params(2))(dy16, dza_skip, h_re, h_im, za16, b_re, b_im, a_re, a_im, c_re, c_im)


def _s5_post_fwd(yssm, u, d_skip, glu_w, glu_b, name):
    t, w = yssm.shape
    tr = _pick(t, 1024, 8)

    def body(y_ref, u_ref, d_ref, w_ref, b_ref, o_ref):
        yg = _gelu(y_ref[...] + d_ref[...] * u_ref[...])
        pre = _dot(yg, w_ref[...]) + b_ref[...]
        o_ref[...] = yg * _sigmoid(pre)

    return pl.pallas_call(
        body, name=name, out_shape=jax.ShapeDtypeStruct((t, w), F32), grid=(t // tr,),
        in_specs=[_rows(tr, w), _rows(tr, w), _whole((1, w)), _whole((w, w)), _whole((1, w))],
        out_specs=_rows(tr, w), compiler_params=_params())(yssm, u, d_skip, glu_w, glu_b)


def _s5_post_bwd(yssm, u, dout, d_skip, glu_w, glu_b, name):
    t, w = yssm.shape
    tr = _pick(t, 1024, 8)

    def body(y_ref, u_ref, do_ref, d_ref, w_ref, b_ref, dy_ref, du_ref, dw_ref, dd_ref, db_ref):
        uu = u_ref[...]
        y = y_ref[...] + d_ref[...] * uu
        yg = _gelu(y)
        sg = _sigmoid(_dot(yg, w_ref[...]) + b_ref[...])
        do = do_ref[...]
        dpre = do * yg * sg * (1.0 - sg)
        dyg = do * sg + _dot(dpre, w_ref[...], tb=True)
        dy = dyg * _gelu_grad(y)
        dy_ref[...] = dy.astype(BF16)
        du_ref[...] = dy * d_ref[...]

        @pl.when(pl.program_id(0) == 0)
        def _():
            dw_ref[...] = jnp.zeros_like(dw_ref)
            dd_ref[...] = jnp.zeros_like(dd_ref)
            db_ref[...] = jnp.zeros_like(db_ref)

        dw_ref[...] += _dot(yg, dpre, ta=True)
        dd_ref[...] += _fold8(dy * uu)
        db_ref[...] += _fold8(dpre)

    return pl.pallas_call(
        body, name=name,
        out_shape=(jax.ShapeDtypeStruct((t, w), BF16), jax.ShapeDtypeStruct((t, w), F32),
                   jax.ShapeDtypeStruct((w, w), F32), jax.ShapeDtypeStruct((8, w), F32),
                   jax.ShapeDtypeStruct((8, w), F32)),
        grid=(t // tr,),
        in_specs=[_rows(tr, w), _rows(tr, w), _rows(tr, w), _whole((1, w)), _whole((w, w)),
                  _whole((1, w))],
        out_specs=(_rows(tr, w), _rows(tr, w), _whole((w, w)), _whole((8, w)), _whole((8, w))),
        compiler_params=_params())(yssm, u, dout, d_skip, glu_w, glu_b)


def _head_select(parts, head_dim):
    col_head = lax.broadcasted_iota(jnp.int32, parts[0].shape, 1) // head_dim
    out = jnp.zeros_like(parts[0])
    for h, part in enumerate(parts):
        out = jnp.where(col_head == h, part, out)
    return out


def _gmlp_fwd(proj, ln_g, ln_b, ws, bs_cols, name):
    t = proj.shape[0]
    n_heads = ws.shape[0]
    w = bs_cols.shape[1]
    head_dim = w // n_heads
    cpb = _pick(t // CHUNK, 8, 1)
    tr = cpb * CHUNK

    def body(zu_ref, zv_ref, g_ref, b_ref, ws_ref, bs_ref, o_ref):
        for c in range(cpb):
            rows = pl.ds(c * CHUNK, CHUNK)
            xhat, _ = _ln_stats(_gelu(zv_ref[rows, :]))
            vn = (xhat * g_ref[...] + b_ref[...]).astype(BF16)
            s = _head_select([_dot(ws_ref[h], vn) for h in range(n_heads)], head_dim) + bs_ref[...]
            o_ref[rows, :] = _gelu(zu_ref[rows, :]) * s

    return pl.pallas_call(
        body, name=name, out_shape=jax.ShapeDtypeStruct((t, w), F32), grid=(t // tr,),
        in_specs=[_rows(tr, w, 1), _rows(tr, w, 2), _whole((1, w)), _whole((1, w)),
                  _whole(ws.shape), _whole(bs_cols.shape)],
        out_specs=_rows(tr, w), compiler_params=_params())(proj, proj, ln_g, ln_b, ws, bs_cols)


def _gmlp_bwd(proj, dout, ln_g, ln_b, ws, ws_t, bs_cols, name):
    t = proj.shape[0]
    n_heads = ws.shape[0]
    w = bs_cols.shape[1]
    head_dim = w // n_heads
    cpb = _pick(t // CHUNK, 8, 1)
    tr = cpb * CHUNK

    def body(zu_ref, zv_ref, do_ref, g_ref, b_ref, ws_ref, wst_ref, bs_ref,
             dzu_ref, dzv_ref, dws_ref, dbs_ref, dg_ref, dbeta_ref):
        @pl.when(pl.program_id(0) == 0)
        def _():
            dws_ref[...] = jnp.zeros_like(dws_ref)
            dbs_ref[...] = jnp.zeros_like(dbs_ref)
            dg_ref[...] = jnp.zeros_like(dg_ref)
            dbeta_ref[...] = jnp.zeros_like(dbeta_ref)

        col_head = lax.broadcasted_iota(jnp.int32, (CHUNK, w), 1) // head_dim
        for c in range(cpb):
            rows = pl.ds(c * CHUNK, CHUNK)
            zu, zv, do = zu_ref[rows, :], zv_ref[rows, :], do_ref[rows, :]
            xhat, rstd = _ln_stats(_gelu(zv))
            vn = (xhat * g_ref[...] + b_ref[...]).astype(BF16)
            s = _head_select([_dot(ws_ref[h], vn) for h in range(n_heads)], head_dim) + bs_ref[...]
            dzu_ref[rows, :] = (do * s * _gelu_grad(zu)).astype(BF16)
            ds = do * _gelu(zu)
            ds16 = ds.astype(BF16)
            dbs_ref[...] += ds
            for h in range(n_heads):
                dws_ref[h] += _dot(jnp.where(col_head == h, ds16, jnp.zeros_like(ds16)), vn, tb=True)
            dvn = _head_select([_dot(wst_ref[h], ds16) for h in range(n_heads)], head_dim)
            dg_ref[...] += _fold8(dvn * xhat)
            dbeta_ref[...] += _fold8(dvn)
            dgv = _ln_bwd(dvn, xhat, rstd, g_ref[...])
            dzv_ref[rows, :] = (dgv * _gelu_grad(zv)).astype(BF16)

    return pl.pallas_call(
        body, name=name,
        out_shape=(jax.ShapeDtypeStruct((t, w), BF16), jax.ShapeDtypeStruct((t, w), BF16),
                   jax.ShapeDtypeStruct(ws.shape, F32), jax.ShapeDtypeStruct((CHUNK, w), F32),
                   jax.ShapeDtypeStruct((8, w), F32), jax.ShapeDtypeStruct((8, w), F32)),
        grid=(t // tr,),
        in_specs=[_rows(tr, w, 1), _rows(tr, w, 2), _rows(tr, w), _whole((1, w)), _whole((1, w)),
                  _whole(ws.shape), _whole(ws.shape), _whole(bs_cols.shape)],
        out_specs=(_rows(tr, w), _rows(tr, w), _whole(ws.shape), _whole((CHUNK, w)),
                   _whole((8, w)), _whole((8, w))),
        compiler_params=_params())(proj, proj, dout, ln_g, ln_b, ws, ws_t, bs_cols)


def _merge_fwd(s5out, gm, proj, up_a, up_b, name):
    t, w = s5out.shape
    d = up_a.shape[1]
    assert d == 2 * w
    tr = _pick(t, 1024, 8)

    def body(s_ref, gm_ref, ga0, ga1, gb0, gb1, ua_ref, ub_ref, m_ref, ya_ref, yb_ref):
        ya = _dot(s_ref[...], ua_ref[...])
        yb = _dot(gm_ref[...], ub_ref[...])
        ya_ref[...] = ya.astype(BF16)
        yb_ref[...] = yb.astype(BF16)
        for half, (ga, gb) in enumerate(((ga0, gb0), (ga1, gb1))):
            cols = slice(half * w, (half + 1) * w)
            m_ref[:, cols] = (_sigmoid(ga[...]) * ya[:, cols] + _sigmoid(gb[...]) * yb[:, cols]).astype(BF16)

    return pl.pallas_call(
        body, name=name,
        out_shape=(jax.ShapeDtypeStruct((t, d), BF16), jax.ShapeDtypeStruct((t, d), BF16),
                   jax.ShapeDtypeStruct((t, d), BF16)),
        grid=(t // tr,),
        in_specs=[_rows(tr, w), _rows(tr, w), _rows(tr, w, 3), _rows(tr, w, 4), _rows(tr, w, 5),
                  _rows(tr, w, 6), _whole(up_a.shape), _whole(up_b.shape)],
        out_specs=(_rows(tr, d), _rows(tr, d), _rows(tr, d)),
        compiler_params=_params())(s5out, gm, proj, proj, proj, proj, up_a, up_b)


def _merge_bwd(dm, ya, yb, s5out, gm, proj, up_a, up_b, name):
    t, d = dm.shape
    w = d // 2
    tr = _pick(t, 512, 8)

    def body(dm_ref, ya_ref, yb_ref, s_ref, gm_ref, ga0, ga1, gb0, gb1, ua_ref, ub_ref,
             dga_ref, dgb_ref, ds_ref, dgm_ref, dua_ref, dub_ref):
        dm_v, ya, yb = dm_ref[...], ya_ref[...].astype(F32), yb_ref[...].astype(F32)
        dya, dyb = [], []
        for half, (ga, gb) in enumerate(((ga0, gb0), (ga1, gb1))):
            cols = slice(half * w, (half + 1) * w)
            sa, sb = _sigmoid(ga[...]), _sigmoid(gb[...])
            dmh = dm_v[:, cols]
            dga_ref[:, cols] = (dmh * ya[:, cols] * sa * (1.0 - sa)).astype(BF16)
            dgb_ref[:, cols] = (dmh * yb[:, cols] * sb * (1.0 - sb)).astype(BF16)
            dya.append((dmh * sa).astype(BF16))
            dyb.append((dmh * sb).astype(BF16))
        dya = jnp.concatenate(dya, axis=1)
        dyb = jnp.concatenate(dyb, axis=1)
        ds_ref[...] = _dot(dya, ua_ref[...], tb=True)
        dgm_ref[...] = _dot(dyb, ub_ref[...], tb=True)

        @pl.when(pl.program_id(0) == 0)
        def _():
            dua_ref[...] = jnp.zeros_like(dua_ref)
            dub_ref[...] = jnp.zeros_like(dub_ref)

        dua_ref[...] += _dot(s_ref[...], dya, ta=True)
        dub_ref[...] += _dot(gm_ref[...], dyb, ta=True)

    return pl.pallas_call(
        body, name=name,
        out_shape=(jax.ShapeDtypeStruct((t, d), BF16), jax.ShapeDtypeStruct((t, d), BF16),
                   jax.ShapeDtypeStruct((t, w), F32), jax.ShapeDtypeStruct((t, w), F32),
                   jax.ShapeDtypeStruct(up_a.shape, F32), jax.ShapeDtypeStruct(up_b.shape, F32)),
        grid=(t // tr,),
        in_specs=[_rows(tr, d), _rows(tr, d), _rows(tr, d), _rows(tr, w), _rows(tr, w),
                  _rows(tr, w, 3), _rows(tr, w, 4), _rows(tr, w, 5), _rows(tr, w, 6),
                  _whole(up_a.shape), _whole(up_b.shape)],
        out_specs=(_rows(tr, d), _rows(tr, d), _rows(tr, w), _rows(tr, w), _whole(up_a.shape),
                   _whole(up_b.shape)),
        compiler_params=_params())(dm, ya, yb, s5out, gm, proj, proj, proj, proj, up_a, up_b)


def _head(x3, r3, ln_g, p, target, w_gate, w_proj, out_scale, name):
    t, d = x3.shape
    pd = p.shape[1]
    tr = _pick(t, 512, 8)
    nb = t // tr

    def body(x_ref, r_ref, g_ref, p_ref, t_ref, wg_ref, wp_ref,
             loss_ref, dr_ref, drs_ref, dwg_ref, dwp_ref, dg_ref, dbeta_ref):
        xv = x_ref[...]
        sg = _sigmoid(_dot(xv, wg_ref[...]))
        pp = _dot(p_ref[...], wp_ref[...])
        err = xv + sg * pp - t_ref[...]
        loss_ref[...] = jnp.full((8, 128), 0.5 * jnp.sum(jnp.mean(err * err, axis=-1)), F32)
        dout = err * (1.0 / d)
        dgpre = dout * pp * sg * (1.0 - sg)
        dpp = dout * sg
        dx = dout + _dot(dgpre, wg_ref[...], tb=True)
        xhat, rstd = _ln_stats(r_ref[...])
        dr = _ln_bwd(dx, xhat, rstd, g_ref[...])
        dr_ref[...] = dr
        drs_ref[...] = (out_scale * dr).astype(BF16)

        @pl.when(pl.program_id(0) == 0)
        def _():
            for ref in (dwg_ref, dwp_ref, dg_ref, dbeta_ref):
                ref[...] = jnp.zeros_like(ref)

        dwg_ref[...] += _dot(xv, dgpre, ta=True)
        dwp_ref[...] += _dot(p_ref[...], dpp, ta=True)
        dg_ref[...] += _fold8(dx * xhat)
        dbeta_ref[...] += _fold8(dx)

    return pl.pallas_call(
        body, name=name,
        out_shape=(jax.ShapeDtypeStruct((nb * 8, 128), F32), jax.ShapeDtypeStruct((t, d), F32),
                   jax.ShapeDtypeStruct((t, d), BF16), jax.ShapeDtypeStruct((d, d), F32),
                   jax.ShapeDtypeStruct((pd, d), F32), jax.ShapeDtypeStruct((8, d), F32),
                   jax.ShapeDtypeStruct((8, d), F32)),
        grid=(nb,),
        in_specs=[_rows(tr, d), _rows(tr, d), _whole((1, d)), _rows(tr, pd), _rows(tr, d), _whole((d, d)),
                  _whole((pd, d))],
        out_specs=(pl.BlockSpec((8, 128), lambda i: (i, 0)), _rows(tr, d), _rows(tr, d), _whole((d, d)),
                   _whole((pd, d)), _whole((8, d)), _whole((8, d))),
        compiler_params=_params())(x3, r3, ln_g, p, target, w_gate, w_proj)


_HBM = pl.BlockSpec(memory_space=pltpu.HBM)


_SEM = pl.BlockSpec(memory_space=pltpu.SEMAPHORE)
_ANY = pl.BlockSpec(memory_space=pl.ANY)
_DATAFLOW = pltpu.SideEffectType.DATAFLOW_SIDE_EFFECTING


def _peer(k, x, y, c):
    return (1 - x if k & 4 else x, 1 - y if k & 2 else y, 1 - c if k & 1 else c)


def _exchange_start(sends, after, name):
    n = len(sends)
    lands = [lax.empty((N_DEV,) + s.shape[1:], s.dtype) for s in sends]

    def body(*refs):
        s_refs, l_refs = refs[:n], refs[n:2 * n]
        send_sems, recv_sems, local_sems, token = refs[2 * n + 1], refs[2 * n + 2], refs[2 * n + 3], refs[-1]
        x, y, c = lax.axis_index("x"), lax.axis_index("y"), lax.axis_index("c")
        me = 4 * x + 2 * y + c
        for a in range(n):
            same = sends[a].shape[0] == 1
            pltpu.make_async_copy(s_refs[a].at[0 if same else me], l_refs[a].at[me], local_sems.at[a]).start()
            for k in range(1, N_DEV):
                px, py, pc = _peer(k, x, y, c)
                pltpu.make_async_remote_copy(
                    src_ref=s_refs[a].at[0 if same else 4 * px + 2 * py + pc], dst_ref=l_refs[a].at[me],
                    send_sem=send_sems.at[7 * a + k - 1], recv_sem=recv_sems.at[7 * a + k - 1],
                    device_id=(px, py, pc), device_id_type=pl.DeviceIdType.MESH).start()
        token[...] = jnp.zeros_like(token)

    outs = pl.pallas_call(
        body, name=name,
        out_shape=(pltpu.SemaphoreType.DMA((7 * n,)), pltpu.SemaphoreType.DMA((7 * n,)),
                   pltpu.SemaphoreType.DMA((n,)),
                   *[pltpu.HBM(s.shape, s.dtype) for s in sends],
                   *[pltpu.HBM(l.shape, l.dtype) for l in lands],
                   jax.ShapeDtypeStruct((8, 128), F32)),
        in_specs=[_HBM] * (2 * n) + [_ANY],
        out_specs=(_SEM, _SEM, _SEM, *([_HBM] * (2 * n)), pl.BlockSpec(memory_space=pltpu.VMEM)),
        input_output_aliases={i: 3 + i for i in range(2 * n)},
        compiler_params=pltpu.CompilerParams(has_side_effects=_DATAFLOW),
    )(*[pltpu.with_memory_space_constraint(v, pltpu.HBM) for v in list(sends) + lands], after)
    return (outs[0], outs[1], outs[2], list(outs[3:3 + n]), list(outs[3 + n:3 + 2 * n])), outs[-1]


def _exchange_wait(handle, after, name):
    send_sems, recv_sems, local_sems, sends, lands = handle
    n = len(sends)

    def body(*refs):
        s_refs, l_refs = refs[:n], refs[n:2 * n]
        send_sems, recv_sems, local_sems = refs[2 * n], refs[2 * n + 1], refs[2 * n + 2]
        x, y, c = lax.axis_index("x"), lax.axis_index("y"), lax.axis_index("c")
        for a in range(n):
            pltpu.make_async_copy(s_refs[a].at[0], l_refs[a].at[0], local_sems.at[a]).wait()
            for k in range(1, N_DEV):
                copy = pltpu.make_async_remote_copy(
                    src_ref=s_refs[a].at[0], dst_ref=l_refs[a].at[0],
                    send_sem=send_sems.at[7 * a + k - 1], recv_sem=recv_sems.at[7 * a + k - 1],
                    device_id=_peer(k, x, y, c), device_id_type=pl.DeviceIdType.MESH)
                copy.wait_send()
                copy.wait_recv()

    outs = pl.pallas_call(
        body, name=name,
        out_shape=(*[pltpu.HBM(s.shape, s.dtype) for s in sends], *[pltpu.HBM(l.shape, l.dtype) for l in lands]),
        in_specs=[_HBM] * (2 * n) + [_SEM, _SEM, _SEM] + [_ANY] * len(after),
        out_specs=tuple([_HBM] * (2 * n)),
        input_output_aliases={i: i for i in range(2 * n)},
        compiler_params=pltpu.CompilerParams(has_side_effects=_DATAFLOW),
    )(*sends, *lands, send_sems, recv_sems, local_sems, *after)
    return list(outs[n:])


def _chips_of(x, y):
    return [(1 - x, y), (x, 1 - y), (1 - x, 1 - y)]


def _gather2_start(shards, after, name):
    n = len(shards)
    lands = [lax.empty((N_DEV,) + s.shape, s.dtype) for s in shards]

    def body(*refs):
        x_refs, l_refs = refs[:n], refs[n:2 * n]
        send_sems, recv_sems, local_sems, token = refs[2 * n + 1], refs[2 * n + 2], refs[2 * n + 3], refs[-1]
        x, y, c = lax.axis_index("x"), lax.axis_index("y"), lax.axis_index("c")
        me = 4 * x + 2 * y + c
        peers = [(x, y, 1 - c)] + [(*chip, c) for chip in _chips_of(x, y)]
        for a in range(n):
            pltpu.make_async_copy(x_refs[a], l_refs[a].at[me], local_sems.at[a]).start()
            for k, peer in enumerate(peers):
                pltpu.make_async_remote_copy(
                    src_ref=x_refs[a], dst_ref=l_refs[a].at[me],
                    send_sem=send_sems.at[4 * a + k], recv_sem=recv_sems.at[4 * a + k],
                    device_id=peer, device_id_type=pl.DeviceIdType.MESH).start()
        token[...] = jnp.zeros_like(token)

    outs = pl.pallas_call(
        body, name=name,
        out_shape=(pltpu.SemaphoreType.DMA((4 * n,)), pltpu.SemaphoreType.DMA((4 * n,)),
                   pltpu.SemaphoreType.DMA((n,)),
                   *[pltpu.HBM(s.shape, s.dtype) for s in shards],
                   *[pltpu.HBM(l.shape, l.dtype) for l in lands],
                   jax.ShapeDtypeStruct((8, 128), F32)),
        in_specs=[_HBM] * (2 * n) + [_ANY],
        out_specs=(_SEM, _SEM, _SEM, *([_HBM] * (2 * n)), pl.BlockSpec(memory_space=pltpu.VMEM)),
        input_output_aliases={i: 3 + i for i in range(2 * n)},
        compiler_params=pltpu.CompilerParams(has_side_effects=_DATAFLOW),
    )(*[pltpu.with_memory_space_constraint(v, pltpu.HBM) for v in list(shards) + lands], after)
    return (outs[0], outs[1], outs[2], list(outs[3:3 + n]), list(outs[3 + n:3 + 2 * n])), outs[-1]


def _gather2_forward(handle, after, name):
    send_sems, recv_sems, local_sems, shards, lands = handle
    n = len(shards)

    def body(*refs):
        x_refs, l_refs = refs[:n], refs[n:2 * n]
        send_sems, recv_sems = refs[2 * n], refs[2 * n + 1]
        out0 = 2 * n + 2 + len(after)
        fwd_send, fwd_recv, token = refs[out0], refs[out0 + 1], refs[-1]
        x, y, c = lax.axis_index("x"), lax.axis_index("y"), lax.axis_index("c")
        for a in range(n):
            for j, chip in enumerate(_chips_of(x, y)):
                block = l_refs[a].at[4 * chip[0] + 2 * chip[1] + c]
                pltpu.make_async_remote_copy(
                    src_ref=x_refs[a], dst_ref=block, send_sem=send_sems.at[4 * a + 1 + j],
                    recv_sem=recv_sems.at[4 * a + 1 + j], device_id=(*chip, c),
                    device_id_type=pl.DeviceIdType.MESH).wait_recv()
                pltpu.make_async_remote_copy(
                    src_ref=block, dst_ref=block, send_sem=fwd_send.at[3 * a + j], recv_sem=fwd_recv.at[3 * a + j],
                    device_id=(x, y, 1 - c), device_id_type=pl.DeviceIdType.MESH).start()
        token[...] = jnp.zeros_like(token)

    outs = pl.pallas_call(
        body, name=name,
        out_shape=(pltpu.SemaphoreType.DMA((3 * n,)), pltpu.SemaphoreType.DMA((3 * n,)),
                   *[pltpu.HBM(s.shape, s.dtype) for s in shards], *[pltpu.HBM(l.shape, l.dtype) for l in lands],
                   jax.ShapeDtypeStruct((8, 128), F32)),
        in_specs=[_HBM] * (2 * n) + [_SEM, _SEM] + [_ANY] * len(after),
        out_specs=(_SEM, _SEM, *([_HBM] * (2 * n)), pl.BlockSpec(memory_space=pltpu.VMEM)),
        input_output_aliases={i: 2 + i for i in range(2 * n)},
        compiler_params=pltpu.CompilerParams(has_side_effects=_DATAFLOW),
    )(*shards, *lands, send_sems, recv_sems, *after)
    handle = (send_sems, recv_sems, local_sems, outs[0], outs[1], list(outs[2:2 + n]), list(outs[2 + n:2 + 2 * n]))
    return handle, outs[-1]


def _gather2_wait(handle, after, name):
    send_sems, recv_sems, local_sems, fwd_send, fwd_recv, shards, lands = handle
    n = len(shards)

    def body(*refs):
        x_refs, l_refs = refs[:n], refs[n:2 * n]
        send_sems, recv_sems, local_sems, fwd_send, fwd_recv = refs[2 * n:2 * n + 5]
        x, y, c = lax.axis_index("x"), lax.axis_index("y"), lax.axis_index("c")
        sibling = (x, y, 1 - c)

        def copy(a, send_sem, recv_sem):
            return pltpu.make_async_remote_copy(
                src_ref=x_refs[a], dst_ref=l_refs[a].at[0], send_sem=send_sem, recv_sem=recv_sem,
                device_id=sibling, device_id_type=pl.DeviceIdType.MESH)

        for a in range(n):
            pltpu.make_async_copy(x_refs[a], l_refs[a].at[0], local_sems.at[a]).wait()
            for k in range(4):
                copy(a, send_sems.at[4 * a + k], recv_sems.at[4 * a + k]).wait_send()
            copy(a, send_sems.at[4 * a], recv_sems.at[4 * a]).wait_recv()
            for j in range(3):
                passed = copy(a, fwd_send.at[3 * a + j], fwd_recv.at[3 * a + j])
                passed.wait_send()
                passed.wait_recv()

    outs = pl.pallas_call(
        body, name=name,
        out_shape=(*[pltpu.HBM(s.shape, s.dtype) for s in shards], *[pltpu.HBM(l.shape, l.dtype) for l in lands]),
        in_specs=[_HBM] * (2 * n) + [_SEM] * 5 + [_ANY] * len(after),
        out_specs=tuple([_HBM] * (2 * n)),
        input_output_aliases={i: i for i in range(2 * n)},
        compiler_params=pltpu.CompilerParams(has_side_effects=_DATAFLOW),
    )(*shards, *lands, send_sems, recv_sems, local_sems, fwd_send, fwd_recv, *after)
    return list(outs[n:])


def _adamw(recv, w, m, v, name):
    n, rows, width = recv.shape
    tr = _pick(rows, max(8, ADAM_BLOCK_BYTES // (n * width * recv.dtype.itemsize)), 8)
    c_m = 1.0 - ADAM_B1 ** ADAM_STEP
    c_v = 1.0 - ADAM_B2 ** ADAM_STEP

    def body(r_ref, w_ref, m_ref, v_ref, g_ref, d_ref, nm_ref, nv_ref):
        g = r_ref[0].astype(F32)
        for i in range(1, n):
            g = g + r_ref[i].astype(F32)
        m2 = ADAM_B1 * m_ref[...] + (1.0 - ADAM_B1) * g
        v2 = ADAM_B2 * v_ref[...] + (1.0 - ADAM_B2) * (g * g)
        m_hat = m2 / c_m
        v_hat = v2 / c_v
        g_ref[...] = g
        d_ref[...] = -ADAM_LR * (m_hat / (jnp.sqrt(v_hat) + ADAM_EPS) + ADAM_WD * w_ref[...])
        nm_ref[...] = m2
        nv_ref[...] = v2

    blk = _rows(tr, width)
    shp = jax.ShapeDtypeStruct((rows, width), F32)
    return pl.pallas_call(
        body, name=name, out_shape=(shp, shp, shp, shp), grid=(rows // tr,),
        in_specs=[pl.BlockSpec((n, tr, width), lambda i: (0, i, 0)), blk, blk, blk],
        out_specs=(blk, blk, blk, blk), compiler_params=_params())(recv, w, m, v)


def _join_cols(gathered):
    n, r, c = gathered.shape
    return gathered.transpose(1, 0, 2).reshape(r, n * c)


def _split_cols(full):
    r, c = full.shape
    return full.reshape(r, N_DEV, c // N_DEV).transpose(1, 0, 2)


def _adamw_small(items, name):
    n = len(items)
    c_m = 1.0 - ADAM_B1 ** ADAM_STEP
    c_v = 1.0 - ADAM_B2 ** ADAM_STEP

    def body(*refs):
        ins, outs = refs[:4 * n], refs[4 * n:]
        for k in range(n):
            r_ref, w_ref, m_ref, v_ref = ins[4 * k:4 * k + 4]
            g = r_ref[0].astype(F32)
            for i in range(1, N_DEV):
                g = g + r_ref[i].astype(F32)
            m2 = ADAM_B1 * m_ref[...] + (1.0 - ADAM_B1) * g
            v2 = ADAM_B2 * v_ref[...] + (1.0 - ADAM_B2) * (g * g)
            outs[4 * k][...] = g
            outs[4 * k + 1][...] = -ADAM_LR * ((m2 / c_m) / (jnp.sqrt(v2 / c_v) + ADAM_EPS) + ADAM_WD * w_ref[...])
            outs[4 * k + 2][...] = m2
            outs[4 * k + 3][...] = v2

    vmem = pl.BlockSpec(memory_space=pltpu.VMEM)
    flat = pl.pallas_call(
        body, name=name,
        out_shape=tuple(jax.ShapeDtypeStruct(w.shape, F32) for _, w, _, _ in items for _ in range(4)),
        in_specs=[vmem] * (4 * n), out_specs=tuple([vmem] * (4 * n)),
        compiler_params=pltpu.CompilerParams(vmem_limit_bytes=VMEM_LIMIT_BYTES),
    )(*[a for item in items for a in item])
    return [tuple(flat[4 * k:4 * k + 4]) for k in range(n)]


def _discretise(lam_re, lam_im, log_dt, b_re, b_im):
    dt = jnp.exp(log_dt)
    mag = jnp.exp(lam_re * dt)
    ab_re = mag * jnp.cos(lam_im * dt)
    ab_im = mag * jnp.sin(lam_im * dt)
    nr = ab_re - 1.0
    ni = ab_im
    den = lam_re * lam_re + lam_im * lam_im
    coef_re = (nr * lam_re + ni * lam_im) / den
    coef_im = (ni * lam_re - nr * lam_im) / den
    return ab_re, ab_im, coef_re * b_re - coef_im * b_im, coef_re * b_im + coef_im * b_re


def _s5_discretise(operands, name, token):
    def body(*refs):
        for out_ref, v in zip(refs[6:], _discretise(*[r[...] for r in refs[:5]])):
            out_ref[...] = v

    vmem = pl.BlockSpec(memory_space=pltpu.VMEM)
    shape = jax.ShapeDtypeStruct(operands[0].shape, F32)
    return pl.pallas_call(body, name=name, out_shape=(shape,) * 4, in_specs=[vmem] * 6,
                          out_specs=(vmem,) * 4)(*operands, token)


def _s5_discretise_bwd(operands, cotangents, name):
    def body(*refs):
        _, vjp = jax.vjp(_discretise, *[r[...] for r in refs[:5]])
        for out_ref, v in zip(refs[9:], vjp(tuple(r[...] for r in refs[5:9]))):
            out_ref[...] = v

    vmem = pl.BlockSpec(memory_space=pltpu.VMEM)
    shape = jax.ShapeDtypeStruct(operands[0].shape, F32)
    return pl.pallas_call(body, name=name, out_shape=(shape,) * 5, in_specs=[vmem] * 9,
                          out_specs=(vmem,) * 5)(*operands, *cotangents)


def _same_group(gl):
    return jnp.eye(gl, dtype=bool)[None, :, None, :, None]


def _super_groups_in(bb, gl):
    g, p, i = bb.shape
    blocks = bb.reshape(g // gl, gl, p, i).transpose(0, 1, 3, 2)[:, :, :, None, :]
    return jnp.where(_same_group(gl), blocks, 0.0).reshape(g // gl, gl * i, gl * p)


def _super_groups_in_take(dense, gl, p, i):
    n_sg = dense.shape[0]
    blocks = jnp.where(_same_group(gl), dense.reshape(n_sg, gl, i, gl, p), 0.0).sum(axis=3)
    return blocks.transpose(0, 1, 3, 2).reshape(n_sg * gl, p, i)


def _super_groups_out(cc, gl):
    g, i, p = cc.shape
    blocks = cc.reshape(g // gl, gl, i, p).transpose(0, 1, 3, 2)[:, :, :, None, :]
    return jnp.where(_same_group(gl), blocks, 0.0).reshape(g // gl, gl * p, gl * i)


def _super_groups_out_take(dense, gl, i, p):
    n_sg = dense.shape[0]
    blocks = jnp.where(_same_group(gl), dense.reshape(n_sg, gl, p, gl, i), 0.0).sum(axis=3)
    return blocks.transpose(0, 1, 3, 2).reshape(n_sg * gl, i, p)


def _perm_rows(z, n_seq):
    t, w = z.shape
    steps = t // n_seq // N_SEG
    return z.reshape(n_seq, N_SEG, steps, w).transpose(0, 2, 1, 3).reshape(t, w)


def _unperm_rows(z, n_seq):
    t, w = z.shape
    steps = t // n_seq // N_SEG
    return z.reshape(n_seq, steps, N_SEG, w).transpose(0, 2, 1, 3).reshape(t, w)


def kernel(*args):
    assert len(args) == len(INPUT_NAMES)
    inp = dict(zip(INPUT_NAMES, args))
    depth = inp["ffn1_w_in"].shape[0]
    assert depth == 1
    alpha = (2.0 * depth) ** 0.25

    n_seq, seq, d_model = inp["x"].shape
    t = n_seq * seq
    x = inp["x"].reshape(t, d_model)
    target = inp["loss_target"].reshape(t, d_model)
    wts = {n: inp[n][0] if n in SHARDED else inp[n] for n in WEIGHTS}
    mom = {n: inp["m_" + n][0] if n in SHARDED else inp["m_" + n] for n in WEIGHTS}
    var = {n: inp["v_" + n][0] if n in SHARDED else inp["v_" + n] for n in WEIGHTS}

    full = {}

    def place(n, g):
        if n in ("ffn1_w_in", "ffn2_w_in"):
            full[n] = g.reshape((2, N_DEV // 2) + g.shape[1:])
        elif n in ("ffn1_w_out", "ffn2_w_out"):
            full[n] = g.reshape(N_DEV // 2, 2 * g.shape[1], g.shape[2])
        elif n in COL_SHARDED:
            full[n] = _join_cols(g)
        else:
            full[n] = g.reshape(N_DEV * g.shape[1], g.shape[2])

    w16 = dict(zip(GATHER_FIRST, _to_bf16([wts[n] for n in GATHER_FIRST], "cast_ffn1")))
    gather_first, gather_first_token = _gather2_start([w16[n] for n in GATHER_FIRST], w16[GATHER_FIRST[0]],
                                                      "gather_first_start")
    later = tuple(n for n in SHARDED if n not in GATHER_FIRST)
    casts = _to_bf16([wts[n] for n in later] + [x, inp["p"][0].reshape(t, -1)], "cast_rest",
                     token=gather_first_token)
    w16.update(zip(later, casts))
    x16, p16 = casts[-2:]

    def gather_start(names, after, name):
        return _exchange_start([w16[n][None] for n in names], after, name)

    def gather_wait(names, handle, after, name):
        for n, g in zip(names, _exchange_wait(handle, after, name)):
            place(n, g)

    def gather2_wait(names, handle, after, name):
        for n, g in zip(names, _gather2_wait(handle, after, name)):
            place(n, g)


    def row(v):
        return v.reshape(1, -1)

    _, n_groups, n_state = wts["ssm_lambda_re"].shape
    n_ch = wts["ssm_b_re"].shape[3]
    disc_in = (jnp.repeat(wts["ssm_lambda_re"][0], n_ch, axis=1), jnp.repeat(wts["ssm_lambda_im"][0], n_ch, axis=1),
               jnp.broadcast_to(wts["ssm_log_dt"][0][:, None], (n_groups, n_state * n_ch)),
               wts["ssm_b_re"][0].reshape(n_groups, -1), wts["ssm_b_im"][0].reshape(n_groups, -1))
    ab_re, ab_im, bb_re, bb_im = _s5_discretise(disc_in, "s5_discretise", gather_first_token)
    a_re, a_im = row(ab_re[:, ::n_ch]), row(ab_im[:, ::n_ch])
    bb_re, bb_im = bb_re.reshape(n_groups, n_state, n_ch), bb_im.reshape(n_groups, n_state, n_ch)
    gl = S5_CHANNELS_PER_STEP // n_ch
    b_sg_re = _super_groups_in(bb_re, gl).astype(BF16)
    b_sg_im = _super_groups_in(bb_im, gl).astype(BF16)
    c_sg_re = _super_groups_out(wts["ssm_c_re"][0], gl).astype(BF16)
    c_sg_im = _super_groups_out(-wts["ssm_c_im"][0], gl).astype(BF16)

    ws = wts["gmlp_w_s"][0]
    n_heads = ws.shape[0]
    d_gmlp = wts["gmlp_ln_g"].shape[1]
    causal = jnp.tril(jnp.ones((CHUNK, CHUNK), dtype=bool))
    ws_masked = jnp.where(causal[None], ws, 0.0).astype(BF16)
    ws_masked_t = ws_masked.transpose(0, 2, 1)
    bs_cols = jnp.repeat(wts["gmlp_b_s"][0].T, d_gmlp // n_heads, axis=1)
    d_ssm = n_groups * n_ch
    assert d_ssm == d_gmlp and d_model == 2 * d_ssm

    prepared = [x16, b_sg_re, b_sg_im, c_sg_re, c_sg_im, ws_masked_t, bs_cols]
    gather_first, _ = _gather2_forward(gather_first, prepared, "gather_first_forward")
    gather2_wait(GATHER_FIRST, gather_first, prepared, "gather_first_wait")
    gather_early, gather_early_token = _gather2_start([w16[n] for n in GATHER_EARLY], full["ffn1_w_in"],
                                                      "gather_early_start")
    h1, a1 = _ffn_in(x16, full["ffn1_w_in"], "ffn1_in", token=gather_early_token)
    gather_early, _ = _gather2_forward(gather_early, [a1], "gather_early_forward")
    gather2_wait(GATHER_EARLY, gather_early, [a1], "gather_early_wait")
    r1, x1, x1_16 = _ffn_out_ln(a1, full["ffn1_w_out"], x, row(wts["ln1_g"]), row(wts["ln1_b"]), alpha,
                                "ffn1_out_ln")

    gather_mid, token = gather_start(GATHER_MID, x1_16, "gather_mid_start")
    gather_last, token = _gather2_start([w16[n] for n in GATHER_LAST], token, "gather_last_start")
    proj = _mm(x1_16, full["mix_w_in"], name="mix_in", token=token)
    za_p = _perm_rows(proj[:, :d_ssm], n_seq)
    h_re, h_im, yssm = _s5_fwd(za_p, b_sg_re, b_sg_im, a_re, a_im, c_sg_re, c_sg_im, n_seq, "s5_fwd")
    gather_wait(GATHER_MID, gather_mid, [yssm], "gather_mid_wait")
    s5out_p = _s5_post_fwd(yssm, za_p, row(wts["ssm_d"]), full["ssm_glu_w"], row(wts["ssm_glu_b"]),
                           "s5_post")
    s5out = _unperm_rows(s5out_p, n_seq)
    gm = _gmlp_fwd(proj, row(wts["gmlp_ln_g"]), row(wts["gmlp_ln_b"]), ws_masked, bs_cols, "gmlp")
    m_mix, y_a, y_b = _merge_fwd(s5out, gm, proj, full["up_a"], full["up_b"], "merge")
    gather_last, token = _gather2_forward(gather_last, [m_mix], "gather_last_forward")
    r2, x2, x2_16 = _ffn_out_ln(m_mix[None], full["mix_w_out"][None], x1, row(wts["ln2_g"]), row(wts["ln2_b"]),
                                alpha, "mix_out_ln2", scale=1.0, token=token)

    gather2_wait(GATHER_LAST, gather_last, [x2_16], "gather_last_wait")
    h2, a2 = _ffn_in(x2_16, full["ffn2_w_in"], "ffn2_in")
    r3, x3, _ = _ffn_out_ln(a2, full["ffn2_w_out"], x2, row(wts["ln3_g"]), row(wts["ln3_b"]), alpha,
                            "ffn2_out_ln")

    small = {}
    send = {}

    def lane_dense(n, v):
        return v.reshape(v.shape[:2] + (-1,)) if n in ("ssm_b_re", "ssm_b_im") else v

    def on_wire(n, g):
        g = lane_dense(n, g)
        return (g.astype(BF16) if n in SMALL_BF16 else g)[None]
    loss_parts, dr3, dr3_h, dw_gate, dw_proj, dg, db = _head(
        x3, r3, row(wts["ln3_g"]), p16, target, full["ple_w_gate"], full["ple_w_proj"], 0.5, "head")
    loss_mine = jnp.full((1, 1, 8, 128), jnp.sum(loss_parts[::8, 0]), F32)
    send["ple_w_gate"] = dw_gate.astype(BF16).reshape(N_DEV, -1, d_model)
    send["ple_w_proj"] = _split_cols(dw_proj.astype(BF16))
    small["ln3_g"], small["ln3_b"] = dg.sum(0, keepdims=True), db.sum(0, keepdims=True)
    dh2 = _ffn_out_dx(dr3_h, full["ffn2_w_out"], h2, "ffn2_out_dx")
    dw = _ffn_out_dw(a2, dr3_h, "ffn2_out_dw")
    send["ffn2_w_out"] = dw.reshape(N_DEV, -1, d_model)
    dw = _ffn_in_dw(x2_16, dh2, "ffn2_in_dw")
    send["ffn2_w_in"] = dw.reshape((N_DEV,) + dw.shape[2:])
    group1 = ("ffn2_w_in", "ffn2_w_out", "ple_w_gate", "ple_w_proj")
    exchange1, token = _exchange_start(
        [send[n] for n in group1] + [on_wire(n, small[n]) for n in SMALL_HEAD] + [loss_mine], dh2,
        "grad_exchange1_start")
    dr2, dr2_h, dg, db = _ffn_in_dx_ln(dh2, full["ffn2_w_in"], r2, dr3, alpha, row(wts["ln2_g"]), 1.0,
                                       "ffn2_in_dx_ln2_bwd", token=token)
    small["ln2_g"], small["ln2_b"] = dg.sum(0, keepdims=True), db.sum(0, keepdims=True)
    dm = _mm(dr2_h, full["mix_w_out"], tb=True, name="mix_out_dx")
    send["mix_w_out"] = _mm(m_mix, dr2_h, ta=True, name="mix_out_dw", out_dtype=BF16).reshape(
        N_DEV, -1, d_model)
    dga, dgb, ds5out, dgm, dw_a, dw_b = _merge_bwd(
        dm, y_a, y_b, s5out, gm, proj, full["up_a"], full["up_b"], "merge_bwd")
    send["up_a"] = _split_cols(dw_a.astype(BF16))
    send["up_b"] = _split_cols(dw_b.astype(BF16))

    dzu, dzv, dws, dbs_cols, dg, db = _gmlp_bwd(
        proj, dgm, row(wts["gmlp_ln_g"]), row(wts["gmlp_ln_b"]), ws_masked, ws_masked_t, bs_cols,
        "gmlp_bwd")
    small["gmlp_ln_g"], small["gmlp_ln_b"] = dg.sum(0, keepdims=True), db.sum(0, keepdims=True)
    small["gmlp_w_s"] = jnp.where(causal[None], dws, 0.0)[None]
    small["gmlp_b_s"] = dbs_cols.reshape(CHUNK, n_heads, -1).sum(-1).T[None]

    dy_p, dza_skip, dw_glu, dd, dgb_glu = _s5_post_bwd(
        yssm, za_p, _perm_rows(ds5out, n_seq), row(wts["ssm_d"]), full["ssm_glu_w"], row(wts["ssm_glu_b"]),
        "s5_post_bwd")
    send["ssm_glu_w"] = dw_glu.astype(BF16).reshape(N_DEV, -1, d_ssm)
    small["ssm_d"], small["ssm_glu_b"] = dd.sum(0, keepdims=True), dgb_glu.sum(0, keepdims=True)
    dza_p, dbb_re, dbb_im, dc_re, dc_im, da_re, da_im = _s5_bwd(
        dy_p, dza_skip, h_re, h_im, za_p, b_sg_re, b_sg_im, a_re, a_im, c_sg_re, c_sg_im, n_seq, "s5_bwd")
    small["ssm_c_re"] = _super_groups_out_take(dc_re, gl, n_ch, n_state)[None]
    small["ssm_c_im"] = -_super_groups_out_take(dc_im, gl, n_ch, n_state)[None]
    dbb_re = _super_groups_in_take(dbb_re, gl, n_state, n_ch)
    dbb_im = _super_groups_in_take(dbb_im, gl, n_state, n_ch)
    def first_channel(v):
        placed = jnp.pad(v.sum(0).reshape(n_groups, n_state, 1), ((0, 0), (0, 0), (0, n_ch - 1)))
        return placed.reshape(n_groups, -1)

    cotangents = (first_channel(da_re), first_channel(da_im), dbb_re.reshape(n_groups, -1),
                  dbb_im.reshape(n_groups, -1))
    d_lre, d_lim, d_ldt, d_bre, d_bim = _s5_discretise_bwd(disc_in, cotangents, "s5_discretise_bwd")
    per_state = (n_groups, n_state, n_ch)
    small["ssm_lambda_re"] = d_lre.reshape(per_state).sum(-1)[None]
    small["ssm_lambda_im"] = d_lim.reshape(per_state).sum(-1)[None]
    small["ssm_log_dt"] = d_ldt.sum(-1)[None]
    small["ssm_b_re"] = d_bre.reshape((1,) + per_state)
    small["ssm_b_im"] = d_bim.reshape((1,) + per_state)

    dproj = jnp.concatenate([_unperm_rows(dza_p, n_seq), dzu, dzv, dga, dgb], axis=1)
    group2 = ("mix_w_out", "up_a", "up_b", "ssm_glu_w")
    exchange2, token = _exchange_start(
        [send[n] for n in group2] + [on_wire(n, small[n]) for n in SMALL_MID], dproj, "grad_exchange2_start")
    send["mix_w_in"] = _split_cols(_mm(x1_16, dproj, ta=True, name="mix_in_dw", out_dtype=BF16, token=token))
    exchange3, token = _exchange_start([send["mix_w_in"]], dproj, "grad_exchange3_start")
    dr1, dr1_h, dg, db = _mm_ln_bwd(dproj, full["mix_w_in"], r1, dr2, alpha, row(wts["ln1_g"]), 0.5,
                                    "mix_in_dx_ln1_bwd", token=token)
    small["ln1_g"], small["ln1_b"] = dg.sum(0, keepdims=True), db.sum(0, keepdims=True)
    dw = _ffn_out_dw(a1, dr1_h, "ffn1_out_dw")
    send["ffn1_w_out"] = dw.reshape(N_DEV, -1, d_model)
    exchange4, token = _exchange_start([send["ffn1_w_out"]] + [on_wire(n, small[n]) for n in SMALL_LATE], dr1_h,
                                       "grad_exchange4_start")
    dh1 = _ffn_out_dx(dr1_h, full["ffn1_w_out"], h1, "ffn1_out_dx", token=token)
    dw = _ffn_in_dw(x16, dh1, "ffn1_in_dw")
    send["ffn1_w_in"] = dw.reshape((N_DEV,) + dw.shape[2:])
    exchange5, token = _exchange_start([send["ffn1_w_in"]], dh1, "grad_exchange5_start")
    grad_x = _ffn_in_dx(dh1, full["ffn1_w_in"], dr1, alpha, "ffn1_in_dx", token=token)

    results = {}

    def update(names, recv, prefix):
        for n, r in zip(names, recv):
            results[n] = _adamw(r, wts[n], mom[n], var[n], prefix + n)

    def update_small(names, recv, name):
        items = [(r, lane_dense(n, wts[n]), lane_dense(n, mom[n]), lane_dense(n, var[n])) for n, r in zip(names, recv)]
        for n, outs in zip(names, _adamw_small(items, name)):
            results[n] = tuple(o.reshape(wts[n].shape) for o in outs)

    def done(names):
        return [results[n][3] for n in names]

    recv = _exchange_wait(exchange1, [grad_x], "grad_exchange1_wait")
    update(group1, recv[:len(group1)], "adamw_")
    update_small(SMALL_HEAD, recv[len(group1):-1], "adamw_ln3")
    loss = jnp.sum(recv[-1][:, 0, 0, 0])
    recv = _exchange_wait(exchange2, done(group1 + SMALL_HEAD), "grad_exchange2_wait")
    update(group2, recv[:len(group2)], "adamw_")
    update_small(SMALL_MID, recv[len(group2):], "adamw_small")
    recv = _exchange_wait(exchange3, done(group2 + SMALL_MID), "grad_exchange3_wait")
    update(("mix_w_in",), recv, "adamw_")
    recv = _exchange_wait(exchange4, done(("mix_w_in",)), "grad_exchange4_wait")
    update(("ffn1_w_out",), recv[:1], "adamw_")
    update_small(SMALL_LATE, recv[1:], "adamw_ln1")
    recv = _exchange_wait(exchange5, done(("ffn1_w_out",) + SMALL_LATE), "grad_exchange5_wait")
    update(("ffn1_w_in",), recv, "adamw_")

    outs = [loss, grad_x.reshape(n_seq, seq, d_model)]
    for k in range(4):
        outs += [results[n][k][None] if n in SHARDED else results[n][k] for n in WEIGHTS]
    return tuple(outs)
```

```python
import math

import jax
import jax.numpy as jnp
from jax import lax
from jax.experimental import pallas as pl
from jax.experimental.pallas import tpu as pltpu

F32 = jnp.float32
BF16 = jnp.bfloat16

N_DEV = 8
LN_EPS = 1e-5
CHUNK = 128
N_SEG = 8
S5_CHANNELS_PER_STEP = 128
S5_FWD_GROUPS = 2
VMEM_LIMIT_BYTES = 56 * 1024 * 1024
MM_OPERAND_BLOCK_BYTES = 8 * 1024 * 1024
ADAM_BLOCK_BYTES = 6 * 1024 * 1024

ADAM_LR = 0.001
ADAM_B1 = 0.9
ADAM_B2 = 0.999
ADAM_EPS = 1e-08
ADAM_WD = 0.01
ADAM_STEP = 10

COL_SHARDED = ("ffn1_w_in", "mix_w_in", "up_a", "up_b", "ffn2_w_in", "ple_w_proj")
SHARDED = ("ffn1_w_in", "ffn1_w_out", "mix_w_in", "ssm_glu_w", "up_a", "up_b", "mix_w_out",
           "ffn2_w_in", "ffn2_w_out", "ple_w_proj", "ple_w_gate")
WEIGHTS = ("ffn1_w_in", "ffn1_w_out", "ln1_g", "ln1_b", "mix_w_in", "ssm_lambda_re", "ssm_lambda_im",
           "ssm_log_dt", "ssm_b_re", "ssm_b_im", "ssm_c_re", "ssm_c_im", "ssm_d", "ssm_glu_w",
           "ssm_glu_b", "gmlp_ln_g", "gmlp_ln_b", "gmlp_w_s", "gmlp_b_s", "up_a", "up_b", "mix_w_out",
           "ln2_g", "ln2_b", "ffn2_w_in", "ffn2_w_out", "ln3_g", "ln3_b", "ple_w_proj", "ple_w_gate")
GATHER_FIRST = ("ffn1_w_in",)
GATHER_EARLY = ("ffn1_w_out", "mix_w_in")
GATHER_MID = ("ssm_glu_w", "up_a", "up_b", "mix_w_out")
GATHER_LAST = ("ffn2_w_in", "ffn2_w_out", "ple_w_proj", "ple_w_gate")
SMALL = tuple(n for n in WEIGHTS if n not in SHARDED)
SMALL_HEAD = ("ln3_g", "ln3_b")
SMALL_LATE = ("ln1_g", "ln1_b")
SMALL_MID = tuple(n for n in SMALL if n not in SMALL_HEAD + SMALL_LATE)
SMALL_BF16 = ("gmlp_w_s", "ssm_b_re", "ssm_b_im", "ssm_c_re", "ssm_c_im")
INPUT_NAMES = ("x", "p") + WEIGHTS + ("loss_target",) + tuple("m_" + n for n in WEIGHTS) + tuple(
    "v_" + n for n in WEIGHTS)


def _pick(dim, pref, mult=128):
    if dim <= pref:
        return dim
    d = (pref // mult) * mult
    while d >= mult:
        if dim % d == 0:
            return d
        d -= mult
    return dim


def _params(n_axes=1):
    return pltpu.CompilerParams(dimension_semantics=("arbitrary",) * n_axes,
                                vmem_limit_bytes=VMEM_LIMIT_BYTES)


def _sigmoid(v):
    return 0.5 * jnp.tanh(0.5 * v) + 0.5


_GELU_C = math.sqrt(2.0 / math.pi)


def _gelu(v):
    return 0.5 * v * (1.0 + jnp.tanh(_GELU_C * (v + 0.044715 * (v * v * v))))


def _gelu_grad(v):
    t = jnp.tanh(_GELU_C * (v + 0.044715 * (v * v * v)))
    return 0.5 * (1.0 + t) + 0.5 * v * (1.0 - t * t) * (_GELU_C * (1.0 + 3.0 * 0.044715 * v * v))


def _ln_stats(r):
    mu = jnp.mean(r, axis=-1, keepdims=True)
    xc = r - mu
    var = jnp.mean(xc * xc, axis=-1, keepdims=True)
    rstd = lax.rsqrt(var + LN_EPS)
    return xc * rstd, rstd


def _ln_bwd(dy, xhat, rstd, g):
    dxh = dy * g
    m1 = jnp.mean(dxh, axis=-1, keepdims=True)
    m2 = jnp.mean(dxh * xhat, axis=-1, keepdims=True)
    return rstd * (dxh - m1 - xhat * m2)


def _fold8(v):
    rows, w = v.shape
    return jnp.sum(v.reshape(rows // 8, 8, w), axis=0)


def _dot(a, b, ta=False, tb=False):
    dn = (((0 if ta else 1,), (1 if tb else 0,)), ((), ()))
    return lax.dot_general(a.astype(BF16), b.astype(BF16), dn, preferred_element_type=F32)


_TOKEN = pl.BlockSpec((8, 128), lambda *_: (0, 0))


def _mm(a, b, *, name, ta=False, tb=False, out_dtype=F32, add=None, add_scale=1.0, token=None,
        tm=2048, tn=512, tk=4096):
    m_dim = a.shape[1] if ta else a.shape[0]
    k_dim = a.shape[0] if ta else a.shape[1]
    n_dim = b.shape[0] if tb else b.shape[1]
    assert (b.shape[1] if tb else b.shape[0]) == k_dim, (name, a.shape, b.shape)
    tk = _pick(k_dim, tk)
    tm = min(tm, max(256, MM_OPERAND_BLOCK_BYTES // (tk * a.dtype.itemsize)))
    tm, tn = _pick(m_dim, tm), _pick(n_dim, tn)
    nk = k_dim // tk
    has_add = add is not None

    def finish(r, add_ref, o_ref):
        if has_add:
            r = r + add_scale * add_ref[...].astype(F32)
        o_ref[...] = r.astype(out_dtype)

    def body(*refs):
        a_ref, b_ref = refs[:2]
        add_ref = refs[2] if has_add else None
        o_ref = refs[n_in]
        if nk == 1:
            finish(_dot(a_ref[...], b_ref[...], ta, tb), add_ref, o_ref)
            return
        acc_ref = refs[-1]
        k = pl.program_id(2)

        @pl.when(k == 0)
        def _():
            acc_ref[...] = jnp.zeros_like(acc_ref)

        acc_ref[...] += _dot(a_ref[...], b_ref[...], ta, tb)

        @pl.when(k == nk - 1)
        def _():
            finish(acc_ref[...], add_ref, o_ref)

    a_spec = (pl.BlockSpec((tk, tm), lambda i, j, k: (k, i)) if ta
              else pl.BlockSpec((tm, tk), lambda i, j, k: (i, k)))
    b_spec = (pl.BlockSpec((tn, tk), lambda i, j, k: (j, k)) if tb
              else pl.BlockSpec((tk, tn), lambda i, j, k: (k, j)))
    in_specs = [a_spec, b_spec]
    operands = [a, b]
    if has_add:
        in_specs.append(pl.BlockSpec((tm, tn), lambda i, j, k: (i, j)))
        operands.append(add)
    if token is not None:
        in_specs.append(_TOKEN)
        operands.append(token)
    n_in = len(operands)
    return pl.pallas_call(
        body, name=name,
        out_shape=jax.ShapeDtypeStruct((m_dim, n_dim), out_dtype),
        grid=(m_dim // tm, n_dim // tn, nk),
        in_specs=in_specs,
        out_specs=pl.BlockSpec((tm, tn), lambda i, j, k: (i, j)),
        scratch_shapes=[pltpu.VMEM((tm, tn), F32)] if nk > 1 else [],
        compiler_params=_params(3),
    )(*operands)


def _to_bf16(arrays, name, token=None):
    n = len(arrays)
    extra = [] if token is None else [token]

    def body(*refs):
        for src, dst in zip(refs[:n], refs[n + len(extra):]):
            dst[...] = src[...].astype(BF16)

    vmem = pl.BlockSpec(memory_space=pltpu.VMEM)
    return pl.pallas_call(
        body, name=name, out_shape=tuple(jax.ShapeDtypeStruct(a.shape, BF16) for a in arrays),
        in_specs=[vmem] * (n + len(extra)), out_specs=tuple([vmem] * n),
        compiler_params=pltpu.CompilerParams(vmem_limit_bytes=VMEM_LIMIT_BYTES))(*arrays, *extra)


def _rows(tr, w, cb=0):
    return pl.BlockSpec((tr, w), lambda i: (i, cb))


def _whole(shape):
    nd = len(shape)
    return pl.BlockSpec(tuple(shape), lambda i: (0,) * nd)


def _ffn_in(x16, w_in, name, token=None):
    t, d = x16.shape
    _, nb, _, fb = w_in.shape
    tm = _pick(t, 1024, 8)
    extra = [] if token is None else [token]

    def body(*refs):
        x_ref, wg_ref, wu_ref = refs[:3]
        h_ref, a_ref = refs[-2:]
        xv = x_ref[...]
        g = _dot(xv, wg_ref[0, 0])
        u = _dot(xv, wu_ref[0, 0])
        h_ref[0, 0] = g.astype(BF16)
        h_ref[0, 1] = u.astype(BF16)
        a_ref[0] = (g * _sigmoid(g) * u).astype(BF16)

    return pl.pallas_call(
        body, name=name,
        out_shape=(jax.ShapeDtypeStruct((nb, 2, t, fb), BF16), jax.ShapeDtypeStruct((nb, t, fb), BF16)),
        grid=(t // tm, nb),
        in_specs=[pl.BlockSpec((tm, d), lambda i, j: (i, 0)),
                  pl.BlockSpec((1, 1, d, fb), lambda i, j: (0, j, 0, 0)),
                  pl.BlockSpec((1, 1, d, fb), lambda i, j: (1, j, 0, 0))] + [_TOKEN] * len(extra),
        out_specs=(pl.BlockSpec((1, 2, tm, fb), lambda i, j: (j, 0, i, 0)),
                   pl.BlockSpec((1, tm, fb), lambda i, j: (j, i, 0))),
        compiler_params=_params(2))(x16, w_in, w_in, *extra)


def _ffn_out_ln(a, w_out, xin, g, b, alpha, name, scale=0.5, token=None):
    nb, t, fb = a.shape
    d = w_out.shape[2]
    tm = _pick(t, 512, 8)
    extra = [] if token is None else [token]

    def body(*refs):
        a_ref, w_ref, x_ref, g_ref, b_ref = refs[:5]
        r_ref, y_ref, y16_ref = refs[-3:]
        f = _dot(a_ref[0], w_ref[0])
        for jb in range(1, nb):
            f = f + _dot(a_ref[jb], w_ref[jb])
        r = alpha * x_ref[...] + scale * f
        xhat, _ = _ln_stats(r)
        y = xhat * g_ref[...] + b_ref[...]
        r_ref[...] = r
        y_ref[...] = y
        y16_ref[...] = y.astype(BF16)

    return pl.pallas_call(
        body, name=name,
        out_shape=(jax.ShapeDtypeStruct((t, d), F32), jax.ShapeDtypeStruct((t, d), F32),
                   jax.ShapeDtypeStruct((t, d), BF16)),
        grid=(t // tm,),
        in_specs=[pl.BlockSpec((nb, tm, fb), lambda i: (0, i, 0)), _whole(w_out.shape), _rows(tm, d),
                  _whole((1, d)), _whole((1, d))] + [_TOKEN] * len(extra),
        out_specs=(_rows(tm, d), _rows(tm, d), _rows(tm, d)),
        compiler_params=_params())(a, w_out, xin, g, b, *extra)


def _ffn_out_dx(drs, w_out, h, name, token=None):
    nb, _, t, fb = h.shape
    d = w_out.shape[2]
    tm = _pick(t, 1024, 8)
    extra = [] if token is None else [token]

    def body(*refs):
        dr_ref, w_ref, h_ref, dh_ref = refs[0], refs[1], refs[2], refs[-1]
        da = _dot(dr_ref[...], w_ref[0], tb=True)
        g, u = h_ref[0, 0].astype(F32), h_ref[0, 1].astype(F32)
        sg = _sigmoid(g)
        silu = g * sg
        dh_ref[0, 0] = ((da * sg) * u * (1.0 + (g - silu))).astype(BF16)
        dh_ref[0, 1] = (da * silu).astype(BF16)

    return pl.pallas_call(
        body, name=name, out_shape=jax.ShapeDtypeStruct((nb, 2, t, fb), BF16), grid=(t // tm, nb),
        in_specs=[pl.BlockSpec((tm, d), lambda i, j: (i, 0)),
                  pl.BlockSpec((1, fb, d), lambda i, j: (j, 0, 0)),
                  pl.BlockSpec((1, 2, tm, fb), lambda i, j: (j, 0, i, 0))] + [_TOKEN] * len(extra),
        out_specs=pl.BlockSpec((1, 2, tm, fb), lambda i, j: (j, 0, i, 0)),
        compiler_params=_params(2))(drs, w_out, h, *extra)


def _ffn_out_dw(a, drs, name):
    nb, t, fb = a.shape
    d = drs.shape[1]

    def body(a_ref, dr_ref, o_ref):
        o_ref[0] = _dot(a_ref[0], dr_ref[...], ta=True).astype(BF16)

    return pl.pallas_call(
        body, name=name, out_shape=jax.ShapeDtypeStruct((nb, fb, d), BF16), grid=(nb,),
        in_specs=[pl.BlockSpec((1, t, fb), lambda j: (j, 0, 0)), pl.BlockSpec((t, d), lambda j: (0, 0))],
        out_specs=pl.BlockSpec((1, fb, d), lambda j: (j, 0, 0)),
        compiler_params=_params())(a, drs)


def _ffn_in_dw(x16, dh, name, token=None):
    t, d = x16.shape
    nb, _, _, fb = dh.shape
    extra = [] if token is None else [token]

    def body(*refs):
        x_ref, dh_ref, o_ref = refs[0], refs[1], refs[-1]
        o_ref[0, 0] = _dot(x_ref[...], dh_ref[0, 0], ta=True).astype(BF16)

    return pl.pallas_call(
        body, name=name, out_shape=jax.ShapeDtypeStruct((2, nb, d, fb), BF16), grid=(nb, 2),
        in_specs=[pl.BlockSpec((t, d), lambda j, s: (0, 0)),
                  pl.BlockSpec((1, 1, t, fb), lambda j, s: (j, s, 0, 0))] + [_TOKEN] * len(extra),
        out_specs=pl.BlockSpec((1, 1, d, fb), lambda j, s: (s, j, 0, 0)),
        compiler_params=_params(2))(x16, dh, *extra)


def _ffn_in_dx(dh, w_in, add, add_scale, name, token=None):
    nb, _, t, fb = dh.shape
    d = w_in.shape[2]
    tm = _pick(t, 512, 8)
    has_add = add is not None
    extra = [] if token is None else [token]

    def body(*refs):
        dh_ref, w_ref = refs[:2]
        o_ref = refs[-1]
        acc = None
        for jb in range(nb):
            for s in range(2):
                part = _dot(dh_ref[jb, s], w_ref[s, jb], tb=True)
                acc = part if acc is None else acc + part
        if has_add:
            acc = acc + add_scale * refs[2][...]
        o_ref[...] = acc

    return pl.pallas_call(
        body, name=name, out_shape=jax.ShapeDtypeStruct((t, d), F32), grid=(t // tm,),
        in_specs=[pl.BlockSpec((nb, 2, tm, fb), lambda i: (0, 0, i, 0)), _whole(w_in.shape)]
        + ([_rows(tm, d)] if has_add else []) + [_TOKEN] * len(extra),
        out_specs=_rows(tm, d),
        compiler_params=_params())(*([dh, w_in] + ([add] if has_add else []) + extra))


def _ffn_in_dx_ln(dh, w_in, r, dy_b, b_scale, ln_g, out_scale, name, token=None):
    nb, _, t, fb = dh.shape
    d = w_in.shape[2]
    tm = _pick(t, 512, 8)
    extra = [] if token is None else [token]

    def body(*refs):
        dh_ref, w_ref, r_ref, dyb_ref, g_ref = refs[:5]
        dr_ref, drs_ref, dg_ref, dbeta_ref = refs[-4:]
        dy = b_scale * dyb_ref[...]
        for jb in range(nb):
            for s in range(2):
                dy = dy + _dot(dh_ref[jb, s], w_ref[s, jb], tb=True)
        xhat, rstd = _ln_stats(r_ref[...])
        dr = _ln_bwd(dy, xhat, rstd, g_ref[...])
        dr_ref[...] = dr
        drs_ref[...] = (out_scale * dr).astype(BF16)

        @pl.when(pl.program_id(0) == 0)
        def _():
            dg_ref[...] = jnp.zeros_like(dg_ref)
            dbeta_ref[...] = jnp.zeros_like(dbeta_ref)

        dg_ref[...] += _fold8(dy * xhat)
        dbeta_ref[...] += _fold8(dy)

    return pl.pallas_call(
        body, name=name,
        out_shape=(jax.ShapeDtypeStruct((t, d), F32), jax.ShapeDtypeStruct((t, d), BF16),
                   jax.ShapeDtypeStruct((8, d), F32), jax.ShapeDtypeStruct((8, d), F32)),
        grid=(t // tm,),
        in_specs=[pl.BlockSpec((nb, 2, tm, fb), lambda i: (0, 0, i, 0)), _whole(w_in.shape), _rows(tm, d),
                  _rows(tm, d), _whole((1, d))] + [_TOKEN] * len(extra),
        out_specs=(_rows(tm, d), _rows(tm, d), _whole((8, d)), _whole((8, d))),
        compiler_params=_params())(dh, w_in, r, dy_b, ln_g, *extra)


def _mm_ln_bwd(a, w, r, dy_b, b_scale, ln_g, out_scale, name, token=None):
    t, k_dim = a.shape
    d = w.shape[0]
    tm = _pick(t, 512, 8)
    extra = [] if token is None else [token]

    def body(*refs):
        a_ref, w_ref, r_ref, dyb_ref, g_ref = refs[:5]
        dr_ref, drs_ref, dg_ref, dbeta_ref = refs[-4:]
        dy = _dot(a_ref[...], w_ref[...], tb=True) + b_scale * dyb_ref[...]
        xhat, rstd = _ln_stats(r_ref[...])
        dr = _ln_bwd(dy, xhat, rstd, g_ref[...])
        dr_ref[...] = dr
        drs_ref[...] = (out_scale * dr).astype(BF16)

        @pl.when(pl.program_id(0) == 0)
        def _():
            dg_ref[...] = jnp.zeros_like(dg_ref)
            dbeta_ref[...] = jnp.zeros_like(dbeta_ref)

        dg_ref[...] += _fold8(dy * xhat)
        dbeta_ref[...] += _fold8(dy)

    return pl.pallas_call(
        body, name=name,
        out_shape=(jax.ShapeDtypeStruct((t, d), F32), jax.ShapeDtypeStruct((t, d), BF16),
                   jax.ShapeDtypeStruct((8, d), F32), jax.ShapeDtypeStruct((8, d), F32)),
        grid=(t // tm,),
        in_specs=[_rows(tm, k_dim), _whole(w.shape), _rows(tm, d), _rows(tm, d), _whole((1, d))]
        + [_TOKEN] * len(extra),
        out_specs=(_rows(tm, d), _rows(tm, d), _whole((8, d)), _whole((8, d))),
        compiler_params=_params())(a, w, r, dy_b, ln_g, *extra)


def _take_row(v, row_id, s):
    return jnp.sum(jnp.where(row_id == s, v, 0.0), axis=0, keepdims=True)


def _complex_power(ar, ai, n):
    out = None
    while n:
        if n & 1:
            out = (ar, ai) if out is None else (out[0] * ar - out[1] * ai, out[0] * ai + out[1] * ar)
        ar, ai = ar * ar - ai * ai, 2.0 * ar * ai
        n >>= 1
    return out


def _seg_carries(f_re, f_im, p_re, p_im, reverse):
    row_id = lax.broadcasted_iota(jnp.int32, f_re.shape, 0)
    p_re, p_im = _take_row(p_re, row_id, 0), _take_row(p_im, row_id, 0)
    out_re, out_im = jnp.zeros_like(f_re), jnp.zeros_like(f_im)
    c_re, c_im = jnp.zeros_like(p_re), jnp.zeros_like(p_im)
    order = range(N_SEG - 1, -1, -1) if reverse else range(N_SEG)
    for s in order:
        out_re = jnp.where(row_id == s, c_re, out_re)
        out_im = jnp.where(row_id == s, c_im, out_im)
        fr, fi = _take_row(f_re, row_id, s), _take_row(f_im, row_id, s)
        c_re, c_im = fr + p_re * c_re - p_im * c_im, fi + p_re * c_im + p_im * c_re
    return out_re, out_im


def _s5_fwd(za16, b_re, b_im, a_re, a_im, c_re, c_im, n_seq, name):
    t = za16.shape[0]
    n_sg, ch1, st1 = b_re.shape
    per = S5_FWD_GROUPS if n_sg % S5_FWD_GROUPS == 0 else 1
    ch, st = per * ch1, per * st1
    seq = t // n_seq
    steps = seq // N_SEG

    def body(za_ref, bre_ref, bim_ref, are, aim, cre_ref, cim_ref, hre, him, y_ref):
        for g in range(per):
            za = za_ref[:, g * ch1:(g + 1) * ch1]
            hre[:, g * st1:(g + 1) * st1] = _dot(za, bre_ref[g])
            him[:, g * st1:(g + 1) * st1] = _dot(za, bim_ref[g])
        ar = jnp.broadcast_to(are[...], (N_SEG, st))
        ai = jnp.broadcast_to(aim[...], (N_SEG, st))
        zero = jnp.zeros((N_SEG, st), F32)
        p_re, p_im = _complex_power(ar, ai, steps)

        def local(k, carry):
            lr, li = carry
            rows = pl.ds(pl.multiple_of(k * N_SEG, N_SEG), N_SEG)
            nr = ar * lr - ai * li + hre[rows, :]
            ni = ar * li + ai * lr + him[rows, :]
            hre[rows, :] = nr
            him[rows, :] = ni
            return nr, ni

        f_re, f_im = lax.fori_loop(0, steps, local, (zero, zero))
        c_re, c_im = _seg_carries(f_re, f_im, p_re, p_im, reverse=False)

        def fix(k, carry):
            qr, qi = carry
            rows = pl.ds(pl.multiple_of(k * N_SEG, N_SEG), N_SEG)
            hre[rows, :] = hre[rows, :] + qr
            him[rows, :] = him[rows, :] + qi
            return ar * qr - ai * qi, ar * qi + ai * qr

        lax.fori_loop(0, steps, fix, (ar * c_re - ai * c_im, ar * c_im + ai * c_re))
        for g in range(per):
            states = slice(g * st1, (g + 1) * st1)
            y_ref[:, g * ch1:(g + 1) * ch1] = _dot(hre[:, states], cre_ref[g]) + _dot(him[:, states], cim_ref[g])

    rows_ch = pl.BlockSpec((seq, ch), lambda s, j: (s, j))
    rows_st = pl.BlockSpec((seq, st), lambda s, j: (s, j))
    vec = pl.BlockSpec((1, st), lambda s, j: (0, j))
    b_blk = pl.BlockSpec((per, ch1, st1), lambda s, j: (j, 0, 0))
    c_blk = pl.BlockSpec((per, st1, ch1), lambda s, j: (j, 0, 0))
    return pl.pallas_call(
        body, name=name,
        out_shape=(jax.ShapeDtypeStruct((t, n_sg * st1), F32), jax.ShapeDtypeStruct((t, n_sg * st1), F32),
                   jax.ShapeDtypeStruct((t, n_sg * ch1), F32)),
        grid=(n_seq, n_sg // per), in_specs=[rows_ch, b_blk, b_blk, vec, vec, c_blk, c_blk],
        out_specs=(rows_st, rows_st, rows_ch),
        compiler_params=_params(2))(za16, b_re, b_im, a_re, a_im, c_re, c_im)


def _s5_bwd(dy16, dza_skip, h_re, h_im, za16, b_re, b_im, a_re, a_im, c_re, c_im, n_seq, name):
    t = dy16.shape[0]
    n_sg, ch, st = b_re.shape
    seq = t // n_seq
    steps = seq // N_SEG

    def body(dy_ref, skip_ref, hre, him, za_ref, bre_ref, bim_ref, are, aim, cre_ref, cim_ref,
             dza_ref, dbre_ref, dbim_ref, dcre_ref, dcim_ref, dare, daim, lre, lim):
        dy = dy_ref[...]
        lre[...] = _dot(dy, cre_ref[0], tb=True)
        lim[...] = _dot(dy, cim_ref[0], tb=True)
        ar = jnp.broadcast_to(are[...], (N_SEG, st))
        ai = -jnp.broadcast_to(aim[...], (N_SEG, st))
        zero = jnp.zeros((N_SEG, st), F32)
        p_re, p_im = _complex_power(ar, ai, steps)

        def local(i, carry):
            lr, li = carry
            k = steps - 1 - i
            rows = pl.ds(pl.multiple_of(k * N_SEG, N_SEG), N_SEG)
            nr = ar * lr - ai * li + lre[rows, :]
            ni = ar * li + ai * lr + lim[rows, :]
            lre[rows, :] = nr
            lim[rows, :] = ni
            return nr, ni

        f_re, f_im = lax.fori_loop(0, steps, local, (zero, zero))
        c_re, c_im = _seg_carries(f_re, f_im, p_re, p_im, reverse=True)

        def accumulate(lam_r, lam_i, hp_r, hp_i, acc_r, acc_i):
            return acc_r + (lam_r * hp_r + lam_i * hp_i), acc_i + (lam_i * hp_r - lam_r * hp_i)

        def fix(i, carry):
            qr, qi, acc_r, acc_i = carry
            k = steps - 1 - i
            rows = pl.ds(pl.multiple_of(k * N_SEG, N_SEG), N_SEG)
            prev = pl.ds(pl.multiple_of((k - 1) * N_SEG, N_SEG), N_SEG)
            lam_r = lre[rows, :] + qr
            lam_i = lim[rows, :] + qi
            lre[rows, :] = lam_r
            lim[rows, :] = lam_i
            acc_r, acc_i = accumulate(lam_r, lam_i, hre[prev, :], him[prev, :], acc_r, acc_i)
            return ar * qr - ai * qi, ar * qi + ai * qr, acc_r, acc_i

        qr, qi, acc_r, acc_i = lax.fori_loop(
            0, steps - 1, fix, (ar * c_re - ai * c_im, ar * c_im + ai * c_re, zero, zero))
        first = pl.ds(0, N_SEG)
        last = pl.ds((steps - 1) * N_SEG, N_SEG)
        lam_r = lre[first, :] + qr
        lam_i = lim[first, :] + qi
        lre[first, :] = lam_r
        lim[first, :] = lam_i
        row_id = lax.broadcasted_iota(jnp.int32, (N_SEG, st), 0)
        hp_r = jnp.where(row_id == 0, 0.0, pltpu.roll(hre[last, :], 1, 0))
        hp_i = jnp.where(row_id == 0, 0.0, pltpu.roll(him[last, :], 1, 0))
        acc_r, acc_i = accumulate(lam_r, lam_i, hp_r, hp_i, acc_r, acc_i)

        lam_re16 = lre[...].astype(BF16)
        lam_im16 = lim[...].astype(BF16)
        dza_ref[...] = (_dot(lam_re16, bre_ref[0], tb=True) + _dot(lam_im16, bim_ref[0], tb=True)
                        + skip_ref[...]).astype(BF16)

        @pl.when(pl.program_id(1) == 0)
        def _():
            for ref in (dbre_ref, dbim_ref, dcre_ref, dcim_ref, dare, daim):
                ref[...] = jnp.zeros_like(ref)

        za = za_ref[...]
        dbre_ref[0] += _dot(za, lam_re16, ta=True)
        dbim_ref[0] += _dot(za, lam_im16, ta=True)
        dcre_ref[0] += _dot(hre[...], dy, ta=True)
        dcim_ref[0] += _dot(him[...], dy, ta=True)
        dare[...] += acc_r
        daim[...] += acc_i

    rows_ch = pl.BlockSpec((seq, ch), lambda j, s: (s, j))
    rows_st = pl.BlockSpec((seq, st), lambda j, s: (s, j))
    vec = pl.BlockSpec((1, st), lambda j, s: (0, j))
    b_blk = pl.BlockSpec((1, ch, st), lambda j, s: (j, 0, 0))
    c_blk = pl.BlockSpec((1, st, ch), lambda j, s: (j, 0, 0))
    acc = pl.BlockSpec((N_SEG, st), lambda j, s: (0, j))
    return pl.pallas_call(
        body, name=name,
        out_shape=(jax.ShapeDtypeStruct((t, n_sg * ch), BF16),
                   jax.ShapeDtypeStruct((n_sg, ch, st), F32), jax.ShapeDtypeStruct((n_sg, ch, st), F32),
                   jax.ShapeDtypeStruct((n_sg, st, ch), F32), jax.ShapeDtypeStruct((n_sg, st, ch), F32),
                   jax.ShapeDtypeStruct((N_SEG, n_sg * st), F32), jax.ShapeDtypeStruct((N_SEG, n_sg * st), F32)),
        grid=(n_sg, n_seq),
        in_specs=[rows_ch, rows_ch, rows_st, rows_st, rows_ch, b_blk, b_blk, vec, vec, c_blk, c_blk],
        out_specs=(rows_ch, b_blk, b_blk, c_blk, c_blk, acc, acc),
        scratch_shapes=[pltpu.VMEM((seq, st), F32), pltpu.VMEM((seq, st), F32)],
        compiler_params=_params(2))(dy16, dza_skip, h_re, h_im, za16, b_re, b_im, a_re, a_im, c_re, c_im)


def _s5_post_fwd(yssm, u, d_skip, glu_w, glu_b, name):
    t, w = yssm.shape
    tr = _pick(t, 512, 8)

    def body(y_ref, u_ref, d_ref, w_ref, b_ref, o_ref):
        yg = _gelu(y_ref[...] + d_ref[...] * u_ref[...])
        pre = _dot(yg, w_ref[...]) + b_ref[...]
        o_ref[...] = yg * _sigmoid(pre)

    return pl.pallas_call(
        body, name=name, out_shape=jax.ShapeDtypeStruct((t, w), F32), grid=(t // tr,),
        in_specs=[_rows(tr, w), _rows(tr, w), _whole((1, w)), _whole((w, w)), _whole((1, w))],
        out_specs=_rows(tr, w), compiler_params=_params())(yssm, u, d_skip, glu_w, glu_b)


def _s5_post_bwd(yssm, u, dout, d_skip, glu_w, glu_b, name):
    t, w = yssm.shape
    tr = _pick(t, 512, 8)

    def body(y_ref, u_ref, do_ref, d_ref, w_ref, b_ref, dy_ref, du_ref, dw_ref, dd_ref, db_ref):
        uu = u_ref[...]
        y = y_ref[...] + d_ref[...] * uu
        yg = _gelu(y)
        sg = _sigmoid(_dot(yg, w_ref[...]) + b_ref[...])
        do = do_ref[...]
        dpre = do * yg * sg * (1.0 - sg)
        dyg = do * sg + _dot(dpre, w_ref[...], tb=True)
        dy = dyg * _gelu_grad(y)
        dy_ref[...] = dy.astype(BF16)
        du_ref[...] = dy * d_ref[...]

        @pl.when(pl.program_id(0) == 0)
        def _():
            dw_ref[...] = jnp.zeros_like(dw_ref)
            dd_ref[...] = jnp.zeros_like(dd_ref)
            db_ref[...] = jnp.zeros_like(db_ref)

        dw_ref[...] += _dot(yg, dpre, ta=True)
        dd_ref[...] += _fold8(dy * uu)
        db_ref[...] += _fold8(dpre)

    return pl.pallas_call(
        body, name=name,
        out_shape=(jax.ShapeDtypeStruct((t, w), BF16), jax.ShapeDtypeStruct((t, w), F32),
                   jax.ShapeDtypeStruct((w, w), F32), jax.ShapeDtypeStruct((8, w), F32),
                   jax.ShapeDtypeStruct((8, w), F32)),
        grid=(t // tr,),
        in_specs=[_rows(tr, w), _rows(tr, w), _rows(tr, w), _whole((1, w)), _whole((w, w)),
                  _whole((1, w))],
        out_specs=(_rows(tr, w), _rows(tr, w), _whole((w, w)), _whole((8, w)), _whole((8, w))),
        compiler_params=_params())(yssm, u, dout, d_skip, glu_w, glu_b)


def _head_select(parts, head_dim):
    col_head = lax.broadcasted_iota(jnp.int32, parts[0].shape, 1) // head_dim
    out = jnp.zeros_like(parts[0])
    for h, part in enumerate(parts):
        out = jnp.where(col_head == h, part, out)
    return out


def _gmlp_fwd(proj, ln_g, ln_b, ws, bs_cols, name):
    t = proj.shape[0]
    n_heads = ws.shape[0]
    w = bs_cols.shape[1]
    head_dim = w // n_heads
    cpb = _pick(t // CHUNK, 4, 1)
    tr = cpb * CHUNK

    def body(zu_ref, zv_ref, g_ref, b_ref, ws_ref, bs_ref, o_ref):
        for c in range(cpb):
            rows = pl.ds(c * CHUNK, CHUNK)
            xhat, _ = _ln_stats(_gelu(zv_ref[rows, :]))
            vn = (xhat * g_ref[...] + b_ref[...]).astype(BF16)
            s = _head_select([_dot(ws_ref[h], vn) for h in range(n_heads)], head_dim) + bs_ref[...]
            o_ref[rows, :] = _gelu(zu_ref[rows, :]) * s

    return pl.pallas_call(
        body, name=name, out_shape=jax.ShapeDtypeStruct((t, w), F32), grid=(t // tr,),
        in_specs=[_rows(tr, w, 1), _rows(tr, w, 2), _whole((1, w)), _whole((1, w)),
                  _whole(ws.shape), _whole(bs_cols.shape)],
        out_specs=_rows(tr, w), compiler_params=_params())(proj, proj, ln_g, ln_b, ws, bs_cols)


def _gmlp_bwd(proj, dout, ln_g, ln_b, ws, ws_t, bs_cols, name):
    t = proj.shape[0]
    n_heads = ws.shape[0]
    w = bs_cols.shape[1]
    head_dim = w // n_heads
    cpb = _pick(t // CHUNK, 4, 1)
    tr = cpb * CHUNK

    def body(zu_ref, zv_ref, do_ref, g_ref, b_ref, ws_ref, wst_ref, bs_ref,
             dzu_ref, dzv_ref, dws_ref, dbs_ref, dg_ref, dbeta_ref):
        @pl.when(pl.program_id(0) == 0)
        def _():
            dws_ref[...] = jnp.zeros_like(dws_ref)
            dbs_ref[...] = jnp.zeros_like(dbs_ref)
            dg_ref[...] = jnp.zeros_like(dg_ref)
            dbeta_ref[...] = jnp.zeros_like(dbeta_ref)

        col_head = lax.broadcasted_iota(jnp.int32, (CHUNK, w), 1) // head_dim
        for c in range(cpb):
            rows = pl.ds(c * CHUNK, CHUNK)
            zu, zv, do = zu_ref[rows, :], zv_ref[rows, :], do_ref[rows, :]
            xhat, rstd = _ln_stats(_gelu(zv))
            vn = (xhat * g_ref[...] + b_ref[...]).astype(BF16)
            s = _head_select([_dot(ws_ref[h], vn) for h in range(n_heads)], head_dim) + bs_ref[...]
            dzu_ref[rows, :] = (do * s * _gelu_grad(zu)).astype(BF16)
            ds = do * _gelu(zu)
            ds16 = ds.astype(BF16)
            dbs_ref[...] += ds
            for h in range(n_heads):
                dws_ref[h] += _dot(jnp.where(col_head == h, ds16, jnp.zeros_like(ds16)), vn, tb=True)
            dvn = _head_select([_dot(wst_ref[h], ds16) for h in range(n_heads)], head_dim)
            dg_ref[...] += _fold8(dvn * xhat)
            dbeta_ref[...] += _fold8(dvn)
            dgv = _ln_bwd(dvn, xhat, rstd, g_ref[...])
            dzv_ref[rows, :] = (dgv * _gelu_grad(zv)).astype(BF16)

    return pl.pallas_call(
        body, name=name,
        out_shape=(jax.ShapeDtypeStruct((t, w), BF16), jax.ShapeDtypeStruct((t, w), BF16),
                   jax.ShapeDtypeStruct(ws.shape, F32), jax.ShapeDtypeStruct((CHUNK, w), F32),
                   jax.ShapeDtypeStruct((8, w), F32), jax.ShapeDtypeStruct((8, w), F32)),
        grid=(t // tr,),
        in_specs=[_rows(tr, w, 1), _rows(tr, w, 2), _rows(tr, w), _whole((1, w)), _whole((1, w)),
                  _whole(ws.shape), _whole(ws.shape), _whole(bs_cols.shape)],
        out_specs=(_rows(tr, w), _rows(tr, w), _whole(ws.shape), _whole((CHUNK, w)),
                   _whole((8, w)), _whole((8, w))),
        compiler_params=_params())(proj, proj, dout, ln_g, ln_b, ws, ws_t, bs_cols)


def _merge_fwd(s5out, gm, proj, up_a, up_b, name):
    t, w = s5out.shape
    d = up_a.shape[1]
    assert d == 2 * w
    tr = _pick(t, 512, 8)

    def body(s_ref, gm_ref, ga0, ga1, gb0, gb1, ua_ref, ub_ref, m_ref, ya_ref, yb_ref):
        ya = _dot(s_ref[...], ua_ref[...])
        yb = _dot(gm_ref[...], ub_ref[...])
        ya_ref[...] = ya.astype(BF16)
        yb_ref[...] = yb.astype(BF16)
        for half, (ga, gb) in enumerate(((ga0, gb0), (ga1, gb1))):
            cols = slice(half * w, (half + 1) * w)
            m_ref[:, cols] = (_sigmoid(ga[...]) * ya[:, cols] + _sigmoid(gb[...]) * yb[:, cols]).astype(BF16)

    return pl.pallas_call(
        body, name=name,
        out_shape=(jax.ShapeDtypeStruct((t, d), BF16), jax.ShapeDtypeStruct((t, d), BF16),
                   jax.ShapeDtypeStruct((t, d), BF16)),
        grid=(t // tr,),
        in_specs=[_rows(tr, w), _rows(tr, w), _rows(tr, w, 3), _rows(tr, w, 4), _rows(tr, w, 5),
                  _rows(tr, w, 6), _whole(up_a.shape), _whole(up_b.shape)],
        out_specs=(_rows(tr, d), _rows(tr, d), _rows(tr, d)),
        compiler_params=_params())(s5out, gm, proj, proj, proj, proj, up_a, up_b)


def _merge_bwd(dm, ya, yb, s5out, gm, proj, up_a, up_b, name):
    t, d = dm.shape
    w = d // 2
    tr = _pick(t, 512, 8)

    def body(dm_ref, ya_ref, yb_ref, s_ref, gm_ref, ga0, ga1, gb0, gb1, ua_ref, ub_ref,
             dga_ref, dgb_ref, ds_ref, dgm_ref, dua_ref, dub_ref):
        dm_v, ya, yb = dm_ref[...], ya_ref[...].astype(F32), yb_ref[...].astype(F32)
        dya, dyb = [], []
        for half, (ga, gb) in enumerate(((ga0, gb0), (ga1, gb1))):
            cols = slice(half * w, (half + 1) * w)
            sa, sb = _sigmoid(ga[...]), _sigmoid(gb[...])
            dmh = dm_v[:, cols]
            dga_ref[:, cols] = (dmh * ya[:, cols] * sa * (1.0 - sa)).astype(BF16)
            dgb_ref[:, cols] = (dmh * yb[:, cols] * sb * (1.0 - sb)).astype(BF16)
            dya.append((dmh * sa).astype(BF16))
            dyb.append((dmh * sb).astype(BF16))
        dya = jnp.concatenate(dya, axis=1)
        dyb = jnp.concatenate(dyb, axis=1)
        ds_ref[...] = _dot(dya, ua_ref[...], tb=True)
        dgm_ref[...] = _dot(dyb, ub_ref[...], tb=True)

        @pl.when(pl.program_id(0) == 0)
        def _():
            dua_ref[...] = jnp.zeros_like(dua_ref)
            dub_ref[...] = jnp.zeros_like(dub_ref)

        dua_ref[...] += _dot(s_ref[...], dya, ta=True)
        dub_ref[...] += _dot(gm_ref[...], dyb, ta=True)

    return pl.pallas_call(
        body, name=name,
        out_shape=(jax.ShapeDtypeStruct((t, d), BF16), jax.ShapeDtypeStruct((t, d), BF16),
                   jax.ShapeDtypeStruct((t, w), F32), jax.ShapeDtypeStruct((t, w), F32),
                   jax.ShapeDtypeStruct(up_a.shape, F32), jax.ShapeDtypeStruct(up_b.shape, F32)),
        grid=(t // tr,),
        in_specs=[_rows(tr, d), _rows(tr, d), _rows(tr, d), _rows(tr, w), _rows(tr, w),
                  _rows(tr, w, 3), _rows(tr, w, 4), _rows(tr, w, 5), _rows(tr, w, 6),
                  _whole(up_a.shape), _whole(up_b.shape)],
        out_specs=(_rows(tr, d), _rows(tr, d), _rows(tr, w), _rows(tr, w), _whole(up_a.shape),
                   _whole(up_b.shape)),
        compiler_params=_params())(dm, ya, yb, s5out, gm, proj, proj, proj, proj, up_a, up_b)


def _head(x3, r3, ln_g, p, target, w_gate, w_proj, out_scale, name):
    t, d = x3.shape
    pd = p.shape[1]
    tr = _pick(t, 512, 8)
    nb = t // tr

    def body(x_ref, r_ref, g_ref, p_ref, t_ref, wg_ref, wp_ref,
             loss_ref, dr_ref, drs_ref, dwg_ref, dwp_ref, dg_ref, dbeta_ref):
        xv = x_ref[...]
        sg = _sigmoid(_dot(xv, wg_ref[...]))
        pp = _dot(p_ref[...], wp_ref[...])
        err = xv + sg * pp - t_ref[...]
        loss_ref[...] = jnp.full((8, 128), 0.5 * jnp.sum(jnp.mean(err * err, axis=-1)), F32)
        dout = err * (1.0 / d)
        dgpre = dout * pp * sg * (1.0 - sg)
        dpp = dout * sg
        dx = dout + _dot(dgpre, wg_ref[...], tb=True)
        xhat, rstd = _ln_stats(r_ref[...])
        dr = _ln_bwd(dx, xhat, rstd, g_ref[...])
        dr_ref[...] = dr
        drs_ref[...] = (out_scale * dr).astype(BF16)

        @pl.when(pl.program_id(0) == 0)
        def _():
            for ref in (dwg_ref, dwp_ref, dg_ref, dbeta_ref):
                ref[...] = jnp.zeros_like(ref)

        dwg_ref[...] += _dot(xv, dgpre, ta=True)
        dwp_ref[...] += _dot(p_ref[...], dpp, ta=True)
        dg_ref[...] += _fold8(dx * xhat)
        dbeta_ref[...] += _fold8(dx)

    return pl.pallas_call(
        body, name=name,
        out_shape=(jax.ShapeDtypeStruct((nb * 8, 128), F32), jax.ShapeDtypeStruct((t, d), F32),
                   jax.ShapeDtypeStruct((t, d), BF16), jax.ShapeDtypeStruct((d, d), F32),
                   jax.ShapeDtypeStruct((pd, d), F32), jax.ShapeDtypeStruct((8, d), F32),
                   jax.ShapeDtypeStruct((8, d), F32)),
        grid=(nb,),
        in_specs=[_rows(tr, d), _rows(tr, d), _whole((1, d)), _rows(tr, pd), _rows(tr, d), _whole((d, d)),
                  _whole((pd, d))],
        out_specs=(pl.BlockSpec((8, 128), lambda i: (i, 0)), _rows(tr, d), _rows(tr, d), _whole((d, d)),
                   _whole((pd, d)), _whole((8, d)), _whole((8, d))),
        compiler_params=_params())(x3, r3, ln_g, p, target, w_gate, w_proj)


_HBM = pl.BlockSpec(memory_space=pltpu.HBM)


_SEM = pl.BlockSpec(memory_space=pltpu.SEMAPHORE)
_ANY = pl.BlockSpec(memory_space=pl.ANY)
_DATAFLOW = pltpu.SideEffectType.DATAFLOW_SIDE_EFFECTING


def _peer(k, x, y, c):
    return (1 - x if k & 4 else x, 1 - y if k & 2 else y, 1 - c if k & 1 else c)


def _exchange_start(sends, after, name):
    n = len(sends)
    lands = [lax.empty((N_DEV,) + s.shape[1:], s.dtype) for s in sends]

    def body(*refs):
        s_refs, l_refs = refs[:n], refs[n:2 * n]
        send_sems, recv_sems, local_sems, token = refs[2 * n + 1], refs[2 * n + 2], refs[2 * n + 3], refs[-1]
        x, y, c = lax.axis_index("x"), lax.axis_index("y"), lax.axis_index("c")
        me = 4 * x + 2 * y + c
        for a in range(n):
            same = sends[a].shape[0] == 1
            pltpu.make_async_copy(s_refs[a].at[0 if same else me], l_refs[a].at[me], local_sems.at[a]).start()
            for k in range(1, N_DEV):
                px, py, pc = _peer(k, x, y, c)
                pltpu.make_async_remote_copy(
                    src_ref=s_refs[a].at[0 if same else 4 * px + 2 * py + pc], dst_ref=l_refs[a].at[me],
                    send_sem=send_sems.at[7 * a + k - 1], recv_sem=recv_sems.at[7 * a + k - 1],
                    device_id=(px, py, pc), device_id_type=pl.DeviceIdType.MESH).start()
        token[...] = jnp.zeros_like(token)

    outs = pl.pallas_call(
        body, name=name,
        out_shape=(pltpu.SemaphoreType.DMA((7 * n,)), pltpu.SemaphoreType.DMA((7 * n,)),
                   pltpu.SemaphoreType.DMA((n,)),
                   *[pltpu.HBM(s.shape, s.dtype) for s in sends],
                   *[pltpu.HBM(l.shape, l.dtype) for l in lands],
                   jax.ShapeDtypeStruct((8, 128), F32)),
        in_specs=[_HBM] * (2 * n) + [_ANY],
        out_specs=(_SEM, _SEM, _SEM, *([_HBM] * (2 * n)), pl.BlockSpec(memory_space=pltpu.VMEM)),
        input_output_aliases={i: 3 + i for i in range(2 * n)},
        compiler_params=pltpu.CompilerParams(has_side_effects=_DATAFLOW),
    )(*[pltpu.with_memory_space_constraint(v, pltpu.HBM) for v in list(sends) + lands], after)
    return (outs[0], outs[1], outs[2], list(outs[3:3 + n]), list(outs[3 + n:3 + 2 * n])), outs[-1]


def _exchange_wait(handle, after, name):
    send_sems, recv_sems, local_sems, sends, lands = handle
    n = len(sends)

    def body(*refs):
        s_refs, l_refs = refs[:n], refs[n:2 * n]
        send_sems, recv_sems, local_sems = refs[2 * n], refs[2 * n + 1], refs[2 * n + 2]
        x, y, c = lax.axis_index("x"), lax.axis_index("y"), lax.axis_index("c")
        for a in range(n):
            pltpu.make_async_copy(s_refs[a].at[0], l_refs[a].at[0], local_sems.at[a]).wait()
            for k in range(1, N_DEV):
                copy = pltpu.make_async_remote_copy(
                    src_ref=s_refs[a].at[0], dst_ref=l_refs[a].at[0],
                    send_sem=send_sems.at[7 * a + k - 1], recv_sem=recv_sems.at[7 * a + k - 1],
                    device_id=_peer(k, x, y, c), device_id_type=pl.DeviceIdType.MESH)
                copy.wait_send()
                copy.wait_recv()

    outs = pl.pallas_call(
        body, name=name,
        out_shape=(*[pltpu.HBM(s.shape, s.dtype) for s in sends], *[pltpu.HBM(l.shape, l.dtype) for l in lands]),
        in_specs=[_HBM] * (2 * n) + [_SEM, _SEM, _SEM] + [_ANY] * len(after),
        out_specs=tuple([_HBM] * (2 * n)),
        input_output_aliases={i: i for i in range(2 * n)},
        compiler_params=pltpu.CompilerParams(has_side_effects=_DATAFLOW),
    )(*sends, *lands, send_sems, recv_sems, local_sems, *after)
    return list(outs[n:])


def _chips_of(x, y):
    return [(1 - x, y), (x, 1 - y), (1 - x, 1 - y)]


def _gather2_start(shards, after, name):
    n = len(shards)
    lands = [lax.empty((N_DEV,) + s.shape, s.dtype) for s in shards]

    def body(*refs):
        x_refs, l_refs = refs[:n], refs[n:2 * n]
        send_sems, recv_sems, local_sems, token = refs[2 * n + 1], refs[2 * n + 2], refs[2 * n + 3], refs[-1]
        x, y, c = lax.axis_index("x"), lax.axis_index("y"), lax.axis_index("c")
        me = 4 * x + 2 * y + c
        peers = [(x, y, 1 - c)] + [(*chip, c) for chip in _chips_of(x, y)]
        for a in range(n):
            pltpu.make_async_copy(x_refs[a], l_refs[a].at[me], local_sems.at[a]).start()
            for k, peer in enumerate(peers):
                pltpu.make_async_remote_copy(
                    src_ref=x_refs[a], dst_ref=l_refs[a].at[me],
                    send_sem=send_sems.at[4 * a + k], recv_sem=recv_sems.at[4 * a + k],
                    device_id=peer, device_id_type=pl.DeviceIdType.MESH).start()
        token[...] = jnp.zeros_like(token)

    outs = pl.pallas_call(
        body, name=name,
        out_shape=(pltpu.SemaphoreType.DMA((4 * n,)), pltpu.SemaphoreType.DMA((4 * n,)),
                   pltpu.SemaphoreType.DMA((n,)),
                   *[pltpu.HBM(s.shape, s.dtype) for s in shards],
                   *[pltpu.HBM(l.shape, l.dtype) for l in lands],
                   jax.ShapeDtypeStruct((8, 128), F32)),
        in_specs=[_HBM] * (2 * n) + [_ANY],
        out_specs=(_SEM, _SEM, _SEM, *([_HBM] * (2 * n)), pl.BlockSpec(memory_space=pltpu.VMEM)),
        input_output_aliases={i: 3 + i for i in range(2 * n)},
        compiler_params=pltpu.CompilerParams(has_side_effects=_DATAFLOW),
    )(*[pltpu.with_memory_space_constraint(v, pltpu.HBM) for v in list(shards) + lands], after)
    return (outs[0], outs[1], outs[2], list(outs[3:3 + n]), list(outs[3 + n:3 + 2 * n])), outs[-1]


def _gather2_forward(handle, after, name):
    send_sems, recv_sems, local_sems, shards, lands = handle
    n = len(shards)

    def body(*refs):
        x_refs, l_refs = refs[:n], refs[n:2 * n]
        send_sems, recv_sems = refs[2 * n], refs[2 * n + 1]
        out0 = 2 * n + 2 + len(after)
        fwd_send, fwd_recv, token = refs[out0], refs[out0 + 1], refs[-1]
        x, y, c = lax.axis_index("x"), lax.axis_index("y"), lax.axis_index("c")
        for a in range(n):
            for j, chip in enumerate(_chips_of(x, y)):
                block = l_refs[a].at[4 * chip[0] + 2 * chip[1] + c]
                pltpu.make_async_remote_copy(
                    src_ref=x_refs[a], dst_ref=block, send_sem=send_sems.at[4 * a + 1 + j],
                    recv_sem=recv_sems.at[4 * a + 1 + j], device_id=(*chip, c),
                    device_id_type=pl.DeviceIdType.MESH).wait_recv()
                pltpu.make_async_remote_copy(
                    src_ref=block, dst_ref=block, send_sem=fwd_send.at[3 * a + j], recv_sem=fwd_recv.at[3 * a + j],
                    device_id=(x, y, 1 - c), device_id_type=pl.DeviceIdType.MESH).start()
        token[...] = jnp.zeros_like(token)

    outs = pl.pallas_call(
        body, name=name,
        out_shape=(pltpu.SemaphoreType.DMA((3 * n,)), pltpu.SemaphoreType.DMA((3 * n,)),
                   *[pltpu.HBM(s.shape, s.dtype) for s in shards], *[pltpu.HBM(l.shape, l.dtype) for l in lands],
                   jax.ShapeDtypeStruct((8, 128), F32)),
        in_specs=[_HBM] * (2 * n) + [_SEM, _SEM] + [_ANY] * len(after),
        out_specs=(_SEM, _SEM, *([_HBM] * (2 * n)), pl.BlockSpec(memory_space=pltpu.VMEM)),
        input_output_aliases={i: 2 + i for i in range(2 * n)},
        compiler_params=pltpu.CompilerParams(has_side_effects=_DATAFLOW),
    )(*shards, *lands, send_sems, recv_sems, *after)
    handle = (send_sems, recv_sems, local_sems, outs[0], outs[1], list(outs[2:2 + n]), list(outs[2 + n:2 + 2 * n]))
    return handle, outs[-1]


def _gather2_wait(handle, after, name):
    send_sems, recv_sems, local_sems, fwd_send, fwd_recv, shards, lands = handle
    n = len(shards)

    def body(*refs):
        x_refs, l_refs = refs[:n], refs[n:2 * n]
        send_sems, recv_sems, local_sems, fwd_send, fwd_recv = refs[2 * n:2 * n + 5]
        x, y, c = lax.axis_index("x"), lax.axis_index("y"), lax.axis_index("c")
        sibling = (x, y, 1 - c)

        def copy(a, send_sem, recv_sem):
            return pltpu.make_async_remote_copy(
                src_ref=x_refs[a], dst_ref=l_refs[a].at[0], send_sem=send_sem, recv_sem=recv_sem,
                device_id=sibling, device_id_type=pl.DeviceIdType.MESH)

        for a in range(n):
            pltpu.make_async_copy(x_refs[a], l_refs[a].at[0], local_sems.at[a]).wait()
            for k in range(4):
                copy(a, send_sems.at[4 * a + k], recv_sems.at[4 * a + k]).wait_send()
            copy(a, send_sems.at[4 * a], recv_sems.at[4 * a]).wait_recv()
            for j in range(3):
                passed = copy(a, fwd_send.at[3 * a + j], fwd_recv.at[3 * a + j])
                passed.wait_send()
                passed.wait_recv()

    outs = pl.pallas_call(
        body, name=name,
        out_shape=(*[pltpu.HBM(s.shape, s.dtype) for s in shards], *[pltpu.HBM(l.shape, l.dtype) for l in lands]),
        in_specs=[_HBM] * (2 * n) + [_SEM] * 5 + [_ANY] * len(after),
        out_specs=tuple([_HBM] * (2 * n)),
        input_output_aliases={i: i for i in range(2 * n)},
        compiler_params=pltpu.CompilerParams(has_side_effects=_DATAFLOW),
    )(*shards, *lands, send_sems, recv_sems, local_sems, fwd_send, fwd_recv, *after)
    return list(outs[n:])


def _adamw(recv, w, m, v, name):
    n, rows, width = recv.shape
    tr = _pick(rows, max(8, ADAM_BLOCK_BYTES // (n * width * recv.dtype.itemsize)), 8)
    c_m = 1.0 - ADAM_B1 ** ADAM_STEP
    c_v = 1.0 - ADAM_B2 ** ADAM_STEP

    def body(r_ref, w_ref, m_ref, v_ref, g_ref, d_ref, nm_ref, nv_ref):
        g = r_ref[0].astype(F32)
        for i in range(1, n):
            g = g + r_ref[i].astype(F32)
        m2 = ADAM_B1 * m_ref[...] + (1.0 - ADAM_B1) * g
        v2 = ADAM_B2 * v_ref[...] + (1.0 - ADAM_B2) * (g * g)
        m_hat = m2 / c_m
        v_hat = v2 / c_v
        g_ref[...] = g
        d_ref[...] = -ADAM_LR * (m_hat / (jnp.sqrt(v_hat) + ADAM_EPS) + ADAM_WD * w_ref[...])
        nm_ref[...] = m2
        nv_ref[...] = v2

    blk = _rows(tr, width)
    shp = jax.ShapeDtypeStruct((rows, width), F32)
    return pl.pallas_call(
        body, name=name, out_shape=(shp, shp, shp, shp), grid=(rows // tr,),
        in_specs=[pl.BlockSpec((n, tr, width), lambda i: (0, i, 0)), blk, blk, blk],
        out_specs=(blk, blk, blk, blk), compiler_params=_params())(recv, w, m, v)


def _join_cols(gathered):
    n, r, c = gathered.shape
    return gathered.transpose(1, 0, 2).reshape(r, n * c)


def _split_cols(full):
    r, c = full.shape
    return full.reshape(r, N_DEV, c // N_DEV).transpose(1, 0, 2)


def _adamw_small(items, name):
    n = len(items)
    c_m = 1.0 - ADAM_B1 ** ADAM_STEP
    c_v = 1.0 - ADAM_B2 ** ADAM_STEP

    def body(*refs):
        ins, outs = refs[:4 * n], refs[4 * n:]
        for k in range(n):
            r_ref, w_ref, m_ref, v_ref = ins[4 * k:4 * k + 4]
            g = r_ref[0].astype(F32)
            for i in range(1, N_DEV):
                g = g + r_ref[i].astype(F32)
            m2 = ADAM_B1 * m_ref[...] + (1.0 - ADAM_B1) * g
            v2 = ADAM_B2 * v_ref[...] + (1.0 - ADAM_B2) * (g * g)
            outs[4 * k][...] = g
            outs[4 * k + 1][...] = -ADAM_LR * ((m2 / c_m) / (jnp.sqrt(v2 / c_v) + ADAM_EPS) + ADAM_WD * w_ref[...])
            outs[4 * k + 2][...] = m2
            outs[4 * k + 3][...] = v2

    vmem = pl.BlockSpec(memory_space=pltpu.VMEM)
    flat = pl.pallas_call(
        body, name=name,
        out_shape=tuple(jax.ShapeDtypeStruct(w.shape, F32) for _, w, _, _ in items for _ in range(4)),
        in_specs=[vmem] * (4 * n), out_specs=tuple([vmem] * (4 * n)),
        compiler_params=pltpu.CompilerParams(vmem_limit_bytes=VMEM_LIMIT_BYTES),
    )(*[a for item in items for a in item])
    return [tuple(flat[4 * k:4 * k + 4]) for k in range(n)]


def _discretise(lam_re, lam_im, log_dt, b_re, b_im):
    dt = jnp.exp(log_dt)
    mag = jnp.exp(lam_re * dt)
    ab_re = mag * jnp.cos(lam_im * dt)
    ab_im = mag * jnp.sin(lam_im * dt)
    nr = ab_re - 1.0
    ni = ab_im
    den = lam_re * lam_re + lam_im * lam_im
    coef_re = (nr * lam_re + ni * lam_im) / den
    coef_im = (ni * lam_re - nr * lam_im) / den
    return ab_re, ab_im, coef_re * b_re - coef_im * b_im, coef_re * b_im + coef_im * b_re


def _s5_discretise(operands, name, token):
    def body(*refs):
        for out_ref, v in zip(refs[6:], _discretise(*[r[...] for r in refs[:5]])):
            out_ref[...] = v

    vmem = pl.BlockSpec(memory_space=pltpu.VMEM)
    shape = jax.ShapeDtypeStruct(operands[0].shape, F32)
    return pl.pallas_call(body, name=name, out_shape=(shape,) * 4, in_specs=[vmem] * 6,
                          out_specs=(vmem,) * 4)(*operands, token)


def _s5_discretise_bwd(operands, cotangents, name):
    def body(*refs):
        _, vjp = jax.vjp(_discretise, *[r[...] for r in refs[:5]])
        for out_ref, v in zip(refs[9:], vjp(tuple(r[...] for r in refs[5:9]))):
            out_ref[...] = v

    vmem = pl.BlockSpec(memory_space=pltpu.VMEM)
    shape = jax.ShapeDtypeStruct(operands[0].shape, F32)
    return pl.pallas_call(body, name=name, out_shape=(shape,) * 5, in_specs=[vmem] * 9,
                          out_specs=(vmem,) * 5)(*operands, *cotangents)


def _same_group(gl):
    return jnp.eye(gl, dtype=bool)[None, :, None, :, None]


def _super_groups_in(bb, gl):
    g, p, i = bb.shape
    blocks = bb.reshape(g // gl, gl, p, i).transpose(0, 1, 3, 2)[:, :, :, None, :]
    return jnp.where(_same_group(gl), blocks, 0.0).reshape(g // gl, gl * i, gl * p)


def _super_groups_in_take(dense, gl, p, i):
    n_sg = dense.shape[0]
    blocks = jnp.where(_same_group(gl), dense.reshape(n_sg, gl, i, gl, p), 0.0).sum(axis=3)
    return blocks.transpose(0, 1, 3, 2).reshape(n_sg * gl, p, i)


def _super_groups_out(cc, gl):
    g, i, p = cc.shape
    blocks = cc.reshape(g // gl, gl, i, p).transpose(0, 1, 3, 2)[:, :, :, None, :]
    return jnp.where(_same_group(gl), blocks, 0.0).reshape(g // gl, gl * p, gl * i)


def _super_groups_out_take(dense, gl, i, p):
    n_sg = dense.shape[0]
    blocks = jnp.where(_same_group(gl), dense.reshape(n_sg, gl, p, gl, i), 0.0).sum(axis=3)
    return blocks.transpose(0, 1, 3, 2).reshape(n_sg * gl, i, p)


def _perm_rows(z, n_seq):
    t, w = z.shape
    steps = t // n_seq // N_SEG
    return z.reshape(n_seq, N_SEG, steps, w).transpose(0, 2, 1, 3).reshape(t, w)


def _unperm_rows(z, n_seq):
    t, w = z.shape
    steps = t // n_seq // N_SEG
    return z.reshape(n_seq, steps, N_SEG, w).transpose(0, 2, 1, 3).reshape(t, w)


def kernel(*args):
    assert len(args) == len(INPUT_NAMES)
    inp = dict(zip(INPUT_NAMES, args))
    depth = inp["ffn1_w_in"].shape[0]
    assert depth == 1
    alpha = (2.0 * depth) ** 0.25

    n_seq, seq, d_model = inp["x"].shape
    t = n_seq * seq
    x = inp["x"].reshape(t, d_model)
    target = inp["loss_target"].reshape(t, d_model)
    wts = {n: inp[n][0] if n in SHARDED else inp[n] for n in WEIGHTS}
    mom = {n: inp["m_" + n][0] if n in SHARDED else inp["m_" + n] for n in WEIGHTS}
    var = {n: inp["v_" + n][0] if n in SHARDED else inp["v_" + n] for n in WEIGHTS}

    full = {}

    def place(n, g):
        if n in ("ffn1_w_in", "ffn2_w_in"):
            full[n] = g.reshape((2, N_DEV // 2) + g.shape[1:])
        elif n in ("ffn1_w_out", "ffn2_w_out"):
            full[n] = g.reshape(N_DEV // 2, 2 * g.shape[1], g.shape[2])
        elif n in COL_SHARDED:
            full[n] = _join_cols(g)
        else:
            full[n] = g.reshape(N_DEV * g.shape[1], g.shape[2])

    w16 = dict(zip(GATHER_FIRST, _to_bf16([wts[n] for n in GATHER_FIRST], "cast_ffn1")))
    gather_first, gather_first_token = _gather2_start([w16[n] for n in GATHER_FIRST], w16[GATHER_FIRST[0]],
                                                      "gather_first_start")
    later = tuple(n for n in SHARDED if n not in GATHER_FIRST)
    casts = _to_bf16([wts[n] for n in later] + [x, inp["p"][0].reshape(t, -1)], "cast_rest",
                     token=gather_first_token)
    w16.update(zip(later, casts))
    x16, p16 = casts[-2:]

    def gather_start(names, after, name):
        return _exchange_start([w16[n][None] for n in names], after, name)

    def gather_wait(names, handle, after, name):
        for n, g in zip(names, _exchange_wait(handle, after, name)):
            place(n, g)

    def gather2_wait(names, handle, after, name):
        for n, g in zip(names, _gather2_wait(handle, after, name)):
            place(n, g)


    def row(v):
        return v.reshape(1, -1)

    _, n_groups, n_state = wts["ssm_lambda_re"].shape
    n_ch = wts["ssm_b_re"].shape[3]
    disc_in = (jnp.repeat(wts["ssm_lambda_re"][0], n_ch, axis=1), jnp.repeat(wts["ssm_lambda_im"][0], n_ch, axis=1),
               jnp.broadcast_to(wts["ssm_log_dt"][0][:, None], (n_groups, n_state * n_ch)),
               wts["ssm_b_re"][0].reshape(n_groups, -1), wts["ssm_b_im"][0].reshape(n_groups, -1))
    ab_re, ab_im, bb_re, bb_im = _s5_discretise(disc_in, "s5_discretise", gather_first_token)
    a_re, a_im = row(ab_re[:, ::n_ch]), row(ab_im[:, ::n_ch])
    bb_re, bb_im = bb_re.reshape(n_groups, n_state, n_ch), bb_im.reshape(n_groups, n_state, n_ch)
    gl = S5_CHANNELS_PER_STEP // n_ch
    b_sg_re = _super_groups_in(bb_re, gl).astype(BF16)
    b_sg_im = _super_groups_in(bb_im, gl).astype(BF16)
    c_sg_re = _super_groups_out(wts["ssm_c_re"][0], gl).astype(BF16)
    c_sg_im = _super_groups_out(-wts["ssm_c_im"][0], gl).astype(BF16)

    ws = wts["gmlp_w_s"][0]
    n_heads = ws.shape[0]
    d_gmlp = wts["gmlp_ln_g"].shape[1]
    causal = jnp.tril(jnp.ones((CHUNK, CHUNK), dtype=bool))
    ws_masked = jnp.where(causal[None], ws, 0.0).astype(BF16)
    ws_masked_t = ws_masked.transpose(0, 2, 1)
    bs_cols = jnp.repeat(wts["gmlp_b_s"][0].T, d_gmlp // n_heads, axis=1)
    d_ssm = n_groups * n_ch
    assert d_ssm == d_gmlp and d_model == 2 * d_ssm

    prepared = [x16, b_sg_re, b_sg_im, c_sg_re, c_sg_im, ws_masked_t, bs_cols]
    gather_first, _ = _gather2_forward(gather_first, prepared, "gather_first_forward")
    gather2_wait(GATHER_FIRST, gather_first, prepared, "gather_first_wait")
    gather_early, gather_early_token = _gather2_start([w16[n] for n in GATHER_EARLY], full["ffn1_w_in"],
                                                      "gather_early_start")
    h1, a1 = _ffn_in(x16, full["ffn1_w_in"], "ffn1_in", token=gather_early_token)
    gather_early, _ = _gather2_forward(gather_early, [a1], "gather_early_forward")
    gather2_wait(GATHER_EARLY, gather_early, [a1], "gather_early_wait")
    r1, x1, x1_16 = _ffn_out_ln(a1, full["ffn1_w_out"], x, row(wts["ln1_g"]), row(wts["ln1_b"]), alpha,
                                "ffn1_out_ln")

    gather_mid, token = gather_start(GATHER_MID, x1_16, "gather_mid_start")
    gather_last, token = _gather2_start([w16[n] for n in GATHER_LAST], token, "gather_last_start")
    proj = _mm(x1_16, full["mix_w_in"], name="mix_in", token=token)
    za_p = _perm_rows(proj[:, :d_ssm], n_seq)
    h_re, h_im, yssm = _s5_fwd(za_p, b_sg_re, b_sg_im, a_re, a_im, c_sg_re, c_sg_im, n_seq, "s5_fwd")
    gather_wait(GATHER_MID, gather_mid, [yssm], "gather_mid_wait")
    s5out_p = _s5_post_fwd(yssm, za_p, row(wts["ssm_d"]), full["ssm_glu_w"], row(wts["ssm_glu_b"]),
                           "s5_post")
    s5out = _unperm_rows(s5out_p, n_seq)
    gm = _gmlp_fwd(proj, row(wts["gmlp_ln_g"]), row(wts["gmlp_ln_b"]), ws_masked, bs_cols, "gmlp")
    m_mix, y_a, y_b = _merge_fwd(s5out, gm, proj, full["up_a"], full["up_b"], "merge")
    gather_last, token = _gather2_forward(gather_last, [m_mix], "gather_last_forward")
    r2, x2, x2_16 = _ffn_out_ln(m_mix[None], full["mix_w_out"][None], x1, row(wts["ln2_g"]), row(wts["ln2_b"]),
                                alpha, "mix_out_ln2", scale=1.0, token=token)

    gather2_wait(GATHER_LAST, gather_last, [x2_16], "gather_last_wait")
    h2, a2 = _ffn_in(x2_16, full["ffn2_w_in"], "ffn2_in")
    r3, x3, _ = _ffn_out_ln(a2, full["ffn2_w_out"], x2, row(wts["ln3_g"]), row(wts["ln3_b"]), alpha,
                            "ffn2_out_ln")

    small = {}
    send = {}

    def lane_dense(n, v):
        return v.reshape(v.shape[:2] + (-1,)) if n in ("ssm_b_re", "ssm_b_im") else v

    def on_wire(n, g):
        g = lane_dense(n, g)
        return (g.astype(BF16) if n in SMALL_BF16 else g)[None]
    loss_parts, dr3, dr3_h, dw_gate, dw_proj, dg, db = _head(
        x3, r3, row(wts["ln3_g"]), p16, target, full["ple_w_gate"], full["ple_w_proj"], 0.5, "head")
    loss_mine = jnp.full((1, 1, 8, 128), jnp.sum(loss_parts[::8, 0]), F32)
    send["ple_w_gate"] = dw_gate.astype(BF16).reshape(N_DEV, -1, d_model)
    send["ple_w_proj"] = _split_cols(dw_proj.astype(BF16))
    small["ln3_g"], small["ln3_b"] = dg.sum(0, keepdims=True), db.sum(0, keepdims=True)
    dh2 = _ffn_out_dx(dr3_h, full["ffn2_w_out"], h2, "ffn2_out_dx")
    dw = _ffn_out_dw(a2, dr3_h, "ffn2_out_dw")
    send["ffn2_w_out"] = dw.reshape(N_DEV, -1, d_model)
    dw = _ffn_in_dw(x2_16, dh2, "ffn2_in_dw")
    send["ffn2_w_in"] = dw.reshape((N_DEV,) + dw.shape[2:])
    group1 = ("ffn2_w_in", "ffn2_w_out", "ple_w_gate", "ple_w_proj")
    exchange1, token = _exchange_start(
        [send[n] for n in group1] + [on_wire(n, small[n]) for n in SMALL_HEAD] + [loss_mine], dh2,
        "grad_exchange1_start")
    dr2, dr2_h, dg, db = _ffn_in_dx_ln(dh2, full["ffn2_w_in"], r2, dr3, alpha, row(wts["ln2_g"]), 1.0,
                                       "ffn2_in_dx_ln2_bwd", token=token)
    small["ln2_g"], small["ln2_b"] = dg.sum(0, keepdims=True), db.sum(0, keepdims=True)
    dm = _mm(dr2_h, full["mix_w_out"], tb=True, name="mix_out_dx")
    send["mix_w_out"] = _mm(m_mix, dr2_h, ta=True, name="mix_out_dw", out_dtype=BF16).reshape(
        N_DEV, -1, d_model)
    dga, dgb, ds5out, dgm, dw_a, dw_b = _merge_bwd(
        dm, y_a, y_b, s5out, gm, proj, full["up_a"], full["up_b"], "merge_bwd")
    send["up_a"] = _split_cols(dw_a.astype(BF16))
    send["up_b"] = _split_cols(dw_b.astype(BF16))

    dzu, dzv, dws, dbs_cols, dg, db = _gmlp_bwd(
        proj, dgm, row(wts["gmlp_ln_g"]), row(wts["gmlp_ln_b"]), ws_masked, ws_masked_t, bs_cols,
        "gmlp_bwd")
    small["gmlp_ln_g"], small["gmlp_ln_b"] = dg.sum(0, keepdims=True), db.sum(0, keepdims=True)
    small["gmlp_w_s"] = jnp.where(causal[None], dws, 0.0)[None]
    small["gmlp_b_s"] = dbs_cols.reshape(CHUNK, n_heads, -1).sum(-1).T[None]

    dy_p, dza_skip, dw_glu, dd, dgb_glu = _s5_post_bwd(
        yssm, za_p, _perm_rows(ds5out, n_seq), row(wts["ssm_d"]), full["ssm_glu_w"], row(wts["ssm_glu_b"]),
        "s5_post_bwd")
    send["ssm_glu_w"] = dw_glu.astype(BF16).reshape(N_DEV, -1, d_ssm)
    small["ssm_d"], small["ssm_glu_b"] = dd.sum(0, keepdims=True), dgb_glu.sum(0, keepdims=True)
    dza_p, dbb_re, dbb_im, dc_re, dc_im, da_re, da_im = _s5_bwd(
        dy_p, dza_skip, h_re, h_im, za_p, b_sg_re, b_sg_im, a_re, a_im, c_sg_re, c_sg_im, n_seq, "s5_bwd")
    small["ssm_c_re"] = _super_groups_out_take(dc_re, gl, n_ch, n_state)[None]
    small["ssm_c_im"] = -_super_groups_out_take(dc_im, gl, n_ch, n_state)[None]
    dbb_re = _super_groups_in_take(dbb_re, gl, n_state, n_ch)
    dbb_im = _super_groups_in_take(dbb_im, gl, n_state, n_ch)
    def first_channel(v):
        placed = jnp.pad(v.sum(0).reshape(n_groups, n_state, 1), ((0, 0), (0, 0), (0, n_ch - 1)))
        return placed.reshape(n_groups, -1)

    cotangents = (first_channel(da_re), first_channel(da_im), dbb_re.reshape(n_groups, -1),
                  dbb_im.reshape(n_groups, -1))
    d_lre, d_lim, d_ldt, d_bre, d_bim = _s5_discretise_bwd(disc_in, cotangents, "s5_discretise_bwd")
    per_state = (n_groups, n_state, n_ch)
    small["ssm_lambda_re"] = d_lre.reshape(per_state).sum(-1)[None]
    small["ssm_lambda_im"] = d_lim.reshape(per_state).sum(-1)[None]
    small["ssm_log_dt"] = d_ldt.sum(-1)[None]
    small["ssm_b_re"] = d_bre.reshape((1,) + per_state)
    small["ssm_b_im"] = d_bim.reshape((1,) + per_state)

    dproj = jnp.concatenate([_unperm_rows(dza_p, n_seq), dzu, dzv, dga, dgb], axis=1)
    group2 = ("mix_w_out", "up_a", "up_b", "ssm_glu_w")
    exchange2, token = _exchange_start(
        [send[n] for n in group2] + [on_wire(n, small[n]) for n in SMALL_MID], dproj, "grad_exchange2_start")
    send["mix_w_in"] = _split_cols(_mm(x1_16, dproj, ta=True, name="mix_in_dw", out_dtype=BF16, token=token))
    exchange3, token = _exchange_start([send["mix_w_in"]], dproj, "grad_exchange3_start")
    dr1, dr1_h, dg, db = _mm_ln_bwd(dproj, full["mix_w_in"], r1, dr2, alpha, row(wts["ln1_g"]), 0.5,
                                    "mix_in_dx_ln1_bwd", token=token)
    small["ln1_g"], small["ln1_b"] = dg.sum(0, keepdims=True), db.sum(0, keepdims=True)
    dw = _ffn_out_dw(a1, dr1_h, "ffn1_out_dw")
    send["ffn1_w_out"] = dw.reshape(N_DEV, -1, d_model)
    exchange4, token = _exchange_start([send["ffn1_w_out"]] + [on_wire(n, small[n]) for n in SMALL_LATE], dr1_h,
                                       "grad_exchange4_start")
    dh1 = _ffn_out_dx(dr1_h, full["ffn1_w_out"], h1, "ffn1_out_dx", token=token)
    dw = _ffn_in_dw(x16, dh1, "ffn1_in_dw")
    send["ffn1_w_in"] = dw.reshape((N_DEV,) + dw.shape[2:])
    exchange5, token = _exchange_start([send["ffn1_w_in"]], dh1, "grad_exchange5_start")
    grad_x = _ffn_in_dx(dh1, full["ffn1_w_in"], dr1, alpha, "ffn1_in_dx", token=token)

    results = {}

    def update(names, recv, prefix):
        for n, r in zip(names, recv):
            results[n] = _adamw(r, wts[n], mom[n], var[n], prefix + n)

    def update_small(names, recv, name):
        items = [(r, lane_dense(n, wts[n]), lane_dense(n, mom[n]), lane_dense(n, var[n])) for n, r in zip(names, recv)]
        for n, outs in zip(names, _adamw_small(items, name)):
            results[n] = tuple(o.reshape(wts[n].shape) for o in outs)

    def done(names):
        return [results[n][3] for n in names]

    recv = _exchange_wait(exchange1, [grad_x], "grad_exchange1_wait")
    update(group1, recv[:len(group1)], "adamw_")
    update_small(SMALL_HEAD, recv[len(group1):-1], "adamw_ln3")
    loss = jnp.sum(recv[-1][:, 0, 0, 0])
    recv = _exchange_wait(exchange2, done(group1 + SMALL_HEAD), "grad_exchange2_wait")
    update(group2, recv[:len(group2)], "adamw_")
    update_small(SMALL_MID, recv[len(group2):], "adamw_small")
    recv = _exchange_wait(exchange3, done(group2 + SMALL_MID), "grad_exchange3_wait")
    update(("mix_w_in",), recv, "adamw_")
    recv = _exchange_wait(exchange4, done(("mix_w_in",)), "grad_exchange4_wait")
    update(("ffn1_w_out",), recv[:1], "adamw_")
    update_small(SMALL_LATE, recv[1:], "adamw_ln1")
    recv = _exchange_wait(exchange5, done(("ffn1_w_out",) + SMALL_LATE), "grad_exchange5_wait")
    update(("ffn1_w_in",), recv, "adamw_")

    outs = [loss, grad_x.reshape(n_seq, seq, d_model)]
    for k in range(4):
        outs += [results[n][k][None] if n in SHARDED else results[n][k] for n in WEIGHTS]
    return tuple(outs)
```

```python
import math

import jax
import jax.numpy as jnp
from jax import lax
from jax.experimental import pallas as pl
from jax.experimental.pallas import tpu as pltpu

F32 = jnp.float32
BF16 = jnp.bfloat16

N_DEV = 8
LN_EPS = 1e-5
CHUNK = 128
N_SEG = 8
S5_CHANNELS_PER_STEP = 128
S5_FWD_GROUPS = 2
VMEM_LIMIT_BYTES = 56 * 1024 * 1024
MM_OPERAND_BLOCK_BYTES = 8 * 1024 * 1024
ADAM_BLOCK_BYTES = 6 * 1024 * 1024

ADAM_LR = 0.001
ADAM_B1 = 0.9
ADAM_B2 = 0.999
ADAM_EPS = 1e-08
ADAM_WD = 0.01
ADAM_STEP = 10

COL_SHARDED = ("ffn1_w_in", "mix_w_in", "up_a", "up_b", "ffn2_w_in", "ple_w_proj")
SHARDED = ("ffn1_w_in", "ffn1_w_out", "mix_w_in", "ssm_glu_w", "up_a", "up_b", "mix_w_out",
           "ffn2_w_in", "ffn2_w_out", "ple_w_proj", "ple_w_gate")
WEIGHTS = ("ffn1_w_in", "ffn1_w_out", "ln1_g", "ln1_b", "mix_w_in", "ssm_lambda_re", "ssm_lambda_im",
           "ssm_log_dt", "ssm_b_re", "ssm_b_im", "ssm_c_re", "ssm_c_im", "ssm_d", "ssm_glu_w",
           "ssm_glu_b", "gmlp_ln_g", "gmlp_ln_b", "gmlp_w_s", "gmlp_b_s", "up_a", "up_b", "mix_w_out",
           "ln2_g", "ln2_b", "ffn2_w_in", "ffn2_w_out", "ln3_g", "ln3_b", "ple_w_proj", "ple_w_gate")
GATHER_FIRST = ("ffn1_w_in",)
GATHER_EARLY = ("ffn1_w_out", "mix_w_in")
GATHER_MID = ("ssm_glu_w", "up_a", "up_b", "mix_w_out")
GATHER_LAST = ("ffn2_w_in", "ffn2_w_out", "ple_w_proj", "ple_w_gate")
SMALL = tuple(n for n in WEIGHTS if n not in SHARDED)
SMALL_HEAD = ("ln3_g", "ln3_b")
SMALL_LATE = ("ln1_g", "ln1_b")
SMALL_MID = tuple(n for n in SMALL if n not in SMALL_HEAD + SMALL_LATE)
SMALL_BF16 = ("gmlp_w_s", "ssm_b_re", "ssm_b_im", "ssm_c_re", "ssm_c_im")
INPUT_NAMES = ("x", "p") + WEIGHTS + ("loss_target",) + tuple("m_" + n for n in WEIGHTS) + tuple(
    "v_" + n for n in WEIGHTS)


def _pick(dim, pref, mult=128):
    if dim <= pref:
        return dim
    d = (pref // mult) * mult
    while d >= mult:
        if dim % d == 0:
            return d
        d -= mult
    return dim


def _params(n_axes=1):
    return pltpu.CompilerParams(dimension_semantics=("arbitrary",) * n_axes,
                                vmem_limit_bytes=VMEM_LIMIT_BYTES)


def _sigmoid(v):
    return 0.5 * jnp.tanh(0.5 * v) + 0.5


_GELU_C = math.sqrt(2.0 / math.pi)


def _gelu(v):
    return 0.5 * v * (1.0 + jnp.tanh(_GELU_C * (v + 0.044715 * (v * v * v))))


def _gelu_grad(v):
    t = jnp.tanh(_GELU_C * (v + 0.044715 * (v * v * v)))
    return 0.5 * (1.0 + t) + 0.5 * v * (1.0 - t * t) * (_GELU_C * (1.0 + 3.0 * 0.044715 * v * v))


def _ln_stats(r):
    mu = jnp.mean(r, axis=-1, keepdims=True)
    xc = r - mu
    var = jnp.mean(xc * xc, axis=-1, keepdims=True)
    rstd = lax.rsqrt(var + LN_EPS)
    return xc * rstd, rstd


def _ln_bwd(dy, xhat, rstd, g):
    dxh = dy * g
    m1 = jnp.mean(dxh, axis=-1, keepdims=True)
    m2 = jnp.mean(dxh * xhat, axis=-1, keepdims=True)
    return rstd * (dxh - m1 - xhat * m2)


def _fold8(v):
    rows, w = v.shape
    return jnp.sum(v.reshape(rows // 8, 8, w), axis=0)


def _dot(a, b, ta=False, tb=False):
    dn = (((0 if ta else 1,), (1 if tb else 0,)), ((), ()))
    return lax.dot_general(a.astype(BF16), b.astype(BF16), dn, preferred_element_type=F32)


_TOKEN = pl.BlockSpec((8, 128), lambda *_: (0, 0))


def _mm(a, b, *, name, ta=False, tb=False, out_dtype=F32, add=None, add_scale=1.0, token=None,
        tm=2048, tn=512, tk=4096):
    m_dim = a.shape[1] if ta else a.shape[0]
    k_dim = a.shape[0] if ta else a.shape[1]
    n_dim = b.shape[0] if tb else b.shape[1]
    assert (b.shape[1] if tb else b.shape[0]) == k_dim, (name, a.shape, b.shape)
    tk = _pick(k_dim, tk)
    tm = min(tm, max(256, MM_OPERAND_BLOCK_BYTES // (tk * a.dtype.itemsize)))
    tm, tn = _pick(m_dim, tm), _pick(n_dim, tn)
    nk = k_dim // tk
    has_add = add is not None

    def finish(r, add_ref, o_ref):
        if has_add:
            r = r + add_scale * add_ref[...].astype(F32)
        o_ref[...] = r.astype(out_dtype)

    def body(*refs):
        a_ref, b_ref = refs[:2]
        add_ref = refs[2] if has_add else None
        o_ref = refs[n_in]
        if nk == 1:
            finish(_dot(a_ref[...], b_ref[...], ta, tb), add_ref, o_ref)
            return
        acc_ref = refs[-1]
        k = pl.program_id(2)

        @pl.when(k == 0)
        def _():
            acc_ref[...] = jnp.zeros_like(acc_ref)

        acc_ref[...] += _dot(a_ref[...], b_ref[...], ta, tb)

        @pl.when(k == nk - 1)
        def _():
            finish(acc_ref[...], add_ref, o_ref)

    a_spec = (pl.BlockSpec((tk, tm), lambda i, j, k: (k, i)) if ta
              else pl.BlockSpec((tm, tk), lambda i, j, k: (i, k)))
    b_spec = (pl.BlockSpec((tn, tk), lambda i, j, k: (j, k)) if tb
              else pl.BlockSpec((tk, tn), lambda i, j, k: (k, j)))
    in_specs = [a_spec, b_spec]
    operands = [a, b]
    if has_add:
        in_specs.append(pl.BlockSpec((tm, tn), lambda i, j, k: (i, j)))
        operands.append(add)
    if token is not None:
        in_specs.append(_TOKEN)
        operands.append(token)
    n_in = len(operands)
    return pl.pallas_call(
        body, name=name,
        out_shape=jax.ShapeDtypeStruct((m_dim, n_dim), out_dtype),
        grid=(m_dim // tm, n_dim // tn, nk),
        in_specs=in_specs,
        out_specs=pl.BlockSpec((tm, tn), lambda i, j, k: (i, j)),
        scratch_shapes=[pltpu.VMEM((tm, tn), F32)] if nk > 1 else [],
        compiler_params=_params(3),
    )(*operands)


def _to_bf16(arrays, name, token=None):
    n = len(arrays)
    extra = [] if token is None else [token]

    def body(*refs):
        for src, dst in zip(refs[:n], refs[n + len(extra):]):
            dst[...] = src[...].astype(BF16)

    vmem = pl.BlockSpec(memory_space=pltpu.VMEM)
    return pl.pallas_call(
        body, name=name, out_shape=tuple(jax.ShapeDtypeStruct(a.shape, BF16) for a in arrays),
        in_specs=[vmem] * (n + len(extra)), out_specs=tuple([vmem] * n),
        compiler_params=pltpu.CompilerParams(vmem_limit_bytes=VMEM_LIMIT_BYTES))(*arrays, *extra)


def _rows(tr, w, cb=0):
    return pl.BlockSpec((tr, w), lambda i: (i, cb))


def _whole(shape):
    nd = len(shape)
    return pl.BlockSpec(tuple(shape), lambda i: (0,) * nd)


def _ffn_in(x16, w_in, name, token=None):
    t, d = x16.shape
    _, nb, _, fb = w_in.shape
    tm = _pick(t, 2048, 8)
    extra = [] if token is None else [token]

    def body(*refs):
        x_ref, wg_ref, wu_ref = refs[:3]
        h_ref, a_ref = refs[-2:]
        xv = x_ref[...]
        g = _dot(xv, wg_ref[0, 0])
        u = _dot(xv, wu_ref[0, 0])
        h_ref[0, 0] = g.astype(BF16)
        h_ref[0, 1] = u.astype(BF16)
        a_ref[0] = (g * _sigmoid(g) * u).astype(BF16)

    return pl.pallas_call(
        body, name=name,
        out_shape=(jax.ShapeDtypeStruct((nb, 2, t, fb), BF16), jax.ShapeDtypeStruct((nb, t, fb), BF16)),
        grid=(t // tm, nb),
        in_specs=[pl.BlockSpec((tm, d), lambda i, j: (i, 0)),
                  pl.BlockSpec((1, 1, d, fb), lambda i, j: (0, j, 0, 0)),
                  pl.BlockSpec((1, 1, d, fb), lambda i, j: (1, j, 0, 0))] + [_TOKEN] * len(extra),
        out_specs=(pl.BlockSpec((1, 2, tm, fb), lambda i, j: (j, 0, i, 0)),
                   pl.BlockSpec((1, tm, fb), lambda i, j: (j, i, 0))),
        compiler_params=_params(2))(x16, w_in, w_in, *extra)


def _ffn_out_ln(a, w_out, xin, g, b, alpha, name, scale=0.5, token=None):
    nb, t, fb = a.shape
    d = w_out.shape[2]
    tm = _pick(t, 512, 8)
    extra = [] if token is None else [token]

    def body(*refs):
        a_ref, w_ref, x_ref, g_ref, b_ref = refs[:5]
        r_ref, y_ref, y16_ref = refs[-3:]
        f = _dot(a_ref[0], w_ref[0])
        for jb in range(1, nb):
            f = f + _dot(a_ref[jb], w_ref[jb])
        r = alpha * x_ref[...] + scale * f
        xhat, _ = _ln_stats(r)
        y = xhat * g_ref[...] + b_ref[...]
        r_ref[...] = r
        y_ref[...] = y
        y16_ref[...] = y.astype(BF16)

    return pl.pallas_call(
        body, name=name,
        out_shape=(jax.ShapeDtypeStruct((t, d), F32), jax.ShapeDtypeStruct((t, d), F32),
                   jax.ShapeDtypeStruct((t, d), BF16)),
        grid=(t // tm,),
        in_specs=[pl.BlockSpec((nb, tm, fb), lambda i: (0, i, 0)), _whole(w_out.shape), _rows(tm, d),
                  _whole((1, d)), _whole((1, d))] + [_TOKEN] * len(extra),
        out_specs=(_rows(tm, d), _rows(tm, d), _rows(tm, d)),
        compiler_params=_params())(a, w_out, xin, g, b, *extra)


def _ffn_out_dx(drs, w_out, h, name, token=None):
    nb, _, t, fb = h.shape
    d = w_out.shape[2]
    tm = _pick(t, 2048, 8)
    extra = [] if token is None else [token]

    def body(*refs):
        dr_ref, w_ref, h_ref, dh_ref = refs[0], refs[1], refs[2], refs[-1]
        da = _dot(dr_ref[...], w_ref[0], tb=True)
        g, u = h_ref[0, 0].astype(F32), h_ref[0, 1].astype(F32)
        sg = _sigmoid(g)
        silu = g * sg
        dh_ref[0, 0] = ((da * sg) * u * (1.0 + (g - silu))).astype(BF16)
        dh_ref[0, 1] = (da * silu).astype(BF16)

    return pl.pallas_call(
        body, name=name, out_shape=jax.ShapeDtypeStruct((nb, 2, t, fb), BF16), grid=(t // tm, nb),
        in_specs=[pl.BlockSpec((tm, d), lambda i, j: (i, 0)),
                  pl.BlockSpec((1, fb, d), lambda i, j: (j, 0, 0)),
                  pl.BlockSpec((1, 2, tm, fb), lambda i, j: (j, 0, i, 0))] + [_TOKEN] * len(extra),
        out_specs=pl.BlockSpec((1, 2, tm, fb), lambda i, j: (j, 0, i, 0)),
        compiler_params=_params(2))(drs, w_out, h, *extra)


def _ffn_out_dw(a, drs, name):
    nb, t, fb = a.shape
    d = drs.shape[1]

    def body(a_ref, dr_ref, o_ref):
        o_ref[0] = _dot(a_ref[0], dr_ref[...], ta=True).astype(BF16)

    return pl.pallas_call(
        body, name=name, out_shape=jax.ShapeDtypeStruct((nb, fb, d), BF16), grid=(nb,),
        in_specs=[pl.BlockSpec((1, t, fb), lambda j: (j, 0, 0)), pl.BlockSpec((t, d), lambda j: (0, 0))],
        out_specs=pl.BlockSpec((1, fb, d), lambda j: (j, 0, 0)),
        compiler_params=_params())(a, drs)


def _ffn_in_dw(x16, dh, name, token=None):
    t, d = x16.shape
    nb, _, _, fb = dh.shape
    extra = [] if token is None else [token]

    def body(*refs):
        x_ref, dh_ref, o_ref = refs[0], refs[1], refs[-1]
        o_ref[0, 0] = _dot(x_ref[...], dh_ref[0, 0], ta=True).astype(BF16)

    return pl.pallas_call(
        body, name=name, out_shape=jax.ShapeDtypeStruct((2, nb, d, fb), BF16), grid=(nb, 2),
        in_specs=[pl.BlockSpec((t, d), lambda j, s: (0, 0)),
                  pl.BlockSpec((1, 1, t, fb), lambda j, s: (j, s, 0, 0))] + [_TOKEN] * len(extra),
        out_specs=pl.BlockSpec((1, 1, d, fb), lambda j, s: (s, j, 0, 0)),
        compiler_params=_params(2))(x16, dh, *extra)


def _ffn_in_dx(dh, w_in, add, add_scale, name, token=None):
    nb, _, t, fb = dh.shape
    d = w_in.shape[2]
    tm = _pick(t, 512, 8)
    has_add = add is not None
    extra = [] if token is None else [token]

    def body(*refs):
        dh_ref, w_ref = refs[:2]
        o_ref = refs[-1]
        acc = None
        for jb in range(nb):
            for s in range(2):
                part = _dot(dh_ref[jb, s], w_ref[s, jb], tb=True)
                acc = part if acc is None else acc + part
        if has_add:
            acc = acc + add_scale * refs[2][...]
        o_ref[...] = acc

    return pl.pallas_call(
        body, name=name, out_shape=jax.ShapeDtypeStruct((t, d), F32), grid=(t // tm,),
        in_specs=[pl.BlockSpec((nb, 2, tm, fb), lambda i: (0, 0, i, 0)), _whole(w_in.shape)]
        + ([_rows(tm, d)] if has_add else []) + [_TOKEN] * len(extra),
        out_specs=_rows(tm, d),
        compiler_params=_params())(*([dh, w_in] + ([add] if has_add else []) + extra))


def _ffn_in_dx_ln(dh, w_in, r, dy_b, b_scale, ln_g, out_scale, name, token=None):
    nb, _, t, fb = dh.shape
    d = w_in.shape[2]
    tm = _pick(t, 512, 8)
    extra = [] if token is None else [token]

    def body(*refs):
        dh_ref, w_ref, r_ref, dyb_ref, g_ref = refs[:5]
        dr_ref, drs_ref, dg_ref, dbeta_ref = refs[-4:]
        dy = b_scale * dyb_ref[...]
        for jb in range(nb):
            for s in range(2):
                dy = dy + _dot(dh_ref[jb, s], w_ref[s, jb], tb=True)
        xhat, rstd = _ln_stats(r_ref[...])
        dr = _ln_bwd(dy, xhat, rstd, g_ref[...])
        dr_ref[...] = dr
        drs_ref[...] = (out_scale * dr).astype(BF16)

        @pl.when(pl.program_id(0) == 0)
        def _():
            dg_ref[...] = jnp.zeros_like(dg_ref)
            dbeta_ref[...] = jnp.zeros_like(dbeta_ref)

        dg_ref[...] += _fold8(dy * xhat)
        dbeta_ref[...] += _fold8(dy)

    return pl.pallas_call(
        body, name=name,
        out_shape=(jax.ShapeDtypeStruct((t, d), F32), jax.ShapeDtypeStruct((t, d), BF16),
                   jax.ShapeDtypeStruct((8, d), F32), jax.ShapeDtypeStruct((8, d), F32)),
        grid=(t // tm,),
        in_specs=[pl.BlockSpec((nb, 2, tm, fb), lambda i: (0, 0, i, 0)), _whole(w_in.shape), _rows(tm, d),
                  _rows(tm, d), _whole((1, d))] + [_TOKEN] * len(extra),
        out_specs=(_rows(tm, d), _rows(tm, d), _whole((8, d)), _whole((8, d))),
        compiler_params=_params())(dh, w_in, r, dy_b, ln_g, *extra)


def _mm_ln_bwd(a, w, r, dy_b, b_scale, ln_g, out_scale, name, token=None):
    t, k_dim = a.shape
    d = w.shape[0]
    tm = _pick(t, 512, 8)
    extra = [] if token is None else [token]

    def body(*refs):
        a_ref, w_ref, r_ref, dyb_ref, g_ref = refs[:5]
        dr_ref, drs_ref, dg_ref, dbeta_ref = refs[-4:]
        dy = _dot(a_ref[...], w_ref[...], tb=True) + b_scale * dyb_ref[...]
        xhat, rstd = _ln_stats(r_ref[...])
        dr = _ln_bwd(dy, xhat, rstd, g_ref[...])
        dr_ref[...] = dr
        drs_ref[...] = (out_scale * dr).astype(BF16)

        @pl.when(pl.program_id(0) == 0)
        def _():
            dg_ref[...] = jnp.zeros_like(dg_ref)
            dbeta_ref[...] = jnp.zeros_like(dbeta_ref)

        dg_ref[...] += _fold8(dy * xhat)
        dbeta_ref[...] += _fold8(dy)

    return pl.pallas_call(
        body, name=name,
        out_shape=(jax.ShapeDtypeStruct((t, d), F32), jax.ShapeDtypeStruct((t, d), BF16),
                   jax.ShapeDtypeStruct((8, d), F32), jax.ShapeDtypeStruct((8, d), F32)),
        grid=(t // tm,),
        in_specs=[_rows(tm, k_dim), _whole(w.shape), _rows(tm, d), _rows(tm, d), _whole((1, d))]
        + [_TOKEN] * len(extra),
        out_specs=(_rows(tm, d), _rows(tm, d), _whole((8, d)), _whole((8, d))),
        compiler_params=_params())(a, w, r, dy_b, ln_g, *extra)


def _take_row(v, row_id, s):
    return jnp.sum(jnp.where(row_id == s, v, 0.0), axis=0, keepdims=True)


def _complex_power(ar, ai, n):
    out = None
    while n:
        if n & 1:
            out = (ar, ai) if out is None else (out[0] * ar - out[1] * ai, out[0] * ai + out[1] * ar)
        ar, ai = ar * ar - ai * ai, 2.0 * ar * ai
        n >>= 1
    return out


def _seg_carries(f_re, f_im, p_re, p_im, reverse):
    row_id = lax.broadcasted_iota(jnp.int32, f_re.shape, 0)
    p_re, p_im = _take_row(p_re, row_id, 0), _take_row(p_im, row_id, 0)
    out_re, out_im = jnp.zeros_like(f_re), jnp.zeros_like(f_im)
    c_re, c_im = jnp.zeros_like(p_re), jnp.zeros_like(p_im)
    order = range(N_SEG - 1, -1, -1) if reverse else range(N_SEG)
    for s in order:
        out_re = jnp.where(row_id == s, c_re, out_re)
        out_im = jnp.where(row_id == s, c_im, out_im)
        fr, fi = _take_row(f_re, row_id, s), _take_row(f_im, row_id, s)
        c_re, c_im = fr + p_re * c_re - p_im * c_im, fi + p_re * c_im + p_im * c_re
    return out_re, out_im


def _s5_fwd(za16, b_re, b_im, a_re, a_im, c_re, c_im, n_seq, name):
    t = za16.shape[0]
    n_sg, ch1, st1 = b_re.shape
    per = S5_FWD_GROUPS if n_sg % S5_FWD_GROUPS == 0 else 1
    ch, st = per * ch1, per * st1
    seq = t // n_seq
    steps = seq // N_SEG

    def body(za_ref, bre_ref, bim_ref, are, aim, cre_ref, cim_ref, hre, him, y_ref):
        for g in range(per):
            za = za_ref[:, g * ch1:(g + 1) * ch1]
            hre[:, g * st1:(g + 1) * st1] = _dot(za, bre_ref[g])
            him[:, g * st1:(g + 1) * st1] = _dot(za, bim_ref[g])
        ar = jnp.broadcast_to(are[...], (N_SEG, st))
        ai = jnp.broadcast_to(aim[...], (N_SEG, st))
        zero = jnp.zeros((N_SEG, st), F32)
        p_re, p_im = _complex_power(ar, ai, steps)

        def local(k, carry):
            lr, li = carry
            rows = pl.ds(pl.multiple_of(k * N_SEG, N_SEG), N_SEG)
            nr = ar * lr - ai * li + hre[rows, :]
            ni = ar * li + ai * lr + him[rows, :]
            hre[rows, :] = nr
            him[rows, :] = ni
            return nr, ni

        f_re, f_im = lax.fori_loop(0, steps, local, (zero, zero))
        c_re, c_im = _seg_carries(f_re, f_im, p_re, p_im, reverse=False)

        def fix(k, carry):
            qr, qi = carry
            rows = pl.ds(pl.multiple_of(k * N_SEG, N_SEG), N_SEG)
            hre[rows, :] = hre[rows, :] + qr
            him[rows, :] = him[rows, :] + qi
            return ar * qr - ai * qi, ar * qi + ai * qr

        lax.fori_loop(0, steps, fix, (ar * c_re - ai * c_im, ar * c_im + ai * c_re))
        for g in range(per):
            states = slice(g * st1, (g + 1) * st1)
            y_ref[:, g * ch1:(g + 1) * ch1] = _dot(hre[:, states], cre_ref[g]) + _dot(him[:, states], cim_ref[g])

    rows_ch = pl.BlockSpec((seq, ch), lambda s, j: (s, j))
    rows_st = pl.BlockSpec((seq, st), lambda s, j: (s, j))
    vec = pl.BlockSpec((1, st), lambda s, j: (0, j))
    b_blk = pl.BlockSpec((per, ch1, st1), lambda s, j: (j, 0, 0))
    c_blk = pl.BlockSpec((per, st1, ch1), lambda s, j: (j, 0, 0))
    return pl.pallas_call(
        body, name=name,
        out_shape=(jax.ShapeDtypeStruct((t, n_sg * st1), F32), jax.ShapeDtypeStruct((t, n_sg * st1), F32),
                   jax.ShapeDtypeStruct((t, n_sg * ch1), F32)),
        grid=(n_seq, n_sg // per), in_specs=[rows_ch, b_blk, b_blk, vec, vec, c_blk, c_blk],
        out_specs=(rows_st, rows_st, rows_ch),
        compiler_params=_params(2))(za16, b_re, b_im, a_re, a_im, c_re, c_im)


def _s5_bwd(dy16, dza_skip, h_re, h_im, za16, b_re, b_im, a_re, a_im, c_re, c_im, n_seq, name):
    t = dy16.shape[0]
    n_sg, ch, st = b_re.shape
    seq = t // n_seq
    steps = seq // N_SEG

    def body(dy_ref, skip_ref, hre, him, za_ref, bre_ref, bim_ref, are, aim, cre_ref, cim_ref,
             dza_ref, dbre_ref, dbim_ref, dcre_ref, dcim_ref, dare, daim, lre, lim):
        dy = dy_ref[...]
        lre[...] = _dot(dy, cre_ref[0], tb=True)
        lim[...] = _dot(dy, cim_ref[0], tb=True)
        ar = jnp.broadcast_to(are[...], (N_SEG, st))
        ai = -jnp.broadcast_to(aim[...], (N_SEG, st))
        zero = jnp.zeros((N_SEG, st), F32)
        p_re, p_im = _complex_power(ar, ai, steps)

        def local(i, carry):
            lr, li = carry
            k = steps - 1 - i
            rows = pl.ds(pl.multiple_of(k * N_SEG, N_SEG), N_SEG)
            nr = ar * lr - ai * li + lre[rows, :]
            ni = ar * li + ai * lr + lim[rows, :]
            lre[rows, :] = nr
            lim[rows, :] = ni
            return nr, ni

        f_re, f_im = lax.fori_loop(0, steps, local, (zero, zero))
        c_re, c_im = _seg_carries(f_re, f_im, p_re, p_im, reverse=True)

        def accumulate(lam_r, lam_i, hp_r, hp_i, acc_r, acc_i):
            return acc_r + (lam_r * hp_r + lam_i * hp_i), acc_i + (lam_i * hp_r - lam_r * hp_i)

        def fix(i, carry):
            qr, qi, acc_r, acc_i = carry
            k = steps - 1 - i
            rows = pl.ds(pl.multiple_of(k * N_SEG, N_SEG), N_SEG)
            prev = pl.ds(pl.multiple_of((k - 1) * N_SEG, N_SEG), N_SEG)
            lam_r = lre[rows, :] + qr
            lam_i = lim[rows, :] + qi
            lre[rows, :] = lam_r
            lim[rows, :] = lam_i
            acc_r, acc_i = accumulate(lam_r, lam_i, hre[prev, :], him[prev, :], acc_r, acc_i)
            return ar * qr - ai * qi, ar * qi + ai * qr, acc_r, acc_i

        qr, qi, acc_r, acc_i = lax.fori_loop(
            0, steps - 1, fix, (ar * c_re - ai * c_im, ar * c_im + ai * c_re, zero, zero))
        first = pl.ds(0, N_SEG)
        last = pl.ds((steps - 1) * N_SEG, N_SEG)
        lam_r = lre[first, :] + qr
        lam_i = lim[first, :] + qi
        lre[first, :] = lam_r
        lim[first, :] = lam_i
        row_id = lax.broadcasted_iota(jnp.int32, (N_SEG, st), 0)
        hp_r = jnp.where(row_id == 0, 0.0, pltpu.roll(hre[last, :], 1, 0))
        hp_i = jnp.where(row_id == 0, 0.0, pltpu.roll(him[last, :], 1, 0))
        acc_r, acc_i = accumulate(lam_r, lam_i, hp_r, hp_i, acc_r, acc_i)

        lam_re16 = lre[...].astype(BF16)
        lam_im16 = lim[...].astype(BF16)
        dza_ref[...] = (_dot(lam_re16, bre_ref[0], tb=True) + _dot(lam_im16, bim_ref[0], tb=True)
                        + skip_ref[...]).astype(BF16)

        @pl.when(pl.program_id(1) == 0)
        def _():
            for ref in (dbre_ref, dbim_ref, dcre_ref, dcim_ref, dare, daim):
                ref[...] = jnp.zeros_like(ref)

        za = za_ref[...]
        dbre_ref[0] += _dot(za, lam_re16, ta=True)
        dbim_ref[0] += _dot(za, lam_im16, ta=True)
        dcre_ref[0] += _dot(hre[...], dy, ta=True)
        dcim_ref[0] += _dot(him[...], dy, ta=True)
        dare[...] += acc_r
        daim[...] += acc_i

    rows_ch = pl.BlockSpec((seq, ch), lambda j, s: (s, j))
    rows_st = pl.BlockSpec((seq, st), lambda j, s: (s, j))
    vec = pl.BlockSpec((1, st), lambda j, s: (0, j))
    b_blk = pl.BlockSpec((1, ch, st), lambda j, s: (j, 0, 0))
    c_blk = pl.BlockSpec((1, st, ch), lambda j, s: (j, 0, 0))
    acc = pl.BlockSpec((N_SEG, st), lambda j, s: (0, j))
    return pl.pallas_call(
        body, name=name,
        out_shape=(jax.ShapeDtypeStruct((t, n_sg * ch), BF16),
                   jax.ShapeDtypeStruct((n_sg, ch, st), F32), jax.ShapeDtypeStruct((n_sg, ch, st), F32),
                   jax.ShapeDtypeStruct((n_sg, st, ch), F32), jax.ShapeDtypeStruct((n_sg, st, ch), F32),
                   jax.ShapeDtypeStruct((N_SEG, n_sg * st), F32), jax.ShapeDtypeStruct((N_SEG, n_sg * st), F32)),
        grid=(n_sg, n_seq),
        in_specs=[rows_ch, rows_ch, rows_st, rows_st, rows_ch, b_blk, b_blk, vec, vec, c_blk, c_blk],
        out_specs=(rows_ch, b_blk, b_blk, c_blk, c_blk, acc, acc),
        scratch_shapes=[pltpu.VMEM((seq, st), F32), pltpu.VMEM((seq, st), F32)],
        compiler_params=_params(2))(dy16, dza_skip, h_re, h_im, za16, b_re, b_im, a_re, a_im, c_re, c_im)


def _s5_post_fwd(yssm, u, d_skip, glu_w, glu_b, name):
    t, w = yssm.shape
    tr = _pick(t, 512, 8)

    def body(y_ref, u_ref, d_ref, w_ref, b_ref, o_ref):
        yg = _gelu(y_ref[...] + d_ref[...] * u_ref[...])
        pre = _dot(yg, w_ref[...]) + b_ref[...]
        o_ref[...] = yg * _sigmoid(pre)

    return pl.pallas_call(
        body, name=name, out_shape=jax.ShapeDtypeStruct((t, w), F32), grid=(t // tr,),
        in_specs=[_rows(tr, w), _rows(tr, w), _whole((1, w)), _whole((w, w)), _whole((1, w))],
        out_specs=_rows(tr, w), compiler_params=_params())(yssm, u, d_skip, glu_w, glu_b)


def _s5_post_bwd(yssm, u, dout, d_skip, glu_w, glu_b, name):
    t, w = yssm.shape
    tr = _pick(t, 512, 8)

    def body(y_ref, u_ref, do_ref, d_ref, w_ref, b_ref, dy_ref, du_ref, dw_ref, dd_ref, db_ref):
        uu = u_ref[...]
        y = y_ref[...] + d_ref[...] * uu
        yg = _gelu(y)
        sg = _sigmoid(_dot(yg, w_ref[...]) + b_ref[...])
        do = do_ref[...]
        dpre = do * yg * sg * (1.0 - sg)
        dyg = do * sg + _dot(dpre, w_ref[...], tb=True)
        dy = dyg * _gelu_grad(y)
        dy_ref[...] = dy.astype(BF16)
        du_ref[...] = dy * d_ref[...]

        @pl.when(pl.program_id(0) == 0)
        def _():
            dw_ref[...] = jnp.zeros_like(dw_ref)
            dd_ref[...] = jnp.zeros_like(dd_ref)
            db_ref[...] = jnp.zeros_like(db_ref)

        dw_ref[...] += _dot(yg, dpre, ta=True)
        dd_ref[...] += _fold8(dy * uu)
        db_ref[...] += _fold8(dpre)

    return pl.pallas_call(
        body, name=name,
        out_shape=(jax.ShapeDtypeStruct((t, w), BF16), jax.ShapeDtypeStruct((t, w), F32),
                   jax.ShapeDtypeStruct((w, w), F32), jax.ShapeDtypeStruct((8, w), F32),
                   jax.ShapeDtypeStruct((8, w), F32)),
        grid=(t // tr,),
        in_specs=[_rows(tr, w), _rows(tr, w), _rows(tr, w), _whole((1, w)), _whole((w, w)),
                  _whole((1, w))],
        out_specs=(_rows(tr, w), _rows(tr, w), _whole((w, w)), _whole((8, w)), _whole((8, w))),
        compiler_params=_params())(yssm, u, dout, d_skip, glu_w, glu_b)


def _head_select(parts, head_dim):
    col_head = lax.broadcasted_iota(jnp.int32, parts[0].shape, 1) // head_dim
    out = jnp.zeros_like(parts[0])
    for h, part in enumerate(parts):
        out = jnp.where(col_head == h, part, out)
    return out


def _gmlp_fwd(proj, ln_g, ln_b, ws, bs_cols, name):
    t = proj.shape[0]
    n_heads = ws.shape[0]
    w = bs_cols.shape[1]
    head_dim = w // n_heads
    cpb = _pick(t // CHUNK, 4, 1)
    tr = cpb * CHUNK

    def body(zu_ref, zv_ref, g_ref, b_ref, ws_ref, bs_ref, o_ref):
        for c in range(cpb):
            rows = pl.ds(c * CHUNK, CHUNK)
            xhat, _ = _ln_stats(_gelu(zv_ref[rows, :]))
            vn = (xhat * g_ref[...] + b_ref[...]).astype(BF16)
            s = _head_select([_dot(ws_ref[h], vn) for h in range(n_heads)], head_dim) + bs_ref[...]
            o_ref[rows, :] = _gelu(zu_ref[rows, :]) * s

    return pl.pallas_call(
        body, name=name, out_shape=jax.ShapeDtypeStruct((t, w), F32), grid=(t // tr,),
        in_specs=[_rows(tr, w, 1), _rows(tr, w, 2), _whole((1, w)), _whole((1, w)),
                  _whole(ws.shape), _whole(bs_cols.shape)],
        out_specs=_rows(tr, w), compiler_params=_params())(proj, proj, ln_g, ln_b, ws, bs_cols)


def _gmlp_bwd(proj, dout, ln_g, ln_b, ws, ws_t, bs_cols, name):
    t = proj.shape[0]
    n_heads = ws.shape[0]
    w = bs_cols.shape[1]
    head_dim = w // n_heads
    cpb = _pick(t // CHUNK, 4, 1)
    tr = cpb * CHUNK

    def body(zu_ref, zv_ref, do_ref, g_ref, b_ref, ws_ref, wst_ref, bs_ref,
             dzu_ref, dzv_ref, dws_ref, dbs_ref, dg_ref, dbeta_ref):
        @pl.when(pl.program_id(0) == 0)
        def _():
            dws_ref[...] = jnp.zeros_like(dws_ref)
            dbs_ref[...] = jnp.zeros_like(dbs_ref)
            dg_ref[...] = jnp.zeros_like(dg_ref)
            dbeta_ref[...] = jnp.zeros_like(dbeta_ref)

        col_head = lax.broadcasted_iota(jnp.int32, (CHUNK, w), 1) // head_dim
        for c in range(cpb):
            rows = pl.ds(c * CHUNK, CHUNK)
            zu, zv, do = zu_ref[rows, :], zv_ref[rows, :], do_ref[rows, :]
            xhat, rstd = _ln_stats(_gelu(zv))
            vn = (xhat * g_ref[...] + b_ref[...]).astype(BF16)
            s = _head_select([_dot(ws_ref[h], vn) for h in range(n_heads)], head_dim) + bs_ref[...]
            dzu_ref[rows, :] = (do * s * _gelu_grad(zu)).astype(BF16)
            ds = do * _gelu(zu)
            ds16 = ds.astype(BF16)
            dbs_ref[...] += ds
            for h in range(n_heads):
                dws_ref[h] += _dot(jnp.where(col_head == h, ds16, jnp.zeros_like(ds16)), vn, tb=True)
            dvn = _head_select([_dot(wst_ref[h], ds16) for h in range(n_heads)], head_dim)
            dg_ref[...] += _fold8(dvn * xhat)
            dbeta_ref[...] += _fold8(dvn)
            dgv = _ln_bwd(dvn, xhat, rstd, g_ref[...])
            dzv_ref[rows, :] = (dgv * _gelu_grad(zv)).astype(BF16)

    return pl.pallas_call(
        body, name=name,
        out_shape=(jax.ShapeDtypeStruct((t, w), BF16), jax.ShapeDtypeStruct((t, w), BF16),
                   jax.ShapeDtypeStruct(ws.shape, F32), jax.ShapeDtypeStruct((CHUNK, w), F32),
                   jax.ShapeDtypeStruct((8, w), F32), jax.ShapeDtypeStruct((8, w), F32)),
        grid=(t // tr,),
        in_specs=[_rows(tr, w, 1), _rows(tr, w, 2), _rows(tr, w), _whole((1, w)), _whole((1, w)),
                  _whole(ws.shape), _whole(ws.shape), _whole(bs_cols.shape)],
        out_specs=(_rows(tr, w), _rows(tr, w), _whole(ws.shape), _whole((CHUNK, w)),
                   _whole((8, w)), _whole((8, w))),
        compiler_params=_params())(proj, proj, dout, ln_g, ln_b, ws, ws_t, bs_cols)


def _merge_fwd(s5out, gm, proj, up_a, up_b, name):
    t, w = s5out.shape
    d = up_a.shape[1]
    assert d == 2 * w
    tr = _pick(t, 512, 8)

    def body(s_ref, gm_ref, ga0, ga1, gb0, gb1, ua_ref, ub_ref, m_ref, ya_ref, yb_ref):
        ya = _dot(s_ref[...], ua_ref[...])
        yb = _dot(gm_ref[...], ub_ref[...])
        ya_ref[...] = ya.astype(BF16)
        yb_ref[...] = yb.astype(BF16)
        for half, (ga, gb) in enumerate(((ga0, gb0), (ga1, gb1))):
            cols = slice(half * w, (half + 1) * w)
            m_ref[:, cols] = (_sigmoid(ga[...]) * ya[:, cols] + _sigmoid(gb[...]) * yb[:, cols]).astype(BF16)

    return pl.pallas_call(
        body, name=name,
        out_shape=(jax.ShapeDtypeStruct((t, d), BF16), jax.ShapeDtypeStruct((t, d), BF16),
                   jax.ShapeDtypeStruct((t, d), BF16)),
        grid=(t // tr,),
        in_specs=[_rows(tr, w), _rows(tr, w), _rows(tr, w, 3), _rows(tr, w, 4), _rows(tr, w, 5),
                  _rows(tr, w, 6), _whole(up_a.shape), _whole(up_b.shape)],
        out_specs=(_rows(tr, d), _rows(tr, d), _rows(tr, d)),
        compiler_params=_params())(s5out, gm, proj, proj, proj, proj, up_a, up_b)


def _merge_bwd(dm, ya, yb, s5out, gm, proj, up_a, up_b, name):
    t, d = dm.shape
    w = d // 2
    tr = _pick(t, 512, 8)

    def body(dm_ref, ya_ref, yb_ref, s_ref, gm_ref, ga0, ga1, gb0, gb1, ua_ref, ub_ref,
             dga_ref, dgb_ref, ds_ref, dgm_ref, dua_ref, dub_ref):
        dm_v, ya, yb = dm_ref[...], ya_ref[...].astype(F32), yb_ref[...].astype(F32)
        dya, dyb = [], []
        for half, (ga, gb) in enumerate(((ga0, gb0), (ga1, gb1))):
            cols = slice(half * w, (half + 1) * w)
            sa, sb = _sigmoid(ga[...]), _sigmoid(gb[...])
            dmh = dm_v[:, cols]
            dga_ref[:, cols] = (dmh * ya[:, cols] * sa * (1.0 - sa)).astype(BF16)
            dgb_ref[:, cols] = (dmh * yb[:, cols] * sb * (1.0 - sb)).astype(BF16)
            dya.append((dmh * sa).astype(BF16))
            dyb.append((dmh * sb).astype(BF16))
        dya = jnp.concatenate(dya, axis=1)
        dyb = jnp.concatenate(dyb, axis=1)
        ds_ref[...] = _dot(dya, ua_ref[...], tb=True)
        dgm_ref[...] = _dot(dyb, ub_ref[...], tb=True)

        @pl.when(pl.program_id(0) == 0)
        def _():
            dua_ref[...] = jnp.zeros_like(dua_ref)
            dub_ref[...] = jnp.zeros_like(dub_ref)

        dua_ref[...] += _dot(s_ref[...], dya, ta=True)
        dub_ref[...] += _dot(gm_ref[...], dyb, ta=True)

    return pl.pallas_call(
        body, name=name,
        out_shape=(jax.ShapeDtypeStruct((t, d), BF16), jax.ShapeDtypeStruct((t, d), BF16),
                   jax.ShapeDtypeStruct((t, w), F32), jax.ShapeDtypeStruct((t, w), F32),
                   jax.ShapeDtypeStruct(up_a.shape, F32), jax.ShapeDtypeStruct(up_b.shape, F32)),
        grid=(t // tr,),
        in_specs=[_rows(tr, d), _rows(tr, d), _rows(tr, d), _rows(tr, w), _rows(tr, w),
                  _rows(tr, w, 3), _rows(tr, w, 4), _rows(tr, w, 5), _rows(tr, w, 6),
                  _whole(up_a.shape), _whole(up_b.shape)],
        out_specs=(_rows(tr, d), _rows(tr, d), _rows(tr, w), _rows(tr, w), _whole(up_a.shape),
                   _whole(up_b.shape)),
        compiler_params=_params())(dm, ya, yb, s5out, gm, proj, proj, proj, proj, up_a, up_b)


def _head(x3, r3, ln_g, p, target, w_gate, w_proj, out_scale, name):
    t, d = x3.shape
    pd = p.shape[1]
    tr = _pick(t, 512, 8)
    nb = t // tr

    def body(x_ref, r_ref, g_ref, p_ref, t_ref, wg_ref, wp_ref,
             loss_ref, dr_ref, drs_ref, dwg_ref, dwp_ref, dg_ref, dbeta_ref):
        xv = x_ref[...]
        sg = _sigmoid(_dot(xv, wg_ref[...]))
        pp = _dot(p_ref[...], wp_ref[...])
        err = xv + sg * pp - t_ref[...]
        loss_ref[...] = jnp.full((8, 128), 0.5 * jnp.sum(jnp.mean(err * err, axis=-1)), F32)
        dout = err * (1.0 / d)
        dgpre = dout * pp * sg * (1.0 - sg)
        dpp = dout * sg
        dx = dout + _dot(dgpre, wg_ref[...], tb=True)
        xhat, rstd = _ln_stats(r_ref[...])
        dr = _ln_bwd(dx, xhat, rstd, g_ref[...])
        dr_ref[...] = dr
        drs_ref[...] = (out_scale * dr).astype(BF16)

        @pl.when(pl.program_id(0) == 0)
        def _():
            for ref in (dwg_ref, dwp_ref, dg_ref, dbeta_ref):
                ref[...] = jnp.zeros_like(ref)

        dwg_ref[...] += _dot(xv, dgpre, ta=True)
        dwp_ref[...] += _dot(p_ref[...], dpp, ta=True)
        dg_ref[...] += _fold8(dx * xhat)
        dbeta_ref[...] += _fold8(dx)

    return pl.pallas_call(
        body, name=name,
        out_shape=(jax.ShapeDtypeStruct((nb * 8, 128), F32), jax.ShapeDtypeStruct((t, d), F32),
                   jax.ShapeDtypeStruct((t, d), BF16), jax.ShapeDtypeStruct((d, d), F32),
                   jax.ShapeDtypeStruct((pd, d), F32), jax.ShapeDtypeStruct((8, d), F32),
                   jax.ShapeDtypeStruct((8, d), F32)),
        grid=(nb,),
        in_specs=[_rows(tr, d), _rows(tr, d), _whole((1, d)), _rows(tr, pd), _rows(tr, d), _whole((d, d)),
                  _whole((pd, d))],
        out_specs=(pl.BlockSpec((8, 128), lambda i: (i, 0)), _rows(tr, d), _rows(tr, d), _whole((d, d)),
                   _whole((pd, d)), _whole((8, d)), _whole((8, d))),
        compiler_params=_params())(x3, r3, ln_g, p, target, w_gate, w_proj)


_HBM = pl.BlockSpec(memory_space=pltpu.HBM)


_SEM = pl.BlockSpec(memory_space=pltpu.SEMAPHORE)
_ANY = pl.BlockSpec(memory_space=pl.ANY)
_DATAFLOW = pltpu.SideEffectType.DATAFLOW_SIDE_EFFECTING


def _peer(k, x, y, c):
    return (1 - x if k & 4 else x, 1 - y if k & 2 else y, 1 - c if k & 1 else c)


def _exchange_start(sends, after, name):
    n = len(sends)
    lands = [lax.empty((N_DEV,) + s.shape[1:], s.dtype) for s in sends]

    def body(*refs):
        s_refs, l_refs = refs[:n], refs[n:2 * n]
        send_sems, recv_sems, local_sems, token = refs[2 * n + 1], refs[2 * n + 2], refs[2 * n + 3], refs[-1]
        x, y, c = lax.axis_index("x"), lax.axis_index("y"), lax.axis_index("c")
        me = 4 * x + 2 * y + c
        for a in range(n):
            same = sends[a].shape[0] == 1
            pltpu.make_async_copy(s_refs[a].at[0 if same else me], l_refs[a].at[me], local_sems.at[a]).start()
            for k in range(1, N_DEV):
                px, py, pc = _peer(k, x, y, c)
                pltpu.make_async_remote_copy(
                    src_ref=s_refs[a].at[0 if same else 4 * px + 2 * py + pc], dst_ref=l_refs[a].at[me],
                    send_sem=send_sems.at[7 * a + k - 1], recv_sem=recv_sems.at[7 * a + k - 1],
                    device_id=(px, py, pc), device_id_type=pl.DeviceIdType.MESH).start()
        token[...] = jnp.zeros_like(token)

    outs = pl.pallas_call(
        body, name=name,
        out_shape=(pltpu.SemaphoreType.DMA((7 * n,)), pltpu.SemaphoreType.DMA((7 * n,)),
                   pltpu.SemaphoreType.DMA((n,)),
                   *[pltpu.HBM(s.shape, s.dtype) for s in sends],
                   *[pltpu.HBM(l.shape, l.dtype) for l in lands],
                   jax.ShapeDtypeStruct((8, 128), F32)),
        in_specs=[_HBM] * (2 * n) + [_ANY],
        out_specs=(_SEM, _SEM, _SEM, *([_HBM] * (2 * n)), pl.BlockSpec(memory_space=pltpu.VMEM)),
        input_output_aliases={i: 3 + i for i in range(2 * n)},
        compiler_params=pltpu.CompilerParams(has_side_effects=_DATAFLOW),
    )(*[pltpu.with_memory_space_constraint(v, pltpu.HBM) for v in list(sends) + lands], after)
    return (outs[0], outs[1], outs[2], list(outs[3:3 + n]), list(outs[3 + n:3 + 2 * n])), outs[-1]


def _exchange_wait(handle, after, name):
    send_sems, recv_sems, local_sems, sends, lands = handle
    n = len(sends)

    def body(*refs):
        s_refs, l_refs = refs[:n], refs[n:2 * n]
        send_sems, recv_sems, local_sems = refs[2 * n], refs[2 * n + 1], refs[2 * n + 2]
        x, y, c = lax.axis_index("x"), lax.axis_index("y"), lax.axis_index("c")
        for a in range(n):
            pltpu.make_async_copy(s_refs[a].at[0], l_refs[a].at[0], local_sems.at[a]).wait()
            for k in range(1, N_DEV):
                copy = pltpu.make_async_remote_copy(
                    src_ref=s_refs[a].at[0], dst_ref=l_refs[a].at[0],
                    send_sem=send_sems.at[7 * a + k - 1], recv_sem=recv_sems.at[7 * a + k - 1],
                    device_id=_peer(k, x, y, c), device_id_type=pl.DeviceIdType.MESH)
                copy.wait_send()
                copy.wait_recv()

    outs = pl.pallas_call(
        body, name=name,
        out_shape=(*[pltpu.HBM(s.shape, s.dtype) for s in sends], *[pltpu.HBM(l.shape, l.dtype) for l in lands]),
        in_specs=[_HBM] * (2 * n) + [_SEM, _SEM, _SEM] + [_ANY] * len(after),
        out_specs=tuple([_HBM] * (2 * n)),
        input_output_aliases={i: i for i in range(2 * n)},
        compiler_params=pltpu.CompilerParams(has_side_effects=_DATAFLOW),
    )(*sends, *lands, send_sems, recv_sems, local_sems, *after)
    return list(outs[n:])


def _chips_of(x, y):
    return [(1 - x, y), (x, 1 - y), (1 - x, 1 - y)]


def _gather2_start(shards, after, name):
    n = len(shards)
    lands = [lax.empty((N_DEV,) + s.shape, s.dtype) for s in shards]

    def body(*refs):
        x_refs, l_refs = refs[:n], refs[n:2 * n]
        send_sems, recv_sems, local_sems, token = refs[2 * n + 1], refs[2 * n + 2], refs[2 * n + 3], refs[-1]
        x, y, c = lax.axis_index("x"), lax.axis_index("y"), lax.axis_index("c")
        me = 4 * x + 2 * y + c
        peers = [(x, y, 1 - c)] + [(*chip, c) for chip in _chips_of(x, y)]
        for a in range(n):
            pltpu.make_async_copy(x_refs[a], l_refs[a].at[me], local_sems.at[a]).start()
            for k, peer in enumerate(peers):
                pltpu.make_async_remote_copy(
                    src_ref=x_refs[a], dst_ref=l_refs[a].at[me],
                    send_sem=send_sems.at[4 * a + k], recv_sem=recv_sems.at[4 * a + k],
                    device_id=peer, device_id_type=pl.DeviceIdType.MESH).start()
        token[...] = jnp.zeros_like(token)

    outs = pl.pallas_call(
        body, name=name,
        out_shape=(pltpu.SemaphoreType.DMA((4 * n,)), pltpu.SemaphoreType.DMA((4 * n,)),
                   pltpu.SemaphoreType.DMA((n,)),
                   *[pltpu.HBM(s.shape, s.dtype) for s in shards],
                   *[pltpu.HBM(l.shape, l.dtype) for l in lands],
                   jax.ShapeDtypeStruct((8, 128), F32)),
        in_specs=[_HBM] * (2 * n) + [_ANY],
        out_specs=(_SEM, _SEM, _SEM, *([_HBM] * (2 * n)), pl.BlockSpec(memory_space=pltpu.VMEM)),
        input_output_aliases={i: 3 + i for i in range(2 * n)},
        compiler_params=pltpu.CompilerParams(has_side_effects=_DATAFLOW),
    )(*[pltpu.with_memory_space_constraint(v, pltpu.HBM) for v in list(shards) + lands], after)
    return (outs[0], outs[1], outs[2], list(outs[3:3 + n]), list(outs[3 + n:3 + 2 * n])), outs[-1]


def _gather2_forward(handle, after, name):
    send_sems, recv_sems, local_sems, shards, lands = handle
    n = len(shards)

    def body(*refs):
        x_refs, l_refs = refs[:n], refs[n:2 * n]
        send_sems, recv_sems = refs[2 * n], refs[2 * n + 1]
        out0 = 2 * n + 2 + len(after)
        fwd_send, fwd_recv, token = refs[out0], refs[out0 + 1], refs[-1]
        x, y, c = lax.axis_index("x"), lax.axis_index("y"), lax.axis_index("c")
        for a in range(n):
            for j, chip in enumerate(_chips_of(x, y)):
                block = l_refs[a].at[4 * chip[0] + 2 * chip[1] + c]
                pltpu.make_async_remote_copy(
                    src_ref=x_refs[a], dst_ref=block, send_sem=send_sems.at[4 * a + 1 + j],
                    recv_sem=recv_sems.at[4 * a + 1 + j], device_id=(*chip, c),
                    device_id_type=pl.DeviceIdType.MESH).wait_recv()
                pltpu.make_async_remote_copy(
                    src_ref=block, dst_ref=block, send_sem=fwd_send.at[3 * a + j], recv_sem=fwd_recv.at[3 * a + j],
                    device_id=(x, y, 1 - c), device_id_type=pl.DeviceIdType.MESH).start()
        token[...] = jnp.zeros_like(token)

    outs = pl.pallas_call(
        body, name=name,
        out_shape=(pltpu.SemaphoreType.DMA((3 * n,)), pltpu.SemaphoreType.DMA((3 * n,)),
                   *[pltpu.HBM(s.shape, s.dtype) for s in shards], *[pltpu.HBM(l.shape, l.dtype) for l in lands],
                   jax.ShapeDtypeStruct((8, 128), F32)),
        in_specs=[_HBM] * (2 * n) + [_SEM, _SEM] + [_ANY] * len(after),
        out_specs=(_SEM, _SEM, *([_HBM] * (2 * n)), pl.BlockSpec(memory_space=pltpu.VMEM)),
        input_output_aliases={i: 2 + i for i in range(2 * n)},
        compiler_params=pltpu.CompilerParams(has_side_effects=_DATAFLOW),
    )(*shards, *lands, send_sems, recv_sems, *after)
    handle = (send_sems, recv_sems, local_sems, outs[0], outs[1], list(outs[2:2 + n]), list(outs[2 + n:2 + 2 * n]))
    return handle, outs[-1]


def _gather2_wait(handle, after, name):
    send_sems, recv_sems, local_sems, fwd_send, fwd_recv, shards, lands = handle
    n = len(shards)

    def body(*refs):
        x_refs, l_refs = refs[:n], refs[n:2 * n]
        send_sems, recv_sems, local_sems, fwd_send, fwd_recv = refs[2 * n:2 * n + 5]
        x, y, c = lax.axis_index("x"), lax.axis_index("y"), lax.axis_index("c")
        sibling = (x, y, 1 - c)

        def copy(a, send_sem, recv_sem):
            return pltpu.make_async_remote_copy(
                src_ref=x_refs[a], dst_ref=l_refs[a].at[0], send_sem=send_sem, recv_sem=recv_sem,
                device_id=sibling, device_id_type=pl.DeviceIdType.MESH)

        for a in range(n):
            pltpu.make_async_copy(x_refs[a], l_refs[a].at[0], local_sems.at[a]).wait()
            for k in range(4):
                copy(a, send_sems.at[4 * a + k], recv_sems.at[4 * a + k]).wait_send()
            copy(a, send_sems.at[4 * a], recv_sems.at[4 * a]).wait_recv()
            for j in range(3):
                passed = copy(a, fwd_send.at[3 * a + j], fwd_recv.at[3 * a + j])
                passed.wait_send()
                passed.wait_recv()

    outs = pl.pallas_call(
        body, name=name,
        out_shape=(*[pltpu.HBM(s.shape, s.dtype) for s in shards], *[pltpu.HBM(l.shape, l.dtype) for l in lands]),
        in_specs=[_HBM] * (2 * n) + [_SEM] * 5 + [_ANY] * len(after),
        out_specs=tuple([_HBM] * (2 * n)),
        input_output_aliases={i: i for i in range(2 * n)},
        compiler_params=pltpu.CompilerParams(has_side_effects=_DATAFLOW),
    )(*shards, *lands, send_sems, recv_sems, local_sems, fwd_send, fwd_recv, *after)
    return list(outs[n:])


def _adamw(recv, w, m, v, name):
    n, rows, width = recv.shape
    tr = _pick(rows, max(8, ADAM_BLOCK_BYTES // (n * width * recv.dtype.itemsize)), 8)
    c_m = 1.0 - ADAM_B1 ** ADAM_STEP
    c_v = 1.0 - ADAM_B2 ** ADAM_STEP

    def body(r_ref, w_ref, m_ref, v_ref, g_ref, d_ref, nm_ref, nv_ref):
        g = r_ref[0].astype(F32)
        for i in range(1, n):
            g = g + r_ref[i].astype(F32)
        m2 = ADAM_B1 * m_ref[...] + (1.0 - ADAM_B1) * g
        v2 = ADAM_B2 * v_ref[...] + (1.0 - ADAM_B2) * (g * g)
        m_hat = m2 / c_m
        v_hat = v2 / c_v
        g_ref[...] = g
        d_ref[...] = -ADAM_LR * (m_hat / (jnp.sqrt(v_hat) + ADAM_EPS) + ADAM_WD * w_ref[...])
        nm_ref[...] = m2
        nv_ref[...] = v2

    blk = _rows(tr, width)
    shp = jax.ShapeDtypeStruct((rows, width), F32)
    return pl.pallas_call(
        body, name=name, out_shape=(shp, shp, shp, shp), grid=(rows // tr,),
        in_specs=[pl.BlockSpec((n, tr, width), lambda i: (0, i, 0)), blk, blk, blk],
        out_specs=(blk, blk, blk, blk), compiler_params=_params())(recv, w, m, v)


def _join_cols(gathered):
    n, r, c = gathered.shape
    return gathered.transpose(1, 0, 2).reshape(r, n * c)


def _split_cols(full):
    r, c = full.shape
    return full.reshape(r, N_DEV, c // N_DEV).transpose(1, 0, 2)


def _adamw_small(items, name):
    n = len(items)
    c_m = 1.0 - ADAM_B1 ** ADAM_STEP
    c_v = 1.0 - ADAM_B2 ** ADAM_STEP

    def body(*refs):
        ins, outs = refs[:4 * n], refs[4 * n:]
        for k in range(n):
            r_ref, w_ref, m_ref, v_ref = ins[4 * k:4 * k + 4]
            g = r_ref[0].astype(F32)
            for i in range(1, N_DEV):
                g = g + r_ref[i].astype(F32)
            m2 = ADAM_B1 * m_ref[...] + (1.0 - ADAM_B1) * g
            v2 = ADAM_B2 * v_ref[...] + (1.0 - ADAM_B2) * (g * g)
            outs[4 * k][...] = g
            outs[4 * k + 1][...] = -ADAM_LR * ((m2 / c_m) / (jnp.sqrt(v2 / c_v) + ADAM_EPS) + ADAM_WD * w_ref[...])
            outs[4 * k + 2][...] = m2
            outs[4 * k + 3][...] = v2

    vmem = pl.BlockSpec(memory_space=pltpu.VMEM)
    flat = pl.pallas_call(
        body, name=name,
        out_shape=tuple(jax.ShapeDtypeStruct(w.shape, F32) for _, w, _, _ in items for _ in range(4)),
        in_specs=[vmem] * (4 * n), out_specs=tuple([vmem] * (4 * n)),
        compiler_params=pltpu.CompilerParams(vmem_limit_bytes=VMEM_LIMIT_BYTES),
    )(*[a for item in items for a in item])
    return [tuple(flat[4 * k:4 * k + 4]) for k in range(n)]


def _discretise(lam_re, lam_im, log_dt, b_re, b_im):
    dt = jnp.exp(log_dt)
    mag = jnp.exp(lam_re * dt)
    ab_re = mag * jnp.cos(lam_im * dt)
    ab_im = mag * jnp.sin(lam_im * dt)
    nr = ab_re - 1.0
    ni = ab_im
    den = lam_re * lam_re + lam_im * lam_im
    coef_re = (nr * lam_re + ni * lam_im) / den
    coef_im = (ni * lam_re - nr * lam_im) / den
    return ab_re, ab_im, coef_re * b_re - coef_im * b_im, coef_re * b_im + coef_im * b_re


def _s5_discretise(operands, name, token):
    def body(*refs):
        for out_ref, v in zip(refs[6:], _discretise(*[r[...] for r in refs[:5]])):
            out_ref[...] = v

    vmem = pl.BlockSpec(memory_space=pltpu.VMEM)
    shape = jax.ShapeDtypeStruct(operands[0].shape, F32)
    return pl.pallas_call(body, name=name, out_shape=(shape,) * 4, in_specs=[vmem] * 6,
                          out_specs=(vmem,) * 4)(*operands, token)


def _s5_discretise_bwd(operands, cotangents, name):
    def body(*refs):
        _, vjp = jax.vjp(_discretise, *[r[...] for r in refs[:5]])
        for out_ref, v in zip(refs[9:], vjp(tuple(r[...] for r in refs[5:9]))):
            out_ref[...] = v

    vmem = pl.BlockSpec(memory_space=pltpu.VMEM)
    shape = jax.ShapeDtypeStruct(operands[0].shape, F32)
    return pl.pallas_call(body, name=name, out_shape=(shape,) * 5, in_specs=[vmem] * 9,
                          out_specs=(vmem,) * 5)(*operands, *cotangents)


def _same_group(gl):
    return jnp.eye(gl, dtype=bool)[None, :, None, :, None]


def _super_groups_in(bb, gl):
    g, p, i = bb.shape
    blocks = bb.reshape(g // gl, gl, p, i).transpose(0, 1, 3, 2)[:, :, :, None, :]
    return jnp.where(_same_group(gl), blocks, 0.0).reshape(g // gl, gl * i, gl * p)


def _super_groups_in_take(dense, gl, p, i):
    n_sg = dense.shape[0]
    blocks = jnp.where(_same_group(gl), dense.reshape(n_sg, gl, i, gl, p), 0.0).sum(axis=3)
    return blocks.transpose(0, 1, 3, 2).reshape(n_sg * gl, p, i)


def _super_groups_out(cc, gl):
    g, i, p = cc.shape
    blocks = cc.reshape(g // gl, gl, i, p).transpose(0, 1, 3, 2)[:, :, :, None, :]
    return jnp.where(_same_group(gl), blocks, 0.0).reshape(g // gl, gl * p, gl * i)


def _super_groups_out_take(dense, gl, i, p):
    n_sg = dense.shape[0]
    blocks = jnp.where(_same_group(gl), dense.reshape(n_sg, gl, p, gl, i), 0.0).sum(axis=3)
    return blocks.transpose(0, 1, 3, 2).reshape(n_sg * gl, i, p)


def _perm_rows(z, n_seq):
    t, w = z.shape
    steps = t // n_seq // N_SEG
    return z.reshape(n_seq, N_SEG, steps, w).transpose(0, 2, 1, 3).reshape(t, w)


def _unperm_rows(z, n_seq):
    t, w = z.shape
    steps = t // n_seq // N_SEG
    return z.reshape(n_seq, steps, N_SEG, w).transpose(0, 2, 1, 3).reshape(t, w)


def kernel(*args):
    assert len(args) == len(INPUT_NAMES)
    inp = dict(zip(INPUT_NAMES, args))
    depth = inp["ffn1_w_in"].shape[0]
    assert depth == 1
    alpha = (2.0 * depth) ** 0.25

    n_seq, seq, d_model = inp["x"].shape
    t = n_seq * seq
    x = inp["x"].reshape(t, d_model)
    target = inp["loss_target"].reshape(t, d_model)
    wts = {n: inp[n][0] if n in SHARDED else inp[n] for n in WEIGHTS}
    mom = {n: inp["m_" + n][0] if n in SHARDED else inp["m_" + n] for n in WEIGHTS}
    var = {n: inp["v_" + n][0] if n in SHARDED else inp["v_" + n] for n in WEIGHTS}

    full = {}

    def place(n, g):
        if n in ("ffn1_w_in", "ffn2_w_in"):
            full[n] = g.reshape((2, N_DEV // 2) + g.shape[1:])
        elif n in ("ffn1_w_out", "ffn2_w_out"):
            full[n] = g.reshape(N_DEV // 2, 2 * g.shape[1], g.shape[2])
        elif n in COL_SHARDED:
            full[n] = _join_cols(g)
        else:
            full[n] = g.reshape(N_DEV * g.shape[1], g.shape[2])

    w16 = dict(zip(GATHER_FIRST, _to_bf16([wts[n] for n in GATHER_FIRST], "cast_ffn1")))
    gather_first, gather_first_token = _gather2_start([w16[n] for n in GATHER_FIRST], w16[GATHER_FIRST[0]],
                                                      "gather_first_start")
    later = tuple(n for n in SHARDED if n not in GATHER_FIRST)
    casts = _to_bf16([wts[n] for n in later] + [x, inp["p"][0].reshape(t, -1)], "cast_rest",
                     token=gather_first_token)
    w16.update(zip(later, casts))
    x16, p16 = casts[-2:]

    def gather_start(names, after, name):
        return _exchange_start([w16[n][None] for n in names], after, name)

    def gather_wait(names, handle, after, name):
        for n, g in zip(names, _exchange_wait(handle, after, name)):
            place(n, g)

    def gather2_wait(names, handle, after, name):
        for n, g in zip(names, _gather2_wait(handle, after, name)):
            place(n, g)


    def row(v):
        return v.reshape(1, -1)

    _, n_groups, n_state = wts["ssm_lambda_re"].shape
    n_ch = wts["ssm_b_re"].shape[3]
    disc_in = (jnp.repeat(wts["ssm_lambda_re"][0], n_ch, axis=1), jnp.repeat(wts["ssm_lambda_im"][0], n_ch, axis=1),
               jnp.broadcast_to(wts["ssm_log_dt"][0][:, None], (n_groups, n_state * n_ch)),
               wts["ssm_b_re"][0].reshape(n_groups, -1), wts["ssm_b_im"][0].reshape(n_groups, -1))
    ab_re, ab_im, bb_re, bb_im = _s5_discretise(disc_in, "s5_discretise", gather_first_token)
    a_re, a_im = row(ab_re[:, ::n_ch]), row(ab_im[:, ::n_ch])
    bb_re, bb_im = bb_re.reshape(n_groups, n_state, n_ch), bb_im.reshape(n_groups, n_state, n_ch)
    gl = S5_CHANNELS_PER_STEP // n_ch
    b_sg_re = _super_groups_in(bb_re, gl).astype(BF16)
    b_sg_im = _super_groups_in(bb_im, gl).astype(BF16)
    c_sg_re = _super_groups_out(wts["ssm_c_re"][0], gl).astype(BF16)
    c_sg_im = _super_groups_out(-wts["ssm_c_im"][0], gl).astype(BF16)

    ws = wts["gmlp_w_s"][0]
    n_heads = ws.shape[0]
    d_gmlp = wts["gmlp_ln_g"].shape[1]
    causal = jnp.tril(jnp.ones((CHUNK, CHUNK), dtype=bool))
    ws_masked = jnp.where(causal[None], ws, 0.0).astype(BF16)
    ws_masked_t = ws_masked.transpose(0, 2, 1)
    bs_cols = jnp.repeat(wts["gmlp_b_s"][0].T, d_gmlp // n_heads, axis=1)
    d_ssm = n_groups * n_ch
    assert d_ssm == d_gmlp and d_model == 2 * d_ssm

    prepared = [x16, b_sg_re, b_sg_im, c_sg_re, c_sg_im, ws_masked_t, bs_cols]
    gather_first, _ = _gather2_forward(gather_first, prepared, "gather_first_forward")
    gather2_wait(GATHER_FIRST, gather_first, prepared, "gather_first_wait")
    gather_early, gather_early_token = _gather2_start([w16[n] for n in GATHER_EARLY], full["ffn1_w_in"],
                                                      "gather_early_start")
    h1, a1 = _ffn_in(x16, full["ffn1_w_in"], "ffn1_in", token=gather_early_token)
    gather_early, _ = _gather2_forward(gather_early, [a1], "gather_early_forward")
    gather2_wait(GATHER_EARLY, gather_early, [a1], "gather_early_wait")
    r1, x1, x1_16 = _ffn_out_ln(a1, full["ffn1_w_out"], x, row(wts["ln1_g"]), row(wts["ln1_b"]), alpha,
                                "ffn1_out_ln")

    gather_mid, token = gather_start(GATHER_MID, x1_16, "gather_mid_start")
    gather_last, token = _gather2_start([w16[n] for n in GATHER_LAST], token, "gather_last_start")
    proj = _mm(x1_16, full["mix_w_in"], name="mix_in", token=token)
    za_p = _perm_rows(proj[:, :d_ssm], n_seq)
    h_re, h_im, yssm = _s5_fwd(za_p, b_sg_re, b_sg_im, a_re, a_im, c_sg_re, c_sg_im, n_seq, "s5_fwd")
    gather_wait(GATHER_MID, gather_mid, [yssm], "gather_mid_wait")
    s5out_p = _s5_post_fwd(yssm, za_p, row(wts["ssm_d"]), full["ssm_glu_w"], row(wts["ssm_glu_b"]),
                           "s5_post")
    s5out = _unperm_rows(s5out_p, n_seq)
    gm = _gmlp_fwd(proj, row(wts["gmlp_ln_g"]), row(wts["gmlp_ln_b"]), ws_masked, bs_cols, "gmlp")
    m_mix, y_a, y_b = _merge_fwd(s5out, gm, proj, full["up_a"], full["up_b"], "merge")
    gather_last, token = _gather2_forward(gather_last, [m_mix], "gather_last_forward")
    r2, x2, x2_16 = _ffn_out_ln(m_mix[None], full["mix_w_out"][None], x1, row(wts["ln2_g"]), row(wts["ln2_b"]),
                                alpha, "mix_out_ln2", scale=1.0, token=token)

    gather2_wait(GATHER_LAST, gather_last, [x2_16], "gather_last_wait")
    h2, a2 = _ffn_in(x2_16, full["ffn2_w_in"], "ffn2_in")
    r3, x3, _ = _ffn_out_ln(a2, full["ffn2_w_out"], x2, row(wts["ln3_g"]), row(wts["ln3_b"]), alpha,
                            "ffn2_out_ln")

    small = {}
    send = {}

    def lane_dense(n, v):
        return v.reshape(v.shape[:2] + (-1,)) if n in ("ssm_b_re", "ssm_b_im") else v

    def on_wire(n, g):
        g = lane_dense(n, g)
        return (g.astype(BF16) if n in SMALL_BF16 else g)[None]
    loss_parts, dr3, dr3_h, dw_gate, dw_proj, dg, db = _head(
        x3, r3, row(wts["ln3_g"]), p16, target, full["ple_w_gate"], full["ple_w_proj"], 0.5, "head")
    loss_mine = jnp.full((1, 1, 8, 128), jnp.sum(loss_parts[::8, 0]), F32)
    send["ple_w_gate"] = dw_gate.astype(BF16).reshape(N_DEV, -1, d_model)
    send["ple_w_proj"] = _split_cols(dw_proj.astype(BF16))
    small["ln3_g"], small["ln3_b"] = dg.sum(0, keepdims=True), db.sum(0, keepdims=True)
    dh2 = _ffn_out_dx(dr3_h, full["ffn2_w_out"], h2, "ffn2_out_dx")
    dw = _ffn_out_dw(a2, dr3_h, "ffn2_out_dw")
    send["ffn2_w_out"] = dw.reshape(N_DEV, -1, d_model)
    dw = _ffn_in_dw(x2_16, dh2, "ffn2_in_dw")
    send["ffn2_w_in"] = dw.reshape((N_DEV,) + dw.shape[2:])
    group1 = ("ffn2_w_in", "ffn2_w_out", "ple_w_gate", "ple_w_proj")
    exchange1, token = _exchange_start(
        [send[n] for n in group1] + [on_wire(n, small[n]) for n in SMALL_HEAD] + [loss_mine], dh2,
        "grad_exchange1_start")
    dr2, dr2_h, dg, db = _ffn_in_dx_ln(dh2, full["ffn2_w_in"], r2, dr3, alpha, row(wts["ln2_g"]), 1.0,
                                       "ffn2_in_dx_ln2_bwd", token=token)
    small["ln2_g"], small["ln2_b"] = dg.sum(0, keepdims=True), db.sum(0, keepdims=True)
    dm = _mm(dr2_h, full["mix_w_out"], tb=True, name="mix_out_dx")
    send["mix_w_out"] = _mm(m_mix, dr2_h, ta=True, name="mix_out_dw", out_dtype=BF16).reshape(
        N_DEV, -1, d_model)
    dga, dgb, ds5out, dgm, dw_a, dw_b = _merge_bwd(
        dm, y_a, y_b, s5out, gm, proj, full["up_a"], full["up_b"], "merge_bwd")
    send["up_a"] = _split_cols(dw_a.astype(BF16))
    send["up_b"] = _split_cols(dw_b.astype(BF16))

    dzu, dzv, dws, dbs_cols, dg, db = _gmlp_bwd(
        proj, dgm, row(wts["gmlp_ln_g"]), row(wts["gmlp_ln_b"]), ws_masked, ws_masked_t, bs_cols,
        "gmlp_bwd")
    small["gmlp_ln_g"], small["gmlp_ln_b"] = dg.sum(0, keepdims=True), db.sum(0, keepdims=True)
    small["gmlp_w_s"] = jnp.where(causal[None], dws, 0.0)[None]
    small["gmlp_b_s"] = dbs_cols.reshape(CHUNK, n_heads, -1).sum(-1).T[None]

    dy_p, dza_skip, dw_glu, dd, dgb_glu = _s5_post_bwd(
        yssm, za_p, _perm_rows(ds5out, n_seq), row(wts["ssm_d"]), full["ssm_glu_w"], row(wts["ssm_glu_b"]),
        "s5_post_bwd")
    send["ssm_glu_w"] = dw_glu.astype(BF16).reshape(N_DEV, -1, d_ssm)
    small["ssm_d"], small["ssm_glu_b"] = dd.sum(0, keepdims=True), dgb_glu.sum(0, keepdims=True)
    dza_p, dbb_re, dbb_im, dc_re, dc_im, da_re, da_im = _s5_bwd(
        dy_p, dza_skip, h_re, h_im, za_p, b_sg_re, b_sg_im, a_re, a_im, c_sg_re, c_sg_im, n_seq, "s5_bwd")
    small["ssm_c_re"] = _super_groups_out_take(dc_re, gl, n_ch, n_state)[None]
    small["ssm_c_im"] = -_super_groups_out_take(dc_im, gl, n_ch, n_state)[None]
    dbb_re = _super_groups_in_take(dbb_re, gl, n_state, n_ch)
    dbb_im = _super_groups_in_take(dbb_im, gl, n_state, n_ch)
    def first_channel(v):
        placed = jnp.pad(v.sum(0).reshape(n_groups, n_state, 1), ((0, 0), (0, 0), (0, n_ch - 1)))
        return placed.reshape(n_groups, -1)

    cotangents = (first_channel(da_re), first_channel(da_im), dbb_re.reshape(n_groups, -1),
                  dbb_im.reshape(n_groups, -1))
    d_lre, d_lim, d_ldt, d_bre, d_bim = _s5_discretise_bwd(disc_in, cotangents, "s5_discretise_bwd")
    per_state = (n_groups, n_state, n_ch)
    small["ssm_lambda_re"] = d_lre.reshape(per_state).sum(-1)[None]
    small["ssm_lambda_im"] = d_lim.reshape(per_state).sum(-1)[None]
    small["ssm_log_dt"] = d_ldt.sum(-1)[None]
    small["ssm_b_re"] = d_bre.reshape((1,) + per_state)
    small["ssm_b_im"] = d_bim.reshape((1,) + per_state)

    dproj = jnp.concatenate([_unperm_rows(dza_p, n_seq), dzu, dzv, dga, dgb], axis=1)
    group2 = ("mix_w_out", "up_a", "up_b", "ssm_glu_w")
    exchange2, token = _exchange_start(
        [send[n] for n in group2] + [on_wire(n, small[n]) for n in SMALL_MID], dproj, "grad_exchange2_start")
    send["mix_w_in"] = _split_cols(_mm(x1_16, dproj, ta=True, name="mix_in_dw", out_dtype=BF16, token=token))
    exchange3, token = _exchange_start([send["mix_w_in"]], dproj, "grad_exchange3_start")
    dr1, dr1_h, dg, db = _mm_ln_bwd(dproj, full["mix_w_in"], r1, dr2, alpha, row(wts["ln1_g"]), 0.5,
                                    "mix_in_dx_ln1_bwd", token=token)
    small["ln1_g"], small["ln1_b"] = dg.sum(0, keepdims=True), db.sum(0, keepdims=True)
    dw = _ffn_out_dw(a1, dr1_h, "ffn1_out_dw")
    send["ffn1_w_out"] = dw.reshape(N_DEV, -1, d_model)
    exchange4, token = _exchange_start([send["ffn1_w_out"]] + [on_wire(n, small[n]) for n in SMALL_LATE], dr1_h,
                                       "grad_exchange4_start")
    dh1 = _ffn_out_dx(dr1_h, full["ffn1_w_out"], h1, "ffn1_out_dx", token=token)
    dw = _ffn_in_dw(x16, dh1, "ffn1_in_dw")
    send["ffn1_w_in"] = dw.reshape((N_DEV,) + dw.shape[2:])
    exchange5, token = _exchange_start([send["ffn1_w_in"]], dh1, "grad_exchange5_start")
    grad_x = _ffn_in_dx(dh1, full["ffn1_w_in"], dr1, alpha, "ffn1_in_dx", token=token)

    results = {}

    def update(names, recv, prefix):
        for n, r in zip(names, recv):
            results[n] = _adamw(r, wts[n], mom[n], var[n], prefix + n)

    def update_small(names, recv, name):
        items = [(r, lane_dense(n, wts[n]), lane_dense(n, mom[n]), lane_dense(n, var[n])) for n, r in zip(names, recv)]
        for n, outs in zip(names, _adamw_small(items, name)):
            results[n] = tuple(o.reshape(wts[n].shape) for o in outs)

    def done(names):
        return [results[n][3] for n in names]

    recv = _exchange_wait(exchange1, [grad_x], "grad_exchange1_wait")
    update(group1, recv[:len(group1)], "adamw_")
    update_small(SMALL_HEAD, recv[len(group1):-1], "adamw_ln3")
    loss = jnp.sum(recv[-1][:, 0, 0, 0])
    recv = _exchange_wait(exchange2, done(group1 + SMALL_HEAD), "grad_exchange2_wait")
    update(group2, recv[:len(group2)], "adamw_")
    update_small(SMALL_MID, recv[len(group2):], "adamw_small")
    recv = _exchange_wait(exchange3, done(group2 + SMALL_MID), "grad_exchange3_wait")
    update(("mix_w_in",), recv, "adamw_")
    recv = _exchange_wait(exchange4, done(("mix_w_in",)), "grad_exchange4_wait")
    update(("ffn1_w_out",), recv[:1], "adamw_")
    update_small(SMALL_LATE, recv[1:], "adamw_ln1")
    recv = _exchange_wait(exchange5, done(("ffn1_w_out",) + SMALL_LATE), "grad_exchange5_wait")
    update(("ffn1_w_in",), recv, "adamw_")

    outs = [loss, grad_x.reshape(n_seq, seq, d_model)]
    for k in range(4):
        outs += [results[n][k][None] if n in SHARDED else results[n][k] for n in WEIGHTS]
    return tuple(outs)
```

```python
import math

import jax
import jax.numpy as jnp
from jax import lax
from jax.experimental import pallas as pl
from jax.experimental.pallas import tpu as pltpu

F32 = jnp.float32
BF16 = jnp.bfloat16

N_DEV = 8
LN_EPS = 1e-5
CHUNK = 128
N_SEG = 8
LANES = 128
S5_CHANNELS_PER_STEP = 128
S5_FWD_GROUPS = 2
VMEM_LIMIT_BYTES = 56 * 1024 * 1024
MM_OPERAND_BLOCK_BYTES = 8 * 1024 * 1024
ADAM_BLOCK_BYTES = 6 * 1024 * 1024

ADAM_LR = 0.001
ADAM_B1 = 0.9
ADAM_B2 = 0.999
ADAM_EPS = 1e-08
ADAM_WD = 0.01
ADAM_STEP = 10

COL_SHARDED = ("ffn1_w_in", "mix_w_in", "up_a", "up_b", "ffn2_w_in", "ple_w_proj")
SHARDED = ("ffn1_w_in", "ffn1_w_out", "mix_w_in", "ssm_glu_w", "up_a", "up_b", "mix_w_out",
           "ffn2_w_in", "ffn2_w_out", "ple_w_proj", "ple_w_gate")
WEIGHTS = ("ffn1_w_in", "ffn1_w_out", "ln1_g", "ln1_b", "mix_w_in", "ssm_lambda_re", "ssm_lambda_im",
           "ssm_log_dt", "ssm_b_re", "ssm_b_im", "ssm_c_re", "ssm_c_im", "ssm_d", "ssm_glu_w",
           "ssm_glu_b", "gmlp_ln_g", "gmlp_ln_b", "gmlp_w_s", "gmlp_b_s", "up_a", "up_b", "mix_w_out",
           "ln2_g", "ln2_b", "ffn2_w_in", "ffn2_w_out", "ln3_g", "ln3_b", "ple_w_proj", "ple_w_gate")
GATHER_FIRST = ("ffn1_w_in",)
GATHER_EARLY = ("ffn1_w_out", "mix_w_in")
GATHER_MID = ("ssm_glu_w", "up_a", "up_b", "mix_w_out")
GATHER_LAST = ("ffn2_w_in", "ffn2_w_out", "ple_w_proj", "ple_w_gate")
SMALL = tuple(n for n in WEIGHTS if n not in SHARDED)
SMALL_HEAD = ("ln3_g", "ln3_b")
SMALL_LATE = ("ln1_g", "ln1_b")
SMALL_MID = tuple(n for n in SMALL if n not in SMALL_HEAD + SMALL_LATE)
SMALL_BF16 = ("gmlp_w_s", "ssm_b_re", "ssm_b_im", "ssm_c_re", "ssm_c_im")
INPUT_NAMES = ("x", "p") + WEIGHTS + ("loss_target",) + tuple("m_" + n for n in WEIGHTS) + tuple(
    "v_" + n for n in WEIGHTS)


def _pick(dim, pref, mult=128):
    if dim <= pref:
        return dim
    d = (pref // mult) * mult
    while d >= mult:
        if dim % d == 0:
            return d
        d -= mult
    return dim


def _params(n_axes=1):
    return pltpu.CompilerParams(dimension_semantics=("arbitrary",) * n_axes,
                                vmem_limit_bytes=VMEM_LIMIT_BYTES)


def _sigmoid(v):
    return 0.5 * jnp.tanh(0.5 * v) + 0.5


_GELU_C = math.sqrt(2.0 / math.pi)


def _gelu(v):
    return 0.5 * v * (1.0 + jnp.tanh(_GELU_C * (v + 0.044715 * (v * v * v))))


def _gelu_grad(v):
    t = jnp.tanh(_GELU_C * (v + 0.044715 * (v * v * v)))
    return 0.5 * (1.0 + t) + 0.5 * v * (1.0 - t * t) * (_GELU_C * (1.0 + 3.0 * 0.044715 * v * v))


def _ln_stats(r):
    mu = jnp.mean(r, axis=-1, keepdims=True)
    xc = r - mu
    var = jnp.mean(xc * xc, axis=-1, keepdims=True)
    rstd = lax.rsqrt(var + LN_EPS)
    return xc * rstd, rstd


def _ln_bwd(dy, xhat, rstd, g):
    dxh = dy * g
    m1 = jnp.mean(dxh, axis=-1, keepdims=True)
    m2 = jnp.mean(dxh * xhat, axis=-1, keepdims=True)
    return rstd * (dxh - m1 - xhat * m2)


def _fold8(v):
    rows, w = v.shape
    return jnp.sum(v.reshape(rows // 8, 8, w), axis=0)


def _dot(a, b, ta=False, tb=False):
    dn = (((0 if ta else 1,), (1 if tb else 0,)), ((), ()))
    return lax.dot_general(a.astype(BF16), b.astype(BF16), dn, preferred_element_type=F32)


_TOKEN = pl.BlockSpec((8, 128), lambda *_: (0, 0))


def _mm(a, b, *, name, ta=False, tb=False, out_dtype=F32, add=None, add_scale=1.0, token=None,
        tm=2048, tn=512, tk=4096):
    m_dim = a.shape[1] if ta else a.shape[0]
    k_dim = a.shape[0] if ta else a.shape[1]
    n_dim = b.shape[0] if tb else b.shape[1]
    assert (b.shape[1] if tb else b.shape[0]) == k_dim, (name, a.shape, b.shape)
    tk = _pick(k_dim, tk)
    tm = min(tm, max(256, MM_OPERAND_BLOCK_BYTES // (tk * a.dtype.itemsize)))
    tm, tn = _pick(m_dim, tm), _pick(n_dim, tn)
    nk = k_dim // tk
    has_add = add is not None

    def finish(r, add_ref, o_ref):
        if has_add:
            r = r + add_scale * add_ref[...].astype(F32)
        o_ref[...] = r.astype(out_dtype)

    def body(*refs):
        a_ref, b_ref = refs[:2]
        add_ref = refs[2] if has_add else None
        o_ref = refs[n_in]
        if nk == 1:
            finish(_dot(a_ref[...], b_ref[...], ta, tb), add_ref, o_ref)
            return
        acc_ref = refs[-1]
        k = pl.program_id(2)

        @pl.when(k == 0)
        def _():
            acc_ref[...] = jnp.zeros_like(acc_ref)

        acc_ref[...] += _dot(a_ref[...], b_ref[...], ta, tb)

        @pl.when(k == nk - 1)
        def _():
            finish(acc_ref[...], add_ref, o_ref)

    a_spec = (pl.BlockSpec((tk, tm), lambda i, j, k: (k, i)) if ta
              else pl.BlockSpec((tm, tk), lambda i, j, k: (i, k)))
    b_spec = (pl.BlockSpec((tn, tk), lambda i, j, k: (j, k)) if tb
              else pl.BlockSpec((tk, tn), lambda i, j, k: (k, j)))
    in_specs = [a_spec, b_spec]
    operands = [a, b]
    if has_add:
        in_specs.append(pl.BlockSpec((tm, tn), lambda i, j, k: (i, j)))
        operands.append(add)
    if token is not None:
        in_specs.append(_TOKEN)
        operands.append(token)
    n_in = len(operands)
    return pl.pallas_call(
        body, name=name,
        out_shape=jax.ShapeDtypeStruct((m_dim, n_dim), out_dtype),
        grid=(m_dim // tm, n_dim // tn, nk),
        in_specs=in_specs,
        out_specs=pl.BlockSpec((tm, tn), lambda i, j, k: (i, j)),
        scratch_shapes=[pltpu.VMEM((tm, tn), F32)] if nk > 1 else [],
        compiler_params=_params(3),
    )(*operands)


def _to_bf16(arrays, name, token=None):
    n = len(arrays)
    extra = [] if token is None else [token]

    def body(*refs):
        for src, dst in zip(refs[:n], refs[n + len(extra):]):
            dst[...] = src[...].astype(BF16)

    vmem = pl.BlockSpec(memory_space=pltpu.VMEM)
    return pl.pallas_call(
        body, name=name, out_shape=tuple(jax.ShapeDtypeStruct(a.shape, BF16) for a in arrays),
        in_specs=[vmem] * (n + len(extra)), out_specs=tuple([vmem] * n),
        compiler_params=pltpu.CompilerParams(vmem_limit_bytes=VMEM_LIMIT_BYTES))(*arrays, *extra)


def _rows(tr, w, cb=0):
    return pl.BlockSpec((tr, w), lambda i: (i, cb))


def _whole(shape):
    nd = len(shape)
    return pl.BlockSpec(tuple(shape), lambda i: (0,) * nd)


def _ffn_in(x16, w_in, name, token=None):
    t, d = x16.shape
    _, nb, _, fb = w_in.shape
    tm = _pick(t, 2048, 8)
    extra = [] if token is None else [token]

    def body(*refs):
        x_ref, wg_ref, wu_ref = refs[:3]
        h_ref, a_ref = refs[-2:]
        xv = x_ref[...]
        g = _dot(xv, wg_ref[0, 0])
        u = _dot(xv, wu_ref[0, 0])
        h_ref[0, 0] = g.astype(BF16)
        h_ref[0, 1] = u.astype(BF16)
        a_ref[0] = (g * _sigmoid(g) * u).astype(BF16)

    return pl.pallas_call(
        body, name=name,
        out_shape=(jax.ShapeDtypeStruct((nb, 2, t, fb), BF16), jax.ShapeDtypeStruct((nb, t, fb), BF16)),
        grid=(t // tm, nb),
        in_specs=[pl.BlockSpec((tm, d), lambda i, j: (i, 0)),
                  pl.BlockSpec((1, 1, d, fb), lambda i, j: (0, j, 0, 0)),
                  pl.BlockSpec((1, 1, d, fb), lambda i, j: (1, j, 0, 0))] + [_TOKEN] * len(extra),
        out_specs=(pl.BlockSpec((1, 2, tm, fb), lambda i, j: (j, 0, i, 0)),
                   pl.BlockSpec((1, tm, fb), lambda i, j: (j, i, 0))),
        compiler_params=_params(2))(x16, w_in, w_in, *extra)


def _ffn_out_ln(a, w_out, xin, g, b, alpha, name, scale=0.5, token=None):
    nb, t, fb = a.shape
    d = w_out.shape[2]
    tm = _pick(t, 512, 8)
    extra = [] if token is None else [token]

    def body(*refs):
        a_ref, w_ref, x_ref, g_ref, b_ref = refs[:5]
        r_ref, y_ref, y16_ref = refs[-3:]
        f = _dot(a_ref[0], w_ref[0])
        for jb in range(1, nb):
            f = f + _dot(a_ref[jb], w_ref[jb])
        r = alpha * x_ref[...] + scale * f
        xhat, _ = _ln_stats(r)
        y = xhat * g_ref[...] + b_ref[...]
        r_ref[...] = r
        y_ref[...] = y
        y16_ref[...] = y.astype(BF16)

    return pl.pallas_call(
        body, name=name,
        out_shape=(jax.ShapeDtypeStruct((t, d), F32), jax.ShapeDtypeStruct((t, d), F32),
                   jax.ShapeDtypeStruct((t, d), BF16)),
        grid=(t // tm,),
        in_specs=[pl.BlockSpec((nb, tm, fb), lambda i: (0, i, 0)), _whole(w_out.shape), _rows(tm, d),
                  _whole((1, d)), _whole((1, d))] + [_TOKEN] * len(extra),
        out_specs=(_rows(tm, d), _rows(tm, d), _rows(tm, d)),
        compiler_params=_params())(a, w_out, xin, g, b, *extra)


def _ffn_out_dx(drs, w_out, h, name, token=None):
    nb, _, t, fb = h.shape
    d = w_out.shape[2]
    tm = _pick(t, 2048, 8)
    extra = [] if token is None else [token]

    def body(*refs):
        dr_ref, w_ref, h_ref, dh_ref = refs[0], refs[1], refs[2], refs[-1]
        da = _dot(dr_ref[...], w_ref[0], tb=True)
        g, u = h_ref[0, 0].astype(F32), h_ref[0, 1].astype(F32)
        sg = _sigmoid(g)
        silu = g * sg
        dh_ref[0, 0] = ((da * sg) * u * (1.0 + (g - silu))).astype(BF16)
        dh_ref[0, 1] = (da * silu).astype(BF16)

    return pl.pallas_call(
        body, name=name, out_shape=jax.ShapeDtypeStruct((nb, 2, t, fb), BF16), grid=(t // tm, nb),
        in_specs=[pl.BlockSpec((tm, d), lambda i, j: (i, 0)),
                  pl.BlockSpec((1, fb, d), lambda i, j: (j, 0, 0)),
                  pl.BlockSpec((1, 2, tm, fb), lambda i, j: (j, 0, i, 0))] + [_TOKEN] * len(extra),
        out_specs=pl.BlockSpec((1, 2, tm, fb), lambda i, j: (j, 0, i, 0)),
        compiler_params=_params(2))(drs, w_out, h, *extra)


def _ffn_out_dw(a, drs, name):
    nb, t, fb = a.shape
    d = drs.shape[1]

    def body(a_ref, dr_ref, o_ref):
        o_ref[0] = _dot(a_ref[0], dr_ref[...], ta=True).astype(BF16)

    return pl.pallas_call(
        body, name=name, out_shape=jax.ShapeDtypeStruct((nb, fb, d), BF16), grid=(nb,),
        in_specs=[pl.BlockSpec((1, t, fb), lambda j: (j, 0, 0)), pl.BlockSpec((t, d), lambda j: (0, 0))],
        out_specs=pl.BlockSpec((1, fb, d), lambda j: (j, 0, 0)),
        compiler_params=_params())(a, drs)


def _ffn_in_dw(x16, dh, name, token=None):
    t, d = x16.shape
    nb, _, _, fb = dh.shape
    extra = [] if token is None else [token]

    def body(*refs):
        x_ref, dh_ref, o_ref = refs[0], refs[1], refs[-1]
        o_ref[0, 0] = _dot(x_ref[...], dh_ref[0, 0], ta=True).astype(BF16)

    return pl.pallas_call(
        body, name=name, out_shape=jax.ShapeDtypeStruct((2, nb, d, fb), BF16), grid=(nb, 2),
        in_specs=[pl.BlockSpec((t, d), lambda j, s: (0, 0)),
                  pl.BlockSpec((1, 1, t, fb), lambda j, s: (j, s, 0, 0))] + [_TOKEN] * len(extra),
        out_specs=pl.BlockSpec((1, 1, d, fb), lambda j, s: (s, j, 0, 0)),
        compiler_params=_params(2))(x16, dh, *extra)


def _ffn_in_dx(dh, w_in, add, add_scale, name, token=None):
    nb, _, t, fb = dh.shape
    d = w_in.shape[2]
    tm = _pick(t, 512, 8)
    has_add = add is not None
    extra = [] if token is None else [token]

    def body(*refs):
        dh_ref, w_ref = refs[:2]
        o_ref = refs[-1]
        acc = None
        for jb in range(nb):
            for s in range(2):
                part = _dot(dh_ref[jb, s], w_ref[s, jb], tb=True)
                acc = part if acc is None else acc + part
        if has_add:
            acc = acc + add_scale * refs[2][...]
        o_ref[...] = acc

    return pl.pallas_call(
        body, name=name, out_shape=jax.ShapeDtypeStruct((t, d), F32), grid=(t // tm,),
        in_specs=[pl.BlockSpec((nb, 2, tm, fb), lambda i: (0, 0, i, 0)), _whole(w_in.shape)]
        + ([_rows(tm, d)] if has_add else []) + [_TOKEN] * len(extra),
        out_specs=_rows(tm, d),
        compiler_params=_params())(*([dh, w_in] + ([add] if has_add else []) + extra))


def _ffn_in_dx_ln(dh, w_in, r, dy_b, b_scale, ln_g, out_scale, name, token=None):
    nb, _, t, fb = dh.shape
    d = w_in.shape[2]
    tm = _pick(t, 512, 8)
    extra = [] if token is None else [token]

    def body(*refs):
        dh_ref, w_ref, r_ref, dyb_ref, g_ref = refs[:5]
        dr_ref, drs_ref, dg_ref, dbeta_ref = refs[-4:]
        dy = b_scale * dyb_ref[...]
        for jb in range(nb):
            for s in range(2):
                dy = dy + _dot(dh_ref[jb, s], w_ref[s, jb], tb=True)
        xhat, rstd = _ln_stats(r_ref[...])
        dr = _ln_bwd(dy, xhat, rstd, g_ref[...])
        dr_ref[...] = dr
        drs_ref[...] = (out_scale * dr).astype(BF16)

        @pl.when(pl.program_id(0) == 0)
        def _():
            dg_ref[...] = jnp.zeros_like(dg_ref)
            dbeta_ref[...] = jnp.zeros_like(dbeta_ref)

        dg_ref[...] += _fold8(dy * xhat)
        dbeta_ref[...] += _fold8(dy)

    return pl.pallas_call(
        body, name=name,
        out_shape=(jax.ShapeDtypeStruct((t, d), F32), jax.ShapeDtypeStruct((t, d), BF16),
                   jax.ShapeDtypeStruct((8, d), F32), jax.ShapeDtypeStruct((8, d), F32)),
        grid=(t // tm,),
        in_specs=[pl.BlockSpec((nb, 2, tm, fb), lambda i: (0, 0, i, 0)), _whole(w_in.shape), _rows(tm, d),
                  _rows(tm, d), _whole((1, d))] + [_TOKEN] * len(extra),
        out_specs=(_rows(tm, d), _rows(tm, d), _whole((8, d)), _whole((8, d))),
        compiler_params=_params())(dh, w_in, r, dy_b, ln_g, *extra)


def _mm_ln_bwd(a, w, r, dy_b, b_scale, ln_g, out_scale, name, token=None):
    t, k_dim = a.shape
    d = w.shape[0]
    tm = _pick(t, 512, 8)
    extra = [] if token is None else [token]

    def body(*refs):
        a_ref, w_ref, r_ref, dyb_ref, g_ref = refs[:5]
        dr_ref, drs_ref, dg_ref, dbeta_ref = refs[-4:]
        dy = _dot(a_ref[...], w_ref[...], tb=True) + b_scale * dyb_ref[...]
        xhat, rstd = _ln_stats(r_ref[...])
        dr = _ln_bwd(dy, xhat, rstd, g_ref[...])
        dr_ref[...] = dr
        drs_ref[...] = (out_scale * dr).astype(BF16)

        @pl.when(pl.program_id(0) == 0)
        def _():
            dg_ref[...] = jnp.zeros_like(dg_ref)
            dbeta_ref[...] = jnp.zeros_like(dbeta_ref)

        dg_ref[...] += _fold8(dy * xhat)
        dbeta_ref[...] += _fold8(dy)

    return pl.pallas_call(
        body, name=name,
        out_shape=(jax.ShapeDtypeStruct((t, d), F32), jax.ShapeDtypeStruct((t, d), BF16),
                   jax.ShapeDtypeStruct((8, d), F32), jax.ShapeDtypeStruct((8, d), F32)),
        grid=(t // tm,),
        in_specs=[_rows(tm, k_dim), _whole(w.shape), _rows(tm, d), _rows(tm, d), _whole((1, d))]
        + [_TOKEN] * len(extra),
        out_specs=(_rows(tm, d), _rows(tm, d), _whole((8, d)), _whole((8, d))),
        compiler_params=_params())(a, w, r, dy_b, ln_g, *extra)


def _take_row(v, row_id, s):
    return jnp.sum(jnp.where(row_id == s, v, 0.0), axis=0, keepdims=True)


def _complex_power(ar, ai, n):
    out = None
    while n:
        if n & 1:
            out = (ar, ai) if out is None else (out[0] * ar - out[1] * ai, out[0] * ai + out[1] * ar)
        ar, ai = ar * ar - ai * ai, 2.0 * ar * ai
        n >>= 1
    return out


def _seg_carries(f_re, f_im, p_re, p_im, reverse):
    row_id = lax.broadcasted_iota(jnp.int32, f_re.shape, 0)
    p_re, p_im = _take_row(p_re, row_id, 0), _take_row(p_im, row_id, 0)
    out_re, out_im = jnp.zeros_like(f_re), jnp.zeros_like(f_im)
    c_re, c_im = jnp.zeros_like(p_re), jnp.zeros_like(p_im)
    order = range(N_SEG - 1, -1, -1) if reverse else range(N_SEG)
    for s in order:
        out_re = jnp.where(row_id == s, c_re, out_re)
        out_im = jnp.where(row_id == s, c_im, out_im)
        fr, fi = _take_row(f_re, row_id, s), _take_row(f_im, row_id, s)
        c_re, c_im = fr + p_re * c_re - p_im * c_im, fi + p_re * c_im + p_im * c_re
    return out_re, out_im


def _s5_fwd(za16, b_re, b_im, a_re, a_im, c_re, c_im, n_seq, name):
    t = za16.shape[0]
    n_sg, ch1, st1 = b_re.shape
    per = S5_FWD_GROUPS if n_sg % S5_FWD_GROUPS == 0 else 1
    ch, st = per * ch1, per * st1
    seq = t // n_seq
    steps = seq // N_SEG

    def body(za_ref, bre_ref, bim_ref, are, aim, cre_ref, cim_ref, hre, him, y_ref):
        for g in range(per):
            za = za_ref[:, g * ch1:(g + 1) * ch1]
            hre[:, g * st1:(g + 1) * st1] = _dot(za, bre_ref[g])
            him[:, g * st1:(g + 1) * st1] = _dot(za, bim_ref[g])
        ar = jnp.broadcast_to(are[...], (N_SEG, st))
        ai = jnp.broadcast_to(aim[...], (N_SEG, st))
        zero = jnp.zeros((N_SEG, st), F32)
        p_re, p_im = _complex_power(ar, ai, steps)

        def local(k, carry):
            lr, li = carry
            rows = pl.ds(pl.multiple_of(k * N_SEG, N_SEG), N_SEG)
            nr = ar * lr - ai * li + hre[rows, :]
            ni = ar * li + ai * lr + him[rows, :]
            hre[rows, :] = nr
            him[rows, :] = ni
            return nr, ni

        f_re, f_im = lax.fori_loop(0, steps, local, (zero, zero))
        c_re, c_im = _seg_carries(f_re, f_im, p_re, p_im, reverse=False)

        def fix(k, carry):
            qr, qi = carry
            rows = pl.ds(pl.multiple_of(k * N_SEG, N_SEG), N_SEG)
            hre[rows, :] = hre[rows, :] + qr
            him[rows, :] = him[rows, :] + qi
            return ar * qr - ai * qi, ar * qi + ai * qr

        lax.fori_loop(0, steps, fix, (ar * c_re - ai * c_im, ar * c_im + ai * c_re))
        for g in range(per):
            states = slice(g * st1, (g + 1) * st1)
            y_ref[:, g * ch1:(g + 1) * ch1] = _dot(hre[:, states], cre_ref[g]) + _dot(him[:, states], cim_ref[g])

    rows_ch = pl.BlockSpec((seq, ch), lambda s, j: (s, j))
    rows_st = pl.BlockSpec((seq, st), lambda s, j: (s, j))
    vec = pl.BlockSpec((1, st), lambda s, j: (0, j))
    b_blk = pl.BlockSpec((per, ch1, st1), lambda s, j: (j, 0, 0))
    c_blk = pl.BlockSpec((per, st1, ch1), lambda s, j: (j, 0, 0))
    return pl.pallas_call(
        body, name=name,
        out_shape=(jax.ShapeDtypeStruct((t, n_sg * st1), F32), jax.ShapeDtypeStruct((t, n_sg * st1), F32),
                   jax.ShapeDtypeStruct((t, n_sg * ch1), F32)),
        grid=(n_seq, n_sg // per), in_specs=[rows_ch, b_blk, b_blk, vec, vec, c_blk, c_blk],
        out_specs=(rows_st, rows_st, rows_ch),
        compiler_params=_params(2))(za16, b_re, b_im, a_re, a_im, c_re, c_im)


def _s5_bwd(dy16, dza_skip, h_re, h_im, za16, b_re, b_im, a_re, a_im, c_re, c_im, n_seq, name):
    t = dy16.shape[0]
    n_sg, ch, st = b_re.shape
    seq = t // n_seq
    steps = seq // N_SEG

    def body(dy_ref, skip_ref, hre, him, za_ref, bre_ref, bim_ref, are, aim, cre_ref, cim_ref,
             dza_ref, dbre_ref, dbim_ref, dcre_ref, dcim_ref, dare, daim, lre, lim):
        dy = dy_ref[...]
        lre[...] = _dot(dy, cre_ref[0], tb=True)
        lim[...] = _dot(dy, cim_ref[0], tb=True)
        ar = jnp.broadcast_to(are[...], (N_SEG, st))
        ai = -jnp.broadcast_to(aim[...], (N_SEG, st))
        zero = jnp.zeros((N_SEG, st), F32)
        p_re, p_im = _complex_power(ar, ai, steps)

        def local(i, carry):
            lr, li = carry
            k = steps - 1 - i
            rows = pl.ds(pl.multiple_of(k * N_SEG, N_SEG), N_SEG)
            nr = ar * lr - ai * li + lre[rows, :]
            ni = ar * li + ai * lr + lim[rows, :]
            lre[rows, :] = nr
            lim[rows, :] = ni
            return nr, ni

        f_re, f_im = lax.fori_loop(0, steps, local, (zero, zero))
        c_re, c_im = _seg_carries(f_re, f_im, p_re, p_im, reverse=True)

        def accumulate(lam_r, lam_i, hp_r, hp_i, acc_r, acc_i):
            return acc_r + (lam_r * hp_r + lam_i * hp_i), acc_i + (lam_i * hp_r - lam_r * hp_i)

        def fix(i, carry):
            qr, qi, acc_r, acc_i = carry
            k = steps - 1 - i
            rows = pl.ds(pl.multiple_of(k * N_SEG, N_SEG), N_SEG)
            prev = pl.ds(pl.multiple_of((k - 1) * N_SEG, N_SEG), N_SEG)
            lam_r = lre[rows, :] + qr
            lam_i = lim[rows, :] + qi
            lre[rows, :] = lam_r
            lim[rows, :] = lam_i
            acc_r, acc_i = accumulate(lam_r, lam_i, hre[prev, :], him[prev, :], acc_r, acc_i)
            return ar * qr - ai * qi, ar * qi + ai * qr, acc_r, acc_i

        qr, qi, acc_r, acc_i = lax.fori_loop(
            0, steps - 1, fix, (ar * c_re - ai * c_im, ar * c_im + ai * c_re, zero, zero))
        first = pl.ds(0, N_SEG)
        last = pl.ds((steps - 1) * N_SEG, N_SEG)
        lam_r = lre[first, :] + qr
        lam_i = lim[first, :] + qi
        lre[first, :] = lam_r
        lim[first, :] = lam_i
        row_id = lax.broadcasted_iota(jnp.int32, (N_SEG, st), 0)
        hp_r = jnp.where(row_id == 0, 0.0, pltpu.roll(hre[last, :], 1, 0))
        hp_i = jnp.where(row_id == 0, 0.0, pltpu.roll(him[last, :], 1, 0))
        acc_r, acc_i = accumulate(lam_r, lam_i, hp_r, hp_i, acc_r, acc_i)

        lam_re16 = lre[...].astype(BF16)
        lam_im16 = lim[...].astype(BF16)
        dza_ref[...] = (_dot(lam_re16, bre_ref[0], tb=True) + _dot(lam_im16, bim_ref[0], tb=True)
                        + skip_ref[...]).astype(BF16)

        @pl.when(pl.program_id(1) == 0)
        def _():
            for ref in (dbre_ref, dbim_ref, dcre_ref, dcim_ref, dare, daim):
                ref[...] = jnp.zeros_like(ref)

        za = za_ref[...]
        dbre_ref[0] += _dot(za, lam_re16, ta=True)
        dbim_ref[0] += _dot(za, lam_im16, ta=True)
        dcre_ref[0] += _dot(hre[...], dy, ta=True)
        dcim_ref[0] += _dot(him[...], dy, ta=True)
        dare[...] += acc_r
        daim[...] += acc_i

    rows_ch = pl.BlockSpec((seq, ch), lambda j, s: (s, j))
    rows_st = pl.BlockSpec((seq, st), lambda j, s: (s, j))
    vec = pl.BlockSpec((1, st), lambda j, s: (0, j))
    b_blk = pl.BlockSpec((1, ch, st), lambda j, s: (j, 0, 0))
    c_blk = pl.BlockSpec((1, st, ch), lambda j, s: (j, 0, 0))
    acc = pl.BlockSpec((N_SEG, st), lambda j, s: (0, j))
    return pl.pallas_call(
        body, name=name,
        out_shape=(jax.ShapeDtypeStruct((t, n_sg * ch), BF16),
                   jax.ShapeDtypeStruct((n_sg, ch, st), F32), jax.ShapeDtypeStruct((n_sg, ch, st), F32),
                   jax.ShapeDtypeStruct((n_sg, st, ch), F32), jax.ShapeDtypeStruct((n_sg, st, ch), F32),
                   jax.ShapeDtypeStruct((N_SEG, n_sg * st), F32), jax.ShapeDtypeStruct((N_SEG, n_sg * st), F32)),
        grid=(n_sg, n_seq),
        in_specs=[rows_ch, rows_ch, rows_st, rows_st, rows_ch, b_blk, b_blk, vec, vec, c_blk, c_blk],
        out_specs=(rows_ch, b_blk, b_blk, c_blk, c_blk, acc, acc),
        scratch_shapes=[pltpu.VMEM((seq, st), F32), pltpu.VMEM((seq, st), F32)],
        compiler_params=_params(2))(dy16, dza_skip, h_re, h_im, za16, b_re, b_im, a_re, a_im, c_re, c_im)


def _s5_post_fwd(yssm, u, d_skip, glu_w, glu_b, n_seq, name):
    t, w = yssm.shape
    steps = t // n_seq // N_SEG
    ts = _pick(steps, 128, 8)
    nj = steps // ts

    nc = w // LANES

    def body(*refs):
        y_refs, u_refs = refs[:nc], refs[nc:2 * nc]
        d_ref, w_ref, b_ref, o_ref = refs[2 * nc:]
        for seg in range(N_SEG):
            rows = pl.ds(seg, ts, stride=N_SEG)
            y = jnp.concatenate([r[rows, :] for r in y_refs], axis=1)
            uu = jnp.concatenate([r[rows, :] for r in u_refs], axis=1)
            yg = _gelu(y + d_ref[...] * uu)
            pre = _dot(yg, w_ref[...]) + b_ref[...]
            o_ref[seg] = yg * _sigmoid(pre)

    scan_rows = [pl.BlockSpec((ts * N_SEG, LANES), lambda s, j, c=c: (s * nj + j, c)) for c in range(nc)]
    out = pl.pallas_call(
        body, name=name, out_shape=jax.ShapeDtypeStruct((n_seq * N_SEG, steps, w), F32), grid=(n_seq, nj),
        in_specs=scan_rows + scan_rows + [pl.BlockSpec((1, w), lambda s, j: (0, 0)),
                                          pl.BlockSpec((w, w), lambda s, j: (0, 0)),
                                          pl.BlockSpec((1, w), lambda s, j: (0, 0))],
        out_specs=pl.BlockSpec((N_SEG, ts, w), lambda s, j: (s, j, 0)),
        compiler_params=_params(2))(*([yssm] * nc), *([u] * nc), d_skip, glu_w, glu_b)
    return out.reshape(t, w)


def _s5_post_bwd(yssm, u, dout, d_skip, glu_w, glu_b, name):
    t, w = yssm.shape
    tr = _pick(t, 512, 8)

    def body(y_ref, u_ref, do_ref, d_ref, w_ref, b_ref, dy_ref, du_ref, dw_ref, dd_ref, db_ref):
        uu = u_ref[...]
        y = y_ref[...] + d_ref[...] * uu
        yg = _gelu(y)
        sg = _sigmoid(_dot(yg, w_ref[...]) + b_ref[...])
        do = do_ref[...]
        dpre = do * yg * sg * (1.0 - sg)
        dyg = do * sg + _dot(dpre, w_ref[...], tb=True)
        dy = dyg * _gelu_grad(y)
        dy_ref[...] = dy.astype(BF16)
        du_ref[...] = dy * d_ref[...]

        @pl.when(pl.program_id(0) == 0)
        def _():
            dw_ref[...] = jnp.zeros_like(dw_ref)
            dd_ref[...] = jnp.zeros_like(dd_ref)
            db_ref[...] = jnp.zeros_like(db_ref)

        dw_ref[...] += _dot(yg, dpre, ta=True)
        dd_ref[...] += _fold8(dy * uu)
        db_ref[...] += _fold8(dpre)

    return pl.pallas_call(
        body, name=name,
        out_shape=(jax.ShapeDtypeStruct((t, w), BF16), jax.ShapeDtypeStruct((t, w), F32),
                   jax.ShapeDtypeStruct((w, w), F32), jax.ShapeDtypeStruct((8, w), F32),
                   jax.ShapeDtypeStruct((8, w), F32)),
        grid=(t // tr,),
        in_specs=[_rows(tr, w), _rows(tr, w), _rows(tr, w), _whole((1, w)), _whole((w, w)),
                  _whole((1, w))],
        out_specs=(_rows(tr, w), _rows(tr, w), _whole((w, w)), _whole((8, w)), _whole((8, w))),
        compiler_params=_params())(yssm, u, dout, d_skip, glu_w, glu_b)


def _head_select(parts, head_dim):
    col_head = lax.broadcasted_iota(jnp.int32, parts[0].shape, 1) // head_dim
    out = jnp.zeros_like(parts[0])
    for h, part in enumerate(parts):
        out = jnp.where(col_head == h, part, out)
    return out


def _gmlp_fwd(proj, ln_g, ln_b, ws, bs_cols, name):
    t = proj.shape[0]
    n_heads = ws.shape[0]
    w = bs_cols.shape[1]
    head_dim = w // n_heads
    cpb = _pick(t // CHUNK, 4, 1)
    tr = cpb * CHUNK

    def body(zu_ref, zv_ref, g_ref, b_ref, ws_ref, bs_ref, o_ref):
        for c in range(cpb):
            rows = pl.ds(c * CHUNK, CHUNK)
            xhat, _ = _ln_stats(_gelu(zv_ref[rows, :]))
            vn = (xhat * g_ref[...] + b_ref[...]).astype(BF16)
            s = _head_select([_dot(ws_ref[h], vn) for h in range(n_heads)], head_dim) + bs_ref[...]
            o_ref[rows, :] = _gelu(zu_ref[rows, :]) * s

    return pl.pallas_call(
        body, name=name, out_shape=jax.ShapeDtypeStruct((t, w), F32), grid=(t // tr,),
        in_specs=[_rows(tr, w, 1), _rows(tr, w, 2), _whole((1, w)), _whole((1, w)),
                  _whole(ws.shape), _whole(bs_cols.shape)],
        out_specs=_rows(tr, w), compiler_params=_params())(proj, proj, ln_g, ln_b, ws, bs_cols)


def _gmlp_bwd(proj, dout, ln_g, ln_b, ws, ws_t, bs_cols, name):
    t = proj.shape[0]
    n_heads = ws.shape[0]
    w = bs_cols.shape[1]
    head_dim = w // n_heads
    cpb = _pick(t // CHUNK, 4, 1)
    tr = cpb * CHUNK

    def body(zu_ref, zv_ref, do_ref, g_ref, b_ref, ws_ref, wst_ref, bs_ref,
             dzu_ref, dzv_ref, dws_ref, dbs_ref, dg_ref, dbeta_ref):
        @pl.when(pl.program_id(0) == 0)
        def _():
            dws_ref[...] = jnp.zeros_like(dws_ref)
            dbs_ref[...] = jnp.zeros_like(dbs_ref)
            dg_ref[...] = jnp.zeros_like(dg_ref)
            dbeta_ref[...] = jnp.zeros_like(dbeta_ref)

        col_head = lax.broadcasted_iota(jnp.int32, (CHUNK, w), 1) // head_dim
        for c in range(cpb):
            rows = pl.ds(c * CHUNK, CHUNK)
            zu, zv, do = zu_ref[rows, :], zv_ref[rows, :], do_ref[rows, :]
            xhat, rstd = _ln_stats(_gelu(zv))
            vn = (xhat * g_ref[...] + b_ref[...]).astype(BF16)
            s = _head_select([_dot(ws_ref[h], vn) for h in range(n_heads)], head_dim) + bs_ref[...]
            dzu_ref[rows, :] = (do * s * _gelu_grad(zu)).astype(BF16)
            ds = do * _gelu(zu)
            ds16 = ds.astype(BF16)
            dbs_ref[...] += ds
            for h in range(n_heads):
                dws_ref[h] += _dot(jnp.where(col_head == h, ds16, jnp.zeros_like(ds16)), vn, tb=True)
            dvn = _head_select([_dot(wst_ref[h], ds16) for h in range(n_heads)], head_dim)
            dg_ref[...] += _fold8(dvn * xhat)
            dbeta_ref[...] += _fold8(dvn)
            dgv = _ln_bwd(dvn, xhat, rstd, g_ref[...])
            dzv_ref[rows, :] = (dgv * _gelu_grad(zv)).astype(BF16)

    return pl.pallas_call(
        body, name=name,
        out_shape=(jax.ShapeDtypeStruct((t, w), BF16), jax.ShapeDtypeStruct((t, w), BF16),
                   jax.ShapeDtypeStruct(ws.shape, F32), jax.ShapeDtypeStruct((CHUNK, w), F32),
                   jax.ShapeDtypeStruct((8, w), F32), jax.ShapeDtypeStruct((8, w), F32)),
        grid=(t // tr,),
        in_specs=[_rows(tr, w, 1), _rows(tr, w, 2), _rows(tr, w), _whole((1, w)), _whole((1, w)),
                  _whole(ws.shape), _whole(ws.shape), _whole(bs_cols.shape)],
        out_specs=(_rows(tr, w), _rows(tr, w), _whole(ws.shape), _whole((CHUNK, w)),
                   _whole((8, w)), _whole((8, w))),
        compiler_params=_params())(proj, proj, dout, ln_g, ln_b, ws, ws_t, bs_cols)


def _merge_fwd(s5out, gm, proj, up_a, up_b, name):
    t, w = s5out.shape
    d = up_a.shape[1]
    assert d == 2 * w
    tr = _pick(t, 512, 8)

    def body(s_ref, gm_ref, ga0, ga1, gb0, gb1, ua_ref, ub_ref, m_ref, ya_ref, yb_ref):
        ya = _dot(s_ref[...], ua_ref[...])
        yb = _dot(gm_ref[...], ub_ref[...])
        ya_ref[...] = ya.astype(BF16)
        yb_ref[...] = yb.astype(BF16)
        for half, (ga, gb) in enumerate(((ga0, gb0), (ga1, gb1))):
            cols = slice(half * w, (half + 1) * w)
            m_ref[:, cols] = (_sigmoid(ga[...]) * ya[:, cols] + _sigmoid(gb[...]) * yb[:, cols]).astype(BF16)

    return pl.pallas_call(
        body, name=name,
        out_shape=(jax.ShapeDtypeStruct((t, d), BF16), jax.ShapeDtypeStruct((t, d), BF16),
                   jax.ShapeDtypeStruct((t, d), BF16)),
        grid=(t // tr,),
        in_specs=[_rows(tr, w), _rows(tr, w), _rows(tr, w, 3), _rows(tr, w, 4), _rows(tr, w, 5),
                  _rows(tr, w, 6), _whole(up_a.shape), _whole(up_b.shape)],
        out_specs=(_rows(tr, d), _rows(tr, d), _rows(tr, d)),
        compiler_params=_params())(s5out, gm, proj, proj, proj, proj, up_a, up_b)


def _merge_bwd(dm, ya, yb, s5out, gm, proj, up_a, up_b, name):
    t, d = dm.shape
    w = d // 2
    tr = _pick(t, 512, 8)

    def body(dm_ref, ya_ref, yb_ref, s_ref, gm_ref, ga0, ga1, gb0, gb1, ua_ref, ub_ref,
             dga_ref, dgb_ref, ds_ref, dgm_ref, dua_ref, dub_ref):
        dm_v, ya, yb = dm_ref[...], ya_ref[...].astype(F32), yb_ref[...].astype(F32)
        dya, dyb = [], []
        for half, (ga, gb) in enumerate(((ga0, gb0), (ga1, gb1))):
            cols = slice(half * w, (half + 1) * w)
            sa, sb = _sigmoid(ga[...]), _sigmoid(gb[...])
            dmh = dm_v[:, cols]
            dga_ref[:, cols] = (dmh * ya[:, cols] * sa * (1.0 - sa)).astype(BF16)
            dgb_ref[:, cols] = (dmh * yb[:, cols] * sb * (1.0 - sb)).astype(BF16)
            dya.append((dmh * sa).astype(BF16))
            dyb.append((dmh * sb).astype(BF16))
        dya = jnp.concatenate(dya, axis=1)
        dyb = jnp.concatenate(dyb, axis=1)
        ds_ref[...] = _dot(dya, ua_ref[...], tb=True)
        dgm_ref[...] = _dot(dyb, ub_ref[...], tb=True)

        @pl.when(pl.program_id(0) == 0)
        def _():
            dua_ref[...] = jnp.zeros_like(dua_ref)
            dub_ref[...] = jnp.zeros_like(dub_ref)

        dua_ref[...] += _dot(s_ref[...], dya, ta=True)
        dub_ref[...] += _dot(gm_ref[...], dyb, ta=True)

    return pl.pallas_call(
        body, name=name,
        out_shape=(jax.ShapeDtypeStruct((t, d), BF16), jax.ShapeDtypeStruct((t, d), BF16),
                   jax.ShapeDtypeStruct((t, w), F32), jax.ShapeDtypeStruct((t, w), F32),
                   jax.ShapeDtypeStruct(up_a.shape, F32), jax.ShapeDtypeStruct(up_b.shape, F32)),
        grid=(t // tr,),
        in_specs=[_rows(tr, d), _rows(tr, d), _rows(tr, d), _rows(tr, w), _rows(tr, w),
                  _rows(tr, w, 3), _rows(tr, w, 4), _rows(tr, w, 5), _rows(tr, w, 6),
                  _whole(up_a.shape), _whole(up_b.shape)],
        out_specs=(_rows(tr, d), _rows(tr, d), _rows(tr, w), _rows(tr, w), _whole(up_a.shape),
                   _whole(up_b.shape)),
        compiler_params=_params())(dm, ya, yb, s5out, gm, proj, proj, proj, proj, up_a, up_b)


def _head(x3, r3, ln_g, p, target, w_gate, w_proj, out_scale, name):
    t, d = x3.shape
    pd = p.shape[1]
    tr = _pick(t, 512, 8)
    nb = t // tr

    def body(x_ref, r_ref, g_ref, p_ref, t_ref, wg_ref, wp_ref,
             loss_ref, dr_ref, drs_ref, dwg_ref, dwp_ref, dg_ref, dbeta_ref):
        xv = x_ref[...]
        sg = _sigmoid(_dot(xv, wg_ref[...]))
        pp = _dot(p_ref[...], wp_ref[...])
        err = xv + sg * pp - t_ref[...]
        loss_ref[...] = jnp.full((8, 128), 0.5 * jnp.sum(jnp.mean(err * err, axis=-1)), F32)
        dout = err * (1.0 / d)
        dgpre = dout * pp * sg * (1.0 - sg)
        dpp = dout * sg
        dx = dout + _dot(dgpre, wg_ref[...], tb=True)
        xhat, rstd = _ln_stats(r_ref[...])
        dr = _ln_bwd(dx, xhat, rstd, g_ref[...])
        dr_ref[...] = dr
        drs_ref[...] = (out_scale * dr).astype(BF16)

        @pl.when(pl.program_id(0) == 0)
        def _():
            for ref in (dwg_ref, dwp_ref, dg_ref, dbeta_ref):
                ref[...] = jnp.zeros_like(ref)

        dwg_ref[...] += _dot(xv, dgpre, ta=True)
        dwp_ref[...] += _dot(p_ref[...], dpp, ta=True)
        dg_ref[...] += _fold8(dx * xhat)
        dbeta_ref[...] += _fold8(dx)

    return pl.pallas_call(
        body, name=name,
        out_shape=(jax.ShapeDtypeStruct((nb * 8, 128), F32), jax.ShapeDtypeStruct((t, d), F32),
                   jax.ShapeDtypeStruct((t, d), BF16), jax.ShapeDtypeStruct((d, d), F32),
                   jax.ShapeDtypeStruct((pd, d), F32), jax.ShapeDtypeStruct((8, d), F32),
                   jax.ShapeDtypeStruct((8, d), F32)),
        grid=(nb,),
        in_specs=[_rows(tr, d), _rows(tr, d), _whole((1, d)), _rows(tr, pd), _rows(tr, d), _whole((d, d)),
                  _whole((pd, d))],
        out_specs=(pl.BlockSpec((8, 128), lambda i: (i, 0)), _rows(tr, d), _rows(tr, d), _whole((d, d)),
                   _whole((pd, d)), _whole((8, d)), _whole((8, d))),
        compiler_params=_params())(x3, r3, ln_g, p, target, w_gate, w_proj)


_HBM = pl.BlockSpec(memory_space=pltpu.HBM)


_SEM = pl.BlockSpec(memory_space=pltpu.SEMAPHORE)
_ANY = pl.BlockSpec(memory_space=pl.ANY)
_DATAFLOW = pltpu.SideEffectType.DATAFLOW_SIDE_EFFECTING


def _peer(k, x, y, c):
    return (1 - x if k & 4 else x, 1 - y if k & 2 else y, 1 - c if k & 1 else c)


def _exchange_start(sends, after, name):
    n = len(sends)
    lands = [lax.empty((N_DEV,) + s.shape[1:], s.dtype) for s in sends]

    def body(*refs):
        s_refs, l_refs = refs[:n], refs[n:2 * n]
        send_sems, recv_sems, local_sems, token = refs[2 * n + 1], refs[2 * n + 2], refs[2 * n + 3], refs[-1]
        x, y, c = lax.axis_index("x"), lax.axis_index("y"), lax.axis_index("c")
        me = 4 * x + 2 * y + c
        for a in range(n):
            same = sends[a].shape[0] == 1
            pltpu.make_async_copy(s_refs[a].at[0 if same else me], l_refs[a].at[me], local_sems.at[a]).start()
            for k in range(1, N_DEV):
                px, py, pc = _peer(k, x, y, c)
                pltpu.make_async_remote_copy(
                    src_ref=s_refs[a].at[0 if same else 4 * px + 2 * py + pc], dst_ref=l_refs[a].at[me],
                    send_sem=send_sems.at[7 * a + k - 1], recv_sem=recv_sems.at[7 * a + k - 1],
                    device_id=(px, py, pc), device_id_type=pl.DeviceIdType.MESH).start()
        token[...] = jnp.zeros_like(token)

    outs = pl.pallas_call(
        body, name=name,
        out_shape=(pltpu.SemaphoreType.DMA((7 * n,)), pltpu.SemaphoreType.DMA((7 * n,)),
                   pltpu.SemaphoreType.DMA((n,)),
                   *[pltpu.HBM(s.shape, s.dtype) for s in sends],
                   *[pltpu.HBM(l.shape, l.dtype) for l in lands],
                   jax.ShapeDtypeStruct((8, 128), F32)),
        in_specs=[_HBM] * (2 * n) + [_ANY],
        out_specs=(_SEM, _SEM, _SEM, *([_HBM] * (2 * n)), pl.BlockSpec(memory_space=pltpu.VMEM)),
        input_output_aliases={i: 3 + i for i in range(2 * n)},
        compiler_params=pltpu.CompilerParams(has_side_effects=_DATAFLOW),
    )(*[pltpu.with_memory_space_constraint(v, pltpu.HBM) for v in list(sends) + lands], after)
    return (outs[0], outs[1], outs[2], list(outs[3:3 + n]), list(outs[3 + n:3 + 2 * n])), outs[-1]


def _exchange_wait(handle, after, name):
    send_sems, recv_sems, local_sems, sends, lands = handle
    n = len(sends)

    def body(*refs):
        s_refs, l_refs = refs[:n], refs[n:2 * n]
        send_sems, recv_sems, local_sems = refs[2 * n], refs[2 * n + 1], refs[2 * n + 2]
        x, y, c = lax.axis_index("x"), lax.axis_index("y"), lax.axis_index("c")
        for a in range(n):
            pltpu.make_async_copy(s_refs[a].at[0], l_refs[a].at[0], local_sems.at[a]).wait()
            for k in range(1, N_DEV):
                copy = pltpu.make_async_remote_copy(
                    src_ref=s_refs[a].at[0], dst_ref=l_refs[a].at[0],
                    send_sem=send_sems.at[7 * a + k - 1], recv_sem=recv_sems.at[7 * a + k - 1],
                    device_id=_peer(k, x, y, c), device_id_type=pl.DeviceIdType.MESH)
                copy.wait_send()
                copy.wait_recv()

    outs = pl.pallas_call(
        body, name=name,
        out_shape=(*[pltpu.HBM(s.shape, s.dtype) for s in sends], *[pltpu.HBM(l.shape, l.dtype) for l in lands]),
        in_specs=[_HBM] * (2 * n) + [_SEM, _SEM, _SEM] + [_ANY] * len(after),
        out_specs=tuple([_HBM] * (2 * n)),
        input_output_aliases={i: i for i in range(2 * n)},
        compiler_params=pltpu.CompilerParams(has_side_effects=_DATAFLOW),
    )(*sends, *lands, send_sems, recv_sems, local_sems, *after)
    return list(outs[n:])


def _chips_of(x, y):
    return [(1 - x, y), (x, 1 - y), (1 - x, 1 - y)]


def _gather2_start(shards, after, name):
    n = len(shards)
    lands = [lax.empty((N_DEV,) + s.shape, s.dtype) for s in shards]

    def body(*refs):
        x_refs, l_refs = refs[:n], refs[n:2 * n]
        send_sems, recv_sems, local_sems, token = refs[2 * n + 1], refs[2 * n + 2], refs[2 * n + 3], refs[-1]
        x, y, c = lax.axis_index("x"), lax.axis_index("y"), lax.axis_index("c")
        me = 4 * x + 2 * y + c
        peers = [(x, y, 1 - c)] + [(*chip, c) for chip in _chips_of(x, y)]
        for a in range(n):
            pltpu.make_async_copy(x_refs[a], l_refs[a].at[me], local_sems.at[a]).start()
            for k, peer in enumerate(peers):
                pltpu.make_async_remote_copy(
                    src_ref=x_refs[a], dst_ref=l_refs[a].at[me],
                    send_sem=send_sems.at[4 * a + k], recv_sem=recv_sems.at[4 * a + k],
                    device_id=peer, device_id_type=pl.DeviceIdType.MESH).start()
        token[...] = jnp.zeros_like(token)

    outs = pl.pallas_call(
        body, name=name,
        out_shape=(pltpu.SemaphoreType.DMA((4 * n,)), pltpu.SemaphoreType.DMA((4 * n,)),
                   pltpu.SemaphoreType.DMA((n,)),
                   *[pltpu.HBM(s.shape, s.dtype) for s in shards],
                   *[pltpu.HBM(l.shape, l.dtype) for l in lands],
                   jax.ShapeDtypeStruct((8, 128), F32)),
        in_specs=[_HBM] * (2 * n) + [_ANY],
        out_specs=(_SEM, _SEM, _SEM, *([_HBM] * (2 * n)), pl.BlockSpec(memory_space=pltpu.VMEM)),
        input_output_aliases={i: 3 + i for i in range(2 * n)},
        compiler_params=pltpu.CompilerParams(has_side_effects=_DATAFLOW),
    )(*[pltpu.with_memory_space_constraint(v, pltpu.HBM) for v in list(shards) + lands], after)
    return (outs[0], outs[1], outs[2], list(outs[3:3 + n]), list(outs[3 + n:3 + 2 * n])), outs[-1]


def _gather2_forward(handle, after, name):
    send_sems, recv_sems, local_sems, shards, lands = handle
    n = len(shards)

    def body(*refs):
        x_refs, l_refs = refs[:n], refs[n:2 * n]
        send_sems, recv_sems = refs[2 * n], refs[2 * n + 1]
        out0 = 2 * n + 2 + len(after)
        fwd_send, fwd_recv, token = refs[out0], refs[out0 + 1], refs[-1]
        x, y, c = lax.axis_index("x"), lax.axis_index("y"), lax.axis_index("c")
        for a in range(n):
            for j, chip in enumerate(_chips_of(x, y)):
                block = l_refs[a].at[4 * chip[0] + 2 * chip[1] + c]
                pltpu.make_async_remote_copy(
                    src_ref=x_refs[a], dst_ref=block, send_sem=send_sems.at[4 * a + 1 + j],
                    recv_sem=recv_sems.at[4 * a + 1 + j], device_id=(*chip, c),
                    device_id_type=pl.DeviceIdType.MESH).wait_recv()
                pltpu.make_async_remote_copy(
                    src_ref=block, dst_ref=block, send_sem=fwd_send.at[3 * a + j], recv_sem=fwd_recv.at[3 * a + j],
                    device_id=(x, y, 1 - c), device_id_type=pl.DeviceIdType.MESH).start()
        token[...] = jnp.zeros_like(token)

    outs = pl.pallas_call(
        body, name=name,
        out_shape=(pltpu.SemaphoreType.DMA((3 * n,)), pltpu.SemaphoreType.DMA((3 * n,)),
                   *[pltpu.HBM(s.shape, s.dtype) for s in shards], *[pltpu.HBM(l.shape, l.dtype) for l in lands],
                   jax.ShapeDtypeStruct((8, 128), F32)),
        in_specs=[_HBM] * (2 * n) + [_SEM, _SEM] + [_ANY] * len(after),
        out_specs=(_SEM, _SEM, *([_HBM] * (2 * n)), pl.BlockSpec(memory_space=pltpu.VMEM)),
        input_output_aliases={i: 2 + i for i in range(2 * n)},
        compiler_params=pltpu.CompilerParams(has_side_effects=_DATAFLOW),
    )(*shards, *lands, send_sems, recv_sems, *after)
    handle = (send_sems, recv_sems, local_sems, outs[0], outs[1], list(outs[2:2 + n]), list(outs[2 + n:2 + 2 * n]))
    return handle, outs[-1]


def _gather2_wait(handle, after, name):
    send_sems, recv_sems, local_sems, fwd_send, fwd_recv, shards, lands = handle
    n = len(shards)

    def body(*refs):
        x_refs, l_refs = refs[:n], refs[n:2 * n]
        send_sems, recv_sems, local_sems, fwd_send, fwd_recv = refs[2 * n:2 * n + 5]
        x, y, c = lax.axis_index("x"), lax.axis_index("y"), lax.axis_index("c")
        sibling = (x, y, 1 - c)

        def copy(a, send_sem, recv_sem):
            return pltpu.make_async_remote_copy(
                src_ref=x_refs[a], dst_ref=l_refs[a].at[0], send_sem=send_sem, recv_sem=recv_sem,
                device_id=sibling, device_id_type=pl.DeviceIdType.MESH)

        for a in range(n):
            pltpu.make_async_copy(x_refs[a], l_refs[a].at[0], local_sems.at[a]).wait()
            for k in range(4):
                copy(a, send_sems.at[4 * a + k], recv_sems.at[4 * a + k]).wait_send()
            copy(a, send_sems.at[4 * a], recv_sems.at[4 * a]).wait_recv()
            for j in range(3):
                passed = copy(a, fwd_send.at[3 * a + j], fwd_recv.at[3 * a + j])
                passed.wait_send()
                passed.wait_recv()

    outs = pl.pallas_call(
        body, name=name,
        out_shape=(*[pltpu.HBM(s.shape, s.dtype) for s in shards], *[pltpu.HBM(l.shape, l.dtype) for l in lands]),
        in_specs=[_HBM] * (2 * n) + [_SEM] * 5 + [_ANY] * len(after),
        out_specs=tuple([_HBM] * (2 * n)),
        input_output_aliases={i: i for i in range(2 * n)},
        compiler_params=pltpu.CompilerParams(has_side_effects=_DATAFLOW),
    )(*shards, *lands, send_sems, recv_sems, local_sems, fwd_send, fwd_recv, *after)
    return list(outs[n:])


def _adamw(recv, w, m, v, name):
    n, rows, width = recv.shape
    tr = _pick(rows, max(8, ADAM_BLOCK_BYTES // (n * width * recv.dtype.itemsize)), 8)
    c_m = 1.0 - ADAM_B1 ** ADAM_STEP
    c_v = 1.0 - ADAM_B2 ** ADAM_STEP

    def body(r_ref, w_ref, m_ref, v_ref, g_ref, d_ref, nm_ref, nv_ref):
        g = r_ref[0].astype(F32)
        for i in range(1, n):
            g = g + r_ref[i].astype(F32)
        m2 = ADAM_B1 * m_ref[...] + (1.0 - ADAM_B1) * g
        v2 = ADAM_B2 * v_ref[...] + (1.0 - ADAM_B2) * (g * g)
        m_hat = m2 / c_m
        v_hat = v2 / c_v
        g_ref[...] = g
        d_ref[...] = -ADAM_LR * (m_hat / (jnp.sqrt(v_hat) + ADAM_EPS) + ADAM_WD * w_ref[...])
        nm_ref[...] = m2
        nv_ref[...] = v2

    blk = _rows(tr, width)
    shp = jax.ShapeDtypeStruct((rows, width), F32)
    return pl.pallas_call(
        body, name=name, out_shape=(shp, shp, shp, shp), grid=(rows // tr,),
        in_specs=[pl.BlockSpec((n, tr, width), lambda i: (0, i, 0)), blk, blk, blk],
        out_specs=(blk, blk, blk, blk), compiler_params=_params())(recv, w, m, v)


def _join_cols(gathered):
    n, r, c = gathered.shape
    return gathered.transpose(1, 0, 2).reshape(r, n * c)


def _split_cols(full):
    r, c = full.shape
    return full.reshape(r, N_DEV, c // N_DEV).transpose(1, 0, 2)


def _adamw_small(items, name):
    n = len(items)
    c_m = 1.0 - ADAM_B1 ** ADAM_STEP
    c_v = 1.0 - ADAM_B2 ** ADAM_STEP

    def body(*refs):
        ins, outs = refs[:4 * n], refs[4 * n:]
        for k in range(n):
            r_ref, w_ref, m_ref, v_ref = ins[4 * k:4 * k + 4]
            g = r_ref[0].astype(F32)
            for i in range(1, N_DEV):
                g = g + r_ref[i].astype(F32)
            m2 = ADAM_B1 * m_ref[...] + (1.0 - ADAM_B1) * g
            v2 = ADAM_B2 * v_ref[...] + (1.0 - ADAM_B2) * (g * g)
            outs[4 * k][...] = g
            outs[4 * k + 1][...] = -ADAM_LR * ((m2 / c_m) / (jnp.sqrt(v2 / c_v) + ADAM_EPS) + ADAM_WD * w_ref[...])
            outs[4 * k + 2][...] = m2
            outs[4 * k + 3][...] = v2

    vmem = pl.BlockSpec(memory_space=pltpu.VMEM)
    flat = pl.pallas_call(
        body, name=name,
        out_shape=tuple(jax.ShapeDtypeStruct(w.shape, F32) for _, w, _, _ in items for _ in range(4)),
        in_specs=[vmem] * (4 * n), out_specs=tuple([vmem] * (4 * n)),
        compiler_params=pltpu.CompilerParams(vmem_limit_bytes=VMEM_LIMIT_BYTES),
    )(*[a for item in items for a in item])
    return [tuple(flat[4 * k:4 * k + 4]) for k in range(n)]


def _discretise(lam_re, lam_im, log_dt, b_re, b_im):
    dt = jnp.exp(log_dt)
    mag = jnp.exp(lam_re * dt)
    ab_re = mag * jnp.cos(lam_im * dt)
    ab_im = mag * jnp.sin(lam_im * dt)
    nr = ab_re - 1.0
    ni = ab_im
    den = lam_re * lam_re + lam_im * lam_im
    coef_re = (nr * lam_re + ni * lam_im) / den
    coef_im = (ni * lam_re - nr * lam_im) / den
    return ab_re, ab_im, coef_re * b_re - coef_im * b_im, coef_re * b_im + coef_im * b_re


def _s5_discretise(operands, name, token):
    def body(*refs):
        for out_ref, v in zip(refs[6:], _discretise(*[r[...] for r in refs[:5]])):
            out_ref[...] = v

    vmem = pl.BlockSpec(memory_space=pltpu.VMEM)
    shape = jax.ShapeDtypeStruct(operands[0].shape, F32)
    return pl.pallas_call(body, name=name, out_shape=(shape,) * 4, in_specs=[vmem] * 6,
                          out_specs=(vmem,) * 4)(*operands, token)


def _s5_discretise_bwd(operands, cotangents, name):
    def body(*refs):
        _, vjp = jax.vjp(_discretise, *[r[...] for r in refs[:5]])
        for out_ref, v in zip(refs[9:], vjp(tuple(r[...] for r in refs[5:9]))):
            out_ref[...] = v

    vmem = pl.BlockSpec(memory_space=pltpu.VMEM)
    shape = jax.ShapeDtypeStruct(operands[0].shape, F32)
    return pl.pallas_call(body, name=name, out_shape=(shape,) * 5, in_specs=[vmem] * 9,
                          out_specs=(vmem,) * 5)(*operands, *cotangents)


def _same_group(gl):
    return jnp.eye(gl, dtype=bool)[None, :, None, :, None]


def _super_groups_in(bb, gl):
    g, p, i = bb.shape
    blocks = bb.reshape(g // gl, gl, p, i).transpose(0, 1, 3, 2)[:, :, :, None, :]
    return jnp.where(_same_group(gl), blocks, 0.0).reshape(g // gl, gl * i, gl * p)


def _super_groups_in_take(dense, gl, p, i):
    n_sg = dense.shape[0]
    blocks = jnp.where(_same_group(gl), dense.reshape(n_sg, gl, i, gl, p), 0.0).sum(axis=3)
    return blocks.transpose(0, 1, 3, 2).reshape(n_sg * gl, p, i)


def _super_groups_out(cc, gl):
    g, i, p = cc.shape
    blocks = cc.reshape(g // gl, gl, i, p).transpose(0, 1, 3, 2)[:, :, :, None, :]
    return jnp.where(_same_group(gl), blocks, 0.0).reshape(g // gl, gl * p, gl * i)


def _super_groups_out_take(dense, gl, i, p):
    n_sg = dense.shape[0]
    blocks = jnp.where(_same_group(gl), dense.reshape(n_sg, gl, p, gl, i), 0.0).sum(axis=3)
    return blocks.transpose(0, 1, 3, 2).reshape(n_sg * gl, i, p)


def _perm_rows(z, n_seq):
    t, w = z.shape
    steps = t // n_seq // N_SEG
    return z.reshape(n_seq, N_SEG, steps, w).transpose(0, 2, 1, 3).reshape(t, w)


def _unperm_rows(z, n_seq):
    t, w = z.shape
    steps = t // n_seq // N_SEG
    return z.reshape(n_seq, steps, N_SEG, w).transpose(0, 2, 1, 3).reshape(t, w)


def kernel(*args):
    assert len(args) == len(INPUT_NAMES)
    inp = dict(zip(INPUT_NAMES, args))
    depth = inp["ffn1_w_in"].shape[0]
    assert depth == 1
    alpha = (2.0 * depth) ** 0.25

    n_seq, seq, d_model = inp["x"].shape
    t = n_seq * seq
    x = inp["x"].reshape(t, d_model)
    target = inp["loss_target"].reshape(t, d_model)
    wts = {n: inp[n][0] if n in SHARDED else inp[n] for n in WEIGHTS}
    mom = {n: inp["m_" + n][0] if n in SHARDED else inp["m_" + n] for n in WEIGHTS}
    var = {n: inp["v_" + n][0] if n in SHARDED else inp["v_" + n] for n in WEIGHTS}

    full = {}

    def place(n, g):
        if n in ("ffn1_w_in", "ffn2_w_in"):
            full[n] = g.reshape((2, N_DEV // 2) + g.shape[1:])
        elif n in ("ffn1_w_out", "ffn2_w_out"):
            full[n] = g.reshape(N_DEV // 2, 2 * g.shape[1], g.shape[2])
        elif n in COL_SHARDED:
            full[n] = _join_cols(g)
        else:
            full[n] = g.reshape(N_DEV * g.shape[1], g.shape[2])

    w16 = dict(zip(GATHER_FIRST, _to_bf16([wts[n] for n in GATHER_FIRST], "cast_ffn1")))
    gather_first, gather_first_token = _gather2_start([w16[n] for n in GATHER_FIRST], w16[GATHER_FIRST[0]],
                                                      "gather_first_start")
    later = tuple(n for n in SHARDED if n not in GATHER_FIRST)
    casts = _to_bf16([wts[n] for n in later] + [x, inp["p"][0].reshape(t, -1)], "cast_rest",
                     token=gather_first_token)
    w16.update(zip(later, casts))
    x16, p16 = casts[-2:]

    def gather_start(names, after, name):
        return _exchange_start([w16[n][None] for n in names], after, name)

    def gather_wait(names, handle, after, name):
        for n, g in zip(names, _exchange_wait(handle, after, name)):
            place(n, g)

    def gather2_wait(names, handle, after, name):
        for n, g in zip(names, _gather2_wait(handle, after, name)):
            place(n, g)


    def row(v):
        return v.reshape(1, -1)

    _, n_groups, n_state = wts["ssm_lambda_re"].shape
    n_ch = wts["ssm_b_re"].shape[3]
    disc_in = (jnp.repeat(wts["ssm_lambda_re"][0], n_ch, axis=1), jnp.repeat(wts["ssm_lambda_im"][0], n_ch, axis=1),
               jnp.broadcast_to(wts["ssm_log_dt"][0][:, None], (n_groups, n_state * n_ch)),
               wts["ssm_b_re"][0].reshape(n_groups, -1), wts["ssm_b_im"][0].reshape(n_groups, -1))
    ab_re, ab_im, bb_re, bb_im = _s5_discretise(disc_in, "s5_discretise", gather_first_token)
    a_re, a_im = row(ab_re[:, ::n_ch]), row(ab_im[:, ::n_ch])
    bb_re, bb_im = bb_re.reshape(n_groups, n_state, n_ch), bb_im.reshape(n_groups, n_state, n_ch)
    gl = S5_CHANNELS_PER_STEP // n_ch
    b_sg_re = _super_groups_in(bb_re, gl).astype(BF16)
    b_sg_im = _super_groups_in(bb_im, gl).astype(BF16)
    c_sg_re = _super_groups_out(wts["ssm_c_re"][0], gl).astype(BF16)
    c_sg_im = _super_groups_out(-wts["ssm_c_im"][0], gl).astype(BF16)

    ws = wts["gmlp_w_s"][0]
    n_heads = ws.shape[0]
    d_gmlp = wts["gmlp_ln_g"].shape[1]
    causal = jnp.tril(jnp.ones((CHUNK, CHUNK), dtype=bool))
    ws_masked = jnp.where(causal[None], ws, 0.0).astype(BF16)
    ws_masked_t = ws_masked.transpose(0, 2, 1)
    bs_cols = jnp.repeat(wts["gmlp_b_s"][0].T, d_gmlp // n_heads, axis=1)
    d_ssm = n_groups * n_ch
    assert d_ssm == d_gmlp and d_model == 2 * d_ssm

    prepared = [x16, b_sg_re, b_sg_im, c_sg_re, c_sg_im, ws_masked_t, bs_cols]
    gather_first, _ = _gather2_forward(gather_first, prepared, "gather_first_forward")
    gather2_wait(GATHER_FIRST, gather_first, prepared, "gather_first_wait")
    gather_early, gather_early_token = _gather2_start([w16[n] for n in GATHER_EARLY], full["ffn1_w_in"],
                                                      "gather_early_start")
    h1, a1 = _ffn_in(x16, full["ffn1_w_in"], "ffn1_in", token=gather_early_token)
    gather_early, _ = _gather2_forward(gather_early, [a1], "gather_early_forward")
    gather2_wait(GATHER_EARLY, gather_early, [a1], "gather_early_wait")
    r1, x1, x1_16 = _ffn_out_ln(a1, full["ffn1_w_out"], x, row(wts["ln1_g"]), row(wts["ln1_b"]), alpha,
                                "ffn1_out_ln")

    gather_mid, token = gather_start(GATHER_MID, x1_16, "gather_mid_start")
    gather_last, token = _gather2_start([w16[n] for n in GATHER_LAST], token, "gather_last_start")
    proj = _mm(x1_16, full["mix_w_in"], name="mix_in", token=token)
    za_p = _perm_rows(proj[:, :d_ssm], n_seq)
    h_re, h_im, yssm = _s5_fwd(za_p, b_sg_re, b_sg_im, a_re, a_im, c_sg_re, c_sg_im, n_seq, "s5_fwd")
    gather_wait(GATHER_MID, gather_mid, [yssm], "gather_mid_wait")
    s5out = _s5_post_fwd(yssm, za_p, row(wts["ssm_d"]), full["ssm_glu_w"], row(wts["ssm_glu_b"]), n_seq,
                         "s5_post")
    gm = _gmlp_fwd(proj, row(wts["gmlp_ln_g"]), row(wts["gmlp_ln_b"]), ws_masked, bs_cols, "gmlp")
    m_mix, y_a, y_b = _merge_fwd(s5out, gm, proj, full["up_a"], full["up_b"], "merge")
    gather_last, token = _gather2_forward(gather_last, [m_mix], "gather_last_forward")
    r2, x2, x2_16 = _ffn_out_ln(m_mix[None], full["mix_w_out"][None], x1, row(wts["ln2_g"]), row(wts["ln2_b"]),
                                alpha, "mix_out_ln2", scale=1.0, token=token)

    gather2_wait(GATHER_LAST, gather_last, [x2_16], "gather_last_wait")
    h2, a2 = _ffn_in(x2_16, full["ffn2_w_in"], "ffn2_in")
    r3, x3, _ = _ffn_out_ln(a2, full["ffn2_w_out"], x2, row(wts["ln3_g"]), row(wts["ln3_b"]), alpha,
                            "ffn2_out_ln")

    small = {}
    send = {}

    def lane_dense(n, v):
        return v.reshape(v.shape[:2] + (-1,)) if n in ("ssm_b_re", "ssm_b_im") else v

    def on_wire(n, g):
        g = lane_dense(n, g)
        return (g.astype(BF16) if n in SMALL_BF16 else g)[None]
    loss_parts, dr3, dr3_h, dw_gate, dw_proj, dg, db = _head(
        x3, r3, row(wts["ln3_g"]), p16, target, full["ple_w_gate"], full["ple_w_proj"], 0.5, "head")
    loss_mine = jnp.full((1, 1, 8, 128), jnp.sum(loss_parts[::8, 0]), F32)
    send["ple_w_gate"] = dw_gate.astype(BF16).reshape(N_DEV, -1, d_model)
    send["ple_w_proj"] = _split_cols(dw_proj.astype(BF16))
    small["ln3_g"], small["ln3_b"] = dg.sum(0, keepdims=True), db.sum(0, keepdims=True)
    dh2 = _ffn_out_dx(dr3_h, full["ffn2_w_out"], h2, "ffn2_out_dx")
    dw = _ffn_out_dw(a2, dr3_h, "ffn2_out_dw")
    send["ffn2_w_out"] = dw.reshape(N_DEV, -1, d_model)
    dw = _ffn_in_dw(x2_16, dh2, "ffn2_in_dw")
    send["ffn2_w_in"] = dw.reshape((N_DEV,) + dw.shape[2:])
    group1 = ("ffn2_w_in", "ffn2_w_out", "ple_w_gate", "ple_w_proj")
    exchange1, token = _exchange_start(
        [send[n] for n in group1] + [on_wire(n, small[n]) for n in SMALL_HEAD] + [loss_mine], dh2,
        "grad_exchange1_start")
    dr2, dr2_h, dg, db = _ffn_in_dx_ln(dh2, full["ffn2_w_in"], r2, dr3, alpha, row(wts["ln2_g"]), 1.0,
                                       "ffn2_in_dx_ln2_bwd", token=token)
    small["ln2_g"], small["ln2_b"] = dg.sum(0, keepdims=True), db.sum(0, keepdims=True)
    dm = _mm(dr2_h, full["mix_w_out"], tb=True, name="mix_out_dx")
    send["mix_w_out"] = _mm(m_mix, dr2_h, ta=True, name="mix_out_dw", out_dtype=BF16).reshape(
        N_DEV, -1, d_model)
    dga, dgb, ds5out, dgm, dw_a, dw_b = _merge_bwd(
        dm, y_a, y_b, s5out, gm, proj, full["up_a"], full["up_b"], "merge_bwd")
    send["up_a"] = _split_cols(dw_a.astype(BF16))
    send["up_b"] = _split_cols(dw_b.astype(BF16))

    dzu, dzv, dws, dbs_cols, dg, db = _gmlp_bwd(
        proj, dgm, row(wts["gmlp_ln_g"]), row(wts["gmlp_ln_b"]), ws_masked, ws_masked_t, bs_cols,
        "gmlp_bwd")
    small["gmlp_ln_g"], small["gmlp_ln_b"] = dg.sum(0, keepdims=True), db.sum(0, keepdims=True)
    small["gmlp_w_s"] = jnp.where(causal[None], dws, 0.0)[None]
    small["gmlp_b_s"] = dbs_cols.reshape(CHUNK, n_heads, -1).sum(-1).T[None]

    dy_p, dza_skip, dw_glu, dd, dgb_glu = _s5_post_bwd(
        yssm, za_p, _perm_rows(ds5out, n_seq), row(wts["ssm_d"]), full["ssm_glu_w"], row(wts["ssm_glu_b"]),
        "s5_post_bwd")
    send["ssm_glu_w"] = dw_glu.astype(BF16).reshape(N_DEV, -1, d_ssm)
    small["ssm_d"], small["ssm_glu_b"] = dd.sum(0, keepdims=True), dgb_glu.sum(0, keepdims=True)
    dza_p, dbb_re, dbb_im, dc_re, dc_im, da_re, da_im = _s5_bwd(
        dy_p, dza_skip, h_re, h_im, za_p, b_sg_re, b_sg_im, a_re, a_im, c_sg_re, c_sg_im, n_seq, "s5_bwd")
    small["ssm_c_re"] = _super_groups_out_take(dc_re, gl, n_ch, n_state)[None]
    small["ssm_c_im"] = -_super_groups_out_take(dc_im, gl, n_ch, n_state)[None]
    dbb_re = _super_groups_in_take(dbb_re, gl, n_state, n_ch)
    dbb_im = _super_groups_in_take(dbb_im, gl, n_state, n_ch)
    def first_channel(v):
        placed = jnp.pad(v.sum(0).reshape(n_groups, n_state, 1), ((0, 0), (0, 0), (0, n_ch - 1)))
        return placed.reshape(n_groups, -1)

    cotangents = (first_channel(da_re), first_channel(da_im), dbb_re.reshape(n_groups, -1),
                  dbb_im.reshape(n_groups, -1))
    d_lre, d_lim, d_ldt, d_bre, d_bim = _s5_discretise_bwd(disc_in, cotangents, "s5_discretise_bwd")
    per_state = (n_groups, n_state, n_ch)
    small["ssm_lambda_re"] = d_lre.reshape(per_state).sum(-1)[None]
    small["ssm_lambda_im"] = d_lim.reshape(per_state).sum(-1)[None]
    small["ssm_log_dt"] = d_ldt.sum(-1)[None]
    small["ssm_b_re"] = d_bre.reshape((1,) + per_state)
    small["ssm_b_im"] = d_bim.reshape((1,) + per_state)

    dproj = jnp.concatenate([_unperm_rows(dza_p, n_seq), dzu, dzv, dga, dgb], axis=1)
    group2 = ("mix_w_out", "up_a", "up_b", "ssm_glu_w")
    exchange2, token = _exchange_start(
        [send[n] for n in group2] + [on_wire(n, small[n]) for n in SMALL_MID], dproj, "grad_exchange2_start")
    send["mix_w_in"] = _split_cols(_mm(x1_16, dproj, ta=True, name="mix_in_dw", out_dtype=BF16, token=token))
    exchange3, token = _exchange_start([send["mix_w_in"]], dproj, "grad_exchange3_start")
    dr1, dr1_h, dg, db = _mm_ln_bwd(dproj, full["mix_w_in"], r1, dr2, alpha, row(wts["ln1_g"]), 0.5,
                                    "mix_in_dx_ln1_bwd", token=token)
    small["ln1_g"], small["ln1_b"] = dg.sum(0, keepdims=True), db.sum(0, keepdims=True)
    dw = _ffn_out_dw(a1, dr1_h, "ffn1_out_dw")
    send["ffn1_w_out"] = dw.reshape(N_DEV, -1, d_model)
    exchange4, token = _exchange_start([send["ffn1_w_out"]] + [on_wire(n, small[n]) for n in SMALL_LATE], dr1_h,
                                       "grad_exchange4_start")
    dh1 = _ffn_out_dx(dr1_h, full["ffn1_w_out"], h1, "ffn1_out_dx", token=token)
    dw = _ffn_in_dw(x16, dh1, "ffn1_in_dw")
    send["ffn1_w_in"] = dw.reshape((N_DEV,) + dw.shape[2:])
    exchange5, token = _exchange_start([send["ffn1_w_in"]], dh1, "grad_exchange5_start")
    grad_x = _ffn_in_dx(dh1, full["ffn1_w_in"], dr1, alpha, "ffn1_in_dx", token=token)

    results = {}

    def update(names, recv, prefix):
        for n, r in zip(names, recv):
            results[n] = _adamw(r, wts[n], mom[n], var[n], prefix + n)

    def update_small(names, recv, name):
        items = [(r, lane_dense(n, wts[n]), lane_dense(n, mom[n]), lane_dense(n, var[n])) for n, r in zip(names, recv)]
        for n, outs in zip(names, _adamw_small(items, name)):
            results[n] = tuple(o.reshape(wts[n].shape) for o in outs)

    def done(names):
        return [results[n][3] for n in names]

    recv = _exchange_wait(exchange1, [grad_x], "grad_exchange1_wait")
    update(group1, recv[:len(group1)], "adamw_")
    update_small(SMALL_HEAD, recv[len(group1):-1], "adamw_ln3")
    loss = jnp.sum(recv[-1][:, 0, 0, 0])
    recv = _exchange_wait(exchange2, done(group1 + SMALL_HEAD), "grad_exchange2_wait")
    update(group2, recv[:len(group2)], "adamw_")
    update_small(SMALL_MID, recv[len(group2):], "adamw_small")
    recv = _exchange_wait(exchange3, done(group2 + SMALL_MID), "grad_exchange3_wait")
    update(("mix_w_in",), recv, "adamw_")
    recv = _exchange_wait(exchange4, done(("mix_w_in",)), "grad_exchange4_wait")
    update(("ffn1_w_out",), recv[:1], "adamw_")
    update_small(SMALL_LATE, recv[1:], "adamw_ln1")
    recv = _exchange_wait(exchange5, done(("ffn1_w_out",) + SMALL_LATE), "grad_exchange5_wait")
    update(("ffn1_w_in",), recv, "adamw_")

    outs = [loss, grad_x.reshape(n_seq, seq, d_model)]
    for k in range(4):
        outs += [results[n][k][None] if n in SHARDED else results[n][k] for n in WEIGHTS]
    return tuple(outs)
```

```python
import math

import jax
import jax.numpy as jnp
from jax import lax
from jax.experimental import pallas as pl
from jax.experimental.pallas import tpu as pltpu

F32 = jnp.float32
BF16 = jnp.bfloat16

N_DEV = 8
LN_EPS = 1e-5
CHUNK = 128
N_SEG = 8
S5_CHANNELS_PER_STEP = 128
S5_FWD_GROUPS = 2
VMEM_LIMIT_BYTES = 56 * 1024 * 1024
MM_OPERAND_BLOCK_BYTES = 8 * 1024 * 1024
ADAM_BLOCK_BYTES = 6 * 1024 * 1024

ADAM_LR = 0.001
ADAM_B1 = 0.9
ADAM_B2 = 0.999
ADAM_EPS = 1e-08
ADAM_WD = 0.01
ADAM_STEP = 10

COL_SHARDED = ("ffn1_w_in", "mix_w_in", "up_a", "up_b", "ffn2_w_in", "ple_w_proj")
SHARDED = ("ffn1_w_in", "ffn1_w_out", "mix_w_in", "ssm_glu_w", "up_a", "up_b", "mix_w_out",
           "ffn2_w_in", "ffn2_w_out", "ple_w_proj", "ple_w_gate")
WEIGHTS = ("ffn1_w_in", "ffn1_w_out", "ln1_g", "ln1_b", "mix_w_in", "ssm_lambda_re", "ssm_lambda_im",
           "ssm_log_dt", "ssm_b_re", "ssm_b_im", "ssm_c_re", "ssm_c_im", "ssm_d", "ssm_glu_w",
           "ssm_glu_b", "gmlp_ln_g", "gmlp_ln_b", "gmlp_w_s", "gmlp_b_s", "up_a", "up_b", "mix_w_out",
           "ln2_g", "ln2_b", "ffn2_w_in", "ffn2_w_out", "ln3_g", "ln3_b", "ple_w_proj", "ple_w_gate")
GATHER_FIRST = ("ffn1_w_in",)
GATHER_EARLY = ("ffn1_w_out", "mix_w_in")
GATHER_MID = ("ssm_glu_w", "up_a", "up_b", "mix_w_out")
GATHER_LAST = ("ffn2_w_in", "ffn2_w_out", "ple_w_proj", "ple_w_gate")
SMALL = tuple(n for n in WEIGHTS if n not in SHARDED)
SMALL_HEAD = ("ln3_g", "ln3_b")
SMALL_LATE = ("ln1_g", "ln1_b")
SMALL_MID = tuple(n for n in SMALL if n not in SMALL_HEAD + SMALL_LATE)
SMALL_BF16 = ("gmlp_w_s", "ssm_b_re", "ssm_b_im", "ssm_c_re", "ssm_c_im")
INPUT_NAMES = ("x", "p") + WEIGHTS + ("loss_target",) + tuple("m_" + n for n in WEIGHTS) + tuple(
    "v_" + n for n in WEIGHTS)


def _pick(dim, pref, mult=128):
    if dim <= pref:
        return dim
    d = (pref // mult) * mult
    while d >= mult:
        if dim % d == 0:
            return d
        d -= mult
    return dim


def _params(n_axes=1):
    return pltpu.CompilerParams(dimension_semantics=("arbitrary",) * n_axes,
                                vmem_limit_bytes=VMEM_LIMIT_BYTES)


def _sigmoid(v):
    return 0.5 * jnp.tanh(0.5 * v) + 0.5


_GELU_C = math.sqrt(2.0 / math.pi)


def _gelu(v):
    return 0.5 * v * (1.0 + jnp.tanh(_GELU_C * (v + 0.044715 * (v * v * v))))


def _gelu_grad(v):
    t = jnp.tanh(_GELU_C * (v + 0.044715 * (v * v * v)))
    return 0.5 * (1.0 + t) + 0.5 * v * (1.0 - t * t) * (_GELU_C * (1.0 + 3.0 * 0.044715 * v * v))


def _ln_stats(r):
    mu = jnp.mean(r, axis=-1, keepdims=True)
    xc = r - mu
    var = jnp.mean(xc * xc, axis=-1, keepdims=True)
    rstd = lax.rsqrt(var + LN_EPS)
    return xc * rstd, rstd


def _ln_bwd(dy, xhat, rstd, g):
    dxh = dy * g
    m1 = jnp.mean(dxh, axis=-1, keepdims=True)
    m2 = jnp.mean(dxh * xhat, axis=-1, keepdims=True)
    return rstd * (dxh - m1 - xhat * m2)


def _fold8(v):
    rows, w = v.shape
    return jnp.sum(v.reshape(rows // 8, 8, w), axis=0)


def _dot(a, b, ta=False, tb=False):
    dn = (((0 if ta else 1,), (1 if tb else 0,)), ((), ()))
    return lax.dot_general(a.astype(BF16), b.astype(BF16), dn, preferred_element_type=F32)


_TOKEN = pl.BlockSpec((8, 128), lambda *_: (0, 0))


def _mm(a, b, *, name, ta=False, tb=False, out_dtype=F32, add=None, add_scale=1.0, token=None,
        tm=2048, tn=512, tk=4096):
    m_dim = a.shape[1] if ta else a.shape[0]
    k_dim = a.shape[0] if ta else a.shape[1]
    n_dim = b.shape[0] if tb else b.shape[1]
    assert (b.shape[1] if tb else b.shape[0]) == k_dim, (name, a.shape, b.shape)
    tk = _pick(k_dim, tk)
    tm = min(tm, max(256, MM_OPERAND_BLOCK_BYTES // (tk * a.dtype.itemsize)))
    tm, tn = _pick(m_dim, tm), _pick(n_dim, tn)
    nk = k_dim // tk
    has_add = add is not None

    def finish(r, add_ref, o_ref):
        if has_add:
            r = r + add_scale * add_ref[...].astype(F32)
        o_ref[...] = r.astype(out_dtype)

    def body(*refs):
        a_ref, b_ref = refs[:2]
        add_ref = refs[2] if has_add else None
        o_ref = refs[n_in]
        if nk == 1:
            finish(_dot(a_ref[...], b_ref[...], ta, tb), add_ref, o_ref)
            return
        acc_ref = refs[-1]
        k = pl.program_id(2)

        @pl.when(k == 0)
        def _():
            acc_ref[...] = jnp.zeros_like(acc_ref)

        acc_ref[...] += _dot(a_ref[...], b_ref[...], ta, tb)

        @pl.when(k == nk - 1)
        def _():
            finish(acc_ref[...], add_ref, o_ref)

    a_spec = (pl.BlockSpec((tk, tm), lambda i, j, k: (k, i)) if ta
              else pl.BlockSpec((tm, tk), lambda i, j, k: (i, k)))
    b_spec = (pl.BlockSpec((tn, tk), lambda i, j, k: (j, k)) if tb
              else pl.BlockSpec((tk, tn), lambda i, j, k: (k, j)))
    in_specs = [a_spec, b_spec]
    operands = [a, b]
    if has_add:
        in_specs.append(pl.BlockSpec((tm, tn), lambda i, j, k: (i, j)))
        operands.append(add)
    if token is not None:
        in_specs.append(_TOKEN)
        operands.append(token)
    n_in = len(operands)
    return pl.pallas_call(
        body, name=name,
        out_shape=jax.ShapeDtypeStruct((m_dim, n_dim), out_dtype),
        grid=(m_dim // tm, n_dim // tn, nk),
        in_specs=in_specs,
        out_specs=pl.BlockSpec((tm, tn), lambda i, j, k: (i, j)),
        scratch_shapes=[pltpu.VMEM((tm, tn), F32)] if nk > 1 else [],
        compiler_params=_params(3),
    )(*operands)


def _to_bf16(arrays, name, token=None):
    n = len(arrays)
    extra = [] if token is None else [token]

    def body(*refs):
        for src, dst in zip(refs[:n], refs[n + len(extra):]):
            dst[...] = src[...].astype(BF16)

    vmem = pl.BlockSpec(memory_space=pltpu.VMEM)
    return pl.pallas_call(
        body, name=name, out_shape=tuple(jax.ShapeDtypeStruct(a.shape, BF16) for a in arrays),
        in_specs=[vmem] * (n + len(extra)), out_specs=tuple([vmem] * n),
        compiler_params=pltpu.CompilerParams(vmem_limit_bytes=VMEM_LIMIT_BYTES))(*arrays, *extra)


def _rows(tr, w, cb=0):
    return pl.BlockSpec((tr, w), lambda i: (i, cb))


def _whole(shape):
    nd = len(shape)
    return pl.BlockSpec(tuple(shape), lambda i: (0,) * nd)


def _ffn_in(x16, w_in, name, token=None):
    t, d = x16.shape
    _, nb, _, fb = w_in.shape
    tm = _pick(t, 2048, 8)
    extra = [] if token is None else [token]

    def body(*refs):
        x_ref, wg_ref, wu_ref = refs[:3]
        h_ref, a_ref = refs[-2:]
        xv = x_ref[...]
        g = _dot(xv, wg_ref[0, 0])
        u = _dot(xv, wu_ref[0, 0])
        h_ref[0, 0] = g.astype(BF16)
        h_ref[0, 1] = u.astype(BF16)
        a_ref[0] = (g * _sigmoid(g) * u).astype(BF16)

    return pl.pallas_call(
        body, name=name,
        out_shape=(jax.ShapeDtypeStruct((nb, 2, t, fb), BF16), jax.ShapeDtypeStruct((nb, t, fb), BF16)),
        grid=(t // tm, nb),
        in_specs=[pl.BlockSpec((tm, d), lambda i, j: (i, 0)),
                  pl.BlockSpec((1, 1, d, fb), lambda i, j: (0, j, 0, 0)),
                  pl.BlockSpec((1, 1, d, fb), lambda i, j: (1, j, 0, 0))] + [_TOKEN] * len(extra),
        out_specs=(pl.BlockSpec((1, 2, tm, fb), lambda i, j: (j, 0, i, 0)),
                   pl.BlockSpec((1, tm, fb), lambda i, j: (j, i, 0))),
        compiler_params=_params(2))(x16, w_in, w_in, *extra)


def _ffn_out_ln(a, w_out, xin, g, b, alpha, name, scale=0.5, token=None):
    nb, t, fb = a.shape
    d = w_out.shape[2]
    tm = _pick(t, 512, 8)
    extra = [] if token is None else [token]

    def body(*refs):
        a_ref, w_ref, x_ref, g_ref, b_ref = refs[:5]
        r_ref, y_ref, y16_ref = refs[-3:]
        f = _dot(a_ref[0], w_ref[0])
        for jb in range(1, nb):
            f = f + _dot(a_ref[jb], w_ref[jb])
        r = alpha * x_ref[...] + scale * f
        xhat, _ = _ln_stats(r)
        y = xhat * g_ref[...] + b_ref[...]
        r_ref[...] = r
        y_ref[...] = y
        y16_ref[...] = y.astype(BF16)

    return pl.pallas_call(
        body, name=name,
        out_shape=(jax.ShapeDtypeStruct((t, d), F32), jax.ShapeDtypeStruct((t, d), F32),
                   jax.ShapeDtypeStruct((t, d), BF16)),
        grid=(t // tm,),
        in_specs=[pl.BlockSpec((nb, tm, fb), lambda i: (0, i, 0)), _whole(w_out.shape), _rows(tm, d),
                  _whole((1, d)), _whole((1, d))] + [_TOKEN] * len(extra),
        out_specs=(_rows(tm, d), _rows(tm, d), _rows(tm, d)),
        compiler_params=_params())(a, w_out, xin, g, b, *extra)


def _ffn_out_dx(drs, w_out, h, name, token=None):
    nb, _, t, fb = h.shape
    d = w_out.shape[2]
    tm = _pick(t, 2048, 8)
    extra = [] if token is None else [token]

    def body(*refs):
        dr_ref, w_ref, h_ref, dh_ref = refs[0], refs[1], refs[2], refs[-1]
        da = _dot(dr_ref[...], w_ref[0], tb=True)
        g, u = h_ref[0, 0].astype(F32), h_ref[0, 1].astype(F32)
        sg = _sigmoid(g)
        silu = g * sg
        dh_ref[0, 0] = ((da * sg) * u * (1.0 + (g - silu))).astype(BF16)
        dh_ref[0, 1] = (da * silu).astype(BF16)

    return pl.pallas_call(
        body, name=name, out_shape=jax.ShapeDtypeStruct((nb, 2, t, fb), BF16), grid=(t // tm, nb),
        in_specs=[pl.BlockSpec((tm, d), lambda i, j: (i, 0)),
                  pl.BlockSpec((1, fb, d), lambda i, j: (j, 0, 0)),
                  pl.BlockSpec((1, 2, tm, fb), lambda i, j: (j, 0, i, 0))] + [_TOKEN] * len(extra),
        out_specs=pl.BlockSpec((1, 2, tm, fb), lambda i, j: (j, 0, i, 0)),
        compiler_params=_params(2))(drs, w_out, h, *extra)


def _ffn_out_dw(a, drs, name):
    nb, t, fb = a.shape
    d = drs.shape[1]

    def body(a_ref, dr_ref, o_ref):
        o_ref[0] = _dot(a_ref[0], dr_ref[...], ta=True).astype(BF16)

    return pl.pallas_call(
        body, name=name, out_shape=jax.ShapeDtypeStruct((nb, fb, d), BF16), grid=(nb,),
        in_specs=[pl.BlockSpec((1, t, fb), lambda j: (j, 0, 0)), pl.BlockSpec((t, d), lambda j: (0, 0))],
        out_specs=pl.BlockSpec((1, fb, d), lambda j: (j, 0, 0)),
        compiler_params=_params())(a, drs)


def _ffn_in_dw(x16, dh, name, token=None):
    t, d = x16.shape
    nb, _, _, fb = dh.shape
    extra = [] if token is None else [token]

    def body(*refs):
        x_ref, dh_ref, o_ref = refs[0], refs[1], refs[-1]
        o_ref[0, 0] = _dot(x_ref[...], dh_ref[0, 0], ta=True).astype(BF16)

    return pl.pallas_call(
        body, name=name, out_shape=jax.ShapeDtypeStruct((2, nb, d, fb), BF16), grid=(nb, 2),
        in_specs=[pl.BlockSpec((t, d), lambda j, s: (0, 0)),
                  pl.BlockSpec((1, 1, t, fb), lambda j, s: (j, s, 0, 0))] + [_TOKEN] * len(extra),
        out_specs=pl.BlockSpec((1, 1, d, fb), lambda j, s: (s, j, 0, 0)),
        compiler_params=_params(2))(x16, dh, *extra)


def _ffn_in_dx(dh, w_in, add, add_scale, name, token=None):
    nb, _, t, fb = dh.shape
    d = w_in.shape[2]
    tm = _pick(t, 512, 8)
    has_add = add is not None
    extra = [] if token is None else [token]

    def body(*refs):
        dh_ref, w_ref = refs[:2]
        o_ref = refs[-1]
        acc = None
        for jb in range(nb):
            for s in range(2):
                part = _dot(dh_ref[jb, s], w_ref[s, jb], tb=True)
                acc = part if acc is None else acc + part
        if has_add:
            acc = acc + add_scale * refs[2][...]
        o_ref[...] = acc

    return pl.pallas_call(
        body, name=name, out_shape=jax.ShapeDtypeStruct((t, d), F32), grid=(t // tm,),
        in_specs=[pl.BlockSpec((nb, 2, tm, fb), lambda i: (0, 0, i, 0)), _whole(w_in.shape)]
        + ([_rows(tm, d)] if has_add else []) + [_TOKEN] * len(extra),
        out_specs=_rows(tm, d),
        compiler_params=_params())(*([dh, w_in] + ([add] if has_add else []) + extra))


def _ffn_in_dx_ln(dh, w_in, r, dy_b, b_scale, ln_g, out_scale, name, token=None):
    nb, _, t, fb = dh.shape
    d = w_in.shape[2]
    tm = _pick(t, 512, 8)
    extra = [] if token is None else [token]

    def body(*refs):
        dh_ref, w_ref, r_ref, dyb_ref, g_ref = refs[:5]
        dr_ref, drs_ref, dg_ref, dbeta_ref = refs[-4:]
        dy = b_scale * dyb_ref[...]
        for jb in range(nb):
            for s in range(2):
                dy = dy + _dot(dh_ref[jb, s], w_ref[s, jb], tb=True)
        xhat, rstd = _ln_stats(r_ref[...])
        dr = _ln_bwd(dy, xhat, rstd, g_ref[...])
        dr_ref[...] = dr
        drs_ref[...] = (out_scale * dr).astype(BF16)

        @pl.when(pl.program_id(0) == 0)
        def _():
            dg_ref[...] = jnp.zeros_like(dg_ref)
            dbeta_ref[...] = jnp.zeros_like(dbeta_ref)

        dg_ref[...] += _fold8(dy * xhat)
        dbeta_ref[...] += _fold8(dy)

    return pl.pallas_call(
        body, name=name,
        out_shape=(jax.ShapeDtypeStruct((t, d), F32), jax.ShapeDtypeStruct((t, d), BF16),
                   jax.ShapeDtypeStruct((8, d), F32), jax.ShapeDtypeStruct((8, d), F32)),
        grid=(t // tm,),
        in_specs=[pl.BlockSpec((nb, 2, tm, fb), lambda i: (0, 0, i, 0)), _whole(w_in.shape), _rows(tm, d),
                  _rows(tm, d), _whole((1, d))] + [_TOKEN] * len(extra),
        out_specs=(_rows(tm, d), _rows(tm, d), _whole((8, d)), _whole((8, d))),
        compiler_params=_params())(dh, w_in, r, dy_b, ln_g, *extra)


def _mm_ln_bwd(a, w, r, dy_b, b_scale, ln_g, out_scale, name, token=None):
    t, k_dim = a.shape
    d = w.shape[0]
    tm = _pick(t, 512, 8)
    extra = [] if token is None else [token]

    def body(*refs):
        a_ref, w_ref, r_ref, dyb_ref, g_ref = refs[:5]
        dr_ref, drs_ref, dg_ref, dbeta_ref = refs[-4:]
        dy = _dot(a_ref[...], w_ref[...], tb=True) + b_scale * dyb_ref[...]
        xhat, rstd = _ln_stats(r_ref[...])
        dr = _ln_bwd(dy, xhat, rstd, g_ref[...])
        dr_ref[...] = dr
        drs_ref[...] = (out_scale * dr).astype(BF16)

        @pl.when(pl.program_id(0) == 0)
        def _():
            dg_ref[...] = jnp.zeros_like(dg_ref)
            dbeta_ref[...] = jnp.zeros_like(dbeta_ref)

        dg_ref[...] += _fold8(dy * xhat)
        dbeta_ref[...] += _fold8(dy)

    return pl.pallas_call(
        body, name=name,
        out_shape=(jax.ShapeDtypeStruct((t, d), F32), jax.ShapeDtypeStruct((t, d), BF16),
                   jax.ShapeDtypeStruct((8, d), F32), jax.ShapeDtypeStruct((8, d), F32)),
        grid=(t // tm,),
        in_specs=[_rows(tm, k_dim), _whole(w.shape), _rows(tm, d), _rows(tm, d), _whole((1, d))]
        + [_TOKEN] * len(extra),
        out_specs=(_rows(tm, d), _rows(tm, d), _whole((8, d)), _whole((8, d))),
        compiler_params=_params())(a, w, r, dy_b, ln_g, *extra)


def _take_row(v, row_id, s):
    return jnp.sum(jnp.where(row_id == s, v, 0.0), axis=0, keepdims=True)


def _complex_power(ar, ai, n):
    out = None
    while n:
        if n & 1:
            out = (ar, ai) if out is None else (out[0] * ar - out[1] * ai, out[0] * ai + out[1] * ar)
        ar, ai = ar * ar - ai * ai, 2.0 * ar * ai
        n >>= 1
    return out


def _seg_carries(f_re, f_im, p_re, p_im, reverse):
    row_id = lax.broadcasted_iota(jnp.int32, f_re.shape, 0)
    p_re, p_im = _take_row(p_re, row_id, 0), _take_row(p_im, row_id, 0)
    out_re, out_im = jnp.zeros_like(f_re), jnp.zeros_like(f_im)
    c_re, c_im = jnp.zeros_like(p_re), jnp.zeros_like(p_im)
    order = range(N_SEG - 1, -1, -1) if reverse else range(N_SEG)
    for s in order:
        out_re = jnp.where(row_id == s, c_re, out_re)
        out_im = jnp.where(row_id == s, c_im, out_im)
        fr, fi = _take_row(f_re, row_id, s), _take_row(f_im, row_id, s)
        c_re, c_im = fr + p_re * c_re - p_im * c_im, fi + p_re * c_im + p_im * c_re
    return out_re, out_im


def _s5_fwd(za16, b_re, b_im, a_re, a_im, c_re, c_im, n_seq, name):
    t = za16.shape[0]
    n_sg, ch1, st1 = b_re.shape
    per = S5_FWD_GROUPS if n_sg % S5_FWD_GROUPS == 0 else 1
    ch, st = per * ch1, per * st1
    seq = t // n_seq
    steps = seq // N_SEG

    def body(za_ref, bre_ref, bim_ref, are, aim, cre_ref, cim_ref, hre, him, y_ref):
        for g in range(per):
            za = za_ref[:, g * ch1:(g + 1) * ch1]
            hre[:, g * st1:(g + 1) * st1] = _dot(za, bre_ref[g])
            him[:, g * st1:(g + 1) * st1] = _dot(za, bim_ref[g])
        ar = jnp.broadcast_to(are[...], (N_SEG, st))
        ai = jnp.broadcast_to(aim[...], (N_SEG, st))
        zero = jnp.zeros((N_SEG, st), F32)
        p_re, p_im = _complex_power(ar, ai, steps)

        def local(k, carry):
            lr, li = carry
            rows = pl.ds(pl.multiple_of(k * N_SEG, N_SEG), N_SEG)
            nr = ar * lr - ai * li + hre[rows, :]
            ni = ar * li + ai * lr + him[rows, :]
            hre[rows, :] = nr
            him[rows, :] = ni
            return nr, ni

        f_re, f_im = lax.fori_loop(0, steps, local, (zero, zero))
        c_re, c_im = _seg_carries(f_re, f_im, p_re, p_im, reverse=False)

        def fix(k, carry):
            qr, qi = carry
            rows = pl.ds(pl.multiple_of(k * N_SEG, N_SEG), N_SEG)
            hre[rows, :] = hre[rows, :] + qr
            him[rows, :] = him[rows, :] + qi
            return ar * qr - ai * qi, ar * qi + ai * qr

        lax.fori_loop(0, steps, fix, (ar * c_re - ai * c_im, ar * c_im + ai * c_re))
        for g in range(per):
            states = slice(g * st1, (g + 1) * st1)
            y_ref[:, g * ch1:(g + 1) * ch1] = _dot(hre[:, states], cre_ref[g]) + _dot(him[:, states], cim_ref[g])

    rows_ch = pl.BlockSpec((seq, ch), lambda s, j: (s, j))
    rows_st = pl.BlockSpec((seq, st), lambda s, j: (s, j))
    vec = pl.BlockSpec((1, st), lambda s, j: (0, j))
    b_blk = pl.BlockSpec((per, ch1, st1), lambda s, j: (j, 0, 0))
    c_blk = pl.BlockSpec((per, st1, ch1), lambda s, j: (j, 0, 0))
    return pl.pallas_call(
        body, name=name,
        out_shape=(jax.ShapeDtypeStruct((t, n_sg * st1), F32), jax.ShapeDtypeStruct((t, n_sg * st1), F32),
                   jax.ShapeDtypeStruct((t, n_sg * ch1), F32)),
        grid=(n_seq, n_sg // per), in_specs=[rows_ch, b_blk, b_blk, vec, vec, c_blk, c_blk],
        out_specs=(rows_st, rows_st, rows_ch),
        compiler_params=_params(2))(za16, b_re, b_im, a_re, a_im, c_re, c_im)


def _s5_bwd(dy16, dza_skip, h_re, h_im, za16, b_re, b_im, a_re, a_im, c_re, c_im, n_seq, name):
    t = dy16.shape[0]
    n_sg, ch, st = b_re.shape
    seq = t // n_seq
    steps = seq // N_SEG

    def body(dy_ref, skip_ref, hre, him, za_ref, bre_ref, bim_ref, are, aim, cre_ref, cim_ref,
             dza_ref, dbre_ref, dbim_ref, dcre_ref, dcim_ref, dare, daim, lre, lim):
        dy = dy_ref[...]
        lre[...] = _dot(dy, cre_ref[0], tb=True)
        lim[...] = _dot(dy, cim_ref[0], tb=True)
        ar = jnp.broadcast_to(are[...], (N_SEG, st))
        ai = -jnp.broadcast_to(aim[...], (N_SEG, st))
        zero = jnp.zeros((N_SEG, st), F32)
        p_re, p_im = _complex_power(ar, ai, steps)

        def local(i, carry):
            lr, li = carry
            k = steps - 1 - i
            rows = pl.ds(pl.multiple_of(k * N_SEG, N_SEG), N_SEG)
            nr = ar * lr - ai * li + lre[rows, :]
            ni = ar * li + ai * lr + lim[rows, :]
            lre[rows, :] = nr
            lim[rows, :] = ni
            return nr, ni

        f_re, f_im = lax.fori_loop(0, steps, local, (zero, zero))
        c_re, c_im = _seg_carries(f_re, f_im, p_re, p_im, reverse=True)

        def accumulate(lam_r, lam_i, hp_r, hp_i, acc_r, acc_i):
            return acc_r + (lam_r * hp_r + lam_i * hp_i), acc_i + (lam_i * hp_r - lam_r * hp_i)

        def fix(i, carry):
            qr, qi, acc_r, acc_i = carry
            k = steps - 1 - i
            rows = pl.ds(pl.multiple_of(k * N_SEG, N_SEG), N_SEG)
            prev = pl.ds(pl.multiple_of((k - 1) * N_SEG, N_SEG), N_SEG)
            lam_r = lre[rows, :] + qr
            lam_i = lim[rows, :] + qi
            lre[rows, :] = lam_r
            lim[rows, :] = lam_i
            acc_r, acc_i = accumulate(lam_r, lam_i, hre[prev, :], him[prev, :], acc_r, acc_i)
            return ar * qr - ai * qi, ar * qi + ai * qr, acc_r, acc_i

        qr, qi, acc_r, acc_i = lax.fori_loop(
            0, steps - 1, fix, (ar * c_re - ai * c_im, ar * c_im + ai * c_re, zero, zero))
        first = pl.ds(0, N_SEG)
        last = pl.ds((steps - 1) * N_SEG, N_SEG)
        lam_r = lre[first, :] + qr
        lam_i = lim[first, :] + qi
        lre[first, :] = lam_r
        lim[first, :] = lam_i
        row_id = lax.broadcasted_iota(jnp.int32, (N_SEG, st), 0)
        hp_r = jnp.where(row_id == 0, 0.0, pltpu.roll(hre[last, :], 1, 0))
        hp_i = jnp.where(row_id == 0, 0.0, pltpu.roll(him[last, :], 1, 0))
        acc_r, acc_i = accumulate(lam_r, lam_i, hp_r, hp_i, acc_r, acc_i)

        lam_re16 = lre[...].astype(BF16)
        lam_im16 = lim[...].astype(BF16)
        dza_ref[...] = (_dot(lam_re16, bre_ref[0], tb=True) + _dot(lam_im16, bim_ref[0], tb=True)
                        + skip_ref[...]).astype(BF16)

        @pl.when(pl.program_id(1) == 0)
        def _():
            for ref in (dbre_ref, dbim_ref, dcre_ref, dcim_ref, dare, daim):
                ref[...] = jnp.zeros_like(ref)

        za = za_ref[...]
        dbre_ref[0] += _dot(za, lam_re16, ta=True)
        dbim_ref[0] += _dot(za, lam_im16, ta=True)
        dcre_ref[0] += _dot(hre[...], dy, ta=True)
        dcim_ref[0] += _dot(him[...], dy, ta=True)
        dare[...] += acc_r
        daim[...] += acc_i

    rows_ch = pl.BlockSpec((seq, ch), lambda j, s: (s, j))
    rows_st = pl.BlockSpec((seq, st), lambda j, s: (s, j))
    vec = pl.BlockSpec((1, st), lambda j, s: (0, j))
    b_blk = pl.BlockSpec((1, ch, st), lambda j, s: (j, 0, 0))
    c_blk = pl.BlockSpec((1, st, ch), lambda j, s: (j, 0, 0))
    acc = pl.BlockSpec((N_SEG, st), lambda j, s: (0, j))
    return pl.pallas_call(
        body, name=name,
        out_shape=(jax.ShapeDtypeStruct((t, n_sg * ch), BF16),
                   jax.ShapeDtypeStruct((n_sg, ch, st), F32), jax.ShapeDtypeStruct((n_sg, ch, st), F32),
                   jax.ShapeDtypeStruct((n_sg, st, ch), F32), jax.ShapeDtypeStruct((n_sg, st, ch), F32),
                   jax.ShapeDtypeStruct((N_SEG, n_sg * st), F32), jax.ShapeDtypeStruct((N_SEG, n_sg * st), F32)),
        grid=(n_sg, n_seq),
        in_specs=[rows_ch, rows_ch, rows_st, rows_st, rows_ch, b_blk, b_blk, vec, vec, c_blk, c_blk],
        out_specs=(rows_ch, b_blk, b_blk, c_blk, c_blk, acc, acc),
        scratch_shapes=[pltpu.VMEM((seq, st), F32), pltpu.VMEM((seq, st), F32)],
        compiler_params=_params(2))(dy16, dza_skip, h_re, h_im, za16, b_re, b_im, a_re, a_im, c_re, c_im)


def _s5_post_fwd(yssm, u, d_skip, glu_w, glu_b, name):
    t, w = yssm.shape
    tr = _pick(t, 512, 8)

    def body(y_ref, u_ref, d_ref, w_ref, b_ref, o_ref):
        yg = _gelu(y_ref[...] + d_ref[...] * u_ref[...])
        pre = _dot(yg, w_ref[...]) + b_ref[...]
        o_ref[...] = yg * _sigmoid(pre)

    return pl.pallas_call(
        body, name=name, out_shape=jax.ShapeDtypeStruct((t, w), F32), grid=(t // tr,),
        in_specs=[_rows(tr, w), _rows(tr, w), _whole((1, w)), _whole((w, w)), _whole((1, w))],
        out_specs=_rows(tr, w), compiler_params=_params())(yssm, u, d_skip, glu_w, glu_b)


def _s5_post_bwd(yssm, u, dout, d_skip, glu_w, glu_b, name):
    t, w = yssm.shape
    tr = _pick(t, 512, 8)

    def body(y_ref, u_ref, do_ref, d_ref, w_ref, b_ref, dy_ref, du_ref, dw_ref, dd_ref, db_ref):
        uu = u_ref[...]
        y = y_ref[...] + d_ref[...] * uu
        yg = _gelu(y)
        sg = _sigmoid(_dot(yg, w_ref[...]) + b_ref[...])
        do = do_ref[...]
        dpre = do * yg * sg * (1.0 - sg)
        dyg = do * sg + _dot(dpre, w_ref[...], tb=True)
        dy = dyg * _gelu_grad(y)
        dy_ref[...] = dy.astype(BF16)
        du_ref[...] = dy * d_ref[...]

        @pl.when(pl.program_id(0) == 0)
        def _():
            dw_ref[...] = jnp.zeros_like(dw_ref)
            dd_ref[...] = jnp.zeros_like(dd_ref)
            db_ref[...] = jnp.zeros_like(db_ref)

        dw_ref[...] += _dot(yg, dpre, ta=True)
        dd_ref[...] += _fold8(dy * uu)
        db_ref[...] += _fold8(dpre)

    return pl.pallas_call(
        body, name=name,
        out_shape=(jax.ShapeDtypeStruct((t, w), BF16), jax.ShapeDtypeStruct((t, w), F32),
                   jax.ShapeDtypeStruct((w, w), F32), jax.ShapeDtypeStruct((8, w), F32),
                   jax.ShapeDtypeStruct((8, w), F32)),
        grid=(t // tr,),
        in_specs=[_rows(tr, w), _rows(tr, w), _rows(tr, w), _whole((1, w)), _whole((w, w)),
                  _whole((1, w))],
        out_specs=(_rows(tr, w), _rows(tr, w), _whole((w, w)), _whole((8, w)), _whole((8, w))),
        compiler_params=_params())(yssm, u, dout, d_skip, glu_w, glu_b)


def _head_select(parts, head_dim):
    col_head = lax.broadcasted_iota(jnp.int32, parts[0].shape, 1) // head_dim
    out = jnp.zeros_like(parts[0])
    for h, part in enumerate(parts):
        out = jnp.where(col_head == h, part, out)
    return out


def _gmlp_fwd(proj, ln_g, ln_b, ws, bs_cols, name):
    t = proj.shape[0]
    n_heads = ws.shape[0]
    w = bs_cols.shape[1]
    head_dim = w // n_heads
    cpb = _pick(t // CHUNK, 4, 1)
    tr = cpb * CHUNK

    def body(zu_ref, zv_ref, g_ref, b_ref, ws_ref, bs_ref, o_ref):
        for c in range(cpb):
            rows = pl.ds(c * CHUNK, CHUNK)
            xhat, _ = _ln_stats(_gelu(zv_ref[rows, :]))
            vn = (xhat * g_ref[...] + b_ref[...]).astype(BF16)
            s = _head_select([_dot(ws_ref[h], vn) for h in range(n_heads)], head_dim) + bs_ref[...]
            o_ref[rows, :] = _gelu(zu_ref[rows, :]) * s

    return pl.pallas_call(
        body, name=name, out_shape=jax.ShapeDtypeStruct((t, w), F32), grid=(t // tr,),
        in_specs=[_rows(tr, w, 1), _rows(tr, w, 2), _whole((1, w)), _whole((1, w)),
                  _whole(ws.shape), _whole(bs_cols.shape)],
        out_specs=_rows(tr, w), compiler_params=_params())(proj, proj, ln_g, ln_b, ws, bs_cols)


def _gmlp_bwd(proj, dout, ln_g, ln_b, ws, ws_t, bs_cols, name):
    t = proj.shape[0]
    n_heads = ws.shape[0]
    w = bs_cols.shape[1]
    head_dim = w // n_heads
    cpb = _pick(t // CHUNK, 4, 1)
    tr = cpb * CHUNK

    def body(zu_ref, zv_ref, do_ref, g_ref, b_ref, ws_ref, wst_ref, bs_ref,
             dzu_ref, dzv_ref, dws_ref, dbs_ref, dg_ref, dbeta_ref):
        @pl.when(pl.program_id(0) == 0)
        def _():
            dws_ref[...] = jnp.zeros_like(dws_ref)
            dbs_ref[...] = jnp.zeros_like(dbs_ref)
            dg_ref[...] = jnp.zeros_like(dg_ref)
            dbeta_ref[...] = jnp.zeros_like(dbeta_ref)

        col_head = lax.broadcasted_iota(jnp.int32, (CHUNK, w), 1) // head_dim
        for c in range(cpb):
            rows = pl.ds(c * CHUNK, CHUNK)
            zu, zv, do = zu_ref[rows, :], zv_ref[rows, :], do_ref[rows, :]
            xhat, rstd = _ln_stats(_gelu(zv))
            vn = (xhat * g_ref[...] + b_ref[...]).astype(BF16)
            s = _head_select([_dot(ws_ref[h], vn) for h in range(n_heads)], head_dim) + bs_ref[...]
            dzu_ref[rows, :] = (do * s * _gelu_grad(zu)).astype(BF16)
            ds = do * _gelu(zu)
            ds16 = ds.astype(BF16)
            dbs_ref[...] += ds
            for h in range(n_heads):
                dws_ref[h] += _dot(jnp.where(col_head == h, ds16, jnp.zeros_like(ds16)), vn, tb=True)
            dvn = _head_select([_dot(wst_ref[h], ds16) for h in range(n_heads)], head_dim)
            dg_ref[...] += _fold8(dvn * xhat)
            dbeta_ref[...] += _fold8(dvn)
            dgv = _ln_bwd(dvn, xhat, rstd, g_ref[...])
            dzv_ref[rows, :] = (dgv * _gelu_grad(zv)).astype(BF16)

    return pl.pallas_call(
        body, name=name,
        out_shape=(jax.ShapeDtypeStruct((t, w), BF16), jax.ShapeDtypeStruct((t, w), BF16),
                   jax.ShapeDtypeStruct(ws.shape, F32), jax.ShapeDtypeStruct((CHUNK, w), F32),
                   jax.ShapeDtypeStruct((8, w), F32), jax.ShapeDtypeStruct((8, w), F32)),
        grid=(t // tr,),
        in_specs=[_rows(tr, w, 1), _rows(tr, w, 2), _rows(tr, w), _whole((1, w)), _whole((1, w)),
                  _whole(ws.shape), _whole(ws.shape), _whole(bs_cols.shape)],
        out_specs=(_rows(tr, w), _rows(tr, w), _whole(ws.shape), _whole((CHUNK, w)),
                   _whole((8, w)), _whole((8, w))),
        compiler_params=_params())(proj, proj, dout, ln_g, ln_b, ws, ws_t, bs_cols)


def _merge_fwd(s5out, gm, proj, up_a, up_b, name):
    t, w = s5out.shape
    d = up_a.shape[1]
    assert d == 2 * w
    tr = _pick(t, 512, 8)

    def body(s_ref, gm_ref, ga0, ga1, gb0, gb1, ua_ref, ub_ref, m_ref, ya_ref, yb_ref):
        ya = _dot(s_ref[...], ua_ref[...])
        yb = _dot(gm_ref[...], ub_ref[...])
        ya_ref[...] = ya.astype(BF16)
        yb_ref[...] = yb.astype(BF16)
        for half, (ga, gb) in enumerate(((ga0, gb0), (ga1, gb1))):
            cols = slice(half * w, (half + 1) * w)
            m_ref[:, cols] = (_sigmoid(ga[...]) * ya[:, cols] + _sigmoid(gb[...]) * yb[:, cols]).astype(BF16)

    return pl.pallas_call(
        body, name=name,
        out_shape=(jax.ShapeDtypeStruct((t, d), BF16), jax.ShapeDtypeStruct((t, d), BF16),
                   jax.ShapeDtypeStruct((t, d), BF16)),
        grid=(t // tr,),
        in_specs=[_rows(tr, w), _rows(tr, w), _rows(tr, w, 3), _rows(tr, w, 4), _rows(tr, w, 5),
                  _rows(tr, w, 6), _whole(up_a.shape), _whole(up_b.shape)],
        out_specs=(_rows(tr, d), _rows(tr, d), _rows(tr, d)),
        compiler_params=_params())(s5out, gm, proj, proj, proj, proj, up_a, up_b)


def _merge_bwd(dm, ya, yb, s5out, gm, proj, up_a, up_b, name):
    t, d = dm.shape
    w = d // 2
    tr = _pick(t, 512, 8)

    def body(dm_ref, ya_ref, yb_ref, s_ref, gm_ref, ga0, ga1, gb0, gb1, ua_ref, ub_ref,
             dga_ref, dgb_ref, ds_ref, dgm_ref, dua_ref, dub_ref):
        dm_v, ya, yb = dm_ref[...], ya_ref[...].astype(F32), yb_ref[...].astype(F32)
        dya, dyb = [], []
        for half, (ga, gb) in enumerate(((ga0, gb0), (ga1, gb1))):
            cols = slice(half * w, (half + 1) * w)
            sa, sb = _sigmoid(ga[...]), _sigmoid(gb[...])
            dmh = dm_v[:, cols]
            dga_ref[:, cols] = (dmh * ya[:, cols] * sa * (1.0 - sa)).astype(BF16)
            dgb_ref[:, cols] = (dmh * yb[:, cols] * sb * (1.0 - sb)).astype(BF16)
            dya.append((dmh * sa).astype(BF16))
            dyb.append((dmh * sb).astype(BF16))
        dya = jnp.concatenate(dya, axis=1)
        dyb = jnp.concatenate(dyb, axis=1)
        ds_ref[...] = _dot(dya, ua_ref[...], tb=True)
        dgm_ref[...] = _dot(dyb, ub_ref[...], tb=True)

        @pl.when(pl.program_id(0) == 0)
        def _():
            dua_ref[...] = jnp.zeros_like(dua_ref)
            dub_ref[...] = jnp.zeros_like(dub_ref)

        dua_ref[...] += _dot(s_ref[...], dya, ta=True)
        dub_ref[...] += _dot(gm_ref[...], dyb, ta=True)

    return pl.pallas_call(
        body, name=name,
        out_shape=(jax.ShapeDtypeStruct((t, d), BF16), jax.ShapeDtypeStruct((t, d), BF16),
                   jax.ShapeDtypeStruct((t, w), F32), jax.ShapeDtypeStruct((t, w), F32),
                   jax.ShapeDtypeStruct(up_a.shape, F32), jax.ShapeDtypeStruct(up_b.shape, F32)),
        grid=(t // tr,),
        in_specs=[_rows(tr, d), _rows(tr, d), _rows(tr, d), _rows(tr, w), _rows(tr, w),
                  _rows(tr, w, 3), _rows(tr, w, 4), _rows(tr, w, 5), _rows(tr, w, 6),
                  _whole(up_a.shape), _whole(up_b.shape)],
        out_specs=(_rows(tr, d), _rows(tr, d), _rows(tr, w), _rows(tr, w), _whole(up_a.shape),
                   _whole(up_b.shape)),
        compiler_params=_params())(dm, ya, yb, s5out, gm, proj, proj, proj, proj, up_a, up_b)


def _head(x3, r3, ln_g, p, target, w_gate, w_proj, out_scale, name):
    t, d = x3.shape
    pd = p.shape[1]
    tr = _pick(t, 512, 8)
    nb = t // tr

    def body(x_ref, r_ref, g_ref, p_ref, t_ref, wg_ref, wp_ref,
             loss_ref, dr_ref, drs_ref, dwg_ref, dwp_ref, dg_ref, dbeta_ref):
        xv = x_ref[...]
        sg = _sigmoid(_dot(xv, wg_ref[...]))
        pp = _dot(p_ref[...], wp_ref[...])
        err = xv + sg * pp - t_ref[...]
        loss_ref[...] = jnp.full((8, 128), 0.5 * jnp.sum(jnp.mean(err * err, axis=-1)), F32)
        dout = err * (1.0 / d)
        dgpre = dout * pp * sg * (1.0 - sg)
        dpp = dout * sg
        dx = dout + _dot(dgpre, wg_ref[...], tb=True)
        xhat, rstd = _ln_stats(r_ref[...])
        dr = _ln_bwd(dx, xhat, rstd, g_ref[...])
        dr_ref[...] = dr
        drs_ref[...] = (out_scale * dr).astype(BF16)

        @pl.when(pl.program_id(0) == 0)
        def _():
            for ref in (dwg_ref, dwp_ref, dg_ref, dbeta_ref):
                ref[...] = jnp.zeros_like(ref)

        dwg_ref[...] += _dot(xv, dgpre, ta=True)
        dwp_ref[...] += _dot(p_ref[...], dpp, ta=True)
        dg_ref[...] += _fold8(dx * xhat)
        dbeta_ref[...] += _fold8(dx)

    return pl.pallas_call(
        body, name=name,
        out_shape=(jax.ShapeDtypeStruct((nb * 8, 128), F32), jax.ShapeDtypeStruct((t, d), F32),
                   jax.ShapeDtypeStruct((t, d), BF16), jax.ShapeDtypeStruct((d, d), F32),
                   jax.ShapeDtypeStruct((pd, d), F32), jax.ShapeDtypeStruct((8, d), F32),
                   jax.ShapeDtypeStruct((8, d), F32)),
        grid=(nb,),
        in_specs=[_rows(tr, d), _rows(tr, d), _whole((1, d)), _rows(tr, pd), _rows(tr, d), _whole((d, d)),
                  _whole((pd, d))],
        out_specs=(pl.BlockSpec((8, 128), lambda i: (i, 0)), _rows(tr, d), _rows(tr, d), _whole((d, d)),
                   _whole((pd, d)), _whole((8, d)), _whole((8, d))),
        compiler_params=_params())(x3, r3, ln_g, p, target, w_gate, w_proj)


_HBM = pl.BlockSpec(memory_space=pltpu.HBM)


_SEM = pl.BlockSpec(memory_space=pltpu.SEMAPHORE)
_ANY = pl.BlockSpec(memory_space=pl.ANY)
_DATAFLOW = pltpu.SideEffectType.DATAFLOW_SIDE_EFFECTING


def _peer(k, x, y, c):
    return (1 - x if k & 4 else x, 1 - y if k & 2 else y, 1 - c if k & 1 else c)


def _exchange_start(sends, after, name):
    n = len(sends)
    lands = [lax.empty((N_DEV,) + s.shape[1:], s.dtype) for s in sends]

    def body(*refs):
        s_refs, l_refs = refs[:n], refs[n:2 * n]
        send_sems, recv_sems, local_sems, token = refs[2 * n + 1], refs[2 * n + 2], refs[2 * n + 3], refs[-1]
        x, y, c = lax.axis_index("x"), lax.axis_index("y"), lax.axis_index("c")
        me = 4 * x + 2 * y + c
        for a in range(n):
            same = sends[a].shape[0] == 1
            pltpu.make_async_copy(s_refs[a].at[0 if same else me], l_refs[a].at[me], local_sems.at[a]).start()
            for k in range(1, N_DEV):
                px, py, pc = _peer(k, x, y, c)
                pltpu.make_async_remote_copy(
                    src_ref=s_refs[a].at[0 if same else 4 * px + 2 * py + pc], dst_ref=l_refs[a].at[me],
                    send_sem=send_sems.at[7 * a + k - 1], recv_sem=recv_sems.at[7 * a + k - 1],
                    device_id=(px, py, pc), device_id_type=pl.DeviceIdType.MESH).start()
        token[...] = jnp.zeros_like(token)

    outs = pl.pallas_call(
        body, name=name,
        out_shape=(pltpu.SemaphoreType.DMA((7 * n,)), pltpu.SemaphoreType.DMA((7 * n,)),
                   pltpu.SemaphoreType.DMA((n,)),
                   *[pltpu.HBM(s.shape, s.dtype) for s in sends],
                   *[pltpu.HBM(l.shape, l.dtype) for l in lands],
                   jax.ShapeDtypeStruct((8, 128), F32)),
        in_specs=[_HBM] * (2 * n) + [_ANY],
        out_specs=(_SEM, _SEM, _SEM, *([_HBM] * (2 * n)), pl.BlockSpec(memory_space=pltpu.VMEM)),
        input_output_aliases={i: 3 + i for i in range(2 * n)},
        compiler_params=pltpu.CompilerParams(has_side_effects=_DATAFLOW),
    )(*[pltpu.with_memory_space_constraint(v, pltpu.HBM) for v in list(sends) + lands], after)
    return (outs[0], outs[1], outs[2], list(outs[3:3 + n]), list(outs[3 + n:3 + 2 * n])), outs[-1]


def _exchange_wait(handle, after, name):
    send_sems, recv_sems, local_sems, sends, lands = handle
    n = len(sends)

    def body(*refs):
        s_refs, l_refs = refs[:n], refs[n:2 * n]
        send_sems, recv_sems, local_sems = refs[2 * n], refs[2 * n + 1], refs[2 * n + 2]
        x, y, c = lax.axis_index("x"), lax.axis_index("y"), lax.axis_index("c")
        for a in range(n):
            pltpu.make_async_copy(s_refs[a].at[0], l_refs[a].at[0], local_sems.at[a]).wait()
            for k in range(1, N_DEV):
                copy = pltpu.make_async_remote_copy(
                    src_ref=s_refs[a].at[0], dst_ref=l_refs[a].at[0],
                    send_sem=send_sems.at[7 * a + k - 1], recv_sem=recv_sems.at[7 * a + k - 1],
                    device_id=_peer(k, x, y, c), device_id_type=pl.DeviceIdType.MESH)
                copy.wait_send()
                copy.wait_recv()

    outs = pl.pallas_call(
        body, name=name,
        out_shape=(*[pltpu.HBM(s.shape, s.dtype) for s in sends], *[pltpu.HBM(l.shape, l.dtype) for l in lands]),
        in_specs=[_HBM] * (2 * n) + [_SEM, _SEM, _SEM] + [_ANY] * len(after),
        out_specs=tuple([_HBM] * (2 * n)),
        input_output_aliases={i: i for i in range(2 * n)},
        compiler_params=pltpu.CompilerParams(has_side_effects=_DATAFLOW),
    )(*sends, *lands, send_sems, recv_sems, local_sems, *after)
    return list(outs[n:])


def _chips_of(x, y):
    return [(1 - x, y), (x, 1 - y), (1 - x, 1 - y)]


def _gather2_start(shards, after, name):
    n = len(shards)
    lands = [lax.empty((N_DEV,) + s.shape, s.dtype) for s in shards]

    def body(*refs):
        x_refs, l_refs = refs[:n], refs[n:2 * n]
        send_sems, recv_sems, local_sems, token = refs[2 * n + 1], refs[2 * n + 2], refs[2 * n + 3], refs[-1]
        x, y, c = lax.axis_index("x"), lax.axis_index("y"), lax.axis_index("c")
        me = 4 * x + 2 * y + c
        peers = [(x, y, 1 - c)] + [(*chip, c) for chip in _chips_of(x, y)]
        for a in range(n):
            pltpu.make_async_copy(x_refs[a], l_refs[a].at[me], local_sems.at[a]).start()
            for k, peer in enumerate(peers):
                pltpu.make_async_remote_copy(
                    src_ref=x_refs[a], dst_ref=l_refs[a].at[me],
                    send_sem=send_sems.at[4 * a + k], recv_sem=recv_sems.at[4 * a + k],
                    device_id=peer, device_id_type=pl.DeviceIdType.MESH).start()
        token[...] = jnp.zeros_like(token)

    outs = pl.pallas_call(
        body, name=name,
        out_shape=(pltpu.SemaphoreType.DMA((4 * n,)), pltpu.SemaphoreType.DMA((4 * n,)),
                   pltpu.SemaphoreType.DMA((n,)),
                   *[pltpu.HBM(s.shape, s.dtype) for s in shards],
                   *[pltpu.HBM(l.shape, l.dtype) for l in lands],
                   jax.ShapeDtypeStruct((8, 128), F32)),
        in_specs=[_HBM] * (2 * n) + [_ANY],
        out_specs=(_SEM, _SEM, _SEM, *([_HBM] * (2 * n)), pl.BlockSpec(memory_space=pltpu.VMEM)),
        input_output_aliases={i: 3 + i for i in range(2 * n)},
        compiler_params=pltpu.CompilerParams(has_side_effects=_DATAFLOW),
    )(*[pltpu.with_memory_space_constraint(v, pltpu.HBM) for v in list(shards) + lands], after)
    return (outs[0], outs[1], outs[2], list(outs[3:3 + n]), list(outs[3 + n:3 + 2 * n])), outs[-1]


def _gather2_forward(handle, after, name):
    send_sems, recv_sems, local_sems, shards, lands = handle
    n = len(shards)

    def body(*refs):
        x_refs, l_refs = refs[:n], refs[n:2 * n]
        send_sems, recv_sems = refs[2 * n], refs[2 * n + 1]
        out0 = 2 * n + 2 + len(after)
        fwd_send, fwd_recv, token = refs[out0], refs[out0 + 1], refs[-1]
        x, y, c = lax.axis_index("x"), lax.axis_index("y"), lax.axis_index("c")
        for a in range(n):
            for j, chip in enumerate(_chips_of(x, y)):
                block = l_refs[a].at[4 * chip[0] + 2 * chip[1] + c]
                pltpu.make_async_remote_copy(
                    src_ref=x_refs[a], dst_ref=block, send_sem=send_sems.at[4 * a + 1 + j],
                    recv_sem=recv_sems.at[4 * a + 1 + j], device_id=(*chip, c),
                    device_id_type=pl.DeviceIdType.MESH).wait_recv()
                pltpu.make_async_remote_copy(
                    src_ref=block, dst_ref=block, send_sem=fwd_send.at[3 * a + j], recv_sem=fwd_recv.at[3 * a + j],
                    device_id=(x, y, 1 - c), device_id_type=pl.DeviceIdType.MESH).start()
        token[...] = jnp.zeros_like(token)

    outs = pl.pallas_call(
        body, name=name,
        out_shape=(pltpu.SemaphoreType.DMA((3 * n,)), pltpu.SemaphoreType.DMA((3 * n,)),
                   *[pltpu.HBM(s.shape, s.dtype) for s in shards], *[pltpu.HBM(l.shape, l.dtype) for l in lands],
                   jax.ShapeDtypeStruct((8, 128), F32)),
        in_specs=[_HBM] * (2 * n) + [_SEM, _SEM] + [_ANY] * len(after),
        out_specs=(_SEM, _SEM, *([_HBM] * (2 * n)), pl.BlockSpec(memory_space=pltpu.VMEM)),
        input_output_aliases={i: 2 + i for i in range(2 * n)},
        compiler_params=pltpu.CompilerParams(has_side_effects=_DATAFLOW),
    )(*shards, *lands, send_sems, recv_sems, *after)
    handle = (send_sems, recv_sems, local_sems, outs[0], outs[1], list(outs[2:2 + n]), list(outs[2 + n:2 + 2 * n]))
    return handle, outs[-1]


def _gather2_wait(handle, after, name):
    send_sems, recv_sems, local_sems, fwd_send, fwd_recv, shards, lands = handle
    n = len(shards)

    def body(*refs):
        x_refs, l_refs = refs[:n], refs[n:2 * n]
        send_sems, recv_sems, local_sems, fwd_send, fwd_recv = refs[2 * n:2 * n + 5]
        x, y, c = lax.axis_index("x"), lax.axis_index("y"), lax.axis_index("c")
        sibling = (x, y, 1 - c)

        def copy(a, send_sem, recv_sem):
            return pltpu.make_async_remote_copy(
                src_ref=x_refs[a], dst_ref=l_refs[a].at[0], send_sem=send_sem, recv_sem=recv_sem,
                device_id=sibling, device_id_type=pl.DeviceIdType.MESH)

        for a in range(n):
            pltpu.make_async_copy(x_refs[a], l_refs[a].at[0], local_sems.at[a]).wait()
            for k in range(4):
                copy(a, send_sems.at[4 * a + k], recv_sems.at[4 * a + k]).wait_send()
            copy(a, send_sems.at[4 * a], recv_sems.at[4 * a]).wait_recv()
            for j in range(3):
                passed = copy(a, fwd_send.at[3 * a + j], fwd_recv.at[3 * a + j])
                passed.wait_send()
                passed.wait_recv()

    outs = pl.pallas_call(
        body, name=name,
        out_shape=(*[pltpu.HBM(s.shape, s.dtype) for s in shards], *[pltpu.HBM(l.shape, l.dtype) for l in lands]),
        in_specs=[_HBM] * (2 * n) + [_SEM] * 5 + [_ANY] * len(after),
        out_specs=tuple([_HBM] * (2 * n)),
        input_output_aliases={i: i for i in range(2 * n)},
        compiler_params=pltpu.CompilerParams(has_side_effects=_DATAFLOW),
    )(*shards, *lands, send_sems, recv_sems, local_sems, fwd_send, fwd_recv, *after)
    return list(outs[n:])


def _adamw(recv, w, m, v, name):
    n, rows, width = recv.shape
    tr = _pick(rows, max(8, ADAM_BLOCK_BYTES // (n * width * recv.dtype.itemsize)), 8)
    c_m = 1.0 - ADAM_B1 ** ADAM_STEP
    c_v = 1.0 - ADAM_B2 ** ADAM_STEP

    def body(r_ref, w_ref, m_ref, v_ref, g_ref, d_ref, nm_ref, nv_ref):
        g = r_ref[0].astype(F32)
        for i in range(1, n):
            g = g + r_ref[i].astype(F32)
        m2 = ADAM_B1 * m_ref[...] + (1.0 - ADAM_B1) * g
        v2 = ADAM_B2 * v_ref[...] + (1.0 - ADAM_B2) * (g * g)
        m_hat = m2 / c_m
        v_hat = v2 / c_v
        g_ref[...] = g
        d_ref[...] = -ADAM_LR * (m_hat / (jnp.sqrt(v_hat) + ADAM_EPS) + ADAM_WD * w_ref[...])
        nm_ref[...] = m2
        nv_ref[...] = v2

    blk = _rows(tr, width)
    shp = jax.ShapeDtypeStruct((rows, width), F32)
    return pl.pallas_call(
        body, name=name, out_shape=(shp, shp, shp, shp), grid=(rows // tr,),
        in_specs=[pl.BlockSpec((n, tr, width), lambda i: (0, i, 0)), blk, blk, blk],
        out_specs=(blk, blk, blk, blk), compiler_params=_params())(recv, w, m, v)


def _join_cols(gathered):
    n, r, c = gathered.shape
    return gathered.transpose(1, 0, 2).reshape(r, n * c)


def _split_cols(full):
    r, c = full.shape
    return full.reshape(r, N_DEV, c // N_DEV).transpose(1, 0, 2)


def _adamw_small(items, name):
    n = len(items)
    c_m = 1.0 - ADAM_B1 ** ADAM_STEP
    c_v = 1.0 - ADAM_B2 ** ADAM_STEP

    def body(*refs):
        ins, outs = refs[:4 * n], refs[4 * n:]
        for k in range(n):
            r_ref, w_ref, m_ref, v_ref = ins[4 * k:4 * k + 4]
            g = r_ref[0].astype(F32)
            for i in range(1, N_DEV):
                g = g + r_ref[i].astype(F32)
            m2 = ADAM_B1 * m_ref[...] + (1.0 - ADAM_B1) * g
            v2 = ADAM_B2 * v_ref[...] + (1.0 - ADAM_B2) * (g * g)
            outs[4 * k][...] = g
            outs[4 * k + 1][...] = -ADAM_LR * ((m2 / c_m) / (jnp.sqrt(v2 / c_v) + ADAM_EPS) + ADAM_WD * w_ref[...])
            outs[4 * k + 2][...] = m2
            outs[4 * k + 3][...] = v2

    vmem = pl.BlockSpec(memory_space=pltpu.VMEM)
    flat = pl.pallas_call(
        body, name=name,
        out_shape=tuple(jax.ShapeDtypeStruct(w.shape, F32) for _, w, _, _ in items for _ in range(4)),
        in_specs=[vmem] * (4 * n), out_specs=tuple([vmem] * (4 * n)),
        compiler_params=pltpu.CompilerParams(vmem_limit_bytes=VMEM_LIMIT_BYTES),
    )(*[a for item in items for a in item])
    return [tuple(flat[4 * k:4 * k + 4]) for k in range(n)]


def _discretise(lam_re, lam_im, log_dt, b_re, b_im):
    dt = jnp.exp(log_dt)
    mag = jnp.exp(lam_re * dt)
    ab_re = mag * jnp.cos(lam_im * dt)
    ab_im = mag * jnp.sin(lam_im * dt)
    nr = ab_re - 1.0
    ni = ab_im
    den = lam_re * lam_re + lam_im * lam_im
    coef_re = (nr * lam_re + ni * lam_im) / den
    coef_im = (ni * lam_re - nr * lam_im) / den
    return ab_re, ab_im, coef_re * b_re - coef_im * b_im, coef_re * b_im + coef_im * b_re


def _s5_discretise(operands, name, token):
    def body(*refs):
        for out_ref, v in zip(refs[6:], _discretise(*[r[...] for r in refs[:5]])):
            out_ref[...] = v

    vmem = pl.BlockSpec(memory_space=pltpu.VMEM)
    shape = jax.ShapeDtypeStruct(operands[0].shape, F32)
    return pl.pallas_call(body, name=name, out_shape=(shape,) * 4, in_specs=[vmem] * 6,
                          out_specs=(vmem,) * 4)(*operands, token)


def _s5_discretise_bwd(operands, cotangents, name):
    def body(*refs):
        _, vjp = jax.vjp(_discretise, *[r[...] for r in refs[:5]])
        for out_ref, v in zip(refs[9:], vjp(tuple(r[...] for r in refs[5:9]))):
            out_ref[...] = v

    vmem = pl.BlockSpec(memory_space=pltpu.VMEM)
    shape = jax.ShapeDtypeStruct(operands[0].shape, F32)
    return pl.pallas_call(body, name=name, out_shape=(shape,) * 5, in_specs=[vmem] * 9,
                          out_specs=(vmem,) * 5)(*operands, *cotangents)


def _same_group(gl):
    return jnp.eye(gl, dtype=bool)[None, :, None, :, None]


def _super_groups_in(bb, gl):
    g, p, i = bb.shape
    blocks = bb.reshape(g // gl, gl, p, i).transpose(0, 1, 3, 2)[:, :, :, None, :]
    return jnp.where(_same_group(gl), blocks, 0.0).reshape(g // gl, gl * i, gl * p)


def _super_groups_in_take(dense, gl, p, i):
    n_sg = dense.shape[0]
    blocks = jnp.where(_same_group(gl), dense.reshape(n_sg, gl, i, gl, p), 0.0).sum(axis=3)
    return blocks.transpose(0, 1, 3, 2).reshape(n_sg * gl, p, i)


def _super_groups_out(cc, gl):
    g, i, p = cc.shape
    blocks = cc.reshape(g // gl, gl, i, p).transpose(0, 1, 3, 2)[:, :, :, None, :]
    return jnp.where(_same_group(gl), blocks, 0.0).reshape(g // gl, gl * p, gl * i)


def _super_groups_out_take(dense, gl, i, p):
    n_sg = dense.shape[0]
    blocks = jnp.where(_same_group(gl), dense.reshape(n_sg, gl, p, gl, i), 0.0).sum(axis=3)
    return blocks.transpose(0, 1, 3, 2).reshape(n_sg * gl, i, p)


def _perm_rows(z, n_seq):
    t, w = z.shape
    steps = t // n_seq // N_SEG
    return z.reshape(n_seq, N_SEG, steps, w).transpose(0, 2, 1, 3).reshape(t, w)


def _unperm_rows(z, n_seq):
    t, w = z.shape
    steps = t // n_seq // N_SEG
    return z.reshape(n_seq, steps, N_SEG, w).transpose(0, 2, 1, 3).reshape(t, w)


def kernel(*args):
    assert len(args) == len(INPUT_NAMES)
    inp = dict(zip(INPUT_NAMES, args))
    depth = inp["ffn1_w_in"].shape[0]
    assert depth == 1
    alpha = (2.0 * depth) ** 0.25

    n_seq, seq, d_model = inp["x"].shape
    t = n_seq * seq
    x = inp["x"].reshape(t, d_model)
    target = inp["loss_target"].reshape(t, d_model)
    wts = {n: inp[n][0] if n in SHARDED else inp[n] for n in WEIGHTS}
    mom = {n: inp["m_" + n][0] if n in SHARDED else inp["m_" + n] for n in WEIGHTS}
    var = {n: inp["v_" + n][0] if n in SHARDED else inp["v_" + n] for n in WEIGHTS}

    full = {}

    def place(n, g):
        if n in ("ffn1_w_in", "ffn2_w_in"):
            full[n] = g.reshape((2, N_DEV // 2) + g.shape[1:])
        elif n in ("ffn1_w_out", "ffn2_w_out"):
            full[n] = g.reshape(N_DEV // 2, 2 * g.shape[1], g.shape[2])
        elif n in COL_SHARDED:
            full[n] = _join_cols(g)
        else:
            full[n] = g.reshape(N_DEV * g.shape[1], g.shape[2])

    w16 = dict(zip(GATHER_FIRST, _to_bf16([wts[n] for n in GATHER_FIRST], "cast_ffn1")))
    gather_first, gather_first_token = _gather2_start([w16[n] for n in GATHER_FIRST], w16[GATHER_FIRST[0]],
                                                      "gather_first_start")
    later = tuple(n for n in SHARDED if n not in GATHER_FIRST)
    casts = _to_bf16([wts[n] for n in later] + [x, inp["p"][0].reshape(t, -1)], "cast_rest",
                     token=gather_first_token)
    w16.update(zip(later, casts))
    x16, p16 = casts[-2:]

    def gather_start(names, after, name):
        return _exchange_start([w16[n][None] for n in names], after, name)

    def gather_wait(names, handle, after, name):
        for n, g in zip(names, _exchange_wait(handle, after, name)):
            place(n, g)

    def gather2_wait(names, handle, after, name):
        for n, g in zip(names, _gather2_wait(handle, after, name)):
            place(n, g)


    def row(v):
        return v.reshape(1, -1)

    _, n_groups, n_state = wts["ssm_lambda_re"].shape
    n_ch = wts["ssm_b_re"].shape[3]
    disc_in = (jnp.repeat(wts["ssm_lambda_re"][0], n_ch, axis=1), jnp.repeat(wts["ssm_lambda_im"][0], n_ch, axis=1),
               jnp.broadcast_to(wts["ssm_log_dt"][0][:, None], (n_groups, n_state * n_ch)),
               wts["ssm_b_re"][0].reshape(n_groups, -1), wts["ssm_b_im"][0].reshape(n_groups, -1))
    ab_re, ab_im, bb_re, bb_im = _s5_discretise(disc_in, "s5_discretise", gather_first_token)
    a_re, a_im = row(ab_re[:, ::n_ch]), row(ab_im[:, ::n_ch])
    bb_re, bb_im = bb_re.reshape(n_groups, n_state, n_ch), bb_im.reshape(n_groups, n_state, n_ch)
    gl = S5_CHANNELS_PER_STEP // n_ch
    b_sg_re = _super_groups_in(bb_re, gl).astype(BF16)
    b_sg_im = _super_groups_in(bb_im, gl).astype(BF16)
    c_sg_re = _super_groups_out(wts["ssm_c_re"][0], gl).astype(BF16)
    c_sg_im = _super_groups_out(-wts["ssm_c_im"][0], gl).astype(BF16)

    ws = wts["gmlp_w_s"][0]
    n_heads = ws.shape[0]
    d_gmlp = wts["gmlp_ln_g"].shape[1]
    causal = jnp.tril(jnp.ones((CHUNK, CHUNK), dtype=bool))
    ws_masked = jnp.where(causal[None], ws, 0.0).astype(BF16)
    ws_masked_t = ws_masked.transpose(0, 2, 1)
    bs_cols = jnp.repeat(wts["gmlp_b_s"][0].T, d_gmlp // n_heads, axis=1)
    d_ssm = n_groups * n_ch
    assert d_ssm == d_gmlp and d_model == 2 * d_ssm

    prepared = [x16, b_sg_re, b_sg_im, c_sg_re, c_sg_im, ws_masked_t, bs_cols]
    gather_first, _ = _gather2_forward(gather_first, prepared, "gather_first_forward")
    gather2_wait(GATHER_FIRST, gather_first, prepared, "gather_first_wait")
    gather_early, gather_early_token = _gather2_start([w16[n] for n in GATHER_EARLY], full["ffn1_w_in"],
                                                      "gather_early_start")
    h1, a1 = _ffn_in(x16, full["ffn1_w_in"], "ffn1_in", token=gather_early_token)
    gather_early, _ = _gather2_forward(gather_early, [a1], "gather_early_forward")
    gather2_wait(GATHER_EARLY, gather_early, [a1], "gather_early_wait")
    r1, x1, x1_16 = _ffn_out_ln(a1, full["ffn1_w_out"], x, row(wts["ln1_g"]), row(wts["ln1_b"]), alpha,
                                "ffn1_out_ln")

    gather_mid, token = gather_start(GATHER_MID, x1_16, "gather_mid_start")
    gather_last, token = _gather2_start([w16[n] for n in GATHER_LAST], token, "gather_last_start")
    proj = _mm(x1_16, full["mix_w_in"], name="mix_in", token=token)
    za_p = _perm_rows(proj[:, :d_ssm], n_seq)
    h_re, h_im, yssm = _s5_fwd(za_p, b_sg_re, b_sg_im, a_re, a_im, c_sg_re, c_sg_im, n_seq, "s5_fwd")
    gather_wait(GATHER_MID, gather_mid, [yssm], "gather_mid_wait")
    s5out_p = _s5_post_fwd(yssm, za_p, row(wts["ssm_d"]), full["ssm_glu_w"], row(wts["ssm_glu_b"]),
                           "s5_post")
    s5out = _unperm_rows(s5out_p, n_seq)
    gm = _gmlp_fwd(proj, row(wts["gmlp_ln_g"]), row(wts["gmlp_ln_b"]), ws_masked, bs_cols, "gmlp")
    m_mix, y_a, y_b = _merge_fwd(s5out, gm, proj, full["up_a"], full["up_b"], "merge")
    gather_last, token = _gather2_forward(gather_last, [m_mix], "gather_last_forward")
    r2, x2, x2_16 = _ffn_out_ln(m_mix[None], full["mix_w_out"][None], x1, row(wts["ln2_g"]), row(wts["ln2_b"]),
                                alpha, "mix_out_ln2", scale=1.0, token=token)

    gather2_wait(GATHER_LAST, gather_last, [x2_16], "gather_last_wait")
    h2, a2 = _ffn_in(x2_16, full["ffn2_w_in"], "ffn2_in")
    r3, x3, _ = _ffn_out_ln(a2, full["ffn2_w_out"], x2, row(wts["ln3_g"]), row(wts["ln3_b"]), alpha,
                            "ffn2_out_ln")

    small = {}
    send = {}

    def lane_dense(n, v):
        return v.reshape(v.shape[:2] + (-1,)) if n in ("ssm_b_re", "ssm_b_im") else v

    def on_wire(n, g):
        g = lane_dense(n, g)
        return (g.astype(BF16) if n in SMALL_BF16 else g)[None]
    loss_parts, dr3, dr3_h, dw_gate, dw_proj, dg, db = _head(
        x3, r3, row(wts["ln3_g"]), p16, target, full["ple_w_gate"], full["ple_w_proj"], 0.5, "head")
    loss_mine = jnp.full((1, 1, 8, 128), jnp.sum(loss_parts[::8, 0]), F32)
    send["ple_w_gate"] = dw_gate.astype(BF16).reshape(N_DEV, -1, d_model)
    send["ple_w_proj"] = _split_cols(dw_proj.astype(BF16))
    small["ln3_g"], small["ln3_b"] = dg.sum(0, keepdims=True), db.sum(0, keepdims=True)
    dh2 = _ffn_out_dx(dr3_h, full["ffn2_w_out"], h2, "ffn2_out_dx")
    dw = _ffn_out_dw(a2, dr3_h, "ffn2_out_dw")
    send["ffn2_w_out"] = dw.reshape(N_DEV, -1, d_model)
    dw = _ffn_in_dw(x2_16, dh2, "ffn2_in_dw")
    send["ffn2_w_in"] = dw.reshape((N_DEV,) + dw.shape[2:])
    group1 = ("ffn2_w_in", "ffn2_w_out", "ple_w_gate", "ple_w_proj")
    exchange1, token = _exchange_start(
        [send[n] for n in group1] + [on_wire(n, small[n]) for n in SMALL_HEAD] + [loss_mine], dh2,
        "grad_exchange1_start")
    dr2, dr2_h, dg, db = _ffn_in_dx_ln(dh2, full["ffn2_w_in"], r2, dr3, alpha, row(wts["ln2_g"]), 1.0,
                                       "ffn2_in_dx_ln2_bwd", token=token)
    small["ln2_g"], small["ln2_b"] = dg.sum(0, keepdims=True), db.sum(0, keepdims=True)
    dm = _mm(dr2_h, full["mix_w_out"], tb=True, name="mix_out_dx")
    send["mix_w_out"] = _mm(m_mix, dr2_h, ta=True, name="mix_out_dw", out_dtype=BF16).reshape(
        N_DEV, -1, d_model)
    dga, dgb, ds5out, dgm, dw_a, dw_b = _merge_bwd(
        dm, y_a, y_b, s5out, gm, proj, full["up_a"], full["up_b"], "merge_bwd")
    send["up_a"] = _split_cols(dw_a.astype(BF16))
    send["up_b"] = _split_cols(dw_b.astype(BF16))

    dzu, dzv, dws, dbs_cols, dg, db = _gmlp_bwd(
        proj, dgm, row(wts["gmlp_ln_g"]), row(wts["gmlp_ln_b"]), ws_masked, ws_masked_t, bs_cols,
        "gmlp_bwd")
    small["gmlp_ln_g"], small["gmlp_ln_b"] = dg.sum(0, keepdims=True), db.sum(0, keepdims=True)
    small["gmlp_w_s"] = jnp.where(causal[None], dws, 0.0)[None]
    small["gmlp_b_s"] = dbs_cols.reshape(CHUNK, n_heads, -1).sum(-1).T[None]

    dy_p, dza_skip, dw_glu, dd, dgb_glu = _s5_post_bwd(
        yssm, za_p, _perm_rows(ds5out, n_seq), row(wts["ssm_d"]), full["ssm_glu_w"], row(wts["ssm_glu_b"]),
        "s5_post_bwd")
    send["ssm_glu_w"] = dw_glu.astype(BF16).reshape(N_DEV, -1, d_ssm)
    small["ssm_d"], small["ssm_glu_b"] = dd.sum(0, keepdims=True), dgb_glu.sum(0, keepdims=True)
    dza_p, dbb_re, dbb_im, dc_re, dc_im, da_re, da_im = _s5_bwd(
        dy_p, dza_skip, h_re, h_im, za_p, b_sg_re, b_sg_im, a_re, a_im, c_sg_re, c_sg_im, n_seq, "s5_bwd")
    small["ssm_c_re"] = _super_groups_out_take(dc_re, gl, n_ch, n_state)[None]
    small["ssm_c_im"] = -_super_groups_out_take(dc_im, gl, n_ch, n_state)[None]
    dbb_re = _super_groups_in_take(dbb_re, gl, n_state, n_ch)
    dbb_im = _super_groups_in_take(dbb_im, gl, n_state, n_ch)
    def first_channel(v):
        placed = jnp.pad(v.sum(0).reshape(n_groups, n_state, 1), ((0, 0), (0, 0), (0, n_ch - 1)))
        return placed.reshape(n_groups, -1)

    cotangents = (first_channel(da_re), first_channel(da_im), dbb_re.reshape(n_groups, -1),
                  dbb_im.reshape(n_groups, -1))
    d_lre, d_lim, d_ldt, d_bre, d_bim = _s5_discretise_bwd(disc_in, cotangents, "s5_discretise_bwd")
    per_state = (n_groups, n_state, n_ch)
    small["ssm_lambda_re"] = d_lre.reshape(per_state).sum(-1)[None]
    small["ssm_lambda_im"] = d_lim.reshape(per_state).sum(-1)[None]
    small["ssm_log_dt"] = d_ldt.sum(-1)[None]
    small["ssm_b_re"] = d_bre.reshape((1,) + per_state)
    small["ssm_b_im"] = d_bim.reshape((1,) + per_state)

    dproj = jnp.concatenate([_unperm_rows(dza_p, n_seq), dzu, dzv, dga, dgb], axis=1)
    group2 = ("mix_w_out", "up_a", "up_b", "ssm_glu_w")
    exchange2, token = _exchange_start(
        [send[n] for n in group2] + [on_wire(n, small[n]) for n in SMALL_MID], dproj, "grad_exchange2_start")
    send["mix_w_in"] = _split_cols(_mm(x1_16, dproj, ta=True, name="mix_in_dw", out_dtype=BF16, token=token))
    dr1, dr1_h, dg, db = _mm_ln_bwd(dproj, full["mix_w_in"], r1, dr2, alpha, row(wts["ln1_g"]), 0.5,
                                    "mix_in_dx_ln1_bwd", token=token)
    small["ln1_g"], small["ln1_b"] = dg.sum(0, keepdims=True), db.sum(0, keepdims=True)
    dw = _ffn_out_dw(a1, dr1_h, "ffn1_out_dw")
    send["ffn1_w_out"] = dw.reshape(N_DEV, -1, d_model)
    exchange4, token = _exchange_start(
        [send["mix_w_in"], send["ffn1_w_out"]] + [on_wire(n, small[n]) for n in SMALL_LATE], dr1_h,
        "grad_exchange4_start")
    dh1 = _ffn_out_dx(dr1_h, full["ffn1_w_out"], h1, "ffn1_out_dx", token=token)
    dw = _ffn_in_dw(x16, dh1, "ffn1_in_dw")
    send["ffn1_w_in"] = dw.reshape((N_DEV,) + dw.shape[2:])
    exchange5, token = _exchange_start([send["ffn1_w_in"]], dh1, "grad_exchange5_start")
    grad_x = _ffn_in_dx(dh1, full["ffn1_w_in"], dr1, alpha, "ffn1_in_dx", token=token)

    results = {}

    def update(names, recv, prefix):
        for n, r in zip(names, recv):
            results[n] = _adamw(r, wts[n], mom[n], var[n], prefix + n)

    def update_small(names, recv, name):
        items = [(r, lane_dense(n, wts[n]), lane_dense(n, mom[n]), lane_dense(n, var[n])) for n, r in zip(names, recv)]
        for n, outs in zip(names, _adamw_small(items, name)):
            results[n] = tuple(o.reshape(wts[n].shape) for o in outs)

    def done(names):
        return [results[n][3] for n in names]

    recv = _exchange_wait(exchange1, [grad_x], "grad_exchange1_wait")
    update(group1, recv[:len(group1)], "adamw_")
    update_small(SMALL_HEAD, recv[len(group1):-1], "adamw_ln3")
    loss = jnp.sum(recv[-1][:, 0, 0, 0])
    recv = _exchange_wait(exchange2, done(group1 + SMALL_HEAD), "grad_exchange2_wait")
    update(group2, recv[:len(group2)], "adamw_")
    update_small(SMALL_MID, recv[len(group2):], "adamw_small")
    recv = _exchange_wait(exchange4, done(group2 + SMALL_MID), "grad_exchange4_wait")
    update(("mix_w_in", "ffn1_w_out"), recv[:2], "adamw_")
    update_small(SMALL_LATE, recv[2:], "adamw_ln1")
    recv = _exchange_wait(exchange5, done(("mix_w_in", "ffn1_w_out") + SMALL_LATE), "grad_exchange5_wait")
    update(("ffn1_w_in",), recv, "adamw_")

    outs = [loss, grad_x.reshape(n_seq, seq, d_model)]
    for k in range(4):
        outs += [results[n][k][None] if n in SHARDED else results[n][k] for n in WEIGHTS]
    return tuple(outs)
```
